```python
import jax, jax.numpy as jnp
from jax import lax
import numpy as np

D_MODEL = 1024
BATCH = 8
SEQ = 2048
DEPTH = 1

MEM_LEN = 256
CONV_W = D_MODEL
CONV_K = 3
SB_HEADS = 8
SB_HEAD_DIM = 128
SB_W = SB_HEADS * SB_HEAD_DIM
Q_BLOCK = 128
X_HEADS = 4
X_HEAD_DIM = D_MODEL // X_HEADS
D_FF = 2816
RMS_EPS = 1e-6
PROJ_SIZES = (CONV_W, CONV_W, CONV_W, SB_W, SB_W, SB_W, D_MODEL, D_MODEL)

kernel_name = 'hybrid_conv_stickbreak_macaron_layer'


def rms_norm(x, g):
    xf = x.astype(jnp.float32)
    y = xf * lax.rsqrt(jnp.mean(xf * xf, axis=-1, keepdims=True) + RMS_EPS)
    return (y * g.astype(jnp.float32)).astype(x.dtype)


def swiglu(x, w_gu, w_down):
    gate, up = jnp.split(x @ w_gu, 2, axis=-1)
    return (jax.nn.silu(gate) * up) @ w_down


def short_conv(x, w):
    c = x.shape[-1]
    return lax.conv_general_dilated(
        x, w[:, None, :].astype(x.dtype), window_strides=(1,), padding=[(CONV_K - 1, 0)],
        dimension_numbers=('NWC', 'WIO', 'NWC'), feature_group_count=c)


def stick_breaking_attention(q, k, v):
    seq = q.shape[2]
    scale = SB_HEAD_DIM ** -0.5
    outs = []
    for blk in range(seq // Q_BLOCK):
        start = blk * Q_BLOCK
        end = start + Q_BLOCK
        z = jnp.einsum('bhqd,bhkd->bhqk', q[:, :, start:end], k[:, :, :end]).astype(jnp.float32) * scale
        t_pos = start + jnp.arange(Q_BLOCK)[:, None]
        s_pos = jnp.arange(end)[None, :]
        causal = s_pos < t_pos
        log_1m_beta = jnp.where(causal, jax.nn.log_sigmoid(-z), 0.0)
        after = lax.cumsum(log_1m_beta, axis=3, reverse=True) - log_1m_beta
        a = jnp.where(causal, jnp.exp(jax.nn.log_sigmoid(z) + after), 0.0)
        outs.append(jnp.einsum('bhqk,bhkd->bhqd', a, v[:, :, :end].astype(jnp.float32)))
    return jnp.concatenate(outs, axis=2).astype(q.dtype)


def memory_cross_attention(hn, mn, w_cq, w_ckv, w_co):
    b, s, _ = hn.shape
    m = mn.shape[1]
    q = (hn @ w_cq).reshape(b, s, X_HEADS, X_HEAD_DIM)
    k, v = jnp.split(mn @ w_ckv, 2, axis=-1)
    k = k.reshape(b, m, X_HEADS, X_HEAD_DIM)
    v = v.reshape(b, m, X_HEADS, X_HEAD_DIM)
    scores = jnp.einsum('bshd,bmhd->bhsm', q, k).astype(jnp.float32) * (X_HEAD_DIM ** -0.5)
    p = jax.nn.softmax(scores, axis=-1)
    o = jnp.einsum('bhsm,bmhd->bshd', p, v.astype(jnp.float32)).astype(hn.dtype)
    return o.reshape(b, s, D_MODEL) @ w_co


def hybrid_mixer(u, w_in, b_gate, conv_w, w_conv_out, w_attn_out, w_o):
    b, s, _ = u.shape
    split_at = np.cumsum(PROJ_SIZES)[:6].tolist()
    cb, cc, cx, q, k, v, gates = jnp.split(u @ w_in, split_at, axis=-1)
    gate_pre = gates + b_gate
    g_conv, g_sb = jnp.split(jax.nn.sigmoid(gate_pre), 2, axis=-1)
    y_conv = cb * short_conv(cc * cx, conv_w)
    to_heads = lambda t: t.reshape(b, s, SB_HEADS, SB_HEAD_DIM).transpose(0, 2, 1, 3)
    y_sb = stick_breaking_attention(to_heads(q), to_heads(k), to_heads(v))
    y_sb = y_sb.transpose(0, 2, 1, 3).reshape(b, s, SB_W)
    merged = g_conv * (y_conv @ w_conv_out) + g_sb * (y_sb @ w_attn_out)
    return merged @ w_o


def _fwd_setup_inputs(seed: int = 0) -> dict:
    key = jax.random.key(seed)
    ks = jax.random.split(key, 21)
    f32 = jnp.float32

    def dense(k, shape):
        return jax.random.normal(k, shape, f32) * (shape[0] ** -0.5)

    def gain(k):
        return 1.0 + 0.01 * jax.random.normal(k, (D_MODEL,), f32)

    return {
        'x': jax.random.normal(ks[0], (BATCH, SEQ, D_MODEL), f32),
        'mem': jax.random.normal(ks[1], (BATCH, MEM_LEN, D_MODEL), f32),
        'g_ffn1': gain(ks[2]),
        'w_ffn1_gu': dense(ks[3], (D_MODEL, 2 * D_FF)),
        'w_ffn1_down': dense(ks[4], (D_FF, D_MODEL)),
        'g_mix': gain(ks[5]),
        'w_in': dense(ks[6], (D_MODEL, sum(PROJ_SIZES))),
        'b_gate': 0.01 * jax.random.normal(ks[7], (2 * D_MODEL,), f32),
        'conv_w': jax.random.normal(ks[8], (CONV_K, CONV_W), f32) * (CONV_K ** -0.5),
        'w_conv_out': dense(ks[9], (CONV_W, D_MODEL)),
        'w_attn_out': dense(ks[10], (SB_W, D_MODEL)),
        'w_o': dense(ks[11], (D_MODEL, D_MODEL)),
        'g_cross': gain(ks[12]),
        'g_mem': gain(ks[13]),
        'w_cq': dense(ks[14], (D_MODEL, D_MODEL)),
        'w_ckv': dense(ks[15], (D_MODEL, 2 * D_MODEL)),
        'w_co': dense(ks[16], (D_MODEL, D_MODEL)),
        'g_ffn2': gain(ks[17]),
        'w_ffn2_gu': dense(ks[18], (D_MODEL, 2 * D_FF)),
        'w_ffn2_down': dense(ks[19], (D_FF, D_MODEL)),
        'g_final': gain(ks[20]),
    }


def _fwd_reference(x, mem, g_ffn1, w_ffn1_gu, w_ffn1_down, g_mix, w_in, b_gate, conv_w,
              w_conv_out, w_attn_out, w_o, g_cross, g_mem, w_cq, w_ckv, w_co,
              g_ffn2, w_ffn2_gu, w_ffn2_down, g_final):
    h = x
    for _ in range(DEPTH):
        h = h + 0.5 * swiglu(rms_norm(h, g_ffn1), w_ffn1_gu, w_ffn1_down)
        h = h + hybrid_mixer(rms_norm(h, g_mix), w_in, b_gate, conv_w, w_conv_out, w_attn_out, w_o)
        h = h + memory_cross_attention(rms_norm(h, g_cross), rms_norm(mem, g_mem), w_cq, w_ckv, w_co)
        h = h + 0.5 * swiglu(rms_norm(h, g_ffn2), w_ffn2_gu, w_ffn2_down)
    return rms_norm(h, g_final)


import jax as _jax
import jax.numpy as _jnp

TWIN_FORMAT = 'train_step'
FWD_PARAMS = ['x', 'mem', 'g_ffn1', 'w_ffn1_gu', 'w_ffn1_down', 'g_mix', 'w_in', 'b_gate', 'conv_w', 'w_conv_out', 'w_attn_out', 'w_o', 'g_cross', 'g_mem', 'w_cq', 'w_ckv', 'w_co', 'g_ffn2', 'w_ffn2_gu', 'w_ffn2_down', 'g_final']
TWIN_WEIGHTS = ['g_ffn1', 'w_ffn1_gu', 'w_ffn1_down', 'g_mix', 'w_in', 'b_gate', 'conv_w', 'w_conv_out', 'w_attn_out', 'w_o', 'g_cross', 'g_mem', 'w_cq', 'w_ckv', 'w_co', 'g_ffn2', 'w_ffn2_gu', 'w_ffn2_down', 'g_final']
TWIN_DIFF_INPUT = 'x'
TWIN_INPUTS = ['x', 'mem', 'g_ffn1', 'w_ffn1_gu', 'w_ffn1_down', 'g_mix', 'w_in', 'b_gate', 'conv_w', 'w_conv_out', 'w_attn_out', 'w_o', 'g_cross', 'g_mem', 'w_cq', 'w_ckv', 'w_co', 'g_ffn2', 'w_ffn2_gu', 'w_ffn2_down', 'g_final', 'loss_target', 'm_g_ffn1', 'm_w_ffn1_gu', 'm_w_ffn1_down', 'm_g_mix', 'm_w_in', 'm_b_gate', 'm_conv_w', 'm_w_conv_out', 'm_w_attn_out', 'm_w_o', 'm_g_cross', 'm_g_mem', 'm_w_cq', 'm_w_ckv', 'm_w_co', 'm_g_ffn2', 'm_w_ffn2_gu', 'm_w_ffn2_down', 'm_g_final', 'v_g_ffn1', 'v_w_ffn1_gu', 'v_w_ffn1_down', 'v_g_mix', 'v_w_in', 'v_b_gate', 'v_conv_w', 'v_w_conv_out', 'v_w_attn_out', 'v_w_o', 'v_g_cross', 'v_g_mem', 'v_w_cq', 'v_w_ckv', 'v_w_co', 'v_g_ffn2', 'v_w_ffn2_gu', 'v_w_ffn2_down', 'v_g_final']
TWIN_OUTPUTS = ['loss', 'grad_x', 'grad_g_ffn1', 'grad_w_ffn1_gu', 'grad_w_ffn1_down', 'grad_g_mix', 'grad_w_in', 'grad_b_gate', 'grad_conv_w', 'grad_w_conv_out', 'grad_w_attn_out', 'grad_w_o', 'grad_g_cross', 'grad_g_mem', 'grad_w_cq', 'grad_w_ckv', 'grad_w_co', 'grad_g_ffn2', 'grad_w_ffn2_gu', 'grad_w_ffn2_down', 'grad_g_final', 'delta_g_ffn1', 'delta_w_ffn1_gu', 'delta_w_ffn1_down', 'delta_g_mix', 'delta_w_in', 'delta_b_gate', 'delta_conv_w', 'delta_w_conv_out', 'delta_w_attn_out', 'delta_w_o', 'delta_g_cross', 'delta_g_mem', 'delta_w_cq', 'delta_w_ckv', 'delta_w_co', 'delta_g_ffn2', 'delta_w_ffn2_gu', 'delta_w_ffn2_down', 'delta_g_final', 'new_m_g_ffn1', 'new_m_w_ffn1_gu', 'new_m_w_ffn1_down', 'new_m_g_mix', 'new_m_w_in', 'new_m_b_gate', 'new_m_conv_w', 'new_m_w_conv_out', 'new_m_w_attn_out', 'new_m_w_o', 'new_m_g_cross', 'new_m_g_mem', 'new_m_w_cq', 'new_m_w_ckv', 'new_m_w_co', 'new_m_g_ffn2', 'new_m_w_ffn2_gu', 'new_m_w_ffn2_down', 'new_m_g_final', 'new_v_g_ffn1', 'new_v_w_ffn1_gu', 'new_v_w_ffn1_down', 'new_v_g_mix', 'new_v_w_in', 'new_v_b_gate', 'new_v_conv_w', 'new_v_w_conv_out', 'new_v_w_attn_out', 'new_v_w_o', 'new_v_g_cross', 'new_v_g_mem', 'new_v_w_cq', 'new_v_w_ckv', 'new_v_w_co', 'new_v_g_ffn2', 'new_v_w_ffn2_gu', 'new_v_w_ffn2_down', 'new_v_g_final']
TWIN_LEAF_KINDS = {'loss': 'loss', 'grad_x': 'grad_x', 'grad_g_ffn1': 'grad_w', 'grad_w_ffn1_gu': 'grad_w', 'grad_w_ffn1_down': 'grad_w', 'grad_g_mix': 'grad_w', 'grad_w_in': 'grad_w', 'grad_b_gate': 'grad_w', 'grad_conv_w': 'grad_w', 'grad_w_conv_out': 'grad_w', 'grad_w_attn_out': 'grad_w', 'grad_w_o': 'grad_w', 'grad_g_cross': 'grad_w', 'grad_g_mem': 'grad_w', 'grad_w_cq': 'grad_w', 'grad_w_ckv': 'grad_w', 'grad_w_co': 'grad_w', 'grad_g_ffn2': 'grad_w', 'grad_w_ffn2_gu': 'grad_w', 'grad_w_ffn2_down': 'grad_w', 'grad_g_final': 'grad_w', 'delta_g_ffn1': 'delta_w', 'delta_w_ffn1_gu': 'delta_w', 'delta_w_ffn1_down': 'delta_w', 'delta_g_mix': 'delta_w', 'delta_w_in': 'delta_w', 'delta_b_gate': 'delta_w', 'delta_conv_w': 'delta_w', 'delta_w_conv_out': 'delta_w', 'delta_w_attn_out': 'delta_w', 'delta_w_o': 'delta_w', 'delta_g_cross': 'delta_w', 'delta_g_mem': 'delta_w', 'delta_w_cq': 'delta_w', 'delta_w_ckv': 'delta_w', 'delta_w_co': 'delta_w', 'delta_g_ffn2': 'delta_w', 'delta_w_ffn2_gu': 'delta_w', 'delta_w_ffn2_down': 'delta_w', 'delta_g_final': 'delta_w', 'new_m_g_ffn1': 'new_m', 'new_m_w_ffn1_gu': 'new_m', 'new_m_w_ffn1_down': 'new_m', 'new_m_g_mix': 'new_m', 'new_m_w_in': 'new_m', 'new_m_b_gate': 'new_m', 'new_m_conv_w': 'new_m', 'new_m_w_conv_out': 'new_m', 'new_m_w_attn_out': 'new_m', 'new_m_w_o': 'new_m', 'new_m_g_cross': 'new_m', 'new_m_g_mem': 'new_m', 'new_m_w_cq': 'new_m', 'new_m_w_ckv': 'new_m', 'new_m_w_co': 'new_m', 'new_m_g_ffn2': 'new_m', 'new_m_w_ffn2_gu': 'new_m', 'new_m_w_ffn2_down': 'new_m', 'new_m_g_final': 'new_m', 'new_v_g_ffn1': 'new_v', 'new_v_w_ffn1_gu': 'new_v', 'new_v_w_ffn1_down': 'new_v', 'new_v_g_mix': 'new_v', 'new_v_w_in': 'new_v', 'new_v_b_gate': 'new_v', 'new_v_conv_w': 'new_v', 'new_v_w_conv_out': 'new_v', 'new_v_w_attn_out': 'new_v', 'new_v_w_o': 'new_v', 'new_v_g_cross': 'new_v', 'new_v_g_mem': 'new_v', 'new_v_w_cq': 'new_v', 'new_v_w_ckv': 'new_v', 'new_v_w_co': 'new_v', 'new_v_g_ffn2': 'new_v', 'new_v_w_ffn2_gu': 'new_v', 'new_v_w_ffn2_down': 'new_v', 'new_v_g_final': 'new_v'}


def _forward(args):
    return _fwd_reference(*[args[k] for k in FWD_PARAMS])


def _output_shape():
    out = _jax.eval_shape(lambda: _forward(_fwd_setup_inputs(0)))
    return out.shape, out.dtype

N_MICROBATCH = 1
ADAM_LR = 0.001
ADAM_B1 = 0.9
ADAM_B2 = 0.999
ADAM_EPS = 1e-08
ADAM_WD = 0.01
ADAM_STEP = 10
PER_EXAMPLE_BATCH_AXIS = {'x': 0, 'mem': 0, 'loss_target': 0}
SHARED_INPUTS = []
_WEIGHT_DTYPES = {'g_ffn1': _jnp.float32, 'w_ffn1_gu': _jnp.float32, 'w_ffn1_down': _jnp.float32, 'g_mix': _jnp.float32, 'w_in': _jnp.float32, 'b_gate': _jnp.float32, 'conv_w': _jnp.float32, 'w_conv_out': _jnp.float32, 'w_attn_out': _jnp.float32, 'w_o': _jnp.float32, 'g_cross': _jnp.float32, 'g_mem': _jnp.float32, 'w_cq': _jnp.float32, 'w_ckv': _jnp.float32, 'w_co': _jnp.float32, 'g_ffn2': _jnp.float32, 'w_ffn2_gu': _jnp.float32, 'w_ffn2_down': _jnp.float32, 'g_final': _jnp.float32}
MOMENT_SCALE = {'g_ffn1': 6.581798e-02, 'w_ffn1_gu': 2.727555e-02, 'w_ffn1_down': 4.452713e-02, 'g_mix': 1.087338e-01, 'w_in': 3.901860e-02, 'b_gate': 1.840513e-02, 'conv_w': 5.718026e-02, 'w_conv_out': 5.631519e-02, 'w_attn_out': 3.709324e-02, 'w_o': 6.767077e-02, 'g_cross': 1.134422e-02, 'g_mem': 1.674202e-02, 'w_cq': 1.098908e-02, 'w_ckv': 1.103125e-02, 'w_co': 1.117366e-02, 'g_ffn2': 4.278778e-02, 'w_ffn2_gu': 1.816592e-02, 'w_ffn2_down': 2.964744e-02, 'g_final': 1.599616e+01}


def _to_microbatches(a, axis):
    t = _jnp.moveaxis(a, axis, 0)
    t = t.reshape((N_MICROBATCH, t.shape[0] // N_MICROBATCH) + t.shape[1:])
    return _jnp.moveaxis(t, 1, axis + 1)


def setup_inputs(seed: int = 0) -> dict:
    inp = _fwd_setup_inputs(seed)
    key = _jax.random.fold_in(_jax.random.key(seed), 7919)
    shape, _ = _output_shape()
    out = dict(inp)
    out["loss_target"] = _jax.random.normal(_jax.random.fold_in(key, 0), shape, _jnp.float32)
    for i, name in enumerate(TWIN_WEIGHTS):
        w = inp[name].astype(_jnp.float32)
        if MOMENT_SCALE is None:
            s = _jnp.sqrt(_jnp.mean(_jnp.square(w)) + 1e-30)
        else:
            s = MOMENT_SCALE[name]
        km, kv = _jax.random.split(_jax.random.fold_in(key, i + 1))
        out[name] = w
        out["m_" + name] = s * _jax.random.normal(km, w.shape, _jnp.float32)
        out["v_" + name] = (s * s) * _jax.random.uniform(kv, w.shape, _jnp.float32, 0.5, 1.5)
    if N_MICROBATCH > 1:
        for name, axis in PER_EXAMPLE_BATCH_AXIS.items():
            out[name] = _to_microbatches(out[name], axis)
    return {'x': out['x'], 'mem': out['mem'], 'g_ffn1': out['g_ffn1'], 'w_ffn1_gu': out['w_ffn1_gu'], 'w_ffn1_down': out['w_ffn1_down'], 'g_mix': out['g_mix'], 'w_in': out['w_in'], 'b_gate': out['b_gate'], 'conv_w': out['conv_w'], 'w_conv_out': out['w_conv_out'], 'w_attn_out': out['w_attn_out'], 'w_o': out['w_o'], 'g_cross': out['g_cross'], 'g_mem': out['g_mem'], 'w_cq': out['w_cq'], 'w_ckv': out['w_ckv'], 'w_co': out['w_co'], 'g_ffn2': out['g_ffn2'], 'w_ffn2_gu': out['w_ffn2_gu'], 'w_ffn2_down': out['w_ffn2_down'], 'g_final': out['g_final'], 'loss_target': out['loss_target'], 'm_g_ffn1': out['m_g_ffn1'], 'm_w_ffn1_gu': out['m_w_ffn1_gu'], 'm_w_ffn1_down': out['m_w_ffn1_down'], 'm_g_mix': out['m_g_mix'], 'm_w_in': out['m_w_in'], 'm_b_gate': out['m_b_gate'], 'm_conv_w': out['m_conv_w'], 'm_w_conv_out': out['m_w_conv_out'], 'm_w_attn_out': out['m_w_attn_out'], 'm_w_o': out['m_w_o'], 'm_g_cross': out['m_g_cross'], 'm_g_mem': out['m_g_mem'], 'm_w_cq': out['m_w_cq'], 'm_w_ckv': out['m_w_ckv'], 'm_w_co': out['m_w_co'], 'm_g_ffn2': out['m_g_ffn2'], 'm_w_ffn2_gu': out['m_w_ffn2_gu'], 'm_w_ffn2_down': out['m_w_ffn2_down'], 'm_g_final': out['m_g_final'], 'v_g_ffn1': out['v_g_ffn1'], 'v_w_ffn1_gu': out['v_w_ffn1_gu'], 'v_w_ffn1_down': out['v_w_ffn1_down'], 'v_g_mix': out['v_g_mix'], 'v_w_in': out['v_w_in'], 'v_b_gate': out['v_b_gate'], 'v_conv_w': out['v_conv_w'], 'v_w_conv_out': out['v_w_conv_out'], 'v_w_attn_out': out['v_w_attn_out'], 'v_w_o': out['v_w_o'], 'v_g_cross': out['v_g_cross'], 'v_g_mem': out['v_g_mem'], 'v_w_cq': out['v_w_cq'], 'v_w_ckv': out['v_w_ckv'], 'v_w_co': out['v_w_co'], 'v_g_ffn2': out['v_g_ffn2'], 'v_w_ffn2_gu': out['v_w_ffn2_gu'], 'v_w_ffn2_down': out['v_w_ffn2_down'], 'v_g_final': out['v_g_final']}


def _loss(weights, diff, rest, loss_target):
    with _jax.named_scope("forward"):
        args = {**rest, TWIN_DIFF_INPUT: diff, **{k: w.astype(_WEIGHT_DTYPES[k]) for k, w in weights.items()}}
        y = _forward(args)
    with _jax.named_scope("loss_head"):
        err = _jnp.square(y.astype(_jnp.float32) - loss_target)
        return 0.5 * _jnp.sum(_jnp.mean(err, axis=-1)) if err.ndim else 0.5 * err


def _adamw(w, g, m, v):
    m = ADAM_B1 * m + (1.0 - ADAM_B1) * g
    v = ADAM_B2 * v + (1.0 - ADAM_B2) * _jnp.square(g)
    m_hat = m / (1.0 - ADAM_B1 ** ADAM_STEP)
    v_hat = v / (1.0 - ADAM_B2 ** ADAM_STEP)
    delta = -ADAM_LR * (m_hat / (_jnp.sqrt(v_hat) + ADAM_EPS) + ADAM_WD * w)
    return delta, m, v


def reference(x, mem, g_ffn1, w_ffn1_gu, w_ffn1_down, g_mix, w_in, b_gate, conv_w, w_conv_out, w_attn_out, w_o, g_cross, g_mem, w_cq, w_ckv, w_co, g_ffn2, w_ffn2_gu, w_ffn2_down, g_final, loss_target, m_g_ffn1, m_w_ffn1_gu, m_w_ffn1_down, m_g_mix, m_w_in, m_b_gate, m_conv_w, m_w_conv_out, m_w_attn_out, m_w_o, m_g_cross, m_g_mem, m_w_cq, m_w_ckv, m_w_co, m_g_ffn2, m_w_ffn2_gu, m_w_ffn2_down, m_g_final, v_g_ffn1, v_w_ffn1_gu, v_w_ffn1_down, v_g_mix, v_w_in, v_b_gate, v_conv_w, v_w_conv_out, v_w_attn_out, v_w_o, v_g_cross, v_g_mem, v_w_cq, v_w_ckv, v_w_co, v_g_ffn2, v_w_ffn2_gu, v_w_ffn2_down, v_g_final):
    given = dict(x=x, mem=mem, g_ffn1=g_ffn1, w_ffn1_gu=w_ffn1_gu, w_ffn1_down=w_ffn1_down, g_mix=g_mix, w_in=w_in, b_gate=b_gate, conv_w=conv_w, w_conv_out=w_conv_out, w_attn_out=w_attn_out, w_o=w_o, g_cross=g_cross, g_mem=g_mem, w_cq=w_cq, w_ckv=w_ckv, w_co=w_co, g_ffn2=g_ffn2, w_ffn2_gu=w_ffn2_gu, w_ffn2_down=w_ffn2_down, g_final=g_final, loss_target=loss_target, m_g_ffn1=m_g_ffn1, m_w_ffn1_gu=m_w_ffn1_gu, m_w_ffn1_down=m_w_ffn1_down, m_g_mix=m_g_mix, m_w_in=m_w_in, m_b_gate=m_b_gate, m_conv_w=m_conv_w, m_w_conv_out=m_w_conv_out, m_w_attn_out=m_w_attn_out, m_w_o=m_w_o, m_g_cross=m_g_cross, m_g_mem=m_g_mem, m_w_cq=m_w_cq, m_w_ckv=m_w_ckv, m_w_co=m_w_co, m_g_ffn2=m_g_ffn2, m_w_ffn2_gu=m_w_ffn2_gu, m_w_ffn2_down=m_w_ffn2_down, m_g_final=m_g_final, v_g_ffn1=v_g_ffn1, v_w_ffn1_gu=v_w_ffn1_gu, v_w_ffn1_down=v_w_ffn1_down, v_g_mix=v_g_mix, v_w_in=v_w_in, v_b_gate=v_b_gate, v_conv_w=v_conv_w, v_w_conv_out=v_w_conv_out, v_w_attn_out=v_w_attn_out, v_w_o=v_w_o, v_g_cross=v_g_cross, v_g_mem=v_g_mem, v_w_cq=v_w_cq, v_w_ckv=v_w_ckv, v_w_co=v_w_co, v_g_ffn2=v_g_ffn2, v_w_ffn2_gu=v_w_ffn2_gu, v_w_ffn2_down=v_w_ffn2_down, v_g_final=v_g_final)
    weights = {n: given[n] for n in TWIN_WEIGHTS}
    shared = {n: given[n] for n in SHARED_INPUTS}
    per_example = {n: given[n] for n in ['x', 'mem']}
    grad_fn = _jax.value_and_grad(_loss, argnums=(0, 1))

    def one_microbatch(ex, loss_target):
        ex = dict(ex)
        diff = ex.pop(TWIN_DIFF_INPUT)
        return grad_fn(weights, diff, {**shared, **ex}, loss_target)

    if N_MICROBATCH == 1:
        loss, (grad_w, grad_x) = one_microbatch(per_example, given["loss_target"])
    else:
        def body(carry, xs):
            loss_sum, grad_sum = carry
            l_k, (gw_k, gx_k) = one_microbatch(xs[0], xs[1])
            with _jax.named_scope("update"):
                return (loss_sum + l_k, _jax.tree.map(_jnp.add, grad_sum, gw_k)), gx_k

        init = (_jnp.zeros((), _jnp.float32), _jax.tree.map(_jnp.zeros_like, weights))
        (loss, grad_w), grad_x = _jax.lax.scan(body, init, (per_example, given["loss_target"]))
    with _jax.named_scope("update"):
        delta_w, new_m, new_v = {}, {}, {}
        for n in TWIN_WEIGHTS:
            delta_w[n], new_m[n], new_v[n] = _adamw(weights[n], grad_w[n], given["m_" + n], given["v_" + n])
    return (loss, grad_x, *[grad_w[n] for n in TWIN_WEIGHTS], *[delta_w[n] for n in TWIN_WEIGHTS],
            *[new_m[n] for n in TWIN_WEIGHTS], *[new_v[n] for n in TWIN_WEIGHTS])
```

```python
import functools

import jax
import jax.numpy as jnp
from jax import lax
from jax.experimental import pallas as pl
from jax.experimental.pallas import tpu as pltpu

F32 = jnp.float32
BF16 = jnp.bfloat16
MESH = pl.DeviceIdType.MESH

V7X_VMEM_LIMIT_BYTES = 48 * 1024 * 1024
LANES = 128
SB_HEAD_DIM = 128
X_HEADS = 4
CONV_K = 3
RMS_EPS = 1e-6
N_CHIPS = 4
N_DEV = 8
ADAM_LR, ADAM_B1, ADAM_B2, ADAM_EPS, ADAM_WD, ADAM_STEP = 0.001, 0.9, 0.999, 1e-08, 0.01, 10


def _pcall(body, **kw):
    return pl.pallas_call(body, **kw)


def _params(*sem):
    return pltpu.CompilerParams(dimension_semantics=sem, vmem_limit_bytes=V7X_VMEM_LIMIT_BYTES)


def _pick(dim, cands):
    for c in cands:
        if dim % c == 0:
            return c
    return dim


def _dot(a, b, ca, cb):
    return lax.dot_general(a, b, (((ca,), (cb,)), ((), ())), preferred_element_type=F32)


def _mm(a, b, *, name, ta=False, tb=False, out_dtype=BF16, res=None, alpha=1.0, tm=None, tn=None, tk=None):
    m, k = (a.shape[1], a.shape[0]) if ta else a.shape
    n = b.shape[0] if tb else b.shape[1]
    assert k == (b.shape[1] if tb else b.shape[0]), (a.shape, b.shape, ta, tb)
    tm = tm or _pick(m, (512, 256, 128))
    tn = tn or _pick(n, (1024, 512, 256, 128))
    tk = tk or (k if k <= 2816 else _pick(k, (1024, 512, 256, 128)))
    nk = k // tk
    assert m % tm == 0 and n % tn == 0 and k % tk == 0
    a_spec = pl.BlockSpec((tk, tm), lambda i, j, kk: (kk, i)) if ta else pl.BlockSpec((tm, tk), lambda i, j, kk: (i, kk))
    b_spec = pl.BlockSpec((tn, tk), lambda i, j, kk: (j, kk)) if tb else pl.BlockSpec((tk, tn), lambda i, j, kk: (kk, j))
    o_spec = pl.BlockSpec((tm, tn), lambda i, j, kk: (i, j))
    ca, cb = (0 if ta else 1), (1 if tb else 0)

    def body(*refs):
        if res is None:
            a_ref, b_ref, o_ref = refs[:3]
            res_ref = None
            scratch = refs[3:]
        else:
            a_ref, b_ref, res_ref, o_ref = refs[:4]
            scratch = refs[4:]

        def finish(acc):
            val = acc if alpha == 1.0 else alpha * acc
            if res_ref is not None:
                val = res_ref[...].astype(F32) + val
            o_ref[...] = val.astype(o_ref.dtype)

        part = _dot(a_ref[...].astype(BF16), b_ref[...].astype(BF16), ca, cb)
        if nk == 1:
            finish(part)
        else:
            acc_ref = scratch[0]
            kk = pl.program_id(2)

            @pl.when(kk == 0)
            def _():
                acc_ref[...] = part

            @pl.when(kk > 0)
            def _():
                acc_ref[...] += part

            @pl.when(kk == nk - 1)
            def _():
                finish(acc_ref[...])

    ins = [a, b] + ([] if res is None else [res])
    in_specs = [a_spec, b_spec] + ([] if res is None else [o_spec])
    return _pcall(
        body, name=name, grid=(m // tm, n // tn, nk), in_specs=in_specs, out_specs=o_spec,
        out_shape=jax.ShapeDtypeStruct((m, n), out_dtype),
        scratch_shapes=[pltpu.VMEM((tm, tn), F32)] if nk > 1 else [],
        compiler_params=_params("parallel", "parallel", "arbitrary"),
    )(*ins)


def _rowcall(fn, rows, consts, outs, accs=(), *, tm, name):
    s = rows[0][0].shape[0]
    assert s % tm == 0
    n_in, n_out = len(rows) + len(consts), len(outs)

    def body(*refs):
        vals = fn(*[r[...] for r in refs[:n_in]])
        vals = vals if isinstance(vals, (tuple, list)) else (vals,)
        for o_ref, v in zip(refs[n_in:n_in + n_out], vals[:n_out]):
            o_ref[...] = v.astype(o_ref.dtype)
        if accs:
            first = pl.program_id(0) == 0
            for a_ref, v in zip(refs[n_in + n_out:], vals[n_out:]):
                tot = jnp.sum(v.astype(F32), axis=0, keepdims=True)

                @pl.when(first)
                def _(a_ref=a_ref, tot=tot):
                    a_ref[...] = tot

                @pl.when(jnp.logical_not(first))
                def _(a_ref=a_ref, tot=tot):
                    a_ref[...] += tot

    in_specs = [pl.BlockSpec((tm, w), lambda i, cb=cb: (i, cb)) for (_, cb, w) in rows]
    in_specs += [pl.BlockSpec(c.shape, lambda i: (0, 0)) for c in consts]
    out_specs = [pl.BlockSpec((tm, w), lambda i: (i, 0)) for (w, _) in outs]
    out_specs += [pl.BlockSpec((1, w), lambda i: (0, 0)) for w in accs]
    out_shape = [jax.ShapeDtypeStruct((s, w), dt) for (w, dt) in outs]
    out_shape += [jax.ShapeDtypeStruct((1, w), F32) for w in accs]
    return _pcall(
        body, name=name, grid=(s // tm,), in_specs=in_specs, out_specs=out_specs, out_shape=out_shape,
        compiler_params=_params("arbitrary" if accs else "parallel"),
    )(*[r[0] for r in rows], *consts)


def _whole(a):
    return (a, 0, a.shape[1])


def _xhat(x):
    x = x.astype(F32)
    r = lax.rsqrt(jnp.mean(x * x, axis=-1, keepdims=True) + RMS_EPS)
    return x * r, r


def _rms_bwd(dy, x, g):
    xh, r = _xhat(x)
    dxh = dy.astype(F32) * g
    dx = r * (dxh - xh * jnp.mean(dxh * xh, axis=-1, keepdims=True))
    return dx, dy.astype(F32) * xh


def _sigmoid(x):
    return 1.0 / (1.0 + jnp.exp(-x))


def _rms_fwd(x, g, name, tm):
    d = x.shape[1]
    return _rowcall(lambda xb, gb: _xhat(xb)[0] * gb, [_whole(x)], [g], [(d, BF16)], tm=tm, name=name)[0]


def _swiglu_fwd(gu, name, tm):
    f = gu.shape[1] // 2

    def fn(gate, up):
        gate, up = gate.astype(F32), up.astype(F32)
        return gate * _sigmoid(gate) * up

    return _rowcall(fn, [(gu, 0, f), (gu, 1, f)], [], [(f, BF16)], tm=tm, name=name)[0]


def _swiglu_bwd(dact, gu, name, tm):
    f = gu.shape[1] // 2

    def fn(da, gate, up):
        da, gate, up = da.astype(F32), gate.astype(F32), up.astype(F32)
        sg = _sigmoid(gate)
        silu = gate * sg
        dgate = da * up * (sg + silu * (1.0 - sg))
        return jnp.concatenate([dgate, da * silu], axis=1)

    return _rowcall(fn, [_whole(dact), (gu, 0, f), (gu, 1, f)], [], [(2 * f, BF16)], tm=tm, name=name)[0]


def _resid_rms_bwd(dh, dn, x, g, name, tm):
    d = x.shape[1]

    def fn(dhb, dnb, xb, gb):
        dx, dg = _rms_bwd(dnb, xb, gb)
        return dhb.astype(F32) + dx, dg

    return _rowcall(fn, [_whole(dh), _whole(dn), _whole(x)], [g], [(d, F32)], [d], tm=tm, name=name)


def _shift_down(p, k):
    if k == 0:
        return p
    rows = lax.broadcasted_iota(jnp.int32, p.shape, 0)
    return jnp.where(rows >= k, pltpu.roll(p, k, 0), 0.0)


def _shift_up(p, k):
    if k == 0:
        return p
    s = p.shape[0]
    rows = lax.broadcasted_iota(jnp.int32, p.shape, 0)
    return jnp.where(rows < s - k, pltpu.roll(p, s - k, 0), 0.0)


def _conv_fwd(proj, conv_w, d, tc, name):
    s = proj.shape[0]
    nb = d // tc

    def body(cb_ref, cc_ref, cx_ref, w_ref, y_ref):
        p = cc_ref[...].astype(F32) * cx_ref[...].astype(F32)
        w = w_ref[...]
        acc = p * w[CONV_K - 1:CONV_K, :]
        for k in range(1, CONV_K):
            acc = acc + _shift_down(p, k) * w[CONV_K - 1 - k:CONV_K - k, :]
        y_ref[...] = (cb_ref[...].astype(F32) * acc).astype(y_ref.dtype)

    col = lambda off: pl.BlockSpec((s, tc), lambda j: (0, off * nb + j))
    return _pcall(
        body, name=name, grid=(nb,), in_specs=[col(0), col(1), col(2), pl.BlockSpec((CONV_K, tc), lambda j: (0, j))],
        out_specs=pl.BlockSpec((s, tc), lambda j: (0, j)), out_shape=jax.ShapeDtypeStruct((s, d), BF16),
        compiler_params=_params("parallel"),
    )(proj, proj, proj, conv_w)


def _conv_bwd(dy, proj, conv_w, d, tc, name):
    s = proj.shape[0]
    nb = d // tc

    def body(dy_ref, cb_ref, cc_ref, cx_ref, w_ref, dcb_ref, dcc_ref, dcx_ref, dw_ref):
        cc, cx = cc_ref[...].astype(F32), cx_ref[...].astype(F32)
        p = cc * cx
        w = w_ref[...]
        dyv = dy_ref[...].astype(F32)
        shifted = [_shift_down(p, CONV_K - 1 - k) for k in range(CONV_K)]
        conv = shifted[0] * w[0:1, :]
        for k in range(1, CONV_K):
            conv = conv + shifted[k] * w[k:k + 1, :]
        dcb_ref[...] = (dyv * conv).astype(dcb_ref.dtype)
        ds = dyv * cb_ref[...].astype(F32)
        dp = ds * w[CONV_K - 1:CONV_K, :]
        for k in range(1, CONV_K):
            dp = dp + _shift_up(ds, k) * w[CONV_K - 1 - k:CONV_K - k, :]
        dcc_ref[...] = (dp * cx).astype(dcc_ref.dtype)
        dcx_ref[...] = (dp * cc).astype(dcx_ref.dtype)
        for k in range(CONV_K):
            dw_ref[k:k + 1, :] = jnp.sum(ds * shifted[k], axis=0, keepdims=True)

    col = lambda off: pl.BlockSpec((s, tc), lambda j: (0, off * nb + j))
    blk = pl.BlockSpec((s, tc), lambda j: (0, j))
    wblk = pl.BlockSpec((CONV_K, tc), lambda j: (0, j))
    act = jax.ShapeDtypeStruct((s, d), BF16)
    return _pcall(
        body, name=name, grid=(nb,), in_specs=[blk, col(0), col(1), col(2), wblk],
        out_specs=[blk, blk, blk, wblk], out_shape=[act, act, act, jax.ShapeDtypeStruct((CONV_K, d), F32)],
        compiler_params=_params("parallel"),
    )(dy, proj, proj, proj, conv_w)


def _split_bf16(x):
    hi = x.astype(BF16)
    return hi, (x - hi.astype(F32)).astype(BF16)


def _sb_tile(q, kj, scale, carry, tri, mask):
    z = _dot(q, kj, 1, 1) * scale
    lsz = jnp.minimum(z, 0.0) - jnp.log(1.0 + jnp.exp(-jnp.abs(z)))
    l1m = lsz - z
    if mask is not None:
        l1m = jnp.where(mask, l1m, 0.0)
    hi, lo = _split_bf16(l1m)
    after = carry + _dot(hi, tri, 1, 0) + _dot(lo, tri, 1, 0)
    a = jnp.exp(lsz + after)
    if mask is not None:
        a = jnp.where(mask, a, 0.0)
    return lsz, l1m, a.astype(BF16)


def _sb_fwd(proj, heads, col0, tq, name):
    s = proj.shape[0]
    dh = SB_HEAD_DIM
    nq = s // tq
    scale = dh ** -0.5

    def body(q_ref, k_ref, v_ref, o_ref):
        i = pl.program_id(1)
        q = q_ref[...]
        row = lax.broadcasted_iota(jnp.int32, (tq, tq), 0)
        col = lax.broadcasted_iota(jnp.int32, (tq, tq), 1)
        tri = (row > col).astype(BF16)

        def tile(j, carry, acc, mask):
            start = pl.multiple_of(j * tq, tq)
            kj = k_ref[pl.ds(start, tq), :]
            vj = v_ref[pl.ds(start, tq), :]
            _, l1m, ab = _sb_tile(q, kj, scale, carry, tri, mask)
            return carry + jnp.sum(l1m, axis=1, keepdims=True), acc + _dot(ab, vj, 1, 0)

        carry, acc = tile(i, jnp.zeros((tq, 1), F32), jnp.zeros((tq, dh), F32), col < row)
        carry, acc = lax.fori_loop(0, i, lambda t, ca: tile(i - 1 - t, ca[0], ca[1], None), (carry, acc))
        o_ref[...] = acc

    qspec = pl.BlockSpec((tq, dh), lambda h, i: (i, col0[0] + h))
    kspec = pl.BlockSpec((s, dh), lambda h, i: (0, col0[1] + h))
    vspec = pl.BlockSpec((s, dh), lambda h, i: (0, col0[2] + h))
    return _pcall(
        body, name=name, grid=(heads, nq), in_specs=[qspec, kspec, vspec],
        out_specs=pl.BlockSpec((tq, dh), lambda h, i: (i, h)), out_shape=jax.ShapeDtypeStruct((s, heads * dh), F32),
        compiler_params=_params("parallel", "parallel"),
    )(proj, proj, proj)


def _sb_bwd(proj, o, do, heads, col0, tq, name):
    s = proj.shape[0]
    dh = SB_HEAD_DIM
    nq = s // tq
    scale = dh ** -0.5

    def body(q_ref, k_ref, v_ref, o_ref, do_ref, dq_ref, dk_ref, dv_ref, dk_acc, dv_acc):
        i = pl.program_id(1)

        @pl.when(i == 0)
        def _():
            dk_acc[...] = jnp.zeros_like(dk_acc)
            dv_acc[...] = jnp.zeros_like(dv_acc)

        q = q_ref[...]
        dob = do_ref[...].astype(BF16)
        delta = jnp.sum(dob.astype(F32) * o_ref[...], axis=1, keepdims=True)
        row = lax.broadcasted_iota(jnp.int32, (tq, tq), 0)
        col = lax.broadcasted_iota(jnp.int32, (tq, tq), 1)
        tri = (row > col).astype(BF16)
        tri_incl = (row >= col).astype(BF16)

        def tile(j, carry_l, carry_g, dq, mask):
            start = pl.multiple_of(j * tq, tq)
            kj = k_ref[pl.ds(start, tq), :]
            vj = v_ref[pl.ds(start, tq), :]
            lsz, l1m, ab = _sb_tile(q, kj, scale, carry_l, tri, mask)
            g = _dot(dob, vj, 1, 1) * ab.astype(F32)
            hi, lo = _split_bf16(g)
            left = delta - (carry_g + _dot(hi, tri_incl, 1, 0) + _dot(lo, tri_incl, 1, 0))
            beta = jnp.exp(lsz)
            dz = g * (1.0 - beta) - left * beta
            if mask is not None:
                dz = jnp.where(mask, dz, 0.0)
            dzb = (dz * scale).astype(BF16)
            dk_acc[pl.ds(start, tq), :] += _dot(dzb, q, 0, 0)
            dv_acc[pl.ds(start, tq), :] += _dot(ab, dob, 0, 0)
            return (carry_l + jnp.sum(l1m, axis=1, keepdims=True), carry_g + jnp.sum(g, axis=1, keepdims=True),
                    dq + _dot(dzb, kj, 1, 0))

        zero = jnp.zeros((tq, 1), F32)
        state = tile(i, zero, zero, jnp.zeros((tq, dh), F32), col < row)
        state = lax.fori_loop(0, i, lambda t, st: tile(i - 1 - t, st[0], st[1], st[2], None), state)
        dq_ref[...] = state[2].astype(dq_ref.dtype)

        @pl.when(i == nq - 1)
        def _():
            dk_ref[...] = dk_acc[...].astype(dk_ref.dtype)
            dv_ref[...] = dv_acc[...].astype(dv_ref.dtype)

    qspec = pl.BlockSpec((tq, dh), lambda h, i: (i, col0[0] + h))
    kspec = pl.BlockSpec((s, dh), lambda h, i: (0, col0[1] + h))
    vspec = pl.BlockSpec((s, dh), lambda h, i: (0, col0[2] + h))
    blk = pl.BlockSpec((tq, dh), lambda h, i: (i, h))
    full = pl.BlockSpec((s, dh), lambda h, i: (0, h))
    act = jax.ShapeDtypeStruct((s, heads * dh), BF16)
    return _pcall(
        body, name=name, grid=(heads, nq), in_specs=[qspec, kspec, vspec, blk, blk],
        out_specs=[blk, full, full], out_shape=[act, act, act],
        scratch_shapes=[pltpu.VMEM((s, dh), F32), pltpu.VMEM((s, dh), F32)],
        compiler_params=_params("parallel", "arbitrary"),
    )(proj, proj, proj, o, do)


def _xattn_probs(q, k, scale):
    sc = _dot(q, k, 1, 1) * scale
    e = jnp.exp(sc - jnp.max(sc, axis=1, keepdims=True))
    return e / jnp.sum(e, axis=1, keepdims=True)


def _xattn_fwd(qc, kv, tq, name):
    s, d = qc.shape
    m = kv.shape[0]
    dh = d // X_HEADS
    scale = dh ** -0.5

    def body(q_ref, k_ref, v_ref, o_ref):
        p = _xattn_probs(q_ref[...], k_ref[...], scale)
        o_ref[...] = _dot(p.astype(BF16), v_ref[...], 1, 0).astype(o_ref.dtype)

    blk = pl.BlockSpec((tq, dh), lambda h, i: (i, h))
    return _pcall(
        body, name=name, grid=(X_HEADS, s // tq),
        in_specs=[blk, pl.BlockSpec((m, dh), lambda h, i: (0, h)), pl.BlockSpec((m, dh), lambda h, i: (0, X_HEADS + h))],
        out_specs=blk, out_shape=jax.ShapeDtypeStruct((s, d), BF16), compiler_params=_params("parallel", "parallel"),
    )(qc, kv, kv)


def _xattn_bwd(qc, kv, do, tq, name):
    s, d = qc.shape
    m = kv.shape[0]
    dh = d // X_HEADS
    scale = dh ** -0.5
    nq = s // tq

    def body(q_ref, k_ref, v_ref, do_ref, dq_ref, dk_ref, dv_ref, dk_acc, dv_acc):
        i = pl.program_id(1)
        q, k, v = q_ref[...], k_ref[...], v_ref[...]
        dob = do_ref[...].astype(BF16)
        p = _xattn_probs(q, k, scale)
        pb = p.astype(BF16)
        dp = _dot(dob, v, 1, 1)
        ds = pb.astype(F32) * (dp - jnp.sum(dp * pb.astype(F32), axis=1, keepdims=True))
        dsb = (ds * scale).astype(BF16)
        dq_ref[...] = _dot(dsb, k, 1, 0).astype(dq_ref.dtype)
        dk_part = _dot(dsb, q, 0, 0)
        dv_part = _dot(pb, dob, 0, 0)

        @pl.when(i == 0)
        def _():
            dk_acc[...] = dk_part
            dv_acc[...] = dv_part

        @pl.when(i > 0)
        def _():
            dk_acc[...] += dk_part
            dv_acc[...] += dv_part

        @pl.when(i == nq - 1)
        def _():
            dk_ref[...] = dk_acc[...].astype(dk_ref.dtype)
            dv_ref[...] = dv_acc[...].astype(dv_ref.dtype)

    blk = pl.BlockSpec((tq, dh), lambda h, i: (i, h))
    kblk = pl.BlockSpec((m, dh), lambda h, i: (0, h))
    return _pcall(
        body, name=name, grid=(X_HEADS, nq),
        in_specs=[blk, kblk, pl.BlockSpec((m, dh), lambda h, i: (0, X_HEADS + h)), blk],
        out_specs=[blk, kblk, kblk],
        out_shape=[jax.ShapeDtypeStruct((s, d), BF16), jax.ShapeDtypeStruct((m, d), BF16), jax.ShapeDtypeStruct((m, d), BF16)],
        scratch_shapes=[pltpu.VMEM((m, dh), F32), pltpu.VMEM((m, dh), F32)],
        compiler_params=_params("parallel", "arbitrary"),
    )(qc, kv, kv, do)


def _local_step(x, mem, tgt, w):
    s, d = x.shape
    heads = d // SB_HEAD_DIM
    tm = _pick(s, (256, 128))
    tq = _pick(s, (256, 128))
    tc = _pick(d, (256, 128))
    g = {}

    def ffn_fwd(h, gname, wgu, wdown, tag):
        n = _rms_fwd(h, w[gname], tag + "_norm", tm)
        gu = _mm(n, w[wgu], name=tag + "_gu")
        act = _swiglu_fwd(gu, tag + "_act", tm)
        return n, gu, act, _mm(act, w[wdown], name=tag + "_down", out_dtype=F32, res=h, alpha=0.5)

    def ffn_bwd(dh, h, saved, gname, wgu, wdown, tag):
        n, gu, act = saved
        dhb = _rowcall(lambda v: 0.5 * v, [_whole(dh)], [], [(d, BF16)], tm=tm, name=tag + "_half")[0]
        g[wdown] = _mm(act, dhb, ta=True, name=tag + "_dwdown")
        dact = _mm(dhb, w[wdown], tb=True, name=tag + "_dact")
        dgu = _swiglu_bwd(dact, gu, tag + "_dgu", tm)
        g[wgu] = _mm(n, dgu, ta=True, name=tag + "_dwgu")
        dn = _mm(dgu, w[wgu], tb=True, name=tag + "_dn", out_dtype=F32)
        dh_in, g[gname] = _resid_rms_bwd(dh, dn, h, w[gname], tag + "_dnorm", tm)
        return dh_in

    n1, gu1, act1, h1 = ffn_fwd(x, "g_ffn1", "w_ffn1_gu", "w_ffn1_down", "ffn1")
    u = _rms_fwd(h1, w["g_mix"], "mix_norm", tm)
    proj = _mm(u, w["w_in"], name="mix_in")
    nd = d // SB_HEAD_DIM
    y_conv = _conv_fwd(proj, w["conv_w"], d, tc, "conv_fwd")
    sb_cols = (3 * nd, 4 * nd, 5 * nd)
    y_sb = _sb_fwd(proj, heads, sb_cols, tq, "sb_fwd")
    a_conv = _mm(y_conv, w["w_conv_out"], name="conv_out")
    a_sb = _mm(y_sb, w["w_attn_out"], name="attn_out")
    b_conv, b_sb = w["b_gate"][:, :d], w["b_gate"][:, d:]

    def merge(ac, asb, gcp, gsp, bc, bs):
        gc = _sigmoid(gcp.astype(F32) + bc)
        gs = _sigmoid(gsp.astype(F32) + bs)
        return gc * ac.astype(F32) + gs * asb.astype(F32)

    merged = _rowcall(merge, [_whole(a_conv), _whole(a_sb), (proj, 6, d), (proj, 7, d)], [b_conv, b_sb], [(d, BF16)],
                      tm=tm, name="merge")[0]
    h2 = _mm(merged, w["w_o"], name="mix_out", out_dtype=F32, res=h1)
    hn = _rms_fwd(h2, w["g_cross"], "cross_norm", tm)
    mn = _rms_fwd(mem, w["g_mem"], "mem_norm", _pick(mem.shape[0], (256, 128)))
    qc = _mm(hn, w["w_cq"], name="cross_q")
    kv = _mm(mn, w["w_ckv"], name="cross_kv")
    oc = _xattn_fwd(qc, kv, tq, "xattn_fwd")
    h3 = _mm(oc, w["w_co"], name="cross_out", out_dtype=F32, res=h2)
    n2, gu2, act2, h4 = ffn_fwd(h3, "g_ffn2", "w_ffn2_gu", "w_ffn2_down", "ffn2")

    def head(hb, tb, gb):
        xh, r = _xhat(hb)
        err = xh * gb - tb
        dy = err * (1.0 / d)
        dxh = dy * gb
        dx = r * (dxh - xh * jnp.mean(dxh * xh, axis=-1, keepdims=True))
        row_loss = 0.5 * jnp.mean(err * err, axis=-1, keepdims=True)
        return dx, dy * xh, jnp.broadcast_to(row_loss, (row_loss.shape[0], LANES))

    dh4, g["g_final"], loss_lanes = _rowcall(head, [_whole(h4), _whole(tgt)], [w["g_final"]], [(d, F32)], [d, LANES],
                                             tm=tm, name="loss_head")

    dh3 = ffn_bwd(dh4, h3, (n2, gu2, act2), "g_ffn2", "w_ffn2_gu", "w_ffn2_down", "ffn2")
    dh3b = _rowcall(lambda v: v, [_whole(dh3)], [], [(d, BF16)], tm=tm, name="cross_cast")[0]
    g["w_co"] = _mm(oc, dh3b, ta=True, name="cross_dwco")
    doc = _mm(dh3b, w["w_co"], tb=True, name="cross_doc")
    dqc, dk, dv = _xattn_bwd(qc, kv, doc, tq, "xattn_bwd")
    dkv = jnp.concatenate([dk, dv], axis=1)
    g["w_cq"] = _mm(hn, dqc, ta=True, name="cross_dwcq")
    g["w_ckv"] = _mm(mn, dkv, ta=True, name="cross_dwckv")
    dhn = _mm(dqc, w["w_cq"], tb=True, name="cross_dhn", out_dtype=F32)
    dmn = _mm(dkv, w["w_ckv"], tb=True, name="cross_dmn", out_dtype=F32)
    g["g_mem"] = _rowcall(lambda dy, xb: dy * _xhat(xb)[0], [_whole(dmn), _whole(mem)], [], [], [d],
                          tm=_pick(mem.shape[0], (256, 128)), name="mem_dnorm")[0]
    dh2, g["g_cross"] = _resid_rms_bwd(dh3, dhn, h2, w["g_cross"], "cross_dnorm", tm)

    dh2b = _rowcall(lambda v: v, [_whole(dh2)], [], [(d, BF16)], tm=tm, name="mix_cast")[0]
    g["w_o"] = _mm(merged, dh2b, ta=True, name="mix_dwo")
    dmerged = _mm(dh2b, w["w_o"], tb=True, name="mix_dmerged")

    def merge_bwd(dm, ac, asb, gcp, gsp, bc, bs):
        dm, ac, asb = dm.astype(F32), ac.astype(F32), asb.astype(F32)
        gc = _sigmoid(gcp.astype(F32) + bc)
        gs = _sigmoid(gsp.astype(F32) + bs)
        dgc = dm * ac * gc * (1.0 - gc)
        dgs = dm * asb * gs * (1.0 - gs)
        return dm * gc, dm * gs, dgc, dgs, dgc, dgs

    da_conv, da_sb, dgc, dgs, db_conv, db_sb = _rowcall(
        merge_bwd, [_whole(dmerged), _whole(a_conv), _whole(a_sb), (proj, 6, d), (proj, 7, d)], [b_conv, b_sb],
        [(d, BF16)] * 4, [d, d], tm=tm, name="merge_bwd")
    g["b_gate"] = jnp.concatenate([db_conv, db_sb], axis=1)
    g["w_conv_out"] = _mm(y_conv, da_conv, ta=True, name="conv_dwout")
    g["w_attn_out"] = _mm(y_sb, da_sb, ta=True, name="attn_dwout")
    dy_conv = _mm(da_conv, w["w_conv_out"], tb=True, name="conv_dy")
    dy_sb = _mm(da_sb, w["w_attn_out"], tb=True, name="attn_dy")
    dcb, dcc, dcx, g["conv_w"] = _conv_bwd(dy_conv, proj, w["conv_w"], d, tc, "conv_bwd")
    dq, dk_sb, dv_sb = _sb_bwd(proj, y_sb, dy_sb, heads, sb_cols, tq, "sb_bwd")
    dproj = jnp.concatenate([dcb, dcc, dcx, dq, dk_sb, dv_sb, dgc, dgs], axis=1)
    g["w_in"] = _mm(u, dproj, ta=True, name="mix_dwin")
    du = _mm(dproj, w["w_in"], tb=True, name="mix_du", out_dtype=F32)
    dh1, g["g_mix"] = _resid_rms_bwd(dh2, du, h1, w["g_mix"], "mix_dnorm", tm)
    dx = ffn_bwd(dh1, x, (n1, gu1, act1), "g_ffn1", "w_ffn1_gu", "w_ffn1_down", "ffn1")
    return loss_lanes, dx, g


MATS = (("w_ffn1_gu", "col"), ("w_ffn1_down", "row"), ("w_in", "col"), ("w_conv_out", "row"), ("w_attn_out", "row"),
        ("w_o", "row"), ("w_cq", "row"), ("w_ckv", "col"), ("w_co", "row"), ("w_ffn2_gu", "col"), ("w_ffn2_down", "row"))
VECS = ("g_ffn1", "g_mix", "g_cross", "g_mem", "g_ffn2", "g_final")
WEIGHTS = ("g_ffn1", "w_ffn1_gu", "w_ffn1_down", "g_mix", "w_in", "b_gate", "conv_w", "w_conv_out", "w_attn_out", "w_o",
           "g_cross", "g_mem", "w_cq", "w_ckv", "w_co", "g_ffn2", "w_ffn2_gu", "w_ffn2_down", "g_final")
CONV_ROWS = 8
ANY = pl.BlockSpec(memory_space=pl.ANY)


def _full_shape(kind, r, c):
    return (r, N_CHIPS * c) if kind == "col" else (N_CHIPS * r, c)


def _piece(ref, kind, r, c, chip, half):
    hr = r // 2
    if kind == "col":
        return ref.at[pl.ds(pl.multiple_of(half * hr, 16), hr), pl.ds(pl.multiple_of(chip * c, LANES), c)]
    return ref.at[pl.ds(pl.multiple_of(chip * r + half * hr, 16), hr), :]


def _shard_of(ref, kind, r, c, chip):
    if kind == "col":
        return ref.at[:, pl.ds(pl.multiple_of(chip * c, LANES), c)]
    return ref.at[pl.ds(pl.multiple_of(chip * r, 16), r), :]


def _place():
    x, y, c = lax.axis_index("x"), lax.axis_index("y"), lax.axis_index("c")
    others = [(1 - x, y), (x, 1 - y), (1 - x, 1 - y)]
    return x, y, c, 2 * x + y, others


def _remote(src, dst, send_sem, recv_sem, to):
    return pltpu.make_async_remote_copy(src_ref=src, dst_ref=dst, send_sem=send_sem, recv_sem=recv_sem,
                                        device_id=to, device_id_type=MESH)


def _gather_weights(shards, conv_shard):
    dims = [(kind, *sh.shape) for (_, kind), sh in zip(MATS, shards)]
    nw = len(shards)
    cc = conv_shard.shape[1]

    def body(*refs):
        shard_refs, conv_ref = refs[:nw], refs[nw]
        full_refs, conv_full = refs[nw + 1:2 * nw + 1], refs[2 * nw + 1]
        s1, r1, s2, r2, loc, cs, cr, cl = refs[2 * nw + 2:]
        x, y, c, me, others = _place()
        sib = (x, y, 1 - c)

        def first(wi, k, chip_from, to):
            kind, r, cw = dims[wi]
            src = shard_refs[wi].at[pl.ds(pl.multiple_of(c * (r // 2), 16), r // 2), :]
            return _remote(src, _piece(full_refs[wi], kind, r, cw, chip_from, c), s1.at[wi, k], r1.at[wi, k], to)

        def second(wi, k, chip_from, half):
            kind, r, cw = dims[wi]
            pc = _piece(full_refs[wi], kind, r, cw, chip_from, half)
            return _remote(pc, pc, s2.at[wi, k], r2.at[wi, k], sib)

        def conv(k, chip_from, to):
            dst = conv_full.at[:, pl.ds(pl.multiple_of(chip_from * cc, LANES), cc)]
            return _remote(conv_ref, dst, cs.at[k], cr.at[k], to)

        local = [pltpu.make_async_copy(shard_refs[wi], _shard_of(full_refs[wi], *dims[wi], me), loc.at[wi]) for wi in range(nw)]
        conv_local = pltpu.make_async_copy(conv_ref, conv_full.at[:, pl.ds(pl.multiple_of(me * cc, LANES), cc)], cl.at[0])
        for cp in local:
            cp.start()
        conv_local.start()
        for k, (ox, oy) in enumerate(others):
            conv(k, me, (ox, oy, c)).start()
        for wi in range(nw):
            for k, (ox, oy) in enumerate(others):
                first(wi, k, me, (ox, oy, c)).start()
        for wi in range(nw):
            for k, (ox, oy) in enumerate(others):
                first(wi, k, 2 * ox + oy, (x, y, c)).wait_recv()
                second(wi, k, 2 * ox + oy, c).start()
        for wi in range(nw):
            for k, (ox, oy) in enumerate(others):
                second(wi, k, 2 * ox + oy, 1 - c).wait_recv()
        for k, (ox, oy) in enumerate(others):
            conv(k, 2 * ox + oy, (x, y, c)).wait_recv()
            conv(k, me, (ox, oy, c)).wait_send()
        for wi in range(nw):
            for k, (ox, oy) in enumerate(others):
                first(wi, k, me, (ox, oy, c)).wait_send()
                second(wi, k, 2 * ox + oy, c).wait_send()
        for cp in local:
            cp.wait()
        conv_local.wait()

    out_shape = [jax.ShapeDtypeStruct(_full_shape(*dm), BF16) for dm in dims]
    out_shape.append(jax.ShapeDtypeStruct((CONV_ROWS, N_CHIPS * cc), F32))
    dma = pltpu.SemaphoreType.DMA
    outs = _pcall(
        body, name="gather_weights", in_specs=[ANY] * (nw + 1), out_specs=[ANY] * (nw + 1), out_shape=out_shape,
        scratch_shapes=[dma((nw, 3)), dma((nw, 3)), dma((nw, 3)), dma((nw, 3)), dma((nw,)), dma((3,)), dma((3,)), dma((1,))],
    )(*shards, conv_shard)
    return outs[:nw], outs[nw]


def _allreduce_small(packed):
    rows, n = packed.shape

    def body(in_ref, out_ref, gath_ref, send_sems, recv_sems):
        x, y, c = lax.axis_index("x"), lax.axis_index("y"), lax.axis_index("c")
        me = 4 * x + 2 * y + c

        def peer(rel):
            return (x ^ (rel >> 2 & 1), y ^ (rel >> 1 & 1), c ^ (rel & 1))

        def copy(rel, slot, to):
            return _remote(in_ref, gath_ref.at[slot], send_sems.at[rel - 1], recv_sems.at[rel - 1], to)

        for rel in range(1, N_DEV):
            copy(rel, me, peer(rel)).start()
        gath_ref[me] = in_ref[...]
        for rel in range(1, N_DEV):
            px, py, pc = peer(rel)
            copy(rel, 4 * px + 2 * py + pc, (x, y, c)).wait_recv()
        for rel in range(1, N_DEV):
            copy(rel, me, peer(rel)).wait_send()
        tot = gath_ref[0]
        for dev in range(1, N_DEV):
            tot = tot + gath_ref[dev]
        out_ref[...] = tot

    vm = pl.BlockSpec(memory_space=pltpu.VMEM)
    return _pcall(
        body, name="allreduce_small", in_specs=[vm], out_specs=[vm, vm],
        out_shape=[jax.ShapeDtypeStruct((rows, n), F32), jax.ShapeDtypeStruct((N_DEV, rows, n), F32)],
        scratch_shapes=[pltpu.SemaphoreType.DMA((N_DEV - 1,)), pltpu.SemaphoreType.DMA((N_DEV - 1,))],
    )(packed)[0]


def _rs_cores(grads, dims):
    nw = len(grads)

    def body(*refs):
        g_refs, own_refs, got_refs = refs[:nw], refs[nw:2 * nw], refs[2 * nw:3 * nw]
        ss, rs, ls = refs[3 * nw:]
        x, y, c, _, _ = _place()
        sib = (x, y, 1 - c)

        def give(wi, chip):
            return _remote(_piece(g_refs[wi], *dims[wi], chip, 1 - c), got_refs[wi].at[chip], ss.at[wi, chip], rs.at[wi, chip], sib)

        def keep(wi, chip):
            return pltpu.make_async_copy(_piece(g_refs[wi], *dims[wi], chip, c), own_refs[wi].at[chip], ls.at[wi, chip])

        every = [(wi, chip) for wi in range(nw) for chip in range(N_CHIPS)]
        for wi, chip in every:
            give(wi, chip).start()
            keep(wi, chip).start()
        for wi, chip in every:
            give(wi, chip).wait()
            keep(wi, chip).wait()

    out_shape = [jax.ShapeDtypeStruct((N_CHIPS, r // 2, cw), BF16) for (_, r, cw) in dims] * 2
    dma = pltpu.SemaphoreType.DMA
    outs = _pcall(
        body, name="rs_cores", in_specs=[ANY] * nw, out_specs=[ANY] * (2 * nw), out_shape=out_shape,
        scratch_shapes=[dma((nw, N_CHIPS)), dma((nw, N_CHIPS)), dma((nw, N_CHIPS))],
    )(*grads)
    return outs[:nw], outs[nw:]


def _rs_chips(parts):
    nw = len(parts)

    def body(*refs):
        p_refs, got_refs = refs[:nw], refs[nw:2 * nw]
        ss, rs, ls = refs[2 * nw:]
        x, y, c, me, others = _place()

        def send(wi, k, chip_to, chip_from, to):
            return _remote(p_refs[wi].at[chip_to], got_refs[wi].at[chip_from], ss.at[wi, k], rs.at[wi, k], to)

        def keep(wi):
            return pltpu.make_async_copy(p_refs[wi].at[me], got_refs[wi].at[me], ls.at[wi])

        for wi in range(nw):
            for k, (ox, oy) in enumerate(others):
                send(wi, k, 2 * ox + oy, me, (ox, oy, c)).start()
            keep(wi).start()
        for wi in range(nw):
            for k, (ox, oy) in enumerate(others):
                send(wi, k, me, 2 * ox + oy, (x, y, c)).wait_recv()
        for wi in range(nw):
            for k, (ox, oy) in enumerate(others):
                send(wi, k, 2 * ox + oy, me, (ox, oy, c)).wait_send()
            keep(wi).wait()

    dma = pltpu.SemaphoreType.DMA
    return _pcall(
        body, name="rs_chips", in_specs=[ANY] * nw, out_specs=[ANY] * nw,
        out_shape=[jax.ShapeDtypeStruct(p.shape, p.dtype) for p in parts],
        scratch_shapes=[dma((nw, 3)), dma((nw, 3)), dma((nw,))],
    )(*parts)


def _share_halves(halves):
    nw = len(halves)

    def body(*refs):
        h_refs, out_refs = refs[:nw], refs[nw:2 * nw]
        ss, rs, ls = refs[2 * nw:]
        x, y, c, _, _ = _place()
        sib = (x, y, 1 - c)

        def send(wi, half):
            return _remote(h_refs[wi], out_refs[wi].at[half], ss.at[wi], rs.at[wi], sib)

        def keep(wi):
            return pltpu.make_async_copy(h_refs[wi], out_refs[wi].at[c], ls.at[wi])

        for wi in range(nw):
            send(wi, c).start()
            keep(wi).start()
        for wi in range(nw):
            send(wi, 1 - c).wait_recv()
        for wi in range(nw):
            send(wi, c).wait_send()
            keep(wi).wait()

    dma = pltpu.SemaphoreType.DMA
    return _pcall(
        body, name="share_halves", in_specs=[ANY] * nw, out_specs=[ANY] * nw,
        out_shape=[jax.ShapeDtypeStruct((2, *h.shape), F32) for h in halves],
        scratch_shapes=[dma((nw,)), dma((nw,)), dma((nw,))],
    )(*halves)


def _rows_per_block(n, c, limit_bytes=1 << 20):
    best = None
    for tm in range(16, n + 1, 16):
        if n % tm == 0 and tm * c * 4 <= limit_bytes:
            best = tm
    return best or n


def _sum_pair(a, b, name):
    n, c = a.shape
    return _rowcall(lambda u, v: u.astype(F32) + v.astype(F32), [_whole(a), _whole(b)], [], [(c, BF16)],
                    tm=_rows_per_block(n, c), name=name)[0]


def _sum_chips(got, name):
    _, n, c = got.shape
    tm = _rows_per_block(n, c)

    def body(g_ref, o_ref):
        tot = g_ref[0].astype(F32)
        for p in range(1, N_CHIPS):
            tot = tot + g_ref[p].astype(F32)
        o_ref[...] = tot

    return _pcall(
        body, name=name, grid=(n // tm,), in_specs=[pl.BlockSpec((N_CHIPS, tm, c), lambda i: (0, i, 0))],
        out_specs=pl.BlockSpec((tm, c), lambda i: (i, 0)), out_shape=jax.ShapeDtypeStruct((n, c), F32),
        compiler_params=_params("parallel"),
    )(got)


def _adamw(g, w, m, v, name):
    n, c = g.shape
    c1 = 1.0 - ADAM_B1 ** ADAM_STEP
    c2 = 1.0 - ADAM_B2 ** ADAM_STEP

    def fn(gb, wb, mb, vb):
        m_new = ADAM_B1 * mb + (1.0 - ADAM_B1) * gb
        v_new = ADAM_B2 * vb + (1.0 - ADAM_B2) * (gb * gb)
        delta = -ADAM_LR * ((m_new / c1) / (jnp.sqrt(v_new / c2) + ADAM_EPS) + ADAM_WD * wb)
        return gb, delta, m_new, v_new

    tm = _rows_per_block(n, c) if n % 16 == 0 else n
    return _rowcall(fn, [_whole(g), _whole(w), _whole(m), _whole(v)], [], [(c, F32)] * 4, tm=tm, name=name)


PACK_ROWS = 16


def _pack_rows(parts, width, name):
    assert sum(p.shape[0] for p in parts) <= PACK_ROWS

    def body(*refs):
        out_ref = refs[-1]
        out_ref[...] = jnp.zeros_like(out_ref)
        at = 0
        for r in refs[:-1]:
            k, n = r.shape
            if n == width:
                out_ref[at:at + k, :] = r[...]
            else:
                out_ref[at:at + k, :] = jnp.broadcast_to(r[:, :1], (k, width))
            at += k

    vm = pl.BlockSpec(memory_space=pltpu.VMEM)
    return _pcall(body, name=name, in_specs=[vm] * len(parts), out_specs=vm,
                  out_shape=jax.ShapeDtypeStruct((PACK_ROWS, width), F32))(*parts)


def _cast_shard(wm, name):
    n, c = wm.shape
    return _rowcall(lambda v: v, [_whole(wm)], [], [(c, BF16)], tm=_rows_per_block(n, c), name=name)[0]


def _step(x, mem, tgt, wts, m_in, v_in):
    d = x.shape[-1]
    cc = wts["conv_w"].shape[1]
    shards = [_cast_shard(wts[n], "cast_" + n) for n, _ in MATS]
    dims = [(kind, *sh.shape) for (_, kind), sh in zip(MATS, shards)]
    conv_pad = jnp.pad(wts["conv_w"], ((0, CONV_ROWS - CONV_K), (0, 0)))
    fulls, conv_full = _gather_weights(shards, conv_pad)
    w = {n: f for (n, _), f in zip(MATS, fulls)}
    w["conv_w"] = conv_full[:CONV_K]
    for n in VECS + ("b_gate",):
        w[n] = wts[n].reshape(1, -1)

    loss_lanes, dx, g = _local_step(x[0], mem[0], tgt[0], w)

    own, got = _rs_cores([g[n] for n, _ in MATS], dims)
    flat = lambda a: a.reshape(-1, a.shape[-1])
    parts = [_sum_pair(flat(o), flat(t), "sum_cores_" + n).reshape(o.shape) for (n, _), o, t in zip(MATS, own, got)]
    landed = _rs_chips(parts)
    halves = [_sum_chips(t, "sum_chips_" + n) for (n, _), t in zip(MATS, landed)]
    both = _share_halves(halves)
    grads = {n: flat(b) for (n, _), b in zip(MATS, both)}

    rows = [g[n] for n in VECS] + [g["b_gate"][:, :d], g["b_gate"][:, d:], g["conv_w"], loss_lanes]
    red = _allreduce_small(_pack_rows(rows, d, "pack_small"))
    for i, n in enumerate(VECS):
        grads[n] = red[i:i + 1]
    nv = len(VECS)
    grads["b_gate"] = jnp.concatenate([red[nv:nv + 1], red[nv + 1:nv + 2]], axis=1)
    me = 2 * lax.axis_index("x") + lax.axis_index("y")
    grads["conv_w"] = lax.dynamic_slice_in_dim(red[nv + 2:nv + 2 + CONV_K], me * cc, cc, axis=1)
    loss = red[nv + 2 + CONV_K, 0]

    out = {}
    for n in WEIGHTS:
        shape = wts[n].shape
        as2d = (lambda a: a.reshape(1, -1)) if len(shape) == 1 else (lambda a: a)
        res = _adamw(grads[n], as2d(wts[n]), as2d(m_in[n]), as2d(v_in[n]), "adamw_" + n)
        out[n] = [r.reshape(shape) for r in res]
    return (loss, dx[None], *[out[n][0] for n in WEIGHTS], *[out[n][1] for n in WEIGHTS],
            *[out[n][2] for n in WEIGHTS], *[out[n][3] for n in WEIGHTS])


def kernel(x, mem, g_ffn1, w_ffn1_gu, w_ffn1_down, g_mix, w_in, b_gate, conv_w, w_conv_out, w_attn_out, w_o, g_cross, g_mem, w_cq, w_ckv, w_co, g_ffn2, w_ffn2_gu, w_ffn2_down, g_final, loss_target, m_g_ffn1, m_w_ffn1_gu, m_w_ffn1_down, m_g_mix, m_w_in, m_b_gate, m_conv_w, m_w_conv_out, m_w_attn_out, m_w_o, m_g_cross, m_g_mem, m_w_cq, m_w_ckv, m_w_co, m_g_ffn2, m_w_ffn2_gu, m_w_ffn2_down, m_g_final, v_g_ffn1, v_w_ffn1_gu, v_w_ffn1_down, v_g_mix, v_w_in, v_b_gate, v_conv_w, v_w_conv_out, v_w_attn_out, v_w_o, v_g_cross, v_g_mem, v_w_cq, v_w_ckv, v_w_co, v_g_ffn2, v_w_ffn2_gu, v_w_ffn2_down, v_g_final):
    given = dict(locals())
    wts = {n: given[n] for n in WEIGHTS}
    m_in = {n: given["m_" + n] for n in WEIGHTS}
    v_in = {n: given["v_" + n] for n in WEIGHTS}
    return _step(x, mem, loss_target, wts, m_in, v_in)
```

```python
import functools

import jax
import jax.numpy as jnp
from jax import lax
from jax.experimental import pallas as pl
from jax.experimental.pallas import tpu as pltpu

F32 = jnp.float32
BF16 = jnp.bfloat16
MESH = pl.DeviceIdType.MESH

V7X_VMEM_LIMIT_BYTES = 48 * 1024 * 1024
LANES = 128
SB_HEAD_DIM = 128
X_HEADS = 4
CONV_K = 3
RMS_EPS = 1e-6
N_CHIPS = 4
N_DEV = 8
ADAM_LR, ADAM_B1, ADAM_B2, ADAM_EPS, ADAM_WD, ADAM_STEP = 0.001, 0.9, 0.999, 1e-08, 0.01, 10


def _pcall(body, **kw):
    return pl.pallas_call(body, **kw)


def _params(*sem):
    return pltpu.CompilerParams(dimension_semantics=sem, vmem_limit_bytes=V7X_VMEM_LIMIT_BYTES)


def _pick(dim, cands):
    for c in cands:
        if dim % c == 0:
            return c
    return dim


def _dot(a, b, ca, cb):
    return lax.dot_general(a, b, (((ca,), (cb,)), ((), ())), preferred_element_type=F32)


def _mm(a, b, *, name, ta=False, tb=False, out_dtype=BF16, res=None, alpha=1.0, tm=None, tn=None, tk=None):
    m, k = (a.shape[1], a.shape[0]) if ta else a.shape
    n = b.shape[0] if tb else b.shape[1]
    assert k == (b.shape[1] if tb else b.shape[0]), (a.shape, b.shape, ta, tb)
    tm = tm or _pick(m, (512, 256, 128))
    tn = tn or _pick(n, (1024, 512, 256, 128))
    tk = tk or (k if k <= 2816 else _pick(k, (1024, 512, 256, 128)))
    nk = k // tk
    assert m % tm == 0 and n % tn == 0 and k % tk == 0
    a_spec = pl.BlockSpec((tk, tm), lambda i, j, kk: (kk, i)) if ta else pl.BlockSpec((tm, tk), lambda i, j, kk: (i, kk))
    b_spec = pl.BlockSpec((tn, tk), lambda i, j, kk: (j, kk)) if tb else pl.BlockSpec((tk, tn), lambda i, j, kk: (kk, j))
    o_spec = pl.BlockSpec((tm, tn), lambda i, j, kk: (i, j))
    ca, cb = (0 if ta else 1), (1 if tb else 0)

    def body(*refs):
        if res is None:
            a_ref, b_ref, o_ref = refs[:3]
            res_ref = None
            scratch = refs[3:]
        else:
            a_ref, b_ref, res_ref, o_ref = refs[:4]
            scratch = refs[4:]

        def finish(acc):
            val = acc if alpha == 1.0 else alpha * acc
            if res_ref is not None:
                val = res_ref[...].astype(F32) + val
            o_ref[...] = val.astype(o_ref.dtype)

        part = _dot(a_ref[...].astype(BF16), b_ref[...].astype(BF16), ca, cb)
        if nk == 1:
            finish(part)
        else:
            acc_ref = scratch[0]
            kk = pl.program_id(2)

            @pl.when(kk == 0)
            def _():
                acc_ref[...] = part

            @pl.when(kk > 0)
            def _():
                acc_ref[...] += part

            @pl.when(kk == nk - 1)
            def _():
                finish(acc_ref[...])

    ins = [a, b] + ([] if res is None else [res])
    in_specs = [a_spec, b_spec] + ([] if res is None else [o_spec])
    return _pcall(
        body, name=name, grid=(m // tm, n // tn, nk), in_specs=in_specs, out_specs=o_spec,
        out_shape=jax.ShapeDtypeStruct((m, n), out_dtype),
        scratch_shapes=[pltpu.VMEM((tm, tn), F32)] if nk > 1 else [],
        compiler_params=_params("parallel", "parallel", "arbitrary"),
    )(*ins)


def _rowcall(fn, rows, consts, outs, accs=(), *, tm, name):
    s = rows[0][0].shape[0]
    assert s % tm == 0
    n_in, n_out = len(rows) + len(consts), len(outs)

    def body(*refs):
        vals = fn(*[r[...] for r in refs[:n_in]])
        vals = vals if isinstance(vals, (tuple, list)) else (vals,)
        for o_ref, v in zip(refs[n_in:n_in + n_out], vals[:n_out]):
            o_ref[...] = v.astype(o_ref.dtype)
        if accs:
            first = pl.program_id(0) == 0
            for a_ref, v in zip(refs[n_in + n_out:], vals[n_out:]):
                tot = jnp.sum(v.astype(F32), axis=0, keepdims=True)

                @pl.when(first)
                def _(a_ref=a_ref, tot=tot):
                    a_ref[...] = tot

                @pl.when(jnp.logical_not(first))
                def _(a_ref=a_ref, tot=tot):
                    a_ref[...] += tot

    in_specs = [pl.BlockSpec((tm, w), lambda i, cb=cb: (i, cb)) for (_, cb, w) in rows]
    in_specs += [pl.BlockSpec(c.shape, lambda i: (0, 0)) for c in consts]
    out_specs = [pl.BlockSpec((tm, w), lambda i: (i, 0)) for (w, _) in outs]
    out_specs += [pl.BlockSpec((1, w), lambda i: (0, 0)) for w in accs]
    out_shape = [jax.ShapeDtypeStruct((s, w), dt) for (w, dt) in outs]
    out_shape += [jax.ShapeDtypeStruct((1, w), F32) for w in accs]
    return _pcall(
        body, name=name, grid=(s // tm,), in_specs=in_specs, out_specs=out_specs, out_shape=out_shape,
        compiler_params=_params("arbitrary" if accs else "parallel"),
    )(*[r[0] for r in rows], *consts)


def _whole(a):
    return (a, 0, a.shape[1])


def _xhat(x):
    x = x.astype(F32)
    r = lax.rsqrt(jnp.mean(x * x, axis=-1, keepdims=True) + RMS_EPS)
    return x * r, r


def _rms_bwd(dy, x, g):
    xh, r = _xhat(x)
    dxh = dy.astype(F32) * g
    dx = r * (dxh - xh * jnp.mean(dxh * xh, axis=-1, keepdims=True))
    return dx, dy.astype(F32) * xh


def _sigmoid(x):
    return 1.0 / (1.0 + jnp.exp(-x))


def _rms_fwd(x, g, name, tm):
    d = x.shape[1]
    return _rowcall(lambda xb, gb: _xhat(xb)[0] * gb, [_whole(x)], [g], [(d, BF16)], tm=tm, name=name)[0]


def _swiglu_fwd(gu, name, tm):
    f = gu.shape[1] // 2

    def fn(gate, up):
        gate, up = gate.astype(F32), up.astype(F32)
        return gate * _sigmoid(gate) * up

    return _rowcall(fn, [(gu, 0, f), (gu, 1, f)], [], [(f, BF16)], tm=tm, name=name)[0]


def _swiglu_bwd(dact, gu, name, tm):
    f = gu.shape[1] // 2

    def fn(da, gate, up):
        da, gate, up = da.astype(F32), gate.astype(F32), up.astype(F32)
        sg = _sigmoid(gate)
        silu = gate * sg
        dgate = da * up * (sg + silu * (1.0 - sg))
        return jnp.concatenate([dgate, da * silu], axis=1)

    return _rowcall(fn, [_whole(dact), (gu, 0, f), (gu, 1, f)], [], [(2 * f, BF16)], tm=tm, name=name)[0]


def _resid_rms_bwd(dh, dn, x, g, name, tm):
    d = x.shape[1]

    def fn(dhb, dnb, xb, gb):
        dx, dg = _rms_bwd(dnb, xb, gb)
        return dhb.astype(F32) + dx, dg

    return _rowcall(fn, [_whole(dh), _whole(dn), _whole(x)], [g], [(d, F32)], [d], tm=tm, name=name)


def _shift_down(p, k):
    if k == 0:
        return p
    rows = lax.broadcasted_iota(jnp.int32, p.shape, 0)
    return jnp.where(rows >= k, pltpu.roll(p, k, 0), 0.0)


def _shift_up(p, k):
    if k == 0:
        return p
    s = p.shape[0]
    rows = lax.broadcasted_iota(jnp.int32, p.shape, 0)
    return jnp.where(rows < s - k, pltpu.roll(p, s - k, 0), 0.0)


def _conv_fwd(proj, conv_w, d, tc, name):
    s = proj.shape[0]
    nb = d // tc

    def body(cb_ref, cc_ref, cx_ref, w_ref, y_ref):
        p = cc_ref[...].astype(F32) * cx_ref[...].astype(F32)
        w = w_ref[...]
        acc = p * w[CONV_K - 1:CONV_K, :]
        for k in range(1, CONV_K):
            acc = acc + _shift_down(p, k) * w[CONV_K - 1 - k:CONV_K - k, :]
        y_ref[...] = (cb_ref[...].astype(F32) * acc).astype(y_ref.dtype)

    col = lambda off: pl.BlockSpec((s, tc), lambda j: (0, off * nb + j))
    return _pcall(
        body, name=name, grid=(nb,), in_specs=[col(0), col(1), col(2), pl.BlockSpec((CONV_K, tc), lambda j: (0, j))],
        out_specs=pl.BlockSpec((s, tc), lambda j: (0, j)), out_shape=jax.ShapeDtypeStruct((s, d), BF16),
        compiler_params=_params("parallel"),
    )(proj, proj, proj, conv_w)


def _conv_bwd(dy, proj, conv_w, d, tc, name):
    s = proj.shape[0]
    nb = d // tc

    def body(dy_ref, cb_ref, cc_ref, cx_ref, w_ref, dcb_ref, dcc_ref, dcx_ref, dw_ref):
        cc, cx = cc_ref[...].astype(F32), cx_ref[...].astype(F32)
        p = cc * cx
        w = w_ref[...]
        dyv = dy_ref[...].astype(F32)
        shifted = [_shift_down(p, CONV_K - 1 - k) for k in range(CONV_K)]
        conv = shifted[0] * w[0:1, :]
        for k in range(1, CONV_K):
            conv = conv + shifted[k] * w[k:k + 1, :]
        dcb_ref[...] = (dyv * conv).astype(dcb_ref.dtype)
        ds = dyv * cb_ref[...].astype(F32)
        dp = ds * w[CONV_K - 1:CONV_K, :]
        for k in range(1, CONV_K):
            dp = dp + _shift_up(ds, k) * w[CONV_K - 1 - k:CONV_K - k, :]
        dcc_ref[...] = (dp * cx).astype(dcc_ref.dtype)
        dcx_ref[...] = (dp * cc).astype(dcx_ref.dtype)
        for k in range(CONV_K):
            dw_ref[k:k + 1, :] = jnp.sum(ds * shifted[k], axis=0, keepdims=True)

    col = lambda off: pl.BlockSpec((s, tc), lambda j: (0, off * nb + j))
    blk = pl.BlockSpec((s, tc), lambda j: (0, j))
    wblk = pl.BlockSpec((CONV_K, tc), lambda j: (0, j))
    act = jax.ShapeDtypeStruct((s, d), BF16)
    return _pcall(
        body, name=name, grid=(nb,), in_specs=[blk, col(0), col(1), col(2), wblk],
        out_specs=[blk, blk, blk, wblk], out_shape=[act, act, act, jax.ShapeDtypeStruct((CONV_K, d), F32)],
        compiler_params=_params("parallel"),
    )(dy, proj, proj, proj, conv_w)


def _split_bf16(x):
    hi = x.astype(BF16)
    return hi, (x - hi.astype(F32)).astype(BF16)


def _sb_tile(q, kj, scale, carry, tri, mask):
    z = _dot(q, kj, 1, 1) * scale
    lsz = jnp.minimum(z, 0.0) - jnp.log(1.0 + jnp.exp(-jnp.abs(z)))
    l1m = lsz - z
    if mask is not None:
        l1m = jnp.where(mask, l1m, 0.0)
    hi, lo = _split_bf16(l1m)
    after = carry + _dot(hi, tri, 1, 0) + _dot(lo, tri, 1, 0)
    a = jnp.exp(lsz + after)
    if mask is not None:
        a = jnp.where(mask, a, 0.0)
    return lsz, l1m, a.astype(BF16)


def _sb_fwd(proj, heads, col0, tq, name):
    s = proj.shape[0]
    dh = SB_HEAD_DIM
    nq = s // tq
    scale = dh ** -0.5

    def body(q_ref, k_ref, v_ref, o_ref):
        i = pl.program_id(1)
        q = q_ref[...]
        row = lax.broadcasted_iota(jnp.int32, (tq, tq), 0)
        col = lax.broadcasted_iota(jnp.int32, (tq, tq), 1)
        tri = (row > col).astype(BF16)

        def tile(j, carry, acc, mask):
            start = pl.multiple_of(j * tq, tq)
            kj = k_ref[pl.ds(start, tq), :]
            vj = v_ref[pl.ds(start, tq), :]
            _, l1m, ab = _sb_tile(q, kj, scale, carry, tri, mask)
            return carry + jnp.sum(l1m, axis=1, keepdims=True), acc + _dot(ab, vj, 1, 0)

        carry, acc = tile(i, jnp.zeros((tq, 1), F32), jnp.zeros((tq, dh), F32), col < row)
        carry, acc = lax.fori_loop(0, i, lambda t, ca: tile(i - 1 - t, ca[0], ca[1], None), (carry, acc))
        o_ref[...] = acc

    qspec = pl.BlockSpec((tq, dh), lambda h, i: (i, col0[0] + h))
    kspec = pl.BlockSpec((s, dh), lambda h, i: (0, col0[1] + h))
    vspec = pl.BlockSpec((s, dh), lambda h, i: (0, col0[2] + h))
    return _pcall(
        body, name=name, grid=(heads, nq), in_specs=[qspec, kspec, vspec],
        out_specs=pl.BlockSpec((tq, dh), lambda h, i: (i, h)), out_shape=jax.ShapeDtypeStruct((s, heads * dh), F32),
        compiler_params=_params("parallel", "parallel"),
    )(proj, proj, proj)


def _sb_bwd(proj, o, do, heads, col0, tq, name):
    s = proj.shape[0]
    dh = SB_HEAD_DIM
    nq = s // tq
    scale = dh ** -0.5

    def body(q_ref, k_ref, v_ref, o_ref, do_ref, dq_ref, dk_ref, dv_ref, dk_acc, dv_acc):
        i = pl.program_id(1)

        @pl.when(i == 0)
        def _():
            dk_acc[...] = jnp.zeros_like(dk_acc)
            dv_acc[...] = jnp.zeros_like(dv_acc)

        q = q_ref[...]
        dob = do_ref[...].astype(BF16)
        delta = jnp.sum(dob.astype(F32) * o_ref[...], axis=1, keepdims=True)
        row = lax.broadcasted_iota(jnp.int32, (tq, tq), 0)
        col = lax.broadcasted_iota(jnp.int32, (tq, tq), 1)
        tri = (row > col).astype(BF16)
        tri_incl = (row >= col).astype(BF16)

        def tile(j, carry_l, carry_g, dq, mask):
            start = pl.multiple_of(j * tq, tq)
            kj = k_ref[pl.ds(start, tq), :]
            vj = v_ref[pl.ds(start, tq), :]
            lsz, l1m, ab = _sb_tile(q, kj, scale, carry_l, tri, mask)
            g = _dot(dob, vj, 1, 1) * ab.astype(F32)
            hi, lo = _split_bf16(g)
            left = delta - (carry_g + _dot(hi, tri_incl, 1, 0) + _dot(lo, tri_incl, 1, 0))
            beta = jnp.exp(lsz)
            dz = g * (1.0 - beta) - left * beta
            if mask is not None:
                dz = jnp.where(mask, dz, 0.0)
            dzb = (dz * scale).astype(BF16)
            dk_acc[pl.ds(start, tq), :] += _dot(dzb, q, 0, 0)
            dv_acc[pl.ds(start, tq), :] += _dot(ab, dob, 0, 0)
            return (carry_l + jnp.sum(l1m, axis=1, keepdims=True), carry_g + jnp.sum(g, axis=1, keepdims=True),
                    dq + _dot(dzb, kj, 1, 0))

        zero = jnp.zeros((tq, 1), F32)
        state = tile(i, zero, zero, jnp.zeros((tq, dh), F32), col < row)
        state = lax.fori_loop(0, i, lambda t, st: tile(i - 1 - t, st[0], st[1], st[2], None), state)
        dq_ref[...] = state[2].astype(dq_ref.dtype)

        @pl.when(i == nq - 1)
        def _():
            dk_ref[...] = dk_acc[...].astype(dk_ref.dtype)
            dv_ref[...] = dv_acc[...].astype(dv_ref.dtype)

    qspec = pl.BlockSpec((tq, dh), lambda h, i: (i, col0[0] + h))
    kspec = pl.BlockSpec((s, dh), lambda h, i: (0, col0[1] + h))
    vspec = pl.BlockSpec((s, dh), lambda h, i: (0, col0[2] + h))
    blk = pl.BlockSpec((tq, dh), lambda h, i: (i, h))
    full = pl.BlockSpec((s, dh), lambda h, i: (0, h))
    act = jax.ShapeDtypeStruct((s, heads * dh), BF16)
    return _pcall(
        body, name=name, grid=(heads, nq), in_specs=[qspec, kspec, vspec, blk, blk],
        out_specs=[blk, full, full], out_shape=[act, act, act],
        scratch_shapes=[pltpu.VMEM((s, dh), F32), pltpu.VMEM((s, dh), F32)],
        compiler_params=_params("parallel", "arbitrary"),
    )(proj, proj, proj, o, do)


def _xattn_probs(q, k, scale):
    sc = _dot(q, k, 1, 1) * scale
    e = jnp.exp(sc - jnp.max(sc, axis=1, keepdims=True))
    return e / jnp.sum(e, axis=1, keepdims=True)


def _xattn_fwd(qc, kv, tq, name):
    s, d = qc.shape
    m = kv.shape[0]
    dh = d // X_HEADS
    scale = dh ** -0.5

    def body(q_ref, k_ref, v_ref, o_ref):
        p = _xattn_probs(q_ref[...], k_ref[...], scale)
        o_ref[...] = _dot(p.astype(BF16), v_ref[...], 1, 0).astype(o_ref.dtype)

    blk = pl.BlockSpec((tq, dh), lambda h, i: (i, h))
    return _pcall(
        body, name=name, grid=(X_HEADS, s // tq),
        in_specs=[blk, pl.BlockSpec((m, dh), lambda h, i: (0, h)), pl.BlockSpec((m, dh), lambda h, i: (0, X_HEADS + h))],
        out_specs=blk, out_shape=jax.ShapeDtypeStruct((s, d), BF16), compiler_params=_params("parallel", "parallel"),
    )(qc, kv, kv)


def _xattn_bwd(qc, kv, do, tq, name):
    s, d = qc.shape
    m = kv.shape[0]
    dh = d // X_HEADS
    scale = dh ** -0.5
    nq = s // tq

    def body(q_ref, k_ref, v_ref, do_ref, dq_ref, dk_ref, dv_ref, dk_acc, dv_acc):
        i = pl.program_id(1)
        q, k, v = q_ref[...], k_ref[...], v_ref[...]
        dob = do_ref[...].astype(BF16)
        p = _xattn_probs(q, k, scale)
        pb = p.astype(BF16)
        dp = _dot(dob, v, 1, 1)
        ds = pb.astype(F32) * (dp - jnp.sum(dp * pb.astype(F32), axis=1, keepdims=True))
        dsb = (ds * scale).astype(BF16)
        dq_ref[...] = _dot(dsb, k, 1, 0).astype(dq_ref.dtype)
        dk_part = _dot(dsb, q, 0, 0)
        dv_part = _dot(pb, dob, 0, 0)

        @pl.when(i == 0)
        def _():
            dk_acc[...] = dk_part
            dv_acc[...] = dv_part

        @pl.when(i > 0)
        def _():
            dk_acc[...] += dk_part
            dv_acc[...] += dv_part

        @pl.when(i == nq - 1)
        def _():
            dk_ref[...] = dk_acc[...].astype(dk_ref.dtype)
            dv_ref[...] = dv_acc[...].astype(dv_ref.dtype)

    blk = pl.BlockSpec((tq, dh), lambda h, i: (i, h))
    kblk = pl.BlockSpec((m, dh), lambda h, i: (0, h))
    return _pcall(
        body, name=name, grid=(X_HEADS, nq),
        in_specs=[blk, kblk, pl.BlockSpec((m, dh), lambda h, i: (0, X_HEADS + h)), blk],
        out_specs=[blk, kblk, kblk],
        out_shape=[jax.ShapeDtypeStruct((s, d), BF16), jax.ShapeDtypeStruct((m, d), BF16), jax.ShapeDtypeStruct((m, d), BF16)],
        scratch_shapes=[pltpu.VMEM((m, dh), F32), pltpu.VMEM((m, dh), F32)],
        compiler_params=_params("parallel", "arbitrary"),
    )(qc, kv, kv, do)


def _local_step(x, mem, tgt, w):
    s, d = x.shape
    heads = d // SB_HEAD_DIM
    tm = _pick(s, (256, 128))
    tq = _pick(s, (256, 128))
    tc = _pick(d, (256, 128))
    g = {}

    def ffn_fwd(h, gname, wgu, wdown, tag):
        n = _rms_fwd(h, w[gname], tag + "_norm", tm)
        gu = _mm(n, w[wgu], name=tag + "_gu")
        act = _swiglu_fwd(gu, tag + "_act", tm)
        return n, gu, act, _mm(act, w[wdown], name=tag + "_down", out_dtype=F32, res=h, alpha=0.5)

    def ffn_bwd(dh, h, saved, gname, wgu, wdown, tag):
        n, gu, act = saved
        dhb = _rowcall(lambda v: 0.5 * v, [_whole(dh)], [], [(d, BF16)], tm=tm, name=tag + "_half")[0]
        g[wdown] = _mm(act, dhb, ta=True, name=tag + "_dwdown")
        dact = _mm(dhb, w[wdown], tb=True, name=tag + "_dact")
        dgu = _swiglu_bwd(dact, gu, tag + "_dgu", tm)
        g[wgu] = _mm(n, dgu, ta=True, name=tag + "_dwgu")
        dn = _mm(dgu, w[wgu], tb=True, name=tag + "_dn", out_dtype=F32)
        dh_in, g[gname] = _resid_rms_bwd(dh, dn, h, w[gname], tag + "_dnorm", tm)
        return dh_in

    n1, gu1, act1, h1 = ffn_fwd(x, "g_ffn1", "w_ffn1_gu", "w_ffn1_down", "ffn1")
    u = _rms_fwd(h1, w["g_mix"], "mix_norm", tm)
    proj = _mm(u, w["w_in"], name="mix_in")
    nd = d // SB_HEAD_DIM
    y_conv = _conv_fwd(proj, w["conv_w"], d, tc, "conv_fwd")
    sb_cols = (3 * nd, 4 * nd, 5 * nd)
    y_sb = _sb_fwd(proj, heads, sb_cols, tq, "sb_fwd")
    a_conv = _mm(y_conv, w["w_conv_out"], name="conv_out")
    a_sb = _mm(y_sb, w["w_attn_out"], name="attn_out")
    b_conv, b_sb = w["b_gate"][:, :d], w["b_gate"][:, d:]

    def merge(ac, asb, gcp, gsp, bc, bs):
        gc = _sigmoid(gcp.astype(F32) + bc)
        gs = _sigmoid(gsp.astype(F32) + bs)
        return gc * ac.astype(F32) + gs * asb.astype(F32)

    merged = _rowcall(merge, [_whole(a_conv), _whole(a_sb), (proj, 6, d), (proj, 7, d)], [b_conv, b_sb], [(d, BF16)],
                      tm=tm, name="merge")[0]
    h2 = _mm(merged, w["w_o"], name="mix_out", out_dtype=F32, res=h1)
    hn = _rms_fwd(h2, w["g_cross"], "cross_norm", tm)
    mn = _rms_fwd(mem, w["g_mem"], "mem_norm", _pick(mem.shape[0], (256, 128)))
    qc = _mm(hn, w["w_cq"], name="cross_q")
    kv = _mm(mn, w["w_ckv"], name="cross_kv")
    oc = _xattn_fwd(qc, kv, tq, "xattn_fwd")
    h3 = _mm(oc, w["w_co"], name="cross_out", out_dtype=F32, res=h2)
    n2, gu2, act2, h4 = ffn_fwd(h3, "g_ffn2", "w_ffn2_gu", "w_ffn2_down", "ffn2")

    def head(hb, tb, gb):
        xh, r = _xhat(hb)
        err = xh * gb - tb
        dy = err * (1.0 / d)
        dxh = dy * gb
        dx = r * (dxh - xh * jnp.mean(dxh * xh, axis=-1, keepdims=True))
        row_loss = 0.5 * jnp.mean(err * err, axis=-1, keepdims=True)
        return dx, dy * xh, jnp.broadcast_to(row_loss, (row_loss.shape[0], LANES))

    dh4, g["g_final"], loss_lanes = _rowcall(head, [_whole(h4), _whole(tgt)], [w["g_final"]], [(d, F32)], [d, LANES],
                                             tm=tm, name="loss_head")

    dh3 = ffn_bwd(dh4, h3, (n2, gu2, act2), "g_ffn2", "w_ffn2_gu", "w_ffn2_down", "ffn2")
    dh3b = _rowcall(lambda v: v, [_whole(dh3)], [], [(d, BF16)], tm=tm, name="cross_cast")[0]
    g["w_co"] = _mm(oc, dh3b, ta=True, name="cross_dwco")
    doc = _mm(dh3b, w["w_co"], tb=True, name="cross_doc")
    dqc, dk, dv = _xattn_bwd(qc, kv, doc, tq, "xattn_bwd")
    dkv = jnp.concatenate([dk, dv], axis=1)
    g["w_cq"] = _mm(hn, dqc, ta=True, name="cross_dwcq")
    g["w_ckv"] = _mm(mn, dkv, ta=True, name="cross_dwckv")
    dhn = _mm(dqc, w["w_cq"], tb=True, name="cross_dhn", out_dtype=F32)
    dmn = _mm(dkv, w["w_ckv"], tb=True, name="cross_dmn", out_dtype=F32)
    g["g_mem"] = _rowcall(lambda dy, xb: dy * _xhat(xb)[0], [_whole(dmn), _whole(mem)], [], [], [d],
                          tm=_pick(mem.shape[0], (256, 128)), name="mem_dnorm")[0]
    dh2, g["g_cross"] = _resid_rms_bwd(dh3, dhn, h2, w["g_cross"], "cross_dnorm", tm)

    dh2b = _rowcall(lambda v: v, [_whole(dh2)], [], [(d, BF16)], tm=tm, name="mix_cast")[0]
    g["w_o"] = _mm(merged, dh2b, ta=True, name="mix_dwo")
    dmerged = _mm(dh2b, w["w_o"], tb=True, name="mix_dmerged")

    def merge_bwd(dm, ac, asb, gcp, gsp, bc, bs):
        dm, ac, asb = dm.astype(F32), ac.astype(F32), asb.astype(F32)
        gc = _sigmoid(gcp.astype(F32) + bc)
        gs = _sigmoid(gsp.astype(F32) + bs)
        dgc = dm * ac * gc * (1.0 - gc)
        dgs = dm * asb * gs * (1.0 - gs)
        return dm * gc, dm * gs, dgc, dgs, dgc, dgs

    da_conv, da_sb, dgc, dgs, db_conv, db_sb = _rowcall(
        merge_bwd, [_whole(dmerged), _whole(a_conv), _whole(a_sb), (proj, 6, d), (proj, 7, d)], [b_conv, b_sb],
        [(d, BF16)] * 4, [d, d], tm=tm, name="merge_bwd")
    g["b_gate"] = jnp.concatenate([db_conv, db_sb], axis=1)
    g["w_conv_out"] = _mm(y_conv, da_conv, ta=True, name="conv_dwout")
    g["w_attn_out"] = _mm(y_sb, da_sb, ta=True, name="attn_dwout")
    dy_conv = _mm(da_conv, w["w_conv_out"], tb=True, name="conv_dy")
    dy_sb = _mm(da_sb, w["w_attn_out"], tb=True, name="attn_dy")
    dcb, dcc, dcx, g["conv_w"] = _conv_bwd(dy_conv, proj, w["conv_w"], d, tc, "conv_bwd")
    dq, dk_sb, dv_sb = _sb_bwd(proj, y_sb, dy_sb, heads, sb_cols, tq, "sb_bwd")
    dproj = jnp.concatenate([dcb, dcc, dcx, dq, dk_sb, dv_sb, dgc, dgs], axis=1)
    g["w_in"] = _mm(u, dproj, ta=True, name="mix_dwin")
    du = _mm(dproj, w["w_in"], tb=True, name="mix_du", out_dtype=F32)
    dh1, g["g_mix"] = _resid_rms_bwd(dh2, du, h1, w["g_mix"], "mix_dnorm", tm)
    dx = ffn_bwd(dh1, x, (n1, gu1, act1), "g_ffn1", "w_ffn1_gu", "w_ffn1_down", "ffn1")
    return loss_lanes, dx, g


MATS = (("w_ffn1_gu", "col"), ("w_ffn1_down", "row"), ("w_in", "col"), ("w_conv_out", "row"), ("w_attn_out", "row"),
        ("w_o", "row"), ("w_cq", "row"), ("w_ckv", "col"), ("w_co", "row"), ("w_ffn2_gu", "col"), ("w_ffn2_down", "row"))
VECS = ("g_ffn1", "g_mix", "g_cross", "g_mem", "g_ffn2", "g_final")
WEIGHTS = ("g_ffn1", "w_ffn1_gu", "w_ffn1_down", "g_mix", "w_in", "b_gate", "conv_w", "w_conv_out", "w_attn_out", "w_o",
           "g_cross", "g_mem", "w_cq", "w_ckv", "w_co", "g_ffn2", "w_ffn2_gu", "w_ffn2_down", "g_final")
CONV_ROWS = 8
ANY = pl.BlockSpec(memory_space=pl.ANY)


def _full_shape(kind, r, c):
    return (r, N_CHIPS * c) if kind == "col" else (N_CHIPS * r, c)


def _piece(ref, kind, r, c, chip, half):
    hr = r // 2
    if kind == "col":
        return ref.at[pl.ds(pl.multiple_of(half * hr, 16), hr), pl.ds(pl.multiple_of(chip * c, LANES), c)]
    return ref.at[pl.ds(pl.multiple_of(chip * r + half * hr, 16), hr), :]


def _shard_of(ref, kind, r, c, chip):
    if kind == "col":
        return ref.at[:, pl.ds(pl.multiple_of(chip * c, LANES), c)]
    return ref.at[pl.ds(pl.multiple_of(chip * r, 16), r), :]


def _place():
    x, y, c = lax.axis_index("x"), lax.axis_index("y"), lax.axis_index("c")
    others = [(1 - x, y), (x, 1 - y), (1 - x, 1 - y)]
    return x, y, c, 2 * x + y, others


def _remote(src, dst, send_sem, recv_sem, to):
    return pltpu.make_async_remote_copy(src_ref=src, dst_ref=dst, send_sem=send_sem, recv_sem=recv_sem,
                                        device_id=to, device_id_type=MESH)


def _gather_weights(shards, conv_shard):
    dims = [(kind, *sh.shape) for (_, kind), sh in zip(MATS, shards)]
    nw = len(shards)
    cc = conv_shard.shape[1]

    def body(*refs):
        shard_refs, conv_ref = refs[:nw], refs[nw]
        full_refs, conv_full = refs[nw + 1:2 * nw + 1], refs[2 * nw + 1]
        s1, r1, s2, r2, loc, r3, cs, cr, cl = refs[2 * nw + 2:]
        x, y, c, me, others = _place()
        sib = (x, y, 1 - c)

        def first(wi, k, chip_from, to):
            kind, r, cw = dims[wi]
            src = shard_refs[wi].at[pl.ds(pl.multiple_of(c * (r // 2), 16), r // 2), :]
            return _remote(src, _piece(full_refs[wi], kind, r, cw, chip_from, c), s1.at[wi, k], r1.at[wi, k], to)

        def second(wi, k, chip_from, half):
            kind, r, cw = dims[wi]
            pc = _piece(full_refs[wi], kind, r, cw, chip_from, half)
            return _remote(pc, pc, s2.at[wi, k], r2.at[wi, k], sib)

        def conv(k, chip_from, to):
            dst = conv_full.at[:, pl.ds(pl.multiple_of(chip_from * cc, LANES), cc)]
            return _remote(conv_ref, dst, cs.at[k], cr.at[k], to)

        def own(wi):
            return _remote(shard_refs[wi], _shard_of(full_refs[wi], *dims[wi], me), loc.at[wi], r3.at[wi], sib)

        conv_local = pltpu.make_async_copy(conv_ref, conv_full.at[:, pl.ds(pl.multiple_of(me * cc, LANES), cc)], cl.at[0])
        conv_local.start()
        for k, (ox, oy) in enumerate(others):
            conv(k, me, (ox, oy, c)).start()
        for wi in range(nw):
            for k, (ox, oy) in enumerate(others):
                first(wi, k, me, (ox, oy, c)).start()
        for wi in range(nw):
            own(wi).start()
        for wi in range(nw):
            for k, (ox, oy) in enumerate(others):
                first(wi, k, 2 * ox + oy, (x, y, c)).wait_recv()
                second(wi, k, 2 * ox + oy, c).start()
        for wi in range(nw):
            own(wi).wait_recv()
            for k, (ox, oy) in enumerate(others):
                second(wi, k, 2 * ox + oy, 1 - c).wait_recv()
        for k, (ox, oy) in enumerate(others):
            conv(k, 2 * ox + oy, (x, y, c)).wait_recv()
            conv(k, me, (ox, oy, c)).wait_send()
        for wi in range(nw):
            own(wi).wait_send()
            for k, (ox, oy) in enumerate(others):
                first(wi, k, me, (ox, oy, c)).wait_send()
                second(wi, k, 2 * ox + oy, c).wait_send()
        conv_local.wait()

    out_shape = [jax.ShapeDtypeStruct(_full_shape(*dm), BF16) for dm in dims]
    out_shape.append(jax.ShapeDtypeStruct((CONV_ROWS, N_CHIPS * cc), F32))
    dma = pltpu.SemaphoreType.DMA
    outs = _pcall(
        body, name="gather_weights", in_specs=[ANY] * (nw + 1), out_specs=[ANY] * (nw + 1), out_shape=out_shape,
        scratch_shapes=[dma((nw, 3)), dma((nw, 3)), dma((nw, 3)), dma((nw, 3)), dma((nw,)), dma((nw,)), dma((3,)), dma((3,)), dma((1,))],
    )(*shards, conv_shard)
    return outs[:nw], outs[nw]


def _allreduce_small(packed):
    rows, n = packed.shape

    def body(in_ref, out_ref, gath_ref, send_sems, recv_sems):
        x, y, c = lax.axis_index("x"), lax.axis_index("y"), lax.axis_index("c")
        me = 4 * x + 2 * y + c

        def peer(rel):
            return (x ^ (rel >> 2 & 1), y ^ (rel >> 1 & 1), c ^ (rel & 1))

        def copy(rel, slot, to):
            return _remote(in_ref, gath_ref.at[slot], send_sems.at[rel - 1], recv_sems.at[rel - 1], to)

        for rel in range(1, N_DEV):
            copy(rel, me, peer(rel)).start()
        gath_ref[me] = in_ref[...]
        for rel in range(1, N_DEV):
            px, py, pc = peer(rel)
            copy(rel, 4 * px + 2 * py + pc, (x, y, c)).wait_recv()
        for rel in range(1, N_DEV):
            copy(rel, me, peer(rel)).wait_send()
        tot = gath_ref[0]
        for dev in range(1, N_DEV):
            tot = tot + gath_ref[dev]
        out_ref[...] = tot

    vm = pl.BlockSpec(memory_space=pltpu.VMEM)
    return _pcall(
        body, name="allreduce_small", in_specs=[vm], out_specs=[vm, vm],
        out_shape=[jax.ShapeDtypeStruct((rows, n), F32), jax.ShapeDtypeStruct((N_DEV, rows, n), F32)],
        scratch_shapes=[pltpu.SemaphoreType.DMA((N_DEV - 1,)), pltpu.SemaphoreType.DMA((N_DEV - 1,))],
    )(packed)[0]


def _rs_cores(grads, dims):
    nw = len(grads)

    def body(*refs):
        g_refs, got_refs = refs[:nw], refs[nw:2 * nw]
        ss, rs = refs[2 * nw:]
        x, y, c, _, _ = _place()
        sib = (x, y, 1 - c)

        def give(wi, chip):
            return _remote(_piece(g_refs[wi], *dims[wi], chip, 1 - c), got_refs[wi].at[chip], ss.at[wi, chip], rs.at[wi, chip], sib)

        every = [(wi, chip) for wi in range(nw) for chip in range(N_CHIPS)]
        for wi, chip in every:
            give(wi, chip).start()
        for wi, chip in every:
            give(wi, chip).wait()

    dma = pltpu.SemaphoreType.DMA
    return _pcall(
        body, name="rs_cores", in_specs=[ANY] * nw, out_specs=[ANY] * nw,
        out_shape=[jax.ShapeDtypeStruct((N_CHIPS, r // 2, cw), BF16) for (_, r, cw) in dims],
        scratch_shapes=[dma((nw, N_CHIPS)), dma((nw, N_CHIPS))],
    )(*grads)


def _rs_chips(parts):
    nw = len(parts)

    def body(*refs):
        p_refs, got_refs = refs[:nw], refs[nw:2 * nw]
        ss, rs = refs[2 * nw:]
        x, y, c, me, others = _place()

        def send(wi, k, chip_to, to):
            return _remote(p_refs[wi].at[chip_to], got_refs[wi].at[k], ss.at[wi, k], rs.at[wi, k], to)

        for wi in range(nw):
            for k, (ox, oy) in enumerate(others):
                send(wi, k, 2 * ox + oy, (ox, oy, c)).start()
        for wi in range(nw):
            for k, (ox, oy) in enumerate(others):
                send(wi, k, me, (x, y, c)).wait_recv()
        for wi in range(nw):
            for k, (ox, oy) in enumerate(others):
                send(wi, k, 2 * ox + oy, (ox, oy, c)).wait_send()

    dma = pltpu.SemaphoreType.DMA
    return _pcall(
        body, name="rs_chips", in_specs=[ANY] * nw, out_specs=[ANY] * nw,
        out_shape=[jax.ShapeDtypeStruct((3, *p.shape[1:]), p.dtype) for p in parts],
        scratch_shapes=[dma((nw, 3)), dma((nw, 3))],
    )(*parts)


def _share_halves(bufs):
    nw = len(bufs)

    def body(*refs):
        out_refs = refs[nw:2 * nw]
        ss, rs = refs[2 * nw:]
        x, y, c, _, _ = _place()
        sib = (x, y, 1 - c)

        def send(wi, half):
            return _remote(out_refs[wi].at[half], out_refs[wi].at[half], ss.at[wi], rs.at[wi], sib)

        for wi in range(nw):
            send(wi, c).start()
        for wi in range(nw):
            send(wi, 1 - c).wait_recv()
        for wi in range(nw):
            send(wi, c).wait_send()

    dma = pltpu.SemaphoreType.DMA
    return _pcall(
        body, name="share_halves", in_specs=[ANY] * nw, out_specs=[ANY] * nw,
        out_shape=[jax.ShapeDtypeStruct(b.shape, b.dtype) for b in bufs],
        input_output_aliases={i: i for i in range(nw)}, scratch_shapes=[dma((nw,)), dma((nw,))],
    )(*bufs)


def _rows_per_block(n, c, limit_bytes=1 << 20):
    best = None
    for tm in range(16, n + 1, 16):
        if n % tm == 0 and tm * c * 4 <= limit_bytes:
            best = tm
    return best or n


def _sum_cores(grad, got, kind, place, name):
    _, hr, cw = got.shape
    tm = _rows_per_block(hr, cw)
    nb = hr // tm

    def body(place_ref, g_ref, t_ref, o_ref):
        o_ref[...] = (g_ref[...].astype(F32) + t_ref[...].astype(F32)).astype(o_ref.dtype)

    if kind == "col":
        g_spec = pl.BlockSpec((tm, cw), lambda j, i, pr: (pr[0] * nb + i, j))
    else:
        g_spec = pl.BlockSpec((tm, cw), lambda j, i, pr: ((2 * j + pr[0]) * nb + i, 0))
    blk = pl.BlockSpec((None, tm, cw), lambda j, i, pr: (j, i, 0))
    return _pcall(
        body, name=name, out_shape=jax.ShapeDtypeStruct(got.shape, BF16),
        grid_spec=pltpu.PrefetchScalarGridSpec(num_scalar_prefetch=1, grid=(N_CHIPS, nb), in_specs=[g_spec, blk], out_specs=blk),
        compiler_params=_params("parallel", "parallel"),
    )(place, grad, got)


def _sum_chips(parts, got, place, name):
    _, n, cw = got.shape
    tm = _rows_per_block(n, cw)

    def body(place_ref, p_ref, g_ref, o_ref):
        tot = p_ref[...].astype(F32)
        for k in range(3):
            tot = tot + g_ref[k].astype(F32)
        o_ref[...] = tot

    return _pcall(
        body, name=name, out_shape=jax.ShapeDtypeStruct((2, n, cw), F32),
        grid_spec=pltpu.PrefetchScalarGridSpec(
            num_scalar_prefetch=1, grid=(n // tm,),
            in_specs=[pl.BlockSpec((None, tm, cw), lambda i, pr: (pr[1], i, 0)), pl.BlockSpec((3, tm, cw), lambda i, pr: (0, i, 0))],
            out_specs=pl.BlockSpec((None, tm, cw), lambda i, pr: (pr[0], i, 0))),
        compiler_params=_params("parallel"),
    )(place, parts, got)


def _adamw(g, w, m, v, name):
    n, c = g.shape
    c1 = 1.0 - ADAM_B1 ** ADAM_STEP
    c2 = 1.0 - ADAM_B2 ** ADAM_STEP

    def fn(gb, wb, mb, vb):
        m_new = ADAM_B1 * mb + (1.0 - ADAM_B1) * gb
        v_new = ADAM_B2 * vb + (1.0 - ADAM_B2) * (gb * gb)
        delta = -ADAM_LR * ((m_new / c1) / (jnp.sqrt(v_new / c2) + ADAM_EPS) + ADAM_WD * wb)
        return gb, delta, m_new, v_new

    tm = _rows_per_block(n, c) if n % 16 == 0 else n
    return _rowcall(fn, [_whole(g), _whole(w), _whole(m), _whole(v)], [], [(c, F32)] * 4, tm=tm, name=name)


PACK_ROWS = 16


def _pack_rows(parts, width, name):
    assert sum(p.shape[0] for p in parts) <= PACK_ROWS

    def body(*refs):
        out_ref = refs[-1]
        out_ref[...] = jnp.zeros_like(out_ref)
        at = 0
        for r in refs[:-1]:
            k, n = r.shape
            if n == width:
                out_ref[at:at + k, :] = r[...]
            else:
                out_ref[at:at + k, :] = jnp.broadcast_to(r[:, :1], (k, width))
            at += k

    vm = pl.BlockSpec(memory_space=pltpu.VMEM)
    return _pcall(body, name=name, in_specs=[vm] * len(parts), out_specs=vm,
                  out_shape=jax.ShapeDtypeStruct((PACK_ROWS, width), F32))(*parts)


def _cast_shard(wm, name):
    n, c = wm.shape
    return _rowcall(lambda v: v, [_whole(wm)], [], [(c, BF16)], tm=_rows_per_block(n, c), name=name)[0]


def _step(x, mem, tgt, wts, m_in, v_in):
    d = x.shape[-1]
    cc = wts["conv_w"].shape[1]
    shards = [_cast_shard(wts[n], "cast_" + n) for n, _ in MATS]
    dims = [(kind, *sh.shape) for (_, kind), sh in zip(MATS, shards)]
    conv_pad = jnp.pad(wts["conv_w"], ((0, CONV_ROWS - CONV_K), (0, 0)))
    fulls, conv_full = _gather_weights(shards, conv_pad)
    w = {n: f for (n, _), f in zip(MATS, fulls)}
    w["conv_w"] = conv_full[:CONV_K]
    for n in VECS + ("b_gate",):
        w[n] = wts[n].reshape(1, -1)

    loss_lanes, dx, g = _local_step(x[0], mem[0], tgt[0], w)

    place = jnp.stack([lax.axis_index("c"), 2 * lax.axis_index("x") + lax.axis_index("y")]).astype(jnp.int32)
    got = _rs_cores([g[n] for n, _ in MATS], dims)
    parts = [_sum_cores(g[n], t, kind, place, "sum_cores_" + n) for (n, kind), t in zip(MATS, got)]
    landed = _rs_chips(parts)
    halves = [_sum_chips(p, t, place, "sum_chips_" + n) for (n, _), p, t in zip(MATS, parts, landed)]
    both = _share_halves(halves)
    grads = {n: b.reshape(-1, b.shape[-1]) for (n, _), b in zip(MATS, both)}

    rows = [g[n] for n in VECS] + [g["b_gate"][:, :d], g["b_gate"][:, d:], g["conv_w"], loss_lanes]
    red = _allreduce_small(_pack_rows(rows, d, "pack_small"))
    for i, n in enumerate(VECS):
        grads[n] = red[i:i + 1]
    nv = len(VECS)
    grads["b_gate"] = jnp.concatenate([red[nv:nv + 1], red[nv + 1:nv + 2]], axis=1)
    me = 2 * lax.axis_index("x") + lax.axis_index("y")
    grads["conv_w"] = lax.dynamic_slice_in_dim(red[nv + 2:nv + 2 + CONV_K], me * cc, cc, axis=1)
    loss = red[nv + 2 + CONV_K, 0]

    out = {}
    for n in WEIGHTS:
        shape = wts[n].shape
        as2d = (lambda a: a.reshape(1, -1)) if len(shape) == 1 else (lambda a: a)
        res = _adamw(grads[n], as2d(wts[n]), as2d(m_in[n]), as2d(v_in[n]), "adamw_" + n)
        out[n] = [r.reshape(shape) for r in res]
    return (loss, dx[None], *[out[n][0] for n in WEIGHTS], *[out[n][1] for n in WEIGHTS],
            *[out[n][2] for n in WEIGHTS], *[out[n][3] for n in WEIGHTS])


def kernel(x, mem, g_ffn1, w_ffn1_gu, w_ffn1_down, g_mix, w_in, b_gate, conv_w, w_conv_out, w_attn_out, w_o, g_cross, g_mem, w_cq, w_ckv, w_co, g_ffn2, w_ffn2_gu, w_ffn2_down, g_final, loss_target, m_g_ffn1, m_w_ffn1_gu, m_w_ffn1_down, m_g_mix, m_w_in, m_b_gate, m_conv_w, m_w_conv_out, m_w_attn_out, m_w_o, m_g_cross, m_g_mem, m_w_cq, m_w_ckv, m_w_co, m_g_ffn2, m_w_ffn2_gu, m_w_ffn2_down, m_g_final, v_g_ffn1, v_w_ffn1_gu, v_w_ffn1_down, v_g_mix, v_w_in, v_b_gate, v_conv_w, v_w_conv_out, v_w_attn_out, v_w_o, v_g_cross, v_g_mem, v_w_cq, v_w_ckv, v_w_co, v_g_ffn2, v_w_ffn2_gu, v_w_ffn2_down, v_g_final):
    given = dict(locals())
    wts = {n: given[n] for n in WEIGHTS}
    m_in = {n: given["m_" + n] for n in WEIGHTS}
    v_in = {n: given["v_" + n] for n in WEIGHTS}
    return _step(x, mem, loss_target, wts, m_in, v_in)
```

```python
import functools

import jax
import jax.numpy as jnp
from jax import lax
from jax.experimental import pallas as pl
from jax.experimental.pallas import tpu as pltpu

F32 = jnp.float32
BF16 = jnp.bfloat16
MESH = pl.DeviceIdType.MESH

V7X_VMEM_LIMIT_BYTES = 48 * 1024 * 1024
LANES = 128
SB_HEAD_DIM = 128
X_HEADS = 4
CONV_K = 3
RMS_EPS = 1e-6
N_CHIPS = 4
N_DEV = 8
ADAM_LR, ADAM_B1, ADAM_B2, ADAM_EPS, ADAM_WD, ADAM_STEP = 0.001, 0.9, 0.999, 1e-08, 0.01, 10


def _pcall(body, **kw):
    return pl.pallas_call(body, **kw)


def _params(*sem):
    return pltpu.CompilerParams(dimension_semantics=sem, vmem_limit_bytes=V7X_VMEM_LIMIT_BYTES)


def _pick(dim, cands):
    for c in cands:
        if dim % c == 0:
            return c
    return dim


def _dot(a, b, ca, cb):
    return lax.dot_general(a, b, (((ca,), (cb,)), ((), ())), preferred_element_type=F32)


def _mm(a, b, *, name, ta=False, tb=False, out_dtype=BF16, res=None, alpha=1.0, tm=None, tn=None, tk=None):
    m, k = (a.shape[1], a.shape[0]) if ta else a.shape
    n = b.shape[0] if tb else b.shape[1]
    assert k == (b.shape[1] if tb else b.shape[0]), (a.shape, b.shape, ta, tb)
    tm = tm or _pick(m, (512, 256, 128))
    tn = tn or _pick(n, (1024, 512, 256, 128))
    tk = tk or (k if k <= 2816 else _pick(k, (1024, 512, 256, 128)))
    nk = k // tk
    assert m % tm == 0 and n % tn == 0 and k % tk == 0
    a_spec = pl.BlockSpec((tk, tm), lambda i, j, kk: (kk, i)) if ta else pl.BlockSpec((tm, tk), lambda i, j, kk: (i, kk))
    b_spec = pl.BlockSpec((tn, tk), lambda i, j, kk: (j, kk)) if tb else pl.BlockSpec((tk, tn), lambda i, j, kk: (kk, j))
    o_spec = pl.BlockSpec((tm, tn), lambda i, j, kk: (i, j))
    ca, cb = (0 if ta else 1), (1 if tb else 0)

    def body(*refs):
        if res is None:
            a_ref, b_ref, o_ref = refs[:3]
            res_ref = None
            scratch = refs[3:]
        else:
            a_ref, b_ref, res_ref, o_ref = refs[:4]
            scratch = refs[4:]

        def finish(acc):
            val = acc if alpha == 1.0 else alpha * acc
            if res_ref is not None:
                val = res_ref[...].astype(F32) + val
            o_ref[...] = val.astype(o_ref.dtype)

        part = _dot(a_ref[...].astype(BF16), b_ref[...].astype(BF16), ca, cb)
        if nk == 1:
            finish(part)
        else:
            acc_ref = scratch[0]
            kk = pl.program_id(2)

            @pl.when(kk == 0)
            def _():
                acc_ref[...] = part

            @pl.when(kk > 0)
            def _():
                acc_ref[...] += part

            @pl.when(kk == nk - 1)
            def _():
                finish(acc_ref[...])

    ins = [a, b] + ([] if res is None else [res])
    in_specs = [a_spec, b_spec] + ([] if res is None else [o_spec])
    return _pcall(
        body, name=name, grid=(m // tm, n // tn, nk), in_specs=in_specs, out_specs=o_spec,
        out_shape=jax.ShapeDtypeStruct((m, n), out_dtype),
        scratch_shapes=[pltpu.VMEM((tm, tn), F32)] if nk > 1 else [],
        compiler_params=_params("parallel", "parallel", "arbitrary"),
    )(*ins)


def _rowcall(fn, rows, consts, outs, accs=(), *, tm, name):
    s = rows[0][0].shape[0]
    assert s % tm == 0
    n_in, n_out = len(rows) + len(consts), len(outs)

    def body(*refs):
        vals = fn(*[r[...] for r in refs[:n_in]])
        vals = vals if isinstance(vals, (tuple, list)) else (vals,)
        for o_ref, v in zip(refs[n_in:n_in + n_out], vals[:n_out]):
            o_ref[...] = v.astype(o_ref.dtype)
        if accs:
            first = pl.program_id(0) == 0
            for a_ref, v in zip(refs[n_in + n_out:], vals[n_out:]):
                tot = jnp.sum(v.astype(F32), axis=0, keepdims=True)

                @pl.when(first)
                def _(a_ref=a_ref, tot=tot):
                    a_ref[...] = tot

                @pl.when(jnp.logical_not(first))
                def _(a_ref=a_ref, tot=tot):
                    a_ref[...] += tot

    in_specs = [pl.BlockSpec((tm, w), lambda i, cb=cb: (i, cb)) for (_, cb, w) in rows]
    in_specs += [pl.BlockSpec(c.shape, lambda i: (0, 0)) for c in consts]
    out_specs = [pl.BlockSpec((tm, w), lambda i: (i, 0)) for (w, _) in outs]
    out_specs += [pl.BlockSpec((1, w), lambda i: (0, 0)) for w in accs]
    out_shape = [jax.ShapeDtypeStruct((s, w), dt) for (w, dt) in outs]
    out_shape += [jax.ShapeDtypeStruct((1, w), F32) for w in accs]
    return _pcall(
        body, name=name, grid=(s // tm,), in_specs=in_specs, out_specs=out_specs, out_shape=out_shape,
        compiler_params=_params("arbitrary" if accs else "parallel"),
    )(*[r[0] for r in rows], *consts)


def _whole(a):
    return (a, 0, a.shape[1])


def _xhat(x):
    x = x.astype(F32)
    r = lax.rsqrt(jnp.mean(x * x, axis=-1, keepdims=True) + RMS_EPS)
    return x * r, r


def _rms_bwd(dy, x, g):
    xh, r = _xhat(x)
    dxh = dy.astype(F32) * g
    dx = r * (dxh - xh * jnp.mean(dxh * xh, axis=-1, keepdims=True))
    return dx, dy.astype(F32) * xh


def _sigmoid(x):
    return 1.0 / (1.0 + jnp.exp(-x))


def _rms_fwd(x, g, name, tm):
    d = x.shape[1]
    return _rowcall(lambda xb, gb: _xhat(xb)[0] * gb, [_whole(x)], [g], [(d, BF16)], tm=tm, name=name)[0]


def _swiglu_fwd(gu, name, tm):
    f = gu.shape[1] // 2

    def fn(gate, up):
        gate, up = gate.astype(F32), up.astype(F32)
        return gate * _sigmoid(gate) * up

    return _rowcall(fn, [(gu, 0, f), (gu, 1, f)], [], [(f, BF16)], tm=tm, name=name)[0]


def _swiglu_bwd(dact, gu, name, tm):
    f = gu.shape[1] // 2

    def fn(da, gate, up):
        da, gate, up = da.astype(F32), gate.astype(F32), up.astype(F32)
        sg = _sigmoid(gate)
        silu = gate * sg
        dgate = da * up * (sg + silu * (1.0 - sg))
        return jnp.concatenate([dgate, da * silu], axis=1)

    return _rowcall(fn, [_whole(dact), (gu, 0, f), (gu, 1, f)], [], [(2 * f, BF16)], tm=tm, name=name)[0]


def _resid_rms_bwd(dh, dn, x, g, name, tm):
    d = x.shape[1]

    def fn(dhb, dnb, xb, gb):
        dx, dg = _rms_bwd(dnb, xb, gb)
        return dhb.astype(F32) + dx, dg

    return _rowcall(fn, [_whole(dh), _whole(dn), _whole(x)], [g], [(d, F32)], [d], tm=tm, name=name)


def _shift_down(p, k):
    if k == 0:
        return p
    rows = lax.broadcasted_iota(jnp.int32, p.shape, 0)
    return jnp.where(rows >= k, pltpu.roll(p, k, 0), 0.0)


def _shift_up(p, k):
    if k == 0:
        return p
    s = p.shape[0]
    rows = lax.broadcasted_iota(jnp.int32, p.shape, 0)
    return jnp.where(rows < s - k, pltpu.roll(p, s - k, 0), 0.0)


def _conv_fwd(proj, conv_w, d, tc, name):
    s = proj.shape[0]
    nb = d // tc

    def body(cb_ref, cc_ref, cx_ref, w_ref, y_ref):
        p = cc_ref[...].astype(F32) * cx_ref[...].astype(F32)
        w = w_ref[...]
        acc = p * w[CONV_K - 1:CONV_K, :]
        for k in range(1, CONV_K):
            acc = acc + _shift_down(p, k) * w[CONV_K - 1 - k:CONV_K - k, :]
        y_ref[...] = (cb_ref[...].astype(F32) * acc).astype(y_ref.dtype)

    col = lambda off: pl.BlockSpec((s, tc), lambda j: (0, off * nb + j))
    return _pcall(
        body, name=name, grid=(nb,), in_specs=[col(0), col(1), col(2), pl.BlockSpec((CONV_K, tc), lambda j: (0, j))],
        out_specs=pl.BlockSpec((s, tc), lambda j: (0, j)), out_shape=jax.ShapeDtypeStruct((s, d), BF16),
        compiler_params=_params("parallel"),
    )(proj, proj, proj, conv_w)


def _conv_bwd(dy, proj, conv_w, d, tc, name):
    s = proj.shape[0]
    nb = d // tc

    def body(dy_ref, cb_ref, cc_ref, cx_ref, w_ref, dcb_ref, dcc_ref, dcx_ref, dw_ref):
        cc, cx = cc_ref[...].astype(F32), cx_ref[...].astype(F32)
        p = cc * cx
        w = w_ref[...]
        dyv = dy_ref[...].astype(F32)
        shifted = [_shift_down(p, CONV_K - 1 - k) for k in range(CONV_K)]
        conv = shifted[0] * w[0:1, :]
        for k in range(1, CONV_K):
            conv = conv + shifted[k] * w[k:k + 1, :]
        dcb_ref[...] = (dyv * conv).astype(dcb_ref.dtype)
        ds = dyv * cb_ref[...].astype(F32)
        dp = ds * w[CONV_K - 1:CONV_K, :]
        for k in range(1, CONV_K):
            dp = dp + _shift_up(ds, k) * w[CONV_K - 1 - k:CONV_K - k, :]
        dcc_ref[...] = (dp * cx).astype(dcc_ref.dtype)
        dcx_ref[...] = (dp * cc).astype(dcx_ref.dtype)
        for k in range(CONV_K):
            dw_ref[k:k + 1, :] = jnp.sum(ds * shifted[k], axis=0, keepdims=True)

    col = lambda off: pl.BlockSpec((s, tc), lambda j: (0, off * nb + j))
    blk = pl.BlockSpec((s, tc), lambda j: (0, j))
    wblk = pl.BlockSpec((CONV_K, tc), lambda j: (0, j))
    act = jax.ShapeDtypeStruct((s, d), BF16)
    return _pcall(
        body, name=name, grid=(nb,), in_specs=[blk, col(0), col(1), col(2), wblk],
        out_specs=[blk, blk, blk, wblk], out_shape=[act, act, act, jax.ShapeDtypeStruct((CONV_K, d), F32)],
        compiler_params=_params("parallel"),
    )(dy, proj, proj, proj, conv_w)


def _sb_tile(q, kj, scale, carry, tri, mask):
    z = _dot(q, kj, 1, 1) * scale
    lsz = jnp.minimum(z, 0.0) - jnp.log(1.0 + jnp.exp(-jnp.abs(z)))
    l1m = lsz - z
    if mask is not None:
        l1m = jnp.where(mask, l1m, 0.0)
    l1b = l1m.astype(BF16)
    a = jnp.exp(lsz + (carry + _dot(l1b, tri, 1, 0)))
    if mask is not None:
        a = jnp.where(mask, a, 0.0)
    return lsz, l1b, a.astype(BF16)


def _sb_masks(tq, tk):
    row = lax.broadcasted_iota(jnp.int32, (tq, tk), 0)
    col = lax.broadcasted_iota(jnp.int32, (tq, tk), 1)
    masks = [col + dj * tk < row for dj in range(tq // tk)]
    r2 = lax.broadcasted_iota(jnp.int32, (tk, tk), 0)
    c2 = lax.broadcasted_iota(jnp.int32, (tk, tk), 1)
    return masks, (r2 > c2).astype(BF16), (r2 < c2).astype(BF16)


def _sb_fwd(proj, heads, col0, tq, tk, name):
    s = proj.shape[0]
    dh = SB_HEAD_DIM
    nq, nd = s // tq, tq // tk
    scale = dh ** -0.5

    def body(q_ref, k_ref, v_ref, o_ref):
        i = pl.program_id(1)
        q = q_ref[...]
        masks, tri_right, _ = _sb_masks(tq, tk)

        def tile(j, carry, acc, mask):
            start = pl.multiple_of(j * tk, tk)
            kj = k_ref[pl.ds(start, tk), :]
            vj = v_ref[pl.ds(start, tk), :]
            _, l1b, ab = _sb_tile(q, kj, scale, carry, tri_right, mask)
            return carry + jnp.sum(l1b.astype(F32), axis=1, keepdims=True), acc + _dot(ab, vj, 1, 0)

        state = (jnp.zeros((tq, 1), F32), jnp.zeros((tq, dh), F32))
        for dj in reversed(range(nd)):
            state = tile(i * nd + dj, *state, masks[dj])
        state = lax.fori_loop(0, i * nd, lambda t, st: tile(i * nd - 1 - t, st[0], st[1], None), state)
        o_ref[...] = state[1]

    qspec = pl.BlockSpec((tq, dh), lambda h, i: (i, col0[0] + h))
    kspec = pl.BlockSpec((s, dh), lambda h, i: (0, col0[1] + h))
    vspec = pl.BlockSpec((s, dh), lambda h, i: (0, col0[2] + h))
    return _pcall(
        body, name=name, grid=(heads, nq), in_specs=[qspec, kspec, vspec],
        out_specs=pl.BlockSpec((tq, dh), lambda h, i: (i, h)), out_shape=jax.ShapeDtypeStruct((s, heads * dh), F32),
        compiler_params=_params("parallel", "parallel"),
    )(proj, proj, proj)


def _sb_bwd(proj, o, do, heads, col0, tq, tk, name):
    s = proj.shape[0]
    dh = SB_HEAD_DIM
    nq, nd = s // tq, tq // tk
    scale = dh ** -0.5

    def body(q_ref, k_ref, v_ref, o_ref, do_ref, dq_ref, dk_ref, dv_ref, dk_acc, dv_acc):
        i = pl.program_id(1)

        @pl.when(i == 0)
        def _():
            dk_acc[...] = jnp.zeros_like(dk_acc)
            dv_acc[...] = jnp.zeros_like(dv_acc)

        q = q_ref[...]
        dob = do_ref[...].astype(BF16)
        delta = jnp.sum(dob.astype(F32) * o_ref[...], axis=1, keepdims=True)
        masks, tri_right, tri_left = _sb_masks(tq, tk)

        def tile(j, carry_l, carry_g, dq, mask):
            start = pl.multiple_of(j * tk, tk)
            kj = k_ref[pl.ds(start, tk), :]
            vj = v_ref[pl.ds(start, tk), :]
            lsz, l1b, ab = _sb_tile(q, kj, scale, carry_l, tri_right, mask)
            g = _dot(dob, vj, 1, 1) * ab.astype(F32)
            carry_g = carry_g + jnp.sum(g, axis=1, keepdims=True)
            left = (delta - carry_g) + _dot(g.astype(BF16), tri_left, 1, 0)
            beta = jnp.exp(lsz)
            dz = g * (1.0 - beta) - left * beta
            if mask is not None:
                dz = jnp.where(mask, dz, 0.0)
            dzb = (dz * scale).astype(BF16)
            dk_acc[pl.ds(start, tk), :] += _dot(dzb, q, 0, 0)
            dv_acc[pl.ds(start, tk), :] += _dot(ab, dob, 0, 0)
            return carry_l + jnp.sum(l1b.astype(F32), axis=1, keepdims=True), carry_g, dq + _dot(dzb, kj, 1, 0)

        zero = jnp.zeros((tq, 1), F32)
        state = (zero, zero, jnp.zeros((tq, dh), F32))
        for dj in reversed(range(nd)):
            state = tile(i * nd + dj, *state, masks[dj])
        state = lax.fori_loop(0, i * nd, lambda t, st: tile(i * nd - 1 - t, st[0], st[1], st[2], None), state)
        dq_ref[...] = state[2].astype(dq_ref.dtype)

        @pl.when(i == nq - 1)
        def _():
            dk_ref[...] = dk_acc[...].astype(dk_ref.dtype)
            dv_ref[...] = dv_acc[...].astype(dv_ref.dtype)

    qspec = pl.BlockSpec((tq, dh), lambda h, i: (i, col0[0] + h))
    kspec = pl.BlockSpec((s, dh), lambda h, i: (0, col0[1] + h))
    vspec = pl.BlockSpec((s, dh), lambda h, i: (0, col0[2] + h))
    blk = pl.BlockSpec((tq, dh), lambda h, i: (i, h))
    full = pl.BlockSpec((s, dh), lambda h, i: (0, h))
    act = jax.ShapeDtypeStruct((s, heads * dh), BF16)
    return _pcall(
        body, name=name, grid=(heads, nq), in_specs=[qspec, kspec, vspec, blk, blk],
        out_specs=[blk, full, full], out_shape=[act, act, act],
        scratch_shapes=[pltpu.VMEM((s, dh), F32), pltpu.VMEM((s, dh), F32)],
        compiler_params=_params("parallel", "arbitrary"),
    )(proj, proj, proj, o, do)


def _xattn_probs(q, k, scale):
    sc = _dot(q, k, 1, 1) * scale
    e = jnp.exp(sc - jnp.max(sc, axis=1, keepdims=True))
    return e / jnp.sum(e, axis=1, keepdims=True)


def _xattn_fwd(qc, kv, tq, name):
    s, d = qc.shape
    m = kv.shape[0]
    dh = d // X_HEADS
    scale = dh ** -0.5

    def body(q_ref, k_ref, v_ref, o_ref):
        p = _xattn_probs(q_ref[...], k_ref[...], scale)
        o_ref[...] = _dot(p.astype(BF16), v_ref[...], 1, 0).astype(o_ref.dtype)

    blk = pl.BlockSpec((tq, dh), lambda h, i: (i, h))
    return _pcall(
        body, name=name, grid=(X_HEADS, s // tq),
        in_specs=[blk, pl.BlockSpec((m, dh), lambda h, i: (0, h)), pl.BlockSpec((m, dh), lambda h, i: (0, X_HEADS + h))],
        out_specs=blk, out_shape=jax.ShapeDtypeStruct((s, d), BF16), compiler_params=_params("parallel", "parallel"),
    )(qc, kv, kv)


def _xattn_bwd(qc, kv, do, tq, name):
    s, d = qc.shape
    m = kv.shape[0]
    dh = d // X_HEADS
    scale = dh ** -0.5
    nq = s // tq

    def body(q_ref, k_ref, v_ref, do_ref, dq_ref, dk_ref, dv_ref, dk_acc, dv_acc):
        i = pl.program_id(1)
        q, k, v = q_ref[...], k_ref[...], v_ref[...]
        dob = do_ref[...].astype(BF16)
        p = _xattn_probs(q, k, scale)
        pb = p.astype(BF16)
        dp = _dot(dob, v, 1, 1)
        ds = pb.astype(F32) * (dp - jnp.sum(dp * pb.astype(F32), axis=1, keepdims=True))
        dsb = (ds * scale).astype(BF16)
        dq_ref[...] = _dot(dsb, k, 1, 0).astype(dq_ref.dtype)
        dk_part = _dot(dsb, q, 0, 0)
        dv_part = _dot(pb, dob, 0, 0)

        @pl.when(i == 0)
        def _():
            dk_acc[...] = dk_part
            dv_acc[...] = dv_part

        @pl.when(i > 0)
        def _():
            dk_acc[...] += dk_part
            dv_acc[...] += dv_part

        @pl.when(i == nq - 1)
        def _():
            dk_ref[...] = dk_acc[...].astype(dk_ref.dtype)
            dv_ref[...] = dv_acc[...].astype(dv_ref.dtype)

    blk = pl.BlockSpec((tq, dh), lambda h, i: (i, h))
    kblk = pl.BlockSpec((m, dh), lambda h, i: (0, h))
    return _pcall(
        body, name=name, grid=(X_HEADS, nq),
        in_specs=[blk, kblk, pl.BlockSpec((m, dh), lambda h, i: (0, X_HEADS + h)), blk],
        out_specs=[blk, kblk, kblk],
        out_shape=[jax.ShapeDtypeStruct((s, d), BF16), jax.ShapeDtypeStruct((m, d), BF16), jax.ShapeDtypeStruct((m, d), BF16)],
        scratch_shapes=[pltpu.VMEM((m, dh), F32), pltpu.VMEM((m, dh), F32)],
        compiler_params=_params("parallel", "arbitrary"),
    )(qc, kv, kv, do)


def _local_step(x, mem, tgt, w):
    s, d = x.shape
    heads = d // SB_HEAD_DIM
    tm = _pick(s, (256, 128))
    tq = _pick(s, (256, 128))
    sb_tq, sb_tk = _pick(s, (512, 256, 128)), _pick(s, (256, 128))
    tc = _pick(d, (256, 128))
    g = {}

    def ffn_fwd(h, gname, wgu, wdown, tag):
        n = _rms_fwd(h, w[gname], tag + "_norm", tm)
        gu = _mm(n, w[wgu], name=tag + "_gu")
        act = _swiglu_fwd(gu, tag + "_act", tm)
        return n, gu, act, _mm(act, w[wdown], name=tag + "_down", out_dtype=F32, res=h, alpha=0.5)

    def ffn_bwd(dh, h, saved, gname, wgu, wdown, tag):
        n, gu, act = saved
        dhb = _rowcall(lambda v: 0.5 * v, [_whole(dh)], [], [(d, BF16)], tm=tm, name=tag + "_half")[0]
        g[wdown] = _mm(act, dhb, ta=True, name=tag + "_dwdown")
        dact = _mm(dhb, w[wdown], tb=True, name=tag + "_dact")
        dgu = _swiglu_bwd(dact, gu, tag + "_dgu", tm)
        g[wgu] = _mm(n, dgu, ta=True, name=tag + "_dwgu")
        dn = _mm(dgu, w[wgu], tb=True, name=tag + "_dn", out_dtype=F32)
        dh_in, g[gname] = _resid_rms_bwd(dh, dn, h, w[gname], tag + "_dnorm", tm)
        return dh_in

    n1, gu1, act1, h1 = ffn_fwd(x, "g_ffn1", "w_ffn1_gu", "w_ffn1_down", "ffn1")
    u = _rms_fwd(h1, w["g_mix"], "mix_norm", tm)
    proj = _mm(u, w["w_in"], name="mix_in")
    nd = d // SB_HEAD_DIM
    y_conv = _conv_fwd(proj, w["conv_w"], d, tc, "conv_fwd")
    sb_cols = (3 * nd, 4 * nd, 5 * nd)
    y_sb = _sb_fwd(proj, heads, sb_cols, sb_tq, sb_tk, "sb_fwd")
    a_conv = _mm(y_conv, w["w_conv_out"], name="conv_out")
    a_sb = _mm(y_sb, w["w_attn_out"], name="attn_out")
    b_conv, b_sb = w["b_gate"][:, :d], w["b_gate"][:, d:]

    def merge(ac, asb, gcp, gsp, bc, bs):
        gc = _sigmoid(gcp.astype(F32) + bc)
        gs = _sigmoid(gsp.astype(F32) + bs)
        return gc * ac.astype(F32) + gs * asb.astype(F32)

    merged = _rowcall(merge, [_whole(a_conv), _whole(a_sb), (proj, 6, d), (proj, 7, d)], [b_conv, b_sb], [(d, BF16)],
                      tm=tm, name="merge")[0]
    h2 = _mm(merged, w["w_o"], name="mix_out", out_dtype=F32, res=h1)
    hn = _rms_fwd(h2, w["g_cross"], "cross_norm", tm)
    mn = _rms_fwd(mem, w["g_mem"], "mem_norm", _pick(mem.shape[0], (256, 128)))
    qc = _mm(hn, w["w_cq"], name="cross_q")
    kv = _mm(mn, w["w_ckv"], name="cross_kv")
    oc = _xattn_fwd(qc, kv, tq, "xattn_fwd")
    h3 = _mm(oc, w["w_co"], name="cross_out", out_dtype=F32, res=h2)
    n2, gu2, act2, h4 = ffn_fwd(h3, "g_ffn2", "w_ffn2_gu", "w_ffn2_down", "ffn2")

    def head(hb, tb, gb):
        xh, r = _xhat(hb)
        err = xh * gb - tb
        dy = err * (1.0 / d)
        dxh = dy * gb
        dx = r * (dxh - xh * jnp.mean(dxh * xh, axis=-1, keepdims=True))
        row_loss = 0.5 * jnp.mean(err * err, axis=-1, keepdims=True)
        return dx, dy * xh, jnp.broadcast_to(row_loss, (row_loss.shape[0], LANES))

    dh4, g["g_final"], loss_lanes = _rowcall(head, [_whole(h4), _whole(tgt)], [w["g_final"]], [(d, F32)], [d, LANES],
                                             tm=tm, name="loss_head")

    dh3 = ffn_bwd(dh4, h3, (n2, gu2, act2), "g_ffn2", "w_ffn2_gu", "w_ffn2_down", "ffn2")
    dh3b = _rowcall(lambda v: v, [_whole(dh3)], [], [(d, BF16)], tm=tm, name="cross_cast")[0]
    g["w_co"] = _mm(oc, dh3b, ta=True, name="cross_dwco")
    doc = _mm(dh3b, w["w_co"], tb=True, name="cross_doc")
    dqc, dk, dv = _xattn_bwd(qc, kv, doc, tq, "xattn_bwd")
    dkv = jnp.concatenate([dk, dv], axis=1)
    g["w_cq"] = _mm(hn, dqc, ta=True, name="cross_dwcq")
    g["w_ckv"] = _mm(mn, dkv, ta=True, name="cross_dwckv")
    dhn = _mm(dqc, w["w_cq"], tb=True, name="cross_dhn", out_dtype=F32)
    dmn = _mm(dkv, w["w_ckv"], tb=True, name="cross_dmn", out_dtype=F32)
    g["g_mem"] = _rowcall(lambda dy, xb: dy * _xhat(xb)[0], [_whole(dmn), _whole(mem)], [], [], [d],
                          tm=_pick(mem.shape[0], (256, 128)), name="mem_dnorm")[0]
    dh2, g["g_cross"] = _resid_rms_bwd(dh3, dhn, h2, w["g_cross"], "cross_dnorm", tm)

    dh2b = _rowcall(lambda v: v, [_whole(dh2)], [], [(d, BF16)], tm=tm, name="mix_cast")[0]
    g["w_o"] = _mm(merged, dh2b, ta=True, name="mix_dwo")
    dmerged = _mm(dh2b, w["w_o"], tb=True, name="mix_dmerged")

    def merge_bwd(dm, ac, asb, gcp, gsp, bc, bs):
        dm, ac, asb = dm.astype(F32), ac.astype(F32), asb.astype(F32)
        gc = _sigmoid(gcp.astype(F32) + bc)
        gs = _sigmoid(gsp.astype(F32) + bs)
        dgc = dm * ac * gc * (1.0 - gc)
        dgs = dm * asb * gs * (1.0 - gs)
        return dm * gc, dm * gs, dgc, dgs, dgc, dgs

    da_conv, da_sb, dgc, dgs, db_conv, db_sb = _rowcall(
        merge_bwd, [_whole(dmerged), _whole(a_conv), _whole(a_sb), (proj, 6, d), (proj, 7, d)], [b_conv, b_sb],
        [(d, BF16)] * 4, [d, d], tm=tm, name="merge_bwd")
    g["b_gate"] = jnp.concatenate([db_conv, db_sb], axis=1)
    g["w_conv_out"] = _mm(y_conv, da_conv, ta=True, name="conv_dwout")
    g["w_attn_out"] = _mm(y_sb, da_sb, ta=True, name="attn_dwout")
    dy_conv = _mm(da_conv, w["w_conv_out"], tb=True, name="conv_dy")
    dy_sb = _mm(da_sb, w["w_attn_out"], tb=True, name="attn_dy")
    dcb, dcc, dcx, g["conv_w"] = _conv_bwd(dy_conv, proj, w["conv_w"], d, tc, "conv_bwd")
    dq, dk_sb, dv_sb = _sb_bwd(proj, y_sb, dy_sb, heads, sb_cols, sb_tq, sb_tk, "sb_bwd")
    dproj = jnp.concatenate([dcb, dcc, dcx, dq, dk_sb, dv_sb, dgc, dgs], axis=1)
    g["w_in"] = _mm(u, dproj, ta=True, name="mix_dwin")
    du = _mm(dproj, w["w_in"], tb=True, name="mix_du", out_dtype=F32)
    dh1, g["g_mix"] = _resid_rms_bwd(dh2, du, h1, w["g_mix"], "mix_dnorm", tm)
    dx = ffn_bwd(dh1, x, (n1, gu1, act1), "g_ffn1", "w_ffn1_gu", "w_ffn1_down", "ffn1")
    return loss_lanes, dx, g


MATS = (("w_ffn1_gu", "col"), ("w_ffn1_down", "row"), ("w_in", "col"), ("w_conv_out", "row"), ("w_attn_out", "row"),
        ("w_o", "row"), ("w_cq", "row"), ("w_ckv", "col"), ("w_co", "row"), ("w_ffn2_gu", "col"), ("w_ffn2_down", "row"))
VECS = ("g_ffn1", "g_mix", "g_cross", "g_mem", "g_ffn2", "g_final")
WEIGHTS = ("g_ffn1", "w_ffn1_gu", "w_ffn1_down", "g_mix", "w_in", "b_gate", "conv_w", "w_conv_out", "w_attn_out", "w_o",
           "g_cross", "g_mem", "w_cq", "w_ckv", "w_co", "g_ffn2", "w_ffn2_gu", "w_ffn2_down", "g_final")
CONV_ROWS = 8
ANY = pl.BlockSpec(memory_space=pl.ANY)


def _full_shape(kind, r, c):
    return (r, N_CHIPS * c) if kind == "col" else (N_CHIPS * r, c)


def _piece(ref, kind, r, c, chip, half):
    hr = r // 2
    if kind == "col":
        return ref.at[pl.ds(pl.multiple_of(half * hr, 16), hr), pl.ds(pl.multiple_of(chip * c, LANES), c)]
    return ref.at[pl.ds(pl.multiple_of(chip * r + half * hr, 16), hr), :]


def _shard_of(ref, kind, r, c, chip):
    if kind == "col":
        return ref.at[:, pl.ds(pl.multiple_of(chip * c, LANES), c)]
    return ref.at[pl.ds(pl.multiple_of(chip * r, 16), r), :]


def _place():
    x, y, c = lax.axis_index("x"), lax.axis_index("y"), lax.axis_index("c")
    others = [(1 - x, y), (x, 1 - y), (1 - x, 1 - y)]
    return x, y, c, 2 * x + y, others


def _remote(src, dst, send_sem, recv_sem, to):
    return pltpu.make_async_remote_copy(src_ref=src, dst_ref=dst, send_sem=send_sem, recv_sem=recv_sem,
                                        device_id=to, device_id_type=MESH)


def _gather_weights(shards, conv_shard):
    dims = [(kind, *sh.shape) for (_, kind), sh in zip(MATS, shards)]
    nw = len(shards)
    cc = conv_shard.shape[1]

    def body(*refs):
        shard_refs, conv_ref = refs[:nw], refs[nw]
        full_refs, conv_full = refs[nw + 1:2 * nw + 1], refs[2 * nw + 1]
        s1, r1, s2, r2, loc, r3, cs, cr, cl = refs[2 * nw + 2:]
        x, y, c, me, others = _place()
        sib = (x, y, 1 - c)

        def first(wi, k, chip_from, to):
            kind, r, cw = dims[wi]
            src = shard_refs[wi].at[pl.ds(pl.multiple_of(c * (r // 2), 16), r // 2), :]
            return _remote(src, _piece(full_refs[wi], kind, r, cw, chip_from, c), s1.at[wi, k], r1.at[wi, k], to)

        def second(wi, k, chip_from, half):
            kind, r, cw = dims[wi]
            pc = _piece(full_refs[wi], kind, r, cw, chip_from, half)
            return _remote(pc, pc, s2.at[wi, k], r2.at[wi, k], sib)

        def conv(k, chip_from, to):
            dst = conv_full.at[:, pl.ds(pl.multiple_of(chip_from * cc, LANES), cc)]
            return _remote(conv_ref, dst, cs.at[k], cr.at[k], to)

        def own(wi):
            return _remote(shard_refs[wi], _shard_of(full_refs[wi], *dims[wi], me), loc.at[wi], r3.at[wi], sib)

        conv_local = pltpu.make_async_copy(conv_ref, conv_full.at[:, pl.ds(pl.multiple_of(me * cc, LANES), cc)], cl.at[0])
        conv_local.start()
        for k, (ox, oy) in enumerate(others):
            conv(k, me, (ox, oy, c)).start()
        for wi in range(nw):
            for k, (ox, oy) in enumerate(others):
                first(wi, k, me, (ox, oy, c)).start()
        for wi in range(nw):
            own(wi).start()
        for wi in range(nw):
            for k, (ox, oy) in enumerate(others):
                first(wi, k, 2 * ox + oy, (x, y, c)).wait_recv()
                second(wi, k, 2 * ox + oy, c).start()
        for wi in range(nw):
            own(wi).wait_recv()
            for k, (ox, oy) in enumerate(others):
                second(wi, k, 2 * ox + oy, 1 - c).wait_recv()
        for k, (ox, oy) in enumerate(others):
            conv(k, 2 * ox + oy, (x, y, c)).wait_recv()
            conv(k, me, (ox, oy, c)).wait_send()
        for wi in range(nw):
            own(wi).wait_send()
            for k, (ox, oy) in enumerate(others):
                first(wi, k, me, (ox, oy, c)).wait_send()
                second(wi, k, 2 * ox + oy, c).wait_send()
        conv_local.wait()

    out_shape = [jax.ShapeDtypeStruct(_full_shape(*dm), BF16) for dm in dims]
    out_shape.append(jax.ShapeDtypeStruct((CONV_ROWS, N_CHIPS * cc), F32))
    dma = pltpu.SemaphoreType.DMA
    outs = _pcall(
        body, name="gather_weights", in_specs=[ANY] * (nw + 1), out_specs=[ANY] * (nw + 1), out_shape=out_shape,
        scratch_shapes=[dma((nw, 3)), dma((nw, 3)), dma((nw, 3)), dma((nw, 3)), dma((nw,)), dma((nw,)), dma((3,)), dma((3,)), dma((1,))],
    )(*shards, conv_shard)
    return outs[:nw], outs[nw]


def _allreduce_small(packed):
    rows, n = packed.shape

    def body(in_ref, out_ref, gath_ref, send_sems, recv_sems):
        x, y, c = lax.axis_index("x"), lax.axis_index("y"), lax.axis_index("c")
        me = 4 * x + 2 * y + c

        def peer(rel):
            return (x ^ (rel >> 2 & 1), y ^ (rel >> 1 & 1), c ^ (rel & 1))

        def copy(rel, slot, to):
            return _remote(in_ref, gath_ref.at[slot], send_sems.at[rel - 1], recv_sems.at[rel - 1], to)

        for rel in range(1, N_DEV):
            copy(rel, me, peer(rel)).start()
        gath_ref[me] = in_ref[...]
        for rel in range(1, N_DEV):
            px, py, pc = peer(rel)
            copy(rel, 4 * px + 2 * py + pc, (x, y, c)).wait_recv()
        for rel in range(1, N_DEV):
            copy(rel, me, peer(rel)).wait_send()
        tot = gath_ref[0]
        for dev in range(1, N_DEV):
            tot = tot + gath_ref[dev]
        out_ref[...] = tot

    vm = pl.BlockSpec(memory_space=pltpu.VMEM)
    return _pcall(
        body, name="allreduce_small", in_specs=[vm], out_specs=[vm, vm],
        out_shape=[jax.ShapeDtypeStruct((rows, n), F32), jax.ShapeDtypeStruct((N_DEV, rows, n), F32)],
        scratch_shapes=[pltpu.SemaphoreType.DMA((N_DEV - 1,)), pltpu.SemaphoreType.DMA((N_DEV - 1,))],
    )(packed)[0]


def _rs_cores(grads, dims):
    nw = len(grads)

    def body(*refs):
        g_refs, got_refs = refs[:nw], refs[nw:2 * nw]
        ss, rs = refs[2 * nw:]
        x, y, c, _, _ = _place()
        sib = (x, y, 1 - c)

        def give(wi, chip):
            return _remote(_piece(g_refs[wi], *dims[wi], chip, 1 - c), got_refs[wi].at[chip], ss.at[wi, chip], rs.at[wi, chip], sib)

        every = [(wi, chip) for wi in range(nw) for chip in range(N_CHIPS)]
        for wi, chip in every:
            give(wi, chip).start()
        for wi, chip in every:
            give(wi, chip).wait()

    dma = pltpu.SemaphoreType.DMA
    return _pcall(
        body, name="rs_cores", in_specs=[ANY] * nw, out_specs=[ANY] * nw,
        out_shape=[jax.ShapeDtypeStruct((N_CHIPS, r // 2, cw), BF16) for (_, r, cw) in dims],
        scratch_shapes=[dma((nw, N_CHIPS)), dma((nw, N_CHIPS))],
    )(*grads)


def _rs_chips(parts):
    nw = len(parts)

    def body(*refs):
        p_refs, got_refs = refs[:nw], refs[nw:2 * nw]
        ss, rs = refs[2 * nw:]
        x, y, c, me, others = _place()

        def send(wi, k, chip_to, to):
            return _remote(p_refs[wi].at[chip_to], got_refs[wi].at[k], ss.at[wi, k], rs.at[wi, k], to)

        for wi in range(nw):
            for k, (ox, oy) in enumerate(others):
                send(wi, k, 2 * ox + oy, (ox, oy, c)).start()
        for wi in range(nw):
            for k, (ox, oy) in enumerate(others):
                send(wi, k, me, (x, y, c)).wait_recv()
        for wi in range(nw):
            for k, (ox, oy) in enumerate(others):
                send(wi, k, 2 * ox + oy, (ox, oy, c)).wait_send()

    dma = pltpu.SemaphoreType.DMA
    return _pcall(
        body, name="rs_chips", in_specs=[ANY] * nw, out_specs=[ANY] * nw,
        out_shape=[jax.ShapeDtypeStruct((3, *p.shape[1:]), p.dtype) for p in parts],
        scratch_shapes=[dma((nw, 3)), dma((nw, 3))],
    )(*parts)


def _share_halves(bufs):
    nw = len(bufs)

    def body(*refs):
        out_refs = refs[nw:2 * nw]
        ss, rs = refs[2 * nw:]
        x, y, c, _, _ = _place()
        sib = (x, y, 1 - c)

        def send(wi, half):
            return _remote(out_refs[wi].at[half], out_refs[wi].at[half], ss.at[wi], rs.at[wi], sib)

        for wi in range(nw):
            send(wi, c).start()
        for wi in range(nw):
            send(wi, 1 - c).wait_recv()
        for wi in range(nw):
            send(wi, c).wait_send()

    dma = pltpu.SemaphoreType.DMA
    return _pcall(
        body, name="share_halves", in_specs=[ANY] * nw, out_specs=[ANY] * nw,
        out_shape=[jax.ShapeDtypeStruct(b.shape, b.dtype) for b in bufs],
        input_output_aliases={i: i for i in range(nw)}, scratch_shapes=[dma((nw,)), dma((nw,))],
    )(*bufs)


def _rows_per_block(n, c, limit_bytes=1 << 20):
    best = None
    for tm in range(16, n + 1, 16):
        if n % tm == 0 and tm * c * 4 <= limit_bytes:
            best = tm
    return best or n


def _sum_cores(grad, got, kind, place, name):
    _, hr, cw = got.shape
    tm = _rows_per_block(hr, cw)
    nb = hr // tm

    def body(place_ref, g_ref, t_ref, o_ref):
        o_ref[...] = (g_ref[...].astype(F32) + t_ref[...].astype(F32)).astype(o_ref.dtype)

    if kind == "col":
        g_spec = pl.BlockSpec((tm, cw), lambda j, i, pr: (pr[0] * nb + i, j))
    else:
        g_spec = pl.BlockSpec((tm, cw), lambda j, i, pr: ((2 * j + pr[0]) * nb + i, 0))
    blk = pl.BlockSpec((None, tm, cw), lambda j, i, pr: (j, i, 0))
    return _pcall(
        body, name=name, out_shape=jax.ShapeDtypeStruct(got.shape, BF16),
        grid_spec=pltpu.PrefetchScalarGridSpec(num_scalar_prefetch=1, grid=(N_CHIPS, nb), in_specs=[g_spec, blk], out_specs=blk),
        compiler_params=_params("parallel", "parallel"),
    )(place, grad, got)


def _sum_chips(parts, got, place, name):
    _, n, cw = got.shape
    tm = _rows_per_block(n, cw)

    def body(place_ref, p_ref, g_ref, o_ref):
        tot = p_ref[...].astype(F32)
        for k in range(3):
            tot = tot + g_ref[k].astype(F32)
        o_ref[...] = tot

    return _pcall(
        body, name=name, out_shape=jax.ShapeDtypeStruct((2, n, cw), F32),
        grid_spec=pltpu.PrefetchScalarGridSpec(
            num_scalar_prefetch=1, grid=(n // tm,),
            in_specs=[pl.BlockSpec((None, tm, cw), lambda i, pr: (pr[1], i, 0)), pl.BlockSpec((3, tm, cw), lambda i, pr: (0, i, 0))],
            out_specs=pl.BlockSpec((None, tm, cw), lambda i, pr: (pr[0], i, 0))),
        compiler_params=_params("parallel"),
    )(place, parts, got)


def _adamw(g, w, m, v, name):
    n, c = g.shape
    c1 = 1.0 - ADAM_B1 ** ADAM_STEP
    c2 = 1.0 - ADAM_B2 ** ADAM_STEP

    def fn(gb, wb, mb, vb):
        m_new = ADAM_B1 * mb + (1.0 - ADAM_B1) * gb
        v_new = ADAM_B2 * vb + (1.0 - ADAM_B2) * (gb * gb)
        delta = -ADAM_LR * ((m_new / c1) / (jnp.sqrt(v_new / c2) + ADAM_EPS) + ADAM_WD * wb)
        return gb, delta, m_new, v_new

    tm = _rows_per_block(n, c) if n % 16 == 0 else n
    return _rowcall(fn, [_whole(g), _whole(w), _whole(m), _whole(v)], [], [(c, F32)] * 4, tm=tm, name=name)


PACK_ROWS = 16


def _pack_rows(parts, width, name):
    assert sum(p.shape[0] for p in parts) <= PACK_ROWS

    def body(*refs):
        out_ref = refs[-1]
        out_ref[...] = jnp.zeros_like(out_ref)
        at = 0
        for r in refs[:-1]:
            k, n = r.shape
            if n == width:
                out_ref[at:at + k, :] = r[...]
            else:
                out_ref[at:at + k, :] = jnp.broadcast_to(r[:, :1], (k, width))
            at += k

    vm = pl.BlockSpec(memory_space=pltpu.VMEM)
    return _pcall(body, name=name, in_specs=[vm] * len(parts), out_specs=vm,
                  out_shape=jax.ShapeDtypeStruct((PACK_ROWS, width), F32))(*parts)


def _cast_shard(wm, name):
    n, c = wm.shape
    return _rowcall(lambda v: v, [_whole(wm)], [], [(c, BF16)], tm=_rows_per_block(n, c), name=name)[0]


def _step(x, mem, tgt, wts, m_in, v_in):
    d = x.shape[-1]
    cc = wts["conv_w"].shape[1]
    shards = [_cast_shard(wts[n], "cast_" + n) for n, _ in MATS]
    dims = [(kind, *sh.shape) for (_, kind), sh in zip(MATS, shards)]
    conv_pad = jnp.pad(wts["conv_w"], ((0, CONV_ROWS - CONV_K), (0, 0)))
    fulls, conv_full = _gather_weights(shards, conv_pad)
    w = {n: f for (n, _), f in zip(MATS, fulls)}
    w["conv_w"] = conv_full[:CONV_K]
    for n in VECS + ("b_gate",):
        w[n] = wts[n].reshape(1, -1)

    loss_lanes, dx, g = _local_step(x[0], mem[0], tgt[0], w)

    place = jnp.stack([lax.axis_index("c"), 2 * lax.axis_index("x") + lax.axis_index("y")]).astype(jnp.int32)
    got = _rs_cores([g[n] for n, _ in MATS], dims)
    parts = [_sum_cores(g[n], t, kind, place, "sum_cores_" + n) for (n, kind), t in zip(MATS, got)]
    landed = _rs_chips(parts)
    halves = [_sum_chips(p, t, place, "sum_chips_" + n) for (n, _), p, t in zip(MATS, parts, landed)]
    both = _share_halves(halves)
    grads = {n: b.reshape(-1, b.shape[-1]) for (n, _), b in zip(MATS, both)}

    rows = [g[n] for n in VECS] + [g["b_gate"][:, :d], g["b_gate"][:, d:], g["conv_w"], loss_lanes]
    red = _allreduce_small(_pack_rows(rows, d, "pack_small"))
    for i, n in enumerate(VECS):
        grads[n] = red[i:i + 1]
    nv = len(VECS)
    grads["b_gate"] = jnp.concatenate([red[nv:nv + 1], red[nv + 1:nv + 2]], axis=1)
    me = 2 * lax.axis_index("x") + lax.axis_index("y")
    grads["conv_w"] = lax.dynamic_slice_in_dim(red[nv + 2:nv + 2 + CONV_K], me * cc, cc, axis=1)
    loss = red[nv + 2 + CONV_K, 0]

    out = {}
    for n in WEIGHTS:
        shape = wts[n].shape
        as2d = (lambda a: a.reshape(1, -1)) if len(shape) == 1 else (lambda a: a)
        res = _adamw(grads[n], as2d(wts[n]), as2d(m_in[n]), as2d(v_in[n]), "adamw_" + n)
        out[n] = [r.reshape(shape) for r in res]
    return (loss, dx[None], *[out[n][0] for n in WEIGHTS], *[out[n][1] for n in WEIGHTS],
            *[out[n][2] for n in WEIGHTS], *[out[n][3] for n in WEIGHTS])


def kernel(x, mem, g_ffn1, w_ffn1_gu, w_ffn1_down, g_mix, w_in, b_gate, conv_w, w_conv_out, w_attn_out, w_o, g_cross, g_mem, w_cq, w_ckv, w_co, g_ffn2, w_ffn2_gu, w_ffn2_down, g_final, loss_target, m_g_ffn1, m_w_ffn1_gu, m_w_ffn1_down, m_g_mix, m_w_in, m_b_gate, m_conv_w, m_w_conv_out, m_w_attn_out, m_w_o, m_g_cross, m_g_mem, m_w_cq, m_w_ckv, m_w_co, m_g_ffn2, m_w_ffn2_gu, m_w_ffn2_down, m_g_final, v_g_ffn1, v_w_ffn1_gu, v_w_ffn1_down, v_g_mix, v_w_in, v_b_gate, v_conv_w, v_w_conv_out, v_w_attn_out, v_w_o, v_g_cross, v_g_mem, v_w_cq, v_w_ckv, v_w_co, v_g_ffn2, v_w_ffn2_gu, v_w_ffn2_down, v_g_final):
    given = dict(locals())
    wts = {n: given[n] for n in WEIGHTS}
    m_in = {n: given["m_" + n] for n in WEIGHTS}
    v_in = {n: given["v_" + n] for n in WEIGHTS}
    return _step(x, mem, loss_target, wts, m_in, v_in)
```

```python
import functools

import jax
import jax.numpy as jnp
from jax import lax
from jax.experimental import pallas as pl
from jax.experimental.pallas import tpu as pltpu

F32 = jnp.float32
BF16 = jnp.bfloat16
MESH = pl.DeviceIdType.MESH

V7X_VMEM_LIMIT_BYTES = 48 * 1024 * 1024
LANES = 128
SB_HEAD_DIM = 128
X_HEADS = 4
CONV_K = 3
RMS_EPS = 1e-6
N_CHIPS = 4
N_DEV = 8
ADAM_LR, ADAM_B1, ADAM_B2, ADAM_EPS, ADAM_WD, ADAM_STEP = 0.001, 0.9, 0.999, 1e-08, 0.01, 10


ANY = pl.BlockSpec(memory_space=pl.ANY)


def _pcall(body, **kw):
    return pl.pallas_call(body, **kw)


def _params(*sem):
    return pltpu.CompilerParams(dimension_semantics=sem, vmem_limit_bytes=V7X_VMEM_LIMIT_BYTES)


def _pick(dim, cands):
    for c in cands:
        if dim % c == 0:
            return c
    return dim


def _dot(a, b, ca, cb):
    return lax.dot_general(a, b, (((ca,), (cb,)), ((), ())), preferred_element_type=F32)


def _mm(a, b, *, name, ta=False, tb=False, out_dtype=BF16, res=None, alpha=1.0, tm=None, tn=None, tk=None, after=None):
    m, k = (a.shape[1], a.shape[0]) if ta else a.shape
    n = b.shape[0] if tb else b.shape[1]
    assert k == (b.shape[1] if tb else b.shape[0]), (a.shape, b.shape, ta, tb)
    tm = tm or _pick(m, (512, 256, 128))
    tn = tn or _pick(n, (1024, 512, 256, 128))
    tk = tk or (k if k <= 2816 else _pick(k, (1024, 512, 256, 128)))
    nk = k // tk
    assert m % tm == 0 and n % tn == 0 and k % tk == 0
    a_spec = pl.BlockSpec((tk, tm), lambda i, j, kk: (kk, i)) if ta else pl.BlockSpec((tm, tk), lambda i, j, kk: (i, kk))
    b_spec = pl.BlockSpec((tn, tk), lambda i, j, kk: (j, kk)) if tb else pl.BlockSpec((tk, tn), lambda i, j, kk: (kk, j))
    o_spec = pl.BlockSpec((tm, tn), lambda i, j, kk: (i, j))
    ca, cb = (0 if ta else 1), (1 if tb else 0)

    n_in = 2 + (res is not None) + (after is not None)

    def body(*refs):
        a_ref, b_ref = refs[:2]
        res_ref = refs[2] if res is not None else None
        o_ref = refs[n_in]
        scratch = refs[n_in + 1:]

        def finish(acc):
            val = acc if alpha == 1.0 else alpha * acc
            if res_ref is not None:
                val = res_ref[...].astype(F32) + val
            o_ref[...] = val.astype(o_ref.dtype)

        part = _dot(a_ref[...].astype(BF16), b_ref[...].astype(BF16), ca, cb)
        if nk == 1:
            finish(part)
        else:
            acc_ref = scratch[0]
            kk = pl.program_id(2)

            @pl.when(kk == 0)
            def _():
                acc_ref[...] = part

            @pl.when(kk > 0)
            def _():
                acc_ref[...] += part

            @pl.when(kk == nk - 1)
            def _():
                finish(acc_ref[...])

    ins = [a, b] + ([] if res is None else [res]) + ([] if after is None else [after])
    in_specs = [a_spec, b_spec] + ([] if res is None else [o_spec]) + ([] if after is None else [ANY])
    return _pcall(
        body, name=name, grid=(m // tm, n // tn, nk), in_specs=in_specs, out_specs=o_spec,
        out_shape=jax.ShapeDtypeStruct((m, n), out_dtype),
        scratch_shapes=[pltpu.VMEM((tm, tn), F32)] if nk > 1 else [],
        compiler_params=_params("parallel", "parallel", "arbitrary"),
    )(*ins)


def _rowcall(fn, rows, consts, outs, accs=(), *, tm, name, after=None):
    s = rows[0][0].shape[0]
    assert s % tm == 0
    n_read, n_out = len(rows) + len(consts), len(outs)
    n_in = n_read + (after is not None)

    def body(*refs):
        vals = fn(*[r[...] for r in refs[:n_read]])
        vals = vals if isinstance(vals, (tuple, list)) else (vals,)
        for o_ref, v in zip(refs[n_in:n_in + n_out], vals[:n_out]):
            o_ref[...] = v.astype(o_ref.dtype)
        if accs:
            first = pl.program_id(0) == 0
            for a_ref, v in zip(refs[n_in + n_out:], vals[n_out:]):
                tot = jnp.sum(v.astype(F32), axis=0, keepdims=True)

                @pl.when(first)
                def _(a_ref=a_ref, tot=tot):
                    a_ref[...] = tot

                @pl.when(jnp.logical_not(first))
                def _(a_ref=a_ref, tot=tot):
                    a_ref[...] += tot

    in_specs = [pl.BlockSpec((tm, w), lambda i, cb=cb: (i, cb)) for (_, cb, w) in rows]
    in_specs += [pl.BlockSpec(c.shape, lambda i: (0, 0)) for c in consts]
    in_specs += [] if after is None else [ANY]
    out_specs = [pl.BlockSpec((tm, w), lambda i: (i, 0)) for (w, _) in outs]
    out_specs += [pl.BlockSpec((1, w), lambda i: (0, 0)) for w in accs]
    out_shape = [jax.ShapeDtypeStruct((s, w), dt) for (w, dt) in outs]
    out_shape += [jax.ShapeDtypeStruct((1, w), F32) for w in accs]
    return _pcall(
        body, name=name, grid=(s // tm,), in_specs=in_specs, out_specs=out_specs, out_shape=out_shape,
        compiler_params=_params("arbitrary" if accs else "parallel"),
    )(*[r[0] for r in rows], *consts, *([] if after is None else [after]))


def _whole(a):
    return (a, 0, a.shape[1])


def _xhat(x):
    x = x.astype(F32)
    r = lax.rsqrt(jnp.mean(x * x, axis=-1, keepdims=True) + RMS_EPS)
    return x * r, r


def _rms_bwd(dy, x, g):
    xh, r = _xhat(x)
    dxh = dy.astype(F32) * g
    dx = r * (dxh - xh * jnp.mean(dxh * xh, axis=-1, keepdims=True))
    return dx, dy.astype(F32) * xh


def _sigmoid(x):
    return 1.0 / (1.0 + jnp.exp(-x))


def _rms_fwd(x, g, name, tm, after=None):
    d = x.shape[1]
    return _rowcall(lambda xb, gb: _xhat(xb)[0] * gb, [_whole(x)], [g], [(d, BF16)], tm=tm, name=name, after=after)[0]


def _swiglu_fwd(gu, name, tm):
    f = gu.shape[1] // 2

    def fn(gate, up):
        gate, up = gate.astype(F32), up.astype(F32)
        return gate * _sigmoid(gate) * up

    return _rowcall(fn, [(gu, 0, f), (gu, 1, f)], [], [(f, BF16)], tm=tm, name=name)[0]


def _swiglu_bwd(dact, gu, name, tm):
    f = gu.shape[1] // 2

    def fn(da, gate, up):
        da, gate, up = da.astype(F32), gate.astype(F32), up.astype(F32)
        sg = _sigmoid(gate)
        silu = gate * sg
        dgate = da * up * (sg + silu * (1.0 - sg))
        return jnp.concatenate([dgate, da * silu], axis=1)

    return _rowcall(fn, [_whole(dact), (gu, 0, f), (gu, 1, f)], [], [(2 * f, BF16)], tm=tm, name=name)[0]


def _resid_rms_bwd(dh, dn, x, g, name, tm):
    d = x.shape[1]

    def fn(dhb, dnb, xb, gb):
        dx, dg = _rms_bwd(dnb, xb, gb)
        return dhb.astype(F32) + dx, dg

    return _rowcall(fn, [_whole(dh), _whole(dn), _whole(x)], [g], [(d, F32)], [d], tm=tm, name=name)


def _shift_down(p, k):
    if k == 0:
        return p
    rows = lax.broadcasted_iota(jnp.int32, p.shape, 0)
    return jnp.where(rows >= k, pltpu.roll(p, k, 0), 0.0)


def _shift_up(p, k):
    if k == 0:
        return p
    s = p.shape[0]
    rows = lax.broadcasted_iota(jnp.int32, p.shape, 0)
    return jnp.where(rows < s - k, pltpu.roll(p, s - k, 0), 0.0)


def _conv_fwd(proj, conv_w, d, tc, name):
    s = proj.shape[0]
    nb = d // tc

    def body(cb_ref, cc_ref, cx_ref, w_ref, y_ref):
        p = cc_ref[...].astype(F32) * cx_ref[...].astype(F32)
        w = w_ref[...]
        acc = p * w[CONV_K - 1:CONV_K, :]
        for k in range(1, CONV_K):
            acc = acc + _shift_down(p, k) * w[CONV_K - 1 - k:CONV_K - k, :]
        y_ref[...] = (cb_ref[...].astype(F32) * acc).astype(y_ref.dtype)

    col = lambda off: pl.BlockSpec((s, tc), lambda j: (0, off * nb + j))
    return _pcall(
        body, name=name, grid=(nb,), in_specs=[col(0), col(1), col(2), pl.BlockSpec((CONV_K, tc), lambda j: (0, j))],
        out_specs=pl.BlockSpec((s, tc), lambda j: (0, j)), out_shape=jax.ShapeDtypeStruct((s, d), BF16),
        compiler_params=_params("parallel"),
    )(proj, proj, proj, conv_w)


def _conv_bwd(dy, proj, conv_w, d, tc, name):
    s = proj.shape[0]
    nb = d // tc

    def body(dy_ref, cb_ref, cc_ref, cx_ref, w_ref, dcb_ref, dcc_ref, dcx_ref, dw_ref):
        cc, cx = cc_ref[...].astype(F32), cx_ref[...].astype(F32)
        p = cc * cx
        w = w_ref[...]
        dyv = dy_ref[...].astype(F32)
        shifted = [_shift_down(p, CONV_K - 1 - k) for k in range(CONV_K)]
        conv = shifted[0] * w[0:1, :]
        for k in range(1, CONV_K):
            conv = conv + shifted[k] * w[k:k + 1, :]
        dcb_ref[...] = (dyv * conv).astype(dcb_ref.dtype)
        ds = dyv * cb_ref[...].astype(F32)
        dp = ds * w[CONV_K - 1:CONV_K, :]
        for k in range(1, CONV_K):
            dp = dp + _shift_up(ds, k) * w[CONV_K - 1 - k:CONV_K - k, :]
        dcc_ref[...] = (dp * cx).astype(dcc_ref.dtype)
        dcx_ref[...] = (dp * cc).astype(dcx_ref.dtype)
        for k in range(CONV_K):
            dw_ref[k:k + 1, :] = jnp.sum(ds * shifted[k], axis=0, keepdims=True)

    col = lambda off: pl.BlockSpec((s, tc), lambda j: (0, off * nb + j))
    blk = pl.BlockSpec((s, tc), lambda j: (0, j))
    wblk = pl.BlockSpec((CONV_K, tc), lambda j: (0, j))
    act = jax.ShapeDtypeStruct((s, d), BF16)
    return _pcall(
        body, name=name, grid=(nb,), in_specs=[blk, col(0), col(1), col(2), wblk],
        out_specs=[blk, blk, blk, wblk], out_shape=[act, act, act, jax.ShapeDtypeStruct((CONV_K, d), F32)],
        compiler_params=_params("parallel"),
    )(dy, proj, proj, proj, conv_w)


def _sb_tile(q, kj, scale, carry, tri, mask):
    z = _dot(q, kj, 1, 1) * scale
    lsz = jnp.minimum(z, 0.0) - jnp.log(1.0 + jnp.exp(-jnp.abs(z)))
    l1m = lsz - z
    if mask is not None:
        l1m = jnp.where(mask, l1m, 0.0)
    l1b = l1m.astype(BF16)
    a = jnp.exp(lsz + (carry + _dot(l1b, tri, 1, 0)))
    if mask is not None:
        a = jnp.where(mask, a, 0.0)
    return lsz, l1b, a.astype(BF16)


def _sb_masks(tq, tk):
    row = lax.broadcasted_iota(jnp.int32, (tq, tk), 0)
    col = lax.broadcasted_iota(jnp.int32, (tq, tk), 1)
    masks = [col + dj * tk < row for dj in range(tq // tk)]
    r2 = lax.broadcasted_iota(jnp.int32, (tk, tk), 0)
    c2 = lax.broadcasted_iota(jnp.int32, (tk, tk), 1)
    return masks, (r2 > c2).astype(BF16), (r2 < c2).astype(BF16)


def _sb_fwd(proj, heads, col0, tq, tk, name):
    s = proj.shape[0]
    dh = SB_HEAD_DIM
    nq, nd = s // tq, tq // tk
    scale = dh ** -0.5

    def body(q_ref, k_ref, v_ref, o_ref):
        i = pl.program_id(1)
        q = q_ref[...]
        masks, tri_right, _ = _sb_masks(tq, tk)

        def tile(j, carry, acc, mask):
            start = pl.multiple_of(j * tk, tk)
            kj = k_ref[pl.ds(start, tk), :]
            vj = v_ref[pl.ds(start, tk), :]
            _, l1b, ab = _sb_tile(q, kj, scale, carry, tri_right, mask)
            return carry + jnp.sum(l1b.astype(F32), axis=1, keepdims=True), acc + _dot(ab, vj, 1, 0)

        state = (jnp.zeros((tq, 1), F32), jnp.zeros((tq, dh), F32))
        for dj in reversed(range(nd)):
            state = tile(i * nd + dj, *state, masks[dj])
        state = lax.fori_loop(0, i * nd, lambda t, st: tile(i * nd - 1 - t, st[0], st[1], None), state)
        o_ref[...] = state[1]

    qspec = pl.BlockSpec((tq, dh), lambda h, i: (i, col0[0] + h))
    kspec = pl.BlockSpec((s, dh), lambda h, i: (0, col0[1] + h))
    vspec = pl.BlockSpec((s, dh), lambda h, i: (0, col0[2] + h))
    return _pcall(
        body, name=name, grid=(heads, nq), in_specs=[qspec, kspec, vspec],
        out_specs=pl.BlockSpec((tq, dh), lambda h, i: (i, h)), out_shape=jax.ShapeDtypeStruct((s, heads * dh), F32),
        compiler_params=_params("parallel", "parallel"),
    )(proj, proj, proj)


def _sb_bwd(proj, o, do, heads, col0, tq, tk, name):
    s = proj.shape[0]
    dh = SB_HEAD_DIM
    nq, nd = s // tq, tq // tk
    scale = dh ** -0.5

    def body(q_ref, k_ref, v_ref, o_ref, do_ref, dq_ref, dk_ref, dv_ref, dk_acc, dv_acc):
        i = pl.program_id(1)

        @pl.when(i == 0)
        def _():
            dk_acc[...] = jnp.zeros_like(dk_acc)
            dv_acc[...] = jnp.zeros_like(dv_acc)

        q = q_ref[...]
        dob = do_ref[...].astype(BF16)
        delta = jnp.sum(dob.astype(F32) * o_ref[...], axis=1, keepdims=True)
        masks, tri_right, tri_left = _sb_masks(tq, tk)

        def tile(j, carry_l, carry_g, dq, mask):
            start = pl.multiple_of(j * tk, tk)
            kj = k_ref[pl.ds(start, tk), :]
            vj = v_ref[pl.ds(start, tk), :]
            lsz, l1b, ab = _sb_tile(q, kj, scale, carry_l, tri_right, mask)
            g = _dot(dob, vj, 1, 1) * ab.astype(F32)
            carry_g = carry_g + jnp.sum(g, axis=1, keepdims=True)
            left = (delta - carry_g) + _dot(g.astype(BF16), tri_left, 1, 0)
            beta = jnp.exp(lsz)
            dz = g * (1.0 - beta) - left * beta
            if mask is not None:
                dz = jnp.where(mask, dz, 0.0)
            dzb = (dz * scale).astype(BF16)
            dk_acc[pl.ds(start, tk), :] += _dot(dzb, q, 0, 0)
            dv_acc[pl.ds(start, tk), :] += _dot(ab, dob, 0, 0)
            return carry_l + jnp.sum(l1b.astype(F32), axis=1, keepdims=True), carry_g, dq + _dot(dzb, kj, 1, 0)

        zero = jnp.zeros((tq, 1), F32)
        state = (zero, zero, jnp.zeros((tq, dh), F32))
        for dj in reversed(range(nd)):
            state = tile(i * nd + dj, *state, masks[dj])
        state = lax.fori_loop(0, i * nd, lambda t, st: tile(i * nd - 1 - t, st[0], st[1], st[2], None), state)
        dq_ref[...] = state[2].astype(dq_ref.dtype)

        @pl.when(i == nq - 1)
        def _():
            dk_ref[...] = dk_acc[...].astype(dk_ref.dtype)
            dv_ref[...] = dv_acc[...].astype(dv_ref.dtype)

    qspec = pl.BlockSpec((tq, dh), lambda h, i: (i, col0[0] + h))
    kspec = pl.BlockSpec((s, dh), lambda h, i: (0, col0[1] + h))
    vspec = pl.BlockSpec((s, dh), lambda h, i: (0, col0[2] + h))
    blk = pl.BlockSpec((tq, dh), lambda h, i: (i, h))
    full = pl.BlockSpec((s, dh), lambda h, i: (0, h))
    act = jax.ShapeDtypeStruct((s, heads * dh), BF16)
    return _pcall(
        body, name=name, grid=(heads, nq), in_specs=[qspec, kspec, vspec, blk, blk],
        out_specs=[blk, full, full], out_shape=[act, act, act],
        scratch_shapes=[pltpu.VMEM((s, dh), F32), pltpu.VMEM((s, dh), F32)],
        compiler_params=_params("parallel", "arbitrary"),
    )(proj, proj, proj, o, do)


def _xattn_probs(q, k, scale):
    sc = _dot(q, k, 1, 1) * scale
    e = jnp.exp(sc - jnp.max(sc, axis=1, keepdims=True))
    return e / jnp.sum(e, axis=1, keepdims=True)


def _xattn_fwd(qc, kv, tq, name):
    s, d = qc.shape
    m = kv.shape[0]
    dh = d // X_HEADS
    scale = dh ** -0.5

    def body(q_ref, k_ref, v_ref, o_ref):
        p = _xattn_probs(q_ref[...], k_ref[...], scale)
        o_ref[...] = _dot(p.astype(BF16), v_ref[...], 1, 0).astype(o_ref.dtype)

    blk = pl.BlockSpec((tq, dh), lambda h, i: (i, h))
    return _pcall(
        body, name=name, grid=(X_HEADS, s // tq),
        in_specs=[blk, pl.BlockSpec((m, dh), lambda h, i: (0, h)), pl.BlockSpec((m, dh), lambda h, i: (0, X_HEADS + h))],
        out_specs=blk, out_shape=jax.ShapeDtypeStruct((s, d), BF16), compiler_params=_params("parallel", "parallel"),
    )(qc, kv, kv)


def _xattn_bwd(qc, kv, do, tq, name):
    s, d = qc.shape
    m = kv.shape[0]
    dh = d // X_HEADS
    scale = dh ** -0.5
    nq = s // tq

    def body(q_ref, k_ref, v_ref, do_ref, dq_ref, dk_ref, dv_ref, dk_acc, dv_acc):
        i = pl.program_id(1)
        q, k, v = q_ref[...], k_ref[...], v_ref[...]
        dob = do_ref[...].astype(BF16)
        p = _xattn_probs(q, k, scale)
        pb = p.astype(BF16)
        dp = _dot(dob, v, 1, 1)
        ds = pb.astype(F32) * (dp - jnp.sum(dp * pb.astype(F32), axis=1, keepdims=True))
        dsb = (ds * scale).astype(BF16)
        dq_ref[...] = _dot(dsb, k, 1, 0).astype(dq_ref.dtype)
        dk_part = _dot(dsb, q, 0, 0)
        dv_part = _dot(pb, dob, 0, 0)

        @pl.when(i == 0)
        def _():
            dk_acc[...] = dk_part
            dv_acc[...] = dv_part

        @pl.when(i > 0)
        def _():
            dk_acc[...] += dk_part
            dv_acc[...] += dv_part

        @pl.when(i == nq - 1)
        def _():
            dk_ref[...] = dk_acc[...].astype(dk_ref.dtype)
            dv_ref[...] = dv_acc[...].astype(dv_ref.dtype)

    blk = pl.BlockSpec((tq, dh), lambda h, i: (i, h))
    kblk = pl.BlockSpec((m, dh), lambda h, i: (0, h))
    return _pcall(
        body, name=name, grid=(X_HEADS, nq),
        in_specs=[blk, kblk, pl.BlockSpec((m, dh), lambda h, i: (0, X_HEADS + h)), blk],
        out_specs=[blk, kblk, kblk],
        out_shape=[jax.ShapeDtypeStruct((s, d), BF16), jax.ShapeDtypeStruct((m, d), BF16), jax.ShapeDtypeStruct((m, d), BF16)],
        scratch_shapes=[pltpu.VMEM((m, dh), F32), pltpu.VMEM((m, dh), F32)],
        compiler_params=_params("parallel", "arbitrary"),
    )(qc, kv, kv, do)


def _local_step(x, mem, tgt, w, fetch=None, emit=None, after=None):
    fetch = fetch or (lambda group, after: {})
    emit = emit or (lambda group, g: None)
    w = dict(w)
    s, d = x.shape
    heads = d // SB_HEAD_DIM
    tm = _pick(s, (256, 128))
    tq = _pick(s, (256, 128))
    sb_tq, sb_tk = _pick(s, (512, 256, 128)), _pick(s, (256, 128))
    tc = _pick(d, (256, 128))
    g = {}

    def ffn_fwd(h, gname, wgu, wdown, tag, after=None):
        n = _rms_fwd(h, w[gname], tag + "_norm", tm, after=after)
        gu = _mm(n, w[wgu], name=tag + "_gu")
        act = _swiglu_fwd(gu, tag + "_act", tm)
        return n, gu, act, _mm(act, w[wdown], name=tag + "_down", out_dtype=F32, res=h, alpha=0.5)

    def ffn_bwd(dh, h, saved, gname, wgu, wdown, tag, after=None):
        n, gu, act = saved
        dhb = _rowcall(lambda v: 0.5 * v, [_whole(dh)], [], [(d, BF16)], tm=tm, name=tag + "_half", after=after)[0]
        g[wdown] = _mm(act, dhb, ta=True, name=tag + "_dwdown")
        dact = _mm(dhb, w[wdown], tb=True, name=tag + "_dact")
        dgu = _swiglu_bwd(dact, gu, tag + "_dgu", tm)
        g[wgu] = _mm(n, dgu, ta=True, name=tag + "_dwgu")
        dn = _mm(dgu, w[wgu], tb=True, name=tag + "_dn", out_dtype=F32)
        dh_in, g[gname] = _resid_rms_bwd(dh, dn, h, w[gname], tag + "_dnorm", tm)
        return dh_in

    n1, gu1, act1, h1 = ffn_fwd(x, "g_ffn1", "w_ffn1_gu", "w_ffn1_down", "ffn1", after)
    w.update(fetch("mix", h1))
    u = _rms_fwd(h1, w["g_mix"], "mix_norm", tm)
    proj = _mm(u, w["w_in"], name="mix_in")
    nd = d // SB_HEAD_DIM
    y_conv = _conv_fwd(proj, w["conv_w"], d, tc, "conv_fwd")
    sb_cols = (3 * nd, 4 * nd, 5 * nd)
    y_sb = _sb_fwd(proj, heads, sb_cols, sb_tq, sb_tk, "sb_fwd")
    a_conv = _mm(y_conv, w["w_conv_out"], name="conv_out")
    a_sb = _mm(y_sb, w["w_attn_out"], name="attn_out")
    b_conv, b_sb = w["b_gate"][:, :d], w["b_gate"][:, d:]

    def merge(ac, asb, gcp, gsp, bc, bs):
        gc = _sigmoid(gcp.astype(F32) + bc)
        gs = _sigmoid(gsp.astype(F32) + bs)
        return gc * ac.astype(F32) + gs * asb.astype(F32)

    merged = _rowcall(merge, [_whole(a_conv), _whole(a_sb), (proj, 6, d), (proj, 7, d)], [b_conv, b_sb], [(d, BF16)],
                      tm=tm, name="merge")[0]
    h2 = _mm(merged, w["w_o"], name="mix_out", out_dtype=F32, res=h1)
    w.update(fetch("rest", h2))
    hn = _rms_fwd(h2, w["g_cross"], "cross_norm", tm)
    mn = _rms_fwd(mem, w["g_mem"], "mem_norm", _pick(mem.shape[0], (256, 128)))
    qc = _mm(hn, w["w_cq"], name="cross_q")
    kv = _mm(mn, w["w_ckv"], name="cross_kv")
    oc = _xattn_fwd(qc, kv, tq, "xattn_fwd")
    h3 = _mm(oc, w["w_co"], name="cross_out", out_dtype=F32, res=h2)
    n2, gu2, act2, h4 = ffn_fwd(h3, "g_ffn2", "w_ffn2_gu", "w_ffn2_down", "ffn2")

    def head(hb, tb, gb):
        xh, r = _xhat(hb)
        err = xh * gb - tb
        dy = err * (1.0 / d)
        dxh = dy * gb
        dx = r * (dxh - xh * jnp.mean(dxh * xh, axis=-1, keepdims=True))
        row_loss = 0.5 * jnp.mean(err * err, axis=-1, keepdims=True)
        return dx, dy * xh, jnp.broadcast_to(row_loss, (row_loss.shape[0], LANES))

    dh4, g["g_final"], loss_lanes = _rowcall(head, [_whole(h4), _whole(tgt)], [w["g_final"]], [(d, F32)], [d, LANES],
                                             tm=tm, name="loss_head")

    dh3 = ffn_bwd(dh4, h3, (n2, gu2, act2), "g_ffn2", "w_ffn2_gu", "w_ffn2_down", "ffn2")
    tok = emit("ffn2", g)
    dh3b = _rowcall(lambda v: v, [_whole(dh3)], [], [(d, BF16)], tm=tm, name="cross_cast", after=tok)[0]
    g["w_co"] = _mm(oc, dh3b, ta=True, name="cross_dwco")
    doc = _mm(dh3b, w["w_co"], tb=True, name="cross_doc")
    dqc, dk, dv = _xattn_bwd(qc, kv, doc, tq, "xattn_bwd")
    dkv = jnp.concatenate([dk, dv], axis=1)
    g["w_cq"] = _mm(hn, dqc, ta=True, name="cross_dwcq")
    g["w_ckv"] = _mm(mn, dkv, ta=True, name="cross_dwckv")
    tok = emit("cross", g)
    dhn = _mm(dqc, w["w_cq"], tb=True, name="cross_dhn", out_dtype=F32, after=tok)
    dmn = _mm(dkv, w["w_ckv"], tb=True, name="cross_dmn", out_dtype=F32)
    g["g_mem"] = _rowcall(lambda dy, xb: dy * _xhat(xb)[0], [_whole(dmn), _whole(mem)], [], [], [d],
                          tm=_pick(mem.shape[0], (256, 128)), name="mem_dnorm")[0]
    dh2, g["g_cross"] = _resid_rms_bwd(dh3, dhn, h2, w["g_cross"], "cross_dnorm", tm)

    dh2b = _rowcall(lambda v: v, [_whole(dh2)], [], [(d, BF16)], tm=tm, name="mix_cast")[0]
    g["w_o"] = _mm(merged, dh2b, ta=True, name="mix_dwo")
    dmerged = _mm(dh2b, w["w_o"], tb=True, name="mix_dmerged")

    def merge_bwd(dm, ac, asb, gcp, gsp, bc, bs):
        dm, ac, asb = dm.astype(F32), ac.astype(F32), asb.astype(F32)
        gc = _sigmoid(gcp.astype(F32) + bc)
        gs = _sigmoid(gsp.astype(F32) + bs)
        dgc = dm * ac * gc * (1.0 - gc)
        dgs = dm * asb * gs * (1.0 - gs)
        return dm * gc, dm * gs, dgc, dgs, dgc, dgs

    da_conv, da_sb, dgc, dgs, db_conv, db_sb = _rowcall(
        merge_bwd, [_whole(dmerged), _whole(a_conv), _whole(a_sb), (proj, 6, d), (proj, 7, d)], [b_conv, b_sb],
        [(d, BF16)] * 4, [d, d], tm=tm, name="merge_bwd")
    g["b_gate"] = jnp.concatenate([db_conv, db_sb], axis=1)
    g["w_conv_out"] = _mm(y_conv, da_conv, ta=True, name="conv_dwout")
    g["w_attn_out"] = _mm(y_sb, da_sb, ta=True, name="attn_dwout")
    dy_conv = _mm(da_conv, w["w_conv_out"], tb=True, name="conv_dy")
    dy_sb = _mm(da_sb, w["w_attn_out"], tb=True, name="attn_dy")
    dcb, dcc, dcx, g["conv_w"] = _conv_bwd(dy_conv, proj, w["conv_w"], d, tc, "conv_bwd")
    dq, dk_sb, dv_sb = _sb_bwd(proj, y_sb, dy_sb, heads, sb_cols, sb_tq, sb_tk, "sb_bwd")
    dproj = jnp.concatenate([dcb, dcc, dcx, dq, dk_sb, dv_sb, dgc, dgs], axis=1)
    g["w_in"] = _mm(u, dproj, ta=True, name="mix_dwin")
    tok = emit("mix", g)
    du = _mm(dproj, w["w_in"], tb=True, name="mix_du", out_dtype=F32, after=tok)
    dh1, g["g_mix"] = _resid_rms_bwd(dh2, du, h1, w["g_mix"], "mix_dnorm", tm)
    dx = ffn_bwd(dh1, x, (n1, gu1, act1), "g_ffn1", "w_ffn1_gu", "w_ffn1_down", "ffn1")
    return loss_lanes, dx, g


MATS = (("w_ffn1_gu", "col"), ("w_ffn1_down", "row"), ("w_in", "col"), ("w_conv_out", "row"), ("w_attn_out", "row"),
        ("w_o", "row"), ("w_cq", "row"), ("w_ckv", "col"), ("w_co", "row"), ("w_ffn2_gu", "col"), ("w_ffn2_down", "row"))
VECS = ("g_ffn1", "g_mix", "g_cross", "g_mem", "g_ffn2", "g_final")
WEIGHTS = ("g_ffn1", "w_ffn1_gu", "w_ffn1_down", "g_mix", "w_in", "b_gate", "conv_w", "w_conv_out", "w_attn_out", "w_o",
           "g_cross", "g_mem", "w_cq", "w_ckv", "w_co", "g_ffn2", "w_ffn2_gu", "w_ffn2_down", "g_final")
CONV_ROWS = 8


def _full_shape(kind, r, c):
    return (r, N_CHIPS * c) if kind == "col" else (N_CHIPS * r, c)


def _piece(ref, kind, r, c, chip, half):
    hr = r // 2
    if kind == "col":
        return ref.at[pl.ds(pl.multiple_of(half * hr, 16), hr), pl.ds(pl.multiple_of(chip * c, LANES), c)]
    return ref.at[pl.ds(pl.multiple_of(chip * r + half * hr, 16), hr), :]


def _shard_of(ref, kind, r, c, chip):
    if kind == "col":
        return ref.at[:, pl.ds(pl.multiple_of(chip * c, LANES), c)]
    return ref.at[pl.ds(pl.multiple_of(chip * r, 16), r), :]


def _place():
    x, y, c = lax.axis_index("x"), lax.axis_index("y"), lax.axis_index("c")
    others = [(1 - x, y), (x, 1 - y), (1 - x, 1 - y)]
    return x, y, c, 2 * x + y, others


def _remote(src, dst, send_sem, recv_sem, to):
    return pltpu.make_async_remote_copy(src_ref=src, dst_ref=dst, send_sem=send_sem, recv_sem=recv_sem,
                                        device_id=to, device_id_type=MESH)


def _gather_weights(shards, dims, conv_shard):
    nw = len(shards)
    cc = conv_shard.shape[1]

    def body(*refs):
        shard_refs, conv_ref = refs[:nw], refs[nw]
        full_refs, conv_full = refs[nw + 1:2 * nw + 1], refs[2 * nw + 1]
        s1, r1, s2, r2, loc, r3, cs, cr, cl = refs[2 * nw + 2:]
        x, y, c, me, others = _place()
        sib = (x, y, 1 - c)

        def first(wi, k, chip_from, to):
            kind, r, cw = dims[wi]
            src = shard_refs[wi].at[pl.ds(pl.multiple_of(c * (r // 2), 16), r // 2), :]
            return _remote(src, _piece(full_refs[wi], kind, r, cw, chip_from, c), s1.at[wi, k], r1.at[wi, k], to)

        def second(wi, k, chip_from, half):
            kind, r, cw = dims[wi]
            pc = _piece(full_refs[wi], kind, r, cw, chip_from, half)
            return _remote(pc, pc, s2.at[wi, k], r2.at[wi, k], sib)

        def conv(k, chip_from, to):
            dst = conv_full.at[:, pl.ds(pl.multiple_of(chip_from * cc, LANES), cc)]
            return _remote(conv_ref, dst, cs.at[k], cr.at[k], to)

        def own(wi):
            return _remote(shard_refs[wi], _shard_of(full_refs[wi], *dims[wi], me), loc.at[wi], r3.at[wi], sib)

        conv_local = pltpu.make_async_copy(conv_ref, conv_full.at[:, pl.ds(pl.multiple_of(me * cc, LANES), cc)], cl.at[0])
        conv_local.start()
        for k, (ox, oy) in enumerate(others):
            conv(k, me, (ox, oy, c)).start()
        for wi in range(nw):
            for k, (ox, oy) in enumerate(others):
                first(wi, k, me, (ox, oy, c)).start()
        for wi in range(nw):
            own(wi).start()
        for wi in range(nw):
            for k, (ox, oy) in enumerate(others):
                first(wi, k, 2 * ox + oy, (x, y, c)).wait_recv()
                second(wi, k, 2 * ox + oy, c).start()
        for wi in range(nw):
            own(wi).wait_recv()
            for k, (ox, oy) in enumerate(others):
                second(wi, k, 2 * ox + oy, 1 - c).wait_recv()
        for k, (ox, oy) in enumerate(others):
            conv(k, 2 * ox + oy, (x, y, c)).wait_recv()
            conv(k, me, (ox, oy, c)).wait_send()
        for wi in range(nw):
            own(wi).wait_send()
            for k, (ox, oy) in enumerate(others):
                first(wi, k, me, (ox, oy, c)).wait_send()
                second(wi, k, 2 * ox + oy, c).wait_send()
        conv_local.wait()

    out_shape = [jax.ShapeDtypeStruct(_full_shape(*dm), BF16) for dm in dims]
    out_shape.append(jax.ShapeDtypeStruct((CONV_ROWS, N_CHIPS * cc), F32))
    dma = pltpu.SemaphoreType.DMA
    outs = _pcall(
        body, name="gather_weights", in_specs=[ANY] * (nw + 1), out_specs=[ANY] * (nw + 1), out_shape=out_shape,
        scratch_shapes=[dma((nw, 3)), dma((nw, 3)), dma((nw, 3)), dma((nw, 3)), dma((nw,)), dma((nw,)), dma((3,)), dma((3,)), dma((1,))],
    )(*shards, conv_shard)
    return outs[:nw], outs[nw]


def _allreduce_small(packed):
    rows, n = packed.shape

    def body(in_ref, out_ref, gath_ref, send_sems, recv_sems):
        x, y, c = lax.axis_index("x"), lax.axis_index("y"), lax.axis_index("c")
        me = 4 * x + 2 * y + c

        def peer(rel):
            return (x ^ (rel >> 2 & 1), y ^ (rel >> 1 & 1), c ^ (rel & 1))

        def copy(rel, slot, to):
            return _remote(in_ref, gath_ref.at[slot], send_sems.at[rel - 1], recv_sems.at[rel - 1], to)

        for rel in range(1, N_DEV):
            copy(rel, me, peer(rel)).start()
        gath_ref[me] = in_ref[...]
        for rel in range(1, N_DEV):
            px, py, pc = peer(rel)
            copy(rel, 4 * px + 2 * py + pc, (x, y, c)).wait_recv()
        for rel in range(1, N_DEV):
            copy(rel, me, peer(rel)).wait_send()
        tot = gath_ref[0]
        for dev in range(1, N_DEV):
            tot = tot + gath_ref[dev]
        out_ref[...] = tot

    vm = pl.BlockSpec(memory_space=pltpu.VMEM)
    return _pcall(
        body, name="allreduce_small", in_specs=[vm], out_specs=[vm, vm],
        out_shape=[jax.ShapeDtypeStruct((rows, n), F32), jax.ShapeDtypeStruct((N_DEV, rows, n), F32)],
        scratch_shapes=[pltpu.SemaphoreType.DMA((N_DEV - 1,)), pltpu.SemaphoreType.DMA((N_DEV - 1,))],
    )(packed)[0]


def _rs_cores(grads, dims, name):
    nw = len(grads)

    def body(*refs):
        g_refs, got_refs = refs[:nw], refs[nw:2 * nw]
        ss, rs = refs[2 * nw:]
        x, y, c, _, _ = _place()
        sib = (x, y, 1 - c)

        def give(wi, chip):
            return _remote(_piece(g_refs[wi], *dims[wi], chip, 1 - c), got_refs[wi].at[chip], ss.at[wi, chip], rs.at[wi, chip], sib)

        every = [(wi, chip) for wi in range(nw) for chip in range(N_CHIPS)]
        for wi, chip in every:
            give(wi, chip).start()
        for wi, chip in every:
            give(wi, chip).wait()

    dma = pltpu.SemaphoreType.DMA
    return _pcall(
        body, name=name, in_specs=[ANY] * nw, out_specs=[ANY] * nw,
        out_shape=[jax.ShapeDtypeStruct((N_CHIPS, r // 2, cw), BF16) for (_, r, cw) in dims],
        scratch_shapes=[dma((nw, N_CHIPS)), dma((nw, N_CHIPS))],
    )(*grads)


def _share_halves(bufs, name):
    nw = len(bufs)

    def body(*refs):
        out_refs = refs[nw:2 * nw]
        ss, rs = refs[2 * nw:]
        x, y, c, _, _ = _place()
        sib = (x, y, 1 - c)

        def send(wi, half):
            return _remote(out_refs[wi].at[half], out_refs[wi].at[half], ss.at[wi], rs.at[wi], sib)

        for wi in range(nw):
            send(wi, c).start()
        for wi in range(nw):
            send(wi, 1 - c).wait_recv()
        for wi in range(nw):
            send(wi, c).wait_send()

    dma = pltpu.SemaphoreType.DMA
    return _pcall(
        body, name=name, in_specs=[ANY] * nw, out_specs=[ANY] * nw,
        out_shape=[jax.ShapeDtypeStruct(b.shape, b.dtype) for b in bufs],
        input_output_aliases={i: i for i in range(nw)}, scratch_shapes=[dma((nw,)), dma((nw,))],
    )(*bufs)


HBM = pl.BlockSpec(memory_space=pltpu.HBM)
SEM = pl.BlockSpec(memory_space=pltpu.SEMAPHORE)
EFFECT = pltpu.SideEffectType.DATAFLOW_SIDE_EFFECTING
TOKEN = (8, LANES)


def _split_start(name, plan, n_copies, srcs, lands, after=None):
    ns, nl = len(srcs), len(lands)
    n_in = ns + nl + (after is not None)

    def body(*refs):
        outs = refs[n_in:]
        sends, _ = plan(refs[:ns], refs[ns:ns + nl], outs[0], outs[1])
        for cp in sends:
            cp.start()
        outs[-1][...] = jnp.zeros(TOKEN, F32)

    held = [pltpu.HBM(a.shape, a.dtype) for a in (*srcs, *lands)]
    dma = pltpu.SemaphoreType.DMA((n_copies,))
    ins = [pltpu.with_memory_space_constraint(a, pltpu.HBM) for a in (*srcs, *lands)]
    outs = _pcall(
        body, name=name, in_specs=[HBM] * (ns + nl) + ([] if after is None else [ANY]),
        out_specs=(SEM, SEM, *[HBM] * (ns + nl), pl.BlockSpec(memory_space=pltpu.VMEM)),
        out_shape=(dma, dma, *held, jax.ShapeDtypeStruct(TOKEN, F32)),
        input_output_aliases={i: 2 + i for i in range(ns + nl)},
        compiler_params=pltpu.CompilerParams(has_side_effects=EFFECT),
    )(*ins, *([] if after is None else [after]))
    return outs[0], outs[1], list(outs[2:2 + ns]), list(outs[2 + ns:2 + ns + nl]), outs[-1]


def _split_wait(name, plan, send_sems, recv_sems, srcs, lands, after):
    ns, nl = len(srcs), len(lands)

    def body(*refs):
        sends, recvs = plan(refs[:ns], refs[ns:ns + nl], refs[ns + nl], refs[ns + nl + 1])
        for cp in sends:
            cp.wait_send()
        for cp in recvs:
            cp.wait_recv()

    outs = _pcall(
        body, name=name, in_specs=[HBM] * (ns + nl) + [SEM, SEM, ANY], out_specs=[HBM] * (ns + nl),
        out_shape=[pltpu.HBM(a.shape, a.dtype) for a in (*srcs, *lands)],
        input_output_aliases={i: i for i in range(ns + nl)},
        compiler_params=pltpu.CompilerParams(has_side_effects=EFFECT),
    )(*srcs, *lands, send_sems, recv_sems, after)
    return list(outs[:ns]), list(outs[ns:])


def _gather_plan(dims):
    def plan(shard_refs, full_refs, ss, rs):
        x, y, c, me, others = _place()
        sends, recvs = [], []
        for wi, (kind, r, cw) in enumerate(dims):
            half = shard_refs[wi].at[pl.ds(pl.multiple_of(c * (r // 2), 16), r // 2), :]
            for k, (ox, oy) in enumerate(others):
                sem = 4 * wi + k
                sends.append(_remote(half, _piece(full_refs[wi], kind, r, cw, me, c), ss.at[sem], rs.at[sem], (ox, oy, c)))
                recvs.append(_remote(half, _piece(full_refs[wi], kind, r, cw, 2 * ox + oy, c), ss.at[sem], rs.at[sem], (x, y, c)))
            sem = 4 * wi + 3
            own = _remote(shard_refs[wi], _shard_of(full_refs[wi], kind, r, cw, me), ss.at[sem], rs.at[sem], (x, y, 1 - c))
            sends.append(own)
            recvs.append(own)
        return sends, recvs

    return plan


def _gather_forward(fulls, dims, name):
    nw = len(fulls)

    def body(*refs):
        full_refs = refs[nw:2 * nw]
        ss, rs = refs[2 * nw:]
        x, y, c, _, others = _place()

        def pass_on(wi, k, half):
            ox, oy = others[k]
            pc = _piece(full_refs[wi], *dims[wi], 2 * ox + oy, half)
            return _remote(pc, pc, ss.at[wi, k], rs.at[wi, k], (x, y, 1 - c))

        every = [(wi, k) for wi in range(nw) for k in range(3)]
        for wi, k in every:
            pass_on(wi, k, c).start()
        for wi, k in every:
            pass_on(wi, k, 1 - c).wait_recv()
        for wi, k in every:
            pass_on(wi, k, c).wait_send()

    dma = pltpu.SemaphoreType.DMA
    return _pcall(
        body, name=name, in_specs=[ANY] * nw, out_specs=[ANY] * nw,
        out_shape=[jax.ShapeDtypeStruct(f.shape, f.dtype) for f in fulls],
        input_output_aliases={i: i for i in range(nw)}, scratch_shapes=[dma((nw, 3)), dma((nw, 3))],
    )(*fulls)


def _rs_chips_plan(nw):
    def plan(p_refs, land_refs, ss, rs):
        x, y, c, me, others = _place()
        sends, recvs = [], []
        for wi in range(nw):
            for k, (ox, oy) in enumerate(others):
                sem = 3 * wi + k
                sends.append(_remote(p_refs[wi].at[2 * ox + oy], land_refs[wi].at[k], ss.at[sem], rs.at[sem], (ox, oy, c)))
                recvs.append(_remote(p_refs[wi].at[me], land_refs[wi].at[k], ss.at[sem], rs.at[sem], (x, y, c)))
        return sends, recvs

    return plan


def _rows_per_block(n, c, limit_bytes=1 << 20):
    best = None
    for tm in range(16, n + 1, 16):
        if n % tm == 0 and tm * c * 4 <= limit_bytes:
            best = tm
    return best or n


def _sum_cores(grad, got, kind, place, name):
    _, hr, cw = got.shape
    tm = _rows_per_block(hr, cw)
    nb = hr // tm

    def body(place_ref, g_ref, t_ref, o_ref):
        o_ref[...] = (g_ref[...].astype(F32) + t_ref[...].astype(F32)).astype(o_ref.dtype)

    if kind == "col":
        g_spec = pl.BlockSpec((tm, cw), lambda j, i, pr: (pr[0] * nb + i, j))
    else:
        g_spec = pl.BlockSpec((tm, cw), lambda j, i, pr: ((2 * j + pr[0]) * nb + i, 0))
    blk = pl.BlockSpec((None, tm, cw), lambda j, i, pr: (j, i, 0))
    return _pcall(
        body, name=name, out_shape=jax.ShapeDtypeStruct(got.shape, BF16),
        grid_spec=pltpu.PrefetchScalarGridSpec(num_scalar_prefetch=1, grid=(N_CHIPS, nb), in_specs=[g_spec, blk], out_specs=blk),
        compiler_params=_params("parallel", "parallel"),
    )(place, grad, got)


def _sum_chips(parts, got, place, name):
    _, n, cw = got.shape
    tm = _rows_per_block(n, cw)

    def body(place_ref, p_ref, g_ref, o_ref):
        tot = p_ref[...].astype(F32)
        for k in range(3):
            tot = tot + g_ref[k].astype(F32)
        o_ref[...] = tot

    return _pcall(
        body, name=name, out_shape=jax.ShapeDtypeStruct((2, n, cw), F32),
        grid_spec=pltpu.PrefetchScalarGridSpec(
            num_scalar_prefetch=1, grid=(n // tm,),
            in_specs=[pl.BlockSpec((None, tm, cw), lambda i, pr: (pr[1], i, 0)), pl.BlockSpec((3, tm, cw), lambda i, pr: (0, i, 0))],
            out_specs=pl.BlockSpec((None, tm, cw), lambda i, pr: (pr[0], i, 0))),
        compiler_params=_params("parallel"),
    )(place, parts, got)


def _adamw(g, w, m, v, name):
    n, c = g.shape
    c1 = 1.0 - ADAM_B1 ** ADAM_STEP
    c2 = 1.0 - ADAM_B2 ** ADAM_STEP

    def fn(gb, wb, mb, vb):
        m_new = ADAM_B1 * mb + (1.0 - ADAM_B1) * gb
        v_new = ADAM_B2 * vb + (1.0 - ADAM_B2) * (gb * gb)
        delta = -ADAM_LR * ((m_new / c1) / (jnp.sqrt(v_new / c2) + ADAM_EPS) + ADAM_WD * wb)
        return gb, delta, m_new, v_new

    tm = _rows_per_block(n, c) if n % 16 == 0 else n
    return _rowcall(fn, [_whole(g), _whole(w), _whole(m), _whole(v)], [], [(c, F32)] * 4, tm=tm, name=name)


PACK_ROWS = 16


def _pack_rows(parts, width, name, after=None):
    assert sum(p.shape[0] for p in parts) <= PACK_ROWS

    def body(*refs):
        out_ref = refs[-1]
        out_ref[...] = jnp.zeros_like(out_ref)
        at = 0
        for r in refs[:len(parts)]:
            k, n = r.shape
            if n == width:
                out_ref[at:at + k, :] = r[...]
            else:
                out_ref[at:at + k, :] = jnp.broadcast_to(r[:, :1], (k, width))
            at += k

    vm = pl.BlockSpec(memory_space=pltpu.VMEM)
    return _pcall(body, name=name, in_specs=[vm] * len(parts) + ([] if after is None else [ANY]), out_specs=vm,
                  out_shape=jax.ShapeDtypeStruct((PACK_ROWS, width), F32))(*parts, *([] if after is None else [after]))


def _cast_shard(wm, name):
    n, c = wm.shape
    return _rowcall(lambda v: v, [_whole(wm)], [], [(c, BF16)], tm=_rows_per_block(n, c), name=name)[0]


GROUPS = {
    "ffn1": ("w_ffn1_gu", "w_ffn1_down"),
    "mix": ("w_in", "w_conv_out", "w_attn_out", "w_o"),
    "rest": ("w_cq", "w_ckv", "w_co", "w_ffn2_gu", "w_ffn2_down"),
}
REDUCE_GROUPS = {
    "ffn2": ("w_ffn2_down", "w_ffn2_gu"),
    "cross": ("w_co", "w_cq", "w_ckv"),
    "mix": ("w_o", "w_conv_out", "w_attn_out", "w_in"),
    "ffn1": ("w_ffn1_down", "w_ffn1_gu"),
}
KIND = dict(MATS)


def _step(x, mem, tgt, wts, m_in, v_in):
    d = x.shape[-1]
    cc = wts["conv_w"].shape[1]
    shard = {n: _cast_shard(wts[n], "cast_" + n) for n, _ in MATS}
    dims = {n: (KIND[n], *shard[n].shape) for n in shard}
    place = jnp.stack([lax.axis_index("c"), 2 * lax.axis_index("x") + lax.axis_index("y")]).astype(jnp.int32)

    first = GROUPS["ffn1"]
    conv_pad = jnp.pad(wts["conv_w"], ((0, CONV_ROWS - CONV_K), (0, 0)))
    fulls, conv_full = _gather_weights([shard[n] for n in first], [dims[n] for n in first], conv_pad)
    w = dict(zip(first, fulls))
    w["conv_w"] = conv_full[:CONV_K]
    for n in VECS + ("b_gate",):
        w[n] = wts[n].reshape(1, -1)
    flying, token = {}, conv_full
    for grp in ("mix", "rest"):
        names = GROUPS[grp]
        gd = [dims[n] for n in names]
        lands = [lax.empty(_full_shape(*dm), BF16) for dm in gd]
        plan = _gather_plan(gd)
        ss, rs, srcs, lands, token = _split_start("gather_start_" + grp, plan, 4 * len(names), [shard[n] for n in names], lands, token)
        flying[grp] = (plan, ss, rs, srcs, lands, gd)

    def fetch(grp, after):
        plan, ss, rs, srcs, lands, gd = flying[grp]
        _, lands = _split_wait("gather_wait_" + grp, plan, ss, rs, srcs, lands, after)
        return dict(zip(GROUPS[grp], _gather_forward(lands, gd, "gather_forward_" + grp)))

    sent = {}

    def emit(grp, g):
        names = REDUCE_GROUPS[grp]
        gd = [dims[n] for n in names]
        got = _rs_cores([g[n] for n in names], gd, "rs_cores_" + grp)
        parts = [_sum_cores(g[n], t, KIND[n], place, "sum_cores_" + n) for n, t in zip(names, got)]
        lands = [lax.empty((3, *p.shape[1:]), BF16) for p in parts]
        plan = _rs_chips_plan(len(names))
        ss, rs, srcs, lands, tok = _split_start("rs_chips_start_" + grp, plan, 3 * len(names), parts, lands)
        sent[grp] = (plan, ss, rs, srcs, lands)
        return tok

    loss_lanes, dx, g = _local_step(x[0], mem[0], tgt[0], w, fetch, emit, token)
    last = emit("ffn1", g)

    rows = [g[n] for n in VECS] + [g["b_gate"][:, :d], g["b_gate"][:, d:], g["conv_w"], loss_lanes]
    red = _allreduce_small(_pack_rows(rows, d, "pack_small", after=last))
    grads = {n: red[i:i + 1] for i, n in enumerate(VECS)}
    nv = len(VECS)
    grads["b_gate"] = jnp.concatenate([red[nv:nv + 1], red[nv + 1:nv + 2]], axis=1)
    me = 2 * lax.axis_index("x") + lax.axis_index("y")
    grads["conv_w"] = lax.dynamic_slice_in_dim(red[nv + 2:nv + 2 + CONV_K], me * cc, cc, axis=1)
    loss = red[nv + 2 + CONV_K, 0]

    def update(n):
        shape = wts[n].shape
        as2d = (lambda a: a.reshape(1, -1)) if len(shape) == 1 else (lambda a: a)
        return [r.reshape(shape) for r in _adamw(grads[n], as2d(wts[n]), as2d(m_in[n]), as2d(v_in[n]), "adamw_" + n)]

    out = {n: update(n) for n in WEIGHTS if n not in KIND}

    after = red
    for grp, names in REDUCE_GROUPS.items():
        plan, ss, rs, srcs, lands = sent[grp]
        parts, landed = _split_wait("rs_chips_wait_" + grp, plan, ss, rs, srcs, lands, after)
        halves = [_sum_chips(p, t, place, "sum_chips_" + n) for n, p, t in zip(names, parts, landed)]
        both = _share_halves(halves, "share_halves_" + grp)
        for n, b in zip(names, both):
            grads[n] = b.reshape(-1, b.shape[-1])
            out[n] = update(n)
        after = out[names[-1]][1]
    return (loss, dx[None], *[out[n][0] for n in WEIGHTS], *[out[n][1] for n in WEIGHTS],
            *[out[n][2] for n in WEIGHTS], *[out[n][3] for n in WEIGHTS])


def kernel(x, mem, g_ffn1, w_ffn1_gu, w_ffn1_down, g_mix, w_in, b_gate, conv_w, w_conv_out, w_attn_out, w_o, g_cross, g_mem, w_cq, w_ckv, w_co, g_ffn2, w_ffn2_gu, w_ffn2_down, g_final, loss_target, m_g_ffn1, m_w_ffn1_gu, m_w_ffn1_down, m_g_mix, m_w_in, m_b_gate, m_conv_w, m_w_conv_out, m_w_attn_out, m_w_o, m_g_cross, m_g_mem, m_w_cq, m_w_ckv, m_w_co, m_g_ffn2, m_w_ffn2_gu, m_w_ffn2_down, m_g_final, v_g_ffn1, v_w_ffn1_gu, v_w_ffn1_down, v_g_mix, v_w_in, v_b_gate, v_conv_w, v_w_conv_out, v_w_attn_out, v_w_o, v_g_cross, v_g_mem, v_w_cq, v_w_ckv, v_w_co, v_g_ffn2, v_w_ffn2_gu, v_w_ffn2_down, v_g_final):
    given = dict(locals())
    wts = {n: given[n] for n in WEIGHTS}
    m_in = {n: given["m_" + n] for n in WEIGHTS}
    v_in = {n: given["v_" + n] for n in WEIGHTS}
    return _step(x, mem, loss_target, wts, m_in, v_in)
```

```python
import functools

import jax
import jax.numpy as jnp
from jax import lax
from jax.experimental import pallas as pl
from jax.experimental.pallas import tpu as pltpu

F32 = jnp.float32
BF16 = jnp.bfloat16
MESH = pl.DeviceIdType.MESH

V7X_VMEM_LIMIT_BYTES = 48 * 1024 * 1024
MM_VMEM_BUDGET_BYTES = 36 * 1024 * 1024
MM_WHOLE_K = 2816
LANES = 128
SB_HEAD_DIM = 128
X_HEADS = 4
CONV_K = 3
RMS_EPS = 1e-6
N_CHIPS = 4
N_DEV = 8
ADAM_LR, ADAM_B1, ADAM_B2, ADAM_EPS, ADAM_WD, ADAM_STEP = 0.001, 0.9, 0.999, 1e-08, 0.01, 10


ANY = pl.BlockSpec(memory_space=pl.ANY)


def _pcall(body, **kw):
    return pl.pallas_call(body, **kw)


def _params(*sem):
    return pltpu.CompilerParams(dimension_semantics=sem, vmem_limit_bytes=V7X_VMEM_LIMIT_BYTES)


def _pick(dim, cands):
    for c in cands:
        if dim % c == 0:
            return c
    return dim


def _dot(a, b, ca, cb):
    return lax.dot_general(a, b, (((ca,), (cb,)), ((), ())), preferred_element_type=F32)


def _mm(a, b, *, name, ta=False, tb=False, out_dtype=BF16, res=None, alpha=1.0, tm=None, tn=None, tk=None, after=None):
    m, k = (a.shape[1], a.shape[0]) if ta else a.shape
    n = b.shape[0] if tb else b.shape[1]
    assert k == (b.shape[1] if tb else b.shape[0]), (a.shape, b.shape, ta, tb)
    if ta:
        tm = tm or _pick(m, (512, 256, 128))
        tn = tn or _pick(n, (1024, 512, 256, 128))
        tk = tk or (k if k <= MM_WHOLE_K else _pick(k, (1024, 512, 256, 128)))
    else:
        tk = tk or (k if k <= MM_WHOLE_K else _pick(k, (MM_WHOLE_K, 2048, 1024, 512, 256, 128)))
        tn = tn or _pick(n, (512, 1408, 256, 128))
        per_row = 2 * (tk * a.dtype.itemsize + tn * (jnp.dtype(out_dtype).itemsize + (0 if res is None else res.dtype.itemsize)))
        per_row += 4 * tn if tk < k else 0
        rows = (MM_VMEM_BUDGET_BYTES - 2 * tk * tn * b.dtype.itemsize) // per_row
        tm = tm or next((c for c in (2048, 1024, 512, 256, 128) if m % c == 0 and c <= rows), m)
    nk = k // tk
    assert m % tm == 0 and n % tn == 0 and k % tk == 0
    a_spec = pl.BlockSpec((tk, tm), lambda i, j, kk: (kk, i)) if ta else pl.BlockSpec((tm, tk), lambda i, j, kk: (i, kk))
    b_spec = pl.BlockSpec((tn, tk), lambda i, j, kk: (j, kk)) if tb else pl.BlockSpec((tk, tn), lambda i, j, kk: (kk, j))
    o_spec = pl.BlockSpec((tm, tn), lambda i, j, kk: (i, j))
    ca, cb = (0 if ta else 1), (1 if tb else 0)

    n_in = 2 + (res is not None) + (after is not None)

    def body(*refs):
        a_ref, b_ref = refs[:2]
        res_ref = refs[2] if res is not None else None
        o_ref = refs[n_in]
        scratch = refs[n_in + 1:]

        def finish(acc):
            val = acc if alpha == 1.0 else alpha * acc
            if res_ref is not None:
                val = res_ref[...].astype(F32) + val
            o_ref[...] = val.astype(o_ref.dtype)

        part = _dot(a_ref[...].astype(BF16), b_ref[...].astype(BF16), ca, cb)
        if nk == 1:
            finish(part)
        else:
            acc_ref = scratch[0]
            kk = pl.program_id(2)

            @pl.when(kk == 0)
            def _():
                acc_ref[...] = part

            @pl.when(kk > 0)
            def _():
                acc_ref[...] += part

            @pl.when(kk == nk - 1)
            def _():
                finish(acc_ref[...])

    ins = [a, b] + ([] if res is None else [res]) + ([] if after is None else [after])
    in_specs = [a_spec, b_spec] + ([] if res is None else [o_spec]) + ([] if after is None else [ANY])
    return _pcall(
        body, name=name, grid=(m // tm, n // tn, nk), in_specs=in_specs, out_specs=o_spec,
        out_shape=jax.ShapeDtypeStruct((m, n), out_dtype),
        scratch_shapes=[pltpu.VMEM((tm, tn), F32)] if nk > 1 else [],
        compiler_params=_params("parallel", "parallel", "arbitrary"),
    )(*ins)


def _rowcall(fn, rows, consts, outs, accs=(), *, tm, name, after=None):
    s = rows[0][0].shape[0]
    assert s % tm == 0
    n_read, n_out = len(rows) + len(consts), len(outs)
    n_in = n_read + (after is not None)

    def body(*refs):
        vals = fn(*[r[...] for r in refs[:n_read]])
        vals = vals if isinstance(vals, (tuple, list)) else (vals,)
        for o_ref, v in zip(refs[n_in:n_in + n_out], vals[:n_out]):
            o_ref[...] = v.astype(o_ref.dtype)
        if accs:
            first = pl.program_id(0) == 0
            for a_ref, v in zip(refs[n_in + n_out:], vals[n_out:]):
                tot = jnp.sum(v.astype(F32), axis=0, keepdims=True)

                @pl.when(first)
                def _(a_ref=a_ref, tot=tot):
                    a_ref[...] = tot

                @pl.when(jnp.logical_not(first))
                def _(a_ref=a_ref, tot=tot):
                    a_ref[...] += tot

    in_specs = [pl.BlockSpec((tm, w), lambda i, cb=cb: (i, cb)) for (_, cb, w) in rows]
    in_specs += [pl.BlockSpec(c.shape, lambda i: (0, 0)) for c in consts]
    in_specs += [] if after is None else [ANY]
    out_specs = [pl.BlockSpec((tm, w), lambda i: (i, 0)) for (w, _) in outs]
    out_specs += [pl.BlockSpec((1, w), lambda i: (0, 0)) for w in accs]
    out_shape = [jax.ShapeDtypeStruct((s, w), dt) for (w, dt) in outs]
    out_shape += [jax.ShapeDtypeStruct((1, w), F32) for w in accs]
    return _pcall(
        body, name=name, grid=(s // tm,), in_specs=in_specs, out_specs=out_specs, out_shape=out_shape,
        compiler_params=_params("arbitrary" if accs else "parallel"),
    )(*[r[0] for r in rows], *consts, *([] if after is None else [after]))


def _whole(a):
    return (a, 0, a.shape[1])


def _xhat(x):
    x = x.astype(F32)
    r = lax.rsqrt(jnp.mean(x * x, axis=-1, keepdims=True) + RMS_EPS)
    return x * r, r


def _rms_bwd(dy, x, g):
    xh, r = _xhat(x)
    dxh = dy.astype(F32) * g
    dx = r * (dxh - xh * jnp.mean(dxh * xh, axis=-1, keepdims=True))
    return dx, dy.astype(F32) * xh


def _sigmoid(x):
    return 1.0 / (1.0 + jnp.exp(-x))


def _rms_fwd(x, g, name, tm, after=None):
    d = x.shape[1]
    return _rowcall(lambda xb, gb: _xhat(xb)[0] * gb, [_whole(x)], [g], [(d, BF16)], tm=tm, name=name, after=after)[0]


def _swiglu_fwd(gu, name, tm):
    f = gu.shape[1] // 2

    def fn(gate, up):
        gate, up = gate.astype(F32), up.astype(F32)
        return gate * _sigmoid(gate) * up

    return _rowcall(fn, [(gu, 0, f), (gu, 1, f)], [], [(f, BF16)], tm=tm, name=name)[0]


def _swiglu_bwd(dact, gu, name, tm):
    f = gu.shape[1] // 2

    def fn(da, gate, up):
        da, gate, up = da.astype(F32), gate.astype(F32), up.astype(F32)
        sg = _sigmoid(gate)
        silu = gate * sg
        dgate = da * up * (sg + silu * (1.0 - sg))
        return jnp.concatenate([dgate, da * silu], axis=1)

    return _rowcall(fn, [_whole(dact), (gu, 0, f), (gu, 1, f)], [], [(2 * f, BF16)], tm=tm, name=name)[0]


def _resid_rms_bwd(dh, dn, x, g, name, tm):
    d = x.shape[1]

    def fn(dhb, dnb, xb, gb):
        dx, dg = _rms_bwd(dnb, xb, gb)
        return dhb.astype(F32) + dx, dg

    return _rowcall(fn, [_whole(dh), _whole(dn), _whole(x)], [g], [(d, F32)], [d], tm=tm, name=name)


def _shift_down(p, k):
    if k == 0:
        return p
    rows = lax.broadcasted_iota(jnp.int32, p.shape, 0)
    return jnp.where(rows >= k, pltpu.roll(p, k, 0), 0.0)


def _shift_up(p, k):
    if k == 0:
        return p
    s = p.shape[0]
    rows = lax.broadcasted_iota(jnp.int32, p.shape, 0)
    return jnp.where(rows < s - k, pltpu.roll(p, s - k, 0), 0.0)


def _conv_fwd(proj, conv_w, d, tc, name):
    s = proj.shape[0]
    nb = d // tc

    def body(cb_ref, cc_ref, cx_ref, w_ref, y_ref):
        p = cc_ref[...].astype(F32) * cx_ref[...].astype(F32)
        w = w_ref[...]
        acc = p * w[CONV_K - 1:CONV_K, :]
        for k in range(1, CONV_K):
            acc = acc + _shift_down(p, k) * w[CONV_K - 1 - k:CONV_K - k, :]
        y_ref[...] = (cb_ref[...].astype(F32) * acc).astype(y_ref.dtype)

    col = lambda off: pl.BlockSpec((s, tc), lambda j: (0, off * nb + j))
    return _pcall(
        body, name=name, grid=(nb,), in_specs=[col(0), col(1), col(2), pl.BlockSpec((CONV_K, tc), lambda j: (0, j))],
        out_specs=pl.BlockSpec((s, tc), lambda j: (0, j)), out_shape=jax.ShapeDtypeStruct((s, d), BF16),
        compiler_params=_params("parallel"),
    )(proj, proj, proj, conv_w)


def _conv_bwd(dy, proj, conv_w, d, tc, name):
    s = proj.shape[0]
    nb = d // tc

    def body(dy_ref, cb_ref, cc_ref, cx_ref, w_ref, dcb_ref, dcc_ref, dcx_ref, dw_ref):
        cc, cx = cc_ref[...].astype(F32), cx_ref[...].astype(F32)
        p = cc * cx
        w = w_ref[...]
        dyv = dy_ref[...].astype(F32)
        shifted = [_shift_down(p, CONV_K - 1 - k) for k in range(CONV_K)]
        conv = shifted[0] * w[0:1, :]
        for k in range(1, CONV_K):
            conv = conv + shifted[k] * w[k:k + 1, :]
        dcb_ref[...] = (dyv * conv).astype(dcb_ref.dtype)
        ds = dyv * cb_ref[...].astype(F32)
        dp = ds * w[CONV_K - 1:CONV_K, :]
        for k in range(1, CONV_K):
            dp = dp + _shift_up(ds, k) * w[CONV_K - 1 - k:CONV_K - k, :]
        dcc_ref[...] = (dp * cx).astype(dcc_ref.dtype)
        dcx_ref[...] = (dp * cc).astype(dcx_ref.dtype)
        for k in range(CONV_K):
            dw_ref[k:k + 1, :] = jnp.sum(ds * shifted[k], axis=0, keepdims=True)

    col = lambda off: pl.BlockSpec((s, tc), lambda j: (0, off * nb + j))
    blk = pl.BlockSpec((s, tc), lambda j: (0, j))
    wblk = pl.BlockSpec((CONV_K, tc), lambda j: (0, j))
    act = jax.ShapeDtypeStruct((s, d), BF16)
    return _pcall(
        body, name=name, grid=(nb,), in_specs=[blk, col(0), col(1), col(2), wblk],
        out_specs=[blk, blk, blk, wblk], out_shape=[act, act, act, jax.ShapeDtypeStruct((CONV_K, d), F32)],
        compiler_params=_params("parallel"),
    )(dy, proj, proj, proj, conv_w)


def _sb_tile(q, kj, scale, carry, tri, mask):
    z = _dot(q, kj, 1, 1) * scale
    lsz = jnp.minimum(z, 0.0) - jnp.log(1.0 + jnp.exp(-jnp.abs(z)))
    l1m = lsz - z
    if mask is not None:
        l1m = jnp.where(mask, l1m, 0.0)
    l1b = l1m.astype(BF16)
    a = jnp.exp(lsz + (carry + _dot(l1b, tri, 1, 0)))
    if mask is not None:
        a = jnp.where(mask, a, 0.0)
    return lsz, l1b, a.astype(BF16)


def _sb_masks(tq, tk):
    row = lax.broadcasted_iota(jnp.int32, (tq, tk), 0)
    col = lax.broadcasted_iota(jnp.int32, (tq, tk), 1)
    masks = [col + dj * tk < row for dj in range(tq // tk)]
    r2 = lax.broadcasted_iota(jnp.int32, (tk, tk), 0)
    c2 = lax.broadcasted_iota(jnp.int32, (tk, tk), 1)
    return masks, (r2 > c2).astype(BF16), (r2 < c2).astype(BF16)


def _sb_fwd(proj, heads, col0, tq, tk, name):
    s = proj.shape[0]
    dh = SB_HEAD_DIM
    nq, nd = s // tq, tq // tk
    scale = dh ** -0.5

    def body(q_ref, k_ref, v_ref, o_ref):
        i = pl.program_id(1)
        q = q_ref[...]
        masks, tri_right, _ = _sb_masks(tq, tk)

        def tile(j, carry, acc, mask):
            start = pl.multiple_of(j * tk, tk)
            kj = k_ref[pl.ds(start, tk), :]
            vj = v_ref[pl.ds(start, tk), :]
            _, l1b, ab = _sb_tile(q, kj, scale, carry, tri_right, mask)
            return carry + jnp.sum(l1b.astype(F32), axis=1, keepdims=True), acc + _dot(ab, vj, 1, 0)

        state = (jnp.zeros((tq, 1), F32), jnp.zeros((tq, dh), F32))
        for dj in reversed(range(nd)):
            state = tile(i * nd + dj, *state, masks[dj])
        state = lax.fori_loop(0, i * nd, lambda t, st: tile(i * nd - 1 - t, st[0], st[1], None), state)
        o_ref[...] = state[1]

    qspec = pl.BlockSpec((tq, dh), lambda h, i: (i, col0[0] + h))
    kspec = pl.BlockSpec((s, dh), lambda h, i: (0, col0[1] + h))
    vspec = pl.BlockSpec((s, dh), lambda h, i: (0, col0[2] + h))
    return _pcall(
        body, name=name, grid=(heads, nq), in_specs=[qspec, kspec, vspec],
        out_specs=pl.BlockSpec((tq, dh), lambda h, i: (i, h)), out_shape=jax.ShapeDtypeStruct((s, heads * dh), F32),
        compiler_params=_params("parallel", "parallel"),
    )(proj, proj, proj)


def _sb_bwd(proj, o, do, heads, col0, tq, tk, name):
    s = proj.shape[0]
    dh = SB_HEAD_DIM
    nq, nd = s // tq, tq // tk
    scale = dh ** -0.5

    def body(q_ref, k_ref, v_ref, o_ref, do_ref, dq_ref, dk_ref, dv_ref, dk_acc, dv_acc):
        i = pl.program_id(1)

        @pl.when(i == 0)
        def _():
            dk_acc[...] = jnp.zeros_like(dk_acc)
            dv_acc[...] = jnp.zeros_like(dv_acc)

        q = q_ref[...]
        dob = do_ref[...].astype(BF16)
        delta = jnp.sum(dob.astype(F32) * o_ref[...], axis=1, keepdims=True)
        masks, tri_right, tri_left = _sb_masks(tq, tk)

        def tile(j, carry_l, carry_g, dq, mask):
            start = pl.multiple_of(j * tk, tk)
            kj = k_ref[pl.ds(start, tk), :]
            vj = v_ref[pl.ds(start, tk), :]
            lsz, l1b, ab = _sb_tile(q, kj, scale, carry_l, tri_right, mask)
            g = _dot(dob, vj, 1, 1) * ab.astype(F32)
            carry_g = carry_g + jnp.sum(g, axis=1, keepdims=True)
            left = (delta - carry_g) + _dot(g.astype(BF16), tri_left, 1, 0)
            beta = jnp.exp(lsz)
            dz = g * (1.0 - beta) - left * beta
            if mask is not None:
                dz = jnp.where(mask, dz, 0.0)
            dzb = (dz * scale).astype(BF16)
            dk_acc[pl.ds(start, tk), :] += _dot(dzb, q, 0, 0)
            dv_acc[pl.ds(start, tk), :] += _dot(ab, dob, 0, 0)
            return carry_l + jnp.sum(l1b.astype(F32), axis=1, keepdims=True), carry_g, dq + _dot(dzb, kj, 1, 0)

        zero = jnp.zeros((tq, 1), F32)
        state = (zero, zero, jnp.zeros((tq, dh), F32))
        for dj in reversed(range(nd)):
            state = tile(i * nd + dj, *state, masks[dj])
        state = lax.fori_loop(0, i * nd, lambda t, st: tile(i * nd - 1 - t, st[0], st[1], st[2], None), state)
        dq_ref[...] = state[2].astype(dq_ref.dtype)

        @pl.when(i == nq - 1)
        def _():
            dk_ref[...] = dk_acc[...].astype(dk_ref.dtype)
            dv_ref[...] = dv_acc[...].astype(dv_ref.dtype)

    qspec = pl.BlockSpec((tq, dh), lambda h, i: (i, col0[0] + h))
    kspec = pl.BlockSpec((s, dh), lambda h, i: (0, col0[1] + h))
    vspec = pl.BlockSpec((s, dh), lambda h, i: (0, col0[2] + h))
    blk = pl.BlockSpec((tq, dh), lambda h, i: (i, h))
    full = pl.BlockSpec((s, dh), lambda h, i: (0, h))
    act = jax.ShapeDtypeStruct((s, heads * dh), BF16)
    return _pcall(
        body, name=name, grid=(heads, nq), in_specs=[qspec, kspec, vspec, blk, blk],
        out_specs=[blk, full, full], out_shape=[act, act, act],
        scratch_shapes=[pltpu.VMEM((s, dh), F32), pltpu.VMEM((s, dh), F32)],
        compiler_params=_params("parallel", "arbitrary"),
    )(proj, proj, proj, o, do)


def _xattn_probs(q, k, scale):
    sc = _dot(q, k, 1, 1) * scale
    e = jnp.exp(sc - jnp.max(sc, axis=1, keepdims=True))
    return e / jnp.sum(e, axis=1, keepdims=True)


def _xattn_fwd(qc, kv, tq, name):
    s, d = qc.shape
    m = kv.shape[0]
    dh = d // X_HEADS
    scale = dh ** -0.5

    def body(q_ref, k_ref, v_ref, o_ref):
        p = _xattn_probs(q_ref[...], k_ref[...], scale)
        o_ref[...] = _dot(p.astype(BF16), v_ref[...], 1, 0).astype(o_ref.dtype)

    blk = pl.BlockSpec((tq, dh), lambda h, i: (i, h))
    return _pcall(
        body, name=name, grid=(X_HEADS, s // tq),
        in_specs=[blk, pl.BlockSpec((m, dh), lambda h, i: (0, h)), pl.BlockSpec((m, dh), lambda h, i: (0, X_HEADS + h))],
        out_specs=blk, out_shape=jax.ShapeDtypeStruct((s, d), BF16), compiler_params=_params("parallel", "parallel"),
    )(qc, kv, kv)


def _xattn_bwd(qc, kv, do, tq, name):
    s, d = qc.shape
    m = kv.shape[0]
    dh = d // X_HEADS
    scale = dh ** -0.5
    nq = s // tq

    def body(q_ref, k_ref, v_ref, do_ref, dq_ref, dk_ref, dv_ref, dk_acc, dv_acc):
        i = pl.program_id(1)
        q, k, v = q_ref[...], k_ref[...], v_ref[...]
        dob = do_ref[...].astype(BF16)
        p = _xattn_probs(q, k, scale)
        pb = p.astype(BF16)
        dp = _dot(dob, v, 1, 1)
        ds = pb.astype(F32) * (dp - jnp.sum(dp * pb.astype(F32), axis=1, keepdims=True))
        dsb = (ds * scale).astype(BF16)
        dq_ref[...] = _dot(dsb, k, 1, 0).astype(dq_ref.dtype)
        dk_part = _dot(dsb, q, 0, 0)
        dv_part = _dot(pb, dob, 0, 0)

        @pl.when(i == 0)
        def _():
            dk_acc[...] = dk_part
            dv_acc[...] = dv_part

        @pl.when(i > 0)
        def _():
            dk_acc[...] += dk_part
            dv_acc[...] += dv_part

        @pl.when(i == nq - 1)
        def _():
            dk_ref[...] = dk_acc[...].astype(dk_ref.dtype)
            dv_ref[...] = dv_acc[...].astype(dv_ref.dtype)

    blk = pl.BlockSpec((tq, dh), lambda h, i: (i, h))
    kblk = pl.BlockSpec((m, dh), lambda h, i: (0, h))
    return _pcall(
        body, name=name, grid=(X_HEADS, nq),
        in_specs=[blk, kblk, pl.BlockSpec((m, dh), lambda h, i: (0, X_HEADS + h)), blk],
        out_specs=[blk, kblk, kblk],
        out_shape=[jax.ShapeDtypeStruct((s, d), BF16), jax.ShapeDtypeStruct((m, d), BF16), jax.ShapeDtypeStruct((m, d), BF16)],
        scratch_shapes=[pltpu.VMEM((m, dh), F32), pltpu.VMEM((m, dh), F32)],
        compiler_params=_params("parallel", "arbitrary"),
    )(qc, kv, kv, do)


def _local_step(x, mem, tgt, w, fetch=None, emit=None, after=None):
    fetch = fetch or (lambda name, after: {})
    emit = emit or (lambda group, g: None)
    w = dict(w)
    s, d = x.shape
    heads = d // SB_HEAD_DIM
    tm = _pick(s, (256, 128))
    tq = _pick(s, (256, 128))
    sb_tq, sb_tk = _pick(s, (512, 256, 128)), _pick(s, (256, 128))
    tc = _pick(d, (256, 128))
    g = {}

    def wt(name, after):
        if name not in w:
            w.update(fetch(name, after))
        return w[name]

    def ffn_fwd(h, gname, wgu, wdown, tag, after=None):
        n = _rms_fwd(h, w[gname], tag + "_norm", tm, after=after)
        gu = _mm(n, wt(wgu, n), name=tag + "_gu")
        act = _swiglu_fwd(gu, tag + "_act", tm)
        return n, gu, act, _mm(act, wt(wdown, act), name=tag + "_down", out_dtype=F32, res=h, alpha=0.5)

    def ffn_bwd(dh, h, saved, gname, wgu, wdown, tag, after=None):
        n, gu, act = saved
        dhb = _rowcall(lambda v: 0.5 * v, [_whole(dh)], [], [(d, BF16)], tm=tm, name=tag + "_half", after=after)[0]
        g[wdown] = _mm(act, dhb, ta=True, name=tag + "_dwdown")
        dact = _mm(dhb, w[wdown], tb=True, name=tag + "_dact")
        dgu = _swiglu_bwd(dact, gu, tag + "_dgu", tm)
        g[wgu] = _mm(n, dgu, ta=True, name=tag + "_dwgu")
        dn = _mm(dgu, w[wgu], tb=True, name=tag + "_dn", out_dtype=F32)
        dh_in, g[gname] = _resid_rms_bwd(dh, dn, h, w[gname], tag + "_dnorm", tm)
        return dh_in

    n1, gu1, act1, h1 = ffn_fwd(x, "g_ffn1", "w_ffn1_gu", "w_ffn1_down", "ffn1", after)
    u = _rms_fwd(h1, w["g_mix"], "mix_norm", tm)
    proj = _mm(u, wt("w_in", u), name="mix_in")
    nd = d // SB_HEAD_DIM
    y_conv = _conv_fwd(proj, w["conv_w"], d, tc, "conv_fwd")
    sb_cols = (3 * nd, 4 * nd, 5 * nd)
    y_sb = _sb_fwd(proj, heads, sb_cols, sb_tq, sb_tk, "sb_fwd")
    a_conv = _mm(y_conv, wt("w_conv_out", y_conv), name="conv_out")
    a_sb = _mm(y_sb, wt("w_attn_out", y_sb), name="attn_out")
    b_conv, b_sb = w["b_gate"][:, :d], w["b_gate"][:, d:]

    def merge(ac, asb, gcp, gsp, bc, bs):
        gc = _sigmoid(gcp.astype(F32) + bc)
        gs = _sigmoid(gsp.astype(F32) + bs)
        return gc * ac.astype(F32) + gs * asb.astype(F32)

    merged = _rowcall(merge, [_whole(a_conv), _whole(a_sb), (proj, 6, d), (proj, 7, d)], [b_conv, b_sb], [(d, BF16)],
                      tm=tm, name="merge")[0]
    h2 = _mm(merged, wt("w_o", merged), name="mix_out", out_dtype=F32, res=h1)
    hn = _rms_fwd(h2, w["g_cross"], "cross_norm", tm)
    mn = _rms_fwd(mem, w["g_mem"], "mem_norm", _pick(mem.shape[0], (256, 128)))
    qc = _mm(hn, wt("w_cq", hn), name="cross_q")
    kv = _mm(mn, wt("w_ckv", mn), name="cross_kv")
    oc = _xattn_fwd(qc, kv, tq, "xattn_fwd")
    h3 = _mm(oc, wt("w_co", oc), name="cross_out", out_dtype=F32, res=h2)
    n2, gu2, act2, h4 = ffn_fwd(h3, "g_ffn2", "w_ffn2_gu", "w_ffn2_down", "ffn2")

    def head(hb, tb, gb):
        xh, r = _xhat(hb)
        err = xh * gb - tb
        dy = err * (1.0 / d)
        dxh = dy * gb
        dx = r * (dxh - xh * jnp.mean(dxh * xh, axis=-1, keepdims=True))
        row_loss = 0.5 * jnp.mean(err * err, axis=-1, keepdims=True)
        return dx, dy * xh, jnp.broadcast_to(row_loss, (row_loss.shape[0], LANES))

    dh4, g["g_final"], loss_lanes = _rowcall(head, [_whole(h4), _whole(tgt)], [w["g_final"]], [(d, F32)], [d, LANES],
                                             tm=tm, name="loss_head")

    dh3 = ffn_bwd(dh4, h3, (n2, gu2, act2), "g_ffn2", "w_ffn2_gu", "w_ffn2_down", "ffn2")
    tok = emit("ffn2", g)
    dh3b = _rowcall(lambda v: v, [_whole(dh3)], [], [(d, BF16)], tm=tm, name="cross_cast", after=tok)[0]
    g["w_co"] = _mm(oc, dh3b, ta=True, name="cross_dwco")
    doc = _mm(dh3b, w["w_co"], tb=True, name="cross_doc")
    dqc, dk, dv = _xattn_bwd(qc, kv, doc, tq, "xattn_bwd")
    dkv = jnp.concatenate([dk, dv], axis=1)
    g["w_cq"] = _mm(hn, dqc, ta=True, name="cross_dwcq")
    g["w_ckv"] = _mm(mn, dkv, ta=True, name="cross_dwckv")
    tok = emit("cross", g)
    dhn = _mm(dqc, w["w_cq"], tb=True, name="cross_dhn", out_dtype=F32, after=tok)
    dmn = _mm(dkv, w["w_ckv"], tb=True, name="cross_dmn", out_dtype=F32)
    g["g_mem"] = _rowcall(lambda dy, xb: dy * _xhat(xb)[0], [_whole(dmn), _whole(mem)], [], [], [d],
                          tm=_pick(mem.shape[0], (256, 128)), name="mem_dnorm")[0]
    dh2, g["g_cross"] = _resid_rms_bwd(dh3, dhn, h2, w["g_cross"], "cross_dnorm", tm)

    dh2b = _rowcall(lambda v: v, [_whole(dh2)], [], [(d, BF16)], tm=tm, name="mix_cast")[0]
    g["w_o"] = _mm(merged, dh2b, ta=True, name="mix_dwo")
    dmerged = _mm(dh2b, w["w_o"], tb=True, name="mix_dmerged")

    def merge_bwd(dm, ac, asb, gcp, gsp, bc, bs):
        dm, ac, asb = dm.astype(F32), ac.astype(F32), asb.astype(F32)
        gc = _sigmoid(gcp.astype(F32) + bc)
        gs = _sigmoid(gsp.astype(F32) + bs)
        dgc = dm * ac * gc * (1.0 - gc)
        dgs = dm * asb * gs * (1.0 - gs)
        return dm * gc, dm * gs, dgc, dgs, dgc, dgs

    da_conv, da_sb, dgc, dgs, db_conv, db_sb = _rowcall(
        merge_bwd, [_whole(dmerged), _whole(a_conv), _whole(a_sb), (proj, 6, d), (proj, 7, d)], [b_conv, b_sb],
        [(d, BF16)] * 4, [d, d], tm=tm, name="merge_bwd")
    g["b_gate"] = jnp.concatenate([db_conv, db_sb], axis=1)
    g["w_conv_out"] = _mm(y_conv, da_conv, ta=True, name="conv_dwout")
    g["w_attn_out"] = _mm(y_sb, da_sb, ta=True, name="attn_dwout")
    dy_conv = _mm(da_conv, w["w_conv_out"], tb=True, name="conv_dy")
    dy_sb = _mm(da_sb, w["w_attn_out"], tb=True, name="attn_dy")
    dcb, dcc, dcx, g["conv_w"] = _conv_bwd(dy_conv, proj, w["conv_w"], d, tc, "conv_bwd")
    dq, dk_sb, dv_sb = _sb_bwd(proj, y_sb, dy_sb, heads, sb_cols, sb_tq, sb_tk, "sb_bwd")
    dproj = jnp.concatenate([dcb, dcc, dcx, dq, dk_sb, dv_sb, dgc, dgs], axis=1)
    g["w_in"] = _mm(u, dproj, ta=True, name="mix_dwin")
    tok = emit("mix", g)
    du = _mm(dproj, w["w_in"], tb=True, name="mix_du", out_dtype=F32, after=tok)
    dh1, g["g_mix"] = _resid_rms_bwd(dh2, du, h1, w["g_mix"], "mix_dnorm", tm)
    dx = ffn_bwd(dh1, x, (n1, gu1, act1), "g_ffn1", "w_ffn1_gu", "w_ffn1_down", "ffn1")
    return loss_lanes, dx, g


MATS = (("w_ffn1_gu", "col"), ("w_ffn1_down", "row"), ("w_in", "col"), ("w_conv_out", "row"), ("w_attn_out", "row"),
        ("w_o", "row"), ("w_cq", "row"), ("w_ckv", "col"), ("w_co", "row"), ("w_ffn2_gu", "col"), ("w_ffn2_down", "row"))
VECS = ("g_ffn1", "g_mix", "g_cross", "g_mem", "g_ffn2", "g_final")
WEIGHTS = ("g_ffn1", "w_ffn1_gu", "w_ffn1_down", "g_mix", "w_in", "b_gate", "conv_w", "w_conv_out", "w_attn_out", "w_o",
           "g_cross", "g_mem", "w_cq", "w_ckv", "w_co", "g_ffn2", "w_ffn2_gu", "w_ffn2_down", "g_final")
CONV_ROWS = 8


def _full_shape(kind, r, c):
    return (r, N_CHIPS * c) if kind == "col" else (N_CHIPS * r, c)


def _piece(ref, kind, r, c, chip, half):
    hr = r // 2
    if kind == "col":
        return ref.at[pl.ds(pl.multiple_of(half * hr, 16), hr), pl.ds(pl.multiple_of(chip * c, LANES), c)]
    return ref.at[pl.ds(pl.multiple_of(chip * r + half * hr, 16), hr), :]


def _shard_of(ref, kind, r, c, chip):
    if kind == "col":
        return ref.at[:, pl.ds(pl.multiple_of(chip * c, LANES), c)]
    return ref.at[pl.ds(pl.multiple_of(chip * r, 16), r), :]


def _place():
    x, y, c = lax.axis_index("x"), lax.axis_index("y"), lax.axis_index("c")
    others = [(1 - x, y), (x, 1 - y), (1 - x, 1 - y)]
    return x, y, c, 2 * x + y, others


def _remote(src, dst, send_sem, recv_sem, to):
    return pltpu.make_async_remote_copy(src_ref=src, dst_ref=dst, send_sem=send_sem, recv_sem=recv_sem,
                                        device_id=to, device_id_type=MESH)


def _gather_conv(conv_shard):
    cc = conv_shard.shape[1]

    def body(conv_ref, conv_full, cs, cr, cl):
        x, y, c, me, others = _place()

        def cols(chip):
            return conv_full.at[:, pl.ds(pl.multiple_of(chip * cc, LANES), cc)]

        def conv(k, chip_from, to):
            return _remote(conv_ref, cols(chip_from), cs.at[k], cr.at[k], to)

        mine = pltpu.make_async_copy(conv_ref, cols(me), cl.at[0])
        mine.start()
        for k, (ox, oy) in enumerate(others):
            conv(k, me, (ox, oy, c)).start()
        for k, (ox, oy) in enumerate(others):
            conv(k, 2 * ox + oy, (x, y, c)).wait_recv()
            conv(k, me, (ox, oy, c)).wait_send()
        mine.wait()

    dma = pltpu.SemaphoreType.DMA
    return _pcall(
        body, name="gather_conv", in_specs=[ANY], out_specs=ANY,
        out_shape=jax.ShapeDtypeStruct((CONV_ROWS, N_CHIPS * cc), F32), scratch_shapes=[dma((3,)), dma((3,)), dma((1,))],
    )(conv_shard)


def _allreduce_small(packed):
    rows, n = packed.shape

    def body(in_ref, out_ref, gath_ref, send_sems, recv_sems):
        x, y, c = lax.axis_index("x"), lax.axis_index("y"), lax.axis_index("c")
        me = 4 * x + 2 * y + c

        def peer(rel):
            return (x ^ (rel >> 2 & 1), y ^ (rel >> 1 & 1), c ^ (rel & 1))

        def copy(rel, slot, to):
            return _remote(in_ref, gath_ref.at[slot], send_sems.at[rel - 1], recv_sems.at[rel - 1], to)

        for rel in range(1, N_DEV):
            copy(rel, me, peer(rel)).start()
        gath_ref[me] = in_ref[...]
        for rel in range(1, N_DEV):
            px, py, pc = peer(rel)
            copy(rel, 4 * px + 2 * py + pc, (x, y, c)).wait_recv()
        for rel in range(1, N_DEV):
            copy(rel, me, peer(rel)).wait_send()
        tot = gath_ref[0]
        for dev in range(1, N_DEV):
            tot = tot + gath_ref[dev]
        out_ref[...] = tot

    vm = pl.BlockSpec(memory_space=pltpu.VMEM)
    return _pcall(
        body, name="allreduce_small", in_specs=[vm], out_specs=[vm, vm],
        out_shape=[jax.ShapeDtypeStruct((rows, n), F32), jax.ShapeDtypeStruct((N_DEV, rows, n), F32)],
        scratch_shapes=[pltpu.SemaphoreType.DMA((N_DEV - 1,)), pltpu.SemaphoreType.DMA((N_DEV - 1,))],
    )(packed)[0]


def _rs_cores(grads, dims, name):
    nw = len(grads)

    def body(*refs):
        g_refs, got_refs = refs[:nw], refs[nw:2 * nw]
        ss, rs = refs[2 * nw:]
        x, y, c, _, _ = _place()
        sib = (x, y, 1 - c)

        def give(wi, chip):
            return _remote(_piece(g_refs[wi], *dims[wi], chip, 1 - c), got_refs[wi].at[chip], ss.at[wi, chip], rs.at[wi, chip], sib)

        every = [(wi, chip) for wi in range(nw) for chip in range(N_CHIPS)]
        for wi, chip in every:
            give(wi, chip).start()
        for wi, chip in every:
            give(wi, chip).wait()

    dma = pltpu.SemaphoreType.DMA
    return _pcall(
        body, name=name, in_specs=[ANY] * nw, out_specs=[ANY] * nw,
        out_shape=[jax.ShapeDtypeStruct((N_CHIPS, r // 2, cw), BF16) for (_, r, cw) in dims],
        scratch_shapes=[dma((nw, N_CHIPS)), dma((nw, N_CHIPS))],
    )(*grads)


def _share_halves(bufs, name):
    nw = len(bufs)

    def body(*refs):
        out_refs = refs[nw:2 * nw]
        ss, rs = refs[2 * nw:]
        x, y, c, _, _ = _place()
        sib = (x, y, 1 - c)

        def send(wi, half):
            return _remote(out_refs[wi].at[half], out_refs[wi].at[half], ss.at[wi], rs.at[wi], sib)

        for wi in range(nw):
            send(wi, c).start()
        for wi in range(nw):
            send(wi, 1 - c).wait_recv()
        for wi in range(nw):
            send(wi, c).wait_send()

    dma = pltpu.SemaphoreType.DMA
    return _pcall(
        body, name=name, in_specs=[ANY] * nw, out_specs=[ANY] * nw,
        out_shape=[jax.ShapeDtypeStruct(b.shape, b.dtype) for b in bufs],
        input_output_aliases={i: i for i in range(nw)}, scratch_shapes=[dma((nw,)), dma((nw,))],
    )(*bufs)


HBM = pl.BlockSpec(memory_space=pltpu.HBM)
SEM = pl.BlockSpec(memory_space=pltpu.SEMAPHORE)
EFFECT = pltpu.SideEffectType.DATAFLOW_SIDE_EFFECTING
TOKEN = (8, LANES)


def _split_start(name, plan, n_copies, srcs, lands, after=None):
    ns, nl = len(srcs), len(lands)
    n_in = ns + nl + (after is not None)

    def body(*refs):
        outs = refs[n_in:]
        sends, _ = plan(refs[:ns], refs[ns:ns + nl], outs[0], outs[1])
        for cp in sends:
            cp.start()
        outs[-1][...] = jnp.zeros(TOKEN, F32)

    held = [pltpu.HBM(a.shape, a.dtype) for a in (*srcs, *lands)]
    dma = pltpu.SemaphoreType.DMA((n_copies,))
    ins = [pltpu.with_memory_space_constraint(a, pltpu.HBM) for a in (*srcs, *lands)]
    outs = _pcall(
        body, name=name, in_specs=[HBM] * (ns + nl) + ([] if after is None else [ANY]),
        out_specs=(SEM, SEM, *[HBM] * (ns + nl), pl.BlockSpec(memory_space=pltpu.VMEM)),
        out_shape=(dma, dma, *held, jax.ShapeDtypeStruct(TOKEN, F32)),
        input_output_aliases={i: 2 + i for i in range(ns + nl)},
        compiler_params=pltpu.CompilerParams(has_side_effects=EFFECT),
    )(*ins, *([] if after is None else [after]))
    return outs[0], outs[1], list(outs[2:2 + ns]), list(outs[2 + ns:2 + ns + nl]), outs[-1]


def _split_wait(name, plan, send_sems, recv_sems, srcs, lands, after):
    ns, nl = len(srcs), len(lands)

    def body(*refs):
        sends, recvs = plan(refs[:ns], refs[ns:ns + nl], refs[ns + nl], refs[ns + nl + 1])
        for cp in sends:
            cp.wait_send()
        for cp in recvs:
            cp.wait_recv()

    outs = _pcall(
        body, name=name, in_specs=[HBM] * (ns + nl) + [SEM, SEM, ANY], out_specs=[HBM] * (ns + nl),
        out_shape=[pltpu.HBM(a.shape, a.dtype) for a in (*srcs, *lands)],
        input_output_aliases={i: i for i in range(ns + nl)},
        compiler_params=pltpu.CompilerParams(has_side_effects=EFFECT),
    )(*srcs, *lands, send_sems, recv_sems, after)
    return list(outs[:ns]), list(outs[ns:])


def _gather_plan(dims):
    def plan(shard_refs, full_refs, ss, rs):
        x, y, c, me, others = _place()
        sends, recvs = [], []
        for wi, (kind, r, cw) in enumerate(dims):
            half = shard_refs[wi].at[pl.ds(pl.multiple_of(c * (r // 2), 16), r // 2), :]
            for k, (ox, oy) in enumerate(others):
                sem = 4 * wi + k
                sends.append(_remote(half, _piece(full_refs[wi], kind, r, cw, me, c), ss.at[sem], rs.at[sem], (ox, oy, c)))
                recvs.append(_remote(half, _piece(full_refs[wi], kind, r, cw, 2 * ox + oy, c), ss.at[sem], rs.at[sem], (x, y, c)))
            sem = 4 * wi + 3
            own = _remote(shard_refs[wi], _shard_of(full_refs[wi], kind, r, cw, me), ss.at[sem], rs.at[sem], (x, y, 1 - c))
            sends.append(own)
            recvs.append(own)
        return sends, recvs

    return plan


def _gather_forward(fulls, dims, name):
    nw = len(fulls)

    def body(*refs):
        full_refs = refs[nw:2 * nw]
        ss, rs = refs[2 * nw:]
        x, y, c, _, others = _place()

        def pass_on(wi, k, half):
            ox, oy = others[k]
            pc = _piece(full_refs[wi], *dims[wi], 2 * ox + oy, half)
            return _remote(pc, pc, ss.at[wi, k], rs.at[wi, k], (x, y, 1 - c))

        every = [(wi, k) for wi in range(nw) for k in range(3)]
        for wi, k in every:
            pass_on(wi, k, c).start()
        for wi, k in every:
            pass_on(wi, k, 1 - c).wait_recv()
        for wi, k in every:
            pass_on(wi, k, c).wait_send()

    dma = pltpu.SemaphoreType.DMA
    return _pcall(
        body, name=name, in_specs=[ANY] * nw, out_specs=[ANY] * nw,
        out_shape=[jax.ShapeDtypeStruct(f.shape, f.dtype) for f in fulls],
        input_output_aliases={i: i for i in range(nw)}, scratch_shapes=[dma((nw, 3)), dma((nw, 3))],
    )(*fulls)


def _rs_chips_plan(nw):
    def plan(p_refs, land_refs, ss, rs):
        x, y, c, me, others = _place()
        sends, recvs = [], []
        for wi in range(nw):
            for k, (ox, oy) in enumerate(others):
                sem = 3 * wi + k
                sends.append(_remote(p_refs[wi].at[2 * ox + oy], land_refs[wi].at[k], ss.at[sem], rs.at[sem], (ox, oy, c)))
                recvs.append(_remote(p_refs[wi].at[me], land_refs[wi].at[k], ss.at[sem], rs.at[sem], (x, y, c)))
        return sends, recvs

    return plan


def _rows_per_block(n, c, limit_bytes=1 << 20):
    best = None
    for tm in range(16, n + 1, 16):
        if n % tm == 0 and tm * c * 4 <= limit_bytes:
            best = tm
    return best or n


def _sum_cores(grad, got, kind, place, name):
    _, hr, cw = got.shape
    tm = _rows_per_block(hr, cw)
    nb = hr // tm

    def body(place_ref, g_ref, t_ref, o_ref):
        o_ref[...] = (g_ref[...].astype(F32) + t_ref[...].astype(F32)).astype(o_ref.dtype)

    if kind == "col":
        g_spec = pl.BlockSpec((tm, cw), lambda j, i, pr: (pr[0] * nb + i, j))
    else:
        g_spec = pl.BlockSpec((tm, cw), lambda j, i, pr: ((2 * j + pr[0]) * nb + i, 0))
    blk = pl.BlockSpec((None, tm, cw), lambda j, i, pr: (j, i, 0))
    return _pcall(
        body, name=name, out_shape=jax.ShapeDtypeStruct(got.shape, BF16),
        grid_spec=pltpu.PrefetchScalarGridSpec(num_scalar_prefetch=1, grid=(N_CHIPS, nb), in_specs=[g_spec, blk], out_specs=blk),
        compiler_params=_params("parallel", "parallel"),
    )(place, grad, got)


def _sum_chips(parts, got, place, name):
    _, n, cw = got.shape
    tm = _rows_per_block(n, cw)

    def body(place_ref, p_ref, g_ref, o_ref):
        tot = p_ref[...].astype(F32)
        for k in range(3):
            tot = tot + g_ref[k].astype(F32)
        o_ref[...] = tot

    return _pcall(
        body, name=name, out_shape=jax.ShapeDtypeStruct((2, n, cw), F32),
        grid_spec=pltpu.PrefetchScalarGridSpec(
            num_scalar_prefetch=1, grid=(n // tm,),
            in_specs=[pl.BlockSpec((None, tm, cw), lambda i, pr: (pr[1], i, 0)), pl.BlockSpec((3, tm, cw), lambda i, pr: (0, i, 0))],
            out_specs=pl.BlockSpec((None, tm, cw), lambda i, pr: (pr[0], i, 0))),
        compiler_params=_params("parallel"),
    )(place, parts, got)


def _adamw(g, w, m, v, name):
    n, c = g.shape
    c1 = 1.0 - ADAM_B1 ** ADAM_STEP
    c2 = 1.0 - ADAM_B2 ** ADAM_STEP

    def fn(gb, wb, mb, vb):
        m_new = ADAM_B1 * mb + (1.0 - ADAM_B1) * gb
        v_new = ADAM_B2 * vb + (1.0 - ADAM_B2) * (gb * gb)
        delta = -ADAM_LR * ((m_new / c1) / (jnp.sqrt(v_new / c2) + ADAM_EPS) + ADAM_WD * wb)
        return gb, delta, m_new, v_new

    tm = _rows_per_block(n, c) if n % 16 == 0 else n
    return _rowcall(fn, [_whole(g), _whole(w), _whole(m), _whole(v)], [], [(c, F32)] * 4, tm=tm, name=name)


PACK_ROWS = 16


def _pack_rows(parts, width, name, after=None):
    assert sum(p.shape[0] for p in parts) <= PACK_ROWS

    def body(*refs):
        out_ref = refs[-1]
        out_ref[...] = jnp.zeros_like(out_ref)
        at = 0
        for r in refs[:len(parts)]:
            k, n = r.shape
            if n == width:
                out_ref[at:at + k, :] = r[...]
            else:
                out_ref[at:at + k, :] = jnp.broadcast_to(r[:, :1], (k, width))
            at += k

    vm = pl.BlockSpec(memory_space=pltpu.VMEM)
    return _pcall(body, name=name, in_specs=[vm] * len(parts) + ([] if after is None else [ANY]), out_specs=vm,
                  out_shape=jax.ShapeDtypeStruct((PACK_ROWS, width), F32))(*parts, *([] if after is None else [after]))


def _cast_shard(wm, name, after):
    n, c = wm.shape
    return _rowcall(lambda v: v, [_whole(wm)], [], [(c, BF16)], tm=_rows_per_block(n, c), name=name, after=after)[0]


GATHER_GROUPS = (
    ("w_ffn1_gu",), ("w_ffn1_down",), ("w_in",), ("w_conv_out", "w_attn_out", "w_o"), ("w_cq", "w_ckv", "w_co"),
    ("w_ffn2_gu", "w_ffn2_down"),
)
REDUCE_GROUPS = {
    "ffn2": ("w_ffn2_down", "w_ffn2_gu"),
    "cross": ("w_co", "w_cq", "w_ckv"),
    "mix": ("w_o", "w_conv_out", "w_attn_out", "w_in"),
    "ffn1": ("w_ffn1_down", "w_ffn1_gu"),
}
KIND = dict(MATS)


def _step(x, mem, tgt, wts, m_in, v_in):
    d = x.shape[-1]
    cc = wts["conv_w"].shape[1]
    place = jnp.stack([lax.axis_index("c"), 2 * lax.axis_index("x") + lax.axis_index("y")]).astype(jnp.int32)
    dims = {n: (kind, *wts[n].shape) for n, kind in MATS}

    conv_pad = jnp.pad(wts["conv_w"], ((0, CONV_ROWS - CONV_K), (0, 0)))
    conv_full = _gather_conv(conv_pad)
    w = {"conv_w": conv_full[:CONV_K]}
    for n in VECS + ("b_gate",):
        w[n] = wts[n].reshape(1, -1)
    flying, token = {}, conv_full
    for names in GATHER_GROUPS:
        gd = [dims[n] for n in names]
        shards = [_cast_shard(wts[n], "cast_" + n, token) for n in names]
        lands = [lax.empty(_full_shape(*dm), BF16) for dm in gd]
        plan = _gather_plan(gd)
        ss, rs, srcs, lands, token = _split_start("gather_start_" + names[0], plan, 4 * len(names), shards, lands, token)
        flying.update({n: (names, plan, ss, rs, srcs, lands, gd) for n in names})

    def fetch(name, after):
        names, plan, ss, rs, srcs, lands, gd = flying[name]
        _, lands = _split_wait("gather_wait_" + names[0], plan, ss, rs, srcs, lands, after)
        return dict(zip(names, _gather_forward(lands, gd, "gather_forward_" + names[0])))

    sent = {}

    def emit(grp, g):
        names = REDUCE_GROUPS[grp]
        gd = [dims[n] for n in names]
        got = _rs_cores([g[n] for n in names], gd, "rs_cores_" + grp)
        parts = [_sum_cores(g[n], t, KIND[n], place, "sum_cores_" + n) for n, t in zip(names, got)]
        lands = [lax.empty((3, *p.shape[1:]), BF16) for p in parts]
        plan = _rs_chips_plan(len(names))
        ss, rs, srcs, lands, tok = _split_start("rs_chips_start_" + grp, plan, 3 * len(names), parts, lands)
        sent[grp] = (plan, ss, rs, srcs, lands)
        return tok

    loss_lanes, dx, g = _local_step(x[0], mem[0], tgt[0], w, fetch, emit, token)
    last = emit("ffn1", g)

    rows = [g[n] for n in VECS] + [g["b_gate"][:, :d], g["b_gate"][:, d:], g["conv_w"], loss_lanes]
    red = _allreduce_small(_pack_rows(rows, d, "pack_small", after=last))
    grads = {n: red[i:i + 1] for i, n in enumerate(VECS)}
    nv = len(VECS)
    grads["b_gate"] = jnp.concatenate([red[nv:nv + 1], red[nv + 1:nv + 2]], axis=1)
    me = 2 * lax.axis_index("x") + lax.axis_index("y")
    grads["conv_w"] = lax.dynamic_slice_in_dim(red[nv + 2:nv + 2 + CONV_K], me * cc, cc, axis=1)
    loss = red[nv + 2 + CONV_K, 0]

    def update(n):
        shape = wts[n].shape
        as2d = (lambda a: a.reshape(1, -1)) if len(shape) == 1 else (lambda a: a)
        return [r.reshape(shape) for r in _adamw(grads[n], as2d(wts[n]), as2d(m_in[n]), as2d(v_in[n]), "adamw_" + n)]

    out = {n: update(n) for n in WEIGHTS if n not in KIND}

    after = red
    for grp, names in REDUCE_GROUPS.items():
        plan, ss, rs, srcs, lands = sent[grp]
        parts, landed = _split_wait("rs_chips_wait_" + grp, plan, ss, rs, srcs, lands, after)
        halves = [_sum_chips(p, t, place, "sum_chips_" + n) for n, p, t in zip(names, parts, landed)]
        both = _share_halves(halves, "share_halves_" + grp)
        for n, b in zip(names, both):
            grads[n] = b.reshape(-1, b.shape[-1])
            out[n] = update(n)
        after = out[names[-1]][1]
    return (loss, dx[None], *[out[n][0] for n in WEIGHTS], *[out[n][1] for n in WEIGHTS],
            *[out[n][2] for n in WEIGHTS], *[out[n][3] for n in WEIGHTS])


def kernel(x, mem, g_ffn1, w_ffn1_gu, w_ffn1_down, g_mix, w_in, b_gate, conv_w, w_conv_out, w_attn_out, w_o, g_cross, g_mem, w_cq, w_ckv, w_co, g_ffn2, w_ffn2_gu, w_ffn2_down, g_final, loss_target, m_g_ffn1, m_w_ffn1_gu, m_w_ffn1_down, m_g_mix, m_w_in, m_b_gate, m_conv_w, m_w_conv_out, m_w_attn_out, m_w_o, m_g_cross, m_g_mem, m_w_cq, m_w_ckv, m_w_co, m_g_ffn2, m_w_ffn2_gu, m_w_ffn2_down, m_g_final, v_g_ffn1, v_w_ffn1_gu, v_w_ffn1_down, v_g_mix, v_w_in, v_b_gate, v_conv_w, v_w_conv_out, v_w_attn_out, v_w_o, v_g_cross, v_g_mem, v_w_cq, v_w_ckv, v_w_co, v_g_ffn2, v_w_ffn2_gu, v_w_ffn2_down, v_g_final):
    given = dict(locals())
    wts = {n: given[n] for n in WEIGHTS}
    m_in = {n: given["m_" + n] for n in WEIGHTS}
    v_in = {n: given["v_" + n] for n in WEIGHTS}
    return _step(x, mem, loss_target, wts, m_in, v_in)
```

```python
import functools

import jax
import jax.numpy as jnp
from jax import lax
from jax.experimental import pallas as pl
from jax.experimental.pallas import tpu as pltpu

F32 = jnp.float32
BF16 = jnp.bfloat16
MESH = pl.DeviceIdType.MESH

V7X_VMEM_LIMIT_BYTES = 48 * 1024 * 1024
MM_VMEM_BUDGET_BYTES = 36 * 1024 * 1024
MM_WHOLE_K = 2816
LANES = 128
SB_HEAD_DIM = 128
X_HEADS = 4
CONV_K = 3
RMS_EPS = 1e-6
N_CHIPS = 4
N_DEV = 8
ADAM_LR, ADAM_B1, ADAM_B2, ADAM_EPS, ADAM_WD, ADAM_STEP = 0.001, 0.9, 0.999, 1e-08, 0.01, 10


ANY = pl.BlockSpec(memory_space=pl.ANY)


def _pcall(body, **kw):
    return pl.pallas_call(body, **kw)


def _params(*sem):
    return pltpu.CompilerParams(dimension_semantics=sem, vmem_limit_bytes=V7X_VMEM_LIMIT_BYTES)


def _pick(dim, cands):
    for c in cands:
        if dim % c == 0:
            return c
    return dim


def _dot(a, b, ca, cb):
    return lax.dot_general(a, b, (((ca,), (cb,)), ((), ())), preferred_element_type=F32)


def _mm(a, b, *, name, ta=False, tb=False, out_dtype=BF16, res=None, alpha=1.0, tm=None, tn=None, tk=None, after=None):
    m, k = (a.shape[1], a.shape[0]) if ta else a.shape
    n = b.shape[0] if tb else b.shape[1]
    assert k == (b.shape[1] if tb else b.shape[0]), (a.shape, b.shape, ta, tb)
    if ta:
        tm = tm or _pick(m, (512, 256, 128))
        tn = tn or _pick(n, (1024, 512, 256, 128))
        tk = tk or (k if k <= MM_WHOLE_K else _pick(k, (1024, 512, 256, 128)))
    else:
        tk = tk or (k if k <= MM_WHOLE_K else _pick(k, (MM_WHOLE_K, 2048, 1024, 512, 256, 128)))
        tn = tn or _pick(n, (512, 1408, 256, 128))
        per_row = 2 * (tk * a.dtype.itemsize + tn * (jnp.dtype(out_dtype).itemsize + (0 if res is None else res.dtype.itemsize)))
        per_row += 4 * tn if tk < k else 0
        rows = (MM_VMEM_BUDGET_BYTES - 2 * tk * tn * b.dtype.itemsize) // per_row
        tm = tm or next((c for c in (2048, 1024, 512, 256, 128) if m % c == 0 and c <= rows), m)
    nk = k // tk
    assert m % tm == 0 and n % tn == 0 and k % tk == 0
    a_spec = pl.BlockSpec((tk, tm), lambda i, j, kk: (kk, i)) if ta else pl.BlockSpec((tm, tk), lambda i, j, kk: (i, kk))
    b_spec = pl.BlockSpec((tn, tk), lambda i, j, kk: (j, kk)) if tb else pl.BlockSpec((tk, tn), lambda i, j, kk: (kk, j))
    o_spec = pl.BlockSpec((tm, tn), lambda i, j, kk: (i, j))
    ca, cb = (0 if ta else 1), (1 if tb else 0)

    n_in = 2 + (res is not None) + (after is not None)

    def body(*refs):
        a_ref, b_ref = refs[:2]
        res_ref = refs[2] if res is not None else None
        o_ref = refs[n_in]
        scratch = refs[n_in + 1:]

        def finish(acc):
            val = acc if alpha == 1.0 else alpha * acc
            if res_ref is not None:
                val = res_ref[...].astype(F32) + val
            o_ref[...] = val.astype(o_ref.dtype)

        part = _dot(a_ref[...].astype(BF16), b_ref[...].astype(BF16), ca, cb)
        if nk == 1:
            finish(part)
        else:
            acc_ref = scratch[0]
            kk = pl.program_id(2)

            @pl.when(kk == 0)
            def _():
                acc_ref[...] = part

            @pl.when(kk > 0)
            def _():
                acc_ref[...] += part

            @pl.when(kk == nk - 1)
            def _():
                finish(acc_ref[...])

    ins = [a, b] + ([] if res is None else [res]) + ([] if after is None else [after])
    in_specs = [a_spec, b_spec] + ([] if res is None else [o_spec]) + ([] if after is None else [ANY])
    return _pcall(
        body, name=name, grid=(m // tm, n // tn, nk), in_specs=in_specs, out_specs=o_spec,
        out_shape=jax.ShapeDtypeStruct((m, n), out_dtype),
        scratch_shapes=[pltpu.VMEM((tm, tn), F32)] if nk > 1 else [],
        compiler_params=_params("parallel", "parallel", "arbitrary"),
    )(*ins)


def _rowcall(fn, rows, consts, outs, accs=(), *, tm, name, after=None):
    s = rows[0][0].shape[0]
    assert s % tm == 0
    n_read, n_out = len(rows) + len(consts), len(outs)
    n_in = n_read + (after is not None)

    def body(*refs):
        vals = fn(*[r[...] for r in refs[:n_read]])
        vals = vals if isinstance(vals, (tuple, list)) else (vals,)
        for o_ref, v in zip(refs[n_in:n_in + n_out], vals[:n_out]):
            o_ref[...] = v.astype(o_ref.dtype)
        if accs:
            first = pl.program_id(0) == 0
            for a_ref, v in zip(refs[n_in + n_out:], vals[n_out:]):
                tot = jnp.sum(v.astype(F32), axis=0, keepdims=True)

                @pl.when(first)
                def _(a_ref=a_ref, tot=tot):
                    a_ref[...] = tot

                @pl.when(jnp.logical_not(first))
                def _(a_ref=a_ref, tot=tot):
                    a_ref[...] += tot

    in_specs = [pl.BlockSpec((tm, w), lambda i, cb=cb: (i, cb)) for (_, cb, w) in rows]
    in_specs += [pl.BlockSpec(c.shape, lambda i: (0, 0)) for c in consts]
    in_specs += [] if after is None else [ANY]
    out_specs = [pl.BlockSpec((tm, w), lambda i: (i, 0)) for (w, _) in outs]
    out_specs += [pl.BlockSpec((1, w), lambda i: (0, 0)) for w in accs]
    out_shape = [jax.ShapeDtypeStruct((s, w), dt) for (w, dt) in outs]
    out_shape += [jax.ShapeDtypeStruct((1, w), F32) for w in accs]
    return _pcall(
        body, name=name, grid=(s // tm,), in_specs=in_specs, out_specs=out_specs, out_shape=out_shape,
        compiler_params=_params("arbitrary" if accs else "parallel"),
    )(*[r[0] for r in rows], *consts, *([] if after is None else [after]))


def _whole(a):
    return (a, 0, a.shape[1])


def _xhat(x):
    x = x.astype(F32)
    r = lax.rsqrt(jnp.mean(x * x, axis=-1, keepdims=True) + RMS_EPS)
    return x * r, r


def _rms_bwd(dy, x, g):
    xh, r = _xhat(x)
    dxh = dy.astype(F32) * g
    dx = r * (dxh - xh * jnp.mean(dxh * xh, axis=-1, keepdims=True))
    return dx, dy.astype(F32) * xh


def _sigmoid(x):
    return 1.0 / (1.0 + jnp.exp(-x))


def _rms_fwd(x, g, name, tm, after=None):
    d = x.shape[1]
    return _rowcall(lambda xb, gb: _xhat(xb)[0] * gb, [_whole(x)], [g], [(d, BF16)], tm=tm, name=name, after=after)[0]


def _swiglu_fwd(gu, name, tm):
    f = gu.shape[1] // 2

    def fn(gate, up):
        gate, up = gate.astype(F32), up.astype(F32)
        return gate * _sigmoid(gate) * up

    return _rowcall(fn, [(gu, 0, f), (gu, 1, f)], [], [(f, BF16)], tm=tm, name=name)[0]


def _swiglu_bwd(dact, gu, name, tm):
    f = gu.shape[1] // 2

    def fn(da, gate, up):
        da, gate, up = da.astype(F32), gate.astype(F32), up.astype(F32)
        sg = _sigmoid(gate)
        silu = gate * sg
        dgate = da * up * (sg + silu * (1.0 - sg))
        return jnp.concatenate([dgate, da * silu], axis=1)

    return _rowcall(fn, [_whole(dact), (gu, 0, f), (gu, 1, f)], [], [(2 * f, BF16)], tm=tm, name=name)[0]


def _resid_rms_bwd(dh, dn, x, g, name, tm, after=None):
    d = x.shape[1]

    def fn(dhb, dnb, xb, gb):
        dx, dg = _rms_bwd(dnb, xb, gb)
        return dhb.astype(F32) + dx, dg

    return _rowcall(fn, [_whole(dh), _whole(dn), _whole(x)], [g], [(d, F32)], [d], tm=tm, name=name, after=after)


def _shift_down(p, k):
    if k == 0:
        return p
    rows = lax.broadcasted_iota(jnp.int32, p.shape, 0)
    return jnp.where(rows >= k, pltpu.roll(p, k, 0), 0.0)


def _shift_up(p, k):
    if k == 0:
        return p
    s = p.shape[0]
    rows = lax.broadcasted_iota(jnp.int32, p.shape, 0)
    return jnp.where(rows < s - k, pltpu.roll(p, s - k, 0), 0.0)


def _conv_fwd(proj, conv_w, d, tc, name):
    s = proj.shape[0]
    nb = d // tc

    def body(cb_ref, cc_ref, cx_ref, w_ref, y_ref):
        p = cc_ref[...].astype(F32) * cx_ref[...].astype(F32)
        w = w_ref[...]
        acc = p * w[CONV_K - 1:CONV_K, :]
        for k in range(1, CONV_K):
            acc = acc + _shift_down(p, k) * w[CONV_K - 1 - k:CONV_K - k, :]
        y_ref[...] = (cb_ref[...].astype(F32) * acc).astype(y_ref.dtype)

    col = lambda off: pl.BlockSpec((s, tc), lambda j: (0, off * nb + j))
    return _pcall(
        body, name=name, grid=(nb,), in_specs=[col(0), col(1), col(2), pl.BlockSpec((CONV_K, tc), lambda j: (0, j))],
        out_specs=pl.BlockSpec((s, tc), lambda j: (0, j)), out_shape=jax.ShapeDtypeStruct((s, d), BF16),
        compiler_params=_params("parallel"),
    )(proj, proj, proj, conv_w)


def _conv_bwd(dy, proj, conv_w, d, tc, name):
    s = proj.shape[0]
    nb = d // tc

    def body(dy_ref, cb_ref, cc_ref, cx_ref, w_ref, dcb_ref, dcc_ref, dcx_ref, dw_ref):
        cc, cx = cc_ref[...].astype(F32), cx_ref[...].astype(F32)
        p = cc * cx
        w = w_ref[...]
        dyv = dy_ref[...].astype(F32)
        shifted = [_shift_down(p, CONV_K - 1 - k) for k in range(CONV_K)]
        conv = shifted[0] * w[0:1, :]
        for k in range(1, CONV_K):
            conv = conv + shifted[k] * w[k:k + 1, :]
        dcb_ref[...] = (dyv * conv).astype(dcb_ref.dtype)
        ds = dyv * cb_ref[...].astype(F32)
        dp = ds * w[CONV_K - 1:CONV_K, :]
        for k in range(1, CONV_K):
            dp = dp + _shift_up(ds, k) * w[CONV_K - 1 - k:CONV_K - k, :]
        dcc_ref[...] = (dp * cx).astype(dcc_ref.dtype)
        dcx_ref[...] = (dp * cc).astype(dcx_ref.dtype)
        for k in range(CONV_K):
            dw_ref[k:k + 1, :] = jnp.sum(ds * shifted[k], axis=0, keepdims=True)

    col = lambda off: pl.BlockSpec((s, tc), lambda j: (0, off * nb + j))
    blk = pl.BlockSpec((s, tc), lambda j: (0, j))
    wblk = pl.BlockSpec((CONV_K, tc), lambda j: (0, j))
    act = jax.ShapeDtypeStruct((s, d), BF16)
    return _pcall(
        body, name=name, grid=(nb,), in_specs=[blk, col(0), col(1), col(2), wblk],
        out_specs=[blk, blk, blk, wblk], out_shape=[act, act, act, jax.ShapeDtypeStruct((CONV_K, d), F32)],
        compiler_params=_params("parallel"),
    )(dy, proj, proj, proj, conv_w)


def _sb_tile(q, kj, scale, carry, tri, mask):
    z = _dot(q, kj, 1, 1) * scale
    lsz = jnp.minimum(z, 0.0) - jnp.log(1.0 + jnp.exp(-jnp.abs(z)))
    l1m = lsz - z
    if mask is not None:
        l1m = jnp.where(mask, l1m, 0.0)
    l1b = l1m.astype(BF16)
    a = jnp.exp(lsz + (carry + _dot(l1b, tri, 1, 0)))
    if mask is not None:
        a = jnp.where(mask, a, 0.0)
    return lsz, l1b, a.astype(BF16)


def _sb_masks(tq, tk):
    row = lax.broadcasted_iota(jnp.int32, (tq, tk), 0)
    col = lax.broadcasted_iota(jnp.int32, (tq, tk), 1)
    masks = [col + dj * tk < row for dj in range(tq // tk)]
    r2 = lax.broadcasted_iota(jnp.int32, (tk, tk), 0)
    c2 = lax.broadcasted_iota(jnp.int32, (tk, tk), 1)
    return masks, (r2 > c2).astype(BF16), (r2 < c2).astype(BF16)


def _sb_fwd(proj, heads, col0, tq, tk, name):
    s = proj.shape[0]
    dh = SB_HEAD_DIM
    nq, nd = s // tq, tq // tk
    scale = dh ** -0.5

    def body(q_ref, k_ref, v_ref, o_ref):
        i = pl.program_id(1)
        q = q_ref[...]
        masks, tri_right, _ = _sb_masks(tq, tk)

        def tile(j, carry, acc, mask):
            start = pl.multiple_of(j * tk, tk)
            kj = k_ref[pl.ds(start, tk), :]
            vj = v_ref[pl.ds(start, tk), :]
            _, l1b, ab = _sb_tile(q, kj, scale, carry, tri_right, mask)
            return carry + jnp.sum(l1b.astype(F32), axis=1, keepdims=True), acc + _dot(ab, vj, 1, 0)

        state = (jnp.zeros((tq, 1), F32), jnp.zeros((tq, dh), F32))
        for dj in reversed(range(nd)):
            state = tile(i * nd + dj, *state, masks[dj])
        state = lax.fori_loop(0, i * nd, lambda t, st: tile(i * nd - 1 - t, st[0], st[1], None), state)
        o_ref[...] = state[1]

    qspec = pl.BlockSpec((tq, dh), lambda h, i: (i, col0[0] + h))
    kspec = pl.BlockSpec((s, dh), lambda h, i: (0, col0[1] + h))
    vspec = pl.BlockSpec((s, dh), lambda h, i: (0, col0[2] + h))
    return _pcall(
        body, name=name, grid=(heads, nq), in_specs=[qspec, kspec, vspec],
        out_specs=pl.BlockSpec((tq, dh), lambda h, i: (i, h)), out_shape=jax.ShapeDtypeStruct((s, heads * dh), F32),
        compiler_params=_params("parallel", "parallel"),
    )(proj, proj, proj)


def _sb_bwd(proj, o, do, heads, col0, tq, tk, name):
    s = proj.shape[0]
    dh = SB_HEAD_DIM
    nq, nd = s // tq, tq // tk
    scale = dh ** -0.5

    def body(q_ref, k_ref, v_ref, o_ref, do_ref, dq_ref, dk_ref, dv_ref, dk_acc, dv_acc):
        i = pl.program_id(1)

        @pl.when(i == 0)
        def _():
            dk_acc[...] = jnp.zeros_like(dk_acc)
            dv_acc[...] = jnp.zeros_like(dv_acc)

        q = q_ref[...]
        dob = do_ref[...].astype(BF16)
        delta = jnp.sum(dob.astype(F32) * o_ref[...], axis=1, keepdims=True)
        masks, tri_right, tri_left = _sb_masks(tq, tk)

        def tile(j, carry_l, carry_g, dq, mask):
            start = pl.multiple_of(j * tk, tk)
            kj = k_ref[pl.ds(start, tk), :]
            vj = v_ref[pl.ds(start, tk), :]
            lsz, l1b, ab = _sb_tile(q, kj, scale, carry_l, tri_right, mask)
            g = _dot(dob, vj, 1, 1) * ab.astype(F32)
            carry_g = carry_g + jnp.sum(g, axis=1, keepdims=True)
            left = (delta - carry_g) + _dot(g.astype(BF16), tri_left, 1, 0)
            beta = jnp.exp(lsz)
            dz = g * (1.0 - beta) - left * beta
            if mask is not None:
                dz = jnp.where(mask, dz, 0.0)
            dzb = (dz * scale).astype(BF16)
            dk_acc[pl.ds(start, tk), :] += _dot(dzb, q, 0, 0)
            dv_acc[pl.ds(start, tk), :] += _dot(ab, dob, 0, 0)
            return carry_l + jnp.sum(l1b.astype(F32), axis=1, keepdims=True), carry_g, dq + _dot(dzb, kj, 1, 0)

        zero = jnp.zeros((tq, 1), F32)
        state = (zero, zero, jnp.zeros((tq, dh), F32))
        for dj in reversed(range(nd)):
            state = tile(i * nd + dj, *state, masks[dj])
        state = lax.fori_loop(0, i * nd, lambda t, st: tile(i * nd - 1 - t, st[0], st[1], st[2], None), state)
        dq_ref[...] = state[2].astype(dq_ref.dtype)

        @pl.when(i == nq - 1)
        def _():
            dk_ref[...] = dk_acc[...].astype(dk_ref.dtype)
            dv_ref[...] = dv_acc[...].astype(dv_ref.dtype)

    qspec = pl.BlockSpec((tq, dh), lambda h, i: (i, col0[0] + h))
    kspec = pl.BlockSpec((s, dh), lambda h, i: (0, col0[1] + h))
    vspec = pl.BlockSpec((s, dh), lambda h, i: (0, col0[2] + h))
    blk = pl.BlockSpec((tq, dh), lambda h, i: (i, h))
    full = pl.BlockSpec((s, dh), lambda h, i: (0, h))
    act = jax.ShapeDtypeStruct((s, heads * dh), BF16)
    return _pcall(
        body, name=name, grid=(heads, nq), in_specs=[qspec, kspec, vspec, blk, blk],
        out_specs=[blk, full, full], out_shape=[act, act, act],
        scratch_shapes=[pltpu.VMEM((s, dh), F32), pltpu.VMEM((s, dh), F32)],
        compiler_params=_params("parallel", "arbitrary"),
    )(proj, proj, proj, o, do)


def _xattn_probs(q, k, scale):
    sc = _dot(q, k, 1, 1) * scale
    e = jnp.exp(sc - jnp.max(sc, axis=1, keepdims=True))
    return e / jnp.sum(e, axis=1, keepdims=True)


def _xattn_fwd(qc, kv, tq, name):
    s, d = qc.shape
    m = kv.shape[0]
    dh = d // X_HEADS
    scale = dh ** -0.5

    def body(q_ref, k_ref, v_ref, o_ref):
        p = _xattn_probs(q_ref[...], k_ref[...], scale)
        o_ref[...] = _dot(p.astype(BF16), v_ref[...], 1, 0).astype(o_ref.dtype)

    blk = pl.BlockSpec((tq, dh), lambda h, i: (i, h))
    return _pcall(
        body, name=name, grid=(X_HEADS, s // tq),
        in_specs=[blk, pl.BlockSpec((m, dh), lambda h, i: (0, h)), pl.BlockSpec((m, dh), lambda h, i: (0, X_HEADS + h))],
        out_specs=blk, out_shape=jax.ShapeDtypeStruct((s, d), BF16), compiler_params=_params("parallel", "parallel"),
    )(qc, kv, kv)


def _xattn_bwd(qc, kv, do, tq, name):
    s, d = qc.shape
    m = kv.shape[0]
    dh = d // X_HEADS
    scale = dh ** -0.5
    nq = s // tq

    def body(q_ref, k_ref, v_ref, do_ref, dq_ref, dk_ref, dv_ref, dk_acc, dv_acc):
        i = pl.program_id(1)
        q, k, v = q_ref[...], k_ref[...], v_ref[...]
        dob = do_ref[...].astype(BF16)
        p = _xattn_probs(q, k, scale)
        pb = p.astype(BF16)
        dp = _dot(dob, v, 1, 1)
        ds = pb.astype(F32) * (dp - jnp.sum(dp * pb.astype(F32), axis=1, keepdims=True))
        dsb = (ds * scale).astype(BF16)
        dq_ref[...] = _dot(dsb, k, 1, 0).astype(dq_ref.dtype)
        dk_part = _dot(dsb, q, 0, 0)
        dv_part = _dot(pb, dob, 0, 0)

        @pl.when(i == 0)
        def _():
            dk_acc[...] = dk_part
            dv_acc[...] = dv_part

        @pl.when(i > 0)
        def _():
            dk_acc[...] += dk_part
            dv_acc[...] += dv_part

        @pl.when(i == nq - 1)
        def _():
            dk_ref[...] = dk_acc[...].astype(dk_ref.dtype)
            dv_ref[...] = dv_acc[...].astype(dv_ref.dtype)

    blk = pl.BlockSpec((tq, dh), lambda h, i: (i, h))
    kblk = pl.BlockSpec((m, dh), lambda h, i: (0, h))
    return _pcall(
        body, name=name, grid=(X_HEADS, nq),
        in_specs=[blk, kblk, pl.BlockSpec((m, dh), lambda h, i: (0, X_HEADS + h)), blk],
        out_specs=[blk, kblk, kblk],
        out_shape=[jax.ShapeDtypeStruct((s, d), BF16), jax.ShapeDtypeStruct((m, d), BF16), jax.ShapeDtypeStruct((m, d), BF16)],
        scratch_shapes=[pltpu.VMEM((m, dh), F32), pltpu.VMEM((m, dh), F32)],
        compiler_params=_params("parallel", "arbitrary"),
    )(qc, kv, kv, do)


def _local_step(x, mem, tgt, w, fetch=None, prefetch=None, emit=None, tick=None, after=None):
    fetch = fetch or (lambda name, after: {})
    prefetch = prefetch or (lambda name, after: None)
    emit = emit or (lambda group, g: None)
    tick = tick or (lambda group, after: None)
    w = dict(w)
    s, d = x.shape
    heads = d // SB_HEAD_DIM
    tm = _pick(s, (256, 128))
    tq = _pick(s, (256, 128))
    sb_tq, sb_tk = _pick(s, (512, 256, 128)), _pick(s, (256, 128))
    tc = _pick(d, (256, 128))
    g = {}

    def wt(name, after):
        if name not in w:
            w.update(fetch(name, after))
        return w[name]

    def ffn_fwd(h, gname, wgu, wdown, tag, after=None):
        n = _rms_fwd(h, w[gname], tag + "_norm", tm, after=after)
        gu = _mm(n, wt(wgu, n), name=tag + "_gu")
        prefetch(wdown, gu)
        act = _swiglu_fwd(gu, tag + "_act", tm)
        return n, gu, act, _mm(act, wt(wdown, act), name=tag + "_down", out_dtype=F32, res=h, alpha=0.5)

    def ffn_bwd(dh, h, saved, gname, wgu, wdown, tag):
        n, gu, act = saved
        dhb = _rowcall(lambda v: 0.5 * v, [_whole(dh)], [], [(d, BF16)], tm=tm, name=tag + "_half")[0]
        g[wdown] = _mm(act, dhb, ta=True, name=tag + "_dwdown")
        dact = _mm(dhb, w[wdown], tb=True, name=tag + "_dact")
        dgu = _swiglu_bwd(dact, gu, tag + "_dgu", tm)
        g[wgu] = _mm(n, dgu, ta=True, name=tag + "_dwgu")
        dn = _mm(dgu, w[wgu], tb=True, name=tag + "_dn", out_dtype=F32, after=emit(tag, g))
        dh_in, g[gname] = _resid_rms_bwd(dh, dn, h, w[gname], tag + "_dnorm", tm, after=tick(tag, dn))
        return dh_in

    n1, gu1, act1, h1 = ffn_fwd(x, "g_ffn1", "w_ffn1_gu", "w_ffn1_down", "ffn1", after)
    prefetch("w_in", h1)
    u = _rms_fwd(h1, w["g_mix"], "mix_norm", tm)
    proj = _mm(u, wt("w_in", u), name="mix_in")
    prefetch("w_conv_out", proj)
    nd = d // SB_HEAD_DIM
    y_conv = _conv_fwd(proj, w["conv_w"], d, tc, "conv_fwd")
    sb_cols = (3 * nd, 4 * nd, 5 * nd)
    y_sb = _sb_fwd(proj, heads, sb_cols, sb_tq, sb_tk, "sb_fwd")
    prefetch("w_cq", y_sb)
    a_conv = _mm(y_conv, wt("w_conv_out", y_conv), name="conv_out")
    a_sb = _mm(y_sb, wt("w_attn_out", y_sb), name="attn_out")
    b_conv, b_sb = w["b_gate"][:, :d], w["b_gate"][:, d:]

    def merge(ac, asb, gcp, gsp, bc, bs):
        gc = _sigmoid(gcp.astype(F32) + bc)
        gs = _sigmoid(gsp.astype(F32) + bs)
        return gc * ac.astype(F32) + gs * asb.astype(F32)

    merged = _rowcall(merge, [_whole(a_conv), _whole(a_sb), (proj, 6, d), (proj, 7, d)], [b_conv, b_sb], [(d, BF16)],
                      tm=tm, name="merge")[0]
    prefetch("w_ffn2_gu", merged)
    h2 = _mm(merged, wt("w_o", merged), name="mix_out", out_dtype=F32, res=h1)
    hn = _rms_fwd(h2, w["g_cross"], "cross_norm", tm)
    mn = _rms_fwd(mem, w["g_mem"], "mem_norm", _pick(mem.shape[0], (256, 128)))
    qc = _mm(hn, wt("w_cq", hn), name="cross_q")
    kv = _mm(mn, wt("w_ckv", mn), name="cross_kv")
    oc = _xattn_fwd(qc, kv, tq, "xattn_fwd")
    h3 = _mm(oc, wt("w_co", oc), name="cross_out", out_dtype=F32, res=h2)
    n2, gu2, act2, h4 = ffn_fwd(h3, "g_ffn2", "w_ffn2_gu", "w_ffn2_down", "ffn2")

    def head(hb, tb, gb):
        xh, r = _xhat(hb)
        err = xh * gb - tb
        dy = err * (1.0 / d)
        dxh = dy * gb
        dx = r * (dxh - xh * jnp.mean(dxh * xh, axis=-1, keepdims=True))
        row_loss = 0.5 * jnp.mean(err * err, axis=-1, keepdims=True)
        return dx, dy * xh, jnp.broadcast_to(row_loss, (row_loss.shape[0], LANES))

    dh4, g["g_final"], loss_lanes = _rowcall(head, [_whole(h4), _whole(tgt)], [w["g_final"]], [(d, F32)], [d, LANES],
                                             tm=tm, name="loss_head")

    dh3 = ffn_bwd(dh4, h3, (n2, gu2, act2), "g_ffn2", "w_ffn2_gu", "w_ffn2_down", "ffn2")
    dh3b = _rowcall(lambda v: v, [_whole(dh3)], [], [(d, BF16)], tm=tm, name="cross_cast")[0]
    g["w_co"] = _mm(oc, dh3b, ta=True, name="cross_dwco")
    doc = _mm(dh3b, w["w_co"], tb=True, name="cross_doc")
    dqc, dk, dv = _xattn_bwd(qc, kv, doc, tq, "xattn_bwd")
    dkv = jnp.concatenate([dk, dv], axis=1)
    g["w_cq"] = _mm(hn, dqc, ta=True, name="cross_dwcq")
    g["w_ckv"] = _mm(mn, dkv, ta=True, name="cross_dwckv")
    dhn = _mm(dqc, w["w_cq"], tb=True, name="cross_dhn", out_dtype=F32, after=emit("cross", g))
    dmn = _mm(dkv, w["w_ckv"], tb=True, name="cross_dmn", out_dtype=F32)
    g["g_mem"] = _rowcall(lambda dy, xb: dy * _xhat(xb)[0], [_whole(dmn), _whole(mem)], [], [], [d],
                          tm=_pick(mem.shape[0], (256, 128)), name="mem_dnorm")[0]
    dh2, g["g_cross"] = _resid_rms_bwd(dh3, dhn, h2, w["g_cross"], "cross_dnorm", tm, after=tick("cross", dhn))

    dh2b = _rowcall(lambda v: v, [_whole(dh2)], [], [(d, BF16)], tm=tm, name="mix_cast")[0]
    g["w_o"] = _mm(merged, dh2b, ta=True, name="mix_dwo")
    dmerged = _mm(dh2b, w["w_o"], tb=True, name="mix_dmerged")

    def merge_bwd(dm, ac, asb, gcp, gsp, bc, bs):
        dm, ac, asb = dm.astype(F32), ac.astype(F32), asb.astype(F32)
        gc = _sigmoid(gcp.astype(F32) + bc)
        gs = _sigmoid(gsp.astype(F32) + bs)
        dgc = dm * ac * gc * (1.0 - gc)
        dgs = dm * asb * gs * (1.0 - gs)
        return dm * gc, dm * gs, dgc, dgs, dgc, dgs

    da_conv, da_sb, dgc, dgs, db_conv, db_sb = _rowcall(
        merge_bwd, [_whole(dmerged), _whole(a_conv), _whole(a_sb), (proj, 6, d), (proj, 7, d)], [b_conv, b_sb],
        [(d, BF16)] * 4, [d, d], tm=tm, name="merge_bwd")
    g["b_gate"] = jnp.concatenate([db_conv, db_sb], axis=1)
    g["w_conv_out"] = _mm(y_conv, da_conv, ta=True, name="conv_dwout")
    g["w_attn_out"] = _mm(y_sb, da_sb, ta=True, name="attn_dwout")
    dy_conv = _mm(da_conv, w["w_conv_out"], tb=True, name="conv_dy")
    dy_sb = _mm(da_sb, w["w_attn_out"], tb=True, name="attn_dy")
    dcb, dcc, dcx, g["conv_w"] = _conv_bwd(dy_conv, proj, w["conv_w"], d, tc, "conv_bwd")
    dq, dk_sb, dv_sb = _sb_bwd(proj, y_sb, dy_sb, heads, sb_cols, sb_tq, sb_tk, "sb_bwd")
    dproj = jnp.concatenate([dcb, dcc, dcx, dq, dk_sb, dv_sb, dgc, dgs], axis=1)
    g["w_in"] = _mm(u, dproj, ta=True, name="mix_dwin")
    du = _mm(dproj, w["w_in"], tb=True, name="mix_du", out_dtype=F32, after=emit("mix", g))
    dh1, g["g_mix"] = _resid_rms_bwd(dh2, du, h1, w["g_mix"], "mix_dnorm", tm, after=tick("mix", du))
    dx = ffn_bwd(dh1, x, (n1, gu1, act1), "g_ffn1", "w_ffn1_gu", "w_ffn1_down", "ffn1")
    return loss_lanes, dx, g


MATS = (("w_ffn1_gu", "col"), ("w_ffn1_down", "row"), ("w_in", "col"), ("w_conv_out", "row"), ("w_attn_out", "row"),
        ("w_o", "row"), ("w_cq", "row"), ("w_ckv", "col"), ("w_co", "row"), ("w_ffn2_gu", "col"), ("w_ffn2_down", "row"))
VECS = ("g_ffn1", "g_mix", "g_cross", "g_mem", "g_ffn2", "g_final")
WEIGHTS = ("g_ffn1", "w_ffn1_gu", "w_ffn1_down", "g_mix", "w_in", "b_gate", "conv_w", "w_conv_out", "w_attn_out", "w_o",
           "g_cross", "g_mem", "w_cq", "w_ckv", "w_co", "g_ffn2", "w_ffn2_gu", "w_ffn2_down", "g_final")
CONV_ROWS = 8


def _full_shape(kind, r, c):
    return (r, N_CHIPS * c) if kind == "col" else (N_CHIPS * r, c)


def _piece(ref, kind, r, c, chip, half):
    hr = r // 2
    if kind == "col":
        return ref.at[pl.ds(pl.multiple_of(half * hr, 16), hr), pl.ds(pl.multiple_of(chip * c, LANES), c)]
    return ref.at[pl.ds(pl.multiple_of(chip * r + half * hr, 16), hr), :]


def _shard_of(ref, kind, r, c, chip):
    if kind == "col":
        return ref.at[:, pl.ds(pl.multiple_of(chip * c, LANES), c)]
    return ref.at[pl.ds(pl.multiple_of(chip * r, 16), r), :]


def _place():
    x, y, c = lax.axis_index("x"), lax.axis_index("y"), lax.axis_index("c")
    others = [(1 - x, y), (x, 1 - y), (1 - x, 1 - y)]
    return x, y, c, 2 * x + y, others


def _remote(src, dst, send_sem, recv_sem, to):
    return pltpu.make_async_remote_copy(src_ref=src, dst_ref=dst, send_sem=send_sem, recv_sem=recv_sem,
                                        device_id=to, device_id_type=MESH)


def _gather_conv(conv_shard):
    cc = conv_shard.shape[1]

    def body(conv_ref, conv_full, cs, cr, cl):
        x, y, c, me, others = _place()

        def cols(chip):
            return conv_full.at[:, pl.ds(pl.multiple_of(chip * cc, LANES), cc)]

        def conv(k, chip_from, to):
            return _remote(conv_ref, cols(chip_from), cs.at[k], cr.at[k], to)

        mine = pltpu.make_async_copy(conv_ref, cols(me), cl.at[0])
        mine.start()
        for k, (ox, oy) in enumerate(others):
            conv(k, me, (ox, oy, c)).start()
        for k, (ox, oy) in enumerate(others):
            conv(k, 2 * ox + oy, (x, y, c)).wait_recv()
            conv(k, me, (ox, oy, c)).wait_send()
        mine.wait()

    dma = pltpu.SemaphoreType.DMA
    return _pcall(
        body, name="gather_conv", in_specs=[ANY], out_specs=ANY,
        out_shape=jax.ShapeDtypeStruct((CONV_ROWS, N_CHIPS * cc), F32), scratch_shapes=[dma((3,)), dma((3,)), dma((1,))],
    )(conv_shard)


HBM = pl.BlockSpec(memory_space=pltpu.HBM)
SEM = pl.BlockSpec(memory_space=pltpu.SEMAPHORE)
EFFECT = pltpu.SideEffectType.DATAFLOW_SIDE_EFFECTING
TOKEN = (8, LANES)


def _split_start(name, plan, n_copies, srcs, lands, after=None):
    ns, nl = len(srcs), len(lands)
    n_in = ns + nl + (after is not None)

    def body(*refs):
        outs = refs[n_in:]
        sends, _ = plan(refs[:ns], refs[ns:ns + nl], outs[0], outs[1])
        for cp in sends:
            cp.start()
        outs[-1][...] = jnp.zeros(TOKEN, F32)

    held = [pltpu.HBM(a.shape, a.dtype) for a in (*srcs, *lands)]
    dma = pltpu.SemaphoreType.DMA((n_copies,))
    ins = [pltpu.with_memory_space_constraint(a, pltpu.HBM) for a in (*srcs, *lands)]
    outs = _pcall(
        body, name=name, in_specs=[HBM] * (ns + nl) + ([] if after is None else [ANY]),
        out_specs=(SEM, SEM, *[HBM] * (ns + nl), pl.BlockSpec(memory_space=pltpu.VMEM)),
        out_shape=(dma, dma, *held, jax.ShapeDtypeStruct(TOKEN, F32)),
        input_output_aliases={i: 2 + i for i in range(ns + nl)},
        compiler_params=pltpu.CompilerParams(has_side_effects=EFFECT),
    )(*ins, *([] if after is None else [after]))
    return outs[0], outs[1], list(outs[2:2 + ns]), list(outs[2 + ns:2 + ns + nl]), outs[-1]


def _split_wait(name, plan, send_sems, recv_sems, srcs, lands, after):
    ns, nl = len(srcs), len(lands)

    def body(*refs):
        sends, recvs = plan(refs[:ns], refs[ns:ns + nl], refs[ns + nl], refs[ns + nl + 1])
        for cp in sends:
            cp.wait_send()
        for cp in recvs:
            cp.wait_recv()

    outs = _pcall(
        body, name=name, in_specs=[HBM] * (ns + nl) + [SEM, SEM, ANY], out_specs=[HBM] * (ns + nl),
        out_shape=[pltpu.HBM(a.shape, a.dtype) for a in (*srcs, *lands)],
        input_output_aliases={i: i for i in range(ns + nl)},
        compiler_params=pltpu.CompilerParams(has_side_effects=EFFECT),
    )(*srcs, *lands, send_sems, recv_sems, after)
    return list(outs[:ns]), list(outs[ns:])


def _gather_plan(dims):
    def plan(shard_refs, full_refs, ss, rs):
        x, y, c, me, others = _place()
        sends, recvs = [], []
        for wi, (kind, r, cw) in enumerate(dims):
            half = shard_refs[wi].at[pl.ds(pl.multiple_of(c * (r // 2), 16), r // 2), :]
            for k, (ox, oy) in enumerate(others):
                sem = 4 * wi + k
                sends.append(_remote(half, _piece(full_refs[wi], kind, r, cw, me, c), ss.at[sem], rs.at[sem], (ox, oy, c)))
                recvs.append(_remote(half, _piece(full_refs[wi], kind, r, cw, 2 * ox + oy, c), ss.at[sem], rs.at[sem], (x, y, c)))
            sem = 4 * wi + 3
            own = _remote(shard_refs[wi], _shard_of(full_refs[wi], kind, r, cw, me), ss.at[sem], rs.at[sem], (x, y, 1 - c))
            sends.append(own)
            recvs.append(own)
        return sends, recvs

    return plan


def _forward_plan(dims):
    def plan(_, full_refs, ss, rs):
        x, y, c, _, others = _place()
        sends, recvs = [], []
        for wi, (kind, r, cw) in enumerate(dims):
            for k, (ox, oy) in enumerate(others):
                sem = 3 * wi + k
                mine = _piece(full_refs[wi], kind, r, cw, 2 * ox + oy, c)
                theirs = _piece(full_refs[wi], kind, r, cw, 2 * ox + oy, 1 - c)
                sends.append(_remote(mine, mine, ss.at[sem], rs.at[sem], (x, y, 1 - c)))
                recvs.append(_remote(theirs, theirs, ss.at[sem], rs.at[sem], (x, y, 1 - c)))
        return sends, recvs

    return plan


def _rs_cores_plan(dims):
    def plan(g_refs, land_refs, ss, rs):
        x, y, c, _, _ = _place()
        sends, recvs = [], []
        for wi, dm in enumerate(dims):
            for chip in range(N_CHIPS):
                sem = N_CHIPS * wi + chip
                sends.append(_remote(_piece(g_refs[wi], *dm, chip, 1 - c), land_refs[wi].at[chip], ss.at[sem], rs.at[sem], (x, y, 1 - c)))
                recvs.append(_remote(_piece(g_refs[wi], *dm, chip, c), land_refs[wi].at[chip], ss.at[sem], rs.at[sem], (x, y, 1 - c)))
        return sends, recvs

    return plan


def _share_plan(nw):
    def plan(_, buf_refs, ss, rs):
        x, y, c, _, _ = _place()
        sends = [_remote(buf_refs[wi].at[c], buf_refs[wi].at[c], ss.at[wi], rs.at[wi], (x, y, 1 - c)) for wi in range(nw)]
        recvs = [_remote(buf_refs[wi].at[1 - c], buf_refs[wi].at[1 - c], ss.at[wi], rs.at[wi], (x, y, 1 - c)) for wi in range(nw)]
        return sends, recvs

    return plan


def _small_plan():
    def plan(_, buf_refs, ss, rs):
        x, y, c = lax.axis_index("x"), lax.axis_index("y"), lax.axis_index("c")
        buf = buf_refs[0]
        sends, recvs = [], []
        for rel in range(1, N_DEV):
            peer = (x ^ (rel >> 2 & 1), y ^ (rel >> 1 & 1), c ^ (rel & 1))
            sends.append(_remote(buf.at[0], buf.at[rel], ss.at[rel - 1], rs.at[rel - 1], peer))
            recvs.append(_remote(buf.at[0], buf.at[rel], ss.at[rel - 1], rs.at[rel - 1], peer))
        return sends, recvs

    return plan


def _sum_small(buf, me, name):
    _, rows, n = buf.shape

    def body(me_ref, b_ref, o_ref):
        tot = b_ref[me_ref[0]]
        for dev in range(1, N_DEV):
            tot = tot + b_ref[dev ^ me_ref[0]]
        o_ref[...] = tot

    return _pcall(
        body, name=name, out_shape=jax.ShapeDtypeStruct((rows, n), F32),
        grid_spec=pltpu.PrefetchScalarGridSpec(
            num_scalar_prefetch=1, grid=(1,), in_specs=[pl.BlockSpec((N_DEV, rows, n), lambda i, m: (0, 0, 0))],
            out_specs=pl.BlockSpec((rows, n), lambda i, m: (0, 0))),
    )(me, buf)


def _rs_chips_plan(nw):
    def plan(p_refs, land_refs, ss, rs):
        x, y, c, me, others = _place()
        sends, recvs = [], []
        for wi in range(nw):
            for k, (ox, oy) in enumerate(others):
                sem = 3 * wi + k
                sends.append(_remote(p_refs[wi].at[2 * ox + oy], land_refs[wi].at[k], ss.at[sem], rs.at[sem], (ox, oy, c)))
                recvs.append(_remote(p_refs[wi].at[me], land_refs[wi].at[k], ss.at[sem], rs.at[sem], (x, y, c)))
        return sends, recvs

    return plan


def _rows_per_block(n, c, limit_bytes=1 << 20):
    best = None
    for tm in range(16, n + 1, 16):
        if n % tm == 0 and tm * c * 4 <= limit_bytes:
            best = tm
    return best or n


def _sum_cores(grad, got, kind, place, name):
    _, hr, cw = got.shape
    tm = _rows_per_block(hr, cw)
    nb = hr // tm

    def body(place_ref, g_ref, t_ref, o_ref):
        o_ref[...] = (g_ref[...].astype(F32) + t_ref[...].astype(F32)).astype(o_ref.dtype)

    if kind == "col":
        g_spec = pl.BlockSpec((tm, cw), lambda j, i, pr: (pr[0] * nb + i, j))
    else:
        g_spec = pl.BlockSpec((tm, cw), lambda j, i, pr: ((2 * j + pr[0]) * nb + i, 0))
    blk = pl.BlockSpec((None, tm, cw), lambda j, i, pr: (j, i, 0))
    return _pcall(
        body, name=name, out_shape=jax.ShapeDtypeStruct(got.shape, BF16),
        grid_spec=pltpu.PrefetchScalarGridSpec(num_scalar_prefetch=1, grid=(N_CHIPS, nb), in_specs=[g_spec, blk], out_specs=blk),
        compiler_params=_params("parallel", "parallel"),
    )(place, grad, got)


def _sum_chips(parts, got, place, name):
    _, n, cw = got.shape
    tm = _rows_per_block(n, cw)

    def body(place_ref, p_ref, g_ref, o_ref):
        tot = p_ref[...].astype(F32)
        for k in range(3):
            tot = tot + g_ref[k].astype(F32)
        o_ref[...] = tot

    return _pcall(
        body, name=name, out_shape=jax.ShapeDtypeStruct((2, n, cw), F32),
        grid_spec=pltpu.PrefetchScalarGridSpec(
            num_scalar_prefetch=1, grid=(n // tm,),
            in_specs=[pl.BlockSpec((None, tm, cw), lambda i, pr: (pr[1], i, 0)), pl.BlockSpec((3, tm, cw), lambda i, pr: (0, i, 0))],
            out_specs=pl.BlockSpec((None, tm, cw), lambda i, pr: (pr[0], i, 0))),
        compiler_params=_params("parallel"),
    )(place, parts, got)


def _adamw(g, w, m, v, name):
    n, c = g.shape
    c1 = 1.0 - ADAM_B1 ** ADAM_STEP
    c2 = 1.0 - ADAM_B2 ** ADAM_STEP

    def fn(gb, wb, mb, vb):
        m_new = ADAM_B1 * mb + (1.0 - ADAM_B1) * gb
        v_new = ADAM_B2 * vb + (1.0 - ADAM_B2) * (gb * gb)
        delta = -ADAM_LR * ((m_new / c1) / (jnp.sqrt(v_new / c2) + ADAM_EPS) + ADAM_WD * wb)
        return gb, delta, m_new, v_new

    tm = _rows_per_block(n, c) if n % 16 == 0 else n
    return _rowcall(fn, [_whole(g), _whole(w), _whole(m), _whole(v)], [], [(c, F32)] * 4, tm=tm, name=name)


PACK_ROWS = 16


def _pack_rows(parts, width, name, after=None):
    assert sum(p.shape[0] for p in parts) <= PACK_ROWS

    def body(*refs):
        out_ref = refs[-1]
        out_ref[...] = jnp.zeros_like(out_ref)
        at = 0
        for r in refs[:len(parts)]:
            k, n = r.shape
            if n == width:
                out_ref[at:at + k, :] = r[...]
            else:
                out_ref[at:at + k, :] = jnp.broadcast_to(r[:, :1], (k, width))
            at += k

    vm = pl.BlockSpec(memory_space=pltpu.VMEM)
    return _pcall(body, name=name, in_specs=[vm] * len(parts) + ([] if after is None else [ANY]), out_specs=vm,
                  out_shape=jax.ShapeDtypeStruct((PACK_ROWS, width), F32))(*parts, *([] if after is None else [after]))


def _cast_shard(wm, name, after):
    n, c = wm.shape
    return _rowcall(lambda v: v, [_whole(wm)], [], [(c, BF16)], tm=_rows_per_block(n, c), name=name, after=after)[0]


GATHER_GROUPS = (
    ("w_ffn1_gu",), ("w_ffn1_down",), ("w_in",), ("w_conv_out", "w_attn_out", "w_o"), ("w_cq", "w_ckv", "w_co"),
    ("w_ffn2_gu", "w_ffn2_down"),
)
REDUCE_GROUPS = {
    "ffn2": ("w_ffn2_down", "w_ffn2_gu"),
    "cross": ("w_co", "w_cq", "w_ckv"),
    "mix": ("w_o", "w_conv_out", "w_attn_out", "w_in"),
    "ffn1": ("w_ffn1_down", "w_ffn1_gu"),
}
KIND = dict(MATS)


def _step(x, mem, tgt, wts, m_in, v_in):
    d = x.shape[-1]
    cc = wts["conv_w"].shape[1]
    place = jnp.stack([lax.axis_index("c"), 2 * lax.axis_index("x") + lax.axis_index("y")]).astype(jnp.int32)
    dims = {n: (kind, *wts[n].shape) for n, kind in MATS}

    conv_pad = jnp.pad(wts["conv_w"], ((0, CONV_ROWS - CONV_K), (0, 0)))
    conv_full = _gather_conv(conv_pad)
    w = {"conv_w": conv_full[:CONV_K]}
    for n in VECS + ("b_gate",):
        w[n] = wts[n].reshape(1, -1)
    flying, token = {}, conv_full
    for names in GATHER_GROUPS:
        gd = [dims[n] for n in names]
        shards = [_cast_shard(wts[n], "cast_" + n, token) for n in names]
        lands = [lax.empty(_full_shape(*dm), BF16) for dm in gd]
        plan = _gather_plan(gd)
        ss, rs, srcs, lands, token = _split_start("gather_start_" + names[0], plan, 4 * len(names), shards, lands, token)
        flying.update({n: (names, plan, ss, rs, srcs, lands, gd) for n in names})

    passing = {}

    def prefetch(name, after):
        if name not in passing:
            names, plan, ss, rs, srcs, lands, gd = flying[name]
            _, lands = _split_wait("gather_wait_" + names[0], plan, ss, rs, srcs, lands, after)
            plan = _forward_plan(gd)
            ss, rs, _, lands, _ = _split_start("forward_start_" + names[0], plan, 3 * len(names), [], lands)
            passing.update({n: (names, plan, ss, rs, lands) for n in names})

    def fetch(name, after):
        prefetch(name, after)
        names, plan, ss, rs, lands = passing[name]
        _, lands = _split_wait("forward_wait_" + names[0], plan, ss, rs, [], lands, after)
        return dict(zip(names, lands))

    swapping, sent = {}, {}

    def emit(tag, g):
        names = REDUCE_GROUPS[tag]
        gd = [dims[n] for n in names]
        lands = [lax.empty((N_CHIPS, r // 2, cw), BF16) for (_, r, cw) in gd]
        plan = _rs_cores_plan(gd)
        ss, rs, srcs, lands, tok = _split_start("rs_cores_start_" + tag, plan, N_CHIPS * len(names), [g[n] for n in names], lands)
        swapping[tag] = (plan, ss, rs, srcs, lands)
        return tok

    def tick(tag, after):
        names = REDUCE_GROUPS[tag]
        plan, ss, rs, srcs, lands = swapping[tag]
        mine, got = _split_wait("rs_cores_wait_" + tag, plan, ss, rs, srcs, lands, after)
        parts = [_sum_cores(gm, t, KIND[n], place, "sum_cores_" + n) for n, gm, t in zip(names, mine, got)]
        lands = [lax.empty((3, *p.shape[1:]), BF16) for p in parts]
        plan = _rs_chips_plan(len(names))
        ss, rs, srcs, lands, tok = _split_start("rs_chips_start_" + tag, plan, 3 * len(names), parts, lands)
        sent[tag] = (plan, ss, rs, srcs, lands)
        return tok

    loss_lanes, dx, g = _local_step(x[0], mem[0], tgt[0], w, fetch, prefetch, emit, tick, token)

    rows = [g[n] for n in VECS] + [g["b_gate"][:, :d], g["b_gate"][:, d:], g["conv_w"], loss_lanes]
    packed = _pack_rows(rows, d, "pack_small")
    small = jnp.concatenate([packed[None], jnp.zeros((N_DEV - 1, *packed.shape), F32)], axis=0)
    small_plan = _small_plan()
    small_ss, small_rs, _, small, after = _split_start("small_start", small_plan, N_DEV - 1, [], [small])

    grads, out = {}, {}

    def update(n):
        shape = wts[n].shape
        as2d = (lambda a: a.reshape(1, -1)) if len(shape) == 1 else (lambda a: a)
        return [r.reshape(shape) for r in _adamw(grads[n], as2d(wts[n]), as2d(m_in[n]), as2d(v_in[n]), "adamw_" + n)]

    def finish(sharing, after):
        tag, names, plan, ss, rs, halves = sharing
        _, both = _split_wait("share_wait_" + tag, plan, ss, rs, [], halves, after)
        for n, b in zip(names, both):
            grads[n] = b.reshape(-1, b.shape[-1])
            out[n] = update(n)
        return out[names[-1]][1]

    sharing = None
    for tag, names in REDUCE_GROUPS.items():
        plan, ss, rs, srcs, lands = sent[tag]
        parts, landed = _split_wait("rs_chips_wait_" + tag, plan, ss, rs, srcs, lands, after)
        halves = [_sum_chips(p, t, place, "sum_chips_" + n) for n, p, t in zip(names, parts, landed)]
        plan = _share_plan(len(names))
        ss, rs, _, halves, after = _split_start("share_start_" + tag, plan, len(names), [], halves)
        if sharing is None:
            _, small = _split_wait("small_wait", small_plan, small_ss, small_rs, [], small, after)
            me = (4 * lax.axis_index("x") + 2 * lax.axis_index("y") + lax.axis_index("c")).astype(jnp.int32).reshape(1)
            red = _sum_small(small[0], me, "sum_small")
            grads.update({n: red[i:i + 1] for i, n in enumerate(VECS)})
            nv = len(VECS)
            grads["b_gate"] = jnp.concatenate([red[nv:nv + 1], red[nv + 1:nv + 2]], axis=1)
            chip = 2 * lax.axis_index("x") + lax.axis_index("y")
            grads["conv_w"] = lax.dynamic_slice_in_dim(red[nv + 2:nv + 2 + CONV_K], chip * cc, cc, axis=1)
            loss = red[nv + 2 + CONV_K, 0]
            out.update({n: update(n) for n in WEIGHTS if n not in KIND})
            after = out["g_final"][1]
        else:
            after = finish(sharing, after)
        sharing = (tag, names, plan, ss, rs, halves)

    finish(sharing, after)
    return (loss, dx[None], *[out[n][0] for n in WEIGHTS], *[out[n][1] for n in WEIGHTS],
            *[out[n][2] for n in WEIGHTS], *[out[n][3] for n in WEIGHTS])


def kernel(x, mem, g_ffn1, w_ffn1_gu, w_ffn1_down, g_mix, w_in, b_gate, conv_w, w_conv_out, w_attn_out, w_o, g_cross, g_mem, w_cq, w_ckv, w_co, g_ffn2, w_ffn2_gu, w_ffn2_down, g_final, loss_target, m_g_ffn1, m_w_ffn1_gu, m_w_ffn1_down, m_g_mix, m_w_in, m_b_gate, m_conv_w, m_w_conv_out, m_w_attn_out, m_w_o, m_g_cross, m_g_mem, m_w_cq, m_w_ckv, m_w_co, m_g_ffn2, m_w_ffn2_gu, m_w_ffn2_down, m_g_final, v_g_ffn1, v_w_ffn1_gu, v_w_ffn1_down, v_g_mix, v_w_in, v_b_gate, v_conv_w, v_w_conv_out, v_w_attn_out, v_w_o, v_g_cross, v_g_mem, v_w_cq, v_w_ckv, v_w_co, v_g_ffn2, v_w_ffn2_gu, v_w_ffn2_down, v_g_final):
    given = dict(locals())
    wts = {n: given[n] for n in WEIGHTS}
    m_in = {n: given["m_" + n] for n in WEIGHTS}
    v_in = {n: given["v_" + n] for n in WEIGHTS}
    return _step(x, mem, loss_target, wts, m_in, v_in)
```

```python
import functools

import jax
import jax.numpy as jnp
from jax import lax
from jax.experimental import pallas as pl
from jax.experimental.pallas import tpu as pltpu

F32 = jnp.float32
BF16 = jnp.bfloat16
MESH = pl.DeviceIdType.MESH

V7X_VMEM_LIMIT_BYTES = 48 * 1024 * 1024
MM_VMEM_BUDGET_BYTES = 36 * 1024 * 1024
MM_WHOLE_K = 2816
LANES = 128
SB_HEAD_DIM = 128
X_HEADS = 4
CONV_K = 3
RMS_EPS = 1e-6
N_CHIPS = 4
N_DEV = 8
ADAM_LR, ADAM_B1, ADAM_B2, ADAM_EPS, ADAM_WD, ADAM_STEP = 0.001, 0.9, 0.999, 1e-08, 0.01, 10


ANY = pl.BlockSpec(memory_space=pl.ANY)


def _pcall(body, **kw):
    return pl.pallas_call(body, **kw)


def _params(*sem):
    return pltpu.CompilerParams(dimension_semantics=sem, vmem_limit_bytes=V7X_VMEM_LIMIT_BYTES)


def _pick(dim, cands):
    for c in cands:
        if dim % c == 0:
            return c
    return dim


def _dot(a, b, ca, cb):
    return lax.dot_general(a, b, (((ca,), (cb,)), ((), ())), preferred_element_type=F32)


def _mm(a, b, *, name, ta=False, tb=False, out_dtype=BF16, res=None, alpha=1.0, tm=None, tn=None, tk=None, after=None):
    m, k = (a.shape[1], a.shape[0]) if ta else a.shape
    n = b.shape[0] if tb else b.shape[1]
    assert k == (b.shape[1] if tb else b.shape[0]), (a.shape, b.shape, ta, tb)
    if ta:
        tm = tm or _pick(m, (512, 256, 128))
        tn = tn or _pick(n, (1024, 512, 256, 128))
        tk = tk or (k if k <= MM_WHOLE_K else _pick(k, (1024, 512, 256, 128)))
    else:
        tk = tk or (k if k <= MM_WHOLE_K else _pick(k, (MM_WHOLE_K, 2048, 1024, 512, 256, 128)))
        tn = tn or _pick(n, (512, 1408, 256, 128))
        per_row = 2 * (tk * a.dtype.itemsize + tn * (jnp.dtype(out_dtype).itemsize + (0 if res is None else res.dtype.itemsize)))
        per_row += 4 * tn if tk < k else 0
        rows = (MM_VMEM_BUDGET_BYTES - 2 * tk * tn * b.dtype.itemsize) // per_row
        tm = tm or next((c for c in (2048, 1024, 512, 256, 128) if m % c == 0 and c <= rows), m)
    nk = k // tk
    assert m % tm == 0 and n % tn == 0 and k % tk == 0
    a_spec = pl.BlockSpec((tk, tm), lambda i, j, kk: (kk, i)) if ta else pl.BlockSpec((tm, tk), lambda i, j, kk: (i, kk))
    b_spec = pl.BlockSpec((tn, tk), lambda i, j, kk: (j, kk)) if tb else pl.BlockSpec((tk, tn), lambda i, j, kk: (kk, j))
    o_spec = pl.BlockSpec((tm, tn), lambda i, j, kk: (i, j))
    ca, cb = (0 if ta else 1), (1 if tb else 0)

    n_in = 2 + (res is not None) + (after is not None)

    def body(*refs):
        a_ref, b_ref = refs[:2]
        res_ref = refs[2] if res is not None else None
        o_ref = refs[n_in]
        scratch = refs[n_in + 1:]

        def finish(acc):
            val = acc if alpha == 1.0 else alpha * acc
            if res_ref is not None:
                val = res_ref[...].astype(F32) + val
            o_ref[...] = val.astype(o_ref.dtype)

        part = _dot(a_ref[...].astype(BF16), b_ref[...].astype(BF16), ca, cb)
        if nk == 1:
            finish(part)
        else:
            acc_ref = scratch[0]
            kk = pl.program_id(2)

            @pl.when(kk == 0)
            def _():
                acc_ref[...] = part

            @pl.when(kk > 0)
            def _():
                acc_ref[...] += part

            @pl.when(kk == nk - 1)
            def _():
                finish(acc_ref[...])

    ins = [a, b] + ([] if res is None else [res]) + ([] if after is None else [after])
    in_specs = [a_spec, b_spec] + ([] if res is None else [o_spec]) + ([] if after is None else [ANY])
    return _pcall(
        body, name=name, grid=(m // tm, n // tn, nk), in_specs=in_specs, out_specs=o_spec,
        out_shape=jax.ShapeDtypeStruct((m, n), out_dtype),
        scratch_shapes=[pltpu.VMEM((tm, tn), F32)] if nk > 1 else [],
        compiler_params=_params("parallel", "parallel", "arbitrary"),
    )(*ins)


def _rowcall(fn, rows, consts, outs, accs=(), *, tm, name, after=None):
    s = rows[0][0].shape[0]
    assert s % tm == 0
    n_read, n_out = len(rows) + len(consts), len(outs)
    n_in = n_read + (after is not None)

    def body(*refs):
        vals = fn(*[r[...] for r in refs[:n_read]])
        vals = vals if isinstance(vals, (tuple, list)) else (vals,)
        for o_ref, v in zip(refs[n_in:n_in + n_out], vals[:n_out]):
            o_ref[...] = v.astype(o_ref.dtype)
        if accs:
            first = pl.program_id(0) == 0
            for a_ref, v in zip(refs[n_in + n_out:], vals[n_out:]):
                tot = jnp.sum(v.astype(F32), axis=0, keepdims=True)

                @pl.when(first)
                def _(a_ref=a_ref, tot=tot):
                    a_ref[...] = tot

                @pl.when(jnp.logical_not(first))
                def _(a_ref=a_ref, tot=tot):
                    a_ref[...] += tot

    in_specs = [pl.BlockSpec((tm, w), lambda i, cb=cb: (i, cb)) for (_, cb, w) in rows]
    in_specs += [pl.BlockSpec(c.shape, lambda i: (0, 0)) for c in consts]
    in_specs += [] if after is None else [ANY]
    out_specs = [pl.BlockSpec((tm, w), lambda i: (i, 0)) for (w, _) in outs]
    out_specs += [pl.BlockSpec((1, w), lambda i: (0, 0)) for w in accs]
    out_shape = [jax.ShapeDtypeStruct((s, w), dt) for (w, dt) in outs]
    out_shape += [jax.ShapeDtypeStruct((1, w), F32) for w in accs]
    return _pcall(
        body, name=name, grid=(s // tm,), in_specs=in_specs, out_specs=out_specs, out_shape=out_shape,
        compiler_params=_params("arbitrary" if accs else "parallel"),
    )(*[r[0] for r in rows], *consts, *([] if after is None else [after]))


def _whole(a):
    return (a, 0, a.shape[1])


def _xhat(x):
    x = x.astype(F32)
    r = lax.rsqrt(jnp.mean(x * x, axis=-1, keepdims=True) + RMS_EPS)
    return x * r, r


def _rms_bwd(dy, x, g):
    xh, r = _xhat(x)
    dxh = dy.astype(F32) * g
    dx = r * (dxh - xh * jnp.mean(dxh * xh, axis=-1, keepdims=True))
    return dx, dy.astype(F32) * xh


def _sigmoid(x):
    return 1.0 / (1.0 + jnp.exp(-x))


def _rms_fwd(x, g, name, tm, after=None):
    d = x.shape[1]
    return _rowcall(lambda xb, gb: _xhat(xb)[0] * gb, [_whole(x)], [g], [(d, BF16)], tm=tm, name=name, after=after)[0]


def _swiglu_fwd(gu, name, tm):
    f = gu.shape[1] // 2

    def fn(gate, up):
        gate, up = gate.astype(F32), up.astype(F32)
        return gate * _sigmoid(gate) * up

    return _rowcall(fn, [(gu, 0, f), (gu, 1, f)], [], [(f, BF16)], tm=tm, name=name)[0]


def _swiglu_bwd(dact, gu, name, tm):
    f = gu.shape[1] // 2

    def fn(da, gate, up):
        da, gate, up = da.astype(F32), gate.astype(F32), up.astype(F32)
        sg = _sigmoid(gate)
        silu = gate * sg
        dgate = da * up * (sg + silu * (1.0 - sg))
        return jnp.concatenate([dgate, da * silu], axis=1)

    return _rowcall(fn, [_whole(dact), (gu, 0, f), (gu, 1, f)], [], [(2 * f, BF16)], tm=tm, name=name)[0]


def _resid_rms_bwd(dh, dn, x, g, name, tm, after=None, copy_scale=None):
    d = x.shape[1]

    def fn(dhb, dnb, xb, gb):
        dx, dg = _rms_bwd(dnb, xb, gb)
        tot = dhb.astype(F32) + dx
        return (tot, dg) if copy_scale is None else (tot, copy_scale * tot, dg)

    outs = [(d, F32)] + ([] if copy_scale is None else [(d, BF16)])
    return _rowcall(fn, [_whole(dh), _whole(dn), _whole(x)], [g], outs, [d], tm=tm, name=name, after=after)


def _shift_down(p, k):
    if k == 0:
        return p
    rows = lax.broadcasted_iota(jnp.int32, p.shape, 0)
    return jnp.where(rows >= k, pltpu.roll(p, k, 0), 0.0)


def _shift_up(p, k):
    if k == 0:
        return p
    s = p.shape[0]
    rows = lax.broadcasted_iota(jnp.int32, p.shape, 0)
    return jnp.where(rows < s - k, pltpu.roll(p, s - k, 0), 0.0)


def _conv_fwd(proj, conv_w, d, tc, name):
    s = proj.shape[0]
    nb = d // tc

    def body(cb_ref, cc_ref, cx_ref, w_ref, y_ref):
        p = cc_ref[...].astype(F32) * cx_ref[...].astype(F32)
        w = w_ref[...]
        acc = p * w[CONV_K - 1:CONV_K, :]
        for k in range(1, CONV_K):
            acc = acc + _shift_down(p, k) * w[CONV_K - 1 - k:CONV_K - k, :]
        y_ref[...] = (cb_ref[...].astype(F32) * acc).astype(y_ref.dtype)

    col = lambda off: pl.BlockSpec((s, tc), lambda j: (0, off * nb + j))
    return _pcall(
        body, name=name, grid=(nb,), in_specs=[col(0), col(1), col(2), pl.BlockSpec((CONV_K, tc), lambda j: (0, j))],
        out_specs=pl.BlockSpec((s, tc), lambda j: (0, j)), out_shape=jax.ShapeDtypeStruct((s, d), BF16),
        compiler_params=_params("parallel"),
    )(proj, proj, proj, conv_w)


def _conv_bwd(dy, proj, conv_w, d, tc, name):
    s = proj.shape[0]
    nb = d // tc

    def body(dy_ref, cb_ref, cc_ref, cx_ref, w_ref, dcb_ref, dcc_ref, dcx_ref, dw_ref):
        cc, cx = cc_ref[...].astype(F32), cx_ref[...].astype(F32)
        p = cc * cx
        w = w_ref[...]
        dyv = dy_ref[...].astype(F32)
        shifted = [_shift_down(p, CONV_K - 1 - k) for k in range(CONV_K)]
        conv = shifted[0] * w[0:1, :]
        for k in range(1, CONV_K):
            conv = conv + shifted[k] * w[k:k + 1, :]
        dcb_ref[...] = (dyv * conv).astype(dcb_ref.dtype)
        ds = dyv * cb_ref[...].astype(F32)
        dp = ds * w[CONV_K - 1:CONV_K, :]
        for k in range(1, CONV_K):
            dp = dp + _shift_up(ds, k) * w[CONV_K - 1 - k:CONV_K - k, :]
        dcc_ref[...] = (dp * cx).astype(dcc_ref.dtype)
        dcx_ref[...] = (dp * cc).astype(dcx_ref.dtype)
        for k in range(CONV_K):
            dw_ref[k:k + 1, :] = jnp.sum(ds * shifted[k], axis=0, keepdims=True)

    col = lambda off: pl.BlockSpec((s, tc), lambda j: (0, off * nb + j))
    blk = pl.BlockSpec((s, tc), lambda j: (0, j))
    wblk = pl.BlockSpec((CONV_K, tc), lambda j: (0, j))
    act = jax.ShapeDtypeStruct((s, d), BF16)
    return _pcall(
        body, name=name, grid=(nb,), in_specs=[blk, col(0), col(1), col(2), wblk],
        out_specs=[blk, blk, blk, wblk], out_shape=[act, act, act, jax.ShapeDtypeStruct((CONV_K, d), F32)],
        compiler_params=_params("parallel"),
    )(dy, proj, proj, proj, conv_w)


def _sb_tile(q, kj, scale, carry, tri, mask):
    z = _dot(q, kj, 1, 1) * scale
    lsz = jnp.minimum(z, 0.0) - jnp.log(1.0 + jnp.exp(-jnp.abs(z)))
    l1m = lsz - z
    if mask is not None:
        l1m = jnp.where(mask, l1m, 0.0)
    l1b = l1m.astype(BF16)
    a = jnp.exp(lsz + (carry + _dot(l1b, tri, 1, 0)))
    if mask is not None:
        a = jnp.where(mask, a, 0.0)
    return lsz, l1b, a.astype(BF16)


def _sb_masks(tq, tk):
    row = lax.broadcasted_iota(jnp.int32, (tq, tk), 0)
    col = lax.broadcasted_iota(jnp.int32, (tq, tk), 1)
    masks = [col + dj * tk < row for dj in range(tq // tk)]
    r2 = lax.broadcasted_iota(jnp.int32, (tk, tk), 0)
    c2 = lax.broadcasted_iota(jnp.int32, (tk, tk), 1)
    return masks, (r2 > c2).astype(BF16), (r2 < c2).astype(BF16)


def _sb_fwd(proj, heads, col0, tq, tk, name):
    s = proj.shape[0]
    dh = SB_HEAD_DIM
    nq, nd = s // tq, tq // tk
    scale = dh ** -0.5

    def body(q_ref, k_ref, v_ref, o_ref):
        i = pl.program_id(1)
        q = q_ref[...]
        masks, tri_right, _ = _sb_masks(tq, tk)

        def tile(j, carry, acc, mask):
            start = pl.multiple_of(j * tk, tk)
            kj = k_ref[pl.ds(start, tk), :]
            vj = v_ref[pl.ds(start, tk), :]
            _, l1b, ab = _sb_tile(q, kj, scale, carry, tri_right, mask)
            return carry + jnp.sum(l1b.astype(F32), axis=1, keepdims=True), acc + _dot(ab, vj, 1, 0)

        state = (jnp.zeros((tq, 1), F32), jnp.zeros((tq, dh), F32))
        for dj in reversed(range(nd)):
            state = tile(i * nd + dj, *state, masks[dj])
        state = lax.fori_loop(0, i * nd, lambda t, st: tile(i * nd - 1 - t, st[0], st[1], None), state)
        o_ref[...] = state[1]

    qspec = pl.BlockSpec((tq, dh), lambda h, i: (i, col0[0] + h))
    kspec = pl.BlockSpec((s, dh), lambda h, i: (0, col0[1] + h))
    vspec = pl.BlockSpec((s, dh), lambda h, i: (0, col0[2] + h))
    return _pcall(
        body, name=name, grid=(heads, nq), in_specs=[qspec, kspec, vspec],
        out_specs=pl.BlockSpec((tq, dh), lambda h, i: (i, h)), out_shape=jax.ShapeDtypeStruct((s, heads * dh), F32),
        compiler_params=_params("parallel", "parallel"),
    )(proj, proj, proj)


def _sb_bwd(proj, o, do, heads, col0, tq, tk, name):
    s = proj.shape[0]
    dh = SB_HEAD_DIM
    nq, nd = s // tq, tq // tk
    scale = dh ** -0.5

    def body(q_ref, k_ref, v_ref, o_ref, do_ref, dq_ref, dk_ref, dv_ref, dk_acc, dv_acc):
        i = pl.program_id(1)

        @pl.when(i == 0)
        def _():
            dk_acc[...] = jnp.zeros_like(dk_acc)
            dv_acc[...] = jnp.zeros_like(dv_acc)

        q = q_ref[...]
        dob = do_ref[...].astype(BF16)
        delta = jnp.sum(dob.astype(F32) * o_ref[...], axis=1, keepdims=True)
        masks, tri_right, tri_left = _sb_masks(tq, tk)

        def tile(j, carry_l, carry_g, dq, mask):
            start = pl.multiple_of(j * tk, tk)
            kj = k_ref[pl.ds(start, tk), :]
            vj = v_ref[pl.ds(start, tk), :]
            lsz, l1b, ab = _sb_tile(q, kj, scale, carry_l, tri_right, mask)
            g = _dot(dob, vj, 1, 1) * ab.astype(F32)
            carry_g = carry_g + jnp.sum(g, axis=1, keepdims=True)
            left = (delta - carry_g) + _dot(g.astype(BF16), tri_left, 1, 0)
            beta = jnp.exp(lsz)
            dz = g * (1.0 - beta) - left * beta
            if mask is not None:
                dz = jnp.where(mask, dz, 0.0)
            dzb = (dz * scale).astype(BF16)
            dk_acc[pl.ds(start, tk), :] += _dot(dzb, q, 0, 0)
            dv_acc[pl.ds(start, tk), :] += _dot(ab, dob, 0, 0)
            return carry_l + jnp.sum(l1b.astype(F32), axis=1, keepdims=True), carry_g, dq + _dot(dzb, kj, 1, 0)

        zero = jnp.zeros((tq, 1), F32)
        state = (zero, zero, jnp.zeros((tq, dh), F32))
        for dj in reversed(range(nd)):
            state = tile(i * nd + dj, *state, masks[dj])
        state = lax.fori_loop(0, i * nd, lambda t, st: tile(i * nd - 1 - t, st[0], st[1], st[2], None), state)
        dq_ref[...] = state[2].astype(dq_ref.dtype)

        @pl.when(i == nq - 1)
        def _():
            dk_ref[...] = dk_acc[...].astype(dk_ref.dtype)
            dv_ref[...] = dv_acc[...].astype(dv_ref.dtype)

    qspec = pl.BlockSpec((tq, dh), lambda h, i: (i, col0[0] + h))
    kspec = pl.BlockSpec((s, dh), lambda h, i: (0, col0[1] + h))
    vspec = pl.BlockSpec((s, dh), lambda h, i: (0, col0[2] + h))
    blk = pl.BlockSpec((tq, dh), lambda h, i: (i, h))
    full = pl.BlockSpec((s, dh), lambda h, i: (0, h))
    act = jax.ShapeDtypeStruct((s, heads * dh), BF16)
    return _pcall(
        body, name=name, grid=(heads, nq), in_specs=[qspec, kspec, vspec, blk, blk],
        out_specs=[blk, full, full], out_shape=[act, act, act],
        scratch_shapes=[pltpu.VMEM((s, dh), F32), pltpu.VMEM((s, dh), F32)],
        compiler_params=_params("parallel", "arbitrary"),
    )(proj, proj, proj, o, do)


def _xattn_probs(q, k, scale):
    sc = _dot(q, k, 1, 1) * scale
    e = jnp.exp(sc - jnp.max(sc, axis=1, keepdims=True))
    return e / jnp.sum(e, axis=1, keepdims=True)


def _xattn_fwd(qc, kv, tq, name):
    s, d = qc.shape
    m = kv.shape[0]
    dh = d // X_HEADS
    scale = dh ** -0.5

    def body(q_ref, k_ref, v_ref, o_ref):
        p = _xattn_probs(q_ref[...], k_ref[...], scale)
        o_ref[...] = _dot(p.astype(BF16), v_ref[...], 1, 0).astype(o_ref.dtype)

    blk = pl.BlockSpec((tq, dh), lambda h, i: (i, h))
    return _pcall(
        body, name=name, grid=(X_HEADS, s // tq),
        in_specs=[blk, pl.BlockSpec((m, dh), lambda h, i: (0, h)), pl.BlockSpec((m, dh), lambda h, i: (0, X_HEADS + h))],
        out_specs=blk, out_shape=jax.ShapeDtypeStruct((s, d), BF16), compiler_params=_params("parallel", "parallel"),
    )(qc, kv, kv)


def _xattn_bwd(qc, kv, do, tq, name):
    s, d = qc.shape
    m = kv.shape[0]
    dh = d // X_HEADS
    scale = dh ** -0.5
    nq = s // tq

    def body(q_ref, k_ref, v_ref, do_ref, dq_ref, dk_ref, dv_ref, dk_acc, dv_acc):
        i = pl.program_id(1)
        q, k, v = q_ref[...], k_ref[...], v_ref[...]
        dob = do_ref[...].astype(BF16)
        p = _xattn_probs(q, k, scale)
        pb = p.astype(BF16)
        dp = _dot(dob, v, 1, 1)
        ds = pb.astype(F32) * (dp - jnp.sum(dp * pb.astype(F32), axis=1, keepdims=True))
        dsb = (ds * scale).astype(BF16)
        dq_ref[...] = _dot(dsb, k, 1, 0).astype(dq_ref.dtype)
        dk_part = _dot(dsb, q, 0, 0)
        dv_part = _dot(pb, dob, 0, 0)

        @pl.when(i == 0)
        def _():
            dk_acc[...] = dk_part
            dv_acc[...] = dv_part

        @pl.when(i > 0)
        def _():
            dk_acc[...] += dk_part
            dv_acc[...] += dv_part

        @pl.when(i == nq - 1)
        def _():
            dk_ref[...] = dk_acc[...].astype(dk_ref.dtype)
            dv_ref[...] = dv_acc[...].astype(dv_ref.dtype)

    blk = pl.BlockSpec((tq, dh), lambda h, i: (i, h))
    kblk = pl.BlockSpec((m, dh), lambda h, i: (0, h))
    return _pcall(
        body, name=name, grid=(X_HEADS, nq),
        in_specs=[blk, kblk, pl.BlockSpec((m, dh), lambda h, i: (0, X_HEADS + h)), blk],
        out_specs=[blk, kblk, kblk],
        out_shape=[jax.ShapeDtypeStruct((s, d), BF16), jax.ShapeDtypeStruct((m, d), BF16), jax.ShapeDtypeStruct((m, d), BF16)],
        scratch_shapes=[pltpu.VMEM((m, dh), F32), pltpu.VMEM((m, dh), F32)],
        compiler_params=_params("parallel", "arbitrary"),
    )(qc, kv, kv, do)


def _local_step(x, mem, tgt, w, fetch=None, prefetch=None, emit=None, tick=None, after=None):
    fetch = fetch or (lambda name, after: {})
    prefetch = prefetch or (lambda name, after: None)
    emit = emit or (lambda group, g: None)
    tick = tick or (lambda group, after: None)
    w = dict(w)
    s, d = x.shape
    heads = d // SB_HEAD_DIM
    tm = _pick(s, (256, 128))
    tq = _pick(s, (256, 128))
    sb_tq, sb_tk = _pick(s, (512, 256, 128)), _pick(s, (256, 128))
    tc = _pick(d, (256, 128))
    g = {}

    def wt(name, after):
        if name not in w:
            w.update(fetch(name, after))
        return w[name]

    def ffn_fwd(h, gname, wgu, wdown, tag, after=None):
        n = _rms_fwd(h, w[gname], tag + "_norm", tm, after=after)
        gu = _mm(n, wt(wgu, n), name=tag + "_gu")
        prefetch(wdown, gu)
        act = _swiglu_fwd(gu, tag + "_act", tm)
        return n, gu, act, _mm(act, wt(wdown, act), name=tag + "_down", out_dtype=F32, res=h, alpha=0.5)

    def ffn_bwd(dh, dhb, h, saved, gname, wgu, wdown, tag, copy_scale=None):
        n, gu, act = saved
        g[wdown] = _mm(act, dhb, ta=True, name=tag + "_dwdown")
        dact = _mm(dhb, w[wdown], tb=True, name=tag + "_dact")
        dgu = _swiglu_bwd(dact, gu, tag + "_dgu", tm)
        g[wgu] = _mm(n, dgu, ta=True, name=tag + "_dwgu")
        dn = _mm(dgu, w[wgu], tb=True, name=tag + "_dn", out_dtype=F32, after=emit(tag, g))
        *dh_in, g[gname] = _resid_rms_bwd(dh, dn, h, w[gname], tag + "_dnorm", tm, after=tick(tag, dn), copy_scale=copy_scale)
        return dh_in

    n1, gu1, act1, h1 = ffn_fwd(x, "g_ffn1", "w_ffn1_gu", "w_ffn1_down", "ffn1", after)
    prefetch("w_in", h1)
    u = _rms_fwd(h1, w["g_mix"], "mix_norm", tm)
    proj = _mm(u, wt("w_in", u), name="mix_in")
    prefetch("w_conv_out", proj)
    nd = d // SB_HEAD_DIM
    y_conv = _conv_fwd(proj, w["conv_w"], d, tc, "conv_fwd")
    sb_cols = (3 * nd, 4 * nd, 5 * nd)
    y_sb = _sb_fwd(proj, heads, sb_cols, sb_tq, sb_tk, "sb_fwd")
    prefetch("w_cq", y_sb)
    a_conv = _mm(y_conv, wt("w_conv_out", y_conv), name="conv_out")
    a_sb = _mm(y_sb, wt("w_attn_out", y_sb), name="attn_out")
    b_conv, b_sb = w["b_gate"][:, :d], w["b_gate"][:, d:]

    def merge(ac, asb, gcp, gsp, bc, bs):
        gc = _sigmoid(gcp.astype(F32) + bc)
        gs = _sigmoid(gsp.astype(F32) + bs)
        return gc * ac.astype(F32) + gs * asb.astype(F32)

    merged = _rowcall(merge, [_whole(a_conv), _whole(a_sb), (proj, 6, d), (proj, 7, d)], [b_conv, b_sb], [(d, BF16)],
                      tm=tm, name="merge")[0]
    prefetch("w_ffn2_gu", merged)
    h2 = _mm(merged, wt("w_o", merged), name="mix_out", out_dtype=F32, res=h1)
    hn = _rms_fwd(h2, w["g_cross"], "cross_norm", tm)
    mn = _rms_fwd(mem, w["g_mem"], "mem_norm", _pick(mem.shape[0], (256, 128)))
    qc = _mm(hn, wt("w_cq", hn), name="cross_q")
    kv = _mm(mn, wt("w_ckv", mn), name="cross_kv")
    oc = _xattn_fwd(qc, kv, tq, "xattn_fwd")
    h3 = _mm(oc, wt("w_co", oc), name="cross_out", out_dtype=F32, res=h2)
    n2, gu2, act2, h4 = ffn_fwd(h3, "g_ffn2", "w_ffn2_gu", "w_ffn2_down", "ffn2")

    def head(hb, tb, gb):
        xh, r = _xhat(hb)
        err = xh * gb - tb
        dy = err * (1.0 / d)
        dxh = dy * gb
        dx = r * (dxh - xh * jnp.mean(dxh * xh, axis=-1, keepdims=True))
        row_loss = 0.5 * jnp.mean(err * err, axis=-1, keepdims=True)
        return dx, 0.5 * dx, dy * xh, jnp.broadcast_to(row_loss, (row_loss.shape[0], LANES))

    dh4, dh4b, g["g_final"], loss_lanes = _rowcall(head, [_whole(h4), _whole(tgt)], [w["g_final"]], [(d, F32), (d, BF16)],
                                                   [d, LANES], tm=tm, name="loss_head")

    dh3, dh3b = ffn_bwd(dh4, dh4b, h3, (n2, gu2, act2), "g_ffn2", "w_ffn2_gu", "w_ffn2_down", "ffn2", copy_scale=1.0)
    g["w_co"] = _mm(oc, dh3b, ta=True, name="cross_dwco")
    doc = _mm(dh3b, w["w_co"], tb=True, name="cross_doc")
    dqc, dk, dv = _xattn_bwd(qc, kv, doc, tq, "xattn_bwd")
    dkv = jnp.concatenate([dk, dv], axis=1)
    g["w_cq"] = _mm(hn, dqc, ta=True, name="cross_dwcq")
    g["w_ckv"] = _mm(mn, dkv, ta=True, name="cross_dwckv")
    dhn = _mm(dqc, w["w_cq"], tb=True, name="cross_dhn", out_dtype=F32, after=emit("cross", g))
    dmn = _mm(dkv, w["w_ckv"], tb=True, name="cross_dmn", out_dtype=F32)
    g["g_mem"] = _rowcall(lambda dy, xb: dy * _xhat(xb)[0], [_whole(dmn), _whole(mem)], [], [], [d],
                          tm=_pick(mem.shape[0], (256, 128)), name="mem_dnorm")[0]
    dh2, dh2b, g["g_cross"] = _resid_rms_bwd(dh3, dhn, h2, w["g_cross"], "cross_dnorm", tm, after=tick("cross", dhn), copy_scale=1.0)

    g["w_o"] = _mm(merged, dh2b, ta=True, name="mix_dwo")
    dmerged = _mm(dh2b, w["w_o"], tb=True, name="mix_dmerged")

    def merge_bwd(dm, ac, asb, gcp, gsp, bc, bs):
        dm, ac, asb = dm.astype(F32), ac.astype(F32), asb.astype(F32)
        gc = _sigmoid(gcp.astype(F32) + bc)
        gs = _sigmoid(gsp.astype(F32) + bs)
        dgc = dm * ac * gc * (1.0 - gc)
        dgs = dm * asb * gs * (1.0 - gs)
        return dm * gc, dm * gs, dgc, dgs, dgc, dgs

    da_conv, da_sb, dgc, dgs, db_conv, db_sb = _rowcall(
        merge_bwd, [_whole(dmerged), _whole(a_conv), _whole(a_sb), (proj, 6, d), (proj, 7, d)], [b_conv, b_sb],
        [(d, BF16)] * 4, [d, d], tm=tm, name="merge_bwd")
    g["b_gate"] = jnp.concatenate([db_conv, db_sb], axis=1)
    g["w_conv_out"] = _mm(y_conv, da_conv, ta=True, name="conv_dwout")
    g["w_attn_out"] = _mm(y_sb, da_sb, ta=True, name="attn_dwout")
    dy_conv = _mm(da_conv, w["w_conv_out"], tb=True, name="conv_dy")
    dy_sb = _mm(da_sb, w["w_attn_out"], tb=True, name="attn_dy")
    dcb, dcc, dcx, g["conv_w"] = _conv_bwd(dy_conv, proj, w["conv_w"], d, tc, "conv_bwd")
    dq, dk_sb, dv_sb = _sb_bwd(proj, y_sb, dy_sb, heads, sb_cols, sb_tq, sb_tk, "sb_bwd")
    dproj = jnp.concatenate([dcb, dcc, dcx, dq, dk_sb, dv_sb, dgc, dgs], axis=1)
    g["w_in"] = _mm(u, dproj, ta=True, name="mix_dwin")
    du = _mm(dproj, w["w_in"], tb=True, name="mix_du", out_dtype=F32, after=emit("mix", g))
    dh1, dh1b, g["g_mix"] = _resid_rms_bwd(dh2, du, h1, w["g_mix"], "mix_dnorm", tm, after=tick("mix", du), copy_scale=0.5)
    dx, = ffn_bwd(dh1, dh1b, x, (n1, gu1, act1), "g_ffn1", "w_ffn1_gu", "w_ffn1_down", "ffn1")
    return loss_lanes, dx, g


MATS = (("w_ffn1_gu", "col"), ("w_ffn1_down", "row"), ("w_in", "col"), ("w_conv_out", "row"), ("w_attn_out", "row"),
        ("w_o", "row"), ("w_cq", "row"), ("w_ckv", "col"), ("w_co", "row"), ("w_ffn2_gu", "col"), ("w_ffn2_down", "row"))
VECS = ("g_ffn1", "g_mix", "g_cross", "g_mem", "g_ffn2", "g_final")
WEIGHTS = ("g_ffn1", "w_ffn1_gu", "w_ffn1_down", "g_mix", "w_in", "b_gate", "conv_w", "w_conv_out", "w_attn_out", "w_o",
           "g_cross", "g_mem", "w_cq", "w_ckv", "w_co", "g_ffn2", "w_ffn2_gu", "w_ffn2_down", "g_final")
CONV_ROWS = 8


def _full_shape(kind, r, c):
    return (r, N_CHIPS * c) if kind == "col" else (N_CHIPS * r, c)


def _piece(ref, kind, r, c, chip, half):
    hr = r // 2
    if kind == "col":
        return ref.at[pl.ds(pl.multiple_of(half * hr, 16), hr), pl.ds(pl.multiple_of(chip * c, LANES), c)]
    return ref.at[pl.ds(pl.multiple_of(chip * r + half * hr, 16), hr), :]


def _shard_of(ref, kind, r, c, chip):
    if kind == "col":
        return ref.at[:, pl.ds(pl.multiple_of(chip * c, LANES), c)]
    return ref.at[pl.ds(pl.multiple_of(chip * r, 16), r), :]


def _place():
    x, y, c = lax.axis_index("x"), lax.axis_index("y"), lax.axis_index("c")
    others = [(1 - x, y), (x, 1 - y), (1 - x, 1 - y)]
    return x, y, c, 2 * x + y, others


def _remote(src, dst, send_sem, recv_sem, to):
    return pltpu.make_async_remote_copy(src_ref=src, dst_ref=dst, send_sem=send_sem, recv_sem=recv_sem,
                                        device_id=to, device_id_type=MESH)


def _gather_conv(conv_shard):
    cc = conv_shard.shape[1]

    def body(conv_ref, conv_full, cs, cr, cl):
        x, y, c, me, others = _place()

        def cols(chip):
            return conv_full.at[:, pl.ds(pl.multiple_of(chip * cc, LANES), cc)]

        def conv(k, chip_from, to):
            return _remote(conv_ref, cols(chip_from), cs.at[k], cr.at[k], to)

        mine = pltpu.make_async_copy(conv_ref, cols(me), cl.at[0])
        mine.start()
        for k, (ox, oy) in enumerate(others):
            conv(k, me, (ox, oy, c)).start()
        for k, (ox, oy) in enumerate(others):
            conv(k, 2 * ox + oy, (x, y, c)).wait_recv()
            conv(k, me, (ox, oy, c)).wait_send()
        mine.wait()

    dma = pltpu.SemaphoreType.DMA
    return _pcall(
        body, name="gather_conv", in_specs=[ANY], out_specs=ANY,
        out_shape=jax.ShapeDtypeStruct((CONV_ROWS, N_CHIPS * cc), F32), scratch_shapes=[dma((3,)), dma((3,)), dma((1,))],
    )(conv_shard)


HBM = pl.BlockSpec(memory_space=pltpu.HBM)
SEM = pl.BlockSpec(memory_space=pltpu.SEMAPHORE)
EFFECT = pltpu.SideEffectType.DATAFLOW_SIDE_EFFECTING
TOKEN = (8, LANES)


def _split_start(name, plan, n_copies, srcs, lands, after=None):
    ns, nl = len(srcs), len(lands)
    n_in = ns + nl + (after is not None)

    def body(*refs):
        outs = refs[n_in:]
        sends, _ = plan(refs[:ns], refs[ns:ns + nl], outs[0], outs[1])
        for cp in sends:
            cp.start()
        outs[-1][...] = jnp.zeros(TOKEN, F32)

    held = [pltpu.HBM(a.shape, a.dtype) for a in (*srcs, *lands)]
    dma = pltpu.SemaphoreType.DMA((n_copies,))
    ins = [pltpu.with_memory_space_constraint(a, pltpu.HBM) for a in (*srcs, *lands)]
    outs = _pcall(
        body, name=name, in_specs=[HBM] * (ns + nl) + ([] if after is None else [ANY]),
        out_specs=(SEM, SEM, *[HBM] * (ns + nl), pl.BlockSpec(memory_space=pltpu.VMEM)),
        out_shape=(dma, dma, *held, jax.ShapeDtypeStruct(TOKEN, F32)),
        input_output_aliases={i: 2 + i for i in range(ns + nl)},
        compiler_params=pltpu.CompilerParams(has_side_effects=EFFECT),
    )(*ins, *([] if after is None else [after]))
    return outs[0], outs[1], list(outs[2:2 + ns]), list(outs[2 + ns:2 + ns + nl]), outs[-1]


def _split_wait(name, plan, send_sems, recv_sems, srcs, lands, after):
    ns, nl = len(srcs), len(lands)

    def body(*refs):
        sends, recvs = plan(refs[:ns], refs[ns:ns + nl], refs[ns + nl], refs[ns + nl + 1])
        for cp in sends:
            cp.wait_send()
        for cp in recvs:
            cp.wait_recv()

    outs = _pcall(
        body, name=name, in_specs=[HBM] * (ns + nl) + [SEM, SEM, ANY], out_specs=[HBM] * (ns + nl),
        out_shape=[pltpu.HBM(a.shape, a.dtype) for a in (*srcs, *lands)],
        input_output_aliases={i: i for i in range(ns + nl)},
        compiler_params=pltpu.CompilerParams(has_side_effects=EFFECT),
    )(*srcs, *lands, send_sems, recv_sems, after)
    return list(outs[:ns]), list(outs[ns:])


def _gather_plan(dims):
    def plan(shard_refs, full_refs, ss, rs):
        x, y, c, me, others = _place()
        sends, recvs = [], []
        for wi, (kind, r, cw) in enumerate(dims):
            half = shard_refs[wi].at[pl.ds(pl.multiple_of(c * (r // 2), 16), r // 2), :]
            for k, (ox, oy) in enumerate(others):
                sem = 4 * wi + k
                sends.append(_remote(half, _piece(full_refs[wi], kind, r, cw, me, c), ss.at[sem], rs.at[sem], (ox, oy, c)))
                recvs.append(_remote(half, _piece(full_refs[wi], kind, r, cw, 2 * ox + oy, c), ss.at[sem], rs.at[sem], (x, y, c)))
            sem = 4 * wi + 3
            own = _remote(shard_refs[wi], _shard_of(full_refs[wi], kind, r, cw, me), ss.at[sem], rs.at[sem], (x, y, 1 - c))
            sends.append(own)
            recvs.append(own)
        return sends, recvs

    return plan


def _forward_plan(dims):
    def plan(_, full_refs, ss, rs):
        x, y, c, _, others = _place()
        sends, recvs = [], []
        for wi, (kind, r, cw) in enumerate(dims):
            for k, (ox, oy) in enumerate(others):
                sem = 3 * wi + k
                mine = _piece(full_refs[wi], kind, r, cw, 2 * ox + oy, c)
                theirs = _piece(full_refs[wi], kind, r, cw, 2 * ox + oy, 1 - c)
                sends.append(_remote(mine, mine, ss.at[sem], rs.at[sem], (x, y, 1 - c)))
                recvs.append(_remote(theirs, theirs, ss.at[sem], rs.at[sem], (x, y, 1 - c)))
        return sends, recvs

    return plan


def _rs_cores_plan(dims):
    def plan(g_refs, land_refs, ss, rs):
        x, y, c, _, _ = _place()
        sends, recvs = [], []
        for wi, dm in enumerate(dims):
            for chip in range(N_CHIPS):
                sem = N_CHIPS * wi + chip
                sends.append(_remote(_piece(g_refs[wi], *dm, chip, 1 - c), land_refs[wi].at[chip], ss.at[sem], rs.at[sem], (x, y, 1 - c)))
                recvs.append(_remote(_piece(g_refs[wi], *dm, chip, c), land_refs[wi].at[chip], ss.at[sem], rs.at[sem], (x, y, 1 - c)))
        return sends, recvs

    return plan


def _share_plan(nw):
    def plan(_, buf_refs, ss, rs):
        x, y, c, _, _ = _place()
        sends = [_remote(buf_refs[wi].at[c], buf_refs[wi].at[c], ss.at[wi], rs.at[wi], (x, y, 1 - c)) for wi in range(nw)]
        recvs = [_remote(buf_refs[wi].at[1 - c], buf_refs[wi].at[1 - c], ss.at[wi], rs.at[wi], (x, y, 1 - c)) for wi in range(nw)]
        return sends, recvs

    return plan


def _small_plan():
    def plan(_, buf_refs, ss, rs):
        x, y, c = lax.axis_index("x"), lax.axis_index("y"), lax.axis_index("c")
        buf = buf_refs[0]
        sends, recvs = [], []
        for rel in range(1, N_DEV):
            peer = (x ^ (rel >> 2 & 1), y ^ (rel >> 1 & 1), c ^ (rel & 1))
            sends.append(_remote(buf.at[0], buf.at[rel], ss.at[rel - 1], rs.at[rel - 1], peer))
            recvs.append(_remote(buf.at[0], buf.at[rel], ss.at[rel - 1], rs.at[rel - 1], peer))
        return sends, recvs

    return plan


def _sum_small(buf, me, name):
    _, rows, n = buf.shape

    def body(me_ref, b_ref, o_ref):
        tot = b_ref[me_ref[0]]
        for dev in range(1, N_DEV):
            tot = tot + b_ref[dev ^ me_ref[0]]
        o_ref[...] = tot

    return _pcall(
        body, name=name, out_shape=jax.ShapeDtypeStruct((rows, n), F32),
        grid_spec=pltpu.PrefetchScalarGridSpec(
            num_scalar_prefetch=1, grid=(1,), in_specs=[pl.BlockSpec((N_DEV, rows, n), lambda i, m: (0, 0, 0))],
            out_specs=pl.BlockSpec((rows, n), lambda i, m: (0, 0))),
    )(me, buf)


def _rs_chips_plan(nw):
    def plan(p_refs, land_refs, ss, rs):
        x, y, c, me, others = _place()
        sends, recvs = [], []
        for wi in range(nw):
            for k, (ox, oy) in enumerate(others):
                sem = 3 * wi + k
                sends.append(_remote(p_refs[wi].at[2 * ox + oy], land_refs[wi].at[k], ss.at[sem], rs.at[sem], (ox, oy, c)))
                recvs.append(_remote(p_refs[wi].at[me], land_refs[wi].at[k], ss.at[sem], rs.at[sem], (x, y, c)))
        return sends, recvs

    return plan


def _rows_per_block(n, c, limit_bytes=1 << 20):
    best = None
    for tm in range(16, n + 1, 16):
        if n % tm == 0 and tm * c * 4 <= limit_bytes:
            best = tm
    return best or n


def _sum_cores(grad, got, kind, place, name):
    _, hr, cw = got.shape
    tm = _rows_per_block(hr, cw)
    nb = hr // tm

    def body(place_ref, g_ref, t_ref, o_ref):
        o_ref[...] = (g_ref[...].astype(F32) + t_ref[...].astype(F32)).astype(o_ref.dtype)

    if kind == "col":
        g_spec = pl.BlockSpec((tm, cw), lambda j, i, pr: (pr[0] * nb + i, j))
    else:
        g_spec = pl.BlockSpec((tm, cw), lambda j, i, pr: ((2 * j + pr[0]) * nb + i, 0))
    blk = pl.BlockSpec((None, tm, cw), lambda j, i, pr: (j, i, 0))
    return _pcall(
        body, name=name, out_shape=jax.ShapeDtypeStruct(got.shape, BF16),
        grid_spec=pltpu.PrefetchScalarGridSpec(num_scalar_prefetch=1, grid=(N_CHIPS, nb), in_specs=[g_spec, blk], out_specs=blk),
        compiler_params=_params("parallel", "parallel"),
    )(place, grad, got)


def _sum_chips(parts, got, place, name):
    _, n, cw = got.shape
    tm = _rows_per_block(n, cw)

    def body(place_ref, p_ref, g_ref, o_ref):
        tot = p_ref[...].astype(F32)
        for k in range(3):
            tot = tot + g_ref[k].astype(F32)
        o_ref[...] = tot

    return _pcall(
        body, name=name, out_shape=jax.ShapeDtypeStruct((2, n, cw), F32),
        grid_spec=pltpu.PrefetchScalarGridSpec(
            num_scalar_prefetch=1, grid=(n // tm,),
            in_specs=[pl.BlockSpec((None, tm, cw), lambda i, pr: (pr[1], i, 0)), pl.BlockSpec((3, tm, cw), lambda i, pr: (0, i, 0))],
            out_specs=pl.BlockSpec((None, tm, cw), lambda i, pr: (pr[0], i, 0))),
        compiler_params=_params("parallel"),
    )(place, parts, got)


def _adamw(g, w, m, v, name):
    n, c = g.shape
    c1 = 1.0 - ADAM_B1 ** ADAM_STEP
    c2 = 1.0 - ADAM_B2 ** ADAM_STEP

    def fn(gb, wb, mb, vb):
        m_new = ADAM_B1 * mb + (1.0 - ADAM_B1) * gb
        v_new = ADAM_B2 * vb + (1.0 - ADAM_B2) * (gb * gb)
        delta = -ADAM_LR * ((m_new / c1) / (jnp.sqrt(v_new / c2) + ADAM_EPS) + ADAM_WD * wb)
        return gb, delta, m_new, v_new

    tm = _rows_per_block(n, c) if n % 16 == 0 else n
    return _rowcall(fn, [_whole(g), _whole(w), _whole(m), _whole(v)], [], [(c, F32)] * 4, tm=tm, name=name)


PACK_ROWS = 16


def _pack_rows(parts, width, name, after=None):
    assert sum(p.shape[0] for p in parts) <= PACK_ROWS

    def body(*refs):
        out_ref = refs[-1]
        out_ref[...] = jnp.zeros_like(out_ref)
        at = 0
        for r in refs[:len(parts)]:
            k, n = r.shape
            if n == width:
                out_ref[at:at + k, :] = r[...]
            else:
                out_ref[at:at + k, :] = jnp.broadcast_to(r[:, :1], (k, width))
            at += k

    vm = pl.BlockSpec(memory_space=pltpu.VMEM)
    return _pcall(body, name=name, in_specs=[vm] * len(parts) + ([] if after is None else [ANY]), out_specs=vm,
                  out_shape=jax.ShapeDtypeStruct((PACK_ROWS, width), F32))(*parts, *([] if after is None else [after]))


def _cast_shard(wm, name, after):
    n, c = wm.shape
    return _rowcall(lambda v: v, [_whole(wm)], [], [(c, BF16)], tm=_rows_per_block(n, c), name=name, after=after)[0]


GATHER_GROUPS = (
    ("w_ffn1_gu",), ("w_ffn1_down",), ("w_in",), ("w_conv_out", "w_attn_out", "w_o"), ("w_cq", "w_ckv", "w_co"),
    ("w_ffn2_gu", "w_ffn2_down"),
)
REDUCE_GROUPS = {
    "ffn2": ("w_ffn2_down", "w_ffn2_gu"),
    "cross": ("w_co", "w_cq", "w_ckv"),
    "mix": ("w_o", "w_conv_out", "w_attn_out", "w_in"),
    "ffn1": ("w_ffn1_down", "w_ffn1_gu"),
}
KIND = dict(MATS)


def _step(x, mem, tgt, wts, m_in, v_in):
    d = x.shape[-1]
    cc = wts["conv_w"].shape[1]
    place = jnp.stack([lax.axis_index("c"), 2 * lax.axis_index("x") + lax.axis_index("y")]).astype(jnp.int32)
    dims = {n: (kind, *wts[n].shape) for n, kind in MATS}

    conv_pad = jnp.pad(wts["conv_w"], ((0, CONV_ROWS - CONV_K), (0, 0)))
    conv_full = _gather_conv(conv_pad)
    w = {"conv_w": conv_full[:CONV_K]}
    for n in VECS + ("b_gate",):
        w[n] = wts[n].reshape(1, -1)
    flying, token = {}, conv_full
    for names in GATHER_GROUPS:
        gd = [dims[n] for n in names]
        shards = [_cast_shard(wts[n], "cast_" + n, token) for n in names]
        lands = [lax.empty(_full_shape(*dm), BF16) for dm in gd]
        plan = _gather_plan(gd)
        ss, rs, srcs, lands, token = _split_start("gather_start_" + names[0], plan, 4 * len(names), shards, lands, token)
        flying.update({n: (names, plan, ss, rs, srcs, lands, gd) for n in names})

    passing = {}

    def prefetch(name, after):
        if name not in passing:
            names, plan, ss, rs, srcs, lands, gd = flying[name]
            _, lands = _split_wait("gather_wait_" + names[0], plan, ss, rs, srcs, lands, after)
            plan = _forward_plan(gd)
            ss, rs, _, lands, _ = _split_start("forward_start_" + names[0], plan, 3 * len(names), [], lands)
            passing.update({n: (names, plan, ss, rs, lands) for n in names})

    def fetch(name, after):
        prefetch(name, after)
        names, plan, ss, rs, lands = passing[name]
        _, lands = _split_wait("forward_wait_" + names[0], plan, ss, rs, [], lands, after)
        return dict(zip(names, lands))

    swapping, sent = {}, {}

    def emit(tag, g):
        names = REDUCE_GROUPS[tag]
        gd = [dims[n] for n in names]
        lands = [lax.empty((N_CHIPS, r // 2, cw), BF16) for (_, r, cw) in gd]
        plan = _rs_cores_plan(gd)
        ss, rs, srcs, lands, tok = _split_start("rs_cores_start_" + tag, plan, N_CHIPS * len(names), [g[n] for n in names], lands)
        swapping[tag] = (plan, ss, rs, srcs, lands)
        return tok

    def tick(tag, after):
        names = REDUCE_GROUPS[tag]
        plan, ss, rs, srcs, lands = swapping[tag]
        mine, got = _split_wait("rs_cores_wait_" + tag, plan, ss, rs, srcs, lands, after)
        parts = [_sum_cores(gm, t, KIND[n], place, "sum_cores_" + n) for n, gm, t in zip(names, mine, got)]
        lands = [lax.empty((3, *p.shape[1:]), BF16) for p in parts]
        plan = _rs_chips_plan(len(names))
        ss, rs, srcs, lands, tok = _split_start("rs_chips_start_" + tag, plan, 3 * len(names), parts, lands)
        sent[tag] = (plan, ss, rs, srcs, lands)
        return tok

    loss_lanes, dx, g = _local_step(x[0], mem[0], tgt[0], w, fetch, prefetch, emit, tick, token)

    rows = [g[n] for n in VECS] + [g["b_gate"][:, :d], g["b_gate"][:, d:], g["conv_w"], loss_lanes]
    packed = _pack_rows(rows, d, "pack_small")
    small = jnp.concatenate([packed[None], jnp.zeros((N_DEV - 1, *packed.shape), F32)], axis=0)
    small_plan = _small_plan()
    small_ss, small_rs, _, small, after = _split_start("small_start", small_plan, N_DEV - 1, [], [small])

    grads, out = {}, {}

    def update(n):
        shape = wts[n].shape
        as2d = (lambda a: a.reshape(1, -1)) if len(shape) == 1 else (lambda a: a)
        return [r.reshape(shape) for r in _adamw(grads[n], as2d(wts[n]), as2d(m_in[n]), as2d(v_in[n]), "adamw_" + n)]

    def finish(sharing, after):
        tag, names, plan, ss, rs, halves = sharing
        _, both = _split_wait("share_wait_" + tag, plan, ss, rs, [], halves, after)
        for n, b in zip(names, both):
            grads[n] = b.reshape(-1, b.shape[-1])
            out[n] = update(n)
        return out[names[-1]][1]

    sharing = None
    for tag, names in REDUCE_GROUPS.items():
        plan, ss, rs, srcs, lands = sent[tag]
        parts, landed = _split_wait("rs_chips_wait_" + tag, plan, ss, rs, srcs, lands, after)
        halves = [_sum_chips(p, t, place, "sum_chips_" + n) for n, p, t in zip(names, parts, landed)]
        plan = _share_plan(len(names))
        ss, rs, _, halves, after = _split_start("share_start_" + tag, plan, len(names), [], halves)
        if sharing is not None:
            after = finish(sharing, after)
        sharing = (tag, names, plan, ss, rs, halves)
    after = finish(sharing, after)

    _, small = _split_wait("small_wait", small_plan, small_ss, small_rs, [], small, after)
    me = (4 * lax.axis_index("x") + 2 * lax.axis_index("y") + lax.axis_index("c")).astype(jnp.int32).reshape(1)
    red = _sum_small(small[0], me, "sum_small")
    grads.update({n: red[i:i + 1] for i, n in enumerate(VECS)})
    nv = len(VECS)
    grads["b_gate"] = jnp.concatenate([red[nv:nv + 1], red[nv + 1:nv + 2]], axis=1)
    chip = 2 * lax.axis_index("x") + lax.axis_index("y")
    grads["conv_w"] = lax.dynamic_slice_in_dim(red[nv + 2:nv + 2 + CONV_K], chip * cc, cc, axis=1)
    loss = red[nv + 2 + CONV_K, 0]
    out.update({n: update(n) for n in WEIGHTS if n not in KIND})
    return (loss, dx[None], *[out[n][0] for n in WEIGHTS], *[out[n][1] for n in WEIGHTS],
            *[out[n][2] for n in WEIGHTS], *[out[n][3] for n in WEIGHTS])


def kernel(x, mem, g_ffn1, w_ffn1_gu, w_ffn1_down, g_mix, w_in, b_gate, conv_w, w_conv_out, w_attn_out, w_o, g_cross, g_mem, w_cq, w_ckv, w_co, g_ffn2, w_ffn2_gu, w_ffn2_down, g_final, loss_target, m_g_ffn1, m_w_ffn1_gu, m_w_ffn1_down, m_g_mix, m_w_in, m_b_gate, m_conv_w, m_w_conv_out, m_w_attn_out, m_w_o, m_g_cross, m_g_mem, m_w_cq, m_w_ckv, m_w_co, m_g_ffn2, m_w_ffn2_gu, m_w_ffn2_down, m_g_final, v_g_ffn1, v_w_ffn1_gu, v_w_ffn1_down, v_g_mix, v_w_in, v_b_gate, v_conv_w, v_w_conv_out, v_w_attn_out, v_w_o, v_g_cross, v_g_mem, v_w_cq, v_w_ckv, v_w_co, v_g_ffn2, v_w_ffn2_gu, v_w_ffn2_down, v_g_final):
    given = dict(locals())
    wts = {n: given[n] for n in WEIGHTS}
    m_in = {n: given["m_" + n] for n in WEIGHTS}
    v_in = {n: given["v_" + n] for n in WEIGHTS}
    return _step(x, mem, loss_target, wts, m_in, v_in)
```

```python
import functools

import jax
import jax.numpy as jnp
from jax import lax
from jax.experimental import pallas as pl
from jax.experimental.pallas import tpu as pltpu

F32 = jnp.float32
BF16 = jnp.bfloat16
MESH = pl.DeviceIdType.MESH

V7X_VMEM_LIMIT_BYTES = 48 * 1024 * 1024
MM_VMEM_BUDGET_BYTES = 36 * 1024 * 1024
MM_WHOLE_K = 2816
LANES = 128
SB_HEAD_DIM = 128
X_HEADS = 4
CONV_K = 3
RMS_EPS = 1e-6
N_CHIPS = 4
N_DEV = 8
ADAM_LR, ADAM_B1, ADAM_B2, ADAM_EPS, ADAM_WD, ADAM_STEP = 0.001, 0.9, 0.999, 1e-08, 0.01, 10


ANY = pl.BlockSpec(memory_space=pl.ANY)


BULK_BYTES = 256 * 1024


def _bulk(a):
    return getattr(a, "ndim", 0) >= 2 and jnp.issubdtype(a.dtype, jnp.floating) and a.size * a.dtype.itemsize >= BULK_BYTES


def _out(shape, dtype):
    spec = jax.ShapeDtypeStruct(shape, dtype)
    return pltpu.HBM(shape, dtype) if _bulk(spec) else spec


def _pcall(body, pin=_bulk, **kw):
    call = pl.pallas_call(body, **kw)
    return lambda *args: call(*[pltpu.with_memory_space_constraint(a, pltpu.HBM) if pin(a) else a for a in args])


def _params(*sem):
    return pltpu.CompilerParams(dimension_semantics=sem, vmem_limit_bytes=V7X_VMEM_LIMIT_BYTES)


def _pick(dim, cands):
    for c in cands:
        if dim % c == 0:
            return c
    return dim


def _dot(a, b, ca, cb):
    return lax.dot_general(a, b, (((ca,), (cb,)), ((), ())), preferred_element_type=F32)


def _mm(a, b, *, name, ta=False, tb=False, out_dtype=BF16, res=None, alpha=1.0, tm=None, tn=None, tk=None, after=None):
    m, k = (a.shape[1], a.shape[0]) if ta else a.shape
    n = b.shape[0] if tb else b.shape[1]
    assert k == (b.shape[1] if tb else b.shape[0]), (a.shape, b.shape, ta, tb)
    if ta:
        tm = tm or _pick(m, (512, 256, 128))
        tn = tn or _pick(n, (1024, 512, 256, 128))
        tk = tk or (k if k <= MM_WHOLE_K else _pick(k, (1024, 512, 256, 128)))
    else:
        tk = tk or (k if k <= MM_WHOLE_K else _pick(k, (MM_WHOLE_K, 2048, 1024, 512, 256, 128)))
        tn = tn or _pick(n, (512, 1408, 256, 128))
        per_row = 2 * (tk * a.dtype.itemsize + tn * (jnp.dtype(out_dtype).itemsize + (0 if res is None else res.dtype.itemsize)))
        per_row += 4 * tn if tk < k else 0
        rows = (MM_VMEM_BUDGET_BYTES - 2 * tk * tn * b.dtype.itemsize) // per_row
        tm = tm or next((c for c in (2048, 1024, 512, 256, 128) if m % c == 0 and c <= rows), m)
    nk = k // tk
    assert m % tm == 0 and n % tn == 0 and k % tk == 0
    a_spec = pl.BlockSpec((tk, tm), lambda i, j, kk: (kk, i)) if ta else pl.BlockSpec((tm, tk), lambda i, j, kk: (i, kk))
    b_spec = pl.BlockSpec((tn, tk), lambda i, j, kk: (j, kk)) if tb else pl.BlockSpec((tk, tn), lambda i, j, kk: (kk, j))
    o_spec = pl.BlockSpec((tm, tn), lambda i, j, kk: (i, j))
    ca, cb = (0 if ta else 1), (1 if tb else 0)

    n_in = 2 + (res is not None) + (after is not None)

    def body(*refs):
        a_ref, b_ref = refs[:2]
        res_ref = refs[2] if res is not None else None
        o_ref = refs[n_in]
        scratch = refs[n_in + 1:]

        def finish(acc):
            val = acc if alpha == 1.0 else alpha * acc
            if res_ref is not None:
                val = res_ref[...].astype(F32) + val
            o_ref[...] = val.astype(o_ref.dtype)

        part = _dot(a_ref[...].astype(BF16), b_ref[...].astype(BF16), ca, cb)
        if nk == 1:
            finish(part)
        else:
            acc_ref = scratch[0]
            kk = pl.program_id(2)

            @pl.when(kk == 0)
            def _():
                acc_ref[...] = part

            @pl.when(kk > 0)
            def _():
                acc_ref[...] += part

            @pl.when(kk == nk - 1)
            def _():
                finish(acc_ref[...])

    ins = [a, b] + ([] if res is None else [res]) + ([] if after is None else [after])
    in_specs = [a_spec, b_spec] + ([] if res is None else [o_spec]) + ([] if after is None else [ANY])
    return _pcall(
        body, name=name, grid=(m // tm, n // tn, nk), in_specs=in_specs, out_specs=o_spec,
        out_shape=_out((m, n), out_dtype),
        scratch_shapes=[pltpu.VMEM((tm, tn), F32)] if nk > 1 else [],
        compiler_params=_params("parallel", "parallel", "arbitrary"),
    )(*ins)


def _rowcall(fn, rows, consts, outs, accs=(), *, tm, name, after=None):
    s = rows[0][0].shape[0]
    assert s % tm == 0
    n_read, n_out = len(rows) + len(consts), len(outs)
    n_in = n_read + (after is not None)

    def body(*refs):
        vals = fn(*[r[...] for r in refs[:n_read]])
        vals = vals if isinstance(vals, (tuple, list)) else (vals,)
        for o_ref, v in zip(refs[n_in:n_in + n_out], vals[:n_out]):
            o_ref[...] = v.astype(o_ref.dtype)
        if accs:
            first = pl.program_id(0) == 0
            for a_ref, v in zip(refs[n_in + n_out:], vals[n_out:]):
                tot = jnp.sum(v.astype(F32), axis=0, keepdims=True)

                @pl.when(first)
                def _(a_ref=a_ref, tot=tot):
                    a_ref[...] = tot

                @pl.when(jnp.logical_not(first))
                def _(a_ref=a_ref, tot=tot):
                    a_ref[...] += tot

    in_specs = [pl.BlockSpec((tm, w), lambda i, cb=cb: (i, cb)) for (_, cb, w) in rows]
    in_specs += [pl.BlockSpec(c.shape, lambda i: (0, 0)) for c in consts]
    in_specs += [] if after is None else [ANY]
    out_specs = [pl.BlockSpec((tm, w), lambda i: (i, 0)) for (w, _) in outs]
    out_specs += [pl.BlockSpec((1, w), lambda i: (0, 0)) for w in accs]
    out_shape = [_out((s, w), dt) for (w, dt) in outs]
    out_shape += [jax.ShapeDtypeStruct((1, w), F32) for w in accs]
    return _pcall(
        body, name=name, grid=(s // tm,), in_specs=in_specs, out_specs=out_specs, out_shape=out_shape,
        compiler_params=_params("arbitrary" if accs else "parallel"),
    )(*[r[0] for r in rows], *consts, *([] if after is None else [after]))


def _whole(a):
    return (a, 0, a.shape[1])


def _xhat(x):
    x = x.astype(F32)
    r = lax.rsqrt(jnp.mean(x * x, axis=-1, keepdims=True) + RMS_EPS)
    return x * r, r


def _rms_bwd(dy, x, g):
    xh, r = _xhat(x)
    dxh = dy.astype(F32) * g
    dx = r * (dxh - xh * jnp.mean(dxh * xh, axis=-1, keepdims=True))
    return dx, dy.astype(F32) * xh


def _sigmoid(x):
    return 1.0 / (1.0 + jnp.exp(-x))


def _rms_fwd(x, g, name, tm, after=None):
    d = x.shape[1]
    return _rowcall(lambda xb, gb: _xhat(xb)[0] * gb, [_whole(x)], [g], [(d, BF16)], tm=tm, name=name, after=after)[0]


def _swiglu_fwd(gu, name, tm):
    f = gu.shape[1] // 2

    def fn(gate, up):
        gate, up = gate.astype(F32), up.astype(F32)
        return gate * _sigmoid(gate) * up

    return _rowcall(fn, [(gu, 0, f), (gu, 1, f)], [], [(f, BF16)], tm=tm, name=name)[0]


def _swiglu_bwd(dact, gu, name, tm):
    f = gu.shape[1] // 2

    def fn(da, gate, up):
        da, gate, up = da.astype(F32), gate.astype(F32), up.astype(F32)
        sg = _sigmoid(gate)
        silu = gate * sg
        dgate = da * up * (sg + silu * (1.0 - sg))
        return jnp.concatenate([dgate, da * silu], axis=1)

    return _rowcall(fn, [_whole(dact), (gu, 0, f), (gu, 1, f)], [], [(2 * f, BF16)], tm=tm, name=name)[0]


def _resid_rms_bwd(dh, dn, x, g, name, tm, after=None, copy_scale=None):
    d = x.shape[1]

    def fn(dhb, dnb, xb, gb):
        dx, dg = _rms_bwd(dnb, xb, gb)
        tot = dhb.astype(F32) + dx
        return (tot, dg) if copy_scale is None else (tot, copy_scale * tot, dg)

    outs = [(d, F32)] + ([] if copy_scale is None else [(d, BF16)])
    return _rowcall(fn, [_whole(dh), _whole(dn), _whole(x)], [g], outs, [d], tm=tm, name=name, after=after)


def _shift_down(p, k):
    if k == 0:
        return p
    rows = lax.broadcasted_iota(jnp.int32, p.shape, 0)
    return jnp.where(rows >= k, pltpu.roll(p, k, 0), 0.0)


def _shift_up(p, k):
    if k == 0:
        return p
    s = p.shape[0]
    rows = lax.broadcasted_iota(jnp.int32, p.shape, 0)
    return jnp.where(rows < s - k, pltpu.roll(p, s - k, 0), 0.0)


def _conv_fwd(proj, conv_w, d, tc, name):
    s = proj.shape[0]
    nb = d // tc

    def body(cb_ref, cc_ref, cx_ref, w_ref, y_ref):
        p = cc_ref[...].astype(F32) * cx_ref[...].astype(F32)
        w = w_ref[...]
        acc = p * w[CONV_K - 1:CONV_K, :]
        for k in range(1, CONV_K):
            acc = acc + _shift_down(p, k) * w[CONV_K - 1 - k:CONV_K - k, :]
        y_ref[...] = (cb_ref[...].astype(F32) * acc).astype(y_ref.dtype)

    col = lambda off: pl.BlockSpec((s, tc), lambda j: (0, off * nb + j))
    return _pcall(
        body, name=name, grid=(nb,), in_specs=[col(0), col(1), col(2), pl.BlockSpec((CONV_K, tc), lambda j: (0, j))],
        out_specs=pl.BlockSpec((s, tc), lambda j: (0, j)), out_shape=_out((s, d), BF16),
        compiler_params=_params("parallel"),
    )(proj, proj, proj, conv_w)


def _conv_bwd(dy, proj, conv_w, d, tc, name):
    s = proj.shape[0]
    nb = d // tc

    def body(dy_ref, cb_ref, cc_ref, cx_ref, w_ref, dcb_ref, dcc_ref, dcx_ref, dw_ref):
        cc, cx = cc_ref[...].astype(F32), cx_ref[...].astype(F32)
        p = cc * cx
        w = w_ref[...]
        dyv = dy_ref[...].astype(F32)
        shifted = [_shift_down(p, CONV_K - 1 - k) for k in range(CONV_K)]
        conv = shifted[0] * w[0:1, :]
        for k in range(1, CONV_K):
            conv = conv + shifted[k] * w[k:k + 1, :]
        dcb_ref[...] = (dyv * conv).astype(dcb_ref.dtype)
        ds = dyv * cb_ref[...].astype(F32)
        dp = ds * w[CONV_K - 1:CONV_K, :]
        for k in range(1, CONV_K):
            dp = dp + _shift_up(ds, k) * w[CONV_K - 1 - k:CONV_K - k, :]
        dcc_ref[...] = (dp * cx).astype(dcc_ref.dtype)
        dcx_ref[...] = (dp * cc).astype(dcx_ref.dtype)
        for k in range(CONV_K):
            dw_ref[k:k + 1, :] = jnp.sum(ds * shifted[k], axis=0, keepdims=True)

    col = lambda off: pl.BlockSpec((s, tc), lambda j: (0, off * nb + j))
    blk = pl.BlockSpec((s, tc), lambda j: (0, j))
    wblk = pl.BlockSpec((CONV_K, tc), lambda j: (0, j))
    act = _out((s, d), BF16)
    return _pcall(
        body, name=name, grid=(nb,), in_specs=[blk, col(0), col(1), col(2), wblk],
        out_specs=[blk, blk, blk, wblk], out_shape=[act, act, act, jax.ShapeDtypeStruct((CONV_K, d), F32)],
        compiler_params=_params("parallel"),
    )(dy, proj, proj, proj, conv_w)


def _sb_tile(q, kj, scale, carry, tri, mask):
    z = _dot(q, kj, 1, 1) * scale
    lsz = jnp.minimum(z, 0.0) - jnp.log(1.0 + jnp.exp(-jnp.abs(z)))
    l1m = lsz - z
    if mask is not None:
        l1m = jnp.where(mask, l1m, 0.0)
    l1b = l1m.astype(BF16)
    a = jnp.exp(lsz + (carry + _dot(l1b, tri, 1, 0)))
    if mask is not None:
        a = jnp.where(mask, a, 0.0)
    return lsz, l1b, a.astype(BF16)


def _sb_masks(tq, tk):
    row = lax.broadcasted_iota(jnp.int32, (tq, tk), 0)
    col = lax.broadcasted_iota(jnp.int32, (tq, tk), 1)
    masks = [col + dj * tk < row for dj in range(tq // tk)]
    r2 = lax.broadcasted_iota(jnp.int32, (tk, tk), 0)
    c2 = lax.broadcasted_iota(jnp.int32, (tk, tk), 1)
    return masks, (r2 > c2).astype(BF16), (r2 < c2).astype(BF16)


def _sb_fwd(proj, heads, col0, tq, tk, name):
    s = proj.shape[0]
    dh = SB_HEAD_DIM
    nq, nd = s // tq, tq // tk
    scale = dh ** -0.5

    def body(q_ref, k_ref, v_ref, o_ref):
        i = pl.program_id(1)
        q = q_ref[...]
        masks, tri_right, _ = _sb_masks(tq, tk)

        def tile(j, carry, acc, mask):
            start = pl.multiple_of(j * tk, tk)
            kj = k_ref[pl.ds(start, tk), :]
            vj = v_ref[pl.ds(start, tk), :]
            _, l1b, ab = _sb_tile(q, kj, scale, carry, tri_right, mask)
            return carry + jnp.sum(l1b.astype(F32), axis=1, keepdims=True), acc + _dot(ab, vj, 1, 0)

        state = (jnp.zeros((tq, 1), F32), jnp.zeros((tq, dh), F32))
        for dj in reversed(range(nd)):
            state = tile(i * nd + dj, *state, masks[dj])
        state = lax.fori_loop(0, i * nd, lambda t, st: tile(i * nd - 1 - t, st[0], st[1], None), state)
        o_ref[...] = state[1]

    qspec = pl.BlockSpec((tq, dh), lambda h, i: (i, col0[0] + h))
    kspec = pl.BlockSpec((s, dh), lambda h, i: (0, col0[1] + h))
    vspec = pl.BlockSpec((s, dh), lambda h, i: (0, col0[2] + h))
    return _pcall(
        body, name=name, grid=(heads, nq), in_specs=[qspec, kspec, vspec],
        out_specs=pl.BlockSpec((tq, dh), lambda h, i: (i, h)), out_shape=_out((s, heads * dh), F32),
        compiler_params=_params("parallel", "parallel"),
    )(proj, proj, proj)


def _sb_bwd(proj, o, do, heads, col0, tq, tk, name):
    s = proj.shape[0]
    dh = SB_HEAD_DIM
    nq, nd = s // tq, tq // tk
    scale = dh ** -0.5

    def body(q_ref, k_ref, v_ref, o_ref, do_ref, dq_ref, dk_ref, dv_ref, dk_acc, dv_acc):
        i = pl.program_id(1)

        @pl.when(i == 0)
        def _():
            dk_acc[...] = jnp.zeros_like(dk_acc)
            dv_acc[...] = jnp.zeros_like(dv_acc)

        q = q_ref[...]
        dob = do_ref[...].astype(BF16)
        delta = jnp.sum(dob.astype(F32) * o_ref[...], axis=1, keepdims=True)
        masks, tri_right, tri_left = _sb_masks(tq, tk)

        def tile(j, carry_l, carry_g, dq, mask):
            start = pl.multiple_of(j * tk, tk)
            kj = k_ref[pl.ds(start, tk), :]
            vj = v_ref[pl.ds(start, tk), :]
            lsz, l1b, ab = _sb_tile(q, kj, scale, carry_l, tri_right, mask)
            g = _dot(dob, vj, 1, 1) * ab.astype(F32)
            carry_g = carry_g + jnp.sum(g, axis=1, keepdims=True)
            left = (delta - carry_g) + _dot(g.astype(BF16), tri_left, 1, 0)
            beta = jnp.exp(lsz)
            dz = g * (1.0 - beta) - left * beta
            if mask is not None:
                dz = jnp.where(mask, dz, 0.0)
            dzb = (dz * scale).astype(BF16)
            dk_acc[pl.ds(start, tk), :] += _dot(dzb, q, 0, 0)
            dv_acc[pl.ds(start, tk), :] += _dot(ab, dob, 0, 0)
            return carry_l + jnp.sum(l1b.astype(F32), axis=1, keepdims=True), carry_g, dq + _dot(dzb, kj, 1, 0)

        zero = jnp.zeros((tq, 1), F32)
        state = (zero, zero, jnp.zeros((tq, dh), F32))
        for dj in reversed(range(nd)):
            state = tile(i * nd + dj, *state, masks[dj])
        state = lax.fori_loop(0, i * nd, lambda t, st: tile(i * nd - 1 - t, st[0], st[1], st[2], None), state)
        dq_ref[...] = state[2].astype(dq_ref.dtype)

        @pl.when(i == nq - 1)
        def _():
            dk_ref[...] = dk_acc[...].astype(dk_ref.dtype)
            dv_ref[...] = dv_acc[...].astype(dv_ref.dtype)

    qspec = pl.BlockSpec((tq, dh), lambda h, i: (i, col0[0] + h))
    kspec = pl.BlockSpec((s, dh), lambda h, i: (0, col0[1] + h))
    vspec = pl.BlockSpec((s, dh), lambda h, i: (0, col0[2] + h))
    blk = pl.BlockSpec((tq, dh), lambda h, i: (i, h))
    full = pl.BlockSpec((s, dh), lambda h, i: (0, h))
    act = _out((s, heads * dh), BF16)
    return _pcall(
        body, name=name, grid=(heads, nq), in_specs=[qspec, kspec, vspec, blk, blk],
        out_specs=[blk, full, full], out_shape=[act, act, act],
        scratch_shapes=[pltpu.VMEM((s, dh), F32), pltpu.VMEM((s, dh), F32)],
        compiler_params=_params("parallel", "arbitrary"),
    )(proj, proj, proj, o, do)


def _xattn_probs(q, k, scale):
    sc = _dot(q, k, 1, 1) * scale
    e = jnp.exp(sc - jnp.max(sc, axis=1, keepdims=True))
    return e / jnp.sum(e, axis=1, keepdims=True)


def _xattn_fwd(qc, kv, tq, name):
    s, d = qc.shape
    m = kv.shape[0]
    dh = d // X_HEADS
    scale = dh ** -0.5

    def body(q_ref, k_ref, v_ref, o_ref):
        p = _xattn_probs(q_ref[...], k_ref[...], scale)
        o_ref[...] = _dot(p.astype(BF16), v_ref[...], 1, 0).astype(o_ref.dtype)

    blk = pl.BlockSpec((tq, dh), lambda h, i: (i, h))
    return _pcall(
        body, name=name, grid=(X_HEADS, s // tq),
        in_specs=[blk, pl.BlockSpec((m, dh), lambda h, i: (0, h)), pl.BlockSpec((m, dh), lambda h, i: (0, X_HEADS + h))],
        out_specs=blk, out_shape=_out((s, d), BF16), compiler_params=_params("parallel", "parallel"),
    )(qc, kv, kv)


def _xattn_bwd(qc, kv, do, tq, name):
    s, d = qc.shape
    m = kv.shape[0]
    dh = d // X_HEADS
    scale = dh ** -0.5
    nq = s // tq

    def body(q_ref, k_ref, v_ref, do_ref, dq_ref, dk_ref, dv_ref, dk_acc, dv_acc):
        i = pl.program_id(1)
        q, k, v = q_ref[...], k_ref[...], v_ref[...]
        dob = do_ref[...].astype(BF16)
        p = _xattn_probs(q, k, scale)
        pb = p.astype(BF16)
        dp = _dot(dob, v, 1, 1)
        ds = pb.astype(F32) * (dp - jnp.sum(dp * pb.astype(F32), axis=1, keepdims=True))
        dsb = (ds * scale).astype(BF16)
        dq_ref[...] = _dot(dsb, k, 1, 0).astype(dq_ref.dtype)
        dk_part = _dot(dsb, q, 0, 0)
        dv_part = _dot(pb, dob, 0, 0)

        @pl.when(i == 0)
        def _():
            dk_acc[...] = dk_part
            dv_acc[...] = dv_part

        @pl.when(i > 0)
        def _():
            dk_acc[...] += dk_part
            dv_acc[...] += dv_part

        @pl.when(i == nq - 1)
        def _():
            dk_ref[...] = dk_acc[...].astype(dk_ref.dtype)
            dv_ref[...] = dv_acc[...].astype(dv_ref.dtype)

    blk = pl.BlockSpec((tq, dh), lambda h, i: (i, h))
    kblk = pl.BlockSpec((m, dh), lambda h, i: (0, h))
    return _pcall(
        body, name=name, grid=(X_HEADS, nq),
        in_specs=[blk, kblk, pl.BlockSpec((m, dh), lambda h, i: (0, X_HEADS + h)), blk],
        out_specs=[blk, kblk, kblk],
        out_shape=[_out((s, d), BF16), _out((m, d), BF16), _out((m, d), BF16)],
        scratch_shapes=[pltpu.VMEM((m, dh), F32), pltpu.VMEM((m, dh), F32)],
        compiler_params=_params("parallel", "arbitrary"),
    )(qc, kv, kv, do)


def _local_step(x, mem, tgt, w, fetch=None, prefetch=None, emit=None, tick=None, after=None):
    fetch = fetch or (lambda name, after: {})
    prefetch = prefetch or (lambda name, after: None)
    emit = emit or (lambda group, g: None)
    tick = tick or (lambda group, after: None)
    w = dict(w)
    s, d = x.shape
    heads = d // SB_HEAD_DIM
    tm = _pick(s, (256, 128))
    tq = _pick(s, (256, 128))
    sb_tq, sb_tk = _pick(s, (512, 256, 128)), _pick(s, (256, 128))
    tc = _pick(d, (256, 128))
    g = {}

    def wt(name, after):
        if name not in w:
            w.update(fetch(name, after))
        return w[name]

    def ffn_fwd(h, gname, wgu, wdown, tag, after=None):
        n = _rms_fwd(h, w[gname], tag + "_norm", tm, after=after)
        gu = _mm(n, wt(wgu, n), name=tag + "_gu")
        prefetch(wdown, gu)
        act = _swiglu_fwd(gu, tag + "_act", tm)
        return n, gu, act, _mm(act, wt(wdown, act), name=tag + "_down", out_dtype=F32, res=h, alpha=0.5)

    def ffn_bwd(dh, dhb, h, saved, gname, wgu, wdown, tag, copy_scale=None):
        n, gu, act = saved
        g[wdown] = _mm(act, dhb, ta=True, name=tag + "_dwdown")
        dact = _mm(dhb, w[wdown], tb=True, name=tag + "_dact")
        dgu = _swiglu_bwd(dact, gu, tag + "_dgu", tm)
        g[wgu] = _mm(n, dgu, ta=True, name=tag + "_dwgu")
        dn = _mm(dgu, w[wgu], tb=True, name=tag + "_dn", out_dtype=F32, after=emit(tag, g))
        *dh_in, g[gname] = _resid_rms_bwd(dh, dn, h, w[gname], tag + "_dnorm", tm, after=tick(tag, dn), copy_scale=copy_scale)
        return dh_in

    n1, gu1, act1, h1 = ffn_fwd(x, "g_ffn1", "w_ffn1_gu", "w_ffn1_down", "ffn1", after)
    prefetch("w_in", h1)
    u = _rms_fwd(h1, w["g_mix"], "mix_norm", tm)
    proj = _mm(u, wt("w_in", u), name="mix_in")
    prefetch("w_conv_out", proj)
    nd = d // SB_HEAD_DIM
    y_conv = _conv_fwd(proj, w["conv_w"], d, tc, "conv_fwd")
    sb_cols = (3 * nd, 4 * nd, 5 * nd)
    y_sb = _sb_fwd(proj, heads, sb_cols, sb_tq, sb_tk, "sb_fwd")
    prefetch("w_cq", y_sb)
    a_conv = _mm(y_conv, wt("w_conv_out", y_conv), name="conv_out")
    a_sb = _mm(y_sb, wt("w_attn_out", y_sb), name="attn_out")
    b_conv, b_sb = w["b_gate"][:, :d], w["b_gate"][:, d:]

    def merge(ac, asb, gcp, gsp, bc, bs):
        gc = _sigmoid(gcp.astype(F32) + bc)
        gs = _sigmoid(gsp.astype(F32) + bs)
        return gc * ac.astype(F32) + gs * asb.astype(F32)

    merged = _rowcall(merge, [_whole(a_conv), _whole(a_sb), (proj, 6, d), (proj, 7, d)], [b_conv, b_sb], [(d, BF16)],
                      tm=tm, name="merge")[0]
    prefetch("w_ffn2_gu", merged)
    h2 = _mm(merged, wt("w_o", merged), name="mix_out", out_dtype=F32, res=h1)
    hn = _rms_fwd(h2, w["g_cross"], "cross_norm", tm)
    mn = _rms_fwd(mem, w["g_mem"], "mem_norm", _pick(mem.shape[0], (256, 128)))
    qc = _mm(hn, wt("w_cq", hn), name="cross_q")
    kv = _mm(mn, wt("w_ckv", mn), name="cross_kv")
    oc = _xattn_fwd(qc, kv, tq, "xattn_fwd")
    h3 = _mm(oc, wt("w_co", oc), name="cross_out", out_dtype=F32, res=h2)
    n2, gu2, act2, h4 = ffn_fwd(h3, "g_ffn2", "w_ffn2_gu", "w_ffn2_down", "ffn2")

    def head(hb, tb, gb):
        xh, r = _xhat(hb)
        err = xh * gb - tb
        dy = err * (1.0 / d)
        dxh = dy * gb
        dx = r * (dxh - xh * jnp.mean(dxh * xh, axis=-1, keepdims=True))
        row_loss = 0.5 * jnp.mean(err * err, axis=-1, keepdims=True)
        return dx, 0.5 * dx, dy * xh, jnp.broadcast_to(row_loss, (row_loss.shape[0], LANES))

    dh4, dh4b, g["g_final"], loss_lanes = _rowcall(head, [_whole(h4), _whole(tgt)], [w["g_final"]], [(d, F32), (d, BF16)],
                                                   [d, LANES], tm=tm, name="loss_head")

    dh3, dh3b = ffn_bwd(dh4, dh4b, h3, (n2, gu2, act2), "g_ffn2", "w_ffn2_gu", "w_ffn2_down", "ffn2", copy_scale=1.0)
    g["w_co"] = _mm(oc, dh3b, ta=True, name="cross_dwco")
    doc = _mm(dh3b, w["w_co"], tb=True, name="cross_doc")
    dqc, dk, dv = _xattn_bwd(qc, kv, doc, tq, "xattn_bwd")
    dkv = jnp.concatenate([dk, dv], axis=1)
    g["w_cq"] = _mm(hn, dqc, ta=True, name="cross_dwcq")
    g["w_ckv"] = _mm(mn, dkv, ta=True, name="cross_dwckv")
    dhn = _mm(dqc, w["w_cq"], tb=True, name="cross_dhn", out_dtype=F32, after=emit("cross", g))
    dmn = _mm(dkv, w["w_ckv"], tb=True, name="cross_dmn", out_dtype=F32)
    g["g_mem"] = _rowcall(lambda dy, xb: dy * _xhat(xb)[0], [_whole(dmn), _whole(mem)], [], [], [d],
                          tm=_pick(mem.shape[0], (256, 128)), name="mem_dnorm")[0]
    dh2, dh2b, g["g_cross"] = _resid_rms_bwd(dh3, dhn, h2, w["g_cross"], "cross_dnorm", tm, after=tick("cross", dhn), copy_scale=1.0)

    g["w_o"] = _mm(merged, dh2b, ta=True, name="mix_dwo")
    dmerged = _mm(dh2b, w["w_o"], tb=True, name="mix_dmerged")

    def merge_bwd(dm, ac, asb, gcp, gsp, bc, bs):
        dm, ac, asb = dm.astype(F32), ac.astype(F32), asb.astype(F32)
        gc = _sigmoid(gcp.astype(F32) + bc)
        gs = _sigmoid(gsp.astype(F32) + bs)
        dgc = dm * ac * gc * (1.0 - gc)
        dgs = dm * asb * gs * (1.0 - gs)
        return dm * gc, dm * gs, dgc, dgs, dgc, dgs

    da_conv, da_sb, dgc, dgs, db_conv, db_sb = _rowcall(
        merge_bwd, [_whole(dmerged), _whole(a_conv), _whole(a_sb), (proj, 6, d), (proj, 7, d)], [b_conv, b_sb],
        [(d, BF16)] * 4, [d, d], tm=tm, name="merge_bwd")
    g["b_gate"] = jnp.concatenate([db_conv, db_sb], axis=1)
    g["w_conv_out"] = _mm(y_conv, da_conv, ta=True, name="conv_dwout")
    g["w_attn_out"] = _mm(y_sb, da_sb, ta=True, name="attn_dwout")
    dy_conv = _mm(da_conv, w["w_conv_out"], tb=True, name="conv_dy")
    dy_sb = _mm(da_sb, w["w_attn_out"], tb=True, name="attn_dy")
    dcb, dcc, dcx, g["conv_w"] = _conv_bwd(dy_conv, proj, w["conv_w"], d, tc, "conv_bwd")
    dq, dk_sb, dv_sb = _sb_bwd(proj, y_sb, dy_sb, heads, sb_cols, sb_tq, sb_tk, "sb_bwd")
    dproj = jnp.concatenate([dcb, dcc, dcx, dq, dk_sb, dv_sb, dgc, dgs], axis=1)
    g["w_in"] = _mm(u, dproj, ta=True, name="mix_dwin")
    du = _mm(dproj, w["w_in"], tb=True, name="mix_du", out_dtype=F32, after=emit("mix", g))
    dh1, dh1b, g["g_mix"] = _resid_rms_bwd(dh2, du, h1, w["g_mix"], "mix_dnorm", tm, after=tick("mix", du), copy_scale=0.5)
    dx, = ffn_bwd(dh1, dh1b, x, (n1, gu1, act1), "g_ffn1", "w_ffn1_gu", "w_ffn1_down", "ffn1")
    return loss_lanes, dx, g


MATS = (("w_ffn1_gu", "col"), ("w_ffn1_down", "row"), ("w_in", "col"), ("w_conv_out", "row"), ("w_attn_out", "row"),
        ("w_o", "row"), ("w_cq", "row"), ("w_ckv", "col"), ("w_co", "row"), ("w_ffn2_gu", "col"), ("w_ffn2_down", "row"))
VECS = ("g_ffn1", "g_mix", "g_cross", "g_mem", "g_ffn2", "g_final")
WEIGHTS = ("g_ffn1", "w_ffn1_gu", "w_ffn1_down", "g_mix", "w_in", "b_gate", "conv_w", "w_conv_out", "w_attn_out", "w_o",
           "g_cross", "g_mem", "w_cq", "w_ckv", "w_co", "g_ffn2", "w_ffn2_gu", "w_ffn2_down", "g_final")
CONV_ROWS = 8


def _full_shape(kind, r, c):
    return (r, N_CHIPS * c) if kind == "col" else (N_CHIPS * r, c)


def _piece(ref, kind, r, c, chip, half):
    hr = r // 2
    if kind == "col":
        return ref.at[pl.ds(pl.multiple_of(half * hr, 16), hr), pl.ds(pl.multiple_of(chip * c, LANES), c)]
    return ref.at[pl.ds(pl.multiple_of(chip * r + half * hr, 16), hr), :]


def _shard_of(ref, kind, r, c, chip):
    if kind == "col":
        return ref.at[:, pl.ds(pl.multiple_of(chip * c, LANES), c)]
    return ref.at[pl.ds(pl.multiple_of(chip * r, 16), r), :]


def _place():
    x, y, c = lax.axis_index("x"), lax.axis_index("y"), lax.axis_index("c")
    others = [(1 - x, y), (x, 1 - y), (1 - x, 1 - y)]
    return x, y, c, 2 * x + y, others


def _remote(src, dst, send_sem, recv_sem, to):
    return pltpu.make_async_remote_copy(src_ref=src, dst_ref=dst, send_sem=send_sem, recv_sem=recv_sem,
                                        device_id=to, device_id_type=MESH)


def _gather_conv(conv_shard):
    cc = conv_shard.shape[1]

    def body(conv_ref, conv_full, cs, cr, cl):
        x, y, c, me, others = _place()

        def cols(chip):
            return conv_full.at[:, pl.ds(pl.multiple_of(chip * cc, LANES), cc)]

        def conv(k, chip_from, to):
            return _remote(conv_ref, cols(chip_from), cs.at[k], cr.at[k], to)

        mine = pltpu.make_async_copy(conv_ref, cols(me), cl.at[0])
        mine.start()
        for k, (ox, oy) in enumerate(others):
            conv(k, me, (ox, oy, c)).start()
        for k, (ox, oy) in enumerate(others):
            conv(k, 2 * ox + oy, (x, y, c)).wait_recv()
            conv(k, me, (ox, oy, c)).wait_send()
        mine.wait()

    dma = pltpu.SemaphoreType.DMA
    return _pcall(
        body, name="gather_conv", in_specs=[ANY], out_specs=ANY,
        out_shape=jax.ShapeDtypeStruct((CONV_ROWS, N_CHIPS * cc), F32), scratch_shapes=[dma((3,)), dma((3,)), dma((1,))],
    )(conv_shard)


HBM = pl.BlockSpec(memory_space=pltpu.HBM)
SEM = pl.BlockSpec(memory_space=pltpu.SEMAPHORE)
EFFECT = pltpu.SideEffectType.DATAFLOW_SIDE_EFFECTING
TOKEN = (8, LANES)


def _split_start(name, plan, n_copies, srcs, lands, after=None):
    ns, nl = len(srcs), len(lands)
    n_in = ns + nl + (after is not None)

    def body(*refs):
        outs = refs[n_in:]
        sends, _ = plan(refs[:ns], refs[ns:ns + nl], outs[0], outs[1])
        for cp in sends:
            cp.start()
        outs[-1][...] = jnp.zeros(TOKEN, F32)

    held = [pltpu.HBM(a.shape, a.dtype) for a in (*srcs, *lands)]
    dma = pltpu.SemaphoreType.DMA((n_copies,))
    held_ids = {id(a) for a in (*srcs, *lands)}
    outs = _pcall(
        body, pin=lambda a: id(a) in held_ids, name=name, in_specs=[HBM] * (ns + nl) + ([] if after is None else [ANY]),
        out_specs=(SEM, SEM, *[HBM] * (ns + nl), pl.BlockSpec(memory_space=pltpu.VMEM)),
        out_shape=(dma, dma, *held, jax.ShapeDtypeStruct(TOKEN, F32)),
        input_output_aliases={i: 2 + i for i in range(ns + nl)},
        compiler_params=pltpu.CompilerParams(has_side_effects=EFFECT),
    )(*srcs, *lands, *([] if after is None else [after]))
    return outs[0], outs[1], list(outs[2:2 + ns]), list(outs[2 + ns:2 + ns + nl]), outs[-1]


def _split_wait(name, plan, send_sems, recv_sems, srcs, lands, after):
    ns, nl = len(srcs), len(lands)

    def body(*refs):
        sends, recvs = plan(refs[:ns], refs[ns:ns + nl], refs[ns + nl], refs[ns + nl + 1])
        for cp in sends:
            cp.wait_send()
        for cp in recvs:
            cp.wait_recv()

    outs = _pcall(
        body, pin=lambda a: False, name=name, in_specs=[HBM] * (ns + nl) + [SEM, SEM, ANY], out_specs=[HBM] * (ns + nl),
        out_shape=[pltpu.HBM(a.shape, a.dtype) for a in (*srcs, *lands)],
        input_output_aliases={i: i for i in range(ns + nl)},
        compiler_params=pltpu.CompilerParams(has_side_effects=EFFECT),
    )(*srcs, *lands, send_sems, recv_sems, after)
    return list(outs[:ns]), list(outs[ns:])


def _gather_plan(dims):
    def plan(shard_refs, full_refs, ss, rs):
        x, y, c, me, others = _place()
        sends, recvs = [], []
        for wi, (kind, r, cw) in enumerate(dims):
            half = shard_refs[wi].at[pl.ds(pl.multiple_of(c * (r // 2), 16), r // 2), :]
            for k, (ox, oy) in enumerate(others):
                sem = 4 * wi + k
                sends.append(_remote(half, _piece(full_refs[wi], kind, r, cw, me, c), ss.at[sem], rs.at[sem], (ox, oy, c)))
                recvs.append(_remote(half, _piece(full_refs[wi], kind, r, cw, 2 * ox + oy, c), ss.at[sem], rs.at[sem], (x, y, c)))
            sem = 4 * wi + 3
            own = _remote(shard_refs[wi], _shard_of(full_refs[wi], kind, r, cw, me), ss.at[sem], rs.at[sem], (x, y, 1 - c))
            sends.append(own)
            recvs.append(own)
        return sends, recvs

    return plan


def _forward_plan(dims):
    def plan(_, full_refs, ss, rs):
        x, y, c, _, others = _place()
        sends, recvs = [], []
        for wi, (kind, r, cw) in enumerate(dims):
            for k, (ox, oy) in enumerate(others):
                sem = 3 * wi + k
                mine = _piece(full_refs[wi], kind, r, cw, 2 * ox + oy, c)
                theirs = _piece(full_refs[wi], kind, r, cw, 2 * ox + oy, 1 - c)
                sends.append(_remote(mine, mine, ss.at[sem], rs.at[sem], (x, y, 1 - c)))
                recvs.append(_remote(theirs, theirs, ss.at[sem], rs.at[sem], (x, y, 1 - c)))
        return sends, recvs

    return plan


def _rs_cores_plan(dims):
    def plan(g_refs, land_refs, ss, rs):
        x, y, c, _, _ = _place()
        sends, recvs = [], []
        for wi, dm in enumerate(dims):
            for chip in range(N_CHIPS):
                sem = N_CHIPS * wi + chip
                sends.append(_remote(_piece(g_refs[wi], *dm, chip, 1 - c), land_refs[wi].at[chip], ss.at[sem], rs.at[sem], (x, y, 1 - c)))
                recvs.append(_remote(_piece(g_refs[wi], *dm, chip, c), land_refs[wi].at[chip], ss.at[sem], rs.at[sem], (x, y, 1 - c)))
        return sends, recvs

    return plan


def _share_plan(nw):
    def plan(_, buf_refs, ss, rs):
        x, y, c, _, _ = _place()
        sends = [_remote(buf_refs[wi].at[c], buf_refs[wi].at[c], ss.at[wi], rs.at[wi], (x, y, 1 - c)) for wi in range(nw)]
        recvs = [_remote(buf_refs[wi].at[1 - c], buf_refs[wi].at[1 - c], ss.at[wi], rs.at[wi], (x, y, 1 - c)) for wi in range(nw)]
        return sends, recvs

    return plan


def _small_plan():
    def plan(_, buf_refs, ss, rs):
        x, y, c = lax.axis_index("x"), lax.axis_index("y"), lax.axis_index("c")
        buf = buf_refs[0]
        sends, recvs = [], []
        for rel in range(1, N_DEV):
            peer = (x ^ (rel >> 2 & 1), y ^ (rel >> 1 & 1), c ^ (rel & 1))
            sends.append(_remote(buf.at[0], buf.at[rel], ss.at[rel - 1], rs.at[rel - 1], peer))
            recvs.append(_remote(buf.at[0], buf.at[rel], ss.at[rel - 1], rs.at[rel - 1], peer))
        return sends, recvs

    return plan


def _sum_small(buf, me, name):
    _, rows, n = buf.shape

    def body(me_ref, b_ref, o_ref):
        tot = b_ref[me_ref[0]]
        for dev in range(1, N_DEV):
            tot = tot + b_ref[dev ^ me_ref[0]]
        o_ref[...] = tot

    return _pcall(
        body, name=name, out_shape=jax.ShapeDtypeStruct((rows, n), F32),
        grid_spec=pltpu.PrefetchScalarGridSpec(
            num_scalar_prefetch=1, grid=(1,), in_specs=[pl.BlockSpec((N_DEV, rows, n), lambda i, m: (0, 0, 0))],
            out_specs=pl.BlockSpec((rows, n), lambda i, m: (0, 0))),
    )(me, buf)


def _rs_chips_plan(nw):
    def plan(p_refs, land_refs, ss, rs):
        x, y, c, me, others = _place()
        sends, recvs = [], []
        for wi in range(nw):
            for k, (ox, oy) in enumerate(others):
                sem = 3 * wi + k
                sends.append(_remote(p_refs[wi].at[2 * ox + oy], land_refs[wi].at[k], ss.at[sem], rs.at[sem], (ox, oy, c)))
                recvs.append(_remote(p_refs[wi].at[me], land_refs[wi].at[k], ss.at[sem], rs.at[sem], (x, y, c)))
        return sends, recvs

    return plan


def _rows_per_block(n, c, limit_bytes=1 << 20):
    best = None
    for tm in range(16, n + 1, 16):
        if n % tm == 0 and tm * c * 4 <= limit_bytes:
            best = tm
    return best or n


def _sum_cores(grad, got, kind, place, name):
    _, hr, cw = got.shape
    tm = _rows_per_block(hr, cw)
    nb = hr // tm

    def body(place_ref, g_ref, t_ref, o_ref):
        o_ref[...] = (g_ref[...].astype(F32) + t_ref[...].astype(F32)).astype(o_ref.dtype)

    if kind == "col":
        g_spec = pl.BlockSpec((tm, cw), lambda j, i, pr: (pr[0] * nb + i, j))
    else:
        g_spec = pl.BlockSpec((tm, cw), lambda j, i, pr: ((2 * j + pr[0]) * nb + i, 0))
    blk = pl.BlockSpec((None, tm, cw), lambda j, i, pr: (j, i, 0))
    return _pcall(
        body, name=name, out_shape=_out(got.shape, BF16),
        grid_spec=pltpu.PrefetchScalarGridSpec(num_scalar_prefetch=1, grid=(N_CHIPS, nb), in_specs=[g_spec, blk], out_specs=blk),
        compiler_params=_params("parallel", "parallel"),
    )(place, grad, got)


def _sum_chips(parts, got, place, name):
    _, n, cw = got.shape
    tm = _rows_per_block(n, cw)

    def body(place_ref, p_ref, g_ref, o_ref):
        tot = p_ref[...].astype(F32)
        for k in range(3):
            tot = tot + g_ref[k].astype(F32)
        o_ref[...] = tot

    return _pcall(
        body, name=name, out_shape=_out((2, n, cw), F32),
        grid_spec=pltpu.PrefetchScalarGridSpec(
            num_scalar_prefetch=1, grid=(n // tm,),
            in_specs=[pl.BlockSpec((None, tm, cw), lambda i, pr: (pr[1], i, 0)), pl.BlockSpec((3, tm, cw), lambda i, pr: (0, i, 0))],
            out_specs=pl.BlockSpec((None, tm, cw), lambda i, pr: (pr[0], i, 0))),
        compiler_params=_params("parallel"),
    )(place, parts, got)


def _adamw(g, w, m, v, name):
    n, c = g.shape
    c1 = 1.0 - ADAM_B1 ** ADAM_STEP
    c2 = 1.0 - ADAM_B2 ** ADAM_STEP

    def fn(gb, wb, mb, vb):
        m_new = ADAM_B1 * mb + (1.0 - ADAM_B1) * gb
        v_new = ADAM_B2 * vb + (1.0 - ADAM_B2) * (gb * gb)
        delta = -ADAM_LR * ((m_new / c1) / (jnp.sqrt(v_new / c2) + ADAM_EPS) + ADAM_WD * wb)
        return gb, delta, m_new, v_new

    tm = _rows_per_block(n, c) if n % 16 == 0 else n
    return _rowcall(fn, [_whole(g), _whole(w), _whole(m), _whole(v)], [], [(c, F32)] * 4, tm=tm, name=name)


PACK_ROWS = 16


def _pack_rows(parts, width, name, after=None):
    assert sum(p.shape[0] for p in parts) <= PACK_ROWS

    def body(*refs):
        out_ref = refs[-1]
        out_ref[...] = jnp.zeros_like(out_ref)
        at = 0
        for r in refs[:len(parts)]:
            k, n = r.shape
            if n == width:
                out_ref[at:at + k, :] = r[...]
            else:
                out_ref[at:at + k, :] = jnp.broadcast_to(r[:, :1], (k, width))
            at += k

    vm = pl.BlockSpec(memory_space=pltpu.VMEM)
    return _pcall(body, name=name, in_specs=[vm] * len(parts) + ([] if after is None else [ANY]), out_specs=vm,
                  out_shape=jax.ShapeDtypeStruct((PACK_ROWS, width), F32))(*parts, *([] if after is None else [after]))


def _cast_shard(wm, name, after):
    n, c = wm.shape
    return _rowcall(lambda v: v, [_whole(wm)], [], [(c, BF16)], tm=_rows_per_block(n, c), name=name, after=after)[0]


GATHER_GROUPS = (
    ("w_ffn1_gu",), ("w_ffn1_down",), ("w_in",), ("w_conv_out", "w_attn_out", "w_o"), ("w_cq", "w_ckv", "w_co"),
    ("w_ffn2_gu", "w_ffn2_down"),
)
REDUCE_GROUPS = {
    "ffn2": ("w_ffn2_down", "w_ffn2_gu"),
    "cross": ("w_co", "w_cq", "w_ckv"),
    "mix": ("w_o", "w_conv_out", "w_attn_out", "w_in"),
    "ffn1": ("w_ffn1_down", "w_ffn1_gu"),
}
KIND = dict(MATS)


def _step(x, mem, tgt, wts, m_in, v_in):
    d = x.shape[-1]
    cc = wts["conv_w"].shape[1]
    place = jnp.stack([lax.axis_index("c"), 2 * lax.axis_index("x") + lax.axis_index("y")]).astype(jnp.int32)
    dims = {n: (kind, *wts[n].shape) for n, kind in MATS}

    conv_pad = jnp.pad(wts["conv_w"], ((0, CONV_ROWS - CONV_K), (0, 0)))
    conv_full = _gather_conv(conv_pad)
    w = {"conv_w": conv_full[:CONV_K]}
    for n in VECS + ("b_gate",):
        w[n] = wts[n].reshape(1, -1)
    flying, token = {}, conv_full
    for names in GATHER_GROUPS:
        gd = [dims[n] for n in names]
        shards = [_cast_shard(wts[n], "cast_" + n, token) for n in names]
        lands = [lax.empty(_full_shape(*dm), BF16) for dm in gd]
        plan = _gather_plan(gd)
        ss, rs, srcs, lands, token = _split_start("gather_start_" + names[0], plan, 4 * len(names), shards, lands, token)
        flying.update({n: (names, plan, ss, rs, srcs, lands, gd) for n in names})

    passing = {}

    def prefetch(name, after):
        if name not in passing:
            names, plan, ss, rs, srcs, lands, gd = flying[name]
            _, lands = _split_wait("gather_wait_" + names[0], plan, ss, rs, srcs, lands, after)
            plan = _forward_plan(gd)
            ss, rs, _, lands, _ = _split_start("forward_start_" + names[0], plan, 3 * len(names), [], lands)
            passing.update({n: (names, plan, ss, rs, lands) for n in names})

    def fetch(name, after):
        prefetch(name, after)
        names, plan, ss, rs, lands = passing[name]
        _, lands = _split_wait("forward_wait_" + names[0], plan, ss, rs, [], lands, after)
        return dict(zip(names, lands))

    swapping, sent = {}, {}

    def emit(tag, g):
        names = REDUCE_GROUPS[tag]
        gd = [dims[n] for n in names]
        lands = [lax.empty((N_CHIPS, r // 2, cw), BF16) for (_, r, cw) in gd]
        plan = _rs_cores_plan(gd)
        ss, rs, srcs, lands, tok = _split_start("rs_cores_start_" + tag, plan, N_CHIPS * len(names), [g[n] for n in names], lands)
        swapping[tag] = (plan, ss, rs, srcs, lands)
        return tok

    def tick(tag, after):
        names = REDUCE_GROUPS[tag]
        plan, ss, rs, srcs, lands = swapping[tag]
        mine, got = _split_wait("rs_cores_wait_" + tag, plan, ss, rs, srcs, lands, after)
        parts = [_sum_cores(gm, t, KIND[n], place, "sum_cores_" + n) for n, gm, t in zip(names, mine, got)]
        lands = [lax.empty((3, *p.shape[1:]), BF16) for p in parts]
        plan = _rs_chips_plan(len(names))
        ss, rs, srcs, lands, tok = _split_start("rs_chips_start_" + tag, plan, 3 * len(names), parts, lands)
        sent[tag] = (plan, ss, rs, srcs, lands)
        return tok

    loss_lanes, dx, g = _local_step(x[0], mem[0], tgt[0], w, fetch, prefetch, emit, tick, token)

    rows = [g[n] for n in VECS] + [g["b_gate"][:, :d], g["b_gate"][:, d:], g["conv_w"], loss_lanes]
    packed = _pack_rows(rows, d, "pack_small")
    small = jnp.concatenate([packed[None], jnp.zeros((N_DEV - 1, *packed.shape), F32)], axis=0)
    small_plan = _small_plan()
    small_ss, small_rs, _, small, after = _split_start("small_start", small_plan, N_DEV - 1, [], [small])

    grads, out = {}, {}

    def update(n):
        shape = wts[n].shape
        as2d = (lambda a: a.reshape(1, -1)) if len(shape) == 1 else (lambda a: a)
        return [r.reshape(shape) for r in _adamw(grads[n], as2d(wts[n]), as2d(m_in[n]), as2d(v_in[n]), "adamw_" + n)]

    def finish(sharing, after):
        tag, names, plan, ss, rs, halves = sharing
        _, both = _split_wait("share_wait_" + tag, plan, ss, rs, [], halves, after)
        for n, b in zip(names, both):
            grads[n] = b.reshape(-1, b.shape[-1])
            out[n] = update(n)
        return out[names[-1]][1]

    sharing = None
    for tag, names in REDUCE_GROUPS.items():
        plan, ss, rs, srcs, lands = sent[tag]
        parts, landed = _split_wait("rs_chips_wait_" + tag, plan, ss, rs, srcs, lands, after)
        halves = [_sum_chips(p, t, place, "sum_chips_" + n) for n, p, t in zip(names, parts, landed)]
        plan = _share_plan(len(names))
        ss, rs, _, halves, after = _split_start("share_start_" + tag, plan, len(names), [], halves)
        if sharing is not None:
            after = finish(sharing, after)
        sharing = (tag, names, plan, ss, rs, halves)
    after = finish(sharing, after)

    _, small = _split_wait("small_wait", small_plan, small_ss, small_rs, [], small, after)
    me = (4 * lax.axis_index("x") + 2 * lax.axis_index("y") + lax.axis_index("c")).astype(jnp.int32).reshape(1)
    red = _sum_small(small[0], me, "sum_small")
    grads.update({n: red[i:i + 1] for i, n in enumerate(VECS)})
    nv = len(VECS)
    grads["b_gate"] = jnp.concatenate([red[nv:nv + 1], red[nv + 1:nv + 2]], axis=1)
    chip = 2 * lax.axis_index("x") + lax.axis_index("y")
    grads["conv_w"] = lax.dynamic_slice_in_dim(red[nv + 2:nv + 2 + CONV_K], chip * cc, cc, axis=1)
    loss = red[nv + 2 + CONV_K, 0]
    out.update({n: update(n) for n in WEIGHTS if n not in KIND})
    return (loss, dx[None], *[out[n][0] for n in WEIGHTS], *[out[n][1] for n in WEIGHTS],
            *[out[n][2] for n in WEIGHTS], *[out[n][3] for n in WEIGHTS])


def kernel(x, mem, g_ffn1, w_ffn1_gu, w_ffn1_down, g_mix, w_in, b_gate, conv_w, w_conv_out, w_attn_out, w_o, g_cross, g_mem, w_cq, w_ckv, w_co, g_ffn2, w_ffn2_gu, w_ffn2_down, g_final, loss_target, m_g_ffn1, m_w_ffn1_gu, m_w_ffn1_down, m_g_mix, m_w_in, m_b_gate, m_conv_w, m_w_conv_out, m_w_attn_out, m_w_o, m_g_cross, m_g_mem, m_w_cq, m_w_ckv, m_w_co, m_g_ffn2, m_w_ffn2_gu, m_w_ffn2_down, m_g_final, v_g_ffn1, v_w_ffn1_gu, v_w_ffn1_down, v_g_mix, v_w_in, v_b_gate, v_conv_w, v_w_conv_out, v_w_attn_out, v_w_o, v_g_cross, v_g_mem, v_w_cq, v_w_ckv, v_w_co, v_g_ffn2, v_w_ffn2_gu, v_w_ffn2_down, v_g_final):
    given = dict(locals())
    wts = {n: given[n] for n in WEIGHTS}
    m_in = {n: given["m_" + n] for n in WEIGHTS}
    v_in = {n: given["v_" + n] for n in WEIGHTS}
    return _step(x, mem, loss_target, wts, m_in, v_in)
```

```python
import functools

import jax
import jax.numpy as jnp
from jax import lax
from jax.experimental import pallas as pl
from jax.experimental.pallas import tpu as pltpu

F32 = jnp.float32
BF16 = jnp.bfloat16
MESH = pl.DeviceIdType.MESH

V7X_VMEM_LIMIT_BYTES = 48 * 1024 * 1024
MM_VMEM_BUDGET_BYTES = 36 * 1024 * 1024
MM_WHOLE_K = 2816
LANES = 128
SB_HEAD_DIM = 128
X_HEADS = 4
CONV_K = 3
RMS_EPS = 1e-6
N_CHIPS = 4
N_DEV = 8
ADAM_LR, ADAM_B1, ADAM_B2, ADAM_EPS, ADAM_WD, ADAM_STEP = 0.001, 0.9, 0.999, 1e-08, 0.01, 10


ANY = pl.BlockSpec(memory_space=pl.ANY)


def _pcall(body, **kw):
    return pl.pallas_call(body, **kw)


def _params(*sem):
    return pltpu.CompilerParams(dimension_semantics=sem, vmem_limit_bytes=V7X_VMEM_LIMIT_BYTES)


def _pick(dim, cands):
    for c in cands:
        if dim % c == 0:
            return c
    return dim


def _dot(a, b, ca, cb):
    return lax.dot_general(a, b, (((ca,), (cb,)), ((), ())), preferred_element_type=F32)


def _mm(a, b, *, name, ta=False, tb=False, out_dtype=BF16, res=None, alpha=1.0, tm=None, tn=None, tk=None, after=None):
    m, k = (a.shape[1], a.shape[0]) if ta else a.shape
    n = b.shape[0] if tb else b.shape[1]
    assert k == (b.shape[1] if tb else b.shape[0]), (a.shape, b.shape, ta, tb)
    if ta:
        tm = tm or _pick(m, (512, 256, 128))
        tn = tn or _pick(n, (1024, 512, 256, 128))
        tk = tk or (k if k <= MM_WHOLE_K else _pick(k, (1024, 512, 256, 128)))
    else:
        tk = tk or (k if k <= MM_WHOLE_K else _pick(k, (MM_WHOLE_K, 2048, 1024, 512, 256, 128)))
        tn = tn or _pick(n, (512, 1408, 256, 128) if tk == k else (1024, 512, 256, 128))
        per_row = 2 * (tk * a.dtype.itemsize + tn * (jnp.dtype(out_dtype).itemsize + (0 if res is None else res.dtype.itemsize)))
        per_row += 4 * tn if tk < k else 0
        rows = (MM_VMEM_BUDGET_BYTES - 2 * tk * tn * b.dtype.itemsize) // per_row
        tm = tm or next((c for c in (2048, 1024, 512, 256, 128) if m % c == 0 and c <= rows), m)
    nk = k // tk
    assert m % tm == 0 and n % tn == 0 and k % tk == 0
    a_spec = pl.BlockSpec((tk, tm), lambda i, j, kk: (kk, i)) if ta else pl.BlockSpec((tm, tk), lambda i, j, kk: (i, kk))
    b_spec = pl.BlockSpec((tn, tk), lambda i, j, kk: (j, kk)) if tb else pl.BlockSpec((tk, tn), lambda i, j, kk: (kk, j))
    o_spec = pl.BlockSpec((tm, tn), lambda i, j, kk: (i, j))
    ca, cb = (0 if ta else 1), (1 if tb else 0)

    n_in = 2 + (res is not None) + (after is not None)

    def body(*refs):
        a_ref, b_ref = refs[:2]
        res_ref = refs[2] if res is not None else None
        o_ref = refs[n_in]
        scratch = refs[n_in + 1:]

        def finish(acc):
            val = acc if alpha == 1.0 else alpha * acc
            if res_ref is not None:
                val = res_ref[...].astype(F32) + val
            o_ref[...] = val.astype(o_ref.dtype)

        part = _dot(a_ref[...].astype(BF16), b_ref[...].astype(BF16), ca, cb)
        if nk == 1:
            finish(part)
        else:
            acc_ref = scratch[0]
            kk = pl.program_id(2)

            @pl.when(kk == 0)
            def _():
                acc_ref[...] = part

            @pl.when(kk > 0)
            def _():
                acc_ref[...] += part

            @pl.when(kk == nk - 1)
            def _():
                finish(acc_ref[...])

    ins = [a, b] + ([] if res is None else [res]) + ([] if after is None else [after])
    in_specs = [a_spec, b_spec] + ([] if res is None else [o_spec]) + ([] if after is None else [ANY])
    return _pcall(
        body, name=name, grid=(m // tm, n // tn, nk), in_specs=in_specs, out_specs=o_spec,
        out_shape=jax.ShapeDtypeStruct((m, n), out_dtype),
        scratch_shapes=[pltpu.VMEM((tm, tn), F32)] if nk > 1 else [],
        compiler_params=_params("parallel", "parallel", "arbitrary"),
    )(*ins)


def _rowcall(fn, rows, consts, outs, accs=(), *, tm, name, after=None):
    s = rows[0][0].shape[0]
    assert s % tm == 0
    n_read, n_out = len(rows) + len(consts), len(outs)
    n_in = n_read + (after is not None)

    def body(*refs):
        vals = fn(*[r[...] for r in refs[:n_read]])
        vals = vals if isinstance(vals, (tuple, list)) else (vals,)
        for o_ref, v in zip(refs[n_in:n_in + n_out], vals[:n_out]):
            o_ref[...] = v.astype(o_ref.dtype)
        if accs:
            first = pl.program_id(0) == 0
            for a_ref, v in zip(refs[n_in + n_out:], vals[n_out:]):
                tot = jnp.sum(v.astype(F32), axis=0, keepdims=True)

                @pl.when(first)
                def _(a_ref=a_ref, tot=tot):
                    a_ref[...] = tot

                @pl.when(jnp.logical_not(first))
                def _(a_ref=a_ref, tot=tot):
                    a_ref[...] += tot

    in_specs = [pl.BlockSpec((tm, w), lambda i, cb=cb: (i, cb)) for (_, cb, w) in rows]
    in_specs += [pl.BlockSpec(c.shape, lambda i: (0, 0)) for c in consts]
    in_specs += [] if after is None else [ANY]
    out_specs = [pl.BlockSpec((tm, w), lambda i: (i, 0)) for (w, _) in outs]
    out_specs += [pl.BlockSpec((1, w), lambda i: (0, 0)) for w in accs]
    out_shape = [jax.ShapeDtypeStruct((s, w), dt) for (w, dt) in outs]
    out_shape += [jax.ShapeDtypeStruct((1, w), F32) for w in accs]
    return _pcall(
        body, name=name, grid=(s // tm,), in_specs=in_specs, out_specs=out_specs, out_shape=out_shape,
        compiler_params=_params("arbitrary" if accs else "parallel"),
    )(*[r[0] for r in rows], *consts, *([] if after is None else [after]))


def _whole(a):
    return (a, 0, a.shape[1])


def _xhat(x):
    x = x.astype(F32)
    r = lax.rsqrt(jnp.mean(x * x, axis=-1, keepdims=True) + RMS_EPS)
    return x * r, r


def _rms_bwd(dy, x, g):
    xh, r = _xhat(x)
    dxh = dy.astype(F32) * g
    dx = r * (dxh - xh * jnp.mean(dxh * xh, axis=-1, keepdims=True))
    return dx, dy.astype(F32) * xh


def _sigmoid(x):
    return 1.0 / (1.0 + jnp.exp(-x))


def _rms_fwd(x, g, name, tm, after=None):
    d = x.shape[1]
    return _rowcall(lambda xb, gb: _xhat(xb)[0] * gb, [_whole(x)], [g], [(d, BF16)], tm=tm, name=name, after=after)[0]


def _swiglu_fwd(gu, name, tm):
    f = gu.shape[1] // 2

    def fn(gate, up):
        gate, up = gate.astype(F32), up.astype(F32)
        return gate * _sigmoid(gate) * up

    return _rowcall(fn, [(gu, 0, f), (gu, 1, f)], [], [(f, BF16)], tm=tm, name=name)[0]


def _swiglu_bwd(dact, gu, name, tm):
    f = gu.shape[1] // 2

    def fn(da, gate, up):
        da, gate, up = da.astype(F32), gate.astype(F32), up.astype(F32)
        sg = _sigmoid(gate)
        silu = gate * sg
        dgate = da * up * (sg + silu * (1.0 - sg))
        return jnp.concatenate([dgate, da * silu], axis=1)

    return _rowcall(fn, [_whole(dact), (gu, 0, f), (gu, 1, f)], [], [(2 * f, BF16)], tm=tm, name=name)[0]


def _resid_rms_bwd(dh, dn, x, g, name, tm, after=None, copy_scale=None):
    d = x.shape[1]

    def fn(dhb, dnb, xb, gb):
        dx, dg = _rms_bwd(dnb, xb, gb)
        tot = dhb.astype(F32) + dx
        return (tot, dg) if copy_scale is None else (tot, copy_scale * tot, dg)

    outs = [(d, F32)] + ([] if copy_scale is None else [(d, BF16)])
    return _rowcall(fn, [_whole(dh), _whole(dn), _whole(x)], [g], outs, [d], tm=tm, name=name, after=after)


def _shift_down(p, k):
    if k == 0:
        return p
    rows = lax.broadcasted_iota(jnp.int32, p.shape, 0)
    return jnp.where(rows >= k, pltpu.roll(p, k, 0), 0.0)


def _shift_up(p, k):
    if k == 0:
        return p
    s = p.shape[0]
    rows = lax.broadcasted_iota(jnp.int32, p.shape, 0)
    return jnp.where(rows < s - k, pltpu.roll(p, s - k, 0), 0.0)


def _conv_fwd(proj, conv_w, d, tc, name):
    s = proj.shape[0]
    nb = d // tc

    def body(cb_ref, cc_ref, cx_ref, w_ref, y_ref):
        p = cc_ref[...].astype(F32) * cx_ref[...].astype(F32)
        w = w_ref[...]
        acc = p * w[CONV_K - 1:CONV_K, :]
        for k in range(1, CONV_K):
            acc = acc + _shift_down(p, k) * w[CONV_K - 1 - k:CONV_K - k, :]
        y_ref[...] = (cb_ref[...].astype(F32) * acc).astype(y_ref.dtype)

    col = lambda off: pl.BlockSpec((s, tc), lambda j: (0, off * nb + j))
    return _pcall(
        body, name=name, grid=(nb,), in_specs=[col(0), col(1), col(2), pl.BlockSpec((CONV_K, tc), lambda j: (0, j))],
        out_specs=pl.BlockSpec((s, tc), lambda j: (0, j)), out_shape=jax.ShapeDtypeStruct((s, d), BF16),
        compiler_params=_params("parallel"),
    )(proj, proj, proj, conv_w)


def _conv_bwd(dy, proj, conv_w, d, tc, name):
    s = proj.shape[0]
    nb = d // tc

    def body(dy_ref, cb_ref, cc_ref, cx_ref, w_ref, dcb_ref, dcc_ref, dcx_ref, dw_ref):
        cc, cx = cc_ref[...].astype(F32), cx_ref[...].astype(F32)
        p = cc * cx
        w = w_ref[...]
        dyv = dy_ref[...].astype(F32)
        shifted = [_shift_down(p, CONV_K - 1 - k) for k in range(CONV_K)]
        conv = shifted[0] * w[0:1, :]
        for k in range(1, CONV_K):
            conv = conv + shifted[k] * w[k:k + 1, :]
        dcb_ref[...] = (dyv * conv).astype(dcb_ref.dtype)
        ds = dyv * cb_ref[...].astype(F32)
        dp = ds * w[CONV_K - 1:CONV_K, :]
        for k in range(1, CONV_K):
            dp = dp + _shift_up(ds, k) * w[CONV_K - 1 - k:CONV_K - k, :]
        dcc_ref[...] = (dp * cx).astype(dcc_ref.dtype)
        dcx_ref[...] = (dp * cc).astype(dcx_ref.dtype)
        for k in range(CONV_K):
            dw_ref[k:k + 1, :] = jnp.sum(ds * shifted[k], axis=0, keepdims=True)

    col = lambda off: pl.BlockSpec((s, tc), lambda j: (0, off * nb + j))
    blk = pl.BlockSpec((s, tc), lambda j: (0, j))
    wblk = pl.BlockSpec((CONV_K, tc), lambda j: (0, j))
    act = jax.ShapeDtypeStruct((s, d), BF16)
    return _pcall(
        body, name=name, grid=(nb,), in_specs=[blk, col(0), col(1), col(2), wblk],
        out_specs=[blk, blk, blk, wblk], out_shape=[act, act, act, jax.ShapeDtypeStruct((CONV_K, d), F32)],
        compiler_params=_params("parallel"),
    )(dy, proj, proj, proj, conv_w)


def _sb_tile(q, kj, scale, carry, tri, mask):
    z = _dot(q, kj, 1, 1) * scale
    lsz = jnp.minimum(z, 0.0) - jnp.log(1.0 + jnp.exp(-jnp.abs(z)))
    l1m = lsz - z
    if mask is not None:
        l1m = jnp.where(mask, l1m, 0.0)
    l1b = l1m.astype(BF16)
    a = jnp.exp(lsz + (carry + _dot(l1b, tri, 1, 0)))
    if mask is not None:
        a = jnp.where(mask, a, 0.0)
    return lsz, l1b, a.astype(BF16)


def _sb_masks(tq, tk):
    row = lax.broadcasted_iota(jnp.int32, (tq, tk), 0)
    col = lax.broadcasted_iota(jnp.int32, (tq, tk), 1)
    masks = [col + dj * tk < row for dj in range(tq // tk)]
    r2 = lax.broadcasted_iota(jnp.int32, (tk, tk), 0)
    c2 = lax.broadcasted_iota(jnp.int32, (tk, tk), 1)
    return masks, (r2 > c2).astype(BF16), (r2 < c2).astype(BF16)


def _sb_fwd(proj, heads, col0, tq, tk, name):
    s = proj.shape[0]
    dh = SB_HEAD_DIM
    nq, nd = s // tq, tq // tk
    scale = dh ** -0.5

    def body(q_ref, k_ref, v_ref, o_ref):
        i = pl.program_id(1)
        q = q_ref[...]
        masks, tri_right, _ = _sb_masks(tq, tk)

        def tile(j, carry, acc, mask):
            start = pl.multiple_of(j * tk, tk)
            kj = k_ref[pl.ds(start, tk), :]
            vj = v_ref[pl.ds(start, tk), :]
            _, l1b, ab = _sb_tile(q, kj, scale, carry, tri_right, mask)
            return carry + jnp.sum(l1b.astype(F32), axis=1, keepdims=True), acc + _dot(ab, vj, 1, 0)

        state = (jnp.zeros((tq, 1), F32), jnp.zeros((tq, dh), F32))
        for dj in reversed(range(nd)):
            state = tile(i * nd + dj, *state, masks[dj])
        state = lax.fori_loop(0, i * nd, lambda t, st: tile(i * nd - 1 - t, st[0], st[1], None), state)
        o_ref[...] = state[1]

    qspec = pl.BlockSpec((tq, dh), lambda h, i: (i, col0[0] + h))
    kspec = pl.BlockSpec((s, dh), lambda h, i: (0, col0[1] + h))
    vspec = pl.BlockSpec((s, dh), lambda h, i: (0, col0[2] + h))
    return _pcall(
        body, name=name, grid=(heads, nq), in_specs=[qspec, kspec, vspec],
        out_specs=pl.BlockSpec((tq, dh), lambda h, i: (i, h)), out_shape=jax.ShapeDtypeStruct((s, heads * dh), F32),
        compiler_params=_params("parallel", "parallel"),
    )(proj, proj, proj)


def _sb_bwd(proj, o, do, heads, col0, tq, tk, name):
    s = proj.shape[0]
    dh = SB_HEAD_DIM
    nq, nd = s // tq, tq // tk
    scale = dh ** -0.5

    def body(q_ref, k_ref, v_ref, o_ref, do_ref, dq_ref, dk_ref, dv_ref, dk_acc, dv_acc):
        i = pl.program_id(1)

        @pl.when(i == 0)
        def _():
            dk_acc[...] = jnp.zeros_like(dk_acc)
            dv_acc[...] = jnp.zeros_like(dv_acc)

        q = q_ref[...]
        dob = do_ref[...].astype(BF16)
        delta = jnp.sum(dob.astype(F32) * o_ref[...], axis=1, keepdims=True)
        masks, tri_right, tri_left = _sb_masks(tq, tk)

        def tile(j, carry_l, carry_g, dq, mask):
            start = pl.multiple_of(j * tk, tk)
            kj = k_ref[pl.ds(start, tk), :]
            vj = v_ref[pl.ds(start, tk), :]
            lsz, l1b, ab = _sb_tile(q, kj, scale, carry_l, tri_right, mask)
            g = _dot(dob, vj, 1, 1) * ab.astype(F32)
            carry_g = carry_g + jnp.sum(g, axis=1, keepdims=True)
            left = (delta - carry_g) + _dot(g.astype(BF16), tri_left, 1, 0)
            dz = g - jnp.exp(lsz) * (g + left)
            if mask is not None:
                dz = jnp.where(mask, dz, 0.0)
            dzb = dz.astype(BF16)
            dk_acc[pl.ds(start, tk), :] += _dot(dzb, q, 0, 0)
            dv_acc[pl.ds(start, tk), :] += _dot(ab, dob, 0, 0)
            return carry_l + jnp.sum(l1b.astype(F32), axis=1, keepdims=True), carry_g, dq + _dot(dzb, kj, 1, 0)

        zero = jnp.zeros((tq, 1), F32)
        state = (zero, zero, jnp.zeros((tq, dh), F32))
        for dj in reversed(range(nd)):
            state = tile(i * nd + dj, *state, masks[dj])
        state = lax.fori_loop(0, i * nd, lambda t, st: tile(i * nd - 1 - t, st[0], st[1], st[2], None), state)
        dq_ref[...] = (state[2] * scale).astype(dq_ref.dtype)

        @pl.when(i == nq - 1)
        def _():
            dk_ref[...] = (dk_acc[...] * scale).astype(dk_ref.dtype)
            dv_ref[...] = dv_acc[...].astype(dv_ref.dtype)

    qspec = pl.BlockSpec((tq, dh), lambda h, i: (i, col0[0] + h))
    kspec = pl.BlockSpec((s, dh), lambda h, i: (0, col0[1] + h))
    vspec = pl.BlockSpec((s, dh), lambda h, i: (0, col0[2] + h))
    blk = pl.BlockSpec((tq, dh), lambda h, i: (i, h))
    full = pl.BlockSpec((s, dh), lambda h, i: (0, h))
    act = jax.ShapeDtypeStruct((s, heads * dh), BF16)
    return _pcall(
        body, name=name, grid=(heads, nq), in_specs=[qspec, kspec, vspec, blk, blk],
        out_specs=[blk, full, full], out_shape=[act, act, act],
        scratch_shapes=[pltpu.VMEM((s, dh), F32), pltpu.VMEM((s, dh), F32)],
        compiler_params=_params("parallel", "arbitrary"),
    )(proj, proj, proj, o, do)


def _xattn_probs(q, k, scale):
    sc = _dot(q, k, 1, 1) * scale
    e = jnp.exp(sc - jnp.max(sc, axis=1, keepdims=True))
    return e / jnp.sum(e, axis=1, keepdims=True)


def _xattn_fwd(qc, kv, tq, name):
    s, d = qc.shape
    m = kv.shape[0]
    dh = d // X_HEADS
    scale = dh ** -0.5

    def body(q_ref, k_ref, v_ref, o_ref):
        p = _xattn_probs(q_ref[...], k_ref[...], scale)
        o_ref[...] = _dot(p.astype(BF16), v_ref[...], 1, 0).astype(o_ref.dtype)

    blk = pl.BlockSpec((tq, dh), lambda h, i: (i, h))
    return _pcall(
        body, name=name, grid=(X_HEADS, s // tq),
        in_specs=[blk, pl.BlockSpec((m, dh), lambda h, i: (0, h)), pl.BlockSpec((m, dh), lambda h, i: (0, X_HEADS + h))],
        out_specs=blk, out_shape=jax.ShapeDtypeStruct((s, d), BF16), compiler_params=_params("parallel", "parallel"),
    )(qc, kv, kv)


def _xattn_bwd(qc, kv, do, tq, name):
    s, d = qc.shape
    m = kv.shape[0]
    dh = d // X_HEADS
    scale = dh ** -0.5
    nq = s // tq

    def body(q_ref, k_ref, v_ref, do_ref, dq_ref, dk_ref, dv_ref, dk_acc, dv_acc):
        i = pl.program_id(1)
        q, k, v = q_ref[...], k_ref[...], v_ref[...]
        dob = do_ref[...].astype(BF16)
        p = _xattn_probs(q, k, scale)
        pb = p.astype(BF16)
        dp = _dot(dob, v, 1, 1)
        ds = pb.astype(F32) * (dp - jnp.sum(dp * pb.astype(F32), axis=1, keepdims=True))
        dsb = (ds * scale).astype(BF16)
        dq_ref[...] = _dot(dsb, k, 1, 0).astype(dq_ref.dtype)
        dk_part = _dot(dsb, q, 0, 0)
        dv_part = _dot(pb, dob, 0, 0)

        @pl.when(i == 0)
        def _():
            dk_acc[...] = dk_part
            dv_acc[...] = dv_part

        @pl.when(i > 0)
        def _():
            dk_acc[...] += dk_part
            dv_acc[...] += dv_part

        @pl.when(i == nq - 1)
        def _():
            dk_ref[...] = dk_acc[...].astype(dk_ref.dtype)
            dv_ref[...] = dv_acc[...].astype(dv_ref.dtype)

    blk = pl.BlockSpec((tq, dh), lambda h, i: (i, h))
    kblk = pl.BlockSpec((m, dh), lambda h, i: (0, h))
    return _pcall(
        body, name=name, grid=(X_HEADS, nq),
        in_specs=[blk, kblk, pl.BlockSpec((m, dh), lambda h, i: (0, X_HEADS + h)), blk],
        out_specs=[blk, kblk, kblk],
        out_shape=[jax.ShapeDtypeStruct((s, d), BF16), jax.ShapeDtypeStruct((m, d), BF16), jax.ShapeDtypeStruct((m, d), BF16)],
        scratch_shapes=[pltpu.VMEM((m, dh), F32), pltpu.VMEM((m, dh), F32)],
        compiler_params=_params("parallel", "arbitrary"),
    )(qc, kv, kv, do)


def _local_step(x, mem, tgt, w, fetch=None, prefetch=None, emit=None, tick=None, after=None):
    fetch = fetch or (lambda name, after: {})
    prefetch = prefetch or (lambda name, after: None)
    emit = emit or (lambda group, g: None)
    tick = tick or (lambda group, after: None)
    w = dict(w)
    s, d = x.shape
    heads = d // SB_HEAD_DIM
    tm = _pick(s, (512, 256, 128))
    tq = _pick(s, (256, 128))
    sb_tq, sb_tk = _pick(s, (512, 256, 128)), _pick(s, (256, 128))
    tc = _pick(d, (256, 128))
    g = {}

    def wt(name, after):
        if name not in w:
            w.update(fetch(name, after))
        return w[name]

    def ffn_fwd(h, gname, wgu, wdown, tag, after=None):
        n = _rms_fwd(h, w[gname], tag + "_norm", tm, after=after)
        gu = _mm(n, wt(wgu, n), name=tag + "_gu")
        prefetch(wdown, gu)
        act = _swiglu_fwd(gu, tag + "_act", tm)
        return n, gu, act, _mm(act, wt(wdown, act), name=tag + "_down", out_dtype=F32, res=h, alpha=0.5)

    def ffn_bwd(dh, dhb, h, saved, gname, wgu, wdown, tag, copy_scale=None):
        n, gu, act = saved
        g[wdown] = _mm(act, dhb, ta=True, name=tag + "_dwdown")
        dact = _mm(dhb, w[wdown], tb=True, name=tag + "_dact")
        dgu = _swiglu_bwd(dact, gu, tag + "_dgu", tm)
        g[wgu] = _mm(n, dgu, ta=True, name=tag + "_dwgu")
        dn = _mm(dgu, w[wgu], tb=True, name=tag + "_dn", out_dtype=F32, after=emit(tag, g))
        *dh_in, g[gname] = _resid_rms_bwd(dh, dn, h, w[gname], tag + "_dnorm", tm, after=tick(tag, dn), copy_scale=copy_scale)
        return dh_in

    n1, gu1, act1, h1 = ffn_fwd(x, "g_ffn1", "w_ffn1_gu", "w_ffn1_down", "ffn1", after)
    prefetch("w_in", h1)
    u = _rms_fwd(h1, w["g_mix"], "mix_norm", tm)
    proj = _mm(u, wt("w_in", u), name="mix_in")
    prefetch("w_conv_out", proj)
    nd = d // SB_HEAD_DIM
    y_conv = _conv_fwd(proj, w["conv_w"], d, tc, "conv_fwd")
    sb_cols = (3 * nd, 4 * nd, 5 * nd)
    y_sb = _sb_fwd(proj, heads, sb_cols, sb_tq, sb_tk, "sb_fwd")
    prefetch("w_cq", y_sb)
    a_conv = _mm(y_conv, wt("w_conv_out", y_conv), name="conv_out")
    a_sb = _mm(y_sb, wt("w_attn_out", y_sb), name="attn_out")
    b_conv, b_sb = w["b_gate"][:, :d], w["b_gate"][:, d:]

    def merge(ac, asb, gcp, gsp, bc, bs):
        gc = _sigmoid(gcp.astype(F32) + bc)
        gs = _sigmoid(gsp.astype(F32) + bs)
        return gc * ac.astype(F32) + gs * asb.astype(F32)

    merged = _rowcall(merge, [_whole(a_conv), _whole(a_sb), (proj, 6, d), (proj, 7, d)], [b_conv, b_sb], [(d, BF16)],
                      tm=tm, name="merge")[0]
    prefetch("w_ffn2_gu", merged)
    h2 = _mm(merged, wt("w_o", merged), name="mix_out", out_dtype=F32, res=h1)
    hn = _rms_fwd(h2, w["g_cross"], "cross_norm", tm)
    mn = _rms_fwd(mem, w["g_mem"], "mem_norm", _pick(mem.shape[0], (256, 128)))
    qc = _mm(hn, wt("w_cq", hn), name="cross_q")
    kv = _mm(mn, wt("w_ckv", mn), name="cross_kv")
    oc = _xattn_fwd(qc, kv, tq, "xattn_fwd")
    h3 = _mm(oc, wt("w_co", oc), name="cross_out", out_dtype=F32, res=h2)
    n2, gu2, act2, h4 = ffn_fwd(h3, "g_ffn2", "w_ffn2_gu", "w_ffn2_down", "ffn2")

    def head(hb, tb, gb):
        xh, r = _xhat(hb)
        err = xh * gb - tb
        dy = err * (1.0 / d)
        dxh = dy * gb
        dx = r * (dxh - xh * jnp.mean(dxh * xh, axis=-1, keepdims=True))
        row_loss = 0.5 * jnp.mean(err * err, axis=-1, keepdims=True)
        return dx, 0.5 * dx, dy * xh, jnp.broadcast_to(row_loss, (row_loss.shape[0], LANES))

    dh4, dh4b, g["g_final"], loss_lanes = _rowcall(head, [_whole(h4), _whole(tgt)], [w["g_final"]], [(d, F32), (d, BF16)],
                                                   [d, LANES], tm=tm, name="loss_head")

    dh3, dh3b = ffn_bwd(dh4, dh4b, h3, (n2, gu2, act2), "g_ffn2", "w_ffn2_gu", "w_ffn2_down", "ffn2", copy_scale=1.0)
    g["w_co"] = _mm(oc, dh3b, ta=True, name="cross_dwco")
    doc = _mm(dh3b, w["w_co"], tb=True, name="cross_doc")
    dqc, dk, dv = _xattn_bwd(qc, kv, doc, tq, "xattn_bwd")
    dkv = jnp.concatenate([dk, dv], axis=1)
    g["w_cq"] = _mm(hn, dqc, ta=True, name="cross_dwcq")
    g["w_ckv"] = _mm(mn, dkv, ta=True, name="cross_dwckv")
    dhn = _mm(dqc, w["w_cq"], tb=True, name="cross_dhn", out_dtype=F32, after=emit("cross", g))
    dmn = _mm(dkv, w["w_ckv"], tb=True, name="cross_dmn", out_dtype=F32)
    g["g_mem"] = _rowcall(lambda dy, xb: dy * _xhat(xb)[0], [_whole(dmn), _whole(mem)], [], [], [d],
                          tm=_pick(mem.shape[0], (256, 128)), name="mem_dnorm")[0]
    dh2, dh2b, g["g_cross"] = _resid_rms_bwd(dh3, dhn, h2, w["g_cross"], "cross_dnorm", tm, after=tick("cross", dhn), copy_scale=1.0)

    g["w_o"] = _mm(merged, dh2b, ta=True, name="mix_dwo")
    dmerged = _mm(dh2b, w["w_o"], tb=True, name="mix_dmerged")

    def merge_bwd(dm, ac, asb, gcp, gsp, bc, bs):
        dm, ac, asb = dm.astype(F32), ac.astype(F32), asb.astype(F32)
        gc = _sigmoid(gcp.astype(F32) + bc)
        gs = _sigmoid(gsp.astype(F32) + bs)
        dgc = dm * ac * gc * (1.0 - gc)
        dgs = dm * asb * gs * (1.0 - gs)
        return dm * gc, dm * gs, dgc, dgs, dgc, dgs

    da_conv, da_sb, dgc, dgs, db_conv, db_sb = _rowcall(
        merge_bwd, [_whole(dmerged), _whole(a_conv), _whole(a_sb), (proj, 6, d), (proj, 7, d)], [b_conv, b_sb],
        [(d, BF16)] * 4, [d, d], tm=tm, name="merge_bwd")
    g["b_gate"] = jnp.concatenate([db_conv, db_sb], axis=1)
    g["w_conv_out"] = _mm(y_conv, da_conv, ta=True, name="conv_dwout")
    g["w_attn_out"] = _mm(y_sb, da_sb, ta=True, name="attn_dwout")
    dy_conv = _mm(da_conv, w["w_conv_out"], tb=True, name="conv_dy")
    dy_sb = _mm(da_sb, w["w_attn_out"], tb=True, name="attn_dy")
    dcb, dcc, dcx, g["conv_w"] = _conv_bwd(dy_conv, proj, w["conv_w"], d, tc, "conv_bwd")
    dq, dk_sb, dv_sb = _sb_bwd(proj, y_sb, dy_sb, heads, sb_cols, sb_tq, sb_tk, "sb_bwd")
    dproj = jnp.concatenate([dcb, dcc, dcx, dq, dk_sb, dv_sb, dgc, dgs], axis=1)
    g["w_in"] = _mm(u, dproj, ta=True, name="mix_dwin")
    du = _mm(dproj, w["w_in"], tb=True, name="mix_du", out_dtype=F32, after=emit("mix", g))
    dh1, dh1b, g["g_mix"] = _resid_rms_bwd(dh2, du, h1, w["g_mix"], "mix_dnorm", tm, after=tick("mix", du), copy_scale=0.5)
    dx, = ffn_bwd(dh1, dh1b, x, (n1, gu1, act1), "g_ffn1", "w_ffn1_gu", "w_ffn1_down", "ffn1")
    return loss_lanes, dx, g


MATS = (("w_ffn1_gu", "col"), ("w_ffn1_down", "row"), ("w_in", "col"), ("w_conv_out", "row"), ("w_attn_out", "row"),
        ("w_o", "row"), ("w_cq", "row"), ("w_ckv", "col"), ("w_co", "row"), ("w_ffn2_gu", "col"), ("w_ffn2_down", "row"))
VECS = ("g_ffn1", "g_mix", "g_cross", "g_mem", "g_ffn2", "g_final")
WEIGHTS = ("g_ffn1", "w_ffn1_gu", "w_ffn1_down", "g_mix", "w_in", "b_gate", "conv_w", "w_conv_out", "w_attn_out", "w_o",
           "g_cross", "g_mem", "w_cq", "w_ckv", "w_co", "g_ffn2", "w_ffn2_gu", "w_ffn2_down", "g_final")
CONV_ROWS = 8


def _full_shape(kind, r, c):
    return (r, N_CHIPS * c) if kind == "col" else (N_CHIPS * r, c)


def _piece(ref, kind, r, c, chip, half):
    hr = r // 2
    if kind == "col":
        return ref.at[pl.ds(pl.multiple_of(half * hr, 16), hr), pl.ds(pl.multiple_of(chip * c, LANES), c)]
    return ref.at[pl.ds(pl.multiple_of(chip * r + half * hr, 16), hr), :]


def _shard_of(ref, kind, r, c, chip):
    if kind == "col":
        return ref.at[:, pl.ds(pl.multiple_of(chip * c, LANES), c)]
    return ref.at[pl.ds(pl.multiple_of(chip * r, 16), r), :]


def _place():
    x, y, c = lax.axis_index("x"), lax.axis_index("y"), lax.axis_index("c")
    others = [(1 - x, y), (x, 1 - y), (1 - x, 1 - y)]
    return x, y, c, 2 * x + y, others


def _remote(src, dst, send_sem, recv_sem, to):
    return pltpu.make_async_remote_copy(src_ref=src, dst_ref=dst, send_sem=send_sem, recv_sem=recv_sem,
                                        device_id=to, device_id_type=MESH)


def _gather_conv(conv_shard):
    cc = conv_shard.shape[1]

    def body(conv_ref, conv_full, cs, cr, cl):
        x, y, c, me, others = _place()

        def cols(chip):
            return conv_full.at[:, pl.ds(pl.multiple_of(chip * cc, LANES), cc)]

        def conv(k, chip_from, to):
            return _remote(conv_ref, cols(chip_from), cs.at[k], cr.at[k], to)

        mine = pltpu.make_async_copy(conv_ref, cols(me), cl.at[0])
        mine.start()
        for k, (ox, oy) in enumerate(others):
            conv(k, me, (ox, oy, c)).start()
        for k, (ox, oy) in enumerate(others):
            conv(k, 2 * ox + oy, (x, y, c)).wait_recv()
            conv(k, me, (ox, oy, c)).wait_send()
        mine.wait()

    dma = pltpu.SemaphoreType.DMA
    return _pcall(
        body, name="gather_conv", in_specs=[ANY], out_specs=ANY,
        out_shape=jax.ShapeDtypeStruct((CONV_ROWS, N_CHIPS * cc), F32), scratch_shapes=[dma((3,)), dma((3,)), dma((1,))],
    )(conv_shard)


HBM = pl.BlockSpec(memory_space=pltpu.HBM)
SEM = pl.BlockSpec(memory_space=pltpu.SEMAPHORE)
EFFECT = pltpu.SideEffectType.DATAFLOW_SIDE_EFFECTING
TOKEN = (8, LANES)


def _split_start(name, plan, n_copies, srcs, lands, after=None):
    ns, nl = len(srcs), len(lands)
    n_in = ns + nl + (after is not None)

    def body(*refs):
        outs = refs[n_in:]
        sends, _ = plan(refs[:ns], refs[ns:ns + nl], outs[0], outs[1])
        for cp in sends:
            cp.start()
        outs[-1][...] = jnp.zeros(TOKEN, F32)

    held = [pltpu.HBM(a.shape, a.dtype) for a in (*srcs, *lands)]
    dma = pltpu.SemaphoreType.DMA((n_copies,))
    ins = [pltpu.with_memory_space_constraint(a, pltpu.HBM) for a in (*srcs, *lands)]
    outs = _pcall(
        body, name=name, in_specs=[HBM] * (ns + nl) + ([] if after is None else [ANY]),
        out_specs=(SEM, SEM, *[HBM] * (ns + nl), pl.BlockSpec(memory_space=pltpu.VMEM)),
        out_shape=(dma, dma, *held, jax.ShapeDtypeStruct(TOKEN, F32)),
        input_output_aliases={i: 2 + i for i in range(ns + nl)},
        compiler_params=pltpu.CompilerParams(has_side_effects=EFFECT),
    )(*ins, *([] if after is None else [after]))
    return outs[0], outs[1], list(outs[2:2 + ns]), list(outs[2 + ns:2 + ns + nl]), outs[-1]


def _split_wait(name, plan, send_sems, recv_sems, srcs, lands, after):
    ns, nl = len(srcs), len(lands)

    def body(*refs):
        sends, recvs = plan(refs[:ns], refs[ns:ns + nl], refs[ns + nl], refs[ns + nl + 1])
        for cp in sends:
            cp.wait_send()
        for cp in recvs:
            cp.wait_recv()

    outs = _pcall(
        body, name=name, in_specs=[HBM] * (ns + nl) + [SEM, SEM, ANY], out_specs=[HBM] * (ns + nl),
        out_shape=[pltpu.HBM(a.shape, a.dtype) for a in (*srcs, *lands)],
        input_output_aliases={i: i for i in range(ns + nl)},
        compiler_params=pltpu.CompilerParams(has_side_effects=EFFECT),
    )(*srcs, *lands, send_sems, recv_sems, after)
    return list(outs[:ns]), list(outs[ns:])


def _gather_plan(dims):
    def plan(shard_refs, full_refs, ss, rs):
        x, y, c, me, others = _place()
        sends, recvs = [], []
        for wi, (kind, r, cw) in enumerate(dims):
            half = shard_refs[wi].at[pl.ds(pl.multiple_of(c * (r // 2), 16), r // 2), :]
            for k, (ox, oy) in enumerate(others):
                sem = 4 * wi + k
                sends.append(_remote(half, _piece(full_refs[wi], kind, r, cw, me, c), ss.at[sem], rs.at[sem], (ox, oy, c)))
                recvs.append(_remote(half, _piece(full_refs[wi], kind, r, cw, 2 * ox + oy, c), ss.at[sem], rs.at[sem], (x, y, c)))
            sem = 4 * wi + 3
            own = _remote(shard_refs[wi], _shard_of(full_refs[wi], kind, r, cw, me), ss.at[sem], rs.at[sem], (x, y, 1 - c))
            sends.append(own)
            recvs.append(own)
        return sends, recvs

    return plan


def _forward_plan(dims):
    def plan(_, full_refs, ss, rs):
        x, y, c, _, others = _place()
        sends, recvs = [], []
        for wi, (kind, r, cw) in enumerate(dims):
            for k, (ox, oy) in enumerate(others):
                sem = 3 * wi + k
                mine = _piece(full_refs[wi], kind, r, cw, 2 * ox + oy, c)
                theirs = _piece(full_refs[wi], kind, r, cw, 2 * ox + oy, 1 - c)
                sends.append(_remote(mine, mine, ss.at[sem], rs.at[sem], (x, y, 1 - c)))
                recvs.append(_remote(theirs, theirs, ss.at[sem], rs.at[sem], (x, y, 1 - c)))
        return sends, recvs

    return plan


def _rs_cores_plan(dims):
    def plan(g_refs, land_refs, ss, rs):
        x, y, c, _, _ = _place()
        sends, recvs = [], []
        for wi, dm in enumerate(dims):
            for chip in range(N_CHIPS):
                sem = N_CHIPS * wi + chip
                sends.append(_remote(_piece(g_refs[wi], *dm, chip, 1 - c), land_refs[wi].at[chip], ss.at[sem], rs.at[sem], (x, y, 1 - c)))
                recvs.append(_remote(_piece(g_refs[wi], *dm, chip, c), land_refs[wi].at[chip], ss.at[sem], rs.at[sem], (x, y, 1 - c)))
        return sends, recvs

    return plan


def _share_plan(nw):
    def plan(_, buf_refs, ss, rs):
        x, y, c, _, _ = _place()
        sends = [_remote(buf_refs[wi].at[c], buf_refs[wi].at[c], ss.at[wi], rs.at[wi], (x, y, 1 - c)) for wi in range(nw)]
        recvs = [_remote(buf_refs[wi].at[1 - c], buf_refs[wi].at[1 - c], ss.at[wi], rs.at[wi], (x, y, 1 - c)) for wi in range(nw)]
        return sends, recvs

    return plan


def _small_plan():
    def plan(_, buf_refs, ss, rs):
        x, y, c = lax.axis_index("x"), lax.axis_index("y"), lax.axis_index("c")
        buf = buf_refs[0]
        sends, recvs = [], []
        for rel in range(1, N_DEV):
            peer = (x ^ (rel >> 2 & 1), y ^ (rel >> 1 & 1), c ^ (rel & 1))
            sends.append(_remote(buf.at[0], buf.at[rel], ss.at[rel - 1], rs.at[rel - 1], peer))
            recvs.append(_remote(buf.at[0], buf.at[rel], ss.at[rel - 1], rs.at[rel - 1], peer))
        return sends, recvs

    return plan


def _sum_small(buf, me, name):
    _, rows, n = buf.shape

    def body(me_ref, b_ref, o_ref):
        tot = b_ref[me_ref[0]]
        for dev in range(1, N_DEV):
            tot = tot + b_ref[dev ^ me_ref[0]]
        o_ref[...] = tot

    return _pcall(
        body, name=name, out_shape=jax.ShapeDtypeStruct((rows, n), F32),
        grid_spec=pltpu.PrefetchScalarGridSpec(
            num_scalar_prefetch=1, grid=(1,), in_specs=[pl.BlockSpec((N_DEV, rows, n), lambda i, m: (0, 0, 0))],
            out_specs=pl.BlockSpec((rows, n), lambda i, m: (0, 0))),
    )(me, buf)


def _rs_chips_plan(nw):
    def plan(p_refs, land_refs, ss, rs):
        x, y, c, me, others = _place()
        sends, recvs = [], []
        for wi in range(nw):
            for k, (ox, oy) in enumerate(others):
                sem = 3 * wi + k
                sends.append(_remote(p_refs[wi].at[2 * ox + oy], land_refs[wi].at[k], ss.at[sem], rs.at[sem], (ox, oy, c)))
                recvs.append(_remote(p_refs[wi].at[me], land_refs[wi].at[k], ss.at[sem], rs.at[sem], (x, y, c)))
        return sends, recvs

    return plan


def _rows_per_block(n, c, limit_bytes=2 << 20):
    best = None
    for tm in range(16, n + 1, 16):
        if n % tm == 0 and tm * c * 4 <= limit_bytes:
            best = tm
    return best or n


def _sum_cores(grad, got, kind, place, name):
    _, hr, cw = got.shape
    tm = _rows_per_block(hr, cw)
    nb = hr // tm

    def body(place_ref, g_ref, t_ref, o_ref):
        o_ref[...] = (g_ref[...].astype(F32) + t_ref[...].astype(F32)).astype(o_ref.dtype)

    if kind == "col":
        g_spec = pl.BlockSpec((tm, cw), lambda j, i, pr: (pr[0] * nb + i, j))
    else:
        g_spec = pl.BlockSpec((tm, cw), lambda j, i, pr: ((2 * j + pr[0]) * nb + i, 0))
    blk = pl.BlockSpec((None, tm, cw), lambda j, i, pr: (j, i, 0))
    return _pcall(
        body, name=name, out_shape=jax.ShapeDtypeStruct(got.shape, BF16),
        grid_spec=pltpu.PrefetchScalarGridSpec(num_scalar_prefetch=1, grid=(N_CHIPS, nb), in_specs=[g_spec, blk], out_specs=blk),
        compiler_params=_params("parallel", "parallel"),
    )(place, grad, got)


def _sum_chips(parts, got, place, name):
    _, n, cw = got.shape
    tm = _rows_per_block(n, cw)

    def body(place_ref, p_ref, g_ref, o_ref):
        tot = p_ref[...].astype(F32)
        for k in range(3):
            tot = tot + g_ref[k].astype(F32)
        o_ref[...] = tot

    return _pcall(
        body, name=name, out_shape=jax.ShapeDtypeStruct((2, n, cw), F32),
        grid_spec=pltpu.PrefetchScalarGridSpec(
            num_scalar_prefetch=1, grid=(n // tm,),
            in_specs=[pl.BlockSpec((None, tm, cw), lambda i, pr: (pr[1], i, 0)), pl.BlockSpec((3, tm, cw), lambda i, pr: (0, i, 0))],
            out_specs=pl.BlockSpec((None, tm, cw), lambda i, pr: (pr[0], i, 0))),
        compiler_params=_params("parallel"),
    )(place, parts, got)


def _adamw(g, w, m, v, name):
    n, c = g.shape
    c1 = 1.0 - ADAM_B1 ** ADAM_STEP
    c2 = 1.0 - ADAM_B2 ** ADAM_STEP

    def fn(gb, wb, mb, vb):
        m_new = ADAM_B1 * mb + (1.0 - ADAM_B1) * gb
        v_new = ADAM_B2 * vb + (1.0 - ADAM_B2) * (gb * gb)
        delta = -ADAM_LR * ((m_new / c1) / (jnp.sqrt(v_new / c2) + ADAM_EPS) + ADAM_WD * wb)
        return gb, delta, m_new, v_new

    tm = _rows_per_block(n, c) if n % 16 == 0 else n
    return _rowcall(fn, [_whole(g), _whole(w), _whole(m), _whole(v)], [], [(c, F32)] * 4, tm=tm, name=name)


PACK_ROWS = 16


def _pack_rows(parts, width, name, after=None):
    assert sum(p.shape[0] for p in parts) <= PACK_ROWS

    def body(*refs):
        out_ref = refs[-1]
        out_ref[...] = jnp.zeros_like(out_ref)
        at = 0
        for r in refs[:len(parts)]:
            k, n = r.shape
            if n == width:
                out_ref[at:at + k, :] = r[...]
            else:
                out_ref[at:at + k, :] = jnp.broadcast_to(r[:, :1], (k, width))
            at += k

    vm = pl.BlockSpec(memory_space=pltpu.VMEM)
    return _pcall(body, name=name, in_specs=[vm] * len(parts) + ([] if after is None else [ANY]), out_specs=vm,
                  out_shape=jax.ShapeDtypeStruct((PACK_ROWS, width), F32))(*parts, *([] if after is None else [after]))


def _cast_shard(wm, name, after):
    n, c = wm.shape
    return _rowcall(lambda v: v, [_whole(wm)], [], [(c, BF16)], tm=_rows_per_block(n, c), name=name, after=after)[0]


GATHER_GROUPS = (
    ("w_ffn1_gu",), ("w_ffn1_down",), ("w_in",), ("w_conv_out", "w_attn_out", "w_o"), ("w_cq", "w_ckv", "w_co"),
    ("w_ffn2_gu", "w_ffn2_down"),
)
REDUCE_GROUPS = {
    "ffn2": ("w_ffn2_down", "w_ffn2_gu"),
    "cross": ("w_co", "w_cq", "w_ckv"),
    "mix": ("w_o", "w_conv_out", "w_attn_out", "w_in"),
    "ffn1": ("w_ffn1_down", "w_ffn1_gu"),
}
KIND = dict(MATS)


def _step(x, mem, tgt, wts, m_in, v_in):
    d = x.shape[-1]
    cc = wts["conv_w"].shape[1]
    place = jnp.stack([lax.axis_index("c"), 2 * lax.axis_index("x") + lax.axis_index("y")]).astype(jnp.int32)
    dims = {n: (kind, *wts[n].shape) for n, kind in MATS}

    conv_pad = jnp.pad(wts["conv_w"], ((0, CONV_ROWS - CONV_K), (0, 0)))
    conv_full = _gather_conv(conv_pad)
    w = {"conv_w": conv_full[:CONV_K]}
    for n in VECS + ("b_gate",):
        w[n] = wts[n].reshape(1, -1)
    flying, token = {}, conv_full
    for names in GATHER_GROUPS:
        gd = [dims[n] for n in names]
        shards = [_cast_shard(wts[n], "cast_" + n, token) for n in names]
        lands = [lax.empty(_full_shape(*dm), BF16) for dm in gd]
        plan = _gather_plan(gd)
        ss, rs, srcs, lands, token = _split_start("gather_start_" + names[0], plan, 4 * len(names), shards, lands, token)
        flying.update({n: (names, plan, ss, rs, srcs, lands, gd) for n in names})

    passing = {}

    def prefetch(name, after):
        if name not in passing:
            names, plan, ss, rs, srcs, lands, gd = flying[name]
            _, lands = _split_wait("gather_wait_" + names[0], plan, ss, rs, srcs, lands, after)
            plan = _forward_plan(gd)
            ss, rs, _, lands, _ = _split_start("forward_start_" + names[0], plan, 3 * len(names), [], lands)
            passing.update({n: (names, plan, ss, rs, lands) for n in names})

    def fetch(name, after):
        prefetch(name, after)
        names, plan, ss, rs, lands = passing[name]
        _, lands = _split_wait("forward_wait_" + names[0], plan, ss, rs, [], lands, after)
        return dict(zip(names, lands))

    swapping, sent = {}, {}

    def emit(tag, g):
        names = REDUCE_GROUPS[tag]
        gd = [dims[n] for n in names]
        lands = [lax.empty((N_CHIPS, r // 2, cw), BF16) for (_, r, cw) in gd]
        plan = _rs_cores_plan(gd)
        ss, rs, srcs, lands, tok = _split_start("rs_cores_start_" + tag, plan, N_CHIPS * len(names), [g[n] for n in names], lands)
        swapping[tag] = (plan, ss, rs, srcs, lands)
        return tok

    def tick(tag, after):
        names = REDUCE_GROUPS[tag]
        plan, ss, rs, srcs, lands = swapping[tag]
        mine, got = _split_wait("rs_cores_wait_" + tag, plan, ss, rs, srcs, lands, after)
        parts = [_sum_cores(gm, t, KIND[n], place, "sum_cores_" + n) for n, gm, t in zip(names, mine, got)]
        lands = [lax.empty((3, *p.shape[1:]), BF16) for p in parts]
        plan = _rs_chips_plan(len(names))
        ss, rs, srcs, lands, tok = _split_start("rs_chips_start_" + tag, plan, 3 * len(names), parts, lands)
        sent[tag] = (plan, ss, rs, srcs, lands)
        return tok

    loss_lanes, dx, g = _local_step(x[0], mem[0], tgt[0], w, fetch, prefetch, emit, tick, token)

    rows = [g[n] for n in VECS] + [g["b_gate"][:, :d], g["b_gate"][:, d:], g["conv_w"], loss_lanes]
    packed = _pack_rows(rows, d, "pack_small")
    small = jnp.concatenate([packed[None], jnp.zeros((N_DEV - 1, *packed.shape), F32)], axis=0)
    small_plan = _small_plan()
    small_ss, small_rs, _, small, after = _split_start("small_start", small_plan, N_DEV - 1, [], [small])

    grads, out = {}, {}

    def update(n):
        shape = wts[n].shape
        as2d = (lambda a: a.reshape(1, -1)) if len(shape) == 1 else (lambda a: a)
        return [r.reshape(shape) for r in _adamw(grads[n], as2d(wts[n]), as2d(m_in[n]), as2d(v_in[n]), "adamw_" + n)]

    def finish(sharing, after):
        tag, names, plan, ss, rs, halves = sharing
        _, both = _split_wait("share_wait_" + tag, plan, ss, rs, [], halves, after)
        for n, b in zip(names, both):
            grads[n] = b.reshape(-1, b.shape[-1])
            out[n] = update(n)
        return out[names[-1]][1]

    sharing = None
    for tag, names in REDUCE_GROUPS.items():
        plan, ss, rs, srcs, lands = sent[tag]
        parts, landed = _split_wait("rs_chips_wait_" + tag, plan, ss, rs, srcs, lands, after)
        halves = [_sum_chips(p, t, place, "sum_chips_" + n) for n, p, t in zip(names, parts, landed)]
        plan = _share_plan(len(names))
        ss, rs, _, halves, after = _split_start("share_start_" + tag, plan, len(names), [], halves)
        if sharing is not None:
            after = finish(sharing, after)
        sharing = (tag, names, plan, ss, rs, halves)
    after = finish(sharing, after)

    _, small = _split_wait("small_wait", small_plan, small_ss, small_rs, [], small, after)
    me = (4 * lax.axis_index("x") + 2 * lax.axis_index("y") + lax.axis_index("c")).astype(jnp.int32).reshape(1)
    red = _sum_small(small[0], me, "sum_small")
    grads.update({n: red[i:i + 1] for i, n in enumerate(VECS)})
    nv = len(VECS)
    grads["b_gate"] = jnp.concatenate([red[nv:nv + 1], red[nv + 1:nv + 2]], axis=1)
    chip = 2 * lax.axis_index("x") + lax.axis_index("y")
    grads["conv_w"] = lax.dynamic_slice_in_dim(red[nv + 2:nv + 2 + CONV_K], chip * cc, cc, axis=1)
    loss = red[nv + 2 + CONV_K, 0]
    out.update({n: update(n) for n in WEIGHTS if n not in KIND})
    return (loss, dx[None], *[out[n][0] for n in WEIGHTS], *[out[n][1] for n in WEIGHTS],
            *[out[n][2] for n in WEIGHTS], *[out[n][3] for n in WEIGHTS])


def kernel(x, mem, g_ffn1, w_ffn1_gu, w_ffn1_down, g_mix, w_in, b_gate, conv_w, w_conv_out, w_attn_out, w_o, g_cross, g_mem, w_cq, w_ckv, w_co, g_ffn2, w_ffn2_gu, w_ffn2_down, g_final, loss_target, m_g_ffn1, m_w_ffn1_gu, m_w_ffn1_down, m_g_mix, m_w_in, m_b_gate, m_conv_w, m_w_conv_out, m_w_attn_out, m_w_o, m_g_cross, m_g_mem, m_w_cq, m_w_ckv, m_w_co, m_g_ffn2, m_w_ffn2_gu, m_w_ffn2_down, m_g_final, v_g_ffn1, v_w_ffn1_gu, v_w_ffn1_down, v_g_mix, v_w_in, v_b_gate, v_conv_w, v_w_conv_out, v_w_attn_out, v_w_o, v_g_cross, v_g_mem, v_w_cq, v_w_ckv, v_w_co, v_g_ffn2, v_w_ffn2_gu, v_w_ffn2_down, v_g_final):
    given = dict(locals())
    wts = {n: given[n] for n in WEIGHTS}
    m_in = {n: given["m_" + n] for n in WEIGHTS}
    v_in = {n: given["v_" + n] for n in WEIGHTS}
    return _step(x, mem, loss_target, wts, m_in, v_in)
```

```python
import functools

import jax
import jax.numpy as jnp
from jax import lax
from jax.experimental import pallas as pl
from jax.experimental.pallas import tpu as pltpu

F32 = jnp.float32
BF16 = jnp.bfloat16
MESH = pl.DeviceIdType.MESH

V7X_VMEM_LIMIT_BYTES = 48 * 1024 * 1024
MM_VMEM_BUDGET_BYTES = 36 * 1024 * 1024
MM_WHOLE_K = 2816
LANES = 128
SB_HEAD_DIM = 128
X_HEADS = 4
CONV_K = 3
RMS_EPS = 1e-6
N_CHIPS = 4
N_DEV = 8
ADAM_LR, ADAM_B1, ADAM_B2, ADAM_EPS, ADAM_WD, ADAM_STEP = 0.001, 0.9, 0.999, 1e-08, 0.01, 10


ANY = pl.BlockSpec(memory_space=pl.ANY)


def _pcall(body, **kw):
    return pl.pallas_call(body, **kw)


def _params(*sem):
    return pltpu.CompilerParams(dimension_semantics=sem, vmem_limit_bytes=V7X_VMEM_LIMIT_BYTES)


def _pick(dim, cands):
    for c in cands:
        if dim % c == 0:
            return c
    return dim


def _dot(a, b, ca, cb):
    return lax.dot_general(a, b, (((ca,), (cb,)), ((), ())), preferred_element_type=F32)


def _mm(a, b, *, name, ta=False, tb=False, out_dtype=BF16, res=None, alpha=1.0, tm=None, tn=None, tk=None, after=None,
        a_halves=False, b_halves=False):
    assert not (a_halves and ta) and not (b_halves and tb)
    if a_halves:
        m, k = a.shape[1], 2 * a.shape[2]
    else:
        m, k = (a.shape[1], a.shape[0]) if ta else a.shape
    if b_halves:
        n = 2 * b.shape[2]
        assert k == b.shape[1]
    else:
        n = b.shape[0] if tb else b.shape[1]
        assert k == (b.shape[1] if tb else b.shape[0]), (a.shape, b.shape, ta, tb)
    if ta:
        tm = tm or _pick(m, (512, 256, 128))
        tn = tn or _pick(n, (1024, 512, 256, 128))
        tk = tk or (k if k <= MM_WHOLE_K else _pick(k, (1024, 512, 256, 128)))
    else:
        tk = tk or (k if k <= MM_WHOLE_K else _pick(k, (MM_WHOLE_K, 2048, 1024, 512, 256, 128)))
        tn = tn or _pick(n, (512, 1408, 256, 128) if tk == k else (1024, 512, 256, 128))
        per_row = 2 * (tk * a.dtype.itemsize + tn * (jnp.dtype(out_dtype).itemsize + (0 if res is None else res.dtype.itemsize)))
        per_row += 4 * tn if tk < k else 0
        rows = (MM_VMEM_BUDGET_BYTES - 2 * tk * tn * b.dtype.itemsize) // per_row
        tm = tm or next((c for c in (2048, 1024, 512, 256, 128) if m % c == 0 and c <= rows), m)
    if a_halves:
        tk = min(tk, k // 2) if (k // 2) % min(tk, k // 2) == 0 else _pick(k // 2, (1408, 1024, 512, 256, 128))
    if b_halves:
        tn = tn if (n // 2) % tn == 0 else _pick(n // 2, (1408, 1024, 512, 256, 128))
    nk = k // tk
    assert m % tm == 0 and n % tn == 0 and k % tk == 0
    a_spec = pl.BlockSpec((tk, tm), lambda i, j, kk: (kk, i)) if ta else pl.BlockSpec((tm, tk), lambda i, j, kk: (i, kk))
    b_spec = pl.BlockSpec((tn, tk), lambda i, j, kk: (j, kk)) if tb else pl.BlockSpec((tk, tn), lambda i, j, kk: (kk, j))
    if a_halves:
        per = (k // 2) // tk
        a_spec = pl.BlockSpec((None, tm, tk), lambda i, j, kk: (kk // per, i, kk % per))
    if b_halves:
        per_n = (n // 2) // tn
        b_spec = pl.BlockSpec((None, tk, tn), lambda i, j, kk: (j // per_n, kk, j % per_n))
    o_spec = pl.BlockSpec((tm, tn), lambda i, j, kk: (i, j))
    ca, cb = (0 if ta else 1), (1 if tb else 0)

    n_in = 2 + (res is not None) + (after is not None)

    def body(*refs):
        a_ref, b_ref = refs[:2]
        res_ref = refs[2] if res is not None else None
        o_ref = refs[n_in]
        scratch = refs[n_in + 1:]

        def finish(acc):
            val = acc if alpha == 1.0 else alpha * acc
            if res_ref is not None:
                val = res_ref[...].astype(F32) + val
            o_ref[...] = val.astype(o_ref.dtype)

        part = _dot(a_ref[...].astype(BF16), b_ref[...].astype(BF16), ca, cb)
        if nk == 1:
            finish(part)
        else:
            acc_ref = scratch[0]
            kk = pl.program_id(2)

            @pl.when(kk == 0)
            def _():
                acc_ref[...] = part

            @pl.when(kk > 0)
            def _():
                acc_ref[...] += part

            @pl.when(kk == nk - 1)
            def _():
                finish(acc_ref[...])

    ins = [a, b] + ([] if res is None else [res]) + ([] if after is None else [after])
    in_specs = [a_spec, b_spec] + ([] if res is None else [o_spec]) + ([] if after is None else [ANY])
    return _pcall(
        body, name=name, grid=(m // tm, n // tn, nk), in_specs=in_specs, out_specs=o_spec,
        out_shape=jax.ShapeDtypeStruct((m, n), out_dtype),
        scratch_shapes=[pltpu.VMEM((tm, tn), F32)] if nk > 1 else [],
        compiler_params=_params("parallel", "parallel", "arbitrary"),
    )(*ins)


def _rowcall(fn, rows, consts, outs, accs=(), *, tm, name, after=None):
    s = rows[0][0].shape[0]
    assert s % tm == 0
    n_read, n_out = len(rows) + len(consts), len(outs)
    n_in = n_read + (after is not None)

    def body(*refs):
        vals = fn(*[r[...] for r in refs[:n_read]])
        vals = vals if isinstance(vals, (tuple, list)) else (vals,)
        for o_ref, v in zip(refs[n_in:n_in + n_out], vals[:n_out]):
            o_ref[...] = v.astype(o_ref.dtype)
        if accs:
            first = pl.program_id(0) == 0
            for a_ref, v in zip(refs[n_in + n_out:], vals[n_out:]):
                tot = jnp.sum(v.astype(F32), axis=0, keepdims=True)

                @pl.when(first)
                def _(a_ref=a_ref, tot=tot):
                    a_ref[...] = tot

                @pl.when(jnp.logical_not(first))
                def _(a_ref=a_ref, tot=tot):
                    a_ref[...] += tot

    in_specs = [pl.BlockSpec((tm, w), lambda i, cb=cb: (i, cb)) for (_, cb, w) in rows]
    in_specs += [pl.BlockSpec(c.shape, lambda i: (0, 0)) for c in consts]
    in_specs += [] if after is None else [ANY]
    out_specs = [pl.BlockSpec((tm, w), lambda i: (i, 0)) for (w, _) in outs]
    out_specs += [pl.BlockSpec((1, w), lambda i: (0, 0)) for w in accs]
    out_shape = [jax.ShapeDtypeStruct((s, w), dt) for (w, dt) in outs]
    out_shape += [jax.ShapeDtypeStruct((1, w), F32) for w in accs]
    return _pcall(
        body, name=name, grid=(s // tm,), in_specs=in_specs, out_specs=out_specs, out_shape=out_shape,
        compiler_params=_params("arbitrary" if accs else "parallel"),
    )(*[r[0] for r in rows], *consts, *([] if after is None else [after]))


def _whole(a):
    return (a, 0, a.shape[1])


def _xhat(x):
    x = x.astype(F32)
    r = lax.rsqrt(jnp.mean(x * x, axis=-1, keepdims=True) + RMS_EPS)
    return x * r, r


def _rms_bwd(dy, x, g):
    xh, r = _xhat(x)
    dxh = dy.astype(F32) * g
    dx = r * (dxh - xh * jnp.mean(dxh * xh, axis=-1, keepdims=True))
    return dx, dy.astype(F32) * xh


def _sigmoid(x):
    return 1.0 / (1.0 + jnp.exp(-x))


def _rms_fwd(x, g, name, tm, after=None):
    d = x.shape[1]
    return _rowcall(lambda xb, gb: _xhat(xb)[0] * gb, [_whole(x)], [g], [(d, BF16)], tm=tm, name=name, after=after)[0]


def _silu_parts(gate):
    sg = _sigmoid(gate)
    return sg, gate * sg


def _ffn_up(n, w_gu, name):
    s, d = n.shape
    f = w_gu.shape[1] // 2
    tn = _pick(f, (1408, 1024, 512, 256, 128))
    tm = _pick(s, (1024, 512, 256, 128))
    nb = f // tn

    def body(n_ref, wg_ref, wu_ref, gu_ref, act_ref):
        nv = n_ref[...]
        gate = _dot(nv, wg_ref[...], 1, 0)
        up = _dot(nv, wu_ref[...], 1, 0)
        gu_ref[0] = gate.astype(gu_ref.dtype)
        gu_ref[1] = up.astype(gu_ref.dtype)
        act_ref[...] = (_silu_parts(gate)[1] * up).astype(act_ref.dtype)

    return _pcall(
        body, name=name, grid=(s // tm, nb),
        in_specs=[pl.BlockSpec((tm, d), lambda i, j: (i, 0)), pl.BlockSpec((d, tn), lambda i, j: (0, j)),
                  pl.BlockSpec((d, tn), lambda i, j: (0, nb + j))],
        out_specs=[pl.BlockSpec((2, tm, tn), lambda i, j: (0, i, j)), pl.BlockSpec((tm, tn), lambda i, j: (i, j))],
        out_shape=[jax.ShapeDtypeStruct((2, s, f), BF16), jax.ShapeDtypeStruct((s, f), BF16)],
        compiler_params=_params("parallel", "parallel"),
    )(n, w_gu, w_gu)


def _ffn_dgu(dhb, w_down, gu, name):
    s, d = dhb.shape
    f = w_down.shape[0]
    tn = _pick(f, (1408, 1024, 512, 256, 128))
    tm = _pick(s, (1024, 512, 256, 128))

    def body(dh_ref, w_ref, gu_ref, o_ref):
        dact = _dot(dh_ref[...], w_ref[...], 1, 1)
        gate, up = gu_ref[0].astype(F32), gu_ref[1].astype(F32)
        sg, silu = _silu_parts(gate)
        o_ref[0] = (dact * up * (sg + silu * (1.0 - sg))).astype(o_ref.dtype)
        o_ref[1] = (dact * silu).astype(o_ref.dtype)

    blk = pl.BlockSpec((2, tm, tn), lambda i, j: (0, i, j))
    return _pcall(
        body, name=name, grid=(s // tm, f // tn),
        in_specs=[pl.BlockSpec((tm, d), lambda i, j: (i, 0)), pl.BlockSpec((tn, d), lambda i, j: (j, 0)), blk],
        out_specs=blk, out_shape=jax.ShapeDtypeStruct((2, s, f), BF16), compiler_params=_params("parallel", "parallel"),
    )(dhb, w_down, gu)


def _resid_rms_bwd(dh, dn, x, g, name, tm, after=None, copy_scale=None):
    d = x.shape[1]

    def fn(dhb, dnb, xb, gb):
        dx, dg = _rms_bwd(dnb, xb, gb)
        tot = dhb.astype(F32) + dx
        return (tot, dg) if copy_scale is None else (tot, copy_scale * tot, dg)

    outs = [(d, F32)] + ([] if copy_scale is None else [(d, BF16)])
    return _rowcall(fn, [_whole(dh), _whole(dn), _whole(x)], [g], outs, [d], tm=tm, name=name, after=after)


def _shift_down(p, k):
    if k == 0:
        return p
    rows = lax.broadcasted_iota(jnp.int32, p.shape, 0)
    return jnp.where(rows >= k, pltpu.roll(p, k, 0), 0.0)


def _shift_up(p, k):
    if k == 0:
        return p
    s = p.shape[0]
    rows = lax.broadcasted_iota(jnp.int32, p.shape, 0)
    return jnp.where(rows < s - k, pltpu.roll(p, s - k, 0), 0.0)


def _conv_fwd(proj, conv_w, d, tc, name):
    s = proj.shape[0]
    nb = d // tc

    def body(cb_ref, cc_ref, cx_ref, w_ref, y_ref):
        p = cc_ref[...].astype(F32) * cx_ref[...].astype(F32)
        w = w_ref[...]
        acc = p * w[CONV_K - 1:CONV_K, :]
        for k in range(1, CONV_K):
            acc = acc + _shift_down(p, k) * w[CONV_K - 1 - k:CONV_K - k, :]
        y_ref[...] = (cb_ref[...].astype(F32) * acc).astype(y_ref.dtype)

    col = lambda off: pl.BlockSpec((s, tc), lambda j: (0, off * nb + j))
    return _pcall(
        body, name=name, grid=(nb,), in_specs=[col(0), col(1), col(2), pl.BlockSpec((CONV_K, tc), lambda j: (0, j))],
        out_specs=pl.BlockSpec((s, tc), lambda j: (0, j)), out_shape=jax.ShapeDtypeStruct((s, d), BF16),
        compiler_params=_params("parallel"),
    )(proj, proj, proj, conv_w)


def _conv_bwd(dy, proj, conv_w, d, tc, name):
    s = proj.shape[0]
    nb = d // tc

    def body(dy_ref, cb_ref, cc_ref, cx_ref, w_ref, dcb_ref, dcc_ref, dcx_ref, dw_ref):
        cc, cx = cc_ref[...].astype(F32), cx_ref[...].astype(F32)
        p = cc * cx
        w = w_ref[...]
        dyv = dy_ref[...].astype(F32)
        shifted = [_shift_down(p, CONV_K - 1 - k) for k in range(CONV_K)]
        conv = shifted[0] * w[0:1, :]
        for k in range(1, CONV_K):
            conv = conv + shifted[k] * w[k:k + 1, :]
        dcb_ref[...] = (dyv * conv).astype(dcb_ref.dtype)
        ds = dyv * cb_ref[...].astype(F32)
        dp = ds * w[CONV_K - 1:CONV_K, :]
        for k in range(1, CONV_K):
            dp = dp + _shift_up(ds, k) * w[CONV_K - 1 - k:CONV_K - k, :]
        dcc_ref[...] = (dp * cx).astype(dcc_ref.dtype)
        dcx_ref[...] = (dp * cc).astype(dcx_ref.dtype)
        for k in range(CONV_K):
            dw_ref[k:k + 1, :] = jnp.sum(ds * shifted[k], axis=0, keepdims=True)

    col = lambda off: pl.BlockSpec((s, tc), lambda j: (0, off * nb + j))
    blk = pl.BlockSpec((s, tc), lambda j: (0, j))
    wblk = pl.BlockSpec((CONV_K, tc), lambda j: (0, j))
    act = jax.ShapeDtypeStruct((s, d), BF16)
    return _pcall(
        body, name=name, grid=(nb,), in_specs=[blk, col(0), col(1), col(2), wblk],
        out_specs=[blk, blk, blk, wblk], out_shape=[act, act, act, jax.ShapeDtypeStruct((CONV_K, d), F32)],
        compiler_params=_params("parallel"),
    )(dy, proj, proj, proj, conv_w)


def _sb_tile(q, kj, scale, carry, tri, mask):
    z = _dot(q, kj, 1, 1) * scale
    lsz = jnp.minimum(z, 0.0) - jnp.log(1.0 + jnp.exp(-jnp.abs(z)))
    l1m = lsz - z
    if mask is not None:
        l1m = jnp.where(mask, l1m, 0.0)
    l1b = l1m.astype(BF16)
    a = jnp.exp(lsz + (carry + _dot(l1b, tri, 1, 0)))
    if mask is not None:
        a = jnp.where(mask, a, 0.0)
    return lsz, l1b, a.astype(BF16)


def _sb_masks(tq, tk):
    row = lax.broadcasted_iota(jnp.int32, (tq, tk), 0)
    col = lax.broadcasted_iota(jnp.int32, (tq, tk), 1)
    masks = [col + dj * tk < row for dj in range(tq // tk)]
    r2 = lax.broadcasted_iota(jnp.int32, (tk, tk), 0)
    c2 = lax.broadcasted_iota(jnp.int32, (tk, tk), 1)
    return masks, (r2 > c2).astype(BF16), (r2 < c2).astype(BF16)


def _sb_fwd(proj, heads, col0, tq, tk, name):
    s = proj.shape[0]
    dh = SB_HEAD_DIM
    nq, nd = s // tq, tq // tk
    scale = dh ** -0.5

    def body(q_ref, k_ref, v_ref, o_ref):
        i = pl.program_id(1)
        q = q_ref[...]
        masks, tri_right, _ = _sb_masks(tq, tk)

        def tile(j, carry, acc, mask):
            start = pl.multiple_of(j * tk, tk)
            kj = k_ref[pl.ds(start, tk), :]
            vj = v_ref[pl.ds(start, tk), :]
            _, l1b, ab = _sb_tile(q, kj, scale, carry, tri_right, mask)
            return carry + jnp.sum(l1b.astype(F32), axis=1, keepdims=True), acc + _dot(ab, vj, 1, 0)

        state = (jnp.zeros((tq, 1), F32), jnp.zeros((tq, dh), F32))
        for dj in reversed(range(nd)):
            state = tile(i * nd + dj, *state, masks[dj])
        state = lax.fori_loop(0, i * nd, lambda t, st: tile(i * nd - 1 - t, st[0], st[1], None), state)
        o_ref[...] = state[1]

    qspec = pl.BlockSpec((tq, dh), lambda h, i: (i, col0[0] + h))
    kspec = pl.BlockSpec((s, dh), lambda h, i: (0, col0[1] + h))
    vspec = pl.BlockSpec((s, dh), lambda h, i: (0, col0[2] + h))
    return _pcall(
        body, name=name, grid=(heads, nq), in_specs=[qspec, kspec, vspec],
        out_specs=pl.BlockSpec((tq, dh), lambda h, i: (i, h)), out_shape=jax.ShapeDtypeStruct((s, heads * dh), F32),
        compiler_params=_params("parallel", "parallel"),
    )(proj, proj, proj)


def _sb_bwd(proj, o, do, heads, col0, tq, tk, name):
    s = proj.shape[0]
    dh = SB_HEAD_DIM
    nq, nd = s // tq, tq // tk
    scale = dh ** -0.5

    def body(q_ref, k_ref, v_ref, o_ref, do_ref, dq_ref, dk_ref, dv_ref, dk_acc, dv_acc):
        i = pl.program_id(1)

        @pl.when(i == 0)
        def _():
            dk_acc[...] = jnp.zeros_like(dk_acc)
            dv_acc[...] = jnp.zeros_like(dv_acc)

        q = q_ref[...]
        dob = do_ref[...].astype(BF16)
        delta = jnp.sum(dob.astype(F32) * o_ref[...], axis=1, keepdims=True)
        masks, tri_right, tri_left = _sb_masks(tq, tk)

        def tile(j, carry_l, carry_g, dq, mask):
            start = pl.multiple_of(j * tk, tk)
            kj = k_ref[pl.ds(start, tk), :]
            vj = v_ref[pl.ds(start, tk), :]
            lsz, l1b, ab = _sb_tile(q, kj, scale, carry_l, tri_right, mask)
            g = _dot(dob, vj, 1, 1) * ab.astype(F32)
            carry_g = carry_g + jnp.sum(g, axis=1, keepdims=True)
            left = (delta - carry_g) + _dot(g.astype(BF16), tri_left, 1, 0)
            dz = g - jnp.exp(lsz) * (g + left)
            if mask is not None:
                dz = jnp.where(mask, dz, 0.0)
            dzb = dz.astype(BF16)
            dk_acc[pl.ds(start, tk), :] += _dot(dzb, q, 0, 0)
            dv_acc[pl.ds(start, tk), :] += _dot(ab, dob, 0, 0)
            return carry_l + jnp.sum(l1b.astype(F32), axis=1, keepdims=True), carry_g, dq + _dot(dzb, kj, 1, 0)

        zero = jnp.zeros((tq, 1), F32)
        state = (zero, zero, jnp.zeros((tq, dh), F32))
        for dj in reversed(range(nd)):
            state = tile(i * nd + dj, *state, masks[dj])
        state = lax.fori_loop(0, i * nd, lambda t, st: tile(i * nd - 1 - t, st[0], st[1], st[2], None), state)
        dq_ref[...] = (state[2] * scale).astype(dq_ref.dtype)

        @pl.when(i == nq - 1)
        def _():
            dk_ref[...] = (dk_acc[...] * scale).astype(dk_ref.dtype)
            dv_ref[...] = dv_acc[...].astype(dv_ref.dtype)

    qspec = pl.BlockSpec((tq, dh), lambda h, i: (i, col0[0] + h))
    kspec = pl.BlockSpec((s, dh), lambda h, i: (0, col0[1] + h))
    vspec = pl.BlockSpec((s, dh), lambda h, i: (0, col0[2] + h))
    blk = pl.BlockSpec((tq, dh), lambda h, i: (i, h))
    full = pl.BlockSpec((s, dh), lambda h, i: (0, h))
    act = jax.ShapeDtypeStruct((s, heads * dh), BF16)
    return _pcall(
        body, name=name, grid=(heads, nq), in_specs=[qspec, kspec, vspec, blk, blk],
        out_specs=[blk, full, full], out_shape=[act, act, act],
        scratch_shapes=[pltpu.VMEM((s, dh), F32), pltpu.VMEM((s, dh), F32)],
        compiler_params=_params("parallel", "arbitrary"),
    )(proj, proj, proj, o, do)


def _xattn_probs(q, k, scale):
    sc = _dot(q, k, 1, 1) * scale
    e = jnp.exp(sc - jnp.max(sc, axis=1, keepdims=True))
    return e / jnp.sum(e, axis=1, keepdims=True)


def _xattn_fwd(qc, kv, tq, name):
    s, d = qc.shape
    m = kv.shape[0]
    dh = d // X_HEADS
    scale = dh ** -0.5

    def body(q_ref, k_ref, v_ref, o_ref):
        p = _xattn_probs(q_ref[...], k_ref[...], scale)
        o_ref[...] = _dot(p.astype(BF16), v_ref[...], 1, 0).astype(o_ref.dtype)

    blk = pl.BlockSpec((tq, dh), lambda h, i: (i, h))
    return _pcall(
        body, name=name, grid=(X_HEADS, s // tq),
        in_specs=[blk, pl.BlockSpec((m, dh), lambda h, i: (0, h)), pl.BlockSpec((m, dh), lambda h, i: (0, X_HEADS + h))],
        out_specs=blk, out_shape=jax.ShapeDtypeStruct((s, d), BF16), compiler_params=_params("parallel", "parallel"),
    )(qc, kv, kv)


def _xattn_bwd(qc, kv, do, tq, name):
    s, d = qc.shape
    m = kv.shape[0]
    dh = d // X_HEADS
    scale = dh ** -0.5
    nq = s // tq

    def body(q_ref, k_ref, v_ref, do_ref, dq_ref, dk_ref, dv_ref, dk_acc, dv_acc):
        i = pl.program_id(1)
        q, k, v = q_ref[...], k_ref[...], v_ref[...]
        dob = do_ref[...].astype(BF16)
        p = _xattn_probs(q, k, scale)
        pb = p.astype(BF16)
        dp = _dot(dob, v, 1, 1)
        ds = pb.astype(F32) * (dp - jnp.sum(dp * pb.astype(F32), axis=1, keepdims=True))
        dsb = (ds * scale).astype(BF16)
        dq_ref[...] = _dot(dsb, k, 1, 0).astype(dq_ref.dtype)
        dk_part = _dot(dsb, q, 0, 0)
        dv_part = _dot(pb, dob, 0, 0)

        @pl.when(i == 0)
        def _():
            dk_acc[...] = dk_part
            dv_acc[...] = dv_part

        @pl.when(i > 0)
        def _():
            dk_acc[...] += dk_part
            dv_acc[...] += dv_part

        @pl.when(i == nq - 1)
        def _():
            dk_ref[...] = dk_acc[...].astype(dk_ref.dtype)
            dv_ref[...] = dv_acc[...].astype(dv_ref.dtype)

    blk = pl.BlockSpec((tq, dh), lambda h, i: (i, h))
    kblk = pl.BlockSpec((m, dh), lambda h, i: (0, h))
    return _pcall(
        body, name=name, grid=(X_HEADS, nq),
        in_specs=[blk, kblk, pl.BlockSpec((m, dh), lambda h, i: (0, X_HEADS + h)), blk],
        out_specs=[blk, kblk, kblk],
        out_shape=[jax.ShapeDtypeStruct((s, d), BF16), jax.ShapeDtypeStruct((m, d), BF16), jax.ShapeDtypeStruct((m, d), BF16)],
        scratch_shapes=[pltpu.VMEM((m, dh), F32), pltpu.VMEM((m, dh), F32)],
        compiler_params=_params("parallel", "arbitrary"),
    )(qc, kv, kv, do)


def _local_step(x, mem, tgt, w, fetch=None, prefetch=None, emit=None, tick=None, after=None):
    fetch = fetch or (lambda name, after: {})
    prefetch = prefetch or (lambda name, after: None)
    emit = emit or (lambda group, g: None)
    tick = tick or (lambda group, after: None)
    w = dict(w)
    s, d = x.shape
    heads = d // SB_HEAD_DIM
    tm = _pick(s, (512, 256, 128))
    tq = _pick(s, (256, 128))
    sb_tq, sb_tk = _pick(s, (512, 256, 128)), _pick(s, (256, 128))
    tc = _pick(d, (256, 128))
    g = {}

    def wt(name, after):
        if name not in w:
            w.update(fetch(name, after))
        return w[name]

    def ffn_fwd(h, gname, wgu, wdown, tag, after=None):
        n = _rms_fwd(h, w[gname], tag + "_norm", tm, after=after)
        gu, act = _ffn_up(n, wt(wgu, n), tag + "_gu")
        prefetch(wdown, n)
        return n, gu, act, _mm(act, wt(wdown, act), name=tag + "_down", out_dtype=F32, res=h, alpha=0.5)

    def ffn_bwd(dh, dhb, h, saved, gname, wgu, wdown, tag, copy_scale=None):
        n, gu, act = saved
        g[wdown] = _mm(act, dhb, ta=True, name=tag + "_dwdown")
        dgu = _ffn_dgu(dhb, w[wdown], gu, tag + "_dgu")
        g[wgu] = _mm(n, dgu, ta=True, b_halves=True, name=tag + "_dwgu")
        dn = _mm(dgu, w[wgu], tb=True, a_halves=True, name=tag + "_dn", out_dtype=F32, after=emit(tag, g))
        *dh_in, g[gname] = _resid_rms_bwd(dh, dn, h, w[gname], tag + "_dnorm", tm, after=tick(tag, dn), copy_scale=copy_scale)
        return dh_in

    n1, gu1, act1, h1 = ffn_fwd(x, "g_ffn1", "w_ffn1_gu", "w_ffn1_down", "ffn1", after)
    prefetch("w_in", h1)
    u = _rms_fwd(h1, w["g_mix"], "mix_norm", tm)
    proj = _mm(u, wt("w_in", u), name="mix_in")
    prefetch("w_conv_out", proj)
    nd = d // SB_HEAD_DIM
    y_conv = _conv_fwd(proj, w["conv_w"], d, tc, "conv_fwd")
    sb_cols = (3 * nd, 4 * nd, 5 * nd)
    y_sb = _sb_fwd(proj, heads, sb_cols, sb_tq, sb_tk, "sb_fwd")
    a_conv = _mm(y_conv, wt("w_conv_out", y_conv), name="conv_out")
    a_sb = _mm(y_sb, wt("w_attn_out", y_sb), name="attn_out")
    b_conv, b_sb = w["b_gate"][:, :d], w["b_gate"][:, d:]

    def merge(ac, asb, gcp, gsp, bc, bs):
        gc = _sigmoid(gcp.astype(F32) + bc)
        gs = _sigmoid(gsp.astype(F32) + bs)
        return gc * ac.astype(F32) + gs * asb.astype(F32)

    merged = _rowcall(merge, [_whole(a_conv), _whole(a_sb), (proj, 6, d), (proj, 7, d)], [b_conv, b_sb], [(d, BF16)],
                      tm=tm, name="merge")[0]
    prefetch("w_ffn2_gu", merged)
    h2 = _mm(merged, wt("w_o", merged), name="mix_out", out_dtype=F32, res=h1)
    hn = _rms_fwd(h2, w["g_cross"], "cross_norm", tm)
    mn = _rms_fwd(mem, w["g_mem"], "mem_norm", _pick(mem.shape[0], (256, 128)))
    qc = _mm(hn, wt("w_cq", hn), name="cross_q")
    kv = _mm(mn, wt("w_ckv", mn), name="cross_kv")
    oc = _xattn_fwd(qc, kv, tq, "xattn_fwd")
    h3 = _mm(oc, wt("w_co", oc), name="cross_out", out_dtype=F32, res=h2)
    n2, gu2, act2, h4 = ffn_fwd(h3, "g_ffn2", "w_ffn2_gu", "w_ffn2_down", "ffn2")

    def head(hb, tb, gb):
        xh, r = _xhat(hb)
        err = xh * gb - tb
        dy = err * (1.0 / d)
        dxh = dy * gb
        dx = r * (dxh - xh * jnp.mean(dxh * xh, axis=-1, keepdims=True))
        row_loss = 0.5 * jnp.mean(err * err, axis=-1, keepdims=True)
        return dx, 0.5 * dx, dy * xh, jnp.broadcast_to(row_loss, (row_loss.shape[0], LANES))

    dh4, dh4b, g["g_final"], loss_lanes = _rowcall(head, [_whole(h4), _whole(tgt)], [w["g_final"]], [(d, F32), (d, BF16)],
                                                   [d, LANES], tm=tm, name="loss_head")

    dh3, dh3b = ffn_bwd(dh4, dh4b, h3, (n2, gu2, act2), "g_ffn2", "w_ffn2_gu", "w_ffn2_down", "ffn2", copy_scale=1.0)
    g["w_co"] = _mm(oc, dh3b, ta=True, name="cross_dwco")
    doc = _mm(dh3b, w["w_co"], tb=True, name="cross_doc")
    dqc, dk, dv = _xattn_bwd(qc, kv, doc, tq, "xattn_bwd")
    dkv = jnp.concatenate([dk, dv], axis=1)
    g["w_cq"] = _mm(hn, dqc, ta=True, name="cross_dwcq")
    g["w_ckv"] = _mm(mn, dkv, ta=True, name="cross_dwckv")
    dhn = _mm(dqc, w["w_cq"], tb=True, name="cross_dhn", out_dtype=F32, after=emit("cross", g))
    dmn = _mm(dkv, w["w_ckv"], tb=True, name="cross_dmn", out_dtype=F32)
    g["g_mem"] = _rowcall(lambda dy, xb: dy * _xhat(xb)[0], [_whole(dmn), _whole(mem)], [], [], [d],
                          tm=_pick(mem.shape[0], (256, 128)), name="mem_dnorm")[0]
    dh2, dh2b, g["g_cross"] = _resid_rms_bwd(dh3, dhn, h2, w["g_cross"], "cross_dnorm", tm, after=tick("cross", dhn), copy_scale=1.0)

    g["w_o"] = _mm(merged, dh2b, ta=True, name="mix_dwo")
    dmerged = _mm(dh2b, w["w_o"], tb=True, name="mix_dmerged")

    def merge_bwd(dm, ac, asb, gcp, gsp, bc, bs):
        dm, ac, asb = dm.astype(F32), ac.astype(F32), asb.astype(F32)
        gc = _sigmoid(gcp.astype(F32) + bc)
        gs = _sigmoid(gsp.astype(F32) + bs)
        dgc = dm * ac * gc * (1.0 - gc)
        dgs = dm * asb * gs * (1.0 - gs)
        return dm * gc, dm * gs, dgc, dgs, dgc, dgs

    da_conv, da_sb, dgc, dgs, db_conv, db_sb = _rowcall(
        merge_bwd, [_whole(dmerged), _whole(a_conv), _whole(a_sb), (proj, 6, d), (proj, 7, d)], [b_conv, b_sb],
        [(d, BF16)] * 4, [d, d], tm=tm, name="merge_bwd")
    g["b_gate"] = jnp.concatenate([db_conv, db_sb], axis=1)
    g["w_conv_out"] = _mm(y_conv, da_conv, ta=True, name="conv_dwout")
    g["w_attn_out"] = _mm(y_sb, da_sb, ta=True, name="attn_dwout")
    dy_conv = _mm(da_conv, w["w_conv_out"], tb=True, name="conv_dy")
    dy_sb = _mm(da_sb, w["w_attn_out"], tb=True, name="attn_dy")
    dcb, dcc, dcx, g["conv_w"] = _conv_bwd(dy_conv, proj, w["conv_w"], d, tc, "conv_bwd")
    dq, dk_sb, dv_sb = _sb_bwd(proj, y_sb, dy_sb, heads, sb_cols, sb_tq, sb_tk, "sb_bwd")
    dproj = jnp.concatenate([dcb, dcc, dcx, dq, dk_sb, dv_sb, dgc, dgs], axis=1)
    g["w_in"] = _mm(u, dproj, ta=True, name="mix_dwin")
    du = _mm(dproj, w["w_in"], tb=True, name="mix_du", out_dtype=F32, after=emit("mix", g))
    dh1, dh1b, g["g_mix"] = _resid_rms_bwd(dh2, du, h1, w["g_mix"], "mix_dnorm", tm, after=tick("mix", du), copy_scale=0.5)
    dx, = ffn_bwd(dh1, dh1b, x, (n1, gu1, act1), "g_ffn1", "w_ffn1_gu", "w_ffn1_down", "ffn1")
    return loss_lanes, dx, g


MATS = (("w_ffn1_gu", "col"), ("w_ffn1_down", "row"), ("w_in", "col"), ("w_conv_out", "row"), ("w_attn_out", "row"),
        ("w_o", "row"), ("w_cq", "row"), ("w_ckv", "col"), ("w_co", "row"), ("w_ffn2_gu", "col"), ("w_ffn2_down", "row"))
VECS = ("g_ffn1", "g_mix", "g_cross", "g_mem", "g_ffn2", "g_final")
WEIGHTS = ("g_ffn1", "w_ffn1_gu", "w_ffn1_down", "g_mix", "w_in", "b_gate", "conv_w", "w_conv_out", "w_attn_out", "w_o",
           "g_cross", "g_mem", "w_cq", "w_ckv", "w_co", "g_ffn2", "w_ffn2_gu", "w_ffn2_down", "g_final")
CONV_ROWS = 8


def _full_shape(kind, r, c):
    return (r, N_CHIPS * c) if kind == "col" else (N_CHIPS * r, c)


def _piece(ref, kind, r, c, chip, half):
    hr = r // 2
    if kind == "col":
        return ref.at[pl.ds(pl.multiple_of(half * hr, 16), hr), pl.ds(pl.multiple_of(chip * c, LANES), c)]
    return ref.at[pl.ds(pl.multiple_of(chip * r + half * hr, 16), hr), :]


def _shard_of(ref, kind, r, c, chip):
    if kind == "col":
        return ref.at[:, pl.ds(pl.multiple_of(chip * c, LANES), c)]
    return ref.at[pl.ds(pl.multiple_of(chip * r, 16), r), :]


def _place():
    x, y, c = lax.axis_index("x"), lax.axis_index("y"), lax.axis_index("c")
    others = [(1 - x, y), (x, 1 - y), (1 - x, 1 - y)]
    return x, y, c, 2 * x + y, others


def _remote(src, dst, send_sem, recv_sem, to):
    return pltpu.make_async_remote_copy(src_ref=src, dst_ref=dst, send_sem=send_sem, recv_sem=recv_sem,
                                        device_id=to, device_id_type=MESH)


def _gather_conv(conv_shard):
    cc = conv_shard.shape[1]

    def body(conv_ref, conv_full, cs, cr, cl):
        x, y, c, me, others = _place()

        def cols(chip):
            return conv_full.at[:, pl.ds(pl.multiple_of(chip * cc, LANES), cc)]

        def conv(k, chip_from, to):
            return _remote(conv_ref, cols(chip_from), cs.at[k], cr.at[k], to)

        mine = pltpu.make_async_copy(conv_ref, cols(me), cl.at[0])
        mine.start()
        for k, (ox, oy) in enumerate(others):
            conv(k, me, (ox, oy, c)).start()
        for k, (ox, oy) in enumerate(others):
            conv(k, 2 * ox + oy, (x, y, c)).wait_recv()
            conv(k, me, (ox, oy, c)).wait_send()
        mine.wait()

    dma = pltpu.SemaphoreType.DMA
    return _pcall(
        body, name="gather_conv", in_specs=[ANY], out_specs=ANY,
        out_shape=jax.ShapeDtypeStruct((CONV_ROWS, N_CHIPS * cc), F32), scratch_shapes=[dma((3,)), dma((3,)), dma((1,))],
    )(conv_shard)


HBM = pl.BlockSpec(memory_space=pltpu.HBM)
SEM = pl.BlockSpec(memory_space=pltpu.SEMAPHORE)
EFFECT = pltpu.SideEffectType.DATAFLOW_SIDE_EFFECTING
TOKEN = (8, LANES)


def _split_start(name, plan, n_copies, srcs, lands, after=None):
    ns, nl = len(srcs), len(lands)
    n_in = ns + nl + (after is not None)

    def body(*refs):
        outs = refs[n_in:]
        sends, _ = plan(refs[:ns], refs[ns:ns + nl], outs[0], outs[1])
        for cp in sends:
            cp.start()
        outs[-1][...] = jnp.zeros(TOKEN, F32)

    held = [pltpu.HBM(a.shape, a.dtype) for a in (*srcs, *lands)]
    dma = pltpu.SemaphoreType.DMA((n_copies,))
    ins = [pltpu.with_memory_space_constraint(a, pltpu.HBM) for a in (*srcs, *lands)]
    outs = _pcall(
        body, name=name, in_specs=[HBM] * (ns + nl) + ([] if after is None else [ANY]),
        out_specs=(SEM, SEM, *[HBM] * (ns + nl), pl.BlockSpec(memory_space=pltpu.VMEM)),
        out_shape=(dma, dma, *held, jax.ShapeDtypeStruct(TOKEN, F32)),
        input_output_aliases={i: 2 + i for i in range(ns + nl)},
        compiler_params=pltpu.CompilerParams(has_side_effects=EFFECT),
    )(*ins, *([] if after is None else [after]))
    return outs[0], outs[1], list(outs[2:2 + ns]), list(outs[2 + ns:2 + ns + nl]), outs[-1]


def _split_wait(name, plan, send_sems, recv_sems, srcs, lands, after):
    ns, nl = len(srcs), len(lands)

    def body(*refs):
        sends, recvs = plan(refs[:ns], refs[ns:ns + nl], refs[ns + nl], refs[ns + nl + 1])
        for cp in sends:
            cp.wait_send()
        for cp in recvs:
            cp.wait_recv()

    outs = _pcall(
        body, name=name, in_specs=[HBM] * (ns + nl) + [SEM, SEM, ANY], out_specs=[HBM] * (ns + nl),
        out_shape=[pltpu.HBM(a.shape, a.dtype) for a in (*srcs, *lands)],
        input_output_aliases={i: i for i in range(ns + nl)},
        compiler_params=pltpu.CompilerParams(has_side_effects=EFFECT),
    )(*srcs, *lands, send_sems, recv_sems, after)
    return list(outs[:ns]), list(outs[ns:])


def _gather_plan(dims):
    def plan(shard_refs, full_refs, ss, rs):
        x, y, c, me, others = _place()
        sends, recvs = [], []
        for wi, (kind, r, cw) in enumerate(dims):
            half = shard_refs[wi].at[pl.ds(pl.multiple_of(c * (r // 2), 16), r // 2), :]
            for k, (ox, oy) in enumerate(others):
                sem = 4 * wi + k
                sends.append(_remote(half, _piece(full_refs[wi], kind, r, cw, me, c), ss.at[sem], rs.at[sem], (ox, oy, c)))
                recvs.append(_remote(half, _piece(full_refs[wi], kind, r, cw, 2 * ox + oy, c), ss.at[sem], rs.at[sem], (x, y, c)))
            sem = 4 * wi + 3
            own = _remote(shard_refs[wi], _shard_of(full_refs[wi], kind, r, cw, me), ss.at[sem], rs.at[sem], (x, y, 1 - c))
            sends.append(own)
            recvs.append(own)
        return sends, recvs

    return plan


def _forward_plan(dims):
    def plan(_, full_refs, ss, rs):
        x, y, c, _, others = _place()
        sends, recvs = [], []
        for wi, (kind, r, cw) in enumerate(dims):
            for k, (ox, oy) in enumerate(others):
                sem = 3 * wi + k
                mine = _piece(full_refs[wi], kind, r, cw, 2 * ox + oy, c)
                theirs = _piece(full_refs[wi], kind, r, cw, 2 * ox + oy, 1 - c)
                sends.append(_remote(mine, mine, ss.at[sem], rs.at[sem], (x, y, 1 - c)))
                recvs.append(_remote(theirs, theirs, ss.at[sem], rs.at[sem], (x, y, 1 - c)))
        return sends, recvs

    return plan


def _rs_cores_plan(dims):
    def plan(g_refs, land_refs, ss, rs):
        x, y, c, _, _ = _place()
        sends, recvs = [], []
        for wi, dm in enumerate(dims):
            for chip in range(N_CHIPS):
                sem = N_CHIPS * wi + chip
                sends.append(_remote(_piece(g_refs[wi], *dm, chip, 1 - c), land_refs[wi].at[chip], ss.at[sem], rs.at[sem], (x, y, 1 - c)))
                recvs.append(_remote(_piece(g_refs[wi], *dm, chip, c), land_refs[wi].at[chip], ss.at[sem], rs.at[sem], (x, y, 1 - c)))
        return sends, recvs

    return plan


def _share_plan(nw):
    def plan(_, buf_refs, ss, rs):
        x, y, c, _, _ = _place()
        sends = [_remote(buf_refs[wi].at[c], buf_refs[wi].at[c], ss.at[wi], rs.at[wi], (x, y, 1 - c)) for wi in range(nw)]
        recvs = [_remote(buf_refs[wi].at[1 - c], buf_refs[wi].at[1 - c], ss.at[wi], rs.at[wi], (x, y, 1 - c)) for wi in range(nw)]
        return sends, recvs

    return plan


def _small_plan():
    def plan(_, buf_refs, ss, rs):
        x, y, c = lax.axis_index("x"), lax.axis_index("y"), lax.axis_index("c")
        buf = buf_refs[0]
        sends, recvs = [], []
        for rel in range(1, N_DEV):
            peer = (x ^ (rel >> 2 & 1), y ^ (rel >> 1 & 1), c ^ (rel & 1))
            sends.append(_remote(buf.at[0], buf.at[rel], ss.at[rel - 1], rs.at[rel - 1], peer))
            recvs.append(_remote(buf.at[0], buf.at[rel], ss.at[rel - 1], rs.at[rel - 1], peer))
        return sends, recvs

    return plan


def _sum_small(buf, me, name):
    _, rows, n = buf.shape

    def body(me_ref, b_ref, o_ref):
        tot = b_ref[me_ref[0]]
        for dev in range(1, N_DEV):
            tot = tot + b_ref[dev ^ me_ref[0]]
        o_ref[...] = tot

    return _pcall(
        body, name=name, out_shape=jax.ShapeDtypeStruct((rows, n), F32),
        grid_spec=pltpu.PrefetchScalarGridSpec(
            num_scalar_prefetch=1, grid=(1,), in_specs=[pl.BlockSpec((N_DEV, rows, n), lambda i, m: (0, 0, 0))],
            out_specs=pl.BlockSpec((rows, n), lambda i, m: (0, 0))),
    )(me, buf)


def _rs_chips_plan(nw):
    def plan(p_refs, land_refs, ss, rs):
        x, y, c, me, others = _place()
        sends, recvs = [], []
        for wi in range(nw):
            for k, (ox, oy) in enumerate(others):
                sem = 3 * wi + k
                sends.append(_remote(p_refs[wi].at[2 * ox + oy], land_refs[wi].at[k], ss.at[sem], rs.at[sem], (ox, oy, c)))
                recvs.append(_remote(p_refs[wi].at[me], land_refs[wi].at[k], ss.at[sem], rs.at[sem], (x, y, c)))
        return sends, recvs

    return plan


def _rows_per_block(n, c, limit_bytes=2 << 20):
    best = None
    for tm in range(16, n + 1, 16):
        if n % tm == 0 and tm * c * 4 <= limit_bytes:
            best = tm
    return best or n


def _sum_cores(grad, got, kind, place, name):
    _, hr, cw = got.shape
    tm = _rows_per_block(hr, cw)
    nb = hr // tm

    def body(place_ref, g_ref, t_ref, o_ref):
        o_ref[...] = (g_ref[...].astype(F32) + t_ref[...].astype(F32)).astype(o_ref.dtype)

    if kind == "col":
        g_spec = pl.BlockSpec((tm, cw), lambda j, i, pr: (pr[0] * nb + i, j))
    else:
        g_spec = pl.BlockSpec((tm, cw), lambda j, i, pr: ((2 * j + pr[0]) * nb + i, 0))
    blk = pl.BlockSpec((None, tm, cw), lambda j, i, pr: (j, i, 0))
    return _pcall(
        body, name=name, out_shape=jax.ShapeDtypeStruct(got.shape, BF16),
        grid_spec=pltpu.PrefetchScalarGridSpec(num_scalar_prefetch=1, grid=(N_CHIPS, nb), in_specs=[g_spec, blk], out_specs=blk),
        compiler_params=_params("parallel", "parallel"),
    )(place, grad, got)


def _sum_chips(parts, got, place, name):
    _, n, cw = got.shape
    tm = _rows_per_block(n, cw)

    def body(place_ref, p_ref, g_ref, o_ref):
        tot = p_ref[...].astype(F32)
        for k in range(3):
            tot = tot + g_ref[k].astype(F32)
        o_ref[...] = tot

    return _pcall(
        body, name=name, out_shape=jax.ShapeDtypeStruct((2, n, cw), F32),
        grid_spec=pltpu.PrefetchScalarGridSpec(
            num_scalar_prefetch=1, grid=(n // tm,),
            in_specs=[pl.BlockSpec((None, tm, cw), lambda i, pr: (pr[1], i, 0)), pl.BlockSpec((3, tm, cw), lambda i, pr: (0, i, 0))],
            out_specs=pl.BlockSpec((None, tm, cw), lambda i, pr: (pr[0], i, 0))),
        compiler_params=_params("parallel"),
    )(place, parts, got)


def _adamw(g, w, m, v, name):
    n, c = g.shape
    c1 = 1.0 - ADAM_B1 ** ADAM_STEP
    c2 = 1.0 - ADAM_B2 ** ADAM_STEP

    def fn(gb, wb, mb, vb):
        m_new = ADAM_B1 * mb + (1.0 - ADAM_B1) * gb
        v_new = ADAM_B2 * vb + (1.0 - ADAM_B2) * (gb * gb)
        delta = -ADAM_LR * ((m_new / c1) / (jnp.sqrt(v_new / c2) + ADAM_EPS) + ADAM_WD * wb)
        return gb, delta, m_new, v_new

    tm = _rows_per_block(n, c) if n % 16 == 0 else n
    return _rowcall(fn, [_whole(g), _whole(w), _whole(m), _whole(v)], [], [(c, F32)] * 4, tm=tm, name=name)


PACK_ROWS = 16


def _pack_rows(parts, width, name, after=None):
    assert sum(p.shape[0] for p in parts) <= PACK_ROWS

    def body(*refs):
        out_ref = refs[-1]
        out_ref[...] = jnp.zeros_like(out_ref)
        at = 0
        for r in refs[:len(parts)]:
            k, n = r.shape
            if n == width:
                out_ref[at:at + k, :] = r[...]
            else:
                out_ref[at:at + k, :] = jnp.broadcast_to(r[:, :1], (k, width))
            at += k

    vm = pl.BlockSpec(memory_space=pltpu.VMEM)
    return _pcall(body, name=name, in_specs=[vm] * len(parts) + ([] if after is None else [ANY]), out_specs=vm,
                  out_shape=jax.ShapeDtypeStruct((PACK_ROWS, width), F32))(*parts, *([] if after is None else [after]))


def _cast_shard(wm, name, after):
    n, c = wm.shape
    return _rowcall(lambda v: v, [_whole(wm)], [], [(c, BF16)], tm=_rows_per_block(n, c), name=name, after=after)[0]


GATHER_GROUPS = (
    ("w_ffn1_gu",), ("w_ffn1_down",), ("w_in",), ("w_conv_out", "w_attn_out", "w_o", "w_cq", "w_ckv", "w_co"),
    ("w_ffn2_gu", "w_ffn2_down"),
)
REDUCE_GROUPS = {
    "ffn2": ("w_ffn2_down", "w_ffn2_gu"),
    "mix": ("w_co", "w_cq", "w_ckv", "w_o", "w_conv_out", "w_attn_out", "w_in"),
    "ffn1": ("w_ffn1_down", "w_ffn1_gu"),
}
KIND = dict(MATS)


def _step(x, mem, tgt, wts, m_in, v_in):
    d = x.shape[-1]
    cc = wts["conv_w"].shape[1]
    place = jnp.stack([lax.axis_index("c"), 2 * lax.axis_index("x") + lax.axis_index("y")]).astype(jnp.int32)
    dims = {n: (kind, *wts[n].shape) for n, kind in MATS}

    conv_pad = jnp.pad(wts["conv_w"], ((0, CONV_ROWS - CONV_K), (0, 0)))
    conv_full = _gather_conv(conv_pad)
    w = {"conv_w": conv_full[:CONV_K]}
    for n in VECS + ("b_gate",):
        w[n] = wts[n].reshape(1, -1)
    flying, token = {}, conv_full
    for names in GATHER_GROUPS:
        gd = [dims[n] for n in names]
        shards = [_cast_shard(wts[n], "cast_" + n, token) for n in names]
        lands = [lax.empty(_full_shape(*dm), BF16) for dm in gd]
        plan = _gather_plan(gd)
        ss, rs, srcs, lands, token = _split_start("gather_start_" + names[0], plan, 4 * len(names), shards, lands, token)
        flying.update({n: (names, plan, ss, rs, srcs, lands, gd) for n in names})

    passing = {}

    def prefetch(name, after):
        if name not in passing:
            names, plan, ss, rs, srcs, lands, gd = flying[name]
            _, lands = _split_wait("gather_wait_" + names[0], plan, ss, rs, srcs, lands, after)
            plan = _forward_plan(gd)
            ss, rs, _, lands, _ = _split_start("forward_start_" + names[0], plan, 3 * len(names), [], lands)
            passing.update({n: (names, plan, ss, rs, lands) for n in names})

    def fetch(name, after):
        prefetch(name, after)
        names, plan, ss, rs, lands = passing[name]
        _, lands = _split_wait("forward_wait_" + names[0], plan, ss, rs, [], lands, after)
        return dict(zip(names, lands))

    swapping, sent = {}, {}

    def emit(tag, g):
        if tag not in REDUCE_GROUPS:
            return None
        names = REDUCE_GROUPS[tag]
        gd = [dims[n] for n in names]
        lands = [lax.empty((N_CHIPS, r // 2, cw), BF16) for (_, r, cw) in gd]
        plan = _rs_cores_plan(gd)
        ss, rs, srcs, lands, tok = _split_start("rs_cores_start_" + tag, plan, N_CHIPS * len(names), [g[n] for n in names], lands)
        swapping[tag] = (plan, ss, rs, srcs, lands)
        return tok

    def tick(tag, after):
        if tag not in REDUCE_GROUPS:
            return None
        names = REDUCE_GROUPS[tag]
        plan, ss, rs, srcs, lands = swapping[tag]
        mine, got = _split_wait("rs_cores_wait_" + tag, plan, ss, rs, srcs, lands, after)
        parts = [_sum_cores(gm, t, KIND[n], place, "sum_cores_" + n) for n, gm, t in zip(names, mine, got)]
        lands = [lax.empty((3, *p.shape[1:]), BF16) for p in parts]
        plan = _rs_chips_plan(len(names))
        ss, rs, srcs, lands, tok = _split_start("rs_chips_start_" + tag, plan, 3 * len(names), parts, lands)
        sent[tag] = (plan, ss, rs, srcs, lands)
        return tok

    loss_lanes, dx, g = _local_step(x[0], mem[0], tgt[0], w, fetch, prefetch, emit, tick, token)

    rows = [g[n] for n in VECS] + [g["b_gate"][:, :d], g["b_gate"][:, d:], g["conv_w"], loss_lanes]
    packed = _pack_rows(rows, d, "pack_small")
    small = jnp.concatenate([packed[None], jnp.zeros((N_DEV - 1, *packed.shape), F32)], axis=0)
    small_plan = _small_plan()
    small_ss, small_rs, _, small, after = _split_start("small_start", small_plan, N_DEV - 1, [], [small])

    grads, out = {}, {}

    def update(n):
        shape = wts[n].shape
        as2d = (lambda a: a.reshape(1, -1)) if len(shape) == 1 else (lambda a: a)
        return [r.reshape(shape) for r in _adamw(grads[n], as2d(wts[n]), as2d(m_in[n]), as2d(v_in[n]), "adamw_" + n)]

    def finish(sharing, after):
        tag, names, plan, ss, rs, halves = sharing
        _, both = _split_wait("share_wait_" + tag, plan, ss, rs, [], halves, after)
        for n, b in zip(names, both):
            grads[n] = b.reshape(-1, b.shape[-1])
            out[n] = update(n)
        return out[names[-1]][1]

    sharing = None
    for tag, names in REDUCE_GROUPS.items():
        plan, ss, rs, srcs, lands = sent[tag]
        parts, landed = _split_wait("rs_chips_wait_" + tag, plan, ss, rs, srcs, lands, after)
        halves = [_sum_chips(p, t, place, "sum_chips_" + n) for n, p, t in zip(names, parts, landed)]
        plan = _share_plan(len(names))
        ss, rs, _, halves, after = _split_start("share_start_" + tag, plan, len(names), [], halves)
        if sharing is not None:
            after = finish(sharing, after)
        sharing = (tag, names, plan, ss, rs, halves)
    after = finish(sharing, after)

    _, small = _split_wait("small_wait", small_plan, small_ss, small_rs, [], small, after)
    me = (4 * lax.axis_index("x") + 2 * lax.axis_index("y") + lax.axis_index("c")).astype(jnp.int32).reshape(1)
    red = _sum_small(small[0], me, "sum_small")
    grads.update({n: red[i:i + 1] for i, n in enumerate(VECS)})
    nv = len(VECS)
    grads["b_gate"] = jnp.concatenate([red[nv:nv + 1], red[nv + 1:nv + 2]], axis=1)
    chip = 2 * lax.axis_index("x") + lax.axis_index("y")
    grads["conv_w"] = lax.dynamic_slice_in_dim(red[nv + 2:nv + 2 + CONV_K], chip * cc, cc, axis=1)
    loss = red[nv + 2 + CONV_K, 0]
    out.update({n: update(n) for n in WEIGHTS if n not in KIND})
    return (loss, dx[None], *[out[n][0] for n in WEIGHTS], *[out[n][1] for n in WEIGHTS],
            *[out[n][2] for n in WEIGHTS], *[out[n][3] for n in WEIGHTS])


def kernel(x, mem, g_ffn1, w_ffn1_gu, w_ffn1_down, g_mix, w_in, b_gate, conv_w, w_conv_out, w_attn_out, w_o, g_cross, g_mem, w_cq, w_ckv, w_co, g_ffn2, w_ffn2_gu, w_ffn2_down, g_final, loss_target, m_g_ffn1, m_w_ffn1_gu, m_w_ffn1_down, m_g_mix, m_w_in, m_b_gate, m_conv_w, m_w_conv_out, m_w_attn_out, m_w_o, m_g_cross, m_g_mem, m_w_cq, m_w_ckv, m_w_co, m_g_ffn2, m_w_ffn2_gu, m_w_ffn2_down, m_g_final, v_g_ffn1, v_w_ffn1_gu, v_w_ffn1_down, v_g_mix, v_w_in, v_b_gate, v_conv_w, v_w_conv_out, v_w_attn_out, v_w_o, v_g_cross, v_g_mem, v_w_cq, v_w_ckv, v_w_co, v_g_ffn2, v_w_ffn2_gu, v_w_ffn2_down, v_g_final):
    given = dict(locals())
    wts = {n: given[n] for n in WEIGHTS}
    m_in = {n: given["m_" + n] for n in WEIGHTS}
    v_in = {n: given["v_" + n] for n in WEIGHTS}
    return _step(x, mem, loss_target, wts, m_in, v_in)
```

```python
import functools

import jax
import jax.numpy as jnp
from jax import lax
from jax.experimental import pallas as pl
from jax.experimental.pallas import tpu as pltpu

F32 = jnp.float32
BF16 = jnp.bfloat16
MESH = pl.DeviceIdType.MESH

V7X_VMEM_LIMIT_BYTES = 48 * 1024 * 1024
MM_VMEM_BUDGET_BYTES = 36 * 1024 * 1024
MM_WHOLE_K = 2816
LANES = 128
SB_HEAD_DIM = 128
X_HEADS = 4
CONV_K = 3
RMS_EPS = 1e-6
N_CHIPS = 4
N_DEV = 8
ADAM_LR, ADAM_B1, ADAM_B2, ADAM_EPS, ADAM_WD, ADAM_STEP = 0.001, 0.9, 0.999, 1e-08, 0.01, 10


ANY = pl.BlockSpec(memory_space=pl.ANY)


def _pcall(body, **kw):
    return pl.pallas_call(body, **kw)


def _params(*sem):
    return pltpu.CompilerParams(dimension_semantics=sem, vmem_limit_bytes=V7X_VMEM_LIMIT_BYTES)


def _pick(dim, cands):
    for c in cands:
        if dim % c == 0:
            return c
    return dim


def _dot(a, b, ca, cb):
    return lax.dot_general(a, b, (((ca,), (cb,)), ((), ())), preferred_element_type=F32)


def _mm(a, b, *, name, ta=False, tb=False, out_dtype=BF16, res=None, alpha=1.0, tm=None, tn=None, tk=None, after=None,
        a_halves=False, b_halves=False):
    assert not (a_halves and ta) and not (b_halves and tb)
    if a_halves:
        m, k = a.shape[1], 2 * a.shape[2]
    else:
        m, k = (a.shape[1], a.shape[0]) if ta else a.shape
    if b_halves:
        n = 2 * b.shape[2]
        assert k == b.shape[1]
    else:
        n = b.shape[0] if tb else b.shape[1]
        assert k == (b.shape[1] if tb else b.shape[0]), (a.shape, b.shape, ta, tb)
    if ta:
        tm = tm or _pick(m, (512, 256, 128))
        tn = tn or _pick(n, (1024, 512, 256, 128))
        tk = tk or (k if k <= MM_WHOLE_K else _pick(k, (1024, 512, 256, 128)))
    else:
        tk = tk or (k if k <= MM_WHOLE_K else _pick(k, (MM_WHOLE_K, 2048, 1024, 512, 256, 128)))
        tn = tn or _pick(n, (512, 1408, 256, 128) if tk == k else (1024, 512, 256, 128))
        per_row = 2 * (tk * a.dtype.itemsize + tn * (jnp.dtype(out_dtype).itemsize + (0 if res is None else res.dtype.itemsize)))
        per_row += 4 * tn if tk < k else 0
        rows = (MM_VMEM_BUDGET_BYTES - 2 * tk * tn * b.dtype.itemsize) // per_row
        tm = tm or next((c for c in (2048, 1024, 512, 256, 128) if m % c == 0 and c <= rows), m)
    if a_halves:
        tk = min(tk, k // 2) if (k // 2) % min(tk, k // 2) == 0 else _pick(k // 2, (1408, 1024, 512, 256, 128))
    if b_halves:
        tn = tn if (n // 2) % tn == 0 else _pick(n // 2, (1408, 1024, 512, 256, 128))
    nk = k // tk
    assert m % tm == 0 and n % tn == 0 and k % tk == 0
    a_spec = pl.BlockSpec((tk, tm), lambda i, j, kk: (kk, i)) if ta else pl.BlockSpec((tm, tk), lambda i, j, kk: (i, kk))
    b_spec = pl.BlockSpec((tn, tk), lambda i, j, kk: (j, kk)) if tb else pl.BlockSpec((tk, tn), lambda i, j, kk: (kk, j))
    if a_halves:
        per = (k // 2) // tk
        a_spec = pl.BlockSpec((None, tm, tk), lambda i, j, kk: (kk // per, i, kk % per))
    if b_halves:
        per_n = (n // 2) // tn
        b_spec = pl.BlockSpec((None, tk, tn), lambda i, j, kk: (j // per_n, kk, j % per_n))
    o_spec = pl.BlockSpec((tm, tn), lambda i, j, kk: (i, j))
    ca, cb = (0 if ta else 1), (1 if tb else 0)

    n_in = 2 + (res is not None) + (after is not None)

    def body(*refs):
        a_ref, b_ref = refs[:2]
        res_ref = refs[2] if res is not None else None
        o_ref = refs[n_in]
        scratch = refs[n_in + 1:]

        def finish(acc):
            val = acc if alpha == 1.0 else alpha * acc
            if res_ref is not None:
                val = res_ref[...].astype(F32) + val
            o_ref[...] = val.astype(o_ref.dtype)

        part = _dot(a_ref[...].astype(BF16), b_ref[...].astype(BF16), ca, cb)
        if nk == 1:
            finish(part)
        else:
            acc_ref = scratch[0]
            kk = pl.program_id(2)

            @pl.when(kk == 0)
            def _():
                acc_ref[...] = part

            @pl.when(kk > 0)
            def _():
                acc_ref[...] += part

            @pl.when(kk == nk - 1)
            def _():
                finish(acc_ref[...])

    ins = [a, b] + ([] if res is None else [res]) + ([] if after is None else [after])
    in_specs = [a_spec, b_spec] + ([] if res is None else [o_spec]) + ([] if after is None else [ANY])
    return _pcall(
        body, name=name, grid=(m // tm, n // tn, nk), in_specs=in_specs, out_specs=o_spec,
        out_shape=jax.ShapeDtypeStruct((m, n), out_dtype),
        scratch_shapes=[pltpu.VMEM((tm, tn), F32)] if nk > 1 else [],
        compiler_params=_params("parallel", "parallel", "arbitrary"),
    )(*ins)


def _rowcall(fn, rows, consts, outs, accs=(), *, tm, name, after=None):
    s = rows[0][0].shape[0]
    assert s % tm == 0
    n_read, n_out = len(rows) + len(consts), len(outs)
    n_in = n_read + (after is not None)

    def body(*refs):
        vals = fn(*[r[...] for r in refs[:n_read]])
        vals = vals if isinstance(vals, (tuple, list)) else (vals,)
        for o_ref, v in zip(refs[n_in:n_in + n_out], vals[:n_out]):
            o_ref[...] = v.astype(o_ref.dtype)
        if accs:
            first = pl.program_id(0) == 0
            for a_ref, v in zip(refs[n_in + n_out:], vals[n_out:]):
                tot = jnp.sum(v.astype(F32), axis=0, keepdims=True)

                @pl.when(first)
                def _(a_ref=a_ref, tot=tot):
                    a_ref[...] = tot

                @pl.when(jnp.logical_not(first))
                def _(a_ref=a_ref, tot=tot):
                    a_ref[...] += tot

    in_specs = [pl.BlockSpec((tm, w), lambda i, cb=cb: (i, cb)) for (_, cb, w) in rows]
    in_specs += [pl.BlockSpec(c.shape, lambda i: (0, 0)) for c in consts]
    in_specs += [] if after is None else [ANY]
    out_specs = [pl.BlockSpec((tm, w), lambda i: (i, 0)) for (w, _) in outs]
    out_specs += [pl.BlockSpec((1, w), lambda i: (0, 0)) for w in accs]
    out_shape = [jax.ShapeDtypeStruct((s, w), dt) for (w, dt) in outs]
    out_shape += [jax.ShapeDtypeStruct((1, w), F32) for w in accs]
    return _pcall(
        body, name=name, grid=(s // tm,), in_specs=in_specs, out_specs=out_specs, out_shape=out_shape,
        compiler_params=_params("arbitrary" if accs else "parallel"),
    )(*[r[0] for r in rows], *consts, *([] if after is None else [after]))


def _whole(a):
    return (a, 0, a.shape[1])


def _xhat(x):
    x = x.astype(F32)
    r = lax.rsqrt(jnp.mean(x * x, axis=-1, keepdims=True) + RMS_EPS)
    return x * r, r


def _rms_bwd(dy, x, g):
    xh, r = _xhat(x)
    dxh = dy.astype(F32) * g
    dx = r * (dxh - xh * jnp.mean(dxh * xh, axis=-1, keepdims=True))
    return dx, dy.astype(F32) * xh


def _sigmoid(x):
    return 1.0 / (1.0 + jnp.exp(-x))


def _rms_fwd(x, g, name, tm, after=None):
    d = x.shape[1]
    return _rowcall(lambda xb, gb: _xhat(xb)[0] * gb, [_whole(x)], [g], [(d, BF16)], tm=tm, name=name, after=after)[0]


def _silu_parts(gate):
    sg = _sigmoid(gate)
    return sg, gate * sg


def _ffn_up(n, w_gu, name):
    s, d = n.shape
    f = w_gu.shape[1] // 2
    tn = _pick(f, (1408, 1024, 512, 256, 128))
    tm = _pick(s, (1024, 512, 256, 128))
    nb = f // tn

    def body(n_ref, wg_ref, wu_ref, gu_ref, act_ref):
        nv = n_ref[...]
        gate = _dot(nv, wg_ref[...], 1, 0)
        up = _dot(nv, wu_ref[...], 1, 0)
        gu_ref[0] = gate.astype(gu_ref.dtype)
        gu_ref[1] = up.astype(gu_ref.dtype)
        act_ref[...] = (_silu_parts(gate)[1] * up).astype(act_ref.dtype)

    return _pcall(
        body, name=name, grid=(s // tm, nb),
        in_specs=[pl.BlockSpec((tm, d), lambda i, j: (i, 0)), pl.BlockSpec((d, tn), lambda i, j: (0, j)),
                  pl.BlockSpec((d, tn), lambda i, j: (0, nb + j))],
        out_specs=[pl.BlockSpec((2, tm, tn), lambda i, j: (0, i, j)), pl.BlockSpec((tm, tn), lambda i, j: (i, j))],
        out_shape=[jax.ShapeDtypeStruct((2, s, f), BF16), jax.ShapeDtypeStruct((s, f), BF16)],
        compiler_params=_params("parallel", "parallel"),
    )(n, w_gu, w_gu)


def _ffn_dgu(dhb, w_down, gu, name):
    s, d = dhb.shape
    f = w_down.shape[0]
    tn = _pick(f, (1408, 1024, 512, 256, 128))
    tm = _pick(s, (1024, 512, 256, 128))

    def body(dh_ref, w_ref, gu_ref, o_ref):
        dact = _dot(dh_ref[...], w_ref[...], 1, 1)
        gate, up = gu_ref[0].astype(F32), gu_ref[1].astype(F32)
        sg, silu = _silu_parts(gate)
        o_ref[0] = (dact * up * (sg + silu * (1.0 - sg))).astype(o_ref.dtype)
        o_ref[1] = (dact * silu).astype(o_ref.dtype)

    blk = pl.BlockSpec((2, tm, tn), lambda i, j: (0, i, j))
    return _pcall(
        body, name=name, grid=(s // tm, f // tn),
        in_specs=[pl.BlockSpec((tm, d), lambda i, j: (i, 0)), pl.BlockSpec((tn, d), lambda i, j: (j, 0)), blk],
        out_specs=blk, out_shape=jax.ShapeDtypeStruct((2, s, f), BF16), compiler_params=_params("parallel", "parallel"),
    )(dhb, w_down, gu)


def _resid_rms_bwd(dh, dn, x, g, name, tm, after=None, copy_scale=None):
    d = x.shape[1]

    def fn(dhb, dnb, xb, gb):
        dx, dg = _rms_bwd(dnb, xb, gb)
        tot = dhb.astype(F32) + dx
        return (tot, dg) if copy_scale is None else (tot, copy_scale * tot, dg)

    outs = [(d, F32)] + ([] if copy_scale is None else [(d, BF16)])
    return _rowcall(fn, [_whole(dh), _whole(dn), _whole(x)], [g], outs, [d], tm=tm, name=name, after=after)


def _shift_down(p, k):
    if k == 0:
        return p
    rows = lax.broadcasted_iota(jnp.int32, p.shape, 0)
    return jnp.where(rows >= k, pltpu.roll(p, k, 0), 0.0)


def _shift_up(p, k):
    if k == 0:
        return p
    s = p.shape[0]
    rows = lax.broadcasted_iota(jnp.int32, p.shape, 0)
    return jnp.where(rows < s - k, pltpu.roll(p, s - k, 0), 0.0)


def _conv_fwd(proj, conv_w, d, tc, name):
    s = proj.shape[0]
    nb = d // tc

    def body(cb_ref, cc_ref, cx_ref, w_ref, y_ref):
        p = cc_ref[...].astype(F32) * cx_ref[...].astype(F32)
        w = w_ref[...]
        acc = p * w[CONV_K - 1:CONV_K, :]
        for k in range(1, CONV_K):
            acc = acc + _shift_down(p, k) * w[CONV_K - 1 - k:CONV_K - k, :]
        y_ref[...] = (cb_ref[...].astype(F32) * acc).astype(y_ref.dtype)

    col = lambda off: pl.BlockSpec((s, tc), lambda j: (0, off * nb + j))
    return _pcall(
        body, name=name, grid=(nb,), in_specs=[col(0), col(1), col(2), pl.BlockSpec((CONV_K, tc), lambda j: (0, j))],
        out_specs=pl.BlockSpec((s, tc), lambda j: (0, j)), out_shape=jax.ShapeDtypeStruct((s, d), BF16),
        compiler_params=_params("parallel"),
    )(proj, proj, proj, conv_w)


def _conv_bwd(dy, proj, conv_w, d, tc, name):
    s = proj.shape[0]
    nb = d // tc

    def body(dy_ref, cb_ref, cc_ref, cx_ref, w_ref, dcb_ref, dcc_ref, dcx_ref, dw_ref):
        cc, cx = cc_ref[...].astype(F32), cx_ref[...].astype(F32)
        p = cc * cx
        w = w_ref[...]
        dyv = dy_ref[...].astype(F32)
        shifted = [_shift_down(p, CONV_K - 1 - k) for k in range(CONV_K)]
        conv = shifted[0] * w[0:1, :]
        for k in range(1, CONV_K):
            conv = conv + shifted[k] * w[k:k + 1, :]
        dcb_ref[...] = (dyv * conv).astype(dcb_ref.dtype)
        ds = dyv * cb_ref[...].astype(F32)
        dp = ds * w[CONV_K - 1:CONV_K, :]
        for k in range(1, CONV_K):
            dp = dp + _shift_up(ds, k) * w[CONV_K - 1 - k:CONV_K - k, :]
        dcc_ref[...] = (dp * cx).astype(dcc_ref.dtype)
        dcx_ref[...] = (dp * cc).astype(dcx_ref.dtype)
        for k in range(CONV_K):
            dw_ref[k:k + 1, :] = jnp.sum(ds * shifted[k], axis=0, keepdims=True)

    col = lambda off: pl.BlockSpec((s, tc), lambda j: (0, off * nb + j))
    blk = pl.BlockSpec((s, tc), lambda j: (0, j))
    wblk = pl.BlockSpec((CONV_K, tc), lambda j: (0, j))
    act = jax.ShapeDtypeStruct((s, d), BF16)
    return _pcall(
        body, name=name, grid=(nb,), in_specs=[blk, col(0), col(1), col(2), wblk],
        out_specs=[blk, blk, blk, wblk], out_shape=[act, act, act, jax.ShapeDtypeStruct((CONV_K, d), F32)],
        compiler_params=_params("parallel"),
    )(dy, proj, proj, proj, conv_w)


def _sb_tile(q, kj, scale, carry, tri, mask):
    z = _dot(q, kj, 1, 1) * scale
    lsz = jnp.minimum(z, 0.0) - jnp.log(1.0 + jnp.exp(-jnp.abs(z)))
    l1m = lsz - z
    if mask is not None:
        l1m = jnp.where(mask, l1m, 0.0)
    l1b = l1m.astype(BF16)
    a = jnp.exp(lsz + (carry + _dot(l1b, tri, 1, 0)))
    if mask is not None:
        a = jnp.where(mask, a, 0.0)
    return lsz, l1b, a.astype(BF16)


def _sb_masks(tq, tk):
    row = lax.broadcasted_iota(jnp.int32, (tq, tk), 0)
    col = lax.broadcasted_iota(jnp.int32, (tq, tk), 1)
    masks = [col + dj * tk < row for dj in range(tq // tk)]
    r2 = lax.broadcasted_iota(jnp.int32, (tk, tk), 0)
    c2 = lax.broadcasted_iota(jnp.int32, (tk, tk), 1)
    return masks, (r2 > c2).astype(BF16), (r2 < c2).astype(BF16)


def _sb_fwd(proj, heads, col0, tq, tk, name):
    s = proj.shape[0]
    dh = SB_HEAD_DIM
    nq, nd = s // tq, tq // tk
    scale = dh ** -0.5

    def body(q_ref, k_ref, v_ref, o_ref):
        i = pl.program_id(1)
        q = q_ref[...]
        masks, tri_right, _ = _sb_masks(tq, tk)

        def tile(j, carry, acc, mask):
            start = pl.multiple_of(j * tk, tk)
            kj = k_ref[pl.ds(start, tk), :]
            vj = v_ref[pl.ds(start, tk), :]
            _, l1b, ab = _sb_tile(q, kj, scale, carry, tri_right, mask)
            return carry + jnp.sum(l1b.astype(F32), axis=1, keepdims=True), acc + _dot(ab, vj, 1, 0)

        state = (jnp.zeros((tq, 1), F32), jnp.zeros((tq, dh), F32))
        for dj in reversed(range(nd)):
            state = tile(i * nd + dj, *state, masks[dj])
        state = lax.fori_loop(0, i * nd, lambda t, st: tile(i * nd - 1 - t, st[0], st[1], None), state)
        o_ref[...] = state[1]

    qspec = pl.BlockSpec((tq, dh), lambda h, i: (i, col0[0] + h))
    kspec = pl.BlockSpec((s, dh), lambda h, i: (0, col0[1] + h))
    vspec = pl.BlockSpec((s, dh), lambda h, i: (0, col0[2] + h))
    return _pcall(
        body, name=name, grid=(heads, nq), in_specs=[qspec, kspec, vspec],
        out_specs=pl.BlockSpec((tq, dh), lambda h, i: (i, h)), out_shape=jax.ShapeDtypeStruct((s, heads * dh), F32),
        compiler_params=_params("parallel", "parallel"),
    )(proj, proj, proj)


def _sb_bwd(proj, o, do, heads, col0, tq, tk, name):
    s = proj.shape[0]
    dh = SB_HEAD_DIM
    nq, nd = s // tq, tq // tk
    scale = dh ** -0.5

    def body(q_ref, k_ref, v_ref, o_ref, do_ref, dq_ref, dk_ref, dv_ref, dk_acc, dv_acc):
        i = pl.program_id(1)

        @pl.when(i == 0)
        def _():
            dk_acc[...] = jnp.zeros_like(dk_acc)
            dv_acc[...] = jnp.zeros_like(dv_acc)

        q = q_ref[...]
        dob = do_ref[...].astype(BF16)
        delta = jnp.sum(dob.astype(F32) * o_ref[...], axis=1, keepdims=True)
        masks, tri_right, tri_left = _sb_masks(tq, tk)

        def tile(j, carry_l, carry_g, dq, mask):
            start = pl.multiple_of(j * tk, tk)
            kj = k_ref[pl.ds(start, tk), :]
            vj = v_ref[pl.ds(start, tk), :]
            lsz, l1b, ab = _sb_tile(q, kj, scale, carry_l, tri_right, mask)
            g = _dot(dob, vj, 1, 1) * ab.astype(F32)
            carry_g = carry_g + jnp.sum(g, axis=1, keepdims=True)
            left = (delta - carry_g) + _dot(g.astype(BF16), tri_left, 1, 0)
            dz = g - jnp.exp(lsz) * (g + left)
            if mask is not None:
                dz = jnp.where(mask, dz, 0.0)
            dzb = dz.astype(BF16)
            dk_acc[pl.ds(start, tk), :] += _dot(dzb, q, 0, 0)
            dv_acc[pl.ds(start, tk), :] += _dot(ab, dob, 0, 0)
            return carry_l + jnp.sum(l1b.astype(F32), axis=1, keepdims=True), carry_g, dq + _dot(dzb, kj, 1, 0)

        zero = jnp.zeros((tq, 1), F32)
        state = (zero, zero, jnp.zeros((tq, dh), F32))
        for dj in reversed(range(nd)):
            state = tile(i * nd + dj, *state, masks[dj])
        state = lax.fori_loop(0, i * nd, lambda t, st: tile(i * nd - 1 - t, st[0], st[1], st[2], None), state)
        dq_ref[...] = (state[2] * scale).astype(dq_ref.dtype)

        @pl.when(i == nq - 1)
        def _():
            dk_ref[...] = (dk_acc[...] * scale).astype(dk_ref.dtype)
            dv_ref[...] = dv_acc[...].astype(dv_ref.dtype)

    qspec = pl.BlockSpec((tq, dh), lambda h, i: (i, col0[0] + h))
    kspec = pl.BlockSpec((s, dh), lambda h, i: (0, col0[1] + h))
    vspec = pl.BlockSpec((s, dh), lambda h, i: (0, col0[2] + h))
    blk = pl.BlockSpec((tq, dh), lambda h, i: (i, h))
    full = pl.BlockSpec((s, dh), lambda h, i: (0, h))
    act = jax.ShapeDtypeStruct((s, heads * dh), BF16)
    return _pcall(
        body, name=name, grid=(heads, nq), in_specs=[qspec, kspec, vspec, blk, blk],
        out_specs=[blk, full, full], out_shape=[act, act, act],
        scratch_shapes=[pltpu.VMEM((s, dh), F32), pltpu.VMEM((s, dh), F32)],
        compiler_params=_params("parallel", "arbitrary"),
    )(proj, proj, proj, o, do)


def _xattn_probs(q, k, scale):
    sc = _dot(q, k, 1, 1) * scale
    e = jnp.exp(sc - jnp.max(sc, axis=1, keepdims=True))
    return e / jnp.sum(e, axis=1, keepdims=True)


def _xattn_fwd(qc, kv, tq, name):
    s, d = qc.shape
    m = kv.shape[0]
    dh = d // X_HEADS
    scale = dh ** -0.5

    def body(q_ref, k_ref, v_ref, o_ref):
        p = _xattn_probs(q_ref[...], k_ref[...], scale)
        o_ref[...] = _dot(p.astype(BF16), v_ref[...], 1, 0).astype(o_ref.dtype)

    blk = pl.BlockSpec((tq, dh), lambda h, i: (i, h))
    return _pcall(
        body, name=name, grid=(X_HEADS, s // tq),
        in_specs=[blk, pl.BlockSpec((m, dh), lambda h, i: (0, h)), pl.BlockSpec((m, dh), lambda h, i: (0, X_HEADS + h))],
        out_specs=blk, out_shape=jax.ShapeDtypeStruct((s, d), BF16), compiler_params=_params("parallel", "parallel"),
    )(qc, kv, kv)


def _xattn_bwd(qc, kv, do, tq, name):
    s, d = qc.shape
    m = kv.shape[0]
    dh = d // X_HEADS
    scale = dh ** -0.5
    nq = s // tq

    def body(q_ref, k_ref, v_ref, do_ref, dq_ref, dk_ref, dv_ref, dk_acc, dv_acc):
        i = pl.program_id(1)
        q, k, v = q_ref[...], k_ref[...], v_ref[...]
        dob = do_ref[...].astype(BF16)
        p = _xattn_probs(q, k, scale)
        pb = p.astype(BF16)
        dp = _dot(dob, v, 1, 1)
        ds = pb.astype(F32) * (dp - jnp.sum(dp * pb.astype(F32), axis=1, keepdims=True))
        dsb = (ds * scale).astype(BF16)
        dq_ref[...] = _dot(dsb, k, 1, 0).astype(dq_ref.dtype)
        dk_part = _dot(dsb, q, 0, 0)
        dv_part = _dot(pb, dob, 0, 0)

        @pl.when(i == 0)
        def _():
            dk_acc[...] = dk_part
            dv_acc[...] = dv_part

        @pl.when(i > 0)
        def _():
            dk_acc[...] += dk_part
            dv_acc[...] += dv_part

        @pl.when(i == nq - 1)
        def _():
            dk_ref[...] = dk_acc[...].astype(dk_ref.dtype)
            dv_ref[...] = dv_acc[...].astype(dv_ref.dtype)

    blk = pl.BlockSpec((tq, dh), lambda h, i: (i, h))
    kblk = pl.BlockSpec((m, dh), lambda h, i: (0, h))
    return _pcall(
        body, name=name, grid=(X_HEADS, nq),
        in_specs=[blk, kblk, pl.BlockSpec((m, dh), lambda h, i: (0, X_HEADS + h)), blk],
        out_specs=[blk, kblk, kblk],
        out_shape=[jax.ShapeDtypeStruct((s, d), BF16), jax.ShapeDtypeStruct((m, d), BF16), jax.ShapeDtypeStruct((m, d), BF16)],
        scratch_shapes=[pltpu.VMEM((m, dh), F32), pltpu.VMEM((m, dh), F32)],
        compiler_params=_params("parallel", "arbitrary"),
    )(qc, kv, kv, do)


def _local_step(x, mem, tgt, w, fetch=None, prefetch=None, emit=None, tick=None, after=None):
    fetch = fetch or (lambda name, after: {})
    prefetch = prefetch or (lambda name, after: None)
    emit = emit or (lambda group, g: None)
    tick = tick or (lambda group, after: None)
    w = dict(w)
    s, d = x.shape
    heads = d // SB_HEAD_DIM
    tm = _pick(s, (512, 256, 128))
    tq = _pick(s, (256, 128))
    sb_tq, sb_tk = _pick(s, (512, 256, 128)), _pick(s, (256, 128))
    tc = _pick(d, (256, 128))
    g = {}

    def wt(name, after):
        if name not in w:
            w.update(fetch(name, after))
        return w[name]

    def ffn_fwd(h, gname, wgu, wdown, tag, after=None):
        n = _rms_fwd(h, w[gname], tag + "_norm", tm, after=after)
        gu, act = _ffn_up(n, wt(wgu, n), tag + "_gu")
        prefetch(wdown, gu)
        return n, gu, act, _mm(act, wt(wdown, act), name=tag + "_down", out_dtype=F32, res=h, alpha=0.5)

    def ffn_bwd(dh, dhb, h, saved, gname, wgu, wdown, tag, copy_scale=None):
        n, gu, act = saved
        g[wdown] = _mm(act, dhb, ta=True, name=tag + "_dwdown")
        dgu = _ffn_dgu(dhb, w[wdown], gu, tag + "_dgu")
        g[wgu] = _mm(n, dgu, ta=True, b_halves=True, name=tag + "_dwgu")
        dn = _mm(dgu, w[wgu], tb=True, a_halves=True, name=tag + "_dn", out_dtype=F32, after=emit(tag, g))
        *dh_in, g[gname] = _resid_rms_bwd(dh, dn, h, w[gname], tag + "_dnorm", tm, after=tick(tag, dn), copy_scale=copy_scale)
        return dh_in

    n1, gu1, act1, h1 = ffn_fwd(x, "g_ffn1", "w_ffn1_gu", "w_ffn1_down", "ffn1", after)
    prefetch("w_in", h1)
    u = _rms_fwd(h1, w["g_mix"], "mix_norm", tm)
    proj = _mm(u, wt("w_in", u), name="mix_in")
    prefetch("w_conv_out", proj)
    nd = d // SB_HEAD_DIM
    y_conv = _conv_fwd(proj, w["conv_w"], d, tc, "conv_fwd")
    sb_cols = (3 * nd, 4 * nd, 5 * nd)
    y_sb = _sb_fwd(proj, heads, sb_cols, sb_tq, sb_tk, "sb_fwd")
    prefetch("w_cq", y_sb)
    a_conv = _mm(y_conv, wt("w_conv_out", y_conv), name="conv_out")
    a_sb = _mm(y_sb, wt("w_attn_out", y_sb), name="attn_out")
    b_conv, b_sb = w["b_gate"][:, :d], w["b_gate"][:, d:]

    def merge(ac, asb, gcp, gsp, bc, bs):
        gc = _sigmoid(gcp.astype(F32) + bc)
        gs = _sigmoid(gsp.astype(F32) + bs)
        return gc * ac.astype(F32) + gs * asb.astype(F32)

    merged = _rowcall(merge, [_whole(a_conv), _whole(a_sb), (proj, 6, d), (proj, 7, d)], [b_conv, b_sb], [(d, BF16)],
                      tm=tm, name="merge")[0]
    prefetch("w_ffn2_gu", merged)
    h2 = _mm(merged, wt("w_o", merged), name="mix_out", out_dtype=F32, res=h1)
    hn = _rms_fwd(h2, w["g_cross"], "cross_norm", tm)
    mn = _rms_fwd(mem, w["g_mem"], "mem_norm", _pick(mem.shape[0], (256, 128)))
    qc = _mm(hn, wt("w_cq", hn), name="cross_q")
    kv = _mm(mn, wt("w_ckv", mn), name="cross_kv")
    oc = _xattn_fwd(qc, kv, tq, "xattn_fwd")
    h3 = _mm(oc, wt("w_co", oc), name="cross_out", out_dtype=F32, res=h2)
    n2, gu2, act2, h4 = ffn_fwd(h3, "g_ffn2", "w_ffn2_gu", "w_ffn2_down", "ffn2")

    def head(hb, tb, gb):
        xh, r = _xhat(hb)
        err = xh * gb - tb
        dy = err * (1.0 / d)
        dxh = dy * gb
        dx = r * (dxh - xh * jnp.mean(dxh * xh, axis=-1, keepdims=True))
        row_loss = 0.5 * jnp.mean(err * err, axis=-1, keepdims=True)
        return dx, 0.5 * dx, dy * xh, jnp.broadcast_to(row_loss, (row_loss.shape[0], LANES))

    dh4, dh4b, g["g_final"], loss_lanes = _rowcall(head, [_whole(h4), _whole(tgt)], [w["g_final"]], [(d, F32), (d, BF16)],
                                                   [d, LANES], tm=tm, name="loss_head")

    dh3, dh3b = ffn_bwd(dh4, dh4b, h3, (n2, gu2, act2), "g_ffn2", "w_ffn2_gu", "w_ffn2_down", "ffn2", copy_scale=1.0)
    g["w_co"] = _mm(oc, dh3b, ta=True, name="cross_dwco")
    doc = _mm(dh3b, w["w_co"], tb=True, name="cross_doc")
    dqc, dk, dv = _xattn_bwd(qc, kv, doc, tq, "xattn_bwd")
    dkv = jnp.concatenate([dk, dv], axis=1)
    g["w_cq"] = _mm(hn, dqc, ta=True, name="cross_dwcq")
    g["w_ckv"] = _mm(mn, dkv, ta=True, name="cross_dwckv")
    dhn = _mm(dqc, w["w_cq"], tb=True, name="cross_dhn", out_dtype=F32, after=emit("cross", g))
    dmn = _mm(dkv, w["w_ckv"], tb=True, name="cross_dmn", out_dtype=F32)
    g["g_mem"] = _rowcall(lambda dy, xb: dy * _xhat(xb)[0], [_whole(dmn), _whole(mem)], [], [], [d],
                          tm=_pick(mem.shape[0], (256, 128)), name="mem_dnorm")[0]
    dh2, dh2b, g["g_cross"] = _resid_rms_bwd(dh3, dhn, h2, w["g_cross"], "cross_dnorm", tm, after=tick("cross", dhn), copy_scale=1.0)

    g["w_o"] = _mm(merged, dh2b, ta=True, name="mix_dwo")
    dmerged = _mm(dh2b, w["w_o"], tb=True, name="mix_dmerged")

    def merge_bwd(dm, ac, asb, gcp, gsp, bc, bs):
        dm, ac, asb = dm.astype(F32), ac.astype(F32), asb.astype(F32)
        gc = _sigmoid(gcp.astype(F32) + bc)
        gs = _sigmoid(gsp.astype(F32) + bs)
        dgc = dm * ac * gc * (1.0 - gc)
        dgs = dm * asb * gs * (1.0 - gs)
        return dm * gc, dm * gs, dgc, dgs, dgc, dgs

    da_conv, da_sb, dgc, dgs, db_conv, db_sb = _rowcall(
        merge_bwd, [_whole(dmerged), _whole(a_conv), _whole(a_sb), (proj, 6, d), (proj, 7, d)], [b_conv, b_sb],
        [(d, BF16)] * 4, [d, d], tm=tm, name="merge_bwd")
    g["b_gate"] = jnp.concatenate([db_conv, db_sb], axis=1)
    g["w_conv_out"] = _mm(y_conv, da_conv, ta=True, name="conv_dwout")
    g["w_attn_out"] = _mm(y_sb, da_sb, ta=True, name="attn_dwout")
    dy_conv = _mm(da_conv, w["w_conv_out"], tb=True, name="conv_dy")
    dy_sb = _mm(da_sb, w["w_attn_out"], tb=True, name="attn_dy")
    dcb, dcc, dcx, g["conv_w"] = _conv_bwd(dy_conv, proj, w["conv_w"], d, tc, "conv_bwd")
    dq, dk_sb, dv_sb = _sb_bwd(proj, y_sb, dy_sb, heads, sb_cols, sb_tq, sb_tk, "sb_bwd")
    dproj = jnp.concatenate([dcb, dcc, dcx, dq, dk_sb, dv_sb, dgc, dgs], axis=1)
    g["w_in"] = _mm(u, dproj, ta=True, name="mix_dwin")
    du = _mm(dproj, w["w_in"], tb=True, name="mix_du", out_dtype=F32, after=emit("mix", g))
    dh1, dh1b, g["g_mix"] = _resid_rms_bwd(dh2, du, h1, w["g_mix"], "mix_dnorm", tm, after=tick("mix", du), copy_scale=0.5)
    dx, = ffn_bwd(dh1, dh1b, x, (n1, gu1, act1), "g_ffn1", "w_ffn1_gu", "w_ffn1_down", "ffn1")
    return loss_lanes, dx, g


MATS = (("w_ffn1_gu", "col"), ("w_ffn1_down", "row"), ("w_in", "col"), ("w_conv_out", "row"), ("w_attn_out", "row"),
        ("w_o", "row"), ("w_cq", "row"), ("w_ckv", "col"), ("w_co", "row"), ("w_ffn2_gu", "col"), ("w_ffn2_down", "row"))
VECS = ("g_ffn1", "g_mix", "g_cross", "g_mem", "g_ffn2", "g_final")
WEIGHTS = ("g_ffn1", "w_ffn1_gu", "w_ffn1_down", "g_mix", "w_in", "b_gate", "conv_w", "w_conv_out", "w_attn_out", "w_o",
           "g_cross", "g_mem", "w_cq", "w_ckv", "w_co", "g_ffn2", "w_ffn2_gu", "w_ffn2_down", "g_final")
CONV_ROWS = 8


def _full_shape(kind, r, c):
    return (r, N_CHIPS * c) if kind == "col" else (N_CHIPS * r, c)


def _piece(ref, kind, r, c, chip, half):
    hr = r // 2
    if kind == "col":
        return ref.at[pl.ds(pl.multiple_of(half * hr, 16), hr), pl.ds(pl.multiple_of(chip * c, LANES), c)]
    return ref.at[pl.ds(pl.multiple_of(chip * r + half * hr, 16), hr), :]


def _shard_of(ref, kind, r, c, chip):
    if kind == "col":
        return ref.at[:, pl.ds(pl.multiple_of(chip * c, LANES), c)]
    return ref.at[pl.ds(pl.multiple_of(chip * r, 16), r), :]


def _place():
    x, y, c = lax.axis_index("x"), lax.axis_index("y"), lax.axis_index("c")
    others = [(1 - x, y), (x, 1 - y), (1 - x, 1 - y)]
    return x, y, c, 2 * x + y, others


def _remote(src, dst, send_sem, recv_sem, to):
    return pltpu.make_async_remote_copy(src_ref=src, dst_ref=dst, send_sem=send_sem, recv_sem=recv_sem,
                                        device_id=to, device_id_type=MESH)


def _gather_conv(conv_shard):
    cc = conv_shard.shape[1]

    def body(conv_ref, conv_full, cs, cr, cl):
        x, y, c, me, others = _place()

        def cols(chip):
            return conv_full.at[:, pl.ds(pl.multiple_of(chip * cc, LANES), cc)]

        def conv(k, chip_from, to):
            return _remote(conv_ref, cols(chip_from), cs.at[k], cr.at[k], to)

        mine = pltpu.make_async_copy(conv_ref, cols(me), cl.at[0])
        mine.start()
        for k, (ox, oy) in enumerate(others):
            conv(k, me, (ox, oy, c)).start()
        for k, (ox, oy) in enumerate(others):
            conv(k, 2 * ox + oy, (x, y, c)).wait_recv()
            conv(k, me, (ox, oy, c)).wait_send()
        mine.wait()

    dma = pltpu.SemaphoreType.DMA
    return _pcall(
        body, name="gather_conv", in_specs=[ANY], out_specs=ANY,
        out_shape=jax.ShapeDtypeStruct((CONV_ROWS, N_CHIPS * cc), F32), scratch_shapes=[dma((3,)), dma((3,)), dma((1,))],
    )(conv_shard)


HBM = pl.BlockSpec(memory_space=pltpu.HBM)
SEM = pl.BlockSpec(memory_space=pltpu.SEMAPHORE)
EFFECT = pltpu.SideEffectType.DATAFLOW_SIDE_EFFECTING
TOKEN = (8, LANES)


def _split_start(name, plan, n_copies, srcs, lands, after=None):
    ns, nl = len(srcs), len(lands)
    n_in = ns + nl + (after is not None)

    def body(*refs):
        outs = refs[n_in:]
        sends, _ = plan(refs[:ns], refs[ns:ns + nl], outs[0], outs[1])
        for cp in sends:
            cp.start()
        outs[-1][...] = jnp.zeros(TOKEN, F32)

    held = [pltpu.HBM(a.shape, a.dtype) for a in (*srcs, *lands)]
    dma = pltpu.SemaphoreType.DMA((n_copies,))
    ins = [pltpu.with_memory_space_constraint(a, pltpu.HBM) for a in (*srcs, *lands)]
    outs = _pcall(
        body, name=name, in_specs=[HBM] * (ns + nl) + ([] if after is None else [ANY]),
        out_specs=(SEM, SEM, *[HBM] * (ns + nl), pl.BlockSpec(memory_space=pltpu.VMEM)),
        out_shape=(dma, dma, *held, jax.ShapeDtypeStruct(TOKEN, F32)),
        input_output_aliases={i: 2 + i for i in range(ns + nl)},
        compiler_params=pltpu.CompilerParams(has_side_effects=EFFECT),
    )(*ins, *([] if after is None else [after]))
    return outs[0], outs[1], list(outs[2:2 + ns]), list(outs[2 + ns:2 + ns + nl]), outs[-1]


def _split_wait(name, plan, send_sems, recv_sems, srcs, lands, after):
    ns, nl = len(srcs), len(lands)

    def body(*refs):
        sends, recvs = plan(refs[:ns], refs[ns:ns + nl], refs[ns + nl], refs[ns + nl + 1])
        for cp in sends:
            cp.wait_send()
        for cp in recvs:
            cp.wait_recv()

    outs = _pcall(
        body, name=name, in_specs=[HBM] * (ns + nl) + [SEM, SEM, ANY], out_specs=[HBM] * (ns + nl),
        out_shape=[pltpu.HBM(a.shape, a.dtype) for a in (*srcs, *lands)],
        input_output_aliases={i: i for i in range(ns + nl)},
        compiler_params=pltpu.CompilerParams(has_side_effects=EFFECT),
    )(*srcs, *lands, send_sems, recv_sems, after)
    return list(outs[:ns]), list(outs[ns:])


def _gather_plan(dims):
    def plan(shard_refs, full_refs, ss, rs):
        x, y, c, me, others = _place()
        sends, recvs = [], []
        for wi, (kind, r, cw) in enumerate(dims):
            half = shard_refs[wi].at[pl.ds(pl.multiple_of(c * (r // 2), 16), r // 2), :]
            for k, (ox, oy) in enumerate(others):
                sem = 4 * wi + k
                sends.append(_remote(half, _piece(full_refs[wi], kind, r, cw, me, c), ss.at[sem], rs.at[sem], (ox, oy, c)))
                recvs.append(_remote(half, _piece(full_refs[wi], kind, r, cw, 2 * ox + oy, c), ss.at[sem], rs.at[sem], (x, y, c)))
            sem = 4 * wi + 3
            own = _remote(shard_refs[wi], _shard_of(full_refs[wi], kind, r, cw, me), ss.at[sem], rs.at[sem], (x, y, 1 - c))
            sends.append(own)
            recvs.append(own)
        return sends, recvs

    return plan


def _forward_plan(dims):
    def plan(_, full_refs, ss, rs):
        x, y, c, _, others = _place()
        sends, recvs = [], []
        for wi, (kind, r, cw) in enumerate(dims):
            for k, (ox, oy) in enumerate(others):
                sem = 3 * wi + k
                mine = _piece(full_refs[wi], kind, r, cw, 2 * ox + oy, c)
                theirs = _piece(full_refs[wi], kind, r, cw, 2 * ox + oy, 1 - c)
                sends.append(_remote(mine, mine, ss.at[sem], rs.at[sem], (x, y, 1 - c)))
                recvs.append(_remote(theirs, theirs, ss.at[sem], rs.at[sem], (x, y, 1 - c)))
        return sends, recvs

    return plan


def _rs_cores_plan(dims):
    def plan(g_refs, land_refs, ss, rs):
        x, y, c, _, _ = _place()
        sends, recvs = [], []
        for wi, dm in enumerate(dims):
            for chip in range(N_CHIPS):
                sem = N_CHIPS * wi + chip
                sends.append(_remote(_piece(g_refs[wi], *dm, chip, 1 - c), land_refs[wi].at[chip], ss.at[sem], rs.at[sem], (x, y, 1 - c)))
                recvs.append(_remote(_piece(g_refs[wi], *dm, chip, c), land_refs[wi].at[chip], ss.at[sem], rs.at[sem], (x, y, 1 - c)))
        return sends, recvs

    return plan


def _share_plan(nw):
    def plan(_, buf_refs, ss, rs):
        x, y, c, _, _ = _place()
        sends = [_remote(buf_refs[wi].at[c], buf_refs[wi].at[c], ss.at[wi], rs.at[wi], (x, y, 1 - c)) for wi in range(nw)]
        recvs = [_remote(buf_refs[wi].at[1 - c], buf_refs[wi].at[1 - c], ss.at[wi], rs.at[wi], (x, y, 1 - c)) for wi in range(nw)]
        return sends, recvs

    return plan


def _small_plan():
    def plan(_, buf_refs, ss, rs):
        x, y, c = lax.axis_index("x"), lax.axis_index("y"), lax.axis_index("c")
        buf = buf_refs[0]
        sends, recvs = [], []
        for rel in range(1, N_DEV):
            peer = (x ^ (rel >> 2 & 1), y ^ (rel >> 1 & 1), c ^ (rel & 1))
            sends.append(_remote(buf.at[0], buf.at[rel], ss.at[rel - 1], rs.at[rel - 1], peer))
            recvs.append(_remote(buf.at[0], buf.at[rel], ss.at[rel - 1], rs.at[rel - 1], peer))
        return sends, recvs

    return plan


def _sum_small(buf, me, name):
    _, rows, n = buf.shape

    def body(me_ref, b_ref, o_ref):
        tot = b_ref[me_ref[0]]
        for dev in range(1, N_DEV):
            tot = tot + b_ref[dev ^ me_ref[0]]
        o_ref[...] = tot

    return _pcall(
        body, name=name, out_shape=jax.ShapeDtypeStruct((rows, n), F32),
        grid_spec=pltpu.PrefetchScalarGridSpec(
            num_scalar_prefetch=1, grid=(1,), in_specs=[pl.BlockSpec((N_DEV, rows, n), lambda i, m: (0, 0, 0))],
            out_specs=pl.BlockSpec((rows, n), lambda i, m: (0, 0))),
    )(me, buf)


def _rs_chips_plan(nw):
    def plan(p_refs, land_refs, ss, rs):
        x, y, c, me, others = _place()
        sends, recvs = [], []
        for wi in range(nw):
            for k, (ox, oy) in enumerate(others):
                sem = 3 * wi + k
                sends.append(_remote(p_refs[wi].at[2 * ox + oy], land_refs[wi].at[k], ss.at[sem], rs.at[sem], (ox, oy, c)))
                recvs.append(_remote(p_refs[wi].at[me], land_refs[wi].at[k], ss.at[sem], rs.at[sem], (x, y, c)))
        return sends, recvs

    return plan


def _rows_per_block(n, c, limit_bytes=2 << 20):
    best = None
    for tm in range(16, n + 1, 16):
        if n % tm == 0 and tm * c * 4 <= limit_bytes:
            best = tm
    return best or n


def _sum_cores(grad, got, kind, place, name):
    _, hr, cw = got.shape
    tm = _rows_per_block(hr, cw)
    nb = hr // tm

    def body(place_ref, g_ref, t_ref, o_ref):
        o_ref[...] = (g_ref[...].astype(F32) + t_ref[...].astype(F32)).astype(o_ref.dtype)

    if kind == "col":
        g_spec = pl.BlockSpec((tm, cw), lambda j, i, pr: (pr[0] * nb + i, j))
    else:
        g_spec = pl.BlockSpec((tm, cw), lambda j, i, pr: ((2 * j + pr[0]) * nb + i, 0))
    blk = pl.BlockSpec((None, tm, cw), lambda j, i, pr: (j, i, 0))
    return _pcall(
        body, name=name, out_shape=jax.ShapeDtypeStruct(got.shape, BF16),
        grid_spec=pltpu.PrefetchScalarGridSpec(num_scalar_prefetch=1, grid=(N_CHIPS, nb), in_specs=[g_spec, blk], out_specs=blk),
        compiler_params=_params("parallel", "parallel"),
    )(place, grad, got)


def _sum_chips(parts, got, place, name):
    _, n, cw = got.shape
    tm = _rows_per_block(n, cw)

    def body(place_ref, p_ref, g_ref, o_ref):
        tot = p_ref[...].astype(F32)
        for k in range(3):
            tot = tot + g_ref[k].astype(F32)
        o_ref[...] = tot

    return _pcall(
        body, name=name, out_shape=jax.ShapeDtypeStruct((2, n, cw), F32),
        grid_spec=pltpu.PrefetchScalarGridSpec(
            num_scalar_prefetch=1, grid=(n // tm,),
            in_specs=[pl.BlockSpec((None, tm, cw), lambda i, pr: (pr[1], i, 0)), pl.BlockSpec((3, tm, cw), lambda i, pr: (0, i, 0))],
            out_specs=pl.BlockSpec((None, tm, cw), lambda i, pr: (pr[0], i, 0))),
        compiler_params=_params("parallel"),
    )(place, parts, got)


def _adamw(g, w, m, v, name):
    n, c = g.shape
    c1 = 1.0 - ADAM_B1 ** ADAM_STEP
    c2 = 1.0 - ADAM_B2 ** ADAM_STEP

    def fn(gb, wb, mb, vb):
        m_new = ADAM_B1 * mb + (1.0 - ADAM_B1) * gb
        v_new = ADAM_B2 * vb + (1.0 - ADAM_B2) * (gb * gb)
        delta = -ADAM_LR * ((m_new / c1) / (jnp.sqrt(v_new / c2) + ADAM_EPS) + ADAM_WD * wb)
        return gb, delta, m_new, v_new

    tm = _rows_per_block(n, c) if n % 16 == 0 else n
    return _rowcall(fn, [_whole(g), _whole(w), _whole(m), _whole(v)], [], [(c, F32)] * 4, tm=tm, name=name)


PACK_ROWS = 16


def _pack_rows(parts, width, name, after=None):
    assert sum(p.shape[0] for p in parts) <= PACK_ROWS

    def body(*refs):
        out_ref = refs[-1]
        out_ref[...] = jnp.zeros_like(out_ref)
        at = 0
        for r in refs[:len(parts)]:
            k, n = r.shape
            if n == width:
                out_ref[at:at + k, :] = r[...]
            else:
                out_ref[at:at + k, :] = jnp.broadcast_to(r[:, :1], (k, width))
            at += k

    vm = pl.BlockSpec(memory_space=pltpu.VMEM)
    return _pcall(body, name=name, in_specs=[vm] * len(parts) + ([] if after is None else [ANY]), out_specs=vm,
                  out_shape=jax.ShapeDtypeStruct((PACK_ROWS, width), F32))(*parts, *([] if after is None else [after]))


def _cast_shard(wm, name, after):
    n, c = wm.shape
    return _rowcall(lambda v: v, [_whole(wm)], [], [(c, BF16)], tm=_rows_per_block(n, c), name=name, after=after)[0]


GATHER_GROUPS = (
    ("w_ffn1_gu",), ("w_ffn1_down",), ("w_in",), ("w_conv_out", "w_attn_out", "w_o"), ("w_cq", "w_ckv", "w_co"),
    ("w_ffn2_gu", "w_ffn2_down"),
)
REDUCE_GROUPS = {
    "ffn2": ("w_ffn2_down", "w_ffn2_gu"),
    "cross": ("w_co", "w_cq", "w_ckv"),
    "mix": ("w_o", "w_conv_out", "w_attn_out", "w_in"),
    "ffn1": ("w_ffn1_down", "w_ffn1_gu"),
}
KIND = dict(MATS)


def _step(x, mem, tgt, wts, m_in, v_in):
    d = x.shape[-1]
    cc = wts["conv_w"].shape[1]
    place = jnp.stack([lax.axis_index("c"), 2 * lax.axis_index("x") + lax.axis_index("y")]).astype(jnp.int32)
    dims = {n: (kind, *wts[n].shape) for n, kind in MATS}

    conv_pad = jnp.pad(wts["conv_w"], ((0, CONV_ROWS - CONV_K), (0, 0)))
    conv_full = _gather_conv(conv_pad)
    w = {"conv_w": conv_full[:CONV_K]}
    for n in VECS + ("b_gate",):
        w[n] = wts[n].reshape(1, -1)
    flying, token = {}, conv_full
    for names in GATHER_GROUPS:
        gd = [dims[n] for n in names]
        shards = [_cast_shard(wts[n], "cast_" + n, token) for n in names]
        lands = [lax.empty(_full_shape(*dm), BF16) for dm in gd]
        plan = _gather_plan(gd)
        ss, rs, srcs, lands, token = _split_start("gather_start_" + names[0], plan, 4 * len(names), shards, lands, token)
        flying.update({n: (names, plan, ss, rs, srcs, lands, gd) for n in names})

    passing = {}

    def prefetch(name, after):
        if name not in passing:
            names, plan, ss, rs, srcs, lands, gd = flying[name]
            _, lands = _split_wait("gather_wait_" + names[0], plan, ss, rs, srcs, lands, after)
            plan = _forward_plan(gd)
            ss, rs, _, lands, _ = _split_start("forward_start_" + names[0], plan, 3 * len(names), [], lands)
            passing.update({n: (names, plan, ss, rs, lands) for n in names})

    def fetch(name, after):
        prefetch(name, after)
        names, plan, ss, rs, lands = passing[name]
        _, lands = _split_wait("forward_wait_" + names[0], plan, ss, rs, [], lands, after)
        return dict(zip(names, lands))

    swapping, sent = {}, {}

    def emit(tag, g):
        if tag not in REDUCE_GROUPS:
            return None
        names = REDUCE_GROUPS[tag]
        gd = [dims[n] for n in names]
        lands = [lax.empty((N_CHIPS, r // 2, cw), BF16) for (_, r, cw) in gd]
        plan = _rs_cores_plan(gd)
        ss, rs, srcs, lands, tok = _split_start("rs_cores_start_" + tag, plan, N_CHIPS * len(names), [g[n] for n in names], lands)
        swapping[tag] = (plan, ss, rs, srcs, lands)
        return tok

    def tick(tag, after):
        if tag not in REDUCE_GROUPS:
            return None
        names = REDUCE_GROUPS[tag]
        plan, ss, rs, srcs, lands = swapping[tag]
        mine, got = _split_wait("rs_cores_wait_" + tag, plan, ss, rs, srcs, lands, after)
        parts = [_sum_cores(gm, t, KIND[n], place, "sum_cores_" + n) for n, gm, t in zip(names, mine, got)]
        lands = [lax.empty((3, *p.shape[1:]), BF16) for p in parts]
        plan = _rs_chips_plan(len(names))
        ss, rs, srcs, lands, tok = _split_start("rs_chips_start_" + tag, plan, 3 * len(names), parts, lands)
        sent[tag] = (plan, ss, rs, srcs, lands)
        return tok

    loss_lanes, dx, g = _local_step(x[0], mem[0], tgt[0], w, fetch, prefetch, emit, tick, token)

    rows = [g[n] for n in VECS] + [g["b_gate"][:, :d], g["b_gate"][:, d:], g["conv_w"], loss_lanes]
    packed = _pack_rows(rows, d, "pack_small")
    small = jnp.concatenate([packed[None], jnp.zeros((N_DEV - 1, *packed.shape), F32)], axis=0)
    small_plan = _small_plan()
    small_ss, small_rs, _, small, after = _split_start("small_start", small_plan, N_DEV - 1, [], [small])

    grads, out = {}, {}

    def update(n):
        shape = wts[n].shape
        as2d = (lambda a: a.reshape(1, -1)) if len(shape) == 1 else (lambda a: a)
        return [r.reshape(shape) for r in _adamw(grads[n], as2d(wts[n]), as2d(m_in[n]), as2d(v_in[n]), "adamw_" + n)]

    def finish(sharing, after):
        tag, names, plan, ss, rs, halves = sharing
        _, both = _split_wait("share_wait_" + tag, plan, ss, rs, [], halves, after)
        for n, b in zip(names, both):
            grads[n] = b.reshape(-1, b.shape[-1])
            out[n] = update(n)
        return out[names[-1]][1]

    sharing = None
    for tag, names in REDUCE_GROUPS.items():
        plan, ss, rs, srcs, lands = sent[tag]
        parts, landed = _split_wait("rs_chips_wait_" + tag, plan, ss, rs, srcs, lands, after)
        halves = [_sum_chips(p, t, place, "sum_chips_" + n) for n, p, t in zip(names, parts, landed)]
        plan = _share_plan(len(names))
        ss, rs, _, halves, after = _split_start("share_start_" + tag, plan, len(names), [], halves)
        if sharing is not None:
            after = finish(sharing, after)
        sharing = (tag, names, plan, ss, rs, halves)
    after = finish(sharing, after)

    _, small = _split_wait("small_wait", small_plan, small_ss, small_rs, [], small, after)
    me = (4 * lax.axis_index("x") + 2 * lax.axis_index("y") + lax.axis_index("c")).astype(jnp.int32).reshape(1)
    red = _sum_small(small[0], me, "sum_small")
    grads.update({n: red[i:i + 1] for i, n in enumerate(VECS)})
    nv = len(VECS)
    grads["b_gate"] = jnp.concatenate([red[nv:nv + 1], red[nv + 1:nv + 2]], axis=1)
    chip = 2 * lax.axis_index("x") + lax.axis_index("y")
    grads["conv_w"] = lax.dynamic_slice_in_dim(red[nv + 2:nv + 2 + CONV_K], chip * cc, cc, axis=1)
    loss = red[nv + 2 + CONV_K, 0]
    out.update({n: update(n) for n in WEIGHTS if n not in KIND})
    return (loss, dx[None], *[out[n][0] for n in WEIGHTS], *[out[n][1] for n in WEIGHTS],
            *[out[n][2] for n in WEIGHTS], *[out[n][3] for n in WEIGHTS])


def kernel(x, mem, g_ffn1, w_ffn1_gu, w_ffn1_down, g_mix, w_in, b_gate, conv_w, w_conv_out, w_attn_out, w_o, g_cross, g_mem, w_cq, w_ckv, w_co, g_ffn2, w_ffn2_gu, w_ffn2_down, g_final, loss_target, m_g_ffn1, m_w_ffn1_gu, m_w_ffn1_down, m_g_mix, m_w_in, m_b_gate, m_conv_w, m_w_conv_out, m_w_attn_out, m_w_o, m_g_cross, m_g_mem, m_w_cq, m_w_ckv, m_w_co, m_g_ffn2, m_w_ffn2_gu, m_w_ffn2_down, m_g_final, v_g_ffn1, v_w_ffn1_gu, v_w_ffn1_down, v_g_mix, v_w_in, v_b_gate, v_conv_w, v_w_conv_out, v_w_attn_out, v_w_o, v_g_cross, v_g_mem, v_w_cq, v_w_ckv, v_w_co, v_g_ffn2, v_w_ffn2_gu, v_w_ffn2_down, v_g_final):
    given = dict(locals())
    wts = {n: given[n] for n in WEIGHTS}
    m_in = {n: given["m_" + n] for n in WEIGHTS}
    v_in = {n: given["v_" + n] for n in WEIGHTS}
    return _step(x, mem, loss_target, wts, m_in, v_in)
```

```python
import functools

import jax
import jax.numpy as jnp
from jax import lax
from jax.experimental import pallas as pl
from jax.experimental.pallas import tpu as pltpu

F32 = jnp.float32
BF16 = jnp.bfloat16
MESH = pl.DeviceIdType.MESH

V7X_VMEM_LIMIT_BYTES = 48 * 1024 * 1024
MM_VMEM_BUDGET_BYTES = 36 * 1024 * 1024
MM_WHOLE_K = 2816
LANES = 128
SB_HEAD_DIM = 128
X_HEADS = 4
CONV_K = 3
RMS_EPS = 1e-6
N_CHIPS = 4
N_DEV = 8
ADAM_LR, ADAM_B1, ADAM_B2, ADAM_EPS, ADAM_WD, ADAM_STEP = 0.001, 0.9, 0.999, 1e-08, 0.01, 10


ANY = pl.BlockSpec(memory_space=pl.ANY)


def _pcall(body, **kw):
    return pl.pallas_call(body, **kw)


def _params(*sem):
    return pltpu.CompilerParams(dimension_semantics=sem, vmem_limit_bytes=V7X_VMEM_LIMIT_BYTES)


def _pick(dim, cands):
    for c in cands:
        if dim % c == 0:
            return c
    return dim


def _dot(a, b, ca, cb):
    return lax.dot_general(a, b, (((ca,), (cb,)), ((), ())), preferred_element_type=F32)


def _mm(a, b, *, name, ta=False, tb=False, out_dtype=BF16, res=None, alpha=1.0, tm=None, tn=None, tk=None, after=None,
        a_halves=False, b_halves=False):
    assert not (a_halves and ta) and not (b_halves and tb)
    if a_halves:
        m, k = a.shape[1], 2 * a.shape[2]
    else:
        m, k = (a.shape[1], a.shape[0]) if ta else a.shape
    if b_halves:
        n = 2 * b.shape[2]
        assert k == b.shape[1]
    else:
        n = b.shape[0] if tb else b.shape[1]
        assert k == (b.shape[1] if tb else b.shape[0]), (a.shape, b.shape, ta, tb)
    if ta:
        tm = tm or _pick(m, (512, 256, 128))
        tn = tn or _pick(n, (1024, 512, 256, 128))
        tk = tk or (k if k <= MM_WHOLE_K else _pick(k, (1024, 512, 256, 128)))
    else:
        tk = tk or (k if k <= MM_WHOLE_K else _pick(k, (MM_WHOLE_K, 2048, 1024, 512, 256, 128)))
        tn = tn or _pick(n, (512, 1408, 256, 128) if tk == k else (1024, 512, 256, 128))
        per_row = 2 * (tk * a.dtype.itemsize + tn * (jnp.dtype(out_dtype).itemsize + (0 if res is None else res.dtype.itemsize)))
        per_row += 4 * tn if tk < k else 0
        rows = (MM_VMEM_BUDGET_BYTES - 2 * tk * tn * b.dtype.itemsize) // per_row
        tm = tm or next((c for c in (2048, 1024, 512, 256, 128) if m % c == 0 and c <= rows), m)
    if a_halves:
        tk = min(tk, k // 2) if (k // 2) % min(tk, k // 2) == 0 else _pick(k // 2, (1408, 1024, 512, 256, 128))
    if b_halves:
        tn = tn if (n // 2) % tn == 0 else _pick(n // 2, (1408, 1024, 512, 256, 128))
    nk = k // tk
    assert m % tm == 0 and n % tn == 0 and k % tk == 0
    a_spec = pl.BlockSpec((tk, tm), lambda i, j, kk: (kk, i)) if ta else pl.BlockSpec((tm, tk), lambda i, j, kk: (i, kk))
    b_spec = pl.BlockSpec((tn, tk), lambda i, j, kk: (j, kk)) if tb else pl.BlockSpec((tk, tn), lambda i, j, kk: (kk, j))
    if a_halves:
        per = (k // 2) // tk
        a_spec = pl.BlockSpec((None, tm, tk), lambda i, j, kk: (kk // per, i, kk % per))
    if b_halves:
        per_n = (n // 2) // tn
        b_spec = pl.BlockSpec((None, tk, tn), lambda i, j, kk: (j // per_n, kk, j % per_n))
    o_spec = pl.BlockSpec((tm, tn), lambda i, j, kk: (i, j))
    ca, cb = (0 if ta else 1), (1 if tb else 0)

    n_in = 2 + (res is not None) + (after is not None)

    def body(*refs):
        a_ref, b_ref = refs[:2]
        res_ref = refs[2] if res is not None else None
        o_ref = refs[n_in]
        scratch = refs[n_in + 1:]

        def finish(acc):
            val = acc if alpha == 1.0 else alpha * acc
            if res_ref is not None:
                val = res_ref[...].astype(F32) + val
            o_ref[...] = val.astype(o_ref.dtype)

        part = _dot(a_ref[...].astype(BF16), b_ref[...].astype(BF16), ca, cb)
        if nk == 1:
            finish(part)
        else:
            acc_ref = scratch[0]
            kk = pl.program_id(2)

            @pl.when(kk == 0)
            def _():
                acc_ref[...] = part

            @pl.when(kk > 0)
            def _():
                acc_ref[...] += part

            @pl.when(kk == nk - 1)
            def _():
                finish(acc_ref[...])

    ins = [a, b] + ([] if res is None else [res]) + ([] if after is None else [after])
    in_specs = [a_spec, b_spec] + ([] if res is None else [o_spec]) + ([] if after is None else [ANY])
    return _pcall(
        body, name=name, grid=(m // tm, n // tn, nk), in_specs=in_specs, out_specs=o_spec,
        out_shape=jax.ShapeDtypeStruct((m, n), out_dtype),
        scratch_shapes=[pltpu.VMEM((tm, tn), F32)] if nk > 1 else [],
        compiler_params=_params("parallel", "parallel", "arbitrary"),
    )(*ins)


def _rowcall(fn, rows, consts, outs, accs=(), *, tm, name, after=None):
    s = rows[0][0].shape[0]
    assert s % tm == 0
    n_read, n_out = len(rows) + len(consts), len(outs)
    n_in = n_read + (after is not None)

    def body(*refs):
        vals = fn(*[r[...] for r in refs[:n_read]])
        vals = vals if isinstance(vals, (tuple, list)) else (vals,)
        for o_ref, v in zip(refs[n_in:n_in + n_out], vals[:n_out]):
            o_ref[...] = v.astype(o_ref.dtype)
        if accs:
            first = pl.program_id(0) == 0
            for a_ref, v in zip(refs[n_in + n_out:], vals[n_out:]):
                tot = jnp.sum(v.astype(F32), axis=0, keepdims=True)

                @pl.when(first)
                def _(a_ref=a_ref, tot=tot):
                    a_ref[...] = tot

                @pl.when(jnp.logical_not(first))
                def _(a_ref=a_ref, tot=tot):
                    a_ref[...] += tot

    in_specs = [pl.BlockSpec((tm, w), lambda i, cb=cb: (i, cb)) for (_, cb, w) in rows]
    in_specs += [pl.BlockSpec(c.shape, lambda i: (0, 0)) for c in consts]
    in_specs += [] if after is None else [ANY]
    out_specs = [pl.BlockSpec((tm, w), lambda i: (i, 0)) for (w, _) in outs]
    out_specs += [pl.BlockSpec((1, w), lambda i: (0, 0)) for w in accs]
    out_shape = [jax.ShapeDtypeStruct((s, w), dt) for (w, dt) in outs]
    out_shape += [jax.ShapeDtypeStruct((1, w), F32) for w in accs]
    return _pcall(
        body, name=name, grid=(s // tm,), in_specs=in_specs, out_specs=out_specs, out_shape=out_shape,
        compiler_params=_params("arbitrary" if accs else "parallel"),
    )(*[r[0] for r in rows], *consts, *([] if after is None else [after]))


def _whole(a):
    return (a, 0, a.shape[1])


def _xhat(x):
    x = x.astype(F32)
    r = lax.rsqrt(jnp.mean(x * x, axis=-1, keepdims=True) + RMS_EPS)
    return x * r, r


def _rms_bwd(dy, x, g):
    xh, r = _xhat(x)
    dxh = dy.astype(F32) * g
    dx = r * (dxh - xh * jnp.mean(dxh * xh, axis=-1, keepdims=True))
    return dx, dy.astype(F32) * xh


def _sigmoid(x):
    return 1.0 / (1.0 + jnp.exp(-x))


def _rms_fwd(x, g, name, tm, after=None):
    d = x.shape[1]
    return _rowcall(lambda xb, gb: _xhat(xb)[0] * gb, [_whole(x)], [g], [(d, BF16)], tm=tm, name=name, after=after)[0]


def _silu_parts(gate):
    sg = _sigmoid(gate)
    return sg, gate * sg


def _ffn_up(n, w_gu, name):
    s, d = n.shape
    f = w_gu.shape[1] // 2
    tn = _pick(f, (1408, 1024, 512, 256, 128))
    tm = _pick(s, (1024, 512, 256, 128))
    nb = f // tn

    def body(n_ref, wg_ref, wu_ref, gu_ref, act_ref):
        nv = n_ref[...]
        gate = _dot(nv, wg_ref[...], 1, 0)
        up = _dot(nv, wu_ref[...], 1, 0)
        gu_ref[0] = gate.astype(gu_ref.dtype)
        gu_ref[1] = up.astype(gu_ref.dtype)
        act_ref[...] = (_silu_parts(gate)[1] * up).astype(act_ref.dtype)

    return _pcall(
        body, name=name, grid=(s // tm, nb),
        in_specs=[pl.BlockSpec((tm, d), lambda i, j: (i, 0)), pl.BlockSpec((d, tn), lambda i, j: (0, j)),
                  pl.BlockSpec((d, tn), lambda i, j: (0, nb + j))],
        out_specs=[pl.BlockSpec((2, tm, tn), lambda i, j: (0, i, j)), pl.BlockSpec((tm, tn), lambda i, j: (i, j))],
        out_shape=[jax.ShapeDtypeStruct((2, s, f), BF16), jax.ShapeDtypeStruct((s, f), BF16)],
        compiler_params=_params("parallel", "parallel"),
    )(n, w_gu, w_gu)


def _ffn_dgu(dhb, w_down, gu, name):
    s, d = dhb.shape
    f = w_down.shape[0]
    tn = _pick(f, (1408, 1024, 512, 256, 128))
    tm = _pick(s, (1024, 512, 256, 128))

    def body(dh_ref, w_ref, gu_ref, o_ref):
        dact = _dot(dh_ref[...], w_ref[...], 1, 1)
        gate, up = gu_ref[0].astype(F32), gu_ref[1].astype(F32)
        sg, silu = _silu_parts(gate)
        o_ref[0] = (dact * up * (sg + silu * (1.0 - sg))).astype(o_ref.dtype)
        o_ref[1] = (dact * silu).astype(o_ref.dtype)

    blk = pl.BlockSpec((2, tm, tn), lambda i, j: (0, i, j))
    return _pcall(
        body, name=name, grid=(s // tm, f // tn),
        in_specs=[pl.BlockSpec((tm, d), lambda i, j: (i, 0)), pl.BlockSpec((tn, d), lambda i, j: (j, 0)), blk],
        out_specs=blk, out_shape=jax.ShapeDtypeStruct((2, s, f), BF16), compiler_params=_params("parallel", "parallel"),
    )(dhb, w_down, gu)


def _dgrad_norm(dy, wmat, dh, x, g, name, *, dy_halves=False, copy_scale=None, after=None):
    s, d = dh.shape
    k = wmat.shape[1]
    tk = k if k <= MM_WHOLE_K else _pick(k, (MM_WHOLE_K, 2048, 1024, 512, 256, 128))
    if dy_halves and (k // 2) % tk:
        tk = _pick(k // 2, (1408, 1024, 512, 256, 128))
    tm = _pick(s, (512, 256, 128))
    nk, per = k // tk, (k // 2) // tk if dy_halves else 0
    n_in = 5 + (after is not None)
    n_out = 2 + (copy_scale is not None)

    def body(*refs):
        dy_ref, w_ref, dh_ref, x_ref, g_ref = refs[:5]
        outs, scratch = refs[n_in:n_in + n_out], refs[n_in + n_out:]
        i, kk = pl.program_id(0), pl.program_id(1)
        part = _dot(dy_ref[...], w_ref[...], 1, 1)

        def finish(dn):
            dx, dg = _rms_bwd(dn, x_ref[...], g_ref[...])
            tot = dh_ref[...] + dx
            outs[0][...] = tot
            if copy_scale is not None:
                outs[1][...] = (copy_scale * tot).astype(outs[1].dtype)
            dg = jnp.sum(dg, axis=0, keepdims=True)

            @pl.when(i == 0)
            def _():
                outs[-1][...] = dg

            @pl.when(i > 0)
            def _():
                outs[-1][...] += dg

        if nk == 1:
            finish(part)
        else:
            acc_ref = scratch[0]

            @pl.when(kk == 0)
            def _():
                acc_ref[...] = part

            @pl.when(kk > 0)
            def _():
                acc_ref[...] += part

            @pl.when(kk == nk - 1)
            def _():
                finish(acc_ref[...])

    row = pl.BlockSpec((tm, d), lambda i, kk: (i, 0))
    dy_spec = pl.BlockSpec((None, tm, tk), lambda i, kk: (kk // per, i, kk % per)) if dy_halves else pl.BlockSpec((tm, tk), lambda i, kk: (i, kk))
    in_specs = [dy_spec, pl.BlockSpec((d, tk), lambda i, kk: (0, kk)), row, row, pl.BlockSpec((1, d), lambda i, kk: (0, 0))]
    out_specs = [row] * (n_out - 1) + [pl.BlockSpec((1, d), lambda i, kk: (0, 0))]
    out_shape = [jax.ShapeDtypeStruct((s, d), F32)] + ([] if copy_scale is None else [jax.ShapeDtypeStruct((s, d), BF16)])
    return _pcall(
        body, name=name, grid=(s // tm, nk), in_specs=in_specs + ([] if after is None else [ANY]), out_specs=out_specs,
        out_shape=out_shape + [jax.ShapeDtypeStruct((1, d), F32)], scratch_shapes=[pltpu.VMEM((tm, d), F32)] if nk > 1 else [],
        compiler_params=_params("arbitrary", "arbitrary"),
    )(dy, wmat, dh, x, g, *([] if after is None else [after]))


def _shift_down(p, k):
    if k == 0:
        return p
    rows = lax.broadcasted_iota(jnp.int32, p.shape, 0)
    return jnp.where(rows >= k, pltpu.roll(p, k, 0), 0.0)


def _shift_up(p, k):
    if k == 0:
        return p
    s = p.shape[0]
    rows = lax.broadcasted_iota(jnp.int32, p.shape, 0)
    return jnp.where(rows < s - k, pltpu.roll(p, s - k, 0), 0.0)


def _conv_fwd(proj, conv_w, d, tc, name):
    s = proj.shape[0]
    nb = d // tc

    def body(cb_ref, cc_ref, cx_ref, w_ref, y_ref):
        p = cc_ref[...].astype(F32) * cx_ref[...].astype(F32)
        w = w_ref[...]
        acc = p * w[CONV_K - 1:CONV_K, :]
        for k in range(1, CONV_K):
            acc = acc + _shift_down(p, k) * w[CONV_K - 1 - k:CONV_K - k, :]
        y_ref[...] = (cb_ref[...].astype(F32) * acc).astype(y_ref.dtype)

    col = lambda off: pl.BlockSpec((s, tc), lambda j: (0, off * nb + j))
    return _pcall(
        body, name=name, grid=(nb,), in_specs=[col(0), col(1), col(2), pl.BlockSpec((CONV_K, tc), lambda j: (0, j))],
        out_specs=pl.BlockSpec((s, tc), lambda j: (0, j)), out_shape=jax.ShapeDtypeStruct((s, d), BF16),
        compiler_params=_params("parallel"),
    )(proj, proj, proj, conv_w)


def _conv_bwd(dy, proj, conv_w, d, tc, name):
    s = proj.shape[0]
    nb = d // tc

    def body(dy_ref, cb_ref, cc_ref, cx_ref, w_ref, dcb_ref, dcc_ref, dcx_ref, dw_ref):
        cc, cx = cc_ref[...].astype(F32), cx_ref[...].astype(F32)
        p = cc * cx
        w = w_ref[...]
        dyv = dy_ref[...].astype(F32)
        shifted = [_shift_down(p, CONV_K - 1 - k) for k in range(CONV_K)]
        conv = shifted[0] * w[0:1, :]
        for k in range(1, CONV_K):
            conv = conv + shifted[k] * w[k:k + 1, :]
        dcb_ref[...] = (dyv * conv).astype(dcb_ref.dtype)
        ds = dyv * cb_ref[...].astype(F32)
        dp = ds * w[CONV_K - 1:CONV_K, :]
        for k in range(1, CONV_K):
            dp = dp + _shift_up(ds, k) * w[CONV_K - 1 - k:CONV_K - k, :]
        dcc_ref[...] = (dp * cx).astype(dcc_ref.dtype)
        dcx_ref[...] = (dp * cc).astype(dcx_ref.dtype)
        for k in range(CONV_K):
            dw_ref[k:k + 1, :] = jnp.sum(ds * shifted[k], axis=0, keepdims=True)

    col = lambda off: pl.BlockSpec((s, tc), lambda j: (0, off * nb + j))
    blk = pl.BlockSpec((s, tc), lambda j: (0, j))
    wblk = pl.BlockSpec((CONV_K, tc), lambda j: (0, j))
    act = jax.ShapeDtypeStruct((s, d), BF16)
    return _pcall(
        body, name=name, grid=(nb,), in_specs=[blk, col(0), col(1), col(2), wblk],
        out_specs=[blk, blk, blk, wblk], out_shape=[act, act, act, jax.ShapeDtypeStruct((CONV_K, d), F32)],
        compiler_params=_params("parallel"),
    )(dy, proj, proj, proj, conv_w)


def _sb_tile(q, kj, scale, carry, tri, mask):
    z = _dot(q, kj, 1, 1) * scale
    lsz = jnp.minimum(z, 0.0) - jnp.log(1.0 + jnp.exp(-jnp.abs(z)))
    l1m = lsz - z
    if mask is not None:
        l1m = jnp.where(mask, l1m, 0.0)
    l1b = l1m.astype(BF16)
    a = jnp.exp(lsz + (carry + _dot(l1b, tri, 1, 0)))
    if mask is not None:
        a = jnp.where(mask, a, 0.0)
    return lsz, l1b, a.astype(BF16)


def _sb_masks(tq, tk):
    row = lax.broadcasted_iota(jnp.int32, (tq, tk), 0)
    col = lax.broadcasted_iota(jnp.int32, (tq, tk), 1)
    masks = [col + dj * tk < row for dj in range(tq // tk)]
    r2 = lax.broadcasted_iota(jnp.int32, (tk, tk), 0)
    c2 = lax.broadcasted_iota(jnp.int32, (tk, tk), 1)
    return masks, (r2 > c2).astype(BF16), (r2 < c2).astype(BF16)


def _sb_fwd(proj, heads, col0, tq, tk, name):
    s = proj.shape[0]
    dh = SB_HEAD_DIM
    nq, nd = s // tq, tq // tk
    scale = dh ** -0.5

    def body(q_ref, k_ref, v_ref, o_ref):
        i = pl.program_id(1)
        q = q_ref[...]
        masks, tri_right, _ = _sb_masks(tq, tk)

        def tile(j, carry, acc, mask):
            start = pl.multiple_of(j * tk, tk)
            kj = k_ref[pl.ds(start, tk), :]
            vj = v_ref[pl.ds(start, tk), :]
            _, l1b, ab = _sb_tile(q, kj, scale, carry, tri_right, mask)
            return carry + jnp.sum(l1b.astype(F32), axis=1, keepdims=True), acc + _dot(ab, vj, 1, 0)

        state = (jnp.zeros((tq, 1), F32), jnp.zeros((tq, dh), F32))
        for dj in reversed(range(nd)):
            state = tile(i * nd + dj, *state, masks[dj])
        state = lax.fori_loop(0, i * nd, lambda t, st: tile(i * nd - 1 - t, st[0], st[1], None), state)
        o_ref[...] = state[1]

    qspec = pl.BlockSpec((tq, dh), lambda h, i: (i, col0[0] + h))
    kspec = pl.BlockSpec((s, dh), lambda h, i: (0, col0[1] + h))
    vspec = pl.BlockSpec((s, dh), lambda h, i: (0, col0[2] + h))
    return _pcall(
        body, name=name, grid=(heads, nq), in_specs=[qspec, kspec, vspec],
        out_specs=pl.BlockSpec((tq, dh), lambda h, i: (i, h)), out_shape=jax.ShapeDtypeStruct((s, heads * dh), F32),
        compiler_params=_params("parallel", "parallel"),
    )(proj, proj, proj)


def _sb_bwd(proj, o, do, heads, col0, tq, tk, name):
    s = proj.shape[0]
    dh = SB_HEAD_DIM
    nq, nd = s // tq, tq // tk
    scale = dh ** -0.5

    def body(q_ref, k_ref, v_ref, o_ref, do_ref, dq_ref, dk_ref, dv_ref, dk_acc, dv_acc):
        i = pl.program_id(1)

        @pl.when(i == 0)
        def _():
            dk_acc[...] = jnp.zeros_like(dk_acc)
            dv_acc[...] = jnp.zeros_like(dv_acc)

        q = q_ref[...]
        dob = do_ref[...].astype(BF16)
        delta = jnp.sum(dob.astype(F32) * o_ref[...], axis=1, keepdims=True)
        masks, tri_right, tri_left = _sb_masks(tq, tk)

        def tile(j, carry_l, carry_g, dq, mask):
            start = pl.multiple_of(j * tk, tk)
            kj = k_ref[pl.ds(start, tk), :]
            vj = v_ref[pl.ds(start, tk), :]
            lsz, l1b, ab = _sb_tile(q, kj, scale, carry_l, tri_right, mask)
            g = _dot(dob, vj, 1, 1) * ab.astype(F32)
            carry_g = carry_g + jnp.sum(g, axis=1, keepdims=True)
            left = (delta - carry_g) + _dot(g.astype(BF16), tri_left, 1, 0)
            dz = g - jnp.exp(lsz) * (g + left)
            if mask is not None:
                dz = jnp.where(mask, dz, 0.0)
            dzb = dz.astype(BF16)
            dk_acc[pl.ds(start, tk), :] += _dot(dzb, q, 0, 0)
            dv_acc[pl.ds(start, tk), :] += _dot(ab, dob, 0, 0)
            return carry_l + jnp.sum(l1b.astype(F32), axis=1, keepdims=True), carry_g, dq + _dot(dzb, kj, 1, 0)

        zero = jnp.zeros((tq, 1), F32)
        state = (zero, zero, jnp.zeros((tq, dh), F32))
        for dj in reversed(range(nd)):
            state = tile(i * nd + dj, *state, masks[dj])
        state = lax.fori_loop(0, i * nd, lambda t, st: tile(i * nd - 1 - t, st[0], st[1], st[2], None), state)
        dq_ref[...] = (state[2] * scale).astype(dq_ref.dtype)

        @pl.when(i == nq - 1)
        def _():
            dk_ref[...] = (dk_acc[...] * scale).astype(dk_ref.dtype)
            dv_ref[...] = dv_acc[...].astype(dv_ref.dtype)

    qspec = pl.BlockSpec((tq, dh), lambda h, i: (i, col0[0] + h))
    kspec = pl.BlockSpec((s, dh), lambda h, i: (0, col0[1] + h))
    vspec = pl.BlockSpec((s, dh), lambda h, i: (0, col0[2] + h))
    blk = pl.BlockSpec((tq, dh), lambda h, i: (i, h))
    full = pl.BlockSpec((s, dh), lambda h, i: (0, h))
    act = jax.ShapeDtypeStruct((s, heads * dh), BF16)
    return _pcall(
        body, name=name, grid=(heads, nq), in_specs=[qspec, kspec, vspec, blk, blk],
        out_specs=[blk, full, full], out_shape=[act, act, act],
        scratch_shapes=[pltpu.VMEM((s, dh), F32), pltpu.VMEM((s, dh), F32)],
        compiler_params=_params("parallel", "arbitrary"),
    )(proj, proj, proj, o, do)


def _xattn_probs(q, k, scale):
    sc = _dot(q, k, 1, 1) * scale
    e = jnp.exp(sc - jnp.max(sc, axis=1, keepdims=True))
    return e / jnp.sum(e, axis=1, keepdims=True)


def _xattn_fwd(qc, kv, tq, name):
    s, d = qc.shape
    m = kv.shape[0]
    dh = d // X_HEADS
    scale = dh ** -0.5

    def body(q_ref, k_ref, v_ref, o_ref):
        p = _xattn_probs(q_ref[...], k_ref[...], scale)
        o_ref[...] = _dot(p.astype(BF16), v_ref[...], 1, 0).astype(o_ref.dtype)

    blk = pl.BlockSpec((tq, dh), lambda h, i: (i, h))
    return _pcall(
        body, name=name, grid=(X_HEADS, s // tq),
        in_specs=[blk, pl.BlockSpec((m, dh), lambda h, i: (0, h)), pl.BlockSpec((m, dh), lambda h, i: (0, X_HEADS + h))],
        out_specs=blk, out_shape=jax.ShapeDtypeStruct((s, d), BF16), compiler_params=_params("parallel", "parallel"),
    )(qc, kv, kv)


def _xattn_bwd(qc, kv, do, tq, name):
    s, d = qc.shape
    m = kv.shape[0]
    dh = d // X_HEADS
    scale = dh ** -0.5
    nq = s // tq

    def body(q_ref, k_ref, v_ref, do_ref, dq_ref, dk_ref, dv_ref, dk_acc, dv_acc):
        i = pl.program_id(1)
        q, k, v = q_ref[...], k_ref[...], v_ref[...]
        dob = do_ref[...].astype(BF16)
        p = _xattn_probs(q, k, scale)
        pb = p.astype(BF16)
        dp = _dot(dob, v, 1, 1)
        ds = pb.astype(F32) * (dp - jnp.sum(dp * pb.astype(F32), axis=1, keepdims=True))
        dsb = (ds * scale).astype(BF16)
        dq_ref[...] = _dot(dsb, k, 1, 0).astype(dq_ref.dtype)
        dk_part = _dot(dsb, q, 0, 0)
        dv_part = _dot(pb, dob, 0, 0)

        @pl.when(i == 0)
        def _():
            dk_acc[...] = dk_part
            dv_acc[...] = dv_part

        @pl.when(i > 0)
        def _():
            dk_acc[...] += dk_part
            dv_acc[...] += dv_part

        @pl.when(i == nq - 1)
        def _():
            dk_ref[...] = dk_acc[...].astype(dk_ref.dtype)
            dv_ref[...] = dv_acc[...].astype(dv_ref.dtype)

    blk = pl.BlockSpec((tq, dh), lambda h, i: (i, h))
    kblk = pl.BlockSpec((m, dh), lambda h, i: (0, h))
    return _pcall(
        body, name=name, grid=(X_HEADS, nq),
        in_specs=[blk, kblk, pl.BlockSpec((m, dh), lambda h, i: (0, X_HEADS + h)), blk],
        out_specs=[blk, kblk, kblk],
        out_shape=[jax.ShapeDtypeStruct((s, d), BF16), jax.ShapeDtypeStruct((m, d), BF16), jax.ShapeDtypeStruct((m, d), BF16)],
        scratch_shapes=[pltpu.VMEM((m, dh), F32), pltpu.VMEM((m, dh), F32)],
        compiler_params=_params("parallel", "arbitrary"),
    )(qc, kv, kv, do)


def _local_step(x, mem, tgt, w, fetch=None, prefetch=None, emit=None, tick=None, after=None):
    fetch = fetch or (lambda name, after: {})
    prefetch = prefetch or (lambda name, after: None)
    emit = emit or (lambda group, g: None)
    tick = tick or (lambda group, after: None)
    w = dict(w)
    s, d = x.shape
    heads = d // SB_HEAD_DIM
    tm = _pick(s, (512, 256, 128))
    tq = _pick(s, (256, 128))
    sb_tq, sb_tk = _pick(s, (512, 256, 128)), _pick(s, (256, 128))
    tc = _pick(d, (256, 128))
    g = {}

    def wt(name, after):
        if name not in w:
            w.update(fetch(name, after))
        return w[name]

    def ffn_fwd(h, gname, wgu, wdown, tag, after=None):
        n = _rms_fwd(h, w[gname], tag + "_norm", tm, after=after)
        gu, act = _ffn_up(n, wt(wgu, n), tag + "_gu")
        prefetch(wdown, gu)
        return n, gu, act, _mm(act, wt(wdown, act), name=tag + "_down", out_dtype=F32, res=h, alpha=0.5)

    def ffn_bwd(dh, dhb, h, saved, gname, wgu, wdown, tag, copy_scale=None, after=None):
        n, gu, act = saved
        g[wdown] = _mm(act, dhb, ta=True, name=tag + "_dwdown", after=after)
        dgu = _ffn_dgu(dhb, w[wdown], gu, tag + "_dgu")
        g[wgu] = _mm(n, dgu, ta=True, b_halves=True, name=tag + "_dwgu")
        *dh_in, g[gname] = _dgrad_norm(dgu, w[wgu], dh, h, w[gname], tag + "_dn", dy_halves=True, copy_scale=copy_scale,
                                       after=emit(tag, g))
        return dh_in, tick(tag, dh_in[0])

    n1, gu1, act1, h1 = ffn_fwd(x, "g_ffn1", "w_ffn1_gu", "w_ffn1_down", "ffn1", after)
    prefetch("w_in", h1)
    u = _rms_fwd(h1, w["g_mix"], "mix_norm", tm)
    proj = _mm(u, wt("w_in", u), name="mix_in")
    prefetch("w_conv_out", proj)
    nd = d // SB_HEAD_DIM
    y_conv = _conv_fwd(proj, w["conv_w"], d, tc, "conv_fwd")
    sb_cols = (3 * nd, 4 * nd, 5 * nd)
    y_sb = _sb_fwd(proj, heads, sb_cols, sb_tq, sb_tk, "sb_fwd")
    prefetch("w_cq", y_sb)
    a_conv = _mm(y_conv, wt("w_conv_out", y_conv), name="conv_out")
    a_sb = _mm(y_sb, wt("w_attn_out", y_sb), name="attn_out")
    b_conv, b_sb = w["b_gate"][:, :d], w["b_gate"][:, d:]

    def merge(ac, asb, gcp, gsp, bc, bs):
        gc = _sigmoid(gcp.astype(F32) + bc)
        gs = _sigmoid(gsp.astype(F32) + bs)
        return gc * ac.astype(F32) + gs * asb.astype(F32)

    merged = _rowcall(merge, [_whole(a_conv), _whole(a_sb), (proj, 6, d), (proj, 7, d)], [b_conv, b_sb], [(d, BF16)],
                      tm=tm, name="merge")[0]
    prefetch("w_ffn2_gu", merged)
    h2 = _mm(merged, wt("w_o", merged), name="mix_out", out_dtype=F32, res=h1)
    hn = _rms_fwd(h2, w["g_cross"], "cross_norm", tm)
    mn = _rms_fwd(mem, w["g_mem"], "mem_norm", _pick(mem.shape[0], (256, 128)))
    qc = _mm(hn, wt("w_cq", hn), name="cross_q")
    kv = _mm(mn, wt("w_ckv", mn), name="cross_kv")
    oc = _xattn_fwd(qc, kv, tq, "xattn_fwd")
    h3 = _mm(oc, wt("w_co", oc), name="cross_out", out_dtype=F32, res=h2)
    n2, gu2, act2, h4 = ffn_fwd(h3, "g_ffn2", "w_ffn2_gu", "w_ffn2_down", "ffn2")

    def head(hb, tb, gb):
        xh, r = _xhat(hb)
        err = xh * gb - tb
        dy = err * (1.0 / d)
        dxh = dy * gb
        dx = r * (dxh - xh * jnp.mean(dxh * xh, axis=-1, keepdims=True))
        row_loss = 0.5 * jnp.mean(err * err, axis=-1, keepdims=True)
        return dx, 0.5 * dx, dy * xh, jnp.broadcast_to(row_loss, (row_loss.shape[0], LANES))

    dh4, dh4b, g["g_final"], loss_lanes = _rowcall(head, [_whole(h4), _whole(tgt)], [w["g_final"]], [(d, F32), (d, BF16)],
                                                   [d, LANES], tm=tm, name="loss_head")

    (dh3, dh3b), tok = ffn_bwd(dh4, dh4b, h3, (n2, gu2, act2), "g_ffn2", "w_ffn2_gu", "w_ffn2_down", "ffn2", copy_scale=1.0)
    g["w_co"] = _mm(oc, dh3b, ta=True, name="cross_dwco", after=tok)
    doc = _mm(dh3b, w["w_co"], tb=True, name="cross_doc")
    dqc, dk, dv = _xattn_bwd(qc, kv, doc, tq, "xattn_bwd")
    dkv = jnp.concatenate([dk, dv], axis=1)
    g["w_cq"] = _mm(hn, dqc, ta=True, name="cross_dwcq")
    g["w_ckv"] = _mm(mn, dkv, ta=True, name="cross_dwckv")
    dmn = _mm(dkv, w["w_ckv"], tb=True, name="cross_dmn", out_dtype=F32)
    g["g_mem"] = _rowcall(lambda dy, xb: dy * _xhat(xb)[0], [_whole(dmn), _whole(mem)], [], [], [d],
                          tm=_pick(mem.shape[0], (256, 128)), name="mem_dnorm")[0]
    dh2, dh2b, g["g_cross"] = _dgrad_norm(dqc, w["w_cq"], dh3, h2, w["g_cross"], "cross_dhn", copy_scale=1.0, after=emit("cross", g))

    g["w_o"] = _mm(merged, dh2b, ta=True, name="mix_dwo", after=tick("cross", dh2))
    dmerged = _mm(dh2b, w["w_o"], tb=True, name="mix_dmerged")

    def merge_bwd(dm, ac, asb, gcp, gsp, bc, bs):
        dm, ac, asb = dm.astype(F32), ac.astype(F32), asb.astype(F32)
        gc = _sigmoid(gcp.astype(F32) + bc)
        gs = _sigmoid(gsp.astype(F32) + bs)
        dgc = dm * ac * gc * (1.0 - gc)
        dgs = dm * asb * gs * (1.0 - gs)
        return dm * gc, dm * gs, dgc, dgs, dgc, dgs

    da_conv, da_sb, dgc, dgs, db_conv, db_sb = _rowcall(
        merge_bwd, [_whole(dmerged), _whole(a_conv), _whole(a_sb), (proj, 6, d), (proj, 7, d)], [b_conv, b_sb],
        [(d, BF16)] * 4, [d, d], tm=tm, name="merge_bwd")
    g["b_gate"] = jnp.concatenate([db_conv, db_sb], axis=1)
    g["w_conv_out"] = _mm(y_conv, da_conv, ta=True, name="conv_dwout")
    g["w_attn_out"] = _mm(y_sb, da_sb, ta=True, name="attn_dwout")
    dy_conv = _mm(da_conv, w["w_conv_out"], tb=True, name="conv_dy")
    dy_sb = _mm(da_sb, w["w_attn_out"], tb=True, name="attn_dy")
    dcb, dcc, dcx, g["conv_w"] = _conv_bwd(dy_conv, proj, w["conv_w"], d, tc, "conv_bwd")
    dq, dk_sb, dv_sb = _sb_bwd(proj, y_sb, dy_sb, heads, sb_cols, sb_tq, sb_tk, "sb_bwd")
    dproj = jnp.concatenate([dcb, dcc, dcx, dq, dk_sb, dv_sb, dgc, dgs], axis=1)
    g["w_in"] = _mm(u, dproj, ta=True, name="mix_dwin")
    dh1, dh1b, g["g_mix"] = _dgrad_norm(dproj, w["w_in"], dh2, h1, w["g_mix"], "mix_du", copy_scale=0.5, after=emit("mix", g))
    (dx,), _ = ffn_bwd(dh1, dh1b, x, (n1, gu1, act1), "g_ffn1", "w_ffn1_gu", "w_ffn1_down", "ffn1", after=tick("mix", dh1))
    return loss_lanes, dx, g


MATS = (("w_ffn1_gu", "col"), ("w_ffn1_down", "row"), ("w_in", "col"), ("w_conv_out", "row"), ("w_attn_out", "row"),
        ("w_o", "row"), ("w_cq", "row"), ("w_ckv", "col"), ("w_co", "row"), ("w_ffn2_gu", "col"), ("w_ffn2_down", "row"))
VECS = ("g_ffn1", "g_mix", "g_cross", "g_mem", "g_ffn2", "g_final")
WEIGHTS = ("g_ffn1", "w_ffn1_gu", "w_ffn1_down", "g_mix", "w_in", "b_gate", "conv_w", "w_conv_out", "w_attn_out", "w_o",
           "g_cross", "g_mem", "w_cq", "w_ckv", "w_co", "g_ffn2", "w_ffn2_gu", "w_ffn2_down", "g_final")
CONV_ROWS = 8


def _full_shape(kind, r, c):
    return (r, N_CHIPS * c) if kind == "col" else (N_CHIPS * r, c)


def _piece(ref, kind, r, c, chip, half):
    hr = r // 2
    if kind == "col":
        return ref.at[pl.ds(pl.multiple_of(half * hr, 16), hr), pl.ds(pl.multiple_of(chip * c, LANES), c)]
    return ref.at[pl.ds(pl.multiple_of(chip * r + half * hr, 16), hr), :]


def _shard_of(ref, kind, r, c, chip):
    if kind == "col":
        return ref.at[:, pl.ds(pl.multiple_of(chip * c, LANES), c)]
    return ref.at[pl.ds(pl.multiple_of(chip * r, 16), r), :]


def _place():
    x, y, c = lax.axis_index("x"), lax.axis_index("y"), lax.axis_index("c")
    others = [(1 - x, y), (x, 1 - y), (1 - x, 1 - y)]
    return x, y, c, 2 * x + y, others


def _remote(src, dst, send_sem, recv_sem, to):
    return pltpu.make_async_remote_copy(src_ref=src, dst_ref=dst, send_sem=send_sem, recv_sem=recv_sem,
                                        device_id=to, device_id_type=MESH)


def _gather_conv(conv_shard, after):
    cc = conv_shard.shape[1]

    def body(conv_ref, _, conv_full, cs, cr, cl):
        x, y, c, me, others = _place()

        def cols(chip):
            return conv_full.at[:, pl.ds(pl.multiple_of(chip * cc, LANES), cc)]

        def conv(k, chip_from, to):
            return _remote(conv_ref, cols(chip_from), cs.at[k], cr.at[k], to)

        mine = pltpu.make_async_copy(conv_ref, cols(me), cl.at[0])
        mine.start()
        for k, (ox, oy) in enumerate(others):
            conv(k, me, (ox, oy, c)).start()
        for k, (ox, oy) in enumerate(others):
            conv(k, 2 * ox + oy, (x, y, c)).wait_recv()
            conv(k, me, (ox, oy, c)).wait_send()
        mine.wait()

    dma = pltpu.SemaphoreType.DMA
    return _pcall(
        body, name="gather_conv", in_specs=[ANY, ANY], out_specs=ANY,
        out_shape=jax.ShapeDtypeStruct((CONV_ROWS, N_CHIPS * cc), F32), scratch_shapes=[dma((3,)), dma((3,)), dma((1,))],
    )(conv_shard, after)


HBM = pl.BlockSpec(memory_space=pltpu.HBM)
SEM = pl.BlockSpec(memory_space=pltpu.SEMAPHORE)
EFFECT = pltpu.SideEffectType.DATAFLOW_SIDE_EFFECTING
TOKEN = (8, LANES)


def _split_start(name, plan, n_copies, srcs, lands, after=None):
    ns, nl = len(srcs), len(lands)
    n_in = ns + nl + (after is not None)

    def body(*refs):
        outs = refs[n_in:]
        sends, _ = plan(refs[:ns], refs[ns:ns + nl], outs[0], outs[1])
        for cp in sends:
            cp.start()
        outs[-1][...] = jnp.zeros(TOKEN, F32)

    held = [pltpu.HBM(a.shape, a.dtype) for a in (*srcs, *lands)]
    dma = pltpu.SemaphoreType.DMA((n_copies,))
    ins = [pltpu.with_memory_space_constraint(a, pltpu.HBM) for a in (*srcs, *lands)]
    outs = _pcall(
        body, name=name, in_specs=[HBM] * (ns + nl) + ([] if after is None else [ANY]),
        out_specs=(SEM, SEM, *[HBM] * (ns + nl), pl.BlockSpec(memory_space=pltpu.VMEM)),
        out_shape=(dma, dma, *held, jax.ShapeDtypeStruct(TOKEN, F32)),
        input_output_aliases={i: 2 + i for i in range(ns + nl)},
        compiler_params=pltpu.CompilerParams(has_side_effects=EFFECT),
    )(*ins, *([] if after is None else [after]))
    return outs[0], outs[1], list(outs[2:2 + ns]), list(outs[2 + ns:2 + ns + nl]), outs[-1]


def _split_wait(name, plan, send_sems, recv_sems, srcs, lands, after):
    ns, nl = len(srcs), len(lands)

    def body(*refs):
        sends, recvs = plan(refs[:ns], refs[ns:ns + nl], refs[ns + nl], refs[ns + nl + 1])
        for cp in sends:
            cp.wait_send()
        for cp in recvs:
            cp.wait_recv()

    outs = _pcall(
        body, name=name, in_specs=[HBM] * (ns + nl) + [SEM, SEM, ANY], out_specs=[HBM] * (ns + nl),
        out_shape=[pltpu.HBM(a.shape, a.dtype) for a in (*srcs, *lands)],
        input_output_aliases={i: i for i in range(ns + nl)},
        compiler_params=pltpu.CompilerParams(has_side_effects=EFFECT),
    )(*srcs, *lands, send_sems, recv_sems, after)
    return list(outs[:ns]), list(outs[ns:])


def _gather_plan(dims):
    def plan(shard_refs, full_refs, ss, rs):
        x, y, c, me, others = _place()
        sends, recvs = [], []
        for wi, (kind, r, cw) in enumerate(dims):
            half = shard_refs[wi].at[pl.ds(pl.multiple_of(c * (r // 2), 16), r // 2), :]
            for k, (ox, oy) in enumerate(others):
                sem = 4 * wi + k
                sends.append(_remote(half, _piece(full_refs[wi], kind, r, cw, me, c), ss.at[sem], rs.at[sem], (ox, oy, c)))
                recvs.append(_remote(half, _piece(full_refs[wi], kind, r, cw, 2 * ox + oy, c), ss.at[sem], rs.at[sem], (x, y, c)))
            sem = 4 * wi + 3
            own = _remote(shard_refs[wi], _shard_of(full_refs[wi], kind, r, cw, me), ss.at[sem], rs.at[sem], (x, y, 1 - c))
            sends.append(own)
            recvs.append(own)
        return sends, recvs

    return plan


def _forward_plan(dims):
    def plan(_, full_refs, ss, rs):
        x, y, c, _, others = _place()
        sends, recvs = [], []
        for wi, (kind, r, cw) in enumerate(dims):
            for k, (ox, oy) in enumerate(others):
                sem = 3 * wi + k
                mine = _piece(full_refs[wi], kind, r, cw, 2 * ox + oy, c)
                theirs = _piece(full_refs[wi], kind, r, cw, 2 * ox + oy, 1 - c)
                sends.append(_remote(mine, mine, ss.at[sem], rs.at[sem], (x, y, 1 - c)))
                recvs.append(_remote(theirs, theirs, ss.at[sem], rs.at[sem], (x, y, 1 - c)))
        return sends, recvs

    return plan


def _rs_cores_plan(dims):
    def plan(g_refs, land_refs, ss, rs):
        x, y, c, _, _ = _place()
        sends, recvs = [], []
        for wi, dm in enumerate(dims):
            for chip in range(N_CHIPS):
                sem = N_CHIPS * wi + chip
                sends.append(_remote(_piece(g_refs[wi], *dm, chip, 1 - c), land_refs[wi].at[chip], ss.at[sem], rs.at[sem], (x, y, 1 - c)))
                recvs.append(_remote(_piece(g_refs[wi], *dm, chip, c), land_refs[wi].at[chip], ss.at[sem], rs.at[sem], (x, y, 1 - c)))
        return sends, recvs

    return plan


def _share_plan(nw):
    def plan(_, buf_refs, ss, rs):
        x, y, c, _, _ = _place()
        sends = [_remote(buf_refs[wi].at[c], buf_refs[wi].at[c], ss.at[wi], rs.at[wi], (x, y, 1 - c)) for wi in range(nw)]
        recvs = [_remote(buf_refs[wi].at[1 - c], buf_refs[wi].at[1 - c], ss.at[wi], rs.at[wi], (x, y, 1 - c)) for wi in range(nw)]
        return sends, recvs

    return plan


def _small_plan():
    def plan(_, buf_refs, ss, rs):
        x, y, c = lax.axis_index("x"), lax.axis_index("y"), lax.axis_index("c")
        buf = buf_refs[0]
        sends, recvs = [], []
        for rel in range(1, N_DEV):
            peer = (x ^ (rel >> 2 & 1), y ^ (rel >> 1 & 1), c ^ (rel & 1))
            sends.append(_remote(buf.at[0], buf.at[rel], ss.at[rel - 1], rs.at[rel - 1], peer))
            recvs.append(_remote(buf.at[0], buf.at[rel], ss.at[rel - 1], rs.at[rel - 1], peer))
        return sends, recvs

    return plan


def _sum_small(buf, me, name):
    _, rows, n = buf.shape

    def body(me_ref, b_ref, o_ref):
        tot = b_ref[me_ref[0]]
        for dev in range(1, N_DEV):
            tot = tot + b_ref[dev ^ me_ref[0]]
        o_ref[...] = tot

    return _pcall(
        body, name=name, out_shape=jax.ShapeDtypeStruct((rows, n), F32),
        grid_spec=pltpu.PrefetchScalarGridSpec(
            num_scalar_prefetch=1, grid=(1,), in_specs=[pl.BlockSpec((N_DEV, rows, n), lambda i, m: (0, 0, 0))],
            out_specs=pl.BlockSpec((rows, n), lambda i, m: (0, 0))),
    )(me, buf)


def _rs_chips_plan(nw):
    def plan(p_refs, land_refs, ss, rs):
        x, y, c, me, others = _place()
        sends, recvs = [], []
        for wi in range(nw):
            for k, (ox, oy) in enumerate(others):
                sem = 3 * wi + k
                sends.append(_remote(p_refs[wi].at[2 * ox + oy], land_refs[wi].at[k], ss.at[sem], rs.at[sem], (ox, oy, c)))
                recvs.append(_remote(p_refs[wi].at[me], land_refs[wi].at[k], ss.at[sem], rs.at[sem], (x, y, c)))
        return sends, recvs

    return plan


def _rows_per_block(n, c, limit_bytes=2 << 20):
    best = None
    for tm in range(16, n + 1, 16):
        if n % tm == 0 and tm * c * 4 <= limit_bytes:
            best = tm
    return best or n


def _sum_cores(grad, got, kind, place, name):
    _, hr, cw = got.shape
    tm = _rows_per_block(hr, cw)
    nb = hr // tm

    def body(place_ref, g_ref, t_ref, o_ref):
        o_ref[...] = (g_ref[...].astype(F32) + t_ref[...].astype(F32)).astype(o_ref.dtype)

    if kind == "col":
        g_spec = pl.BlockSpec((tm, cw), lambda j, i, pr: (pr[0] * nb + i, j))
    else:
        g_spec = pl.BlockSpec((tm, cw), lambda j, i, pr: ((2 * j + pr[0]) * nb + i, 0))
    blk = pl.BlockSpec((None, tm, cw), lambda j, i, pr: (j, i, 0))
    return _pcall(
        body, name=name, out_shape=jax.ShapeDtypeStruct(got.shape, BF16),
        grid_spec=pltpu.PrefetchScalarGridSpec(num_scalar_prefetch=1, grid=(N_CHIPS, nb), in_specs=[g_spec, blk], out_specs=blk),
        compiler_params=_params("parallel", "parallel"),
    )(place, grad, got)


def _sum_chips(parts, got, place, name):
    _, n, cw = got.shape
    tm = _rows_per_block(n, cw)

    def body(place_ref, p_ref, g_ref, o_ref):
        tot = p_ref[...].astype(F32)
        for k in range(3):
            tot = tot + g_ref[k].astype(F32)
        o_ref[...] = tot

    return _pcall(
        body, name=name, out_shape=jax.ShapeDtypeStruct((2, n, cw), F32),
        grid_spec=pltpu.PrefetchScalarGridSpec(
            num_scalar_prefetch=1, grid=(n // tm,),
            in_specs=[pl.BlockSpec((None, tm, cw), lambda i, pr: (pr[1], i, 0)), pl.BlockSpec((3, tm, cw), lambda i, pr: (0, i, 0))],
            out_specs=pl.BlockSpec((None, tm, cw), lambda i, pr: (pr[0], i, 0))),
        compiler_params=_params("parallel"),
    )(place, parts, got)


def _adamw(g, w, m, v, name):
    n, c = g.shape
    c1 = 1.0 - ADAM_B1 ** ADAM_STEP
    c2 = 1.0 - ADAM_B2 ** ADAM_STEP

    def fn(gb, wb, mb, vb):
        m_new = ADAM_B1 * mb + (1.0 - ADAM_B1) * gb
        v_new = ADAM_B2 * vb + (1.0 - ADAM_B2) * (gb * gb)
        delta = -ADAM_LR * ((m_new / c1) / (jnp.sqrt(v_new / c2) + ADAM_EPS) + ADAM_WD * wb)
        return gb, delta, m_new, v_new

    tm = _rows_per_block(n, c) if n % 16 == 0 else n
    return _rowcall(fn, [_whole(g), _whole(w), _whole(m), _whole(v)], [], [(c, F32)] * 4, tm=tm, name=name)


PACK_ROWS = 16


def _pack_rows(parts, width, name, after=None):
    assert sum(p.shape[0] for p in parts) <= PACK_ROWS

    def body(*refs):
        out_ref = refs[-1]
        out_ref[...] = jnp.zeros_like(out_ref)
        at = 0
        for r in refs[:len(parts)]:
            k, n = r.shape
            if n == width:
                out_ref[at:at + k, :] = r[...]
            else:
                out_ref[at:at + k, :] = jnp.broadcast_to(r[:, :1], (k, width))
            at += k

    vm = pl.BlockSpec(memory_space=pltpu.VMEM)
    return _pcall(body, name=name, in_specs=[vm] * len(parts) + ([] if after is None else [ANY]), out_specs=vm,
                  out_shape=jax.ShapeDtypeStruct((PACK_ROWS, width), F32))(*parts, *([] if after is None else [after]))


def _cast_shard(wm, name, after):
    n, c = wm.shape
    return _rowcall(lambda v: v, [_whole(wm)], [], [(c, BF16)], tm=_rows_per_block(n, c), name=name, after=after)[0]


GATHER_GROUPS = (
    ("w_ffn1_gu",), ("w_ffn1_down",), ("w_in",), ("w_conv_out", "w_attn_out", "w_o"), ("w_cq", "w_ckv", "w_co"),
    ("w_ffn2_gu", "w_ffn2_down"),
)
REDUCE_GROUPS = {
    "ffn2": ("w_ffn2_down", "w_ffn2_gu"),
    "cross": ("w_co", "w_cq", "w_ckv"),
    "mix": ("w_o", "w_conv_out", "w_attn_out", "w_in"),
    "ffn1": ("w_ffn1_down", "w_ffn1_gu"),
}
KIND = dict(MATS)


def _step(x, mem, tgt, wts, m_in, v_in):
    d = x.shape[-1]
    cc = wts["conv_w"].shape[1]
    place = jnp.stack([lax.axis_index("c"), 2 * lax.axis_index("x") + lax.axis_index("y")]).astype(jnp.int32)
    dims = {n: (kind, *wts[n].shape) for n, kind in MATS}

    w = {n: wts[n].reshape(1, -1) for n in VECS + ("b_gate",)}
    flying, token = {}, None
    for names in GATHER_GROUPS:
        gd = [dims[n] for n in names]
        shards = [_cast_shard(wts[n], "cast_" + n, token) for n in names]
        lands = [lax.empty(_full_shape(*dm), BF16) for dm in gd]
        plan = _gather_plan(gd)
        ss, rs, srcs, lands, token = _split_start("gather_start_" + names[0], plan, 4 * len(names), shards, lands, token)
        flying.update({n: (names, plan, ss, rs, srcs, lands, gd) for n in names})
    conv_full = _gather_conv(jnp.pad(wts["conv_w"], ((0, CONV_ROWS - CONV_K), (0, 0))), token)
    w["conv_w"], token = conv_full[:CONV_K], conv_full

    passing = {}

    def prefetch(name, after):
        if name not in passing:
            names, plan, ss, rs, srcs, lands, gd = flying[name]
            _, lands = _split_wait("gather_wait_" + names[0], plan, ss, rs, srcs, lands, after)
            plan = _forward_plan(gd)
            ss, rs, _, lands, _ = _split_start("forward_start_" + names[0], plan, 3 * len(names), [], lands)
            passing.update({n: (names, plan, ss, rs, lands) for n in names})

    def fetch(name, after):
        prefetch(name, after)
        names, plan, ss, rs, lands = passing[name]
        _, lands = _split_wait("forward_wait_" + names[0], plan, ss, rs, [], lands, after)
        return dict(zip(names, lands))

    swapping, sent = {}, {}

    def emit(tag, g):
        if tag not in REDUCE_GROUPS:
            return None
        names = REDUCE_GROUPS[tag]
        gd = [dims[n] for n in names]
        lands = [lax.empty((N_CHIPS, r // 2, cw), BF16) for (_, r, cw) in gd]
        plan = _rs_cores_plan(gd)
        ss, rs, srcs, lands, tok = _split_start("rs_cores_start_" + tag, plan, N_CHIPS * len(names), [g[n] for n in names], lands)
        swapping[tag] = (plan, ss, rs, srcs, lands)
        return tok

    def tick(tag, after):
        if tag not in REDUCE_GROUPS:
            return None
        names = REDUCE_GROUPS[tag]
        plan, ss, rs, srcs, lands = swapping[tag]
        mine, got = _split_wait("rs_cores_wait_" + tag, plan, ss, rs, srcs, lands, after)
        parts = [_sum_cores(gm, t, KIND[n], place, "sum_cores_" + n) for n, gm, t in zip(names, mine, got)]
        lands = [lax.empty((3, *p.shape[1:]), BF16) for p in parts]
        plan = _rs_chips_plan(len(names))
        ss, rs, srcs, lands, tok = _split_start("rs_chips_start_" + tag, plan, 3 * len(names), parts, lands)
        sent[tag] = (plan, ss, rs, srcs, lands)
        return tok

    loss_lanes, dx, g = _local_step(x[0], mem[0], tgt[0], w, fetch, prefetch, emit, tick, token)

    rows = [g[n] for n in VECS] + [g["b_gate"][:, :d], g["b_gate"][:, d:], g["conv_w"], loss_lanes]
    packed = _pack_rows(rows, d, "pack_small")
    small = jnp.concatenate([packed[None], jnp.zeros((N_DEV - 1, *packed.shape), F32)], axis=0)
    small_plan = _small_plan()
    small_ss, small_rs, _, small, after = _split_start("small_start", small_plan, N_DEV - 1, [], [small])

    grads, out = {}, {}

    def update(n):
        shape = wts[n].shape
        as2d = (lambda a: a.reshape(1, -1)) if len(shape) == 1 else (lambda a: a)
        return [r.reshape(shape) for r in _adamw(grads[n], as2d(wts[n]), as2d(m_in[n]), as2d(v_in[n]), "adamw_" + n)]

    def finish(sharing, after):
        tag, names, plan, ss, rs, halves = sharing
        _, both = _split_wait("share_wait_" + tag, plan, ss, rs, [], halves, after)
        for n, b in zip(names, both):
            grads[n] = b.reshape(-1, b.shape[-1])
            out[n] = update(n)
        return out[names[-1]][1]

    sharing = None
    for tag, names in REDUCE_GROUPS.items():
        plan, ss, rs, srcs, lands = sent[tag]
        parts, landed = _split_wait("rs_chips_wait_" + tag, plan, ss, rs, srcs, lands, after)
        halves = [_sum_chips(p, t, place, "sum_chips_" + n) for n, p, t in zip(names, parts, landed)]
        plan = _share_plan(len(names))
        ss, rs, _, halves, after = _split_start("share_start_" + tag, plan, len(names), [], halves)
        if sharing is not None:
            after = finish(sharing, after)
        sharing = (tag, names, plan, ss, rs, halves)
    after = finish(sharing, after)

    _, small = _split_wait("small_wait", small_plan, small_ss, small_rs, [], small, after)
    me = (4 * lax.axis_index("x") + 2 * lax.axis_index("y") + lax.axis_index("c")).astype(jnp.int32).reshape(1)
    red = _sum_small(small[0], me, "sum_small")
    grads.update({n: red[i:i + 1] for i, n in enumerate(VECS)})
    nv = len(VECS)
    grads["b_gate"] = jnp.concatenate([red[nv:nv + 1], red[nv + 1:nv + 2]], axis=1)
    chip = 2 * lax.axis_index("x") + lax.axis_index("y")
    grads["conv_w"] = lax.dynamic_slice_in_dim(red[nv + 2:nv + 2 + CONV_K], chip * cc, cc, axis=1)
    loss = red[nv + 2 + CONV_K, 0]
    out.update({n: update(n) for n in WEIGHTS if n not in KIND})
    return (loss, dx[None], *[out[n][0] for n in WEIGHTS], *[out[n][1] for n in WEIGHTS],
            *[out[n][2] for n in WEIGHTS], *[out[n][3] for n in WEIGHTS])


def kernel(x, mem, g_ffn1, w_ffn1_gu, w_ffn1_down, g_mix, w_in, b_gate, conv_w, w_conv_out, w_attn_out, w_o, g_cross, g_mem, w_cq, w_ckv, w_co, g_ffn2, w_ffn2_gu, w_ffn2_down, g_final, loss_target, m_g_ffn1, m_w_ffn1_gu, m_w_ffn1_down, m_g_mix, m_w_in, m_b_gate, m_conv_w, m_w_conv_out, m_w_attn_out, m_w_o, m_g_cross, m_g_mem, m_w_cq, m_w_ckv, m_w_co, m_g_ffn2, m_w_ffn2_gu, m_w_ffn2_down, m_g_final, v_g_ffn1, v_w_ffn1_gu, v_w_ffn1_down, v_g_mix, v_w_in, v_b_gate, v_conv_w, v_w_conv_out, v_w_attn_out, v_w_o, v_g_cross, v_g_mem, v_w_cq, v_w_ckv, v_w_co, v_g_ffn2, v_w_ffn2_gu, v_w_ffn2_down, v_g_final):
    given = dict(locals())
    wts = {n: given[n] for n in WEIGHTS}
    m_in = {n: given["m_" + n] for n in WEIGHTS}
    v_in = {n: given["v_" + n] for n in WEIGHTS}
    return _step(x, mem, loss_target, wts, m_in, v_in)
```

```python
import functools

import jax
import jax.numpy as jnp
from jax import lax
from jax.experimental import pallas as pl
from jax.experimental.pallas import tpu as pltpu

F32 = jnp.float32
BF16 = jnp.bfloat16
MESH = pl.DeviceIdType.MESH

V7X_VMEM_LIMIT_BYTES = 48 * 1024 * 1024
MM_VMEM_BUDGET_BYTES = 36 * 1024 * 1024
MM_WHOLE_K = 2816
LANES = 128
SB_HEAD_DIM = 128
X_HEADS = 4
CONV_K = 3
RMS_EPS = 1e-6
N_CHIPS = 4
N_DEV = 8
ADAM_LR, ADAM_B1, ADAM_B2, ADAM_EPS, ADAM_WD, ADAM_STEP = 0.001, 0.9, 0.999, 1e-08, 0.01, 10


ANY = pl.BlockSpec(memory_space=pl.ANY)


def _pcall(body, **kw):
    return pl.pallas_call(body, **kw)


def _params(*sem):
    return pltpu.CompilerParams(dimension_semantics=sem, vmem_limit_bytes=V7X_VMEM_LIMIT_BYTES)


def _pick(dim, cands):
    for c in cands:
        if dim % c == 0:
            return c
    return dim


def _dot(a, b, ca, cb):
    return lax.dot_general(a, b, (((ca,), (cb,)), ((), ())), preferred_element_type=F32)


def _mm(a, b, *, name, ta=False, tb=False, out_dtype=BF16, res=None, alpha=1.0, tm=None, tn=None, tk=None, after=None,
        a_halves=False, b_halves=False):
    assert not (a_halves and ta) and not (b_halves and tb)
    if a_halves:
        m, k = a.shape[1], 2 * a.shape[2]
    else:
        m, k = (a.shape[1], a.shape[0]) if ta else a.shape
    if b_halves:
        n = 2 * b.shape[2]
        assert k == b.shape[1]
    else:
        n = b.shape[0] if tb else b.shape[1]
        assert k == (b.shape[1] if tb else b.shape[0]), (a.shape, b.shape, ta, tb)
    if ta:
        tm = tm or _pick(m, (512, 256, 128))
        tn = tn or _pick(n, (1024, 512, 256, 128))
        tk = tk or (k if k <= MM_WHOLE_K else _pick(k, (1024, 512, 256, 128)))
    else:
        tk = tk or (k if k <= MM_WHOLE_K else _pick(k, (MM_WHOLE_K, 2048, 1024, 512, 256, 128)))
        tn = tn or _pick(n, (512, 1408, 256, 128) if tk == k else (1024, 512, 256, 128))
        per_row = 2 * (tk * a.dtype.itemsize + tn * (jnp.dtype(out_dtype).itemsize + (0 if res is None else res.dtype.itemsize)))
        per_row += 4 * tn if tk < k else 0
        rows = (MM_VMEM_BUDGET_BYTES - 2 * tk * tn * b.dtype.itemsize) // per_row
        tm = tm or next((c for c in (2048, 1024, 512, 256, 128) if m % c == 0 and c <= rows), m)
    if a_halves:
        tk = min(tk, k // 2) if (k // 2) % min(tk, k // 2) == 0 else _pick(k // 2, (1408, 1024, 512, 256, 128))
    if b_halves:
        tn = tn if (n // 2) % tn == 0 else _pick(n // 2, (1408, 1024, 512, 256, 128))
    nk = k // tk
    assert m % tm == 0 and n % tn == 0 and k % tk == 0
    a_spec = pl.BlockSpec((tk, tm), lambda i, j, kk: (kk, i)) if ta else pl.BlockSpec((tm, tk), lambda i, j, kk: (i, kk))
    b_spec = pl.BlockSpec((tn, tk), lambda i, j, kk: (j, kk)) if tb else pl.BlockSpec((tk, tn), lambda i, j, kk: (kk, j))
    if a_halves:
        per = (k // 2) // tk
        a_spec = pl.BlockSpec((None, tm, tk), lambda i, j, kk: (kk // per, i, kk % per))
    if b_halves:
        per_n = (n // 2) // tn
        b_spec = pl.BlockSpec((None, tk, tn), lambda i, j, kk: (j // per_n, kk, j % per_n))
    o_spec = pl.BlockSpec((tm, tn), lambda i, j, kk: (i, j))
    ca, cb = (0 if ta else 1), (1 if tb else 0)

    n_in = 2 + (res is not None) + (after is not None)

    def body(*refs):
        a_ref, b_ref = refs[:2]
        res_ref = refs[2] if res is not None else None
        o_ref = refs[n_in]
        scratch = refs[n_in + 1:]

        def finish(acc):
            val = acc if alpha == 1.0 else alpha * acc
            if res_ref is not None:
                val = res_ref[...].astype(F32) + val
            o_ref[...] = val.astype(o_ref.dtype)

        part = _dot(a_ref[...].astype(BF16), b_ref[...].astype(BF16), ca, cb)
        if nk == 1:
            finish(part)
        else:
            acc_ref = scratch[0]
            kk = pl.program_id(2)

            @pl.when(kk == 0)
            def _():
                acc_ref[...] = part

            @pl.when(kk > 0)
            def _():
                acc_ref[...] += part

            @pl.when(kk == nk - 1)
            def _():
                finish(acc_ref[...])

    ins = [a, b] + ([] if res is None else [res]) + ([] if after is None else [after])
    in_specs = [a_spec, b_spec] + ([] if res is None else [o_spec]) + ([] if after is None else [ANY])
    return _pcall(
        body, name=name, grid=(m // tm, n // tn, nk), in_specs=in_specs, out_specs=o_spec,
        out_shape=jax.ShapeDtypeStruct((m, n), out_dtype),
        scratch_shapes=[pltpu.VMEM((tm, tn), F32)] if nk > 1 else [],
        compiler_params=_params("parallel", "parallel", "arbitrary"),
    )(*ins)


def _rowcall(fn, rows, consts, outs, accs=(), *, tm, name, after=None):
    s = rows[0][0].shape[0]
    assert s % tm == 0
    n_read, n_out = len(rows) + len(consts), len(outs)
    n_in = n_read + (after is not None)

    def body(*refs):
        vals = fn(*[r[...] for r in refs[:n_read]])
        vals = vals if isinstance(vals, (tuple, list)) else (vals,)
        for o_ref, v in zip(refs[n_in:n_in + n_out], vals[:n_out]):
            o_ref[...] = v.astype(o_ref.dtype)
        if accs:
            first = pl.program_id(0) == 0
            for a_ref, v in zip(refs[n_in + n_out:], vals[n_out:]):
                tot = jnp.sum(v.astype(F32), axis=0, keepdims=True)

                @pl.when(first)
                def _(a_ref=a_ref, tot=tot):
                    a_ref[...] = tot

                @pl.when(jnp.logical_not(first))
                def _(a_ref=a_ref, tot=tot):
                    a_ref[...] += tot

    in_specs = [pl.BlockSpec((tm, w), lambda i, cb=cb: (i, cb)) for (_, cb, w) in rows]
    in_specs += [pl.BlockSpec(c.shape, lambda i: (0, 0)) for c in consts]
    in_specs += [] if after is None else [ANY]
    out_specs = [pl.BlockSpec((tm, w), lambda i: (i, 0)) for (w, _) in outs]
    out_specs += [pl.BlockSpec((1, w), lambda i: (0, 0)) for w in accs]
    out_shape = [jax.ShapeDtypeStruct((s, w), dt) for (w, dt) in outs]
    out_shape += [jax.ShapeDtypeStruct((1, w), F32) for w in accs]
    return _pcall(
        body, name=name, grid=(s // tm,), in_specs=in_specs, out_specs=out_specs, out_shape=out_shape,
        compiler_params=_params("arbitrary" if accs else "parallel"),
    )(*[r[0] for r in rows], *consts, *([] if after is None else [after]))


def _whole(a):
    return (a, 0, a.shape[1])


def _xhat(x):
    x = x.astype(F32)
    r = lax.rsqrt(jnp.mean(x * x, axis=-1, keepdims=True) + RMS_EPS)
    return x * r, r


def _rms_bwd(dy, x, g):
    xh, r = _xhat(x)
    dxh = dy.astype(F32) * g
    dx = r * (dxh - xh * jnp.mean(dxh * xh, axis=-1, keepdims=True))
    return dx, dy.astype(F32) * xh


def _sigmoid(x):
    return 1.0 / (1.0 + jnp.exp(-x))


def _rms_fwd(x, g, name, tm, after=None):
    d = x.shape[1]
    return _rowcall(lambda xb, gb: _xhat(xb)[0] * gb, [_whole(x)], [g], [(d, BF16)], tm=tm, name=name, after=after)[0]


def _silu_parts(gate):
    sg = _sigmoid(gate)
    return sg, gate * sg


def _ffn_up(n, w_gu, name):
    s, d = n.shape
    f = w_gu.shape[1] // 2
    tn = _pick(f, (1408, 1024, 512, 256, 128))
    tm = _pick(s, (1024, 512, 256, 128))
    nb = f // tn

    def body(n_ref, wg_ref, wu_ref, gu_ref, act_ref):
        nv = n_ref[...]
        gate = _dot(nv, wg_ref[...], 1, 0)
        up = _dot(nv, wu_ref[...], 1, 0)
        gu_ref[0] = gate.astype(gu_ref.dtype)
        gu_ref[1] = up.astype(gu_ref.dtype)
        act_ref[...] = (_silu_parts(gate)[1] * up).astype(act_ref.dtype)

    return _pcall(
        body, name=name, grid=(s // tm, nb),
        in_specs=[pl.BlockSpec((tm, d), lambda i, j: (i, 0)), pl.BlockSpec((d, tn), lambda i, j: (0, j)),
                  pl.BlockSpec((d, tn), lambda i, j: (0, nb + j))],
        out_specs=[pl.BlockSpec((2, tm, tn), lambda i, j: (0, i, j)), pl.BlockSpec((tm, tn), lambda i, j: (i, j))],
        out_shape=[jax.ShapeDtypeStruct((2, s, f), BF16), jax.ShapeDtypeStruct((s, f), BF16)],
        compiler_params=_params("parallel", "parallel"),
    )(n, w_gu, w_gu)


def _ffn_dgu(dhb, w_down, gu, name):
    s, d = dhb.shape
    f = w_down.shape[0]
    tn = _pick(f, (1408, 1024, 512, 256, 128))
    tm = _pick(s, (1024, 512, 256, 128))

    def body(dh_ref, w_ref, gu_ref, o_ref):
        dact = _dot(dh_ref[...], w_ref[...], 1, 1)
        gate, up = gu_ref[0].astype(F32), gu_ref[1].astype(F32)
        sg, silu = _silu_parts(gate)
        o_ref[0] = (dact * up * (sg + silu * (1.0 - sg))).astype(o_ref.dtype)
        o_ref[1] = (dact * silu).astype(o_ref.dtype)

    blk = pl.BlockSpec((2, tm, tn), lambda i, j: (0, i, j))
    return _pcall(
        body, name=name, grid=(s // tm, f // tn),
        in_specs=[pl.BlockSpec((tm, d), lambda i, j: (i, 0)), pl.BlockSpec((tn, d), lambda i, j: (j, 0)), blk],
        out_specs=blk, out_shape=jax.ShapeDtypeStruct((2, s, f), BF16), compiler_params=_params("parallel", "parallel"),
    )(dhb, w_down, gu)


def _dgrad_norm(dy, wmat, dh, x, g, name, *, dy_halves=False, copy_scale=None, after=None):
    s, d = dh.shape
    k = wmat.shape[1]
    tk = k if k <= MM_WHOLE_K else _pick(k, (MM_WHOLE_K, 2048, 1024, 512, 256, 128))
    if dy_halves and (k // 2) % tk:
        tk = _pick(k // 2, (1408, 1024, 512, 256, 128))
    tm = _pick(s, (512, 256, 128))
    nk, per = k // tk, (k // 2) // tk if dy_halves else 0
    n_in = 5 + (after is not None)
    n_out = 2 + (copy_scale is not None)

    def body(*refs):
        dy_ref, w_ref, dh_ref, x_ref, g_ref = refs[:5]
        outs, scratch = refs[n_in:n_in + n_out], refs[n_in + n_out:]
        i, kk = pl.program_id(0), pl.program_id(1)
        part = _dot(dy_ref[...], w_ref[...], 1, 1)

        def finish(dn):
            dx, dg = _rms_bwd(dn, x_ref[...], g_ref[...])
            tot = dh_ref[...] + dx
            outs[0][...] = tot
            if copy_scale is not None:
                outs[1][...] = (copy_scale * tot).astype(outs[1].dtype)
            dg = jnp.sum(dg, axis=0, keepdims=True)

            @pl.when(i == 0)
            def _():
                outs[-1][...] = dg

            @pl.when(i > 0)
            def _():
                outs[-1][...] += dg

        if nk == 1:
            finish(part)
        else:
            acc_ref = scratch[0]

            @pl.when(kk == 0)
            def _():
                acc_ref[...] = part

            @pl.when(kk > 0)
            def _():
                acc_ref[...] += part

            @pl.when(kk == nk - 1)
            def _():
                finish(acc_ref[...])

    row = pl.BlockSpec((tm, d), lambda i, kk: (i, 0))
    dy_spec = pl.BlockSpec((None, tm, tk), lambda i, kk: (kk // per, i, kk % per)) if dy_halves else pl.BlockSpec((tm, tk), lambda i, kk: (i, kk))
    in_specs = [dy_spec, pl.BlockSpec((d, tk), lambda i, kk: (0, kk)), row, row, pl.BlockSpec((1, d), lambda i, kk: (0, 0))]
    out_specs = [row] * (n_out - 1) + [pl.BlockSpec((1, d), lambda i, kk: (0, 0))]
    out_shape = [jax.ShapeDtypeStruct((s, d), F32)] + ([] if copy_scale is None else [jax.ShapeDtypeStruct((s, d), BF16)])
    return _pcall(
        body, name=name, grid=(s // tm, nk), in_specs=in_specs + ([] if after is None else [ANY]), out_specs=out_specs,
        out_shape=out_shape + [jax.ShapeDtypeStruct((1, d), F32)], scratch_shapes=[pltpu.VMEM((tm, d), F32)] if nk > 1 else [],
        compiler_params=_params("arbitrary", "arbitrary"),
    )(dy, wmat, dh, x, g, *([] if after is None else [after]))


def _shift_down(p, k):
    if k == 0:
        return p
    rows = lax.broadcasted_iota(jnp.int32, p.shape, 0)
    return jnp.where(rows >= k, pltpu.roll(p, k, 0), 0.0)


def _shift_up(p, k):
    if k == 0:
        return p
    s = p.shape[0]
    rows = lax.broadcasted_iota(jnp.int32, p.shape, 0)
    return jnp.where(rows < s - k, pltpu.roll(p, s - k, 0), 0.0)


def _conv_fwd(proj, conv_w, d, tc, name):
    s = proj.shape[0]
    nb = d // tc

    def body(cb_ref, cc_ref, cx_ref, w_ref, y_ref):
        p = cc_ref[...].astype(F32) * cx_ref[...].astype(F32)
        w = w_ref[...]
        acc = p * w[CONV_K - 1:CONV_K, :]
        for k in range(1, CONV_K):
            acc = acc + _shift_down(p, k) * w[CONV_K - 1 - k:CONV_K - k, :]
        y_ref[...] = (cb_ref[...].astype(F32) * acc).astype(y_ref.dtype)

    col = lambda off: pl.BlockSpec((s, tc), lambda j: (0, off * nb + j))
    return _pcall(
        body, name=name, grid=(nb,), in_specs=[col(0), col(1), col(2), pl.BlockSpec((CONV_K, tc), lambda j: (0, j))],
        out_specs=pl.BlockSpec((s, tc), lambda j: (0, j)), out_shape=jax.ShapeDtypeStruct((s, d), BF16),
        compiler_params=_params("parallel"),
    )(proj, proj, proj, conv_w)


def _conv_bwd(dy, proj, conv_w, d, tc, name):
    s = proj.shape[0]
    nb = d // tc

    def body(dy_ref, cb_ref, cc_ref, cx_ref, w_ref, dcb_ref, dcc_ref, dcx_ref, dw_ref):
        cc, cx = cc_ref[...].astype(F32), cx_ref[...].astype(F32)
        p = cc * cx
        w = w_ref[...]
        dyv = dy_ref[...].astype(F32)
        shifted = [_shift_down(p, CONV_K - 1 - k) for k in range(CONV_K)]
        conv = shifted[0] * w[0:1, :]
        for k in range(1, CONV_K):
            conv = conv + shifted[k] * w[k:k + 1, :]
        dcb_ref[...] = (dyv * conv).astype(dcb_ref.dtype)
        ds = dyv * cb_ref[...].astype(F32)
        dp = ds * w[CONV_K - 1:CONV_K, :]
        for k in range(1, CONV_K):
            dp = dp + _shift_up(ds, k) * w[CONV_K - 1 - k:CONV_K - k, :]
        dcc_ref[...] = (dp * cx).astype(dcc_ref.dtype)
        dcx_ref[...] = (dp * cc).astype(dcx_ref.dtype)
        for k in range(CONV_K):
            dw_ref[k:k + 1, :] = jnp.sum(ds * shifted[k], axis=0, keepdims=True)

    col = lambda off: pl.BlockSpec((s, tc), lambda j: (0, off * nb + j))
    blk = pl.BlockSpec((s, tc), lambda j: (0, j))
    wblk = pl.BlockSpec((CONV_K, tc), lambda j: (0, j))
    act = jax.ShapeDtypeStruct((s, d), BF16)
    return _pcall(
        body, name=name, grid=(nb,), in_specs=[blk, col(0), col(1), col(2), wblk],
        out_specs=[blk, blk, blk, wblk], out_shape=[act, act, act, jax.ShapeDtypeStruct((CONV_K, d), F32)],
        compiler_params=_params("parallel"),
    )(dy, proj, proj, proj, conv_w)


def _sb_tile(q, kj, scale, carry, tri, mask):
    z = _dot(q, kj, 1, 1) * scale
    lsz = jnp.minimum(z, 0.0) - jnp.log(1.0 + jnp.exp(-jnp.abs(z)))
    l1m = lsz - z
    if mask is not None:
        l1m = jnp.where(mask, l1m, 0.0)
    l1b = l1m.astype(BF16)
    a = jnp.exp(lsz + (carry + _dot(l1b, tri, 1, 0)))
    if mask is not None:
        a = jnp.where(mask, a, 0.0)
    return lsz, l1b, a.astype(BF16)


def _sb_masks(tq, tk):
    row = lax.broadcasted_iota(jnp.int32, (tq, tk), 0)
    col = lax.broadcasted_iota(jnp.int32, (tq, tk), 1)
    masks = [col + dj * tk < row for dj in range(tq // tk)]
    r2 = lax.broadcasted_iota(jnp.int32, (tk, tk), 0)
    c2 = lax.broadcasted_iota(jnp.int32, (tk, tk), 1)
    return masks, (r2 > c2).astype(BF16), (r2 < c2).astype(BF16)


def _sb_fwd(proj, heads, col0, tq, tk, name):
    s = proj.shape[0]
    dh = SB_HEAD_DIM
    nq, nd = s // tq, tq // tk
    scale = dh ** -0.5

    def body(q_ref, k_ref, v_ref, o_ref):
        i = pl.program_id(1)
        q = q_ref[...]
        masks, tri_right, _ = _sb_masks(tq, tk)

        def tile(j, carry, acc, mask):
            start = pl.multiple_of(j * tk, tk)
            kj = k_ref[pl.ds(start, tk), :]
            vj = v_ref[pl.ds(start, tk), :]
            _, l1b, ab = _sb_tile(q, kj, scale, carry, tri_right, mask)
            return carry + jnp.sum(l1b.astype(F32), axis=1, keepdims=True), acc + _dot(ab, vj, 1, 0)

        state = (jnp.zeros((tq, 1), F32), jnp.zeros((tq, dh), F32))
        for dj in reversed(range(nd)):
            state = tile(i * nd + dj, *state, masks[dj])
        state = lax.fori_loop(0, i * nd, lambda t, st: tile(i * nd - 1 - t, st[0], st[1], None), state)
        o_ref[...] = state[1]

    qspec = pl.BlockSpec((tq, dh), lambda h, i: (i, col0[0] + h))
    kspec = pl.BlockSpec((s, dh), lambda h, i: (0, col0[1] + h))
    vspec = pl.BlockSpec((s, dh), lambda h, i: (0, col0[2] + h))
    return _pcall(
        body, name=name, grid=(heads, nq), in_specs=[qspec, kspec, vspec],
        out_specs=pl.BlockSpec((tq, dh), lambda h, i: (i, h)), out_shape=jax.ShapeDtypeStruct((s, heads * dh), F32),
        compiler_params=_params("parallel", "parallel"),
    )(proj, proj, proj)


def _sb_bwd(proj, o, do, heads, col0, tq, tk, name):
    s = proj.shape[0]
    dh = SB_HEAD_DIM
    nq, nd = s // tq, tq // tk
    scale = dh ** -0.5

    def body(q_ref, k_ref, v_ref, o_ref, do_ref, dq_ref, dk_ref, dv_ref, dk_acc, dv_acc):
        i = pl.program_id(1)

        @pl.when(i == 0)
        def _():
            dk_acc[...] = jnp.zeros_like(dk_acc)
            dv_acc[...] = jnp.zeros_like(dv_acc)

        q = q_ref[...]
        dob = do_ref[...].astype(BF16)
        delta = jnp.sum(dob.astype(F32) * o_ref[...], axis=1, keepdims=True)
        masks, tri_right, tri_left = _sb_masks(tq, tk)

        def tile(j, carry_l, carry_g, dq, mask):
            start = pl.multiple_of(j * tk, tk)
            kj = k_ref[pl.ds(start, tk), :]
            vj = v_ref[pl.ds(start, tk), :]
            lsz, l1b, ab = _sb_tile(q, kj, scale, carry_l, tri_right, mask)
            g = _dot(dob, vj, 1, 1) * ab.astype(F32)
            carry_g = carry_g + jnp.sum(g, axis=1, keepdims=True)
            left = (delta - carry_g) + _dot(g.astype(BF16), tri_left, 1, 0)
            dz = g - jnp.exp(lsz) * (g + left)
            if mask is not None:
                dz = jnp.where(mask, dz, 0.0)
            dzb = dz.astype(BF16)
            dk_acc[pl.ds(start, tk), :] += _dot(dzb, q, 0, 0)
            dv_acc[pl.ds(start, tk), :] += _dot(ab, dob, 0, 0)
            return carry_l + jnp.sum(l1b.astype(F32), axis=1, keepdims=True), carry_g, dq + _dot(dzb, kj, 1, 0)

        zero = jnp.zeros((tq, 1), F32)
        state = (zero, zero, jnp.zeros((tq, dh), F32))
        for dj in reversed(range(nd)):
            state = tile(i * nd + dj, *state, masks[dj])
        state = lax.fori_loop(0, i * nd, lambda t, st: tile(i * nd - 1 - t, st[0], st[1], st[2], None), state)
        dq_ref[...] = (state[2] * scale).astype(dq_ref.dtype)

        @pl.when(i == nq - 1)
        def _():
            dk_ref[...] = (dk_acc[...] * scale).astype(dk_ref.dtype)
            dv_ref[...] = dv_acc[...].astype(dv_ref.dtype)

    qspec = pl.BlockSpec((tq, dh), lambda h, i: (i, col0[0] + h))
    kspec = pl.BlockSpec((s, dh), lambda h, i: (0, col0[1] + h))
    vspec = pl.BlockSpec((s, dh), lambda h, i: (0, col0[2] + h))
    blk = pl.BlockSpec((tq, dh), lambda h, i: (i, h))
    full = pl.BlockSpec((s, dh), lambda h, i: (0, h))
    act = jax.ShapeDtypeStruct((s, heads * dh), BF16)
    return _pcall(
        body, name=name, grid=(heads, nq), in_specs=[qspec, kspec, vspec, blk, blk],
        out_specs=[blk, full, full], out_shape=[act, act, act],
        scratch_shapes=[pltpu.VMEM((s, dh), F32), pltpu.VMEM((s, dh), F32)],
        compiler_params=_params("parallel", "arbitrary"),
    )(proj, proj, proj, o, do)


def _xattn_probs(q, k, scale):
    sc = _dot(q, k, 1, 1) * scale
    e = jnp.exp(sc - jnp.max(sc, axis=1, keepdims=True))
    return e / jnp.sum(e, axis=1, keepdims=True)


def _xattn_fwd(qc, kv, tq, name):
    s, d = qc.shape
    m = kv.shape[0]
    dh = d // X_HEADS
    scale = dh ** -0.5

    def body(q_ref, k_ref, v_ref, o_ref):
        p = _xattn_probs(q_ref[...], k_ref[...], scale)
        o_ref[...] = _dot(p.astype(BF16), v_ref[...], 1, 0).astype(o_ref.dtype)

    blk = pl.BlockSpec((tq, dh), lambda h, i: (i, h))
    return _pcall(
        body, name=name, grid=(X_HEADS, s // tq),
        in_specs=[blk, pl.BlockSpec((m, dh), lambda h, i: (0, h)), pl.BlockSpec((m, dh), lambda h, i: (0, X_HEADS + h))],
        out_specs=blk, out_shape=jax.ShapeDtypeStruct((s, d), BF16), compiler_params=_params("parallel", "parallel"),
    )(qc, kv, kv)


def _xattn_bwd(qc, kv, do, tq, name):
    s, d = qc.shape
    m = kv.shape[0]
    dh = d // X_HEADS
    scale = dh ** -0.5
    nq = s // tq

    def body(q_ref, k_ref, v_ref, do_ref, dq_ref, dk_ref, dv_ref, dk_acc, dv_acc):
        i = pl.program_id(1)
        q, k, v = q_ref[...], k_ref[...], v_ref[...]
        dob = do_ref[...].astype(BF16)
        p = _xattn_probs(q, k, scale)
        pb = p.astype(BF16)
        dp = _dot(dob, v, 1, 1)
        ds = pb.astype(F32) * (dp - jnp.sum(dp * pb.astype(F32), axis=1, keepdims=True))
        dsb = (ds * scale).astype(BF16)
        dq_ref[...] = _dot(dsb, k, 1, 0).astype(dq_ref.dtype)
        dk_part = _dot(dsb, q, 0, 0)
        dv_part = _dot(pb, dob, 0, 0)

        @pl.when(i == 0)
        def _():
            dk_acc[...] = dk_part
            dv_acc[...] = dv_part

        @pl.when(i > 0)
        def _():
            dk_acc[...] += dk_part
            dv_acc[...] += dv_part

        @pl.when(i == nq - 1)
        def _():
            dk_ref[...] = dk_acc[...].astype(dk_ref.dtype)
            dv_ref[...] = dv_acc[...].astype(dv_ref.dtype)

    blk = pl.BlockSpec((tq, dh), lambda h, i: (i, h))
    kblk = pl.BlockSpec((m, dh), lambda h, i: (0, h))
    return _pcall(
        body, name=name, grid=(X_HEADS, nq),
        in_specs=[blk, kblk, pl.BlockSpec((m, dh), lambda h, i: (0, X_HEADS + h)), blk],
        out_specs=[blk, kblk, kblk],
        out_shape=[jax.ShapeDtypeStruct((s, d), BF16), jax.ShapeDtypeStruct((m, d), BF16), jax.ShapeDtypeStruct((m, d), BF16)],
        scratch_shapes=[pltpu.VMEM((m, dh), F32), pltpu.VMEM((m, dh), F32)],
        compiler_params=_params("parallel", "arbitrary"),
    )(qc, kv, kv, do)


def _local_step(x, mem, tgt, w, fetch=None, prefetch=None, emit=None, tick=None, after=None):
    fetch = fetch or (lambda name, after: {})
    prefetch = prefetch or (lambda name, after: None)
    emit = emit or (lambda group, g: None)
    tick = tick or (lambda group, after: None)
    w = dict(w)
    s, d = x.shape
    heads = d // SB_HEAD_DIM
    tm = _pick(s, (512, 256, 128))
    tq = _pick(s, (256, 128))
    sb_tq, sb_tk = _pick(s, (512, 256, 128)), _pick(s, (256, 128))
    tc = _pick(d, (256, 128))
    g = {}

    def wt(name, after):
        if name not in w:
            w.update(fetch(name, after))
        return w[name]

    def ffn_fwd(h, gname, wgu, wdown, tag, after=None):
        n = _rms_fwd(h, w[gname], tag + "_norm", tm, after=after)
        gu, act = _ffn_up(n, wt(wgu, n), tag + "_gu")
        prefetch(wdown, gu)
        return n, gu, act, _mm(act, wt(wdown, act), name=tag + "_down", out_dtype=F32, res=h, alpha=0.5)

    def ffn_bwd(dh, dhb, h, saved, gname, wgu, wdown, tag, copy_scale=None, after=None):
        n, gu, act = saved
        g[wdown] = _mm(act, dhb, ta=True, name=tag + "_dwdown", after=after)
        dgu = _ffn_dgu(dhb, w[wdown], gu, tag + "_dgu")
        g[wgu] = _mm(n, dgu, ta=True, b_halves=True, name=tag + "_dwgu")
        *dh_in, g[gname] = _dgrad_norm(dgu, w[wgu], dh, h, w[gname], tag + "_dn", dy_halves=True, copy_scale=copy_scale,
                                       after=emit(tag, g))
        return dh_in, tick(tag, dh_in[0])

    n1, gu1, act1, h1 = ffn_fwd(x, "g_ffn1", "w_ffn1_gu", "w_ffn1_down", "ffn1", after)
    prefetch("w_in", h1)
    u = _rms_fwd(h1, w["g_mix"], "mix_norm", tm)
    proj = _mm(u, wt("w_in", u), name="mix_in")
    prefetch("w_conv_out", proj)
    nd = d // SB_HEAD_DIM
    y_conv = _conv_fwd(proj, w["conv_w"], d, tc, "conv_fwd")
    sb_cols = (3 * nd, 4 * nd, 5 * nd)
    y_sb = _sb_fwd(proj, heads, sb_cols, sb_tq, sb_tk, "sb_fwd")
    prefetch("w_cq", y_sb)
    a_conv = _mm(y_conv, wt("w_conv_out", y_conv), name="conv_out")
    a_sb = _mm(y_sb, wt("w_attn_out", y_sb), name="attn_out")
    b_conv, b_sb = w["b_gate"][:, :d], w["b_gate"][:, d:]

    def merge(ac, asb, gcp, gsp, bc, bs):
        gc = _sigmoid(gcp.astype(F32) + bc)
        gs = _sigmoid(gsp.astype(F32) + bs)
        return gc * ac.astype(F32) + gs * asb.astype(F32)

    merged = _rowcall(merge, [_whole(a_conv), _whole(a_sb), (proj, 6, d), (proj, 7, d)], [b_conv, b_sb], [(d, BF16)],
                      tm=tm, name="merge")[0]
    prefetch("w_ffn2_gu", merged)
    h2 = _mm(merged, wt("w_o", merged), name="mix_out", out_dtype=F32, res=h1)
    hn = _rms_fwd(h2, w["g_cross"], "cross_norm", tm)
    mn = _rms_fwd(mem, w["g_mem"], "mem_norm", _pick(mem.shape[0], (256, 128)))
    qc = _mm(hn, wt("w_cq", hn), name="cross_q")
    kv = _mm(mn, wt("w_ckv", mn), name="cross_kv")
    oc = _xattn_fwd(qc, kv, tq, "xattn_fwd")
    h3 = _mm(oc, wt("w_co", oc), name="cross_out", out_dtype=F32, res=h2)
    n2, gu2, act2, h4 = ffn_fwd(h3, "g_ffn2", "w_ffn2_gu", "w_ffn2_down", "ffn2")

    def head(hb, tb, gb):
        xh, r = _xhat(hb)
        err = xh * gb - tb
        dy = err * (1.0 / d)
        dxh = dy * gb
        dx = r * (dxh - xh * jnp.mean(dxh * xh, axis=-1, keepdims=True))
        row_loss = 0.5 * jnp.mean(err * err, axis=-1, keepdims=True)
        return dx, 0.5 * dx, dy * xh, jnp.broadcast_to(row_loss, (row_loss.shape[0], LANES))

    dh4, dh4b, g["g_final"], loss_lanes = _rowcall(head, [_whole(h4), _whole(tgt)], [w["g_final"]], [(d, F32), (d, BF16)],
                                                   [d, LANES], tm=tm, name="loss_head")

    (dh3, dh3b), tok = ffn_bwd(dh4, dh4b, h3, (n2, gu2, act2), "g_ffn2", "w_ffn2_gu", "w_ffn2_down", "ffn2", copy_scale=1.0)
    g["w_co"] = _mm(oc, dh3b, ta=True, name="cross_dwco", after=tok)
    doc = _mm(dh3b, w["w_co"], tb=True, name="cross_doc")
    dqc, dk, dv = _xattn_bwd(qc, kv, doc, tq, "xattn_bwd")
    dkv = jnp.concatenate([dk, dv], axis=1)
    g["w_cq"] = _mm(hn, dqc, ta=True, name="cross_dwcq")
    g["w_ckv"] = _mm(mn, dkv, ta=True, name="cross_dwckv")
    dmn = _mm(dkv, w["w_ckv"], tb=True, name="cross_dmn", out_dtype=F32)
    g["g_mem"] = _rowcall(lambda dy, xb: dy * _xhat(xb)[0], [_whole(dmn), _whole(mem)], [], [], [d],
                          tm=_pick(mem.shape[0], (256, 128)), name="mem_dnorm")[0]
    dh2, dh2b, g["g_cross"] = _dgrad_norm(dqc, w["w_cq"], dh3, h2, w["g_cross"], "cross_dhn", copy_scale=1.0, after=emit("cross", g))

    g["w_o"] = _mm(merged, dh2b, ta=True, name="mix_dwo", after=tick("cross", dh2))
    dmerged = _mm(dh2b, w["w_o"], tb=True, name="mix_dmerged")

    def merge_bwd(dm, ac, asb, gcp, gsp, bc, bs):
        dm, ac, asb = dm.astype(F32), ac.astype(F32), asb.astype(F32)
        gc = _sigmoid(gcp.astype(F32) + bc)
        gs = _sigmoid(gsp.astype(F32) + bs)
        dgc = dm * ac * gc * (1.0 - gc)
        dgs = dm * asb * gs * (1.0 - gs)
        return dm * gc, dm * gs, dgc, dgs, dgc, dgs

    da_conv, da_sb, dgc, dgs, db_conv, db_sb = _rowcall(
        merge_bwd, [_whole(dmerged), _whole(a_conv), _whole(a_sb), (proj, 6, d), (proj, 7, d)], [b_conv, b_sb],
        [(d, BF16)] * 4, [d, d], tm=tm, name="merge_bwd")
    g["b_gate"] = jnp.concatenate([db_conv, db_sb], axis=1)
    g["w_conv_out"] = _mm(y_conv, da_conv, ta=True, name="conv_dwout")
    g["w_attn_out"] = _mm(y_sb, da_sb, ta=True, name="attn_dwout")
    dy_conv = _mm(da_conv, w["w_conv_out"], tb=True, name="conv_dy")
    dy_sb = _mm(da_sb, w["w_attn_out"], tb=True, name="attn_dy")
    dcb, dcc, dcx, g["conv_w"] = _conv_bwd(dy_conv, proj, w["conv_w"], d, tc, "conv_bwd")
    dq, dk_sb, dv_sb = _sb_bwd(proj, y_sb, dy_sb, heads, sb_cols, sb_tq, sb_tk, "sb_bwd")
    dproj = jnp.concatenate([dcb, dcc, dcx, dq, dk_sb, dv_sb, dgc, dgs], axis=1)
    g["w_in"] = _mm(u, dproj, ta=True, name="mix_dwin")
    dh1, dh1b, g["g_mix"] = _dgrad_norm(dproj, w["w_in"], dh2, h1, w["g_mix"], "mix_du", copy_scale=0.5, after=emit("mix", g))
    (dx,), tok = ffn_bwd(dh1, dh1b, x, (n1, gu1, act1), "g_ffn1", "w_ffn1_gu", "w_ffn1_down", "ffn1", after=tick("mix", dh1))
    return loss_lanes, dx, g, tok


MATS = (("w_ffn1_gu", "col"), ("w_ffn1_down", "row"), ("w_in", "col"), ("w_conv_out", "row"), ("w_attn_out", "row"),
        ("w_o", "row"), ("w_cq", "row"), ("w_ckv", "col"), ("w_co", "row"), ("w_ffn2_gu", "col"), ("w_ffn2_down", "row"))
VECS = ("g_ffn1", "g_mix", "g_cross", "g_mem", "g_ffn2", "g_final")
WEIGHTS = ("g_ffn1", "w_ffn1_gu", "w_ffn1_down", "g_mix", "w_in", "b_gate", "conv_w", "w_conv_out", "w_attn_out", "w_o",
           "g_cross", "g_mem", "w_cq", "w_ckv", "w_co", "g_ffn2", "w_ffn2_gu", "w_ffn2_down", "g_final")
CONV_ROWS = 8


def _full_shape(kind, r, c):
    return (r, N_CHIPS * c) if kind == "col" else (N_CHIPS * r, c)


def _piece(ref, kind, r, c, chip, half):
    hr = r // 2
    if kind == "col":
        return ref.at[pl.ds(pl.multiple_of(half * hr, 16), hr), pl.ds(pl.multiple_of(chip * c, LANES), c)]
    return ref.at[pl.ds(pl.multiple_of(chip * r + half * hr, 16), hr), :]


def _shard_of(ref, kind, r, c, chip):
    if kind == "col":
        return ref.at[:, pl.ds(pl.multiple_of(chip * c, LANES), c)]
    return ref.at[pl.ds(pl.multiple_of(chip * r, 16), r), :]


def _place():
    x, y, c = lax.axis_index("x"), lax.axis_index("y"), lax.axis_index("c")
    others = [(1 - x, y), (x, 1 - y), (1 - x, 1 - y)]
    return x, y, c, 2 * x + y, others


def _remote(src, dst, send_sem, recv_sem, to):
    return pltpu.make_async_remote_copy(src_ref=src, dst_ref=dst, send_sem=send_sem, recv_sem=recv_sem,
                                        device_id=to, device_id_type=MESH)


def _gather_conv(conv_shard):
    cc = conv_shard.shape[1]

    def body(conv_ref, conv_full, cs, cr, cl):
        x, y, c, me, others = _place()

        def cols(chip):
            return conv_full.at[:, pl.ds(pl.multiple_of(chip * cc, LANES), cc)]

        def conv(k, chip_from, to):
            return _remote(conv_ref, cols(chip_from), cs.at[k], cr.at[k], to)

        mine = pltpu.make_async_copy(conv_ref, cols(me), cl.at[0])
        mine.start()
        for k, (ox, oy) in enumerate(others):
            conv(k, me, (ox, oy, c)).start()
        for k, (ox, oy) in enumerate(others):
            conv(k, 2 * ox + oy, (x, y, c)).wait_recv()
            conv(k, me, (ox, oy, c)).wait_send()
        mine.wait()

    dma = pltpu.SemaphoreType.DMA
    return _pcall(
        body, name="gather_conv", in_specs=[ANY], out_specs=ANY,
        out_shape=jax.ShapeDtypeStruct((CONV_ROWS, N_CHIPS * cc), F32), scratch_shapes=[dma((3,)), dma((3,)), dma((1,))],
    )(conv_shard)


HBM = pl.BlockSpec(memory_space=pltpu.HBM)
SEM = pl.BlockSpec(memory_space=pltpu.SEMAPHORE)
EFFECT = pltpu.SideEffectType.DATAFLOW_SIDE_EFFECTING
TOKEN = (8, LANES)


def _split_start(name, plan, n_copies, srcs, lands, after=None):
    ns, nl = len(srcs), len(lands)
    n_in = ns + nl + (after is not None)

    def body(*refs):
        outs = refs[n_in:]
        sends, _ = plan(refs[:ns], refs[ns:ns + nl], outs[0], outs[1])
        for cp in sends:
            cp.start()
        outs[-1][...] = jnp.zeros(TOKEN, F32)

    held = [pltpu.HBM(a.shape, a.dtype) for a in (*srcs, *lands)]
    dma = pltpu.SemaphoreType.DMA((n_copies,))
    ins = [pltpu.with_memory_space_constraint(a, pltpu.HBM) for a in (*srcs, *lands)]
    outs = _pcall(
        body, name=name, in_specs=[HBM] * (ns + nl) + ([] if after is None else [ANY]),
        out_specs=(SEM, SEM, *[HBM] * (ns + nl), pl.BlockSpec(memory_space=pltpu.VMEM)),
        out_shape=(dma, dma, *held, jax.ShapeDtypeStruct(TOKEN, F32)),
        input_output_aliases={i: 2 + i for i in range(ns + nl)},
        compiler_params=pltpu.CompilerParams(has_side_effects=EFFECT),
    )(*ins, *([] if after is None else [after]))
    return outs[0], outs[1], list(outs[2:2 + ns]), list(outs[2 + ns:2 + ns + nl]), outs[-1]


def _split_wait(name, plan, send_sems, recv_sems, srcs, lands, after):
    ns, nl = len(srcs), len(lands)

    def body(*refs):
        sends, recvs = plan(refs[:ns], refs[ns:ns + nl], refs[ns + nl], refs[ns + nl + 1])
        for cp in sends:
            cp.wait_send()
        for cp in recvs:
            cp.wait_recv()

    outs = _pcall(
        body, name=name, in_specs=[HBM] * (ns + nl) + [SEM, SEM, ANY], out_specs=[HBM] * (ns + nl),
        out_shape=[pltpu.HBM(a.shape, a.dtype) for a in (*srcs, *lands)],
        input_output_aliases={i: i for i in range(ns + nl)},
        compiler_params=pltpu.CompilerParams(has_side_effects=EFFECT),
    )(*srcs, *lands, send_sems, recv_sems, after)
    return list(outs[:ns]), list(outs[ns:])


def _gather_plan(dims):
    def plan(shard_refs, full_refs, ss, rs):
        x, y, c, me, others = _place()
        sends, recvs = [], []
        for wi, (kind, r, cw) in enumerate(dims):
            half = shard_refs[wi].at[pl.ds(pl.multiple_of(c * (r // 2), 16), r // 2), :]
            for k, (ox, oy) in enumerate(others):
                sem = 4 * wi + k
                sends.append(_remote(half, _piece(full_refs[wi], kind, r, cw, me, c), ss.at[sem], rs.at[sem], (ox, oy, c)))
                recvs.append(_remote(half, _piece(full_refs[wi], kind, r, cw, 2 * ox + oy, c), ss.at[sem], rs.at[sem], (x, y, c)))
            sem = 4 * wi + 3
            own = _remote(shard_refs[wi], _shard_of(full_refs[wi], kind, r, cw, me), ss.at[sem], rs.at[sem], (x, y, 1 - c))
            sends.append(own)
            recvs.append(own)
        return sends, recvs

    return plan


def _forward_plan(dims):
    def plan(_, full_refs, ss, rs):
        x, y, c, _, others = _place()
        sends, recvs = [], []
        for wi, (kind, r, cw) in enumerate(dims):
            for k, (ox, oy) in enumerate(others):
                sem = 3 * wi + k
                mine = _piece(full_refs[wi], kind, r, cw, 2 * ox + oy, c)
                theirs = _piece(full_refs[wi], kind, r, cw, 2 * ox + oy, 1 - c)
                sends.append(_remote(mine, mine, ss.at[sem], rs.at[sem], (x, y, 1 - c)))
                recvs.append(_remote(theirs, theirs, ss.at[sem], rs.at[sem], (x, y, 1 - c)))
        return sends, recvs

    return plan


def _rs_cores_plan(dims):
    def plan(g_refs, land_refs, ss, rs):
        x, y, c, _, _ = _place()
        sends, recvs = [], []
        for wi, dm in enumerate(dims):
            for chip in range(N_CHIPS):
                sem = N_CHIPS * wi + chip
                sends.append(_remote(_piece(g_refs[wi], *dm, chip, 1 - c), land_refs[wi].at[chip], ss.at[sem], rs.at[sem], (x, y, 1 - c)))
                recvs.append(_remote(_piece(g_refs[wi], *dm, chip, c), land_refs[wi].at[chip], ss.at[sem], rs.at[sem], (x, y, 1 - c)))
        return sends, recvs

    return plan


def _share_plan(nw):
    def plan(_, buf_refs, ss, rs):
        x, y, c, _, _ = _place()
        sends = [_remote(buf_refs[wi].at[c], buf_refs[wi].at[c], ss.at[wi], rs.at[wi], (x, y, 1 - c)) for wi in range(nw)]
        recvs = [_remote(buf_refs[wi].at[1 - c], buf_refs[wi].at[1 - c], ss.at[wi], rs.at[wi], (x, y, 1 - c)) for wi in range(nw)]
        return sends, recvs

    return plan


def _small_plan():
    def plan(_, buf_refs, ss, rs):
        x, y, c = lax.axis_index("x"), lax.axis_index("y"), lax.axis_index("c")
        buf = buf_refs[0]
        sends, recvs = [], []
        for rel in range(1, N_DEV):
            peer = (x ^ (rel >> 2 & 1), y ^ (rel >> 1 & 1), c ^ (rel & 1))
            sends.append(_remote(buf.at[0], buf.at[rel], ss.at[rel - 1], rs.at[rel - 1], peer))
            recvs.append(_remote(buf.at[0], buf.at[rel], ss.at[rel - 1], rs.at[rel - 1], peer))
        return sends, recvs

    return plan


def _sum_small(buf, me, name):
    _, rows, n = buf.shape

    def body(me_ref, b_ref, o_ref):
        tot = b_ref[me_ref[0]]
        for dev in range(1, N_DEV):
            tot = tot + b_ref[dev ^ me_ref[0]]
        o_ref[...] = tot

    return _pcall(
        body, name=name, out_shape=jax.ShapeDtypeStruct((rows, n), F32),
        grid_spec=pltpu.PrefetchScalarGridSpec(
            num_scalar_prefetch=1, grid=(1,), in_specs=[pl.BlockSpec((N_DEV, rows, n), lambda i, m: (0, 0, 0))],
            out_specs=pl.BlockSpec((rows, n), lambda i, m: (0, 0))),
    )(me, buf)


def _rs_chips_plan(nw):
    def plan(p_refs, land_refs, ss, rs):
        x, y, c, me, others = _place()
        sends, recvs = [], []
        for wi in range(nw):
            for k, (ox, oy) in enumerate(others):
                sem = 3 * wi + k
                sends.append(_remote(p_refs[wi].at[2 * ox + oy], land_refs[wi].at[k], ss.at[sem], rs.at[sem], (ox, oy, c)))
                recvs.append(_remote(p_refs[wi].at[me], land_refs[wi].at[k], ss.at[sem], rs.at[sem], (x, y, c)))
        return sends, recvs

    return plan


def _rows_per_block(n, c, limit_bytes=2 << 20):
    best = None
    for tm in range(16, n + 1, 16):
        if n % tm == 0 and tm * c * 4 <= limit_bytes:
            best = tm
    return best or n


def _sum_cores(grad, got, kind, place, name):
    _, hr, cw = got.shape
    tm = _rows_per_block(hr, cw)
    nb = hr // tm

    def body(place_ref, g_ref, t_ref, o_ref):
        o_ref[...] = (g_ref[...].astype(F32) + t_ref[...].astype(F32)).astype(o_ref.dtype)

    if kind == "col":
        g_spec = pl.BlockSpec((tm, cw), lambda j, i, pr: (pr[0] * nb + i, j))
    else:
        g_spec = pl.BlockSpec((tm, cw), lambda j, i, pr: ((2 * j + pr[0]) * nb + i, 0))
    blk = pl.BlockSpec((None, tm, cw), lambda j, i, pr: (j, i, 0))
    return _pcall(
        body, name=name, out_shape=jax.ShapeDtypeStruct(got.shape, BF16),
        grid_spec=pltpu.PrefetchScalarGridSpec(num_scalar_prefetch=1, grid=(N_CHIPS, nb), in_specs=[g_spec, blk], out_specs=blk),
        compiler_params=_params("parallel", "parallel"),
    )(place, grad, got)


def _sum_chips(parts, got, place, name):
    _, n, cw = got.shape
    tm = _rows_per_block(n, cw)

    def body(place_ref, p_ref, g_ref, o_ref):
        tot = p_ref[...].astype(F32)
        for k in range(3):
            tot = tot + g_ref[k].astype(F32)
        o_ref[...] = tot

    return _pcall(
        body, name=name, out_shape=jax.ShapeDtypeStruct((2, n, cw), F32),
        grid_spec=pltpu.PrefetchScalarGridSpec(
            num_scalar_prefetch=1, grid=(n // tm,),
            in_specs=[pl.BlockSpec((None, tm, cw), lambda i, pr: (pr[1], i, 0)), pl.BlockSpec((3, tm, cw), lambda i, pr: (0, i, 0))],
            out_specs=pl.BlockSpec((None, tm, cw), lambda i, pr: (pr[0], i, 0))),
        compiler_params=_params("parallel"),
    )(place, parts, got)


def _adamw(g, w, m, v, name):
    n, c = g.shape
    c1 = 1.0 - ADAM_B1 ** ADAM_STEP
    c2 = 1.0 - ADAM_B2 ** ADAM_STEP

    def fn(gb, wb, mb, vb):
        m_new = ADAM_B1 * mb + (1.0 - ADAM_B1) * gb
        v_new = ADAM_B2 * vb + (1.0 - ADAM_B2) * (gb * gb)
        delta = -ADAM_LR * ((m_new / c1) / (jnp.sqrt(v_new / c2) + ADAM_EPS) + ADAM_WD * wb)
        return gb, delta, m_new, v_new

    tm = _rows_per_block(n, c) if n % 16 == 0 else n
    return _rowcall(fn, [_whole(g), _whole(w), _whole(m), _whole(v)], [], [(c, F32)] * 4, tm=tm, name=name)


PACK_ROWS = 16


def _pack_rows(parts, width, name, after=None):
    assert sum(p.shape[0] for p in parts) <= PACK_ROWS

    def body(*refs):
        out_ref = refs[-1]
        out_ref[...] = jnp.zeros_like(out_ref)
        at = 0
        for r in refs[:len(parts)]:
            k, n = r.shape
            if n == width:
                out_ref[at:at + k, :] = r[...]
            else:
                out_ref[at:at + k, :] = jnp.broadcast_to(r[:, :1], (k, width))
            at += k

    vm = pl.BlockSpec(memory_space=pltpu.VMEM)
    return _pcall(body, name=name, in_specs=[vm] * len(parts) + ([] if after is None else [ANY]), out_specs=vm,
                  out_shape=jax.ShapeDtypeStruct((PACK_ROWS, width), F32))(*parts, *([] if after is None else [after]))


def _cast_shard(wm, name, after):
    n, c = wm.shape
    return _rowcall(lambda v: v, [_whole(wm)], [], [(c, BF16)], tm=_rows_per_block(n, c), name=name, after=after)[0]


GATHER_GROUPS = (
    ("w_ffn1_gu",), ("w_ffn1_down",), ("w_in",), ("w_conv_out", "w_attn_out", "w_o"), ("w_cq", "w_ckv", "w_co"),
    ("w_ffn2_gu", "w_ffn2_down"),
)
REDUCE_GROUPS = {
    "ffn2": ("w_ffn2_down", "w_ffn2_gu"),
    "cross": ("w_co", "w_cq", "w_ckv"),
    "mix": ("w_o", "w_conv_out", "w_attn_out", "w_in"),
    "ffn1": ("w_ffn1_down", "w_ffn1_gu"),
}
KIND = dict(MATS)


def _step(x, mem, tgt, wts, m_in, v_in):
    d = x.shape[-1]
    cc = wts["conv_w"].shape[1]
    place = jnp.stack([lax.axis_index("c"), 2 * lax.axis_index("x") + lax.axis_index("y")]).astype(jnp.int32)
    dims = {n: (kind, *wts[n].shape) for n, kind in MATS}

    conv_full = _gather_conv(jnp.pad(wts["conv_w"], ((0, CONV_ROWS - CONV_K), (0, 0))))
    w = {n: wts[n].reshape(1, -1) for n in VECS + ("b_gate",)}
    w["conv_w"] = conv_full[:CONV_K]
    flying, token = {}, conv_full
    for names in GATHER_GROUPS:
        gd = [dims[n] for n in names]
        shards = [_cast_shard(wts[n], "cast_" + n, token) for n in names]
        lands = [lax.empty(_full_shape(*dm), BF16) for dm in gd]
        plan = _gather_plan(gd)
        ss, rs, srcs, lands, token = _split_start("gather_start_" + names[0], plan, 4 * len(names), shards, lands, token)
        flying.update({n: (names, plan, ss, rs, srcs, lands, gd) for n in names})

    passing = {}

    def prefetch(name, after):
        if name not in passing:
            names, plan, ss, rs, srcs, lands, gd = flying[name]
            _, lands = _split_wait("gather_wait_" + names[0], plan, ss, rs, srcs, lands, after)
            plan = _forward_plan(gd)
            ss, rs, _, lands, _ = _split_start("forward_start_" + names[0], plan, 3 * len(names), [], lands)
            passing.update({n: (names, plan, ss, rs, lands) for n in names})

    def fetch(name, after):
        prefetch(name, after)
        names, plan, ss, rs, lands = passing[name]
        _, lands = _split_wait("forward_wait_" + names[0], plan, ss, rs, [], lands, after)
        return dict(zip(names, lands))

    swapping, sent = {}, {}

    def emit(tag, g):
        if tag not in REDUCE_GROUPS:
            return None
        names = REDUCE_GROUPS[tag]
        gd = [dims[n] for n in names]
        lands = [lax.empty((N_CHIPS, r // 2, cw), BF16) for (_, r, cw) in gd]
        plan = _rs_cores_plan(gd)
        ss, rs, srcs, lands, tok = _split_start("rs_cores_start_" + tag, plan, N_CHIPS * len(names), [g[n] for n in names], lands)
        swapping[tag] = (plan, ss, rs, srcs, lands)
        return tok

    def tick(tag, after):
        if tag not in REDUCE_GROUPS:
            return None
        names = REDUCE_GROUPS[tag]
        plan, ss, rs, srcs, lands = swapping[tag]
        mine, got = _split_wait("rs_cores_wait_" + tag, plan, ss, rs, srcs, lands, after)
        parts = [_sum_cores(gm, t, KIND[n], place, "sum_cores_" + n) for n, gm, t in zip(names, mine, got)]
        lands = [lax.empty((3, *p.shape[1:]), BF16) for p in parts]
        plan = _rs_chips_plan(len(names))
        ss, rs, srcs, lands, tok = _split_start("rs_chips_start_" + tag, plan, 3 * len(names), parts, lands)
        sent[tag] = (plan, ss, rs, srcs, lands)
        return tok

    loss_lanes, dx, g, last = _local_step(x[0], mem[0], tgt[0], w, fetch, prefetch, emit, tick, token)

    rows = [g[n] for n in VECS] + [g["b_gate"][:, :d], g["b_gate"][:, d:], g["conv_w"], loss_lanes]
    packed = _pack_rows(rows, d, "pack_small", after=last)
    small = jnp.concatenate([packed[None], jnp.zeros((N_DEV - 1, *packed.shape), F32)], axis=0)
    small_plan = _small_plan()
    small_ss, small_rs, _, small, after = _split_start("small_start", small_plan, N_DEV - 1, [], [small])

    grads, out = {}, {}

    def update(n):
        shape = wts[n].shape
        as2d = (lambda a: a.reshape(1, -1)) if len(shape) == 1 else (lambda a: a)
        return [r.reshape(shape) for r in _adamw(grads[n], as2d(wts[n]), as2d(m_in[n]), as2d(v_in[n]), "adamw_" + n)]

    def finish(sharing, after):
        tag, names, plan, ss, rs, halves = sharing
        _, both = _split_wait("share_wait_" + tag, plan, ss, rs, [], halves, after)
        for n, b in zip(names, both):
            grads[n] = b.reshape(-1, b.shape[-1])
            out[n] = update(n)
        return out[names[-1]][1]

    sharing = None
    for tag, names in REDUCE_GROUPS.items():
        plan, ss, rs, srcs, lands = sent[tag]
        parts, landed = _split_wait("rs_chips_wait_" + tag, plan, ss, rs, srcs, lands, after)
        halves = [_sum_chips(p, t, place, "sum_chips_" + n) for n, p, t in zip(names, parts, landed)]
        plan = _share_plan(len(names))
        ss, rs, _, halves, after = _split_start("share_start_" + tag, plan, len(names), [], halves)
        if sharing is not None:
            after = finish(sharing, after)
        sharing = (tag, names, plan, ss, rs, halves)
    after = finish(sharing, after)

    _, small = _split_wait("small_wait", small_plan, small_ss, small_rs, [], small, after)
    me = (4 * lax.axis_index("x") + 2 * lax.axis_index("y") + lax.axis_index("c")).astype(jnp.int32).reshape(1)
    red = _sum_small(small[0], me, "sum_small")
    grads.update({n: red[i:i + 1] for i, n in enumerate(VECS)})
    nv = len(VECS)
    grads["b_gate"] = jnp.concatenate([red[nv:nv + 1], red[nv + 1:nv + 2]], axis=1)
    chip = 2 * lax.axis_index("x") + lax.axis_index("y")
    grads["conv_w"] = lax.dynamic_slice_in_dim(red[nv + 2:nv + 2 + CONV_K], chip * cc, cc, axis=1)
    loss = red[nv + 2 + CONV_K, 0]
    out.update({n: update(n) for n in WEIGHTS if n not in KIND})
    return (loss, dx[None], *[out[n][0] for n in WEIGHTS], *[out[n][1] for n in WEIGHTS],
            *[out[n][2] for n in WEIGHTS], *[out[n][3] for n in WEIGHTS])


def kernel(x, mem, g_ffn1, w_ffn1_gu, w_ffn1_down, g_mix, w_in, b_gate, conv_w, w_conv_out, w_attn_out, w_o, g_cross, g_mem, w_cq, w_ckv, w_co, g_ffn2, w_ffn2_gu, w_ffn2_down, g_final, loss_target, m_g_ffn1, m_w_ffn1_gu, m_w_ffn1_down, m_g_mix, m_w_in, m_b_gate, m_conv_w, m_w_conv_out, m_w_attn_out, m_w_o, m_g_cross, m_g_mem, m_w_cq, m_w_ckv, m_w_co, m_g_ffn2, m_w_ffn2_gu, m_w_ffn2_down, m_g_final, v_g_ffn1, v_w_ffn1_gu, v_w_ffn1_down, v_g_mix, v_w_in, v_b_gate, v_conv_w, v_w_conv_out, v_w_attn_out, v_w_o, v_g_cross, v_g_mem, v_w_cq, v_w_ckv, v_w_co, v_g_ffn2, v_w_ffn2_gu, v_w_ffn2_down, v_g_final):
    given = dict(locals())
    wts = {n: given[n] for n in WEIGHTS}
    m_in = {n: given["m_" + n] for n in WEIGHTS}
    v_in = {n: given["v_" + n] for n in WEIGHTS}
    return _step(x, mem, loss_target, wts, m_in, v_in)
```

```python
import functools

import jax
import jax.numpy as jnp
from jax import lax
from jax.experimental import pallas as pl
from jax.experimental.pallas import tpu as pltpu

F32 = jnp.float32
BF16 = jnp.bfloat16
MESH = pl.DeviceIdType.MESH

V7X_VMEM_LIMIT_BYTES = 48 * 1024 * 1024
MM_VMEM_BUDGET_BYTES = 36 * 1024 * 1024
MM_WHOLE_K = 2816
LANES = 128
SB_HEAD_DIM = 128
X_HEADS = 4
CONV_K = 3
RMS_EPS = 1e-6
N_CHIPS = 4
N_DEV = 8
ADAM_LR, ADAM_B1, ADAM_B2, ADAM_EPS, ADAM_WD, ADAM_STEP = 0.001, 0.9, 0.999, 1e-08, 0.01, 10


ANY = pl.BlockSpec(memory_space=pl.ANY)


def _pcall(body, **kw):
    return pl.pallas_call(body, **kw)


def _params(*sem):
    return pltpu.CompilerParams(dimension_semantics=sem, vmem_limit_bytes=V7X_VMEM_LIMIT_BYTES)


def _pick(dim, cands):
    for c in cands:
        if dim % c == 0:
            return c
    return dim


def _dot(a, b, ca, cb):
    return lax.dot_general(a, b, (((ca,), (cb,)), ((), ())), preferred_element_type=F32)


def _mm(a, b, *, name, ta=False, tb=False, out_dtype=BF16, res=None, alpha=1.0, tm=None, tn=None, tk=None, after=None,
        a_halves=False, b_halves=False):
    assert not (a_halves and ta) and not (b_halves and tb)
    if a_halves:
        m, k = a.shape[1], 2 * a.shape[2]
    else:
        m, k = (a.shape[1], a.shape[0]) if ta else a.shape
    if b_halves:
        n = 2 * b.shape[2]
        assert k == b.shape[1]
    else:
        n = b.shape[0] if tb else b.shape[1]
        assert k == (b.shape[1] if tb else b.shape[0]), (a.shape, b.shape, ta, tb)
    if ta:
        tm = tm or _pick(m, (512, 256, 128))
        tn = tn or _pick(n, (1024, 512, 256, 128))
        tk = tk or (k if k <= MM_WHOLE_K else _pick(k, (1024, 512, 256, 128)))
    else:
        tk = tk or (k if k <= MM_WHOLE_K else _pick(k, (MM_WHOLE_K, 2048, 1024, 512, 256, 128)))
        tn = tn or _pick(n, (512, 1408, 256, 128) if tk == k else (1024, 512, 256, 128))
        per_row = 2 * (tk * a.dtype.itemsize + tn * (jnp.dtype(out_dtype).itemsize + (0 if res is None else res.dtype.itemsize)))
        per_row += 4 * tn if tk < k else 0
        rows = (MM_VMEM_BUDGET_BYTES - 2 * tk * tn * b.dtype.itemsize) // per_row
        tm = tm or next((c for c in (2048, 1024, 512, 256, 128) if m % c == 0 and c <= rows), m)
    if a_halves:
        tk = min(tk, k // 2) if (k // 2) % min(tk, k // 2) == 0 else _pick(k // 2, (1408, 1024, 512, 256, 128))
    if b_halves:
        tn = tn if (n // 2) % tn == 0 else _pick(n // 2, (1408, 1024, 512, 256, 128))
    nk = k // tk
    assert m % tm == 0 and n % tn == 0 and k % tk == 0
    a_spec = pl.BlockSpec((tk, tm), lambda i, j, kk: (kk, i)) if ta else pl.BlockSpec((tm, tk), lambda i, j, kk: (i, kk))
    b_spec = pl.BlockSpec((tn, tk), lambda i, j, kk: (j, kk)) if tb else pl.BlockSpec((tk, tn), lambda i, j, kk: (kk, j))
    if a_halves:
        per = (k // 2) // tk
        a_spec = pl.BlockSpec((None, tm, tk), lambda i, j, kk: (kk // per, i, kk % per))
    if b_halves:
        per_n = (n // 2) // tn
        b_spec = pl.BlockSpec((None, tk, tn), lambda i, j, kk: (j // per_n, kk, j % per_n))
    o_spec = pl.BlockSpec((tm, tn), lambda i, j, kk: (i, j))
    ca, cb = (0 if ta else 1), (1 if tb else 0)

    n_in = 2 + (res is not None) + (after is not None)

    def body(*refs):
        a_ref, b_ref = refs[:2]
        res_ref = refs[2] if res is not None else None
        o_ref = refs[n_in]
        scratch = refs[n_in + 1:]

        def finish(acc):
            val = acc if alpha == 1.0 else alpha * acc
            if res_ref is not None:
                val = res_ref[...].astype(F32) + val
            o_ref[...] = val.astype(o_ref.dtype)

        part = _dot(a_ref[...].astype(BF16), b_ref[...].astype(BF16), ca, cb)
        if nk == 1:
            finish(part)
        else:
            acc_ref = scratch[0]
            kk = pl.program_id(2)

            @pl.when(kk == 0)
            def _():
                acc_ref[...] = part

            @pl.when(kk > 0)
            def _():
                acc_ref[...] += part

            @pl.when(kk == nk - 1)
            def _():
                finish(acc_ref[...])

    ins = [a, b] + ([] if res is None else [res]) + ([] if after is None else [after])
    in_specs = [a_spec, b_spec] + ([] if res is None else [o_spec]) + ([] if after is None else [ANY])
    return _pcall(
        body, name=name, grid=(m // tm, n // tn, nk), in_specs=in_specs, out_specs=o_spec,
        out_shape=jax.ShapeDtypeStruct((m, n), out_dtype),
        scratch_shapes=[pltpu.VMEM((tm, tn), F32)] if nk > 1 else [],
        compiler_params=_params("parallel", "parallel", "arbitrary"),
    )(*ins)


def _rowcall(fn, rows, consts, outs, accs=(), *, tm, name, after=None):
    s = rows[0][0].shape[0]
    assert s % tm == 0
    n_read, n_out = len(rows) + len(consts), len(outs)
    n_in = n_read + (after is not None)

    def body(*refs):
        vals = fn(*[r[...] for r in refs[:n_read]])
        vals = vals if isinstance(vals, (tuple, list)) else (vals,)
        for o_ref, v in zip(refs[n_in:n_in + n_out], vals[:n_out]):
            o_ref[...] = v.astype(o_ref.dtype)
        if accs:
            first = pl.program_id(0) == 0
            for a_ref, v in zip(refs[n_in + n_out:], vals[n_out:]):
                tot = jnp.sum(v.astype(F32), axis=0, keepdims=True)

                @pl.when(first)
                def _(a_ref=a_ref, tot=tot):
                    a_ref[...] = tot

                @pl.when(jnp.logical_not(first))
                def _(a_ref=a_ref, tot=tot):
                    a_ref[...] += tot

    in_specs = [pl.BlockSpec((tm, w), lambda i, cb=cb: (i, cb)) for (_, cb, w) in rows]
    in_specs += [pl.BlockSpec(c.shape, lambda i: (0, 0)) for c in consts]
    in_specs += [] if after is None else [ANY]
    out_specs = [pl.BlockSpec((tm, w), lambda i: (i, 0)) for (w, _) in outs]
    out_specs += [pl.BlockSpec((1, w), lambda i: (0, 0)) for w in accs]
    out_shape = [jax.ShapeDtypeStruct((s, w), dt) for (w, dt) in outs]
    out_shape += [jax.ShapeDtypeStruct((1, w), F32) for w in accs]
    return _pcall(
        body, name=name, grid=(s // tm,), in_specs=in_specs, out_specs=out_specs, out_shape=out_shape,
        compiler_params=_params("arbitrary" if accs else "parallel"),
    )(*[r[0] for r in rows], *consts, *([] if after is None else [after]))


def _whole(a):
    return (a, 0, a.shape[1])


def _xhat(x):
    x = x.astype(F32)
    r = lax.rsqrt(jnp.mean(x * x, axis=-1, keepdims=True) + RMS_EPS)
    return x * r, r


def _rms_bwd(dy, x, g):
    xh, r = _xhat(x)
    dxh = dy.astype(F32) * g
    dx = r * (dxh - xh * jnp.mean(dxh * xh, axis=-1, keepdims=True))
    return dx, dy.astype(F32) * xh


def _sigmoid(x):
    return 1.0 / (1.0 + jnp.exp(-x))


def _rms_fwd(x, g, name, tm, after=None):
    d = x.shape[1]
    return _rowcall(lambda xb, gb: _xhat(xb)[0] * gb, [_whole(x)], [g], [(d, BF16)], tm=tm, name=name, after=after)[0]


def _silu_parts(gate):
    sg = _sigmoid(gate)
    return sg, gate * sg


def _ffn_up(n, w_gu, name):
    s, d = n.shape
    f = w_gu.shape[1] // 2
    tn = _pick(f, (1408, 1024, 512, 256, 128))
    tm = _pick(s, (1024, 512, 256, 128))
    nb = f // tn

    def body(n_ref, wg_ref, wu_ref, gu_ref, act_ref):
        nv = n_ref[...]
        gate = _dot(nv, wg_ref[...], 1, 0)
        up = _dot(nv, wu_ref[...], 1, 0)
        gu_ref[0] = gate.astype(gu_ref.dtype)
        gu_ref[1] = up.astype(gu_ref.dtype)
        act_ref[...] = (_silu_parts(gate)[1] * up).astype(act_ref.dtype)

    return _pcall(
        body, name=name, grid=(s // tm, nb),
        in_specs=[pl.BlockSpec((tm, d), lambda i, j: (i, 0)), pl.BlockSpec((d, tn), lambda i, j: (0, j)),
                  pl.BlockSpec((d, tn), lambda i, j: (0, nb + j))],
        out_specs=[pl.BlockSpec((2, tm, tn), lambda i, j: (0, i, j)), pl.BlockSpec((tm, tn), lambda i, j: (i, j))],
        out_shape=[jax.ShapeDtypeStruct((2, s, f), BF16), jax.ShapeDtypeStruct((s, f), BF16)],
        compiler_params=_params("parallel", "parallel"),
    )(n, w_gu, w_gu)


def _ffn_dgu(dhb, w_down, gu, name, after=None):
    s, d = dhb.shape
    f = w_down.shape[0]
    tn = _pick(f, (1408, 1024, 512, 256, 128))
    tm = _pick(s, (1024, 512, 256, 128))

    def body(dh_ref, w_ref, gu_ref, *rest):
        o_ref = rest[-1]
        dact = _dot(dh_ref[...], w_ref[...], 1, 1)
        gate, up = gu_ref[0].astype(F32), gu_ref[1].astype(F32)
        sg, silu = _silu_parts(gate)
        o_ref[0] = (dact * up * (sg + silu * (1.0 - sg))).astype(o_ref.dtype)
        o_ref[1] = (dact * silu).astype(o_ref.dtype)

    blk = pl.BlockSpec((2, tm, tn), lambda i, j: (0, i, j))
    return _pcall(
        body, name=name, grid=(s // tm, f // tn),
        in_specs=[pl.BlockSpec((tm, d), lambda i, j: (i, 0)), pl.BlockSpec((tn, d), lambda i, j: (j, 0)), blk]
        + ([] if after is None else [ANY]),
        out_specs=blk, out_shape=jax.ShapeDtypeStruct((2, s, f), BF16), compiler_params=_params("parallel", "parallel"),
    )(dhb, w_down, gu, *([] if after is None else [after]))


def _dgrad_norm(dy, wmat, dh, x, g, name, *, dy_halves=False, copy_scale=None, after=None):
    s, d = dh.shape
    k = wmat.shape[1]
    tk = k if k <= MM_WHOLE_K else _pick(k, (MM_WHOLE_K, 2048, 1024, 512, 256, 128))
    if dy_halves and (k // 2) % tk:
        tk = _pick(k // 2, (1408, 1024, 512, 256, 128))
    tm = _pick(s, (512, 256, 128))
    nk, per = k // tk, (k // 2) // tk if dy_halves else 0
    n_in = 5 + (after is not None)
    n_out = 2 + (copy_scale is not None)

    def body(*refs):
        dy_ref, w_ref, dh_ref, x_ref, g_ref = refs[:5]
        outs, scratch = refs[n_in:n_in + n_out], refs[n_in + n_out:]
        i, kk = pl.program_id(0), pl.program_id(1)
        part = _dot(dy_ref[...], w_ref[...], 1, 1)

        def finish(dn):
            dx, dg = _rms_bwd(dn, x_ref[...], g_ref[...])
            tot = dh_ref[...] + dx
            outs[0][...] = tot
            if copy_scale is not None:
                outs[1][...] = (copy_scale * tot).astype(outs[1].dtype)
            dg = jnp.sum(dg, axis=0, keepdims=True)

            @pl.when(i == 0)
            def _():
                outs[-1][...] = dg

            @pl.when(i > 0)
            def _():
                outs[-1][...] += dg

        if nk == 1:
            finish(part)
        else:
            acc_ref = scratch[0]

            @pl.when(kk == 0)
            def _():
                acc_ref[...] = part

            @pl.when(kk > 0)
            def _():
                acc_ref[...] += part

            @pl.when(kk == nk - 1)
            def _():
                finish(acc_ref[...])

    row = pl.BlockSpec((tm, d), lambda i, kk: (i, 0))
    dy_spec = pl.BlockSpec((None, tm, tk), lambda i, kk: (kk // per, i, kk % per)) if dy_halves else pl.BlockSpec((tm, tk), lambda i, kk: (i, kk))
    in_specs = [dy_spec, pl.BlockSpec((d, tk), lambda i, kk: (0, kk)), row, row, pl.BlockSpec((1, d), lambda i, kk: (0, 0))]
    out_specs = [row] * (n_out - 1) + [pl.BlockSpec((1, d), lambda i, kk: (0, 0))]
    out_shape = [jax.ShapeDtypeStruct((s, d), F32)] + ([] if copy_scale is None else [jax.ShapeDtypeStruct((s, d), BF16)])
    return _pcall(
        body, name=name, grid=(s // tm, nk), in_specs=in_specs + ([] if after is None else [ANY]), out_specs=out_specs,
        out_shape=out_shape + [jax.ShapeDtypeStruct((1, d), F32)], scratch_shapes=[pltpu.VMEM((tm, d), F32)] if nk > 1 else [],
        compiler_params=_params("arbitrary", "arbitrary"),
    )(dy, wmat, dh, x, g, *([] if after is None else [after]))


def _shift_down(p, k):
    if k == 0:
        return p
    rows = lax.broadcasted_iota(jnp.int32, p.shape, 0)
    return jnp.where(rows >= k, pltpu.roll(p, k, 0), 0.0)


def _shift_up(p, k):
    if k == 0:
        return p
    s = p.shape[0]
    rows = lax.broadcasted_iota(jnp.int32, p.shape, 0)
    return jnp.where(rows < s - k, pltpu.roll(p, s - k, 0), 0.0)


def _conv_fwd(proj, conv_w, d, tc, name):
    s = proj.shape[0]
    nb = d // tc

    def body(cb_ref, cc_ref, cx_ref, w_ref, y_ref):
        p = cc_ref[...].astype(F32) * cx_ref[...].astype(F32)
        w = w_ref[...]
        acc = p * w[CONV_K - 1:CONV_K, :]
        for k in range(1, CONV_K):
            acc = acc + _shift_down(p, k) * w[CONV_K - 1 - k:CONV_K - k, :]
        y_ref[...] = (cb_ref[...].astype(F32) * acc).astype(y_ref.dtype)

    col = lambda off: pl.BlockSpec((s, tc), lambda j: (0, off * nb + j))
    return _pcall(
        body, name=name, grid=(nb,), in_specs=[col(0), col(1), col(2), pl.BlockSpec((CONV_K, tc), lambda j: (0, j))],
        out_specs=pl.BlockSpec((s, tc), lambda j: (0, j)), out_shape=jax.ShapeDtypeStruct((s, d), BF16),
        compiler_params=_params("parallel"),
    )(proj, proj, proj, conv_w)


def _conv_bwd(dy, proj, conv_w, d, tc, name):
    s = proj.shape[0]
    nb = d // tc

    def body(dy_ref, cb_ref, cc_ref, cx_ref, w_ref, dcb_ref, dcc_ref, dcx_ref, dw_ref):
        cc, cx = cc_ref[...].astype(F32), cx_ref[...].astype(F32)
        p = cc * cx
        w = w_ref[...]
        dyv = dy_ref[...].astype(F32)
        shifted = [_shift_down(p, CONV_K - 1 - k) for k in range(CONV_K)]
        conv = shifted[0] * w[0:1, :]
        for k in range(1, CONV_K):
            conv = conv + shifted[k] * w[k:k + 1, :]
        dcb_ref[...] = (dyv * conv).astype(dcb_ref.dtype)
        ds = dyv * cb_ref[...].astype(F32)
        dp = ds * w[CONV_K - 1:CONV_K, :]
        for k in range(1, CONV_K):
            dp = dp + _shift_up(ds, k) * w[CONV_K - 1 - k:CONV_K - k, :]
        dcc_ref[...] = (dp * cx).astype(dcc_ref.dtype)
        dcx_ref[...] = (dp * cc).astype(dcx_ref.dtype)
        for k in range(CONV_K):
            dw_ref[k:k + 1, :] = jnp.sum(ds * shifted[k], axis=0, keepdims=True)

    col = lambda off: pl.BlockSpec((s, tc), lambda j: (0, off * nb + j))
    blk = pl.BlockSpec((s, tc), lambda j: (0, j))
    wblk = pl.BlockSpec((CONV_K, tc), lambda j: (0, j))
    act = jax.ShapeDtypeStruct((s, d), BF16)
    return _pcall(
        body, name=name, grid=(nb,), in_specs=[blk, col(0), col(1), col(2), wblk],
        out_specs=[blk, blk, blk, wblk], out_shape=[act, act, act, jax.ShapeDtypeStruct((CONV_K, d), F32)],
        compiler_params=_params("parallel"),
    )(dy, proj, proj, proj, conv_w)


def _sb_tile(q, kj, scale, carry, tri, mask):
    z = _dot(q, kj, 1, 1) * scale
    lsz = jnp.minimum(z, 0.0) - jnp.log(1.0 + jnp.exp(-jnp.abs(z)))
    l1m = lsz - z
    if mask is not None:
        l1m = jnp.where(mask, l1m, 0.0)
    l1b = l1m.astype(BF16)
    a = jnp.exp(lsz + (carry + _dot(l1b, tri, 1, 0)))
    if mask is not None:
        a = jnp.where(mask, a, 0.0)
    return lsz, l1b, a.astype(BF16)


def _sb_masks(tq, tk):
    row = lax.broadcasted_iota(jnp.int32, (tq, tk), 0)
    col = lax.broadcasted_iota(jnp.int32, (tq, tk), 1)
    masks = [col + dj * tk < row for dj in range(tq // tk)]
    r2 = lax.broadcasted_iota(jnp.int32, (tk, tk), 0)
    c2 = lax.broadcasted_iota(jnp.int32, (tk, tk), 1)
    return masks, (r2 > c2).astype(BF16), (r2 < c2).astype(BF16)


def _sb_fwd(proj, heads, col0, tq, tk, name):
    s = proj.shape[0]
    dh = SB_HEAD_DIM
    nq, nd, nkt = s // tq, tq // tk, s // tk
    scale = dh ** -0.5

    def body(q_ref, k_ref, v_ref, o_ref, a_ref, b_ref):
        i = pl.program_id(1)
        q = q_ref[...]
        masks, tri_right, _ = _sb_masks(tq, tk)

        def tile(j, carry, acc, mask):
            start = pl.multiple_of(j * tk, tk)
            kj = k_ref[pl.ds(start, tk), :]
            vj = v_ref[pl.ds(start, tk), :]
            lsz, l1b, ab = _sb_tile(q, kj, scale, carry, tri_right, mask)
            a_ref[j] = ab
            b_ref[j] = jnp.exp(lsz).astype(b_ref.dtype)
            return carry + jnp.sum(l1b.astype(F32), axis=1, keepdims=True), acc + _dot(ab, vj, 1, 0)

        state = (jnp.zeros((tq, 1), F32), jnp.zeros((tq, dh), F32))
        for dj in reversed(range(nd)):
            state = tile(i * nd + dj, *state, masks[dj])
        state = lax.fori_loop(0, i * nd, lambda t, st: tile(i * nd - 1 - t, st[0], st[1], None), state)
        o_ref[...] = state[1]

    qspec = pl.BlockSpec((tq, dh), lambda h, i: (i, col0[0] + h))
    kspec = pl.BlockSpec((s, dh), lambda h, i: (0, col0[1] + h))
    vspec = pl.BlockSpec((s, dh), lambda h, i: (0, col0[2] + h))
    saved = pl.BlockSpec((None, nkt, tq, tk), lambda h, i: (h, 0, i, 0))
    saved_shape = jax.ShapeDtypeStruct((heads, nkt, s, tk), BF16)
    return _pcall(
        body, name=name, grid=(heads, nq), in_specs=[qspec, kspec, vspec],
        out_specs=[pl.BlockSpec((tq, dh), lambda h, i: (i, h)), saved, saved],
        out_shape=[jax.ShapeDtypeStruct((s, heads * dh), F32), saved_shape, saved_shape],
        compiler_params=_params("parallel", "parallel"),
    )(proj, proj, proj)


def _sb_bwd(proj, o, a_all, beta_all, do, heads, col0, tq, tk, name):
    s = proj.shape[0]
    dh = SB_HEAD_DIM
    nq, nd, nkt = s // tq, tq // tk, s // tk
    scale = dh ** -0.5

    def body(q_ref, k_ref, v_ref, o_ref, a_ref, b_ref, do_ref, dq_ref, dk_ref, dv_ref, dk_acc, dv_acc):
        i = pl.program_id(1)

        @pl.when(i == 0)
        def _():
            dk_acc[...] = jnp.zeros_like(dk_acc)
            dv_acc[...] = jnp.zeros_like(dv_acc)

        q = q_ref[...]
        dob = do_ref[...].astype(BF16)
        delta = jnp.sum(dob.astype(F32) * o_ref[...], axis=1, keepdims=True)
        masks, _, tri_left = _sb_masks(tq, tk)

        def tile(j, carry_g, dq, mask):
            start = pl.multiple_of(j * tk, tk)
            kj = k_ref[pl.ds(start, tk), :]
            vj = v_ref[pl.ds(start, tk), :]
            ab = a_ref[j]
            g = _dot(dob, vj, 1, 1) * ab.astype(F32)
            carry_g = carry_g + jnp.sum(g, axis=1, keepdims=True)
            left = (delta - carry_g) + _dot(g.astype(BF16), tri_left, 1, 0)
            dz = g - b_ref[j].astype(F32) * (g + left)
            if mask is not None:
                dz = jnp.where(mask, dz, 0.0)
            dzb = dz.astype(BF16)
            dk_acc[pl.ds(start, tk), :] += _dot(dzb, q, 0, 0)
            dv_acc[pl.ds(start, tk), :] += _dot(ab, dob, 0, 0)
            return carry_g, dq + _dot(dzb, kj, 1, 0)

        state = (jnp.zeros((tq, 1), F32), jnp.zeros((tq, dh), F32))
        for dj in reversed(range(nd)):
            state = tile(i * nd + dj, *state, masks[dj])
        state = lax.fori_loop(0, i * nd, lambda t, st: tile(i * nd - 1 - t, st[0], st[1], None), state)
        dq_ref[...] = (state[1] * scale).astype(dq_ref.dtype)

        @pl.when(i == nq - 1)
        def _():
            dk_ref[...] = (dk_acc[...] * scale).astype(dk_ref.dtype)
            dv_ref[...] = dv_acc[...].astype(dv_ref.dtype)

    qspec = pl.BlockSpec((tq, dh), lambda h, i: (i, col0[0] + h))
    kspec = pl.BlockSpec((s, dh), lambda h, i: (0, col0[1] + h))
    vspec = pl.BlockSpec((s, dh), lambda h, i: (0, col0[2] + h))
    blk = pl.BlockSpec((tq, dh), lambda h, i: (i, h))
    full = pl.BlockSpec((s, dh), lambda h, i: (0, h))
    saved = pl.BlockSpec((None, nkt, tq, tk), lambda h, i: (h, 0, i, 0))
    act = jax.ShapeDtypeStruct((s, heads * dh), BF16)
    return _pcall(
        body, name=name, grid=(heads, nq), in_specs=[qspec, kspec, vspec, blk, saved, saved, blk],
        out_specs=[blk, full, full], out_shape=[act, act, act],
        scratch_shapes=[pltpu.VMEM((s, dh), F32), pltpu.VMEM((s, dh), F32)],
        compiler_params=_params("parallel", "arbitrary"),
    )(proj, proj, proj, o, a_all, beta_all, do)


def _xattn_probs(q, k, scale):
    sc = _dot(q, k, 1, 1) * scale
    e = jnp.exp(sc - jnp.max(sc, axis=1, keepdims=True))
    return e / jnp.sum(e, axis=1, keepdims=True)


def _xattn_fwd(qc, kv, tq, name):
    s, d = qc.shape
    m = kv.shape[0]
    dh = d // X_HEADS
    scale = dh ** -0.5

    def body(q_ref, k_ref, v_ref, o_ref):
        p = _xattn_probs(q_ref[...], k_ref[...], scale)
        o_ref[...] = _dot(p.astype(BF16), v_ref[...], 1, 0).astype(o_ref.dtype)

    blk = pl.BlockSpec((tq, dh), lambda h, i: (i, h))
    return _pcall(
        body, name=name, grid=(X_HEADS, s // tq),
        in_specs=[blk, pl.BlockSpec((m, dh), lambda h, i: (0, h)), pl.BlockSpec((m, dh), lambda h, i: (0, X_HEADS + h))],
        out_specs=blk, out_shape=jax.ShapeDtypeStruct((s, d), BF16), compiler_params=_params("parallel", "parallel"),
    )(qc, kv, kv)


def _xattn_bwd(qc, kv, do, tq, name):
    s, d = qc.shape
    m = kv.shape[0]
    dh = d // X_HEADS
    scale = dh ** -0.5
    nq = s // tq

    def body(q_ref, k_ref, v_ref, do_ref, dq_ref, dk_ref, dv_ref, dk_acc, dv_acc):
        i = pl.program_id(1)
        q, k, v = q_ref[...], k_ref[...], v_ref[...]
        dob = do_ref[...].astype(BF16)
        p = _xattn_probs(q, k, scale)
        pb = p.astype(BF16)
        dp = _dot(dob, v, 1, 1)
        ds = pb.astype(F32) * (dp - jnp.sum(dp * pb.astype(F32), axis=1, keepdims=True))
        dsb = (ds * scale).astype(BF16)
        dq_ref[...] = _dot(dsb, k, 1, 0).astype(dq_ref.dtype)
        dk_part = _dot(dsb, q, 0, 0)
        dv_part = _dot(pb, dob, 0, 0)

        @pl.when(i == 0)
        def _():
            dk_acc[...] = dk_part
            dv_acc[...] = dv_part

        @pl.when(i > 0)
        def _():
            dk_acc[...] += dk_part
            dv_acc[...] += dv_part

        @pl.when(i == nq - 1)
        def _():
            dk_ref[...] = dk_acc[...].astype(dk_ref.dtype)
            dv_ref[...] = dv_acc[...].astype(dv_ref.dtype)

    blk = pl.BlockSpec((tq, dh), lambda h, i: (i, h))
    kblk = pl.BlockSpec((m, dh), lambda h, i: (0, h))
    return _pcall(
        body, name=name, grid=(X_HEADS, nq),
        in_specs=[blk, kblk, pl.BlockSpec((m, dh), lambda h, i: (0, X_HEADS + h)), blk],
        out_specs=[blk, kblk, kblk],
        out_shape=[jax.ShapeDtypeStruct((s, d), BF16), jax.ShapeDtypeStruct((m, d), BF16), jax.ShapeDtypeStruct((m, d), BF16)],
        scratch_shapes=[pltpu.VMEM((m, dh), F32), pltpu.VMEM((m, dh), F32)],
        compiler_params=_params("parallel", "arbitrary"),
    )(qc, kv, kv, do)


def _local_step(x, mem, tgt, w, fetch=None, prefetch=None, emit=None, tick=None, after=None):
    fetch = fetch or (lambda name, after: {})
    prefetch = prefetch or (lambda name, after: None)
    emit = emit or (lambda group, g: None)
    tick = tick or (lambda group, after: None)
    w = dict(w)
    s, d = x.shape
    heads = d // SB_HEAD_DIM
    tm = _pick(s, (512, 256, 128))
    tq = _pick(s, (256, 128))
    sb_tq, sb_tk = _pick(s, (512, 256, 128)), _pick(s, (256, 128))
    tc = _pick(d, (256, 128))
    g = {}

    def wt(name, after):
        if name not in w:
            w.update(fetch(name, after))
        return w[name]

    def ffn_fwd(h, gname, wgu, wdown, tag, after=None):
        n = _rms_fwd(h, w[gname], tag + "_norm", tm, after=after)
        gu, act = _ffn_up(n, wt(wgu, n), tag + "_gu")
        prefetch(wdown, gu)
        return n, gu, act, _mm(act, wt(wdown, act), name=tag + "_down", out_dtype=F32, res=h, alpha=0.5)

    def ffn_bwd(dh, dhb, h, saved, gname, wgu, wdown, tag, copy_scale=None, after=None):
        n, gu, act = saved
        g[wdown] = _mm(act, dhb, ta=True, name=tag + "_dwdown", after=after)
        dgu = _ffn_dgu(dhb, w[wdown], gu, tag + "_dgu", after=emit(tag + "_down", g))
        g[wgu] = _mm(n, dgu, ta=True, b_halves=True, name=tag + "_dwgu", after=tick(tag + "_down", dgu))
        *dh_in, g[gname] = _dgrad_norm(dgu, w[wgu], dh, h, w[gname], tag + "_dn", dy_halves=True, copy_scale=copy_scale,
                                       after=emit(tag, g))
        return dh_in, tick(tag, dh_in[0])

    n1, gu1, act1, h1 = ffn_fwd(x, "g_ffn1", "w_ffn1_gu", "w_ffn1_down", "ffn1", after)
    prefetch("w_in", h1)
    u = _rms_fwd(h1, w["g_mix"], "mix_norm", tm)
    proj = _mm(u, wt("w_in", u), name="mix_in")
    prefetch("w_conv_out", proj)
    nd = d // SB_HEAD_DIM
    y_conv = _conv_fwd(proj, w["conv_w"], d, tc, "conv_fwd")
    sb_cols = (3 * nd, 4 * nd, 5 * nd)
    y_sb, sb_a, sb_beta = _sb_fwd(proj, heads, sb_cols, sb_tq, sb_tk, "sb_fwd")
    prefetch("w_cq", y_sb)
    a_conv = _mm(y_conv, wt("w_conv_out", y_conv), name="conv_out")
    a_sb = _mm(y_sb, wt("w_attn_out", y_sb), name="attn_out")
    b_conv, b_sb = w["b_gate"][:, :d], w["b_gate"][:, d:]

    def merge(ac, asb, gcp, gsp, bc, bs):
        gc = _sigmoid(gcp.astype(F32) + bc)
        gs = _sigmoid(gsp.astype(F32) + bs)
        return gc * ac.astype(F32) + gs * asb.astype(F32)

    merged = _rowcall(merge, [_whole(a_conv), _whole(a_sb), (proj, 6, d), (proj, 7, d)], [b_conv, b_sb], [(d, BF16)],
                      tm=tm, name="merge")[0]
    prefetch("w_ffn2_gu", merged)
    h2 = _mm(merged, wt("w_o", merged), name="mix_out", out_dtype=F32, res=h1)
    hn = _rms_fwd(h2, w["g_cross"], "cross_norm", tm)
    mn = _rms_fwd(mem, w["g_mem"], "mem_norm", _pick(mem.shape[0], (256, 128)))
    qc = _mm(hn, wt("w_cq", hn), name="cross_q")
    kv = _mm(mn, wt("w_ckv", mn), name="cross_kv")
    oc = _xattn_fwd(qc, kv, tq, "xattn_fwd")
    h3 = _mm(oc, wt("w_co", oc), name="cross_out", out_dtype=F32, res=h2)
    n2, gu2, act2, h4 = ffn_fwd(h3, "g_ffn2", "w_ffn2_gu", "w_ffn2_down", "ffn2")

    def head(hb, tb, gb):
        xh, r = _xhat(hb)
        err = xh * gb - tb
        dy = err * (1.0 / d)
        dxh = dy * gb
        dx = r * (dxh - xh * jnp.mean(dxh * xh, axis=-1, keepdims=True))
        row_loss = 0.5 * jnp.mean(err * err, axis=-1, keepdims=True)
        return dx, 0.5 * dx, dy * xh, jnp.broadcast_to(row_loss, (row_loss.shape[0], LANES))

    dh4, dh4b, g["g_final"], loss_lanes = _rowcall(head, [_whole(h4), _whole(tgt)], [w["g_final"]], [(d, F32), (d, BF16)],
                                                   [d, LANES], tm=tm, name="loss_head")

    (dh3, dh3b), tok = ffn_bwd(dh4, dh4b, h3, (n2, gu2, act2), "g_ffn2", "w_ffn2_gu", "w_ffn2_down", "ffn2", copy_scale=1.0)
    g["w_co"] = _mm(oc, dh3b, ta=True, name="cross_dwco", after=tok)
    doc = _mm(dh3b, w["w_co"], tb=True, name="cross_doc")
    dqc, dk, dv = _xattn_bwd(qc, kv, doc, tq, "xattn_bwd")
    dkv = jnp.concatenate([dk, dv], axis=1)
    g["w_cq"] = _mm(hn, dqc, ta=True, name="cross_dwcq")
    g["w_ckv"] = _mm(mn, dkv, ta=True, name="cross_dwckv")
    dmn = _mm(dkv, w["w_ckv"], tb=True, name="cross_dmn", out_dtype=F32)
    g["g_mem"] = _rowcall(lambda dy, xb: dy * _xhat(xb)[0], [_whole(dmn), _whole(mem)], [], [], [d],
                          tm=_pick(mem.shape[0], (256, 128)), name="mem_dnorm")[0]
    dh2, dh2b, g["g_cross"] = _dgrad_norm(dqc, w["w_cq"], dh3, h2, w["g_cross"], "cross_dhn", copy_scale=1.0, after=emit("cross", g))

    g["w_o"] = _mm(merged, dh2b, ta=True, name="mix_dwo", after=tick("cross", dh2))
    dmerged = _mm(dh2b, w["w_o"], tb=True, name="mix_dmerged")

    def merge_bwd(dm, ac, asb, gcp, gsp, bc, bs):
        dm, ac, asb = dm.astype(F32), ac.astype(F32), asb.astype(F32)
        gc = _sigmoid(gcp.astype(F32) + bc)
        gs = _sigmoid(gsp.astype(F32) + bs)
        dgc = dm * ac * gc * (1.0 - gc)
        dgs = dm * asb * gs * (1.0 - gs)
        return dm * gc, dm * gs, dgc, dgs, dgc, dgs

    da_conv, da_sb, dgc, dgs, db_conv, db_sb = _rowcall(
        merge_bwd, [_whole(dmerged), _whole(a_conv), _whole(a_sb), (proj, 6, d), (proj, 7, d)], [b_conv, b_sb],
        [(d, BF16)] * 4, [d, d], tm=tm, name="merge_bwd")
    g["b_gate"] = jnp.concatenate([db_conv, db_sb], axis=1)
    g["w_conv_out"] = _mm(y_conv, da_conv, ta=True, name="conv_dwout")
    g["w_attn_out"] = _mm(y_sb, da_sb, ta=True, name="attn_dwout")
    dy_conv = _mm(da_conv, w["w_conv_out"], tb=True, name="conv_dy")
    dy_sb = _mm(da_sb, w["w_attn_out"], tb=True, name="attn_dy")
    dcb, dcc, dcx, g["conv_w"] = _conv_bwd(dy_conv, proj, w["conv_w"], d, tc, "conv_bwd")
    dq, dk_sb, dv_sb = _sb_bwd(proj, y_sb, sb_a, sb_beta, dy_sb, heads, sb_cols, sb_tq, sb_tk, "sb_bwd")
    dproj = jnp.concatenate([dcb, dcc, dcx, dq, dk_sb, dv_sb, dgc, dgs], axis=1)
    g["w_in"] = _mm(u, dproj, ta=True, name="mix_dwin")
    dh1, dh1b, g["g_mix"] = _dgrad_norm(dproj, w["w_in"], dh2, h1, w["g_mix"], "mix_du", copy_scale=0.5, after=emit("mix", g))
    (dx,), tok = ffn_bwd(dh1, dh1b, x, (n1, gu1, act1), "g_ffn1", "w_ffn1_gu", "w_ffn1_down", "ffn1", after=tick("mix", dh1))
    return loss_lanes, dx, g, tok


MATS = (("w_ffn1_gu", "col"), ("w_ffn1_down", "row"), ("w_in", "col"), ("w_conv_out", "row"), ("w_attn_out", "row"),
        ("w_o", "row"), ("w_cq", "row"), ("w_ckv", "col"), ("w_co", "row"), ("w_ffn2_gu", "col"), ("w_ffn2_down", "row"))
VECS = ("g_ffn1", "g_mix", "g_cross", "g_mem", "g_ffn2", "g_final")
WEIGHTS = ("g_ffn1", "w_ffn1_gu", "w_ffn1_down", "g_mix", "w_in", "b_gate", "conv_w", "w_conv_out", "w_attn_out", "w_o",
           "g_cross", "g_mem", "w_cq", "w_ckv", "w_co", "g_ffn2", "w_ffn2_gu", "w_ffn2_down", "g_final")
CONV_ROWS = 8


def _full_shape(kind, r, c):
    return (r, N_CHIPS * c) if kind == "col" else (N_CHIPS * r, c)


def _piece(ref, kind, r, c, chip, half):
    hr = r // 2
    if kind == "col":
        return ref.at[pl.ds(pl.multiple_of(half * hr, 16), hr), pl.ds(pl.multiple_of(chip * c, LANES), c)]
    return ref.at[pl.ds(pl.multiple_of(chip * r + half * hr, 16), hr), :]


def _shard_of(ref, kind, r, c, chip):
    if kind == "col":
        return ref.at[:, pl.ds(pl.multiple_of(chip * c, LANES), c)]
    return ref.at[pl.ds(pl.multiple_of(chip * r, 16), r), :]


def _place():
    x, y, c = lax.axis_index("x"), lax.axis_index("y"), lax.axis_index("c")
    others = [(1 - x, y), (x, 1 - y), (1 - x, 1 - y)]
    return x, y, c, 2 * x + y, others


def _remote(src, dst, send_sem, recv_sem, to):
    return pltpu.make_async_remote_copy(src_ref=src, dst_ref=dst, send_sem=send_sem, recv_sem=recv_sem,
                                        device_id=to, device_id_type=MESH)


def _gather_conv(conv_shard):
    cc = conv_shard.shape[1]

    def body(conv_ref, conv_full, cs, cr, cl):
        x, y, c, me, others = _place()

        def cols(chip):
            return conv_full.at[:, pl.ds(pl.multiple_of(chip * cc, LANES), cc)]

        def conv(k, chip_from, to):
            return _remote(conv_ref, cols(chip_from), cs.at[k], cr.at[k], to)

        mine = pltpu.make_async_copy(conv_ref, cols(me), cl.at[0])
        mine.start()
        for k, (ox, oy) in enumerate(others):
            conv(k, me, (ox, oy, c)).start()
        for k, (ox, oy) in enumerate(others):
            conv(k, 2 * ox + oy, (x, y, c)).wait_recv()
            conv(k, me, (ox, oy, c)).wait_send()
        mine.wait()

    dma = pltpu.SemaphoreType.DMA
    return _pcall(
        body, name="gather_conv", in_specs=[ANY], out_specs=ANY,
        out_shape=jax.ShapeDtypeStruct((CONV_ROWS, N_CHIPS * cc), F32), scratch_shapes=[dma((3,)), dma((3,)), dma((1,))],
    )(conv_shard)


HBM = pl.BlockSpec(memory_space=pltpu.HBM)
SEM = pl.BlockSpec(memory_space=pltpu.SEMAPHORE)
EFFECT = pltpu.SideEffectType.DATAFLOW_SIDE_EFFECTING
TOKEN = (8, LANES)


def _split_start(name, plan, n_copies, srcs, lands, after=None):
    ns, nl = len(srcs), len(lands)
    n_in = ns + nl + (after is not None)

    def body(*refs):
        outs = refs[n_in:]
        sends, _ = plan(refs[:ns], refs[ns:ns + nl], outs[0], outs[1])
        for cp in sends:
            cp.start()
        outs[-1][...] = jnp.zeros(TOKEN, F32)

    held = [pltpu.HBM(a.shape, a.dtype) for a in (*srcs, *lands)]
    dma = pltpu.SemaphoreType.DMA((n_copies,))
    ins = [pltpu.with_memory_space_constraint(a, pltpu.HBM) for a in (*srcs, *lands)]
    outs = _pcall(
        body, name=name, in_specs=[HBM] * (ns + nl) + ([] if after is None else [ANY]),
        out_specs=(SEM, SEM, *[HBM] * (ns + nl), pl.BlockSpec(memory_space=pltpu.VMEM)),
        out_shape=(dma, dma, *held, jax.ShapeDtypeStruct(TOKEN, F32)),
        input_output_aliases={i: 2 + i for i in range(ns + nl)},
        compiler_params=pltpu.CompilerParams(has_side_effects=EFFECT),
    )(*ins, *([] if after is None else [after]))
    return outs[0], outs[1], list(outs[2:2 + ns]), list(outs[2 + ns:2 + ns + nl]), outs[-1]


def _split_wait(name, plan, send_sems, recv_sems, srcs, lands, after):
    ns, nl = len(srcs), len(lands)

    def body(*refs):
        sends, recvs = plan(refs[:ns], refs[ns:ns + nl], refs[ns + nl], refs[ns + nl + 1])
        for cp in sends:
            cp.wait_send()
        for cp in recvs:
            cp.wait_recv()

    outs = _pcall(
        body, name=name, in_specs=[HBM] * (ns + nl) + [SEM, SEM, ANY], out_specs=[HBM] * (ns + nl),
        out_shape=[pltpu.HBM(a.shape, a.dtype) for a in (*srcs, *lands)],
        input_output_aliases={i: i for i in range(ns + nl)},
        compiler_params=pltpu.CompilerParams(has_side_effects=EFFECT),
    )(*srcs, *lands, send_sems, recv_sems, after)
    return list(outs[:ns]), list(outs[ns:])


def _gather_plan(dims):
    def plan(shard_refs, full_refs, ss, rs):
        x, y, c, me, others = _place()
        sends, recvs = [], []
        for wi, (kind, r, cw) in enumerate(dims):
            half = shard_refs[wi].at[pl.ds(pl.multiple_of(c * (r // 2), 16), r // 2), :]
            for k, (ox, oy) in enumerate(others):
                sem = 4 * wi + k
                sends.append(_remote(half, _piece(full_refs[wi], kind, r, cw, me, c), ss.at[sem], rs.at[sem], (ox, oy, c)))
                recvs.append(_remote(half, _piece(full_refs[wi], kind, r, cw, 2 * ox + oy, c), ss.at[sem], rs.at[sem], (x, y, c)))
            sem = 4 * wi + 3
            own = _remote(shard_refs[wi], _shard_of(full_refs[wi], kind, r, cw, me), ss.at[sem], rs.at[sem], (x, y, 1 - c))
            sends.append(own)
            recvs.append(own)
        return sends, recvs

    return plan


def _forward_plan(dims):
    def plan(_, full_refs, ss, rs):
        x, y, c, _, others = _place()
        sends, recvs = [], []
        for wi, (kind, r, cw) in enumerate(dims):
            for k, (ox, oy) in enumerate(others):
                sem = 3 * wi + k
                mine = _piece(full_refs[wi], kind, r, cw, 2 * ox + oy, c)
                theirs = _piece(full_refs[wi], kind, r, cw, 2 * ox + oy, 1 - c)
                sends.append(_remote(mine, mine, ss.at[sem], rs.at[sem], (x, y, 1 - c)))
                recvs.append(_remote(theirs, theirs, ss.at[sem], rs.at[sem], (x, y, 1 - c)))
        return sends, recvs

    return plan


def _rs_cores_plan(dims):
    def plan(g_refs, land_refs, ss, rs):
        x, y, c, _, _ = _place()
        sends, recvs = [], []
        for wi, dm in enumerate(dims):
            for chip in range(N_CHIPS):
                sem = N_CHIPS * wi + chip
                sends.append(_remote(_piece(g_refs[wi], *dm, chip, 1 - c), land_refs[wi].at[chip], ss.at[sem], rs.at[sem], (x, y, 1 - c)))
                recvs.append(_remote(_piece(g_refs[wi], *dm, chip, c), land_refs[wi].at[chip], ss.at[sem], rs.at[sem], (x, y, 1 - c)))
        return sends, recvs

    return plan


def _share_plan(nw):
    def plan(_, buf_refs, ss, rs):
        x, y, c, _, _ = _place()
        sends = [_remote(buf_refs[wi].at[c], buf_refs[wi].at[c], ss.at[wi], rs.at[wi], (x, y, 1 - c)) for wi in range(nw)]
        recvs = [_remote(buf_refs[wi].at[1 - c], buf_refs[wi].at[1 - c], ss.at[wi], rs.at[wi], (x, y, 1 - c)) for wi in range(nw)]
        return sends, recvs

    return plan


def _small_plan():
    def plan(_, buf_refs, ss, rs):
        x, y, c = lax.axis_index("x"), lax.axis_index("y"), lax.axis_index("c")
        buf = buf_refs[0]
        sends, recvs = [], []
        for rel in range(1, N_DEV):
            peer = (x ^ (rel >> 2 & 1), y ^ (rel >> 1 & 1), c ^ (rel & 1))
            sends.append(_remote(buf.at[0], buf.at[rel], ss.at[rel - 1], rs.at[rel - 1], peer))
            recvs.append(_remote(buf.at[0], buf.at[rel], ss.at[rel - 1], rs.at[rel - 1], peer))
        return sends, recvs

    return plan


def _sum_small(buf, me, name):
    _, rows, n = buf.shape

    def body(me_ref, b_ref, o_ref):
        tot = b_ref[me_ref[0]]
        for dev in range(1, N_DEV):
            tot = tot + b_ref[dev ^ me_ref[0]]
        o_ref[...] = tot

    return _pcall(
        body, name=name, out_shape=jax.ShapeDtypeStruct((rows, n), F32),
        grid_spec=pltpu.PrefetchScalarGridSpec(
            num_scalar_prefetch=1, grid=(1,), in_specs=[pl.BlockSpec((N_DEV, rows, n), lambda i, m: (0, 0, 0))],
            out_specs=pl.BlockSpec((rows, n), lambda i, m: (0, 0))),
    )(me, buf)


def _rs_chips_plan(nw):
    def plan(p_refs, land_refs, ss, rs):
        x, y, c, me, others = _place()
        sends, recvs = [], []
        for wi in range(nw):
            for k, (ox, oy) in enumerate(others):
                sem = 3 * wi + k
                sends.append(_remote(p_refs[wi].at[2 * ox + oy], land_refs[wi].at[k], ss.at[sem], rs.at[sem], (ox, oy, c)))
                recvs.append(_remote(p_refs[wi].at[me], land_refs[wi].at[k], ss.at[sem], rs.at[sem], (x, y, c)))
        return sends, recvs

    return plan


def _rows_per_block(n, c, limit_bytes=2 << 20):
    best = None
    for tm in range(16, n + 1, 16):
        if n % tm == 0 and tm * c * 4 <= limit_bytes:
            best = tm
    return best or n


def _sum_cores(grad, got, kind, place, name):
    _, hr, cw = got.shape
    tm = _rows_per_block(hr, cw)
    nb = hr // tm

    def body(place_ref, g_ref, t_ref, o_ref):
        o_ref[...] = (g_ref[...].astype(F32) + t_ref[...].astype(F32)).astype(o_ref.dtype)

    if kind == "col":
        g_spec = pl.BlockSpec((tm, cw), lambda j, i, pr: (pr[0] * nb + i, j))
    else:
        g_spec = pl.BlockSpec((tm, cw), lambda j, i, pr: ((2 * j + pr[0]) * nb + i, 0))
    blk = pl.BlockSpec((None, tm, cw), lambda j, i, pr: (j, i, 0))
    return _pcall(
        body, name=name, out_shape=jax.ShapeDtypeStruct(got.shape, BF16),
        grid_spec=pltpu.PrefetchScalarGridSpec(num_scalar_prefetch=1, grid=(N_CHIPS, nb), in_specs=[g_spec, blk], out_specs=blk),
        compiler_params=_params("parallel", "parallel"),
    )(place, grad, got)


def _sum_chips(parts, got, place, name):
    _, n, cw = got.shape
    tm = _rows_per_block(n, cw)

    def body(place_ref, p_ref, g_ref, o_ref):
        tot = p_ref[...].astype(F32)
        for k in range(3):
            tot = tot + g_ref[k].astype(F32)
        o_ref[...] = tot

    return _pcall(
        body, name=name, out_shape=jax.ShapeDtypeStruct((2, n, cw), F32),
        grid_spec=pltpu.PrefetchScalarGridSpec(
            num_scalar_prefetch=1, grid=(n // tm,),
            in_specs=[pl.BlockSpec((None, tm, cw), lambda i, pr: (pr[1], i, 0)), pl.BlockSpec((3, tm, cw), lambda i, pr: (0, i, 0))],
            out_specs=pl.BlockSpec((None, tm, cw), lambda i, pr: (pr[0], i, 0))),
        compiler_params=_params("parallel"),
    )(place, parts, got)


def _adamw(g, w, m, v, name):
    n, c = g.shape
    c1 = 1.0 - ADAM_B1 ** ADAM_STEP
    c2 = 1.0 - ADAM_B2 ** ADAM_STEP

    def fn(gb, wb, mb, vb):
        m_new = ADAM_B1 * mb + (1.0 - ADAM_B1) * gb
        v_new = ADAM_B2 * vb + (1.0 - ADAM_B2) * (gb * gb)
        delta = -ADAM_LR * ((m_new / c1) / (jnp.sqrt(v_new / c2) + ADAM_EPS) + ADAM_WD * wb)
        return gb, delta, m_new, v_new

    tm = _rows_per_block(n, c) if n % 16 == 0 else n
    return _rowcall(fn, [_whole(g), _whole(w), _whole(m), _whole(v)], [], [(c, F32)] * 4, tm=tm, name=name)


PACK_ROWS = 16


def _pack_rows(parts, width, name, after=None):
    assert sum(p.shape[0] for p in parts) <= PACK_ROWS

    def body(*refs):
        out_ref = refs[-1]
        out_ref[...] = jnp.zeros_like(out_ref)
        at = 0
        for r in refs[:len(parts)]:
            k, n = r.shape
            if n == width:
                out_ref[at:at + k, :] = r[...]
            else:
                out_ref[at:at + k, :] = jnp.broadcast_to(r[:, :1], (k, width))
            at += k

    vm = pl.BlockSpec(memory_space=pltpu.VMEM)
    return _pcall(body, name=name, in_specs=[vm] * len(parts) + ([] if after is None else [ANY]), out_specs=vm,
                  out_shape=jax.ShapeDtypeStruct((PACK_ROWS, width), F32))(*parts, *([] if after is None else [after]))


def _cast_shard(wm, name, after):
    n, c = wm.shape
    return _rowcall(lambda v: v, [_whole(wm)], [], [(c, BF16)], tm=_rows_per_block(n, c), name=name, after=after)[0]


GATHER_GROUPS = (
    ("w_ffn1_gu",), ("w_ffn1_down",), ("w_in",), ("w_conv_out", "w_attn_out", "w_o"), ("w_cq", "w_ckv", "w_co"),
    ("w_ffn2_gu", "w_ffn2_down"),
)
REDUCE_GROUPS = {
    "ffn2": ("w_ffn2_down", "w_ffn2_gu"),
    "cross": ("w_co", "w_cq", "w_ckv"),
    "mix": ("w_o", "w_conv_out", "w_attn_out", "w_in"),
    "ffn1_down": ("w_ffn1_down",),
    "ffn1": ("w_ffn1_gu",),
}
KIND = dict(MATS)


def _step(x, mem, tgt, wts, m_in, v_in):
    d = x.shape[-1]
    cc = wts["conv_w"].shape[1]
    place = jnp.stack([lax.axis_index("c"), 2 * lax.axis_index("x") + lax.axis_index("y")]).astype(jnp.int32)
    dims = {n: (kind, *wts[n].shape) for n, kind in MATS}

    conv_full = _gather_conv(jnp.pad(wts["conv_w"], ((0, CONV_ROWS - CONV_K), (0, 0))))
    w = {n: wts[n].reshape(1, -1) for n in VECS + ("b_gate",)}
    w["conv_w"] = conv_full[:CONV_K]
    flying, token = {}, conv_full
    for names in GATHER_GROUPS:
        gd = [dims[n] for n in names]
        shards = [_cast_shard(wts[n], "cast_" + n, token) for n in names]
        lands = [lax.empty(_full_shape(*dm), BF16) for dm in gd]
        plan = _gather_plan(gd)
        ss, rs, srcs, lands, token = _split_start("gather_start_" + names[0], plan, 4 * len(names), shards, lands, token)
        flying.update({n: (names, plan, ss, rs, srcs, lands, gd) for n in names})

    passing = {}

    def prefetch(name, after):
        if name not in passing:
            names, plan, ss, rs, srcs, lands, gd = flying[name]
            _, lands = _split_wait("gather_wait_" + names[0], plan, ss, rs, srcs, lands, after)
            plan = _forward_plan(gd)
            ss, rs, _, lands, _ = _split_start("forward_start_" + names[0], plan, 3 * len(names), [], lands)
            passing.update({n: (names, plan, ss, rs, lands) for n in names})

    def fetch(name, after):
        prefetch(name, after)
        names, plan, ss, rs, lands = passing[name]
        _, lands = _split_wait("forward_wait_" + names[0], plan, ss, rs, [], lands, after)
        return dict(zip(names, lands))

    swapping, sent = {}, {}

    def emit(tag, g):
        if tag not in REDUCE_GROUPS:
            return None
        names = REDUCE_GROUPS[tag]
        gd = [dims[n] for n in names]
        lands = [lax.empty((N_CHIPS, r // 2, cw), BF16) for (_, r, cw) in gd]
        plan = _rs_cores_plan(gd)
        ss, rs, srcs, lands, tok = _split_start("rs_cores_start_" + tag, plan, N_CHIPS * len(names), [g[n] for n in names], lands)
        swapping[tag] = (plan, ss, rs, srcs, lands)
        return tok

    def tick(tag, after):
        if tag not in REDUCE_GROUPS:
            return None
        names = REDUCE_GROUPS[tag]
        plan, ss, rs, srcs, lands = swapping[tag]
        mine, got = _split_wait("rs_cores_wait_" + tag, plan, ss, rs, srcs, lands, after)
        parts = [_sum_cores(gm, t, KIND[n], place, "sum_cores_" + n) for n, gm, t in zip(names, mine, got)]
        lands = [lax.empty((3, *p.shape[1:]), BF16) for p in parts]
        plan = _rs_chips_plan(len(names))
        ss, rs, srcs, lands, tok = _split_start("rs_chips_start_" + tag, plan, 3 * len(names), parts, lands)
        sent[tag] = (plan, ss, rs, srcs, lands)
        return tok

    loss_lanes, dx, g, last = _local_step(x[0], mem[0], tgt[0], w, fetch, prefetch, emit, tick, token)

    rows = [g[n] for n in VECS] + [g["b_gate"][:, :d], g["b_gate"][:, d:], g["conv_w"], loss_lanes]
    packed = _pack_rows(rows, d, "pack_small", after=last)
    small = jnp.concatenate([packed[None], jnp.zeros((N_DEV - 1, *packed.shape), F32)], axis=0)
    small_plan = _small_plan()
    small_ss, small_rs, _, small, after = _split_start("small_start", small_plan, N_DEV - 1, [], [small])

    grads, out = {}, {}

    def update(n):
        shape = wts[n].shape
        as2d = (lambda a: a.reshape(1, -1)) if len(shape) == 1 else (lambda a: a)
        return [r.reshape(shape) for r in _adamw(grads[n], as2d(wts[n]), as2d(m_in[n]), as2d(v_in[n]), "adamw_" + n)]

    def finish(sharing, after):
        tag, names, plan, ss, rs, halves = sharing
        _, both = _split_wait("share_wait_" + tag, plan, ss, rs, [], halves, after)
        for n, b in zip(names, both):
            grads[n] = b.reshape(-1, b.shape[-1])
            out[n] = update(n)
        return out[names[-1]][1]

    sharing = None
    for tag, names in REDUCE_GROUPS.items():
        plan, ss, rs, srcs, lands = sent[tag]
        parts, landed = _split_wait("rs_chips_wait_" + tag, plan, ss, rs, srcs, lands, after)
        halves = [_sum_chips(p, t, place, "sum_chips_" + n) for n, p, t in zip(names, parts, landed)]
        plan = _share_plan(len(names))
        ss, rs, _, halves, after = _split_start("share_start_" + tag, plan, len(names), [], halves)
        if sharing is not None:
            after = finish(sharing, after)
        sharing = (tag, names, plan, ss, rs, halves)
    after = finish(sharing, after)

    _, small = _split_wait("small_wait", small_plan, small_ss, small_rs, [], small, after)
    me = (4 * lax.axis_index("x") + 2 * lax.axis_index("y") + lax.axis_index("c")).astype(jnp.int32).reshape(1)
    red = _sum_small(small[0], me, "sum_small")
    grads.update({n: red[i:i + 1] for i, n in enumerate(VECS)})
    nv = len(VECS)
    grads["b_gate"] = jnp.concatenate([red[nv:nv + 1], red[nv + 1:nv + 2]], axis=1)
    chip = 2 * lax.axis_index("x") + lax.axis_index("y")
    grads["conv_w"] = lax.dynamic_slice_in_dim(red[nv + 2:nv + 2 + CONV_K], chip * cc, cc, axis=1)
    loss = red[nv + 2 + CONV_K, 0]
    out.update({n: update(n) for n in WEIGHTS if n not in KIND})
    return (loss, dx[None], *[out[n][0] for n in WEIGHTS], *[out[n][1] for n in WEIGHTS],
            *[out[n][2] for n in WEIGHTS], *[out[n][3] for n in WEIGHTS])


def kernel(x, mem, g_ffn1, w_ffn1_gu, w_ffn1_down, g_mix, w_in, b_gate, conv_w, w_conv_out, w_attn_out, w_o, g_cross, g_mem, w_cq, w_ckv, w_co, g_ffn2, w_ffn2_gu, w_ffn2_down, g_final, loss_target, m_g_ffn1, m_w_ffn1_gu, m_w_ffn1_down, m_g_mix, m_w_in, m_b_gate, m_conv_w, m_w_conv_out, m_w_attn_out, m_w_o, m_g_cross, m_g_mem, m_w_cq, m_w_ckv, m_w_co, m_g_ffn2, m_w_ffn2_gu, m_w_ffn2_down, m_g_final, v_g_ffn1, v_w_ffn1_gu, v_w_ffn1_down, v_g_mix, v_w_in, v_b_gate, v_conv_w, v_w_conv_out, v_w_attn_out, v_w_o, v_g_cross, v_g_mem, v_w_cq, v_w_ckv, v_w_co, v_g_ffn2, v_w_ffn2_gu, v_w_ffn2_down, v_g_final):
    given = dict(locals())
    wts = {n: given[n] for n in WEIGHTS}
    m_in = {n: given["m_" + n] for n in WEIGHTS}
    v_in = {n: given["v_" + n] for n in WEIGHTS}
    return _step(x, mem, loss_target, wts, m_in, v_in)
```

```python
import functools

import jax
import jax.numpy as jnp
from jax import lax
from jax.experimental import pallas as pl
from jax.experimental.pallas import tpu as pltpu

F32 = jnp.float32
BF16 = jnp.bfloat16
MESH = pl.DeviceIdType.MESH

V7X_VMEM_LIMIT_BYTES = 48 * 1024 * 1024
MM_VMEM_BUDGET_BYTES = 36 * 1024 * 1024
MM_WHOLE_K = 2816
LANES = 128
SB_HEAD_DIM = 128
X_HEADS = 4
CONV_K = 3
RMS_EPS = 1e-6
N_CHIPS = 4
N_DEV = 8
ADAM_LR, ADAM_B1, ADAM_B2, ADAM_EPS, ADAM_WD, ADAM_STEP = 0.001, 0.9, 0.999, 1e-08, 0.01, 10


ANY = pl.BlockSpec(memory_space=pl.ANY)


def _pcall(body, **kw):
    return pl.pallas_call(body, **kw)


def _params(*sem):
    return pltpu.CompilerParams(dimension_semantics=sem, vmem_limit_bytes=V7X_VMEM_LIMIT_BYTES)


def _pick(dim, cands):
    for c in cands:
        if dim % c == 0:
            return c
    return dim


def _dot(a, b, ca, cb):
    return lax.dot_general(a, b, (((ca,), (cb,)), ((), ())), preferred_element_type=F32)


def _mm(a, b, *, name, ta=False, tb=False, out_dtype=BF16, res=None, alpha=1.0, tm=None, tn=None, tk=None, after=None,
        a_halves=False, b_halves=False):
    assert not (a_halves and ta) and not (b_halves and tb)
    if a_halves:
        m, k = a.shape[1], 2 * a.shape[2]
    else:
        m, k = (a.shape[1], a.shape[0]) if ta else a.shape
    if b_halves:
        n = 2 * b.shape[2]
        assert k == b.shape[1]
    else:
        n = b.shape[0] if tb else b.shape[1]
        assert k == (b.shape[1] if tb else b.shape[0]), (a.shape, b.shape, ta, tb)
    if ta:
        tm = tm or _pick(m, (512, 256, 128))
        tn = tn or _pick(n, (1024, 512, 256, 128))
        tk = tk or (k if k <= MM_WHOLE_K else _pick(k, (1024, 512, 256, 128)))
    else:
        tk = tk or (k if k <= MM_WHOLE_K else _pick(k, (MM_WHOLE_K, 2048, 1024, 512, 256, 128)))
        tn = tn or _pick(n, (512, 1408, 256, 128) if tk == k else (1024, 512, 256, 128))
        per_row = 2 * (tk * a.dtype.itemsize + tn * (jnp.dtype(out_dtype).itemsize + (0 if res is None else res.dtype.itemsize)))
        per_row += 4 * tn if tk < k else 0
        rows = (MM_VMEM_BUDGET_BYTES - 2 * tk * tn * b.dtype.itemsize) // per_row
        tm = tm or next((c for c in (2048, 1024, 512, 256, 128) if m % c == 0 and c <= rows), m)
    if a_halves:
        tk = min(tk, k // 2) if (k // 2) % min(tk, k // 2) == 0 else _pick(k // 2, (1408, 1024, 512, 256, 128))
    if b_halves:
        tn = tn if (n // 2) % tn == 0 else _pick(n // 2, (1408, 1024, 512, 256, 128))
    nk = k // tk
    assert m % tm == 0 and n % tn == 0 and k % tk == 0
    a_spec = pl.BlockSpec((tk, tm), lambda i, j, kk: (kk, i)) if ta else pl.BlockSpec((tm, tk), lambda i, j, kk: (i, kk))
    b_spec = pl.BlockSpec((tn, tk), lambda i, j, kk: (j, kk)) if tb else pl.BlockSpec((tk, tn), lambda i, j, kk: (kk, j))
    if a_halves:
        per = (k // 2) // tk
        a_spec = pl.BlockSpec((None, tm, tk), lambda i, j, kk: (kk // per, i, kk % per))
    if b_halves:
        per_n = (n // 2) // tn
        b_spec = pl.BlockSpec((None, tk, tn), lambda i, j, kk: (j // per_n, kk, j % per_n))
    o_spec = pl.BlockSpec((tm, tn), lambda i, j, kk: (i, j))
    ca, cb = (0 if ta else 1), (1 if tb else 0)

    n_in = 2 + (res is not None) + (after is not None)

    def body(*refs):
        a_ref, b_ref = refs[:2]
        res_ref = refs[2] if res is not None else None
        o_ref = refs[n_in]
        scratch = refs[n_in + 1:]

        def finish(acc):
            val = acc if alpha == 1.0 else alpha * acc
            if res_ref is not None:
                val = res_ref[...].astype(F32) + val
            o_ref[...] = val.astype(o_ref.dtype)

        part = _dot(a_ref[...].astype(BF16), b_ref[...].astype(BF16), ca, cb)
        if nk == 1:
            finish(part)
        else:
            acc_ref = scratch[0]
            kk = pl.program_id(2)

            @pl.when(kk == 0)
            def _():
                acc_ref[...] = part

            @pl.when(kk > 0)
            def _():
                acc_ref[...] += part

            @pl.when(kk == nk - 1)
            def _():
                finish(acc_ref[...])

    ins = [a, b] + ([] if res is None else [res]) + ([] if after is None else [after])
    in_specs = [a_spec, b_spec] + ([] if res is None else [o_spec]) + ([] if after is None else [ANY])
    return _pcall(
        body, name=name, grid=(m // tm, n // tn, nk), in_specs=in_specs, out_specs=o_spec,
        out_shape=jax.ShapeDtypeStruct((m, n), out_dtype),
        scratch_shapes=[pltpu.VMEM((tm, tn), F32)] if nk > 1 else [],
        compiler_params=_params("parallel", "parallel", "arbitrary"),
    )(*ins)


def _rowcall(fn, rows, consts, outs, accs=(), *, tm, name, after=None):
    s = rows[0][0].shape[0]
    assert s % tm == 0
    n_read, n_out = len(rows) + len(consts), len(outs)
    n_in = n_read + (after is not None)

    def body(*refs):
        vals = fn(*[r[...] for r in refs[:n_read]])
        vals = vals if isinstance(vals, (tuple, list)) else (vals,)
        for o_ref, v in zip(refs[n_in:n_in + n_out], vals[:n_out]):
            o_ref[...] = v.astype(o_ref.dtype)
        if accs:
            first = pl.program_id(0) == 0
            for a_ref, v in zip(refs[n_in + n_out:], vals[n_out:]):
                tot = jnp.sum(v.astype(F32), axis=0, keepdims=True)

                @pl.when(first)
                def _(a_ref=a_ref, tot=tot):
                    a_ref[...] = tot

                @pl.when(jnp.logical_not(first))
                def _(a_ref=a_ref, tot=tot):
                    a_ref[...] += tot

    in_specs = [pl.BlockSpec((tm, w), lambda i, cb=cb: (i, cb)) for (_, cb, w) in rows]
    in_specs += [pl.BlockSpec(c.shape, lambda i: (0, 0)) for c in consts]
    in_specs += [] if after is None else [ANY]
    out_specs = [pl.BlockSpec((tm, w), lambda i: (i, 0)) for (w, _) in outs]
    out_specs += [pl.BlockSpec((1, w), lambda i: (0, 0)) for w in accs]
    out_shape = [jax.ShapeDtypeStruct((s, w), dt) for (w, dt) in outs]
    out_shape += [jax.ShapeDtypeStruct((1, w), F32) for w in accs]
    return _pcall(
        body, name=name, grid=(s // tm,), in_specs=in_specs, out_specs=out_specs, out_shape=out_shape,
        compiler_params=_params("arbitrary" if accs else "parallel"),
    )(*[r[0] for r in rows], *consts, *([] if after is None else [after]))


def _whole(a):
    return (a, 0, a.shape[1])


def _xhat(x):
    x = x.astype(F32)
    r = lax.rsqrt(jnp.mean(x * x, axis=-1, keepdims=True) + RMS_EPS)
    return x * r, r


def _rms_bwd(dy, x, g):
    xh, r = _xhat(x)
    dxh = dy.astype(F32) * g
    dx = r * (dxh - xh * jnp.mean(dxh * xh, axis=-1, keepdims=True))
    return dx, dy.astype(F32) * xh


def _sigmoid(x):
    return 1.0 / (1.0 + jnp.exp(-x))


def _rms_fwd(x, g, name, tm, after=None):
    d = x.shape[1]
    return _rowcall(lambda xb, gb: _xhat(xb)[0] * gb, [_whole(x)], [g], [(d, BF16)], tm=tm, name=name, after=after)[0]


def _silu_parts(gate):
    sg = _sigmoid(gate)
    return sg, gate * sg


def _ffn_up(n, w_gu, name):
    s, d = n.shape
    f = w_gu.shape[1] // 2
    tn = _pick(f, (1408, 1024, 512, 256, 128))
    tm = _pick(s, (1024, 512, 256, 128))
    nb = f // tn

    def body(n_ref, wg_ref, wu_ref, gu_ref, act_ref):
        nv = n_ref[...]
        gate = _dot(nv, wg_ref[...], 1, 0)
        up = _dot(nv, wu_ref[...], 1, 0)
        gu_ref[0] = gate.astype(gu_ref.dtype)
        gu_ref[1] = up.astype(gu_ref.dtype)
        act_ref[...] = (_silu_parts(gate)[1] * up).astype(act_ref.dtype)

    return _pcall(
        body, name=name, grid=(s // tm, nb),
        in_specs=[pl.BlockSpec((tm, d), lambda i, j: (i, 0)), pl.BlockSpec((d, tn), lambda i, j: (0, j)),
                  pl.BlockSpec((d, tn), lambda i, j: (0, nb + j))],
        out_specs=[pl.BlockSpec((2, tm, tn), lambda i, j: (0, i, j)), pl.BlockSpec((tm, tn), lambda i, j: (i, j))],
        out_shape=[jax.ShapeDtypeStruct((2, s, f), BF16), jax.ShapeDtypeStruct((s, f), BF16)],
        compiler_params=_params("parallel", "parallel"),
    )(n, w_gu, w_gu)


def _ffn_dgu(dhb, w_down, gu, name, after=None):
    s, d = dhb.shape
    f = w_down.shape[0]
    tn = _pick(f, (1408, 1024, 512, 256, 128))
    tm = _pick(s, (1024, 512, 256, 128))

    def body(dh_ref, w_ref, gu_ref, *rest):
        o_ref = rest[-1]
        dact = _dot(dh_ref[...], w_ref[...], 1, 1)
        gate, up = gu_ref[0].astype(F32), gu_ref[1].astype(F32)
        sg, silu = _silu_parts(gate)
        o_ref[0] = (dact * up * (sg + silu * (1.0 - sg))).astype(o_ref.dtype)
        o_ref[1] = (dact * silu).astype(o_ref.dtype)

    blk = pl.BlockSpec((2, tm, tn), lambda i, j: (0, i, j))
    return _pcall(
        body, name=name, grid=(s // tm, f // tn),
        in_specs=[pl.BlockSpec((tm, d), lambda i, j: (i, 0)), pl.BlockSpec((tn, d), lambda i, j: (j, 0)), blk]
        + ([] if after is None else [ANY]),
        out_specs=blk, out_shape=jax.ShapeDtypeStruct((2, s, f), BF16), compiler_params=_params("parallel", "parallel"),
    )(dhb, w_down, gu, *([] if after is None else [after]))


def _dgrad_norm(dy, wmat, dh, x, g, name, *, dy_halves=False, copy_scale=None, after=None):
    s, d = dh.shape
    k = wmat.shape[1]
    tk = k if k <= MM_WHOLE_K else _pick(k, (MM_WHOLE_K, 2048, 1024, 512, 256, 128))
    if dy_halves and (k // 2) % tk:
        tk = _pick(k // 2, (1408, 1024, 512, 256, 128))
    tm = _pick(s, (512, 256, 128))
    nk, per = k // tk, (k // 2) // tk if dy_halves else 0
    n_in = 5 + (after is not None)
    n_out = 2 + (copy_scale is not None)

    def body(*refs):
        dy_ref, w_ref, dh_ref, x_ref, g_ref = refs[:5]
        outs, scratch = refs[n_in:n_in + n_out], refs[n_in + n_out:]
        i, kk = pl.program_id(0), pl.program_id(1)
        part = _dot(dy_ref[...], w_ref[...], 1, 1)

        def finish(dn):
            dx, dg = _rms_bwd(dn, x_ref[...], g_ref[...])
            tot = dh_ref[...] + dx
            outs[0][...] = tot
            if copy_scale is not None:
                outs[1][...] = (copy_scale * tot).astype(outs[1].dtype)
            dg = jnp.sum(dg, axis=0, keepdims=True)

            @pl.when(i == 0)
            def _():
                outs[-1][...] = dg

            @pl.when(i > 0)
            def _():
                outs[-1][...] += dg

        if nk == 1:
            finish(part)
        else:
            acc_ref = scratch[0]

            @pl.when(kk == 0)
            def _():
                acc_ref[...] = part

            @pl.when(kk > 0)
            def _():
                acc_ref[...] += part

            @pl.when(kk == nk - 1)
            def _():
                finish(acc_ref[...])

    row = pl.BlockSpec((tm, d), lambda i, kk: (i, 0))
    dy_spec = pl.BlockSpec((None, tm, tk), lambda i, kk: (kk // per, i, kk % per)) if dy_halves else pl.BlockSpec((tm, tk), lambda i, kk: (i, kk))
    in_specs = [dy_spec, pl.BlockSpec((d, tk), lambda i, kk: (0, kk)), row, row, pl.BlockSpec((1, d), lambda i, kk: (0, 0))]
    out_specs = [row] * (n_out - 1) + [pl.BlockSpec((1, d), lambda i, kk: (0, 0))]
    out_shape = [jax.ShapeDtypeStruct((s, d), F32)] + ([] if copy_scale is None else [jax.ShapeDtypeStruct((s, d), BF16)])
    return _pcall(
        body, name=name, grid=(s // tm, nk), in_specs=in_specs + ([] if after is None else [ANY]), out_specs=out_specs,
        out_shape=out_shape + [jax.ShapeDtypeStruct((1, d), F32)], scratch_shapes=[pltpu.VMEM((tm, d), F32)] if nk > 1 else [],
        compiler_params=_params("arbitrary", "arbitrary"),
    )(dy, wmat, dh, x, g, *([] if after is None else [after]))


def _shift_down(p, k):
    if k == 0:
        return p
    rows = lax.broadcasted_iota(jnp.int32, p.shape, 0)
    return jnp.where(rows >= k, pltpu.roll(p, k, 0), 0.0)


def _shift_up(p, k):
    if k == 0:
        return p
    s = p.shape[0]
    rows = lax.broadcasted_iota(jnp.int32, p.shape, 0)
    return jnp.where(rows < s - k, pltpu.roll(p, s - k, 0), 0.0)


def _conv_fwd(proj, conv_w, d, tc, name):
    s = proj.shape[0]
    nb = d // tc

    def body(cb_ref, cc_ref, cx_ref, w_ref, y_ref):
        p = cc_ref[...].astype(F32) * cx_ref[...].astype(F32)
        w = w_ref[...]
        acc = p * w[CONV_K - 1:CONV_K, :]
        for k in range(1, CONV_K):
            acc = acc + _shift_down(p, k) * w[CONV_K - 1 - k:CONV_K - k, :]
        y_ref[...] = (cb_ref[...].astype(F32) * acc).astype(y_ref.dtype)

    col = lambda off: pl.BlockSpec((s, tc), lambda j: (0, off * nb + j))
    return _pcall(
        body, name=name, grid=(nb,), in_specs=[col(0), col(1), col(2), pl.BlockSpec((CONV_K, tc), lambda j: (0, j))],
        out_specs=pl.BlockSpec((s, tc), lambda j: (0, j)), out_shape=jax.ShapeDtypeStruct((s, d), BF16),
        compiler_params=_params("parallel"),
    )(proj, proj, proj, conv_w)


def _conv_bwd(dy, proj, conv_w, d, tc, name):
    s = proj.shape[0]
    nb = d // tc

    def body(dy_ref, cb_ref, cc_ref, cx_ref, w_ref, dcb_ref, dcc_ref, dcx_ref, dw_ref):
        cc, cx = cc_ref[...].astype(F32), cx_ref[...].astype(F32)
        p = cc * cx
        w = w_ref[...]
        dyv = dy_ref[...].astype(F32)
        shifted = [_shift_down(p, CONV_K - 1 - k) for k in range(CONV_K)]
        conv = shifted[0] * w[0:1, :]
        for k in range(1, CONV_K):
            conv = conv + shifted[k] * w[k:k + 1, :]
        dcb_ref[...] = (dyv * conv).astype(dcb_ref.dtype)
        ds = dyv * cb_ref[...].astype(F32)
        dp = ds * w[CONV_K - 1:CONV_K, :]
        for k in range(1, CONV_K):
            dp = dp + _shift_up(ds, k) * w[CONV_K - 1 - k:CONV_K - k, :]
        dcc_ref[...] = (dp * cx).astype(dcc_ref.dtype)
        dcx_ref[...] = (dp * cc).astype(dcx_ref.dtype)
        for k in range(CONV_K):
            dw_ref[k:k + 1, :] = jnp.sum(ds * shifted[k], axis=0, keepdims=True)

    col = lambda off: pl.BlockSpec((s, tc), lambda j: (0, off * nb + j))
    blk = pl.BlockSpec((s, tc), lambda j: (0, j))
    wblk = pl.BlockSpec((CONV_K, tc), lambda j: (0, j))
    act = jax.ShapeDtypeStruct((s, d), BF16)
    return _pcall(
        body, name=name, grid=(nb,), in_specs=[blk, col(0), col(1), col(2), wblk],
        out_specs=[blk, blk, blk, wblk], out_shape=[act, act, act, jax.ShapeDtypeStruct((CONV_K, d), F32)],
        compiler_params=_params("parallel"),
    )(dy, proj, proj, proj, conv_w)


def _sb_tile(q, kj, scale, carry, tri, mask):
    z = _dot(q, kj, 1, 1) * scale
    lsz = jnp.minimum(z, 0.0) - jnp.log(1.0 + jnp.exp(-jnp.abs(z)))
    l1m = lsz - z
    if mask is not None:
        l1m = jnp.where(mask, l1m, 0.0)
    l1b = l1m.astype(BF16)
    a = jnp.exp(lsz + (carry + _dot(l1b, tri, 1, 0)))
    if mask is not None:
        a = jnp.where(mask, a, 0.0)
    return lsz, l1b, a.astype(BF16)


def _sb_masks(tq, tk):
    row = lax.broadcasted_iota(jnp.int32, (tq, tk), 0)
    col = lax.broadcasted_iota(jnp.int32, (tq, tk), 1)
    masks = [col + dj * tk < row for dj in range(tq // tk)]
    r2 = lax.broadcasted_iota(jnp.int32, (tk, tk), 0)
    c2 = lax.broadcasted_iota(jnp.int32, (tk, tk), 1)
    return masks, (r2 > c2).astype(BF16), (r2 < c2).astype(BF16)


def _sb_fwd(proj, heads, col0, tq, tk, name):
    s = proj.shape[0]
    dh = SB_HEAD_DIM
    nq, nd, nkt = s // tq, tq // tk, s // tk
    scale = dh ** -0.5

    def body(q_ref, k_ref, v_ref, o_ref, a_ref, b_ref):
        i = pl.program_id(1)
        q = q_ref[...]
        masks, tri_right, _ = _sb_masks(tq, tk)

        def tile(j, carry, acc, mask):
            start = pl.multiple_of(j * tk, tk)
            kj = k_ref[pl.ds(start, tk), :]
            vj = v_ref[pl.ds(start, tk), :]
            lsz, l1b, ab = _sb_tile(q, kj, scale, carry, tri_right, mask)
            a_ref[j] = ab
            b_ref[j] = jnp.exp(lsz).astype(b_ref.dtype)
            return carry + jnp.sum(l1b.astype(F32), axis=1, keepdims=True), acc + _dot(ab, vj, 1, 0)

        state = (jnp.zeros((tq, 1), F32), jnp.zeros((tq, dh), F32))
        for dj in reversed(range(nd)):
            state = tile(i * nd + dj, *state, masks[dj])
        def left_block(t, st):
            for dj in reversed(range(nd)):
                st = tile((i - 1 - t) * nd + dj, st[0], st[1], None)
            return st

        state = lax.fori_loop(0, i, left_block, state)
        o_ref[...] = state[1]

    qspec = pl.BlockSpec((tq, dh), lambda h, i: (i, col0[0] + h))
    kspec = pl.BlockSpec((s, dh), lambda h, i: (0, col0[1] + h))
    vspec = pl.BlockSpec((s, dh), lambda h, i: (0, col0[2] + h))
    saved = pl.BlockSpec((None, nkt, tq, tk), lambda h, i: (h, 0, i, 0))
    saved_shape = jax.ShapeDtypeStruct((heads, nkt, s, tk), BF16)
    return _pcall(
        body, name=name, grid=(heads, nq), in_specs=[qspec, kspec, vspec],
        out_specs=[pl.BlockSpec((tq, dh), lambda h, i: (i, h)), saved, saved],
        out_shape=[jax.ShapeDtypeStruct((s, heads * dh), F32), saved_shape, saved_shape],
        compiler_params=_params("parallel", "parallel"),
    )(proj, proj, proj)


def _sb_bwd(proj, o, a_all, beta_all, do, heads, col0, tq, tk, name):
    s = proj.shape[0]
    dh = SB_HEAD_DIM
    nq, nd, nkt = s // tq, tq // tk, s // tk
    scale = dh ** -0.5

    def body(q_ref, k_ref, v_ref, o_ref, a_ref, b_ref, do_ref, dq_ref, dk_ref, dv_ref, dk_acc, dv_acc):
        i = pl.program_id(1)

        @pl.when(i == 0)
        def _():
            dk_acc[...] = jnp.zeros_like(dk_acc)
            dv_acc[...] = jnp.zeros_like(dv_acc)

        q = q_ref[...]
        dob = do_ref[...].astype(BF16)
        delta = jnp.sum(dob.astype(F32) * o_ref[...], axis=1, keepdims=True)
        masks, _, tri_left = _sb_masks(tq, tk)

        def tile(j, carry_g, dq, mask):
            start = pl.multiple_of(j * tk, tk)
            kj = k_ref[pl.ds(start, tk), :]
            vj = v_ref[pl.ds(start, tk), :]
            ab = a_ref[j]
            g = _dot(dob, vj, 1, 1) * ab.astype(F32)
            carry_g = carry_g + jnp.sum(g, axis=1, keepdims=True)
            left = (delta - carry_g) + _dot(g.astype(BF16), tri_left, 1, 0)
            dz = g - b_ref[j].astype(F32) * (g + left)
            if mask is not None:
                dz = jnp.where(mask, dz, 0.0)
            dzb = dz.astype(BF16)
            dk_acc[pl.ds(start, tk), :] += _dot(dzb, q, 0, 0)
            dv_acc[pl.ds(start, tk), :] += _dot(ab, dob, 0, 0)
            return carry_g, dq + _dot(dzb, kj, 1, 0)

        state = (jnp.zeros((tq, 1), F32), jnp.zeros((tq, dh), F32))
        for dj in reversed(range(nd)):
            state = tile(i * nd + dj, *state, masks[dj])
        def left_block(t, st):
            for dj in reversed(range(nd)):
                st = tile((i - 1 - t) * nd + dj, st[0], st[1], None)
            return st

        state = lax.fori_loop(0, i, left_block, state)
        dq_ref[...] = (state[1] * scale).astype(dq_ref.dtype)

        @pl.when(i == nq - 1)
        def _():
            dk_ref[...] = (dk_acc[...] * scale).astype(dk_ref.dtype)
            dv_ref[...] = dv_acc[...].astype(dv_ref.dtype)

    qspec = pl.BlockSpec((tq, dh), lambda h, i: (i, col0[0] + h))
    kspec = pl.BlockSpec((s, dh), lambda h, i: (0, col0[1] + h))
    vspec = pl.BlockSpec((s, dh), lambda h, i: (0, col0[2] + h))
    blk = pl.BlockSpec((tq, dh), lambda h, i: (i, h))
    full = pl.BlockSpec((s, dh), lambda h, i: (0, h))
    saved = pl.BlockSpec((None, nkt, tq, tk), lambda h, i: (h, 0, i, 0))
    act = jax.ShapeDtypeStruct((s, heads * dh), BF16)
    return _pcall(
        body, name=name, grid=(heads, nq), in_specs=[qspec, kspec, vspec, blk, saved, saved, blk],
        out_specs=[blk, full, full], out_shape=[act, act, act],
        scratch_shapes=[pltpu.VMEM((s, dh), F32), pltpu.VMEM((s, dh), F32)],
        compiler_params=_params("parallel", "arbitrary"),
    )(proj, proj, proj, o, a_all, beta_all, do)


def _xattn_probs(q, k, scale):
    sc = _dot(q, k, 1, 1) * scale
    e = jnp.exp(sc - jnp.max(sc, axis=1, keepdims=True))
    return e / jnp.sum(e, axis=1, keepdims=True)


def _xattn_fwd(qc, kv, tq, name):
    s, d = qc.shape
    m = kv.shape[0]
    dh = d // X_HEADS
    scale = dh ** -0.5

    def body(q_ref, k_ref, v_ref, o_ref):
        p = _xattn_probs(q_ref[...], k_ref[...], scale)
        o_ref[...] = _dot(p.astype(BF16), v_ref[...], 1, 0).astype(o_ref.dtype)

    blk = pl.BlockSpec((tq, dh), lambda h, i: (i, h))
    return _pcall(
        body, name=name, grid=(X_HEADS, s // tq),
        in_specs=[blk, pl.BlockSpec((m, dh), lambda h, i: (0, h)), pl.BlockSpec((m, dh), lambda h, i: (0, X_HEADS + h))],
        out_specs=blk, out_shape=jax.ShapeDtypeStruct((s, d), BF16), compiler_params=_params("parallel", "parallel"),
    )(qc, kv, kv)


def _xattn_bwd(qc, kv, do, tq, name):
    s, d = qc.shape
    m = kv.shape[0]
    dh = d // X_HEADS
    scale = dh ** -0.5
    nq = s // tq

    def body(q_ref, k_ref, v_ref, do_ref, dq_ref, dk_ref, dv_ref, dk_acc, dv_acc):
        i = pl.program_id(1)
        q, k, v = q_ref[...], k_ref[...], v_ref[...]
        dob = do_ref[...].astype(BF16)
        p = _xattn_probs(q, k, scale)
        pb = p.astype(BF16)
        dp = _dot(dob, v, 1, 1)
        ds = pb.astype(F32) * (dp - jnp.sum(dp * pb.astype(F32), axis=1, keepdims=True))
        dsb = (ds * scale).astype(BF16)
        dq_ref[...] = _dot(dsb, k, 1, 0).astype(dq_ref.dtype)
        dk_part = _dot(dsb, q, 0, 0)
        dv_part = _dot(pb, dob, 0, 0)

        @pl.when(i == 0)
        def _():
            dk_acc[...] = dk_part
            dv_acc[...] = dv_part

        @pl.when(i > 0)
        def _():
            dk_acc[...] += dk_part
            dv_acc[...] += dv_part

        @pl.when(i == nq - 1)
        def _():
            dk_ref[...] = dk_acc[...].astype(dk_ref.dtype)
            dv_ref[...] = dv_acc[...].astype(dv_ref.dtype)

    blk = pl.BlockSpec((tq, dh), lambda h, i: (i, h))
    kblk = pl.BlockSpec((m, dh), lambda h, i: (0, h))
    return _pcall(
        body, name=name, grid=(X_HEADS, nq),
        in_specs=[blk, kblk, pl.BlockSpec((m, dh), lambda h, i: (0, X_HEADS + h)), blk],
        out_specs=[blk, kblk, kblk],
        out_shape=[jax.ShapeDtypeStruct((s, d), BF16), jax.ShapeDtypeStruct((m, d), BF16), jax.ShapeDtypeStruct((m, d), BF16)],
        scratch_shapes=[pltpu.VMEM((m, dh), F32), pltpu.VMEM((m, dh), F32)],
        compiler_params=_params("parallel", "arbitrary"),
    )(qc, kv, kv, do)


def _local_step(x, mem, tgt, w, fetch=None, prefetch=None, emit=None, tick=None, after=None):
    fetch = fetch or (lambda name, after: {})
    prefetch = prefetch or (lambda name, after: None)
    emit = emit or (lambda group, g: None)
    tick = tick or (lambda group, after: None)
    w = dict(w)
    s, d = x.shape
    heads = d // SB_HEAD_DIM
    tm = _pick(s, (512, 256, 128))
    tq = _pick(s, (256, 128))
    sb_tq, sb_tk = _pick(s, (512, 256, 128)), _pick(s, (256, 128))
    tc = _pick(d, (256, 128))
    g = {}

    def wt(name, after):
        if name not in w:
            w.update(fetch(name, after))
        return w[name]

    def ffn_fwd(h, gname, wgu, wdown, tag, after=None):
        n = _rms_fwd(h, w[gname], tag + "_norm", tm, after=after)
        gu, act = _ffn_up(n, wt(wgu, n), tag + "_gu")
        prefetch(wdown, gu)
        return n, gu, act, _mm(act, wt(wdown, act), name=tag + "_down", out_dtype=F32, res=h, alpha=0.5)

    def ffn_bwd(dh, dhb, h, saved, gname, wgu, wdown, tag, copy_scale=None, after=None):
        n, gu, act = saved
        g[wdown] = _mm(act, dhb, ta=True, name=tag + "_dwdown", after=after)
        dgu = _ffn_dgu(dhb, w[wdown], gu, tag + "_dgu", after=emit(tag + "_down", g))
        g[wgu] = _mm(n, dgu, ta=True, b_halves=True, name=tag + "_dwgu", after=tick(tag + "_down", dgu))
        *dh_in, g[gname] = _dgrad_norm(dgu, w[wgu], dh, h, w[gname], tag + "_dn", dy_halves=True, copy_scale=copy_scale,
                                       after=emit(tag, g))
        return dh_in, tick(tag, dh_in[0])

    n1, gu1, act1, h1 = ffn_fwd(x, "g_ffn1", "w_ffn1_gu", "w_ffn1_down", "ffn1", after)
    prefetch("w_in", h1)
    u = _rms_fwd(h1, w["g_mix"], "mix_norm", tm)
    proj = _mm(u, wt("w_in", u), name="mix_in")
    prefetch("w_conv_out", proj)
    nd = d // SB_HEAD_DIM
    y_conv = _conv_fwd(proj, w["conv_w"], d, tc, "conv_fwd")
    sb_cols = (3 * nd, 4 * nd, 5 * nd)
    y_sb, sb_a, sb_beta = _sb_fwd(proj, heads, sb_cols, sb_tq, sb_tk, "sb_fwd")
    prefetch("w_cq", y_sb)
    a_conv = _mm(y_conv, wt("w_conv_out", y_conv), name="conv_out")
    a_sb = _mm(y_sb, wt("w_attn_out", y_sb), name="attn_out")
    b_conv, b_sb = w["b_gate"][:, :d], w["b_gate"][:, d:]

    def merge(ac, asb, gcp, gsp, bc, bs):
        gc = _sigmoid(gcp.astype(F32) + bc)
        gs = _sigmoid(gsp.astype(F32) + bs)
        return gc * ac.astype(F32) + gs * asb.astype(F32)

    merged = _rowcall(merge, [_whole(a_conv), _whole(a_sb), (proj, 6, d), (proj, 7, d)], [b_conv, b_sb], [(d, BF16)],
                      tm=tm, name="merge")[0]
    prefetch("w_ffn2_gu", merged)
    h2 = _mm(merged, wt("w_o", merged), name="mix_out", out_dtype=F32, res=h1)
    hn = _rms_fwd(h2, w["g_cross"], "cross_norm", tm)
    mn = _rms_fwd(mem, w["g_mem"], "mem_norm", _pick(mem.shape[0], (256, 128)))
    qc = _mm(hn, wt("w_cq", hn), name="cross_q")
    kv = _mm(mn, wt("w_ckv", mn), name="cross_kv")
    oc = _xattn_fwd(qc, kv, tq, "xattn_fwd")
    h3 = _mm(oc, wt("w_co", oc), name="cross_out", out_dtype=F32, res=h2)
    n2, gu2, act2, h4 = ffn_fwd(h3, "g_ffn2", "w_ffn2_gu", "w_ffn2_down", "ffn2")

    def head(hb, tb, gb):
        xh, r = _xhat(hb)
        err = xh * gb - tb
        dy = err * (1.0 / d)
        dxh = dy * gb
        dx = r * (dxh - xh * jnp.mean(dxh * xh, axis=-1, keepdims=True))
        row_loss = 0.5 * jnp.mean(err * err, axis=-1, keepdims=True)
        return dx, 0.5 * dx, dy * xh, jnp.broadcast_to(row_loss, (row_loss.shape[0], LANES))

    dh4, dh4b, g["g_final"], loss_lanes = _rowcall(head, [_whole(h4), _whole(tgt)], [w["g_final"]], [(d, F32), (d, BF16)],
                                                   [d, LANES], tm=tm, name="loss_head")

    (dh3, dh3b), tok = ffn_bwd(dh4, dh4b, h3, (n2, gu2, act2), "g_ffn2", "w_ffn2_gu", "w_ffn2_down", "ffn2", copy_scale=1.0)
    g["w_co"] = _mm(oc, dh3b, ta=True, name="cross_dwco", after=tok)
    doc = _mm(dh3b, w["w_co"], tb=True, name="cross_doc")
    dqc, dk, dv = _xattn_bwd(qc, kv, doc, tq, "xattn_bwd")
    dkv = jnp.concatenate([dk, dv], axis=1)
    g["w_cq"] = _mm(hn, dqc, ta=True, name="cross_dwcq")
    g["w_ckv"] = _mm(mn, dkv, ta=True, name="cross_dwckv")
    dmn = _mm(dkv, w["w_ckv"], tb=True, name="cross_dmn", out_dtype=F32)
    g["g_mem"] = _rowcall(lambda dy, xb: dy * _xhat(xb)[0], [_whole(dmn), _whole(mem)], [], [], [d],
                          tm=_pick(mem.shape[0], (256, 128)), name="mem_dnorm")[0]
    dh2, dh2b, g["g_cross"] = _dgrad_norm(dqc, w["w_cq"], dh3, h2, w["g_cross"], "cross_dhn", copy_scale=1.0, after=emit("cross", g))

    g["w_o"] = _mm(merged, dh2b, ta=True, name="mix_dwo", after=tick("cross", dh2))
    dmerged = _mm(dh2b, w["w_o"], tb=True, name="mix_dmerged")

    def merge_bwd(dm, ac, asb, gcp, gsp, bc, bs):
        dm, ac, asb = dm.astype(F32), ac.astype(F32), asb.astype(F32)
        gc = _sigmoid(gcp.astype(F32) + bc)
        gs = _sigmoid(gsp.astype(F32) + bs)
        dgc = dm * ac * gc * (1.0 - gc)
        dgs = dm * asb * gs * (1.0 - gs)
        return dm * gc, dm * gs, dgc, dgs, dgc, dgs

    da_conv, da_sb, dgc, dgs, db_conv, db_sb = _rowcall(
        merge_bwd, [_whole(dmerged), _whole(a_conv), _whole(a_sb), (proj, 6, d), (proj, 7, d)], [b_conv, b_sb],
        [(d, BF16)] * 4, [d, d], tm=tm, name="merge_bwd")
    g["b_gate"] = jnp.concatenate([db_conv, db_sb], axis=1)
    g["w_conv_out"] = _mm(y_conv, da_conv, ta=True, name="conv_dwout")
    g["w_attn_out"] = _mm(y_sb, da_sb, ta=True, name="attn_dwout")
    dy_conv = _mm(da_conv, w["w_conv_out"], tb=True, name="conv_dy")
    dy_sb = _mm(da_sb, w["w_attn_out"], tb=True, name="attn_dy")
    dcb, dcc, dcx, g["conv_w"] = _conv_bwd(dy_conv, proj, w["conv_w"], d, tc, "conv_bwd")
    dq, dk_sb, dv_sb = _sb_bwd(proj, y_sb, sb_a, sb_beta, dy_sb, heads, sb_cols, sb_tq, sb_tk, "sb_bwd")
    dproj = jnp.concatenate([dcb, dcc, dcx, dq, dk_sb, dv_sb, dgc, dgs], axis=1)
    g["w_in"] = _mm(u, dproj, ta=True, name="mix_dwin")
    dh1, dh1b, g["g_mix"] = _dgrad_norm(dproj, w["w_in"], dh2, h1, w["g_mix"], "mix_du", copy_scale=0.5, after=emit("mix", g))
    (dx,), tok = ffn_bwd(dh1, dh1b, x, (n1, gu1, act1), "g_ffn1", "w_ffn1_gu", "w_ffn1_down", "ffn1", after=tick("mix", dh1))
    return loss_lanes, dx, g, tok


MATS = (("w_ffn1_gu", "col"), ("w_ffn1_down", "row"), ("w_in", "col"), ("w_conv_out", "row"), ("w_attn_out", "row"),
        ("w_o", "row"), ("w_cq", "row"), ("w_ckv", "col"), ("w_co", "row"), ("w_ffn2_gu", "col"), ("w_ffn2_down", "row"))
VECS = ("g_ffn1", "g_mix", "g_cross", "g_mem", "g_ffn2", "g_final")
WEIGHTS = ("g_ffn1", "w_ffn1_gu", "w_ffn1_down", "g_mix", "w_in", "b_gate", "conv_w", "w_conv_out", "w_attn_out", "w_o",
           "g_cross", "g_mem", "w_cq", "w_ckv", "w_co", "g_ffn2", "w_ffn2_gu", "w_ffn2_down", "g_final")
CONV_ROWS = 8


def _full_shape(kind, r, c):
    return (r, N_CHIPS * c) if kind == "col" else (N_CHIPS * r, c)


def _piece(ref, kind, r, c, chip, half):
    hr = r // 2
    if kind == "col":
        return ref.at[pl.ds(pl.multiple_of(half * hr, 16), hr), pl.ds(pl.multiple_of(chip * c, LANES), c)]
    return ref.at[pl.ds(pl.multiple_of(chip * r + half * hr, 16), hr), :]


def _shard_of(ref, kind, r, c, chip):
    if kind == "col":
        return ref.at[:, pl.ds(pl.multiple_of(chip * c, LANES), c)]
    return ref.at[pl.ds(pl.multiple_of(chip * r, 16), r), :]


def _place():
    x, y, c = lax.axis_index("x"), lax.axis_index("y"), lax.axis_index("c")
    others = [(1 - x, y), (x, 1 - y), (1 - x, 1 - y)]
    return x, y, c, 2 * x + y, others


def _remote(src, dst, send_sem, recv_sem, to):
    return pltpu.make_async_remote_copy(src_ref=src, dst_ref=dst, send_sem=send_sem, recv_sem=recv_sem,
                                        device_id=to, device_id_type=MESH)


def _gather_conv(conv_shard):
    cc = conv_shard.shape[1]

    def body(conv_ref, conv_full, cs, cr, cl):
        x, y, c, me, others = _place()

        def cols(chip):
            return conv_full.at[:, pl.ds(pl.multiple_of(chip * cc, LANES), cc)]

        def conv(k, chip_from, to):
            return _remote(conv_ref, cols(chip_from), cs.at[k], cr.at[k], to)

        mine = pltpu.make_async_copy(conv_ref, cols(me), cl.at[0])
        mine.start()
        for k, (ox, oy) in enumerate(others):
            conv(k, me, (ox, oy, c)).start()
        for k, (ox, oy) in enumerate(others):
            conv(k, 2 * ox + oy, (x, y, c)).wait_recv()
            conv(k, me, (ox, oy, c)).wait_send()
        mine.wait()

    dma = pltpu.SemaphoreType.DMA
    return _pcall(
        body, name="gather_conv", in_specs=[ANY], out_specs=ANY,
        out_shape=jax.ShapeDtypeStruct((CONV_ROWS, N_CHIPS * cc), F32), scratch_shapes=[dma((3,)), dma((3,)), dma((1,))],
    )(conv_shard)


HBM = pl.BlockSpec(memory_space=pltpu.HBM)
SEM = pl.BlockSpec(memory_space=pltpu.SEMAPHORE)
EFFECT = pltpu.SideEffectType.DATAFLOW_SIDE_EFFECTING
TOKEN = (8, LANES)


def _split_start(name, plan, n_copies, srcs, lands, after=None):
    ns, nl = len(srcs), len(lands)
    n_in = ns + nl + (after is not None)

    def body(*refs):
        outs = refs[n_in:]
        sends, _ = plan(refs[:ns], refs[ns:ns + nl], outs[0], outs[1])
        for cp in sends:
            cp.start()
        outs[-1][...] = jnp.zeros(TOKEN, F32)

    held = [pltpu.HBM(a.shape, a.dtype) for a in (*srcs, *lands)]
    dma = pltpu.SemaphoreType.DMA((n_copies,))
    ins = [pltpu.with_memory_space_constraint(a, pltpu.HBM) for a in (*srcs, *lands)]
    outs = _pcall(
        body, name=name, in_specs=[HBM] * (ns + nl) + ([] if after is None else [ANY]),
        out_specs=(SEM, SEM, *[HBM] * (ns + nl), pl.BlockSpec(memory_space=pltpu.VMEM)),
        out_shape=(dma, dma, *held, jax.ShapeDtypeStruct(TOKEN, F32)),
        input_output_aliases={i: 2 + i for i in range(ns + nl)},
        compiler_params=pltpu.CompilerParams(has_side_effects=EFFECT),
    )(*ins, *([] if after is None else [after]))
    return outs[0], outs[1], list(outs[2:2 + ns]), list(outs[2 + ns:2 + ns + nl]), outs[-1]


def _split_wait(name, plan, send_sems, recv_sems, srcs, lands, after):
    ns, nl = len(srcs), len(lands)

    def body(*refs):
        sends, recvs = plan(refs[:ns], refs[ns:ns + nl], refs[ns + nl], refs[ns + nl + 1])
        for cp in sends:
            cp.wait_send()
        for cp in recvs:
            cp.wait_recv()

    outs = _pcall(
        body, name=name, in_specs=[HBM] * (ns + nl) + [SEM, SEM, ANY], out_specs=[HBM] * (ns + nl),
        out_shape=[pltpu.HBM(a.shape, a.dtype) for a in (*srcs, *lands)],
        input_output_aliases={i: i for i in range(ns + nl)},
        compiler_params=pltpu.CompilerParams(has_side_effects=EFFECT),
    )(*srcs, *lands, send_sems, recv_sems, after)
    return list(outs[:ns]), list(outs[ns:])


def _gather_plan(dims):
    def plan(shard_refs, full_refs, ss, rs):
        x, y, c, me, others = _place()
        sends, recvs = [], []
        for wi, (kind, r, cw) in enumerate(dims):
            half = shard_refs[wi].at[pl.ds(pl.multiple_of(c * (r // 2), 16), r // 2), :]
            for k, (ox, oy) in enumerate(others):
                sem = 4 * wi + k
                sends.append(_remote(half, _piece(full_refs[wi], kind, r, cw, me, c), ss.at[sem], rs.at[sem], (ox, oy, c)))
                recvs.append(_remote(half, _piece(full_refs[wi], kind, r, cw, 2 * ox + oy, c), ss.at[sem], rs.at[sem], (x, y, c)))
            sem = 4 * wi + 3
            own = _remote(shard_refs[wi], _shard_of(full_refs[wi], kind, r, cw, me), ss.at[sem], rs.at[sem], (x, y, 1 - c))
            sends.append(own)
            recvs.append(own)
        return sends, recvs

    return plan


def _forward_plan(dims):
    def plan(_, full_refs, ss, rs):
        x, y, c, _, others = _place()
        sends, recvs = [], []
        for wi, (kind, r, cw) in enumerate(dims):
            for k, (ox, oy) in enumerate(others):
                sem = 3 * wi + k
                mine = _piece(full_refs[wi], kind, r, cw, 2 * ox + oy, c)
                theirs = _piece(full_refs[wi], kind, r, cw, 2 * ox + oy, 1 - c)
                sends.append(_remote(mine, mine, ss.at[sem], rs.at[sem], (x, y, 1 - c)))
                recvs.append(_remote(theirs, theirs, ss.at[sem], rs.at[sem], (x, y, 1 - c)))
        return sends, recvs

    return plan


def _rs_cores_plan(dims):
    def plan(g_refs, land_refs, ss, rs):
        x, y, c, _, _ = _place()
        sends, recvs = [], []
        for wi, dm in enumerate(dims):
            for chip in range(N_CHIPS):
                sem = N_CHIPS * wi + chip
                sends.append(_remote(_piece(g_refs[wi], *dm, chip, 1 - c), land_refs[wi].at[chip], ss.at[sem], rs.at[sem], (x, y, 1 - c)))
                recvs.append(_remote(_piece(g_refs[wi], *dm, chip, c), land_refs[wi].at[chip], ss.at[sem], rs.at[sem], (x, y, 1 - c)))
        return sends, recvs

    return plan


def _share_plan(nw):
    def plan(_, buf_refs, ss, rs):
        x, y, c, _, _ = _place()
        sends = [_remote(buf_refs[wi].at[c], buf_refs[wi].at[c], ss.at[wi], rs.at[wi], (x, y, 1 - c)) for wi in range(nw)]
        recvs = [_remote(buf_refs[wi].at[1 - c], buf_refs[wi].at[1 - c], ss.at[wi], rs.at[wi], (x, y, 1 - c)) for wi in range(nw)]
        return sends, recvs

    return plan


def _small_plan():
    def plan(_, buf_refs, ss, rs):
        x, y, c = lax.axis_index("x"), lax.axis_index("y"), lax.axis_index("c")
        buf = buf_refs[0]
        sends, recvs = [], []
        for rel in range(1, N_DEV):
            peer = (x ^ (rel >> 2 & 1), y ^ (rel >> 1 & 1), c ^ (rel & 1))
            sends.append(_remote(buf.at[0], buf.at[rel], ss.at[rel - 1], rs.at[rel - 1], peer))
            recvs.append(_remote(buf.at[0], buf.at[rel], ss.at[rel - 1], rs.at[rel - 1], peer))
        return sends, recvs

    return plan


def _sum_small(buf, me, name):
    _, rows, n = buf.shape

    def body(me_ref, b_ref, o_ref):
        tot = b_ref[me_ref[0]]
        for dev in range(1, N_DEV):
            tot = tot + b_ref[dev ^ me_ref[0]]
        o_ref[...] = tot

    return _pcall(
        body, name=name, out_shape=jax.ShapeDtypeStruct((rows, n), F32),
        grid_spec=pltpu.PrefetchScalarGridSpec(
            num_scalar_prefetch=1, grid=(1,), in_specs=[pl.BlockSpec((N_DEV, rows, n), lambda i, m: (0, 0, 0))],
            out_specs=pl.BlockSpec((rows, n), lambda i, m: (0, 0))),
    )(me, buf)


def _rs_chips_plan(nw):
    def plan(p_refs, land_refs, ss, rs):
        x, y, c, me, others = _place()
        sends, recvs = [], []
        for wi in range(nw):
            for k, (ox, oy) in enumerate(others):
                sem = 3 * wi + k
                sends.append(_remote(p_refs[wi].at[2 * ox + oy], land_refs[wi].at[k], ss.at[sem], rs.at[sem], (ox, oy, c)))
                recvs.append(_remote(p_refs[wi].at[me], land_refs[wi].at[k], ss.at[sem], rs.at[sem], (x, y, c)))
        return sends, recvs

    return plan


def _rows_per_block(n, c, limit_bytes=2 << 20):
    best = None
    for tm in range(16, n + 1, 16):
        if n % tm == 0 and tm * c * 4 <= limit_bytes:
            best = tm
    return best or n


def _sum_cores(grad, got, kind, place, name):
    _, hr, cw = got.shape
    tm = _rows_per_block(hr, cw)
    nb = hr // tm

    def body(place_ref, g_ref, t_ref, o_ref):
        o_ref[...] = (g_ref[...].astype(F32) + t_ref[...].astype(F32)).astype(o_ref.dtype)

    if kind == "col":
        g_spec = pl.BlockSpec((tm, cw), lambda j, i, pr: (pr[0] * nb + i, j))
    else:
        g_spec = pl.BlockSpec((tm, cw), lambda j, i, pr: ((2 * j + pr[0]) * nb + i, 0))
    blk = pl.BlockSpec((None, tm, cw), lambda j, i, pr: (j, i, 0))
    return _pcall(
        body, name=name, out_shape=jax.ShapeDtypeStruct(got.shape, BF16),
        grid_spec=pltpu.PrefetchScalarGridSpec(num_scalar_prefetch=1, grid=(N_CHIPS, nb), in_specs=[g_spec, blk], out_specs=blk),
        compiler_params=_params("parallel", "parallel"),
    )(place, grad, got)


def _sum_chips(parts, got, place, name):
    _, n, cw = got.shape
    tm = _rows_per_block(n, cw)

    def body(place_ref, p_ref, g_ref, o_ref):
        tot = p_ref[...].astype(F32)
        for k in range(3):
            tot = tot + g_ref[k].astype(F32)
        o_ref[...] = tot

    return _pcall(
        body, name=name, out_shape=jax.ShapeDtypeStruct((2, n, cw), F32),
        grid_spec=pltpu.PrefetchScalarGridSpec(
            num_scalar_prefetch=1, grid=(n // tm,),
            in_specs=[pl.BlockSpec((None, tm, cw), lambda i, pr: (pr[1], i, 0)), pl.BlockSpec((3, tm, cw), lambda i, pr: (0, i, 0))],
            out_specs=pl.BlockSpec((None, tm, cw), lambda i, pr: (pr[0], i, 0))),
        compiler_params=_params("parallel"),
    )(place, parts, got)


def _adamw(g, w, m, v, name):
    n, c = g.shape
    c1 = 1.0 - ADAM_B1 ** ADAM_STEP
    c2 = 1.0 - ADAM_B2 ** ADAM_STEP

    def fn(gb, wb, mb, vb):
        m_new = ADAM_B1 * mb + (1.0 - ADAM_B1) * gb
        v_new = ADAM_B2 * vb + (1.0 - ADAM_B2) * (gb * gb)
        delta = -ADAM_LR * ((m_new / c1) / (jnp.sqrt(v_new / c2) + ADAM_EPS) + ADAM_WD * wb)
        return gb, delta, m_new, v_new

    tm = _rows_per_block(n, c) if n % 16 == 0 else n
    return _rowcall(fn, [_whole(g), _whole(w), _whole(m), _whole(v)], [], [(c, F32)] * 4, tm=tm, name=name)


PACK_ROWS = 16


def _pack_rows(parts, width, name, after=None):
    assert sum(p.shape[0] for p in parts) <= PACK_ROWS

    def body(*refs):
        out_ref = refs[-1]
        out_ref[...] = jnp.zeros_like(out_ref)
        at = 0
        for r in refs[:len(parts)]:
            k, n = r.shape
            if n == width:
                out_ref[at:at + k, :] = r[...]
            else:
                out_ref[at:at + k, :] = jnp.broadcast_to(r[:, :1], (k, width))
            at += k

    vm = pl.BlockSpec(memory_space=pltpu.VMEM)
    return _pcall(body, name=name, in_specs=[vm] * len(parts) + ([] if after is None else [ANY]), out_specs=vm,
                  out_shape=jax.ShapeDtypeStruct((PACK_ROWS, width), F32))(*parts, *([] if after is None else [after]))


def _cast_shard(wm, name, after):
    n, c = wm.shape
    return _rowcall(lambda v: v, [_whole(wm)], [], [(c, BF16)], tm=_rows_per_block(n, c), name=name, after=after)[0]


GATHER_GROUPS = (
    ("w_ffn1_gu",), ("w_ffn1_down",), ("w_in",), ("w_conv_out", "w_attn_out", "w_o"), ("w_cq", "w_ckv", "w_co"),
    ("w_ffn2_gu", "w_ffn2_down"),
)
REDUCE_GROUPS = {
    "ffn2": ("w_ffn2_down", "w_ffn2_gu"),
    "cross": ("w_co", "w_cq", "w_ckv"),
    "mix": ("w_o", "w_conv_out", "w_attn_out", "w_in"),
    "ffn1_down": ("w_ffn1_down",),
    "ffn1": ("w_ffn1_gu",),
}
KIND = dict(MATS)


def _step(x, mem, tgt, wts, m_in, v_in):
    d = x.shape[-1]
    cc = wts["conv_w"].shape[1]
    place = jnp.stack([lax.axis_index("c"), 2 * lax.axis_index("x") + lax.axis_index("y")]).astype(jnp.int32)
    dims = {n: (kind, *wts[n].shape) for n, kind in MATS}

    conv_full = _gather_conv(jnp.pad(wts["conv_w"], ((0, CONV_ROWS - CONV_K), (0, 0))))
    w = {n: wts[n].reshape(1, -1) for n in VECS + ("b_gate",)}
    w["conv_w"] = conv_full[:CONV_K]
    flying, token = {}, conv_full
    for names in GATHER_GROUPS:
        gd = [dims[n] for n in names]
        shards = [_cast_shard(wts[n], "cast_" + n, token) for n in names]
        lands = [lax.empty(_full_shape(*dm), BF16) for dm in gd]
        plan = _gather_plan(gd)
        ss, rs, srcs, lands, token = _split_start("gather_start_" + names[0], plan, 4 * len(names), shards, lands, token)
        flying.update({n: (names, plan, ss, rs, srcs, lands, gd) for n in names})

    passing = {}

    def prefetch(name, after):
        if name not in passing:
            names, plan, ss, rs, srcs, lands, gd = flying[name]
            _, lands = _split_wait("gather_wait_" + names[0], plan, ss, rs, srcs, lands, after)
            plan = _forward_plan(gd)
            ss, rs, _, lands, _ = _split_start("forward_start_" + names[0], plan, 3 * len(names), [], lands)
            passing.update({n: (names, plan, ss, rs, lands) for n in names})

    def fetch(name, after):
        prefetch(name, after)
        names, plan, ss, rs, lands = passing[name]
        _, lands = _split_wait("forward_wait_" + names[0], plan, ss, rs, [], lands, after)
        return dict(zip(names, lands))

    swapping, sent = {}, {}

    def emit(tag, g):
        if tag not in REDUCE_GROUPS:
            return None
        names = REDUCE_GROUPS[tag]
        gd = [dims[n] for n in names]
        lands = [lax.empty((N_CHIPS, r // 2, cw), BF16) for (_, r, cw) in gd]
        plan = _rs_cores_plan(gd)
        ss, rs, srcs, lands, tok = _split_start("rs_cores_start_" + tag, plan, N_CHIPS * len(names), [g[n] for n in names], lands)
        swapping[tag] = (plan, ss, rs, srcs, lands)
        return tok

    def tick(tag, after):
        if tag not in REDUCE_GROUPS:
            return None
        names = REDUCE_GROUPS[tag]
        plan, ss, rs, srcs, lands = swapping[tag]
        mine, got = _split_wait("rs_cores_wait_" + tag, plan, ss, rs, srcs, lands, after)
        parts = [_sum_cores(gm, t, KIND[n], place, "sum_cores_" + n) for n, gm, t in zip(names, mine, got)]
        lands = [lax.empty((3, *p.shape[1:]), BF16) for p in parts]
        plan = _rs_chips_plan(len(names))
        ss, rs, srcs, lands, tok = _split_start("rs_chips_start_" + tag, plan, 3 * len(names), parts, lands)
        sent[tag] = (plan, ss, rs, srcs, lands)
        return tok

    loss_lanes, dx, g, last = _local_step(x[0], mem[0], tgt[0], w, fetch, prefetch, emit, tick, token)

    rows = [g[n] for n in VECS] + [g["b_gate"][:, :d], g["b_gate"][:, d:], g["conv_w"], loss_lanes]
    packed = _pack_rows(rows, d, "pack_small", after=last)
    small = jnp.concatenate([packed[None], jnp.zeros((N_DEV - 1, *packed.shape), F32)], axis=0)
    small_plan = _small_plan()
    small_ss, small_rs, _, small, after = _split_start("small_start", small_plan, N_DEV - 1, [], [small])

    grads, out = {}, {}

    def update(n):
        shape = wts[n].shape
        as2d = (lambda a: a.reshape(1, -1)) if len(shape) == 1 else (lambda a: a)
        return [r.reshape(shape) for r in _adamw(grads[n], as2d(wts[n]), as2d(m_in[n]), as2d(v_in[n]), "adamw_" + n)]

    def finish(sharing, after):
        tag, names, plan, ss, rs, halves = sharing
        _, both = _split_wait("share_wait_" + tag, plan, ss, rs, [], halves, after)
        for n, b in zip(names, both):
            grads[n] = b.reshape(-1, b.shape[-1])
            out[n] = update(n)
        return out[names[-1]][1]

    sharing = None
    for tag, names in REDUCE_GROUPS.items():
        plan, ss, rs, srcs, lands = sent[tag]
        parts, landed = _split_wait("rs_chips_wait_" + tag, plan, ss, rs, srcs, lands, after)
        halves = [_sum_chips(p, t, place, "sum_chips_" + n) for n, p, t in zip(names, parts, landed)]
        plan = _share_plan(len(names))
        ss, rs, _, halves, after = _split_start("share_start_" + tag, plan, len(names), [], halves)
        if sharing is not None:
            after = finish(sharing, after)
        sharing = (tag, names, plan, ss, rs, halves)
    after = finish(sharing, after)

    _, small = _split_wait("small_wait", small_plan, small_ss, small_rs, [], small, after)
    me = (4 * lax.axis_index("x") + 2 * lax.axis_index("y") + lax.axis_index("c")).astype(jnp.int32).reshape(1)
    red = _sum_small(small[0], me, "sum_small")
    grads.update({n: red[i:i + 1] for i, n in enumerate(VECS)})
    nv = len(VECS)
    grads["b_gate"] = jnp.concatenate([red[nv:nv + 1], red[nv + 1:nv + 2]], axis=1)
    chip = 2 * lax.axis_index("x") + lax.axis_index("y")
    grads["conv_w"] = lax.dynamic_slice_in_dim(red[nv + 2:nv + 2 + CONV_K], chip * cc, cc, axis=1)
    loss = red[nv + 2 + CONV_K, 0]
    out.update({n: update(n) for n in WEIGHTS if n not in KIND})
    return (loss, dx[None], *[out[n][0] for n in WEIGHTS], *[out[n][1] for n in WEIGHTS],
            *[out[n][2] for n in WEIGHTS], *[out[n][3] for n in WEIGHTS])


def kernel(x, mem, g_ffn1, w_ffn1_gu, w_ffn1_down, g_mix, w_in, b_gate, conv_w, w_conv_out, w_attn_out, w_o, g_cross, g_mem, w_cq, w_ckv, w_co, g_ffn2, w_ffn2_gu, w_ffn2_down, g_final, loss_target, m_g_ffn1, m_w_ffn1_gu, m_w_ffn1_down, m_g_mix, m_w_in, m_b_gate, m_conv_w, m_w_conv_out, m_w_attn_out, m_w_o, m_g_cross, m_g_mem, m_w_cq, m_w_ckv, m_w_co, m_g_ffn2, m_w_ffn2_gu, m_w_ffn2_down, m_g_final, v_g_ffn1, v_w_ffn1_gu, v_w_ffn1_down, v_g_mix, v_w_in, v_b_gate, v_conv_w, v_w_conv_out, v_w_attn_out, v_w_o, v_g_cross, v_g_mem, v_w_cq, v_w_ckv, v_w_co, v_g_ffn2, v_w_ffn2_gu, v_w_ffn2_down, v_g_final):
    given = dict(locals())
    wts = {n: given[n] for n in WEIGHTS}
    m_in = {n: given["m_" + n] for n in WEIGHTS}
    v_in = {n: given["v_" + n] for n in WEIGHTS}
    return _step(x, mem, loss_target, wts, m_in, v_in)
```

```python
import jax
import jax.numpy as jnp
from jax import lax
from jax.experimental import pallas as pl
from jax.experimental.pallas import tpu as pltpu

F32 = jnp.float32
BF16 = jnp.bfloat16
MESH = pl.DeviceIdType.MESH

V7X_VMEM_LIMIT_BYTES = 48 * 1024 * 1024
MM_VMEM_BUDGET_BYTES = 36 * 1024 * 1024
MM_WHOLE_K = 2816
LANES = 128
SB_HEAD_DIM = 128
X_HEADS = 4
CONV_K = 3
RMS_EPS = 1e-6
N_CHIPS = 4
N_DEV = 8
ADAM_LR, ADAM_B1, ADAM_B2, ADAM_EPS, ADAM_WD, ADAM_STEP = 0.001, 0.9, 0.999, 1e-08, 0.01, 10


ANY = pl.BlockSpec(memory_space=pl.ANY)


def _pcall(body, **kw):
    return pl.pallas_call(body, **kw)


def _params(*sem):
    return pltpu.CompilerParams(dimension_semantics=sem, vmem_limit_bytes=V7X_VMEM_LIMIT_BYTES)


def _pick(dim, cands):
    for c in cands:
        if dim % c == 0:
            return c
    return dim


def _dot(a, b, ca, cb):
    return lax.dot_general(a, b, (((ca,), (cb,)), ((), ())), preferred_element_type=F32)


def _mm(a, b, *, name, ta=False, tb=False, out_dtype=BF16, res=None, alpha=1.0, tm=None, tn=None, tk=None, after=None,
        a_halves=False, b_halves=False):
    assert not (a_halves and ta) and not (b_halves and tb)
    if a_halves:
        m, k = a.shape[1], 2 * a.shape[2]
    else:
        m, k = (a.shape[1], a.shape[0]) if ta else a.shape
    if b_halves:
        n = 2 * b.shape[2]
        assert k == b.shape[1]
    else:
        n = b.shape[0] if tb else b.shape[1]
        assert k == (b.shape[1] if tb else b.shape[0]), (a.shape, b.shape, ta, tb)
    if ta:
        tm = tm or _pick(m, (512, 256, 128))
        tn = tn or _pick(n, (1024, 512, 256, 128))
        tk = tk or (k if k <= MM_WHOLE_K else _pick(k, (1024, 512, 256, 128)))
    else:
        tk = tk or (k if k <= MM_WHOLE_K else _pick(k, (MM_WHOLE_K, 2048, 1024, 512, 256, 128)))
        tn = tn or _pick(n, (512, 1408, 256, 128) if tk == k else (1024, 512, 256, 128))
        per_row = 2 * (tk * a.dtype.itemsize + tn * (jnp.dtype(out_dtype).itemsize + (0 if res is None else res.dtype.itemsize)))
        per_row += 4 * tn if tk < k else 0
        rows = (MM_VMEM_BUDGET_BYTES - 2 * tk * tn * b.dtype.itemsize) // per_row
        tm = tm or next((c for c in (2048, 1024, 512, 256, 128) if m % c == 0 and c <= rows), m)
    if a_halves:
        tk = min(tk, k // 2) if (k // 2) % min(tk, k // 2) == 0 else _pick(k // 2, (1408, 1024, 512, 256, 128))
    if b_halves:
        tn = tn if (n // 2) % tn == 0 else _pick(n // 2, (1408, 1024, 512, 256, 128))
    nk = k // tk
    assert m % tm == 0 and n % tn == 0 and k % tk == 0
    a_spec = pl.BlockSpec((tk, tm), lambda i, j, kk: (kk, i)) if ta else pl.BlockSpec((tm, tk), lambda i, j, kk: (i, kk))
    b_spec = pl.BlockSpec((tn, tk), lambda i, j, kk: (j, kk)) if tb else pl.BlockSpec((tk, tn), lambda i, j, kk: (kk, j))
    if a_halves:
        per = (k // 2) // tk
        a_spec = pl.BlockSpec((None, tm, tk), lambda i, j, kk: (kk // per, i, kk % per))
    if b_halves:
        per_n = (n // 2) // tn
        b_spec = pl.BlockSpec((None, tk, tn), lambda i, j, kk: (j // per_n, kk, j % per_n))
    o_spec = pl.BlockSpec((tm, tn), lambda i, j, kk: (i, j))
    ca, cb = (0 if ta else 1), (1 if tb else 0)

    n_in = 2 + (res is not None) + (after is not None)

    def body(*refs):
        a_ref, b_ref = refs[:2]
        res_ref = refs[2] if res is not None else None
        o_ref = refs[n_in]
        scratch = refs[n_in + 1:]

        def finish(acc):
            val = acc if alpha == 1.0 else alpha * acc
            if res_ref is not None:
                val = res_ref[...].astype(F32) + val
            o_ref[...] = val.astype(o_ref.dtype)

        part = _dot(a_ref[...].astype(BF16), b_ref[...].astype(BF16), ca, cb)
        if nk == 1:
            finish(part)
        else:
            acc_ref = scratch[0]
            kk = pl.program_id(2)

            @pl.when(kk == 0)
            def _():
                acc_ref[...] = part

            @pl.when(kk > 0)
            def _():
                acc_ref[...] += part

            @pl.when(kk == nk - 1)
            def _():
                finish(acc_ref[...])

    ins = [a, b] + ([] if res is None else [res]) + ([] if after is None else [after])
    in_specs = [a_spec, b_spec] + ([] if res is None else [o_spec]) + ([] if after is None else [ANY])
    return _pcall(
        body, name=name, grid=(m // tm, n // tn, nk), in_specs=in_specs, out_specs=o_spec,
        out_shape=jax.ShapeDtypeStruct((m, n), out_dtype),
        scratch_shapes=[pltpu.VMEM((tm, tn), F32)] if nk > 1 else [],
        compiler_params=_params("parallel", "parallel", "arbitrary"),
    )(*ins)


def _rowcall(fn, rows, consts, outs, accs=(), *, tm, name, after=None):
    s = rows[0][0].shape[0]
    assert s % tm == 0
    n_read, n_out = len(rows) + len(consts), len(outs)
    n_in = n_read + (after is not None)

    def body(*refs):
        vals = fn(*[r[...] for r in refs[:n_read]])
        vals = vals if isinstance(vals, (tuple, list)) else (vals,)
        for o_ref, v in zip(refs[n_in:n_in + n_out], vals[:n_out]):
            o_ref[...] = v.astype(o_ref.dtype)
        if accs:
            first = pl.program_id(0) == 0
            for a_ref, v in zip(refs[n_in + n_out:], vals[n_out:]):
                tot = jnp.sum(v.astype(F32), axis=0, keepdims=True)

                @pl.when(first)
                def _(a_ref=a_ref, tot=tot):
                    a_ref[...] = tot

                @pl.when(jnp.logical_not(first))
                def _(a_ref=a_ref, tot=tot):
                    a_ref[...] += tot

    in_specs = [pl.BlockSpec((tm, w), lambda i, cb=cb: (i, cb)) for (_, cb, w) in rows]
    in_specs += [pl.BlockSpec(c.shape, lambda i: (0, 0)) for c in consts]
    in_specs += [] if after is None else [ANY]
    out_specs = [pl.BlockSpec((tm, w), lambda i: (i, 0)) for (w, _) in outs]
    out_specs += [pl.BlockSpec((1, w), lambda i: (0, 0)) for w in accs]
    out_shape = [jax.ShapeDtypeStruct((s, w), dt) for (w, dt) in outs]
    out_shape += [jax.ShapeDtypeStruct((1, w), F32) for w in accs]
    return _pcall(
        body, name=name, grid=(s // tm,), in_specs=in_specs, out_specs=out_specs, out_shape=out_shape,
        compiler_params=_params("arbitrary" if accs else "parallel"),
    )(*[r[0] for r in rows], *consts, *([] if after is None else [after]))


def _whole(a):
    return (a, 0, a.shape[1])


def _xhat(x):
    x = x.astype(F32)
    r = lax.rsqrt(jnp.mean(x * x, axis=-1, keepdims=True) + RMS_EPS)
    return x * r, r


def _rms_bwd(dy, x, g):
    xh, r = _xhat(x)
    dxh = dy.astype(F32) * g
    dx = r * (dxh - xh * jnp.mean(dxh * xh, axis=-1, keepdims=True))
    return dx, dy.astype(F32) * xh


def _sigmoid(x):
    return 1.0 / (1.0 + jnp.exp(-x))


def _rms_fwd(x, g, name, tm, after=None):
    d = x.shape[1]
    return _rowcall(lambda xb, gb: _xhat(xb)[0] * gb, [_whole(x)], [g], [(d, BF16)], tm=tm, name=name, after=after)[0]


def _silu_parts(gate):
    sg = _sigmoid(gate)
    return sg, gate * sg


def _ffn_up(n, w_gu, name):
    s, d = n.shape
    f = w_gu.shape[1] // 2
    tn = _pick(f, (1408, 1024, 512, 256, 128))
    tm = _pick(s, (1024, 512, 256, 128))
    nb = f // tn

    def body(n_ref, wg_ref, wu_ref, gu_ref, act_ref):
        nv = n_ref[...]
        gate = _dot(nv, wg_ref[...], 1, 0)
        up = _dot(nv, wu_ref[...], 1, 0)
        gu_ref[0] = gate.astype(gu_ref.dtype)
        gu_ref[1] = up.astype(gu_ref.dtype)
        act_ref[...] = (_silu_parts(gate)[1] * up).astype(act_ref.dtype)

    return _pcall(
        body, name=name, grid=(s // tm, nb),
        in_specs=[pl.BlockSpec((tm, d), lambda i, j: (i, 0)), pl.BlockSpec((d, tn), lambda i, j: (0, j)),
                  pl.BlockSpec((d, tn), lambda i, j: (0, nb + j))],
        out_specs=[pl.BlockSpec((2, tm, tn), lambda i, j: (0, i, j)), pl.BlockSpec((tm, tn), lambda i, j: (i, j))],
        out_shape=[jax.ShapeDtypeStruct((2, s, f), BF16), jax.ShapeDtypeStruct((s, f), BF16)],
        compiler_params=_params("parallel", "parallel"),
    )(n, w_gu, w_gu)


def _ffn_dgu(dhb, w_down, gu, name, after=None):
    s, d = dhb.shape
    f = w_down.shape[0]
    tn = _pick(f, (1408, 1024, 512, 256, 128))
    tm = _pick(s, (1024, 512, 256, 128))

    def body(dh_ref, w_ref, gu_ref, *rest):
        o_ref = rest[-1]
        dact = _dot(dh_ref[...], w_ref[...], 1, 1)
        gate, up = gu_ref[0].astype(F32), gu_ref[1].astype(F32)
        sg, silu = _silu_parts(gate)
        o_ref[0] = (dact * up * (sg + silu * (1.0 - sg))).astype(o_ref.dtype)
        o_ref[1] = (dact * silu).astype(o_ref.dtype)

    blk = pl.BlockSpec((2, tm, tn), lambda i, j: (0, i, j))
    return _pcall(
        body, name=name, grid=(s // tm, f // tn),
        in_specs=[pl.BlockSpec((tm, d), lambda i, j: (i, 0)), pl.BlockSpec((tn, d), lambda i, j: (j, 0)), blk]
        + ([] if after is None else [ANY]),
        out_specs=blk, out_shape=jax.ShapeDtypeStruct((2, s, f), BF16), compiler_params=_params("parallel", "parallel"),
    )(dhb, w_down, gu, *([] if after is None else [after]))


def _dgrad_norm(dy, wmat, dh, x, g, name, *, dy_halves=False, copy_scale=None, after=None):
    s, d = dh.shape
    k = wmat.shape[1]
    tk = k if k <= MM_WHOLE_K else _pick(k, (MM_WHOLE_K, 2048, 1024, 512, 256, 128))
    if dy_halves and (k // 2) % tk:
        tk = _pick(k // 2, (1408, 1024, 512, 256, 128))
    tm = _pick(s, (512, 256, 128))
    nk, per = k // tk, (k // 2) // tk if dy_halves else 0
    n_in = 5 + (after is not None)
    n_out = 2 + (copy_scale is not None)

    def body(*refs):
        dy_ref, w_ref, dh_ref, x_ref, g_ref = refs[:5]
        outs, scratch = refs[n_in:n_in + n_out], refs[n_in + n_out:]
        i, kk = pl.program_id(0), pl.program_id(1)
        part = _dot(dy_ref[...], w_ref[...], 1, 1)

        def finish(dn):
            dx, dg = _rms_bwd(dn, x_ref[...], g_ref[...])
            tot = dh_ref[...] + dx
            outs[0][...] = tot
            if copy_scale is not None:
                outs[1][...] = (copy_scale * tot).astype(outs[1].dtype)
            dg = jnp.sum(dg, axis=0, keepdims=True)

            @pl.when(i == 0)
            def _():
                outs[-1][...] = dg

            @pl.when(i > 0)
            def _():
                outs[-1][...] += dg

        if nk == 1:
            finish(part)
        else:
            acc_ref = scratch[0]

            @pl.when(kk == 0)
            def _():
                acc_ref[...] = part

            @pl.when(kk > 0)
            def _():
                acc_ref[...] += part

            @pl.when(kk == nk - 1)
            def _():
                finish(acc_ref[...])

    row = pl.BlockSpec((tm, d), lambda i, kk: (i, 0))
    dy_spec = pl.BlockSpec((None, tm, tk), lambda i, kk: (kk // per, i, kk % per)) if dy_halves else pl.BlockSpec((tm, tk), lambda i, kk: (i, kk))
    in_specs = [dy_spec, pl.BlockSpec((d, tk), lambda i, kk: (0, kk)), row, row, pl.BlockSpec((1, d), lambda i, kk: (0, 0))]
    out_specs = [row] * (n_out - 1) + [pl.BlockSpec((1, d), lambda i, kk: (0, 0))]
    out_shape = [jax.ShapeDtypeStruct((s, d), F32)] + ([] if copy_scale is None else [jax.ShapeDtypeStruct((s, d), BF16)])
    return _pcall(
        body, name=name, grid=(s // tm, nk), in_specs=in_specs + ([] if after is None else [ANY]), out_specs=out_specs,
        out_shape=out_shape + [jax.ShapeDtypeStruct((1, d), F32)], scratch_shapes=[pltpu.VMEM((tm, d), F32)] if nk > 1 else [],
        compiler_params=_params("arbitrary", "arbitrary"),
    )(dy, wmat, dh, x, g, *([] if after is None else [after]))


def _shift_down(p, k):
    if k == 0:
        return p
    rows = lax.broadcasted_iota(jnp.int32, p.shape, 0)
    return jnp.where(rows >= k, pltpu.roll(p, k, 0), 0.0)


def _shift_up(p, k):
    if k == 0:
        return p
    s = p.shape[0]
    rows = lax.broadcasted_iota(jnp.int32, p.shape, 0)
    return jnp.where(rows < s - k, pltpu.roll(p, s - k, 0), 0.0)


def _conv_fwd(proj, conv_w, d, tc, name):
    s = proj.shape[0]
    nb = d // tc

    def body(cb_ref, cc_ref, cx_ref, w_ref, y_ref):
        p = cc_ref[...].astype(F32) * cx_ref[...].astype(F32)
        w = w_ref[...]
        acc = p * w[CONV_K - 1:CONV_K, :]
        for k in range(1, CONV_K):
            acc = acc + _shift_down(p, k) * w[CONV_K - 1 - k:CONV_K - k, :]
        y_ref[...] = (cb_ref[...].astype(F32) * acc).astype(y_ref.dtype)

    col = lambda off: pl.BlockSpec((s, tc), lambda j: (0, off * nb + j))
    return _pcall(
        body, name=name, grid=(nb,), in_specs=[col(0), col(1), col(2), pl.BlockSpec((CONV_K, tc), lambda j: (0, j))],
        out_specs=pl.BlockSpec((s, tc), lambda j: (0, j)), out_shape=jax.ShapeDtypeStruct((s, d), BF16),
        compiler_params=_params("parallel"),
    )(proj, proj, proj, conv_w)


def _conv_bwd(dy, proj, conv_w, d, tc, name):
    s = proj.shape[0]
    nb = d // tc

    def body(dy_ref, cb_ref, cc_ref, cx_ref, w_ref, dcb_ref, dcc_ref, dcx_ref, dw_ref):
        cc, cx = cc_ref[...].astype(F32), cx_ref[...].astype(F32)
        p = cc * cx
        w = w_ref[...]
        dyv = dy_ref[...].astype(F32)
        shifted = [_shift_down(p, CONV_K - 1 - k) for k in range(CONV_K)]
        conv = shifted[0] * w[0:1, :]
        for k in range(1, CONV_K):
            conv = conv + shifted[k] * w[k:k + 1, :]
        dcb_ref[...] = (dyv * conv).astype(dcb_ref.dtype)
        ds = dyv * cb_ref[...].astype(F32)
        dp = ds * w[CONV_K - 1:CONV_K, :]
        for k in range(1, CONV_K):
            dp = dp + _shift_up(ds, k) * w[CONV_K - 1 - k:CONV_K - k, :]
        dcc_ref[...] = (dp * cx).astype(dcc_ref.dtype)
        dcx_ref[...] = (dp * cc).astype(dcx_ref.dtype)
        for k in range(CONV_K):
            dw_ref[k:k + 1, :] = jnp.sum(ds * shifted[k], axis=0, keepdims=True)

    col = lambda off: pl.BlockSpec((s, tc), lambda j: (0, off * nb + j))
    blk = pl.BlockSpec((s, tc), lambda j: (0, j))
    wblk = pl.BlockSpec((CONV_K, tc), lambda j: (0, j))
    act = jax.ShapeDtypeStruct((s, d), BF16)
    return _pcall(
        body, name=name, grid=(nb,), in_specs=[blk, col(0), col(1), col(2), wblk],
        out_specs=[blk, blk, blk, wblk], out_shape=[act, act, act, jax.ShapeDtypeStruct((CONV_K, d), F32)],
        compiler_params=_params("parallel"),
    )(dy, proj, proj, proj, conv_w)


def _sb_tile(q, kj, scale, carry, tri, mask):
    z = _dot(q, kj, 1, 1) * scale
    lsz = jnp.minimum(z, 0.0) - jnp.log(1.0 + jnp.exp(-jnp.abs(z)))
    l1m = lsz - z
    if mask is not None:
        l1m = jnp.where(mask, l1m, 0.0)
    l1b = l1m.astype(BF16)
    a = jnp.exp(lsz + (carry + _dot(l1b, tri, 1, 0)))
    if mask is not None:
        a = jnp.where(mask, a, 0.0)
    return lsz, l1b, a.astype(BF16)


def _add_rows(x, upd, r0):
    return x + upd if r0 == 0 else jnp.concatenate([x[:r0], x[r0:] + upd], axis=0)


def _sb_masks(tq, tk):
    row = lax.broadcasted_iota(jnp.int32, (tq, tk), 0)
    col = lax.broadcasted_iota(jnp.int32, (tq, tk), 1)
    masks = [col + dj * tk < row for dj in range(tq // tk)]
    r2 = lax.broadcasted_iota(jnp.int32, (tk, tk), 0)
    c2 = lax.broadcasted_iota(jnp.int32, (tk, tk), 1)
    return masks, (r2 > c2).astype(BF16), (r2 < c2).astype(BF16)


def _sb_fwd(proj, heads, col0, tq, tk, name):
    s = proj.shape[0]
    dh = SB_HEAD_DIM
    nq, nd, nkt = s // tq, tq // tk, s // tk
    scale = dh ** -0.5

    def body(q_ref, k_ref, v_ref, o_ref, a_ref, b_ref):
        i = pl.program_id(1)
        q = q_ref[...]
        masks, tri_right, _ = _sb_masks(tq, tk)

        def tile(j, carry, acc, mask, r0=0):
            start = pl.multiple_of(j * tk, tk)
            kj = k_ref[pl.ds(start, tk), :]
            vj = v_ref[pl.ds(start, tk), :]
            lsz, l1b, ab = _sb_tile(q[r0:], kj, scale, carry[r0:], tri_right, None if mask is None else mask[r0:])
            a_ref[j, r0:, :] = ab
            b_ref[j, r0:, :] = jnp.exp(lsz).astype(b_ref.dtype)
            return (_add_rows(carry, jnp.sum(l1b.astype(F32), axis=1, keepdims=True), r0),
                    _add_rows(acc, _dot(ab, vj, 1, 0), r0))

        state = (jnp.zeros((tq, 1), F32), jnp.zeros((tq, dh), F32))
        for dj in reversed(range(nd)):
            state = tile(i * nd + dj, *state, masks[dj], dj * tk)
        def left_block(t, st):
            for dj in reversed(range(nd)):
                st = tile((i - 1 - t) * nd + dj, st[0], st[1], None)
            return st

        state = lax.fori_loop(0, i, left_block, state)
        o_ref[...] = state[1]

    qspec = pl.BlockSpec((tq, dh), lambda h, i: (i, col0[0] + h))
    kspec = pl.BlockSpec((s, dh), lambda h, i: (0, col0[1] + h))
    vspec = pl.BlockSpec((s, dh), lambda h, i: (0, col0[2] + h))
    saved = pl.BlockSpec((None, nkt, tq, tk), lambda h, i: (h, 0, i, 0))
    saved_shape = jax.ShapeDtypeStruct((heads, nkt, s, tk), BF16)
    return _pcall(
        body, name=name, grid=(heads, nq), in_specs=[qspec, kspec, vspec],
        out_specs=[pl.BlockSpec((tq, dh), lambda h, i: (i, h)), saved, saved],
        out_shape=[jax.ShapeDtypeStruct((s, heads * dh), F32), saved_shape, saved_shape],
        compiler_params=_params("parallel", "parallel"),
    )(proj, proj, proj)


def _sb_bwd(proj, o, a_all, beta_all, do, heads, col0, tq, tk, name):
    s = proj.shape[0]
    dh = SB_HEAD_DIM
    nq, nd, nkt = s // tq, tq // tk, s // tk
    scale = dh ** -0.5

    def body(q_ref, k_ref, v_ref, o_ref, a_ref, b_ref, do_ref, dq_ref, dk_ref, dv_ref, dk_acc, dv_acc):
        i = pl.program_id(1)

        @pl.when(i == 0)
        def _():
            dk_acc[...] = jnp.zeros_like(dk_acc)
            dv_acc[...] = jnp.zeros_like(dv_acc)

        q = q_ref[...]
        dob = do_ref[...].astype(BF16)
        delta = jnp.sum(dob.astype(F32) * o_ref[...], axis=1, keepdims=True)
        masks, _, tri_left = _sb_masks(tq, tk)

        def tile(j, carry_g, dq, mask, r0=0):
            start = pl.multiple_of(j * tk, tk)
            kj = k_ref[pl.ds(start, tk), :]
            vj = v_ref[pl.ds(start, tk), :]
            ab = a_ref[j, r0:, :]
            g = _dot(dob[r0:], vj, 1, 1) * ab.astype(F32)
            carry_g = _add_rows(carry_g, jnp.sum(g, axis=1, keepdims=True), r0)
            left = (delta - carry_g)[r0:] + _dot(g.astype(BF16), tri_left, 1, 0)
            dz = g - b_ref[j, r0:, :].astype(F32) * (g + left)
            if mask is not None:
                dz = jnp.where(mask[r0:], dz, 0.0)
            dzb = dz.astype(BF16)
            dk_acc[pl.ds(start, tk), :] += _dot(dzb, q[r0:], 0, 0)
            dv_acc[pl.ds(start, tk), :] += _dot(ab, dob[r0:], 0, 0)
            return carry_g, _add_rows(dq, _dot(dzb, kj, 1, 0), r0)

        state = (jnp.zeros((tq, 1), F32), jnp.zeros((tq, dh), F32))
        for dj in reversed(range(nd)):
            state = tile(i * nd + dj, *state, masks[dj], dj * tk)
        def left_block(t, st):
            for dj in reversed(range(nd)):
                st = tile((i - 1 - t) * nd + dj, st[0], st[1], None)
            return st

        state = lax.fori_loop(0, i, left_block, state)
        dq_ref[...] = (state[1] * scale).astype(dq_ref.dtype)

        @pl.when(i == nq - 1)
        def _():
            dk_ref[...] = (dk_acc[...] * scale).astype(dk_ref.dtype)
            dv_ref[...] = dv_acc[...].astype(dv_ref.dtype)

    qspec = pl.BlockSpec((tq, dh), lambda h, i: (i, col0[0] + h))
    kspec = pl.BlockSpec((s, dh), lambda h, i: (0, col0[1] + h))
    vspec = pl.BlockSpec((s, dh), lambda h, i: (0, col0[2] + h))
    blk = pl.BlockSpec((tq, dh), lambda h, i: (i, h))
    full = pl.BlockSpec((s, dh), lambda h, i: (0, h))
    saved = pl.BlockSpec((None, nkt, tq, tk), lambda h, i: (h, 0, i, 0))
    act = jax.ShapeDtypeStruct((s, heads * dh), BF16)
    return _pcall(
        body, name=name, grid=(heads, nq), in_specs=[qspec, kspec, vspec, blk, saved, saved, blk],
        out_specs=[blk, full, full], out_shape=[act, act, act],
        scratch_shapes=[pltpu.VMEM((s, dh), F32), pltpu.VMEM((s, dh), F32)],
        compiler_params=_params("parallel", "arbitrary"),
    )(proj, proj, proj, o, a_all, beta_all, do)


def _xattn_probs(q, k, scale):
    sc = _dot(q, k, 1, 1) * scale
    e = jnp.exp(sc - jnp.max(sc, axis=1, keepdims=True))
    return e / jnp.sum(e, axis=1, keepdims=True)


def _xattn_fwd(qc, kv, tq, name):
    s, d = qc.shape
    m = kv.shape[0]
    dh = d // X_HEADS
    scale = dh ** -0.5

    def body(q_ref, k_ref, v_ref, o_ref):
        p = _xattn_probs(q_ref[...], k_ref[...], scale)
        o_ref[...] = _dot(p.astype(BF16), v_ref[...], 1, 0).astype(o_ref.dtype)

    blk = pl.BlockSpec((tq, dh), lambda h, i: (i, h))
    return _pcall(
        body, name=name, grid=(X_HEADS, s // tq),
        in_specs=[blk, pl.BlockSpec((m, dh), lambda h, i: (0, h)), pl.BlockSpec((m, dh), lambda h, i: (0, X_HEADS + h))],
        out_specs=blk, out_shape=jax.ShapeDtypeStruct((s, d), BF16), compiler_params=_params("parallel", "parallel"),
    )(qc, kv, kv)


def _xattn_bwd(qc, kv, do, tq, name):
    s, d = qc.shape
    m = kv.shape[0]
    dh = d // X_HEADS
    scale = dh ** -0.5
    nq = s // tq

    def body(q_ref, k_ref, v_ref, do_ref, dq_ref, dk_ref, dv_ref, dk_acc, dv_acc):
        i = pl.program_id(1)
        q, k, v = q_ref[...], k_ref[...], v_ref[...]
        dob = do_ref[...].astype(BF16)
        p = _xattn_probs(q, k, scale)
        pb = p.astype(BF16)
        dp = _dot(dob, v, 1, 1)
        ds = pb.astype(F32) * (dp - jnp.sum(dp * pb.astype(F32), axis=1, keepdims=True))
        dsb = (ds * scale).astype(BF16)
        dq_ref[...] = _dot(dsb, k, 1, 0).astype(dq_ref.dtype)
        dk_part = _dot(dsb, q, 0, 0)
        dv_part = _dot(pb, dob, 0, 0)

        @pl.when(i == 0)
        def _():
            dk_acc[...] = dk_part
            dv_acc[...] = dv_part

        @pl.when(i > 0)
        def _():
            dk_acc[...] += dk_part
            dv_acc[...] += dv_part

        @pl.when(i == nq - 1)
        def _():
            dk_ref[...] = dk_acc[...].astype(dk_ref.dtype)
            dv_ref[...] = dv_acc[...].astype(dv_ref.dtype)

    blk = pl.BlockSpec((tq, dh), lambda h, i: (i, h))
    kblk = pl.BlockSpec((m, dh), lambda h, i: (0, h))
    return _pcall(
        body, name=name, grid=(X_HEADS, nq),
        in_specs=[blk, kblk, pl.BlockSpec((m, dh), lambda h, i: (0, X_HEADS + h)), blk],
        out_specs=[blk, kblk, kblk],
        out_shape=[jax.ShapeDtypeStruct((s, d), BF16), jax.ShapeDtypeStruct((m, d), BF16), jax.ShapeDtypeStruct((m, d), BF16)],
        scratch_shapes=[pltpu.VMEM((m, dh), F32), pltpu.VMEM((m, dh), F32)],
        compiler_params=_params("parallel", "arbitrary"),
    )(qc, kv, kv, do)


def _local_step(x, mem, tgt, w, fetch=None, prefetch=None, emit=None, tick=None, after=None):
    fetch = fetch or (lambda name, after: {})
    prefetch = prefetch or (lambda name, after: None)
    emit = emit or (lambda group, g: None)
    tick = tick or (lambda group, after: None)
    w = dict(w)
    s, d = x.shape
    heads = d // SB_HEAD_DIM
    tm = _pick(s, (512, 256, 128))
    tq = _pick(s, (256, 128))
    sb_tq, sb_tk = _pick(s, (512, 256, 128)), _pick(s, (256, 128))
    tc = _pick(d, (256, 128))
    g = {}

    def wt(name, after):
        if name not in w:
            w.update(fetch(name, after))
        return w[name]

    def ffn_fwd(h, gname, wgu, wdown, tag, after=None):
        n = _rms_fwd(h, w[gname], tag + "_norm", tm, after=after)
        gu, act = _ffn_up(n, wt(wgu, n), tag + "_gu")
        prefetch(wdown, gu)
        return n, gu, act, _mm(act, wt(wdown, act), name=tag + "_down", out_dtype=F32, res=h, alpha=0.5)

    def ffn_bwd(dh, dhb, h, saved, gname, wgu, wdown, tag, copy_scale=None, after=None):
        n, gu, act = saved
        g[wdown] = _mm(act, dhb, ta=True, name=tag + "_dwdown", after=after)
        dgu = _ffn_dgu(dhb, w[wdown], gu, tag + "_dgu", after=emit(tag + "_down", g))
        g[wgu] = _mm(n, dgu, ta=True, b_halves=True, name=tag + "_dwgu", after=tick(tag + "_down", dgu))
        *dh_in, g[gname] = _dgrad_norm(dgu, w[wgu], dh, h, w[gname], tag + "_dn", dy_halves=True, copy_scale=copy_scale,
                                       after=emit(tag, g))
        return dh_in, tick(tag, dh_in[0])

    n1, gu1, act1, h1 = ffn_fwd(x, "g_ffn1", "w_ffn1_gu", "w_ffn1_down", "ffn1", after)
    prefetch("w_in", h1)
    u = _rms_fwd(h1, w["g_mix"], "mix_norm", tm)
    proj = _mm(u, wt("w_in", u), name="mix_in")
    prefetch("w_conv_out", proj)
    nd = d // SB_HEAD_DIM
    y_conv = _conv_fwd(proj, w["conv_w"], d, tc, "conv_fwd")
    sb_cols = (3 * nd, 4 * nd, 5 * nd)
    y_sb, sb_a, sb_beta = _sb_fwd(proj, heads, sb_cols, sb_tq, sb_tk, "sb_fwd")
    prefetch("w_cq", y_sb)
    a_conv = _mm(y_conv, wt("w_conv_out", y_conv), name="conv_out")
    a_sb = _mm(y_sb, wt("w_attn_out", y_sb), name="attn_out")
    b_conv, b_sb = w["b_gate"][:, :d], w["b_gate"][:, d:]

    def merge(ac, asb, gcp, gsp, bc, bs):
        gc = _sigmoid(gcp.astype(F32) + bc)
        gs = _sigmoid(gsp.astype(F32) + bs)
        return gc * ac.astype(F32) + gs * asb.astype(F32)

    merged = _rowcall(merge, [_whole(a_conv), _whole(a_sb), (proj, 6, d), (proj, 7, d)], [b_conv, b_sb], [(d, BF16)],
                      tm=tm, name="merge")[0]
    prefetch("w_ffn2_gu", merged)
    h2 = _mm(merged, wt("w_o", merged), name="mix_out", out_dtype=F32, res=h1)
    hn = _rms_fwd(h2, w["g_cross"], "cross_norm", tm)
    mn = _rms_fwd(mem, w["g_mem"], "mem_norm", _pick(mem.shape[0], (256, 128)))
    qc = _mm(hn, wt("w_cq", hn), name="cross_q")
    kv = _mm(mn, wt("w_ckv", mn), name="cross_kv")
    oc = _xattn_fwd(qc, kv, tq, "xattn_fwd")
    h3 = _mm(oc, wt("w_co", oc), name="cross_out", out_dtype=F32, res=h2)
    n2, gu2, act2, h4 = ffn_fwd(h3, "g_ffn2", "w_ffn2_gu", "w_ffn2_down", "ffn2")

    def head(hb, tb, gb):
        xh, r = _xhat(hb)
        err = xh * gb - tb
        dy = err * (1.0 / d)
        dxh = dy * gb
        dx = r * (dxh - xh * jnp.mean(dxh * xh, axis=-1, keepdims=True))
        row_loss = 0.5 * jnp.mean(err * err, axis=-1, keepdims=True)
        return dx, 0.5 * dx, dy * xh, jnp.broadcast_to(row_loss, (row_loss.shape[0], LANES))

    dh4, dh4b, g["g_final"], loss_lanes = _rowcall(head, [_whole(h4), _whole(tgt)], [w["g_final"]], [(d, F32), (d, BF16)],
                                                   [d, LANES], tm=tm, name="loss_head")

    (dh3, dh3b), tok = ffn_bwd(dh4, dh4b, h3, (n2, gu2, act2), "g_ffn2", "w_ffn2_gu", "w_ffn2_down", "ffn2", copy_scale=1.0)
    g["w_co"] = _mm(oc, dh3b, ta=True, name="cross_dwco", after=tok)
    doc = _mm(dh3b, w["w_co"], tb=True, name="cross_doc")
    dqc, dk, dv = _xattn_bwd(qc, kv, doc, tq, "xattn_bwd")
    dkv = jnp.concatenate([dk, dv], axis=1)
    g["w_cq"] = _mm(hn, dqc, ta=True, name="cross_dwcq")
    g["w_ckv"] = _mm(mn, dkv, ta=True, name="cross_dwckv")
    dmn = _mm(dkv, w["w_ckv"], tb=True, name="cross_dmn", out_dtype=F32)
    g["g_mem"] = _rowcall(lambda dy, xb: dy * _xhat(xb)[0], [_whole(dmn), _whole(mem)], [], [], [d],
                          tm=_pick(mem.shape[0], (256, 128)), name="mem_dnorm")[0]
    dh2, dh2b, g["g_cross"] = _dgrad_norm(dqc, w["w_cq"], dh3, h2, w["g_cross"], "cross_dhn", copy_scale=1.0, after=emit("cross", g))

    g["w_o"] = _mm(merged, dh2b, ta=True, name="mix_dwo", after=tick("cross", dh2))
    dmerged = _mm(dh2b, w["w_o"], tb=True, name="mix_dmerged")

    def merge_bwd(dm, ac, asb, gcp, gsp, bc, bs):
        dm, ac, asb = dm.astype(F32), ac.astype(F32), asb.astype(F32)
        gc = _sigmoid(gcp.astype(F32) + bc)
        gs = _sigmoid(gsp.astype(F32) + bs)
        dgc = dm * ac * gc * (1.0 - gc)
        dgs = dm * asb * gs * (1.0 - gs)
        return dm * gc, dm * gs, dgc, dgs, dgc, dgs

    da_conv, da_sb, dgc, dgs, db_conv, db_sb = _rowcall(
        merge_bwd, [_whole(dmerged), _whole(a_conv), _whole(a_sb), (proj, 6, d), (proj, 7, d)], [b_conv, b_sb],
        [(d, BF16)] * 4, [d, d], tm=tm, name="merge_bwd")
    g["b_gate"] = jnp.concatenate([db_conv, db_sb], axis=1)
    g["w_conv_out"] = _mm(y_conv, da_conv, ta=True, name="conv_dwout")
    g["w_attn_out"] = _mm(y_sb, da_sb, ta=True, name="attn_dwout")
    dy_conv = _mm(da_conv, w["w_conv_out"], tb=True, name="conv_dy")
    dy_sb = _mm(da_sb, w["w_attn_out"], tb=True, name="attn_dy")
    dcb, dcc, dcx, g["conv_w"] = _conv_bwd(dy_conv, proj, w["conv_w"], d, tc, "conv_bwd")
    dq, dk_sb, dv_sb = _sb_bwd(proj, y_sb, sb_a, sb_beta, dy_sb, heads, sb_cols, sb_tq, sb_tk, "sb_bwd")
    dproj = jnp.concatenate([dcb, dcc, dcx, dq, dk_sb, dv_sb, dgc, dgs], axis=1)
    g["w_in"] = _mm(u, dproj, ta=True, name="mix_dwin")
    dh1, dh1b, g["g_mix"] = _dgrad_norm(dproj, w["w_in"], dh2, h1, w["g_mix"], "mix_du", copy_scale=0.5, after=emit("mix", g))
    (dx,), tok = ffn_bwd(dh1, dh1b, x, (n1, gu1, act1), "g_ffn1", "w_ffn1_gu", "w_ffn1_down", "ffn1", after=tick("mix", dh1))
    return loss_lanes, dx, g, tok


MATS = (("w_ffn1_gu", "col"), ("w_ffn1_down", "row"), ("w_in", "col"), ("w_conv_out", "row"), ("w_attn_out", "row"),
        ("w_o", "row"), ("w_cq", "row"), ("w_ckv", "col"), ("w_co", "row"), ("w_ffn2_gu", "col"), ("w_ffn2_down", "row"))
VECS = ("g_ffn1", "g_mix", "g_cross", "g_mem", "g_ffn2", "g_final")
WEIGHTS = ("g_ffn1", "w_ffn1_gu", "w_ffn1_down", "g_mix", "w_in", "b_gate", "conv_w", "w_conv_out", "w_attn_out", "w_o",
           "g_cross", "g_mem", "w_cq", "w_ckv", "w_co", "g_ffn2", "w_ffn2_gu", "w_ffn2_down", "g_final")
CONV_ROWS = 8


def _full_shape(kind, r, c):
    return (r, N_CHIPS * c) if kind == "col" else (N_CHIPS * r, c)


def _piece(ref, kind, r, c, chip, half):
    hr = r // 2
    if kind == "col":
        return ref.at[pl.ds(pl.multiple_of(half * hr, 16), hr), pl.ds(pl.multiple_of(chip * c, LANES), c)]
    return ref.at[pl.ds(pl.multiple_of(chip * r + half * hr, 16), hr), :]


def _shard_of(ref, kind, r, c, chip):
    if kind == "col":
        return ref.at[:, pl.ds(pl.multiple_of(chip * c, LANES), c)]
    return ref.at[pl.ds(pl.multiple_of(chip * r, 16), r), :]


def _place():
    x, y, c = lax.axis_index("x"), lax.axis_index("y"), lax.axis_index("c")
    others = [(1 - x, y), (x, 1 - y), (1 - x, 1 - y)]
    return x, y, c, 2 * x + y, others


def _remote(src, dst, send_sem, recv_sem, to):
    return pltpu.make_async_remote_copy(src_ref=src, dst_ref=dst, send_sem=send_sem, recv_sem=recv_sem,
                                        device_id=to, device_id_type=MESH)


def _gather_conv(conv_shard):
    cc = conv_shard.shape[1]

    def body(conv_ref, conv_full, cs, cr, cl):
        x, y, c, me, others = _place()

        def cols(chip):
            return conv_full.at[:, pl.ds(pl.multiple_of(chip * cc, LANES), cc)]

        def conv(k, chip_from, to):
            return _remote(conv_ref, cols(chip_from), cs.at[k], cr.at[k], to)

        mine = pltpu.make_async_copy(conv_ref, cols(me), cl.at[0])
        mine.start()
        for k, (ox, oy) in enumerate(others):
            conv(k, me, (ox, oy, c)).start()
        for k, (ox, oy) in enumerate(others):
            conv(k, 2 * ox + oy, (x, y, c)).wait_recv()
            conv(k, me, (ox, oy, c)).wait_send()
        mine.wait()

    dma = pltpu.SemaphoreType.DMA
    return _pcall(
        body, name="gather_conv", in_specs=[ANY], out_specs=ANY,
        out_shape=jax.ShapeDtypeStruct((CONV_ROWS, N_CHIPS * cc), F32), scratch_shapes=[dma((3,)), dma((3,)), dma((1,))],
    )(conv_shard)


HBM = pl.BlockSpec(memory_space=pltpu.HBM)
SEM = pl.BlockSpec(memory_space=pltpu.SEMAPHORE)
EFFECT = pltpu.SideEffectType.DATAFLOW_SIDE_EFFECTING
TOKEN = (8, LANES)


def _split_start(name, plan, n_copies, srcs, lands, after=None):
    ns, nl = len(srcs), len(lands)
    n_in = ns + nl + (after is not None)

    def body(*refs):
        outs = refs[n_in:]
        sends, _ = plan(refs[:ns], refs[ns:ns + nl], outs[0], outs[1])
        for cp in sends:
            cp.start()
        outs[-1][...] = jnp.zeros(TOKEN, F32)

    held = [pltpu.HBM(a.shape, a.dtype) for a in (*srcs, *lands)]
    dma = pltpu.SemaphoreType.DMA((n_copies,))
    ins = [pltpu.with_memory_space_constraint(a, pltpu.HBM) for a in (*srcs, *lands)]
    outs = _pcall(
        body, name=name, in_specs=[HBM] * (ns + nl) + ([] if after is None else [ANY]),
        out_specs=(SEM, SEM, *[HBM] * (ns + nl), pl.BlockSpec(memory_space=pltpu.VMEM)),
        out_shape=(dma, dma, *held, jax.ShapeDtypeStruct(TOKEN, F32)),
        input_output_aliases={i: 2 + i for i in range(ns + nl)},
        compiler_params=pltpu.CompilerParams(has_side_effects=EFFECT),
    )(*ins, *([] if after is None else [after]))
    return outs[0], outs[1], list(outs[2:2 + ns]), list(outs[2 + ns:2 + ns + nl]), outs[-1]


def _split_wait(name, plan, send_sems, recv_sems, srcs, lands, after):
    ns, nl = len(srcs), len(lands)

    def body(*refs):
        sends, recvs = plan(refs[:ns], refs[ns:ns + nl], refs[ns + nl], refs[ns + nl + 1])
        for cp in sends:
            cp.wait_send()
        for cp in recvs:
            cp.wait_recv()

    outs = _pcall(
        body, name=name, in_specs=[HBM] * (ns + nl) + [SEM, SEM, ANY], out_specs=[HBM] * (ns + nl),
        out_shape=[pltpu.HBM(a.shape, a.dtype) for a in (*srcs, *lands)],
        input_output_aliases={i: i for i in range(ns + nl)},
        compiler_params=pltpu.CompilerParams(has_side_effects=EFFECT),
    )(*srcs, *lands, send_sems, recv_sems, after)
    return list(outs[:ns]), list(outs[ns:])


def _gather_plan(dims):
    def plan(shard_refs, full_refs, ss, rs):
        x, y, c, me, others = _place()
        sends, recvs = [], []
        for wi, (kind, r, cw) in enumerate(dims):
            half = shard_refs[wi].at[pl.ds(pl.multiple_of(c * (r // 2), 16), r // 2), :]
            for k, (ox, oy) in enumerate(others):
                sem = 4 * wi + k
                sends.append(_remote(half, _piece(full_refs[wi], kind, r, cw, me, c), ss.at[sem], rs.at[sem], (ox, oy, c)))
                recvs.append(_remote(half, _piece(full_refs[wi], kind, r, cw, 2 * ox + oy, c), ss.at[sem], rs.at[sem], (x, y, c)))
            sem = 4 * wi + 3
            own = _remote(shard_refs[wi], _shard_of(full_refs[wi], kind, r, cw, me), ss.at[sem], rs.at[sem], (x, y, 1 - c))
            sends.append(own)
            recvs.append(own)
        return sends, recvs

    return plan


def _forward_plan(dims):
    def plan(_, full_refs, ss, rs):
        x, y, c, _, others = _place()
        sends, recvs = [], []
        for wi, (kind, r, cw) in enumerate(dims):
            for k, (ox, oy) in enumerate(others):
                sem = 3 * wi + k
                mine = _piece(full_refs[wi], kind, r, cw, 2 * ox + oy, c)
                theirs = _piece(full_refs[wi], kind, r, cw, 2 * ox + oy, 1 - c)
                sends.append(_remote(mine, mine, ss.at[sem], rs.at[sem], (x, y, 1 - c)))
                recvs.append(_remote(theirs, theirs, ss.at[sem], rs.at[sem], (x, y, 1 - c)))
        return sends, recvs

    return plan


def _rs_cores_plan(dims):
    def plan(g_refs, land_refs, ss, rs):
        x, y, c, _, _ = _place()
        sends, recvs = [], []
        for wi, dm in enumerate(dims):
            for chip in range(N_CHIPS):
                sem = N_CHIPS * wi + chip
                sends.append(_remote(_piece(g_refs[wi], *dm, chip, 1 - c), land_refs[wi].at[chip], ss.at[sem], rs.at[sem], (x, y, 1 - c)))
                recvs.append(_remote(_piece(g_refs[wi], *dm, chip, c), land_refs[wi].at[chip], ss.at[sem], rs.at[sem], (x, y, 1 - c)))
        return sends, recvs

    return plan


def _share_plan(nw):
    def plan(_, buf_refs, ss, rs):
        x, y, c, _, _ = _place()
        sends = [_remote(buf_refs[wi].at[c], buf_refs[wi].at[c], ss.at[wi], rs.at[wi], (x, y, 1 - c)) for wi in range(nw)]
        recvs = [_remote(buf_refs[wi].at[1 - c], buf_refs[wi].at[1 - c], ss.at[wi], rs.at[wi], (x, y, 1 - c)) for wi in range(nw)]
        return sends, recvs

    return plan


def _small_plan():
    def plan(_, buf_refs, ss, rs):
        x, y, c = lax.axis_index("x"), lax.axis_index("y"), lax.axis_index("c")
        buf = buf_refs[0]
        sends, recvs = [], []
        for rel in range(1, N_DEV):
            peer = (x ^ (rel >> 2 & 1), y ^ (rel >> 1 & 1), c ^ (rel & 1))
            sends.append(_remote(buf.at[0], buf.at[rel], ss.at[rel - 1], rs.at[rel - 1], peer))
            recvs.append(_remote(buf.at[0], buf.at[rel], ss.at[rel - 1], rs.at[rel - 1], peer))
        return sends, recvs

    return plan


def _sum_small(buf, me, name):
    _, rows, n = buf.shape

    def body(me_ref, b_ref, o_ref):
        tot = b_ref[me_ref[0]]
        for dev in range(1, N_DEV):
            tot = tot + b_ref[dev ^ me_ref[0]]
        o_ref[...] = tot

    return _pcall(
        body, name=name, out_shape=jax.ShapeDtypeStruct((rows, n), F32),
        grid_spec=pltpu.PrefetchScalarGridSpec(
            num_scalar_prefetch=1, grid=(1,), in_specs=[pl.BlockSpec((N_DEV, rows, n), lambda i, m: (0, 0, 0))],
            out_specs=pl.BlockSpec((rows, n), lambda i, m: (0, 0))),
    )(me, buf)


def _rs_chips_plan(nw):
    def plan(p_refs, land_refs, ss, rs):
        x, y, c, me, others = _place()
        sends, recvs = [], []
        for wi in range(nw):
            for k, (ox, oy) in enumerate(others):
                sem = 3 * wi + k
                sends.append(_remote(p_refs[wi].at[2 * ox + oy], land_refs[wi].at[k], ss.at[sem], rs.at[sem], (ox, oy, c)))
                recvs.append(_remote(p_refs[wi].at[me], land_refs[wi].at[k], ss.at[sem], rs.at[sem], (x, y, c)))
        return sends, recvs

    return plan


def _rows_per_block(n, c, limit_bytes=2 << 20):
    best = None
    for tm in range(16, n + 1, 16):
        if n % tm == 0 and tm * c * 4 <= limit_bytes:
            best = tm
    return best or n


def _sum_cores(grad, got, kind, place, name):
    _, hr, cw = got.shape
    tm = _rows_per_block(hr, cw)
    nb = hr // tm

    def body(place_ref, g_ref, t_ref, o_ref):
        o_ref[...] = (g_ref[...].astype(F32) + t_ref[...].astype(F32)).astype(o_ref.dtype)

    if kind == "col":
        g_spec = pl.BlockSpec((tm, cw), lambda j, i, pr: (pr[0] * nb + i, j))
    else:
        g_spec = pl.BlockSpec((tm, cw), lambda j, i, pr: ((2 * j + pr[0]) * nb + i, 0))
    blk = pl.BlockSpec((None, tm, cw), lambda j, i, pr: (j, i, 0))
    return _pcall(
        body, name=name, out_shape=jax.ShapeDtypeStruct(got.shape, BF16),
        grid_spec=pltpu.PrefetchScalarGridSpec(num_scalar_prefetch=1, grid=(N_CHIPS, nb), in_specs=[g_spec, blk], out_specs=blk),
        compiler_params=_params("parallel", "parallel"),
    )(place, grad, got)


def _sum_chips(parts, got, place, name):
    _, n, cw = got.shape
    tm = _rows_per_block(n, cw)

    def body(place_ref, p_ref, g_ref, o_ref):
        tot = p_ref[...].astype(F32)
        for k in range(3):
            tot = tot + g_ref[k].astype(F32)
        o_ref[...] = tot

    return _pcall(
        body, name=name, out_shape=jax.ShapeDtypeStruct((2, n, cw), F32),
        grid_spec=pltpu.PrefetchScalarGridSpec(
            num_scalar_prefetch=1, grid=(n // tm,),
            in_specs=[pl.BlockSpec((None, tm, cw), lambda i, pr: (pr[1], i, 0)), pl.BlockSpec((3, tm, cw), lambda i, pr: (0, i, 0))],
            out_specs=pl.BlockSpec((None, tm, cw), lambda i, pr: (pr[0], i, 0))),
        compiler_params=_params("parallel"),
    )(place, parts, got)


def _adamw(g, w, m, v, name):
    n, c = g.shape
    c1 = 1.0 - ADAM_B1 ** ADAM_STEP
    c2 = 1.0 - ADAM_B2 ** ADAM_STEP

    def fn(gb, wb, mb, vb):
        m_new = ADAM_B1 * mb + (1.0 - ADAM_B1) * gb
        v_new = ADAM_B2 * vb + (1.0 - ADAM_B2) * (gb * gb)
        delta = -ADAM_LR * ((m_new / c1) / (jnp.sqrt(v_new / c2) + ADAM_EPS) + ADAM_WD * wb)
        return gb, delta, m_new, v_new

    tm = _rows_per_block(n, c) if n % 16 == 0 else n
    return _rowcall(fn, [_whole(g), _whole(w), _whole(m), _whole(v)], [], [(c, F32)] * 4, tm=tm, name=name)


PACK_ROWS = 16


def _pack_rows(parts, width, name, after=None):
    assert sum(p.shape[0] for p in parts) <= PACK_ROWS

    def body(*refs):
        out_ref = refs[-1]
        out_ref[...] = jnp.zeros_like(out_ref)
        at = 0
        for r in refs[:len(parts)]:
            k, n = r.shape
            if n == width:
                out_ref[at:at + k, :] = r[...]
            else:
                out_ref[at:at + k, :] = jnp.broadcast_to(r[:, :1], (k, width))
            at += k

    vm = pl.BlockSpec(memory_space=pltpu.VMEM)
    return _pcall(body, name=name, in_specs=[vm] * len(parts) + ([] if after is None else [ANY]), out_specs=vm,
                  out_shape=jax.ShapeDtypeStruct((PACK_ROWS, width), F32))(*parts, *([] if after is None else [after]))


def _cast_shard(wm, name, after):
    n, c = wm.shape
    return _rowcall(lambda v: v, [_whole(wm)], [], [(c, BF16)], tm=_rows_per_block(n, c), name=name, after=after)[0]


GATHER_GROUPS = (
    ("w_ffn1_gu",), ("w_ffn1_down",), ("w_in",), ("w_conv_out", "w_attn_out", "w_o"), ("w_cq", "w_ckv", "w_co"),
    ("w_ffn2_gu", "w_ffn2_down"),
)
REDUCE_GROUPS = {
    "ffn2": ("w_ffn2_down", "w_ffn2_gu"),
    "cross": ("w_co", "w_cq", "w_ckv"),
    "mix": ("w_o", "w_conv_out", "w_attn_out", "w_in"),
    "ffn1_down": ("w_ffn1_down",),
    "ffn1": ("w_ffn1_gu",),
}
TAIL_STAGES = (("ffn2", "cross"), ("mix",), ("ffn1_down", "ffn1"))
KIND = dict(MATS)


def _step(x, mem, tgt, wts, m_in, v_in):
    d = x.shape[-1]
    cc = wts["conv_w"].shape[1]
    place = jnp.stack([lax.axis_index("c"), 2 * lax.axis_index("x") + lax.axis_index("y")]).astype(jnp.int32)
    dims = {n: (kind, *wts[n].shape) for n, kind in MATS}

    conv_full = _gather_conv(jnp.pad(wts["conv_w"], ((0, CONV_ROWS - CONV_K), (0, 0))))
    w = {n: wts[n].reshape(1, -1) for n in VECS + ("b_gate",)}
    w["conv_w"] = conv_full[:CONV_K]
    flying, token = {}, conv_full
    for names in GATHER_GROUPS:
        gd = [dims[n] for n in names]
        shards = [_cast_shard(wts[n], "cast_" + n, token) for n in names]
        lands = [lax.empty(_full_shape(*dm), BF16) for dm in gd]
        plan = _gather_plan(gd)
        ss, rs, srcs, lands, token = _split_start("gather_start_" + names[0], plan, 4 * len(names), shards, lands, token)
        flying.update({n: (names, plan, ss, rs, srcs, lands, gd) for n in names})

    passing = {}

    def prefetch(name, after):
        if name not in passing:
            names, plan, ss, rs, srcs, lands, gd = flying[name]
            _, lands = _split_wait("gather_wait_" + names[0], plan, ss, rs, srcs, lands, after)
            plan = _forward_plan(gd)
            ss, rs, _, lands, _ = _split_start("forward_start_" + names[0], plan, 3 * len(names), [], lands)
            passing.update({n: (names, plan, ss, rs, lands) for n in names})

    def fetch(name, after):
        prefetch(name, after)
        names, plan, ss, rs, lands = passing[name]
        _, lands = _split_wait("forward_wait_" + names[0], plan, ss, rs, [], lands, after)
        return dict(zip(names, lands))

    swapping, sent = {}, {}

    def emit(tag, g):
        if tag not in REDUCE_GROUPS:
            return None
        names = REDUCE_GROUPS[tag]
        gd = [dims[n] for n in names]
        lands = [lax.empty((N_CHIPS, r // 2, cw), BF16) for (_, r, cw) in gd]
        plan = _rs_cores_plan(gd)
        ss, rs, srcs, lands, tok = _split_start("rs_cores_start_" + tag, plan, N_CHIPS * len(names), [g[n] for n in names], lands)
        swapping[tag] = (plan, ss, rs, srcs, lands)
        return tok

    def tick(tag, after):
        if tag not in REDUCE_GROUPS:
            return None
        names = REDUCE_GROUPS[tag]
        plan, ss, rs, srcs, lands = swapping[tag]
        mine, got = _split_wait("rs_cores_wait_" + tag, plan, ss, rs, srcs, lands, after)
        parts = [_sum_cores(gm, t, KIND[n], place, "sum_cores_" + n) for n, gm, t in zip(names, mine, got)]
        lands = [lax.empty((3, *p.shape[1:]), BF16) for p in parts]
        plan = _rs_chips_plan(len(names))
        ss, rs, srcs, lands, tok = _split_start("rs_chips_start_" + tag, plan, 3 * len(names), parts, lands)
        sent[tag] = (plan, ss, rs, srcs, lands)
        return tok

    loss_lanes, dx, g, last = _local_step(x[0], mem[0], tgt[0], w, fetch, prefetch, emit, tick, token)

    rows = [g[n] for n in VECS] + [g["b_gate"][:, :d], g["b_gate"][:, d:], g["conv_w"], loss_lanes]
    packed = _pack_rows(rows, d, "pack_small", after=last)
    small = jnp.concatenate([packed[None], jnp.zeros((N_DEV - 1, *packed.shape), F32)], axis=0)
    small_plan = _small_plan()
    small_ss, small_rs, _, small, after = _split_start("small_start", small_plan, N_DEV - 1, [], [small])

    grads, out = {}, {}

    def update(n):
        shape = wts[n].shape
        as2d = (lambda a: a.reshape(1, -1)) if len(shape) == 1 else (lambda a: a)
        return [r.reshape(shape) for r in _adamw(grads[n], as2d(wts[n]), as2d(m_in[n]), as2d(v_in[n]), "adamw_" + n)]

    def finish(sharing, after):
        tag, names, plan, ss, rs, halves = sharing
        _, both = _split_wait("share_wait_" + tag, plan, ss, rs, [], halves, after)
        for n, b in zip(names, both):
            grads[n] = b.reshape(-1, b.shape[-1])
            out[n] = update(n)
        return out[names[-1]][1]

    sharing = None
    for stage in TAIL_STAGES:
        names, halves = [], []
        for tag in stage:
            plan, ss, rs, srcs, lands = sent[tag]
            parts, landed = _split_wait("rs_chips_wait_" + tag, plan, ss, rs, srcs, lands, after)
            halves += [_sum_chips(p, t, place, "sum_chips_" + n) for n, p, t in zip(REDUCE_GROUPS[tag], parts, landed)]
            names += REDUCE_GROUPS[tag]
        plan = _share_plan(len(names))
        ss, rs, _, halves, after = _split_start("share_start_" + stage[0], plan, len(names), [], halves)
        if sharing is not None:
            after = finish(sharing, after)
        sharing = (stage[0], names, plan, ss, rs, halves)
    after = finish(sharing, after)

    _, small = _split_wait("small_wait", small_plan, small_ss, small_rs, [], small, after)
    me = (4 * lax.axis_index("x") + 2 * lax.axis_index("y") + lax.axis_index("c")).astype(jnp.int32).reshape(1)
    red = _sum_small(small[0], me, "sum_small")
    grads.update({n: red[i:i + 1] for i, n in enumerate(VECS)})
    nv = len(VECS)
    grads["b_gate"] = jnp.concatenate([red[nv:nv + 1], red[nv + 1:nv + 2]], axis=1)
    chip = 2 * lax.axis_index("x") + lax.axis_index("y")
    grads["conv_w"] = lax.dynamic_slice_in_dim(red[nv + 2:nv + 2 + CONV_K], chip * cc, cc, axis=1)
    loss = red[nv + 2 + CONV_K, 0]
    out.update({n: update(n) for n in WEIGHTS if n not in KIND})
    return (loss, dx[None], *[out[n][0] for n in WEIGHTS], *[out[n][1] for n in WEIGHTS],
            *[out[n][2] for n in WEIGHTS], *[out[n][3] for n in WEIGHTS])


def kernel(x, mem, g_ffn1, w_ffn1_gu, w_ffn1_down, g_mix, w_in, b_gate, conv_w, w_conv_out, w_attn_out, w_o, g_cross, g_mem, w_cq, w_ckv, w_co, g_ffn2, w_ffn2_gu, w_ffn2_down, g_final, loss_target, m_g_ffn1, m_w_ffn1_gu, m_w_ffn1_down, m_g_mix, m_w_in, m_b_gate, m_conv_w, m_w_conv_out, m_w_attn_out, m_w_o, m_g_cross, m_g_mem, m_w_cq, m_w_ckv, m_w_co, m_g_ffn2, m_w_ffn2_gu, m_w_ffn2_down, m_g_final, v_g_ffn1, v_w_ffn1_gu, v_w_ffn1_down, v_g_mix, v_w_in, v_b_gate, v_conv_w, v_w_conv_out, v_w_attn_out, v_w_o, v_g_cross, v_g_mem, v_w_cq, v_w_ckv, v_w_co, v_g_ffn2, v_w_ffn2_gu, v_w_ffn2_down, v_g_final):
    given = dict(locals())
    wts = {n: given[n] for n in WEIGHTS}
    m_in = {n: given["m_" + n] for n in WEIGHTS}
    v_in = {n: given["v_" + n] for n in WEIGHTS}
    return _step(x, mem, loss_target, wts, m_in, v_in)
```

```python
import jax
import jax.numpy as jnp
from jax import lax
from jax.experimental import pallas as pl
from jax.experimental.pallas import tpu as pltpu

F32 = jnp.float32
BF16 = jnp.bfloat16
MESH = pl.DeviceIdType.MESH

V7X_VMEM_LIMIT_BYTES = 48 * 1024 * 1024
MM_VMEM_BUDGET_BYTES = 36 * 1024 * 1024
MM_WHOLE_K = 2816
LANES = 128
SB_HEAD_DIM = 128
X_HEADS = 4
CONV_K = 3
RMS_EPS = 1e-6
N_CHIPS = 4
N_DEV = 8
ADAM_LR, ADAM_B1, ADAM_B2, ADAM_EPS, ADAM_WD, ADAM_STEP = 0.001, 0.9, 0.999, 1e-08, 0.01, 10


ANY = pl.BlockSpec(memory_space=pl.ANY)


def _pcall(body, **kw):
    return pl.pallas_call(body, **kw)


def _params(*sem):
    return pltpu.CompilerParams(dimension_semantics=sem, vmem_limit_bytes=V7X_VMEM_LIMIT_BYTES)


def _pick(dim, cands):
    for c in cands:
        if dim % c == 0:
            return c
    return dim


def _dot(a, b, ca, cb):
    return lax.dot_general(a, b, (((ca,), (cb,)), ((), ())), preferred_element_type=F32)


def _mm(a, b, *, name, ta=False, tb=False, out_dtype=BF16, res=None, alpha=1.0, tm=None, tn=None, tk=None, after=None,
        a_halves=False, b_halves=False):
    assert not (a_halves and ta) and not (b_halves and tb)
    if a_halves:
        m, k = a.shape[1], 2 * a.shape[2]
    else:
        m, k = (a.shape[1], a.shape[0]) if ta else a.shape
    if b_halves:
        n = 2 * b.shape[2]
        assert k == b.shape[1]
    else:
        n = b.shape[0] if tb else b.shape[1]
        assert k == (b.shape[1] if tb else b.shape[0]), (a.shape, b.shape, ta, tb)
    if ta:
        tm = tm or _pick(m, (512, 256, 128))
        tn = tn or _pick(n, (1024, 512, 256, 128))
        tk = tk or (k if k <= MM_WHOLE_K else _pick(k, (1024, 512, 256, 128)))
    else:
        tk = tk or (k if k <= MM_WHOLE_K else _pick(k, (MM_WHOLE_K, 2048, 1024, 512, 256, 128)))
        tn = tn or _pick(n, (512, 1408, 256, 128) if tk == k else (1024, 512, 256, 128))
        per_row = 2 * (tk * a.dtype.itemsize + tn * (jnp.dtype(out_dtype).itemsize + (0 if res is None else res.dtype.itemsize)))
        per_row += 4 * tn if tk < k else 0
        rows = (MM_VMEM_BUDGET_BYTES - 2 * tk * tn * b.dtype.itemsize) // per_row
        tm = tm or next((c for c in (2048, 1024, 512, 256, 128) if m % c == 0 and c <= rows), m)
    if a_halves:
        tk = min(tk, k // 2) if (k // 2) % min(tk, k // 2) == 0 else _pick(k // 2, (1408, 1024, 512, 256, 128))
    if b_halves:
        tn = tn if (n // 2) % tn == 0 else _pick(n // 2, (1408, 1024, 512, 256, 128))
    nk = k // tk
    assert m % tm == 0 and n % tn == 0 and k % tk == 0
    a_spec = pl.BlockSpec((tk, tm), lambda i, j, kk: (kk, i)) if ta else pl.BlockSpec((tm, tk), lambda i, j, kk: (i, kk))
    b_spec = pl.BlockSpec((tn, tk), lambda i, j, kk: (j, kk)) if tb else pl.BlockSpec((tk, tn), lambda i, j, kk: (kk, j))
    if a_halves:
        per = (k // 2) // tk
        a_spec = pl.BlockSpec((None, tm, tk), lambda i, j, kk: (kk // per, i, kk % per))
    if b_halves:
        per_n = (n // 2) // tn
        b_spec = pl.BlockSpec((None, tk, tn), lambda i, j, kk: (j // per_n, kk, j % per_n))
    o_spec = pl.BlockSpec((tm, tn), lambda i, j, kk: (i, j))
    ca, cb = (0 if ta else 1), (1 if tb else 0)

    n_in = 2 + (res is not None) + (after is not None)

    def body(*refs):
        a_ref, b_ref = refs[:2]
        res_ref = refs[2] if res is not None else None
        o_ref = refs[n_in]
        scratch = refs[n_in + 1:]

        def finish(acc):
            val = acc if alpha == 1.0 else alpha * acc
            if res_ref is not None:
                val = res_ref[...].astype(F32) + val
            o_ref[...] = val.astype(o_ref.dtype)

        part = _dot(a_ref[...].astype(BF16), b_ref[...].astype(BF16), ca, cb)
        if nk == 1:
            finish(part)
        else:
            acc_ref = scratch[0]
            kk = pl.program_id(2)

            @pl.when(kk == 0)
            def _():
                acc_ref[...] = part

            @pl.when(kk > 0)
            def _():
                acc_ref[...] += part

            @pl.when(kk == nk - 1)
            def _():
                finish(acc_ref[...])

    ins = [a, b] + ([] if res is None else [res]) + ([] if after is None else [after])
    in_specs = [a_spec, b_spec] + ([] if res is None else [o_spec]) + ([] if after is None else [ANY])
    return _pcall(
        body, name=name, grid=(m // tm, n // tn, nk), in_specs=in_specs, out_specs=o_spec,
        out_shape=jax.ShapeDtypeStruct((m, n), out_dtype),
        scratch_shapes=[pltpu.VMEM((tm, tn), F32)] if nk > 1 else [],
        compiler_params=_params("parallel", "parallel", "arbitrary"),
    )(*ins)


def _rowcall(fn, rows, consts, outs, accs=(), *, tm, name, after=None):
    s = rows[0][0].shape[0]
    assert s % tm == 0
    n_read, n_out = len(rows) + len(consts), len(outs)
    n_in = n_read + (after is not None)

    def body(*refs):
        vals = fn(*[r[...] for r in refs[:n_read]])
        vals = vals if isinstance(vals, (tuple, list)) else (vals,)
        for o_ref, v in zip(refs[n_in:n_in + n_out], vals[:n_out]):
            o_ref[...] = v.astype(o_ref.dtype)
        if accs:
            first = pl.program_id(0) == 0
            for a_ref, v in zip(refs[n_in + n_out:], vals[n_out:]):
                tot = jnp.sum(v.astype(F32), axis=0, keepdims=True)

                @pl.when(first)
                def _(a_ref=a_ref, tot=tot):
                    a_ref[...] = tot

                @pl.when(jnp.logical_not(first))
                def _(a_ref=a_ref, tot=tot):
                    a_ref[...] += tot

    in_specs = [pl.BlockSpec((tm, w), lambda i, cb=cb: (i, cb)) for (_, cb, w) in rows]
    in_specs += [pl.BlockSpec(c.shape, lambda i: (0, 0)) for c in consts]
    in_specs += [] if after is None else [ANY]
    out_specs = [pl.BlockSpec((tm, w), lambda i: (i, 0)) for (w, _) in outs]
    out_specs += [pl.BlockSpec((1, w), lambda i: (0, 0)) for w in accs]
    out_shape = [jax.ShapeDtypeStruct((s, w), dt) for (w, dt) in outs]
    out_shape += [jax.ShapeDtypeStruct((1, w), F32) for w in accs]
    return _pcall(
        body, name=name, grid=(s // tm,), in_specs=in_specs, out_specs=out_specs, out_shape=out_shape,
        compiler_params=_params("arbitrary" if accs else "parallel"),
    )(*[r[0] for r in rows], *consts, *([] if after is None else [after]))


def _whole(a):
    return (a, 0, a.shape[1])


def _xhat(x):
    x = x.astype(F32)
    r = lax.rsqrt(jnp.mean(x * x, axis=-1, keepdims=True) + RMS_EPS)
    return x * r, r


def _rms_bwd(dy, x, g):
    xh, r = _xhat(x)
    dxh = dy.astype(F32) * g
    dx = r * (dxh - xh * jnp.mean(dxh * xh, axis=-1, keepdims=True))
    return dx, dy.astype(F32) * xh


def _sigmoid(x):
    return 1.0 / (1.0 + jnp.exp(-x))


def _rms_fwd(x, g, name, tm, after=None):
    d = x.shape[1]
    return _rowcall(lambda xb, gb: _xhat(xb)[0] * gb, [_whole(x)], [g], [(d, BF16)], tm=tm, name=name, after=after)[0]


def _silu_parts(gate):
    sg = _sigmoid(gate)
    return sg, gate * sg


def _ffn_up(n, w_gu, name):
    s, d = n.shape
    f = w_gu.shape[1] // 2
    tn = _pick(f, (1408, 1024, 512, 256, 128))
    tm = _pick(s, (1024, 512, 256, 128))
    nb = f // tn

    def body(n_ref, wg_ref, wu_ref, gu_ref, act_ref):
        nv = n_ref[...]
        gate = _dot(nv, wg_ref[...], 1, 0)
        up = _dot(nv, wu_ref[...], 1, 0)
        gu_ref[0] = gate.astype(gu_ref.dtype)
        gu_ref[1] = up.astype(gu_ref.dtype)
        act_ref[...] = (_silu_parts(gate)[1] * up).astype(act_ref.dtype)

    return _pcall(
        body, name=name, grid=(s // tm, nb),
        in_specs=[pl.BlockSpec((tm, d), lambda i, j: (i, 0)), pl.BlockSpec((d, tn), lambda i, j: (0, j)),
                  pl.BlockSpec((d, tn), lambda i, j: (0, nb + j))],
        out_specs=[pl.BlockSpec((2, tm, tn), lambda i, j: (0, i, j)), pl.BlockSpec((tm, tn), lambda i, j: (i, j))],
        out_shape=[jax.ShapeDtypeStruct((2, s, f), BF16), jax.ShapeDtypeStruct((s, f), BF16)],
        compiler_params=_params("parallel", "parallel"),
    )(n, w_gu, w_gu)


def _ffn_dgu(dhb, w_down, gu, name, after=None):
    s, d = dhb.shape
    f = w_down.shape[0]
    tn = _pick(f, (1408, 1024, 512, 256, 128))
    tm = _pick(s, (1024, 512, 256, 128))

    def body(dh_ref, w_ref, gu_ref, *rest):
        o_ref = rest[-1]
        dact = _dot(dh_ref[...], w_ref[...], 1, 1)
        gate, up = gu_ref[0].astype(F32), gu_ref[1].astype(F32)
        sg, silu = _silu_parts(gate)
        o_ref[0] = (dact * up * (sg + silu * (1.0 - sg))).astype(o_ref.dtype)
        o_ref[1] = (dact * silu).astype(o_ref.dtype)

    blk = pl.BlockSpec((2, tm, tn), lambda i, j: (0, i, j))
    return _pcall(
        body, name=name, grid=(s // tm, f // tn),
        in_specs=[pl.BlockSpec((tm, d), lambda i, j: (i, 0)), pl.BlockSpec((tn, d), lambda i, j: (j, 0)), blk]
        + ([] if after is None else [ANY]),
        out_specs=blk, out_shape=jax.ShapeDtypeStruct((2, s, f), BF16), compiler_params=_params("parallel", "parallel"),
    )(dhb, w_down, gu, *([] if after is None else [after]))


def _dgrad_norm(dy, wmat, dh, x, g, name, *, dy_halves=False, copy_scale=None, after=None):
    s, d = dh.shape
    k = wmat.shape[1]
    tk = k if k <= MM_WHOLE_K else _pick(k, (MM_WHOLE_K, 2048, 1024, 512, 256, 128))
    if dy_halves and (k // 2) % tk:
        tk = _pick(k // 2, (1408, 1024, 512, 256, 128))
    tm = _pick(s, (512, 256, 128))
    nk, per = k // tk, (k // 2) // tk if dy_halves else 0
    n_in = 5 + (after is not None)
    n_out = 2 + (copy_scale is not None)

    def body(*refs):
        dy_ref, w_ref, dh_ref, x_ref, g_ref = refs[:5]
        outs, scratch = refs[n_in:n_in + n_out], refs[n_in + n_out:]
        i, kk = pl.program_id(0), pl.program_id(1)
        part = _dot(dy_ref[...], w_ref[...], 1, 1)

        def finish(dn):
            dx, dg = _rms_bwd(dn, x_ref[...], g_ref[...])
            tot = dh_ref[...] + dx
            outs[0][...] = tot
            if copy_scale is not None:
                outs[1][...] = (copy_scale * tot).astype(outs[1].dtype)
            dg = jnp.sum(dg, axis=0, keepdims=True)

            @pl.when(i == 0)
            def _():
                outs[-1][...] = dg

            @pl.when(i > 0)
            def _():
                outs[-1][...] += dg

        if nk == 1:
            finish(part)
        else:
            acc_ref = scratch[0]

            @pl.when(kk == 0)
            def _():
                acc_ref[...] = part

            @pl.when(kk > 0)
            def _():
                acc_ref[...] += part

            @pl.when(kk == nk - 1)
            def _():
                finish(acc_ref[...])

    row = pl.BlockSpec((tm, d), lambda i, kk: (i, 0))
    dy_spec = pl.BlockSpec((None, tm, tk), lambda i, kk: (kk // per, i, kk % per)) if dy_halves else pl.BlockSpec((tm, tk), lambda i, kk: (i, kk))
    in_specs = [dy_spec, pl.BlockSpec((d, tk), lambda i, kk: (0, kk)), row, row, pl.BlockSpec((1, d), lambda i, kk: (0, 0))]
    out_specs = [row] * (n_out - 1) + [pl.BlockSpec((1, d), lambda i, kk: (0, 0))]
    out_shape = [jax.ShapeDtypeStruct((s, d), F32)] + ([] if copy_scale is None else [jax.ShapeDtypeStruct((s, d), BF16)])
    return _pcall(
        body, name=name, grid=(s // tm, nk), in_specs=in_specs + ([] if after is None else [ANY]), out_specs=out_specs,
        out_shape=out_shape + [jax.ShapeDtypeStruct((1, d), F32)], scratch_shapes=[pltpu.VMEM((tm, d), F32)] if nk > 1 else [],
        compiler_params=_params("arbitrary", "arbitrary"),
    )(dy, wmat, dh, x, g, *([] if after is None else [after]))


def _shift_down(p, k):
    if k == 0:
        return p
    rows = lax.broadcasted_iota(jnp.int32, p.shape, 0)
    return jnp.where(rows >= k, pltpu.roll(p, k, 0), 0.0)


def _shift_up(p, k):
    if k == 0:
        return p
    s = p.shape[0]
    rows = lax.broadcasted_iota(jnp.int32, p.shape, 0)
    return jnp.where(rows < s - k, pltpu.roll(p, s - k, 0), 0.0)


def _conv_fwd(proj, conv_w, d, tc, name):
    s = proj.shape[0]
    nb = d // tc

    def body(cb_ref, cc_ref, cx_ref, w_ref, y_ref):
        p = cc_ref[...].astype(F32) * cx_ref[...].astype(F32)
        w = w_ref[...]
        acc = p * w[CONV_K - 1:CONV_K, :]
        for k in range(1, CONV_K):
            acc = acc + _shift_down(p, k) * w[CONV_K - 1 - k:CONV_K - k, :]
        y_ref[...] = (cb_ref[...].astype(F32) * acc).astype(y_ref.dtype)

    col = lambda off: pl.BlockSpec((s, tc), lambda j: (0, off * nb + j))
    return _pcall(
        body, name=name, grid=(nb,), in_specs=[col(0), col(1), col(2), pl.BlockSpec((CONV_K, tc), lambda j: (0, j))],
        out_specs=pl.BlockSpec((s, tc), lambda j: (0, j)), out_shape=jax.ShapeDtypeStruct((s, d), BF16),
        compiler_params=_params("parallel"),
    )(proj, proj, proj, conv_w)


def _conv_bwd(dy, proj, conv_w, d, tc, name):
    s = proj.shape[0]
    nb = d // tc

    def body(dy_ref, cb_ref, cc_ref, cx_ref, w_ref, dcb_ref, dcc_ref, dcx_ref, dw_ref):
        cc, cx = cc_ref[...].astype(F32), cx_ref[...].astype(F32)
        p = cc * cx
        w = w_ref[...]
        dyv = dy_ref[...].astype(F32)
        shifted = [_shift_down(p, CONV_K - 1 - k) for k in range(CONV_K)]
        conv = shifted[0] * w[0:1, :]
        for k in range(1, CONV_K):
            conv = conv + shifted[k] * w[k:k + 1, :]
        dcb_ref[...] = (dyv * conv).astype(dcb_ref.dtype)
        ds = dyv * cb_ref[...].astype(F32)
        dp = ds * w[CONV_K - 1:CONV_K, :]
        for k in range(1, CONV_K):
            dp = dp + _shift_up(ds, k) * w[CONV_K - 1 - k:CONV_K - k, :]
        dcc_ref[...] = (dp * cx).astype(dcc_ref.dtype)
        dcx_ref[...] = (dp * cc).astype(dcx_ref.dtype)
        for k in range(CONV_K):
            dw_ref[k:k + 1, :] = jnp.sum(ds * shifted[k], axis=0, keepdims=True)

    col = lambda off: pl.BlockSpec((s, tc), lambda j: (0, off * nb + j))
    blk = pl.BlockSpec((s, tc), lambda j: (0, j))
    wblk = pl.BlockSpec((CONV_K, tc), lambda j: (0, j))
    act = jax.ShapeDtypeStruct((s, d), BF16)
    return _pcall(
        body, name=name, grid=(nb,), in_specs=[blk, col(0), col(1), col(2), wblk],
        out_specs=[blk, blk, blk, wblk], out_shape=[act, act, act, jax.ShapeDtypeStruct((CONV_K, d), F32)],
        compiler_params=_params("parallel"),
    )(dy, proj, proj, proj, conv_w)


def _sb_tile(q, kj, scale, carry, tri, mask):
    z = _dot(q, kj, 1, 1) * scale
    lsz = jnp.minimum(z, 0.0) - jnp.log(1.0 + jnp.exp(-jnp.abs(z)))
    l1m = lsz - z
    if mask is not None:
        l1m = jnp.where(mask, l1m, 0.0)
    l1b = l1m.astype(BF16)
    a = jnp.exp(lsz + (carry + _dot(l1b, tri, 1, 0)))
    if mask is not None:
        a = jnp.where(mask, a, 0.0)
    return lsz, l1b, a.astype(BF16)


def _add_rows(x, upd, r0):
    return x + upd if r0 == 0 else jnp.concatenate([x[:r0], x[r0:] + upd], axis=0)


def _sb_masks(tq, tk):
    row = lax.broadcasted_iota(jnp.int32, (tq, tk), 0)
    col = lax.broadcasted_iota(jnp.int32, (tq, tk), 1)
    masks = [col + dj * tk < row for dj in range(tq // tk)]
    r2 = lax.broadcasted_iota(jnp.int32, (tk, tk), 0)
    c2 = lax.broadcasted_iota(jnp.int32, (tk, tk), 1)
    return masks, (r2 > c2).astype(BF16), (r2 < c2).astype(BF16)


def _sb_fwd(proj, heads, col0, tq, tk, name):
    s = proj.shape[0]
    dh = SB_HEAD_DIM
    nq, nd, nkt = s // tq, tq // tk, s // tk
    scale = dh ** -0.5

    def body(q_ref, k_ref, v_ref, o_ref, a_ref, b_ref):
        i = pl.program_id(1)
        q = q_ref[...]
        masks, tri_right, _ = _sb_masks(tq, tk)

        def tile(j, carry, acc, mask, r0=0):
            start = pl.multiple_of(j * tk, tk)
            kj = k_ref[pl.ds(start, tk), :]
            vj = v_ref[pl.ds(start, tk), :]
            lsz, l1b, ab = _sb_tile(q[r0:], kj, scale, carry[r0:], tri_right, None if mask is None else mask[r0:])
            a_ref[j, r0:, :] = ab
            b_ref[j, r0:, :] = jnp.exp(lsz).astype(b_ref.dtype)
            if r0:
                a_ref[j, :r0, :] = jnp.zeros((r0, tk), a_ref.dtype)
                b_ref[j, :r0, :] = jnp.zeros((r0, tk), b_ref.dtype)
            return (_add_rows(carry, jnp.sum(l1b.astype(F32), axis=1, keepdims=True), r0),
                    _add_rows(acc, _dot(ab, vj, 1, 0), r0))

        state = (jnp.zeros((tq, 1), F32), jnp.zeros((tq, dh), F32))
        for dj in reversed(range(nd)):
            state = tile(i * nd + dj, *state, masks[dj], dj * tk)
        def left_block(t, st):
            for dj in reversed(range(nd)):
                st = tile((i - 1 - t) * nd + dj, st[0], st[1], None)
            return st

        state = lax.fori_loop(0, i, left_block, state)
        o_ref[...] = state[1]

    qspec = pl.BlockSpec((tq, dh), lambda h, i: (i, col0[0] + h))
    kspec = pl.BlockSpec((s, dh), lambda h, i: (0, col0[1] + h))
    vspec = pl.BlockSpec((s, dh), lambda h, i: (0, col0[2] + h))
    saved = pl.BlockSpec((None, nkt, tq, tk), lambda h, i: (h, 0, i, 0))
    saved_shape = jax.ShapeDtypeStruct((heads, nkt, s, tk), BF16)
    return _pcall(
        body, name=name, grid=(heads, nq), in_specs=[qspec, kspec, vspec],
        out_specs=[pl.BlockSpec((tq, dh), lambda h, i: (i, h)), saved, saved],
        out_shape=[jax.ShapeDtypeStruct((s, heads * dh), F32), saved_shape, saved_shape],
        compiler_params=_params("parallel", "parallel"),
    )(proj, proj, proj)


def _sb_bwd(proj, o, a_all, beta_all, do, heads, col0, tq, tk, name):
    s = proj.shape[0]
    dh = SB_HEAD_DIM
    nq, nd, nkt = s // tq, tq // tk, s // tk
    scale = dh ** -0.5

    def body(q_ref, k_ref, v_ref, o_ref, a_ref, b_ref, do_ref, dq_ref, dk_ref, dv_ref, dk_acc, dv_acc):
        i = pl.program_id(1)

        @pl.when(i == 0)
        def _():
            dk_acc[...] = jnp.zeros_like(dk_acc)
            dv_acc[...] = jnp.zeros_like(dv_acc)

        q = q_ref[...]
        dob = do_ref[...].astype(BF16)
        delta = jnp.sum(dob.astype(F32) * o_ref[...], axis=1, keepdims=True)
        masks, _, tri_left = _sb_masks(tq, tk)

        def tile(j, carry_g, dq, mask):
            start = pl.multiple_of(j * tk, tk)
            kj = k_ref[pl.ds(start, tk), :]
            vj = v_ref[pl.ds(start, tk), :]
            ab = a_ref[j]
            g = _dot(dob, vj, 1, 1) * ab.astype(F32)
            carry_g = carry_g + jnp.sum(g, axis=1, keepdims=True)
            left = (delta - carry_g) + _dot(g.astype(BF16), tri_left, 1, 0)
            dz = g - b_ref[j].astype(F32) * (g + left)
            if mask is not None:
                dz = jnp.where(mask, dz, 0.0)
            dzb = dz.astype(BF16)
            dk_acc[pl.ds(start, tk), :] += _dot(dzb, q, 0, 0)
            dv_acc[pl.ds(start, tk), :] += _dot(ab, dob, 0, 0)
            return carry_g, dq + _dot(dzb, kj, 1, 0)

        state = (jnp.zeros((tq, 1), F32), jnp.zeros((tq, dh), F32))
        for dj in reversed(range(nd)):
            state = tile(i * nd + dj, *state, masks[dj])
        def left_block(t, st):
            for dj in reversed(range(nd)):
                st = tile((i - 1 - t) * nd + dj, st[0], st[1], None)
            return st

        state = lax.fori_loop(0, i, left_block, state)
        dq_ref[...] = (state[1] * scale).astype(dq_ref.dtype)

        @pl.when(i == nq - 1)
        def _():
            dk_ref[...] = (dk_acc[...] * scale).astype(dk_ref.dtype)
            dv_ref[...] = dv_acc[...].astype(dv_ref.dtype)

    qspec = pl.BlockSpec((tq, dh), lambda h, i: (i, col0[0] + h))
    kspec = pl.BlockSpec((s, dh), lambda h, i: (0, col0[1] + h))
    vspec = pl.BlockSpec((s, dh), lambda h, i: (0, col0[2] + h))
    blk = pl.BlockSpec((tq, dh), lambda h, i: (i, h))
    full = pl.BlockSpec((s, dh), lambda h, i: (0, h))
    saved = pl.BlockSpec((None, nkt, tq, tk), lambda h, i: (h, 0, i, 0))
    act = jax.ShapeDtypeStruct((s, heads * dh), BF16)
    return _pcall(
        body, name=name, grid=(heads, nq), in_specs=[qspec, kspec, vspec, blk, saved, saved, blk],
        out_specs=[blk, full, full], out_shape=[act, act, act],
        scratch_shapes=[pltpu.VMEM((s, dh), F32), pltpu.VMEM((s, dh), F32)],
        compiler_params=_params("parallel", "arbitrary"),
    )(proj, proj, proj, o, a_all, beta_all, do)


def _xattn_probs(q, k, scale):
    sc = _dot(q, k, 1, 1) * scale
    e = jnp.exp(sc - jnp.max(sc, axis=1, keepdims=True))
    return e / jnp.sum(e, axis=1, keepdims=True)


def _xattn_fwd(qc, kv, tq, name):
    s, d = qc.shape
    m = kv.shape[0]
    dh = d // X_HEADS
    scale = dh ** -0.5

    def body(q_ref, k_ref, v_ref, o_ref):
        p = _xattn_probs(q_ref[...], k_ref[...], scale)
        o_ref[...] = _dot(p.astype(BF16), v_ref[...], 1, 0).astype(o_ref.dtype)

    blk = pl.BlockSpec((tq, dh), lambda h, i: (i, h))
    return _pcall(
        body, name=name, grid=(X_HEADS, s // tq),
        in_specs=[blk, pl.BlockSpec((m, dh), lambda h, i: (0, h)), pl.BlockSpec((m, dh), lambda h, i: (0, X_HEADS + h))],
        out_specs=blk, out_shape=jax.ShapeDtypeStruct((s, d), BF16), compiler_params=_params("parallel", "parallel"),
    )(qc, kv, kv)


def _xattn_bwd(qc, kv, do, tq, name):
    s, d = qc.shape
    m = kv.shape[0]
    dh = d // X_HEADS
    scale = dh ** -0.5
    nq = s // tq

    def body(q_ref, k_ref, v_ref, do_ref, dq_ref, dk_ref, dv_ref, dk_acc, dv_acc):
        i = pl.program_id(1)
        q, k, v = q_ref[...], k_ref[...], v_ref[...]
        dob = do_ref[...].astype(BF16)
        p = _xattn_probs(q, k, scale)
        pb = p.astype(BF16)
        dp = _dot(dob, v, 1, 1)
        ds = pb.astype(F32) * (dp - jnp.sum(dp * pb.astype(F32), axis=1, keepdims=True))
        dsb = (ds * scale).astype(BF16)
        dq_ref[...] = _dot(dsb, k, 1, 0).astype(dq_ref.dtype)
        dk_part = _dot(dsb, q, 0, 0)
        dv_part = _dot(pb, dob, 0, 0)

        @pl.when(i == 0)
        def _():
            dk_acc[...] = dk_part
            dv_acc[...] = dv_part

        @pl.when(i > 0)
        def _():
            dk_acc[...] += dk_part
            dv_acc[...] += dv_part

        @pl.when(i == nq - 1)
        def _():
            dk_ref[...] = dk_acc[...].astype(dk_ref.dtype)
            dv_ref[...] = dv_acc[...].astype(dv_ref.dtype)

    blk = pl.BlockSpec((tq, dh), lambda h, i: (i, h))
    kblk = pl.BlockSpec((m, dh), lambda h, i: (0, h))
    return _pcall(
        body, name=name, grid=(X_HEADS, nq),
        in_specs=[blk, kblk, pl.BlockSpec((m, dh), lambda h, i: (0, X_HEADS + h)), blk],
        out_specs=[blk, kblk, kblk],
        out_shape=[jax.ShapeDtypeStruct((s, d), BF16), jax.ShapeDtypeStruct((m, d), BF16), jax.ShapeDtypeStruct((m, d), BF16)],
        scratch_shapes=[pltpu.VMEM((m, dh), F32), pltpu.VMEM((m, dh), F32)],
        compiler_params=_params("parallel", "arbitrary"),
    )(qc, kv, kv, do)


def _local_step(x, mem, tgt, w, fetch=None, prefetch=None, emit=None, tick=None, after=None):
    fetch = fetch or (lambda name, after: {})
    prefetch = prefetch or (lambda name, after: None)
    emit = emit or (lambda group, g: None)
    tick = tick or (lambda group, after: None)
    w = dict(w)
    s, d = x.shape
    heads = d // SB_HEAD_DIM
    tm = _pick(s, (512, 256, 128))
    tq = _pick(s, (256, 128))
    sb_tq, sb_tk = _pick(s, (512, 256, 128)), _pick(s, (256, 128))
    tc = _pick(d, (256, 128))
    g = {}

    def wt(name, after):
        if name not in w:
            w.update(fetch(name, after))
        return w[name]

    def ffn_fwd(h, gname, wgu, wdown, tag, after=None):
        n = _rms_fwd(h, w[gname], tag + "_norm", tm, after=after)
        gu, act = _ffn_up(n, wt(wgu, n), tag + "_gu")
        prefetch(wdown, gu)
        return n, gu, act, _mm(act, wt(wdown, act), name=tag + "_down", out_dtype=F32, res=h, alpha=0.5)

    def ffn_bwd(dh, dhb, h, saved, gname, wgu, wdown, tag, copy_scale=None, after=None):
        n, gu, act = saved
        g[wdown] = _mm(act, dhb, ta=True, name=tag + "_dwdown", after=after)
        dgu = _ffn_dgu(dhb, w[wdown], gu, tag + "_dgu", after=emit(tag + "_down", g))
        g[wgu] = _mm(n, dgu, ta=True, b_halves=True, name=tag + "_dwgu", after=tick(tag + "_down", dgu))
        *dh_in, g[gname] = _dgrad_norm(dgu, w[wgu], dh, h, w[gname], tag + "_dn", dy_halves=True, copy_scale=copy_scale,
                                       after=emit(tag, g))
        return dh_in, tick(tag, dh_in[0])

    n1, gu1, act1, h1 = ffn_fwd(x, "g_ffn1", "w_ffn1_gu", "w_ffn1_down", "ffn1", after)
    prefetch("w_in", h1)
    u = _rms_fwd(h1, w["g_mix"], "mix_norm", tm)
    proj = _mm(u, wt("w_in", u), name="mix_in")
    prefetch("w_conv_out", proj)
    nd = d // SB_HEAD_DIM
    y_conv = _conv_fwd(proj, w["conv_w"], d, tc, "conv_fwd")
    sb_cols = (3 * nd, 4 * nd, 5 * nd)
    y_sb, sb_a, sb_beta = _sb_fwd(proj, heads, sb_cols, sb_tq, sb_tk, "sb_fwd")
    prefetch("w_cq", y_sb)
    a_conv = _mm(y_conv, wt("w_conv_out", y_conv), name="conv_out")
    a_sb = _mm(y_sb, wt("w_attn_out", y_sb), name="attn_out")
    b_conv, b_sb = w["b_gate"][:, :d], w["b_gate"][:, d:]

    def merge(ac, asb, gcp, gsp, bc, bs):
        gc = _sigmoid(gcp.astype(F32) + bc)
        gs = _sigmoid(gsp.astype(F32) + bs)
        return gc * ac.astype(F32) + gs * asb.astype(F32)

    merged = _rowcall(merge, [_whole(a_conv), _whole(a_sb), (proj, 6, d), (proj, 7, d)], [b_conv, b_sb], [(d, BF16)],
                      tm=tm, name="merge")[0]
    prefetch("w_ffn2_gu", merged)
    h2 = _mm(merged, wt("w_o", merged), name="mix_out", out_dtype=F32, res=h1)
    hn = _rms_fwd(h2, w["g_cross"], "cross_norm", tm)
    mn = _rms_fwd(mem, w["g_mem"], "mem_norm", _pick(mem.shape[0], (256, 128)))
    qc = _mm(hn, wt("w_cq", hn), name="cross_q")
    kv = _mm(mn, wt("w_ckv", mn), name="cross_kv")
    oc = _xattn_fwd(qc, kv, tq, "xattn_fwd")
    h3 = _mm(oc, wt("w_co", oc), name="cross_out", out_dtype=F32, res=h2)
    n2, gu2, act2, h4 = ffn_fwd(h3, "g_ffn2", "w_ffn2_gu", "w_ffn2_down", "ffn2")

    def head(hb, tb, gb):
        xh, r = _xhat(hb)
        err = xh * gb - tb
        dy = err * (1.0 / d)
        dxh = dy * gb
        dx = r * (dxh - xh * jnp.mean(dxh * xh, axis=-1, keepdims=True))
        row_loss = 0.5 * jnp.mean(err * err, axis=-1, keepdims=True)
        return dx, 0.5 * dx, dy * xh, jnp.broadcast_to(row_loss, (row_loss.shape[0], LANES))

    dh4, dh4b, g["g_final"], loss_lanes = _rowcall(head, [_whole(h4), _whole(tgt)], [w["g_final"]], [(d, F32), (d, BF16)],
                                                   [d, LANES], tm=tm, name="loss_head")

    (dh3, dh3b), tok = ffn_bwd(dh4, dh4b, h3, (n2, gu2, act2), "g_ffn2", "w_ffn2_gu", "w_ffn2_down", "ffn2", copy_scale=1.0)
    g["w_co"] = _mm(oc, dh3b, ta=True, name="cross_dwco", after=tok)
    doc = _mm(dh3b, w["w_co"], tb=True, name="cross_doc")
    dqc, dk, dv = _xattn_bwd(qc, kv, doc, tq, "xattn_bwd")
    dkv = jnp.concatenate([dk, dv], axis=1)
    g["w_cq"] = _mm(hn, dqc, ta=True, name="cross_dwcq")
    g["w_ckv"] = _mm(mn, dkv, ta=True, name="cross_dwckv")
    dmn = _mm(dkv, w["w_ckv"], tb=True, name="cross_dmn", out_dtype=F32)
    g["g_mem"] = _rowcall(lambda dy, xb: dy * _xhat(xb)[0], [_whole(dmn), _whole(mem)], [], [], [d],
                          tm=_pick(mem.shape[0], (256, 128)), name="mem_dnorm")[0]
    dh2, dh2b, g["g_cross"] = _dgrad_norm(dqc, w["w_cq"], dh3, h2, w["g_cross"], "cross_dhn", copy_scale=1.0, after=emit("cross", g))

    g["w_o"] = _mm(merged, dh2b, ta=True, name="mix_dwo", after=tick("cross", dh2))
    dmerged = _mm(dh2b, w["w_o"], tb=True, name="mix_dmerged")

    def merge_bwd(dm, ac, asb, gcp, gsp, bc, bs):
        dm, ac, asb = dm.astype(F32), ac.astype(F32), asb.astype(F32)
        gc = _sigmoid(gcp.astype(F32) + bc)
        gs = _sigmoid(gsp.astype(F32) + bs)
        dgc = dm * ac * gc * (1.0 - gc)
        dgs = dm * asb * gs * (1.0 - gs)
        return dm * gc, dm * gs, dgc, dgs, dgc, dgs

    da_conv, da_sb, dgc, dgs, db_conv, db_sb = _rowcall(
        merge_bwd, [_whole(dmerged), _whole(a_conv), _whole(a_sb), (proj, 6, d), (proj, 7, d)], [b_conv, b_sb],
        [(d, BF16)] * 4, [d, d], tm=tm, name="merge_bwd")
    g["b_gate"] = jnp.concatenate([db_conv, db_sb], axis=1)
    g["w_conv_out"] = _mm(y_conv, da_conv, ta=True, name="conv_dwout")
    g["w_attn_out"] = _mm(y_sb, da_sb, ta=True, name="attn_dwout")
    dy_conv = _mm(da_conv, w["w_conv_out"], tb=True, name="conv_dy")
    dy_sb = _mm(da_sb, w["w_attn_out"], tb=True, name="attn_dy")
    dcb, dcc, dcx, g["conv_w"] = _conv_bwd(dy_conv, proj, w["conv_w"], d, tc, "conv_bwd")
    dq, dk_sb, dv_sb = _sb_bwd(proj, y_sb, sb_a, sb_beta, dy_sb, heads, sb_cols, sb_tq, sb_tk, "sb_bwd")
    dproj = jnp.concatenate([dcb, dcc, dcx, dq, dk_sb, dv_sb, dgc, dgs], axis=1)
    g["w_in"] = _mm(u, dproj, ta=True, name="mix_dwin")
    dh1, dh1b, g["g_mix"] = _dgrad_norm(dproj, w["w_in"], dh2, h1, w["g_mix"], "mix_du", copy_scale=0.5, after=emit("mix", g))
    (dx,), tok = ffn_bwd(dh1, dh1b, x, (n1, gu1, act1), "g_ffn1", "w_ffn1_gu", "w_ffn1_down", "ffn1", after=tick("mix", dh1))
    return loss_lanes, dx, g, tok


MATS = (("w_ffn1_gu", "col"), ("w_ffn1_down", "row"), ("w_in", "col"), ("w_conv_out", "row"), ("w_attn_out", "row"),
        ("w_o", "row"), ("w_cq", "row"), ("w_ckv", "col"), ("w_co", "row"), ("w_ffn2_gu", "col"), ("w_ffn2_down", "row"))
VECS = ("g_ffn1", "g_mix", "g_cross", "g_mem", "g_ffn2", "g_final")
WEIGHTS = ("g_ffn1", "w_ffn1_gu", "w_ffn1_down", "g_mix", "w_in", "b_gate", "conv_w", "w_conv_out", "w_attn_out", "w_o",
           "g_cross", "g_mem", "w_cq", "w_ckv", "w_co", "g_ffn2", "w_ffn2_gu", "w_ffn2_down", "g_final")
CONV_ROWS = 8


def _full_shape(kind, r, c):
    return (r, N_CHIPS * c) if kind == "col" else (N_CHIPS * r, c)


def _piece(ref, kind, r, c, chip, half):
    hr = r // 2
    if kind == "col":
        return ref.at[pl.ds(pl.multiple_of(half * hr, 16), hr), pl.ds(pl.multiple_of(chip * c, LANES), c)]
    return ref.at[pl.ds(pl.multiple_of(chip * r + half * hr, 16), hr), :]


def _shard_of(ref, kind, r, c, chip):
    if kind == "col":
        return ref.at[:, pl.ds(pl.multiple_of(chip * c, LANES), c)]
    return ref.at[pl.ds(pl.multiple_of(chip * r, 16), r), :]


def _place():
    x, y, c = lax.axis_index("x"), lax.axis_index("y"), lax.axis_index("c")
    others = [(1 - x, y), (x, 1 - y), (1 - x, 1 - y)]
    return x, y, c, 2 * x + y, others


def _remote(src, dst, send_sem, recv_sem, to):
    return pltpu.make_async_remote_copy(src_ref=src, dst_ref=dst, send_sem=send_sem, recv_sem=recv_sem,
                                        device_id=to, device_id_type=MESH)


def _gather_conv(conv_shard):
    cc = conv_shard.shape[1]

    def body(conv_ref, conv_full, cs, cr, cl):
        x, y, c, me, others = _place()

        def cols(chip):
            return conv_full.at[:, pl.ds(pl.multiple_of(chip * cc, LANES), cc)]

        def conv(k, chip_from, to):
            return _remote(conv_ref, cols(chip_from), cs.at[k], cr.at[k], to)

        mine = pltpu.make_async_copy(conv_ref, cols(me), cl.at[0])
        mine.start()
        for k, (ox, oy) in enumerate(others):
            conv(k, me, (ox, oy, c)).start()
        for k, (ox, oy) in enumerate(others):
            conv(k, 2 * ox + oy, (x, y, c)).wait_recv()
            conv(k, me, (ox, oy, c)).wait_send()
        mine.wait()

    dma = pltpu.SemaphoreType.DMA
    return _pcall(
        body, name="gather_conv", in_specs=[ANY], out_specs=ANY,
        out_shape=jax.ShapeDtypeStruct((CONV_ROWS, N_CHIPS * cc), F32), scratch_shapes=[dma((3,)), dma((3,)), dma((1,))],
    )(conv_shard)


HBM = pl.BlockSpec(memory_space=pltpu.HBM)
SEM = pl.BlockSpec(memory_space=pltpu.SEMAPHORE)
EFFECT = pltpu.SideEffectType.DATAFLOW_SIDE_EFFECTING
TOKEN = (8, LANES)


def _split_start(name, plan, n_copies, srcs, lands, after=None):
    ns, nl = len(srcs), len(lands)
    n_in = ns + nl + (after is not None)

    def body(*refs):
        outs = refs[n_in:]
        sends, _ = plan(refs[:ns], refs[ns:ns + nl], outs[0], outs[1])
        for cp in sends:
            cp.start()
        outs[-1][...] = jnp.zeros(TOKEN, F32)

    held = [pltpu.HBM(a.shape, a.dtype) for a in (*srcs, *lands)]
    dma = pltpu.SemaphoreType.DMA((n_copies,))
    ins = [pltpu.with_memory_space_constraint(a, pltpu.HBM) for a in (*srcs, *lands)]
    outs = _pcall(
        body, name=name, in_specs=[HBM] * (ns + nl) + ([] if after is None else [ANY]),
        out_specs=(SEM, SEM, *[HBM] * (ns + nl), pl.BlockSpec(memory_space=pltpu.VMEM)),
        out_shape=(dma, dma, *held, jax.ShapeDtypeStruct(TOKEN, F32)),
        input_output_aliases={i: 2 + i for i in range(ns + nl)},
        compiler_params=pltpu.CompilerParams(has_side_effects=EFFECT),
    )(*ins, *([] if after is None else [after]))
    return outs[0], outs[1], list(outs[2:2 + ns]), list(outs[2 + ns:2 + ns + nl]), outs[-1]


def _split_wait(name, plan, send_sems, recv_sems, srcs, lands, after):
    ns, nl = len(srcs), len(lands)

    def body(*refs):
        sends, recvs = plan(refs[:ns], refs[ns:ns + nl], refs[ns + nl], refs[ns + nl + 1])
        for cp in sends:
            cp.wait_send()
        for cp in recvs:
            cp.wait_recv()

    outs = _pcall(
        body, name=name, in_specs=[HBM] * (ns + nl) + [SEM, SEM, ANY], out_specs=[HBM] * (ns + nl),
        out_shape=[pltpu.HBM(a.shape, a.dtype) for a in (*srcs, *lands)],
        input_output_aliases={i: i for i in range(ns + nl)},
        compiler_params=pltpu.CompilerParams(has_side_effects=EFFECT),
    )(*srcs, *lands, send_sems, recv_sems, after)
    return list(outs[:ns]), list(outs[ns:])


def _gather_plan(dims):
    def plan(shard_refs, full_refs, ss, rs):
        x, y, c, me, others = _place()
        sends, recvs = [], []
        for wi, (kind, r, cw) in enumerate(dims):
            half = shard_refs[wi].at[pl.ds(pl.multiple_of(c * (r // 2), 16), r // 2), :]
            for k, (ox, oy) in enumerate(others):
                sem = 4 * wi + k
                sends.append(_remote(half, _piece(full_refs[wi], kind, r, cw, me, c), ss.at[sem], rs.at[sem], (ox, oy, c)))
                recvs.append(_remote(half, _piece(full_refs[wi], kind, r, cw, 2 * ox + oy, c), ss.at[sem], rs.at[sem], (x, y, c)))
            sem = 4 * wi + 3
            own = _remote(shard_refs[wi], _shard_of(full_refs[wi], kind, r, cw, me), ss.at[sem], rs.at[sem], (x, y, 1 - c))
            sends.append(own)
            recvs.append(own)
        return sends, recvs

    return plan


def _forward_plan(dims):
    def plan(_, full_refs, ss, rs):
        x, y, c, _, others = _place()
        sends, recvs = [], []
        for wi, (kind, r, cw) in enumerate(dims):
            for k, (ox, oy) in enumerate(others):
                sem = 3 * wi + k
                mine = _piece(full_refs[wi], kind, r, cw, 2 * ox + oy, c)
                theirs = _piece(full_refs[wi], kind, r, cw, 2 * ox + oy, 1 - c)
                sends.append(_remote(mine, mine, ss.at[sem], rs.at[sem], (x, y, 1 - c)))
                recvs.append(_remote(theirs, theirs, ss.at[sem], rs.at[sem], (x, y, 1 - c)))
        return sends, recvs

    return plan


def _rs_cores_plan(dims):
    def plan(g_refs, land_refs, ss, rs):
        x, y, c, _, _ = _place()
        sends, recvs = [], []
        for wi, dm in enumerate(dims):
            for chip in range(N_CHIPS):
                sem = N_CHIPS * wi + chip
                sends.append(_remote(_piece(g_refs[wi], *dm, chip, 1 - c), land_refs[wi].at[chip], ss.at[sem], rs.at[sem], (x, y, 1 - c)))
                recvs.append(_remote(_piece(g_refs[wi], *dm, chip, c), land_refs[wi].at[chip], ss.at[sem], rs.at[sem], (x, y, 1 - c)))
        return sends, recvs

    return plan


def _share_plan(nw):
    def plan(_, buf_refs, ss, rs):
        x, y, c, _, _ = _place()
        sends = [_remote(buf_refs[wi].at[c], buf_refs[wi].at[c], ss.at[wi], rs.at[wi], (x, y, 1 - c)) for wi in range(nw)]
        recvs = [_remote(buf_refs[wi].at[1 - c], buf_refs[wi].at[1 - c], ss.at[wi], rs.at[wi], (x, y, 1 - c)) for wi in range(nw)]
        return sends, recvs

    return plan


def _small_plan():
    def plan(_, buf_refs, ss, rs):
        x, y, c = lax.axis_index("x"), lax.axis_index("y"), lax.axis_index("c")
        buf = buf_refs[0]
        sends, recvs = [], []
        for rel in range(1, N_DEV):
            peer = (x ^ (rel >> 2 & 1), y ^ (rel >> 1 & 1), c ^ (rel & 1))
            sends.append(_remote(buf.at[0], buf.at[rel], ss.at[rel - 1], rs.at[rel - 1], peer))
            recvs.append(_remote(buf.at[0], buf.at[rel], ss.at[rel - 1], rs.at[rel - 1], peer))
        return sends, recvs

    return plan


def _sum_small(buf, me, name):
    _, rows, n = buf.shape

    def body(me_ref, b_ref, o_ref):
        tot = b_ref[me_ref[0]]
        for dev in range(1, N_DEV):
            tot = tot + b_ref[dev ^ me_ref[0]]
        o_ref[...] = tot

    return _pcall(
        body, name=name, out_shape=jax.ShapeDtypeStruct((rows, n), F32),
        grid_spec=pltpu.PrefetchScalarGridSpec(
            num_scalar_prefetch=1, grid=(1,), in_specs=[pl.BlockSpec((N_DEV, rows, n), lambda i, m: (0, 0, 0))],
            out_specs=pl.BlockSpec((rows, n), lambda i, m: (0, 0))),
    )(me, buf)


def _rs_chips_plan(nw):
    def plan(p_refs, land_refs, ss, rs):
        x, y, c, me, others = _place()
        sends, recvs = [], []
        for wi in range(nw):
            for k, (ox, oy) in enumerate(others):
                sem = 3 * wi + k
                sends.append(_remote(p_refs[wi].at[2 * ox + oy], land_refs[wi].at[k], ss.at[sem], rs.at[sem], (ox, oy, c)))
                recvs.append(_remote(p_refs[wi].at[me], land_refs[wi].at[k], ss.at[sem], rs.at[sem], (x, y, c)))
        return sends, recvs

    return plan


def _rows_per_block(n, c, limit_bytes=2 << 20):
    best = None
    for tm in range(16, n + 1, 16):
        if n % tm == 0 and tm * c * 4 <= limit_bytes:
            best = tm
    return best or n


def _sum_cores(grad, got, kind, place, name):
    _, hr, cw = got.shape
    tm = _rows_per_block(hr, cw)
    nb = hr // tm

    def body(place_ref, g_ref, t_ref, o_ref):
        o_ref[...] = (g_ref[...].astype(F32) + t_ref[...].astype(F32)).astype(o_ref.dtype)

    if kind == "col":
        g_spec = pl.BlockSpec((tm, cw), lambda j, i, pr: (pr[0] * nb + i, j))
    else:
        g_spec = pl.BlockSpec((tm, cw), lambda j, i, pr: ((2 * j + pr[0]) * nb + i, 0))
    blk = pl.BlockSpec((None, tm, cw), lambda j, i, pr: (j, i, 0))
    return _pcall(
        body, name=name, out_shape=jax.ShapeDtypeStruct(got.shape, BF16),
        grid_spec=pltpu.PrefetchScalarGridSpec(num_scalar_prefetch=1, grid=(N_CHIPS, nb), in_specs=[g_spec, blk], out_specs=blk),
        compiler_params=_params("parallel", "parallel"),
    )(place, grad, got)


def _sum_chips(parts, got, place, name):
    _, n, cw = got.shape
    tm = _rows_per_block(n, cw)

    def body(place_ref, p_ref, g_ref, o_ref):
        tot = p_ref[...].astype(F32)
        for k in range(3):
            tot = tot + g_ref[k].astype(F32)
        o_ref[...] = tot

    return _pcall(
        body, name=name, out_shape=jax.ShapeDtypeStruct((2, n, cw), F32),
        grid_spec=pltpu.PrefetchScalarGridSpec(
            num_scalar_prefetch=1, grid=(n // tm,),
            in_specs=[pl.BlockSpec((None, tm, cw), lambda i, pr: (pr[1], i, 0)), pl.BlockSpec((3, tm, cw), lambda i, pr: (0, i, 0))],
            out_specs=pl.BlockSpec((None, tm, cw), lambda i, pr: (pr[0], i, 0))),
        compiler_params=_params("parallel"),
    )(place, parts, got)


def _adamw(g, w, m, v, name):
    n, c = g.shape
    c1 = 1.0 - ADAM_B1 ** ADAM_STEP
    c2 = 1.0 - ADAM_B2 ** ADAM_STEP

    def fn(gb, wb, mb, vb):
        m_new = ADAM_B1 * mb + (1.0 - ADAM_B1) * gb
        v_new = ADAM_B2 * vb + (1.0 - ADAM_B2) * (gb * gb)
        delta = -ADAM_LR * ((m_new / c1) / (jnp.sqrt(v_new / c2) + ADAM_EPS) + ADAM_WD * wb)
        return gb, delta, m_new, v_new

    tm = _rows_per_block(n, c) if n % 16 == 0 else n
    return _rowcall(fn, [_whole(g), _whole(w), _whole(m), _whole(v)], [], [(c, F32)] * 4, tm=tm, name=name)


PACK_ROWS = 16


def _pack_rows(parts, width, name, after=None):
    assert sum(p.shape[0] for p in parts) <= PACK_ROWS

    def body(*refs):
        out_ref = refs[-1]
        out_ref[...] = jnp.zeros_like(out_ref)
        at = 0
        for r in refs[:len(parts)]:
            k, n = r.shape
            if n == width:
                out_ref[at:at + k, :] = r[...]
            else:
                out_ref[at:at + k, :] = jnp.broadcast_to(r[:, :1], (k, width))
            at += k

    vm = pl.BlockSpec(memory_space=pltpu.VMEM)
    return _pcall(body, name=name, in_specs=[vm] * len(parts) + ([] if after is None else [ANY]), out_specs=vm,
                  out_shape=jax.ShapeDtypeStruct((PACK_ROWS, width), F32))(*parts, *([] if after is None else [after]))


def _cast_shard(wm, name, after):
    n, c = wm.shape
    return _rowcall(lambda v: v, [_whole(wm)], [], [(c, BF16)], tm=_rows_per_block(n, c), name=name, after=after)[0]


GATHER_GROUPS = (
    ("w_ffn1_gu",), ("w_ffn1_down",), ("w_in",), ("w_conv_out", "w_attn_out", "w_o"), ("w_cq", "w_ckv", "w_co"),
    ("w_ffn2_gu", "w_ffn2_down"),
)
REDUCE_GROUPS = {
    "ffn2": ("w_ffn2_down", "w_ffn2_gu"),
    "cross": ("w_co", "w_cq", "w_ckv"),
    "mix": ("w_o", "w_conv_out", "w_attn_out", "w_in"),
    "ffn1_down": ("w_ffn1_down",),
    "ffn1": ("w_ffn1_gu",),
}
TAIL_STAGES = (("ffn2", "cross"), ("mix",), ("ffn1_down", "ffn1"))
KIND = dict(MATS)


def _step(x, mem, tgt, wts, m_in, v_in):
    d = x.shape[-1]
    cc = wts["conv_w"].shape[1]
    place = jnp.stack([lax.axis_index("c"), 2 * lax.axis_index("x") + lax.axis_index("y")]).astype(jnp.int32)
    dims = {n: (kind, *wts[n].shape) for n, kind in MATS}

    conv_full = _gather_conv(jnp.pad(wts["conv_w"], ((0, CONV_ROWS - CONV_K), (0, 0))))
    w = {n: wts[n].reshape(1, -1) for n in VECS + ("b_gate",)}
    w["conv_w"] = conv_full[:CONV_K]
    flying, token = {}, conv_full
    for names in GATHER_GROUPS:
        gd = [dims[n] for n in names]
        shards = [_cast_shard(wts[n], "cast_" + n, token) for n in names]
        lands = [lax.empty(_full_shape(*dm), BF16) for dm in gd]
        plan = _gather_plan(gd)
        ss, rs, srcs, lands, token = _split_start("gather_start_" + names[0], plan, 4 * len(names), shards, lands, token)
        flying.update({n: (names, plan, ss, rs, srcs, lands, gd) for n in names})

    passing = {}

    def prefetch(name, after):
        if name not in passing:
            names, plan, ss, rs, srcs, lands, gd = flying[name]
            _, lands = _split_wait("gather_wait_" + names[0], plan, ss, rs, srcs, lands, after)
            plan = _forward_plan(gd)
            ss, rs, _, lands, _ = _split_start("forward_start_" + names[0], plan, 3 * len(names), [], lands)
            passing.update({n: (names, plan, ss, rs, lands) for n in names})

    def fetch(name, after):
        prefetch(name, after)
        names, plan, ss, rs, lands = passing[name]
        _, lands = _split_wait("forward_wait_" + names[0], plan, ss, rs, [], lands, after)
        return dict(zip(names, lands))

    swapping, sent = {}, {}

    def emit(tag, g):
        if tag not in REDUCE_GROUPS:
            return None
        names = REDUCE_GROUPS[tag]
        gd = [dims[n] for n in names]
        lands = [lax.empty((N_CHIPS, r // 2, cw), BF16) for (_, r, cw) in gd]
        plan = _rs_cores_plan(gd)
        ss, rs, srcs, lands, tok = _split_start("rs_cores_start_" + tag, plan, N_CHIPS * len(names), [g[n] for n in names], lands)
        swapping[tag] = (plan, ss, rs, srcs, lands)
        return tok

    def tick(tag, after):
        if tag not in REDUCE_GROUPS:
            return None
        names = REDUCE_GROUPS[tag]
        plan, ss, rs, srcs, lands = swapping[tag]
        mine, got = _split_wait("rs_cores_wait_" + tag, plan, ss, rs, srcs, lands, after)
        parts = [_sum_cores(gm, t, KIND[n], place, "sum_cores_" + n) for n, gm, t in zip(names, mine, got)]
        lands = [lax.empty((3, *p.shape[1:]), BF16) for p in parts]
        plan = _rs_chips_plan(len(names))
        ss, rs, srcs, lands, tok = _split_start("rs_chips_start_" + tag, plan, 3 * len(names), parts, lands)
        sent[tag] = (plan, ss, rs, srcs, lands)
        return tok

    loss_lanes, dx, g, last = _local_step(x[0], mem[0], tgt[0], w, fetch, prefetch, emit, tick, token)

    rows = [g[n] for n in VECS] + [g["b_gate"][:, :d], g["b_gate"][:, d:], g["conv_w"], loss_lanes]
    packed = _pack_rows(rows, d, "pack_small", after=last)
    small = jnp.concatenate([packed[None], jnp.zeros((N_DEV - 1, *packed.shape), F32)], axis=0)
    small_plan = _small_plan()
    small_ss, small_rs, _, small, after = _split_start("small_start", small_plan, N_DEV - 1, [], [small])

    grads, out = {}, {}

    def update(n):
        shape = wts[n].shape
        as2d = (lambda a: a.reshape(1, -1)) if len(shape) == 1 else (lambda a: a)
        return [r.reshape(shape) for r in _adamw(grads[n], as2d(wts[n]), as2d(m_in[n]), as2d(v_in[n]), "adamw_" + n)]

    def finish(sharing, after):
        tag, names, plan, ss, rs, halves = sharing
        _, both = _split_wait("share_wait_" + tag, plan, ss, rs, [], halves, after)
        for n, b in zip(names, both):
            grads[n] = b.reshape(-1, b.shape[-1])
            out[n] = update(n)
        return out[names[-1]][1]

    sharing = None
    for stage in TAIL_STAGES:
        names, halves = [], []
        for tag in stage:
            plan, ss, rs, srcs, lands = sent[tag]
            parts, landed = _split_wait("rs_chips_wait_" + tag, plan, ss, rs, srcs, lands, after)
            halves += [_sum_chips(p, t, place, "sum_chips_" + n) for n, p, t in zip(REDUCE_GROUPS[tag], parts, landed)]
            names += REDUCE_GROUPS[tag]
        plan = _share_plan(len(names))
        ss, rs, _, halves, after = _split_start("share_start_" + stage[0], plan, len(names), [], halves)
        if sharing is not None:
            after = finish(sharing, after)
        sharing = (stage[0], names, plan, ss, rs, halves)
    after = finish(sharing, after)

    _, small = _split_wait("small_wait", small_plan, small_ss, small_rs, [], small, after)
    me = (4 * lax.axis_index("x") + 2 * lax.axis_index("y") + lax.axis_index("c")).astype(jnp.int32).reshape(1)
    red = _sum_small(small[0], me, "sum_small")
    grads.update({n: red[i:i + 1] for i, n in enumerate(VECS)})
    nv = len(VECS)
    grads["b_gate"] = jnp.concatenate([red[nv:nv + 1], red[nv + 1:nv + 2]], axis=1)
    chip = 2 * lax.axis_index("x") + lax.axis_index("y")
    grads["conv_w"] = lax.dynamic_slice_in_dim(red[nv + 2:nv + 2 + CONV_K], chip * cc, cc, axis=1)
    loss = red[nv + 2 + CONV_K, 0]
    out.update({n: update(n) for n in WEIGHTS if n not in KIND})
    return (loss, dx[None], *[out[n][0] for n in WEIGHTS], *[out[n][1] for n in WEIGHTS],
            *[out[n][2] for n in WEIGHTS], *[out[n][3] for n in WEIGHTS])


def kernel(x, mem, g_ffn1, w_ffn1_gu, w_ffn1_down, g_mix, w_in, b_gate, conv_w, w_conv_out, w_attn_out, w_o, g_cross, g_mem, w_cq, w_ckv, w_co, g_ffn2, w_ffn2_gu, w_ffn2_down, g_final, loss_target, m_g_ffn1, m_w_ffn1_gu, m_w_ffn1_down, m_g_mix, m_w_in, m_b_gate, m_conv_w, m_w_conv_out, m_w_attn_out, m_w_o, m_g_cross, m_g_mem, m_w_cq, m_w_ckv, m_w_co, m_g_ffn2, m_w_ffn2_gu, m_w_ffn2_down, m_g_final, v_g_ffn1, v_w_ffn1_gu, v_w_ffn1_down, v_g_mix, v_w_in, v_b_gate, v_conv_w, v_w_conv_out, v_w_attn_out, v_w_o, v_g_cross, v_g_mem, v_w_cq, v_w_ckv, v_w_co, v_g_ffn2, v_w_ffn2_gu, v_w_ffn2_down, v_g_final):
    given = dict(locals())
    wts = {n: given[n] for n in WEIGHTS}
    m_in = {n: given["m_" + n] for n in WEIGHTS}
    v_in = {n: given["v_" + n] for n in WEIGHTS}
    return _step(x, mem, loss_target, wts, m_in, v_in)
```

```python
import jax
import jax.numpy as jnp
from jax import lax
from jax.experimental import pallas as pl
from jax.experimental.pallas import tpu as pltpu

F32 = jnp.float32
BF16 = jnp.bfloat16
MESH = pl.DeviceIdType.MESH

V7X_VMEM_LIMIT_BYTES = 48 * 1024 * 1024
MM_VMEM_BUDGET_BYTES = 36 * 1024 * 1024
MM_WHOLE_K = 2816
LANES = 128
SB_HEAD_DIM = 128
X_HEADS = 4
CONV_K = 3
RMS_EPS = 1e-6
N_CHIPS = 4
N_DEV = 8
ADAM_LR, ADAM_B1, ADAM_B2, ADAM_EPS, ADAM_WD, ADAM_STEP = 0.001, 0.9, 0.999, 1e-08, 0.01, 10


ANY = pl.BlockSpec(memory_space=pl.ANY)


def _pcall(body, **kw):
    return pl.pallas_call(body, **kw)


def _params(*sem):
    return pltpu.CompilerParams(dimension_semantics=sem, vmem_limit_bytes=V7X_VMEM_LIMIT_BYTES)


def _pick(dim, cands):
    for c in cands:
        if dim % c == 0:
            return c
    return dim


def _dot(a, b, ca, cb):
    return lax.dot_general(a, b, (((ca,), (cb,)), ((), ())), preferred_element_type=F32)


def _mm(a, b, *, name, ta=False, tb=False, out_dtype=BF16, res=None, alpha=1.0, tm=None, tn=None, tk=None, after=None,
        a_halves=False, b_halves=False):
    assert not (a_halves and ta) and not (b_halves and tb)
    if a_halves:
        m, k = a.shape[1], 2 * a.shape[2]
    else:
        m, k = (a.shape[1], a.shape[0]) if ta else a.shape
    if b_halves:
        n = 2 * b.shape[2]
        assert k == b.shape[1]
    else:
        n = b.shape[0] if tb else b.shape[1]
        assert k == (b.shape[1] if tb else b.shape[0]), (a.shape, b.shape, ta, tb)
    if ta:
        tm = tm or _pick(m, (512, 256, 128))
        tn = tn or _pick(n, (1024, 512, 256, 128))
        tk = tk or (k if k <= MM_WHOLE_K else _pick(k, (1024, 512, 256, 128)))
    else:
        tk = tk or (k if k <= MM_WHOLE_K else _pick(k, (MM_WHOLE_K, 2048, 1024, 512, 256, 128)))
        tn = tn or _pick(n, (512, 1408, 256, 128) if tk == k else (1024, 512, 256, 128))
        per_row = 2 * (tk * a.dtype.itemsize + tn * (jnp.dtype(out_dtype).itemsize + (0 if res is None else res.dtype.itemsize)))
        per_row += 4 * tn if tk < k else 0
        rows = (MM_VMEM_BUDGET_BYTES - 2 * tk * tn * b.dtype.itemsize) // per_row
        tm = tm or next((c for c in (2048, 1024, 512, 256, 128) if m % c == 0 and c <= rows), m)
    if a_halves:
        tk = min(tk, k // 2) if (k // 2) % min(tk, k // 2) == 0 else _pick(k // 2, (1408, 1024, 512, 256, 128))
    if b_halves:
        tn = tn if (n // 2) % tn == 0 else _pick(n // 2, (1408, 1024, 512, 256, 128))
    nk = k // tk
    assert m % tm == 0 and n % tn == 0 and k % tk == 0
    a_spec = pl.BlockSpec((tk, tm), lambda i, j, kk: (kk, i)) if ta else pl.BlockSpec((tm, tk), lambda i, j, kk: (i, kk))
    b_spec = pl.BlockSpec((tn, tk), lambda i, j, kk: (j, kk)) if tb else pl.BlockSpec((tk, tn), lambda i, j, kk: (kk, j))
    if a_halves:
        per = (k // 2) // tk
        a_spec = pl.BlockSpec((None, tm, tk), lambda i, j, kk: (kk // per, i, kk % per))
    if b_halves:
        per_n = (n // 2) // tn
        b_spec = pl.BlockSpec((None, tk, tn), lambda i, j, kk: (j // per_n, kk, j % per_n))
    o_spec = pl.BlockSpec((tm, tn), lambda i, j, kk: (i, j))
    ca, cb = (0 if ta else 1), (1 if tb else 0)

    n_in = 2 + (res is not None) + (after is not None)

    def body(*refs):
        a_ref, b_ref = refs[:2]
        res_ref = refs[2] if res is not None else None
        o_ref = refs[n_in]
        scratch = refs[n_in + 1:]

        def finish(acc):
            val = acc if alpha == 1.0 else alpha * acc
            if res_ref is not None:
                val = res_ref[...].astype(F32) + val
            o_ref[...] = val.astype(o_ref.dtype)

        part = _dot(a_ref[...].astype(BF16), b_ref[...].astype(BF16), ca, cb)
        if nk == 1:
            finish(part)
        else:
            acc_ref = scratch[0]
            kk = pl.program_id(2)

            @pl.when(kk == 0)
            def _():
                acc_ref[...] = part

            @pl.when(kk > 0)
            def _():
                acc_ref[...] += part

            @pl.when(kk == nk - 1)
            def _():
                finish(acc_ref[...])

    ins = [a, b] + ([] if res is None else [res]) + ([] if after is None else [after])
    in_specs = [a_spec, b_spec] + ([] if res is None else [o_spec]) + ([] if after is None else [ANY])
    return _pcall(
        body, name=name, grid=(m // tm, n // tn, nk), in_specs=in_specs, out_specs=o_spec,
        out_shape=jax.ShapeDtypeStruct((m, n), out_dtype),
        scratch_shapes=[pltpu.VMEM((tm, tn), F32)] if nk > 1 else [],
        compiler_params=_params("parallel", "parallel", "arbitrary"),
    )(*ins)


def _rowcall(fn, rows, consts, outs, accs=(), *, tm, name, after=None):
    s = rows[0][0].shape[0]
    assert s % tm == 0
    n_read, n_out = len(rows) + len(consts), len(outs)
    n_in = n_read + (after is not None)

    def body(*refs):
        vals = fn(*[r[...] for r in refs[:n_read]])
        vals = vals if isinstance(vals, (tuple, list)) else (vals,)
        for o_ref, v in zip(refs[n_in:n_in + n_out], vals[:n_out]):
            o_ref[...] = v.astype(o_ref.dtype)
        if accs:
            first = pl.program_id(0) == 0
            for a_ref, v in zip(refs[n_in + n_out:], vals[n_out:]):
                tot = jnp.sum(v.astype(F32), axis=0, keepdims=True)

                @pl.when(first)
                def _(a_ref=a_ref, tot=tot):
                    a_ref[...] = tot

                @pl.when(jnp.logical_not(first))
                def _(a_ref=a_ref, tot=tot):
                    a_ref[...] += tot

    in_specs = [pl.BlockSpec((tm, w), lambda i, cb=cb: (i, cb)) for (_, cb, w) in rows]
    in_specs += [pl.BlockSpec(c.shape, lambda i: (0, 0)) for c in consts]
    in_specs += [] if after is None else [ANY]
    out_specs = [pl.BlockSpec((tm, w), lambda i: (i, 0)) for (w, _) in outs]
    out_specs += [pl.BlockSpec((1, w), lambda i: (0, 0)) for w in accs]
    out_shape = [jax.ShapeDtypeStruct((s, w), dt) for (w, dt) in outs]
    out_shape += [jax.ShapeDtypeStruct((1, w), F32) for w in accs]
    return _pcall(
        body, name=name, grid=(s // tm,), in_specs=in_specs, out_specs=out_specs, out_shape=out_shape,
        compiler_params=_params("arbitrary" if accs else "parallel"),
    )(*[r[0] for r in rows], *consts, *([] if after is None else [after]))


def _whole(a):
    return (a, 0, a.shape[1])


def _xhat(x):
    x = x.astype(F32)
    r = lax.rsqrt(jnp.mean(x * x, axis=-1, keepdims=True) + RMS_EPS)
    return x * r, r


def _rms_bwd(dy, x, g):
    xh, r = _xhat(x)
    dxh = dy.astype(F32) * g
    dx = r * (dxh - xh * jnp.mean(dxh * xh, axis=-1, keepdims=True))
    return dx, dy.astype(F32) * xh


def _sigmoid(x):
    return 1.0 / (1.0 + jnp.exp(-x))


def _rms_fwd(x, g, name, tm, after=None):
    d = x.shape[1]
    return _rowcall(lambda xb, gb: _xhat(xb)[0] * gb, [_whole(x)], [g], [(d, BF16)], tm=tm, name=name, after=after)[0]


def _silu_parts(gate):
    sg = _sigmoid(gate)
    return sg, gate * sg


MXU_COLUMNS = 256


def _chunks(n):
    return [(c0, min(MXU_COLUMNS, n - c0)) for c0 in range(0, n, MXU_COLUMNS)]


def _ffn_up(n, w_gu, name):
    s, d = n.shape
    f = w_gu.shape[1] // 2
    tn = _pick(f, (1408, 1024, 512, 256, 128))
    tm = _pick(s, (1024, 512, 256, 128))
    nb = f // tn

    def body(n_ref, wg_ref, wu_ref, gu_ref, act_ref):
        nv = n_ref[...]
        for c0, cn in _chunks(tn):
            gate = _dot(nv, wg_ref[:, c0:c0 + cn], 1, 0)
            up = _dot(nv, wu_ref[:, c0:c0 + cn], 1, 0)
            gu_ref[0, :, c0:c0 + cn] = gate.astype(gu_ref.dtype)
            gu_ref[1, :, c0:c0 + cn] = up.astype(gu_ref.dtype)
            act_ref[:, c0:c0 + cn] = (_silu_parts(gate)[1] * up).astype(act_ref.dtype)

    return _pcall(
        body, name=name, grid=(s // tm, nb),
        in_specs=[pl.BlockSpec((tm, d), lambda i, j: (i, 0)), pl.BlockSpec((d, tn), lambda i, j: (0, j)),
                  pl.BlockSpec((d, tn), lambda i, j: (0, nb + j))],
        out_specs=[pl.BlockSpec((2, tm, tn), lambda i, j: (0, i, j)), pl.BlockSpec((tm, tn), lambda i, j: (i, j))],
        out_shape=[jax.ShapeDtypeStruct((2, s, f), BF16), jax.ShapeDtypeStruct((s, f), BF16)],
        compiler_params=_params("parallel", "parallel"),
    )(n, w_gu, w_gu)


def _ffn_dgu(dhb, w_down, gu, name, after=None):
    s, d = dhb.shape
    f = w_down.shape[0]
    tn = _pick(f, (1408, 1024, 512, 256, 128))
    tm = _pick(s, (1024, 512, 256, 128))

    def body(dh_ref, w_ref, gu_ref, *rest):
        o_ref = rest[-1]
        dh = dh_ref[...]
        for c0, cn in _chunks(tn):
            dact = _dot(dh, w_ref[c0:c0 + cn, :], 1, 1)
            gate, up = gu_ref[0, :, c0:c0 + cn].astype(F32), gu_ref[1, :, c0:c0 + cn].astype(F32)
            sg, silu = _silu_parts(gate)
            o_ref[0, :, c0:c0 + cn] = (dact * up * (sg + silu * (1.0 - sg))).astype(o_ref.dtype)
            o_ref[1, :, c0:c0 + cn] = (dact * silu).astype(o_ref.dtype)

    blk = pl.BlockSpec((2, tm, tn), lambda i, j: (0, i, j))
    return _pcall(
        body, name=name, grid=(s // tm, f // tn),
        in_specs=[pl.BlockSpec((tm, d), lambda i, j: (i, 0)), pl.BlockSpec((tn, d), lambda i, j: (j, 0)), blk]
        + ([] if after is None else [ANY]),
        out_specs=blk, out_shape=jax.ShapeDtypeStruct((2, s, f), BF16), compiler_params=_params("parallel", "parallel"),
    )(dhb, w_down, gu, *([] if after is None else [after]))


def _dgrad_norm(dy, wmat, dh, x, g, name, *, dy_halves=False, copy_scale=None, after=None):
    s, d = dh.shape
    k = wmat.shape[1]
    tk = k if k <= MM_WHOLE_K else _pick(k, (MM_WHOLE_K, 2048, 1024, 512, 256, 128))
    if dy_halves and (k // 2) % tk:
        tk = _pick(k // 2, (1408, 1024, 512, 256, 128))
    tm = _pick(s, (512, 256, 128))
    nk, per = k // tk, (k // 2) // tk if dy_halves else 0
    n_in = 5 + (after is not None)
    n_out = 2 + (copy_scale is not None)

    def body(*refs):
        dy_ref, w_ref, dh_ref, x_ref, g_ref = refs[:5]
        outs, scratch = refs[n_in:n_in + n_out], refs[n_in + n_out:]
        i, kk = pl.program_id(0), pl.program_id(1)
        part = _dot(dy_ref[...], w_ref[...], 1, 1)

        def finish(dn):
            dx, dg = _rms_bwd(dn, x_ref[...], g_ref[...])
            tot = dh_ref[...] + dx
            outs[0][...] = tot
            if copy_scale is not None:
                outs[1][...] = (copy_scale * tot).astype(outs[1].dtype)
            dg = jnp.sum(dg, axis=0, keepdims=True)

            @pl.when(i == 0)
            def _():
                outs[-1][...] = dg

            @pl.when(i > 0)
            def _():
                outs[-1][...] += dg

        if nk == 1:
            finish(part)
        else:
            acc_ref = scratch[0]

            @pl.when(kk == 0)
            def _():
                acc_ref[...] = part

            @pl.when(kk > 0)
            def _():
                acc_ref[...] += part

            @pl.when(kk == nk - 1)
            def _():
                finish(acc_ref[...])

    row = pl.BlockSpec((tm, d), lambda i, kk: (i, 0))
    dy_spec = pl.BlockSpec((None, tm, tk), lambda i, kk: (kk // per, i, kk % per)) if dy_halves else pl.BlockSpec((tm, tk), lambda i, kk: (i, kk))
    in_specs = [dy_spec, pl.BlockSpec((d, tk), lambda i, kk: (0, kk)), row, row, pl.BlockSpec((1, d), lambda i, kk: (0, 0))]
    out_specs = [row] * (n_out - 1) + [pl.BlockSpec((1, d), lambda i, kk: (0, 0))]
    out_shape = [jax.ShapeDtypeStruct((s, d), F32)] + ([] if copy_scale is None else [jax.ShapeDtypeStruct((s, d), BF16)])
    return _pcall(
        body, name=name, grid=(s // tm, nk), in_specs=in_specs + ([] if after is None else [ANY]), out_specs=out_specs,
        out_shape=out_shape + [jax.ShapeDtypeStruct((1, d), F32)], scratch_shapes=[pltpu.VMEM((tm, d), F32)] if nk > 1 else [],
        compiler_params=_params("arbitrary", "arbitrary"),
    )(dy, wmat, dh, x, g, *([] if after is None else [after]))


def _shift_down(p, k):
    if k == 0:
        return p
    rows = lax.broadcasted_iota(jnp.int32, p.shape, 0)
    return jnp.where(rows >= k, pltpu.roll(p, k, 0), 0.0)


def _shift_up(p, k):
    if k == 0:
        return p
    s = p.shape[0]
    rows = lax.broadcasted_iota(jnp.int32, p.shape, 0)
    return jnp.where(rows < s - k, pltpu.roll(p, s - k, 0), 0.0)


def _conv_fwd(proj, conv_w, d, tc, name):
    s = proj.shape[0]
    nb = d // tc

    def body(cb_ref, cc_ref, cx_ref, w_ref, y_ref):
        p = cc_ref[...].astype(F32) * cx_ref[...].astype(F32)
        w = w_ref[...]
        acc = p * w[CONV_K - 1:CONV_K, :]
        for k in range(1, CONV_K):
            acc = acc + _shift_down(p, k) * w[CONV_K - 1 - k:CONV_K - k, :]
        y_ref[...] = (cb_ref[...].astype(F32) * acc).astype(y_ref.dtype)

    col = lambda off: pl.BlockSpec((s, tc), lambda j: (0, off * nb + j))
    return _pcall(
        body, name=name, grid=(nb,), in_specs=[col(0), col(1), col(2), pl.BlockSpec((CONV_K, tc), lambda j: (0, j))],
        out_specs=pl.BlockSpec((s, tc), lambda j: (0, j)), out_shape=jax.ShapeDtypeStruct((s, d), BF16),
        compiler_params=_params("parallel"),
    )(proj, proj, proj, conv_w)


def _conv_bwd(dy, proj, conv_w, d, tc, name):
    s = proj.shape[0]
    nb = d // tc

    def body(dy_ref, cb_ref, cc_ref, cx_ref, w_ref, dcb_ref, dcc_ref, dcx_ref, dw_ref):
        cc, cx = cc_ref[...].astype(F32), cx_ref[...].astype(F32)
        p = cc * cx
        w = w_ref[...]
        dyv = dy_ref[...].astype(F32)
        shifted = [_shift_down(p, CONV_K - 1 - k) for k in range(CONV_K)]
        conv = shifted[0] * w[0:1, :]
        for k in range(1, CONV_K):
            conv = conv + shifted[k] * w[k:k + 1, :]
        dcb_ref[...] = (dyv * conv).astype(dcb_ref.dtype)
        ds = dyv * cb_ref[...].astype(F32)
        dp = ds * w[CONV_K - 1:CONV_K, :]
        for k in range(1, CONV_K):
            dp = dp + _shift_up(ds, k) * w[CONV_K - 1 - k:CONV_K - k, :]
        dcc_ref[...] = (dp * cx).astype(dcc_ref.dtype)
        dcx_ref[...] = (dp * cc).astype(dcx_ref.dtype)
        for k in range(CONV_K):
            dw_ref[k:k + 1, :] = jnp.sum(ds * shifted[k], axis=0, keepdims=True)

    col = lambda off: pl.BlockSpec((s, tc), lambda j: (0, off * nb + j))
    blk = pl.BlockSpec((s, tc), lambda j: (0, j))
    wblk = pl.BlockSpec((CONV_K, tc), lambda j: (0, j))
    act = jax.ShapeDtypeStruct((s, d), BF16)
    return _pcall(
        body, name=name, grid=(nb,), in_specs=[blk, col(0), col(1), col(2), wblk],
        out_specs=[blk, blk, blk, wblk], out_shape=[act, act, act, jax.ShapeDtypeStruct((CONV_K, d), F32)],
        compiler_params=_params("parallel"),
    )(dy, proj, proj, proj, conv_w)


def _sb_tile(q, kj, scale, carry, tri, mask):
    z = _dot(q, kj, 1, 1) * scale
    lsz = jnp.minimum(z, 0.0) - jnp.log(1.0 + jnp.exp(-jnp.abs(z)))
    l1m = lsz - z
    if mask is not None:
        l1m = jnp.where(mask, l1m, 0.0)
    l1b = l1m.astype(BF16)
    a = jnp.exp(lsz + (carry + _dot(l1b, tri, 1, 0)))
    if mask is not None:
        a = jnp.where(mask, a, 0.0)
    return lsz, l1b, a.astype(BF16)


def _add_rows(x, upd, r0):
    return x + upd if r0 == 0 else jnp.concatenate([x[:r0], x[r0:] + upd], axis=0)


def _sb_masks(tq, tk):
    row = lax.broadcasted_iota(jnp.int32, (tq, tk), 0)
    col = lax.broadcasted_iota(jnp.int32, (tq, tk), 1)
    masks = [col + dj * tk < row for dj in range(tq // tk)]
    r2 = lax.broadcasted_iota(jnp.int32, (tk, tk), 0)
    c2 = lax.broadcasted_iota(jnp.int32, (tk, tk), 1)
    return masks, (r2 > c2).astype(BF16), (r2 < c2).astype(BF16)


def _sb_fwd(proj, heads, col0, tq, tk, name):
    s = proj.shape[0]
    dh = SB_HEAD_DIM
    nq, nd, nkt = s // tq, tq // tk, s // tk
    scale = dh ** -0.5

    def body(q_ref, k_ref, v_ref, o_ref, a_ref, b_ref):
        i = pl.program_id(1)
        q = q_ref[...]
        masks, tri_right, _ = _sb_masks(tq, tk)

        def tile(j, carry, acc, mask, r0=0):
            start = pl.multiple_of(j * tk, tk)
            kj = k_ref[pl.ds(start, tk), :]
            vj = v_ref[pl.ds(start, tk), :]
            lsz, l1b, ab = _sb_tile(q[r0:], kj, scale, carry[r0:], tri_right, None if mask is None else mask[r0:])
            a_ref[j, r0:, :] = ab
            b_ref[j, r0:, :] = jnp.exp(lsz).astype(b_ref.dtype)
            if r0:
                a_ref[j, :r0, :] = jnp.zeros((r0, tk), a_ref.dtype)
                b_ref[j, :r0, :] = jnp.zeros((r0, tk), b_ref.dtype)
            return (_add_rows(carry, jnp.sum(l1b.astype(F32), axis=1, keepdims=True), r0),
                    _add_rows(acc, _dot(ab, vj, 1, 0), r0))

        state = (jnp.zeros((tq, 1), F32), jnp.zeros((tq, dh), F32))
        for dj in reversed(range(nd)):
            state = tile(i * nd + dj, *state, masks[dj], dj * tk)
        def left_block(t, st):
            for dj in reversed(range(nd)):
                st = tile((i - 1 - t) * nd + dj, st[0], st[1], None)
            return st

        state = lax.fori_loop(0, i, left_block, state)
        o_ref[...] = state[1]

    qspec = pl.BlockSpec((tq, dh), lambda h, i: (i, col0[0] + h))
    kspec = pl.BlockSpec((s, dh), lambda h, i: (0, col0[1] + h))
    vspec = pl.BlockSpec((s, dh), lambda h, i: (0, col0[2] + h))
    saved = pl.BlockSpec((None, nkt, tq, tk), lambda h, i: (h, 0, i, 0))
    saved_shape = jax.ShapeDtypeStruct((heads, nkt, s, tk), BF16)
    return _pcall(
        body, name=name, grid=(heads, nq), in_specs=[qspec, kspec, vspec],
        out_specs=[pl.BlockSpec((tq, dh), lambda h, i: (i, h)), saved, saved],
        out_shape=[jax.ShapeDtypeStruct((s, heads * dh), F32), saved_shape, saved_shape],
        compiler_params=_params("parallel", "parallel"),
    )(proj, proj, proj)


def _sb_bwd(proj, o, a_all, beta_all, do, heads, col0, tq, tk, name):
    s = proj.shape[0]
    dh = SB_HEAD_DIM
    nq, nd, nkt = s // tq, tq // tk, s // tk
    scale = dh ** -0.5

    def body(q_ref, k_ref, v_ref, o_ref, a_ref, b_ref, do_ref, dq_ref, dk_ref, dv_ref, dk_acc, dv_acc):
        i = pl.program_id(1)

        @pl.when(i == 0)
        def _():
            dk_acc[...] = jnp.zeros_like(dk_acc)
            dv_acc[...] = jnp.zeros_like(dv_acc)

        q = q_ref[...]
        dob = do_ref[...].astype(BF16)
        delta = jnp.sum(dob.astype(F32) * o_ref[...], axis=1, keepdims=True)
        masks, _, tri_left = _sb_masks(tq, tk)

        def tile(j, carry_g, dq, mask):
            start = pl.multiple_of(j * tk, tk)
            kj = k_ref[pl.ds(start, tk), :]
            vj = v_ref[pl.ds(start, tk), :]
            ab = a_ref[j]
            g = _dot(dob, vj, 1, 1) * ab.astype(F32)
            carry_g = carry_g + jnp.sum(g, axis=1, keepdims=True)
            left = (delta - carry_g) + _dot(g.astype(BF16), tri_left, 1, 0)
            dz = g - b_ref[j].astype(F32) * (g + left)
            if mask is not None:
                dz = jnp.where(mask, dz, 0.0)
            dzb = dz.astype(BF16)
            dk_acc[pl.ds(start, tk), :] += _dot(dzb, q, 0, 0)
            dv_acc[pl.ds(start, tk), :] += _dot(ab, dob, 0, 0)
            return carry_g, dq + _dot(dzb, kj, 1, 0)

        state = (jnp.zeros((tq, 1), F32), jnp.zeros((tq, dh), F32))
        for dj in reversed(range(nd)):
            state = tile(i * nd + dj, *state, masks[dj])
        def left_block(t, st):
            for dj in reversed(range(nd)):
                st = tile((i - 1 - t) * nd + dj, st[0], st[1], None)
            return st

        state = lax.fori_loop(0, i, left_block, state)
        dq_ref[...] = (state[1] * scale).astype(dq_ref.dtype)

        @pl.when(i == nq - 1)
        def _():
            dk_ref[...] = (dk_acc[...] * scale).astype(dk_ref.dtype)
            dv_ref[...] = dv_acc[...].astype(dv_ref.dtype)

    qspec = pl.BlockSpec((tq, dh), lambda h, i: (i, col0[0] + h))
    kspec = pl.BlockSpec((s, dh), lambda h, i: (0, col0[1] + h))
    vspec = pl.BlockSpec((s, dh), lambda h, i: (0, col0[2] + h))
    blk = pl.BlockSpec((tq, dh), lambda h, i: (i, h))
    full = pl.BlockSpec((s, dh), lambda h, i: (0, h))
    saved = pl.BlockSpec((None, nkt, tq, tk), lambda h, i: (h, 0, i, 0))
    act = jax.ShapeDtypeStruct((s, heads * dh), BF16)
    return _pcall(
        body, name=name, grid=(heads, nq), in_specs=[qspec, kspec, vspec, blk, saved, saved, blk],
        out_specs=[blk, full, full], out_shape=[act, act, act],
        scratch_shapes=[pltpu.VMEM((s, dh), F32), pltpu.VMEM((s, dh), F32)],
        compiler_params=_params("parallel", "arbitrary"),
    )(proj, proj, proj, o, a_all, beta_all, do)


def _xattn_probs(q, k, scale):
    sc = _dot(q, k, 1, 1) * scale
    e = jnp.exp(sc - jnp.max(sc, axis=1, keepdims=True))
    return e / jnp.sum(e, axis=1, keepdims=True)


def _xattn_fwd(qc, kv, tq, name):
    s, d = qc.shape
    m = kv.shape[0]
    dh = d // X_HEADS
    scale = dh ** -0.5

    def body(q_ref, k_ref, v_ref, o_ref):
        p = _xattn_probs(q_ref[...], k_ref[...], scale)
        o_ref[...] = _dot(p.astype(BF16), v_ref[...], 1, 0).astype(o_ref.dtype)

    blk = pl.BlockSpec((tq, dh), lambda h, i: (i, h))
    return _pcall(
        body, name=name, grid=(X_HEADS, s // tq),
        in_specs=[blk, pl.BlockSpec((m, dh), lambda h, i: (0, h)), pl.BlockSpec((m, dh), lambda h, i: (0, X_HEADS + h))],
        out_specs=blk, out_shape=jax.ShapeDtypeStruct((s, d), BF16), compiler_params=_params("parallel", "parallel"),
    )(qc, kv, kv)


def _xattn_bwd(qc, kv, do, tq, name):
    s, d = qc.shape
    m = kv.shape[0]
    dh = d // X_HEADS
    scale = dh ** -0.5
    nq = s // tq

    def body(q_ref, k_ref, v_ref, do_ref, dq_ref, dk_ref, dv_ref, dk_acc, dv_acc):
        i = pl.program_id(1)
        q, k, v = q_ref[...], k_ref[...], v_ref[...]
        dob = do_ref[...].astype(BF16)
        p = _xattn_probs(q, k, scale)
        pb = p.astype(BF16)
        dp = _dot(dob, v, 1, 1)
        ds = pb.astype(F32) * (dp - jnp.sum(dp * pb.astype(F32), axis=1, keepdims=True))
        dsb = (ds * scale).astype(BF16)
        dq_ref[...] = _dot(dsb, k, 1, 0).astype(dq_ref.dtype)
        dk_part = _dot(dsb, q, 0, 0)
        dv_part = _dot(pb, dob, 0, 0)

        @pl.when(i == 0)
        def _():
            dk_acc[...] = dk_part
            dv_acc[...] = dv_part

        @pl.when(i > 0)
        def _():
            dk_acc[...] += dk_part
            dv_acc[...] += dv_part

        @pl.when(i == nq - 1)
        def _():
            dk_ref[...] = dk_acc[...].astype(dk_ref.dtype)
            dv_ref[...] = dv_acc[...].astype(dv_ref.dtype)

    blk = pl.BlockSpec((tq, dh), lambda h, i: (i, h))
    kblk = pl.BlockSpec((m, dh), lambda h, i: (0, h))
    return _pcall(
        body, name=name, grid=(X_HEADS, nq),
        in_specs=[blk, kblk, pl.BlockSpec((m, dh), lambda h, i: (0, X_HEADS + h)), blk],
        out_specs=[blk, kblk, kblk],
        out_shape=[jax.ShapeDtypeStruct((s, d), BF16), jax.ShapeDtypeStruct((m, d), BF16), jax.ShapeDtypeStruct((m, d), BF16)],
        scratch_shapes=[pltpu.VMEM((m, dh), F32), pltpu.VMEM((m, dh), F32)],
        compiler_params=_params("parallel", "arbitrary"),
    )(qc, kv, kv, do)


def _local_step(x, mem, tgt, w, fetch=None, prefetch=None, emit=None, tick=None, after=None):
    fetch = fetch or (lambda name, after: {})
    prefetch = prefetch or (lambda name, after: None)
    emit = emit or (lambda group, g: None)
    tick = tick or (lambda group, after: None)
    w = dict(w)
    s, d = x.shape
    heads = d // SB_HEAD_DIM
    tm = _pick(s, (512, 256, 128))
    tq = _pick(s, (256, 128))
    sb_tq, sb_tk = _pick(s, (512, 256, 128)), _pick(s, (256, 128))
    tc = _pick(d, (256, 128))
    g = {}

    def wt(name, after):
        if name not in w:
            w.update(fetch(name, after))
        return w[name]

    def ffn_fwd(h, gname, wgu, wdown, tag, after=None):
        n = _rms_fwd(h, w[gname], tag + "_norm", tm, after=after)
        gu, act = _ffn_up(n, wt(wgu, n), tag + "_gu")
        prefetch(wdown, gu)
        return n, gu, act, _mm(act, wt(wdown, act), name=tag + "_down", out_dtype=F32, res=h, alpha=0.5)

    def ffn_bwd(dh, dhb, h, saved, gname, wgu, wdown, tag, copy_scale=None, after=None):
        n, gu, act = saved
        g[wdown] = _mm(act, dhb, ta=True, name=tag + "_dwdown", after=after)
        dgu = _ffn_dgu(dhb, w[wdown], gu, tag + "_dgu", after=emit(tag + "_down", g))
        g[wgu] = _mm(n, dgu, ta=True, b_halves=True, name=tag + "_dwgu", after=tick(tag + "_down", dgu))
        *dh_in, g[gname] = _dgrad_norm(dgu, w[wgu], dh, h, w[gname], tag + "_dn", dy_halves=True, copy_scale=copy_scale,
                                       after=emit(tag, g))
        return dh_in, tick(tag, dh_in[0])

    n1, gu1, act1, h1 = ffn_fwd(x, "g_ffn1", "w_ffn1_gu", "w_ffn1_down", "ffn1", after)
    prefetch("w_in", h1)
    u = _rms_fwd(h1, w["g_mix"], "mix_norm", tm)
    proj = _mm(u, wt("w_in", u), name="mix_in")
    prefetch("w_conv_out", proj)
    nd = d // SB_HEAD_DIM
    y_conv = _conv_fwd(proj, w["conv_w"], d, tc, "conv_fwd")
    sb_cols = (3 * nd, 4 * nd, 5 * nd)
    y_sb, sb_a, sb_beta = _sb_fwd(proj, heads, sb_cols, sb_tq, sb_tk, "sb_fwd")
    prefetch("w_cq", y_sb)
    a_conv = _mm(y_conv, wt("w_conv_out", y_conv), name="conv_out")
    a_sb = _mm(y_sb, wt("w_attn_out", y_sb), name="attn_out")
    b_conv, b_sb = w["b_gate"][:, :d], w["b_gate"][:, d:]

    def merge(ac, asb, gcp, gsp, bc, bs):
        gc = _sigmoid(gcp.astype(F32) + bc)
        gs = _sigmoid(gsp.astype(F32) + bs)
        return gc * ac.astype(F32) + gs * asb.astype(F32)

    merged = _rowcall(merge, [_whole(a_conv), _whole(a_sb), (proj, 6, d), (proj, 7, d)], [b_conv, b_sb], [(d, BF16)],
                      tm=tm, name="merge")[0]
    prefetch("w_ffn2_gu", merged)
    h2 = _mm(merged, wt("w_o", merged), name="mix_out", out_dtype=F32, res=h1)
    hn = _rms_fwd(h2, w["g_cross"], "cross_norm", tm)
    mn = _rms_fwd(mem, w["g_mem"], "mem_norm", _pick(mem.shape[0], (256, 128)))
    qc = _mm(hn, wt("w_cq", hn), name="cross_q")
    kv = _mm(mn, wt("w_ckv", mn), name="cross_kv")
    oc = _xattn_fwd(qc, kv, tq, "xattn_fwd")
    h3 = _mm(oc, wt("w_co", oc), name="cross_out", out_dtype=F32, res=h2)
    n2, gu2, act2, h4 = ffn_fwd(h3, "g_ffn2", "w_ffn2_gu", "w_ffn2_down", "ffn2")

    def head(hb, tb, gb):
        xh, r = _xhat(hb)
        err = xh * gb - tb
        dy = err * (1.0 / d)
        dxh = dy * gb
        dx = r * (dxh - xh * jnp.mean(dxh * xh, axis=-1, keepdims=True))
        row_loss = 0.5 * jnp.mean(err * err, axis=-1, keepdims=True)
        return dx, 0.5 * dx, dy * xh, jnp.broadcast_to(row_loss, (row_loss.shape[0], LANES))

    dh4, dh4b, g["g_final"], loss_lanes = _rowcall(head, [_whole(h4), _whole(tgt)], [w["g_final"]], [(d, F32), (d, BF16)],
                                                   [d, LANES], tm=tm, name="loss_head")

    (dh3, dh3b), tok = ffn_bwd(dh4, dh4b, h3, (n2, gu2, act2), "g_ffn2", "w_ffn2_gu", "w_ffn2_down", "ffn2", copy_scale=1.0)
    g["w_co"] = _mm(oc, dh3b, ta=True, name="cross_dwco", after=tok)
    doc = _mm(dh3b, w["w_co"], tb=True, name="cross_doc")
    dqc, dk, dv = _xattn_bwd(qc, kv, doc, tq, "xattn_bwd")
    dkv = jnp.concatenate([dk, dv], axis=1)
    g["w_cq"] = _mm(hn, dqc, ta=True, name="cross_dwcq")
    g["w_ckv"] = _mm(mn, dkv, ta=True, name="cross_dwckv")
    dmn = _mm(dkv, w["w_ckv"], tb=True, name="cross_dmn", out_dtype=F32)
    g["g_mem"] = _rowcall(lambda dy, xb: dy * _xhat(xb)[0], [_whole(dmn), _whole(mem)], [], [], [d],
                          tm=_pick(mem.shape[0], (256, 128)), name="mem_dnorm")[0]
    dh2, dh2b, g["g_cross"] = _dgrad_norm(dqc, w["w_cq"], dh3, h2, w["g_cross"], "cross_dhn", copy_scale=1.0, after=emit("cross", g))

    g["w_o"] = _mm(merged, dh2b, ta=True, name="mix_dwo", after=tick("cross", dh2))
    dmerged = _mm(dh2b, w["w_o"], tb=True, name="mix_dmerged")

    def merge_bwd(dm, ac, asb, gcp, gsp, bc, bs):
        dm, ac, asb = dm.astype(F32), ac.astype(F32), asb.astype(F32)
        gc = _sigmoid(gcp.astype(F32) + bc)
        gs = _sigmoid(gsp.astype(F32) + bs)
        dgc = dm * ac * gc * (1.0 - gc)
        dgs = dm * asb * gs * (1.0 - gs)
        return dm * gc, dm * gs, dgc, dgs, dgc, dgs

    da_conv, da_sb, dgc, dgs, db_conv, db_sb = _rowcall(
        merge_bwd, [_whole(dmerged), _whole(a_conv), _whole(a_sb), (proj, 6, d), (proj, 7, d)], [b_conv, b_sb],
        [(d, BF16)] * 4, [d, d], tm=tm, name="merge_bwd")
    g["b_gate"] = jnp.concatenate([db_conv, db_sb], axis=1)
    g["w_conv_out"] = _mm(y_conv, da_conv, ta=True, name="conv_dwout")
    g["w_attn_out"] = _mm(y_sb, da_sb, ta=True, name="attn_dwout")
    dy_conv = _mm(da_conv, w["w_conv_out"], tb=True, name="conv_dy")
    dy_sb = _mm(da_sb, w["w_attn_out"], tb=True, name="attn_dy")
    dcb, dcc, dcx, g["conv_w"] = _conv_bwd(dy_conv, proj, w["conv_w"], d, tc, "conv_bwd")
    dq, dk_sb, dv_sb = _sb_bwd(proj, y_sb, sb_a, sb_beta, dy_sb, heads, sb_cols, sb_tq, sb_tk, "sb_bwd")
    dproj = jnp.concatenate([dcb, dcc, dcx, dq, dk_sb, dv_sb, dgc, dgs], axis=1)
    g["w_in"] = _mm(u, dproj, ta=True, name="mix_dwin")
    dh1, dh1b, g["g_mix"] = _dgrad_norm(dproj, w["w_in"], dh2, h1, w["g_mix"], "mix_du", copy_scale=0.5, after=emit("mix", g))
    (dx,), tok = ffn_bwd(dh1, dh1b, x, (n1, gu1, act1), "g_ffn1", "w_ffn1_gu", "w_ffn1_down", "ffn1", after=tick("mix", dh1))
    return loss_lanes, dx, g, tok


MATS = (("w_ffn1_gu", "col"), ("w_ffn1_down", "row"), ("w_in", "col"), ("w_conv_out", "row"), ("w_attn_out", "row"),
        ("w_o", "row"), ("w_cq", "row"), ("w_ckv", "col"), ("w_co", "row"), ("w_ffn2_gu", "col"), ("w_ffn2_down", "row"))
VECS = ("g_ffn1", "g_mix", "g_cross", "g_mem", "g_ffn2", "g_final")
WEIGHTS = ("g_ffn1", "w_ffn1_gu", "w_ffn1_down", "g_mix", "w_in", "b_gate", "conv_w", "w_conv_out", "w_attn_out", "w_o",
           "g_cross", "g_mem", "w_cq", "w_ckv", "w_co", "g_ffn2", "w_ffn2_gu", "w_ffn2_down", "g_final")
CONV_ROWS = 8


def _full_shape(kind, r, c):
    return (r, N_CHIPS * c) if kind == "col" else (N_CHIPS * r, c)


def _piece(ref, kind, r, c, chip, half):
    hr = r // 2
    if kind == "col":
        return ref.at[pl.ds(pl.multiple_of(half * hr, 16), hr), pl.ds(pl.multiple_of(chip * c, LANES), c)]
    return ref.at[pl.ds(pl.multiple_of(chip * r + half * hr, 16), hr), :]


def _shard_of(ref, kind, r, c, chip):
    if kind == "col":
        return ref.at[:, pl.ds(pl.multiple_of(chip * c, LANES), c)]
    return ref.at[pl.ds(pl.multiple_of(chip * r, 16), r), :]


def _place():
    x, y, c = lax.axis_index("x"), lax.axis_index("y"), lax.axis_index("c")
    others = [(1 - x, y), (x, 1 - y), (1 - x, 1 - y)]
    return x, y, c, 2 * x + y, others


def _remote(src, dst, send_sem, recv_sem, to):
    return pltpu.make_async_remote_copy(src_ref=src, dst_ref=dst, send_sem=send_sem, recv_sem=recv_sem,
                                        device_id=to, device_id_type=MESH)


def _gather_conv(conv_shard):
    cc = conv_shard.shape[1]

    def body(conv_ref, conv_full, cs, cr, cl):
        x, y, c, me, others = _place()

        def cols(chip):
            return conv_full.at[:, pl.ds(pl.multiple_of(chip * cc, LANES), cc)]

        def conv(k, chip_from, to):
            return _remote(conv_ref, cols(chip_from), cs.at[k], cr.at[k], to)

        mine = pltpu.make_async_copy(conv_ref, cols(me), cl.at[0])
        mine.start()
        for k, (ox, oy) in enumerate(others):
            conv(k, me, (ox, oy, c)).start()
        for k, (ox, oy) in enumerate(others):
            conv(k, 2 * ox + oy, (x, y, c)).wait_recv()
            conv(k, me, (ox, oy, c)).wait_send()
        mine.wait()

    dma = pltpu.SemaphoreType.DMA
    return _pcall(
        body, name="gather_conv", in_specs=[ANY], out_specs=ANY,
        out_shape=jax.ShapeDtypeStruct((CONV_ROWS, N_CHIPS * cc), F32), scratch_shapes=[dma((3,)), dma((3,)), dma((1,))],
    )(conv_shard)


HBM = pl.BlockSpec(memory_space=pltpu.HBM)
SEM = pl.BlockSpec(memory_space=pltpu.SEMAPHORE)
EFFECT = pltpu.SideEffectType.DATAFLOW_SIDE_EFFECTING
TOKEN = (8, LANES)


def _split_start(name, plan, n_copies, srcs, lands, after=None):
    ns, nl = len(srcs), len(lands)
    n_in = ns + nl + (after is not None)

    def body(*refs):
        outs = refs[n_in:]
        sends, _ = plan(refs[:ns], refs[ns:ns + nl], outs[0], outs[1])
        for cp in sends:
            cp.start()
        outs[-1][...] = jnp.zeros(TOKEN, F32)

    held = [pltpu.HBM(a.shape, a.dtype) for a in (*srcs, *lands)]
    dma = pltpu.SemaphoreType.DMA((n_copies,))
    ins = [pltpu.with_memory_space_constraint(a, pltpu.HBM) for a in (*srcs, *lands)]
    outs = _pcall(
        body, name=name, in_specs=[HBM] * (ns + nl) + ([] if after is None else [ANY]),
        out_specs=(SEM, SEM, *[HBM] * (ns + nl), pl.BlockSpec(memory_space=pltpu.VMEM)),
        out_shape=(dma, dma, *held, jax.ShapeDtypeStruct(TOKEN, F32)),
        input_output_aliases={i: 2 + i for i in range(ns + nl)},
        compiler_params=pltpu.CompilerParams(has_side_effects=EFFECT),
    )(*ins, *([] if after is None else [after]))
    return outs[0], outs[1], list(outs[2:2 + ns]), list(outs[2 + ns:2 + ns + nl]), outs[-1]


def _split_wait(name, plan, send_sems, recv_sems, srcs, lands, after):
    ns, nl = len(srcs), len(lands)

    def body(*refs):
        sends, recvs = plan(refs[:ns], refs[ns:ns + nl], refs[ns + nl], refs[ns + nl + 1])
        for cp in sends:
            cp.wait_send()
        for cp in recvs:
            cp.wait_recv()

    outs = _pcall(
        body, name=name, in_specs=[HBM] * (ns + nl) + [SEM, SEM, ANY], out_specs=[HBM] * (ns + nl),
        out_shape=[pltpu.HBM(a.shape, a.dtype) for a in (*srcs, *lands)],
        input_output_aliases={i: i for i in range(ns + nl)},
        compiler_params=pltpu.CompilerParams(has_side_effects=EFFECT),
    )(*srcs, *lands, send_sems, recv_sems, after)
    return list(outs[:ns]), list(outs[ns:])


def _gather_plan(dims):
    def plan(shard_refs, full_refs, ss, rs):
        x, y, c, me, others = _place()
        sends, recvs = [], []
        for wi, (kind, r, cw) in enumerate(dims):
            half = shard_refs[wi].at[pl.ds(pl.multiple_of(c * (r // 2), 16), r // 2), :]
            for k, (ox, oy) in enumerate(others):
                sem = 4 * wi + k
                sends.append(_remote(half, _piece(full_refs[wi], kind, r, cw, me, c), ss.at[sem], rs.at[sem], (ox, oy, c)))
                recvs.append(_remote(half, _piece(full_refs[wi], kind, r, cw, 2 * ox + oy, c), ss.at[sem], rs.at[sem], (x, y, c)))
            sem = 4 * wi + 3
            own = _remote(shard_refs[wi], _shard_of(full_refs[wi], kind, r, cw, me), ss.at[sem], rs.at[sem], (x, y, 1 - c))
            sends.append(own)
            recvs.append(own)
        return sends, recvs

    return plan


def _forward_plan(dims):
    def plan(_, full_refs, ss, rs):
        x, y, c, _, others = _place()
        sends, recvs = [], []
        for wi, (kind, r, cw) in enumerate(dims):
            for k, (ox, oy) in enumerate(others):
                sem = 3 * wi + k
                mine = _piece(full_refs[wi], kind, r, cw, 2 * ox + oy, c)
                theirs = _piece(full_refs[wi], kind, r, cw, 2 * ox + oy, 1 - c)
                sends.append(_remote(mine, mine, ss.at[sem], rs.at[sem], (x, y, 1 - c)))
                recvs.append(_remote(theirs, theirs, ss.at[sem], rs.at[sem], (x, y, 1 - c)))
        return sends, recvs

    return plan


def _rs_cores_plan(dims):
    def plan(g_refs, land_refs, ss, rs):
        x, y, c, _, _ = _place()
        sends, recvs = [], []
        for wi, dm in enumerate(dims):
            for chip in range(N_CHIPS):
                sem = N_CHIPS * wi + chip
                sends.append(_remote(_piece(g_refs[wi], *dm, chip, 1 - c), land_refs[wi].at[chip], ss.at[sem], rs.at[sem], (x, y, 1 - c)))
                recvs.append(_remote(_piece(g_refs[wi], *dm, chip, c), land_refs[wi].at[chip], ss.at[sem], rs.at[sem], (x, y, 1 - c)))
        return sends, recvs

    return plan


def _share_plan(nw):
    def plan(_, buf_refs, ss, rs):
        x, y, c, _, _ = _place()
        sends = [_remote(buf_refs[wi].at[c], buf_refs[wi].at[c], ss.at[wi], rs.at[wi], (x, y, 1 - c)) for wi in range(nw)]
        recvs = [_remote(buf_refs[wi].at[1 - c], buf_refs[wi].at[1 - c], ss.at[wi], rs.at[wi], (x, y, 1 - c)) for wi in range(nw)]
        return sends, recvs

    return plan


def _small_plan():
    def plan(_, buf_refs, ss, rs):
        x, y, c = lax.axis_index("x"), lax.axis_index("y"), lax.axis_index("c")
        buf = buf_refs[0]
        sends, recvs = [], []
        for rel in range(1, N_DEV):
            peer = (x ^ (rel >> 2 & 1), y ^ (rel >> 1 & 1), c ^ (rel & 1))
            sends.append(_remote(buf.at[0], buf.at[rel], ss.at[rel - 1], rs.at[rel - 1], peer))
            recvs.append(_remote(buf.at[0], buf.at[rel], ss.at[rel - 1], rs.at[rel - 1], peer))
        return sends, recvs

    return plan


def _sum_small(buf, me, name):
    _, rows, n = buf.shape

    def body(me_ref, b_ref, o_ref):
        tot = b_ref[me_ref[0]]
        for dev in range(1, N_DEV):
            tot = tot + b_ref[dev ^ me_ref[0]]
        o_ref[...] = tot

    return _pcall(
        body, name=name, out_shape=jax.ShapeDtypeStruct((rows, n), F32),
        grid_spec=pltpu.PrefetchScalarGridSpec(
            num_scalar_prefetch=1, grid=(1,), in_specs=[pl.BlockSpec((N_DEV, rows, n), lambda i, m: (0, 0, 0))],
            out_specs=pl.BlockSpec((rows, n), lambda i, m: (0, 0))),
    )(me, buf)


def _rs_chips_plan(nw):
    def plan(p_refs, land_refs, ss, rs):
        x, y, c, me, others = _place()
        sends, recvs = [], []
        for wi in range(nw):
            for k, (ox, oy) in enumerate(others):
                sem = 3 * wi + k
                sends.append(_remote(p_refs[wi].at[2 * ox + oy], land_refs[wi].at[k], ss.at[sem], rs.at[sem], (ox, oy, c)))
                recvs.append(_remote(p_refs[wi].at[me], land_refs[wi].at[k], ss.at[sem], rs.at[sem], (x, y, c)))
        return sends, recvs

    return plan


def _rows_per_block(n, c, limit_bytes=2 << 20):
    best = None
    for tm in range(16, n + 1, 16):
        if n % tm == 0 and tm * c * 4 <= limit_bytes:
            best = tm
    return best or n


def _sum_cores(grad, got, kind, place, name):
    _, hr, cw = got.shape
    tm = _rows_per_block(hr, cw)
    nb = hr // tm

    def body(place_ref, g_ref, t_ref, o_ref):
        o_ref[...] = (g_ref[...].astype(F32) + t_ref[...].astype(F32)).astype(o_ref.dtype)

    if kind == "col":
        g_spec = pl.BlockSpec((tm, cw), lambda j, i, pr: (pr[0] * nb + i, j))
    else:
        g_spec = pl.BlockSpec((tm, cw), lambda j, i, pr: ((2 * j + pr[0]) * nb + i, 0))
    blk = pl.BlockSpec((None, tm, cw), lambda j, i, pr: (j, i, 0))
    return _pcall(
        body, name=name, out_shape=jax.ShapeDtypeStruct(got.shape, BF16),
        grid_spec=pltpu.PrefetchScalarGridSpec(num_scalar_prefetch=1, grid=(N_CHIPS, nb), in_specs=[g_spec, blk], out_specs=blk),
        compiler_params=_params("parallel", "parallel"),
    )(place, grad, got)


def _sum_chips(parts, got, place, name):
    _, n, cw = got.shape
    tm = _rows_per_block(n, cw)

    def body(place_ref, p_ref, g_ref, o_ref):
        tot = p_ref[...].astype(F32)
        for k in range(3):
            tot = tot + g_ref[k].astype(F32)
        o_ref[...] = tot

    return _pcall(
        body, name=name, out_shape=jax.ShapeDtypeStruct((2, n, cw), F32),
        grid_spec=pltpu.PrefetchScalarGridSpec(
            num_scalar_prefetch=1, grid=(n // tm,),
            in_specs=[pl.BlockSpec((None, tm, cw), lambda i, pr: (pr[1], i, 0)), pl.BlockSpec((3, tm, cw), lambda i, pr: (0, i, 0))],
            out_specs=pl.BlockSpec((None, tm, cw), lambda i, pr: (pr[0], i, 0))),
        compiler_params=_params("parallel"),
    )(place, parts, got)


def _adamw(g, w, m, v, name):
    n, c = g.shape
    c1 = 1.0 - ADAM_B1 ** ADAM_STEP
    c2 = 1.0 - ADAM_B2 ** ADAM_STEP

    def fn(gb, wb, mb, vb):
        m_new = ADAM_B1 * mb + (1.0 - ADAM_B1) * gb
        v_new = ADAM_B2 * vb + (1.0 - ADAM_B2) * (gb * gb)
        delta = -ADAM_LR * ((m_new / c1) / (jnp.sqrt(v_new / c2) + ADAM_EPS) + ADAM_WD * wb)
        return gb, delta, m_new, v_new

    tm = _rows_per_block(n, c) if n % 16 == 0 else n
    return _rowcall(fn, [_whole(g), _whole(w), _whole(m), _whole(v)], [], [(c, F32)] * 4, tm=tm, name=name)


PACK_ROWS = 16


def _pack_rows(parts, width, name, after=None):
    assert sum(p.shape[0] for p in parts) <= PACK_ROWS

    def body(*refs):
        out_ref = refs[-1]
        out_ref[...] = jnp.zeros_like(out_ref)
        at = 0
        for r in refs[:len(parts)]:
            k, n = r.shape
            if n == width:
                out_ref[at:at + k, :] = r[...]
            else:
                out_ref[at:at + k, :] = jnp.broadcast_to(r[:, :1], (k, width))
            at += k

    vm = pl.BlockSpec(memory_space=pltpu.VMEM)
    return _pcall(body, name=name, in_specs=[vm] * len(parts) + ([] if after is None else [ANY]), out_specs=vm,
                  out_shape=jax.ShapeDtypeStruct((PACK_ROWS, width), F32))(*parts, *([] if after is None else [after]))


def _cast_shard(wm, name, after):
    n, c = wm.shape
    return _rowcall(lambda v: v, [_whole(wm)], [], [(c, BF16)], tm=_rows_per_block(n, c), name=name, after=after)[0]


GATHER_GROUPS = (
    ("w_ffn1_gu",), ("w_ffn1_down",), ("w_in",), ("w_conv_out", "w_attn_out", "w_o"), ("w_cq", "w_ckv", "w_co"),
    ("w_ffn2_gu", "w_ffn2_down"),
)
REDUCE_GROUPS = {
    "ffn2": ("w_ffn2_down", "w_ffn2_gu"),
    "cross": ("w_co", "w_cq", "w_ckv"),
    "mix": ("w_o", "w_conv_out", "w_attn_out", "w_in"),
    "ffn1_down": ("w_ffn1_down",),
    "ffn1": ("w_ffn1_gu",),
}
TAIL_STAGES = (("ffn2", "cross"), ("mix",), ("ffn1_down", "ffn1"))
KIND = dict(MATS)


def _step(x, mem, tgt, wts, m_in, v_in):
    d = x.shape[-1]
    cc = wts["conv_w"].shape[1]
    place = jnp.stack([lax.axis_index("c"), 2 * lax.axis_index("x") + lax.axis_index("y")]).astype(jnp.int32)
    dims = {n: (kind, *wts[n].shape) for n, kind in MATS}

    conv_full = _gather_conv(jnp.pad(wts["conv_w"], ((0, CONV_ROWS - CONV_K), (0, 0))))
    w = {n: wts[n].reshape(1, -1) for n in VECS + ("b_gate",)}
    w["conv_w"] = conv_full[:CONV_K]
    flying, token = {}, conv_full
    for names in GATHER_GROUPS:
        gd = [dims[n] for n in names]
        shards = [_cast_shard(wts[n], "cast_" + n, token) for n in names]
        lands = [lax.empty(_full_shape(*dm), BF16) for dm in gd]
        plan = _gather_plan(gd)
        ss, rs, srcs, lands, token = _split_start("gather_start_" + names[0], plan, 4 * len(names), shards, lands, token)
        flying.update({n: (names, plan, ss, rs, srcs, lands, gd) for n in names})

    passing = {}

    def prefetch(name, after):
        if name not in passing:
            names, plan, ss, rs, srcs, lands, gd = flying[name]
            _, lands = _split_wait("gather_wait_" + names[0], plan, ss, rs, srcs, lands, after)
            plan = _forward_plan(gd)
            ss, rs, _, lands, _ = _split_start("forward_start_" + names[0], plan, 3 * len(names), [], lands)
            passing.update({n: (names, plan, ss, rs, lands) for n in names})

    def fetch(name, after):
        prefetch(name, after)
        names, plan, ss, rs, lands = passing[name]
        _, lands = _split_wait("forward_wait_" + names[0], plan, ss, rs, [], lands, after)
        return dict(zip(names, lands))

    swapping, sent = {}, {}

    def emit(tag, g):
        if tag not in REDUCE_GROUPS:
            return None
        names = REDUCE_GROUPS[tag]
        gd = [dims[n] for n in names]
        lands = [lax.empty((N_CHIPS, r // 2, cw), BF16) for (_, r, cw) in gd]
        plan = _rs_cores_plan(gd)
        ss, rs, srcs, lands, tok = _split_start("rs_cores_start_" + tag, plan, N_CHIPS * len(names), [g[n] for n in names], lands)
        swapping[tag] = (plan, ss, rs, srcs, lands)
        return tok

    def tick(tag, after):
        if tag not in REDUCE_GROUPS:
            return None
        names = REDUCE_GROUPS[tag]
        plan, ss, rs, srcs, lands = swapping[tag]
        mine, got = _split_wait("rs_cores_wait_" + tag, plan, ss, rs, srcs, lands, after)
        parts = [_sum_cores(gm, t, KIND[n], place, "sum_cores_" + n) for n, gm, t in zip(names, mine, got)]
        lands = [lax.empty((3, *p.shape[1:]), BF16) for p in parts]
        plan = _rs_chips_plan(len(names))
        ss, rs, srcs, lands, tok = _split_start("rs_chips_start_" + tag, plan, 3 * len(names), parts, lands)
        sent[tag] = (plan, ss, rs, srcs, lands)
        return tok

    loss_lanes, dx, g, last = _local_step(x[0], mem[0], tgt[0], w, fetch, prefetch, emit, tick, token)

    rows = [g[n] for n in VECS] + [g["b_gate"][:, :d], g["b_gate"][:, d:], g["conv_w"], loss_lanes]
    packed = _pack_rows(rows, d, "pack_small", after=last)
    small = jnp.concatenate([packed[None], jnp.zeros((N_DEV - 1, *packed.shape), F32)], axis=0)
    small_plan = _small_plan()
    small_ss, small_rs, _, small, after = _split_start("small_start", small_plan, N_DEV - 1, [], [small])

    grads, out = {}, {}

    def update(n):
        shape = wts[n].shape
        as2d = (lambda a: a.reshape(1, -1)) if len(shape) == 1 else (lambda a: a)
        return [r.reshape(shape) for r in _adamw(grads[n], as2d(wts[n]), as2d(m_in[n]), as2d(v_in[n]), "adamw_" + n)]

    def finish(sharing, after):
        tag, names, plan, ss, rs, halves = sharing
        _, both = _split_wait("share_wait_" + tag, plan, ss, rs, [], halves, after)
        for n, b in zip(names, both):
            grads[n] = b.reshape(-1, b.shape[-1])
            out[n] = update(n)
        return out[names[-1]][1]

    sharing = None
    for stage in TAIL_STAGES:
        names, halves = [], []
        for tag in stage:
            plan, ss, rs, srcs, lands = sent[tag]
            parts, landed = _split_wait("rs_chips_wait_" + tag, plan, ss, rs, srcs, lands, after)
            halves += [_sum_chips(p, t, place, "sum_chips_" + n) for n, p, t in zip(REDUCE_GROUPS[tag], parts, landed)]
            names += REDUCE_GROUPS[tag]
        plan = _share_plan(len(names))
        ss, rs, _, halves, after = _split_start("share_start_" + stage[0], plan, len(names), [], halves)
        if sharing is not None:
            after = finish(sharing, after)
        sharing = (stage[0], names, plan, ss, rs, halves)
    after = finish(sharing, after)

    _, small = _split_wait("small_wait", small_plan, small_ss, small_rs, [], small, after)
    me = (4 * lax.axis_index("x") + 2 * lax.axis_index("y") + lax.axis_index("c")).astype(jnp.int32).reshape(1)
    red = _sum_small(small[0], me, "sum_small")
    grads.update({n: red[i:i + 1] for i, n in enumerate(VECS)})
    nv = len(VECS)
    grads["b_gate"] = jnp.concatenate([red[nv:nv + 1], red[nv + 1:nv + 2]], axis=1)
    chip = 2 * lax.axis_index("x") + lax.axis_index("y")
    grads["conv_w"] = lax.dynamic_slice_in_dim(red[nv + 2:nv + 2 + CONV_K], chip * cc, cc, axis=1)
    loss = red[nv + 2 + CONV_K, 0]
    out.update({n: update(n) for n in WEIGHTS if n not in KIND})
    return (loss, dx[None], *[out[n][0] for n in WEIGHTS], *[out[n][1] for n in WEIGHTS],
            *[out[n][2] for n in WEIGHTS], *[out[n][3] for n in WEIGHTS])


def kernel(x, mem, g_ffn1, w_ffn1_gu, w_ffn1_down, g_mix, w_in, b_gate, conv_w, w_conv_out, w_attn_out, w_o, g_cross, g_mem, w_cq, w_ckv, w_co, g_ffn2, w_ffn2_gu, w_ffn2_down, g_final, loss_target, m_g_ffn1, m_w_ffn1_gu, m_w_ffn1_down, m_g_mix, m_w_in, m_b_gate, m_conv_w, m_w_conv_out, m_w_attn_out, m_w_o, m_g_cross, m_g_mem, m_w_cq, m_w_ckv, m_w_co, m_g_ffn2, m_w_ffn2_gu, m_w_ffn2_down, m_g_final, v_g_ffn1, v_w_ffn1_gu, v_w_ffn1_down, v_g_mix, v_w_in, v_b_gate, v_conv_w, v_w_conv_out, v_w_attn_out, v_w_o, v_g_cross, v_g_mem, v_w_cq, v_w_ckv, v_w_co, v_g_ffn2, v_w_ffn2_gu, v_w_ffn2_down, v_g_final):
    given = dict(locals())
    wts = {n: given[n] for n in WEIGHTS}
    m_in = {n: given["m_" + n] for n in WEIGHTS}
    v_in = {n: given["v_" + n] for n in WEIGHTS}
    return _step(x, mem, loss_target, wts, m_in, v_in)
```

```python
import jax
import jax.numpy as jnp
from jax import lax
from jax.experimental import pallas as pl
from jax.experimental.pallas import tpu as pltpu

F32 = jnp.float32
BF16 = jnp.bfloat16
MESH = pl.DeviceIdType.MESH

V7X_VMEM_LIMIT_BYTES = 48 * 1024 * 1024
MM_VMEM_BUDGET_BYTES = 36 * 1024 * 1024
MM_WHOLE_K = 2816
LANES = 128
SB_HEAD_DIM = 128
X_HEADS = 4
CONV_K = 3
RMS_EPS = 1e-6
N_CHIPS = 4
N_DEV = 8
ADAM_LR, ADAM_B1, ADAM_B2, ADAM_EPS, ADAM_WD, ADAM_STEP = 0.001, 0.9, 0.999, 1e-08, 0.01, 10


ANY = pl.BlockSpec(memory_space=pl.ANY)


def _pcall(body, **kw):
    return pl.pallas_call(body, **kw)


def _params(*sem):
    return pltpu.CompilerParams(dimension_semantics=sem, vmem_limit_bytes=V7X_VMEM_LIMIT_BYTES)


def _pick(dim, cands):
    for c in cands:
        if dim % c == 0:
            return c
    return dim


def _dot(a, b, ca, cb):
    return lax.dot_general(a, b, (((ca,), (cb,)), ((), ())), preferred_element_type=F32)


def _mm(a, b, *, name, ta=False, tb=False, out_dtype=BF16, res=None, alpha=1.0, tm=None, tn=None, tk=None, after=None,
        a_halves=False, b_halves=False):
    assert not (a_halves and ta) and not (b_halves and tb)
    if a_halves:
        m, k = a.shape[1], 2 * a.shape[2]
    else:
        m, k = (a.shape[1], a.shape[0]) if ta else a.shape
    if b_halves:
        n = 2 * b.shape[2]
        assert k == b.shape[1]
    else:
        n = b.shape[0] if tb else b.shape[1]
        assert k == (b.shape[1] if tb else b.shape[0]), (a.shape, b.shape, ta, tb)
    if ta:
        tm = tm or _pick(m, (512, 256, 128))
        tn = tn or _pick(n, (1024, 512, 256, 128))
        tk = tk or (k if k <= MM_WHOLE_K else _pick(k, (1024, 512, 256, 128)))
    else:
        tk = tk or (k if k <= MM_WHOLE_K else _pick(k, (MM_WHOLE_K, 2048, 1024, 512, 256, 128)))
        tn = tn or _pick(n, (512, 1408, 256, 128) if tk == k else (1024, 512, 256, 128))
        per_row = 2 * (tk * a.dtype.itemsize + tn * (jnp.dtype(out_dtype).itemsize + (0 if res is None else res.dtype.itemsize)))
        per_row += 4 * tn if tk < k else 0
        rows = (MM_VMEM_BUDGET_BYTES - 2 * tk * tn * b.dtype.itemsize) // per_row
        tm = tm or next((c for c in (2048, 1024, 512, 256, 128) if m % c == 0 and c <= rows), m)
    if a_halves:
        tk = min(tk, k // 2) if (k // 2) % min(tk, k // 2) == 0 else _pick(k // 2, (1408, 1024, 512, 256, 128))
    if b_halves:
        tn = tn if (n // 2) % tn == 0 else _pick(n // 2, (1408, 1024, 512, 256, 128))
    nk = k // tk
    assert m % tm == 0 and n % tn == 0 and k % tk == 0
    a_spec = pl.BlockSpec((tk, tm), lambda i, j, kk: (kk, i)) if ta else pl.BlockSpec((tm, tk), lambda i, j, kk: (i, kk))
    b_spec = pl.BlockSpec((tn, tk), lambda i, j, kk: (j, kk)) if tb else pl.BlockSpec((tk, tn), lambda i, j, kk: (kk, j))
    if a_halves:
        per = (k // 2) // tk
        a_spec = pl.BlockSpec((None, tm, tk), lambda i, j, kk: (kk // per, i, kk % per))
    if b_halves:
        per_n = (n // 2) // tn
        b_spec = pl.BlockSpec((None, tk, tn), lambda i, j, kk: (j // per_n, kk, j % per_n))
    o_spec = pl.BlockSpec((tm, tn), lambda i, j, kk: (i, j))
    ca, cb = (0 if ta else 1), (1 if tb else 0)

    n_in = 2 + (res is not None) + (after is not None)

    def body(*refs):
        a_ref, b_ref = refs[:2]
        res_ref = refs[2] if res is not None else None
        o_ref = refs[n_in]
        scratch = refs[n_in + 1:]

        def finish(acc):
            val = acc if alpha == 1.0 else alpha * acc
            if res_ref is not None:
                val = res_ref[...].astype(F32) + val
            o_ref[...] = val.astype(o_ref.dtype)

        part = _dot(a_ref[...].astype(BF16), b_ref[...].astype(BF16), ca, cb)
        if nk == 1:
            finish(part)
        else:
            acc_ref = scratch[0]
            kk = pl.program_id(2)

            @pl.when(kk == 0)
            def _():
                acc_ref[...] = part

            @pl.when(kk > 0)
            def _():
                acc_ref[...] += part

            @pl.when(kk == nk - 1)
            def _():
                finish(acc_ref[...])

    ins = [a, b] + ([] if res is None else [res]) + ([] if after is None else [after])
    in_specs = [a_spec, b_spec] + ([] if res is None else [o_spec]) + ([] if after is None else [ANY])
    return _pcall(
        body, name=name, grid=(m // tm, n // tn, nk), in_specs=in_specs, out_specs=o_spec,
        out_shape=jax.ShapeDtypeStruct((m, n), out_dtype),
        scratch_shapes=[pltpu.VMEM((tm, tn), F32)] if nk > 1 else [],
        compiler_params=_params("parallel", "parallel", "arbitrary"),
    )(*ins)


def _rowcall(fn, rows, consts, outs, accs=(), *, tm, name, after=None):
    s = rows[0][0].shape[0]
    assert s % tm == 0
    n_read, n_out = len(rows) + len(consts), len(outs)
    n_in = n_read + (after is not None)

    def body(*refs):
        vals = fn(*[r[...] for r in refs[:n_read]])
        vals = vals if isinstance(vals, (tuple, list)) else (vals,)
        for o_ref, v in zip(refs[n_in:n_in + n_out], vals[:n_out]):
            o_ref[...] = v.astype(o_ref.dtype)
        if accs:
            first = pl.program_id(0) == 0
            for a_ref, v in zip(refs[n_in + n_out:], vals[n_out:]):
                tot = jnp.sum(v.astype(F32), axis=0, keepdims=True)

                @pl.when(first)
                def _(a_ref=a_ref, tot=tot):
                    a_ref[...] = tot

                @pl.when(jnp.logical_not(first))
                def _(a_ref=a_ref, tot=tot):
                    a_ref[...] += tot

    in_specs = [pl.BlockSpec((tm, w), lambda i, cb=cb: (i, cb)) for (_, cb, w) in rows]
    in_specs += [pl.BlockSpec(c.shape, lambda i: (0, 0)) for c in consts]
    in_specs += [] if after is None else [ANY]
    out_specs = [pl.BlockSpec((tm, w), lambda i: (i, 0)) for (w, _) in outs]
    out_specs += [pl.BlockSpec((1, w), lambda i: (0, 0)) for w in accs]
    out_shape = [jax.ShapeDtypeStruct((s, w), dt) for (w, dt) in outs]
    out_shape += [jax.ShapeDtypeStruct((1, w), F32) for w in accs]
    return _pcall(
        body, name=name, grid=(s // tm,), in_specs=in_specs, out_specs=out_specs, out_shape=out_shape,
        compiler_params=_params("arbitrary" if accs else "parallel"),
    )(*[r[0] for r in rows], *consts, *([] if after is None else [after]))


def _whole(a):
    return (a, 0, a.shape[1])


def _xhat(x):
    x = x.astype(F32)
    r = lax.rsqrt(jnp.mean(x * x, axis=-1, keepdims=True) + RMS_EPS)
    return x * r, r


def _rms_bwd(dy, x, g):
    xh, r = _xhat(x)
    dxh = dy.astype(F32) * g
    dx = r * (dxh - xh * jnp.mean(dxh * xh, axis=-1, keepdims=True))
    return dx, dy.astype(F32) * xh


def _sigmoid(x):
    return 1.0 / (1.0 + jnp.exp(-x))


def _rms_fwd(x, g, name, tm, after=None):
    d = x.shape[1]
    return _rowcall(lambda xb, gb: _xhat(xb)[0] * gb, [_whole(x)], [g], [(d, BF16)], tm=tm, name=name, after=after)[0]


def _silu_parts(gate):
    sg = _sigmoid(gate)
    return sg, gate * sg


def _ffn_up(n, w_gu, name):
    s, d = n.shape
    f = w_gu.shape[1] // 2
    tn = _pick(f, (1408, 1024, 512, 256, 128))
    tm = _pick(s, (1024, 512, 256, 128))
    nb = f // tn

    def body(n_ref, wg_ref, wu_ref, gu_ref, act_ref):
        nv = n_ref[...]
        gate = _dot(nv, wg_ref[...], 1, 0)
        up = _dot(nv, wu_ref[...], 1, 0)
        gu_ref[0] = gate.astype(gu_ref.dtype)
        gu_ref[1] = up.astype(gu_ref.dtype)
        act_ref[...] = (_silu_parts(gate)[1] * up).astype(act_ref.dtype)

    return _pcall(
        body, name=name, grid=(s // tm, nb),
        in_specs=[pl.BlockSpec((tm, d), lambda i, j: (i, 0)), pl.BlockSpec((d, tn), lambda i, j: (0, j)),
                  pl.BlockSpec((d, tn), lambda i, j: (0, nb + j))],
        out_specs=[pl.BlockSpec((2, tm, tn), lambda i, j: (0, i, j)), pl.BlockSpec((tm, tn), lambda i, j: (i, j))],
        out_shape=[jax.ShapeDtypeStruct((2, s, f), BF16), jax.ShapeDtypeStruct((s, f), BF16)],
        compiler_params=_params("parallel", "parallel"),
    )(n, w_gu, w_gu)


def _ffn_dgu(dhb, w_down, gu, name, after=None):
    s, d = dhb.shape
    f = w_down.shape[0]
    tn = _pick(f, (1408, 1024, 512, 256, 128))
    tm = _pick(s, (1024, 512, 256, 128))

    def body(dh_ref, w_ref, gu_ref, *rest):
        o_ref = rest[-1]
        dact = _dot(dh_ref[...], w_ref[...], 1, 1)
        gate, up = gu_ref[0].astype(F32), gu_ref[1].astype(F32)
        sg, silu = _silu_parts(gate)
        o_ref[0] = (dact * up * (sg + silu * (1.0 - sg))).astype(o_ref.dtype)
        o_ref[1] = (dact * silu).astype(o_ref.dtype)

    blk = pl.BlockSpec((2, tm, tn), lambda i, j: (0, i, j))
    return _pcall(
        body, name=name, grid=(s // tm, f // tn),
        in_specs=[pl.BlockSpec((tm, d), lambda i, j: (i, 0)), pl.BlockSpec((tn, d), lambda i, j: (j, 0)), blk]
        + ([] if after is None else [ANY]),
        out_specs=blk, out_shape=jax.ShapeDtypeStruct((2, s, f), BF16), compiler_params=_params("parallel", "parallel"),
    )(dhb, w_down, gu, *([] if after is None else [after]))


def _dgrad_norm(dy, wmat, dh, x, g, name, *, dy_halves=False, copy_scale=None, after=None):
    s, d = dh.shape
    k = wmat.shape[1]
    tk = k if k <= MM_WHOLE_K else _pick(k, (MM_WHOLE_K, 2048, 1024, 512, 256, 128))
    if dy_halves and (k // 2) % tk:
        tk = _pick(k // 2, (1408, 1024, 512, 256, 128))
    tm = _pick(s, (512, 256, 128))
    nk, per = k // tk, (k // 2) // tk if dy_halves else 0
    n_in = 5 + (after is not None)
    n_out = 2 + (copy_scale is not None)

    def body(*refs):
        dy_ref, w_ref, dh_ref, x_ref, g_ref = refs[:5]
        outs, scratch = refs[n_in:n_in + n_out], refs[n_in + n_out:]
        i, kk = pl.program_id(0), pl.program_id(1)
        part = _dot(dy_ref[...], w_ref[...], 1, 1)

        def finish(dn):
            dx, dg = _rms_bwd(dn, x_ref[...], g_ref[...])
            tot = dh_ref[...] + dx
            outs[0][...] = tot
            if copy_scale is not None:
                outs[1][...] = (copy_scale * tot).astype(outs[1].dtype)
            dg = jnp.sum(dg, axis=0, keepdims=True)

            @pl.when(i == 0)
            def _():
                outs[-1][...] = dg

            @pl.when(i > 0)
            def _():
                outs[-1][...] += dg

        if nk == 1:
            finish(part)
        else:
            acc_ref = scratch[0]

            @pl.when(kk == 0)
            def _():
                acc_ref[...] = part

            @pl.when(kk > 0)
            def _():
                acc_ref[...] += part

            @pl.when(kk == nk - 1)
            def _():
                finish(acc_ref[...])

    row = pl.BlockSpec((tm, d), lambda i, kk: (i, 0))
    dy_spec = pl.BlockSpec((None, tm, tk), lambda i, kk: (kk // per, i, kk % per)) if dy_halves else pl.BlockSpec((tm, tk), lambda i, kk: (i, kk))
    in_specs = [dy_spec, pl.BlockSpec((d, tk), lambda i, kk: (0, kk)), row, row, pl.BlockSpec((1, d), lambda i, kk: (0, 0))]
    out_specs = [row] * (n_out - 1) + [pl.BlockSpec((1, d), lambda i, kk: (0, 0))]
    out_shape = [jax.ShapeDtypeStruct((s, d), F32)] + ([] if copy_scale is None else [jax.ShapeDtypeStruct((s, d), BF16)])
    return _pcall(
        body, name=name, grid=(s // tm, nk), in_specs=in_specs + ([] if after is None else [ANY]), out_specs=out_specs,
        out_shape=out_shape + [jax.ShapeDtypeStruct((1, d), F32)], scratch_shapes=[pltpu.VMEM((tm, d), F32)] if nk > 1 else [],
        compiler_params=_params("arbitrary", "arbitrary"),
    )(dy, wmat, dh, x, g, *([] if after is None else [after]))


def _shift_down(p, k):
    if k == 0:
        return p
    rows = lax.broadcasted_iota(jnp.int32, p.shape, 0)
    return jnp.where(rows >= k, pltpu.roll(p, k, 0), 0.0)


def _shift_up(p, k):
    if k == 0:
        return p
    s = p.shape[0]
    rows = lax.broadcasted_iota(jnp.int32, p.shape, 0)
    return jnp.where(rows < s - k, pltpu.roll(p, s - k, 0), 0.0)


def _conv_fwd(proj, conv_w, d, tc, name):
    s = proj.shape[0]
    nb = d // tc

    def body(cb_ref, cc_ref, cx_ref, w_ref, y_ref):
        p = cc_ref[...].astype(F32) * cx_ref[...].astype(F32)
        w = w_ref[...]
        acc = p * w[CONV_K - 1:CONV_K, :]
        for k in range(1, CONV_K):
            acc = acc + _shift_down(p, k) * w[CONV_K - 1 - k:CONV_K - k, :]
        y_ref[...] = (cb_ref[...].astype(F32) * acc).astype(y_ref.dtype)

    col = lambda off: pl.BlockSpec((s, tc), lambda j: (0, off * nb + j))
    return _pcall(
        body, name=name, grid=(nb,), in_specs=[col(0), col(1), col(2), pl.BlockSpec((CONV_K, tc), lambda j: (0, j))],
        out_specs=pl.BlockSpec((s, tc), lambda j: (0, j)), out_shape=jax.ShapeDtypeStruct((s, d), BF16),
        compiler_params=_params("parallel"),
    )(proj, proj, proj, conv_w)


def _conv_bwd(dy, proj, conv_w, d, tc, name):
    s = proj.shape[0]
    nb = d // tc

    def body(dy_ref, cb_ref, cc_ref, cx_ref, w_ref, dcb_ref, dcc_ref, dcx_ref, dw_ref):
        cc, cx = cc_ref[...].astype(F32), cx_ref[...].astype(F32)
        p = cc * cx
        w = w_ref[...]
        dyv = dy_ref[...].astype(F32)
        shifted = [_shift_down(p, CONV_K - 1 - k) for k in range(CONV_K)]
        conv = shifted[0] * w[0:1, :]
        for k in range(1, CONV_K):
            conv = conv + shifted[k] * w[k:k + 1, :]
        dcb_ref[...] = (dyv * conv).astype(dcb_ref.dtype)
        ds = dyv * cb_ref[...].astype(F32)
        dp = ds * w[CONV_K - 1:CONV_K, :]
        for k in range(1, CONV_K):
            dp = dp + _shift_up(ds, k) * w[CONV_K - 1 - k:CONV_K - k, :]
        dcc_ref[...] = (dp * cx).astype(dcc_ref.dtype)
        dcx_ref[...] = (dp * cc).astype(dcx_ref.dtype)
        for k in range(CONV_K):
            dw_ref[k:k + 1, :] = jnp.sum(ds * shifted[k], axis=0, keepdims=True)

    col = lambda off: pl.BlockSpec((s, tc), lambda j: (0, off * nb + j))
    blk = pl.BlockSpec((s, tc), lambda j: (0, j))
    wblk = pl.BlockSpec((CONV_K, tc), lambda j: (0, j))
    act = jax.ShapeDtypeStruct((s, d), BF16)
    return _pcall(
        body, name=name, grid=(nb,), in_specs=[blk, col(0), col(1), col(2), wblk],
        out_specs=[blk, blk, blk, wblk], out_shape=[act, act, act, jax.ShapeDtypeStruct((CONV_K, d), F32)],
        compiler_params=_params("parallel"),
    )(dy, proj, proj, proj, conv_w)


def _sb_tile(q, kj, scale, carry, tri, mask):
    z = _dot(q, kj, 1, 1) * scale
    lsz = jnp.minimum(z, 0.0) - jnp.log(1.0 + jnp.exp(-jnp.abs(z)))
    l1m = lsz - z
    if mask is not None:
        l1m = jnp.where(mask, l1m, 0.0)
    l1b = l1m.astype(BF16)
    a = jnp.exp(lsz + (carry + _dot(l1b, tri, 1, 0)))
    if mask is not None:
        a = jnp.where(mask, a, 0.0)
    return lsz, l1b, a.astype(BF16)


def _add_rows(x, upd, r0):
    return x + upd if r0 == 0 else jnp.concatenate([x[:r0], x[r0:] + upd], axis=0)


def _sb_masks(tq, tk):
    row = lax.broadcasted_iota(jnp.int32, (tq, tk), 0)
    col = lax.broadcasted_iota(jnp.int32, (tq, tk), 1)
    masks = [col + dj * tk < row for dj in range(tq // tk)]
    r2 = lax.broadcasted_iota(jnp.int32, (tk, tk), 0)
    c2 = lax.broadcasted_iota(jnp.int32, (tk, tk), 1)
    return masks, (r2 > c2).astype(BF16), (r2 < c2).astype(BF16)


def _sb_fwd(proj, heads, col0, tq, tk, name):
    s = proj.shape[0]
    dh = SB_HEAD_DIM
    nq, nd, nkt = s // tq, tq // tk, s // tk
    scale = dh ** -0.5

    def body(q_ref, k_ref, v_ref, o_ref, a_ref, b_ref):
        i = pl.program_id(1)
        q = q_ref[...]
        masks, tri_right, _ = _sb_masks(tq, tk)

        def tile(j, carry, acc, mask, r0=0):
            start = pl.multiple_of(j * tk, tk)
            kj = k_ref[pl.ds(start, tk), :]
            vj = v_ref[pl.ds(start, tk), :]
            lsz, l1b, ab = _sb_tile(q[r0:], kj, scale, carry[r0:], tri_right, None if mask is None else mask[r0:])
            a_ref[j, r0:, :] = ab
            b_ref[j, r0:, :] = jnp.exp(lsz).astype(b_ref.dtype)
            if r0:
                a_ref[j, :r0, :] = jnp.zeros((r0, tk), a_ref.dtype)
                b_ref[j, :r0, :] = jnp.zeros((r0, tk), b_ref.dtype)
            return (_add_rows(carry, jnp.sum(l1b.astype(F32), axis=1, keepdims=True), r0),
                    _add_rows(acc, _dot(ab, vj, 1, 0), r0))

        state = (jnp.zeros((tq, 1), F32), jnp.zeros((tq, dh), F32))
        for dj in reversed(range(nd)):
            state = tile(i * nd + dj, *state, masks[dj], dj * tk)
        def left_block(t, st):
            for dj in reversed(range(nd)):
                st = tile((i - 1 - t) * nd + dj, st[0], st[1], None)
            return st

        state = lax.fori_loop(0, i, left_block, state)
        o_ref[...] = state[1]

    qspec = pl.BlockSpec((tq, dh), lambda h, i: (i, col0[0] + h))
    kspec = pl.BlockSpec((s, dh), lambda h, i: (0, col0[1] + h))
    vspec = pl.BlockSpec((s, dh), lambda h, i: (0, col0[2] + h))
    saved = pl.BlockSpec((None, nkt, tq, tk), lambda h, i: (h, 0, i, 0))
    saved_shape = jax.ShapeDtypeStruct((heads, nkt, s, tk), BF16)
    return _pcall(
        body, name=name, grid=(heads, nq), in_specs=[qspec, kspec, vspec],
        out_specs=[pl.BlockSpec((tq, dh), lambda h, i: (i, h)), saved, saved],
        out_shape=[jax.ShapeDtypeStruct((s, heads * dh), F32), saved_shape, saved_shape],
        compiler_params=_params("parallel", "parallel"),
    )(proj, proj, proj)


def _sb_bwd(proj, o, a_all, beta_all, do, heads, col0, tq, tk, name):
    s = proj.shape[0]
    dh = SB_HEAD_DIM
    nq, nd, nkt = s // tq, tq // tk, s // tk
    scale = dh ** -0.5

    def body(q_ref, k_ref, v_ref, o_ref, a_ref, b_ref, do_ref, dq_ref, dk_ref, dv_ref, dk_acc, dv_acc):
        i = pl.program_id(1)

        @pl.when(i == 0)
        def _():
            dk_acc[...] = jnp.zeros_like(dk_acc)
            dv_acc[...] = jnp.zeros_like(dv_acc)

        q = q_ref[...]
        dob = do_ref[...].astype(BF16)
        delta = jnp.sum(dob.astype(F32) * o_ref[...], axis=1, keepdims=True)
        masks, _, tri_left = _sb_masks(tq, tk)

        def tile(j, carry_g, dq, mask):
            start = pl.multiple_of(j * tk, tk)
            kj = k_ref[pl.ds(start, tk), :]
            vj = v_ref[pl.ds(start, tk), :]
            ab = a_ref[j]
            g = _dot(dob, vj, 1, 1) * ab.astype(F32)
            carry_g = carry_g + jnp.sum(g, axis=1, keepdims=True)
            left = (delta - carry_g) + _dot(g.astype(BF16), tri_left, 1, 0)
            dz = g - b_ref[j].astype(F32) * (g + left)
            if mask is not None:
                dz = jnp.where(mask, dz, 0.0)
            dzb = dz.astype(BF16)
            dk_acc[pl.ds(start, tk), :] += _dot(dzb, q, 0, 0)
            dv_acc[pl.ds(start, tk), :] += _dot(ab, dob, 0, 0)
            return carry_g, dq + _dot(dzb, kj, 1, 0)

        state = (jnp.zeros((tq, 1), F32), jnp.zeros((tq, dh), F32))
        for dj in reversed(range(nd)):
            state = tile(i * nd + dj, *state, masks[dj])
        def left_block(t, st):
            for dj in reversed(range(nd)):
                st = tile((i - 1 - t) * nd + dj, st[0], st[1], None)
            return st

        state = lax.fori_loop(0, i, left_block, state)
        dq_ref[...] = (state[1] * scale).astype(dq_ref.dtype)

        @pl.when(i == nq - 1)
        def _():
            dk_ref[...] = (dk_acc[...] * scale).astype(dk_ref.dtype)
            dv_ref[...] = dv_acc[...].astype(dv_ref.dtype)

    qspec = pl.BlockSpec((tq, dh), lambda h, i: (i, col0[0] + h))
    kspec = pl.BlockSpec((s, dh), lambda h, i: (0, col0[1] + h))
    vspec = pl.BlockSpec((s, dh), lambda h, i: (0, col0[2] + h))
    blk = pl.BlockSpec((tq, dh), lambda h, i: (i, h))
    full = pl.BlockSpec((s, dh), lambda h, i: (0, h))
    saved = pl.BlockSpec((None, nkt, tq, tk), lambda h, i: (h, 0, i, 0))
    act = jax.ShapeDtypeStruct((s, heads * dh), BF16)
    return _pcall(
        body, name=name, grid=(heads, nq), in_specs=[qspec, kspec, vspec, blk, saved, saved, blk],
        out_specs=[blk, full, full], out_shape=[act, act, act],
        scratch_shapes=[pltpu.VMEM((s, dh), F32), pltpu.VMEM((s, dh), F32)],
        compiler_params=_params("parallel", "arbitrary"),
    )(proj, proj, proj, o, a_all, beta_all, do)


def _xattn_probs(q, k, scale):
    sc = _dot(q, k, 1, 1) * scale
    e = jnp.exp(sc - jnp.max(sc, axis=1, keepdims=True))
    return e / jnp.sum(e, axis=1, keepdims=True)


def _xattn_fwd(qc, kv, tq, name):
    s, d = qc.shape
    m = kv.shape[0]
    dh = d // X_HEADS
    scale = dh ** -0.5

    def body(q_ref, k_ref, v_ref, o_ref):
        p = _xattn_probs(q_ref[...], k_ref[...], scale)
        o_ref[...] = _dot(p.astype(BF16), v_ref[...], 1, 0).astype(o_ref.dtype)

    blk = pl.BlockSpec((tq, dh), lambda h, i: (i, h))
    return _pcall(
        body, name=name, grid=(X_HEADS, s // tq),
        in_specs=[blk, pl.BlockSpec((m, dh), lambda h, i: (0, h)), pl.BlockSpec((m, dh), lambda h, i: (0, X_HEADS + h))],
        out_specs=blk, out_shape=jax.ShapeDtypeStruct((s, d), BF16), compiler_params=_params("parallel", "parallel"),
    )(qc, kv, kv)


def _xattn_bwd(qc, kv, do, tq, name):
    s, d = qc.shape
    m = kv.shape[0]
    dh = d // X_HEADS
    scale = dh ** -0.5
    nq = s // tq

    def body(q_ref, k_ref, v_ref, do_ref, dq_ref, dk_ref, dv_ref, dk_acc, dv_acc):
        i = pl.program_id(1)
        q, k, v = q_ref[...], k_ref[...], v_ref[...]
        dob = do_ref[...].astype(BF16)
        p = _xattn_probs(q, k, scale)
        pb = p.astype(BF16)
        dp = _dot(dob, v, 1, 1)
        ds = pb.astype(F32) * (dp - jnp.sum(dp * pb.astype(F32), axis=1, keepdims=True))
        dsb = (ds * scale).astype(BF16)
        dq_ref[...] = _dot(dsb, k, 1, 0).astype(dq_ref.dtype)
        dk_part = _dot(dsb, q, 0, 0)
        dv_part = _dot(pb, dob, 0, 0)

        @pl.when(i == 0)
        def _():
            dk_acc[...] = dk_part
            dv_acc[...] = dv_part

        @pl.when(i > 0)
        def _():
            dk_acc[...] += dk_part
            dv_acc[...] += dv_part

        @pl.when(i == nq - 1)
        def _():
            dk_ref[...] = dk_acc[...].astype(dk_ref.dtype)
            dv_ref[...] = dv_acc[...].astype(dv_ref.dtype)

    blk = pl.BlockSpec((tq, dh), lambda h, i: (i, h))
    kblk = pl.BlockSpec((m, dh), lambda h, i: (0, h))
    return _pcall(
        body, name=name, grid=(X_HEADS, nq),
        in_specs=[blk, kblk, pl.BlockSpec((m, dh), lambda h, i: (0, X_HEADS + h)), blk],
        out_specs=[blk, kblk, kblk],
        out_shape=[jax.ShapeDtypeStruct((s, d), BF16), jax.ShapeDtypeStruct((m, d), BF16), jax.ShapeDtypeStruct((m, d), BF16)],
        scratch_shapes=[pltpu.VMEM((m, dh), F32), pltpu.VMEM((m, dh), F32)],
        compiler_params=_params("parallel", "arbitrary"),
    )(qc, kv, kv, do)


def _local_step(x, mem, tgt, w, fetch=None, prefetch=None, emit=None, tick=None, after=None):
    fetch = fetch or (lambda name, after: {})
    prefetch = prefetch or (lambda name, after: None)
    emit = emit or (lambda group, g: None)
    tick = tick or (lambda group, after: None)
    w = dict(w)
    s, d = x.shape
    heads = d // SB_HEAD_DIM
    tm = _pick(s, (512, 256, 128))
    tq = _pick(s, (1024, 512, 256, 128))
    sb_tq, sb_tk = _pick(s, (512, 256, 128)), _pick(s, (256, 128))
    tc = _pick(d, (256, 128))
    g = {}

    def wt(name, after):
        if name not in w:
            w.update(fetch(name, after))
        return w[name]

    def ffn_fwd(h, gname, wgu, wdown, tag, after=None):
        n = _rms_fwd(h, w[gname], tag + "_norm", tm, after=after)
        gu, act = _ffn_up(n, wt(wgu, n), tag + "_gu")
        prefetch(wdown, gu)
        return n, gu, act, _mm(act, wt(wdown, act), name=tag + "_down", out_dtype=F32, res=h, alpha=0.5)

    def ffn_bwd(dh, dhb, h, saved, gname, wgu, wdown, tag, copy_scale=None, after=None):
        n, gu, act = saved
        g[wdown] = _mm(act, dhb, ta=True, name=tag + "_dwdown", after=after)
        dgu = _ffn_dgu(dhb, w[wdown], gu, tag + "_dgu", after=emit(tag + "_down", g))
        g[wgu] = _mm(n, dgu, ta=True, b_halves=True, name=tag + "_dwgu", after=tick(tag + "_down", dgu))
        *dh_in, g[gname] = _dgrad_norm(dgu, w[wgu], dh, h, w[gname], tag + "_dn", dy_halves=True, copy_scale=copy_scale,
                                       after=emit(tag, g))
        return dh_in, tick(tag, dh_in[0])

    n1, gu1, act1, h1 = ffn_fwd(x, "g_ffn1", "w_ffn1_gu", "w_ffn1_down", "ffn1", after)
    prefetch("w_in", h1)
    u = _rms_fwd(h1, w["g_mix"], "mix_norm", tm)
    proj = _mm(u, wt("w_in", u), name="mix_in")
    prefetch("w_conv_out", proj)
    nd = d // SB_HEAD_DIM
    y_conv = _conv_fwd(proj, w["conv_w"], d, tc, "conv_fwd")
    sb_cols = (3 * nd, 4 * nd, 5 * nd)
    y_sb, sb_a, sb_beta = _sb_fwd(proj, heads, sb_cols, sb_tq, sb_tk, "sb_fwd")
    prefetch("w_cq", y_sb)
    a_conv = _mm(y_conv, wt("w_conv_out", y_conv), name="conv_out")
    a_sb = _mm(y_sb, wt("w_attn_out", y_sb), name="attn_out")
    b_conv, b_sb = w["b_gate"][:, :d], w["b_gate"][:, d:]

    def merge(ac, asb, gcp, gsp, bc, bs):
        gc = _sigmoid(gcp.astype(F32) + bc)
        gs = _sigmoid(gsp.astype(F32) + bs)
        return gc * ac.astype(F32) + gs * asb.astype(F32)

    merged = _rowcall(merge, [_whole(a_conv), _whole(a_sb), (proj, 6, d), (proj, 7, d)], [b_conv, b_sb], [(d, BF16)],
                      tm=tm, name="merge")[0]
    prefetch("w_ffn2_gu", merged)
    h2 = _mm(merged, wt("w_o", merged), name="mix_out", out_dtype=F32, res=h1)
    hn = _rms_fwd(h2, w["g_cross"], "cross_norm", tm)
    mn = _rms_fwd(mem, w["g_mem"], "mem_norm", _pick(mem.shape[0], (256, 128)))
    qc = _mm(hn, wt("w_cq", hn), name="cross_q")
    kv = _mm(mn, wt("w_ckv", mn), name="cross_kv")
    oc = _xattn_fwd(qc, kv, tq, "xattn_fwd")
    h3 = _mm(oc, wt("w_co", oc), name="cross_out", out_dtype=F32, res=h2)
    n2, gu2, act2, h4 = ffn_fwd(h3, "g_ffn2", "w_ffn2_gu", "w_ffn2_down", "ffn2")

    def head(hb, tb, gb):
        xh, r = _xhat(hb)
        err = xh * gb - tb
        dy = err * (1.0 / d)
        dxh = dy * gb
        dx = r * (dxh - xh * jnp.mean(dxh * xh, axis=-1, keepdims=True))
        row_loss = 0.5 * jnp.mean(err * err, axis=-1, keepdims=True)
        return dx, 0.5 * dx, dy * xh, jnp.broadcast_to(row_loss, (row_loss.shape[0], LANES))

    dh4, dh4b, g["g_final"], loss_lanes = _rowcall(head, [_whole(h4), _whole(tgt)], [w["g_final"]], [(d, F32), (d, BF16)],
                                                   [d, LANES], tm=tm, name="loss_head")

    (dh3, dh3b), tok = ffn_bwd(dh4, dh4b, h3, (n2, gu2, act2), "g_ffn2", "w_ffn2_gu", "w_ffn2_down", "ffn2", copy_scale=1.0)
    g["w_co"] = _mm(oc, dh3b, ta=True, name="cross_dwco", after=tok)
    doc = _mm(dh3b, w["w_co"], tb=True, name="cross_doc")
    dqc, dk, dv = _xattn_bwd(qc, kv, doc, tq, "xattn_bwd")
    dkv = jnp.concatenate([dk, dv], axis=1)
    g["w_cq"] = _mm(hn, dqc, ta=True, name="cross_dwcq")
    g["w_ckv"] = _mm(mn, dkv, ta=True, name="cross_dwckv")
    dmn = _mm(dkv, w["w_ckv"], tb=True, name="cross_dmn", out_dtype=F32)
    g["g_mem"] = _rowcall(lambda dy, xb: dy * _xhat(xb)[0], [_whole(dmn), _whole(mem)], [], [], [d],
                          tm=_pick(mem.shape[0], (256, 128)), name="mem_dnorm")[0]
    dh2, dh2b, g["g_cross"] = _dgrad_norm(dqc, w["w_cq"], dh3, h2, w["g_cross"], "cross_dhn", copy_scale=1.0, after=emit("cross", g))

    g["w_o"] = _mm(merged, dh2b, ta=True, name="mix_dwo", after=tick("cross", dh2))
    dmerged = _mm(dh2b, w["w_o"], tb=True, name="mix_dmerged")

    def merge_bwd(dm, ac, asb, gcp, gsp, bc, bs):
        dm, ac, asb = dm.astype(F32), ac.astype(F32), asb.astype(F32)
        gc = _sigmoid(gcp.astype(F32) + bc)
        gs = _sigmoid(gsp.astype(F32) + bs)
        dgc = dm * ac * gc * (1.0 - gc)
        dgs = dm * asb * gs * (1.0 - gs)
        return dm * gc, dm * gs, dgc, dgs, dgc, dgs

    da_conv, da_sb, dgc, dgs, db_conv, db_sb = _rowcall(
        merge_bwd, [_whole(dmerged), _whole(a_conv), _whole(a_sb), (proj, 6, d), (proj, 7, d)], [b_conv, b_sb],
        [(d, BF16)] * 4, [d, d], tm=tm, name="merge_bwd")
    g["b_gate"] = jnp.concatenate([db_conv, db_sb], axis=1)
    g["w_conv_out"] = _mm(y_conv, da_conv, ta=True, name="conv_dwout")
    g["w_attn_out"] = _mm(y_sb, da_sb, ta=True, name="attn_dwout")
    dy_conv = _mm(da_conv, w["w_conv_out"], tb=True, name="conv_dy")
    dy_sb = _mm(da_sb, w["w_attn_out"], tb=True, name="attn_dy")
    dcb, dcc, dcx, g["conv_w"] = _conv_bwd(dy_conv, proj, w["conv_w"], d, tc, "conv_bwd")
    dq, dk_sb, dv_sb = _sb_bwd(proj, y_sb, sb_a, sb_beta, dy_sb, heads, sb_cols, sb_tq, sb_tk, "sb_bwd")
    dproj = jnp.concatenate([dcb, dcc, dcx, dq, dk_sb, dv_sb, dgc, dgs], axis=1)
    g["w_in"] = _mm(u, dproj, ta=True, name="mix_dwin")
    dh1, dh1b, g["g_mix"] = _dgrad_norm(dproj, w["w_in"], dh2, h1, w["g_mix"], "mix_du", copy_scale=0.5, after=emit("mix", g))
    (dx,), tok = ffn_bwd(dh1, dh1b, x, (n1, gu1, act1), "g_ffn1", "w_ffn1_gu", "w_ffn1_down", "ffn1", after=tick("mix", dh1))
    return loss_lanes, dx, g, tok


MATS = (("w_ffn1_gu", "col"), ("w_ffn1_down", "row"), ("w_in", "col"), ("w_conv_out", "row"), ("w_attn_out", "row"),
        ("w_o", "row"), ("w_cq", "row"), ("w_ckv", "col"), ("w_co", "row"), ("w_ffn2_gu", "col"), ("w_ffn2_down", "row"))
VECS = ("g_ffn1", "g_mix", "g_cross", "g_mem", "g_ffn2", "g_final")
WEIGHTS = ("g_ffn1", "w_ffn1_gu", "w_ffn1_down", "g_mix", "w_in", "b_gate", "conv_w", "w_conv_out", "w_attn_out", "w_o",
           "g_cross", "g_mem", "w_cq", "w_ckv", "w_co", "g_ffn2", "w_ffn2_gu", "w_ffn2_down", "g_final")
CONV_ROWS = 8


def _full_shape(kind, r, c):
    return (r, N_CHIPS * c) if kind == "col" else (N_CHIPS * r, c)


def _piece(ref, kind, r, c, chip, half):
    hr = r // 2
    if kind == "col":
        return ref.at[pl.ds(pl.multiple_of(half * hr, 16), hr), pl.ds(pl.multiple_of(chip * c, LANES), c)]
    return ref.at[pl.ds(pl.multiple_of(chip * r + half * hr, 16), hr), :]


def _shard_of(ref, kind, r, c, chip):
    if kind == "col":
        return ref.at[:, pl.ds(pl.multiple_of(chip * c, LANES), c)]
    return ref.at[pl.ds(pl.multiple_of(chip * r, 16), r), :]


def _place():
    x, y, c = lax.axis_index("x"), lax.axis_index("y"), lax.axis_index("c")
    others = [(1 - x, y), (x, 1 - y), (1 - x, 1 - y)]
    return x, y, c, 2 * x + y, others


def _remote(src, dst, send_sem, recv_sem, to):
    return pltpu.make_async_remote_copy(src_ref=src, dst_ref=dst, send_sem=send_sem, recv_sem=recv_sem,
                                        device_id=to, device_id_type=MESH)


def _gather_conv(conv_shard):
    cc = conv_shard.shape[1]

    def body(conv_ref, conv_full, cs, cr, cl):
        x, y, c, me, others = _place()

        def cols(chip):
            return conv_full.at[:, pl.ds(pl.multiple_of(chip * cc, LANES), cc)]

        def conv(k, chip_from, to):
            return _remote(conv_ref, cols(chip_from), cs.at[k], cr.at[k], to)

        mine = pltpu.make_async_copy(conv_ref, cols(me), cl.at[0])
        mine.start()
        for k, (ox, oy) in enumerate(others):
            conv(k, me, (ox, oy, c)).start()
        for k, (ox, oy) in enumerate(others):
            conv(k, 2 * ox + oy, (x, y, c)).wait_recv()
            conv(k, me, (ox, oy, c)).wait_send()
        mine.wait()

    dma = pltpu.SemaphoreType.DMA
    return _pcall(
        body, name="gather_conv", in_specs=[ANY], out_specs=ANY,
        out_shape=jax.ShapeDtypeStruct((CONV_ROWS, N_CHIPS * cc), F32), scratch_shapes=[dma((3,)), dma((3,)), dma((1,))],
    )(conv_shard)


HBM = pl.BlockSpec(memory_space=pltpu.HBM)
SEM = pl.BlockSpec(memory_space=pltpu.SEMAPHORE)
EFFECT = pltpu.SideEffectType.DATAFLOW_SIDE_EFFECTING
TOKEN = (8, LANES)


def _split_start(name, plan, n_copies, srcs, lands, after=None):
    ns, nl = len(srcs), len(lands)
    n_in = ns + nl + (after is not None)

    def body(*refs):
        outs = refs[n_in:]
        sends, _ = plan(refs[:ns], refs[ns:ns + nl], outs[0], outs[1])
        for cp in sends:
            cp.start()
        outs[-1][...] = jnp.zeros(TOKEN, F32)

    held = [pltpu.HBM(a.shape, a.dtype) for a in (*srcs, *lands)]
    dma = pltpu.SemaphoreType.DMA((n_copies,))
    ins = [pltpu.with_memory_space_constraint(a, pltpu.HBM) for a in (*srcs, *lands)]
    outs = _pcall(
        body, name=name, in_specs=[HBM] * (ns + nl) + ([] if after is None else [ANY]),
        out_specs=(SEM, SEM, *[HBM] * (ns + nl), pl.BlockSpec(memory_space=pltpu.VMEM)),
        out_shape=(dma, dma, *held, jax.ShapeDtypeStruct(TOKEN, F32)),
        input_output_aliases={i: 2 + i for i in range(ns + nl)},
        compiler_params=pltpu.CompilerParams(has_side_effects=EFFECT),
    )(*ins, *([] if after is None else [after]))
    return outs[0], outs[1], list(outs[2:2 + ns]), list(outs[2 + ns:2 + ns + nl]), outs[-1]


def _split_wait(name, plan, send_sems, recv_sems, srcs, lands, after):
    ns, nl = len(srcs), len(lands)

    def body(*refs):
        sends, recvs = plan(refs[:ns], refs[ns:ns + nl], refs[ns + nl], refs[ns + nl + 1])
        for cp in sends:
            cp.wait_send()
        for cp in recvs:
            cp.wait_recv()

    outs = _pcall(
        body, name=name, in_specs=[HBM] * (ns + nl) + [SEM, SEM, ANY], out_specs=[HBM] * (ns + nl),
        out_shape=[pltpu.HBM(a.shape, a.dtype) for a in (*srcs, *lands)],
        input_output_aliases={i: i for i in range(ns + nl)},
        compiler_params=pltpu.CompilerParams(has_side_effects=EFFECT),
    )(*srcs, *lands, send_sems, recv_sems, after)
    return list(outs[:ns]), list(outs[ns:])


def _gather_plan(dims):
    def plan(shard_refs, full_refs, ss, rs):
        x, y, c, me, others = _place()
        sends, recvs = [], []
        for wi, (kind, r, cw) in enumerate(dims):
            half = shard_refs[wi].at[pl.ds(pl.multiple_of(c * (r // 2), 16), r // 2), :]
            for k, (ox, oy) in enumerate(others):
                sem = 4 * wi + k
                sends.append(_remote(half, _piece(full_refs[wi], kind, r, cw, me, c), ss.at[sem], rs.at[sem], (ox, oy, c)))
                recvs.append(_remote(half, _piece(full_refs[wi], kind, r, cw, 2 * ox + oy, c), ss.at[sem], rs.at[sem], (x, y, c)))
            sem = 4 * wi + 3
            own = _remote(shard_refs[wi], _shard_of(full_refs[wi], kind, r, cw, me), ss.at[sem], rs.at[sem], (x, y, 1 - c))
            sends.append(own)
            recvs.append(own)
        return sends, recvs

    return plan


def _forward_plan(dims):
    def plan(_, full_refs, ss, rs):
        x, y, c, _, others = _place()
        sends, recvs = [], []
        for wi, (kind, r, cw) in enumerate(dims):
            for k, (ox, oy) in enumerate(others):
                sem = 3 * wi + k
                mine = _piece(full_refs[wi], kind, r, cw, 2 * ox + oy, c)
                theirs = _piece(full_refs[wi], kind, r, cw, 2 * ox + oy, 1 - c)
                sends.append(_remote(mine, mine, ss.at[sem], rs.at[sem], (x, y, 1 - c)))
                recvs.append(_remote(theirs, theirs, ss.at[sem], rs.at[sem], (x, y, 1 - c)))
        return sends, recvs

    return plan


def _rs_cores_plan(dims):
    def plan(g_refs, land_refs, ss, rs):
        x, y, c, _, _ = _place()
        sends, recvs = [], []
        for wi, dm in enumerate(dims):
            for chip in range(N_CHIPS):
                sem = N_CHIPS * wi + chip
                sends.append(_remote(_piece(g_refs[wi], *dm, chip, 1 - c), land_refs[wi].at[chip], ss.at[sem], rs.at[sem], (x, y, 1 - c)))
                recvs.append(_remote(_piece(g_refs[wi], *dm, chip, c), land_refs[wi].at[chip], ss.at[sem], rs.at[sem], (x, y, 1 - c)))
        return sends, recvs

    return plan


def _share_plan(nw):
    def plan(_, buf_refs, ss, rs):
        x, y, c, _, _ = _place()
        sends = [_remote(buf_refs[wi].at[c], buf_refs[wi].at[c], ss.at[wi], rs.at[wi], (x, y, 1 - c)) for wi in range(nw)]
        recvs = [_remote(buf_refs[wi].at[1 - c], buf_refs[wi].at[1 - c], ss.at[wi], rs.at[wi], (x, y, 1 - c)) for wi in range(nw)]
        return sends, recvs

    return plan


def _small_plan():
    def plan(_, buf_refs, ss, rs):
        x, y, c = lax.axis_index("x"), lax.axis_index("y"), lax.axis_index("c")
        buf = buf_refs[0]
        sends, recvs = [], []
        for rel in range(1, N_DEV):
            peer = (x ^ (rel >> 2 & 1), y ^ (rel >> 1 & 1), c ^ (rel & 1))
            sends.append(_remote(buf.at[0], buf.at[rel], ss.at[rel - 1], rs.at[rel - 1], peer))
            recvs.append(_remote(buf.at[0], buf.at[rel], ss.at[rel - 1], rs.at[rel - 1], peer))
        return sends, recvs

    return plan


def _sum_small(buf, me, name):
    _, rows, n = buf.shape

    def body(me_ref, b_ref, o_ref):
        tot = b_ref[me_ref[0]]
        for dev in range(1, N_DEV):
            tot = tot + b_ref[dev ^ me_ref[0]]
        o_ref[...] = tot

    return _pcall(
        body, name=name, out_shape=jax.ShapeDtypeStruct((rows, n), F32),
        grid_spec=pltpu.PrefetchScalarGridSpec(
            num_scalar_prefetch=1, grid=(1,), in_specs=[pl.BlockSpec((N_DEV, rows, n), lambda i, m: (0, 0, 0))],
            out_specs=pl.BlockSpec((rows, n), lambda i, m: (0, 0))),
    )(me, buf)


def _rs_chips_plan(nw):
    def plan(p_refs, land_refs, ss, rs):
        x, y, c, me, others = _place()
        sends, recvs = [], []
        for wi in range(nw):
            for k, (ox, oy) in enumerate(others):
                sem = 3 * wi + k
                sends.append(_remote(p_refs[wi].at[2 * ox + oy], land_refs[wi].at[k], ss.at[sem], rs.at[sem], (ox, oy, c)))
                recvs.append(_remote(p_refs[wi].at[me], land_refs[wi].at[k], ss.at[sem], rs.at[sem], (x, y, c)))
        return sends, recvs

    return plan


def _rows_per_block(n, c, limit_bytes=2 << 20):
    best = None
    for tm in range(16, n + 1, 16):
        if n % tm == 0 and tm * c * 4 <= limit_bytes:
            best = tm
    return best or n


def _sum_cores(grad, got, kind, place, name):
    _, hr, cw = got.shape
    tm = _rows_per_block(hr, cw)
    nb = hr // tm

    def body(place_ref, g_ref, t_ref, o_ref):
        o_ref[...] = (g_ref[...].astype(F32) + t_ref[...].astype(F32)).astype(o_ref.dtype)

    if kind == "col":
        g_spec = pl.BlockSpec((tm, cw), lambda j, i, pr: (pr[0] * nb + i, j))
    else:
        g_spec = pl.BlockSpec((tm, cw), lambda j, i, pr: ((2 * j + pr[0]) * nb + i, 0))
    blk = pl.BlockSpec((None, tm, cw), lambda j, i, pr: (j, i, 0))
    return _pcall(
        body, name=name, out_shape=jax.ShapeDtypeStruct(got.shape, BF16),
        grid_spec=pltpu.PrefetchScalarGridSpec(num_scalar_prefetch=1, grid=(N_CHIPS, nb), in_specs=[g_spec, blk], out_specs=blk),
        compiler_params=_params("parallel", "parallel"),
    )(place, grad, got)


def _sum_chips(parts, got, place, name):
    _, n, cw = got.shape
    tm = _rows_per_block(n, cw)

    def body(place_ref, p_ref, g_ref, o_ref):
        tot = p_ref[...].astype(F32)
        for k in range(3):
            tot = tot + g_ref[k].astype(F32)
        o_ref[...] = tot

    return _pcall(
        body, name=name, out_shape=jax.ShapeDtypeStruct((2, n, cw), F32),
        grid_spec=pltpu.PrefetchScalarGridSpec(
            num_scalar_prefetch=1, grid=(n // tm,),
            in_specs=[pl.BlockSpec((None, tm, cw), lambda i, pr: (pr[1], i, 0)), pl.BlockSpec((3, tm, cw), lambda i, pr: (0, i, 0))],
            out_specs=pl.BlockSpec((None, tm, cw), lambda i, pr: (pr[0], i, 0))),
        compiler_params=_params("parallel"),
    )(place, parts, got)


def _adamw(g, w, m, v, name):
    n, c = g.shape
    c1 = 1.0 - ADAM_B1 ** ADAM_STEP
    c2 = 1.0 - ADAM_B2 ** ADAM_STEP

    def fn(gb, wb, mb, vb):
        m_new = ADAM_B1 * mb + (1.0 - ADAM_B1) * gb
        v_new = ADAM_B2 * vb + (1.0 - ADAM_B2) * (gb * gb)
        delta = -ADAM_LR * ((m_new / c1) / (jnp.sqrt(v_new / c2) + ADAM_EPS) + ADAM_WD * wb)
        return gb, delta, m_new, v_new

    tm = _rows_per_block(n, c) if n % 16 == 0 else n
    return _rowcall(fn, [_whole(g), _whole(w), _whole(m), _whole(v)], [], [(c, F32)] * 4, tm=tm, name=name)


PACK_ROWS = 16


def _pack_rows(parts, width, name, after=None):
    assert sum(p.shape[0] for p in parts) <= PACK_ROWS

    def body(*refs):
        out_ref = refs[-1]
        out_ref[...] = jnp.zeros_like(out_ref)
        at = 0
        for r in refs[:len(parts)]:
            k, n = r.shape
            if n == width:
                out_ref[at:at + k, :] = r[...]
            else:
                out_ref[at:at + k, :] = jnp.broadcast_to(r[:, :1], (k, width))
            at += k

    vm = pl.BlockSpec(memory_space=pltpu.VMEM)
    return _pcall(body, name=name, in_specs=[vm] * len(parts) + ([] if after is None else [ANY]), out_specs=vm,
                  out_shape=jax.ShapeDtypeStruct((PACK_ROWS, width), F32))(*parts, *([] if after is None else [after]))


def _cast_shard(wm, name, after):
    n, c = wm.shape
    return _rowcall(lambda v: v, [_whole(wm)], [], [(c, BF16)], tm=_rows_per_block(n, c), name=name, after=after)[0]


GATHER_GROUPS = (
    ("w_ffn1_gu",), ("w_ffn1_down",), ("w_in",), ("w_conv_out", "w_attn_out", "w_o"), ("w_cq", "w_ckv", "w_co"),
    ("w_ffn2_gu", "w_ffn2_down"),
)
REDUCE_GROUPS = {
    "ffn2": ("w_ffn2_down", "w_ffn2_gu"),
    "cross": ("w_co", "w_cq", "w_ckv"),
    "mix": ("w_o", "w_conv_out", "w_attn_out", "w_in"),
    "ffn1_down": ("w_ffn1_down",),
    "ffn1": ("w_ffn1_gu",),
}
TAIL_STAGES = (("ffn2", "cross"), ("mix",), ("ffn1_down", "ffn1"))
KIND = dict(MATS)


def _step(x, mem, tgt, wts, m_in, v_in):
    d = x.shape[-1]
    cc = wts["conv_w"].shape[1]
    place = jnp.stack([lax.axis_index("c"), 2 * lax.axis_index("x") + lax.axis_index("y")]).astype(jnp.int32)
    dims = {n: (kind, *wts[n].shape) for n, kind in MATS}

    conv_full = _gather_conv(jnp.pad(wts["conv_w"], ((0, CONV_ROWS - CONV_K), (0, 0))))
    w = {n: wts[n].reshape(1, -1) for n in VECS + ("b_gate",)}
    w["conv_w"] = conv_full[:CONV_K]
    flying, token = {}, conv_full
    for names in GATHER_GROUPS:
        gd = [dims[n] for n in names]
        shards = [_cast_shard(wts[n], "cast_" + n, token) for n in names]
        lands = [lax.empty(_full_shape(*dm), BF16) for dm in gd]
        plan = _gather_plan(gd)
        ss, rs, srcs, lands, token = _split_start("gather_start_" + names[0], plan, 4 * len(names), shards, lands, token)
        flying.update({n: (names, plan, ss, rs, srcs, lands, gd) for n in names})

    passing = {}

    def prefetch(name, after):
        if name not in passing:
            names, plan, ss, rs, srcs, lands, gd = flying[name]
            _, lands = _split_wait("gather_wait_" + names[0], plan, ss, rs, srcs, lands, after)
            plan = _forward_plan(gd)
            ss, rs, _, lands, _ = _split_start("forward_start_" + names[0], plan, 3 * len(names), [], lands)
            passing.update({n: (names, plan, ss, rs, lands) for n in names})

    def fetch(name, after):
        prefetch(name, after)
        names, plan, ss, rs, lands = passing[name]
        _, lands = _split_wait("forward_wait_" + names[0], plan, ss, rs, [], lands, after)
        return dict(zip(names, lands))

    swapping, sent = {}, {}

    def emit(tag, g):
        if tag not in REDUCE_GROUPS:
            return None
        names = REDUCE_GROUPS[tag]
        gd = [dims[n] for n in names]
        lands = [lax.empty((N_CHIPS, r // 2, cw), BF16) for (_, r, cw) in gd]
        plan = _rs_cores_plan(gd)
        ss, rs, srcs, lands, tok = _split_start("rs_cores_start_" + tag, plan, N_CHIPS * len(names), [g[n] for n in names], lands)
        swapping[tag] = (plan, ss, rs, srcs, lands)
        return tok

    def tick(tag, after):
        if tag not in REDUCE_GROUPS:
            return None
        names = REDUCE_GROUPS[tag]
        plan, ss, rs, srcs, lands = swapping[tag]
        mine, got = _split_wait("rs_cores_wait_" + tag, plan, ss, rs, srcs, lands, after)
        parts = [_sum_cores(gm, t, KIND[n], place, "sum_cores_" + n) for n, gm, t in zip(names, mine, got)]
        lands = [lax.empty((3, *p.shape[1:]), BF16) for p in parts]
        plan = _rs_chips_plan(len(names))
        ss, rs, srcs, lands, tok = _split_start("rs_chips_start_" + tag, plan, 3 * len(names), parts, lands)
        sent[tag] = (plan, ss, rs, srcs, lands)
        return tok

    loss_lanes, dx, g, last = _local_step(x[0], mem[0], tgt[0], w, fetch, prefetch, emit, tick, token)

    rows = [g[n] for n in VECS] + [g["b_gate"][:, :d], g["b_gate"][:, d:], g["conv_w"], loss_lanes]
    packed = _pack_rows(rows, d, "pack_small", after=last)
    small = jnp.concatenate([packed[None], jnp.zeros((N_DEV - 1, *packed.shape), F32)], axis=0)
    small_plan = _small_plan()
    small_ss, small_rs, _, small, after = _split_start("small_start", small_plan, N_DEV - 1, [], [small])

    grads, out = {}, {}

    def update(n):
        shape = wts[n].shape
        as2d = (lambda a: a.reshape(1, -1)) if len(shape) == 1 else (lambda a: a)
        return [r.reshape(shape) for r in _adamw(grads[n], as2d(wts[n]), as2d(m_in[n]), as2d(v_in[n]), "adamw_" + n)]

    def finish(sharing, after):
        tag, names, plan, ss, rs, halves = sharing
        _, both = _split_wait("share_wait_" + tag, plan, ss, rs, [], halves, after)
        for n, b in zip(names, both):
            grads[n] = b.reshape(-1, b.shape[-1])
            out[n] = update(n)
        return out[names[-1]][1]

    sharing = None
    for stage in TAIL_STAGES:
        names, halves = [], []
        for tag in stage:
            plan, ss, rs, srcs, lands = sent[tag]
            parts, landed = _split_wait("rs_chips_wait_" + tag, plan, ss, rs, srcs, lands, after)
            halves += [_sum_chips(p, t, place, "sum_chips_" + n) for n, p, t in zip(REDUCE_GROUPS[tag], parts, landed)]
            names += REDUCE_GROUPS[tag]
        plan = _share_plan(len(names))
        ss, rs, _, halves, after = _split_start("share_start_" + stage[0], plan, len(names), [], halves)
        if sharing is not None:
            after = finish(sharing, after)
        sharing = (stage[0], names, plan, ss, rs, halves)
    after = finish(sharing, after)

    _, small = _split_wait("small_wait", small_plan, small_ss, small_rs, [], small, after)
    me = (4 * lax.axis_index("x") + 2 * lax.axis_index("y") + lax.axis_index("c")).astype(jnp.int32).reshape(1)
    red = _sum_small(small[0], me, "sum_small")
    grads.update({n: red[i:i + 1] for i, n in enumerate(VECS)})
    nv = len(VECS)
    grads["b_gate"] = jnp.concatenate([red[nv:nv + 1], red[nv + 1:nv + 2]], axis=1)
    chip = 2 * lax.axis_index("x") + lax.axis_index("y")
    grads["conv_w"] = lax.dynamic_slice_in_dim(red[nv + 2:nv + 2 + CONV_K], chip * cc, cc, axis=1)
    loss = red[nv + 2 + CONV_K, 0]
    out.update({n: update(n) for n in WEIGHTS if n not in KIND})
    return (loss, dx[None], *[out[n][0] for n in WEIGHTS], *[out[n][1] for n in WEIGHTS],
            *[out[n][2] for n in WEIGHTS], *[out[n][3] for n in WEIGHTS])


def kernel(x, mem, g_ffn1, w_ffn1_gu, w_ffn1_down, g_mix, w_in, b_gate, conv_w, w_conv_out, w_attn_out, w_o, g_cross, g_mem, w_cq, w_ckv, w_co, g_ffn2, w_ffn2_gu, w_ffn2_down, g_final, loss_target, m_g_ffn1, m_w_ffn1_gu, m_w_ffn1_down, m_g_mix, m_w_in, m_b_gate, m_conv_w, m_w_conv_out, m_w_attn_out, m_w_o, m_g_cross, m_g_mem, m_w_cq, m_w_ckv, m_w_co, m_g_ffn2, m_w_ffn2_gu, m_w_ffn2_down, m_g_final, v_g_ffn1, v_w_ffn1_gu, v_w_ffn1_down, v_g_mix, v_w_in, v_b_gate, v_conv_w, v_w_conv_out, v_w_attn_out, v_w_o, v_g_cross, v_g_mem, v_w_cq, v_w_ckv, v_w_co, v_g_ffn2, v_w_ffn2_gu, v_w_ffn2_down, v_g_final):
    given = dict(locals())
    wts = {n: given[n] for n in WEIGHTS}
    m_in = {n: given["m_" + n] for n in WEIGHTS}
    v_in = {n: given["v_" + n] for n in WEIGHTS}
    return _step(x, mem, loss_target, wts, m_in, v_in)
```

```python
import jax
import jax.numpy as jnp
from jax import lax
from jax.experimental import pallas as pl
from jax.experimental.pallas import tpu as pltpu

F32 = jnp.float32
BF16 = jnp.bfloat16
MESH = pl.DeviceIdType.MESH

V7X_VMEM_LIMIT_BYTES = 48 * 1024 * 1024
MM_VMEM_BUDGET_BYTES = 36 * 1024 * 1024
MM_WHOLE_K = 2816
MM_MIN_STEPS = 4
LANES = 128
SB_HEAD_DIM = 128
X_HEADS = 4
CONV_K = 3
RMS_EPS = 1e-6
N_CHIPS = 4
N_DEV = 8
ADAM_LR, ADAM_B1, ADAM_B2, ADAM_EPS, ADAM_WD, ADAM_STEP = 0.001, 0.9, 0.999, 1e-08, 0.01, 10


ANY = pl.BlockSpec(memory_space=pl.ANY)


def _pcall(body, **kw):
    return pl.pallas_call(body, **kw)


def _params(*sem):
    return pltpu.CompilerParams(dimension_semantics=sem, vmem_limit_bytes=V7X_VMEM_LIMIT_BYTES)


def _pick(dim, cands):
    for c in cands:
        if dim % c == 0:
            return c
    return dim


def _dot(a, b, ca, cb):
    return lax.dot_general(a, b, (((ca,), (cb,)), ((), ())), preferred_element_type=F32)


def _mm(a, b, *, name, ta=False, tb=False, out_dtype=BF16, res=None, alpha=1.0, tm=None, tn=None, tk=None, after=None,
        a_halves=False, b_halves=False):
    assert not (a_halves and ta) and not (b_halves and tb)
    if a_halves:
        m, k = a.shape[1], 2 * a.shape[2]
    else:
        m, k = (a.shape[1], a.shape[0]) if ta else a.shape
    if b_halves:
        n = 2 * b.shape[2]
        assert k == b.shape[1]
    else:
        n = b.shape[0] if tb else b.shape[1]
        assert k == (b.shape[1] if tb else b.shape[0]), (a.shape, b.shape, ta, tb)
    if ta:
        tm = tm or _pick(m, (512, 256, 128))
        tn = tn or _pick(n, (1024, 512, 256, 128))
        tk = tk or (k if k <= MM_WHOLE_K else _pick(k, (1024, 512, 256, 128)))
    else:
        tk = tk or (k if k <= MM_WHOLE_K else _pick(k, (MM_WHOLE_K, 2048, 1024, 512, 256, 128)))
        tn = tn or _pick(n, (512, 1408, 256, 128) if tk == k else (1024, 512, 256, 128))
        per_row = 2 * (tk * a.dtype.itemsize + tn * (jnp.dtype(out_dtype).itemsize + (0 if res is None else res.dtype.itemsize)))
        per_row += 4 * tn if tk < k else 0
        rows = (MM_VMEM_BUDGET_BYTES - 2 * tk * tn * b.dtype.itemsize) // per_row
        tm = tm or next((c for c in (2048, 1024, 512, 256, 128) if m % c == 0 and c <= rows), m)
    while (m // tm) * (n // tn) * (k // tk) < MM_MIN_STEPS and (tn if ta else tm) > 512:
        tm, tn = (tm, tn // 2) if ta else (tm // 2, tn)
    if a_halves:
        tk = min(tk, k // 2) if (k // 2) % min(tk, k // 2) == 0 else _pick(k // 2, (1408, 1024, 512, 256, 128))
    if b_halves:
        tn = tn if (n // 2) % tn == 0 else _pick(n // 2, (1408, 1024, 512, 256, 128))
    nk = k // tk
    assert m % tm == 0 and n % tn == 0 and k % tk == 0
    a_spec = pl.BlockSpec((tk, tm), lambda i, j, kk: (kk, i)) if ta else pl.BlockSpec((tm, tk), lambda i, j, kk: (i, kk))
    b_spec = pl.BlockSpec((tn, tk), lambda i, j, kk: (j, kk)) if tb else pl.BlockSpec((tk, tn), lambda i, j, kk: (kk, j))
    if a_halves:
        per = (k // 2) // tk
        a_spec = pl.BlockSpec((None, tm, tk), lambda i, j, kk: (kk // per, i, kk % per))
    if b_halves:
        per_n = (n // 2) // tn
        b_spec = pl.BlockSpec((None, tk, tn), lambda i, j, kk: (j // per_n, kk, j % per_n))
    o_spec = pl.BlockSpec((tm, tn), lambda i, j, kk: (i, j))
    ca, cb = (0 if ta else 1), (1 if tb else 0)

    n_in = 2 + (res is not None) + (after is not None)

    def body(*refs):
        a_ref, b_ref = refs[:2]
        res_ref = refs[2] if res is not None else None
        o_ref = refs[n_in]
        scratch = refs[n_in + 1:]

        def finish(acc):
            val = acc if alpha == 1.0 else alpha * acc
            if res_ref is not None:
                val = res_ref[...].astype(F32) + val
            o_ref[...] = val.astype(o_ref.dtype)

        part = _dot(a_ref[...].astype(BF16), b_ref[...].astype(BF16), ca, cb)
        if nk == 1:
            finish(part)
        else:
            acc_ref = scratch[0]
            kk = pl.program_id(2)

            @pl.when(kk == 0)
            def _():
                acc_ref[...] = part

            @pl.when(kk > 0)
            def _():
                acc_ref[...] += part

            @pl.when(kk == nk - 1)
            def _():
                finish(acc_ref[...])

    ins = [a, b] + ([] if res is None else [res]) + ([] if after is None else [after])
    in_specs = [a_spec, b_spec] + ([] if res is None else [o_spec]) + ([] if after is None else [ANY])
    return _pcall(
        body, name=name, grid=(m // tm, n // tn, nk), in_specs=in_specs, out_specs=o_spec,
        out_shape=jax.ShapeDtypeStruct((m, n), out_dtype),
        scratch_shapes=[pltpu.VMEM((tm, tn), F32)] if nk > 1 else [],
        compiler_params=_params("parallel", "parallel", "arbitrary"),
    )(*ins)


def _rowcall(fn, rows, consts, outs, accs=(), *, tm, name, after=None):
    s = rows[0][0].shape[0]
    assert s % tm == 0
    n_read, n_out = len(rows) + len(consts), len(outs)
    n_in = n_read + (after is not None)

    def body(*refs):
        vals = fn(*[r[...] for r in refs[:n_read]])
        vals = vals if isinstance(vals, (tuple, list)) else (vals,)
        for o_ref, v in zip(refs[n_in:n_in + n_out], vals[:n_out]):
            o_ref[...] = v.astype(o_ref.dtype)
        if accs:
            first = pl.program_id(0) == 0
            for a_ref, v in zip(refs[n_in + n_out:], vals[n_out:]):
                tot = jnp.sum(v.astype(F32), axis=0, keepdims=True)

                @pl.when(first)
                def _(a_ref=a_ref, tot=tot):
                    a_ref[...] = tot

                @pl.when(jnp.logical_not(first))
                def _(a_ref=a_ref, tot=tot):
                    a_ref[...] += tot

    in_specs = [pl.BlockSpec((tm, w), lambda i, cb=cb: (i, cb)) for (_, cb, w) in rows]
    in_specs += [pl.BlockSpec(c.shape, lambda i: (0, 0)) for c in consts]
    in_specs += [] if after is None else [ANY]
    out_specs = [pl.BlockSpec((tm, w), lambda i: (i, 0)) for (w, _) in outs]
    out_specs += [pl.BlockSpec((1, w), lambda i: (0, 0)) for w in accs]
    out_shape = [jax.ShapeDtypeStruct((s, w), dt) for (w, dt) in outs]
    out_shape += [jax.ShapeDtypeStruct((1, w), F32) for w in accs]
    return _pcall(
        body, name=name, grid=(s // tm,), in_specs=in_specs, out_specs=out_specs, out_shape=out_shape,
        compiler_params=_params("arbitrary" if accs else "parallel"),
    )(*[r[0] for r in rows], *consts, *([] if after is None else [after]))


def _whole(a):
    return (a, 0, a.shape[1])


def _xhat(x):
    x = x.astype(F32)
    r = lax.rsqrt(jnp.mean(x * x, axis=-1, keepdims=True) + RMS_EPS)
    return x * r, r


def _rms_bwd(dy, x, g):
    xh, r = _xhat(x)
    dxh = dy.astype(F32) * g
    dx = r * (dxh - xh * jnp.mean(dxh * xh, axis=-1, keepdims=True))
    return dx, dy.astype(F32) * xh


def _sigmoid(x):
    return 1.0 / (1.0 + jnp.exp(-x))


def _rms_fwd(x, g, name, tm, after=None):
    d = x.shape[1]
    return _rowcall(lambda xb, gb: _xhat(xb)[0] * gb, [_whole(x)], [g], [(d, BF16)], tm=tm, name=name, after=after)[0]


def _silu_parts(gate):
    sg = _sigmoid(gate)
    return sg, gate * sg


def _ffn_up(n, w_gu, name):
    s, d = n.shape
    f = w_gu.shape[1] // 2
    tn = _pick(f, (1408, 1024, 512, 256, 128))
    tm = _pick(s, (1024, 512, 256, 128))
    nb = f // tn

    def body(n_ref, wg_ref, wu_ref, gu_ref, act_ref):
        nv = n_ref[...]
        gate = _dot(nv, wg_ref[...], 1, 0)
        up = _dot(nv, wu_ref[...], 1, 0)
        gu_ref[0] = gate.astype(gu_ref.dtype)
        gu_ref[1] = up.astype(gu_ref.dtype)
        act_ref[...] = (_silu_parts(gate)[1] * up).astype(act_ref.dtype)

    return _pcall(
        body, name=name, grid=(s // tm, nb),
        in_specs=[pl.BlockSpec((tm, d), lambda i, j: (i, 0)), pl.BlockSpec((d, tn), lambda i, j: (0, j)),
                  pl.BlockSpec((d, tn), lambda i, j: (0, nb + j))],
        out_specs=[pl.BlockSpec((2, tm, tn), lambda i, j: (0, i, j)), pl.BlockSpec((tm, tn), lambda i, j: (i, j))],
        out_shape=[jax.ShapeDtypeStruct((2, s, f), BF16), jax.ShapeDtypeStruct((s, f), BF16)],
        compiler_params=_params("parallel", "parallel"),
    )(n, w_gu, w_gu)


def _ffn_dgu(dhb, w_down, gu, name, after=None):
    s, d = dhb.shape
    f = w_down.shape[0]
    tn = _pick(f, (1408, 1024, 512, 256, 128))
    tm = _pick(s, (1024, 512, 256, 128))

    def body(dh_ref, w_ref, gu_ref, *rest):
        o_ref = rest[-1]
        dact = _dot(dh_ref[...], w_ref[...], 1, 1)
        gate, up = gu_ref[0].astype(F32), gu_ref[1].astype(F32)
        sg, silu = _silu_parts(gate)
        o_ref[0] = (dact * up * (sg + silu * (1.0 - sg))).astype(o_ref.dtype)
        o_ref[1] = (dact * silu).astype(o_ref.dtype)

    blk = pl.BlockSpec((2, tm, tn), lambda i, j: (0, i, j))
    return _pcall(
        body, name=name, grid=(s // tm, f // tn),
        in_specs=[pl.BlockSpec((tm, d), lambda i, j: (i, 0)), pl.BlockSpec((tn, d), lambda i, j: (j, 0)), blk]
        + ([] if after is None else [ANY]),
        out_specs=blk, out_shape=jax.ShapeDtypeStruct((2, s, f), BF16), compiler_params=_params("parallel", "parallel"),
    )(dhb, w_down, gu, *([] if after is None else [after]))


def _dgrad_norm(dy, wmat, dh, x, g, name, *, dy_halves=False, copy_scale=None, after=None):
    s, d = dh.shape
    k = wmat.shape[1]
    tk = k if k <= MM_WHOLE_K else _pick(k, (MM_WHOLE_K, 2048, 1024, 512, 256, 128))
    if dy_halves and (k // 2) % tk:
        tk = _pick(k // 2, (1408, 1024, 512, 256, 128))
    tm = _pick(s, (512, 256, 128))
    nk, per = k // tk, (k // 2) // tk if dy_halves else 0
    n_in = 5 + (after is not None)
    n_out = 2 + (copy_scale is not None)

    def body(*refs):
        dy_ref, w_ref, dh_ref, x_ref, g_ref = refs[:5]
        outs, scratch = refs[n_in:n_in + n_out], refs[n_in + n_out:]
        i, kk = pl.program_id(0), pl.program_id(1)
        part = _dot(dy_ref[...], w_ref[...], 1, 1)

        def finish(dn):
            dx, dg = _rms_bwd(dn, x_ref[...], g_ref[...])
            tot = dh_ref[...] + dx
            outs[0][...] = tot
            if copy_scale is not None:
                outs[1][...] = (copy_scale * tot).astype(outs[1].dtype)
            dg = jnp.sum(dg, axis=0, keepdims=True)

            @pl.when(i == 0)
            def _():
                outs[-1][...] = dg

            @pl.when(i > 0)
            def _():
                outs[-1][...] += dg

        if nk == 1:
            finish(part)
        else:
            acc_ref = scratch[0]

            @pl.when(kk == 0)
            def _():
                acc_ref[...] = part

            @pl.when(kk > 0)
            def _():
                acc_ref[...] += part

            @pl.when(kk == nk - 1)
            def _():
                finish(acc_ref[...])

    row = pl.BlockSpec((tm, d), lambda i, kk: (i, 0))
    dy_spec = pl.BlockSpec((None, tm, tk), lambda i, kk: (kk // per, i, kk % per)) if dy_halves else pl.BlockSpec((tm, tk), lambda i, kk: (i, kk))
    in_specs = [dy_spec, pl.BlockSpec((d, tk), lambda i, kk: (0, kk)), row, row, pl.BlockSpec((1, d), lambda i, kk: (0, 0))]
    out_specs = [row] * (n_out - 1) + [pl.BlockSpec((1, d), lambda i, kk: (0, 0))]
    out_shape = [jax.ShapeDtypeStruct((s, d), F32)] + ([] if copy_scale is None else [jax.ShapeDtypeStruct((s, d), BF16)])
    return _pcall(
        body, name=name, grid=(s // tm, nk), in_specs=in_specs + ([] if after is None else [ANY]), out_specs=out_specs,
        out_shape=out_shape + [jax.ShapeDtypeStruct((1, d), F32)], scratch_shapes=[pltpu.VMEM((tm, d), F32)] if nk > 1 else [],
        compiler_params=_params("arbitrary", "arbitrary"),
    )(dy, wmat, dh, x, g, *([] if after is None else [after]))


def _shift_down(p, k):
    if k == 0:
        return p
    rows = lax.broadcasted_iota(jnp.int32, p.shape, 0)
    return jnp.where(rows >= k, pltpu.roll(p, k, 0), 0.0)


def _shift_up(p, k):
    if k == 0:
        return p
    s = p.shape[0]
    rows = lax.broadcasted_iota(jnp.int32, p.shape, 0)
    return jnp.where(rows < s - k, pltpu.roll(p, s - k, 0), 0.0)


def _conv_fwd(proj, conv_w, d, tc, name):
    s = proj.shape[0]
    nb = d // tc

    def body(cb_ref, cc_ref, cx_ref, w_ref, y_ref):
        p = cc_ref[...].astype(F32) * cx_ref[...].astype(F32)
        w = w_ref[...]
        acc = p * w[CONV_K - 1:CONV_K, :]
        for k in range(1, CONV_K):
            acc = acc + _shift_down(p, k) * w[CONV_K - 1 - k:CONV_K - k, :]
        y_ref[...] = (cb_ref[...].astype(F32) * acc).astype(y_ref.dtype)

    col = lambda off: pl.BlockSpec((s, tc), lambda j: (0, off * nb + j))
    return _pcall(
        body, name=name, grid=(nb,), in_specs=[col(0), col(1), col(2), pl.BlockSpec((CONV_K, tc), lambda j: (0, j))],
        out_specs=pl.BlockSpec((s, tc), lambda j: (0, j)), out_shape=jax.ShapeDtypeStruct((s, d), BF16),
        compiler_params=_params("parallel"),
    )(proj, proj, proj, conv_w)


def _conv_bwd(dy, proj, conv_w, d, tc, name):
    s = proj.shape[0]
    nb = d // tc

    def body(dy_ref, cb_ref, cc_ref, cx_ref, w_ref, dcb_ref, dcc_ref, dcx_ref, dw_ref):
        cc, cx = cc_ref[...].astype(F32), cx_ref[...].astype(F32)
        p = cc * cx
        w = w_ref[...]
        dyv = dy_ref[...].astype(F32)
        shifted = [_shift_down(p, CONV_K - 1 - k) for k in range(CONV_K)]
        conv = shifted[0] * w[0:1, :]
        for k in range(1, CONV_K):
            conv = conv + shifted[k] * w[k:k + 1, :]
        dcb_ref[...] = (dyv * conv).astype(dcb_ref.dtype)
        ds = dyv * cb_ref[...].astype(F32)
        dp = ds * w[CONV_K - 1:CONV_K, :]
        for k in range(1, CONV_K):
            dp = dp + _shift_up(ds, k) * w[CONV_K - 1 - k:CONV_K - k, :]
        dcc_ref[...] = (dp * cx).astype(dcc_ref.dtype)
        dcx_ref[...] = (dp * cc).astype(dcx_ref.dtype)
        for k in range(CONV_K):
            dw_ref[k:k + 1, :] = jnp.sum(ds * shifted[k], axis=0, keepdims=True)

    col = lambda off: pl.BlockSpec((s, tc), lambda j: (0, off * nb + j))
    blk = pl.BlockSpec((s, tc), lambda j: (0, j))
    wblk = pl.BlockSpec((CONV_K, tc), lambda j: (0, j))
    act = jax.ShapeDtypeStruct((s, d), BF16)
    return _pcall(
        body, name=name, grid=(nb,), in_specs=[blk, col(0), col(1), col(2), wblk],
        out_specs=[blk, blk, blk, wblk], out_shape=[act, act, act, jax.ShapeDtypeStruct((CONV_K, d), F32)],
        compiler_params=_params("parallel"),
    )(dy, proj, proj, proj, conv_w)


def _sb_tile(q, kj, scale, carry, tri, mask):
    z = _dot(q, kj, 1, 1) * scale
    lsz = jnp.minimum(z, 0.0) - jnp.log(1.0 + jnp.exp(-jnp.abs(z)))
    l1m = lsz - z
    if mask is not None:
        l1m = jnp.where(mask, l1m, 0.0)
    l1b = l1m.astype(BF16)
    a = jnp.exp(lsz + (carry + _dot(l1b, tri, 1, 0)))
    if mask is not None:
        a = jnp.where(mask, a, 0.0)
    return lsz, l1b, a.astype(BF16)


def _add_rows(x, upd, r0):
    return x + upd if r0 == 0 else jnp.concatenate([x[:r0], x[r0:] + upd], axis=0)


def _sb_masks(tq, tk):
    row = lax.broadcasted_iota(jnp.int32, (tq, tk), 0)
    col = lax.broadcasted_iota(jnp.int32, (tq, tk), 1)
    masks = [col + dj * tk < row for dj in range(tq // tk)]
    r2 = lax.broadcasted_iota(jnp.int32, (tk, tk), 0)
    c2 = lax.broadcasted_iota(jnp.int32, (tk, tk), 1)
    return masks, (r2 > c2).astype(BF16), (r2 < c2).astype(BF16)


def _sb_fwd(proj, heads, col0, tq, tk, name):
    s = proj.shape[0]
    dh = SB_HEAD_DIM
    nq, nd, nkt = s // tq, tq // tk, s // tk
    scale = dh ** -0.5

    def body(q_ref, k_ref, v_ref, o_ref, a_ref, b_ref):
        i = pl.program_id(1)
        q = q_ref[...]
        masks, tri_right, _ = _sb_masks(tq, tk)

        def tile(j, carry, acc, mask, r0=0):
            start = pl.multiple_of(j * tk, tk)
            kj = k_ref[pl.ds(start, tk), :]
            vj = v_ref[pl.ds(start, tk), :]
            lsz, l1b, ab = _sb_tile(q[r0:], kj, scale, carry[r0:], tri_right, None if mask is None else mask[r0:])
            a_ref[j, r0:, :] = ab
            b_ref[j, r0:, :] = jnp.exp(lsz).astype(b_ref.dtype)
            if r0:
                a_ref[j, :r0, :] = jnp.zeros((r0, tk), a_ref.dtype)
                b_ref[j, :r0, :] = jnp.zeros((r0, tk), b_ref.dtype)
            return (_add_rows(carry, jnp.sum(l1b.astype(F32), axis=1, keepdims=True), r0),
                    _add_rows(acc, _dot(ab, vj, 1, 0), r0))

        state = (jnp.zeros((tq, 1), F32), jnp.zeros((tq, dh), F32))
        for dj in reversed(range(nd)):
            state = tile(i * nd + dj, *state, masks[dj], dj * tk)
        def left_block(t, st):
            for dj in reversed(range(nd)):
                st = tile((i - 1 - t) * nd + dj, st[0], st[1], None)
            return st

        state = lax.fori_loop(0, i, left_block, state)
        o_ref[...] = state[1]

    qspec = pl.BlockSpec((tq, dh), lambda h, i: (i, col0[0] + h))
    kspec = pl.BlockSpec((s, dh), lambda h, i: (0, col0[1] + h))
    vspec = pl.BlockSpec((s, dh), lambda h, i: (0, col0[2] + h))
    saved = pl.BlockSpec((None, nkt, tq, tk), lambda h, i: (h, 0, i, 0))
    saved_shape = jax.ShapeDtypeStruct((heads, nkt, s, tk), BF16)
    return _pcall(
        body, name=name, grid=(heads, nq), in_specs=[qspec, kspec, vspec],
        out_specs=[pl.BlockSpec((tq, dh), lambda h, i: (i, h)), saved, saved],
        out_shape=[jax.ShapeDtypeStruct((s, heads * dh), F32), saved_shape, saved_shape],
        compiler_params=_params("parallel", "parallel"),
    )(proj, proj, proj)


def _sb_bwd(proj, o, a_all, beta_all, do, heads, col0, tq, tk, name):
    s = proj.shape[0]
    dh = SB_HEAD_DIM
    nq, nd, nkt = s // tq, tq // tk, s // tk
    scale = dh ** -0.5

    def body(q_ref, k_ref, v_ref, o_ref, a_ref, b_ref, do_ref, dq_ref, dk_ref, dv_ref, dk_acc, dv_acc):
        i = pl.program_id(1)

        @pl.when(i == 0)
        def _():
            dk_acc[...] = jnp.zeros_like(dk_acc)
            dv_acc[...] = jnp.zeros_like(dv_acc)

        q = q_ref[...]
        dob = do_ref[...].astype(BF16)
        delta = jnp.sum(dob.astype(F32) * o_ref[...], axis=1, keepdims=True)
        masks, _, tri_left = _sb_masks(tq, tk)

        def tile(j, carry_g, dq, mask):
            start = pl.multiple_of(j * tk, tk)
            kj = k_ref[pl.ds(start, tk), :]
            vj = v_ref[pl.ds(start, tk), :]
            ab = a_ref[j]
            g = _dot(dob, vj, 1, 1) * ab.astype(F32)
            carry_g = carry_g + jnp.sum(g, axis=1, keepdims=True)
            left = (delta - carry_g) + _dot(g.astype(BF16), tri_left, 1, 0)
            dz = g - b_ref[j].astype(F32) * (g + left)
            if mask is not None:
                dz = jnp.where(mask, dz, 0.0)
            dzb = dz.astype(BF16)
            dk_acc[pl.ds(start, tk), :] += _dot(dzb, q, 0, 0)
            dv_acc[pl.ds(start, tk), :] += _dot(ab, dob, 0, 0)
            return carry_g, dq + _dot(dzb, kj, 1, 0)

        state = (jnp.zeros((tq, 1), F32), jnp.zeros((tq, dh), F32))
        for dj in reversed(range(nd)):
            state = tile(i * nd + dj, *state, masks[dj])
        def left_block(t, st):
            for dj in reversed(range(nd)):
                st = tile((i - 1 - t) * nd + dj, st[0], st[1], None)
            return st

        state = lax.fori_loop(0, i, left_block, state)
        dq_ref[...] = (state[1] * scale).astype(dq_ref.dtype)

        @pl.when(i == nq - 1)
        def _():
            dk_ref[...] = (dk_acc[...] * scale).astype(dk_ref.dtype)
            dv_ref[...] = dv_acc[...].astype(dv_ref.dtype)

    qspec = pl.BlockSpec((tq, dh), lambda h, i: (i, col0[0] + h))
    kspec = pl.BlockSpec((s, dh), lambda h, i: (0, col0[1] + h))
    vspec = pl.BlockSpec((s, dh), lambda h, i: (0, col0[2] + h))
    blk = pl.BlockSpec((tq, dh), lambda h, i: (i, h))
    full = pl.BlockSpec((s, dh), lambda h, i: (0, h))
    saved = pl.BlockSpec((None, nkt, tq, tk), lambda h, i: (h, 0, i, 0))
    act = jax.ShapeDtypeStruct((s, heads * dh), BF16)
    return _pcall(
        body, name=name, grid=(heads, nq), in_specs=[qspec, kspec, vspec, blk, saved, saved, blk],
        out_specs=[blk, full, full], out_shape=[act, act, act],
        scratch_shapes=[pltpu.VMEM((s, dh), F32), pltpu.VMEM((s, dh), F32)],
        compiler_params=_params("parallel", "arbitrary"),
    )(proj, proj, proj, o, a_all, beta_all, do)


def _xattn_probs(q, k, scale):
    sc = _dot(q, k, 1, 1) * scale
    e = jnp.exp(sc - jnp.max(sc, axis=1, keepdims=True))
    return e / jnp.sum(e, axis=1, keepdims=True)


def _xattn_fwd(qc, kv, tq, name):
    s, d = qc.shape
    m = kv.shape[0]
    dh = d // X_HEADS
    scale = dh ** -0.5

    def body(q_ref, k_ref, v_ref, o_ref):
        p = _xattn_probs(q_ref[...], k_ref[...], scale)
        o_ref[...] = _dot(p.astype(BF16), v_ref[...], 1, 0).astype(o_ref.dtype)

    blk = pl.BlockSpec((tq, dh), lambda h, i: (i, h))
    return _pcall(
        body, name=name, grid=(X_HEADS, s // tq),
        in_specs=[blk, pl.BlockSpec((m, dh), lambda h, i: (0, h)), pl.BlockSpec((m, dh), lambda h, i: (0, X_HEADS + h))],
        out_specs=blk, out_shape=jax.ShapeDtypeStruct((s, d), BF16), compiler_params=_params("parallel", "parallel"),
    )(qc, kv, kv)


def _xattn_bwd(qc, kv, do, tq, name):
    s, d = qc.shape
    m = kv.shape[0]
    dh = d // X_HEADS
    scale = dh ** -0.5
    nq = s // tq

    def body(q_ref, k_ref, v_ref, do_ref, dq_ref, dk_ref, dv_ref, dk_acc, dv_acc):
        i = pl.program_id(1)
        q, k, v = q_ref[...], k_ref[...], v_ref[...]
        dob = do_ref[...].astype(BF16)
        p = _xattn_probs(q, k, scale)
        pb = p.astype(BF16)
        dp = _dot(dob, v, 1, 1)
        ds = pb.astype(F32) * (dp - jnp.sum(dp * pb.astype(F32), axis=1, keepdims=True))
        dsb = (ds * scale).astype(BF16)
        dq_ref[...] = _dot(dsb, k, 1, 0).astype(dq_ref.dtype)
        dk_part = _dot(dsb, q, 0, 0)
        dv_part = _dot(pb, dob, 0, 0)

        @pl.when(i == 0)
        def _():
            dk_acc[...] = dk_part
            dv_acc[...] = dv_part

        @pl.when(i > 0)
        def _():
            dk_acc[...] += dk_part
            dv_acc[...] += dv_part

        @pl.when(i == nq - 1)
        def _():
            dk_ref[...] = dk_acc[...].astype(dk_ref.dtype)
            dv_ref[...] = dv_acc[...].astype(dv_ref.dtype)

    blk = pl.BlockSpec((tq, dh), lambda h, i: (i, h))
    kblk = pl.BlockSpec((m, dh), lambda h, i: (0, h))
    return _pcall(
        body, name=name, grid=(X_HEADS, nq),
        in_specs=[blk, kblk, pl.BlockSpec((m, dh), lambda h, i: (0, X_HEADS + h)), blk],
        out_specs=[blk, kblk, kblk],
        out_shape=[jax.ShapeDtypeStruct((s, d), BF16), jax.ShapeDtypeStruct((m, d), BF16), jax.ShapeDtypeStruct((m, d), BF16)],
        scratch_shapes=[pltpu.VMEM((m, dh), F32), pltpu.VMEM((m, dh), F32)],
        compiler_params=_params("parallel", "arbitrary"),
    )(qc, kv, kv, do)


def _local_step(x, mem, tgt, w, fetch=None, prefetch=None, emit=None, tick=None, after=None):
    fetch = fetch or (lambda name, after: {})
    prefetch = prefetch or (lambda name, after: None)
    emit = emit or (lambda group, g: None)
    tick = tick or (lambda group, after: None)
    w = dict(w)
    s, d = x.shape
    heads = d // SB_HEAD_DIM
    tm = _pick(s, (512, 256, 128))
    tq = _pick(s, (1024, 512, 256, 128))
    sb_tq, sb_tk = _pick(s, (512, 256, 128)), _pick(s, (256, 128))
    tc = _pick(d, (256, 128))
    g = {}

    def wt(name, after):
        if name not in w:
            w.update(fetch(name, after))
        return w[name]

    def ffn_fwd(h, gname, wgu, wdown, tag, after=None):
        n = _rms_fwd(h, w[gname], tag + "_norm", tm, after=after)
        gu, act = _ffn_up(n, wt(wgu, n), tag + "_gu")
        prefetch(wdown, gu)
        return n, gu, act, _mm(act, wt(wdown, act), name=tag + "_down", out_dtype=F32, res=h, alpha=0.5)

    def ffn_bwd(dh, dhb, h, saved, gname, wgu, wdown, tag, copy_scale=None, after=None):
        n, gu, act = saved
        g[wdown] = _mm(act, dhb, ta=True, name=tag + "_dwdown", after=after)
        dgu = _ffn_dgu(dhb, w[wdown], gu, tag + "_dgu", after=emit(tag + "_down", g))
        g[wgu] = _mm(n, dgu, ta=True, b_halves=True, name=tag + "_dwgu", after=tick(tag + "_down", dgu))
        *dh_in, g[gname] = _dgrad_norm(dgu, w[wgu], dh, h, w[gname], tag + "_dn", dy_halves=True, copy_scale=copy_scale,
                                       after=emit(tag, g))
        return dh_in, tick(tag, dh_in[0])

    n1, gu1, act1, h1 = ffn_fwd(x, "g_ffn1", "w_ffn1_gu", "w_ffn1_down", "ffn1", after)
    prefetch("w_in", h1)
    u = _rms_fwd(h1, w["g_mix"], "mix_norm", tm)
    proj = _mm(u, wt("w_in", u), name="mix_in")
    prefetch("w_conv_out", proj)
    nd = d // SB_HEAD_DIM
    y_conv = _conv_fwd(proj, w["conv_w"], d, tc, "conv_fwd")
    sb_cols = (3 * nd, 4 * nd, 5 * nd)
    y_sb, sb_a, sb_beta = _sb_fwd(proj, heads, sb_cols, sb_tq, sb_tk, "sb_fwd")
    prefetch("w_cq", y_sb)
    a_conv = _mm(y_conv, wt("w_conv_out", y_conv), name="conv_out")
    a_sb = _mm(y_sb, wt("w_attn_out", y_sb), name="attn_out")
    b_conv, b_sb = w["b_gate"][:, :d], w["b_gate"][:, d:]

    def merge(ac, asb, gcp, gsp, bc, bs):
        gc = _sigmoid(gcp.astype(F32) + bc)
        gs = _sigmoid(gsp.astype(F32) + bs)
        return gc * ac.astype(F32) + gs * asb.astype(F32)

    merged = _rowcall(merge, [_whole(a_conv), _whole(a_sb), (proj, 6, d), (proj, 7, d)], [b_conv, b_sb], [(d, BF16)],
                      tm=tm, name="merge")[0]
    prefetch("w_ffn2_gu", merged)
    h2 = _mm(merged, wt("w_o", merged), name="mix_out", out_dtype=F32, res=h1)
    hn = _rms_fwd(h2, w["g_cross"], "cross_norm", tm)
    mn = _rms_fwd(mem, w["g_mem"], "mem_norm", _pick(mem.shape[0], (256, 128)))
    qc = _mm(hn, wt("w_cq", hn), name="cross_q")
    kv = _mm(mn, wt("w_ckv", mn), name="cross_kv")
    oc = _xattn_fwd(qc, kv, tq, "xattn_fwd")
    h3 = _mm(oc, wt("w_co", oc), name="cross_out", out_dtype=F32, res=h2)
    n2, gu2, act2, h4 = ffn_fwd(h3, "g_ffn2", "w_ffn2_gu", "w_ffn2_down", "ffn2")

    def head(hb, tb, gb):
        xh, r = _xhat(hb)
        err = xh * gb - tb
        dy = err * (1.0 / d)
        dxh = dy * gb
        dx = r * (dxh - xh * jnp.mean(dxh * xh, axis=-1, keepdims=True))
        row_loss = 0.5 * jnp.mean(err * err, axis=-1, keepdims=True)
        return dx, 0.5 * dx, dy * xh, jnp.broadcast_to(row_loss, (row_loss.shape[0], LANES))

    dh4, dh4b, g["g_final"], loss_lanes = _rowcall(head, [_whole(h4), _whole(tgt)], [w["g_final"]], [(d, F32), (d, BF16)],
                                                   [d, LANES], tm=tm, name="loss_head")

    (dh3, dh3b), tok = ffn_bwd(dh4, dh4b, h3, (n2, gu2, act2), "g_ffn2", "w_ffn2_gu", "w_ffn2_down", "ffn2", copy_scale=1.0)
    g["w_co"] = _mm(oc, dh3b, ta=True, name="cross_dwco", after=tok)
    doc = _mm(dh3b, w["w_co"], tb=True, name="cross_doc")
    dqc, dk, dv = _xattn_bwd(qc, kv, doc, tq, "xattn_bwd")
    dkv = jnp.concatenate([dk, dv], axis=1)
    g["w_cq"] = _mm(hn, dqc, ta=True, name="cross_dwcq")
    g["w_ckv"] = _mm(mn, dkv, ta=True, name="cross_dwckv")
    dmn = _mm(dkv, w["w_ckv"], tb=True, name="cross_dmn", out_dtype=F32)
    g["g_mem"] = _rowcall(lambda dy, xb: dy * _xhat(xb)[0], [_whole(dmn), _whole(mem)], [], [], [d],
                          tm=_pick(mem.shape[0], (256, 128)), name="mem_dnorm")[0]
    dh2, dh2b, g["g_cross"] = _dgrad_norm(dqc, w["w_cq"], dh3, h2, w["g_cross"], "cross_dhn", copy_scale=1.0, after=emit("cross", g))

    g["w_o"] = _mm(merged, dh2b, ta=True, name="mix_dwo", after=tick("cross", dh2))
    dmerged = _mm(dh2b, w["w_o"], tb=True, name="mix_dmerged")

    def merge_bwd(dm, ac, asb, gcp, gsp, bc, bs):
        dm, ac, asb = dm.astype(F32), ac.astype(F32), asb.astype(F32)
        gc = _sigmoid(gcp.astype(F32) + bc)
        gs = _sigmoid(gsp.astype(F32) + bs)
        dgc = dm * ac * gc * (1.0 - gc)
        dgs = dm * asb * gs * (1.0 - gs)
        return dm * gc, dm * gs, dgc, dgs, dgc, dgs

    da_conv, da_sb, dgc, dgs, db_conv, db_sb = _rowcall(
        merge_bwd, [_whole(dmerged), _whole(a_conv), _whole(a_sb), (proj, 6, d), (proj, 7, d)], [b_conv, b_sb],
        [(d, BF16)] * 4, [d, d], tm=tm, name="merge_bwd")
    g["b_gate"] = jnp.concatenate([db_conv, db_sb], axis=1)
    g["w_conv_out"] = _mm(y_conv, da_conv, ta=True, name="conv_dwout")
    g["w_attn_out"] = _mm(y_sb, da_sb, ta=True, name="attn_dwout")
    dy_conv = _mm(da_conv, w["w_conv_out"], tb=True, name="conv_dy")
    dy_sb = _mm(da_sb, w["w_attn_out"], tb=True, name="attn_dy")
    dcb, dcc, dcx, g["conv_w"] = _conv_bwd(dy_conv, proj, w["conv_w"], d, tc, "conv_bwd")
    dq, dk_sb, dv_sb = _sb_bwd(proj, y_sb, sb_a, sb_beta, dy_sb, heads, sb_cols, sb_tq, sb_tk, "sb_bwd")
    dproj = jnp.concatenate([dcb, dcc, dcx, dq, dk_sb, dv_sb, dgc, dgs], axis=1)
    g["w_in"] = _mm(u, dproj, ta=True, name="mix_dwin")
    dh1, dh1b, g["g_mix"] = _dgrad_norm(dproj, w["w_in"], dh2, h1, w["g_mix"], "mix_du", copy_scale=0.5, after=emit("mix", g))
    (dx,), tok = ffn_bwd(dh1, dh1b, x, (n1, gu1, act1), "g_ffn1", "w_ffn1_gu", "w_ffn1_down", "ffn1", after=tick("mix", dh1))
    return loss_lanes, dx, g, tok


MATS = (("w_ffn1_gu", "col"), ("w_ffn1_down", "row"), ("w_in", "col"), ("w_conv_out", "row"), ("w_attn_out", "row"),
        ("w_o", "row"), ("w_cq", "row"), ("w_ckv", "col"), ("w_co", "row"), ("w_ffn2_gu", "col"), ("w_ffn2_down", "row"))
VECS = ("g_ffn1", "g_mix", "g_cross", "g_mem", "g_ffn2", "g_final")
WEIGHTS = ("g_ffn1", "w_ffn1_gu", "w_ffn1_down", "g_mix", "w_in", "b_gate", "conv_w", "w_conv_out", "w_attn_out", "w_o",
           "g_cross", "g_mem", "w_cq", "w_ckv", "w_co", "g_ffn2", "w_ffn2_gu", "w_ffn2_down", "g_final")
CONV_ROWS = 8


def _full_shape(kind, r, c):
    return (r, N_CHIPS * c) if kind == "col" else (N_CHIPS * r, c)


def _piece(ref, kind, r, c, chip, half):
    hr = r // 2
    if kind == "col":
        return ref.at[pl.ds(pl.multiple_of(half * hr, 16), hr), pl.ds(pl.multiple_of(chip * c, LANES), c)]
    return ref.at[pl.ds(pl.multiple_of(chip * r + half * hr, 16), hr), :]


def _shard_of(ref, kind, r, c, chip):
    if kind == "col":
        return ref.at[:, pl.ds(pl.multiple_of(chip * c, LANES), c)]
    return ref.at[pl.ds(pl.multiple_of(chip * r, 16), r), :]


def _place():
    x, y, c = lax.axis_index("x"), lax.axis_index("y"), lax.axis_index("c")
    others = [(1 - x, y), (x, 1 - y), (1 - x, 1 - y)]
    return x, y, c, 2 * x + y, others


def _remote(src, dst, send_sem, recv_sem, to):
    return pltpu.make_async_remote_copy(src_ref=src, dst_ref=dst, send_sem=send_sem, recv_sem=recv_sem,
                                        device_id=to, device_id_type=MESH)


def _gather_conv(conv_shard):
    cc = conv_shard.shape[1]

    def body(conv_ref, conv_full, cs, cr, cl):
        x, y, c, me, others = _place()

        def cols(chip):
            return conv_full.at[:, pl.ds(pl.multiple_of(chip * cc, LANES), cc)]

        def conv(k, chip_from, to):
            return _remote(conv_ref, cols(chip_from), cs.at[k], cr.at[k], to)

        mine = pltpu.make_async_copy(conv_ref, cols(me), cl.at[0])
        mine.start()
        for k, (ox, oy) in enumerate(others):
            conv(k, me, (ox, oy, c)).start()
        for k, (ox, oy) in enumerate(others):
            conv(k, 2 * ox + oy, (x, y, c)).wait_recv()
            conv(k, me, (ox, oy, c)).wait_send()
        mine.wait()

    dma = pltpu.SemaphoreType.DMA
    return _pcall(
        body, name="gather_conv", in_specs=[ANY], out_specs=ANY,
        out_shape=jax.ShapeDtypeStruct((CONV_ROWS, N_CHIPS * cc), F32), scratch_shapes=[dma((3,)), dma((3,)), dma((1,))],
    )(conv_shard)


HBM = pl.BlockSpec(memory_space=pltpu.HBM)
SEM = pl.BlockSpec(memory_space=pltpu.SEMAPHORE)
EFFECT = pltpu.SideEffectType.DATAFLOW_SIDE_EFFECTING
TOKEN = (8, LANES)


def _split_start(name, plan, n_copies, srcs, lands, after=None):
    ns, nl = len(srcs), len(lands)
    n_in = ns + nl + (after is not None)

    def body(*refs):
        outs = refs[n_in:]
        sends, _ = plan(refs[:ns], refs[ns:ns + nl], outs[0], outs[1])
        for cp in sends:
            cp.start()
        outs[-1][...] = jnp.zeros(TOKEN, F32)

    held = [pltpu.HBM(a.shape, a.dtype) for a in (*srcs, *lands)]
    dma = pltpu.SemaphoreType.DMA((n_copies,))
    ins = [pltpu.with_memory_space_constraint(a, pltpu.HBM) for a in (*srcs, *lands)]
    outs = _pcall(
        body, name=name, in_specs=[HBM] * (ns + nl) + ([] if after is None else [ANY]),
        out_specs=(SEM, SEM, *[HBM] * (ns + nl), pl.BlockSpec(memory_space=pltpu.VMEM)),
        out_shape=(dma, dma, *held, jax.ShapeDtypeStruct(TOKEN, F32)),
        input_output_aliases={i: 2 + i for i in range(ns + nl)},
        compiler_params=pltpu.CompilerParams(has_side_effects=EFFECT),
    )(*ins, *([] if after is None else [after]))
    return outs[0], outs[1], list(outs[2:2 + ns]), list(outs[2 + ns:2 + ns + nl]), outs[-1]


def _split_wait(name, plan, send_sems, recv_sems, srcs, lands, after):
    ns, nl = len(srcs), len(lands)

    def body(*refs):
        sends, recvs = plan(refs[:ns], refs[ns:ns + nl], refs[ns + nl], refs[ns + nl + 1])
        for cp in sends:
            cp.wait_send()
        for cp in recvs:
            cp.wait_recv()

    outs = _pcall(
        body, name=name, in_specs=[HBM] * (ns + nl) + [SEM, SEM, ANY], out_specs=[HBM] * (ns + nl),
        out_shape=[pltpu.HBM(a.shape, a.dtype) for a in (*srcs, *lands)],
        input_output_aliases={i: i for i in range(ns + nl)},
        compiler_params=pltpu.CompilerParams(has_side_effects=EFFECT),
    )(*srcs, *lands, send_sems, recv_sems, after)
    return list(outs[:ns]), list(outs[ns:])


def _gather_plan(dims):
    def plan(shard_refs, full_refs, ss, rs):
        x, y, c, me, others = _place()
        sends, recvs = [], []
        for wi, (kind, r, cw) in enumerate(dims):
            half = shard_refs[wi].at[pl.ds(pl.multiple_of(c * (r // 2), 16), r // 2), :]
            for k, (ox, oy) in enumerate(others):
                sem = 4 * wi + k
                sends.append(_remote(half, _piece(full_refs[wi], kind, r, cw, me, c), ss.at[sem], rs.at[sem], (ox, oy, c)))
                recvs.append(_remote(half, _piece(full_refs[wi], kind, r, cw, 2 * ox + oy, c), ss.at[sem], rs.at[sem], (x, y, c)))
            sem = 4 * wi + 3
            own = _remote(shard_refs[wi], _shard_of(full_refs[wi], kind, r, cw, me), ss.at[sem], rs.at[sem], (x, y, 1 - c))
            sends.append(own)
            recvs.append(own)
        return sends, recvs

    return plan


def _forward_plan(dims):
    def plan(_, full_refs, ss, rs):
        x, y, c, _, others = _place()
        sends, recvs = [], []
        for wi, (kind, r, cw) in enumerate(dims):
            for k, (ox, oy) in enumerate(others):
                sem = 3 * wi + k
                mine = _piece(full_refs[wi], kind, r, cw, 2 * ox + oy, c)
                theirs = _piece(full_refs[wi], kind, r, cw, 2 * ox + oy, 1 - c)
                sends.append(_remote(mine, mine, ss.at[sem], rs.at[sem], (x, y, 1 - c)))
                recvs.append(_remote(theirs, theirs, ss.at[sem], rs.at[sem], (x, y, 1 - c)))
        return sends, recvs

    return plan


def _rs_cores_plan(dims):
    def plan(g_refs, land_refs, ss, rs):
        x, y, c, _, _ = _place()
        sends, recvs = [], []
        for wi, dm in enumerate(dims):
            for chip in range(N_CHIPS):
                sem = N_CHIPS * wi + chip
                sends.append(_remote(_piece(g_refs[wi], *dm, chip, 1 - c), land_refs[wi].at[chip], ss.at[sem], rs.at[sem], (x, y, 1 - c)))
                recvs.append(_remote(_piece(g_refs[wi], *dm, chip, c), land_refs[wi].at[chip], ss.at[sem], rs.at[sem], (x, y, 1 - c)))
        return sends, recvs

    return plan


def _share_plan(nw):
    def plan(_, buf_refs, ss, rs):
        x, y, c, _, _ = _place()
        sends = [_remote(buf_refs[wi].at[c], buf_refs[wi].at[c], ss.at[wi], rs.at[wi], (x, y, 1 - c)) for wi in range(nw)]
        recvs = [_remote(buf_refs[wi].at[1 - c], buf_refs[wi].at[1 - c], ss.at[wi], rs.at[wi], (x, y, 1 - c)) for wi in range(nw)]
        return sends, recvs

    return plan


def _small_plan():
    def plan(_, buf_refs, ss, rs):
        x, y, c = lax.axis_index("x"), lax.axis_index("y"), lax.axis_index("c")
        buf = buf_refs[0]
        sends, recvs = [], []
        for rel in range(1, N_DEV):
            peer = (x ^ (rel >> 2 & 1), y ^ (rel >> 1 & 1), c ^ (rel & 1))
            sends.append(_remote(buf.at[0], buf.at[rel], ss.at[rel - 1], rs.at[rel - 1], peer))
            recvs.append(_remote(buf.at[0], buf.at[rel], ss.at[rel - 1], rs.at[rel - 1], peer))
        return sends, recvs

    return plan


def _sum_small(buf, me, name):
    _, rows, n = buf.shape

    def body(me_ref, b_ref, o_ref):
        tot = b_ref[me_ref[0]]
        for dev in range(1, N_DEV):
            tot = tot + b_ref[dev ^ me_ref[0]]
        o_ref[...] = tot

    return _pcall(
        body, name=name, out_shape=jax.ShapeDtypeStruct((rows, n), F32),
        grid_spec=pltpu.PrefetchScalarGridSpec(
            num_scalar_prefetch=1, grid=(1,), in_specs=[pl.BlockSpec((N_DEV, rows, n), lambda i, m: (0, 0, 0))],
            out_specs=pl.BlockSpec((rows, n), lambda i, m: (0, 0))),
    )(me, buf)


def _rs_chips_plan(nw):
    def plan(p_refs, land_refs, ss, rs):
        x, y, c, me, others = _place()
        sends, recvs = [], []
        for wi in range(nw):
            for k, (ox, oy) in enumerate(others):
                sem = 3 * wi + k
                sends.append(_remote(p_refs[wi].at[2 * ox + oy], land_refs[wi].at[k], ss.at[sem], rs.at[sem], (ox, oy, c)))
                recvs.append(_remote(p_refs[wi].at[me], land_refs[wi].at[k], ss.at[sem], rs.at[sem], (x, y, c)))
        return sends, recvs

    return plan


def _rows_per_block(n, c, limit_bytes=2 << 20):
    best = None
    for tm in range(16, n + 1, 16):
        if n % tm == 0 and tm * c * 4 <= limit_bytes:
            best = tm
    return best or n


def _sum_cores(grad, got, kind, place, name):
    _, hr, cw = got.shape
    tm = _rows_per_block(hr, cw)
    nb = hr // tm

    def body(place_ref, g_ref, t_ref, o_ref):
        o_ref[...] = (g_ref[...].astype(F32) + t_ref[...].astype(F32)).astype(o_ref.dtype)

    if kind == "col":
        g_spec = pl.BlockSpec((tm, cw), lambda j, i, pr: (pr[0] * nb + i, j))
    else:
        g_spec = pl.BlockSpec((tm, cw), lambda j, i, pr: ((2 * j + pr[0]) * nb + i, 0))
    blk = pl.BlockSpec((None, tm, cw), lambda j, i, pr: (j, i, 0))
    return _pcall(
        body, name=name, out_shape=jax.ShapeDtypeStruct(got.shape, BF16),
        grid_spec=pltpu.PrefetchScalarGridSpec(num_scalar_prefetch=1, grid=(N_CHIPS, nb), in_specs=[g_spec, blk], out_specs=blk),
        compiler_params=_params("parallel", "parallel"),
    )(place, grad, got)


def _sum_chips(parts, got, place, name):
    _, n, cw = got.shape
    tm = _rows_per_block(n, cw)

    def body(place_ref, p_ref, g_ref, o_ref):
        tot = p_ref[...].astype(F32)
        for k in range(3):
            tot = tot + g_ref[k].astype(F32)
        o_ref[...] = tot

    return _pcall(
        body, name=name, out_shape=jax.ShapeDtypeStruct((2, n, cw), F32),
        grid_spec=pltpu.PrefetchScalarGridSpec(
            num_scalar_prefetch=1, grid=(n // tm,),
            in_specs=[pl.BlockSpec((None, tm, cw), lambda i, pr: (pr[1], i, 0)), pl.BlockSpec((3, tm, cw), lambda i, pr: (0, i, 0))],
            out_specs=pl.BlockSpec((None, tm, cw), lambda i, pr: (pr[0], i, 0))),
        compiler_params=_params("parallel"),
    )(place, parts, got)


def _adamw(g, w, m, v, name):
    n, c = g.shape
    c1 = 1.0 - ADAM_B1 ** ADAM_STEP
    c2 = 1.0 - ADAM_B2 ** ADAM_STEP

    def fn(gb, wb, mb, vb):
        m_new = ADAM_B1 * mb + (1.0 - ADAM_B1) * gb
        v_new = ADAM_B2 * vb + (1.0 - ADAM_B2) * (gb * gb)
        delta = -ADAM_LR * ((m_new / c1) / (jnp.sqrt(v_new / c2) + ADAM_EPS) + ADAM_WD * wb)
        return gb, delta, m_new, v_new

    tm = _rows_per_block(n, c) if n % 16 == 0 else n
    return _rowcall(fn, [_whole(g), _whole(w), _whole(m), _whole(v)], [], [(c, F32)] * 4, tm=tm, name=name)


PACK_ROWS = 16


def _pack_rows(parts, width, name, after=None):
    assert sum(p.shape[0] for p in parts) <= PACK_ROWS

    def body(*refs):
        out_ref = refs[-1]
        out_ref[...] = jnp.zeros_like(out_ref)
        at = 0
        for r in refs[:len(parts)]:
            k, n = r.shape
            if n == width:
                out_ref[at:at + k, :] = r[...]
            else:
                out_ref[at:at + k, :] = jnp.broadcast_to(r[:, :1], (k, width))
            at += k

    vm = pl.BlockSpec(memory_space=pltpu.VMEM)
    return _pcall(body, name=name, in_specs=[vm] * len(parts) + ([] if after is None else [ANY]), out_specs=vm,
                  out_shape=jax.ShapeDtypeStruct((PACK_ROWS, width), F32))(*parts, *([] if after is None else [after]))


def _cast_shard(wm, name, after):
    n, c = wm.shape
    return _rowcall(lambda v: v, [_whole(wm)], [], [(c, BF16)], tm=_rows_per_block(n, c), name=name, after=after)[0]


GATHER_GROUPS = (
    ("w_ffn1_gu",), ("w_ffn1_down",), ("w_in",), ("w_conv_out", "w_attn_out", "w_o"), ("w_cq", "w_ckv", "w_co"),
    ("w_ffn2_gu", "w_ffn2_down"),
)
REDUCE_GROUPS = {
    "ffn2": ("w_ffn2_down", "w_ffn2_gu"),
    "cross": ("w_co", "w_cq", "w_ckv"),
    "mix": ("w_o", "w_conv_out", "w_attn_out", "w_in"),
    "ffn1_down": ("w_ffn1_down",),
    "ffn1": ("w_ffn1_gu",),
}
TAIL_STAGES = (("ffn2", "cross"), ("mix",), ("ffn1_down", "ffn1"))
KIND = dict(MATS)


def _step(x, mem, tgt, wts, m_in, v_in):
    d = x.shape[-1]
    cc = wts["conv_w"].shape[1]
    place = jnp.stack([lax.axis_index("c"), 2 * lax.axis_index("x") + lax.axis_index("y")]).astype(jnp.int32)
    dims = {n: (kind, *wts[n].shape) for n, kind in MATS}

    conv_full = _gather_conv(jnp.pad(wts["conv_w"], ((0, CONV_ROWS - CONV_K), (0, 0))))
    w = {n: wts[n].reshape(1, -1) for n in VECS + ("b_gate",)}
    w["conv_w"] = conv_full[:CONV_K]
    flying, token = {}, conv_full
    for names in GATHER_GROUPS:
        gd = [dims[n] for n in names]
        shards = [_cast_shard(wts[n], "cast_" + n, token) for n in names]
        lands = [lax.empty(_full_shape(*dm), BF16) for dm in gd]
        plan = _gather_plan(gd)
        ss, rs, srcs, lands, token = _split_start("gather_start_" + names[0], plan, 4 * len(names), shards, lands, token)
        flying.update({n: (names, plan, ss, rs, srcs, lands, gd) for n in names})

    passing = {}

    def prefetch(name, after):
        if name not in passing:
            names, plan, ss, rs, srcs, lands, gd = flying[name]
            _, lands = _split_wait("gather_wait_" + names[0], plan, ss, rs, srcs, lands, after)
            plan = _forward_plan(gd)
            ss, rs, _, lands, _ = _split_start("forward_start_" + names[0], plan, 3 * len(names), [], lands)
            passing.update({n: (names, plan, ss, rs, lands) for n in names})

    def fetch(name, after):
        prefetch(name, after)
        names, plan, ss, rs, lands = passing[name]
        _, lands = _split_wait("forward_wait_" + names[0], plan, ss, rs, [], lands, after)
        return dict(zip(names, lands))

    swapping, sent = {}, {}

    def emit(tag, g):
        if tag not in REDUCE_GROUPS:
            return None
        names = REDUCE_GROUPS[tag]
        gd = [dims[n] for n in names]
        lands = [lax.empty((N_CHIPS, r // 2, cw), BF16) for (_, r, cw) in gd]
        plan = _rs_cores_plan(gd)
        ss, rs, srcs, lands, tok = _split_start("rs_cores_start_" + tag, plan, N_CHIPS * len(names), [g[n] for n in names], lands)
        swapping[tag] = (plan, ss, rs, srcs, lands)
        return tok

    def tick(tag, after):
        if tag not in REDUCE_GROUPS:
            return None
        names = REDUCE_GROUPS[tag]
        plan, ss, rs, srcs, lands = swapping[tag]
        mine, got = _split_wait("rs_cores_wait_" + tag, plan, ss, rs, srcs, lands, after)
        parts = [_sum_cores(gm, t, KIND[n], place, "sum_cores_" + n) for n, gm, t in zip(names, mine, got)]
        lands = [lax.empty((3, *p.shape[1:]), BF16) for p in parts]
        plan = _rs_chips_plan(len(names))
        ss, rs, srcs, lands, tok = _split_start("rs_chips_start_" + tag, plan, 3 * len(names), parts, lands)
        sent[tag] = (plan, ss, rs, srcs, lands)
        return tok

    loss_lanes, dx, g, last = _local_step(x[0], mem[0], tgt[0], w, fetch, prefetch, emit, tick, token)

    rows = [g[n] for n in VECS] + [g["b_gate"][:, :d], g["b_gate"][:, d:], g["conv_w"], loss_lanes]
    packed = _pack_rows(rows, d, "pack_small", after=last)
    small = jnp.concatenate([packed[None], jnp.zeros((N_DEV - 1, *packed.shape), F32)], axis=0)
    small_plan = _small_plan()
    small_ss, small_rs, _, small, after = _split_start("small_start", small_plan, N_DEV - 1, [], [small])

    grads, out = {}, {}

    def update(n):
        shape = wts[n].shape
        as2d = (lambda a: a.reshape(1, -1)) if len(shape) == 1 else (lambda a: a)
        return [r.reshape(shape) for r in _adamw(grads[n], as2d(wts[n]), as2d(m_in[n]), as2d(v_in[n]), "adamw_" + n)]

    def finish(sharing, after):
        tag, names, plan, ss, rs, halves = sharing
        _, both = _split_wait("share_wait_" + tag, plan, ss, rs, [], halves, after)
        for n, b in zip(names, both):
            grads[n] = b.reshape(-1, b.shape[-1])
            out[n] = update(n)
        return out[names[-1]][1]

    sharing = None
    for stage in TAIL_STAGES:
        names, halves = [], []
        for tag in stage:
            plan, ss, rs, srcs, lands = sent[tag]
            parts, landed = _split_wait("rs_chips_wait_" + tag, plan, ss, rs, srcs, lands, after)
            halves += [_sum_chips(p, t, place, "sum_chips_" + n) for n, p, t in zip(REDUCE_GROUPS[tag], parts, landed)]
            names += REDUCE_GROUPS[tag]
        plan = _share_plan(len(names))
        ss, rs, _, halves, after = _split_start("share_start_" + stage[0], plan, len(names), [], halves)
        if sharing is not None:
            after = finish(sharing, after)
        sharing = (stage[0], names, plan, ss, rs, halves)
    after = finish(sharing, after)

    _, small = _split_wait("small_wait", small_plan, small_ss, small_rs, [], small, after)
    me = (4 * lax.axis_index("x") + 2 * lax.axis_index("y") + lax.axis_index("c")).astype(jnp.int32).reshape(1)
    red = _sum_small(small[0], me, "sum_small")
    grads.update({n: red[i:i + 1] for i, n in enumerate(VECS)})
    nv = len(VECS)
    grads["b_gate"] = jnp.concatenate([red[nv:nv + 1], red[nv + 1:nv + 2]], axis=1)
    chip = 2 * lax.axis_index("x") + lax.axis_index("y")
    grads["conv_w"] = lax.dynamic_slice_in_dim(red[nv + 2:nv + 2 + CONV_K], chip * cc, cc, axis=1)
    loss = red[nv + 2 + CONV_K, 0]
    out.update({n: update(n) for n in WEIGHTS if n not in KIND})
    return (loss, dx[None], *[out[n][0] for n in WEIGHTS], *[out[n][1] for n in WEIGHTS],
            *[out[n][2] for n in WEIGHTS], *[out[n][3] for n in WEIGHTS])


def kernel(x, mem, g_ffn1, w_ffn1_gu, w_ffn1_down, g_mix, w_in, b_gate, conv_w, w_conv_out, w_attn_out, w_o, g_cross, g_mem, w_cq, w_ckv, w_co, g_ffn2, w_ffn2_gu, w_ffn2_down, g_final, loss_target, m_g_ffn1, m_w_ffn1_gu, m_w_ffn1_down, m_g_mix, m_w_in, m_b_gate, m_conv_w, m_w_conv_out, m_w_attn_out, m_w_o, m_g_cross, m_g_mem, m_w_cq, m_w_ckv, m_w_co, m_g_ffn2, m_w_ffn2_gu, m_w_ffn2_down, m_g_final, v_g_ffn1, v_w_ffn1_gu, v_w_ffn1_down, v_g_mix, v_w_in, v_b_gate, v_conv_w, v_w_conv_out, v_w_attn_out, v_w_o, v_g_cross, v_g_mem, v_w_cq, v_w_ckv, v_w_co, v_g_ffn2, v_w_ffn2_gu, v_w_ffn2_down, v_g_final):
    given = dict(locals())
    wts = {n: given[n] for n in WEIGHTS}
    m_in = {n: given["m_" + n] for n in WEIGHTS}
    v_in = {n: given["v_" + n] for n in WEIGHTS}
    return _step(x, mem, loss_target, wts, m_in, v_in)
```

```python
import math

import jax
import jax.numpy as jnp
from jax import lax
from jax.experimental import pallas as pl
from jax.experimental.pallas import tpu as pltpu

F32 = jnp.float32
BF16 = jnp.bfloat16
MESH = pl.DeviceIdType.MESH

V7X_VMEM_LIMIT_BYTES = 48 * 1024 * 1024
MM_VMEM_BUDGET_BYTES = 36 * 1024 * 1024
MM_WHOLE_K = 2816
LANES = 128
SB_HEAD_DIM = 128
X_HEADS = 4
CONV_K = 3
RMS_EPS = 1e-6
N_CHIPS = 4
N_DEV = 8
ADAM_LR, ADAM_B1, ADAM_B2, ADAM_EPS, ADAM_WD, ADAM_STEP = 0.001, 0.9, 0.999, 1e-08, 0.01, 10


ANY = pl.BlockSpec(memory_space=pl.ANY)


def _pcall(body, **kw):
    return pl.pallas_call(body, **kw)


def _params(*sem):
    return pltpu.CompilerParams(dimension_semantics=sem, vmem_limit_bytes=V7X_VMEM_LIMIT_BYTES)


def _pick(dim, cands):
    for c in cands:
        if dim % c == 0:
            return c
    return dim


def _dot(a, b, ca, cb):
    return lax.dot_general(a, b, (((ca,), (cb,)), ((), ())), preferred_element_type=F32)


def _mm(a, b, *, name, ta=False, tb=False, out_dtype=BF16, res=None, alpha=1.0, tm=None, tn=None, tk=None, after=None,
        a_halves=False, b_halves=False):
    assert not (a_halves and ta) and not (b_halves and tb)
    if a_halves:
        m, k = a.shape[1], 2 * a.shape[2]
    else:
        m, k = (a.shape[1], a.shape[0]) if ta else a.shape
    if b_halves:
        n = 2 * b.shape[2]
        assert k == b.shape[1]
    else:
        n = b.shape[0] if tb else b.shape[1]
        assert k == (b.shape[1] if tb else b.shape[0]), (a.shape, b.shape, ta, tb)
    if ta:
        tm = tm or _pick(m, (512, 256, 128))
        tn = tn or _pick(n, (1024, 512, 256, 128))
        tk = tk or (k if k <= MM_WHOLE_K else _pick(k, (1024, 512, 256, 128)))
    else:
        tk = tk or (k if k <= MM_WHOLE_K else _pick(k, (MM_WHOLE_K, 2048, 1024, 512, 256, 128)))
        tn = tn or _pick(n, (512, 1408, 256, 128) if tk == k else (1024, 512, 256, 128))
        per_row = 2 * (tk * a.dtype.itemsize + tn * (jnp.dtype(out_dtype).itemsize + (0 if res is None else res.dtype.itemsize)))
        per_row += 4 * tn if tk < k else 0
        rows = (MM_VMEM_BUDGET_BYTES - 2 * tk * tn * b.dtype.itemsize) // per_row
        tm = tm or next((c for c in (2048, 1024, 512, 256, 128) if m % c == 0 and c <= rows), m)
    if a_halves:
        tk = min(tk, k // 2) if (k // 2) % min(tk, k // 2) == 0 else _pick(k // 2, (1408, 1024, 512, 256, 128))
    if b_halves:
        tn = tn if (n // 2) % tn == 0 else _pick(n // 2, (1408, 1024, 512, 256, 128))
    nk = k // tk
    assert m % tm == 0 and n % tn == 0 and k % tk == 0
    a_spec = pl.BlockSpec((tk, tm), lambda i, j, kk: (kk, i)) if ta else pl.BlockSpec((tm, tk), lambda i, j, kk: (i, kk))
    b_spec = pl.BlockSpec((tn, tk), lambda i, j, kk: (j, kk)) if tb else pl.BlockSpec((tk, tn), lambda i, j, kk: (kk, j))
    if a_halves:
        per = (k // 2) // tk
        a_spec = pl.BlockSpec((None, tm, tk), lambda i, j, kk: (kk // per, i, kk % per))
    if b_halves:
        per_n = (n // 2) // tn
        b_spec = pl.BlockSpec((None, tk, tn), lambda i, j, kk: (j // per_n, kk, j % per_n))
    o_spec = pl.BlockSpec((tm, tn), lambda i, j, kk: (i, j))
    ca, cb = (0 if ta else 1), (1 if tb else 0)

    n_in = 2 + (res is not None) + (after is not None)

    def body(*refs):
        a_ref, b_ref = refs[:2]
        res_ref = refs[2] if res is not None else None
        o_ref = refs[n_in]
        scratch = refs[n_in + 1:]

        def finish(acc):
            val = acc if alpha == 1.0 else alpha * acc
            if res_ref is not None:
                val = res_ref[...].astype(F32) + val
            o_ref[...] = val.astype(o_ref.dtype)

        part = _dot(a_ref[...].astype(BF16), b_ref[...].astype(BF16), ca, cb)
        if nk == 1:
            finish(part)
        else:
            acc_ref = scratch[0]
            kk = pl.program_id(2)

            @pl.when(kk == 0)
            def _():
                acc_ref[...] = part

            @pl.when(kk > 0)
            def _():
                acc_ref[...] += part

            @pl.when(kk == nk - 1)
            def _():
                finish(acc_ref[...])

    ins = [a, b] + ([] if res is None else [res]) + ([] if after is None else [after])
    in_specs = [a_spec, b_spec] + ([] if res is None else [o_spec]) + ([] if after is None else [ANY])
    return _pcall(
        body, name=name, grid=(m // tm, n // tn, nk), in_specs=in_specs, out_specs=o_spec,
        out_shape=jax.ShapeDtypeStruct((m, n), out_dtype),
        scratch_shapes=[pltpu.VMEM((tm, tn), F32)] if nk > 1 else [],
        compiler_params=_params("parallel", "parallel", "arbitrary"),
    )(*ins)


def _rowcall(fn, rows, consts, outs, accs=(), *, tm, name, after=None):
    s = rows[0][0].shape[0]
    assert s % tm == 0
    n_read, n_out = len(rows) + len(consts), len(outs)
    n_in = n_read + (after is not None)

    def body(*refs):
        vals = fn(*[r[...] for r in refs[:n_read]])
        vals = vals if isinstance(vals, (tuple, list)) else (vals,)
        for o_ref, v in zip(refs[n_in:n_in + n_out], vals[:n_out]):
            o_ref[...] = v.astype(o_ref.dtype)
        if accs:
            first = pl.program_id(0) == 0
            for a_ref, v in zip(refs[n_in + n_out:], vals[n_out:]):
                tot = jnp.sum(v.astype(F32), axis=0, keepdims=True)

                @pl.when(first)
                def _(a_ref=a_ref, tot=tot):
                    a_ref[...] = tot

                @pl.when(jnp.logical_not(first))
                def _(a_ref=a_ref, tot=tot):
                    a_ref[...] += tot

    in_specs = [pl.BlockSpec((tm, w), lambda i, cb=cb: (i, cb)) for (_, cb, w) in rows]
    in_specs += [pl.BlockSpec(c.shape, lambda i: (0, 0)) for c in consts]
    in_specs += [] if after is None else [ANY]
    out_specs = [pl.BlockSpec((tm, w), lambda i: (i, 0)) for (w, _) in outs]
    out_specs += [pl.BlockSpec((1, w), lambda i: (0, 0)) for w in accs]
    out_shape = [jax.ShapeDtypeStruct((s, w), dt) for (w, dt) in outs]
    out_shape += [jax.ShapeDtypeStruct((1, w), F32) for w in accs]
    return _pcall(
        body, name=name, grid=(s // tm,), in_specs=in_specs, out_specs=out_specs, out_shape=out_shape,
        compiler_params=_params("arbitrary" if accs else "parallel"),
    )(*[r[0] for r in rows], *consts, *([] if after is None else [after]))


def _whole(a):
    return (a, 0, a.shape[1])


def _xhat(x):
    x = x.astype(F32)
    r = lax.rsqrt(jnp.mean(x * x, axis=-1, keepdims=True) + RMS_EPS)
    return x * r, r


def _rms_bwd(dy, x, g):
    xh, r = _xhat(x)
    dxh = dy.astype(F32) * g
    dx = r * (dxh - xh * jnp.mean(dxh * xh, axis=-1, keepdims=True))
    return dx, dy.astype(F32) * xh


def _sigmoid(x):
    return 1.0 / (1.0 + jnp.exp(-x))


def _rms_fwd(x, g, name, tm, after=None):
    d = x.shape[1]
    return _rowcall(lambda xb, gb: _xhat(xb)[0] * gb, [_whole(x)], [g], [(d, BF16)], tm=tm, name=name, after=after)[0]


def _silu_parts(gate):
    sg = _sigmoid(gate)
    return sg, gate * sg


def _ffn_up(n, w_gu, name):
    s, d = n.shape
    f = w_gu.shape[1] // 2
    tn = _pick(f, (1408, 1024, 512, 256, 128))
    tm = _pick(s, (1024, 512, 256, 128))
    nb = f // tn

    def body(n_ref, wg_ref, wu_ref, gu_ref, act_ref):
        nv = n_ref[...]
        gate = _dot(nv, wg_ref[...], 1, 0)
        up = _dot(nv, wu_ref[...], 1, 0)
        gu_ref[0] = gate.astype(gu_ref.dtype)
        gu_ref[1] = up.astype(gu_ref.dtype)
        act_ref[...] = (_silu_parts(gate)[1] * up).astype(act_ref.dtype)

    return _pcall(
        body, name=name, grid=(s // tm, nb),
        in_specs=[pl.BlockSpec((tm, d), lambda i, j: (i, 0)), pl.BlockSpec((d, tn), lambda i, j: (0, j)),
                  pl.BlockSpec((d, tn), lambda i, j: (0, nb + j))],
        out_specs=[pl.BlockSpec((2, tm, tn), lambda i, j: (0, i, j)), pl.BlockSpec((tm, tn), lambda i, j: (i, j))],
        out_shape=[jax.ShapeDtypeStruct((2, s, f), BF16), jax.ShapeDtypeStruct((s, f), BF16)],
        compiler_params=_params("parallel", "parallel"),
    )(n, w_gu, w_gu)


def _ffn_dgu(dhb, w_down, gu, name, after=None):
    s, d = dhb.shape
    f = w_down.shape[0]
    tn = _pick(f, (1408, 1024, 512, 256, 128))
    tm = _pick(s, (1024, 512, 256, 128))

    def body(dh_ref, w_ref, gu_ref, *rest):
        o_ref = rest[-1]
        dact = _dot(dh_ref[...], w_ref[...], 1, 1)
        gate, up = gu_ref[0].astype(F32), gu_ref[1].astype(F32)
        sg, silu = _silu_parts(gate)
        o_ref[0] = (dact * up * (sg + silu * (1.0 - sg))).astype(o_ref.dtype)
        o_ref[1] = (dact * silu).astype(o_ref.dtype)

    blk = pl.BlockSpec((2, tm, tn), lambda i, j: (0, i, j))
    return _pcall(
        body, name=name, grid=(s // tm, f // tn),
        in_specs=[pl.BlockSpec((tm, d), lambda i, j: (i, 0)), pl.BlockSpec((tn, d), lambda i, j: (j, 0)), blk]
        + ([] if after is None else [ANY]),
        out_specs=blk, out_shape=jax.ShapeDtypeStruct((2, s, f), BF16), compiler_params=_params("parallel", "parallel"),
    )(dhb, w_down, gu, *([] if after is None else [after]))


def _dgrad_norm(dy, wmat, dh, x, g, name, *, dy_halves=False, copy_scale=None, after=None):
    s, d = dh.shape
    k = wmat.shape[1]
    tk = k if k <= MM_WHOLE_K else _pick(k, (MM_WHOLE_K, 2048, 1024, 512, 256, 128))
    if dy_halves and (k // 2) % tk:
        tk = _pick(k // 2, (1408, 1024, 512, 256, 128))
    tm = _pick(s, (512, 256, 128))
    nk, per = k // tk, (k // 2) // tk if dy_halves else 0
    n_in = 5 + (after is not None)
    n_out = 2 + (copy_scale is not None)

    def body(*refs):
        dy_ref, w_ref, dh_ref, x_ref, g_ref = refs[:5]
        outs, scratch = refs[n_in:n_in + n_out], refs[n_in + n_out:]
        i, kk = pl.program_id(0), pl.program_id(1)
        part = _dot(dy_ref[...], w_ref[...], 1, 1)

        def finish(dn):
            dx, dg = _rms_bwd(dn, x_ref[...], g_ref[...])
            tot = dh_ref[...] + dx
            outs[0][...] = tot
            if copy_scale is not None:
                outs[1][...] = (copy_scale * tot).astype(outs[1].dtype)
            dg = jnp.sum(dg, axis=0, keepdims=True)

            @pl.when(i == 0)
            def _():
                outs[-1][...] = dg

            @pl.when(i > 0)
            def _():
                outs[-1][...] += dg

        if nk == 1:
            finish(part)
        else:
            acc_ref = scratch[0]

            @pl.when(kk == 0)
            def _():
                acc_ref[...] = part

            @pl.when(kk > 0)
            def _():
                acc_ref[...] += part

            @pl.when(kk == nk - 1)
            def _():
                finish(acc_ref[...])

    row = pl.BlockSpec((tm, d), lambda i, kk: (i, 0))
    dy_spec = pl.BlockSpec((None, tm, tk), lambda i, kk: (kk // per, i, kk % per)) if dy_halves else pl.BlockSpec((tm, tk), lambda i, kk: (i, kk))
    in_specs = [dy_spec, pl.BlockSpec((d, tk), lambda i, kk: (0, kk)), row, row, pl.BlockSpec((1, d), lambda i, kk: (0, 0))]
    out_specs = [row] * (n_out - 1) + [pl.BlockSpec((1, d), lambda i, kk: (0, 0))]
    out_shape = [jax.ShapeDtypeStruct((s, d), F32)] + ([] if copy_scale is None else [jax.ShapeDtypeStruct((s, d), BF16)])
    return _pcall(
        body, name=name, grid=(s // tm, nk), in_specs=in_specs + ([] if after is None else [ANY]), out_specs=out_specs,
        out_shape=out_shape + [jax.ShapeDtypeStruct((1, d), F32)], scratch_shapes=[pltpu.VMEM((tm, d), F32)] if nk > 1 else [],
        compiler_params=_params("arbitrary", "arbitrary"),
    )(dy, wmat, dh, x, g, *([] if after is None else [after]))


def _shift_down(p, k):
    if k == 0:
        return p
    rows = lax.broadcasted_iota(jnp.int32, p.shape, 0)
    return jnp.where(rows >= k, pltpu.roll(p, k, 0), 0.0)


def _shift_up(p, k):
    if k == 0:
        return p
    s = p.shape[0]
    rows = lax.broadcasted_iota(jnp.int32, p.shape, 0)
    return jnp.where(rows < s - k, pltpu.roll(p, s - k, 0), 0.0)


def _conv_fwd(proj, conv_w, d, tc, name):
    s = proj.shape[0]
    nb = d // tc

    def body(cb_ref, cc_ref, cx_ref, w_ref, y_ref):
        p = cc_ref[...].astype(F32) * cx_ref[...].astype(F32)
        w = w_ref[...]
        acc = p * w[CONV_K - 1:CONV_K, :]
        for k in range(1, CONV_K):
            acc = acc + _shift_down(p, k) * w[CONV_K - 1 - k:CONV_K - k, :]
        y_ref[...] = (cb_ref[...].astype(F32) * acc).astype(y_ref.dtype)

    col = lambda off: pl.BlockSpec((s, tc), lambda j: (0, off * nb + j))
    return _pcall(
        body, name=name, grid=(nb,), in_specs=[col(0), col(1), col(2), pl.BlockSpec((CONV_K, tc), lambda j: (0, j))],
        out_specs=pl.BlockSpec((s, tc), lambda j: (0, j)), out_shape=jax.ShapeDtypeStruct((s, d), BF16),
        compiler_params=_params("parallel"),
    )(proj, proj, proj, conv_w)


def _conv_bwd(dy, proj, conv_w, d, tc, name):
    s = proj.shape[0]
    nb = d // tc

    def body(dy_ref, cb_ref, cc_ref, cx_ref, w_ref, dcb_ref, dcc_ref, dcx_ref, dw_ref):
        cc, cx = cc_ref[...].astype(F32), cx_ref[...].astype(F32)
        p = cc * cx
        w = w_ref[...]
        dyv = dy_ref[...].astype(F32)
        shifted = [_shift_down(p, CONV_K - 1 - k) for k in range(CONV_K)]
        conv = shifted[0] * w[0:1, :]
        for k in range(1, CONV_K):
            conv = conv + shifted[k] * w[k:k + 1, :]
        dcb_ref[...] = (dyv * conv).astype(dcb_ref.dtype)
        ds = dyv * cb_ref[...].astype(F32)
        dp = ds * w[CONV_K - 1:CONV_K, :]
        for k in range(1, CONV_K):
            dp = dp + _shift_up(ds, k) * w[CONV_K - 1 - k:CONV_K - k, :]
        dcc_ref[...] = (dp * cx).astype(dcc_ref.dtype)
        dcx_ref[...] = (dp * cc).astype(dcx_ref.dtype)
        for k in range(CONV_K):
            dw_ref[k:k + 1, :] = jnp.sum(ds * shifted[k], axis=0, keepdims=True)

    col = lambda off: pl.BlockSpec((s, tc), lambda j: (0, off * nb + j))
    blk = pl.BlockSpec((s, tc), lambda j: (0, j))
    wblk = pl.BlockSpec((CONV_K, tc), lambda j: (0, j))
    act = jax.ShapeDtypeStruct((s, d), BF16)
    return _pcall(
        body, name=name, grid=(nb,), in_specs=[blk, col(0), col(1), col(2), wblk],
        out_specs=[blk, blk, blk, wblk], out_shape=[act, act, act, jax.ShapeDtypeStruct((CONV_K, d), F32)],
        compiler_params=_params("parallel"),
    )(dy, proj, proj, proj, conv_w)


def _sb_tile(q, kj, scale, carry, tri, mask):
    z = _dot(q, kj, 1, 1) * scale
    lsz = jnp.minimum(z, 0.0) - jnp.log(1.0 + jnp.exp(-jnp.abs(z)))
    l1m = lsz - z
    if mask is not None:
        l1m = jnp.where(mask, l1m, 0.0)
    l1b = l1m.astype(BF16)
    a = jnp.exp(lsz + (carry + _dot(l1b, tri, 1, 0)))
    if mask is not None:
        a = jnp.where(mask, a, 0.0)
    return lsz, l1b, a.astype(BF16)


def _add_rows(x, upd, r0):
    return x + upd if r0 == 0 else jnp.concatenate([x[:r0], x[r0:] + upd], axis=0)


def _sb_masks(tq, tk):
    row = lax.broadcasted_iota(jnp.int32, (tq, tk), 0)
    col = lax.broadcasted_iota(jnp.int32, (tq, tk), 1)
    masks = [col + dj * tk < row for dj in range(tq // tk)]
    r2 = lax.broadcasted_iota(jnp.int32, (tk, tk), 0)
    c2 = lax.broadcasted_iota(jnp.int32, (tk, tk), 1)
    return masks, (r2 > c2).astype(BF16), (r2 < c2).astype(BF16)


def _sb_fwd(proj, heads, col0, tq, tk, name):
    s = proj.shape[0]
    dh = SB_HEAD_DIM
    nq, nd, nkt = s // tq, tq // tk, s // tk
    scale = dh ** -0.5

    def body(q_ref, k_ref, v_ref, o_ref, a_ref, b_ref):
        i = pl.program_id(1)
        q = q_ref[...]
        masks, tri_right, _ = _sb_masks(tq, tk)

        def tile(j, carry, acc, mask, r0=0):
            start = pl.multiple_of(j * tk, tk)
            kj = k_ref[pl.ds(start, tk), :]
            vj = v_ref[pl.ds(start, tk), :]
            lsz, l1b, ab = _sb_tile(q[r0:], kj, scale, carry[r0:], tri_right, None if mask is None else mask[r0:])
            a_ref[j, r0:, :] = ab
            b_ref[j, r0:, :] = jnp.exp(lsz).astype(b_ref.dtype)
            if r0:
                a_ref[j, :r0, :] = jnp.zeros((r0, tk), a_ref.dtype)
                b_ref[j, :r0, :] = jnp.zeros((r0, tk), b_ref.dtype)
            return (_add_rows(carry, jnp.sum(l1b.astype(F32), axis=1, keepdims=True), r0),
                    _add_rows(acc, _dot(ab, vj, 1, 0), r0))

        state = (jnp.zeros((tq, 1), F32), jnp.zeros((tq, dh), F32))
        for dj in reversed(range(nd)):
            state = tile(i * nd + dj, *state, masks[dj], dj * tk)
        def left_block(t, st):
            for dj in reversed(range(nd)):
                st = tile((i - 1 - t) * nd + dj, st[0], st[1], None)
            return st

        state = lax.fori_loop(0, i, left_block, state)
        o_ref[...] = state[1]

    qspec = pl.BlockSpec((tq, dh), lambda h, i: (i, col0[0] + h))
    kspec = pl.BlockSpec((s, dh), lambda h, i: (0, col0[1] + h))
    vspec = pl.BlockSpec((s, dh), lambda h, i: (0, col0[2] + h))
    saved = pl.BlockSpec((None, nkt, tq, tk), lambda h, i: (h, 0, i, 0))
    saved_shape = jax.ShapeDtypeStruct((heads, nkt, s, tk), BF16)
    return _pcall(
        body, name=name, grid=(heads, nq), in_specs=[qspec, kspec, vspec],
        out_specs=[pl.BlockSpec((tq, dh), lambda h, i: (i, h)), saved, saved],
        out_shape=[jax.ShapeDtypeStruct((s, heads * dh), F32), saved_shape, saved_shape],
        compiler_params=_params("parallel", "parallel"),
    )(proj, proj, proj)


def _sb_bwd(proj, o, a_all, beta_all, do, heads, col0, tq, tk, name):
    s = proj.shape[0]
    dh = SB_HEAD_DIM
    nq, nd, nkt = s // tq, tq // tk, s // tk
    scale = dh ** -0.5

    def body(q_ref, k_ref, v_ref, o_ref, a_ref, b_ref, do_ref, dq_ref, dk_ref, dv_ref, dk_acc, dv_acc):
        i = pl.program_id(1)

        @pl.when(i == 0)
        def _():
            dk_acc[...] = jnp.zeros_like(dk_acc)
            dv_acc[...] = jnp.zeros_like(dv_acc)

        q = q_ref[...]
        dob = do_ref[...].astype(BF16)
        delta = jnp.sum(dob.astype(F32) * o_ref[...], axis=1, keepdims=True)
        masks, _, tri_left = _sb_masks(tq, tk)

        def tile(j, carry_g, dq, mask):
            start = pl.multiple_of(j * tk, tk)
            kj = k_ref[pl.ds(start, tk), :]
            vj = v_ref[pl.ds(start, tk), :]
            ab = a_ref[j]
            g = _dot(dob, vj, 1, 1) * ab.astype(F32)
            carry_g = carry_g + jnp.sum(g, axis=1, keepdims=True)
            left = (delta - carry_g) + _dot(g.astype(BF16), tri_left, 1, 0)
            dz = g - b_ref[j].astype(F32) * (g + left)
            if mask is not None:
                dz = jnp.where(mask, dz, 0.0)
            dzb = dz.astype(BF16)
            dk_acc[pl.ds(start, tk), :] += _dot(dzb, q, 0, 0)
            dv_acc[pl.ds(start, tk), :] += _dot(ab, dob, 0, 0)
            return carry_g, dq + _dot(dzb, kj, 1, 0)

        state = (jnp.zeros((tq, 1), F32), jnp.zeros((tq, dh), F32))
        for dj in reversed(range(nd)):
            state = tile(i * nd + dj, *state, masks[dj])
        def left_block(t, st):
            for dj in reversed(range(nd)):
                st = tile((i - 1 - t) * nd + dj, st[0], st[1], None)
            return st

        state = lax.fori_loop(0, i, left_block, state)
        dq_ref[...] = (state[1] * scale).astype(dq_ref.dtype)

        @pl.when(i == nq - 1)
        def _():
            dk_ref[...] = (dk_acc[...] * scale).astype(dk_ref.dtype)
            dv_ref[...] = dv_acc[...].astype(dv_ref.dtype)

    qspec = pl.BlockSpec((tq, dh), lambda h, i: (i, col0[0] + h))
    kspec = pl.BlockSpec((s, dh), lambda h, i: (0, col0[1] + h))
    vspec = pl.BlockSpec((s, dh), lambda h, i: (0, col0[2] + h))
    blk = pl.BlockSpec((tq, dh), lambda h, i: (i, h))
    full = pl.BlockSpec((s, dh), lambda h, i: (0, h))
    saved = pl.BlockSpec((None, nkt, tq, tk), lambda h, i: (h, 0, i, 0))
    act = jax.ShapeDtypeStruct((s, heads * dh), BF16)
    return _pcall(
        body, name=name, grid=(heads, nq), in_specs=[qspec, kspec, vspec, blk, saved, saved, blk],
        out_specs=[blk, full, full], out_shape=[act, act, act],
        scratch_shapes=[pltpu.VMEM((s, dh), F32), pltpu.VMEM((s, dh), F32)],
        compiler_params=_params("parallel", "arbitrary"),
    )(proj, proj, proj, o, a_all, beta_all, do)


def _xattn_probs(q, k, scale):
    sc = _dot(q, k, 1, 1) * scale
    e = jnp.exp(sc - jnp.max(sc, axis=1, keepdims=True))
    return e / jnp.sum(e, axis=1, keepdims=True)


def _xattn_fwd(qc, kv, tq, name):
    s, d = qc.shape
    m = kv.shape[0]
    dh = d // X_HEADS
    scale = dh ** -0.5

    def body(q_ref, k_ref, v_ref, o_ref):
        p = _xattn_probs(q_ref[...], k_ref[...], scale)
        o_ref[...] = _dot(p.astype(BF16), v_ref[...], 1, 0).astype(o_ref.dtype)

    blk = pl.BlockSpec((tq, dh), lambda h, i: (i, h))
    return _pcall(
        body, name=name, grid=(X_HEADS, s // tq),
        in_specs=[blk, pl.BlockSpec((m, dh), lambda h, i: (0, h)), pl.BlockSpec((m, dh), lambda h, i: (0, X_HEADS + h))],
        out_specs=blk, out_shape=jax.ShapeDtypeStruct((s, d), BF16), compiler_params=_params("parallel", "parallel"),
    )(qc, kv, kv)


def _xattn_bwd(qc, kv, do, tq, name):
    s, d = qc.shape
    m = kv.shape[0]
    dh = d // X_HEADS
    scale = dh ** -0.5
    nq = s // tq

    def body(q_ref, k_ref, v_ref, do_ref, dq_ref, dk_ref, dv_ref, dk_acc, dv_acc):
        i = pl.program_id(1)
        q, k, v = q_ref[...], k_ref[...], v_ref[...]
        dob = do_ref[...].astype(BF16)
        p = _xattn_probs(q, k, scale)
        pb = p.astype(BF16)
        dp = _dot(dob, v, 1, 1)
        ds = pb.astype(F32) * (dp - jnp.sum(dp * pb.astype(F32), axis=1, keepdims=True))
        dsb = (ds * scale).astype(BF16)
        dq_ref[...] = _dot(dsb, k, 1, 0).astype(dq_ref.dtype)
        dk_part = _dot(dsb, q, 0, 0)
        dv_part = _dot(pb, dob, 0, 0)

        @pl.when(i == 0)
        def _():
            dk_acc[...] = dk_part
            dv_acc[...] = dv_part

        @pl.when(i > 0)
        def _():
            dk_acc[...] += dk_part
            dv_acc[...] += dv_part

        @pl.when(i == nq - 1)
        def _():
            dk_ref[...] = dk_acc[...].astype(dk_ref.dtype)
            dv_ref[...] = dv_acc[...].astype(dv_ref.dtype)

    blk = pl.BlockSpec((tq, dh), lambda h, i: (i, h))
    kblk = pl.BlockSpec((m, dh), lambda h, i: (0, h))
    return _pcall(
        body, name=name, grid=(X_HEADS, nq),
        in_specs=[blk, kblk, pl.BlockSpec((m, dh), lambda h, i: (0, X_HEADS + h)), blk],
        out_specs=[blk, kblk, kblk],
        out_shape=[jax.ShapeDtypeStruct((s, d), BF16), jax.ShapeDtypeStruct((m, d), BF16), jax.ShapeDtypeStruct((m, d), BF16)],
        scratch_shapes=[pltpu.VMEM((m, dh), F32), pltpu.VMEM((m, dh), F32)],
        compiler_params=_params("parallel", "arbitrary"),
    )(qc, kv, kv, do)


def _local_step(x, mem, tgt, w, fetch=None, prefetch=None, emit=None, tick=None, after=None):
    fetch = fetch or (lambda name, after: {})
    prefetch = prefetch or (lambda name, after: None)
    emit = emit or (lambda group, g: None)
    tick = tick or (lambda group, after: None)
    w = dict(w)
    s, d = x.shape
    heads = d // SB_HEAD_DIM
    tm = _pick(s, (512, 256, 128))
    tq = _pick(s, (1024, 512, 256, 128))
    sb_tq, sb_tk = _pick(s, (512, 256, 128)), _pick(s, (256, 128))
    tc = _pick(d, (256, 128))
    g = {}

    def wt(name, after):
        if name not in w:
            w.update(fetch(name, after))
        return w[name]

    def ffn_fwd(h, gname, wgu, wdown, tag, after=None):
        n = _rms_fwd(h, w[gname], tag + "_norm", tm, after=after)
        gu, act = _ffn_up(n, wt(wgu, n), tag + "_gu")
        prefetch(wdown, gu)
        return n, gu, act, _mm(act, wt(wdown, act), name=tag + "_down", out_dtype=F32, res=h, alpha=0.5)

    def ffn_bwd(dh, dhb, h, saved, gname, wgu, wdown, tag, copy_scale=None, after=None):
        n, gu, act = saved
        g[wdown] = _mm(act, dhb, ta=True, name=tag + "_dwdown", after=after)
        dgu = _ffn_dgu(dhb, w[wdown], gu, tag + "_dgu", after=emit(tag + "_down", g))
        g[wgu] = _mm(n, dgu, ta=True, b_halves=True, name=tag + "_dwgu", after=tick(tag + "_down", dgu))
        *dh_in, g[gname] = _dgrad_norm(dgu, w[wgu], dh, h, w[gname], tag + "_dn", dy_halves=True, copy_scale=copy_scale,
                                       after=emit(tag, g))
        return dh_in, tick(tag, dh_in[0])

    n1, gu1, act1, h1 = ffn_fwd(x, "g_ffn1", "w_ffn1_gu", "w_ffn1_down", "ffn1", after)
    prefetch("w_in", h1)
    u = _rms_fwd(h1, w["g_mix"], "mix_norm", tm)
    proj = _mm(u, wt("w_in", u), name="mix_in")
    prefetch("w_conv_out", proj)
    nd = d // SB_HEAD_DIM
    y_conv = _conv_fwd(proj, w["conv_w"], d, tc, "conv_fwd")
    sb_cols = (3 * nd, 4 * nd, 5 * nd)
    y_sb, sb_a, sb_beta = _sb_fwd(proj, heads, sb_cols, sb_tq, sb_tk, "sb_fwd")
    prefetch("w_cq", y_sb)
    a_conv = _mm(y_conv, wt("w_conv_out", y_conv), name="conv_out")
    a_sb = _mm(y_sb, wt("w_attn_out", y_sb), name="attn_out")
    b_conv, b_sb = w["b_gate"][:, :d], w["b_gate"][:, d:]

    def merge(ac, asb, gcp, gsp, bc, bs):
        gc = _sigmoid(gcp.astype(F32) + bc)
        gs = _sigmoid(gsp.astype(F32) + bs)
        return gc * ac.astype(F32) + gs * asb.astype(F32)

    merged = _rowcall(merge, [_whole(a_conv), _whole(a_sb), (proj, 6, d), (proj, 7, d)], [b_conv, b_sb], [(d, BF16)],
                      tm=tm, name="merge")[0]
    prefetch("w_ffn2_gu", merged)
    h2 = _mm(merged, wt("w_o", merged), name="mix_out", out_dtype=F32, res=h1)
    hn = _rms_fwd(h2, w["g_cross"], "cross_norm", tm)
    mn = _rms_fwd(mem, w["g_mem"], "mem_norm", _pick(mem.shape[0], (256, 128)))
    qc = _mm(hn, wt("w_cq", hn), name="cross_q")
    kv = _mm(mn, wt("w_ckv", mn), name="cross_kv")
    oc = _xattn_fwd(qc, kv, tq, "xattn_fwd")
    h3 = _mm(oc, wt("w_co", oc), name="cross_out", out_dtype=F32, res=h2)
    n2, gu2, act2, h4 = ffn_fwd(h3, "g_ffn2", "w_ffn2_gu", "w_ffn2_down", "ffn2")

    def head(hb, tb, gb):
        xh, r = _xhat(hb)
        err = xh * gb - tb
        dy = err * (1.0 / d)
        dxh = dy * gb
        dx = r * (dxh - xh * jnp.mean(dxh * xh, axis=-1, keepdims=True))
        row_loss = 0.5 * jnp.mean(err * err, axis=-1, keepdims=True)
        return dx, 0.5 * dx, dy * xh, jnp.broadcast_to(row_loss, (row_loss.shape[0], LANES))

    dh4, dh4b, g["g_final"], loss_lanes = _rowcall(head, [_whole(h4), _whole(tgt)], [w["g_final"]], [(d, F32), (d, BF16)],
                                                   [d, LANES], tm=tm, name="loss_head")

    (dh3, dh3b), tok = ffn_bwd(dh4, dh4b, h3, (n2, gu2, act2), "g_ffn2", "w_ffn2_gu", "w_ffn2_down", "ffn2", copy_scale=1.0)
    g["w_co"] = _mm(oc, dh3b, ta=True, name="cross_dwco", after=tok)
    doc = _mm(dh3b, w["w_co"], tb=True, name="cross_doc")
    dqc, dk, dv = _xattn_bwd(qc, kv, doc, tq, "xattn_bwd")
    dkv = jnp.concatenate([dk, dv], axis=1)
    g["w_cq"] = _mm(hn, dqc, ta=True, name="cross_dwcq")
    g["w_ckv"] = _mm(mn, dkv, ta=True, name="cross_dwckv")
    dmn = _mm(dkv, w["w_ckv"], tb=True, name="cross_dmn", out_dtype=F32)
    g["g_mem"] = _rowcall(lambda dy, xb: dy * _xhat(xb)[0], [_whole(dmn), _whole(mem)], [], [], [d],
                          tm=_pick(mem.shape[0], (256, 128)), name="mem_dnorm")[0]
    dh2, dh2b, g["g_cross"] = _dgrad_norm(dqc, w["w_cq"], dh3, h2, w["g_cross"], "cross_dhn", copy_scale=1.0, after=emit("cross", g))

    g["w_o"] = _mm(merged, dh2b, ta=True, name="mix_dwo", after=tick("cross", dh2))
    dmerged = _mm(dh2b, w["w_o"], tb=True, name="mix_dmerged")

    def merge_bwd(dm, ac, asb, gcp, gsp, bc, bs):
        dm, ac, asb = dm.astype(F32), ac.astype(F32), asb.astype(F32)
        gc = _sigmoid(gcp.astype(F32) + bc)
        gs = _sigmoid(gsp.astype(F32) + bs)
        dgc = dm * ac * gc * (1.0 - gc)
        dgs = dm * asb * gs * (1.0 - gs)
        return dm * gc, dm * gs, dgc, dgs, dgc, dgs

    da_conv, da_sb, dgc, dgs, db_conv, db_sb = _rowcall(
        merge_bwd, [_whole(dmerged), _whole(a_conv), _whole(a_sb), (proj, 6, d), (proj, 7, d)], [b_conv, b_sb],
        [(d, BF16)] * 4, [d, d], tm=tm, name="merge_bwd")
    g["b_gate"] = jnp.concatenate([db_conv, db_sb], axis=1)
    g["w_conv_out"] = _mm(y_conv, da_conv, ta=True, name="conv_dwout")
    g["w_attn_out"] = _mm(y_sb, da_sb, ta=True, name="attn_dwout")
    dy_conv = _mm(da_conv, w["w_conv_out"], tb=True, name="conv_dy")
    dy_sb = _mm(da_sb, w["w_attn_out"], tb=True, name="attn_dy")
    dcb, dcc, dcx, g["conv_w"] = _conv_bwd(dy_conv, proj, w["conv_w"], d, tc, "conv_bwd")
    dq, dk_sb, dv_sb = _sb_bwd(proj, y_sb, sb_a, sb_beta, dy_sb, heads, sb_cols, sb_tq, sb_tk, "sb_bwd")
    dproj = jnp.concatenate([dcb, dcc, dcx, dq, dk_sb, dv_sb, dgc, dgs], axis=1)
    g["w_in"] = _mm(u, dproj, ta=True, name="mix_dwin")
    dh1, dh1b, g["g_mix"] = _dgrad_norm(dproj, w["w_in"], dh2, h1, w["g_mix"], "mix_du", copy_scale=0.5, after=emit("mix", g))
    (dx,), tok = ffn_bwd(dh1, dh1b, x, (n1, gu1, act1), "g_ffn1", "w_ffn1_gu", "w_ffn1_down", "ffn1", after=tick("mix", dh1))
    return loss_lanes, dx, g, tok


MATS = (("w_ffn1_gu", "col"), ("w_ffn1_down", "row"), ("w_in", "col"), ("w_conv_out", "row"), ("w_attn_out", "row"),
        ("w_o", "row"), ("w_cq", "row"), ("w_ckv", "col"), ("w_co", "row"), ("w_ffn2_gu", "col"), ("w_ffn2_down", "row"))
VECS = ("g_ffn1", "g_mix", "g_cross", "g_mem", "g_ffn2", "g_final")
WEIGHTS = ("g_ffn1", "w_ffn1_gu", "w_ffn1_down", "g_mix", "w_in", "b_gate", "conv_w", "w_conv_out", "w_attn_out", "w_o",
           "g_cross", "g_mem", "w_cq", "w_ckv", "w_co", "g_ffn2", "w_ffn2_gu", "w_ffn2_down", "g_final")
CONV_ROWS = 16


def _full_shape(kind, r, c):
    return (r, N_CHIPS * c) if kind == "col" else (N_CHIPS * r, c)


def _piece(ref, kind, r, c, chip, half):
    hr = r // 2
    if kind == "col":
        return ref.at[pl.ds(pl.multiple_of(half * hr, math.gcd(hr, 16)), hr), pl.ds(pl.multiple_of(chip * c, LANES), c)]
    return ref.at[pl.ds(pl.multiple_of(chip * r + half * hr, math.gcd(hr, 16)), hr), :]


def _shard_of(ref, kind, r, c, chip):
    if kind == "col":
        return ref.at[:, pl.ds(pl.multiple_of(chip * c, LANES), c)]
    return ref.at[pl.ds(pl.multiple_of(chip * r, 16), r), :]


def _place():
    x, y, c = lax.axis_index("x"), lax.axis_index("y"), lax.axis_index("c")
    others = [(1 - x, y), (x, 1 - y), (1 - x, 1 - y)]
    return x, y, c, 2 * x + y, others


def _remote(src, dst, send_sem, recv_sem, to):
    return pltpu.make_async_remote_copy(src_ref=src, dst_ref=dst, send_sem=send_sem, recv_sem=recv_sem,
                                        device_id=to, device_id_type=MESH)


HBM = pl.BlockSpec(memory_space=pltpu.HBM)
SEM = pl.BlockSpec(memory_space=pltpu.SEMAPHORE)
EFFECT = pltpu.SideEffectType.DATAFLOW_SIDE_EFFECTING
TOKEN = (8, LANES)


def _split_start(name, plan, n_copies, srcs, lands, after=None):
    ns, nl = len(srcs), len(lands)
    n_in = ns + nl + (after is not None)

    def body(*refs):
        outs = refs[n_in:]
        sends, _ = plan(refs[:ns], refs[ns:ns + nl], outs[0], outs[1])
        for cp in sends:
            cp.start()
        outs[-1][...] = jnp.zeros(TOKEN, F32)

    held = [pltpu.HBM(a.shape, a.dtype) for a in (*srcs, *lands)]
    dma = pltpu.SemaphoreType.DMA((n_copies,))
    ins = [pltpu.with_memory_space_constraint(a, pltpu.HBM) for a in (*srcs, *lands)]
    outs = _pcall(
        body, name=name, in_specs=[HBM] * (ns + nl) + ([] if after is None else [ANY]),
        out_specs=(SEM, SEM, *[HBM] * (ns + nl), pl.BlockSpec(memory_space=pltpu.VMEM)),
        out_shape=(dma, dma, *held, jax.ShapeDtypeStruct(TOKEN, F32)),
        input_output_aliases={i: 2 + i for i in range(ns + nl)},
        compiler_params=pltpu.CompilerParams(has_side_effects=EFFECT),
    )(*ins, *([] if after is None else [after]))
    return outs[0], outs[1], list(outs[2:2 + ns]), list(outs[2 + ns:2 + ns + nl]), outs[-1]


def _split_wait(name, plan, send_sems, recv_sems, srcs, lands, after):
    ns, nl = len(srcs), len(lands)

    def body(*refs):
        sends, recvs = plan(refs[:ns], refs[ns:ns + nl], refs[ns + nl], refs[ns + nl + 1])
        for cp in sends:
            cp.wait_send()
        for cp in recvs:
            cp.wait_recv()

    outs = _pcall(
        body, name=name, in_specs=[HBM] * (ns + nl) + [SEM, SEM, ANY], out_specs=[HBM] * (ns + nl),
        out_shape=[pltpu.HBM(a.shape, a.dtype) for a in (*srcs, *lands)],
        input_output_aliases={i: i for i in range(ns + nl)},
        compiler_params=pltpu.CompilerParams(has_side_effects=EFFECT),
    )(*srcs, *lands, send_sems, recv_sems, after)
    return list(outs[:ns]), list(outs[ns:])


def _gather_plan(dims):
    def plan(shard_refs, full_refs, ss, rs):
        x, y, c, me, others = _place()
        sends, recvs = [], []
        for wi, (kind, r, cw) in enumerate(dims):
            half = shard_refs[wi].at[pl.ds(pl.multiple_of(c * (r // 2), math.gcd(r // 2, 16)), r // 2), :]
            for k, (ox, oy) in enumerate(others):
                sem = 4 * wi + k
                sends.append(_remote(half, _piece(full_refs[wi], kind, r, cw, me, c), ss.at[sem], rs.at[sem], (ox, oy, c)))
                recvs.append(_remote(half, _piece(full_refs[wi], kind, r, cw, 2 * ox + oy, c), ss.at[sem], rs.at[sem], (x, y, c)))
            sem = 4 * wi + 3
            own = _remote(shard_refs[wi], _shard_of(full_refs[wi], kind, r, cw, me), ss.at[sem], rs.at[sem], (x, y, 1 - c))
            sends.append(own)
            recvs.append(own)
        return sends, recvs

    return plan


def _forward_plan(dims):
    def plan(_, full_refs, ss, rs):
        x, y, c, _, others = _place()
        sends, recvs = [], []
        for wi, (kind, r, cw) in enumerate(dims):
            for k, (ox, oy) in enumerate(others):
                sem = 3 * wi + k
                mine = _piece(full_refs[wi], kind, r, cw, 2 * ox + oy, c)
                theirs = _piece(full_refs[wi], kind, r, cw, 2 * ox + oy, 1 - c)
                sends.append(_remote(mine, mine, ss.at[sem], rs.at[sem], (x, y, 1 - c)))
                recvs.append(_remote(theirs, theirs, ss.at[sem], rs.at[sem], (x, y, 1 - c)))
        return sends, recvs

    return plan


def _rs_cores_plan(dims):
    def plan(g_refs, land_refs, ss, rs):
        x, y, c, _, _ = _place()
        sends, recvs = [], []
        for wi, dm in enumerate(dims):
            for chip in range(N_CHIPS):
                sem = N_CHIPS * wi + chip
                sends.append(_remote(_piece(g_refs[wi], *dm, chip, 1 - c), land_refs[wi].at[chip], ss.at[sem], rs.at[sem], (x, y, 1 - c)))
                recvs.append(_remote(_piece(g_refs[wi], *dm, chip, c), land_refs[wi].at[chip], ss.at[sem], rs.at[sem], (x, y, 1 - c)))
        return sends, recvs

    return plan


def _share_plan(nw):
    def plan(_, buf_refs, ss, rs):
        x, y, c, _, _ = _place()
        sends = [_remote(buf_refs[wi].at[c], buf_refs[wi].at[c], ss.at[wi], rs.at[wi], (x, y, 1 - c)) for wi in range(nw)]
        recvs = [_remote(buf_refs[wi].at[1 - c], buf_refs[wi].at[1 - c], ss.at[wi], rs.at[wi], (x, y, 1 - c)) for wi in range(nw)]
        return sends, recvs

    return plan


def _small_plan():
    def plan(_, buf_refs, ss, rs):
        x, y, c = lax.axis_index("x"), lax.axis_index("y"), lax.axis_index("c")
        buf = buf_refs[0]
        sends, recvs = [], []
        for rel in range(1, N_DEV):
            peer = (x ^ (rel >> 2 & 1), y ^ (rel >> 1 & 1), c ^ (rel & 1))
            sends.append(_remote(buf.at[0], buf.at[rel], ss.at[rel - 1], rs.at[rel - 1], peer))
            recvs.append(_remote(buf.at[0], buf.at[rel], ss.at[rel - 1], rs.at[rel - 1], peer))
        return sends, recvs

    return plan


def _sum_small(buf, me, name):
    _, rows, n = buf.shape

    def body(me_ref, b_ref, o_ref):
        tot = b_ref[me_ref[0]]
        for dev in range(1, N_DEV):
            tot = tot + b_ref[dev ^ me_ref[0]]
        o_ref[...] = tot

    return _pcall(
        body, name=name, out_shape=jax.ShapeDtypeStruct((rows, n), F32),
        grid_spec=pltpu.PrefetchScalarGridSpec(
            num_scalar_prefetch=1, grid=(1,), in_specs=[pl.BlockSpec((N_DEV, rows, n), lambda i, m: (0, 0, 0))],
            out_specs=pl.BlockSpec((rows, n), lambda i, m: (0, 0))),
    )(me, buf)


def _rs_chips_plan(nw):
    def plan(p_refs, land_refs, ss, rs):
        x, y, c, me, others = _place()
        sends, recvs = [], []
        for wi in range(nw):
            for k, (ox, oy) in enumerate(others):
                sem = 3 * wi + k
                sends.append(_remote(p_refs[wi].at[2 * ox + oy], land_refs[wi].at[k], ss.at[sem], rs.at[sem], (ox, oy, c)))
                recvs.append(_remote(p_refs[wi].at[me], land_refs[wi].at[k], ss.at[sem], rs.at[sem], (x, y, c)))
        return sends, recvs

    return plan


def _rows_per_block(n, c, limit_bytes=2 << 20):
    best = None
    for tm in range(16, n + 1, 16):
        if n % tm == 0 and tm * c * 4 <= limit_bytes:
            best = tm
    return best or n


def _sum_cores(grad, got, kind, place, name):
    _, hr, cw = got.shape
    tm = _rows_per_block(hr, cw)
    nb = hr // tm

    def body(place_ref, g_ref, t_ref, o_ref):
        o_ref[...] = (g_ref[...].astype(F32) + t_ref[...].astype(F32)).astype(o_ref.dtype)

    if kind == "col":
        g_spec = pl.BlockSpec((tm, cw), lambda j, i, pr: (pr[0] * nb + i, j))
    else:
        g_spec = pl.BlockSpec((tm, cw), lambda j, i, pr: ((2 * j + pr[0]) * nb + i, 0))
    blk = pl.BlockSpec((None, tm, cw), lambda j, i, pr: (j, i, 0))
    return _pcall(
        body, name=name, out_shape=jax.ShapeDtypeStruct(got.shape, BF16),
        grid_spec=pltpu.PrefetchScalarGridSpec(num_scalar_prefetch=1, grid=(N_CHIPS, nb), in_specs=[g_spec, blk], out_specs=blk),
        compiler_params=_params("parallel", "parallel"),
    )(place, grad, got)


def _sum_chips(parts, got, place, name):
    _, n, cw = got.shape
    tm = _rows_per_block(n, cw)

    def body(place_ref, p_ref, g_ref, o_ref):
        tot = p_ref[...].astype(F32)
        for k in range(3):
            tot = tot + g_ref[k].astype(F32)
        o_ref[...] = tot

    return _pcall(
        body, name=name, out_shape=jax.ShapeDtypeStruct((2, n, cw), F32),
        grid_spec=pltpu.PrefetchScalarGridSpec(
            num_scalar_prefetch=1, grid=(n // tm,),
            in_specs=[pl.BlockSpec((None, tm, cw), lambda i, pr: (pr[1], i, 0)), pl.BlockSpec((3, tm, cw), lambda i, pr: (0, i, 0))],
            out_specs=pl.BlockSpec((None, tm, cw), lambda i, pr: (pr[0], i, 0))),
        compiler_params=_params("parallel"),
    )(place, parts, got)


def _adamw(g, w, m, v, name):
    n, c = g.shape
    c1 = 1.0 - ADAM_B1 ** ADAM_STEP
    c2 = 1.0 - ADAM_B2 ** ADAM_STEP

    def fn(gb, wb, mb, vb):
        m_new = ADAM_B1 * mb + (1.0 - ADAM_B1) * gb
        v_new = ADAM_B2 * vb + (1.0 - ADAM_B2) * (gb * gb)
        delta = -ADAM_LR * ((m_new / c1) / (jnp.sqrt(v_new / c2) + ADAM_EPS) + ADAM_WD * wb)
        return gb, delta, m_new, v_new

    tm = _rows_per_block(n, c) if n % 16 == 0 else n
    return _rowcall(fn, [_whole(g), _whole(w), _whole(m), _whole(v)], [], [(c, F32)] * 4, tm=tm, name=name)


PACK_ROWS = 16


def _pack_rows(parts, width, name, after=None):
    assert sum(p.shape[0] for p in parts) <= PACK_ROWS

    def body(*refs):
        out_ref = refs[-1]
        out_ref[...] = jnp.zeros_like(out_ref)
        at = 0
        for r in refs[:len(parts)]:
            k, n = r.shape
            if n == width:
                out_ref[at:at + k, :] = r[...]
            else:
                out_ref[at:at + k, :] = jnp.broadcast_to(r[:, :1], (k, width))
            at += k

    vm = pl.BlockSpec(memory_space=pltpu.VMEM)
    return _pcall(body, name=name, in_specs=[vm] * len(parts) + ([] if after is None else [ANY]), out_specs=vm,
                  out_shape=jax.ShapeDtypeStruct((PACK_ROWS, width), F32))(*parts, *([] if after is None else [after]))


def _cast_shard(wm, name, after):
    n, c = wm.shape
    return _rowcall(lambda v: v, [_whole(wm)], [], [(c, BF16)], tm=_rows_per_block(n, c), name=name, after=after)[0]


GATHER_GROUPS = (
    ("w_ffn1_gu", "conv_w"), ("w_ffn1_down",), ("w_in",), ("w_conv_out", "w_attn_out", "w_o"), ("w_cq", "w_ckv", "w_co"),
    ("w_ffn2_gu", "w_ffn2_down"),
)
REDUCE_GROUPS = {
    "ffn2": ("w_ffn2_down", "w_ffn2_gu"),
    "cross": ("w_co", "w_cq", "w_ckv"),
    "mix": ("w_o", "w_conv_out", "w_attn_out", "w_in"),
    "ffn1_down": ("w_ffn1_down",),
    "ffn1": ("w_ffn1_gu",),
}
TAIL_STAGES = (("ffn2", "cross"), ("mix",), ("ffn1_down", "ffn1"))
KIND = dict(MATS)


def _step(x, mem, tgt, wts, m_in, v_in):
    d = x.shape[-1]
    cc = wts["conv_w"].shape[1]
    place = jnp.stack([lax.axis_index("c"), 2 * lax.axis_index("x") + lax.axis_index("y")]).astype(jnp.int32)
    dims = {n: (kind, *wts[n].shape) for n, kind in MATS}
    dims["conv_w"] = ("col", CONV_ROWS, cc)

    w = {n: wts[n].reshape(1, -1) for n in VECS + ("b_gate",)}
    flying, token = {}, None
    for names in GATHER_GROUPS:
        gd = [dims[n] for n in names]
        shards = [jnp.pad(wts[n], ((0, CONV_ROWS - CONV_K), (0, 0))) if n == "conv_w" else _cast_shard(wts[n], "cast_" + n, token)
                  for n in names]
        lands = [lax.empty(_full_shape(*dm), sh.dtype) for dm, sh in zip(gd, shards)]
        plan = _gather_plan(gd)
        ss, rs, srcs, lands, token = _split_start("gather_start_" + names[0], plan, 4 * len(names), shards, lands, token)
        flying.update({n: (names, plan, ss, rs, srcs, lands, gd) for n in names})

    passing = {}

    def prefetch(name, after):
        if name not in passing:
            names, plan, ss, rs, srcs, lands, gd = flying[name]
            _, lands = _split_wait("gather_wait_" + names[0], plan, ss, rs, srcs, lands, after)
            plan = _forward_plan(gd)
            ss, rs, _, lands, _ = _split_start("forward_start_" + names[0], plan, 3 * len(names), [], lands)
            passing.update({n: (names, plan, ss, rs, lands) for n in names})

    def fetch(name, after):
        prefetch(name, after)
        names, plan, ss, rs, lands = passing[name]
        _, lands = _split_wait("forward_wait_" + names[0], plan, ss, rs, [], lands, after)
        return {n: (land[:CONV_K] if n == "conv_w" else land) for n, land in zip(names, lands)}

    swapping, sent = {}, {}

    def emit(tag, g):
        if tag not in REDUCE_GROUPS:
            return None
        names = REDUCE_GROUPS[tag]
        gd = [dims[n] for n in names]
        lands = [lax.empty((N_CHIPS, r // 2, cw), BF16) for (_, r, cw) in gd]
        plan = _rs_cores_plan(gd)
        ss, rs, srcs, lands, tok = _split_start("rs_cores_start_" + tag, plan, N_CHIPS * len(names), [g[n] for n in names], lands)
        swapping[tag] = (plan, ss, rs, srcs, lands)
        return tok

    def tick(tag, after):
        if tag not in REDUCE_GROUPS:
            return None
        names = REDUCE_GROUPS[tag]
        plan, ss, rs, srcs, lands = swapping[tag]
        mine, got = _split_wait("rs_cores_wait_" + tag, plan, ss, rs, srcs, lands, after)
        parts = [_sum_cores(gm, t, KIND[n], place, "sum_cores_" + n) for n, gm, t in zip(names, mine, got)]
        lands = [lax.empty((3, *p.shape[1:]), BF16) for p in parts]
        plan = _rs_chips_plan(len(names))
        ss, rs, srcs, lands, tok = _split_start("rs_chips_start_" + tag, plan, 3 * len(names), parts, lands)
        sent[tag] = (plan, ss, rs, srcs, lands)
        return tok

    loss_lanes, dx, g, last = _local_step(x[0], mem[0], tgt[0], w, fetch, prefetch, emit, tick, token)

    rows = [g[n] for n in VECS] + [g["b_gate"][:, :d], g["b_gate"][:, d:], g["conv_w"], loss_lanes]
    packed = _pack_rows(rows, d, "pack_small", after=last)
    small = jnp.concatenate([packed[None], jnp.zeros((N_DEV - 1, *packed.shape), F32)], axis=0)
    small_plan = _small_plan()
    small_ss, small_rs, _, small, after = _split_start("small_start", small_plan, N_DEV - 1, [], [small])

    grads, out = {}, {}

    def update(n):
        shape = wts[n].shape
        as2d = (lambda a: a.reshape(1, -1)) if len(shape) == 1 else (lambda a: a)
        return [r.reshape(shape) for r in _adamw(grads[n], as2d(wts[n]), as2d(m_in[n]), as2d(v_in[n]), "adamw_" + n)]

    def finish(sharing, after):
        tag, names, plan, ss, rs, halves = sharing
        _, both = _split_wait("share_wait_" + tag, plan, ss, rs, [], halves, after)
        for n, b in zip(names, both):
            grads[n] = b.reshape(-1, b.shape[-1])
            out[n] = update(n)
        return out[names[-1]][1]

    sharing = None
    for stage in TAIL_STAGES:
        names, halves = [], []
        for tag in stage:
            plan, ss, rs, srcs, lands = sent[tag]
            parts, landed = _split_wait("rs_chips_wait_" + tag, plan, ss, rs, srcs, lands, after)
            halves += [_sum_chips(p, t, place, "sum_chips_" + n) for n, p, t in zip(REDUCE_GROUPS[tag], parts, landed)]
            names += REDUCE_GROUPS[tag]
        plan = _share_plan(len(names))
        ss, rs, _, halves, after = _split_start("share_start_" + stage[0], plan, len(names), [], halves)
        if sharing is not None:
            after = finish(sharing, after)
        sharing = (stage[0], names, plan, ss, rs, halves)
    after = finish(sharing, after)

    _, small = _split_wait("small_wait", small_plan, small_ss, small_rs, [], small, after)
    me = (4 * lax.axis_index("x") + 2 * lax.axis_index("y") + lax.axis_index("c")).astype(jnp.int32).reshape(1)
    red = _sum_small(small[0], me, "sum_small")
    grads.update({n: red[i:i + 1] for i, n in enumerate(VECS)})
    nv = len(VECS)
    grads["b_gate"] = jnp.concatenate([red[nv:nv + 1], red[nv + 1:nv + 2]], axis=1)
    chip = 2 * lax.axis_index("x") + lax.axis_index("y")
    grads["conv_w"] = lax.dynamic_slice_in_dim(red[nv + 2:nv + 2 + CONV_K], chip * cc, cc, axis=1)
    loss = red[nv + 2 + CONV_K, 0]
    out.update({n: update(n) for n in WEIGHTS if n not in KIND})
    return (loss, dx[None], *[out[n][0] for n in WEIGHTS], *[out[n][1] for n in WEIGHTS],
            *[out[n][2] for n in WEIGHTS], *[out[n][3] for n in WEIGHTS])


def kernel(x, mem, g_ffn1, w_ffn1_gu, w_ffn1_down, g_mix, w_in, b_gate, conv_w, w_conv_out, w_attn_out, w_o, g_cross, g_mem, w_cq, w_ckv, w_co, g_ffn2, w_ffn2_gu, w_ffn2_down, g_final, loss_target, m_g_ffn1, m_w_ffn1_gu, m_w_ffn1_down, m_g_mix, m_w_in, m_b_gate, m_conv_w, m_w_conv_out, m_w_attn_out, m_w_o, m_g_cross, m_g_mem, m_w_cq, m_w_ckv, m_w_co, m_g_ffn2, m_w_ffn2_gu, m_w_ffn2_down, m_g_final, v_g_ffn1, v_w_ffn1_gu, v_w_ffn1_down, v_g_mix, v_w_in, v_b_gate, v_conv_w, v_w_conv_out, v_w_attn_out, v_w_o, v_g_cross, v_g_mem, v_w_cq, v_w_ckv, v_w_co, v_g_ffn2, v_w_ffn2_gu, v_w_ffn2_down, v_g_final):
    given = dict(locals())
    wts = {n: given[n] for n in WEIGHTS}
    m_in = {n: given["m_" + n] for n in WEIGHTS}
    v_in = {n: given["v_" + n] for n in WEIGHTS}
    return _step(x, mem, loss_target, wts, m_in, v_in)
```

```python
import math

import jax
import jax.numpy as jnp
from jax import lax
from jax.experimental import pallas as pl
from jax.experimental.pallas import tpu as pltpu

F32 = jnp.float32
BF16 = jnp.bfloat16
MESH = pl.DeviceIdType.MESH

V7X_VMEM_LIMIT_BYTES = 48 * 1024 * 1024
MM_VMEM_BUDGET_BYTES = 36 * 1024 * 1024
MM_WHOLE_K = 2816
LANES = 128
SB_HEAD_DIM = 128
X_HEADS = 4
CONV_K = 3
RMS_EPS = 1e-6
N_CHIPS = 4
N_DEV = 8
ADAM_LR, ADAM_B1, ADAM_B2, ADAM_EPS, ADAM_WD, ADAM_STEP = 0.001, 0.9, 0.999, 1e-08, 0.01, 10


ANY = pl.BlockSpec(memory_space=pl.ANY)


def _pcall(body, **kw):
    return pl.pallas_call(body, **kw)


def _params(*sem):
    return pltpu.CompilerParams(dimension_semantics=sem, vmem_limit_bytes=V7X_VMEM_LIMIT_BYTES)


def _pick(dim, cands):
    for c in cands:
        if dim % c == 0:
            return c
    return dim


def _dot(a, b, ca, cb):
    return lax.dot_general(a, b, (((ca,), (cb,)), ((), ())), preferred_element_type=F32)


def _mm(a, b, *, name, ta=False, tb=False, out_dtype=BF16, res=None, alpha=1.0, tm=None, tn=None, tk=None, after=None,
        a_halves=False, b_halves=False):
    assert not (a_halves and ta) and not (b_halves and tb)
    if a_halves:
        m, k = a.shape[1], 2 * a.shape[2]
    else:
        m, k = (a.shape[1], a.shape[0]) if ta else a.shape
    if b_halves:
        n = 2 * b.shape[2]
        assert k == b.shape[1]
    else:
        n = b.shape[0] if tb else b.shape[1]
        assert k == (b.shape[1] if tb else b.shape[0]), (a.shape, b.shape, ta, tb)
    if ta:
        tm = tm or _pick(m, (512, 256, 128))
        tn = tn or _pick(n, (1024, 512, 256, 128))
        tk = tk or (k if k <= MM_WHOLE_K else _pick(k, (1024, 512, 256, 128)))
    else:
        tk = tk or (k if k <= MM_WHOLE_K else _pick(k, (MM_WHOLE_K, 2048, 1024, 512, 256, 128)))
        tn = tn or _pick(n, (512, 1408, 256, 128) if tk == k else (1024, 512, 256, 128))
        per_row = 2 * (tk * a.dtype.itemsize + tn * (jnp.dtype(out_dtype).itemsize + (0 if res is None else res.dtype.itemsize)))
        per_row += 4 * tn if tk < k else 0
        rows = (MM_VMEM_BUDGET_BYTES - 2 * tk * tn * b.dtype.itemsize) // per_row
        tm = tm or next((c for c in (2048, 1024, 512, 256, 128) if m % c == 0 and c <= rows), m)
    if a_halves:
        tk = min(tk, k // 2) if (k // 2) % min(tk, k // 2) == 0 else _pick(k // 2, (1408, 1024, 512, 256, 128))
    if b_halves:
        tn = tn if (n // 2) % tn == 0 else _pick(n // 2, (1408, 1024, 512, 256, 128))
    nk = k // tk
    assert m % tm == 0 and n % tn == 0 and k % tk == 0
    a_spec = pl.BlockSpec((tk, tm), lambda i, j, kk: (kk, i)) if ta else pl.BlockSpec((tm, tk), lambda i, j, kk: (i, kk))
    b_spec = pl.BlockSpec((tn, tk), lambda i, j, kk: (j, kk)) if tb else pl.BlockSpec((tk, tn), lambda i, j, kk: (kk, j))
    if a_halves:
        per = (k // 2) // tk
        a_spec = pl.BlockSpec((None, tm, tk), lambda i, j, kk: (kk // per, i, kk % per))
    if b_halves:
        per_n = (n // 2) // tn
        b_spec = pl.BlockSpec((None, tk, tn), lambda i, j, kk: (j // per_n, kk, j % per_n))
    o_spec = pl.BlockSpec((tm, tn), lambda i, j, kk: (i, j))
    ca, cb = (0 if ta else 1), (1 if tb else 0)

    n_in = 2 + (res is not None) + (after is not None)

    def body(*refs):
        a_ref, b_ref = refs[:2]
        res_ref = refs[2] if res is not None else None
        o_ref = refs[n_in]
        scratch = refs[n_in + 1:]

        def finish(acc):
            val = acc if alpha == 1.0 else alpha * acc
            if res_ref is not None:
                val = res_ref[...].astype(F32) + val
            o_ref[...] = val.astype(o_ref.dtype)

        part = _dot(a_ref[...].astype(BF16), b_ref[...].astype(BF16), ca, cb)
        if nk == 1:
            finish(part)
        else:
            acc_ref = scratch[0]
            kk = pl.program_id(2)

            @pl.when(kk == 0)
            def _():
                acc_ref[...] = part

            @pl.when(kk > 0)
            def _():
                acc_ref[...] += part

            @pl.when(kk == nk - 1)
            def _():
                finish(acc_ref[...])

    ins = [a, b] + ([] if res is None else [res]) + ([] if after is None else [after])
    in_specs = [a_spec, b_spec] + ([] if res is None else [o_spec]) + ([] if after is None else [ANY])
    return _pcall(
        body, name=name, grid=(m // tm, n // tn, nk), in_specs=in_specs, out_specs=o_spec,
        out_shape=jax.ShapeDtypeStruct((m, n), out_dtype),
        scratch_shapes=[pltpu.VMEM((tm, tn), F32)] if nk > 1 else [],
        compiler_params=_params("parallel", "parallel", "arbitrary"),
    )(*ins)


def _rowcall(fn, rows, consts, outs, accs=(), *, tm, name, after=None):
    s = rows[0][0].shape[0]
    assert s % tm == 0
    n_read, n_out = len(rows) + len(consts), len(outs)
    n_in = n_read + (after is not None)

    def body(*refs):
        vals = fn(*[r[...] for r in refs[:n_read]])
        vals = vals if isinstance(vals, (tuple, list)) else (vals,)
        for o_ref, v in zip(refs[n_in:n_in + n_out], vals[:n_out]):
            o_ref[...] = v.astype(o_ref.dtype)
        if accs:
            first = pl.program_id(0) == 0
            for a_ref, v in zip(refs[n_in + n_out:], vals[n_out:]):
                tot = jnp.sum(v.astype(F32), axis=0, keepdims=True)

                @pl.when(first)
                def _(a_ref=a_ref, tot=tot):
                    a_ref[...] = tot

                @pl.when(jnp.logical_not(first))
                def _(a_ref=a_ref, tot=tot):
                    a_ref[...] += tot

    in_specs = [pl.BlockSpec((tm, w), lambda i, cb=cb: (i, cb)) for (_, cb, w) in rows]
    in_specs += [pl.BlockSpec(c.shape, lambda i: (0, 0)) for c in consts]
    in_specs += [] if after is None else [ANY]
    out_specs = [pl.BlockSpec((tm, w), lambda i: (i, 0)) for (w, _) in outs]
    out_specs += [pl.BlockSpec((1, w), lambda i: (0, 0)) for w in accs]
    out_shape = [jax.ShapeDtypeStruct((s, w), dt) for (w, dt) in outs]
    out_shape += [jax.ShapeDtypeStruct((1, w), F32) for w in accs]
    return _pcall(
        body, name=name, grid=(s // tm,), in_specs=in_specs, out_specs=out_specs, out_shape=out_shape,
        compiler_params=_params("arbitrary" if accs else "parallel"),
    )(*[r[0] for r in rows], *consts, *([] if after is None else [after]))


def _whole(a):
    return (a, 0, a.shape[1])


def _xhat(x):
    x = x.astype(F32)
    r = lax.rsqrt(jnp.mean(x * x, axis=-1, keepdims=True) + RMS_EPS)
    return x * r, r


def _rms_bwd(dy, x, g):
    xh, r = _xhat(x)
    dxh = dy.astype(F32) * g
    dx = r * (dxh - xh * jnp.mean(dxh * xh, axis=-1, keepdims=True))
    return dx, dy.astype(F32) * xh


def _sigmoid(x):
    return 1.0 / (1.0 + jnp.exp(-x))


def _rms_fwd(x, g, name, tm, after=None):
    d = x.shape[1]
    return _rowcall(lambda xb, gb: _xhat(xb)[0] * gb, [_whole(x)], [g], [(d, BF16)], tm=tm, name=name, after=after)[0]


def _silu_parts(gate):
    sg = _sigmoid(gate)
    return sg, gate * sg


def _ffn_up(n, w_gu, name):
    s, d = n.shape
    f = w_gu.shape[1] // 2
    tn = _pick(f, (1408, 1024, 512, 256, 128))
    tm = _pick(s, (1024, 512, 256, 128))
    nb = f // tn

    def body(n_ref, wg_ref, wu_ref, gu_ref, act_ref):
        nv = n_ref[...]
        gate = _dot(nv, wg_ref[...], 1, 0)
        up = _dot(nv, wu_ref[...], 1, 0)
        gu_ref[0] = gate.astype(gu_ref.dtype)
        gu_ref[1] = up.astype(gu_ref.dtype)
        act_ref[...] = (_silu_parts(gate)[1] * up).astype(act_ref.dtype)

    return _pcall(
        body, name=name, grid=(s // tm, nb),
        in_specs=[pl.BlockSpec((tm, d), lambda i, j: (i, 0)), pl.BlockSpec((d, tn), lambda i, j: (0, j)),
                  pl.BlockSpec((d, tn), lambda i, j: (0, nb + j))],
        out_specs=[pl.BlockSpec((2, tm, tn), lambda i, j: (0, i, j)), pl.BlockSpec((tm, tn), lambda i, j: (i, j))],
        out_shape=[jax.ShapeDtypeStruct((2, s, f), BF16), jax.ShapeDtypeStruct((s, f), BF16)],
        compiler_params=_params("parallel", "parallel"),
    )(n, w_gu, w_gu)


def _ffn_dgu(dhb, w_down, gu, name, after=None):
    s, d = dhb.shape
    f = w_down.shape[0]
    tn = _pick(f, (1408, 1024, 512, 256, 128))
    tm = _pick(s, (1024, 512, 256, 128))

    def body(dh_ref, w_ref, gu_ref, *rest):
        o_ref = rest[-1]
        dact = _dot(dh_ref[...], w_ref[...], 1, 1)
        gate, up = gu_ref[0].astype(F32), gu_ref[1].astype(F32)
        sg, silu = _silu_parts(gate)
        o_ref[0] = (dact * up * (sg + silu * (1.0 - sg))).astype(o_ref.dtype)
        o_ref[1] = (dact * silu).astype(o_ref.dtype)

    blk = pl.BlockSpec((2, tm, tn), lambda i, j: (0, i, j))
    return _pcall(
        body, name=name, grid=(s // tm, f // tn),
        in_specs=[pl.BlockSpec((tm, d), lambda i, j: (i, 0)), pl.BlockSpec((tn, d), lambda i, j: (j, 0)), blk]
        + ([] if after is None else [ANY]),
        out_specs=blk, out_shape=jax.ShapeDtypeStruct((2, s, f), BF16), compiler_params=_params("parallel", "parallel"),
    )(dhb, w_down, gu, *([] if after is None else [after]))


def _dgrad_norm(dy, wmat, dh, x, g, name, *, dy_halves=False, copy_scale=None, after=None):
    s, d = dh.shape
    k = wmat.shape[1]
    tk = k if k <= MM_WHOLE_K else _pick(k, (MM_WHOLE_K, 2048, 1024, 512, 256, 128))
    if dy_halves and (k // 2) % tk:
        tk = _pick(k // 2, (1408, 1024, 512, 256, 128))
    tm = _pick(s, (512, 256, 128))
    nk, per = k // tk, (k // 2) // tk if dy_halves else 0
    n_in = 5 + (after is not None)
    n_out = 2 + (copy_scale is not None)

    def body(*refs):
        dy_ref, w_ref, dh_ref, x_ref, g_ref = refs[:5]
        outs, scratch = refs[n_in:n_in + n_out], refs[n_in + n_out:]
        i, kk = pl.program_id(0), pl.program_id(1)
        part = _dot(dy_ref[...], w_ref[...], 1, 1)

        def finish(dn):
            dx, dg = _rms_bwd(dn, x_ref[...], g_ref[...])
            tot = dh_ref[...] + dx
            outs[0][...] = tot
            if copy_scale is not None:
                outs[1][...] = (copy_scale * tot).astype(outs[1].dtype)
            dg = jnp.sum(dg, axis=0, keepdims=True)

            @pl.when(i == 0)
            def _():
                outs[-1][...] = dg

            @pl.when(i > 0)
            def _():
                outs[-1][...] += dg

        if nk == 1:
            finish(part)
        else:
            acc_ref = scratch[0]

            @pl.when(kk == 0)
            def _():
                acc_ref[...] = part

            @pl.when(kk > 0)
            def _():
                acc_ref[...] += part

            @pl.when(kk == nk - 1)
            def _():
                finish(acc_ref[...])

    row = pl.BlockSpec((tm, d), lambda i, kk: (i, 0))
    dy_spec = pl.BlockSpec((None, tm, tk), lambda i, kk: (kk // per, i, kk % per)) if dy_halves else pl.BlockSpec((tm, tk), lambda i, kk: (i, kk))
    in_specs = [dy_spec, pl.BlockSpec((d, tk), lambda i, kk: (0, kk)), row, row, pl.BlockSpec((1, d), lambda i, kk: (0, 0))]
    out_specs = [row] * (n_out - 1) + [pl.BlockSpec((1, d), lambda i, kk: (0, 0))]
    out_shape = [jax.ShapeDtypeStruct((s, d), F32)] + ([] if copy_scale is None else [jax.ShapeDtypeStruct((s, d), BF16)])
    return _pcall(
        body, name=name, grid=(s // tm, nk), in_specs=in_specs + ([] if after is None else [ANY]), out_specs=out_specs,
        out_shape=out_shape + [jax.ShapeDtypeStruct((1, d), F32)], scratch_shapes=[pltpu.VMEM((tm, d), F32)] if nk > 1 else [],
        compiler_params=_params("arbitrary", "arbitrary"),
    )(dy, wmat, dh, x, g, *([] if after is None else [after]))


def _shift_down(p, k):
    if k == 0:
        return p
    rows = lax.broadcasted_iota(jnp.int32, p.shape, 0)
    return jnp.where(rows >= k, pltpu.roll(p, k, 0), 0.0)


def _shift_up(p, k):
    if k == 0:
        return p
    s = p.shape[0]
    rows = lax.broadcasted_iota(jnp.int32, p.shape, 0)
    return jnp.where(rows < s - k, pltpu.roll(p, s - k, 0), 0.0)


def _conv_fwd(proj, conv_w, d, tc, name):
    s = proj.shape[0]
    nb = d // tc

    def body(cb_ref, cc_ref, cx_ref, w_ref, y_ref):
        p = cc_ref[...].astype(F32) * cx_ref[...].astype(F32)
        w = w_ref[...]
        acc = p * w[CONV_K - 1:CONV_K, :]
        for k in range(1, CONV_K):
            acc = acc + _shift_down(p, k) * w[CONV_K - 1 - k:CONV_K - k, :]
        y_ref[...] = (cb_ref[...].astype(F32) * acc).astype(y_ref.dtype)

    col = lambda off: pl.BlockSpec((s, tc), lambda j: (0, off * nb + j))
    return _pcall(
        body, name=name, grid=(nb,), in_specs=[col(0), col(1), col(2), pl.BlockSpec((CONV_K, tc), lambda j: (0, j))],
        out_specs=pl.BlockSpec((s, tc), lambda j: (0, j)), out_shape=jax.ShapeDtypeStruct((s, d), BF16),
        compiler_params=_params("parallel"),
    )(proj, proj, proj, conv_w)


def _conv_bwd(dy, proj, conv_w, d, tc, name):
    s = proj.shape[0]
    nb = d // tc

    def body(dy_ref, cb_ref, cc_ref, cx_ref, w_ref, dcb_ref, dcc_ref, dcx_ref, dw_ref):
        cc, cx = cc_ref[...].astype(F32), cx_ref[...].astype(F32)
        p = cc * cx
        w = w_ref[...]
        dyv = dy_ref[...].astype(F32)
        shifted = [_shift_down(p, CONV_K - 1 - k) for k in range(CONV_K)]
        conv = shifted[0] * w[0:1, :]
        for k in range(1, CONV_K):
            conv = conv + shifted[k] * w[k:k + 1, :]
        dcb_ref[...] = (dyv * conv).astype(dcb_ref.dtype)
        ds = dyv * cb_ref[...].astype(F32)
        dp = ds * w[CONV_K - 1:CONV_K, :]
        for k in range(1, CONV_K):
            dp = dp + _shift_up(ds, k) * w[CONV_K - 1 - k:CONV_K - k, :]
        dcc_ref[...] = (dp * cx).astype(dcc_ref.dtype)
        dcx_ref[...] = (dp * cc).astype(dcx_ref.dtype)
        for k in range(CONV_K):
            dw_ref[k:k + 1, :] = jnp.sum(ds * shifted[k], axis=0, keepdims=True)

    col = lambda off: pl.BlockSpec((s, tc), lambda j: (0, off * nb + j))
    blk = pl.BlockSpec((s, tc), lambda j: (0, j))
    wblk = pl.BlockSpec((CONV_K, tc), lambda j: (0, j))
    act = jax.ShapeDtypeStruct((s, d), BF16)
    return _pcall(
        body, name=name, grid=(nb,), in_specs=[blk, col(0), col(1), col(2), wblk],
        out_specs=[blk, blk, blk, wblk], out_shape=[act, act, act, jax.ShapeDtypeStruct((CONV_K, d), F32)],
        compiler_params=_params("parallel"),
    )(dy, proj, proj, proj, conv_w)


def _sb_tile(q, kj, scale, carry, tri, mask):
    z = _dot(q, kj, 1, 1) * scale
    lsz = jnp.minimum(z, 0.0) - jnp.log(1.0 + jnp.exp(-jnp.abs(z)))
    l1m = lsz - z
    if mask is not None:
        l1m = jnp.where(mask, l1m, 0.0)
    l1b = l1m.astype(BF16)
    a = jnp.exp(lsz + (carry + _dot(l1b, tri, 1, 0)))
    if mask is not None:
        a = jnp.where(mask, a, 0.0)
    return lsz, l1b, a.astype(BF16)


def _add_rows(x, upd, r0):
    return x + upd if r0 == 0 else jnp.concatenate([x[:r0], x[r0:] + upd], axis=0)


def _sb_masks(tq, tk):
    row = lax.broadcasted_iota(jnp.int32, (tq, tk), 0)
    col = lax.broadcasted_iota(jnp.int32, (tq, tk), 1)
    masks = [col + dj * tk < row for dj in range(tq // tk)]
    r2 = lax.broadcasted_iota(jnp.int32, (tk, tk), 0)
    c2 = lax.broadcasted_iota(jnp.int32, (tk, tk), 1)
    return masks, (r2 > c2).astype(BF16), (r2 < c2).astype(BF16)


def _sb_fwd(proj, heads, col0, tq, tk, name):
    s = proj.shape[0]
    dh = SB_HEAD_DIM
    nq, nd, nkt = s // tq, tq // tk, s // tk
    scale = dh ** -0.5

    def body(q_ref, k_ref, v_ref, o_ref, a_ref, b_ref):
        i = pl.program_id(1)
        q = q_ref[...]
        masks, tri_right, _ = _sb_masks(tq, tk)

        def tile(j, carry, acc, mask, r0=0):
            start = pl.multiple_of(j * tk, tk)
            kj = k_ref[pl.ds(start, tk), :]
            vj = v_ref[pl.ds(start, tk), :]
            lsz, l1b, ab = _sb_tile(q[r0:], kj, scale, carry[r0:], tri_right, None if mask is None else mask[r0:])
            a_ref[j, r0:, :] = ab
            b_ref[j, r0:, :] = jnp.exp(lsz).astype(b_ref.dtype)
            if r0:
                a_ref[j, :r0, :] = jnp.zeros((r0, tk), a_ref.dtype)
                b_ref[j, :r0, :] = jnp.zeros((r0, tk), b_ref.dtype)
            return (_add_rows(carry, jnp.sum(l1b.astype(F32), axis=1, keepdims=True), r0),
                    _add_rows(acc, _dot(ab, vj, 1, 0), r0))

        state = (jnp.zeros((tq, 1), F32), jnp.zeros((tq, dh), F32))
        for dj in reversed(range(nd)):
            state = tile(i * nd + dj, *state, masks[dj], dj * tk)
        def left_block(t, st):
            for dj in reversed(range(nd)):
                st = tile((i - 1 - t) * nd + dj, st[0], st[1], None)
            return st

        state = lax.fori_loop(0, i, left_block, state)
        o_ref[...] = state[1]

    qspec = pl.BlockSpec((tq, dh), lambda h, i: (i, col0[0] + h))
    kspec = pl.BlockSpec((s, dh), lambda h, i: (0, col0[1] + h))
    vspec = pl.BlockSpec((s, dh), lambda h, i: (0, col0[2] + h))
    saved = pl.BlockSpec((None, nkt, tq, tk), lambda h, i: (h, 0, i, 0))
    saved_shape = jax.ShapeDtypeStruct((heads, nkt, s, tk), BF16)
    return _pcall(
        body, name=name, grid=(heads, nq), in_specs=[qspec, kspec, vspec],
        out_specs=[pl.BlockSpec((tq, dh), lambda h, i: (i, h)), saved, saved],
        out_shape=[jax.ShapeDtypeStruct((s, heads * dh), F32), saved_shape, saved_shape],
        compiler_params=_params("parallel", "parallel"),
    )(proj, proj, proj)


def _sb_bwd(proj, o, a_all, beta_all, do, heads, col0, tq, tk, name):
    s = proj.shape[0]
    dh = SB_HEAD_DIM
    nq, nd, nkt = s // tq, tq // tk, s // tk
    scale = dh ** -0.5

    def body(q_ref, k_ref, v_ref, o_ref, a_ref, b_ref, do_ref, dq_ref, dk_ref, dv_ref, dk_acc, dv_acc):
        i = pl.program_id(1)

        @pl.when(i == 0)
        def _():
            dk_acc[...] = jnp.zeros_like(dk_acc)
            dv_acc[...] = jnp.zeros_like(dv_acc)

        q = q_ref[...]
        dob = do_ref[...].astype(BF16)
        delta = jnp.sum(dob.astype(F32) * o_ref[...], axis=1, keepdims=True)
        masks, _, tri_left = _sb_masks(tq, tk)

        def tile(j, carry_g, dq, mask):
            start = pl.multiple_of(j * tk, tk)
            kj = k_ref[pl.ds(start, tk), :]
            vj = v_ref[pl.ds(start, tk), :]
            ab = a_ref[j]
            g = _dot(dob, vj, 1, 1) * ab.astype(F32)
            carry_g = carry_g + jnp.sum(g, axis=1, keepdims=True)
            left = (delta - carry_g) + _dot(g.astype(BF16), tri_left, 1, 0)
            dz = g - b_ref[j].astype(F32) * (g + left)
            if mask is not None:
                dz = jnp.where(mask, dz, 0.0)
            dzb = dz.astype(BF16)
            dk_acc[pl.ds(start, tk), :] += _dot(dzb, q, 0, 0)
            dv_acc[pl.ds(start, tk), :] += _dot(ab, dob, 0, 0)
            return carry_g, dq + _dot(dzb, kj, 1, 0)

        state = (jnp.zeros((tq, 1), F32), jnp.zeros((tq, dh), F32))
        for dj in reversed(range(nd)):
            state = tile(i * nd + dj, *state, masks[dj])
        def left_block(t, st):
            for dj in reversed(range(nd)):
                st = tile((i - 1 - t) * nd + dj, st[0], st[1], None)
            return st

        state = lax.fori_loop(0, i, left_block, state)
        dq_ref[...] = (state[1] * scale).astype(dq_ref.dtype)

        @pl.when(i == nq - 1)
        def _():
            dk_ref[...] = (dk_acc[...] * scale).astype(dk_ref.dtype)
            dv_ref[...] = dv_acc[...].astype(dv_ref.dtype)

    qspec = pl.BlockSpec((tq, dh), lambda h, i: (i, col0[0] + h))
    kspec = pl.BlockSpec((s, dh), lambda h, i: (0, col0[1] + h))
    vspec = pl.BlockSpec((s, dh), lambda h, i: (0, col0[2] + h))
    blk = pl.BlockSpec((tq, dh), lambda h, i: (i, h))
    full = pl.BlockSpec((s, dh), lambda h, i: (0, h))
    saved = pl.BlockSpec((None, nkt, tq, tk), lambda h, i: (h, 0, i, 0))
    act = jax.ShapeDtypeStruct((s, heads * dh), BF16)
    return _pcall(
        body, name=name, grid=(heads, nq), in_specs=[qspec, kspec, vspec, blk, saved, saved, blk],
        out_specs=[blk, full, full], out_shape=[act, act, act],
        scratch_shapes=[pltpu.VMEM((s, dh), F32), pltpu.VMEM((s, dh), F32)],
        compiler_params=_params("parallel", "arbitrary"),
    )(proj, proj, proj, o, a_all, beta_all, do)


def _xattn_probs(q, k, scale):
    sc = _dot(q, k, 1, 1) * scale
    e = jnp.exp(sc - jnp.max(sc, axis=1, keepdims=True))
    return e / jnp.sum(e, axis=1, keepdims=True)


def _xattn_fwd(qc, kv, tq, name):
    s, d = qc.shape
    m = kv.shape[0]
    dh = d // X_HEADS
    scale = dh ** -0.5

    def body(q_ref, k_ref, v_ref, o_ref):
        p = _xattn_probs(q_ref[...], k_ref[...], scale)
        o_ref[...] = _dot(p.astype(BF16), v_ref[...], 1, 0).astype(o_ref.dtype)

    blk = pl.BlockSpec((tq, dh), lambda h, i: (i, h))
    return _pcall(
        body, name=name, grid=(X_HEADS, s // tq),
        in_specs=[blk, pl.BlockSpec((m, dh), lambda h, i: (0, h)), pl.BlockSpec((m, dh), lambda h, i: (0, X_HEADS + h))],
        out_specs=blk, out_shape=jax.ShapeDtypeStruct((s, d), BF16), compiler_params=_params("parallel", "parallel"),
    )(qc, kv, kv)


def _xattn_bwd(qc, kv, do, tq, name):
    s, d = qc.shape
    m = kv.shape[0]
    dh = d // X_HEADS
    scale = dh ** -0.5
    nq = s // tq

    def body(q_ref, k_ref, v_ref, do_ref, dq_ref, dk_ref, dv_ref, dk_acc, dv_acc):
        i = pl.program_id(1)
        q, k, v = q_ref[...], k_ref[...], v_ref[...]
        dob = do_ref[...].astype(BF16)
        p = _xattn_probs(q, k, scale)
        pb = p.astype(BF16)
        dp = _dot(dob, v, 1, 1)
        ds = pb.astype(F32) * (dp - jnp.sum(dp * pb.astype(F32), axis=1, keepdims=True))
        dsb = (ds * scale).astype(BF16)
        dq_ref[...] = _dot(dsb, k, 1, 0).astype(dq_ref.dtype)
        dk_part = _dot(dsb, q, 0, 0)
        dv_part = _dot(pb, dob, 0, 0)

        @pl.when(i == 0)
        def _():
            dk_acc[...] = dk_part
            dv_acc[...] = dv_part

        @pl.when(i > 0)
        def _():
            dk_acc[...] += dk_part
            dv_acc[...] += dv_part

        @pl.when(i == nq - 1)
        def _():
            dk_ref[...] = dk_acc[...].astype(dk_ref.dtype)
            dv_ref[...] = dv_acc[...].astype(dv_ref.dtype)

    blk = pl.BlockSpec((tq, dh), lambda h, i: (i, h))
    kblk = pl.BlockSpec((m, dh), lambda h, i: (0, h))
    return _pcall(
        body, name=name, grid=(X_HEADS, nq),
        in_specs=[blk, kblk, pl.BlockSpec((m, dh), lambda h, i: (0, X_HEADS + h)), blk],
        out_specs=[blk, kblk, kblk],
        out_shape=[jax.ShapeDtypeStruct((s, d), BF16), jax.ShapeDtypeStruct((m, d), BF16), jax.ShapeDtypeStruct((m, d), BF16)],
        scratch_shapes=[pltpu.VMEM((m, dh), F32), pltpu.VMEM((m, dh), F32)],
        compiler_params=_params("parallel", "arbitrary"),
    )(qc, kv, kv, do)


def _local_step(x, mem, tgt, w, fetch=None, prefetch=None, emit=None, tick=None, after=None):
    fetch = fetch or (lambda name, after: {})
    prefetch = prefetch or (lambda name, after: None)
    emit = emit or (lambda group, g: None)
    tick = tick or (lambda group, after: None)
    w = dict(w)
    s, d = x.shape
    heads = d // SB_HEAD_DIM
    tm = _pick(s, (512, 256, 128))
    tq = _pick(s, (1024, 512, 256, 128))
    sb_tq, sb_tk = _pick(s, (512, 256, 128)), _pick(s, (256, 128))
    tc = _pick(d, (256, 128))
    g = {}

    def wt(name, after):
        if name not in w:
            w.update(fetch(name, after))
        return w[name]

    def ffn_fwd(h, gname, wgu, wdown, tag, after=None):
        n = _rms_fwd(h, w[gname], tag + "_norm", tm, after=after)
        gu, act = _ffn_up(n, wt(wgu, n), tag + "_gu")
        prefetch(wdown, gu)
        return n, gu, act, _mm(act, wt(wdown, act), name=tag + "_down", out_dtype=F32, res=h, alpha=0.5)

    def ffn_bwd(dh, dhb, h, saved, gname, wgu, wdown, tag, copy_scale=None, after=None):
        n, gu, act = saved
        g[wdown] = _mm(act, dhb, ta=True, name=tag + "_dwdown", after=after)
        dgu = _ffn_dgu(dhb, w[wdown], gu, tag + "_dgu", after=emit(tag + "_down", g))
        g[wgu] = _mm(n, dgu, ta=True, b_halves=True, name=tag + "_dwgu", after=tick(tag + "_down", dgu))
        *dh_in, g[gname] = _dgrad_norm(dgu, w[wgu], dh, h, w[gname], tag + "_dn", dy_halves=True, copy_scale=copy_scale,
                                       after=emit(tag, g))
        return dh_in, tick(tag, dh_in[0])

    n1, gu1, act1, h1 = ffn_fwd(x, "g_ffn1", "w_ffn1_gu", "w_ffn1_down", "ffn1", after)
    prefetch("w_in", h1)
    u = _rms_fwd(h1, w["g_mix"], "mix_norm", tm)
    proj = _mm(u, wt("w_in", u), name="mix_in")
    prefetch("w_conv_out", proj)
    nd = d // SB_HEAD_DIM
    y_conv = _conv_fwd(proj, w["conv_w"], d, tc, "conv_fwd")
    sb_cols = (3 * nd, 4 * nd, 5 * nd)
    y_sb, sb_a, sb_beta = _sb_fwd(proj, heads, sb_cols, _pick(s, (2 * sb_tq, sb_tq)), sb_tk, "sb_fwd")
    prefetch("w_cq", y_sb)
    a_conv = _mm(y_conv, wt("w_conv_out", y_conv), name="conv_out")
    a_sb = _mm(y_sb, wt("w_attn_out", y_sb), name="attn_out")
    b_conv, b_sb = w["b_gate"][:, :d], w["b_gate"][:, d:]

    def merge(ac, asb, gcp, gsp, bc, bs):
        gc = _sigmoid(gcp.astype(F32) + bc)
        gs = _sigmoid(gsp.astype(F32) + bs)
        return gc * ac.astype(F32) + gs * asb.astype(F32)

    merged = _rowcall(merge, [_whole(a_conv), _whole(a_sb), (proj, 6, d), (proj, 7, d)], [b_conv, b_sb], [(d, BF16)],
                      tm=tm, name="merge")[0]
    prefetch("w_ffn2_gu", merged)
    h2 = _mm(merged, wt("w_o", merged), name="mix_out", out_dtype=F32, res=h1)
    hn = _rms_fwd(h2, w["g_cross"], "cross_norm", tm)
    mn = _rms_fwd(mem, w["g_mem"], "mem_norm", _pick(mem.shape[0], (256, 128)))
    qc = _mm(hn, wt("w_cq", hn), name="cross_q")
    kv = _mm(mn, wt("w_ckv", mn), name="cross_kv")
    oc = _xattn_fwd(qc, kv, tq, "xattn_fwd")
    h3 = _mm(oc, wt("w_co", oc), name="cross_out", out_dtype=F32, res=h2)
    n2, gu2, act2, h4 = ffn_fwd(h3, "g_ffn2", "w_ffn2_gu", "w_ffn2_down", "ffn2")

    def head(hb, tb, gb):
        xh, r = _xhat(hb)
        err = xh * gb - tb
        dy = err * (1.0 / d)
        dxh = dy * gb
        dx = r * (dxh - xh * jnp.mean(dxh * xh, axis=-1, keepdims=True))
        row_loss = 0.5 * jnp.mean(err * err, axis=-1, keepdims=True)
        return dx, 0.5 * dx, dy * xh, jnp.broadcast_to(row_loss, (row_loss.shape[0], LANES))

    dh4, dh4b, g["g_final"], loss_lanes = _rowcall(head, [_whole(h4), _whole(tgt)], [w["g_final"]], [(d, F32), (d, BF16)],
                                                   [d, LANES], tm=tm, name="loss_head")

    (dh3, dh3b), tok = ffn_bwd(dh4, dh4b, h3, (n2, gu2, act2), "g_ffn2", "w_ffn2_gu", "w_ffn2_down", "ffn2", copy_scale=1.0)
    g["w_co"] = _mm(oc, dh3b, ta=True, name="cross_dwco", after=tok)
    doc = _mm(dh3b, w["w_co"], tb=True, name="cross_doc")
    dqc, dk, dv = _xattn_bwd(qc, kv, doc, tq, "xattn_bwd")
    dkv = jnp.concatenate([dk, dv], axis=1)
    g["w_cq"] = _mm(hn, dqc, ta=True, name="cross_dwcq")
    g["w_ckv"] = _mm(mn, dkv, ta=True, name="cross_dwckv")
    dmn = _mm(dkv, w["w_ckv"], tb=True, name="cross_dmn", out_dtype=F32)
    g["g_mem"] = _rowcall(lambda dy, xb: dy * _xhat(xb)[0], [_whole(dmn), _whole(mem)], [], [], [d],
                          tm=_pick(mem.shape[0], (256, 128)), name="mem_dnorm")[0]
    dh2, dh2b, g["g_cross"] = _dgrad_norm(dqc, w["w_cq"], dh3, h2, w["g_cross"], "cross_dhn", copy_scale=1.0, after=emit("cross", g))

    g["w_o"] = _mm(merged, dh2b, ta=True, name="mix_dwo", after=tick("cross", dh2))
    dmerged = _mm(dh2b, w["w_o"], tb=True, name="mix_dmerged")

    def merge_bwd(dm, ac, asb, gcp, gsp, bc, bs):
        dm, ac, asb = dm.astype(F32), ac.astype(F32), asb.astype(F32)
        gc = _sigmoid(gcp.astype(F32) + bc)
        gs = _sigmoid(gsp.astype(F32) + bs)
        dgc = dm * ac * gc * (1.0 - gc)
        dgs = dm * asb * gs * (1.0 - gs)
        return dm * gc, dm * gs, dgc, dgs, dgc, dgs

    da_conv, da_sb, dgc, dgs, db_conv, db_sb = _rowcall(
        merge_bwd, [_whole(dmerged), _whole(a_conv), _whole(a_sb), (proj, 6, d), (proj, 7, d)], [b_conv, b_sb],
        [(d, BF16)] * 4, [d, d], tm=tm, name="merge_bwd")
    g["b_gate"] = jnp.concatenate([db_conv, db_sb], axis=1)
    g["w_conv_out"] = _mm(y_conv, da_conv, ta=True, name="conv_dwout")
    g["w_attn_out"] = _mm(y_sb, da_sb, ta=True, name="attn_dwout")
    dy_conv = _mm(da_conv, w["w_conv_out"], tb=True, name="conv_dy")
    dy_sb = _mm(da_sb, w["w_attn_out"], tb=True, name="attn_dy")
    dcb, dcc, dcx, g["conv_w"] = _conv_bwd(dy_conv, proj, w["conv_w"], d, tc, "conv_bwd")
    dq, dk_sb, dv_sb = _sb_bwd(proj, y_sb, sb_a, sb_beta, dy_sb, heads, sb_cols, sb_tq, sb_tk, "sb_bwd")
    dproj = jnp.concatenate([dcb, dcc, dcx, dq, dk_sb, dv_sb, dgc, dgs], axis=1)
    g["w_in"] = _mm(u, dproj, ta=True, name="mix_dwin")
    dh1, dh1b, g["g_mix"] = _dgrad_norm(dproj, w["w_in"], dh2, h1, w["g_mix"], "mix_du", copy_scale=0.5, after=emit("mix", g))
    (dx,), tok = ffn_bwd(dh1, dh1b, x, (n1, gu1, act1), "g_ffn1", "w_ffn1_gu", "w_ffn1_down", "ffn1", after=tick("mix", dh1))
    return loss_lanes, dx, g, tok


MATS = (("w_ffn1_gu", "col"), ("w_ffn1_down", "row"), ("w_in", "col"), ("w_conv_out", "row"), ("w_attn_out", "row"),
        ("w_o", "row"), ("w_cq", "row"), ("w_ckv", "col"), ("w_co", "row"), ("w_ffn2_gu", "col"), ("w_ffn2_down", "row"))
VECS = ("g_ffn1", "g_mix", "g_cross", "g_mem", "g_ffn2", "g_final")
WEIGHTS = ("g_ffn1", "w_ffn1_gu", "w_ffn1_down", "g_mix", "w_in", "b_gate", "conv_w", "w_conv_out", "w_attn_out", "w_o",
           "g_cross", "g_mem", "w_cq", "w_ckv", "w_co", "g_ffn2", "w_ffn2_gu", "w_ffn2_down", "g_final")
CONV_ROWS = 16


def _full_shape(kind, r, c):
    return (r, N_CHIPS * c) if kind == "col" else (N_CHIPS * r, c)


def _piece(ref, kind, r, c, chip, half):
    hr = r // 2
    if kind == "col":
        return ref.at[pl.ds(pl.multiple_of(half * hr, math.gcd(hr, 16)), hr), pl.ds(pl.multiple_of(chip * c, LANES), c)]
    return ref.at[pl.ds(pl.multiple_of(chip * r + half * hr, math.gcd(hr, 16)), hr), :]


def _shard_of(ref, kind, r, c, chip):
    if kind == "col":
        return ref.at[:, pl.ds(pl.multiple_of(chip * c, LANES), c)]
    return ref.at[pl.ds(pl.multiple_of(chip * r, 16), r), :]


def _place():
    x, y, c = lax.axis_index("x"), lax.axis_index("y"), lax.axis_index("c")
    others = [(1 - x, y), (x, 1 - y), (1 - x, 1 - y)]
    return x, y, c, 2 * x + y, others


def _remote(src, dst, send_sem, recv_sem, to):
    return pltpu.make_async_remote_copy(src_ref=src, dst_ref=dst, send_sem=send_sem, recv_sem=recv_sem,
                                        device_id=to, device_id_type=MESH)


HBM = pl.BlockSpec(memory_space=pltpu.HBM)
SEM = pl.BlockSpec(memory_space=pltpu.SEMAPHORE)
EFFECT = pltpu.SideEffectType.DATAFLOW_SIDE_EFFECTING
TOKEN = (8, LANES)


def _split_start(name, plan, n_copies, srcs, lands, after=None):
    ns, nl = len(srcs), len(lands)
    n_in = ns + nl + (after is not None)

    def body(*refs):
        outs = refs[n_in:]
        sends, _ = plan(refs[:ns], refs[ns:ns + nl], outs[0], outs[1])
        for cp in sends:
            cp.start()
        outs[-1][...] = jnp.zeros(TOKEN, F32)

    held = [pltpu.HBM(a.shape, a.dtype) for a in (*srcs, *lands)]
    dma = pltpu.SemaphoreType.DMA((n_copies,))
    ins = [pltpu.with_memory_space_constraint(a, pltpu.HBM) for a in (*srcs, *lands)]
    outs = _pcall(
        body, name=name, in_specs=[HBM] * (ns + nl) + ([] if after is None else [ANY]),
        out_specs=(SEM, SEM, *[HBM] * (ns + nl), pl.BlockSpec(memory_space=pltpu.VMEM)),
        out_shape=(dma, dma, *held, jax.ShapeDtypeStruct(TOKEN, F32)),
        input_output_aliases={i: 2 + i for i in range(ns + nl)},
        compiler_params=pltpu.CompilerParams(has_side_effects=EFFECT),
    )(*ins, *([] if after is None else [after]))
    return outs[0], outs[1], list(outs[2:2 + ns]), list(outs[2 + ns:2 + ns + nl]), outs[-1]


def _split_wait(name, plan, send_sems, recv_sems, srcs, lands, after):
    ns, nl = len(srcs), len(lands)

    def body(*refs):
        sends, recvs = plan(refs[:ns], refs[ns:ns + nl], refs[ns + nl], refs[ns + nl + 1])
        for cp in sends:
            cp.wait_send()
        for cp in recvs:
            cp.wait_recv()

    outs = _pcall(
        body, name=name, in_specs=[HBM] * (ns + nl) + [SEM, SEM, ANY], out_specs=[HBM] * (ns + nl),
        out_shape=[pltpu.HBM(a.shape, a.dtype) for a in (*srcs, *lands)],
        input_output_aliases={i: i for i in range(ns + nl)},
        compiler_params=pltpu.CompilerParams(has_side_effects=EFFECT),
    )(*srcs, *lands, send_sems, recv_sems, after)
    return list(outs[:ns]), list(outs[ns:])


def _gather_plan(dims):
    def plan(shard_refs, full_refs, ss, rs):
        x, y, c, me, others = _place()
        sends, recvs = [], []
        for wi, (kind, r, cw) in enumerate(dims):
            half = shard_refs[wi].at[pl.ds(pl.multiple_of(c * (r // 2), math.gcd(r // 2, 16)), r // 2), :]
            for k, (ox, oy) in enumerate(others):
                sem = 4 * wi + k
                sends.append(_remote(half, _piece(full_refs[wi], kind, r, cw, me, c), ss.at[sem], rs.at[sem], (ox, oy, c)))
                recvs.append(_remote(half, _piece(full_refs[wi], kind, r, cw, 2 * ox + oy, c), ss.at[sem], rs.at[sem], (x, y, c)))
            sem = 4 * wi + 3
            own = _remote(shard_refs[wi], _shard_of(full_refs[wi], kind, r, cw, me), ss.at[sem], rs.at[sem], (x, y, 1 - c))
            sends.append(own)
            recvs.append(own)
        return sends, recvs

    return plan


def _forward_plan(dims):
    def plan(_, full_refs, ss, rs):
        x, y, c, _, others = _place()
        sends, recvs = [], []
        for wi, (kind, r, cw) in enumerate(dims):
            for k, (ox, oy) in enumerate(others):
                sem = 3 * wi + k
                mine = _piece(full_refs[wi], kind, r, cw, 2 * ox + oy, c)
                theirs = _piece(full_refs[wi], kind, r, cw, 2 * ox + oy, 1 - c)
                sends.append(_remote(mine, mine, ss.at[sem], rs.at[sem], (x, y, 1 - c)))
                recvs.append(_remote(theirs, theirs, ss.at[sem], rs.at[sem], (x, y, 1 - c)))
        return sends, recvs

    return plan


def _rs_cores_plan(dims):
    def plan(g_refs, land_refs, ss, rs):
        x, y, c, _, _ = _place()
        sends, recvs = [], []
        for wi, dm in enumerate(dims):
            for chip in range(N_CHIPS):
                sem = N_CHIPS * wi + chip
                sends.append(_remote(_piece(g_refs[wi], *dm, chip, 1 - c), land_refs[wi].at[chip], ss.at[sem], rs.at[sem], (x, y, 1 - c)))
                recvs.append(_remote(_piece(g_refs[wi], *dm, chip, c), land_refs[wi].at[chip], ss.at[sem], rs.at[sem], (x, y, 1 - c)))
        return sends, recvs

    return plan


def _share_plan(nw):
    def plan(_, buf_refs, ss, rs):
        x, y, c, _, _ = _place()
        sends = [_remote(buf_refs[wi].at[c], buf_refs[wi].at[c], ss.at[wi], rs.at[wi], (x, y, 1 - c)) for wi in range(nw)]
        recvs = [_remote(buf_refs[wi].at[1 - c], buf_refs[wi].at[1 - c], ss.at[wi], rs.at[wi], (x, y, 1 - c)) for wi in range(nw)]
        return sends, recvs

    return plan


def _small_plan():
    def plan(_, buf_refs, ss, rs):
        x, y, c = lax.axis_index("x"), lax.axis_index("y"), lax.axis_index("c")
        buf = buf_refs[0]
        sends, recvs = [], []
        for rel in range(1, N_DEV):
            peer = (x ^ (rel >> 2 & 1), y ^ (rel >> 1 & 1), c ^ (rel & 1))
            sends.append(_remote(buf.at[0], buf.at[rel], ss.at[rel - 1], rs.at[rel - 1], peer))
            recvs.append(_remote(buf.at[0], buf.at[rel], ss.at[rel - 1], rs.at[rel - 1], peer))
        return sends, recvs

    return plan


def _sum_small(buf, me, name):
    _, rows, n = buf.shape

    def body(me_ref, b_ref, o_ref):
        tot = b_ref[me_ref[0]]
        for dev in range(1, N_DEV):
            tot = tot + b_ref[dev ^ me_ref[0]]
        o_ref[...] = tot

    return _pcall(
        body, name=name, out_shape=jax.ShapeDtypeStruct((rows, n), F32),
        grid_spec=pltpu.PrefetchScalarGridSpec(
            num_scalar_prefetch=1, grid=(1,), in_specs=[pl.BlockSpec((N_DEV, rows, n), lambda i, m: (0, 0, 0))],
            out_specs=pl.BlockSpec((rows, n), lambda i, m: (0, 0))),
    )(me, buf)


def _rs_chips_plan(nw):
    def plan(p_refs, land_refs, ss, rs):
        x, y, c, me, others = _place()
        sends, recvs = [], []
        for wi in range(nw):
            for k, (ox, oy) in enumerate(others):
                sem = 3 * wi + k
                sends.append(_remote(p_refs[wi].at[2 * ox + oy], land_refs[wi].at[k], ss.at[sem], rs.at[sem], (ox, oy, c)))
                recvs.append(_remote(p_refs[wi].at[me], land_refs[wi].at[k], ss.at[sem], rs.at[sem], (x, y, c)))
        return sends, recvs

    return plan


def _rows_per_block(n, c, limit_bytes=2 << 20):
    best = None
    for tm in range(16, n + 1, 16):
        if n % tm == 0 and tm * c * 4 <= limit_bytes:
            best = tm
    return best or n


def _sum_cores(grad, got, kind, place, name):
    _, hr, cw = got.shape
    tm = _rows_per_block(hr, cw)
    nb = hr // tm

    def body(place_ref, g_ref, t_ref, o_ref):
        o_ref[...] = (g_ref[...].astype(F32) + t_ref[...].astype(F32)).astype(o_ref.dtype)

    if kind == "col":
        g_spec = pl.BlockSpec((tm, cw), lambda j, i, pr: (pr[0] * nb + i, j))
    else:
        g_spec = pl.BlockSpec((tm, cw), lambda j, i, pr: ((2 * j + pr[0]) * nb + i, 0))
    blk = pl.BlockSpec((None, tm, cw), lambda j, i, pr: (j, i, 0))
    return _pcall(
        body, name=name, out_shape=jax.ShapeDtypeStruct(got.shape, BF16),
        grid_spec=pltpu.PrefetchScalarGridSpec(num_scalar_prefetch=1, grid=(N_CHIPS, nb), in_specs=[g_spec, blk], out_specs=blk),
        compiler_params=_params("parallel", "parallel"),
    )(place, grad, got)


def _sum_chips(parts, got, place, name):
    _, n, cw = got.shape
    tm = _rows_per_block(n, cw)

    def body(place_ref, p_ref, g_ref, o_ref):
        tot = p_ref[...].astype(F32)
        for k in range(3):
            tot = tot + g_ref[k].astype(F32)
        o_ref[...] = tot

    return _pcall(
        body, name=name, out_shape=jax.ShapeDtypeStruct((2, n, cw), F32),
        grid_spec=pltpu.PrefetchScalarGridSpec(
            num_scalar_prefetch=1, grid=(n // tm,),
            in_specs=[pl.BlockSpec((None, tm, cw), lambda i, pr: (pr[1], i, 0)), pl.BlockSpec((3, tm, cw), lambda i, pr: (0, i, 0))],
            out_specs=pl.BlockSpec((None, tm, cw), lambda i, pr: (pr[0], i, 0))),
        compiler_params=_params("parallel"),
    )(place, parts, got)


def _adamw(g, w, m, v, name):
    n, c = g.shape
    c1 = 1.0 - ADAM_B1 ** ADAM_STEP
    c2 = 1.0 - ADAM_B2 ** ADAM_STEP

    def fn(gb, wb, mb, vb):
        m_new = ADAM_B1 * mb + (1.0 - ADAM_B1) * gb
        v_new = ADAM_B2 * vb + (1.0 - ADAM_B2) * (gb * gb)
        delta = -ADAM_LR * ((m_new / c1) / (jnp.sqrt(v_new / c2) + ADAM_EPS) + ADAM_WD * wb)
        return gb, delta, m_new, v_new

    tm = _rows_per_block(n, c) if n % 16 == 0 else n
    return _rowcall(fn, [_whole(g), _whole(w), _whole(m), _whole(v)], [], [(c, F32)] * 4, tm=tm, name=name)


PACK_ROWS = 16


def _pack_rows(parts, width, name, after=None):
    assert sum(p.shape[0] for p in parts) <= PACK_ROWS

    def body(*refs):
        out_ref = refs[-1]
        out_ref[...] = jnp.zeros_like(out_ref)
        at = 0
        for r in refs[:len(parts)]:
            k, n = r.shape
            if n == width:
                out_ref[at:at + k, :] = r[...]
            else:
                out_ref[at:at + k, :] = jnp.broadcast_to(r[:, :1], (k, width))
            at += k

    vm = pl.BlockSpec(memory_space=pltpu.VMEM)
    return _pcall(body, name=name, in_specs=[vm] * len(parts) + ([] if after is None else [ANY]), out_specs=vm,
                  out_shape=jax.ShapeDtypeStruct((PACK_ROWS, width), F32))(*parts, *([] if after is None else [after]))


def _cast_shard(wm, name, after):
    n, c = wm.shape
    return _rowcall(lambda v: v, [_whole(wm)], [], [(c, BF16)], tm=_rows_per_block(n, c), name=name, after=after)[0]


GATHER_GROUPS = (
    ("w_ffn1_gu", "conv_w"), ("w_ffn1_down",), ("w_in",), ("w_conv_out", "w_attn_out", "w_o"), ("w_cq", "w_ckv", "w_co"),
    ("w_ffn2_gu", "w_ffn2_down"),
)
REDUCE_GROUPS = {
    "ffn2": ("w_ffn2_down", "w_ffn2_gu"),
    "cross": ("w_co", "w_cq", "w_ckv"),
    "mix": ("w_o", "w_conv_out", "w_attn_out", "w_in"),
    "ffn1_down": ("w_ffn1_down",),
    "ffn1": ("w_ffn1_gu",),
}
TAIL_STAGES = (("ffn2", "cross"), ("mix",), ("ffn1_down", "ffn1"))
KIND = dict(MATS)


def _step(x, mem, tgt, wts, m_in, v_in):
    d = x.shape[-1]
    cc = wts["conv_w"].shape[1]
    place = jnp.stack([lax.axis_index("c"), 2 * lax.axis_index("x") + lax.axis_index("y")]).astype(jnp.int32)
    dims = {n: (kind, *wts[n].shape) for n, kind in MATS}
    dims["conv_w"] = ("col", CONV_ROWS, cc)

    w = {n: wts[n].reshape(1, -1) for n in VECS + ("b_gate",)}
    flying, token = {}, None
    for names in GATHER_GROUPS:
        gd = [dims[n] for n in names]
        shards = [jnp.pad(wts[n], ((0, CONV_ROWS - CONV_K), (0, 0))) if n == "conv_w" else _cast_shard(wts[n], "cast_" + n, token)
                  for n in names]
        lands = [lax.empty(_full_shape(*dm), sh.dtype) for dm, sh in zip(gd, shards)]
        plan = _gather_plan(gd)
        ss, rs, srcs, lands, token = _split_start("gather_start_" + names[0], plan, 4 * len(names), shards, lands, token)
        flying.update({n: (names, plan, ss, rs, srcs, lands, gd) for n in names})

    passing = {}

    def prefetch(name, after):
        if name not in passing:
            names, plan, ss, rs, srcs, lands, gd = flying[name]
            _, lands = _split_wait("gather_wait_" + names[0], plan, ss, rs, srcs, lands, after)
            plan = _forward_plan(gd)
            ss, rs, _, lands, _ = _split_start("forward_start_" + names[0], plan, 3 * len(names), [], lands)
            passing.update({n: (names, plan, ss, rs, lands) for n in names})

    def fetch(name, after):
        prefetch(name, after)
        names, plan, ss, rs, lands = passing[name]
        _, lands = _split_wait("forward_wait_" + names[0], plan, ss, rs, [], lands, after)
        return {n: (land[:CONV_K] if n == "conv_w" else land) for n, land in zip(names, lands)}

    swapping, sent = {}, {}

    def emit(tag, g):
        if tag not in REDUCE_GROUPS:
            return None
        names = REDUCE_GROUPS[tag]
        gd = [dims[n] for n in names]
        lands = [lax.empty((N_CHIPS, r // 2, cw), BF16) for (_, r, cw) in gd]
        plan = _rs_cores_plan(gd)
        ss, rs, srcs, lands, tok = _split_start("rs_cores_start_" + tag, plan, N_CHIPS * len(names), [g[n] for n in names], lands)
        swapping[tag] = (plan, ss, rs, srcs, lands)
        return tok

    def tick(tag, after):
        if tag not in REDUCE_GROUPS:
            return None
        names = REDUCE_GROUPS[tag]
        plan, ss, rs, srcs, lands = swapping[tag]
        mine, got = _split_wait("rs_cores_wait_" + tag, plan, ss, rs, srcs, lands, after)
        parts = [_sum_cores(gm, t, KIND[n], place, "sum_cores_" + n) for n, gm, t in zip(names, mine, got)]
        lands = [lax.empty((3, *p.shape[1:]), BF16) for p in parts]
        plan = _rs_chips_plan(len(names))
        ss, rs, srcs, lands, tok = _split_start("rs_chips_start_" + tag, plan, 3 * len(names), parts, lands)
        sent[tag] = (plan, ss, rs, srcs, lands)
        return tok

    loss_lanes, dx, g, last = _local_step(x[0], mem[0], tgt[0], w, fetch, prefetch, emit, tick, token)

    rows = [g[n] for n in VECS] + [g["b_gate"][:, :d], g["b_gate"][:, d:], g["conv_w"], loss_lanes]
    packed = _pack_rows(rows, d, "pack_small", after=last)
    small = jnp.concatenate([packed[None], jnp.zeros((N_DEV - 1, *packed.shape), F32)], axis=0)
    small_plan = _small_plan()
    small_ss, small_rs, _, small, after = _split_start("small_start", small_plan, N_DEV - 1, [], [small])

    grads, out = {}, {}

    def update(n):
        shape = wts[n].shape
        as2d = (lambda a: a.reshape(1, -1)) if len(shape) == 1 else (lambda a: a)
        return [r.reshape(shape) for r in _adamw(grads[n], as2d(wts[n]), as2d(m_in[n]), as2d(v_in[n]), "adamw_" + n)]

    def finish(sharing, after):
        tag, names, plan, ss, rs, halves = sharing
        _, both = _split_wait("share_wait_" + tag, plan, ss, rs, [], halves, after)
        for n, b in zip(names, both):
            grads[n] = b.reshape(-1, b.shape[-1])
            out[n] = update(n)
        return out[names[-1]][1]

    sharing = None
    for stage in TAIL_STAGES:
        names, halves = [], []
        for tag in stage:
            plan, ss, rs, srcs, lands = sent[tag]
            parts, landed = _split_wait("rs_chips_wait_" + tag, plan, ss, rs, srcs, lands, after)
            halves += [_sum_chips(p, t, place, "sum_chips_" + n) for n, p, t in zip(REDUCE_GROUPS[tag], parts, landed)]
            names += REDUCE_GROUPS[tag]
        plan = _share_plan(len(names))
        ss, rs, _, halves, after = _split_start("share_start_" + stage[0], plan, len(names), [], halves)
        if sharing is not None:
            after = finish(sharing, after)
        sharing = (stage[0], names, plan, ss, rs, halves)
    after = finish(sharing, after)

    _, small = _split_wait("small_wait", small_plan, small_ss, small_rs, [], small, after)
    me = (4 * lax.axis_index("x") + 2 * lax.axis_index("y") + lax.axis_index("c")).astype(jnp.int32).reshape(1)
    red = _sum_small(small[0], me, "sum_small")
    grads.update({n: red[i:i + 1] for i, n in enumerate(VECS)})
    nv = len(VECS)
    grads["b_gate"] = jnp.concatenate([red[nv:nv + 1], red[nv + 1:nv + 2]], axis=1)
    chip = 2 * lax.axis_index("x") + lax.axis_index("y")
    grads["conv_w"] = lax.dynamic_slice_in_dim(red[nv + 2:nv + 2 + CONV_K], chip * cc, cc, axis=1)
    loss = red[nv + 2 + CONV_K, 0]
    out.update({n: update(n) for n in WEIGHTS if n not in KIND})
    return (loss, dx[None], *[out[n][0] for n in WEIGHTS], *[out[n][1] for n in WEIGHTS],
            *[out[n][2] for n in WEIGHTS], *[out[n][3] for n in WEIGHTS])


def kernel(x, mem, g_ffn1, w_ffn1_gu, w_ffn1_down, g_mix, w_in, b_gate, conv_w, w_conv_out, w_attn_out, w_o, g_cross, g_mem, w_cq, w_ckv, w_co, g_ffn2, w_ffn2_gu, w_ffn2_down, g_final, loss_target, m_g_ffn1, m_w_ffn1_gu, m_w_ffn1_down, m_g_mix, m_w_in, m_b_gate, m_conv_w, m_w_conv_out, m_w_attn_out, m_w_o, m_g_cross, m_g_mem, m_w_cq, m_w_ckv, m_w_co, m_g_ffn2, m_w_ffn2_gu, m_w_ffn2_down, m_g_final, v_g_ffn1, v_w_ffn1_gu, v_w_ffn1_down, v_g_mix, v_w_in, v_b_gate, v_conv_w, v_w_conv_out, v_w_attn_out, v_w_o, v_g_cross, v_g_mem, v_w_cq, v_w_ckv, v_w_co, v_g_ffn2, v_w_ffn2_gu, v_w_ffn2_down, v_g_final):
    given = dict(locals())
    wts = {n: given[n] for n in WEIGHTS}
    m_in = {n: given["m_" + n] for n in WEIGHTS}
    v_in = {n: given["v_" + n] for n in WEIGHTS}
    return _step(x, mem, loss_target, wts, m_in, v_in)
```

```python
import math

import jax
import jax.numpy as jnp
from jax import lax
from jax.experimental import pallas as pl
from jax.experimental.pallas import tpu as pltpu

F32 = jnp.float32
BF16 = jnp.bfloat16
MESH = pl.DeviceIdType.MESH

V7X_VMEM_LIMIT_BYTES = 48 * 1024 * 1024
MM_VMEM_BUDGET_BYTES = 36 * 1024 * 1024
MM_WHOLE_K = 2816
DGRAD_NORM_VMEM_BYTES = 46 * 1024 * 1024
LANES = 128
SB_HEAD_DIM = 128
X_HEADS = 4
CONV_K = 3
RMS_EPS = 1e-6
N_CHIPS = 4
N_DEV = 8
ADAM_LR, ADAM_B1, ADAM_B2, ADAM_EPS, ADAM_WD, ADAM_STEP = 0.001, 0.9, 0.999, 1e-08, 0.01, 10


ANY = pl.BlockSpec(memory_space=pl.ANY)


def _pcall(body, **kw):
    return pl.pallas_call(body, **kw)


def _params(*sem):
    return pltpu.CompilerParams(dimension_semantics=sem, vmem_limit_bytes=V7X_VMEM_LIMIT_BYTES)


def _pick(dim, cands):
    for c in cands:
        if dim % c == 0:
            return c
    return dim


def _dot(a, b, ca, cb):
    return lax.dot_general(a, b, (((ca,), (cb,)), ((), ())), preferred_element_type=F32)


def _mm(a, b, *, name, ta=False, tb=False, out_dtype=BF16, res=None, alpha=1.0, tm=None, tn=None, tk=None, after=None,
        a_halves=False, b_halves=False):
    assert not (a_halves and ta) and not (b_halves and tb)
    if a_halves:
        m, k = a.shape[1], 2 * a.shape[2]
    else:
        m, k = (a.shape[1], a.shape[0]) if ta else a.shape
    if b_halves:
        n = 2 * b.shape[2]
        assert k == b.shape[1]
    else:
        n = b.shape[0] if tb else b.shape[1]
        assert k == (b.shape[1] if tb else b.shape[0]), (a.shape, b.shape, ta, tb)
    if ta:
        tm = tm or _pick(m, (512, 256, 128))
        tn = tn or _pick(n, (1024, 512, 256, 128))
        tk = tk or (k if k <= MM_WHOLE_K else _pick(k, (1024, 512, 256, 128)))
    else:
        tk = tk or (k if k <= MM_WHOLE_K else _pick(k, (MM_WHOLE_K, 2048, 1024, 512, 256, 128)))
        tn = tn or _pick(n, (512, 1408, 256, 128) if tk == k else (1024, 512, 256, 128))
        per_row = 2 * (tk * a.dtype.itemsize + tn * (jnp.dtype(out_dtype).itemsize + (0 if res is None else res.dtype.itemsize)))
        per_row += 4 * tn if tk < k else 0
        rows = (MM_VMEM_BUDGET_BYTES - 2 * tk * tn * b.dtype.itemsize) // per_row
        tm = tm or next((c for c in (2048, 1024, 512, 256, 128) if m % c == 0 and c <= rows), m)
    if a_halves:
        tk = min(tk, k // 2) if (k // 2) % min(tk, k // 2) == 0 else _pick(k // 2, (1408, 1024, 512, 256, 128))
    if b_halves:
        tn = tn if (n // 2) % tn == 0 else _pick(n // 2, (1408, 1024, 512, 256, 128))
    nk = k // tk
    assert m % tm == 0 and n % tn == 0 and k % tk == 0
    a_spec = pl.BlockSpec((tk, tm), lambda i, j, kk: (kk, i)) if ta else pl.BlockSpec((tm, tk), lambda i, j, kk: (i, kk))
    b_spec = pl.BlockSpec((tn, tk), lambda i, j, kk: (j, kk)) if tb else pl.BlockSpec((tk, tn), lambda i, j, kk: (kk, j))
    if a_halves:
        per = (k // 2) // tk
        a_spec = pl.BlockSpec((None, tm, tk), lambda i, j, kk: (kk // per, i, kk % per))
    if b_halves:
        per_n = (n // 2) // tn
        b_spec = pl.BlockSpec((None, tk, tn), lambda i, j, kk: (j // per_n, kk, j % per_n))
    o_spec = pl.BlockSpec((tm, tn), lambda i, j, kk: (i, j))
    ca, cb = (0 if ta else 1), (1 if tb else 0)

    n_in = 2 + (res is not None) + (after is not None)

    def body(*refs):
        a_ref, b_ref = refs[:2]
        res_ref = refs[2] if res is not None else None
        o_ref = refs[n_in]
        scratch = refs[n_in + 1:]

        def finish(acc):
            val = acc if alpha == 1.0 else alpha * acc
            if res_ref is not None:
                val = res_ref[...].astype(F32) + val
            o_ref[...] = val.astype(o_ref.dtype)

        part = _dot(a_ref[...].astype(BF16), b_ref[...].astype(BF16), ca, cb)
        if nk == 1:
            finish(part)
        else:
            acc_ref = scratch[0]
            kk = pl.program_id(2)

            @pl.when(kk == 0)
            def _():
                acc_ref[...] = part

            @pl.when(kk > 0)
            def _():
                acc_ref[...] += part

            @pl.when(kk == nk - 1)
            def _():
                finish(acc_ref[...])

    ins = [a, b] + ([] if res is None else [res]) + ([] if after is None else [after])
    in_specs = [a_spec, b_spec] + ([] if res is None else [o_spec]) + ([] if after is None else [ANY])
    return _pcall(
        body, name=name, grid=(m // tm, n // tn, nk), in_specs=in_specs, out_specs=o_spec,
        out_shape=jax.ShapeDtypeStruct((m, n), out_dtype),
        scratch_shapes=[pltpu.VMEM((tm, tn), F32)] if nk > 1 else [],
        compiler_params=_params("parallel", "parallel", "arbitrary"),
    )(*ins)


def _rowcall(fn, rows, consts, outs, accs=(), *, tm, name, after=None):
    s = rows[0][0].shape[0]
    assert s % tm == 0
    n_read, n_out = len(rows) + len(consts), len(outs)
    n_in = n_read + (after is not None)

    def body(*refs):
        vals = fn(*[r[...] for r in refs[:n_read]])
        vals = vals if isinstance(vals, (tuple, list)) else (vals,)
        for o_ref, v in zip(refs[n_in:n_in + n_out], vals[:n_out]):
            o_ref[...] = v.astype(o_ref.dtype)
        if accs:
            first = pl.program_id(0) == 0
            for a_ref, v in zip(refs[n_in + n_out:], vals[n_out:]):
                tot = jnp.sum(v.astype(F32), axis=0, keepdims=True)

                @pl.when(first)
                def _(a_ref=a_ref, tot=tot):
                    a_ref[...] = tot

                @pl.when(jnp.logical_not(first))
                def _(a_ref=a_ref, tot=tot):
                    a_ref[...] += tot

    in_specs = [pl.BlockSpec((tm, w), lambda i, cb=cb: (i, cb)) for (_, cb, w) in rows]
    in_specs += [pl.BlockSpec(c.shape, lambda i: (0, 0)) for c in consts]
    in_specs += [] if after is None else [ANY]
    out_specs = [pl.BlockSpec((tm, w), lambda i: (i, 0)) for (w, _) in outs]
    out_specs += [pl.BlockSpec((1, w), lambda i: (0, 0)) for w in accs]
    out_shape = [jax.ShapeDtypeStruct((s, w), dt) for (w, dt) in outs]
    out_shape += [jax.ShapeDtypeStruct((1, w), F32) for w in accs]
    return _pcall(
        body, name=name, grid=(s // tm,), in_specs=in_specs, out_specs=out_specs, out_shape=out_shape,
        compiler_params=_params("arbitrary" if accs else "parallel"),
    )(*[r[0] for r in rows], *consts, *([] if after is None else [after]))


def _whole(a):
    return (a, 0, a.shape[1])


def _xhat(x):
    x = x.astype(F32)
    r = lax.rsqrt(jnp.mean(x * x, axis=-1, keepdims=True) + RMS_EPS)
    return x * r, r


def _rms_bwd(dy, x, g):
    xh, r = _xhat(x)
    dxh = dy.astype(F32) * g
    dx = r * (dxh - xh * jnp.mean(dxh * xh, axis=-1, keepdims=True))
    return dx, dy.astype(F32) * xh


def _sigmoid(x):
    return 1.0 / (1.0 + jnp.exp(-x))


def _rms_fwd(x, g, name, tm, after=None):
    d = x.shape[1]
    return _rowcall(lambda xb, gb: _xhat(xb)[0] * gb, [_whole(x)], [g], [(d, BF16)], tm=tm, name=name, after=after)[0]


def _silu_parts(gate):
    sg = _sigmoid(gate)
    return sg, gate * sg


def _ffn_up(n, w_gu, name):
    s, d = n.shape
    f = w_gu.shape[1] // 2
    tn = _pick(f, (1408, 1024, 512, 256, 128))
    tm = _pick(s, (1024, 512, 256, 128))
    nb = f // tn

    def body(n_ref, wg_ref, wu_ref, gu_ref, act_ref):
        nv = n_ref[...]
        gate = _dot(nv, wg_ref[...], 1, 0)
        up = _dot(nv, wu_ref[...], 1, 0)
        gu_ref[0] = gate.astype(gu_ref.dtype)
        gu_ref[1] = up.astype(gu_ref.dtype)
        act_ref[...] = (_silu_parts(gate)[1] * up).astype(act_ref.dtype)

    return _pcall(
        body, name=name, grid=(s // tm, nb),
        in_specs=[pl.BlockSpec((tm, d), lambda i, j: (i, 0)), pl.BlockSpec((d, tn), lambda i, j: (0, j)),
                  pl.BlockSpec((d, tn), lambda i, j: (0, nb + j))],
        out_specs=[pl.BlockSpec((2, tm, tn), lambda i, j: (0, i, j)), pl.BlockSpec((tm, tn), lambda i, j: (i, j))],
        out_shape=[jax.ShapeDtypeStruct((2, s, f), BF16), jax.ShapeDtypeStruct((s, f), BF16)],
        compiler_params=_params("parallel", "parallel"),
    )(n, w_gu, w_gu)


def _ffn_dgu(dhb, w_down, gu, name, after=None):
    s, d = dhb.shape
    f = w_down.shape[0]
    tn = _pick(f, (1408, 1024, 512, 256, 128))
    tm = _pick(s, (1024, 512, 256, 128))

    def body(dh_ref, w_ref, gu_ref, *rest):
        o_ref = rest[-1]
        dact = _dot(dh_ref[...], w_ref[...], 1, 1)
        gate, up = gu_ref[0].astype(F32), gu_ref[1].astype(F32)
        sg, silu = _silu_parts(gate)
        o_ref[0] = (dact * up * (sg + silu * (1.0 - sg))).astype(o_ref.dtype)
        o_ref[1] = (dact * silu).astype(o_ref.dtype)

    blk = pl.BlockSpec((2, tm, tn), lambda i, j: (0, i, j))
    return _pcall(
        body, name=name, grid=(s // tm, f // tn),
        in_specs=[pl.BlockSpec((tm, d), lambda i, j: (i, 0)), pl.BlockSpec((tn, d), lambda i, j: (j, 0)), blk]
        + ([] if after is None else [ANY]),
        out_specs=blk, out_shape=jax.ShapeDtypeStruct((2, s, f), BF16), compiler_params=_params("parallel", "parallel"),
    )(dhb, w_down, gu, *([] if after is None else [after]))


def _dgrad_norm(dy, wmat, dh, x, g, name, *, dy_halves=False, copy_scale=None, after=None):
    s, d = dh.shape
    k = wmat.shape[1]
    kk_max = k // 2 if dy_halves else k
    fits = [(tm, tk) for tm in (1024, 512, 256, 128) if s % tm == 0
            for tk in (MM_WHOLE_K, 2048, 1408, 1024, 512, 256, 128) if kk_max % tk == 0
            and 2 * (2 * tk * (tm + d) + 14 * tm * d) + 4 * tm * d <= DGRAD_NORM_VMEM_BYTES]
    tm, tk = fits[0]
    nk, per = k // tk, (k // 2) // tk if dy_halves else 0
    n_in = 5 + (after is not None)
    n_out = 2 + (copy_scale is not None)

    def body(*refs):
        dy_ref, w_ref, dh_ref, x_ref, g_ref = refs[:5]
        outs, scratch = refs[n_in:n_in + n_out], refs[n_in + n_out:]
        i, kk = pl.program_id(0), pl.program_id(1)
        part = _dot(dy_ref[...], w_ref[...], 1, 1)

        def finish(dn):
            dx, dg = _rms_bwd(dn, x_ref[...], g_ref[...])
            tot = dh_ref[...] + dx
            outs[0][...] = tot
            if copy_scale is not None:
                outs[1][...] = (copy_scale * tot).astype(outs[1].dtype)
            dg = jnp.sum(dg, axis=0, keepdims=True)

            @pl.when(i == 0)
            def _():
                outs[-1][...] = dg

            @pl.when(i > 0)
            def _():
                outs[-1][...] += dg

        if nk == 1:
            finish(part)
        else:
            acc_ref = scratch[0]

            @pl.when(kk == 0)
            def _():
                acc_ref[...] = part

            @pl.when(kk > 0)
            def _():
                acc_ref[...] += part

            @pl.when(kk == nk - 1)
            def _():
                finish(acc_ref[...])

    row = pl.BlockSpec((tm, d), lambda i, kk: (i, 0))
    dy_spec = pl.BlockSpec((None, tm, tk), lambda i, kk: (kk // per, i, kk % per)) if dy_halves else pl.BlockSpec((tm, tk), lambda i, kk: (i, kk))
    in_specs = [dy_spec, pl.BlockSpec((d, tk), lambda i, kk: (0, kk)), row, row, pl.BlockSpec((1, d), lambda i, kk: (0, 0))]
    out_specs = [row] * (n_out - 1) + [pl.BlockSpec((1, d), lambda i, kk: (0, 0))]
    out_shape = [jax.ShapeDtypeStruct((s, d), F32)] + ([] if copy_scale is None else [jax.ShapeDtypeStruct((s, d), BF16)])
    return _pcall(
        body, name=name, grid=(s // tm, nk), in_specs=in_specs + ([] if after is None else [ANY]), out_specs=out_specs,
        out_shape=out_shape + [jax.ShapeDtypeStruct((1, d), F32)], scratch_shapes=[pltpu.VMEM((tm, d), F32)] if nk > 1 else [],
        compiler_params=_params("arbitrary", "arbitrary"),
    )(dy, wmat, dh, x, g, *([] if after is None else [after]))


def _shift_down(p, k):
    if k == 0:
        return p
    rows = lax.broadcasted_iota(jnp.int32, p.shape, 0)
    return jnp.where(rows >= k, pltpu.roll(p, k, 0), 0.0)


def _shift_up(p, k):
    if k == 0:
        return p
    s = p.shape[0]
    rows = lax.broadcasted_iota(jnp.int32, p.shape, 0)
    return jnp.where(rows < s - k, pltpu.roll(p, s - k, 0), 0.0)


def _conv_fwd(proj, conv_w, d, tc, name):
    s = proj.shape[0]
    nb = d // tc

    def body(cb_ref, cc_ref, cx_ref, w_ref, y_ref):
        p = cc_ref[...].astype(F32) * cx_ref[...].astype(F32)
        w = w_ref[...]
        acc = p * w[CONV_K - 1:CONV_K, :]
        for k in range(1, CONV_K):
            acc = acc + _shift_down(p, k) * w[CONV_K - 1 - k:CONV_K - k, :]
        y_ref[...] = (cb_ref[...].astype(F32) * acc).astype(y_ref.dtype)

    col = lambda off: pl.BlockSpec((s, tc), lambda j: (0, off * nb + j))
    return _pcall(
        body, name=name, grid=(nb,), in_specs=[col(0), col(1), col(2), pl.BlockSpec((CONV_K, tc), lambda j: (0, j))],
        out_specs=pl.BlockSpec((s, tc), lambda j: (0, j)), out_shape=jax.ShapeDtypeStruct((s, d), BF16),
        compiler_params=_params("parallel"),
    )(proj, proj, proj, conv_w)


def _conv_bwd(dy, proj, conv_w, d, tc, name):
    s = proj.shape[0]
    nb = d // tc

    def body(dy_ref, cb_ref, cc_ref, cx_ref, w_ref, dcb_ref, dcc_ref, dcx_ref, dw_ref):
        cc, cx = cc_ref[...].astype(F32), cx_ref[...].astype(F32)
        p = cc * cx
        w = w_ref[...]
        dyv = dy_ref[...].astype(F32)
        shifted = [_shift_down(p, CONV_K - 1 - k) for k in range(CONV_K)]
        conv = shifted[0] * w[0:1, :]
        for k in range(1, CONV_K):
            conv = conv + shifted[k] * w[k:k + 1, :]
        dcb_ref[...] = (dyv * conv).astype(dcb_ref.dtype)
        ds = dyv * cb_ref[...].astype(F32)
        dp = ds * w[CONV_K - 1:CONV_K, :]
        for k in range(1, CONV_K):
            dp = dp + _shift_up(ds, k) * w[CONV_K - 1 - k:CONV_K - k, :]
        dcc_ref[...] = (dp * cx).astype(dcc_ref.dtype)
        dcx_ref[...] = (dp * cc).astype(dcx_ref.dtype)
        for k in range(CONV_K):
            dw_ref[k:k + 1, :] = jnp.sum(ds * shifted[k], axis=0, keepdims=True)

    col = lambda off: pl.BlockSpec((s, tc), lambda j: (0, off * nb + j))
    blk = pl.BlockSpec((s, tc), lambda j: (0, j))
    wblk = pl.BlockSpec((CONV_K, tc), lambda j: (0, j))
    act = jax.ShapeDtypeStruct((s, d), BF16)
    return _pcall(
        body, name=name, grid=(nb,), in_specs=[blk, col(0), col(1), col(2), wblk],
        out_specs=[blk, blk, blk, wblk], out_shape=[act, act, act, jax.ShapeDtypeStruct((CONV_K, d), F32)],
        compiler_params=_params("parallel"),
    )(dy, proj, proj, proj, conv_w)


def _sb_tile(q, kj, scale, carry, tri, mask):
    z = _dot(q, kj, 1, 1) * scale
    lsz = jnp.minimum(z, 0.0) - jnp.log(1.0 + jnp.exp(-jnp.abs(z)))
    l1m = lsz - z
    if mask is not None:
        l1m = jnp.where(mask, l1m, 0.0)
    l1b = l1m.astype(BF16)
    a = jnp.exp(lsz + (carry + _dot(l1b, tri, 1, 0)))
    if mask is not None:
        a = jnp.where(mask, a, 0.0)
    return lsz, l1b, a.astype(BF16)


def _add_rows(x, upd, r0):
    return x + upd if r0 == 0 else jnp.concatenate([x[:r0], x[r0:] + upd], axis=0)


def _sb_masks(tq, tk):
    row = lax.broadcasted_iota(jnp.int32, (tq, tk), 0)
    col = lax.broadcasted_iota(jnp.int32, (tq, tk), 1)
    masks = [col + dj * tk < row for dj in range(tq // tk)]
    r2 = lax.broadcasted_iota(jnp.int32, (tk, tk), 0)
    c2 = lax.broadcasted_iota(jnp.int32, (tk, tk), 1)
    return masks, (r2 > c2).astype(BF16), (r2 < c2).astype(BF16)


def _sb_fwd(proj, heads, col0, tq, tk, name):
    s = proj.shape[0]
    dh = SB_HEAD_DIM
    nq, nd, nkt = s // tq, tq // tk, s // tk
    scale = dh ** -0.5

    def body(q_ref, k_ref, v_ref, o_ref, a_ref, b_ref):
        i = pl.program_id(1)
        q = q_ref[...]
        masks, tri_right, _ = _sb_masks(tq, tk)

        def tile(j, carry, acc, mask, r0=0):
            start = pl.multiple_of(j * tk, tk)
            kj = k_ref[pl.ds(start, tk), :]
            vj = v_ref[pl.ds(start, tk), :]
            lsz, l1b, ab = _sb_tile(q[r0:], kj, scale, carry[r0:], tri_right, None if mask is None else mask[r0:])
            a_ref[j, r0:, :] = ab
            b_ref[j, r0:, :] = jnp.exp(lsz).astype(b_ref.dtype)
            if r0:
                a_ref[j, :r0, :] = jnp.zeros((r0, tk), a_ref.dtype)
                b_ref[j, :r0, :] = jnp.zeros((r0, tk), b_ref.dtype)
            return (_add_rows(carry, jnp.sum(l1b.astype(F32), axis=1, keepdims=True), r0),
                    _add_rows(acc, _dot(ab, vj, 1, 0), r0))

        state = (jnp.zeros((tq, 1), F32), jnp.zeros((tq, dh), F32))
        for dj in reversed(range(nd)):
            state = tile(i * nd + dj, *state, masks[dj], dj * tk)
        def left_block(t, st):
            for dj in reversed(range(nd)):
                st = tile((i - 1 - t) * nd + dj, st[0], st[1], None)
            return st

        state = lax.fori_loop(0, i, left_block, state)
        o_ref[...] = state[1]

    qspec = pl.BlockSpec((tq, dh), lambda h, i: (i, col0[0] + h))
    kspec = pl.BlockSpec((s, dh), lambda h, i: (0, col0[1] + h))
    vspec = pl.BlockSpec((s, dh), lambda h, i: (0, col0[2] + h))
    saved = pl.BlockSpec((None, nkt, tq, tk), lambda h, i: (h, 0, i, 0))
    saved_shape = jax.ShapeDtypeStruct((heads, nkt, s, tk), BF16)
    return _pcall(
        body, name=name, grid=(heads, nq), in_specs=[qspec, kspec, vspec],
        out_specs=[pl.BlockSpec((tq, dh), lambda h, i: (i, h)), saved, saved],
        out_shape=[jax.ShapeDtypeStruct((s, heads * dh), F32), saved_shape, saved_shape],
        compiler_params=_params("parallel", "parallel"),
    )(proj, proj, proj)


def _sb_bwd(proj, o, a_all, beta_all, do, heads, col0, tq, tk, name):
    s = proj.shape[0]
    dh = SB_HEAD_DIM
    nq, nd, nkt = s // tq, tq // tk, s // tk
    scale = dh ** -0.5

    def body(q_ref, k_ref, v_ref, o_ref, a_ref, b_ref, do_ref, dq_ref, dk_ref, dv_ref, dk_acc, dv_acc):
        i = pl.program_id(1)

        @pl.when(i == 0)
        def _():
            dk_acc[...] = jnp.zeros_like(dk_acc)
            dv_acc[...] = jnp.zeros_like(dv_acc)

        q = q_ref[...]
        dob = do_ref[...].astype(BF16)
        delta = jnp.sum(dob.astype(F32) * o_ref[...], axis=1, keepdims=True)
        masks, _, tri_left = _sb_masks(tq, tk)

        def tile(j, carry_g, dq, mask):
            start = pl.multiple_of(j * tk, tk)
            kj = k_ref[pl.ds(start, tk), :]
            vj = v_ref[pl.ds(start, tk), :]
            ab = a_ref[j]
            g = _dot(dob, vj, 1, 1) * ab.astype(F32)
            carry_g = carry_g + jnp.sum(g, axis=1, keepdims=True)
            left = (delta - carry_g) + _dot(g.astype(BF16), tri_left, 1, 0)
            dz = g - b_ref[j].astype(F32) * (g + left)
            if mask is not None:
                dz = jnp.where(mask, dz, 0.0)
            dzb = dz.astype(BF16)
            dk_acc[pl.ds(start, tk), :] += _dot(dzb, q, 0, 0)
            dv_acc[pl.ds(start, tk), :] += _dot(ab, dob, 0, 0)
            return carry_g, dq + _dot(dzb, kj, 1, 0)

        state = (jnp.zeros((tq, 1), F32), jnp.zeros((tq, dh), F32))
        for dj in reversed(range(nd)):
            state = tile(i * nd + dj, *state, masks[dj])
        def left_block(t, st):
            for dj in reversed(range(nd)):
                st = tile((i - 1 - t) * nd + dj, st[0], st[1], None)
            return st

        state = lax.fori_loop(0, i, left_block, state)
        dq_ref[...] = (state[1] * scale).astype(dq_ref.dtype)

        @pl.when(i == nq - 1)
        def _():
            dk_ref[...] = (dk_acc[...] * scale).astype(dk_ref.dtype)
            dv_ref[...] = dv_acc[...].astype(dv_ref.dtype)

    qspec = pl.BlockSpec((tq, dh), lambda h, i: (i, col0[0] + h))
    kspec = pl.BlockSpec((s, dh), lambda h, i: (0, col0[1] + h))
    vspec = pl.BlockSpec((s, dh), lambda h, i: (0, col0[2] + h))
    blk = pl.BlockSpec((tq, dh), lambda h, i: (i, h))
    full = pl.BlockSpec((s, dh), lambda h, i: (0, h))
    saved = pl.BlockSpec((None, nkt, tq, tk), lambda h, i: (h, 0, i, 0))
    act = jax.ShapeDtypeStruct((s, heads * dh), BF16)
    return _pcall(
        body, name=name, grid=(heads, nq), in_specs=[qspec, kspec, vspec, blk, saved, saved, blk],
        out_specs=[blk, full, full], out_shape=[act, act, act],
        scratch_shapes=[pltpu.VMEM((s, dh), F32), pltpu.VMEM((s, dh), F32)],
        compiler_params=_params("parallel", "arbitrary"),
    )(proj, proj, proj, o, a_all, beta_all, do)


def _xattn_probs(q, k, scale):
    sc = _dot(q, k, 1, 1) * scale
    e = jnp.exp(sc - jnp.max(sc, axis=1, keepdims=True))
    return e / jnp.sum(e, axis=1, keepdims=True)


def _xattn_fwd(qc, kv, tq, name):
    s, d = qc.shape
    m = kv.shape[0]
    dh = d // X_HEADS
    scale = dh ** -0.5

    def body(q_ref, k_ref, v_ref, o_ref):
        p = _xattn_probs(q_ref[...], k_ref[...], scale)
        o_ref[...] = _dot(p.astype(BF16), v_ref[...], 1, 0).astype(o_ref.dtype)

    blk = pl.BlockSpec((tq, dh), lambda h, i: (i, h))
    return _pcall(
        body, name=name, grid=(X_HEADS, s // tq),
        in_specs=[blk, pl.BlockSpec((m, dh), lambda h, i: (0, h)), pl.BlockSpec((m, dh), lambda h, i: (0, X_HEADS + h))],
        out_specs=blk, out_shape=jax.ShapeDtypeStruct((s, d), BF16), compiler_params=_params("parallel", "parallel"),
    )(qc, kv, kv)


def _xattn_bwd(qc, kv, do, tq, name):
    s, d = qc.shape
    m = kv.shape[0]
    dh = d // X_HEADS
    scale = dh ** -0.5
    nq = s // tq

    def body(q_ref, k_ref, v_ref, do_ref, dq_ref, dk_ref, dv_ref, dk_acc, dv_acc):
        i = pl.program_id(1)
        q, k, v = q_ref[...], k_ref[...], v_ref[...]
        dob = do_ref[...].astype(BF16)
        p = _xattn_probs(q, k, scale)
        pb = p.astype(BF16)
        dp = _dot(dob, v, 1, 1)
        ds = pb.astype(F32) * (dp - jnp.sum(dp * pb.astype(F32), axis=1, keepdims=True))
        dsb = (ds * scale).astype(BF16)
        dq_ref[...] = _dot(dsb, k, 1, 0).astype(dq_ref.dtype)
        dk_part = _dot(dsb, q, 0, 0)
        dv_part = _dot(pb, dob, 0, 0)

        @pl.when(i == 0)
        def _():
            dk_acc[...] = dk_part
            dv_acc[...] = dv_part

        @pl.when(i > 0)
        def _():
            dk_acc[...] += dk_part
            dv_acc[...] += dv_part

        @pl.when(i == nq - 1)
        def _():
            dk_ref[...] = dk_acc[...].astype(dk_ref.dtype)
            dv_ref[...] = dv_acc[...].astype(dv_ref.dtype)

    blk = pl.BlockSpec((tq, dh), lambda h, i: (i, h))
    kblk = pl.BlockSpec((m, dh), lambda h, i: (0, h))
    return _pcall(
        body, name=name, grid=(X_HEADS, nq),
        in_specs=[blk, kblk, pl.BlockSpec((m, dh), lambda h, i: (0, X_HEADS + h)), blk],
        out_specs=[blk, kblk, kblk],
        out_shape=[jax.ShapeDtypeStruct((s, d), BF16), jax.ShapeDtypeStruct((m, d), BF16), jax.ShapeDtypeStruct((m, d), BF16)],
        scratch_shapes=[pltpu.VMEM((m, dh), F32), pltpu.VMEM((m, dh), F32)],
        compiler_params=_params("parallel", "arbitrary"),
    )(qc, kv, kv, do)


def _local_step(x, mem, tgt, w, fetch=None, prefetch=None, emit=None, tick=None, after=None):
    fetch = fetch or (lambda name, after: {})
    prefetch = prefetch or (lambda name, after: None)
    emit = emit or (lambda group, g: None)
    tick = tick or (lambda group, after: None)
    w = dict(w)
    s, d = x.shape
    heads = d // SB_HEAD_DIM
    tm = _pick(s, (512, 256, 128))
    tq = _pick(s, (1024, 512, 256, 128))
    sb_tq, sb_tk = _pick(s, (512, 256, 128)), _pick(s, (256, 128))
    tc = _pick(d, (256, 128))
    g = {}

    def wt(name, after):
        if name not in w:
            w.update(fetch(name, after))
        return w[name]

    def ffn_fwd(h, gname, wgu, wdown, tag, after=None):
        n = _rms_fwd(h, w[gname], tag + "_norm", tm, after=after)
        gu, act = _ffn_up(n, wt(wgu, n), tag + "_gu")
        prefetch(wdown, gu)
        return n, gu, act, _mm(act, wt(wdown, act), name=tag + "_down", out_dtype=F32, res=h, alpha=0.5)

    def ffn_bwd(dh, dhb, h, saved, gname, wgu, wdown, tag, copy_scale=None, after=None):
        n, gu, act = saved
        g[wdown] = _mm(act, dhb, ta=True, name=tag + "_dwdown", after=after)
        dgu = _ffn_dgu(dhb, w[wdown], gu, tag + "_dgu", after=emit(tag + "_down", g))
        g[wgu] = _mm(n, dgu, ta=True, b_halves=True, name=tag + "_dwgu", after=tick(tag + "_down", dgu))
        *dh_in, g[gname] = _dgrad_norm(dgu, w[wgu], dh, h, w[gname], tag + "_dn", dy_halves=True, copy_scale=copy_scale,
                                       after=emit(tag, g))
        return dh_in, tick(tag, dh_in[0])

    n1, gu1, act1, h1 = ffn_fwd(x, "g_ffn1", "w_ffn1_gu", "w_ffn1_down", "ffn1", after)
    prefetch("w_in", h1)
    u = _rms_fwd(h1, w["g_mix"], "mix_norm", tm)
    proj = _mm(u, wt("w_in", u), name="mix_in")
    prefetch("w_conv_out", proj)
    nd = d // SB_HEAD_DIM
    y_conv = _conv_fwd(proj, w["conv_w"], d, tc, "conv_fwd")
    sb_cols = (3 * nd, 4 * nd, 5 * nd)
    y_sb, sb_a, sb_beta = _sb_fwd(proj, heads, sb_cols, _pick(s, (2 * sb_tq, sb_tq)), sb_tk, "sb_fwd")
    prefetch("w_cq", y_sb)
    a_conv = _mm(y_conv, wt("w_conv_out", y_conv), name="conv_out")
    a_sb = _mm(y_sb, wt("w_attn_out", y_sb), name="attn_out")
    b_conv, b_sb = w["b_gate"][:, :d], w["b_gate"][:, d:]

    def merge(ac, asb, gcp, gsp, bc, bs):
        gc = _sigmoid(gcp.astype(F32) + bc)
        gs = _sigmoid(gsp.astype(F32) + bs)
        return gc * ac.astype(F32) + gs * asb.astype(F32)

    merged = _rowcall(merge, [_whole(a_conv), _whole(a_sb), (proj, 6, d), (proj, 7, d)], [b_conv, b_sb], [(d, BF16)],
                      tm=tm, name="merge")[0]
    prefetch("w_ffn2_gu", merged)
    h2 = _mm(merged, wt("w_o", merged), name="mix_out", out_dtype=F32, res=h1)
    hn = _rms_fwd(h2, w["g_cross"], "cross_norm", tm)
    mn = _rms_fwd(mem, w["g_mem"], "mem_norm", _pick(mem.shape[0], (256, 128)))
    qc = _mm(hn, wt("w_cq", hn), name="cross_q")
    kv = _mm(mn, wt("w_ckv", mn), name="cross_kv")
    oc = _xattn_fwd(qc, kv, tq, "xattn_fwd")
    h3 = _mm(oc, wt("w_co", oc), name="cross_out", out_dtype=F32, res=h2)
    n2, gu2, act2, h4 = ffn_fwd(h3, "g_ffn2", "w_ffn2_gu", "w_ffn2_down", "ffn2")

    def head(hb, tb, gb):
        xh, r = _xhat(hb)
        err = xh * gb - tb
        dy = err * (1.0 / d)
        dxh = dy * gb
        dx = r * (dxh - xh * jnp.mean(dxh * xh, axis=-1, keepdims=True))
        row_loss = 0.5 * jnp.mean(err * err, axis=-1, keepdims=True)
        return dx, 0.5 * dx, dy * xh, jnp.broadcast_to(row_loss, (row_loss.shape[0], LANES))

    dh4, dh4b, g["g_final"], loss_lanes = _rowcall(head, [_whole(h4), _whole(tgt)], [w["g_final"]], [(d, F32), (d, BF16)],
                                                   [d, LANES], tm=tm, name="loss_head")

    (dh3, dh3b), tok = ffn_bwd(dh4, dh4b, h3, (n2, gu2, act2), "g_ffn2", "w_ffn2_gu", "w_ffn2_down", "ffn2", copy_scale=1.0)
    g["w_co"] = _mm(oc, dh3b, ta=True, name="cross_dwco", after=tok)
    doc = _mm(dh3b, w["w_co"], tb=True, name="cross_doc")
    dqc, dk, dv = _xattn_bwd(qc, kv, doc, tq, "xattn_bwd")
    dkv = jnp.concatenate([dk, dv], axis=1)
    g["w_cq"] = _mm(hn, dqc, ta=True, name="cross_dwcq")
    g["w_ckv"] = _mm(mn, dkv, ta=True, name="cross_dwckv")
    dmn = _mm(dkv, w["w_ckv"], tb=True, name="cross_dmn", out_dtype=F32)
    g["g_mem"] = _rowcall(lambda dy, xb: dy * _xhat(xb)[0], [_whole(dmn), _whole(mem)], [], [], [d],
                          tm=_pick(mem.shape[0], (256, 128)), name="mem_dnorm")[0]
    dh2, dh2b, g["g_cross"] = _dgrad_norm(dqc, w["w_cq"], dh3, h2, w["g_cross"], "cross_dhn", copy_scale=1.0, after=emit("cross", g))

    g["w_o"] = _mm(merged, dh2b, ta=True, name="mix_dwo", after=tick("cross", dh2))
    dmerged = _mm(dh2b, w["w_o"], tb=True, name="mix_dmerged")

    def merge_bwd(dm, ac, asb, gcp, gsp, bc, bs):
        dm, ac, asb = dm.astype(F32), ac.astype(F32), asb.astype(F32)
        gc = _sigmoid(gcp.astype(F32) + bc)
        gs = _sigmoid(gsp.astype(F32) + bs)
        dgc = dm * ac * gc * (1.0 - gc)
        dgs = dm * asb * gs * (1.0 - gs)
        return dm * gc, dm * gs, dgc, dgs, dgc, dgs

    da_conv, da_sb, dgc, dgs, db_conv, db_sb = _rowcall(
        merge_bwd, [_whole(dmerged), _whole(a_conv), _whole(a_sb), (proj, 6, d), (proj, 7, d)], [b_conv, b_sb],
        [(d, BF16)] * 4, [d, d], tm=tm, name="merge_bwd")
    g["b_gate"] = jnp.concatenate([db_conv, db_sb], axis=1)
    g["w_conv_out"] = _mm(y_conv, da_conv, ta=True, name="conv_dwout")
    g["w_attn_out"] = _mm(y_sb, da_sb, ta=True, name="attn_dwout")
    dy_conv = _mm(da_conv, w["w_conv_out"], tb=True, name="conv_dy")
    dy_sb = _mm(da_sb, w["w_attn_out"], tb=True, name="attn_dy")
    dcb, dcc, dcx, g["conv_w"] = _conv_bwd(dy_conv, proj, w["conv_w"], d, tc, "conv_bwd")
    dq, dk_sb, dv_sb = _sb_bwd(proj, y_sb, sb_a, sb_beta, dy_sb, heads, sb_cols, sb_tq, sb_tk, "sb_bwd")
    dproj = jnp.concatenate([dcb, dcc, dcx, dq, dk_sb, dv_sb, dgc, dgs], axis=1)
    g["w_in"] = _mm(u, dproj, ta=True, name="mix_dwin")
    dh1, dh1b, g["g_mix"] = _dgrad_norm(dproj, w["w_in"], dh2, h1, w["g_mix"], "mix_du", copy_scale=0.5, after=emit("mix", g))
    (dx,), tok = ffn_bwd(dh1, dh1b, x, (n1, gu1, act1), "g_ffn1", "w_ffn1_gu", "w_ffn1_down", "ffn1", after=tick("mix", dh1))
    return loss_lanes, dx, g, tok


MATS = (("w_ffn1_gu", "col"), ("w_ffn1_down", "row"), ("w_in", "col"), ("w_conv_out", "row"), ("w_attn_out", "row"),
        ("w_o", "row"), ("w_cq", "row"), ("w_ckv", "col"), ("w_co", "row"), ("w_ffn2_gu", "col"), ("w_ffn2_down", "row"))
VECS = ("g_ffn1", "g_mix", "g_cross", "g_mem", "g_ffn2", "g_final")
WEIGHTS = ("g_ffn1", "w_ffn1_gu", "w_ffn1_down", "g_mix", "w_in", "b_gate", "conv_w", "w_conv_out", "w_attn_out", "w_o",
           "g_cross", "g_mem", "w_cq", "w_ckv", "w_co", "g_ffn2", "w_ffn2_gu", "w_ffn2_down", "g_final")
CONV_ROWS = 16


def _full_shape(kind, r, c):
    return (r, N_CHIPS * c) if kind == "col" else (N_CHIPS * r, c)


def _piece(ref, kind, r, c, chip, half):
    hr = r // 2
    if kind == "col":
        return ref.at[pl.ds(pl.multiple_of(half * hr, math.gcd(hr, 16)), hr), pl.ds(pl.multiple_of(chip * c, LANES), c)]
    return ref.at[pl.ds(pl.multiple_of(chip * r + half * hr, math.gcd(hr, 16)), hr), :]


def _shard_of(ref, kind, r, c, chip):
    if kind == "col":
        return ref.at[:, pl.ds(pl.multiple_of(chip * c, LANES), c)]
    return ref.at[pl.ds(pl.multiple_of(chip * r, 16), r), :]


def _place():
    x, y, c = lax.axis_index("x"), lax.axis_index("y"), lax.axis_index("c")
    others = [(1 - x, y), (x, 1 - y), (1 - x, 1 - y)]
    return x, y, c, 2 * x + y, others


def _remote(src, dst, send_sem, recv_sem, to):
    return pltpu.make_async_remote_copy(src_ref=src, dst_ref=dst, send_sem=send_sem, recv_sem=recv_sem,
                                        device_id=to, device_id_type=MESH)


HBM = pl.BlockSpec(memory_space=pltpu.HBM)
SEM = pl.BlockSpec(memory_space=pltpu.SEMAPHORE)
EFFECT = pltpu.SideEffectType.DATAFLOW_SIDE_EFFECTING
TOKEN = (8, LANES)


def _split_start(name, plan, n_copies, srcs, lands, after=None):
    ns, nl = len(srcs), len(lands)
    n_in = ns + nl + (after is not None)

    def body(*refs):
        outs = refs[n_in:]
        sends, _ = plan(refs[:ns], refs[ns:ns + nl], outs[0], outs[1])
        for cp in sends:
            cp.start()
        outs[-1][...] = jnp.zeros(TOKEN, F32)

    held = [pltpu.HBM(a.shape, a.dtype) for a in (*srcs, *lands)]
    dma = pltpu.SemaphoreType.DMA((n_copies,))
    ins = [pltpu.with_memory_space_constraint(a, pltpu.HBM) for a in (*srcs, *lands)]
    outs = _pcall(
        body, name=name, in_specs=[HBM] * (ns + nl) + ([] if after is None else [ANY]),
        out_specs=(SEM, SEM, *[HBM] * (ns + nl), pl.BlockSpec(memory_space=pltpu.VMEM)),
        out_shape=(dma, dma, *held, jax.ShapeDtypeStruct(TOKEN, F32)),
        input_output_aliases={i: 2 + i for i in range(ns + nl)},
        compiler_params=pltpu.CompilerParams(has_side_effects=EFFECT),
    )(*ins, *([] if after is None else [after]))
    return outs[0], outs[1], list(outs[2:2 + ns]), list(outs[2 + ns:2 + ns + nl]), outs[-1]


def _split_wait(name, plan, send_sems, recv_sems, srcs, lands, after):
    ns, nl = len(srcs), len(lands)

    def body(*refs):
        sends, recvs = plan(refs[:ns], refs[ns:ns + nl], refs[ns + nl], refs[ns + nl + 1])
        for cp in sends:
            cp.wait_send()
        for cp in recvs:
            cp.wait_recv()

    outs = _pcall(
        body, name=name, in_specs=[HBM] * (ns + nl) + [SEM, SEM, ANY], out_specs=[HBM] * (ns + nl),
        out_shape=[pltpu.HBM(a.shape, a.dtype) for a in (*srcs, *lands)],
        input_output_aliases={i: i for i in range(ns + nl)},
        compiler_params=pltpu.CompilerParams(has_side_effects=EFFECT),
    )(*srcs, *lands, send_sems, recv_sems, after)
    return list(outs[:ns]), list(outs[ns:])


def _gather_plan(dims):
    def plan(shard_refs, full_refs, ss, rs):
        x, y, c, me, others = _place()
        sends, recvs = [], []
        for wi, (kind, r, cw) in enumerate(dims):
            half = shard_refs[wi].at[pl.ds(pl.multiple_of(c * (r // 2), math.gcd(r // 2, 16)), r // 2), :]
            for k, (ox, oy) in enumerate(others):
                sem = 4 * wi + k
                sends.append(_remote(half, _piece(full_refs[wi], kind, r, cw, me, c), ss.at[sem], rs.at[sem], (ox, oy, c)))
                recvs.append(_remote(half, _piece(full_refs[wi], kind, r, cw, 2 * ox + oy, c), ss.at[sem], rs.at[sem], (x, y, c)))
            sem = 4 * wi + 3
            own = _remote(shard_refs[wi], _shard_of(full_refs[wi], kind, r, cw, me), ss.at[sem], rs.at[sem], (x, y, 1 - c))
            sends.append(own)
            recvs.append(own)
        return sends, recvs

    return plan


def _forward_plan(dims):
    def plan(_, full_refs, ss, rs):
        x, y, c, _, others = _place()
        sends, recvs = [], []
        for wi, (kind, r, cw) in enumerate(dims):
            for k, (ox, oy) in enumerate(others):
                sem = 3 * wi + k
                mine = _piece(full_refs[wi], kind, r, cw, 2 * ox + oy, c)
                theirs = _piece(full_refs[wi], kind, r, cw, 2 * ox + oy, 1 - c)
                sends.append(_remote(mine, mine, ss.at[sem], rs.at[sem], (x, y, 1 - c)))
                recvs.append(_remote(theirs, theirs, ss.at[sem], rs.at[sem], (x, y, 1 - c)))
        return sends, recvs

    return plan


def _rs_cores_plan(dims):
    def plan(g_refs, land_refs, ss, rs):
        x, y, c, _, _ = _place()
        sends, recvs = [], []
        for wi, dm in enumerate(dims):
            for chip in range(N_CHIPS):
                sem = N_CHIPS * wi + chip
                sends.append(_remote(_piece(g_refs[wi], *dm, chip, 1 - c), land_refs[wi].at[chip], ss.at[sem], rs.at[sem], (x, y, 1 - c)))
                recvs.append(_remote(_piece(g_refs[wi], *dm, chip, c), land_refs[wi].at[chip], ss.at[sem], rs.at[sem], (x, y, 1 - c)))
        return sends, recvs

    return plan


def _share_plan(nw):
    def plan(_, buf_refs, ss, rs):
        x, y, c, _, _ = _place()
        sends = [_remote(buf_refs[wi].at[c], buf_refs[wi].at[c], ss.at[wi], rs.at[wi], (x, y, 1 - c)) for wi in range(nw)]
        recvs = [_remote(buf_refs[wi].at[1 - c], buf_refs[wi].at[1 - c], ss.at[wi], rs.at[wi], (x, y, 1 - c)) for wi in range(nw)]
        return sends, recvs

    return plan


def _small_plan():
    def plan(_, buf_refs, ss, rs):
        x, y, c = lax.axis_index("x"), lax.axis_index("y"), lax.axis_index("c")
        buf = buf_refs[0]
        sends, recvs = [], []
        for rel in range(1, N_DEV):
            peer = (x ^ (rel >> 2 & 1), y ^ (rel >> 1 & 1), c ^ (rel & 1))
            sends.append(_remote(buf.at[0], buf.at[rel], ss.at[rel - 1], rs.at[rel - 1], peer))
            recvs.append(_remote(buf.at[0], buf.at[rel], ss.at[rel - 1], rs.at[rel - 1], peer))
        return sends, recvs

    return plan


def _sum_small(buf, me, name):
    _, rows, n = buf.shape

    def body(me_ref, b_ref, o_ref):
        tot = b_ref[me_ref[0]]
        for dev in range(1, N_DEV):
            tot = tot + b_ref[dev ^ me_ref[0]]
        o_ref[...] = tot

    return _pcall(
        body, name=name, out_shape=jax.ShapeDtypeStruct((rows, n), F32),
        grid_spec=pltpu.PrefetchScalarGridSpec(
            num_scalar_prefetch=1, grid=(1,), in_specs=[pl.BlockSpec((N_DEV, rows, n), lambda i, m: (0, 0, 0))],
            out_specs=pl.BlockSpec((rows, n), lambda i, m: (0, 0))),
    )(me, buf)


def _rs_chips_plan(nw):
    def plan(p_refs, land_refs, ss, rs):
        x, y, c, me, others = _place()
        sends, recvs = [], []
        for wi in range(nw):
            for k, (ox, oy) in enumerate(others):
                sem = 3 * wi + k
                sends.append(_remote(p_refs[wi].at[2 * ox + oy], land_refs[wi].at[k], ss.at[sem], rs.at[sem], (ox, oy, c)))
                recvs.append(_remote(p_refs[wi].at[me], land_refs[wi].at[k], ss.at[sem], rs.at[sem], (x, y, c)))
        return sends, recvs

    return plan


def _rows_per_block(n, c, limit_bytes=2 << 20):
    best = None
    for tm in range(16, n + 1, 16):
        if n % tm == 0 and tm * c * 4 <= limit_bytes:
            best = tm
    return best or n


def _sum_cores(grad, got, kind, place, name):
    _, hr, cw = got.shape
    tm = _rows_per_block(hr, cw)
    nb = hr // tm

    def body(place_ref, g_ref, t_ref, o_ref):
        o_ref[...] = (g_ref[...].astype(F32) + t_ref[...].astype(F32)).astype(o_ref.dtype)

    if kind == "col":
        g_spec = pl.BlockSpec((tm, cw), lambda j, i, pr: (pr[0] * nb + i, j))
    else:
        g_spec = pl.BlockSpec((tm, cw), lambda j, i, pr: ((2 * j + pr[0]) * nb + i, 0))
    blk = pl.BlockSpec((None, tm, cw), lambda j, i, pr: (j, i, 0))
    return _pcall(
        body, name=name, out_shape=jax.ShapeDtypeStruct(got.shape, BF16),
        grid_spec=pltpu.PrefetchScalarGridSpec(num_scalar_prefetch=1, grid=(N_CHIPS, nb), in_specs=[g_spec, blk], out_specs=blk),
        compiler_params=_params("parallel", "parallel"),
    )(place, grad, got)


def _sum_chips(parts, got, place, name):
    _, n, cw = got.shape
    tm = _rows_per_block(n, cw)

    def body(place_ref, p_ref, g_ref, o_ref):
        tot = p_ref[...].astype(F32)
        for k in range(3):
            tot = tot + g_ref[k].astype(F32)
        o_ref[...] = tot

    return _pcall(
        body, name=name, out_shape=jax.ShapeDtypeStruct((2, n, cw), F32),
        grid_spec=pltpu.PrefetchScalarGridSpec(
            num_scalar_prefetch=1, grid=(n // tm,),
            in_specs=[pl.BlockSpec((None, tm, cw), lambda i, pr: (pr[1], i, 0)), pl.BlockSpec((3, tm, cw), lambda i, pr: (0, i, 0))],
            out_specs=pl.BlockSpec((None, tm, cw), lambda i, pr: (pr[0], i, 0))),
        compiler_params=_params("parallel"),
    )(place, parts, got)


def _adamw(g, w, m, v, name):
    n, c = g.shape
    c1 = 1.0 - ADAM_B1 ** ADAM_STEP
    c2 = 1.0 - ADAM_B2 ** ADAM_STEP

    def fn(gb, wb, mb, vb):
        m_new = ADAM_B1 * mb + (1.0 - ADAM_B1) * gb
        v_new = ADAM_B2 * vb + (1.0 - ADAM_B2) * (gb * gb)
        delta = -ADAM_LR * ((m_new / c1) / (jnp.sqrt(v_new / c2) + ADAM_EPS) + ADAM_WD * wb)
        return gb, delta, m_new, v_new

    tm = _rows_per_block(n, c) if n % 16 == 0 else n
    return _rowcall(fn, [_whole(g), _whole(w), _whole(m), _whole(v)], [], [(c, F32)] * 4, tm=tm, name=name)


PACK_ROWS = 16


def _pack_rows(parts, width, name, after=None):
    assert sum(p.shape[0] for p in parts) <= PACK_ROWS

    def body(*refs):
        out_ref = refs[-1]
        out_ref[...] = jnp.zeros_like(out_ref)
        at = 0
        for r in refs[:len(parts)]:
            k, n = r.shape
            if n == width:
                out_ref[at:at + k, :] = r[...]
            else:
                out_ref[at:at + k, :] = jnp.broadcast_to(r[:, :1], (k, width))
            at += k

    vm = pl.BlockSpec(memory_space=pltpu.VMEM)
    return _pcall(body, name=name, in_specs=[vm] * len(parts) + ([] if after is None else [ANY]), out_specs=vm,
                  out_shape=jax.ShapeDtypeStruct((PACK_ROWS, width), F32))(*parts, *([] if after is None else [after]))


def _cast_shard(wm, name, after):
    n, c = wm.shape
    return _rowcall(lambda v: v, [_whole(wm)], [], [(c, BF16)], tm=_rows_per_block(n, c), name=name, after=after)[0]


GATHER_GROUPS = (
    ("w_ffn1_gu", "conv_w"), ("w_ffn1_down",), ("w_in",), ("w_conv_out", "w_attn_out", "w_o"), ("w_cq", "w_ckv", "w_co"),
    ("w_ffn2_gu", "w_ffn2_down"),
)
REDUCE_GROUPS = {
    "ffn2": ("w_ffn2_down", "w_ffn2_gu"),
    "cross": ("w_co", "w_cq", "w_ckv"),
    "mix": ("w_o", "w_conv_out", "w_attn_out", "w_in"),
    "ffn1_down": ("w_ffn1_down",),
    "ffn1": ("w_ffn1_gu",),
}
TAIL_STAGES = (("ffn2", "cross"), ("mix",), ("ffn1_down", "ffn1"))
KIND = dict(MATS)


def _step(x, mem, tgt, wts, m_in, v_in):
    d = x.shape[-1]
    cc = wts["conv_w"].shape[1]
    place = jnp.stack([lax.axis_index("c"), 2 * lax.axis_index("x") + lax.axis_index("y")]).astype(jnp.int32)
    dims = {n: (kind, *wts[n].shape) for n, kind in MATS}
    dims["conv_w"] = ("col", CONV_ROWS, cc)

    w = {n: wts[n].reshape(1, -1) for n in VECS + ("b_gate",)}
    flying, token = {}, None
    for names in GATHER_GROUPS:
        gd = [dims[n] for n in names]
        shards = [jnp.pad(wts[n], ((0, CONV_ROWS - CONV_K), (0, 0))) if n == "conv_w" else _cast_shard(wts[n], "cast_" + n, token)
                  for n in names]
        lands = [lax.empty(_full_shape(*dm), sh.dtype) for dm, sh in zip(gd, shards)]
        plan = _gather_plan(gd)
        ss, rs, srcs, lands, token = _split_start("gather_start_" + names[0], plan, 4 * len(names), shards, lands, token)
        flying.update({n: (names, plan, ss, rs, srcs, lands, gd) for n in names})

    passing = {}

    def prefetch(name, after):
        if name not in passing:
            names, plan, ss, rs, srcs, lands, gd = flying[name]
            _, lands = _split_wait("gather_wait_" + names[0], plan, ss, rs, srcs, lands, after)
            plan = _forward_plan(gd)
            ss, rs, _, lands, _ = _split_start("forward_start_" + names[0], plan, 3 * len(names), [], lands)
            passing.update({n: (names, plan, ss, rs, lands) for n in names})

    def fetch(name, after):
        prefetch(name, after)
        names, plan, ss, rs, lands = passing[name]
        _, lands = _split_wait("forward_wait_" + names[0], plan, ss, rs, [], lands, after)
        return {n: (land[:CONV_K] if n == "conv_w" else land) for n, land in zip(names, lands)}

    swapping, sent = {}, {}

    def emit(tag, g):
        if tag not in REDUCE_GROUPS:
            return None
        names = REDUCE_GROUPS[tag]
        gd = [dims[n] for n in names]
        lands = [lax.empty((N_CHIPS, r // 2, cw), BF16) for (_, r, cw) in gd]
        plan = _rs_cores_plan(gd)
        ss, rs, srcs, lands, tok = _split_start("rs_cores_start_" + tag, plan, N_CHIPS * len(names), [g[n] for n in names], lands)
        swapping[tag] = (plan, ss, rs, srcs, lands)
        return tok

    def tick(tag, after):
        if tag not in REDUCE_GROUPS:
            return None
        names = REDUCE_GROUPS[tag]
        plan, ss, rs, srcs, lands = swapping[tag]
        mine, got = _split_wait("rs_cores_wait_" + tag, plan, ss, rs, srcs, lands, after)
        parts = [_sum_cores(gm, t, KIND[n], place, "sum_cores_" + n) for n, gm, t in zip(names, mine, got)]
        lands = [lax.empty((3, *p.shape[1:]), BF16) for p in parts]
        plan = _rs_chips_plan(len(names))
        ss, rs, srcs, lands, tok = _split_start("rs_chips_start_" + tag, plan, 3 * len(names), parts, lands)
        sent[tag] = (plan, ss, rs, srcs, lands)
        return tok

    loss_lanes, dx, g, last = _local_step(x[0], mem[0], tgt[0], w, fetch, prefetch, emit, tick, token)

    rows = [g[n] for n in VECS] + [g["b_gate"][:, :d], g["b_gate"][:, d:], g["conv_w"], loss_lanes]
    packed = _pack_rows(rows, d, "pack_small", after=last)
    small = jnp.concatenate([packed[None], jnp.zeros((N_DEV - 1, *packed.shape), F32)], axis=0)
    small_plan = _small_plan()
    small_ss, small_rs, _, small, after = _split_start("small_start", small_plan, N_DEV - 1, [], [small])

    grads, out = {}, {}

    def update(n):
        shape = wts[n].shape
        as2d = (lambda a: a.reshape(1, -1)) if len(shape) == 1 else (lambda a: a)
        return [r.reshape(shape) for r in _adamw(grads[n], as2d(wts[n]), as2d(m_in[n]), as2d(v_in[n]), "adamw_" + n)]

    def finish(sharing, after):
        tag, names, plan, ss, rs, halves = sharing
        _, both = _split_wait("share_wait_" + tag, plan, ss, rs, [], halves, after)
        for n, b in zip(names, both):
            grads[n] = b.reshape(-1, b.shape[-1])
            out[n] = update(n)
        return out[names[-1]][1]

    sharing = None
    for stage in TAIL_STAGES:
        names, halves = [], []
        for tag in stage:
            plan, ss, rs, srcs, lands = sent[tag]
            parts, landed = _split_wait("rs_chips_wait_" + tag, plan, ss, rs, srcs, lands, after)
            halves += [_sum_chips(p, t, place, "sum_chips_" + n) for n, p, t in zip(REDUCE_GROUPS[tag], parts, landed)]
            names += REDUCE_GROUPS[tag]
        plan = _share_plan(len(names))
        ss, rs, _, halves, after = _split_start("share_start_" + stage[0], plan, len(names), [], halves)
        if sharing is not None:
            after = finish(sharing, after)
        sharing = (stage[0], names, plan, ss, rs, halves)
    after = finish(sharing, after)

    _, small = _split_wait("small_wait", small_plan, small_ss, small_rs, [], small, after)
    me = (4 * lax.axis_index("x") + 2 * lax.axis_index("y") + lax.axis_index("c")).astype(jnp.int32).reshape(1)
    red = _sum_small(small[0], me, "sum_small")
    grads.update({n: red[i:i + 1] for i, n in enumerate(VECS)})
    nv = len(VECS)
    grads["b_gate"] = jnp.concatenate([red[nv:nv + 1], red[nv + 1:nv + 2]], axis=1)
    chip = 2 * lax.axis_index("x") + lax.axis_index("y")
    grads["conv_w"] = lax.dynamic_slice_in_dim(red[nv + 2:nv + 2 + CONV_K], chip * cc, cc, axis=1)
    loss = red[nv + 2 + CONV_K, 0]
    out.update({n: update(n) for n in WEIGHTS if n not in KIND})
    return (loss, dx[None], *[out[n][0] for n in WEIGHTS], *[out[n][1] for n in WEIGHTS],
            *[out[n][2] for n in WEIGHTS], *[out[n][3] for n in WEIGHTS])


def kernel(x, mem, g_ffn1, w_ffn1_gu, w_ffn1_down, g_mix, w_in, b_gate, conv_w, w_conv_out, w_attn_out, w_o, g_cross, g_mem, w_cq, w_ckv, w_co, g_ffn2, w_ffn2_gu, w_ffn2_down, g_final, loss_target, m_g_ffn1, m_w_ffn1_gu, m_w_ffn1_down, m_g_mix, m_w_in, m_b_gate, m_conv_w, m_w_conv_out, m_w_attn_out, m_w_o, m_g_cross, m_g_mem, m_w_cq, m_w_ckv, m_w_co, m_g_ffn2, m_w_ffn2_gu, m_w_ffn2_down, m_g_final, v_g_ffn1, v_w_ffn1_gu, v_w_ffn1_down, v_g_mix, v_w_in, v_b_gate, v_conv_w, v_w_conv_out, v_w_attn_out, v_w_o, v_g_cross, v_g_mem, v_w_cq, v_w_ckv, v_w_co, v_g_ffn2, v_w_ffn2_gu, v_w_ffn2_down, v_g_final):
    given = dict(locals())
    wts = {n: given[n] for n in WEIGHTS}
    m_in = {n: given["m_" + n] for n in WEIGHTS}
    v_in = {n: given["v_" + n] for n in WEIGHTS}
    return _step(x, mem, loss_target, wts, m_in, v_in)
```

```python
import math

import jax
import jax.numpy as jnp
from jax import lax
from jax.experimental import pallas as pl
from jax.experimental.pallas import tpu as pltpu

F32 = jnp.float32
BF16 = jnp.bfloat16
MESH = pl.DeviceIdType.MESH

V7X_VMEM_LIMIT_BYTES = 48 * 1024 * 1024
MM_VMEM_BUDGET_BYTES = 36 * 1024 * 1024
MM_WHOLE_K = 2816
LANES = 128
SB_HEAD_DIM = 128
X_HEADS = 4
CONV_K = 3
RMS_EPS = 1e-6
N_CHIPS = 4
N_DEV = 8
ADAM_LR, ADAM_B1, ADAM_B2, ADAM_EPS, ADAM_WD, ADAM_STEP = 0.001, 0.9, 0.999, 1e-08, 0.01, 10


ANY = pl.BlockSpec(memory_space=pl.ANY)


def _pcall(body, **kw):
    return pl.pallas_call(body, **kw)


def _params(*sem):
    return pltpu.CompilerParams(dimension_semantics=sem, vmem_limit_bytes=V7X_VMEM_LIMIT_BYTES)


def _pick(dim, cands):
    for c in cands:
        if dim % c == 0:
            return c
    return dim


def _dot(a, b, ca, cb):
    return lax.dot_general(a, b, (((ca,), (cb,)), ((), ())), preferred_element_type=F32)


def _mm(a, b, *, name, ta=False, tb=False, out_dtype=BF16, res=None, alpha=1.0, tm=None, tn=None, tk=None, after=None,
        a_halves=False, b_halves=False):
    assert not (a_halves and ta) and not (b_halves and tb)
    if a_halves:
        m, k = a.shape[1], 2 * a.shape[2]
    else:
        m, k = (a.shape[1], a.shape[0]) if ta else a.shape
    if b_halves:
        n = 2 * b.shape[2]
        assert k == b.shape[1]
    else:
        n = b.shape[0] if tb else b.shape[1]
        assert k == (b.shape[1] if tb else b.shape[0]), (a.shape, b.shape, ta, tb)
    if ta:
        tm = tm or _pick(m, (512, 256, 128))
        tn = tn or _pick(n, (1024, 512, 256, 128))
        tk = tk or (k if k <= MM_WHOLE_K else _pick(k, (1024, 512, 256, 128)))
    else:
        tk = tk or (k if k <= MM_WHOLE_K else _pick(k, (MM_WHOLE_K, 2048, 1024, 512, 256, 128)))
        tn = tn or _pick(n, (512, 1408, 256, 128) if tk == k else (1024, 512, 256, 128))
        per_row = 2 * (tk * a.dtype.itemsize + tn * (jnp.dtype(out_dtype).itemsize + (0 if res is None else res.dtype.itemsize)))
        per_row += 4 * tn if tk < k else 0
        rows = (MM_VMEM_BUDGET_BYTES - 2 * tk * tn * b.dtype.itemsize) // per_row
        tm = tm or next((c for c in (2048, 1024, 512, 256, 128) if m % c == 0 and c <= rows), m)
    if a_halves:
        tk = min(tk, k // 2) if (k // 2) % min(tk, k // 2) == 0 else _pick(k // 2, (1408, 1024, 512, 256, 128))
    if b_halves:
        tn = tn if (n // 2) % tn == 0 else _pick(n // 2, (1408, 1024, 512, 256, 128))
    nk = k // tk
    assert m % tm == 0 and n % tn == 0 and k % tk == 0
    a_spec = pl.BlockSpec((tk, tm), lambda i, j, kk: (kk, i)) if ta else pl.BlockSpec((tm, tk), lambda i, j, kk: (i, kk))
    b_spec = pl.BlockSpec((tn, tk), lambda i, j, kk: (j, kk)) if tb else pl.BlockSpec((tk, tn), lambda i, j, kk: (kk, j))
    if a_halves:
        per = (k // 2) // tk
        a_spec = pl.BlockSpec((None, tm, tk), lambda i, j, kk: (kk // per, i, kk % per))
    if b_halves:
        per_n = (n // 2) // tn
        b_spec = pl.BlockSpec((None, tk, tn), lambda i, j, kk: (j // per_n, kk, j % per_n))
    o_spec = pl.BlockSpec((tm, tn), lambda i, j, kk: (i, j))
    ca, cb = (0 if ta else 1), (1 if tb else 0)

    n_in = 2 + (res is not None) + (after is not None)

    def body(*refs):
        a_ref, b_ref = refs[:2]
        res_ref = refs[2] if res is not None else None
        o_ref = refs[n_in]
        scratch = refs[n_in + 1:]

        def finish(acc):
            val = acc if alpha == 1.0 else alpha * acc
            if res_ref is not None:
                val = res_ref[...].astype(F32) + val
            o_ref[...] = val.astype(o_ref.dtype)

        part = _dot(a_ref[...].astype(BF16), b_ref[...].astype(BF16), ca, cb)
        if nk == 1:
            finish(part)
        else:
            acc_ref = scratch[0]
            kk = pl.program_id(2)

            @pl.when(kk == 0)
            def _():
                acc_ref[...] = part

            @pl.when(kk > 0)
            def _():
                acc_ref[...] += part

            @pl.when(kk == nk - 1)
            def _():
                finish(acc_ref[...])

    ins = [a, b] + ([] if res is None else [res]) + ([] if after is None else [after])
    in_specs = [a_spec, b_spec] + ([] if res is None else [o_spec]) + ([] if after is None else [ANY])
    return _pcall(
        body, name=name, grid=(m // tm, n // tn, nk), in_specs=in_specs, out_specs=o_spec,
        out_shape=jax.ShapeDtypeStruct((m, n), out_dtype),
        scratch_shapes=[pltpu.VMEM((tm, tn), F32)] if nk > 1 else [],
        compiler_params=_params("parallel", "parallel", "arbitrary"),
    )(*ins)


def _rowcall(fn, rows, consts, outs, accs=(), *, tm, name, after=None):
    s = rows[0][0].shape[0]
    assert s % tm == 0
    n_read, n_out = len(rows) + len(consts), len(outs)
    n_in = n_read + (after is not None)

    def body(*refs):
        vals = fn(*[r[...] for r in refs[:n_read]])
        vals = vals if isinstance(vals, (tuple, list)) else (vals,)
        for o_ref, v in zip(refs[n_in:n_in + n_out], vals[:n_out]):
            o_ref[...] = v.astype(o_ref.dtype)
        if accs:
            first = pl.program_id(0) == 0
            for a_ref, v in zip(refs[n_in + n_out:], vals[n_out:]):
                tot = jnp.sum(v.astype(F32), axis=0, keepdims=True)

                @pl.when(first)
                def _(a_ref=a_ref, tot=tot):
                    a_ref[...] = tot

                @pl.when(jnp.logical_not(first))
                def _(a_ref=a_ref, tot=tot):
                    a_ref[...] += tot

    in_specs = [pl.BlockSpec((tm, w), lambda i, cb=cb: (i, cb)) for (_, cb, w) in rows]
    in_specs += [pl.BlockSpec(c.shape, lambda i: (0, 0)) for c in consts]
    in_specs += [] if after is None else [ANY]
    out_specs = [pl.BlockSpec((tm, w), lambda i: (i, 0)) for (w, _) in outs]
    out_specs += [pl.BlockSpec((1, w), lambda i: (0, 0)) for w in accs]
    out_shape = [jax.ShapeDtypeStruct((s, w), dt) for (w, dt) in outs]
    out_shape += [jax.ShapeDtypeStruct((1, w), F32) for w in accs]
    return _pcall(
        body, name=name, grid=(s // tm,), in_specs=in_specs, out_specs=out_specs, out_shape=out_shape,
        compiler_params=_params("arbitrary" if accs else "parallel"),
    )(*[r[0] for r in rows], *consts, *([] if after is None else [after]))


def _whole(a):
    return (a, 0, a.shape[1])


def _xhat(x):
    x = x.astype(F32)
    r = lax.rsqrt(jnp.mean(x * x, axis=-1, keepdims=True) + RMS_EPS)
    return x * r, r


def _rms_bwd(dy, x, g):
    xh, r = _xhat(x)
    dxh = dy.astype(F32) * g
    dx = r * (dxh - xh * jnp.mean(dxh * xh, axis=-1, keepdims=True))
    return dx, dy.astype(F32) * xh


def _sigmoid(x):
    return 1.0 / (1.0 + jnp.exp(-x))


def _rms_fwd(x, g, name, tm, after=None):
    d = x.shape[1]
    return _rowcall(lambda xb, gb: _xhat(xb)[0] * gb, [_whole(x)], [g], [(d, BF16)], tm=tm, name=name, after=after)[0]


def _silu_parts(gate):
    sg = _sigmoid(gate)
    return sg, gate * sg


def _ffn_up(n, w_gu, name):
    s, d = n.shape
    f = w_gu.shape[1] // 2
    tn = _pick(f, (1408, 1024, 512, 256, 128))
    tm = _pick(s, (1024, 512, 256, 128))
    nb = f // tn

    def body(n_ref, wg_ref, wu_ref, gu_ref, act_ref):
        nv = n_ref[...]
        gate = _dot(nv, wg_ref[...], 1, 0)
        up = _dot(nv, wu_ref[...], 1, 0)
        gu_ref[0] = gate.astype(gu_ref.dtype)
        gu_ref[1] = up.astype(gu_ref.dtype)
        act_ref[...] = (_silu_parts(gate)[1] * up).astype(act_ref.dtype)

    return _pcall(
        body, name=name, grid=(s // tm, nb),
        in_specs=[pl.BlockSpec((tm, d), lambda i, j: (i, 0)), pl.BlockSpec((d, tn), lambda i, j: (0, j)),
                  pl.BlockSpec((d, tn), lambda i, j: (0, nb + j))],
        out_specs=[pl.BlockSpec((2, tm, tn), lambda i, j: (0, i, j)), pl.BlockSpec((tm, tn), lambda i, j: (i, j))],
        out_shape=[jax.ShapeDtypeStruct((2, s, f), BF16), jax.ShapeDtypeStruct((s, f), BF16)],
        compiler_params=_params("parallel", "parallel"),
    )(n, w_gu, w_gu)


def _ffn_dgu(dhb, w_down, gu, name, after=None):
    s, d = dhb.shape
    f = w_down.shape[0]
    tn = _pick(f, (1408, 1024, 512, 256, 128))
    tm = _pick(s, (1024, 512, 256, 128))

    def body(dh_ref, w_ref, gu_ref, *rest):
        o_ref = rest[-1]
        dact = _dot(dh_ref[...], w_ref[...], 1, 1)
        gate, up = gu_ref[0].astype(F32), gu_ref[1].astype(F32)
        sg, silu = _silu_parts(gate)
        o_ref[0] = (dact * up * (sg + silu * (1.0 - sg))).astype(o_ref.dtype)
        o_ref[1] = (dact * silu).astype(o_ref.dtype)

    blk = pl.BlockSpec((2, tm, tn), lambda i, j: (0, i, j))
    return _pcall(
        body, name=name, grid=(s // tm, f // tn),
        in_specs=[pl.BlockSpec((tm, d), lambda i, j: (i, 0)), pl.BlockSpec((tn, d), lambda i, j: (j, 0)), blk]
        + ([] if after is None else [ANY]),
        out_specs=blk, out_shape=jax.ShapeDtypeStruct((2, s, f), BF16), compiler_params=_params("parallel", "parallel"),
    )(dhb, w_down, gu, *([] if after is None else [after]))


def _dgrad_norm(dy, wmat, dh, x, g, name, *, dy_halves=False, copy_scale=None, after=None):
    s, d = dh.shape
    k = wmat.shape[1]
    tk = k if k <= MM_WHOLE_K else _pick(k, (MM_WHOLE_K, 2048, 1024, 512, 256, 128))
    if dy_halves and (k // 2) % tk:
        tk = _pick(k // 2, (1408, 1024, 512, 256, 128))
    tm = _pick(s, (512, 256, 128))
    nk, per = k // tk, (k // 2) // tk if dy_halves else 0
    n_in = 5 + (after is not None)
    n_out = 2 + (copy_scale is not None)

    def body(*refs):
        dy_ref, w_ref, dh_ref, x_ref, g_ref = refs[:5]
        outs, scratch = refs[n_in:n_in + n_out], refs[n_in + n_out:]
        i, kk = pl.program_id(0), pl.program_id(1)
        part = _dot(dy_ref[...], w_ref[...], 1, 1)

        def finish(dn):
            dx, dg = _rms_bwd(dn, x_ref[...], g_ref[...])
            tot = dh_ref[...] + dx
            outs[0][...] = tot
            if copy_scale is not None:
                outs[1][...] = (copy_scale * tot).astype(outs[1].dtype)
            dg = jnp.sum(dg, axis=0, keepdims=True)

            @pl.when(i == 0)
            def _():
                outs[-1][...] = dg

            @pl.when(i > 0)
            def _():
                outs[-1][...] += dg

        if nk == 1:
            finish(part)
        else:
            acc_ref = scratch[0]

            @pl.when(kk == 0)
            def _():
                acc_ref[...] = part

            @pl.when(kk > 0)
            def _():
                acc_ref[...] += part

            @pl.when(kk == nk - 1)
            def _():
                finish(acc_ref[...])

    row = pl.BlockSpec((tm, d), lambda i, kk: (i, 0))
    dy_spec = pl.BlockSpec((None, tm, tk), lambda i, kk: (kk // per, i, kk % per)) if dy_halves else pl.BlockSpec((tm, tk), lambda i, kk: (i, kk))
    in_specs = [dy_spec, pl.BlockSpec((d, tk), lambda i, kk: (0, kk)), row, row, pl.BlockSpec((1, d), lambda i, kk: (0, 0))]
    out_specs = [row] * (n_out - 1) + [pl.BlockSpec((1, d), lambda i, kk: (0, 0))]
    out_shape = [jax.ShapeDtypeStruct((s, d), F32)] + ([] if copy_scale is None else [jax.ShapeDtypeStruct((s, d), BF16)])
    return _pcall(
        body, name=name, grid=(s // tm, nk), in_specs=in_specs + ([] if after is None else [ANY]), out_specs=out_specs,
        out_shape=out_shape + [jax.ShapeDtypeStruct((1, d), F32)], scratch_shapes=[pltpu.VMEM((tm, d), F32)] if nk > 1 else [],
        compiler_params=_params("arbitrary", "arbitrary"),
    )(dy, wmat, dh, x, g, *([] if after is None else [after]))


def _shift_down(p, k):
    if k == 0:
        return p
    rows = lax.broadcasted_iota(jnp.int32, p.shape, 0)
    return jnp.where(rows >= k, pltpu.roll(p, k, 0), 0.0)


def _shift_up(p, k):
    if k == 0:
        return p
    s = p.shape[0]
    rows = lax.broadcasted_iota(jnp.int32, p.shape, 0)
    return jnp.where(rows < s - k, pltpu.roll(p, s - k, 0), 0.0)


def _conv_fwd(proj, conv_w, d, tc, name):
    s = proj.shape[0]
    nb = d // tc

    def body(cb_ref, cc_ref, cx_ref, w_ref, y_ref):
        p = cc_ref[...].astype(F32) * cx_ref[...].astype(F32)
        w = w_ref[...]
        acc = p * w[CONV_K - 1:CONV_K, :]
        for k in range(1, CONV_K):
            acc = acc + _shift_down(p, k) * w[CONV_K - 1 - k:CONV_K - k, :]
        y_ref[...] = (cb_ref[...].astype(F32) * acc).astype(y_ref.dtype)

    col = lambda off: pl.BlockSpec((s, tc), lambda j: (0, off * nb + j))
    return _pcall(
        body, name=name, grid=(nb,), in_specs=[col(0), col(1), col(2), pl.BlockSpec((CONV_K, tc), lambda j: (0, j))],
        out_specs=pl.BlockSpec((s, tc), lambda j: (0, j)), out_shape=jax.ShapeDtypeStruct((s, d), BF16),
        compiler_params=_params("parallel"),
    )(proj, proj, proj, conv_w)


def _conv_bwd(dy, proj, conv_w, d, tc, name):
    s = proj.shape[0]
    nb = d // tc

    def body(dy_ref, cb_ref, cc_ref, cx_ref, w_ref, dcb_ref, dcc_ref, dcx_ref, dw_ref):
        cc, cx = cc_ref[...].astype(F32), cx_ref[...].astype(F32)
        p = cc * cx
        w = w_ref[...]
        dyv = dy_ref[...].astype(F32)
        shifted = [_shift_down(p, CONV_K - 1 - k) for k in range(CONV_K)]
        conv = shifted[0] * w[0:1, :]
        for k in range(1, CONV_K):
            conv = conv + shifted[k] * w[k:k + 1, :]
        dcb_ref[...] = (dyv * conv).astype(dcb_ref.dtype)
        ds = dyv * cb_ref[...].astype(F32)
        dp = ds * w[CONV_K - 1:CONV_K, :]
        for k in range(1, CONV_K):
            dp = dp + _shift_up(ds, k) * w[CONV_K - 1 - k:CONV_K - k, :]
        dcc_ref[...] = (dp * cx).astype(dcc_ref.dtype)
        dcx_ref[...] = (dp * cc).astype(dcx_ref.dtype)
        for k in range(CONV_K):
            dw_ref[k:k + 1, :] = jnp.sum(ds * shifted[k], axis=0, keepdims=True)

    col = lambda off: pl.BlockSpec((s, tc), lambda j: (0, off * nb + j))
    blk = pl.BlockSpec((s, tc), lambda j: (0, j))
    wblk = pl.BlockSpec((CONV_K, tc), lambda j: (0, j))
    act = jax.ShapeDtypeStruct((s, d), BF16)
    return _pcall(
        body, name=name, grid=(nb,), in_specs=[blk, col(0), col(1), col(2), wblk],
        out_specs=[blk, blk, blk, wblk], out_shape=[act, act, act, jax.ShapeDtypeStruct((CONV_K, d), F32)],
        compiler_params=_params("parallel"),
    )(dy, proj, proj, proj, conv_w)


def _sb_tile(q, kj, scale, carry, tri, mask):
    z = _dot(q, kj, 1, 1) * scale
    lsz = jnp.minimum(z, 0.0) - jnp.log(1.0 + jnp.exp(-jnp.abs(z)))
    l1m = lsz - z
    if mask is not None:
        l1m = jnp.where(mask, l1m, 0.0)
    l1b = l1m.astype(BF16)
    a = jnp.exp(lsz + (carry + _dot(l1b, tri, 1, 0)))
    if mask is not None:
        a = jnp.where(mask, a, 0.0)
    return lsz, l1b, a.astype(BF16)


def _add_rows(x, upd, r0):
    return x + upd if r0 == 0 else jnp.concatenate([x[:r0], x[r0:] + upd], axis=0)


def _sb_masks(tq, tk):
    row = lax.broadcasted_iota(jnp.int32, (tq, tk), 0)
    col = lax.broadcasted_iota(jnp.int32, (tq, tk), 1)
    masks = [col + dj * tk < row for dj in range(tq // tk)]
    r2 = lax.broadcasted_iota(jnp.int32, (tk, tk), 0)
    c2 = lax.broadcasted_iota(jnp.int32, (tk, tk), 1)
    return masks, (r2 > c2).astype(BF16), (r2 < c2).astype(BF16)


def _sb_fwd(proj, heads, col0, tq, tk, name):
    s = proj.shape[0]
    dh = SB_HEAD_DIM
    nq, nd, nkt = s // tq, tq // tk, s // tk
    scale = dh ** -0.5

    def body(q_ref, k_ref, v_ref, o_ref, a_ref, b_ref):
        i = pl.program_id(1)
        q = q_ref[...]
        masks, tri_right, _ = _sb_masks(tq, tk)

        def tile(j, carry, acc, mask, r0=0):
            start = pl.multiple_of(j * tk, tk)
            kj = k_ref[pl.ds(start, tk), :]
            vj = v_ref[pl.ds(start, tk), :]
            lsz, l1b, ab = _sb_tile(q[r0:], kj, scale, carry[r0:], tri_right, None if mask is None else mask[r0:])
            a_ref[j, r0:, :] = ab
            b_ref[j, r0:, :] = jnp.exp(lsz).astype(b_ref.dtype)
            if r0:
                a_ref[j, :r0, :] = jnp.zeros((r0, tk), a_ref.dtype)
                b_ref[j, :r0, :] = jnp.zeros((r0, tk), b_ref.dtype)
            return (_add_rows(carry, jnp.sum(l1b.astype(F32), axis=1, keepdims=True), r0),
                    _add_rows(acc, _dot(ab, vj, 1, 0), r0))

        state = (jnp.zeros((tq, 1), F32), jnp.zeros((tq, dh), F32))
        for dj in reversed(range(nd)):
            state = tile(i * nd + dj, *state, masks[dj], dj * tk)
        def left_block(t, st):
            for dj in reversed(range(nd)):
                st = tile((i - 1 - t) * nd + dj, st[0], st[1], None)
            return st

        state = lax.fori_loop(0, i, left_block, state)
        o_ref[...] = state[1]

    qspec = pl.BlockSpec((tq, dh), lambda h, i: (i, col0[0] + h))
    kspec = pl.BlockSpec((s, dh), lambda h, i: (0, col0[1] + h))
    vspec = pl.BlockSpec((s, dh), lambda h, i: (0, col0[2] + h))
    saved = pl.BlockSpec((None, nkt, tq, tk), lambda h, i: (h, 0, i, 0))
    saved_shape = jax.ShapeDtypeStruct((heads, nkt, s, tk), BF16)
    return _pcall(
        body, name=name, grid=(heads, nq), in_specs=[qspec, kspec, vspec],
        out_specs=[pl.BlockSpec((tq, dh), lambda h, i: (i, h)), saved, saved],
        out_shape=[jax.ShapeDtypeStruct((s, heads * dh), F32), saved_shape, saved_shape],
        compiler_params=_params("parallel", "parallel"),
    )(proj, proj, proj)


def _sb_bwd(proj, o, a_all, beta_all, do, heads, col0, tq, tk, name):
    s = proj.shape[0]
    dh = SB_HEAD_DIM
    nq, nd, nkt = s // tq, tq // tk, s // tk
    scale = dh ** -0.5

    def body(q_ref, k_ref, v_ref, o_ref, a_ref, b_ref, do_ref, dq_ref, dk_ref, dv_ref, dk_acc, dv_acc):
        i = pl.program_id(1)

        @pl.when(i == 0)
        def _():
            dk_acc[...] = jnp.zeros_like(dk_acc)
            dv_acc[...] = jnp.zeros_like(dv_acc)

        q = q_ref[...]
        dob = do_ref[...].astype(BF16)
        delta = jnp.sum(dob.astype(F32) * o_ref[...], axis=1, keepdims=True)
        masks, _, tri_left = _sb_masks(tq, tk)

        def tile(j, carry_g, dq, mask):
            start = pl.multiple_of(j * tk, tk)
            kj = k_ref[pl.ds(start, tk), :]
            vj = v_ref[pl.ds(start, tk), :]
            ab = a_ref[j]
            g = _dot(dob, vj, 1, 1) * ab.astype(F32)
            carry_g = carry_g + jnp.sum(g, axis=1, keepdims=True)
            left = (delta - carry_g) + _dot(g.astype(BF16), tri_left, 1, 0)
            dz = g - b_ref[j].astype(F32) * (g + left)
            if mask is not None:
                dz = jnp.where(mask, dz, 0.0)
            dzb = dz.astype(BF16)
            dk_acc[pl.ds(start, tk), :] += _dot(dzb, q, 0, 0)
            dv_acc[pl.ds(start, tk), :] += _dot(ab, dob, 0, 0)
            return carry_g, dq + _dot(dzb, kj, 1, 0)

        state = (jnp.zeros((tq, 1), F32), jnp.zeros((tq, dh), F32))
        for dj in reversed(range(nd)):
            state = tile(i * nd + dj, *state, masks[dj])
        def left_block(t, st):
            for dj in reversed(range(nd)):
                st = tile((i - 1 - t) * nd + dj, st[0], st[1], None)
            return st

        state = lax.fori_loop(0, i, left_block, state)
        dq_ref[...] = (state[1] * scale).astype(dq_ref.dtype)

        @pl.when(i == nq - 1)
        def _():
            dk_ref[...] = (dk_acc[...] * scale).astype(dk_ref.dtype)
            dv_ref[...] = dv_acc[...].astype(dv_ref.dtype)

    qspec = pl.BlockSpec((tq, dh), lambda h, i: (i, col0[0] + h))
    kspec = pl.BlockSpec((s, dh), lambda h, i: (0, col0[1] + h))
    vspec = pl.BlockSpec((s, dh), lambda h, i: (0, col0[2] + h))
    blk = pl.BlockSpec((tq, dh), lambda h, i: (i, h))
    full = pl.BlockSpec((s, dh), lambda h, i: (0, h))
    saved = pl.BlockSpec((None, nkt, tq, tk), lambda h, i: (h, 0, i, 0))
    act = jax.ShapeDtypeStruct((s, heads * dh), BF16)
    return _pcall(
        body, name=name, grid=(heads, nq), in_specs=[qspec, kspec, vspec, blk, saved, saved, blk],
        out_specs=[blk, full, full], out_shape=[act, act, act],
        scratch_shapes=[pltpu.VMEM((s, dh), F32), pltpu.VMEM((s, dh), F32)],
        compiler_params=_params("parallel", "arbitrary"),
    )(proj, proj, proj, o, a_all, beta_all, do)


def _xattn_probs(q, k, scale):
    sc = _dot(q, k, 1, 1) * scale
    e = jnp.exp(sc - jnp.max(sc, axis=1, keepdims=True))
    return e / jnp.sum(e, axis=1, keepdims=True)


def _xattn_fwd(qc, kv, tq, name):
    s, d = qc.shape
    m = kv.shape[0]
    dh = d // X_HEADS
    scale = dh ** -0.5

    def body(q_ref, k_ref, v_ref, o_ref):
        p = _xattn_probs(q_ref[...], k_ref[...], scale)
        o_ref[...] = _dot(p.astype(BF16), v_ref[...], 1, 0).astype(o_ref.dtype)

    blk = pl.BlockSpec((tq, dh), lambda h, i: (i, h))
    return _pcall(
        body, name=name, grid=(X_HEADS, s // tq),
        in_specs=[blk, pl.BlockSpec((m, dh), lambda h, i: (0, h)), pl.BlockSpec((m, dh), lambda h, i: (0, X_HEADS + h))],
        out_specs=blk, out_shape=jax.ShapeDtypeStruct((s, d), BF16), compiler_params=_params("parallel", "parallel"),
    )(qc, kv, kv)


def _xattn_bwd(qc, kv, do, tq, name):
    s, d = qc.shape
    m = kv.shape[0]
    dh = d // X_HEADS
    scale = dh ** -0.5
    nq = s // tq

    def body(q_ref, k_ref, v_ref, do_ref, dq_ref, dk_ref, dv_ref, dk_acc, dv_acc):
        i = pl.program_id(1)
        q, k, v = q_ref[...], k_ref[...], v_ref[...]
        dob = do_ref[...].astype(BF16)
        p = _xattn_probs(q, k, scale)
        pb = p.astype(BF16)
        dp = _dot(dob, v, 1, 1)
        ds = pb.astype(F32) * (dp - jnp.sum(dp * pb.astype(F32), axis=1, keepdims=True))
        dsb = (ds * scale).astype(BF16)
        dq_ref[...] = _dot(dsb, k, 1, 0).astype(dq_ref.dtype)
        dk_part = _dot(dsb, q, 0, 0)
        dv_part = _dot(pb, dob, 0, 0)

        @pl.when(i == 0)
        def _():
            dk_acc[...] = dk_part
            dv_acc[...] = dv_part

        @pl.when(i > 0)
        def _():
            dk_acc[...] += dk_part
            dv_acc[...] += dv_part

        @pl.when(i == nq - 1)
        def _():
            dk_ref[...] = dk_acc[...].astype(dk_ref.dtype)
            dv_ref[...] = dv_acc[...].astype(dv_ref.dtype)

    blk = pl.BlockSpec((tq, dh), lambda h, i: (i, h))
    kblk = pl.BlockSpec((m, dh), lambda h, i: (0, h))
    return _pcall(
        body, name=name, grid=(X_HEADS, nq),
        in_specs=[blk, kblk, pl.BlockSpec((m, dh), lambda h, i: (0, X_HEADS + h)), blk],
        out_specs=[blk, kblk, kblk],
        out_shape=[jax.ShapeDtypeStruct((s, d), BF16), jax.ShapeDtypeStruct((m, d), BF16), jax.ShapeDtypeStruct((m, d), BF16)],
        scratch_shapes=[pltpu.VMEM((m, dh), F32), pltpu.VMEM((m, dh), F32)],
        compiler_params=_params("parallel", "arbitrary"),
    )(qc, kv, kv, do)


def _local_step(x, mem, tgt, w, fetch=None, prefetch=None, emit=None, tick=None, after=None):
    fetch = fetch or (lambda name, after: {})
    prefetch = prefetch or (lambda name, after: None)
    emit = emit or (lambda group, g: None)
    tick = tick or (lambda group, after: None)
    w = dict(w)
    s, d = x.shape
    heads = d // SB_HEAD_DIM
    tm = _pick(s, (1024, 512, 256, 128))
    tq = _pick(s, (1024, 512, 256, 128))
    sb_tq, sb_tk = _pick(s, (512, 256, 128)), _pick(s, (256, 128))
    tc = _pick(d, (256, 128))
    g = {}

    def wt(name, after):
        if name not in w:
            w.update(fetch(name, after))
        return w[name]

    def ffn_fwd(h, gname, wgu, wdown, tag, after=None):
        n = _rms_fwd(h, w[gname], tag + "_norm", tm, after=after)
        gu, act = _ffn_up(n, wt(wgu, n), tag + "_gu")
        prefetch(wdown, gu)
        return n, gu, act, _mm(act, wt(wdown, act), name=tag + "_down", out_dtype=F32, res=h, alpha=0.5)

    def ffn_bwd(dh, dhb, h, saved, gname, wgu, wdown, tag, copy_scale=None, after=None):
        n, gu, act = saved
        g[wdown] = _mm(act, dhb, ta=True, name=tag + "_dwdown", after=after)
        dgu = _ffn_dgu(dhb, w[wdown], gu, tag + "_dgu", after=emit(tag + "_down", g))
        g[wgu] = _mm(n, dgu, ta=True, b_halves=True, name=tag + "_dwgu", after=tick(tag + "_down", dgu))
        *dh_in, g[gname] = _dgrad_norm(dgu, w[wgu], dh, h, w[gname], tag + "_dn", dy_halves=True, copy_scale=copy_scale,
                                       after=emit(tag, g))
        return dh_in, tick(tag, dh_in[0])

    n1, gu1, act1, h1 = ffn_fwd(x, "g_ffn1", "w_ffn1_gu", "w_ffn1_down", "ffn1", after)
    prefetch("w_in", h1)
    u = _rms_fwd(h1, w["g_mix"], "mix_norm", tm)
    proj = _mm(u, wt("w_in", u), name="mix_in")
    prefetch("w_conv_out", proj)
    nd = d // SB_HEAD_DIM
    y_conv = _conv_fwd(proj, w["conv_w"], d, tc, "conv_fwd")
    sb_cols = (3 * nd, 4 * nd, 5 * nd)
    y_sb, sb_a, sb_beta = _sb_fwd(proj, heads, sb_cols, _pick(s, (2 * sb_tq, sb_tq)), sb_tk, "sb_fwd")
    prefetch("w_cq", y_sb)
    a_conv = _mm(y_conv, wt("w_conv_out", y_conv), name="conv_out")
    a_sb = _mm(y_sb, wt("w_attn_out", y_sb), name="attn_out")
    b_conv, b_sb = w["b_gate"][:, :d], w["b_gate"][:, d:]

    def merge(ac, asb, gcp, gsp, bc, bs):
        gc = _sigmoid(gcp.astype(F32) + bc)
        gs = _sigmoid(gsp.astype(F32) + bs)
        return gc * ac.astype(F32) + gs * asb.astype(F32)

    merged = _rowcall(merge, [_whole(a_conv), _whole(a_sb), (proj, 6, d), (proj, 7, d)], [b_conv, b_sb], [(d, BF16)],
                      tm=tm, name="merge")[0]
    prefetch("w_ffn2_gu", merged)
    h2 = _mm(merged, wt("w_o", merged), name="mix_out", out_dtype=F32, res=h1)
    hn = _rms_fwd(h2, w["g_cross"], "cross_norm", tm)
    mn = _rms_fwd(mem, w["g_mem"], "mem_norm", _pick(mem.shape[0], (256, 128)))
    qc = _mm(hn, wt("w_cq", hn), name="cross_q")
    kv = _mm(mn, wt("w_ckv", mn), name="cross_kv")
    oc = _xattn_fwd(qc, kv, tq, "xattn_fwd")
    h3 = _mm(oc, wt("w_co", oc), name="cross_out", out_dtype=F32, res=h2)
    n2, gu2, act2, h4 = ffn_fwd(h3, "g_ffn2", "w_ffn2_gu", "w_ffn2_down", "ffn2")

    def head(hb, tb, gb):
        xh, r = _xhat(hb)
        err = xh * gb - tb
        dy = err * (1.0 / d)
        dxh = dy * gb
        dx = r * (dxh - xh * jnp.mean(dxh * xh, axis=-1, keepdims=True))
        row_loss = 0.5 * jnp.mean(err * err, axis=-1, keepdims=True)
        return dx, 0.5 * dx, dy * xh, jnp.broadcast_to(row_loss, (row_loss.shape[0], LANES))

    dh4, dh4b, g["g_final"], loss_lanes = _rowcall(head, [_whole(h4), _whole(tgt)], [w["g_final"]], [(d, F32), (d, BF16)],
                                                   [d, LANES], tm=tm, name="loss_head")

    (dh3, dh3b), tok = ffn_bwd(dh4, dh4b, h3, (n2, gu2, act2), "g_ffn2", "w_ffn2_gu", "w_ffn2_down", "ffn2", copy_scale=1.0)
    g["w_co"] = _mm(oc, dh3b, ta=True, name="cross_dwco", after=tok)
    doc = _mm(dh3b, w["w_co"], tb=True, name="cross_doc")
    dqc, dk, dv = _xattn_bwd(qc, kv, doc, tq, "xattn_bwd")
    dkv = jnp.concatenate([dk, dv], axis=1)
    g["w_cq"] = _mm(hn, dqc, ta=True, name="cross_dwcq")
    g["w_ckv"] = _mm(mn, dkv, ta=True, name="cross_dwckv")
    dmn = _mm(dkv, w["w_ckv"], tb=True, name="cross_dmn", out_dtype=F32)
    g["g_mem"] = _rowcall(lambda dy, xb: dy * _xhat(xb)[0], [_whole(dmn), _whole(mem)], [], [], [d],
                          tm=_pick(mem.shape[0], (256, 128)), name="mem_dnorm")[0]
    dh2, dh2b, g["g_cross"] = _dgrad_norm(dqc, w["w_cq"], dh3, h2, w["g_cross"], "cross_dhn", copy_scale=1.0, after=emit("cross", g))

    g["w_o"] = _mm(merged, dh2b, ta=True, name="mix_dwo", after=tick("cross", dh2))
    dmerged = _mm(dh2b, w["w_o"], tb=True, name="mix_dmerged")

    def merge_bwd(dm, ac, asb, gcp, gsp, bc, bs):
        dm, ac, asb = dm.astype(F32), ac.astype(F32), asb.astype(F32)
        gc = _sigmoid(gcp.astype(F32) + bc)
        gs = _sigmoid(gsp.astype(F32) + bs)
        dgc = dm * ac * gc * (1.0 - gc)
        dgs = dm * asb * gs * (1.0 - gs)
        return dm * gc, dm * gs, dgc, dgs, dgc, dgs

    da_conv, da_sb, dgc, dgs, db_conv, db_sb = _rowcall(
        merge_bwd, [_whole(dmerged), _whole(a_conv), _whole(a_sb), (proj, 6, d), (proj, 7, d)], [b_conv, b_sb],
        [(d, BF16)] * 4, [d, d], tm=tm, name="merge_bwd")
    g["b_gate"] = jnp.concatenate([db_conv, db_sb], axis=1)
    g["w_conv_out"] = _mm(y_conv, da_conv, ta=True, name="conv_dwout")
    g["w_attn_out"] = _mm(y_sb, da_sb, ta=True, name="attn_dwout")
    dy_conv = _mm(da_conv, w["w_conv_out"], tb=True, name="conv_dy")
    dy_sb = _mm(da_sb, w["w_attn_out"], tb=True, name="attn_dy")
    dcb, dcc, dcx, g["conv_w"] = _conv_bwd(dy_conv, proj, w["conv_w"], d, tc, "conv_bwd")
    dq, dk_sb, dv_sb = _sb_bwd(proj, y_sb, sb_a, sb_beta, dy_sb, heads, sb_cols, sb_tq, sb_tk, "sb_bwd")
    dproj = jnp.concatenate([dcb, dcc, dcx, dq, dk_sb, dv_sb, dgc, dgs], axis=1)
    g["w_in"] = _mm(u, dproj, ta=True, name="mix_dwin")
    dh1, dh1b, g["g_mix"] = _dgrad_norm(dproj, w["w_in"], dh2, h1, w["g_mix"], "mix_du", copy_scale=0.5, after=emit("mix", g))
    (dx,), tok = ffn_bwd(dh1, dh1b, x, (n1, gu1, act1), "g_ffn1", "w_ffn1_gu", "w_ffn1_down", "ffn1", after=tick("mix", dh1))
    return loss_lanes, dx, g, tok


MATS = (("w_ffn1_gu", "col"), ("w_ffn1_down", "row"), ("w_in", "col"), ("w_conv_out", "row"), ("w_attn_out", "row"),
        ("w_o", "row"), ("w_cq", "row"), ("w_ckv", "col"), ("w_co", "row"), ("w_ffn2_gu", "col"), ("w_ffn2_down", "row"))
VECS = ("g_ffn1", "g_mix", "g_cross", "g_mem", "g_ffn2", "g_final")
WEIGHTS = ("g_ffn1", "w_ffn1_gu", "w_ffn1_down", "g_mix", "w_in", "b_gate", "conv_w", "w_conv_out", "w_attn_out", "w_o",
           "g_cross", "g_mem", "w_cq", "w_ckv", "w_co", "g_ffn2", "w_ffn2_gu", "w_ffn2_down", "g_final")
CONV_ROWS = 16


def _full_shape(kind, r, c):
    return (r, N_CHIPS * c) if kind == "col" else (N_CHIPS * r, c)


def _piece(ref, kind, r, c, chip, half):
    hr = r // 2
    if kind == "col":
        return ref.at[pl.ds(pl.multiple_of(half * hr, math.gcd(hr, 16)), hr), pl.ds(pl.multiple_of(chip * c, LANES), c)]
    return ref.at[pl.ds(pl.multiple_of(chip * r + half * hr, math.gcd(hr, 16)), hr), :]


def _shard_of(ref, kind, r, c, chip):
    if kind == "col":
        return ref.at[:, pl.ds(pl.multiple_of(chip * c, LANES), c)]
    return ref.at[pl.ds(pl.multiple_of(chip * r, 16), r), :]


def _place():
    x, y, c = lax.axis_index("x"), lax.axis_index("y"), lax.axis_index("c")
    others = [(1 - x, y), (x, 1 - y), (1 - x, 1 - y)]
    return x, y, c, 2 * x + y, others


def _remote(src, dst, send_sem, recv_sem, to):
    return pltpu.make_async_remote_copy(src_ref=src, dst_ref=dst, send_sem=send_sem, recv_sem=recv_sem,
                                        device_id=to, device_id_type=MESH)


HBM = pl.BlockSpec(memory_space=pltpu.HBM)
SEM = pl.BlockSpec(memory_space=pltpu.SEMAPHORE)
EFFECT = pltpu.SideEffectType.DATAFLOW_SIDE_EFFECTING
TOKEN = (8, LANES)


def _split_start(name, plan, n_copies, srcs, lands, after=None):
    ns, nl = len(srcs), len(lands)
    n_in = ns + nl + (after is not None)

    def body(*refs):
        outs = refs[n_in:]
        sends, _ = plan(refs[:ns], refs[ns:ns + nl], outs[0], outs[1])
        for cp in sends:
            cp.start()
        outs[-1][...] = jnp.zeros(TOKEN, F32)

    held = [pltpu.HBM(a.shape, a.dtype) for a in (*srcs, *lands)]
    dma = pltpu.SemaphoreType.DMA((n_copies,))
    ins = [pltpu.with_memory_space_constraint(a, pltpu.HBM) for a in (*srcs, *lands)]
    outs = _pcall(
        body, name=name, in_specs=[HBM] * (ns + nl) + ([] if after is None else [ANY]),
        out_specs=(SEM, SEM, *[HBM] * (ns + nl), pl.BlockSpec(memory_space=pltpu.VMEM)),
        out_shape=(dma, dma, *held, jax.ShapeDtypeStruct(TOKEN, F32)),
        input_output_aliases={i: 2 + i for i in range(ns + nl)},
        compiler_params=pltpu.CompilerParams(has_side_effects=EFFECT),
    )(*ins, *([] if after is None else [after]))
    return outs[0], outs[1], list(outs[2:2 + ns]), list(outs[2 + ns:2 + ns + nl]), outs[-1]


def _split_wait(name, plan, send_sems, recv_sems, srcs, lands, after):
    ns, nl = len(srcs), len(lands)

    def body(*refs):
        sends, recvs = plan(refs[:ns], refs[ns:ns + nl], refs[ns + nl], refs[ns + nl + 1])
        for cp in sends:
            cp.wait_send()
        for cp in recvs:
            cp.wait_recv()

    outs = _pcall(
        body, name=name, in_specs=[HBM] * (ns + nl) + [SEM, SEM, ANY], out_specs=[HBM] * (ns + nl),
        out_shape=[pltpu.HBM(a.shape, a.dtype) for a in (*srcs, *lands)],
        input_output_aliases={i: i for i in range(ns + nl)},
        compiler_params=pltpu.CompilerParams(has_side_effects=EFFECT),
    )(*srcs, *lands, send_sems, recv_sems, after)
    return list(outs[:ns]), list(outs[ns:])


def _gather_plan(dims):
    def plan(shard_refs, full_refs, ss, rs):
        x, y, c, me, others = _place()
        sends, recvs = [], []
        for wi, (kind, r, cw) in enumerate(dims):
            half = shard_refs[wi].at[pl.ds(pl.multiple_of(c * (r // 2), math.gcd(r // 2, 16)), r // 2), :]
            for k, (ox, oy) in enumerate(others):
                sem = 4 * wi + k
                sends.append(_remote(half, _piece(full_refs[wi], kind, r, cw, me, c), ss.at[sem], rs.at[sem], (ox, oy, c)))
                recvs.append(_remote(half, _piece(full_refs[wi], kind, r, cw, 2 * ox + oy, c), ss.at[sem], rs.at[sem], (x, y, c)))
            sem = 4 * wi + 3
            own = _remote(shard_refs[wi], _shard_of(full_refs[wi], kind, r, cw, me), ss.at[sem], rs.at[sem], (x, y, 1 - c))
            sends.append(own)
            recvs.append(own)
        return sends, recvs

    return plan


def _forward_plan(dims):
    def plan(_, full_refs, ss, rs):
        x, y, c, _, others = _place()
        sends, recvs = [], []
        for wi, (kind, r, cw) in enumerate(dims):
            for k, (ox, oy) in enumerate(others):
                sem = 3 * wi + k
                mine = _piece(full_refs[wi], kind, r, cw, 2 * ox + oy, c)
                theirs = _piece(full_refs[wi], kind, r, cw, 2 * ox + oy, 1 - c)
                sends.append(_remote(mine, mine, ss.at[sem], rs.at[sem], (x, y, 1 - c)))
                recvs.append(_remote(theirs, theirs, ss.at[sem], rs.at[sem], (x, y, 1 - c)))
        return sends, recvs

    return plan


def _rs_cores_plan(dims):
    def plan(g_refs, land_refs, ss, rs):
        x, y, c, _, _ = _place()
        sends, recvs = [], []
        for wi, dm in enumerate(dims):
            for chip in range(N_CHIPS):
                sem = N_CHIPS * wi + chip
                sends.append(_remote(_piece(g_refs[wi], *dm, chip, 1 - c), land_refs[wi].at[chip], ss.at[sem], rs.at[sem], (x, y, 1 - c)))
                recvs.append(_remote(_piece(g_refs[wi], *dm, chip, c), land_refs[wi].at[chip], ss.at[sem], rs.at[sem], (x, y, 1 - c)))
        return sends, recvs

    return plan


def _share_plan(nw):
    def plan(_, buf_refs, ss, rs):
        x, y, c, _, _ = _place()
        sends = [_remote(buf_refs[wi].at[c], buf_refs[wi].at[c], ss.at[wi], rs.at[wi], (x, y, 1 - c)) for wi in range(nw)]
        recvs = [_remote(buf_refs[wi].at[1 - c], buf_refs[wi].at[1 - c], ss.at[wi], rs.at[wi], (x, y, 1 - c)) for wi in range(nw)]
        return sends, recvs

    return plan


def _small_plan():
    def plan(_, buf_refs, ss, rs):
        x, y, c = lax.axis_index("x"), lax.axis_index("y"), lax.axis_index("c")
        buf = buf_refs[0]
        sends, recvs = [], []
        for rel in range(1, N_DEV):
            peer = (x ^ (rel >> 2 & 1), y ^ (rel >> 1 & 1), c ^ (rel & 1))
            sends.append(_remote(buf.at[0], buf.at[rel], ss.at[rel - 1], rs.at[rel - 1], peer))
            recvs.append(_remote(buf.at[0], buf.at[rel], ss.at[rel - 1], rs.at[rel - 1], peer))
        return sends, recvs

    return plan


def _sum_small(buf, me, name):
    _, rows, n = buf.shape

    def body(me_ref, b_ref, o_ref):
        tot = b_ref[me_ref[0]]
        for dev in range(1, N_DEV):
            tot = tot + b_ref[dev ^ me_ref[0]]
        o_ref[...] = tot

    return _pcall(
        body, name=name, out_shape=jax.ShapeDtypeStruct((rows, n), F32),
        grid_spec=pltpu.PrefetchScalarGridSpec(
            num_scalar_prefetch=1, grid=(1,), in_specs=[pl.BlockSpec((N_DEV, rows, n), lambda i, m: (0, 0, 0))],
            out_specs=pl.BlockSpec((rows, n), lambda i, m: (0, 0))),
    )(me, buf)


def _rs_chips_plan(nw):
    def plan(p_refs, land_refs, ss, rs):
        x, y, c, me, others = _place()
        sends, recvs = [], []
        for wi in range(nw):
            for k, (ox, oy) in enumerate(others):
                sem = 3 * wi + k
                sends.append(_remote(p_refs[wi].at[2 * ox + oy], land_refs[wi].at[k], ss.at[sem], rs.at[sem], (ox, oy, c)))
                recvs.append(_remote(p_refs[wi].at[me], land_refs[wi].at[k], ss.at[sem], rs.at[sem], (x, y, c)))
        return sends, recvs

    return plan


SUM_BLOCK_BYTES = 4 << 20


def _rows_per_block(n, c, limit_bytes=2 << 20):
    best = None
    for tm in range(16, n + 1, 16):
        if n % tm == 0 and tm * c * 4 <= limit_bytes:
            best = tm
    return best or n


def _sum_cores(grad, got, kind, place, name):
    _, hr, cw = got.shape
    tm = _rows_per_block(hr, cw, SUM_BLOCK_BYTES)
    nb = hr // tm

    def body(place_ref, g_ref, t_ref, o_ref):
        o_ref[...] = (g_ref[...].astype(F32) + t_ref[...].astype(F32)).astype(o_ref.dtype)

    if kind == "col":
        g_spec = pl.BlockSpec((tm, cw), lambda j, i, pr: (pr[0] * nb + i, j))
    else:
        g_spec = pl.BlockSpec((tm, cw), lambda j, i, pr: ((2 * j + pr[0]) * nb + i, 0))
    blk = pl.BlockSpec((None, tm, cw), lambda j, i, pr: (j, i, 0))
    return _pcall(
        body, name=name, out_shape=jax.ShapeDtypeStruct(got.shape, BF16),
        grid_spec=pltpu.PrefetchScalarGridSpec(num_scalar_prefetch=1, grid=(N_CHIPS, nb), in_specs=[g_spec, blk], out_specs=blk),
        compiler_params=_params("parallel", "parallel"),
    )(place, grad, got)


def _sum_chips(parts, got, place, name):
    _, n, cw = got.shape
    tm = _rows_per_block(n, cw, SUM_BLOCK_BYTES)

    def body(place_ref, p_ref, g_ref, o_ref):
        tot = p_ref[...].astype(F32)
        for k in range(3):
            tot = tot + g_ref[k].astype(F32)
        o_ref[...] = tot

    return _pcall(
        body, name=name, out_shape=jax.ShapeDtypeStruct((2, n, cw), F32),
        grid_spec=pltpu.PrefetchScalarGridSpec(
            num_scalar_prefetch=1, grid=(n // tm,),
            in_specs=[pl.BlockSpec((None, tm, cw), lambda i, pr: (pr[1], i, 0)), pl.BlockSpec((3, tm, cw), lambda i, pr: (0, i, 0))],
            out_specs=pl.BlockSpec((None, tm, cw), lambda i, pr: (pr[0], i, 0))),
        compiler_params=_params("parallel"),
    )(place, parts, got)


def _adamw(g, w, m, v, name):
    n, c = g.shape
    c1 = 1.0 - ADAM_B1 ** ADAM_STEP
    c2 = 1.0 - ADAM_B2 ** ADAM_STEP

    def fn(gb, wb, mb, vb):
        m_new = ADAM_B1 * mb + (1.0 - ADAM_B1) * gb
        v_new = ADAM_B2 * vb + (1.0 - ADAM_B2) * (gb * gb)
        delta = -ADAM_LR * ((m_new / c1) / (jnp.sqrt(v_new / c2) + ADAM_EPS) + ADAM_WD * wb)
        return gb, delta, m_new, v_new

    tm = _rows_per_block(n, c) if n % 16 == 0 else n
    return _rowcall(fn, [_whole(g), _whole(w), _whole(m), _whole(v)], [], [(c, F32)] * 4, tm=tm, name=name)


PACK_ROWS = 16


def _pack_rows(parts, width, name, after=None):
    assert sum(p.shape[0] for p in parts) <= PACK_ROWS

    def body(*refs):
        out_ref = refs[-1]
        out_ref[...] = jnp.zeros_like(out_ref)
        at = 0
        for r in refs[:len(parts)]:
            k, n = r.shape
            if n == width:
                out_ref[at:at + k, :] = r[...]
            else:
                out_ref[at:at + k, :] = jnp.broadcast_to(r[:, :1], (k, width))
            at += k

    vm = pl.BlockSpec(memory_space=pltpu.VMEM)
    return _pcall(body, name=name, in_specs=[vm] * len(parts) + ([] if after is None else [ANY]), out_specs=vm,
                  out_shape=jax.ShapeDtypeStruct((PACK_ROWS, width), F32))(*parts, *([] if after is None else [after]))


def _cast_shard(wm, name, after):
    n, c = wm.shape
    return _rowcall(lambda v: v, [_whole(wm)], [], [(c, BF16)], tm=_rows_per_block(n, c), name=name, after=after)[0]


GATHER_GROUPS = (
    ("w_ffn1_gu", "conv_w"), ("w_ffn1_down",), ("w_in",), ("w_conv_out", "w_attn_out", "w_o"), ("w_cq", "w_ckv", "w_co"),
    ("w_ffn2_gu", "w_ffn2_down"),
)
REDUCE_GROUPS = {
    "ffn2": ("w_ffn2_down", "w_ffn2_gu"),
    "cross": ("w_co", "w_cq", "w_ckv"),
    "mix": ("w_o", "w_conv_out", "w_attn_out", "w_in"),
    "ffn1_down": ("w_ffn1_down",),
    "ffn1": ("w_ffn1_gu",),
}
TAIL_STAGES = (("ffn2", "cross"), ("mix",), ("ffn1_down", "ffn1"))
KIND = dict(MATS)


def _step(x, mem, tgt, wts, m_in, v_in):
    d = x.shape[-1]
    cc = wts["conv_w"].shape[1]
    place = jnp.stack([lax.axis_index("c"), 2 * lax.axis_index("x") + lax.axis_index("y")]).astype(jnp.int32)
    dims = {n: (kind, *wts[n].shape) for n, kind in MATS}
    dims["conv_w"] = ("col", CONV_ROWS, cc)

    w = {n: wts[n].reshape(1, -1) for n in VECS + ("b_gate",)}
    flying, token = {}, None
    for names in GATHER_GROUPS:
        gd = [dims[n] for n in names]
        shards = [jnp.pad(wts[n], ((0, CONV_ROWS - CONV_K), (0, 0))) if n == "conv_w" else _cast_shard(wts[n], "cast_" + n, token)
                  for n in names]
        lands = [lax.empty(_full_shape(*dm), sh.dtype) for dm, sh in zip(gd, shards)]
        plan = _gather_plan(gd)
        ss, rs, srcs, lands, token = _split_start("gather_start_" + names[0], plan, 4 * len(names), shards, lands, token)
        flying.update({n: (names, plan, ss, rs, srcs, lands, gd) for n in names})

    passing = {}

    def prefetch(name, after):
        if name not in passing:
            names, plan, ss, rs, srcs, lands, gd = flying[name]
            _, lands = _split_wait("gather_wait_" + names[0], plan, ss, rs, srcs, lands, after)
            plan = _forward_plan(gd)
            ss, rs, _, lands, _ = _split_start("forward_start_" + names[0], plan, 3 * len(names), [], lands)
            passing.update({n: (names, plan, ss, rs, lands) for n in names})

    def fetch(name, after):
        prefetch(name, after)
        names, plan, ss, rs, lands = passing[name]
        _, lands = _split_wait("forward_wait_" + names[0], plan, ss, rs, [], lands, after)
        return {n: (land[:CONV_K] if n == "conv_w" else land) for n, land in zip(names, lands)}

    swapping, sent = {}, {}

    def emit(tag, g):
        if tag not in REDUCE_GROUPS:
            return None
        names = REDUCE_GROUPS[tag]
        gd = [dims[n] for n in names]
        lands = [lax.empty((N_CHIPS, r // 2, cw), BF16) for (_, r, cw) in gd]
        plan = _rs_cores_plan(gd)
        ss, rs, srcs, lands, tok = _split_start("rs_cores_start_" + tag, plan, N_CHIPS * len(names), [g[n] for n in names], lands)
        swapping[tag] = (plan, ss, rs, srcs, lands)
        return tok

    def tick(tag, after):
        if tag not in REDUCE_GROUPS:
            return None
        names = REDUCE_GROUPS[tag]
        plan, ss, rs, srcs, lands = swapping[tag]
        mine, got = _split_wait("rs_cores_wait_" + tag, plan, ss, rs, srcs, lands, after)
        parts = [_sum_cores(gm, t, KIND[n], place, "sum_cores_" + n) for n, gm, t in zip(names, mine, got)]
        lands = [lax.empty((3, *p.shape[1:]), BF16) for p in parts]
        plan = _rs_chips_plan(len(names))
        ss, rs, srcs, lands, tok = _split_start("rs_chips_start_" + tag, plan, 3 * len(names), parts, lands)
        sent[tag] = (plan, ss, rs, srcs, lands)
        return tok

    loss_lanes, dx, g, last = _local_step(x[0], mem[0], tgt[0], w, fetch, prefetch, emit, tick, token)

    rows = [g[n] for n in VECS] + [g["b_gate"][:, :d], g["b_gate"][:, d:], g["conv_w"], loss_lanes]
    packed = _pack_rows(rows, d, "pack_small", after=last)
    small = jnp.concatenate([packed[None], jnp.zeros((N_DEV - 1, *packed.shape), F32)], axis=0)
    small_plan = _small_plan()
    small_ss, small_rs, _, small, after = _split_start("small_start", small_plan, N_DEV - 1, [], [small])

    grads, out = {}, {}

    def update(n):
        shape = wts[n].shape
        as2d = (lambda a: a.reshape(1, -1)) if len(shape) == 1 else (lambda a: a)
        return [r.reshape(shape) for r in _adamw(grads[n], as2d(wts[n]), as2d(m_in[n]), as2d(v_in[n]), "adamw_" + n)]

    def finish(sharing, after):
        tag, names, plan, ss, rs, halves = sharing
        _, both = _split_wait("share_wait_" + tag, plan, ss, rs, [], halves, after)
        for n, b in zip(names, both):
            grads[n] = b.reshape(-1, b.shape[-1])
            out[n] = update(n)
        return out[names[-1]][1]

    sharing = None
    for stage in TAIL_STAGES:
        names, halves = [], []
        for tag in stage:
            plan, ss, rs, srcs, lands = sent[tag]
            parts, landed = _split_wait("rs_chips_wait_" + tag, plan, ss, rs, srcs, lands, after)
            halves += [_sum_chips(p, t, place, "sum_chips_" + n) for n, p, t in zip(REDUCE_GROUPS[tag], parts, landed)]
            names += REDUCE_GROUPS[tag]
        plan = _share_plan(len(names))
        ss, rs, _, halves, after = _split_start("share_start_" + stage[0], plan, len(names), [], halves)
        if sharing is not None:
            after = finish(sharing, after)
        sharing = (stage[0], names, plan, ss, rs, halves)
    after = finish(sharing, after)

    _, small = _split_wait("small_wait", small_plan, small_ss, small_rs, [], small, after)
    me = (4 * lax.axis_index("x") + 2 * lax.axis_index("y") + lax.axis_index("c")).astype(jnp.int32).reshape(1)
    red = _sum_small(small[0], me, "sum_small")
    grads.update({n: red[i:i + 1] for i, n in enumerate(VECS)})
    nv = len(VECS)
    grads["b_gate"] = jnp.concatenate([red[nv:nv + 1], red[nv + 1:nv + 2]], axis=1)
    chip = 2 * lax.axis_index("x") + lax.axis_index("y")
    grads["conv_w"] = lax.dynamic_slice_in_dim(red[nv + 2:nv + 2 + CONV_K], chip * cc, cc, axis=1)
    loss = red[nv + 2 + CONV_K, 0]
    out.update({n: update(n) for n in WEIGHTS if n not in KIND})
    return (loss, dx[None], *[out[n][0] for n in WEIGHTS], *[out[n][1] for n in WEIGHTS],
            *[out[n][2] for n in WEIGHTS], *[out[n][3] for n in WEIGHTS])


def kernel(x, mem, g_ffn1, w_ffn1_gu, w_ffn1_down, g_mix, w_in, b_gate, conv_w, w_conv_out, w_attn_out, w_o, g_cross, g_mem, w_cq, w_ckv, w_co, g_ffn2, w_ffn2_gu, w_ffn2_down, g_final, loss_target, m_g_ffn1, m_w_ffn1_gu, m_w_ffn1_down, m_g_mix, m_w_in, m_b_gate, m_conv_w, m_w_conv_out, m_w_attn_out, m_w_o, m_g_cross, m_g_mem, m_w_cq, m_w_ckv, m_w_co, m_g_ffn2, m_w_ffn2_gu, m_w_ffn2_down, m_g_final, v_g_ffn1, v_w_ffn1_gu, v_w_ffn1_down, v_g_mix, v_w_in, v_b_gate, v_conv_w, v_w_conv_out, v_w_attn_out, v_w_o, v_g_cross, v_g_mem, v_w_cq, v_w_ckv, v_w_co, v_g_ffn2, v_w_ffn2_gu, v_w_ffn2_down, v_g_final):
    given = dict(locals())
    wts = {n: given[n] for n in WEIGHTS}
    m_in = {n: given["m_" + n] for n in WEIGHTS}
    v_in = {n: given["v_" + n] for n in WEIGHTS}
    return _step(x, mem, loss_target, wts, m_in, v_in)
```

```python
import math

import jax
import jax.numpy as jnp
from jax import lax
from jax.experimental import pallas as pl
from jax.experimental.pallas import tpu as pltpu

F32 = jnp.float32
BF16 = jnp.bfloat16
MESH = pl.DeviceIdType.MESH

V7X_VMEM_LIMIT_BYTES = 48 * 1024 * 1024
MM_VMEM_BUDGET_BYTES = 36 * 1024 * 1024
MM_WHOLE_K = 2816
LANES = 128
SB_HEAD_DIM = 128
X_HEADS = 4
CONV_K = 3
RMS_EPS = 1e-6
N_CHIPS = 4
N_DEV = 8
ADAM_LR, ADAM_B1, ADAM_B2, ADAM_EPS, ADAM_WD, ADAM_STEP = 0.001, 0.9, 0.999, 1e-08, 0.01, 10


ANY = pl.BlockSpec(memory_space=pl.ANY)


def _pcall(body, **kw):
    return pl.pallas_call(body, **kw)


def _params(*sem):
    return pltpu.CompilerParams(dimension_semantics=sem, vmem_limit_bytes=V7X_VMEM_LIMIT_BYTES)


def _pick(dim, cands):
    for c in cands:
        if dim % c == 0:
            return c
    return dim


def _dot(a, b, ca, cb):
    return lax.dot_general(a, b, (((ca,), (cb,)), ((), ())), preferred_element_type=F32)


def _mm(a, b, *, name, ta=False, tb=False, out_dtype=BF16, res=None, alpha=1.0, tm=None, tn=None, tk=None, after=None,
        a_halves=False, b_halves=False):
    assert not (a_halves and ta) and not (b_halves and tb)
    if a_halves:
        m, k = a.shape[1], 2 * a.shape[2]
    else:
        m, k = (a.shape[1], a.shape[0]) if ta else a.shape
    if b_halves:
        n = 2 * b.shape[2]
        assert k == b.shape[1]
    else:
        n = b.shape[0] if tb else b.shape[1]
        assert k == (b.shape[1] if tb else b.shape[0]), (a.shape, b.shape, ta, tb)
    if ta:
        tm = tm or _pick(m, (512, 256, 128))
        tn = tn or _pick(n, (1024, 512, 256, 128))
        tk = tk or (k if k <= MM_WHOLE_K else _pick(k, (1024, 512, 256, 128)))
    else:
        tk = tk or (k if k <= MM_WHOLE_K else _pick(k, (MM_WHOLE_K, 2048, 1024, 512, 256, 128)))
        tn = tn or _pick(n, (512, 1408, 256, 128) if tk == k else (1024, 512, 256, 128))
        per_row = 2 * (tk * a.dtype.itemsize + tn * (jnp.dtype(out_dtype).itemsize + (0 if res is None else res.dtype.itemsize)))
        per_row += 4 * tn if tk < k else 0
        rows = (MM_VMEM_BUDGET_BYTES - 2 * tk * tn * b.dtype.itemsize) // per_row
        tm = tm or next((c for c in (2048, 1024, 512, 256, 128) if m % c == 0 and c <= rows), m)
    if a_halves:
        tk = min(tk, k // 2) if (k // 2) % min(tk, k // 2) == 0 else _pick(k // 2, (1408, 1024, 512, 256, 128))
    if b_halves:
        tn = tn if (n // 2) % tn == 0 else _pick(n // 2, (1408, 1024, 512, 256, 128))
    nk = k // tk
    assert m % tm == 0 and n % tn == 0 and k % tk == 0
    a_spec = pl.BlockSpec((tk, tm), lambda i, j, kk: (kk, i)) if ta else pl.BlockSpec((tm, tk), lambda i, j, kk: (i, kk))
    b_spec = pl.BlockSpec((tn, tk), lambda i, j, kk: (j, kk)) if tb else pl.BlockSpec((tk, tn), lambda i, j, kk: (kk, j))
    if a_halves:
        per = (k // 2) // tk
        a_spec = pl.BlockSpec((None, tm, tk), lambda i, j, kk: (kk // per, i, kk % per))
    if b_halves:
        per_n = (n // 2) // tn
        b_spec = pl.BlockSpec((None, tk, tn), lambda i, j, kk: (j // per_n, kk, j % per_n))
    o_spec = pl.BlockSpec((tm, tn), lambda i, j, kk: (i, j))
    ca, cb = (0 if ta else 1), (1 if tb else 0)

    n_in = 2 + (res is not None) + (after is not None)

    def body(*refs):
        a_ref, b_ref = refs[:2]
        res_ref = refs[2] if res is not None else None
        o_ref = refs[n_in]
        scratch = refs[n_in + 1:]

        def finish(acc):
            val = acc if alpha == 1.0 else alpha * acc
            if res_ref is not None:
                val = res_ref[...].astype(F32) + val
            o_ref[...] = val.astype(o_ref.dtype)

        part = _dot(a_ref[...].astype(BF16), b_ref[...].astype(BF16), ca, cb)
        if nk == 1:
            finish(part)
        else:
            acc_ref = scratch[0]
            kk = pl.program_id(2)

            @pl.when(kk == 0)
            def _():
                acc_ref[...] = part

            @pl.when(kk > 0)
            def _():
                acc_ref[...] += part

            @pl.when(kk == nk - 1)
            def _():
                finish(acc_ref[...])

    ins = [a, b] + ([] if res is None else [res]) + ([] if after is None else [after])
    in_specs = [a_spec, b_spec] + ([] if res is None else [o_spec]) + ([] if after is None else [ANY])
    return _pcall(
        body, name=name, grid=(m // tm, n // tn, nk), in_specs=in_specs, out_specs=o_spec,
        out_shape=jax.ShapeDtypeStruct((m, n), out_dtype),
        scratch_shapes=[pltpu.VMEM((tm, tn), F32)] if nk > 1 else [],
        compiler_params=_params("parallel", "parallel", "arbitrary"),
    )(*ins)


def _rowcall(fn, rows, consts, outs, accs=(), *, tm, name, after=None):
    s = rows[0][0].shape[0]
    assert s % tm == 0
    n_read, n_out = len(rows) + len(consts), len(outs)
    n_in = n_read + (after is not None)

    def body(*refs):
        vals = fn(*[r[...] for r in refs[:n_read]])
        vals = vals if isinstance(vals, (tuple, list)) else (vals,)
        for o_ref, v in zip(refs[n_in:n_in + n_out], vals[:n_out]):
            o_ref[...] = v.astype(o_ref.dtype)
        if accs:
            first = pl.program_id(0) == 0
            for a_ref, v in zip(refs[n_in + n_out:], vals[n_out:]):
                tot = jnp.sum(v.astype(F32), axis=0, keepdims=True)

                @pl.when(first)
                def _(a_ref=a_ref, tot=tot):
                    a_ref[...] = tot

                @pl.when(jnp.logical_not(first))
                def _(a_ref=a_ref, tot=tot):
                    a_ref[...] += tot

    in_specs = [pl.BlockSpec((tm, w), lambda i, cb=cb: (i, cb)) for (_, cb, w) in rows]
    in_specs += [pl.BlockSpec(c.shape, lambda i: (0, 0)) for c in consts]
    in_specs += [] if after is None else [ANY]
    out_specs = [pl.BlockSpec((tm, w), lambda i: (i, 0)) for (w, _) in outs]
    out_specs += [pl.BlockSpec((1, w), lambda i: (0, 0)) for w in accs]
    out_shape = [jax.ShapeDtypeStruct((s, w), dt) for (w, dt) in outs]
    out_shape += [jax.ShapeDtypeStruct((1, w), F32) for w in accs]
    return _pcall(
        body, name=name, grid=(s // tm,), in_specs=in_specs, out_specs=out_specs, out_shape=out_shape,
        compiler_params=_params("arbitrary" if accs else "parallel"),
    )(*[r[0] for r in rows], *consts, *([] if after is None else [after]))


def _whole(a):
    return (a, 0, a.shape[1])


def _xhat(x):
    x = x.astype(F32)
    r = lax.rsqrt(jnp.mean(x * x, axis=-1, keepdims=True) + RMS_EPS)
    return x * r, r


def _rms_bwd(dy, x, g):
    xh, r = _xhat(x)
    dxh = dy.astype(F32) * g
    dx = r * (dxh - xh * jnp.mean(dxh * xh, axis=-1, keepdims=True))
    return dx, dy.astype(F32) * xh


def _sigmoid(x):
    return 1.0 / (1.0 + jnp.exp(-x))


def _rms_fwd(x, g, name, tm, after=None):
    d = x.shape[1]
    return _rowcall(lambda xb, gb: _xhat(xb)[0] * gb, [_whole(x)], [g], [(d, BF16)], tm=tm, name=name, after=after)[0]


def _silu_parts(gate):
    sg = _sigmoid(gate)
    return sg, gate * sg


def _ffn_up(n, w_gu, name):
    s, d = n.shape
    f = w_gu.shape[1] // 2
    tn = _pick(f, (1408, 1024, 512, 256, 128))
    tm = _pick(s, (1024, 512, 256, 128))
    nb = f // tn

    def body(n_ref, wg_ref, wu_ref, gu_ref, act_ref):
        nv = n_ref[...]
        gate = _dot(nv, wg_ref[...], 1, 0)
        up = _dot(nv, wu_ref[...], 1, 0)
        gu_ref[0] = gate.astype(gu_ref.dtype)
        gu_ref[1] = up.astype(gu_ref.dtype)
        act_ref[...] = (_silu_parts(gate)[1] * up).astype(act_ref.dtype)

    return _pcall(
        body, name=name, grid=(s // tm, nb),
        in_specs=[pl.BlockSpec((tm, d), lambda i, j: (i, 0)), pl.BlockSpec((d, tn), lambda i, j: (0, j)),
                  pl.BlockSpec((d, tn), lambda i, j: (0, nb + j))],
        out_specs=[pl.BlockSpec((2, tm, tn), lambda i, j: (0, i, j)), pl.BlockSpec((tm, tn), lambda i, j: (i, j))],
        out_shape=[jax.ShapeDtypeStruct((2, s, f), BF16), jax.ShapeDtypeStruct((s, f), BF16)],
        compiler_params=_params("parallel", "parallel"),
    )(n, w_gu, w_gu)


def _ffn_dgu(dhb, w_down, gu, name, after=None):
    s, d = dhb.shape
    f = w_down.shape[0]
    tn = _pick(f, (1408, 1024, 512, 256, 128))
    tm = _pick(s, (1024, 512, 256, 128))

    def body(dh_ref, w_ref, gu_ref, *rest):
        o_ref = rest[-1]
        dact = _dot(dh_ref[...], w_ref[...], 1, 1)
        gate, up = gu_ref[0].astype(F32), gu_ref[1].astype(F32)
        sg, silu = _silu_parts(gate)
        o_ref[0] = (dact * up * (sg + silu * (1.0 - sg))).astype(o_ref.dtype)
        o_ref[1] = (dact * silu).astype(o_ref.dtype)

    blk = pl.BlockSpec((2, tm, tn), lambda i, j: (0, i, j))
    return _pcall(
        body, name=name, grid=(s // tm, f // tn),
        in_specs=[pl.BlockSpec((tm, d), lambda i, j: (i, 0)), pl.BlockSpec((tn, d), lambda i, j: (j, 0)), blk]
        + ([] if after is None else [ANY]),
        out_specs=blk, out_shape=jax.ShapeDtypeStruct((2, s, f), BF16), compiler_params=_params("parallel", "parallel"),
    )(dhb, w_down, gu, *([] if after is None else [after]))


def _dgrad_norm(dy, wmat, dh, x, g, name, *, dy_halves=False, copy_scale=None, after=None):
    s, d = dh.shape
    k = wmat.shape[1]
    tk = k if k <= MM_WHOLE_K else _pick(k, (MM_WHOLE_K, 2048, 1024, 512, 256, 128))
    if dy_halves and (k // 2) % tk:
        tk = _pick(k // 2, (1408, 1024, 512, 256, 128))
    tm = _pick(s, (512, 256, 128))
    nk, per = k // tk, (k // 2) // tk if dy_halves else 0
    n_in = 5 + (after is not None)
    n_out = 2 + (copy_scale is not None)

    def body(*refs):
        dy_ref, w_ref, dh_ref, x_ref, g_ref = refs[:5]
        outs, scratch = refs[n_in:n_in + n_out], refs[n_in + n_out:]
        i, kk = pl.program_id(0), pl.program_id(1)
        part = _dot(dy_ref[...], w_ref[...], 1, 1)

        def finish(dn):
            dx, dg = _rms_bwd(dn, x_ref[...], g_ref[...])
            tot = dh_ref[...] + dx
            outs[0][...] = tot
            if copy_scale is not None:
                outs[1][...] = (copy_scale * tot).astype(outs[1].dtype)
            dg = jnp.sum(dg, axis=0, keepdims=True)

            @pl.when(i == 0)
            def _():
                outs[-1][...] = dg

            @pl.when(i > 0)
            def _():
                outs[-1][...] += dg

        if nk == 1:
            finish(part)
        else:
            acc_ref = scratch[0]

            @pl.when(kk == 0)
            def _():
                acc_ref[...] = part

            @pl.when(kk > 0)
            def _():
                acc_ref[...] += part

            @pl.when(kk == nk - 1)
            def _():
                finish(acc_ref[...])

    row = pl.BlockSpec((tm, d), lambda i, kk: (i, 0))
    dy_spec = pl.BlockSpec((None, tm, tk), lambda i, kk: (kk // per, i, kk % per)) if dy_halves else pl.BlockSpec((tm, tk), lambda i, kk: (i, kk))
    in_specs = [dy_spec, pl.BlockSpec((d, tk), lambda i, kk: (0, kk)), row, row, pl.BlockSpec((1, d), lambda i, kk: (0, 0))]
    out_specs = [row] * (n_out - 1) + [pl.BlockSpec((1, d), lambda i, kk: (0, 0))]
    out_shape = [jax.ShapeDtypeStruct((s, d), F32)] + ([] if copy_scale is None else [jax.ShapeDtypeStruct((s, d), BF16)])
    return _pcall(
        body, name=name, grid=(s // tm, nk), in_specs=in_specs + ([] if after is None else [ANY]), out_specs=out_specs,
        out_shape=out_shape + [jax.ShapeDtypeStruct((1, d), F32)], scratch_shapes=[pltpu.VMEM((tm, d), F32)] if nk > 1 else [],
        compiler_params=_params("arbitrary", "arbitrary"),
    )(dy, wmat, dh, x, g, *([] if after is None else [after]))


def _shift_down(p, k):
    if k == 0:
        return p
    rows = lax.broadcasted_iota(jnp.int32, p.shape, 0)
    return jnp.where(rows >= k, pltpu.roll(p, k, 0), 0.0)


def _shift_up(p, k):
    if k == 0:
        return p
    s = p.shape[0]
    rows = lax.broadcasted_iota(jnp.int32, p.shape, 0)
    return jnp.where(rows < s - k, pltpu.roll(p, s - k, 0), 0.0)


def _conv_fwd(proj, conv_w, d, tc, name):
    s = proj.shape[0]
    nb = d // tc

    def body(cb_ref, cc_ref, cx_ref, w_ref, y_ref):
        p = cc_ref[...].astype(F32) * cx_ref[...].astype(F32)
        w = w_ref[...]
        acc = p * w[CONV_K - 1:CONV_K, :]
        for k in range(1, CONV_K):
            acc = acc + _shift_down(p, k) * w[CONV_K - 1 - k:CONV_K - k, :]
        y_ref[...] = (cb_ref[...].astype(F32) * acc).astype(y_ref.dtype)

    col = lambda off: pl.BlockSpec((s, tc), lambda j: (0, off * nb + j))
    return _pcall(
        body, name=name, grid=(nb,), in_specs=[col(0), col(1), col(2), pl.BlockSpec((CONV_K, tc), lambda j: (0, j))],
        out_specs=pl.BlockSpec((s, tc), lambda j: (0, j)), out_shape=jax.ShapeDtypeStruct((s, d), BF16),
        compiler_params=_params("parallel"),
    )(proj, proj, proj, conv_w)


def _conv_bwd(dy, proj, conv_w, d, tc, name):
    s = proj.shape[0]
    nb = d // tc

    def body(dy_ref, cb_ref, cc_ref, cx_ref, w_ref, dcb_ref, dcc_ref, dcx_ref, dw_ref):
        cc, cx = cc_ref[...].astype(F32), cx_ref[...].astype(F32)
        p = cc * cx
        w = w_ref[...]
        dyv = dy_ref[...].astype(F32)
        shifted = [_shift_down(p, CONV_K - 1 - k) for k in range(CONV_K)]
        conv = shifted[0] * w[0:1, :]
        for k in range(1, CONV_K):
            conv = conv + shifted[k] * w[k:k + 1, :]
        dcb_ref[...] = (dyv * conv).astype(dcb_ref.dtype)
        ds = dyv * cb_ref[...].astype(F32)
        dp = ds * w[CONV_K - 1:CONV_K, :]
        for k in range(1, CONV_K):
            dp = dp + _shift_up(ds, k) * w[CONV_K - 1 - k:CONV_K - k, :]
        dcc_ref[...] = (dp * cx).astype(dcc_ref.dtype)
        dcx_ref[...] = (dp * cc).astype(dcx_ref.dtype)
        for k in range(CONV_K):
            dw_ref[k:k + 1, :] = jnp.sum(ds * shifted[k], axis=0, keepdims=True)

    col = lambda off: pl.BlockSpec((s, tc), lambda j: (0, off * nb + j))
    blk = pl.BlockSpec((s, tc), lambda j: (0, j))
    wblk = pl.BlockSpec((CONV_K, tc), lambda j: (0, j))
    act = jax.ShapeDtypeStruct((s, d), BF16)
    return _pcall(
        body, name=name, grid=(nb,), in_specs=[blk, col(0), col(1), col(2), wblk],
        out_specs=[blk, blk, blk, wblk], out_shape=[act, act, act, jax.ShapeDtypeStruct((CONV_K, d), F32)],
        compiler_params=_params("parallel"),
    )(dy, proj, proj, proj, conv_w)


def _sb_tile(q, kj, scale, carry, tri, mask):
    z = _dot(q, kj, 1, 1) * scale
    lsz = jnp.minimum(z, 0.0) - jnp.log(1.0 + jnp.exp(-jnp.abs(z)))
    l1m = lsz - z
    if mask is not None:
        l1m = jnp.where(mask, l1m, 0.0)
    l1b = l1m.astype(BF16)
    a = jnp.exp(lsz + (carry + _dot(l1b, tri, 1, 0)))
    if mask is not None:
        a = jnp.where(mask, a, 0.0)
    return lsz, l1b, a.astype(BF16)


def _add_rows(x, upd, r0):
    return x + upd if r0 == 0 else jnp.concatenate([x[:r0], x[r0:] + upd], axis=0)


def _sb_masks(tq, tk):
    row = lax.broadcasted_iota(jnp.int32, (tq, tk), 0)
    col = lax.broadcasted_iota(jnp.int32, (tq, tk), 1)
    masks = [col + dj * tk < row for dj in range(tq // tk)]
    r2 = lax.broadcasted_iota(jnp.int32, (tk, tk), 0)
    c2 = lax.broadcasted_iota(jnp.int32, (tk, tk), 1)
    return masks, (r2 > c2).astype(BF16), (r2 < c2).astype(BF16)


def _sb_fwd(proj, heads, col0, tq, tk, name):
    s = proj.shape[0]
    dh = SB_HEAD_DIM
    nq, nd, nkt = s // tq, tq // tk, s // tk
    scale = dh ** -0.5

    def body(q_ref, k_ref, v_ref, o_ref, a_ref, b_ref):
        i = pl.program_id(1)
        q = q_ref[...]
        masks, tri_right, _ = _sb_masks(tq, tk)

        def tile(j, carry, acc, mask, r0=0):
            start = pl.multiple_of(j * tk, tk)
            kj = k_ref[pl.ds(start, tk), :]
            vj = v_ref[pl.ds(start, tk), :]
            lsz, l1b, ab = _sb_tile(q[r0:], kj, scale, carry[r0:], tri_right, None if mask is None else mask[r0:])
            a_ref[j, r0:, :] = ab
            b_ref[j, r0:, :] = jnp.exp(lsz).astype(b_ref.dtype)
            if r0:
                a_ref[j, :r0, :] = jnp.zeros((r0, tk), a_ref.dtype)
                b_ref[j, :r0, :] = jnp.zeros((r0, tk), b_ref.dtype)
            return (_add_rows(carry, jnp.sum(l1b.astype(F32), axis=1, keepdims=True), r0),
                    _add_rows(acc, _dot(ab, vj, 1, 0), r0))

        state = (jnp.zeros((tq, 1), F32), jnp.zeros((tq, dh), F32))
        for dj in reversed(range(nd)):
            state = tile(i * nd + dj, *state, masks[dj], dj * tk)
        def left_block(t, st):
            for dj in reversed(range(nd)):
                st = tile((i - 1 - t) * nd + dj, st[0], st[1], None)
            return st

        state = lax.fori_loop(0, i, left_block, state)
        o_ref[...] = state[1]

    qspec = pl.BlockSpec((tq, dh), lambda h, i: (i, col0[0] + h))
    kspec = pl.BlockSpec((s, dh), lambda h, i: (0, col0[1] + h))
    vspec = pl.BlockSpec((s, dh), lambda h, i: (0, col0[2] + h))
    saved = pl.BlockSpec((None, nkt, tq, tk), lambda h, i: (h, 0, i, 0))
    saved_shape = jax.ShapeDtypeStruct((heads, nkt, s, tk), BF16)
    return _pcall(
        body, name=name, grid=(heads, nq), in_specs=[qspec, kspec, vspec],
        out_specs=[pl.BlockSpec((tq, dh), lambda h, i: (i, h)), saved, saved],
        out_shape=[jax.ShapeDtypeStruct((s, heads * dh), F32), saved_shape, saved_shape],
        compiler_params=_params("parallel", "parallel"),
    )(proj, proj, proj)


SB_BWD_HEADS = 2


def _sb_bwd(proj, o, a_all, beta_all, do, heads, col0, tq, tk, name):
    s = proj.shape[0]
    dh = SB_HEAD_DIM
    nq, nd, nkt = s // tq, tq // tk, s // tk
    scale = dh ** -0.5
    hb = SB_BWD_HEADS if heads % SB_BWD_HEADS == 0 and all(c % SB_BWD_HEADS == 0 for c in col0) else 1
    wide = hb * dh

    def body(q_ref, k_ref, v_ref, o_ref, a_ref, b_ref, do_ref, dq_ref, dk_ref, dv_ref, dk_acc, dv_acc):
        i = pl.program_id(1)

        @pl.when(i == 0)
        def _():
            dk_acc[...] = jnp.zeros_like(dk_acc)
            dv_acc[...] = jnp.zeros_like(dv_acc)

        lanes = [slice(hh * dh, (hh + 1) * dh) for hh in range(hb)]
        q = [q_ref[:, ln] for ln in lanes]
        dob = [do_ref[:, ln].astype(BF16) for ln in lanes]
        delta = [jnp.sum(dob[hh].astype(F32) * o_ref[:, lanes[hh]], axis=1, keepdims=True) for hh in range(hb)]
        masks, _, tri_left = _sb_masks(tq, tk)

        def tile(hh, j, carry_g, dq, mask):
            start = pl.multiple_of(j * tk, tk)
            kj = k_ref[pl.ds(start, tk), lanes[hh]]
            vj = v_ref[pl.ds(start, tk), lanes[hh]]
            ab = a_ref[hh, j]
            g = _dot(dob[hh], vj, 1, 1) * ab.astype(F32)
            carry_g = carry_g + jnp.sum(g, axis=1, keepdims=True)
            left = (delta[hh] - carry_g) + _dot(g.astype(BF16), tri_left, 1, 0)
            dz = g - b_ref[hh, j].astype(F32) * (g + left)
            if mask is not None:
                dz = jnp.where(mask, dz, 0.0)
            dzb = dz.astype(BF16)
            dk_acc[pl.ds(start, tk), lanes[hh]] += _dot(dzb, q[hh], 0, 0)
            dv_acc[pl.ds(start, tk), lanes[hh]] += _dot(ab, dob[hh], 0, 0)
            return carry_g, dq + _dot(dzb, kj, 1, 0)

        def block(jb, st, use_masks):
            st = list(st)
            for dj in reversed(range(nd)):
                for hh in range(hb):
                    st[hh] = tile(hh, jb * nd + dj, *st[hh], masks[dj] if use_masks else None)
            return tuple(st)

        state = block(i, tuple((jnp.zeros((tq, 1), F32), jnp.zeros((tq, dh), F32)) for _ in range(hb)), True)
        state = lax.fori_loop(0, i, lambda t, st: block(i - 1 - t, st, False), state)
        for hh in range(hb):
            dq_ref[:, lanes[hh]] = (state[hh][1] * scale).astype(dq_ref.dtype)

        @pl.when(i == nq - 1)
        def _():
            dk_ref[...] = (dk_acc[...] * scale).astype(dk_ref.dtype)
            dv_ref[...] = dv_acc[...].astype(dv_ref.dtype)

    qspec = pl.BlockSpec((tq, wide), lambda h, i: (i, col0[0] // hb + h))
    kspec = pl.BlockSpec((s, wide), lambda h, i: (0, col0[1] // hb + h))
    vspec = pl.BlockSpec((s, wide), lambda h, i: (0, col0[2] // hb + h))
    blk = pl.BlockSpec((tq, wide), lambda h, i: (i, h))
    full = pl.BlockSpec((s, wide), lambda h, i: (0, h))
    saved = pl.BlockSpec((hb, nkt, tq, tk), lambda h, i: (h, 0, i, 0))
    act = jax.ShapeDtypeStruct((s, heads * dh), BF16)
    return _pcall(
        body, name=name, grid=(heads // hb, nq), in_specs=[qspec, kspec, vspec, blk, saved, saved, blk],
        out_specs=[blk, full, full], out_shape=[act, act, act],
        scratch_shapes=[pltpu.VMEM((s, wide), F32), pltpu.VMEM((s, wide), F32)],
        compiler_params=_params("parallel", "arbitrary"),
    )(proj, proj, proj, o, a_all, beta_all, do)


def _xattn_probs(q, k, scale):
    sc = _dot(q, k, 1, 1) * scale
    e = jnp.exp(sc - jnp.max(sc, axis=1, keepdims=True))
    return e / jnp.sum(e, axis=1, keepdims=True)


def _xattn_fwd(qc, kv, tq, name):
    s, d = qc.shape
    m = kv.shape[0]
    dh = d // X_HEADS
    scale = dh ** -0.5

    def body(q_ref, k_ref, v_ref, o_ref):
        p = _xattn_probs(q_ref[...], k_ref[...], scale)
        o_ref[...] = _dot(p.astype(BF16), v_ref[...], 1, 0).astype(o_ref.dtype)

    blk = pl.BlockSpec((tq, dh), lambda h, i: (i, h))
    return _pcall(
        body, name=name, grid=(X_HEADS, s // tq),
        in_specs=[blk, pl.BlockSpec((m, dh), lambda h, i: (0, h)), pl.BlockSpec((m, dh), lambda h, i: (0, X_HEADS + h))],
        out_specs=blk, out_shape=jax.ShapeDtypeStruct((s, d), BF16), compiler_params=_params("parallel", "parallel"),
    )(qc, kv, kv)


def _xattn_bwd(qc, kv, do, tq, name):
    s, d = qc.shape
    m = kv.shape[0]
    dh = d // X_HEADS
    scale = dh ** -0.5
    nq = s // tq

    def body(q_ref, k_ref, v_ref, do_ref, dq_ref, dk_ref, dv_ref, dk_acc, dv_acc):
        i = pl.program_id(1)
        q, k, v = q_ref[...], k_ref[...], v_ref[...]
        dob = do_ref[...].astype(BF16)
        p = _xattn_probs(q, k, scale)
        pb = p.astype(BF16)
        dp = _dot(dob, v, 1, 1)
        ds = pb.astype(F32) * (dp - jnp.sum(dp * pb.astype(F32), axis=1, keepdims=True))
        dsb = (ds * scale).astype(BF16)
        dq_ref[...] = _dot(dsb, k, 1, 0).astype(dq_ref.dtype)
        dk_part = _dot(dsb, q, 0, 0)
        dv_part = _dot(pb, dob, 0, 0)

        @pl.when(i == 0)
        def _():
            dk_acc[...] = dk_part
            dv_acc[...] = dv_part

        @pl.when(i > 0)
        def _():
            dk_acc[...] += dk_part
            dv_acc[...] += dv_part

        @pl.when(i == nq - 1)
        def _():
            dk_ref[...] = dk_acc[...].astype(dk_ref.dtype)
            dv_ref[...] = dv_acc[...].astype(dv_ref.dtype)

    blk = pl.BlockSpec((tq, dh), lambda h, i: (i, h))
    kblk = pl.BlockSpec((m, dh), lambda h, i: (0, h))
    return _pcall(
        body, name=name, grid=(X_HEADS, nq),
        in_specs=[blk, kblk, pl.BlockSpec((m, dh), lambda h, i: (0, X_HEADS + h)), blk],
        out_specs=[blk, kblk, kblk],
        out_shape=[jax.ShapeDtypeStruct((s, d), BF16), jax.ShapeDtypeStruct((m, d), BF16), jax.ShapeDtypeStruct((m, d), BF16)],
        scratch_shapes=[pltpu.VMEM((m, dh), F32), pltpu.VMEM((m, dh), F32)],
        compiler_params=_params("parallel", "arbitrary"),
    )(qc, kv, kv, do)


def _local_step(x, mem, tgt, w, fetch=None, prefetch=None, emit=None, tick=None, after=None):
    fetch = fetch or (lambda name, after: {})
    prefetch = prefetch or (lambda name, after: None)
    emit = emit or (lambda group, g: None)
    tick = tick or (lambda group, after: None)
    w = dict(w)
    s, d = x.shape
    heads = d // SB_HEAD_DIM
    tm = _pick(s, (1024, 512, 256, 128))
    tq = _pick(s, (1024, 512, 256, 128))
    sb_tq, sb_tk = _pick(s, (512, 256, 128)), _pick(s, (256, 128))
    tc = _pick(d, (256, 128))
    g = {}

    def wt(name, after):
        if name not in w:
            w.update(fetch(name, after))
        return w[name]

    def ffn_fwd(h, gname, wgu, wdown, tag, after=None):
        n = _rms_fwd(h, w[gname], tag + "_norm", tm, after=after)
        gu, act = _ffn_up(n, wt(wgu, n), tag + "_gu")
        prefetch(wdown, gu)
        return n, gu, act, _mm(act, wt(wdown, act), name=tag + "_down", out_dtype=F32, res=h, alpha=0.5)

    def ffn_bwd(dh, dhb, h, saved, gname, wgu, wdown, tag, copy_scale=None, after=None):
        n, gu, act = saved
        g[wdown] = _mm(act, dhb, ta=True, name=tag + "_dwdown", after=after)
        dgu = _ffn_dgu(dhb, w[wdown], gu, tag + "_dgu", after=emit(tag + "_down", g))
        g[wgu] = _mm(n, dgu, ta=True, b_halves=True, name=tag + "_dwgu", after=tick(tag + "_down", dgu))
        *dh_in, g[gname] = _dgrad_norm(dgu, w[wgu], dh, h, w[gname], tag + "_dn", dy_halves=True, copy_scale=copy_scale,
                                       after=emit(tag, g))
        return dh_in, tick(tag, dh_in[0])

    n1, gu1, act1, h1 = ffn_fwd(x, "g_ffn1", "w_ffn1_gu", "w_ffn1_down", "ffn1", after)
    prefetch("w_in", h1)
    u = _rms_fwd(h1, w["g_mix"], "mix_norm", tm)
    proj = _mm(u, wt("w_in", u), name="mix_in")
    prefetch("w_conv_out", proj)
    nd = d // SB_HEAD_DIM
    y_conv = _conv_fwd(proj, w["conv_w"], d, tc, "conv_fwd")
    sb_cols = (3 * nd, 4 * nd, 5 * nd)
    y_sb, sb_a, sb_beta = _sb_fwd(proj, heads, sb_cols, _pick(s, (2 * sb_tq, sb_tq)), sb_tk, "sb_fwd")
    prefetch("w_cq", y_sb)
    a_conv = _mm(y_conv, wt("w_conv_out", y_conv), name="conv_out")
    a_sb = _mm(y_sb, wt("w_attn_out", y_sb), name="attn_out")
    b_conv, b_sb = w["b_gate"][:, :d], w["b_gate"][:, d:]

    def merge(ac, asb, gcp, gsp, bc, bs):
        gc = _sigmoid(gcp.astype(F32) + bc)
        gs = _sigmoid(gsp.astype(F32) + bs)
        return gc * ac.astype(F32) + gs * asb.astype(F32)

    merged = _rowcall(merge, [_whole(a_conv), _whole(a_sb), (proj, 6, d), (proj, 7, d)], [b_conv, b_sb], [(d, BF16)],
                      tm=tm, name="merge")[0]
    prefetch("w_ffn2_gu", merged)
    h2 = _mm(merged, wt("w_o", merged), name="mix_out", out_dtype=F32, res=h1)
    hn = _rms_fwd(h2, w["g_cross"], "cross_norm", tm)
    mn = _rms_fwd(mem, w["g_mem"], "mem_norm", _pick(mem.shape[0], (256, 128)))
    qc = _mm(hn, wt("w_cq", hn), name="cross_q")
    kv = _mm(mn, wt("w_ckv", mn), name="cross_kv")
    oc = _xattn_fwd(qc, kv, tq, "xattn_fwd")
    h3 = _mm(oc, wt("w_co", oc), name="cross_out", out_dtype=F32, res=h2)
    n2, gu2, act2, h4 = ffn_fwd(h3, "g_ffn2", "w_ffn2_gu", "w_ffn2_down", "ffn2")

    def head(hb, tb, gb):
        xh, r = _xhat(hb)
        err = xh * gb - tb
        dy = err * (1.0 / d)
        dxh = dy * gb
        dx = r * (dxh - xh * jnp.mean(dxh * xh, axis=-1, keepdims=True))
        row_loss = 0.5 * jnp.mean(err * err, axis=-1, keepdims=True)
        return dx, 0.5 * dx, dy * xh, jnp.broadcast_to(row_loss, (row_loss.shape[0], LANES))

    dh4, dh4b, g["g_final"], loss_lanes = _rowcall(head, [_whole(h4), _whole(tgt)], [w["g_final"]], [(d, F32), (d, BF16)],
                                                   [d, LANES], tm=tm, name="loss_head")

    (dh3, dh3b), tok = ffn_bwd(dh4, dh4b, h3, (n2, gu2, act2), "g_ffn2", "w_ffn2_gu", "w_ffn2_down", "ffn2", copy_scale=1.0)
    g["w_co"] = _mm(oc, dh3b, ta=True, name="cross_dwco", after=tok)
    doc = _mm(dh3b, w["w_co"], tb=True, name="cross_doc")
    dqc, dk, dv = _xattn_bwd(qc, kv, doc, tq, "xattn_bwd")
    dkv = jnp.concatenate([dk, dv], axis=1)
    g["w_cq"] = _mm(hn, dqc, ta=True, name="cross_dwcq")
    g["w_ckv"] = _mm(mn, dkv, ta=True, name="cross_dwckv")
    dmn = _mm(dkv, w["w_ckv"], tb=True, name="cross_dmn", out_dtype=F32)
    g["g_mem"] = _rowcall(lambda dy, xb: dy * _xhat(xb)[0], [_whole(dmn), _whole(mem)], [], [], [d],
                          tm=_pick(mem.shape[0], (256, 128)), name="mem_dnorm")[0]
    dh2, dh2b, g["g_cross"] = _dgrad_norm(dqc, w["w_cq"], dh3, h2, w["g_cross"], "cross_dhn", copy_scale=1.0, after=emit("cross", g))

    g["w_o"] = _mm(merged, dh2b, ta=True, name="mix_dwo", after=tick("cross", dh2))
    dmerged = _mm(dh2b, w["w_o"], tb=True, name="mix_dmerged")

    def merge_bwd(dm, ac, asb, gcp, gsp, bc, bs):
        dm, ac, asb = dm.astype(F32), ac.astype(F32), asb.astype(F32)
        gc = _sigmoid(gcp.astype(F32) + bc)
        gs = _sigmoid(gsp.astype(F32) + bs)
        dgc = dm * ac * gc * (1.0 - gc)
        dgs = dm * asb * gs * (1.0 - gs)
        return dm * gc, dm * gs, dgc, dgs, dgc, dgs

    da_conv, da_sb, dgc, dgs, db_conv, db_sb = _rowcall(
        merge_bwd, [_whole(dmerged), _whole(a_conv), _whole(a_sb), (proj, 6, d), (proj, 7, d)], [b_conv, b_sb],
        [(d, BF16)] * 4, [d, d], tm=tm, name="merge_bwd")
    g["b_gate"] = jnp.concatenate([db_conv, db_sb], axis=1)
    g["w_conv_out"] = _mm(y_conv, da_conv, ta=True, name="conv_dwout")
    g["w_attn_out"] = _mm(y_sb, da_sb, ta=True, name="attn_dwout")
    dy_conv = _mm(da_conv, w["w_conv_out"], tb=True, name="conv_dy")
    dy_sb = _mm(da_sb, w["w_attn_out"], tb=True, name="attn_dy")
    dcb, dcc, dcx, g["conv_w"] = _conv_bwd(dy_conv, proj, w["conv_w"], d, tc, "conv_bwd")
    dq, dk_sb, dv_sb = _sb_bwd(proj, y_sb, sb_a, sb_beta, dy_sb, heads, sb_cols, sb_tq, sb_tk, "sb_bwd")
    dproj = jnp.concatenate([dcb, dcc, dcx, dq, dk_sb, dv_sb, dgc, dgs], axis=1)
    g["w_in"] = _mm(u, dproj, ta=True, name="mix_dwin")
    dh1, dh1b, g["g_mix"] = _dgrad_norm(dproj, w["w_in"], dh2, h1, w["g_mix"], "mix_du", copy_scale=0.5, after=emit("mix", g))
    (dx,), tok = ffn_bwd(dh1, dh1b, x, (n1, gu1, act1), "g_ffn1", "w_ffn1_gu", "w_ffn1_down", "ffn1", after=tick("mix", dh1))
    return loss_lanes, dx, g, tok


MATS = (("w_ffn1_gu", "col"), ("w_ffn1_down", "row"), ("w_in", "col"), ("w_conv_out", "row"), ("w_attn_out", "row"),
        ("w_o", "row"), ("w_cq", "row"), ("w_ckv", "col"), ("w_co", "row"), ("w_ffn2_gu", "col"), ("w_ffn2_down", "row"))
VECS = ("g_ffn1", "g_mix", "g_cross", "g_mem", "g_ffn2", "g_final")
WEIGHTS = ("g_ffn1", "w_ffn1_gu", "w_ffn1_down", "g_mix", "w_in", "b_gate", "conv_w", "w_conv_out", "w_attn_out", "w_o",
           "g_cross", "g_mem", "w_cq", "w_ckv", "w_co", "g_ffn2", "w_ffn2_gu", "w_ffn2_down", "g_final")
CONV_ROWS = 16


def _full_shape(kind, r, c):
    return (r, N_CHIPS * c) if kind == "col" else (N_CHIPS * r, c)


def _piece(ref, kind, r, c, chip, half):
    hr = r // 2
    if kind == "col":
        return ref.at[pl.ds(pl.multiple_of(half * hr, math.gcd(hr, 16)), hr), pl.ds(pl.multiple_of(chip * c, LANES), c)]
    return ref.at[pl.ds(pl.multiple_of(chip * r + half * hr, math.gcd(hr, 16)), hr), :]


def _shard_of(ref, kind, r, c, chip):
    if kind == "col":
        return ref.at[:, pl.ds(pl.multiple_of(chip * c, LANES), c)]
    return ref.at[pl.ds(pl.multiple_of(chip * r, 16), r), :]


def _place():
    x, y, c = lax.axis_index("x"), lax.axis_index("y"), lax.axis_index("c")
    others = [(1 - x, y), (x, 1 - y), (1 - x, 1 - y)]
    return x, y, c, 2 * x + y, others


def _remote(src, dst, send_sem, recv_sem, to):
    return pltpu.make_async_remote_copy(src_ref=src, dst_ref=dst, send_sem=send_sem, recv_sem=recv_sem,
                                        device_id=to, device_id_type=MESH)


HBM = pl.BlockSpec(memory_space=pltpu.HBM)
SEM = pl.BlockSpec(memory_space=pltpu.SEMAPHORE)
EFFECT = pltpu.SideEffectType.DATAFLOW_SIDE_EFFECTING
TOKEN = (8, LANES)


def _split_start(name, plan, n_copies, srcs, lands, after=None):
    ns, nl = len(srcs), len(lands)
    n_in = ns + nl + (after is not None)

    def body(*refs):
        outs = refs[n_in:]
        sends, _ = plan(refs[:ns], refs[ns:ns + nl], outs[0], outs[1])
        for cp in sends:
            cp.start()
        outs[-1][...] = jnp.zeros(TOKEN, F32)

    held = [pltpu.HBM(a.shape, a.dtype) for a in (*srcs, *lands)]
    dma = pltpu.SemaphoreType.DMA((n_copies,))
    ins = [pltpu.with_memory_space_constraint(a, pltpu.HBM) for a in (*srcs, *lands)]
    outs = _pcall(
        body, name=name, in_specs=[HBM] * (ns + nl) + ([] if after is None else [ANY]),
        out_specs=(SEM, SEM, *[HBM] * (ns + nl), pl.BlockSpec(memory_space=pltpu.VMEM)),
        out_shape=(dma, dma, *held, jax.ShapeDtypeStruct(TOKEN, F32)),
        input_output_aliases={i: 2 + i for i in range(ns + nl)},
        compiler_params=pltpu.CompilerParams(has_side_effects=EFFECT),
    )(*ins, *([] if after is None else [after]))
    return outs[0], outs[1], list(outs[2:2 + ns]), list(outs[2 + ns:2 + ns + nl]), outs[-1]


def _split_wait(name, plan, send_sems, recv_sems, srcs, lands, after):
    ns, nl = len(srcs), len(lands)

    def body(*refs):
        sends, recvs = plan(refs[:ns], refs[ns:ns + nl], refs[ns + nl], refs[ns + nl + 1])
        for cp in sends:
            cp.wait_send()
        for cp in recvs:
            cp.wait_recv()

    outs = _pcall(
        body, name=name, in_specs=[HBM] * (ns + nl) + [SEM, SEM, ANY], out_specs=[HBM] * (ns + nl),
        out_shape=[pltpu.HBM(a.shape, a.dtype) for a in (*srcs, *lands)],
        input_output_aliases={i: i for i in range(ns + nl)},
        compiler_params=pltpu.CompilerParams(has_side_effects=EFFECT),
    )(*srcs, *lands, send_sems, recv_sems, after)
    return list(outs[:ns]), list(outs[ns:])


def _gather_plan(dims):
    def plan(shard_refs, full_refs, ss, rs):
        x, y, c, me, others = _place()
        sends, recvs = [], []
        for wi, (kind, r, cw) in enumerate(dims):
            half = shard_refs[wi].at[pl.ds(pl.multiple_of(c * (r // 2), math.gcd(r // 2, 16)), r // 2), :]
            for k, (ox, oy) in enumerate(others):
                sem = 4 * wi + k
                sends.append(_remote(half, _piece(full_refs[wi], kind, r, cw, me, c), ss.at[sem], rs.at[sem], (ox, oy, c)))
                recvs.append(_remote(half, _piece(full_refs[wi], kind, r, cw, 2 * ox + oy, c), ss.at[sem], rs.at[sem], (x, y, c)))
            sem = 4 * wi + 3
            own = _remote(shard_refs[wi], _shard_of(full_refs[wi], kind, r, cw, me), ss.at[sem], rs.at[sem], (x, y, 1 - c))
            sends.append(own)
            recvs.append(own)
        return sends, recvs

    return plan


def _forward_plan(dims):
    def plan(_, full_refs, ss, rs):
        x, y, c, _, others = _place()
        sends, recvs = [], []
        for wi, (kind, r, cw) in enumerate(dims):
            for k, (ox, oy) in enumerate(others):
                sem = 3 * wi + k
                mine = _piece(full_refs[wi], kind, r, cw, 2 * ox + oy, c)
                theirs = _piece(full_refs[wi], kind, r, cw, 2 * ox + oy, 1 - c)
                sends.append(_remote(mine, mine, ss.at[sem], rs.at[sem], (x, y, 1 - c)))
                recvs.append(_remote(theirs, theirs, ss.at[sem], rs.at[sem], (x, y, 1 - c)))
        return sends, recvs

    return plan


def _rs_cores_plan(dims):
    def plan(g_refs, land_refs, ss, rs):
        x, y, c, _, _ = _place()
        sends, recvs = [], []
        for wi, dm in enumerate(dims):
            for chip in range(N_CHIPS):
                sem = N_CHIPS * wi + chip
                sends.append(_remote(_piece(g_refs[wi], *dm, chip, 1 - c), land_refs[wi].at[chip], ss.at[sem], rs.at[sem], (x, y, 1 - c)))
                recvs.append(_remote(_piece(g_refs[wi], *dm, chip, c), land_refs[wi].at[chip], ss.at[sem], rs.at[sem], (x, y, 1 - c)))
        return sends, recvs

    return plan


def _share_plan(nw):
    def plan(_, buf_refs, ss, rs):
        x, y, c, _, _ = _place()
        sends = [_remote(buf_refs[wi].at[c], buf_refs[wi].at[c], ss.at[wi], rs.at[wi], (x, y, 1 - c)) for wi in range(nw)]
        recvs = [_remote(buf_refs[wi].at[1 - c], buf_refs[wi].at[1 - c], ss.at[wi], rs.at[wi], (x, y, 1 - c)) for wi in range(nw)]
        return sends, recvs

    return plan


def _small_plan():
    def plan(_, buf_refs, ss, rs):
        x, y, c = lax.axis_index("x"), lax.axis_index("y"), lax.axis_index("c")
        buf = buf_refs[0]
        sends, recvs = [], []
        for rel in range(1, N_DEV):
            peer = (x ^ (rel >> 2 & 1), y ^ (rel >> 1 & 1), c ^ (rel & 1))
            sends.append(_remote(buf.at[0], buf.at[rel], ss.at[rel - 1], rs.at[rel - 1], peer))
            recvs.append(_remote(buf.at[0], buf.at[rel], ss.at[rel - 1], rs.at[rel - 1], peer))
        return sends, recvs

    return plan


def _sum_small(buf, me, name):
    _, rows, n = buf.shape

    def body(me_ref, b_ref, o_ref):
        tot = b_ref[me_ref[0]]
        for dev in range(1, N_DEV):
            tot = tot + b_ref[dev ^ me_ref[0]]
        o_ref[...] = tot

    return _pcall(
        body, name=name, out_shape=jax.ShapeDtypeStruct((rows, n), F32),
        grid_spec=pltpu.PrefetchScalarGridSpec(
            num_scalar_prefetch=1, grid=(1,), in_specs=[pl.BlockSpec((N_DEV, rows, n), lambda i, m: (0, 0, 0))],
            out_specs=pl.BlockSpec((rows, n), lambda i, m: (0, 0))),
    )(me, buf)


def _rs_chips_plan(nw):
    def plan(p_refs, land_refs, ss, rs):
        x, y, c, me, others = _place()
        sends, recvs = [], []
        for wi in range(nw):
            for k, (ox, oy) in enumerate(others):
                sem = 3 * wi + k
                sends.append(_remote(p_refs[wi].at[2 * ox + oy], land_refs[wi].at[k], ss.at[sem], rs.at[sem], (ox, oy, c)))
                recvs.append(_remote(p_refs[wi].at[me], land_refs[wi].at[k], ss.at[sem], rs.at[sem], (x, y, c)))
        return sends, recvs

    return plan


SUM_BLOCK_BYTES = 4 << 20


def _rows_per_block(n, c, limit_bytes=2 << 20):
    best = None
    for tm in range(16, n + 1, 16):
        if n % tm == 0 and tm * c * 4 <= limit_bytes:
            best = tm
    return best or n


def _sum_cores(grad, got, kind, place, name):
    _, hr, cw = got.shape
    tm = _rows_per_block(hr, cw, SUM_BLOCK_BYTES)
    nb = hr // tm

    def body(place_ref, g_ref, t_ref, o_ref):
        o_ref[...] = (g_ref[...].astype(F32) + t_ref[...].astype(F32)).astype(o_ref.dtype)

    if kind == "col":
        g_spec = pl.BlockSpec((tm, cw), lambda j, i, pr: (pr[0] * nb + i, j))
    else:
        g_spec = pl.BlockSpec((tm, cw), lambda j, i, pr: ((2 * j + pr[0]) * nb + i, 0))
    blk = pl.BlockSpec((None, tm, cw), lambda j, i, pr: (j, i, 0))
    return _pcall(
        body, name=name, out_shape=jax.ShapeDtypeStruct(got.shape, BF16),
        grid_spec=pltpu.PrefetchScalarGridSpec(num_scalar_prefetch=1, grid=(N_CHIPS, nb), in_specs=[g_spec, blk], out_specs=blk),
        compiler_params=_params("parallel", "parallel"),
    )(place, grad, got)


def _sum_chips(parts, got, place, name):
    _, n, cw = got.shape
    tm = _rows_per_block(n, cw, SUM_BLOCK_BYTES)

    def body(place_ref, p_ref, g_ref, o_ref):
        tot = p_ref[...].astype(F32)
        for k in range(3):
            tot = tot + g_ref[k].astype(F32)
        o_ref[...] = tot

    return _pcall(
        body, name=name, out_shape=jax.ShapeDtypeStruct((2, n, cw), F32),
        grid_spec=pltpu.PrefetchScalarGridSpec(
            num_scalar_prefetch=1, grid=(n // tm,),
            in_specs=[pl.BlockSpec((None, tm, cw), lambda i, pr: (pr[1], i, 0)), pl.BlockSpec((3, tm, cw), lambda i, pr: (0, i, 0))],
            out_specs=pl.BlockSpec((None, tm, cw), lambda i, pr: (pr[0], i, 0))),
        compiler_params=_params("parallel"),
    )(place, parts, got)


def _adamw(g, w, m, v, name):
    n, c = g.shape
    c1 = 1.0 - ADAM_B1 ** ADAM_STEP
    c2 = 1.0 - ADAM_B2 ** ADAM_STEP

    def fn(gb, wb, mb, vb):
        m_new = ADAM_B1 * mb + (1.0 - ADAM_B1) * gb
        v_new = ADAM_B2 * vb + (1.0 - ADAM_B2) * (gb * gb)
        delta = -ADAM_LR * ((m_new / c1) / (jnp.sqrt(v_new / c2) + ADAM_EPS) + ADAM_WD * wb)
        return gb, delta, m_new, v_new

    tm = _rows_per_block(n, c) if n % 16 == 0 else n
    return _rowcall(fn, [_whole(g), _whole(w), _whole(m), _whole(v)], [], [(c, F32)] * 4, tm=tm, name=name)


PACK_ROWS = 16


def _pack_rows(parts, width, name, after=None):
    assert sum(p.shape[0] for p in parts) <= PACK_ROWS

    def body(*refs):
        out_ref = refs[-1]
        out_ref[...] = jnp.zeros_like(out_ref)
        at = 0
        for r in refs[:len(parts)]:
            k, n = r.shape
            if n == width:
                out_ref[at:at + k, :] = r[...]
            else:
                out_ref[at:at + k, :] = jnp.broadcast_to(r[:, :1], (k, width))
            at += k

    vm = pl.BlockSpec(memory_space=pltpu.VMEM)
    return _pcall(body, name=name, in_specs=[vm] * len(parts) + ([] if after is None else [ANY]), out_specs=vm,
                  out_shape=jax.ShapeDtypeStruct((PACK_ROWS, width), F32))(*parts, *([] if after is None else [after]))


def _cast_shard(wm, name, after):
    n, c = wm.shape
    return _rowcall(lambda v: v, [_whole(wm)], [], [(c, BF16)], tm=_rows_per_block(n, c), name=name, after=after)[0]


GATHER_GROUPS = (
    ("w_ffn1_gu", "conv_w"), ("w_ffn1_down",), ("w_in",), ("w_conv_out", "w_attn_out", "w_o"), ("w_cq", "w_ckv", "w_co"),
    ("w_ffn2_gu", "w_ffn2_down"),
)
REDUCE_GROUPS = {
    "ffn2": ("w_ffn2_down", "w_ffn2_gu"),
    "cross": ("w_co", "w_cq", "w_ckv"),
    "mix": ("w_o", "w_conv_out", "w_attn_out", "w_in"),
    "ffn1_down": ("w_ffn1_down",),
    "ffn1": ("w_ffn1_gu",),
}
TAIL_STAGES = (("ffn2", "cross"), ("mix",), ("ffn1_down", "ffn1"))
KIND = dict(MATS)


def _step(x, mem, tgt, wts, m_in, v_in):
    d = x.shape[-1]
    cc = wts["conv_w"].shape[1]
    place = jnp.stack([lax.axis_index("c"), 2 * lax.axis_index("x") + lax.axis_index("y")]).astype(jnp.int32)
    dims = {n: (kind, *wts[n].shape) for n, kind in MATS}
    dims["conv_w"] = ("col", CONV_ROWS, cc)

    w = {n: wts[n].reshape(1, -1) for n in VECS + ("b_gate",)}
    flying, token = {}, None
    for names in GATHER_GROUPS:
        gd = [dims[n] for n in names]
        shards = [jnp.pad(wts[n], ((0, CONV_ROWS - CONV_K), (0, 0))) if n == "conv_w" else _cast_shard(wts[n], "cast_" + n, token)
                  for n in names]
        lands = [lax.empty(_full_shape(*dm), sh.dtype) for dm, sh in zip(gd, shards)]
        plan = _gather_plan(gd)
        ss, rs, srcs, lands, token = _split_start("gather_start_" + names[0], plan, 4 * len(names), shards, lands, token)
        flying.update({n: (names, plan, ss, rs, srcs, lands, gd) for n in names})

    passing = {}

    def prefetch(name, after):
        if name not in passing:
            names, plan, ss, rs, srcs, lands, gd = flying[name]
            _, lands = _split_wait("gather_wait_" + names[0], plan, ss, rs, srcs, lands, after)
            plan = _forward_plan(gd)
            ss, rs, _, lands, _ = _split_start("forward_start_" + names[0], plan, 3 * len(names), [], lands)
            passing.update({n: (names, plan, ss, rs, lands) for n in names})

    def fetch(name, after):
        prefetch(name, after)
        names, plan, ss, rs, lands = passing[name]
        _, lands = _split_wait("forward_wait_" + names[0], plan, ss, rs, [], lands, after)
        return {n: (land[:CONV_K] if n == "conv_w" else land) for n, land in zip(names, lands)}

    swapping, sent = {}, {}

    def emit(tag, g):
        if tag not in REDUCE_GROUPS:
            return None
        names = REDUCE_GROUPS[tag]
        gd = [dims[n] for n in names]
        lands = [lax.empty((N_CHIPS, r // 2, cw), BF16) for (_, r, cw) in gd]
        plan = _rs_cores_plan(gd)
        ss, rs, srcs, lands, tok = _split_start("rs_cores_start_" + tag, plan, N_CHIPS * len(names), [g[n] for n in names], lands)
        swapping[tag] = (plan, ss, rs, srcs, lands)
        return tok

    def tick(tag, after):
        if tag not in REDUCE_GROUPS:
            return None
        names = REDUCE_GROUPS[tag]
        plan, ss, rs, srcs, lands = swapping[tag]
        mine, got = _split_wait("rs_cores_wait_" + tag, plan, ss, rs, srcs, lands, after)
        parts = [_sum_cores(gm, t, KIND[n], place, "sum_cores_" + n) for n, gm, t in zip(names, mine, got)]
        lands = [lax.empty((3, *p.shape[1:]), BF16) for p in parts]
        plan = _rs_chips_plan(len(names))
        ss, rs, srcs, lands, tok = _split_start("rs_chips_start_" + tag, plan, 3 * len(names), parts, lands)
        sent[tag] = (plan, ss, rs, srcs, lands)
        return tok

    loss_lanes, dx, g, last = _local_step(x[0], mem[0], tgt[0], w, fetch, prefetch, emit, tick, token)

    rows = [g[n] for n in VECS] + [g["b_gate"][:, :d], g["b_gate"][:, d:], g["conv_w"], loss_lanes]
    packed = _pack_rows(rows, d, "pack_small", after=last)
    small = jnp.concatenate([packed[None], jnp.zeros((N_DEV - 1, *packed.shape), F32)], axis=0)
    small_plan = _small_plan()
    small_ss, small_rs, _, small, after = _split_start("small_start", small_plan, N_DEV - 1, [], [small])

    grads, out = {}, {}

    def update(n):
        shape = wts[n].shape
        as2d = (lambda a: a.reshape(1, -1)) if len(shape) == 1 else (lambda a: a)
        return [r.reshape(shape) for r in _adamw(grads[n], as2d(wts[n]), as2d(m_in[n]), as2d(v_in[n]), "adamw_" + n)]

    def finish(sharing, after):
        tag, names, plan, ss, rs, halves = sharing
        _, both = _split_wait("share_wait_" + tag, plan, ss, rs, [], halves, after)
        for n, b in zip(names, both):
            grads[n] = b.reshape(-1, b.shape[-1])
            out[n] = update(n)
        return out[names[-1]][1]

    sharing = None
    for stage in TAIL_STAGES:
        names, halves = [], []
        for tag in stage:
            plan, ss, rs, srcs, lands = sent[tag]
            parts, landed = _split_wait("rs_chips_wait_" + tag, plan, ss, rs, srcs, lands, after)
            halves += [_sum_chips(p, t, place, "sum_chips_" + n) for n, p, t in zip(REDUCE_GROUPS[tag], parts, landed)]
            names += REDUCE_GROUPS[tag]
        plan = _share_plan(len(names))
        ss, rs, _, halves, after = _split_start("share_start_" + stage[0], plan, len(names), [], halves)
        if sharing is not None:
            after = finish(sharing, after)
        sharing = (stage[0], names, plan, ss, rs, halves)
    after = finish(sharing, after)

    _, small = _split_wait("small_wait", small_plan, small_ss, small_rs, [], small, after)
    me = (4 * lax.axis_index("x") + 2 * lax.axis_index("y") + lax.axis_index("c")).astype(jnp.int32).reshape(1)
    red = _sum_small(small[0], me, "sum_small")
    grads.update({n: red[i:i + 1] for i, n in enumerate(VECS)})
    nv = len(VECS)
    grads["b_gate"] = jnp.concatenate([red[nv:nv + 1], red[nv + 1:nv + 2]], axis=1)
    chip = 2 * lax.axis_index("x") + lax.axis_index("y")
    grads["conv_w"] = lax.dynamic_slice_in_dim(red[nv + 2:nv + 2 + CONV_K], chip * cc, cc, axis=1)
    loss = red[nv + 2 + CONV_K, 0]
    out.update({n: update(n) for n in WEIGHTS if n not in KIND})
    return (loss, dx[None], *[out[n][0] for n in WEIGHTS], *[out[n][1] for n in WEIGHTS],
            *[out[n][2] for n in WEIGHTS], *[out[n][3] for n in WEIGHTS])


def kernel(x, mem, g_ffn1, w_ffn1_gu, w_ffn1_down, g_mix, w_in, b_gate, conv_w, w_conv_out, w_attn_out, w_o, g_cross, g_mem, w_cq, w_ckv, w_co, g_ffn2, w_ffn2_gu, w_ffn2_down, g_final, loss_target, m_g_ffn1, m_w_ffn1_gu, m_w_ffn1_down, m_g_mix, m_w_in, m_b_gate, m_conv_w, m_w_conv_out, m_w_attn_out, m_w_o, m_g_cross, m_g_mem, m_w_cq, m_w_ckv, m_w_co, m_g_ffn2, m_w_ffn2_gu, m_w_ffn2_down, m_g_final, v_g_ffn1, v_w_ffn1_gu, v_w_ffn1_down, v_g_mix, v_w_in, v_b_gate, v_conv_w, v_w_conv_out, v_w_attn_out, v_w_o, v_g_cross, v_g_mem, v_w_cq, v_w_ckv, v_w_co, v_g_ffn2, v_w_ffn2_gu, v_w_ffn2_down, v_g_final):
    given = dict(locals())
    wts = {n: given[n] for n in WEIGHTS}
    m_in = {n: given["m_" + n] for n in WEIGHTS}
    v_in = {n: given["v_" + n] for n in WEIGHTS}
    return _step(x, mem, loss_target, wts, m_in, v_in)
```

```python
import math

import jax
import jax.numpy as jnp
from jax import lax
from jax.experimental import pallas as pl
from jax.experimental.pallas import tpu as pltpu

F32 = jnp.float32
BF16 = jnp.bfloat16
MESH = pl.DeviceIdType.MESH

V7X_VMEM_LIMIT_BYTES = 48 * 1024 * 1024
MM_VMEM_BUDGET_BYTES = 36 * 1024 * 1024
MM_WHOLE_K = 2816
LANES = 128
SB_HEAD_DIM = 128
X_HEADS = 4
CONV_K = 3
RMS_EPS = 1e-6
N_CHIPS = 4
N_DEV = 8
ADAM_LR, ADAM_B1, ADAM_B2, ADAM_EPS, ADAM_WD, ADAM_STEP = 0.001, 0.9, 0.999, 1e-08, 0.01, 10


ANY = pl.BlockSpec(memory_space=pl.ANY)


def _pcall(body, **kw):
    return pl.pallas_call(body, **kw)


def _params(*sem):
    return pltpu.CompilerParams(dimension_semantics=sem, vmem_limit_bytes=V7X_VMEM_LIMIT_BYTES)


def _pick(dim, cands):
    for c in cands:
        if dim % c == 0:
            return c
    return dim


def _dot(a, b, ca, cb):
    return lax.dot_general(a, b, (((ca,), (cb,)), ((), ())), preferred_element_type=F32)


def _mm(a, b, *, name, ta=False, tb=False, out_dtype=BF16, res=None, alpha=1.0, tm=None, tn=None, tk=None, after=None,
        a_halves=False, b_halves=False):
    assert not (a_halves and ta) and not (b_halves and tb)
    if a_halves:
        m, k = a.shape[1], 2 * a.shape[2]
    else:
        m, k = (a.shape[1], a.shape[0]) if ta else a.shape
    if b_halves:
        n = 2 * b.shape[2]
        assert k == b.shape[1]
    else:
        n = b.shape[0] if tb else b.shape[1]
        assert k == (b.shape[1] if tb else b.shape[0]), (a.shape, b.shape, ta, tb)
    if ta:
        tm = tm or _pick(m, (512, 256, 128))
        tn = tn or _pick(n, (1024, 512, 256, 128))
        tk = tk or (k if k <= MM_WHOLE_K else _pick(k, (1024, 512, 256, 128)))
    else:
        tk = tk or (k if k <= MM_WHOLE_K else _pick(k, (MM_WHOLE_K, 2048, 1024, 512, 256, 128)))
        tn = tn or _pick(n, (512, 1408, 256, 128) if tk == k else (1024, 512, 256, 128))
        per_row = 2 * (tk * a.dtype.itemsize + tn * (jnp.dtype(out_dtype).itemsize + (0 if res is None else res.dtype.itemsize)))
        per_row += 4 * tn if tk < k else 0
        rows = (MM_VMEM_BUDGET_BYTES - 2 * tk * tn * b.dtype.itemsize) // per_row
        tm = tm or next((c for c in (2048, 1024, 512, 256, 128) if m % c == 0 and c <= rows), m)
    if a_halves:
        tk = min(tk, k // 2) if (k // 2) % min(tk, k // 2) == 0 else _pick(k // 2, (1408, 1024, 512, 256, 128))
    if b_halves:
        tn = tn if (n // 2) % tn == 0 else _pick(n // 2, (1408, 1024, 512, 256, 128))
    nk = k // tk
    assert m % tm == 0 and n % tn == 0 and k % tk == 0
    a_spec = pl.BlockSpec((tk, tm), lambda i, j, kk: (kk, i)) if ta else pl.BlockSpec((tm, tk), lambda i, j, kk: (i, kk))
    b_spec = pl.BlockSpec((tn, tk), lambda i, j, kk: (j, kk)) if tb else pl.BlockSpec((tk, tn), lambda i, j, kk: (kk, j))
    if a_halves:
        per = (k // 2) // tk
        a_spec = pl.BlockSpec((None, tm, tk), lambda i, j, kk: (kk // per, i, kk % per))
    if b_halves:
        per_n = (n // 2) // tn
        b_spec = pl.BlockSpec((None, tk, tn), lambda i, j, kk: (j // per_n, kk, j % per_n))
    o_spec = pl.BlockSpec((tm, tn), lambda i, j, kk: (i, j))
    ca, cb = (0 if ta else 1), (1 if tb else 0)

    n_in = 2 + (res is not None) + (after is not None)

    def body(*refs):
        a_ref, b_ref = refs[:2]
        res_ref = refs[2] if res is not None else None
        o_ref = refs[n_in]
        scratch = refs[n_in + 1:]

        def finish(acc):
            val = acc if alpha == 1.0 else alpha * acc
            if res_ref is not None:
                val = res_ref[...].astype(F32) + val
            o_ref[...] = val.astype(o_ref.dtype)

        part = _dot(a_ref[...].astype(BF16), b_ref[...].astype(BF16), ca, cb)
        if nk == 1:
            finish(part)
        else:
            acc_ref = scratch[0]
            kk = pl.program_id(2)

            @pl.when(kk == 0)
            def _():
                acc_ref[...] = part

            @pl.when(kk > 0)
            def _():
                acc_ref[...] += part

            @pl.when(kk == nk - 1)
            def _():
                finish(acc_ref[...])

    ins = [a, b] + ([] if res is None else [res]) + ([] if after is None else [after])
    in_specs = [a_spec, b_spec] + ([] if res is None else [o_spec]) + ([] if after is None else [ANY])
    return _pcall(
        body, name=name, grid=(m // tm, n // tn, nk), in_specs=in_specs, out_specs=o_spec,
        out_shape=jax.ShapeDtypeStruct((m, n), out_dtype),
        scratch_shapes=[pltpu.VMEM((tm, tn), F32)] if nk > 1 else [],
        compiler_params=_params("parallel", "parallel", "arbitrary"),
    )(*ins)


def _rowcall(fn, rows, consts, outs, accs=(), *, tm, name, after=None):
    s = rows[0][0].shape[0]
    assert s % tm == 0
    n_read, n_out = len(rows) + len(consts), len(outs)
    n_in = n_read + (after is not None)

    def body(*refs):
        vals = fn(*[r[...] for r in refs[:n_read]])
        vals = vals if isinstance(vals, (tuple, list)) else (vals,)
        for o_ref, v in zip(refs[n_in:n_in + n_out], vals[:n_out]):
            o_ref[...] = v.astype(o_ref.dtype)
        if accs:
            first = pl.program_id(0) == 0
            for a_ref, v in zip(refs[n_in + n_out:], vals[n_out:]):
                tot = jnp.sum(v.astype(F32), axis=0, keepdims=True)

                @pl.when(first)
                def _(a_ref=a_ref, tot=tot):
                    a_ref[...] = tot

                @pl.when(jnp.logical_not(first))
                def _(a_ref=a_ref, tot=tot):
                    a_ref[...] += tot

    in_specs = [pl.BlockSpec((tm, w), lambda i, cb=cb: (i, cb)) for (_, cb, w) in rows]
    in_specs += [pl.BlockSpec(c.shape, lambda i: (0, 0)) for c in consts]
    in_specs += [] if after is None else [ANY]
    out_specs = [pl.BlockSpec((tm, w), lambda i: (i, 0)) for (w, _) in outs]
    out_specs += [pl.BlockSpec((1, w), lambda i: (0, 0)) for w in accs]
    out_shape = [jax.ShapeDtypeStruct((s, w), dt) for (w, dt) in outs]
    out_shape += [jax.ShapeDtypeStruct((1, w), F32) for w in accs]
    return _pcall(
        body, name=name, grid=(s // tm,), in_specs=in_specs, out_specs=out_specs, out_shape=out_shape,
        compiler_params=_params("arbitrary" if accs else "parallel"),
    )(*[r[0] for r in rows], *consts, *([] if after is None else [after]))


def _whole(a):
    return (a, 0, a.shape[1])


def _xhat(x):
    x = x.astype(F32)
    r = lax.rsqrt(jnp.mean(x * x, axis=-1, keepdims=True) + RMS_EPS)
    return x * r, r


def _rms_bwd(dy, x, g):
    xh, r = _xhat(x)
    dxh = dy.astype(F32) * g
    dx = r * (dxh - xh * jnp.mean(dxh * xh, axis=-1, keepdims=True))
    return dx, dy.astype(F32) * xh


def _sigmoid(x):
    return 1.0 / (1.0 + jnp.exp(-x))


def _rms_fwd(x, g, name, tm, after=None):
    d = x.shape[1]
    return _rowcall(lambda xb, gb: _xhat(xb)[0] * gb, [_whole(x)], [g], [(d, BF16)], tm=tm, name=name, after=after)[0]


def _silu_parts(gate):
    sg = _sigmoid(gate)
    return sg, gate * sg


def _ffn_up(n, w_gu, name):
    s, d = n.shape
    f = w_gu.shape[1] // 2
    tn = _pick(f, (1408, 1024, 512, 256, 128))
    tm = _pick(s, (1024, 512, 256, 128))
    nb = f // tn

    def body(n_ref, wg_ref, wu_ref, gu_ref, act_ref):
        nv = n_ref[...]
        gate = _dot(nv, wg_ref[...], 1, 0)
        up = _dot(nv, wu_ref[...], 1, 0)
        gu_ref[0] = gate.astype(gu_ref.dtype)
        gu_ref[1] = up.astype(gu_ref.dtype)
        act_ref[...] = (_silu_parts(gate)[1] * up).astype(act_ref.dtype)

    return _pcall(
        body, name=name, grid=(s // tm, nb),
        in_specs=[pl.BlockSpec((tm, d), lambda i, j: (i, 0)), pl.BlockSpec((d, tn), lambda i, j: (0, j)),
                  pl.BlockSpec((d, tn), lambda i, j: (0, nb + j))],
        out_specs=[pl.BlockSpec((2, tm, tn), lambda i, j: (0, i, j)), pl.BlockSpec((tm, tn), lambda i, j: (i, j))],
        out_shape=[jax.ShapeDtypeStruct((2, s, f), BF16), jax.ShapeDtypeStruct((s, f), BF16)],
        compiler_params=_params("parallel", "parallel"),
    )(n, w_gu, w_gu)


def _ffn_dgu(dhb, w_down, gu, name, after=None):
    s, d = dhb.shape
    f = w_down.shape[0]
    tn = _pick(f, (1408, 1024, 512, 256, 128))
    tm = _pick(s, (1024, 512, 256, 128))

    def body(dh_ref, w_ref, gu_ref, *rest):
        o_ref = rest[-1]
        dact = _dot(dh_ref[...], w_ref[...], 1, 1)
        gate, up = gu_ref[0].astype(F32), gu_ref[1].astype(F32)
        sg, silu = _silu_parts(gate)
        o_ref[0] = (dact * up * (sg + silu * (1.0 - sg))).astype(o_ref.dtype)
        o_ref[1] = (dact * silu).astype(o_ref.dtype)

    blk = pl.BlockSpec((2, tm, tn), lambda i, j: (0, i, j))
    return _pcall(
        body, name=name, grid=(s // tm, f // tn),
        in_specs=[pl.BlockSpec((tm, d), lambda i, j: (i, 0)), pl.BlockSpec((tn, d), lambda i, j: (j, 0)), blk]
        + ([] if after is None else [ANY]),
        out_specs=blk, out_shape=jax.ShapeDtypeStruct((2, s, f), BF16), compiler_params=_params("parallel", "parallel"),
    )(dhb, w_down, gu, *([] if after is None else [after]))


def _dgrad_norm(dy, wmat, dh, x, g, name, *, dy_halves=False, copy_scale=None, after=None):
    s, d = dh.shape
    k = wmat.shape[1]
    tk = k if k <= MM_WHOLE_K else _pick(k, (MM_WHOLE_K, 2048, 1024, 512, 256, 128))
    if dy_halves and (k // 2) % tk:
        tk = _pick(k // 2, (1408, 1024, 512, 256, 128))
    tm = _pick(s, (512, 256, 128))
    nk, per = k // tk, (k // 2) // tk if dy_halves else 0
    n_in = 5 + (after is not None)
    n_out = 2 + (copy_scale is not None)

    def body(*refs):
        dy_ref, w_ref, dh_ref, x_ref, g_ref = refs[:5]
        outs, scratch = refs[n_in:n_in + n_out], refs[n_in + n_out:]
        i, kk = pl.program_id(0), pl.program_id(1)
        part = _dot(dy_ref[...], w_ref[...], 1, 1)

        def finish(dn):
            dx, dg = _rms_bwd(dn, x_ref[...], g_ref[...])
            tot = dh_ref[...] + dx
            outs[0][...] = tot
            if copy_scale is not None:
                outs[1][...] = (copy_scale * tot).astype(outs[1].dtype)
            dg = jnp.sum(dg, axis=0, keepdims=True)

            @pl.when(i == 0)
            def _():
                outs[-1][...] = dg

            @pl.when(i > 0)
            def _():
                outs[-1][...] += dg

        if nk == 1:
            finish(part)
        else:
            acc_ref = scratch[0]

            @pl.when(kk == 0)
            def _():
                acc_ref[...] = part

            @pl.when(kk > 0)
            def _():
                acc_ref[...] += part

            @pl.when(kk == nk - 1)
            def _():
                finish(acc_ref[...])

    row = pl.BlockSpec((tm, d), lambda i, kk: (i, 0))
    dy_spec = pl.BlockSpec((None, tm, tk), lambda i, kk: (kk // per, i, kk % per)) if dy_halves else pl.BlockSpec((tm, tk), lambda i, kk: (i, kk))
    in_specs = [dy_spec, pl.BlockSpec((d, tk), lambda i, kk: (0, kk)), row, row, pl.BlockSpec((1, d), lambda i, kk: (0, 0))]
    out_specs = [row] * (n_out - 1) + [pl.BlockSpec((1, d), lambda i, kk: (0, 0))]
    out_shape = [jax.ShapeDtypeStruct((s, d), F32)] + ([] if copy_scale is None else [jax.ShapeDtypeStruct((s, d), BF16)])
    return _pcall(
        body, name=name, grid=(s // tm, nk), in_specs=in_specs + ([] if after is None else [ANY]), out_specs=out_specs,
        out_shape=out_shape + [jax.ShapeDtypeStruct((1, d), F32)], scratch_shapes=[pltpu.VMEM((tm, d), F32)] if nk > 1 else [],
        compiler_params=_params("arbitrary", "arbitrary"),
    )(dy, wmat, dh, x, g, *([] if after is None else [after]))


def _shift_down(p, k):
    if k == 0:
        return p
    rows = lax.broadcasted_iota(jnp.int32, p.shape, 0)
    return jnp.where(rows >= k, pltpu.roll(p, k, 0), 0.0)


def _shift_up(p, k):
    if k == 0:
        return p
    s = p.shape[0]
    rows = lax.broadcasted_iota(jnp.int32, p.shape, 0)
    return jnp.where(rows < s - k, pltpu.roll(p, s - k, 0), 0.0)


def _conv_fwd(proj, conv_w, d, tc, name):
    s = proj.shape[0]
    nb = d // tc

    def body(cb_ref, cc_ref, cx_ref, w_ref, y_ref):
        p = cc_ref[...].astype(F32) * cx_ref[...].astype(F32)
        w = w_ref[...]
        acc = p * w[CONV_K - 1:CONV_K, :]
        for k in range(1, CONV_K):
            acc = acc + _shift_down(p, k) * w[CONV_K - 1 - k:CONV_K - k, :]
        y_ref[...] = (cb_ref[...].astype(F32) * acc).astype(y_ref.dtype)

    col = lambda off: pl.BlockSpec((s, tc), lambda j: (0, off * nb + j))
    return _pcall(
        body, name=name, grid=(nb,), in_specs=[col(0), col(1), col(2), pl.BlockSpec((CONV_K, tc), lambda j: (0, j))],
        out_specs=pl.BlockSpec((s, tc), lambda j: (0, j)), out_shape=jax.ShapeDtypeStruct((s, d), BF16),
        compiler_params=_params("parallel"),
    )(proj, proj, proj, conv_w)


def _conv_bwd(dy, proj, conv_w, d, tc, name):
    s = proj.shape[0]
    nb = d // tc

    def body(dy_ref, cb_ref, cc_ref, cx_ref, w_ref, dcb_ref, dcc_ref, dcx_ref, dw_ref):
        cc, cx = cc_ref[...].astype(F32), cx_ref[...].astype(F32)
        p = cc * cx
        w = w_ref[...]
        dyv = dy_ref[...].astype(F32)
        shifted = [_shift_down(p, CONV_K - 1 - k) for k in range(CONV_K)]
        conv = shifted[0] * w[0:1, :]
        for k in range(1, CONV_K):
            conv = conv + shifted[k] * w[k:k + 1, :]
        dcb_ref[...] = (dyv * conv).astype(dcb_ref.dtype)
        ds = dyv * cb_ref[...].astype(F32)
        dp = ds * w[CONV_K - 1:CONV_K, :]
        for k in range(1, CONV_K):
            dp = dp + _shift_up(ds, k) * w[CONV_K - 1 - k:CONV_K - k, :]
        dcc_ref[...] = (dp * cx).astype(dcc_ref.dtype)
        dcx_ref[...] = (dp * cc).astype(dcx_ref.dtype)
        for k in range(CONV_K):
            dw_ref[k:k + 1, :] = jnp.sum(ds * shifted[k], axis=0, keepdims=True)

    col = lambda off: pl.BlockSpec((s, tc), lambda j: (0, off * nb + j))
    blk = pl.BlockSpec((s, tc), lambda j: (0, j))
    wblk = pl.BlockSpec((CONV_K, tc), lambda j: (0, j))
    act = jax.ShapeDtypeStruct((s, d), BF16)
    return _pcall(
        body, name=name, grid=(nb,), in_specs=[blk, col(0), col(1), col(2), wblk],
        out_specs=[blk, blk, blk, wblk], out_shape=[act, act, act, jax.ShapeDtypeStruct((CONV_K, d), F32)],
        compiler_params=_params("parallel"),
    )(dy, proj, proj, proj, conv_w)


def _sb_tile(q, kj, scale, carry, tri, mask):
    z = _dot(q, kj, 1, 1) * scale
    lsz = jnp.minimum(z, 0.0) - jnp.log(1.0 + jnp.exp(-jnp.abs(z)))
    l1m = lsz - z
    if mask is not None:
        l1m = jnp.where(mask, l1m, 0.0)
    l1b = l1m.astype(BF16)
    a = jnp.exp(lsz + (carry + _dot(l1b, tri, 1, 0)))
    if mask is not None:
        a = jnp.where(mask, a, 0.0)
    return lsz, l1b, a.astype(BF16)


def _add_rows(x, upd, r0):
    return x + upd if r0 == 0 else jnp.concatenate([x[:r0], x[r0:] + upd], axis=0)


def _sb_masks(tq, tk):
    row = lax.broadcasted_iota(jnp.int32, (tq, tk), 0)
    col = lax.broadcasted_iota(jnp.int32, (tq, tk), 1)
    masks = [col + dj * tk < row for dj in range(tq // tk)]
    r2 = lax.broadcasted_iota(jnp.int32, (tk, tk), 0)
    c2 = lax.broadcasted_iota(jnp.int32, (tk, tk), 1)
    return masks, (r2 > c2).astype(BF16), (r2 < c2).astype(BF16)


def _sb_fwd(proj, heads, col0, tq, tk, name):
    s = proj.shape[0]
    dh = SB_HEAD_DIM
    nq, nd, nkt = s // tq, tq // tk, s // tk
    scale = dh ** -0.5

    def body(q_ref, k_ref, v_ref, o_ref, a_ref, b_ref):
        i = pl.program_id(1)
        q = q_ref[...]
        masks, tri_right, _ = _sb_masks(tq, tk)

        def tile(j, carry, acc, mask, r0=0):
            start = pl.multiple_of(j * tk, tk)
            kj = k_ref[pl.ds(start, tk), :]
            vj = v_ref[pl.ds(start, tk), :]
            lsz, l1b, ab = _sb_tile(q[r0:], kj, scale, carry[r0:], tri_right, None if mask is None else mask[r0:])
            a_ref[j, r0:, :] = ab
            b_ref[j, r0:, :] = jnp.exp(lsz).astype(b_ref.dtype)
            if r0:
                a_ref[j, :r0, :] = jnp.zeros((r0, tk), a_ref.dtype)
                b_ref[j, :r0, :] = jnp.zeros((r0, tk), b_ref.dtype)
            return (_add_rows(carry, jnp.sum(l1b.astype(F32), axis=1, keepdims=True), r0),
                    _add_rows(acc, _dot(ab, vj, 1, 0), r0))

        state = (jnp.zeros((tq, 1), F32), jnp.zeros((tq, dh), F32))
        for dj in reversed(range(nd)):
            state = tile(i * nd + dj, *state, masks[dj], dj * tk)
        def left_block(t, st):
            for dj in reversed(range(nd)):
                st = tile((i - 1 - t) * nd + dj, st[0], st[1], None)
            return st

        state = lax.fori_loop(0, i, left_block, state)
        o_ref[...] = state[1]

    qspec = pl.BlockSpec((tq, dh), lambda h, i: (i, col0[0] + h))
    kspec = pl.BlockSpec((s, dh), lambda h, i: (0, col0[1] + h))
    vspec = pl.BlockSpec((s, dh), lambda h, i: (0, col0[2] + h))
    saved = pl.BlockSpec((None, nkt, tq, tk), lambda h, i: (h, 0, i, 0))
    saved_shape = jax.ShapeDtypeStruct((heads, nkt, s, tk), BF16)
    return _pcall(
        body, name=name, grid=(heads, nq), in_specs=[qspec, kspec, vspec],
        out_specs=[pl.BlockSpec((tq, dh), lambda h, i: (i, h)), saved, saved],
        out_shape=[jax.ShapeDtypeStruct((s, heads * dh), F32), saved_shape, saved_shape],
        compiler_params=_params("parallel", "parallel"),
    )(proj, proj, proj)


SB_BWD_HEADS = 2


def _sb_bwd(proj, o, a_all, beta_all, do, heads, col0, tq, tk, name):
    s = proj.shape[0]
    dh = SB_HEAD_DIM
    nq, nd, nkt = s // tq, tq // tk, s // tk
    scale = dh ** -0.5
    hb = SB_BWD_HEADS if heads % SB_BWD_HEADS == 0 and all(c % SB_BWD_HEADS == 0 for c in col0) else 1
    wide = hb * dh

    def body(q_ref, k_ref, v_ref, o_ref, a_ref, b_ref, do_ref, dq_ref, dk_ref, dv_ref, dk_acc, dv_acc):
        i = pl.program_id(1)

        @pl.when(i == 0)
        def _():
            dk_acc[...] = jnp.zeros_like(dk_acc)
            dv_acc[...] = jnp.zeros_like(dv_acc)

        lanes = [slice(hh * dh, (hh + 1) * dh) for hh in range(hb)]
        q = [q_ref[:, ln] for ln in lanes]
        dob = [do_ref[:, ln].astype(BF16) for ln in lanes]
        delta = [jnp.sum(dob[hh].astype(F32) * o_ref[:, lanes[hh]], axis=1, keepdims=True) for hh in range(hb)]
        masks, _, tri_left = _sb_masks(tq, tk)

        def tile(hh, j, carry_g, dq, mask):
            start = pl.multiple_of(j * tk, tk)
            kj = k_ref[pl.ds(start, tk), lanes[hh]]
            vj = v_ref[pl.ds(start, tk), lanes[hh]]
            ab = a_ref[hh, j]
            g = _dot(dob[hh], vj, 1, 1) * ab.astype(F32)
            carry_g = carry_g + jnp.sum(g, axis=1, keepdims=True)
            left = (delta[hh] - carry_g) + _dot(g.astype(BF16), tri_left, 1, 0)
            dz = g - b_ref[hh, j].astype(F32) * (g + left)
            if mask is not None:
                dz = jnp.where(mask, dz, 0.0)
            dzb = dz.astype(BF16)
            dk_acc[pl.ds(start, tk), lanes[hh]] += _dot(dzb, q[hh], 0, 0)
            dv_acc[pl.ds(start, tk), lanes[hh]] += _dot(ab, dob[hh], 0, 0)
            return carry_g, dq + _dot(dzb, kj, 1, 0)

        def block(jb, st, use_masks):
            st = list(st)
            for dj in reversed(range(nd)):
                for hh in range(hb):
                    st[hh] = tile(hh, jb * nd + dj, *st[hh], masks[dj] if use_masks else None)
            return tuple(st)

        state = block(i, tuple((jnp.zeros((tq, 1), F32), jnp.zeros((tq, dh), F32)) for _ in range(hb)), True)
        state = lax.fori_loop(0, i, lambda t, st: block(i - 1 - t, st, False), state)
        for hh in range(hb):
            dq_ref[:, lanes[hh]] = (state[hh][1] * scale).astype(dq_ref.dtype)

        @pl.when(i == nq - 1)
        def _():
            dk_ref[...] = (dk_acc[...] * scale).astype(dk_ref.dtype)
            dv_ref[...] = dv_acc[...].astype(dv_ref.dtype)

    qspec = pl.BlockSpec((tq, wide), lambda h, i: (i, col0[0] // hb + h))
    kspec = pl.BlockSpec((s, wide), lambda h, i: (0, col0[1] // hb + h))
    vspec = pl.BlockSpec((s, wide), lambda h, i: (0, col0[2] // hb + h))
    blk = pl.BlockSpec((tq, wide), lambda h, i: (i, h))
    full = pl.BlockSpec((s, wide), lambda h, i: (0, h))
    saved = pl.BlockSpec((hb, nkt, tq, tk), lambda h, i: (h, 0, i, 0))
    act = jax.ShapeDtypeStruct((s, heads * dh), BF16)
    return _pcall(
        body, name=name, grid=(heads // hb, nq), in_specs=[qspec, kspec, vspec, blk, saved, saved, blk],
        out_specs=[blk, full, full], out_shape=[act, act, act],
        scratch_shapes=[pltpu.VMEM((s, wide), F32), pltpu.VMEM((s, wide), F32)],
        compiler_params=_params("parallel", "arbitrary"),
    )(proj, proj, proj, o, a_all, beta_all, do)


def _xattn_probs(q, k, scale):
    sc = _dot(q, k, 1, 1) * scale
    e = jnp.exp(sc - jnp.max(sc, axis=1, keepdims=True))
    return e / jnp.sum(e, axis=1, keepdims=True)


def _xattn_fwd(qc, kv, tq, name):
    s, d = qc.shape
    m = kv.shape[0]
    dh = d // X_HEADS
    scale = dh ** -0.5

    def body(q_ref, k_ref, v_ref, o_ref):
        p = _xattn_probs(q_ref[...], k_ref[...], scale)
        o_ref[...] = _dot(p.astype(BF16), v_ref[...], 1, 0).astype(o_ref.dtype)

    blk = pl.BlockSpec((tq, dh), lambda h, i: (i, h))
    return _pcall(
        body, name=name, grid=(X_HEADS, s // tq),
        in_specs=[blk, pl.BlockSpec((m, dh), lambda h, i: (0, h)), pl.BlockSpec((m, dh), lambda h, i: (0, X_HEADS + h))],
        out_specs=blk, out_shape=jax.ShapeDtypeStruct((s, d), BF16), compiler_params=_params("parallel", "parallel"),
    )(qc, kv, kv)


def _xattn_bwd(qc, kv, do, tq, name):
    s, d = qc.shape
    m = kv.shape[0]
    dh = d // X_HEADS
    scale = dh ** -0.5
    nq = s // tq

    def body(q_ref, k_ref, v_ref, do_ref, dq_ref, dk_ref, dv_ref, dk_acc, dv_acc):
        i = pl.program_id(1)
        q, k, v = q_ref[...], k_ref[...], v_ref[...]
        dob = do_ref[...].astype(BF16)
        p = _xattn_probs(q, k, scale)
        pb = p.astype(BF16)
        dp = _dot(dob, v, 1, 1)
        ds = pb.astype(F32) * (dp - jnp.sum(dp * pb.astype(F32), axis=1, keepdims=True))
        dsb = (ds * scale).astype(BF16)
        dq_ref[...] = _dot(dsb, k, 1, 0).astype(dq_ref.dtype)
        dk_part = _dot(dsb, q, 0, 0)
        dv_part = _dot(pb, dob, 0, 0)

        @pl.when(i == 0)
        def _():
            dk_acc[...] = dk_part
            dv_acc[...] = dv_part

        @pl.when(i > 0)
        def _():
            dk_acc[...] += dk_part
            dv_acc[...] += dv_part

        @pl.when(i == nq - 1)
        def _():
            dk_ref[...] = dk_acc[...].astype(dk_ref.dtype)
            dv_ref[...] = dv_acc[...].astype(dv_ref.dtype)

    blk = pl.BlockSpec((tq, dh), lambda h, i: (i, h))
    kblk = pl.BlockSpec((m, dh), lambda h, i: (0, h))
    return _pcall(
        body, name=name, grid=(X_HEADS, nq),
        in_specs=[blk, kblk, pl.BlockSpec((m, dh), lambda h, i: (0, X_HEADS + h)), blk],
        out_specs=[blk, kblk, kblk],
        out_shape=[jax.ShapeDtypeStruct((s, d), BF16), jax.ShapeDtypeStruct((m, d), BF16), jax.ShapeDtypeStruct((m, d), BF16)],
        scratch_shapes=[pltpu.VMEM((m, dh), F32), pltpu.VMEM((m, dh), F32)],
        compiler_params=_params("parallel", "arbitrary"),
    )(qc, kv, kv, do)


def _mix_merge(y_conv, y_sb, w_conv_out, w_attn_out, proj, gate_blocks, b_conv, b_sb, name):
    s, d = y_conv.shape
    tm, tn = _pick(s, (1024, 512, 256, 128)), _pick(d, (512, 256, 128))
    nb = d // tn

    def body(yc_ref, ys_ref, wc_ref, ws_ref, gc_ref, gs_ref, bc_ref, bs_ref, ac_ref, as_ref, m_ref):
        ac = _dot(yc_ref[...].astype(BF16), wc_ref[...], 1, 0)
        asb = _dot(ys_ref[...].astype(BF16), ws_ref[...], 1, 0)
        gc = _sigmoid(gc_ref[...].astype(F32) + bc_ref[...])
        gs = _sigmoid(gs_ref[...].astype(F32) + bs_ref[...])
        ac_ref[...] = ac.astype(ac_ref.dtype)
        as_ref[...] = asb.astype(as_ref.dtype)
        m_ref[...] = (gc * ac + gs * asb).astype(m_ref.dtype)

    rows = pl.BlockSpec((tm, d), lambda i, j: (i, 0))
    wcol = pl.BlockSpec((d, tn), lambda i, j: (0, j))
    bias = pl.BlockSpec((1, tn), lambda i, j: (0, j))
    gate = lambda blk: pl.BlockSpec((tm, tn), lambda i, j: (i, blk * nb + j))
    out = pl.BlockSpec((tm, tn), lambda i, j: (i, j))
    act = jax.ShapeDtypeStruct((s, d), BF16)
    return _pcall(
        body, name=name, grid=(s // tm, nb),
        in_specs=[rows, rows, wcol, wcol, gate(gate_blocks[0]), gate(gate_blocks[1]), bias, bias],
        out_specs=[out, out, out], out_shape=[act, act, act], compiler_params=_params("parallel", "parallel"),
    )(y_conv, y_sb, w_conv_out, w_attn_out, proj, proj, b_conv, b_sb)


def _local_step(x, mem, tgt, w, fetch=None, prefetch=None, emit=None, tick=None, after=None):
    fetch = fetch or (lambda name, after: {})
    prefetch = prefetch or (lambda name, after: None)
    emit = emit or (lambda group, g: None)
    tick = tick or (lambda group, after: None)
    w = dict(w)
    s, d = x.shape
    heads = d // SB_HEAD_DIM
    tm = _pick(s, (1024, 512, 256, 128))
    tq = _pick(s, (1024, 512, 256, 128))
    sb_tq, sb_tk = _pick(s, (512, 256, 128)), _pick(s, (256, 128))
    tc = _pick(d, (256, 128))
    g = {}

    def wt(name, after):
        if name not in w:
            w.update(fetch(name, after))
        return w[name]

    def ffn_fwd(h, gname, wgu, wdown, tag, after=None):
        n = _rms_fwd(h, w[gname], tag + "_norm", tm, after=after)
        gu, act = _ffn_up(n, wt(wgu, n), tag + "_gu")
        prefetch(wdown, gu)
        return n, gu, act, _mm(act, wt(wdown, act), name=tag + "_down", out_dtype=F32, res=h, alpha=0.5)

    def ffn_bwd(dh, dhb, h, saved, gname, wgu, wdown, tag, copy_scale=None, after=None):
        n, gu, act = saved
        g[wdown] = _mm(act, dhb, ta=True, name=tag + "_dwdown", after=after)
        dgu = _ffn_dgu(dhb, w[wdown], gu, tag + "_dgu", after=emit(tag + "_down", g))
        g[wgu] = _mm(n, dgu, ta=True, b_halves=True, name=tag + "_dwgu", after=tick(tag + "_down", dgu))
        *dh_in, g[gname] = _dgrad_norm(dgu, w[wgu], dh, h, w[gname], tag + "_dn", dy_halves=True, copy_scale=copy_scale,
                                       after=emit(tag, g))
        return dh_in, tick(tag, dh_in[0])

    n1, gu1, act1, h1 = ffn_fwd(x, "g_ffn1", "w_ffn1_gu", "w_ffn1_down", "ffn1", after)
    prefetch("w_in", h1)
    u = _rms_fwd(h1, w["g_mix"], "mix_norm", tm)
    proj = _mm(u, wt("w_in", u), name="mix_in")
    prefetch("w_conv_out", proj)
    nd = d // SB_HEAD_DIM
    y_conv = _conv_fwd(proj, w["conv_w"], d, tc, "conv_fwd")
    sb_cols = (3 * nd, 4 * nd, 5 * nd)
    y_sb, sb_a, sb_beta = _sb_fwd(proj, heads, sb_cols, _pick(s, (2 * sb_tq, sb_tq)), sb_tk, "sb_fwd")
    prefetch("w_cq", y_sb)
    b_conv, b_sb = w["b_gate"][:, :d], w["b_gate"][:, d:]
    a_conv, a_sb, merged = _mix_merge(y_conv, y_sb, wt("w_conv_out", y_conv), wt("w_attn_out", y_sb), proj, (6, 7), b_conv, b_sb,
                                      "mix_merge")
    prefetch("w_ffn2_gu", merged)
    h2 = _mm(merged, wt("w_o", merged), name="mix_out", out_dtype=F32, res=h1)
    hn = _rms_fwd(h2, w["g_cross"], "cross_norm", tm)
    mn = _rms_fwd(mem, w["g_mem"], "mem_norm", _pick(mem.shape[0], (256, 128)))
    qc = _mm(hn, wt("w_cq", hn), name="cross_q")
    kv = _mm(mn, wt("w_ckv", mn), name="cross_kv")
    oc = _xattn_fwd(qc, kv, tq, "xattn_fwd")
    h3 = _mm(oc, wt("w_co", oc), name="cross_out", out_dtype=F32, res=h2)
    n2, gu2, act2, h4 = ffn_fwd(h3, "g_ffn2", "w_ffn2_gu", "w_ffn2_down", "ffn2")

    def head(hb, tb, gb):
        xh, r = _xhat(hb)
        err = xh * gb - tb
        dy = err * (1.0 / d)
        dxh = dy * gb
        dx = r * (dxh - xh * jnp.mean(dxh * xh, axis=-1, keepdims=True))
        row_loss = 0.5 * jnp.mean(err * err, axis=-1, keepdims=True)
        return dx, 0.5 * dx, dy * xh, jnp.broadcast_to(row_loss, (row_loss.shape[0], LANES))

    dh4, dh4b, g["g_final"], loss_lanes = _rowcall(head, [_whole(h4), _whole(tgt)], [w["g_final"]], [(d, F32), (d, BF16)],
                                                   [d, LANES], tm=tm, name="loss_head")

    (dh3, dh3b), tok = ffn_bwd(dh4, dh4b, h3, (n2, gu2, act2), "g_ffn2", "w_ffn2_gu", "w_ffn2_down", "ffn2", copy_scale=1.0)
    g["w_co"] = _mm(oc, dh3b, ta=True, name="cross_dwco", after=tok)
    doc = _mm(dh3b, w["w_co"], tb=True, name="cross_doc")
    dqc, dk, dv = _xattn_bwd(qc, kv, doc, tq, "xattn_bwd")
    dkv = jnp.concatenate([dk, dv], axis=1)
    g["w_cq"] = _mm(hn, dqc, ta=True, name="cross_dwcq")
    g["w_ckv"] = _mm(mn, dkv, ta=True, name="cross_dwckv")
    dmn = _mm(dkv, w["w_ckv"], tb=True, name="cross_dmn", out_dtype=F32)
    g["g_mem"] = _rowcall(lambda dy, xb: dy * _xhat(xb)[0], [_whole(dmn), _whole(mem)], [], [], [d],
                          tm=_pick(mem.shape[0], (256, 128)), name="mem_dnorm")[0]
    dh2, dh2b, g["g_cross"] = _dgrad_norm(dqc, w["w_cq"], dh3, h2, w["g_cross"], "cross_dhn", copy_scale=1.0, after=emit("cross", g))

    g["w_o"] = _mm(merged, dh2b, ta=True, name="mix_dwo", after=tick("cross", dh2))
    dmerged = _mm(dh2b, w["w_o"], tb=True, name="mix_dmerged")

    def merge_bwd(dm, ac, asb, gcp, gsp, bc, bs):
        dm, ac, asb = dm.astype(F32), ac.astype(F32), asb.astype(F32)
        gc = _sigmoid(gcp.astype(F32) + bc)
        gs = _sigmoid(gsp.astype(F32) + bs)
        dgc = dm * ac * gc * (1.0 - gc)
        dgs = dm * asb * gs * (1.0 - gs)
        return dm * gc, dm * gs, dgc, dgs, dgc, dgs

    da_conv, da_sb, dgc, dgs, db_conv, db_sb = _rowcall(
        merge_bwd, [_whole(dmerged), _whole(a_conv), _whole(a_sb), (proj, 6, d), (proj, 7, d)], [b_conv, b_sb],
        [(d, BF16)] * 4, [d, d], tm=tm, name="merge_bwd")
    g["b_gate"] = jnp.concatenate([db_conv, db_sb], axis=1)
    g["w_conv_out"] = _mm(y_conv, da_conv, ta=True, name="conv_dwout")
    g["w_attn_out"] = _mm(y_sb, da_sb, ta=True, name="attn_dwout")
    dy_conv = _mm(da_conv, w["w_conv_out"], tb=True, name="conv_dy")
    dy_sb = _mm(da_sb, w["w_attn_out"], tb=True, name="attn_dy")
    dcb, dcc, dcx, g["conv_w"] = _conv_bwd(dy_conv, proj, w["conv_w"], d, tc, "conv_bwd")
    dq, dk_sb, dv_sb = _sb_bwd(proj, y_sb, sb_a, sb_beta, dy_sb, heads, sb_cols, sb_tq, sb_tk, "sb_bwd")
    dproj = jnp.concatenate([dcb, dcc, dcx, dq, dk_sb, dv_sb, dgc, dgs], axis=1)
    g["w_in"] = _mm(u, dproj, ta=True, name="mix_dwin")
    dh1, dh1b, g["g_mix"] = _dgrad_norm(dproj, w["w_in"], dh2, h1, w["g_mix"], "mix_du", copy_scale=0.5, after=emit("mix", g))
    (dx,), tok = ffn_bwd(dh1, dh1b, x, (n1, gu1, act1), "g_ffn1", "w_ffn1_gu", "w_ffn1_down", "ffn1", after=tick("mix", dh1))
    return loss_lanes, dx, g, tok


MATS = (("w_ffn1_gu", "col"), ("w_ffn1_down", "row"), ("w_in", "col"), ("w_conv_out", "row"), ("w_attn_out", "row"),
        ("w_o", "row"), ("w_cq", "row"), ("w_ckv", "col"), ("w_co", "row"), ("w_ffn2_gu", "col"), ("w_ffn2_down", "row"))
VECS = ("g_ffn1", "g_mix", "g_cross", "g_mem", "g_ffn2", "g_final")
WEIGHTS = ("g_ffn1", "w_ffn1_gu", "w_ffn1_down", "g_mix", "w_in", "b_gate", "conv_w", "w_conv_out", "w_attn_out", "w_o",
           "g_cross", "g_mem", "w_cq", "w_ckv", "w_co", "g_ffn2", "w_ffn2_gu", "w_ffn2_down", "g_final")
CONV_ROWS = 16


def _full_shape(kind, r, c):
    return (r, N_CHIPS * c) if kind == "col" else (N_CHIPS * r, c)


def _piece(ref, kind, r, c, chip, half):
    hr = r // 2
    if kind == "col":
        return ref.at[pl.ds(pl.multiple_of(half * hr, math.gcd(hr, 16)), hr), pl.ds(pl.multiple_of(chip * c, LANES), c)]
    return ref.at[pl.ds(pl.multiple_of(chip * r + half * hr, math.gcd(hr, 16)), hr), :]


def _shard_of(ref, kind, r, c, chip):
    if kind == "col":
        return ref.at[:, pl.ds(pl.multiple_of(chip * c, LANES), c)]
    return ref.at[pl.ds(pl.multiple_of(chip * r, 16), r), :]


def _place():
    x, y, c = lax.axis_index("x"), lax.axis_index("y"), lax.axis_index("c")
    others = [(1 - x, y), (x, 1 - y), (1 - x, 1 - y)]
    return x, y, c, 2 * x + y, others


def _remote(src, dst, send_sem, recv_sem, to):
    return pltpu.make_async_remote_copy(src_ref=src, dst_ref=dst, send_sem=send_sem, recv_sem=recv_sem,
                                        device_id=to, device_id_type=MESH)


HBM = pl.BlockSpec(memory_space=pltpu.HBM)
SEM = pl.BlockSpec(memory_space=pltpu.SEMAPHORE)
EFFECT = pltpu.SideEffectType.DATAFLOW_SIDE_EFFECTING
TOKEN = (8, LANES)


def _split_start(name, plan, n_copies, srcs, lands, after=None):
    ns, nl = len(srcs), len(lands)
    n_in = ns + nl + (after is not None)

    def body(*refs):
        outs = refs[n_in:]
        sends, _ = plan(refs[:ns], refs[ns:ns + nl], outs[0], outs[1])
        for cp in sends:
            cp.start()
        outs[-1][...] = jnp.zeros(TOKEN, F32)

    held = [pltpu.HBM(a.shape, a.dtype) for a in (*srcs, *lands)]
    dma = pltpu.SemaphoreType.DMA((n_copies,))
    ins = [pltpu.with_memory_space_constraint(a, pltpu.HBM) for a in (*srcs, *lands)]
    outs = _pcall(
        body, name=name, in_specs=[HBM] * (ns + nl) + ([] if after is None else [ANY]),
        out_specs=(SEM, SEM, *[HBM] * (ns + nl), pl.BlockSpec(memory_space=pltpu.VMEM)),
        out_shape=(dma, dma, *held, jax.ShapeDtypeStruct(TOKEN, F32)),
        input_output_aliases={i: 2 + i for i in range(ns + nl)},
        compiler_params=pltpu.CompilerParams(has_side_effects=EFFECT),
    )(*ins, *([] if after is None else [after]))
    return outs[0], outs[1], list(outs[2:2 + ns]), list(outs[2 + ns:2 + ns + nl]), outs[-1]


def _split_wait(name, plan, send_sems, recv_sems, srcs, lands, after):
    ns, nl = len(srcs), len(lands)

    def body(*refs):
        sends, recvs = plan(refs[:ns], refs[ns:ns + nl], refs[ns + nl], refs[ns + nl + 1])
        for cp in sends:
            cp.wait_send()
        for cp in recvs:
            cp.wait_recv()

    outs = _pcall(
        body, name=name, in_specs=[HBM] * (ns + nl) + [SEM, SEM, ANY], out_specs=[HBM] * (ns + nl),
        out_shape=[pltpu.HBM(a.shape, a.dtype) for a in (*srcs, *lands)],
        input_output_aliases={i: i for i in range(ns + nl)},
        compiler_params=pltpu.CompilerParams(has_side_effects=EFFECT),
    )(*srcs, *lands, send_sems, recv_sems, after)
    return list(outs[:ns]), list(outs[ns:])


def _gather_plan(dims):
    def plan(shard_refs, full_refs, ss, rs):
        x, y, c, me, others = _place()
        sends, recvs = [], []
        for wi, (kind, r, cw) in enumerate(dims):
            half = shard_refs[wi].at[pl.ds(pl.multiple_of(c * (r // 2), math.gcd(r // 2, 16)), r // 2), :]
            for k, (ox, oy) in enumerate(others):
                sem = 4 * wi + k
                sends.append(_remote(half, _piece(full_refs[wi], kind, r, cw, me, c), ss.at[sem], rs.at[sem], (ox, oy, c)))
                recvs.append(_remote(half, _piece(full_refs[wi], kind, r, cw, 2 * ox + oy, c), ss.at[sem], rs.at[sem], (x, y, c)))
            sem = 4 * wi + 3
            own = _remote(shard_refs[wi], _shard_of(full_refs[wi], kind, r, cw, me), ss.at[sem], rs.at[sem], (x, y, 1 - c))
            sends.append(own)
            recvs.append(own)
        return sends, recvs

    return plan


def _forward_plan(dims):
    def plan(_, full_refs, ss, rs):
        x, y, c, _, others = _place()
        sends, recvs = [], []
        for wi, (kind, r, cw) in enumerate(dims):
            for k, (ox, oy) in enumerate(others):
                sem = 3 * wi + k
                mine = _piece(full_refs[wi], kind, r, cw, 2 * ox + oy, c)
                theirs = _piece(full_refs[wi], kind, r, cw, 2 * ox + oy, 1 - c)
                sends.append(_remote(mine, mine, ss.at[sem], rs.at[sem], (x, y, 1 - c)))
                recvs.append(_remote(theirs, theirs, ss.at[sem], rs.at[sem], (x, y, 1 - c)))
        return sends, recvs

    return plan


def _rs_cores_plan(dims):
    def plan(g_refs, land_refs, ss, rs):
        x, y, c, _, _ = _place()
        sends, recvs = [], []
        for wi, dm in enumerate(dims):
            for chip in range(N_CHIPS):
                sem = N_CHIPS * wi + chip
                sends.append(_remote(_piece(g_refs[wi], *dm, chip, 1 - c), land_refs[wi].at[chip], ss.at[sem], rs.at[sem], (x, y, 1 - c)))
                recvs.append(_remote(_piece(g_refs[wi], *dm, chip, c), land_refs[wi].at[chip], ss.at[sem], rs.at[sem], (x, y, 1 - c)))
        return sends, recvs

    return plan


def _share_plan(nw):
    def plan(_, buf_refs, ss, rs):
        x, y, c, _, _ = _place()
        sends = [_remote(buf_refs[wi].at[c], buf_refs[wi].at[c], ss.at[wi], rs.at[wi], (x, y, 1 - c)) for wi in range(nw)]
        recvs = [_remote(buf_refs[wi].at[1 - c], buf_refs[wi].at[1 - c], ss.at[wi], rs.at[wi], (x, y, 1 - c)) for wi in range(nw)]
        return sends, recvs

    return plan


def _small_plan():
    def plan(_, buf_refs, ss, rs):
        x, y, c = lax.axis_index("x"), lax.axis_index("y"), lax.axis_index("c")
        buf = buf_refs[0]
        sends, recvs = [], []
        for rel in range(1, N_DEV):
            peer = (x ^ (rel >> 2 & 1), y ^ (rel >> 1 & 1), c ^ (rel & 1))
            sends.append(_remote(buf.at[0], buf.at[rel], ss.at[rel - 1], rs.at[rel - 1], peer))
            recvs.append(_remote(buf.at[0], buf.at[rel], ss.at[rel - 1], rs.at[rel - 1], peer))
        return sends, recvs

    return plan


def _sum_small(buf, me, name):
    _, rows, n = buf.shape

    def body(me_ref, b_ref, o_ref):
        tot = b_ref[me_ref[0]]
        for dev in range(1, N_DEV):
            tot = tot + b_ref[dev ^ me_ref[0]]
        o_ref[...] = tot

    return _pcall(
        body, name=name, out_shape=jax.ShapeDtypeStruct((rows, n), F32),
        grid_spec=pltpu.PrefetchScalarGridSpec(
            num_scalar_prefetch=1, grid=(1,), in_specs=[pl.BlockSpec((N_DEV, rows, n), lambda i, m: (0, 0, 0))],
            out_specs=pl.BlockSpec((rows, n), lambda i, m: (0, 0))),
    )(me, buf)


def _rs_chips_plan(nw):
    def plan(p_refs, land_refs, ss, rs):
        x, y, c, me, others = _place()
        sends, recvs = [], []
        for wi in range(nw):
            for k, (ox, oy) in enumerate(others):
                sem = 3 * wi + k
                sends.append(_remote(p_refs[wi].at[2 * ox + oy], land_refs[wi].at[k], ss.at[sem], rs.at[sem], (ox, oy, c)))
                recvs.append(_remote(p_refs[wi].at[me], land_refs[wi].at[k], ss.at[sem], rs.at[sem], (x, y, c)))
        return sends, recvs

    return plan


SUM_BLOCK_BYTES = 4 << 20


def _rows_per_block(n, c, limit_bytes=2 << 20):
    best = None
    for tm in range(16, n + 1, 16):
        if n % tm == 0 and tm * c * 4 <= limit_bytes:
            best = tm
    return best or n


def _sum_cores(grad, got, kind, place, name):
    _, hr, cw = got.shape
    tm = _rows_per_block(hr, cw, SUM_BLOCK_BYTES)
    nb = hr // tm

    def body(place_ref, g_ref, t_ref, o_ref):
        o_ref[...] = (g_ref[...].astype(F32) + t_ref[...].astype(F32)).astype(o_ref.dtype)

    if kind == "col":
        g_spec = pl.BlockSpec((tm, cw), lambda j, i, pr: (pr[0] * nb + i, j))
    else:
        g_spec = pl.BlockSpec((tm, cw), lambda j, i, pr: ((2 * j + pr[0]) * nb + i, 0))
    blk = pl.BlockSpec((None, tm, cw), lambda j, i, pr: (j, i, 0))
    return _pcall(
        body, name=name, out_shape=jax.ShapeDtypeStruct(got.shape, BF16),
        grid_spec=pltpu.PrefetchScalarGridSpec(num_scalar_prefetch=1, grid=(N_CHIPS, nb), in_specs=[g_spec, blk], out_specs=blk),
        compiler_params=_params("parallel", "parallel"),
    )(place, grad, got)


def _sum_chips(parts, got, place, name):
    _, n, cw = got.shape
    tm = _rows_per_block(n, cw, SUM_BLOCK_BYTES)

    def body(place_ref, p_ref, g_ref, o_ref):
        tot = p_ref[...].astype(F32)
        for k in range(3):
            tot = tot + g_ref[k].astype(F32)
        o_ref[...] = tot

    return _pcall(
        body, name=name, out_shape=jax.ShapeDtypeStruct((2, n, cw), F32),
        grid_spec=pltpu.PrefetchScalarGridSpec(
            num_scalar_prefetch=1, grid=(n // tm,),
            in_specs=[pl.BlockSpec((None, tm, cw), lambda i, pr: (pr[1], i, 0)), pl.BlockSpec((3, tm, cw), lambda i, pr: (0, i, 0))],
            out_specs=pl.BlockSpec((None, tm, cw), lambda i, pr: (pr[0], i, 0))),
        compiler_params=_params("parallel"),
    )(place, parts, got)


def _adamw(g, w, m, v, name):
    n, c = g.shape
    c1 = 1.0 - ADAM_B1 ** ADAM_STEP
    c2 = 1.0 - ADAM_B2 ** ADAM_STEP

    def fn(gb, wb, mb, vb):
        m_new = ADAM_B1 * mb + (1.0 - ADAM_B1) * gb
        v_new = ADAM_B2 * vb + (1.0 - ADAM_B2) * (gb * gb)
        delta = -ADAM_LR * ((m_new / c1) / (jnp.sqrt(v_new / c2) + ADAM_EPS) + ADAM_WD * wb)
        return gb, delta, m_new, v_new

    tm = _rows_per_block(n, c) if n % 16 == 0 else n
    return _rowcall(fn, [_whole(g), _whole(w), _whole(m), _whole(v)], [], [(c, F32)] * 4, tm=tm, name=name)


PACK_ROWS = 16


def _pack_rows(parts, width, name, after=None):
    assert sum(p.shape[0] for p in parts) <= PACK_ROWS

    def body(*refs):
        out_ref = refs[-1]
        out_ref[...] = jnp.zeros_like(out_ref)
        at = 0
        for r in refs[:len(parts)]:
            k, n = r.shape
            if n == width:
                out_ref[at:at + k, :] = r[...]
            else:
                out_ref[at:at + k, :] = jnp.broadcast_to(r[:, :1], (k, width))
            at += k

    vm = pl.BlockSpec(memory_space=pltpu.VMEM)
    return _pcall(body, name=name, in_specs=[vm] * len(parts) + ([] if after is None else [ANY]), out_specs=vm,
                  out_shape=jax.ShapeDtypeStruct((PACK_ROWS, width), F32))(*parts, *([] if after is None else [after]))


def _cast_shard(wm, name, after):
    n, c = wm.shape
    return _rowcall(lambda v: v, [_whole(wm)], [], [(c, BF16)], tm=_rows_per_block(n, c), name=name, after=after)[0]


GATHER_GROUPS = (
    ("w_ffn1_gu", "conv_w"), ("w_ffn1_down",), ("w_in",), ("w_conv_out", "w_attn_out", "w_o"), ("w_cq", "w_ckv", "w_co"),
    ("w_ffn2_gu", "w_ffn2_down"),
)
REDUCE_GROUPS = {
    "ffn2": ("w_ffn2_down", "w_ffn2_gu"),
    "cross": ("w_co", "w_cq", "w_ckv"),
    "mix": ("w_o", "w_conv_out", "w_attn_out", "w_in"),
    "ffn1_down": ("w_ffn1_down",),
    "ffn1": ("w_ffn1_gu",),
}
TAIL_STAGES = (("ffn2", "cross"), ("mix",), ("ffn1_down", "ffn1"))
KIND = dict(MATS)


def _step(x, mem, tgt, wts, m_in, v_in):
    d = x.shape[-1]
    cc = wts["conv_w"].shape[1]
    place = jnp.stack([lax.axis_index("c"), 2 * lax.axis_index("x") + lax.axis_index("y")]).astype(jnp.int32)
    dims = {n: (kind, *wts[n].shape) for n, kind in MATS}
    dims["conv_w"] = ("col", CONV_ROWS, cc)

    w = {n: wts[n].reshape(1, -1) for n in VECS + ("b_gate",)}
    flying, token = {}, None
    for names in GATHER_GROUPS:
        gd = [dims[n] for n in names]
        shards = [jnp.pad(wts[n], ((0, CONV_ROWS - CONV_K), (0, 0))) if n == "conv_w" else _cast_shard(wts[n], "cast_" + n, token)
                  for n in names]
        lands = [lax.empty(_full_shape(*dm), sh.dtype) for dm, sh in zip(gd, shards)]
        plan = _gather_plan(gd)
        ss, rs, srcs, lands, token = _split_start("gather_start_" + names[0], plan, 4 * len(names), shards, lands, token)
        flying.update({n: (names, plan, ss, rs, srcs, lands, gd) for n in names})

    passing = {}

    def prefetch(name, after):
        if name not in passing:
            names, plan, ss, rs, srcs, lands, gd = flying[name]
            _, lands = _split_wait("gather_wait_" + names[0], plan, ss, rs, srcs, lands, after)
            plan = _forward_plan(gd)
            ss, rs, _, lands, _ = _split_start("forward_start_" + names[0], plan, 3 * len(names), [], lands)
            passing.update({n: (names, plan, ss, rs, lands) for n in names})

    def fetch(name, after):
        prefetch(name, after)
        names, plan, ss, rs, lands = passing[name]
        _, lands = _split_wait("forward_wait_" + names[0], plan, ss, rs, [], lands, after)
        return {n: (land[:CONV_K] if n == "conv_w" else land) for n, land in zip(names, lands)}

    swapping, sent = {}, {}

    def emit(tag, g):
        if tag not in REDUCE_GROUPS:
            return None
        names = REDUCE_GROUPS[tag]
        gd = [dims[n] for n in names]
        lands = [lax.empty((N_CHIPS, r // 2, cw), BF16) for (_, r, cw) in gd]
        plan = _rs_cores_plan(gd)
        ss, rs, srcs, lands, tok = _split_start("rs_cores_start_" + tag, plan, N_CHIPS * len(names), [g[n] for n in names], lands)
        swapping[tag] = (plan, ss, rs, srcs, lands)
        return tok

    def tick(tag, after):
        if tag not in REDUCE_GROUPS:
            return None
        names = REDUCE_GROUPS[tag]
        plan, ss, rs, srcs, lands = swapping[tag]
        mine, got = _split_wait("rs_cores_wait_" + tag, plan, ss, rs, srcs, lands, after)
        parts = [_sum_cores(gm, t, KIND[n], place, "sum_cores_" + n) for n, gm, t in zip(names, mine, got)]
        lands = [lax.empty((3, *p.shape[1:]), BF16) for p in parts]
        plan = _rs_chips_plan(len(names))
        ss, rs, srcs, lands, tok = _split_start("rs_chips_start_" + tag, plan, 3 * len(names), parts, lands)
        sent[tag] = (plan, ss, rs, srcs, lands)
        return tok

    loss_lanes, dx, g, last = _local_step(x[0], mem[0], tgt[0], w, fetch, prefetch, emit, tick, token)

    rows = [g[n] for n in VECS] + [g["b_gate"][:, :d], g["b_gate"][:, d:], g["conv_w"], loss_lanes]
    packed = _pack_rows(rows, d, "pack_small", after=last)
    small = jnp.concatenate([packed[None], jnp.zeros((N_DEV - 1, *packed.shape), F32)], axis=0)
    small_plan = _small_plan()
    small_ss, small_rs, _, small, after = _split_start("small_start", small_plan, N_DEV - 1, [], [small])

    grads, out = {}, {}

    def update(n):
        shape = wts[n].shape
        as2d = (lambda a: a.reshape(1, -1)) if len(shape) == 1 else (lambda a: a)
        return [r.reshape(shape) for r in _adamw(grads[n], as2d(wts[n]), as2d(m_in[n]), as2d(v_in[n]), "adamw_" + n)]

    def finish(sharing, after):
        tag, names, plan, ss, rs, halves = sharing
        _, both = _split_wait("share_wait_" + tag, plan, ss, rs, [], halves, after)
        for n, b in zip(names, both):
            grads[n] = b.reshape(-1, b.shape[-1])
            out[n] = update(n)
        return out[names[-1]][1]

    sharing = None
    for stage in TAIL_STAGES:
        names, halves = [], []
        for tag in stage:
            plan, ss, rs, srcs, lands = sent[tag]
            parts, landed = _split_wait("rs_chips_wait_" + tag, plan, ss, rs, srcs, lands, after)
            halves += [_sum_chips(p, t, place, "sum_chips_" + n) for n, p, t in zip(REDUCE_GROUPS[tag], parts, landed)]
            names += REDUCE_GROUPS[tag]
        plan = _share_plan(len(names))
        ss, rs, _, halves, after = _split_start("share_start_" + stage[0], plan, len(names), [], halves)
        if sharing is not None:
            after = finish(sharing, after)
        sharing = (stage[0], names, plan, ss, rs, halves)
    after = finish(sharing, after)

    _, small = _split_wait("small_wait", small_plan, small_ss, small_rs, [], small, after)
    me = (4 * lax.axis_index("x") + 2 * lax.axis_index("y") + lax.axis_index("c")).astype(jnp.int32).reshape(1)
    red = _sum_small(small[0], me, "sum_small")
    grads.update({n: red[i:i + 1] for i, n in enumerate(VECS)})
    nv = len(VECS)
    grads["b_gate"] = jnp.concatenate([red[nv:nv + 1], red[nv + 1:nv + 2]], axis=1)
    chip = 2 * lax.axis_index("x") + lax.axis_index("y")
    grads["conv_w"] = lax.dynamic_slice_in_dim(red[nv + 2:nv + 2 + CONV_K], chip * cc, cc, axis=1)
    loss = red[nv + 2 + CONV_K, 0]
    out.update({n: update(n) for n in WEIGHTS if n not in KIND})
    return (loss, dx[None], *[out[n][0] for n in WEIGHTS], *[out[n][1] for n in WEIGHTS],
            *[out[n][2] for n in WEIGHTS], *[out[n][3] for n in WEIGHTS])


def kernel(x, mem, g_ffn1, w_ffn1_gu, w_ffn1_down, g_mix, w_in, b_gate, conv_w, w_conv_out, w_attn_out, w_o, g_cross, g_mem, w_cq, w_ckv, w_co, g_ffn2, w_ffn2_gu, w_ffn2_down, g_final, loss_target, m_g_ffn1, m_w_ffn1_gu, m_w_ffn1_down, m_g_mix, m_w_in, m_b_gate, m_conv_w, m_w_conv_out, m_w_attn_out, m_w_o, m_g_cross, m_g_mem, m_w_cq, m_w_ckv, m_w_co, m_g_ffn2, m_w_ffn2_gu, m_w_ffn2_down, m_g_final, v_g_ffn1, v_w_ffn1_gu, v_w_ffn1_down, v_g_mix, v_w_in, v_b_gate, v_conv_w, v_w_conv_out, v_w_attn_out, v_w_o, v_g_cross, v_g_mem, v_w_cq, v_w_ckv, v_w_co, v_g_ffn2, v_w_ffn2_gu, v_w_ffn2_down, v_g_final):
    given = dict(locals())
    wts = {n: given[n] for n in WEIGHTS}
    m_in = {n: given["m_" + n] for n in WEIGHTS}
    v_in = {n: given["v_" + n] for n in WEIGHTS}
    return _step(x, mem, loss_target, wts, m_in, v_in)
```

```python
import math

import jax
import jax.numpy as jnp
from jax import lax
from jax.experimental import pallas as pl
from jax.experimental.pallas import tpu as pltpu

F32 = jnp.float32
BF16 = jnp.bfloat16
MESH = pl.DeviceIdType.MESH

V7X_VMEM_LIMIT_BYTES = 48 * 1024 * 1024
MM_VMEM_BUDGET_BYTES = 36 * 1024 * 1024
MM_WHOLE_K = 2816
LANES = 128
SB_HEAD_DIM = 128
X_HEADS = 4
CONV_K = 3
RMS_EPS = 1e-6
N_CHIPS = 4
N_DEV = 8
ADAM_LR, ADAM_B1, ADAM_B2, ADAM_EPS, ADAM_WD, ADAM_STEP = 0.001, 0.9, 0.999, 1e-08, 0.01, 10


ANY = pl.BlockSpec(memory_space=pl.ANY)


def _pcall(body, **kw):
    return pl.pallas_call(body, **kw)


def _params(*sem):
    return pltpu.CompilerParams(dimension_semantics=sem, vmem_limit_bytes=V7X_VMEM_LIMIT_BYTES)


def _pick(dim, cands):
    for c in cands:
        if dim % c == 0:
            return c
    return dim


def _dot(a, b, ca, cb):
    return lax.dot_general(a, b, (((ca,), (cb,)), ((), ())), preferred_element_type=F32)


def _mm(a, b, *, name, ta=False, tb=False, out_dtype=BF16, res=None, alpha=1.0, tm=None, tn=None, tk=None, after=None,
        a_halves=False, b_halves=False, norm_g=None):
    assert not (a_halves and ta) and not (b_halves and tb) and not (norm_g is not None and ta)
    if a_halves:
        m, k = a.shape[1], 2 * a.shape[2]
    else:
        m, k = (a.shape[1], a.shape[0]) if ta else a.shape
    if b_halves:
        n = 2 * b.shape[2]
        assert k == b.shape[1]
    else:
        n = b.shape[0] if tb else b.shape[1]
        assert k == (b.shape[1] if tb else b.shape[0]), (a.shape, b.shape, ta, tb)
    if ta:
        tm = tm or _pick(m, (512, 256, 128))
        tn = tn or _pick(n, (1024, 512, 256, 128))
        tk = tk or (k if k <= MM_WHOLE_K else _pick(k, (1024, 512, 256, 128)))
    else:
        tk = tk or (k if k <= MM_WHOLE_K else _pick(k, (MM_WHOLE_K, 2048, 1024, 512, 256, 128)))
        tn = tn or (n if norm_g is not None else _pick(n, (512, 1408, 256, 128) if tk == k else (1024, 512, 256, 128)))
        out_bytes = jnp.dtype(out_dtype).itemsize + (0 if res is None else res.dtype.itemsize) + (0 if norm_g is None else 2)
        per_row = 2 * (tk * a.dtype.itemsize + tn * out_bytes)
        per_row += 4 * tn if tk < k else 0
        rows = (MM_VMEM_BUDGET_BYTES - 2 * tk * tn * b.dtype.itemsize) // per_row
        tm = tm or next((c for c in (2048, 1024, 512, 256, 128) if m % c == 0 and c <= rows), m)
    if a_halves:
        tk = min(tk, k // 2) if (k // 2) % min(tk, k // 2) == 0 else _pick(k // 2, (1408, 1024, 512, 256, 128))
    if b_halves:
        tn = tn if (n // 2) % tn == 0 else _pick(n // 2, (1408, 1024, 512, 256, 128))
    nk = k // tk
    assert m % tm == 0 and n % tn == 0 and k % tk == 0
    a_spec = pl.BlockSpec((tk, tm), lambda i, j, kk: (kk, i)) if ta else pl.BlockSpec((tm, tk), lambda i, j, kk: (i, kk))
    b_spec = pl.BlockSpec((tn, tk), lambda i, j, kk: (j, kk)) if tb else pl.BlockSpec((tk, tn), lambda i, j, kk: (kk, j))
    if a_halves:
        per = (k // 2) // tk
        a_spec = pl.BlockSpec((None, tm, tk), lambda i, j, kk: (kk // per, i, kk % per))
    if b_halves:
        per_n = (n // 2) // tn
        b_spec = pl.BlockSpec((None, tk, tn), lambda i, j, kk: (j // per_n, kk, j % per_n))
    o_spec = pl.BlockSpec((tm, tn), lambda i, j, kk: (i, j))
    ca, cb = (0 if ta else 1), (1 if tb else 0)

    n_in = 2 + (res is not None) + (norm_g is not None) + (after is not None)
    n_out = 1 + (norm_g is not None)

    def body(*refs):
        a_ref, b_ref = refs[:2]
        res_ref = refs[2] if res is not None else None
        g_ref = refs[2 + (res is not None)] if norm_g is not None else None
        o_ref = refs[n_in]
        scratch = refs[n_in + n_out:]

        def finish(acc):
            val = acc if alpha == 1.0 else alpha * acc
            if res_ref is not None:
                val = res_ref[...].astype(F32) + val
            o_ref[...] = val.astype(o_ref.dtype)
            if g_ref is not None:
                refs[n_in + 1][...] = (_xhat(val)[0] * g_ref[...]).astype(BF16)

        part = _dot(a_ref[...].astype(BF16), b_ref[...].astype(BF16), ca, cb)
        if nk == 1:
            finish(part)
        else:
            acc_ref = scratch[0]
            kk = pl.program_id(2)

            @pl.when(kk == 0)
            def _():
                acc_ref[...] = part

            @pl.when(kk > 0)
            def _():
                acc_ref[...] += part

            @pl.when(kk == nk - 1)
            def _():
                finish(acc_ref[...])

    ins = [a, b] + ([] if res is None else [res]) + ([] if norm_g is None else [norm_g]) + ([] if after is None else [after])
    in_specs = [a_spec, b_spec] + ([] if res is None else [o_spec])
    in_specs += ([] if norm_g is None else [pl.BlockSpec((1, tn), lambda i, j, kk: (0, j))]) + ([] if after is None else [ANY])
    outs = _pcall(
        body, name=name, grid=(m // tm, n // tn, nk), in_specs=in_specs, out_specs=[o_spec] * n_out,
        out_shape=[jax.ShapeDtypeStruct((m, n), out_dtype)] + [jax.ShapeDtypeStruct((m, n), BF16)] * (n_out - 1),
        scratch_shapes=[pltpu.VMEM((tm, tn), F32)] if nk > 1 else [],
        compiler_params=_params("parallel", "parallel", "arbitrary"),
    )(*ins)
    return outs[0] if norm_g is None else outs


def _rowcall(fn, rows, consts, outs, accs=(), *, tm, name, after=None):
    s = rows[0][0].shape[0]
    assert s % tm == 0
    n_read, n_out = len(rows) + len(consts), len(outs)
    n_in = n_read + (after is not None)

    def body(*refs):
        vals = fn(*[r[...] for r in refs[:n_read]])
        vals = vals if isinstance(vals, (tuple, list)) else (vals,)
        for o_ref, v in zip(refs[n_in:n_in + n_out], vals[:n_out]):
            o_ref[...] = v.astype(o_ref.dtype)
        if accs:
            first = pl.program_id(0) == 0
            for a_ref, v in zip(refs[n_in + n_out:], vals[n_out:]):
                tot = jnp.sum(v.astype(F32), axis=0, keepdims=True)

                @pl.when(first)
                def _(a_ref=a_ref, tot=tot):
                    a_ref[...] = tot

                @pl.when(jnp.logical_not(first))
                def _(a_ref=a_ref, tot=tot):
                    a_ref[...] += tot

    in_specs = [pl.BlockSpec((tm, w), lambda i, cb=cb: (i, cb)) for (_, cb, w) in rows]
    in_specs += [pl.BlockSpec(c.shape, lambda i: (0, 0)) for c in consts]
    in_specs += [] if after is None else [ANY]
    out_specs = [pl.BlockSpec((tm, w), lambda i: (i, 0)) for (w, _) in outs]
    out_specs += [pl.BlockSpec((1, w), lambda i: (0, 0)) for w in accs]
    out_shape = [jax.ShapeDtypeStruct((s, w), dt) for (w, dt) in outs]
    out_shape += [jax.ShapeDtypeStruct((1, w), F32) for w in accs]
    return _pcall(
        body, name=name, grid=(s // tm,), in_specs=in_specs, out_specs=out_specs, out_shape=out_shape,
        compiler_params=_params("arbitrary" if accs else "parallel"),
    )(*[r[0] for r in rows], *consts, *([] if after is None else [after]))


def _whole(a):
    return (a, 0, a.shape[1])


def _xhat(x):
    x = x.astype(F32)
    r = lax.rsqrt(jnp.mean(x * x, axis=-1, keepdims=True) + RMS_EPS)
    return x * r, r


def _rms_bwd(dy, x, g):
    xh, r = _xhat(x)
    dxh = dy.astype(F32) * g
    dx = r * (dxh - xh * jnp.mean(dxh * xh, axis=-1, keepdims=True))
    return dx, dy.astype(F32) * xh


def _sigmoid(x):
    return 1.0 / (1.0 + jnp.exp(-x))


def _rms_fwd(x, g, name, tm, after=None):
    d = x.shape[1]
    return _rowcall(lambda xb, gb: _xhat(xb)[0] * gb, [_whole(x)], [g], [(d, BF16)], tm=tm, name=name, after=after)[0]


def _silu_parts(gate):
    sg = _sigmoid(gate)
    return sg, gate * sg


def _ffn_up(n, w_gu, name):
    s, d = n.shape
    f = w_gu.shape[1] // 2
    tn = _pick(f, (1408, 1024, 512, 256, 128))
    tm = _pick(s, (1024, 512, 256, 128))
    nb = f // tn

    def body(n_ref, wg_ref, wu_ref, gu_ref, act_ref):
        nv = n_ref[...]
        gate = _dot(nv, wg_ref[...], 1, 0)
        up = _dot(nv, wu_ref[...], 1, 0)
        gu_ref[0] = gate.astype(gu_ref.dtype)
        gu_ref[1] = up.astype(gu_ref.dtype)
        act_ref[...] = (_silu_parts(gate)[1] * up).astype(act_ref.dtype)

    return _pcall(
        body, name=name, grid=(s // tm, nb),
        in_specs=[pl.BlockSpec((tm, d), lambda i, j: (i, 0)), pl.BlockSpec((d, tn), lambda i, j: (0, j)),
                  pl.BlockSpec((d, tn), lambda i, j: (0, nb + j))],
        out_specs=[pl.BlockSpec((2, tm, tn), lambda i, j: (0, i, j)), pl.BlockSpec((tm, tn), lambda i, j: (i, j))],
        out_shape=[jax.ShapeDtypeStruct((2, s, f), BF16), jax.ShapeDtypeStruct((s, f), BF16)],
        compiler_params=_params("parallel", "parallel"),
    )(n, w_gu, w_gu)


def _ffn_dgu(dhb, w_down, gu, name, after=None):
    s, d = dhb.shape
    f = w_down.shape[0]
    tn = _pick(f, (1408, 1024, 512, 256, 128))
    tm = _pick(s, (1024, 512, 256, 128))

    def body(dh_ref, w_ref, gu_ref, *rest):
        o_ref = rest[-1]
        dact = _dot(dh_ref[...], w_ref[...], 1, 1)
        gate, up = gu_ref[0].astype(F32), gu_ref[1].astype(F32)
        sg, silu = _silu_parts(gate)
        o_ref[0] = (dact * up * (sg + silu * (1.0 - sg))).astype(o_ref.dtype)
        o_ref[1] = (dact * silu).astype(o_ref.dtype)

    blk = pl.BlockSpec((2, tm, tn), lambda i, j: (0, i, j))
    return _pcall(
        body, name=name, grid=(s // tm, f // tn),
        in_specs=[pl.BlockSpec((tm, d), lambda i, j: (i, 0)), pl.BlockSpec((tn, d), lambda i, j: (j, 0)), blk]
        + ([] if after is None else [ANY]),
        out_specs=blk, out_shape=jax.ShapeDtypeStruct((2, s, f), BF16), compiler_params=_params("parallel", "parallel"),
    )(dhb, w_down, gu, *([] if after is None else [after]))


def _dgrad_norm(dy, wmat, dh, x, g, name, *, dy_halves=False, copy_scale=None, after=None):
    s, d = dh.shape
    k = wmat.shape[1]
    tk = k if k <= MM_WHOLE_K else _pick(k, (MM_WHOLE_K, 2048, 1024, 512, 256, 128))
    if dy_halves and (k // 2) % tk:
        tk = _pick(k // 2, (1408, 1024, 512, 256, 128))
    tm = _pick(s, (512, 256, 128))
    nk, per = k // tk, (k // 2) // tk if dy_halves else 0
    n_in = 5 + (after is not None)
    n_out = 2 + (copy_scale is not None)

    def body(*refs):
        dy_ref, w_ref, dh_ref, x_ref, g_ref = refs[:5]
        outs, scratch = refs[n_in:n_in + n_out], refs[n_in + n_out:]
        i, kk = pl.program_id(0), pl.program_id(1)
        part = _dot(dy_ref[...], w_ref[...], 1, 1)

        def finish(dn):
            dx, dg = _rms_bwd(dn, x_ref[...], g_ref[...])
            tot = dh_ref[...] + dx
            outs[0][...] = tot
            if copy_scale is not None:
                outs[1][...] = (copy_scale * tot).astype(outs[1].dtype)
            dg = jnp.sum(dg, axis=0, keepdims=True)

            @pl.when(i == 0)
            def _():
                outs[-1][...] = dg

            @pl.when(i > 0)
            def _():
                outs[-1][...] += dg

        if nk == 1:
            finish(part)
        else:
            acc_ref = scratch[0]

            @pl.when(kk == 0)
            def _():
                acc_ref[...] = part

            @pl.when(kk > 0)
            def _():
                acc_ref[...] += part

            @pl.when(kk == nk - 1)
            def _():
                finish(acc_ref[...])

    row = pl.BlockSpec((tm, d), lambda i, kk: (i, 0))
    dy_spec = pl.BlockSpec((None, tm, tk), lambda i, kk: (kk // per, i, kk % per)) if dy_halves else pl.BlockSpec((tm, tk), lambda i, kk: (i, kk))
    in_specs = [dy_spec, pl.BlockSpec((d, tk), lambda i, kk: (0, kk)), row, row, pl.BlockSpec((1, d), lambda i, kk: (0, 0))]
    out_specs = [row] * (n_out - 1) + [pl.BlockSpec((1, d), lambda i, kk: (0, 0))]
    out_shape = [jax.ShapeDtypeStruct((s, d), F32)] + ([] if copy_scale is None else [jax.ShapeDtypeStruct((s, d), BF16)])
    return _pcall(
        body, name=name, grid=(s // tm, nk), in_specs=in_specs + ([] if after is None else [ANY]), out_specs=out_specs,
        out_shape=out_shape + [jax.ShapeDtypeStruct((1, d), F32)], scratch_shapes=[pltpu.VMEM((tm, d), F32)] if nk > 1 else [],
        compiler_params=_params("arbitrary", "arbitrary"),
    )(dy, wmat, dh, x, g, *([] if after is None else [after]))


def _shift_down(p, k):
    if k == 0:
        return p
    rows = lax.broadcasted_iota(jnp.int32, p.shape, 0)
    return jnp.where(rows >= k, pltpu.roll(p, k, 0), 0.0)


def _shift_up(p, k):
    if k == 0:
        return p
    s = p.shape[0]
    rows = lax.broadcasted_iota(jnp.int32, p.shape, 0)
    return jnp.where(rows < s - k, pltpu.roll(p, s - k, 0), 0.0)


def _conv_fwd(proj, conv_w, d, tc, name):
    s = proj.shape[0]
    nb = d // tc

    def body(cb_ref, cc_ref, cx_ref, w_ref, y_ref):
        p = cc_ref[...].astype(F32) * cx_ref[...].astype(F32)
        w = w_ref[...]
        acc = p * w[CONV_K - 1:CONV_K, :]
        for k in range(1, CONV_K):
            acc = acc + _shift_down(p, k) * w[CONV_K - 1 - k:CONV_K - k, :]
        y_ref[...] = (cb_ref[...].astype(F32) * acc).astype(y_ref.dtype)

    col = lambda off: pl.BlockSpec((s, tc), lambda j: (0, off * nb + j))
    return _pcall(
        body, name=name, grid=(nb,), in_specs=[col(0), col(1), col(2), pl.BlockSpec((CONV_K, tc), lambda j: (0, j))],
        out_specs=pl.BlockSpec((s, tc), lambda j: (0, j)), out_shape=jax.ShapeDtypeStruct((s, d), BF16),
        compiler_params=_params("parallel"),
    )(proj, proj, proj, conv_w)


def _conv_bwd(dy, proj, conv_w, d, tc, name):
    s = proj.shape[0]
    nb = d // tc

    def body(dy_ref, cb_ref, cc_ref, cx_ref, w_ref, dcb_ref, dcc_ref, dcx_ref, dw_ref):
        cc, cx = cc_ref[...].astype(F32), cx_ref[...].astype(F32)
        p = cc * cx
        w = w_ref[...]
        dyv = dy_ref[...].astype(F32)
        shifted = [_shift_down(p, CONV_K - 1 - k) for k in range(CONV_K)]
        conv = shifted[0] * w[0:1, :]
        for k in range(1, CONV_K):
            conv = conv + shifted[k] * w[k:k + 1, :]
        dcb_ref[...] = (dyv * conv).astype(dcb_ref.dtype)
        ds = dyv * cb_ref[...].astype(F32)
        dp = ds * w[CONV_K - 1:CONV_K, :]
        for k in range(1, CONV_K):
            dp = dp + _shift_up(ds, k) * w[CONV_K - 1 - k:CONV_K - k, :]
        dcc_ref[...] = (dp * cx).astype(dcc_ref.dtype)
        dcx_ref[...] = (dp * cc).astype(dcx_ref.dtype)
        for k in range(CONV_K):
            dw_ref[k:k + 1, :] = jnp.sum(ds * shifted[k], axis=0, keepdims=True)

    col = lambda off: pl.BlockSpec((s, tc), lambda j: (0, off * nb + j))
    blk = pl.BlockSpec((s, tc), lambda j: (0, j))
    wblk = pl.BlockSpec((CONV_K, tc), lambda j: (0, j))
    act = jax.ShapeDtypeStruct((s, d), BF16)
    return _pcall(
        body, name=name, grid=(nb,), in_specs=[blk, col(0), col(1), col(2), wblk],
        out_specs=[blk, blk, blk, wblk], out_shape=[act, act, act, jax.ShapeDtypeStruct((CONV_K, d), F32)],
        compiler_params=_params("parallel"),
    )(dy, proj, proj, proj, conv_w)


def _sb_tile(q, kj, scale, carry, tri, mask):
    z = _dot(q, kj, 1, 1) * scale
    lsz = jnp.minimum(z, 0.0) - jnp.log(1.0 + jnp.exp(-jnp.abs(z)))
    l1m = lsz - z
    if mask is not None:
        l1m = jnp.where(mask, l1m, 0.0)
    l1b = l1m.astype(BF16)
    a = jnp.exp(lsz + (carry + _dot(l1b, tri, 1, 0)))
    if mask is not None:
        a = jnp.where(mask, a, 0.0)
    return lsz, l1b, a.astype(BF16)


def _add_rows(x, upd, r0):
    return x + upd if r0 == 0 else jnp.concatenate([x[:r0], x[r0:] + upd], axis=0)


def _sb_masks(tq, tk):
    row = lax.broadcasted_iota(jnp.int32, (tq, tk), 0)
    col = lax.broadcasted_iota(jnp.int32, (tq, tk), 1)
    masks = [col + dj * tk < row for dj in range(tq // tk)]
    r2 = lax.broadcasted_iota(jnp.int32, (tk, tk), 0)
    c2 = lax.broadcasted_iota(jnp.int32, (tk, tk), 1)
    return masks, (r2 > c2).astype(BF16), (r2 < c2).astype(BF16)


def _sb_fwd(proj, heads, col0, tq, tk, name):
    s = proj.shape[0]
    dh = SB_HEAD_DIM
    nq, nd, nkt = s // tq, tq // tk, s // tk
    scale = dh ** -0.5

    def body(q_ref, k_ref, v_ref, o_ref, a_ref, b_ref):
        i = pl.program_id(1)
        q = q_ref[...]
        masks, tri_right, _ = _sb_masks(tq, tk)

        def tile(j, carry, acc, mask, r0=0):
            start = pl.multiple_of(j * tk, tk)
            kj = k_ref[pl.ds(start, tk), :]
            vj = v_ref[pl.ds(start, tk), :]
            lsz, l1b, ab = _sb_tile(q[r0:], kj, scale, carry[r0:], tri_right, None if mask is None else mask[r0:])
            a_ref[j, r0:, :] = ab
            b_ref[j, r0:, :] = jnp.exp(lsz).astype(b_ref.dtype)
            if r0:
                a_ref[j, :r0, :] = jnp.zeros((r0, tk), a_ref.dtype)
                b_ref[j, :r0, :] = jnp.zeros((r0, tk), b_ref.dtype)
            return (_add_rows(carry, jnp.sum(l1b.astype(F32), axis=1, keepdims=True), r0),
                    _add_rows(acc, _dot(ab, vj, 1, 0), r0))

        state = (jnp.zeros((tq, 1), F32), jnp.zeros((tq, dh), F32))
        for dj in reversed(range(nd)):
            state = tile(i * nd + dj, *state, masks[dj], dj * tk)
        def left_block(t, st):
            for dj in reversed(range(nd)):
                st = tile((i - 1 - t) * nd + dj, st[0], st[1], None)
            return st

        state = lax.fori_loop(0, i, left_block, state)
        o_ref[...] = state[1]

    qspec = pl.BlockSpec((tq, dh), lambda h, i: (i, col0[0] + h))
    kspec = pl.BlockSpec((s, dh), lambda h, i: (0, col0[1] + h))
    vspec = pl.BlockSpec((s, dh), lambda h, i: (0, col0[2] + h))
    saved = pl.BlockSpec((None, nkt, tq, tk), lambda h, i: (h, 0, i, 0))
    saved_shape = jax.ShapeDtypeStruct((heads, nkt, s, tk), BF16)
    return _pcall(
        body, name=name, grid=(heads, nq), in_specs=[qspec, kspec, vspec],
        out_specs=[pl.BlockSpec((tq, dh), lambda h, i: (i, h)), saved, saved],
        out_shape=[jax.ShapeDtypeStruct((s, heads * dh), F32), saved_shape, saved_shape],
        compiler_params=_params("parallel", "parallel"),
    )(proj, proj, proj)


SB_BWD_HEADS = 2


def _sb_bwd(proj, o, a_all, beta_all, do, heads, col0, tq, tk, name):
    s = proj.shape[0]
    dh = SB_HEAD_DIM
    nq, nd, nkt = s // tq, tq // tk, s // tk
    scale = dh ** -0.5
    hb = SB_BWD_HEADS if heads % SB_BWD_HEADS == 0 and all(c % SB_BWD_HEADS == 0 for c in col0) else 1
    wide = hb * dh

    def body(q_ref, k_ref, v_ref, o_ref, a_ref, b_ref, do_ref, dq_ref, dk_ref, dv_ref, dk_acc, dv_acc):
        i = pl.program_id(1)

        @pl.when(i == 0)
        def _():
            dk_acc[...] = jnp.zeros_like(dk_acc)
            dv_acc[...] = jnp.zeros_like(dv_acc)

        lanes = [slice(hh * dh, (hh + 1) * dh) for hh in range(hb)]
        q = [q_ref[:, ln] for ln in lanes]
        dob = [do_ref[:, ln].astype(BF16) for ln in lanes]
        delta = [jnp.sum(dob[hh].astype(F32) * o_ref[:, lanes[hh]], axis=1, keepdims=True) for hh in range(hb)]
        masks, _, tri_left = _sb_masks(tq, tk)

        def tile(hh, j, carry_g, dq, mask):
            start = pl.multiple_of(j * tk, tk)
            kj = k_ref[pl.ds(start, tk), lanes[hh]]
            vj = v_ref[pl.ds(start, tk), lanes[hh]]
            ab = a_ref[hh, j]
            g = _dot(dob[hh], vj, 1, 1) * ab.astype(F32)
            carry_g = carry_g + jnp.sum(g, axis=1, keepdims=True)
            left = (delta[hh] - carry_g) + _dot(g.astype(BF16), tri_left, 1, 0)
            dz = g - b_ref[hh, j].astype(F32) * (g + left)
            if mask is not None:
                dz = jnp.where(mask, dz, 0.0)
            dzb = dz.astype(BF16)
            dk_acc[pl.ds(start, tk), lanes[hh]] += _dot(dzb, q[hh], 0, 0)
            dv_acc[pl.ds(start, tk), lanes[hh]] += _dot(ab, dob[hh], 0, 0)
            return carry_g, dq + _dot(dzb, kj, 1, 0)

        def block(jb, st, use_masks):
            st = list(st)
            for dj in reversed(range(nd)):
                for hh in range(hb):
                    st[hh] = tile(hh, jb * nd + dj, *st[hh], masks[dj] if use_masks else None)
            return tuple(st)

        state = block(i, tuple((jnp.zeros((tq, 1), F32), jnp.zeros((tq, dh), F32)) for _ in range(hb)), True)
        state = lax.fori_loop(0, i, lambda t, st: block(i - 1 - t, st, False), state)
        for hh in range(hb):
            dq_ref[:, lanes[hh]] = (state[hh][1] * scale).astype(dq_ref.dtype)

        @pl.when(i == nq - 1)
        def _():
            dk_ref[...] = (dk_acc[...] * scale).astype(dk_ref.dtype)
            dv_ref[...] = dv_acc[...].astype(dv_ref.dtype)

    qspec = pl.BlockSpec((tq, wide), lambda h, i: (i, col0[0] // hb + h))
    kspec = pl.BlockSpec((s, wide), lambda h, i: (0, col0[1] // hb + h))
    vspec = pl.BlockSpec((s, wide), lambda h, i: (0, col0[2] // hb + h))
    blk = pl.BlockSpec((tq, wide), lambda h, i: (i, h))
    full = pl.BlockSpec((s, wide), lambda h, i: (0, h))
    saved = pl.BlockSpec((hb, nkt, tq, tk), lambda h, i: (h, 0, i, 0))
    act = jax.ShapeDtypeStruct((s, heads * dh), BF16)
    return _pcall(
        body, name=name, grid=(heads // hb, nq), in_specs=[qspec, kspec, vspec, blk, saved, saved, blk],
        out_specs=[blk, full, full], out_shape=[act, act, act],
        scratch_shapes=[pltpu.VMEM((s, wide), F32), pltpu.VMEM((s, wide), F32)],
        compiler_params=_params("parallel", "arbitrary"),
    )(proj, proj, proj, o, a_all, beta_all, do)


def _xattn_probs(q, k, scale):
    sc = _dot(q, k, 1, 1) * scale
    e = jnp.exp(sc - jnp.max(sc, axis=1, keepdims=True))
    return e / jnp.sum(e, axis=1, keepdims=True)


def _xattn_fwd(qc, kv, tq, name):
    s, d = qc.shape
    m = kv.shape[0]
    dh = d // X_HEADS
    scale = dh ** -0.5

    def body(q_ref, k_ref, v_ref, o_ref):
        p = _xattn_probs(q_ref[...], k_ref[...], scale)
        o_ref[...] = _dot(p.astype(BF16), v_ref[...], 1, 0).astype(o_ref.dtype)

    blk = pl.BlockSpec((tq, dh), lambda h, i: (i, h))
    return _pcall(
        body, name=name, grid=(X_HEADS, s // tq),
        in_specs=[blk, pl.BlockSpec((m, dh), lambda h, i: (0, h)), pl.BlockSpec((m, dh), lambda h, i: (0, X_HEADS + h))],
        out_specs=blk, out_shape=jax.ShapeDtypeStruct((s, d), BF16), compiler_params=_params("parallel", "parallel"),
    )(qc, kv, kv)


def _xattn_bwd(qc, kv, do, tq, name):
    s, d = qc.shape
    m = kv.shape[0]
    dh = d // X_HEADS
    scale = dh ** -0.5
    nq = s // tq

    def body(q_ref, k_ref, v_ref, do_ref, dq_ref, dk_ref, dv_ref, dk_acc, dv_acc):
        i = pl.program_id(1)
        q, k, v = q_ref[...], k_ref[...], v_ref[...]
        dob = do_ref[...].astype(BF16)
        p = _xattn_probs(q, k, scale)
        pb = p.astype(BF16)
        dp = _dot(dob, v, 1, 1)
        ds = pb.astype(F32) * (dp - jnp.sum(dp * pb.astype(F32), axis=1, keepdims=True))
        dsb = (ds * scale).astype(BF16)
        dq_ref[...] = _dot(dsb, k, 1, 0).astype(dq_ref.dtype)
        dk_part = _dot(dsb, q, 0, 0)
        dv_part = _dot(pb, dob, 0, 0)

        @pl.when(i == 0)
        def _():
            dk_acc[...] = dk_part
            dv_acc[...] = dv_part

        @pl.when(i > 0)
        def _():
            dk_acc[...] += dk_part
            dv_acc[...] += dv_part

        @pl.when(i == nq - 1)
        def _():
            dk_ref[...] = dk_acc[...].astype(dk_ref.dtype)
            dv_ref[...] = dv_acc[...].astype(dv_ref.dtype)

    blk = pl.BlockSpec((tq, dh), lambda h, i: (i, h))
    kblk = pl.BlockSpec((m, dh), lambda h, i: (0, h))
    return _pcall(
        body, name=name, grid=(X_HEADS, nq),
        in_specs=[blk, kblk, pl.BlockSpec((m, dh), lambda h, i: (0, X_HEADS + h)), blk],
        out_specs=[blk, kblk, kblk],
        out_shape=[jax.ShapeDtypeStruct((s, d), BF16), jax.ShapeDtypeStruct((m, d), BF16), jax.ShapeDtypeStruct((m, d), BF16)],
        scratch_shapes=[pltpu.VMEM((m, dh), F32), pltpu.VMEM((m, dh), F32)],
        compiler_params=_params("parallel", "arbitrary"),
    )(qc, kv, kv, do)


def _mix_merge(y_conv, y_sb, w_conv_out, w_attn_out, proj, gate_blocks, b_conv, b_sb, name):
    s, d = y_conv.shape
    tm, tn = _pick(s, (1024, 512, 256, 128)), _pick(d, (512, 256, 128))
    nb = d // tn

    def body(yc_ref, ys_ref, wc_ref, ws_ref, gc_ref, gs_ref, bc_ref, bs_ref, ac_ref, as_ref, m_ref):
        ac = _dot(yc_ref[...].astype(BF16), wc_ref[...], 1, 0)
        asb = _dot(ys_ref[...].astype(BF16), ws_ref[...], 1, 0)
        gc = _sigmoid(gc_ref[...].astype(F32) + bc_ref[...])
        gs = _sigmoid(gs_ref[...].astype(F32) + bs_ref[...])
        ac_ref[...] = ac.astype(ac_ref.dtype)
        as_ref[...] = asb.astype(as_ref.dtype)
        m_ref[...] = (gc * ac + gs * asb).astype(m_ref.dtype)

    rows = pl.BlockSpec((tm, d), lambda i, j: (i, 0))
    wcol = pl.BlockSpec((d, tn), lambda i, j: (0, j))
    bias = pl.BlockSpec((1, tn), lambda i, j: (0, j))
    gate = lambda blk: pl.BlockSpec((tm, tn), lambda i, j: (i, blk * nb + j))
    out = pl.BlockSpec((tm, tn), lambda i, j: (i, j))
    act = jax.ShapeDtypeStruct((s, d), BF16)
    return _pcall(
        body, name=name, grid=(s // tm, nb),
        in_specs=[rows, rows, wcol, wcol, gate(gate_blocks[0]), gate(gate_blocks[1]), bias, bias],
        out_specs=[out, out, out], out_shape=[act, act, act], compiler_params=_params("parallel", "parallel"),
    )(y_conv, y_sb, w_conv_out, w_attn_out, proj, proj, b_conv, b_sb)


def _local_step(x, mem, tgt, w, fetch=None, prefetch=None, emit=None, tick=None, after=None):
    fetch = fetch or (lambda name, after: {})
    prefetch = prefetch or (lambda name, after: None)
    emit = emit or (lambda group, g: None)
    tick = tick or (lambda group, after: None)
    w = dict(w)
    s, d = x.shape
    heads = d // SB_HEAD_DIM
    tm = _pick(s, (1024, 512, 256, 128))
    tq = _pick(s, (1024, 512, 256, 128))
    sb_tq, sb_tk = _pick(s, (512, 256, 128)), _pick(s, (256, 128))
    tc = _pick(d, (256, 128))
    g = {}

    def wt(name, after):
        if name not in w:
            w.update(fetch(name, after))
        return w[name]

    def ffn_fwd(h, n, wgu, wdown, tag, next_g=None):
        gu, act = _ffn_up(n, wt(wgu, n), tag + "_gu")
        prefetch(wdown, gu)
        return gu, act, _mm(act, wt(wdown, act), name=tag + "_down", out_dtype=F32, res=h, alpha=0.5, norm_g=next_g)

    def ffn_bwd(dh, dhb, h, saved, gname, wgu, wdown, tag, copy_scale=None, after=None):
        n, gu, act = saved
        g[wdown] = _mm(act, dhb, ta=True, name=tag + "_dwdown", after=after)
        dgu = _ffn_dgu(dhb, w[wdown], gu, tag + "_dgu", after=emit(tag + "_down", g))
        g[wgu] = _mm(n, dgu, ta=True, b_halves=True, name=tag + "_dwgu", after=tick(tag + "_down", dgu))
        *dh_in, g[gname] = _dgrad_norm(dgu, w[wgu], dh, h, w[gname], tag + "_dn", dy_halves=True, copy_scale=copy_scale,
                                       after=emit(tag, g))
        return dh_in, tick(tag, dh_in[0])

    n1 = _rms_fwd(x, w["g_ffn1"], "ffn1_norm", tm, after=after)
    gu1, act1, (h1, u) = ffn_fwd(x, n1, "w_ffn1_gu", "w_ffn1_down", "ffn1", w["g_mix"])
    prefetch("w_in", h1)
    proj = _mm(u, wt("w_in", u), name="mix_in")
    prefetch("w_conv_out", proj)
    nd = d // SB_HEAD_DIM
    y_conv = _conv_fwd(proj, w["conv_w"], d, tc, "conv_fwd")
    sb_cols = (3 * nd, 4 * nd, 5 * nd)
    y_sb, sb_a, sb_beta = _sb_fwd(proj, heads, sb_cols, _pick(s, (2 * sb_tq, sb_tq)), sb_tk, "sb_fwd")
    prefetch("w_cq", y_sb)
    b_conv, b_sb = w["b_gate"][:, :d], w["b_gate"][:, d:]
    a_conv, a_sb, merged = _mix_merge(y_conv, y_sb, wt("w_conv_out", y_conv), wt("w_attn_out", y_sb), proj, (6, 7), b_conv, b_sb,
                                      "mix_merge")
    prefetch("w_ffn2_gu", merged)
    h2, hn = _mm(merged, wt("w_o", merged), name="mix_out", out_dtype=F32, res=h1, norm_g=w["g_cross"])
    mn = _rms_fwd(mem, w["g_mem"], "mem_norm", _pick(mem.shape[0], (256, 128)))
    qc = _mm(hn, wt("w_cq", hn), name="cross_q")
    kv = _mm(mn, wt("w_ckv", mn), name="cross_kv")
    oc = _xattn_fwd(qc, kv, tq, "xattn_fwd")
    h3, n2 = _mm(oc, wt("w_co", oc), name="cross_out", out_dtype=F32, res=h2, norm_g=w["g_ffn2"])
    gu2, act2, h4 = ffn_fwd(h3, n2, "w_ffn2_gu", "w_ffn2_down", "ffn2")

    def head(hb, tb, gb):
        xh, r = _xhat(hb)
        err = xh * gb - tb
        dy = err * (1.0 / d)
        dxh = dy * gb
        dx = r * (dxh - xh * jnp.mean(dxh * xh, axis=-1, keepdims=True))
        row_loss = 0.5 * jnp.mean(err * err, axis=-1, keepdims=True)
        return dx, 0.5 * dx, dy * xh, jnp.broadcast_to(row_loss, (row_loss.shape[0], LANES))

    dh4, dh4b, g["g_final"], loss_lanes = _rowcall(head, [_whole(h4), _whole(tgt)], [w["g_final"]], [(d, F32), (d, BF16)],
                                                   [d, LANES], tm=tm, name="loss_head")

    (dh3, dh3b), tok = ffn_bwd(dh4, dh4b, h3, (n2, gu2, act2), "g_ffn2", "w_ffn2_gu", "w_ffn2_down", "ffn2", copy_scale=1.0)
    g["w_co"] = _mm(oc, dh3b, ta=True, name="cross_dwco", after=tok)
    doc = _mm(dh3b, w["w_co"], tb=True, name="cross_doc")
    dqc, dk, dv = _xattn_bwd(qc, kv, doc, tq, "xattn_bwd")
    dkv = jnp.concatenate([dk, dv], axis=1)
    g["w_cq"] = _mm(hn, dqc, ta=True, name="cross_dwcq")
    g["w_ckv"] = _mm(mn, dkv, ta=True, name="cross_dwckv")
    dmn = _mm(dkv, w["w_ckv"], tb=True, name="cross_dmn", out_dtype=F32)
    g["g_mem"] = _rowcall(lambda dy, xb: dy * _xhat(xb)[0], [_whole(dmn), _whole(mem)], [], [], [d],
                          tm=_pick(mem.shape[0], (256, 128)), name="mem_dnorm")[0]
    dh2, dh2b, g["g_cross"] = _dgrad_norm(dqc, w["w_cq"], dh3, h2, w["g_cross"], "cross_dhn", copy_scale=1.0, after=emit("cross", g))

    g["w_o"] = _mm(merged, dh2b, ta=True, name="mix_dwo", after=tick("cross", dh2))
    dmerged = _mm(dh2b, w["w_o"], tb=True, name="mix_dmerged")

    def merge_bwd(dm, ac, asb, gcp, gsp, bc, bs):
        dm, ac, asb = dm.astype(F32), ac.astype(F32), asb.astype(F32)
        gc = _sigmoid(gcp.astype(F32) + bc)
        gs = _sigmoid(gsp.astype(F32) + bs)
        dgc = dm * ac * gc * (1.0 - gc)
        dgs = dm * asb * gs * (1.0 - gs)
        return dm * gc, dm * gs, dgc, dgs, dgc, dgs

    da_conv, da_sb, dgc, dgs, db_conv, db_sb = _rowcall(
        merge_bwd, [_whole(dmerged), _whole(a_conv), _whole(a_sb), (proj, 6, d), (proj, 7, d)], [b_conv, b_sb],
        [(d, BF16)] * 4, [d, d], tm=tm, name="merge_bwd")
    g["b_gate"] = jnp.concatenate([db_conv, db_sb], axis=1)
    g["w_conv_out"] = _mm(y_conv, da_conv, ta=True, name="conv_dwout")
    g["w_attn_out"] = _mm(y_sb, da_sb, ta=True, name="attn_dwout")
    dy_conv = _mm(da_conv, w["w_conv_out"], tb=True, name="conv_dy")
    dy_sb = _mm(da_sb, w["w_attn_out"], tb=True, name="attn_dy")
    dcb, dcc, dcx, g["conv_w"] = _conv_bwd(dy_conv, proj, w["conv_w"], d, tc, "conv_bwd")
    dq, dk_sb, dv_sb = _sb_bwd(proj, y_sb, sb_a, sb_beta, dy_sb, heads, sb_cols, sb_tq, sb_tk, "sb_bwd")
    dproj = jnp.concatenate([dcb, dcc, dcx, dq, dk_sb, dv_sb, dgc, dgs], axis=1)
    g["w_in"] = _mm(u, dproj, ta=True, name="mix_dwin")
    dh1, dh1b, g["g_mix"] = _dgrad_norm(dproj, w["w_in"], dh2, h1, w["g_mix"], "mix_du", copy_scale=0.5, after=emit("mix", g))
    (dx,), tok = ffn_bwd(dh1, dh1b, x, (n1, gu1, act1), "g_ffn1", "w_ffn1_gu", "w_ffn1_down", "ffn1", after=tick("mix", dh1))
    return loss_lanes, dx, g, tok


MATS = (("w_ffn1_gu", "col"), ("w_ffn1_down", "row"), ("w_in", "col"), ("w_conv_out", "row"), ("w_attn_out", "row"),
        ("w_o", "row"), ("w_cq", "row"), ("w_ckv", "col"), ("w_co", "row"), ("w_ffn2_gu", "col"), ("w_ffn2_down", "row"))
VECS = ("g_ffn1", "g_mix", "g_cross", "g_mem", "g_ffn2", "g_final")
WEIGHTS = ("g_ffn1", "w_ffn1_gu", "w_ffn1_down", "g_mix", "w_in", "b_gate", "conv_w", "w_conv_out", "w_attn_out", "w_o",
           "g_cross", "g_mem", "w_cq", "w_ckv", "w_co", "g_ffn2", "w_ffn2_gu", "w_ffn2_down", "g_final")
CONV_ROWS = 16


def _full_shape(kind, r, c):
    return (r, N_CHIPS * c) if kind == "col" else (N_CHIPS * r, c)


def _piece(ref, kind, r, c, chip, half):
    hr = r // 2
    if kind == "col":
        return ref.at[pl.ds(pl.multiple_of(half * hr, math.gcd(hr, 16)), hr), pl.ds(pl.multiple_of(chip * c, LANES), c)]
    return ref.at[pl.ds(pl.multiple_of(chip * r + half * hr, math.gcd(hr, 16)), hr), :]


def _shard_of(ref, kind, r, c, chip):
    if kind == "col":
        return ref.at[:, pl.ds(pl.multiple_of(chip * c, LANES), c)]
    return ref.at[pl.ds(pl.multiple_of(chip * r, 16), r), :]


def _place():
    x, y, c = lax.axis_index("x"), lax.axis_index("y"), lax.axis_index("c")
    others = [(1 - x, y), (x, 1 - y), (1 - x, 1 - y)]
    return x, y, c, 2 * x + y, others


def _remote(src, dst, send_sem, recv_sem, to):
    return pltpu.make_async_remote_copy(src_ref=src, dst_ref=dst, send_sem=send_sem, recv_sem=recv_sem,
                                        device_id=to, device_id_type=MESH)


HBM = pl.BlockSpec(memory_space=pltpu.HBM)
SEM = pl.BlockSpec(memory_space=pltpu.SEMAPHORE)
EFFECT = pltpu.SideEffectType.DATAFLOW_SIDE_EFFECTING
TOKEN = (8, LANES)


def _split_start(name, plan, n_copies, srcs, lands, after=None):
    ns, nl = len(srcs), len(lands)
    n_in = ns + nl + (after is not None)

    def body(*refs):
        outs = refs[n_in:]
        sends, _ = plan(refs[:ns], refs[ns:ns + nl], outs[0], outs[1])
        for cp in sends:
            cp.start()
        outs[-1][...] = jnp.zeros(TOKEN, F32)

    held = [pltpu.HBM(a.shape, a.dtype) for a in (*srcs, *lands)]
    dma = pltpu.SemaphoreType.DMA((n_copies,))
    ins = [pltpu.with_memory_space_constraint(a, pltpu.HBM) for a in (*srcs, *lands)]
    outs = _pcall(
        body, name=name, in_specs=[HBM] * (ns + nl) + ([] if after is None else [ANY]),
        out_specs=(SEM, SEM, *[HBM] * (ns + nl), pl.BlockSpec(memory_space=pltpu.VMEM)),
        out_shape=(dma, dma, *held, jax.ShapeDtypeStruct(TOKEN, F32)),
        input_output_aliases={i: 2 + i for i in range(ns + nl)},
        compiler_params=pltpu.CompilerParams(has_side_effects=EFFECT),
    )(*ins, *([] if after is None else [after]))
    return outs[0], outs[1], list(outs[2:2 + ns]), list(outs[2 + ns:2 + ns + nl]), outs[-1]


def _split_wait(name, plan, send_sems, recv_sems, srcs, lands, after):
    ns, nl = len(srcs), len(lands)

    def body(*refs):
        sends, recvs = plan(refs[:ns], refs[ns:ns + nl], refs[ns + nl], refs[ns + nl + 1])
        for cp in sends:
            cp.wait_send()
        for cp in recvs:
            cp.wait_recv()

    outs = _pcall(
        body, name=name, in_specs=[HBM] * (ns + nl) + [SEM, SEM, ANY], out_specs=[HBM] * (ns + nl),
        out_shape=[pltpu.HBM(a.shape, a.dtype) for a in (*srcs, *lands)],
        input_output_aliases={i: i for i in range(ns + nl)},
        compiler_params=pltpu.CompilerParams(has_side_effects=EFFECT),
    )(*srcs, *lands, send_sems, recv_sems, after)
    return list(outs[:ns]), list(outs[ns:])


def _gather_plan(dims):
    def plan(shard_refs, full_refs, ss, rs):
        x, y, c, me, others = _place()
        sends, recvs = [], []
        for wi, (kind, r, cw) in enumerate(dims):
            half = shard_refs[wi].at[pl.ds(pl.multiple_of(c * (r // 2), math.gcd(r // 2, 16)), r // 2), :]
            for k, (ox, oy) in enumerate(others):
                sem = 4 * wi + k
                sends.append(_remote(half, _piece(full_refs[wi], kind, r, cw, me, c), ss.at[sem], rs.at[sem], (ox, oy, c)))
                recvs.append(_remote(half, _piece(full_refs[wi], kind, r, cw, 2 * ox + oy, c), ss.at[sem], rs.at[sem], (x, y, c)))
            sem = 4 * wi + 3
            own = _remote(shard_refs[wi], _shard_of(full_refs[wi], kind, r, cw, me), ss.at[sem], rs.at[sem], (x, y, 1 - c))
            sends.append(own)
            recvs.append(own)
        return sends, recvs

    return plan


def _forward_plan(dims):
    def plan(_, full_refs, ss, rs):
        x, y, c, _, others = _place()
        sends, recvs = [], []
        for wi, (kind, r, cw) in enumerate(dims):
            for k, (ox, oy) in enumerate(others):
                sem = 3 * wi + k
                mine = _piece(full_refs[wi], kind, r, cw, 2 * ox + oy, c)
                theirs = _piece(full_refs[wi], kind, r, cw, 2 * ox + oy, 1 - c)
                sends.append(_remote(mine, mine, ss.at[sem], rs.at[sem], (x, y, 1 - c)))
                recvs.append(_remote(theirs, theirs, ss.at[sem], rs.at[sem], (x, y, 1 - c)))
        return sends, recvs

    return plan


def _rs_cores_plan(dims):
    def plan(g_refs, land_refs, ss, rs):
        x, y, c, _, _ = _place()
        sends, recvs = [], []
        for wi, dm in enumerate(dims):
            for chip in range(N_CHIPS):
                sem = N_CHIPS * wi + chip
                sends.append(_remote(_piece(g_refs[wi], *dm, chip, 1 - c), land_refs[wi].at[chip], ss.at[sem], rs.at[sem], (x, y, 1 - c)))
                recvs.append(_remote(_piece(g_refs[wi], *dm, chip, c), land_refs[wi].at[chip], ss.at[sem], rs.at[sem], (x, y, 1 - c)))
        return sends, recvs

    return plan


def _share_plan(nw):
    def plan(_, buf_refs, ss, rs):
        x, y, c, _, _ = _place()
        sends = [_remote(buf_refs[wi].at[c], buf_refs[wi].at[c], ss.at[wi], rs.at[wi], (x, y, 1 - c)) for wi in range(nw)]
        recvs = [_remote(buf_refs[wi].at[1 - c], buf_refs[wi].at[1 - c], ss.at[wi], rs.at[wi], (x, y, 1 - c)) for wi in range(nw)]
        return sends, recvs

    return plan


def _small_plan():
    def plan(_, buf_refs, ss, rs):
        x, y, c = lax.axis_index("x"), lax.axis_index("y"), lax.axis_index("c")
        buf = buf_refs[0]
        sends, recvs = [], []
        for rel in range(1, N_DEV):
            peer = (x ^ (rel >> 2 & 1), y ^ (rel >> 1 & 1), c ^ (rel & 1))
            sends.append(_remote(buf.at[0], buf.at[rel], ss.at[rel - 1], rs.at[rel - 1], peer))
            recvs.append(_remote(buf.at[0], buf.at[rel], ss.at[rel - 1], rs.at[rel - 1], peer))
        return sends, recvs

    return plan


def _sum_small(buf, me, name):
    _, rows, n = buf.shape

    def body(me_ref, b_ref, o_ref):
        tot = b_ref[me_ref[0]]
        for dev in range(1, N_DEV):
            tot = tot + b_ref[dev ^ me_ref[0]]
        o_ref[...] = tot

    return _pcall(
        body, name=name, out_shape=jax.ShapeDtypeStruct((rows, n), F32),
        grid_spec=pltpu.PrefetchScalarGridSpec(
            num_scalar_prefetch=1, grid=(1,), in_specs=[pl.BlockSpec((N_DEV, rows, n), lambda i, m: (0, 0, 0))],
            out_specs=pl.BlockSpec((rows, n), lambda i, m: (0, 0))),
    )(me, buf)


def _rs_chips_plan(nw):
    def plan(p_refs, land_refs, ss, rs):
        x, y, c, me, others = _place()
        sends, recvs = [], []
        for wi in range(nw):
            for k, (ox, oy) in enumerate(others):
                sem = 3 * wi + k
                sends.append(_remote(p_refs[wi].at[2 * ox + oy], land_refs[wi].at[k], ss.at[sem], rs.at[sem], (ox, oy, c)))
                recvs.append(_remote(p_refs[wi].at[me], land_refs[wi].at[k], ss.at[sem], rs.at[sem], (x, y, c)))
        return sends, recvs

    return plan


SUM_BLOCK_BYTES = 4 << 20


def _rows_per_block(n, c, limit_bytes=2 << 20):
    best = None
    for tm in range(16, n + 1, 16):
        if n % tm == 0 and tm * c * 4 <= limit_bytes:
            best = tm
    return best or n


def _sum_cores(grad, got, kind, place, name):
    _, hr, cw = got.shape
    tm = _rows_per_block(hr, cw, SUM_BLOCK_BYTES)
    nb = hr // tm

    def body(place_ref, g_ref, t_ref, o_ref):
        o_ref[...] = (g_ref[...].astype(F32) + t_ref[...].astype(F32)).astype(o_ref.dtype)

    if kind == "col":
        g_spec = pl.BlockSpec((tm, cw), lambda j, i, pr: (pr[0] * nb + i, j))
    else:
        g_spec = pl.BlockSpec((tm, cw), lambda j, i, pr: ((2 * j + pr[0]) * nb + i, 0))
    blk = pl.BlockSpec((None, tm, cw), lambda j, i, pr: (j, i, 0))
    return _pcall(
        body, name=name, out_shape=jax.ShapeDtypeStruct(got.shape, BF16),
        grid_spec=pltpu.PrefetchScalarGridSpec(num_scalar_prefetch=1, grid=(N_CHIPS, nb), in_specs=[g_spec, blk], out_specs=blk),
        compiler_params=_params("parallel", "parallel"),
    )(place, grad, got)


def _sum_chips(parts, got, place, name):
    _, n, cw = got.shape
    tm = _rows_per_block(n, cw, SUM_BLOCK_BYTES)

    def body(place_ref, p_ref, g_ref, o_ref):
        tot = p_ref[...].astype(F32)
        for k in range(3):
            tot = tot + g_ref[k].astype(F32)
        o_ref[...] = tot

    return _pcall(
        body, name=name, out_shape=jax.ShapeDtypeStruct((2, n, cw), F32),
        grid_spec=pltpu.PrefetchScalarGridSpec(
            num_scalar_prefetch=1, grid=(n // tm,),
            in_specs=[pl.BlockSpec((None, tm, cw), lambda i, pr: (pr[1], i, 0)), pl.BlockSpec((3, tm, cw), lambda i, pr: (0, i, 0))],
            out_specs=pl.BlockSpec((None, tm, cw), lambda i, pr: (pr[0], i, 0))),
        compiler_params=_params("parallel"),
    )(place, parts, got)


def _adamw(g, w, m, v, name):
    n, c = g.shape
    c1 = 1.0 - ADAM_B1 ** ADAM_STEP
    c2 = 1.0 - ADAM_B2 ** ADAM_STEP

    def fn(gb, wb, mb, vb):
        m_new = ADAM_B1 * mb + (1.0 - ADAM_B1) * gb
        v_new = ADAM_B2 * vb + (1.0 - ADAM_B2) * (gb * gb)
        delta = -ADAM_LR * ((m_new / c1) / (jnp.sqrt(v_new / c2) + ADAM_EPS) + ADAM_WD * wb)
        return gb, delta, m_new, v_new

    tm = _rows_per_block(n, c) if n % 16 == 0 else n
    return _rowcall(fn, [_whole(g), _whole(w), _whole(m), _whole(v)], [], [(c, F32)] * 4, tm=tm, name=name)


PACK_ROWS = 16


def _pack_rows(parts, width, name, after=None):
    assert sum(p.shape[0] for p in parts) <= PACK_ROWS

    def body(*refs):
        out_ref = refs[-1]
        out_ref[...] = jnp.zeros_like(out_ref)
        at = 0
        for r in refs[:len(parts)]:
            k, n = r.shape
            if n == width:
                out_ref[at:at + k, :] = r[...]
            else:
                out_ref[at:at + k, :] = jnp.broadcast_to(r[:, :1], (k, width))
            at += k

    vm = pl.BlockSpec(memory_space=pltpu.VMEM)
    return _pcall(body, name=name, in_specs=[vm] * len(parts) + ([] if after is None else [ANY]), out_specs=vm,
                  out_shape=jax.ShapeDtypeStruct((PACK_ROWS, width), F32))(*parts, *([] if after is None else [after]))


def _cast_shard(wm, name, after):
    n, c = wm.shape
    return _rowcall(lambda v: v, [_whole(wm)], [], [(c, BF16)], tm=_rows_per_block(n, c), name=name, after=after)[0]


GATHER_GROUPS = (
    ("w_ffn1_gu", "conv_w"), ("w_ffn1_down",), ("w_in",), ("w_conv_out", "w_attn_out", "w_o"), ("w_cq", "w_ckv", "w_co"),
    ("w_ffn2_gu", "w_ffn2_down"),
)
REDUCE_GROUPS = {
    "ffn2": ("w_ffn2_down", "w_ffn2_gu"),
    "cross": ("w_co", "w_cq", "w_ckv"),
    "mix": ("w_o", "w_conv_out", "w_attn_out", "w_in"),
    "ffn1_down": ("w_ffn1_down",),
    "ffn1": ("w_ffn1_gu",),
}
TAIL_STAGES = (("ffn2", "cross"), ("mix",), ("ffn1_down", "ffn1"))
KIND = dict(MATS)


def _step(x, mem, tgt, wts, m_in, v_in):
    d = x.shape[-1]
    cc = wts["conv_w"].shape[1]
    place = jnp.stack([lax.axis_index("c"), 2 * lax.axis_index("x") + lax.axis_index("y")]).astype(jnp.int32)
    dims = {n: (kind, *wts[n].shape) for n, kind in MATS}
    dims["conv_w"] = ("col", CONV_ROWS, cc)

    w = {n: wts[n].reshape(1, -1) for n in VECS + ("b_gate",)}
    flying, token = {}, None
    for names in GATHER_GROUPS:
        gd = [dims[n] for n in names]
        shards = [jnp.pad(wts[n], ((0, CONV_ROWS - CONV_K), (0, 0))) if n == "conv_w" else _cast_shard(wts[n], "cast_" + n, token)
                  for n in names]
        lands = [lax.empty(_full_shape(*dm), sh.dtype) for dm, sh in zip(gd, shards)]
        plan = _gather_plan(gd)
        ss, rs, srcs, lands, token = _split_start("gather_start_" + names[0], plan, 4 * len(names), shards, lands, token)
        flying.update({n: (names, plan, ss, rs, srcs, lands, gd) for n in names})

    passing = {}

    def prefetch(name, after):
        if name not in passing:
            names, plan, ss, rs, srcs, lands, gd = flying[name]
            _, lands = _split_wait("gather_wait_" + names[0], plan, ss, rs, srcs, lands, after)
            plan = _forward_plan(gd)
            ss, rs, _, lands, _ = _split_start("forward_start_" + names[0], plan, 3 * len(names), [], lands)
            passing.update({n: (names, plan, ss, rs, lands) for n in names})

    def fetch(name, after):
        prefetch(name, after)
        names, plan, ss, rs, lands = passing[name]
        _, lands = _split_wait("forward_wait_" + names[0], plan, ss, rs, [], lands, after)
        return {n: (land[:CONV_K] if n == "conv_w" else land) for n, land in zip(names, lands)}

    swapping, sent = {}, {}

    def emit(tag, g):
        if tag not in REDUCE_GROUPS:
            return None
        names = REDUCE_GROUPS[tag]
        gd = [dims[n] for n in names]
        lands = [lax.empty((N_CHIPS, r // 2, cw), BF16) for (_, r, cw) in gd]
        plan = _rs_cores_plan(gd)
        ss, rs, srcs, lands, tok = _split_start("rs_cores_start_" + tag, plan, N_CHIPS * len(names), [g[n] for n in names], lands)
        swapping[tag] = (plan, ss, rs, srcs, lands)
        return tok

    def tick(tag, after):
        if tag not in REDUCE_GROUPS:
            return None
        names = REDUCE_GROUPS[tag]
        plan, ss, rs, srcs, lands = swapping[tag]
        mine, got = _split_wait("rs_cores_wait_" + tag, plan, ss, rs, srcs, lands, after)
        parts = [_sum_cores(gm, t, KIND[n], place, "sum_cores_" + n) for n, gm, t in zip(names, mine, got)]
        lands = [lax.empty((3, *p.shape[1:]), BF16) for p in parts]
        plan = _rs_chips_plan(len(names))
        ss, rs, srcs, lands, tok = _split_start("rs_chips_start_" + tag, plan, 3 * len(names), parts, lands)
        sent[tag] = (plan, ss, rs, srcs, lands)
        return tok

    loss_lanes, dx, g, last = _local_step(x[0], mem[0], tgt[0], w, fetch, prefetch, emit, tick, token)

    rows = [g[n] for n in VECS] + [g["b_gate"][:, :d], g["b_gate"][:, d:], g["conv_w"], loss_lanes]
    packed = _pack_rows(rows, d, "pack_small", after=last)
    small = jnp.concatenate([packed[None], jnp.zeros((N_DEV - 1, *packed.shape), F32)], axis=0)
    small_plan = _small_plan()
    small_ss, small_rs, _, small, after = _split_start("small_start", small_plan, N_DEV - 1, [], [small])

    grads, out = {}, {}

    def update(n):
        shape = wts[n].shape
        as2d = (lambda a: a.reshape(1, -1)) if len(shape) == 1 else (lambda a: a)
        return [r.reshape(shape) for r in _adamw(grads[n], as2d(wts[n]), as2d(m_in[n]), as2d(v_in[n]), "adamw_" + n)]

    def finish(sharing, after):
        tag, names, plan, ss, rs, halves = sharing
        _, both = _split_wait("share_wait_" + tag, plan, ss, rs, [], halves, after)
        for n, b in zip(names, both):
            grads[n] = b.reshape(-1, b.shape[-1])
            out[n] = update(n)
        return out[names[-1]][1]

    sharing = None
    for stage in TAIL_STAGES:
        names, halves = [], []
        for tag in stage:
            plan, ss, rs, srcs, lands = sent[tag]
            parts, landed = _split_wait("rs_chips_wait_" + tag, plan, ss, rs, srcs, lands, after)
            halves += [_sum_chips(p, t, place, "sum_chips_" + n) for n, p, t in zip(REDUCE_GROUPS[tag], parts, landed)]
            names += REDUCE_GROUPS[tag]
        plan = _share_plan(len(names))
        ss, rs, _, halves, after = _split_start("share_start_" + stage[0], plan, len(names), [], halves)
        if sharing is not None:
            after = finish(sharing, after)
        sharing = (stage[0], names, plan, ss, rs, halves)
    after = finish(sharing, after)

    _, small = _split_wait("small_wait", small_plan, small_ss, small_rs, [], small, after)
    me = (4 * lax.axis_index("x") + 2 * lax.axis_index("y") + lax.axis_index("c")).astype(jnp.int32).reshape(1)
    red = _sum_small(small[0], me, "sum_small")
    grads.update({n: red[i:i + 1] for i, n in enumerate(VECS)})
    nv = len(VECS)
    grads["b_gate"] = jnp.concatenate([red[nv:nv + 1], red[nv + 1:nv + 2]], axis=1)
    chip = 2 * lax.axis_index("x") + lax.axis_index("y")
    grads["conv_w"] = lax.dynamic_slice_in_dim(red[nv + 2:nv + 2 + CONV_K], chip * cc, cc, axis=1)
    loss = red[nv + 2 + CONV_K, 0]
    out.update({n: update(n) for n in WEIGHTS if n not in KIND})
    return (loss, dx[None], *[out[n][0] for n in WEIGHTS], *[out[n][1] for n in WEIGHTS],
            *[out[n][2] for n in WEIGHTS], *[out[n][3] for n in WEIGHTS])


def kernel(x, mem, g_ffn1, w_ffn1_gu, w_ffn1_down, g_mix, w_in, b_gate, conv_w, w_conv_out, w_attn_out, w_o, g_cross, g_mem, w_cq, w_ckv, w_co, g_ffn2, w_ffn2_gu, w_ffn2_down, g_final, loss_target, m_g_ffn1, m_w_ffn1_gu, m_w_ffn1_down, m_g_mix, m_w_in, m_b_gate, m_conv_w, m_w_conv_out, m_w_attn_out, m_w_o, m_g_cross, m_g_mem, m_w_cq, m_w_ckv, m_w_co, m_g_ffn2, m_w_ffn2_gu, m_w_ffn2_down, m_g_final, v_g_ffn1, v_w_ffn1_gu, v_w_ffn1_down, v_g_mix, v_w_in, v_b_gate, v_conv_w, v_w_conv_out, v_w_attn_out, v_w_o, v_g_cross, v_g_mem, v_w_cq, v_w_ckv, v_w_co, v_g_ffn2, v_w_ffn2_gu, v_w_ffn2_down, v_g_final):
    given = dict(locals())
    wts = {n: given[n] for n in WEIGHTS}
    m_in = {n: given["m_" + n] for n in WEIGHTS}
    v_in = {n: given["v_" + n] for n in WEIGHTS}
    return _step(x, mem, loss_target, wts, m_in, v_in)
```

```python
import math

import jax
import jax.numpy as jnp
from jax import lax
from jax.experimental import pallas as pl
from jax.experimental.pallas import tpu as pltpu

F32 = jnp.float32
BF16 = jnp.bfloat16
MESH = pl.DeviceIdType.MESH

V7X_VMEM_LIMIT_BYTES = 48 * 1024 * 1024
MM_VMEM_BUDGET_BYTES = 36 * 1024 * 1024
MM_WHOLE_K = 2816
LANES = 128
SB_HEAD_DIM = 128
X_HEADS = 4
CONV_K = 3
RMS_EPS = 1e-6
N_CHIPS = 4
N_DEV = 8
ADAM_LR, ADAM_B1, ADAM_B2, ADAM_EPS, ADAM_WD, ADAM_STEP = 0.001, 0.9, 0.999, 1e-08, 0.01, 10


ANY = pl.BlockSpec(memory_space=pl.ANY)


def _pcall(body, **kw):
    return pl.pallas_call(body, **kw)


def _params(*sem):
    return pltpu.CompilerParams(dimension_semantics=sem, vmem_limit_bytes=V7X_VMEM_LIMIT_BYTES)


def _pick(dim, cands):
    for c in cands:
        if dim % c == 0:
            return c
    return dim


def _dot(a, b, ca, cb):
    return lax.dot_general(a, b, (((ca,), (cb,)), ((), ())), preferred_element_type=F32)


def _mm(a, b, *, name, ta=False, tb=False, out_dtype=BF16, res=None, alpha=1.0, tm=None, tn=None, tk=None, after=None,
        a_halves=False, b_halves=False, norm_g=None):
    assert not (a_halves and ta) and not (b_halves and tb) and not (norm_g is not None and ta)
    if a_halves:
        m, k = a.shape[1], 2 * a.shape[2]
    else:
        m, k = (a.shape[1], a.shape[0]) if ta else a.shape
    if b_halves:
        n = 2 * b.shape[2]
        assert k == b.shape[1]
    else:
        n = b.shape[0] if tb else b.shape[1]
        assert k == (b.shape[1] if tb else b.shape[0]), (a.shape, b.shape, ta, tb)
    if ta:
        tm = tm or _pick(m, (512, 256, 128))
        tn = tn or _pick(n, (1024, 512, 256, 128))
        tk = tk or (k if k <= MM_WHOLE_K else _pick(k, (1024, 512, 256, 128)))
    else:
        tk = tk or (k if k <= MM_WHOLE_K else _pick(k, (MM_WHOLE_K, 2048, 1024, 512, 256, 128)))
        tn = tn or (n if norm_g is not None else _pick(n, (512, 1408, 256, 128) if tk == k else (1024, 512, 256, 128)))
        out_bytes = jnp.dtype(out_dtype).itemsize + (0 if res is None else res.dtype.itemsize) + (0 if norm_g is None else 2)
        per_row = 2 * (tk * a.dtype.itemsize + tn * out_bytes)
        per_row += 4 * tn if tk < k else 0
        rows = (MM_VMEM_BUDGET_BYTES - 2 * tk * tn * b.dtype.itemsize) // per_row
        tm = tm or next((c for c in (2048, 1024, 512, 256, 128) if m % c == 0 and c <= rows), m)
    if a_halves:
        tk = min(tk, k // 2) if (k // 2) % min(tk, k // 2) == 0 else _pick(k // 2, (1408, 1024, 512, 256, 128))
    if b_halves:
        tn = tn if (n // 2) % tn == 0 else _pick(n // 2, (1408, 1024, 512, 256, 128))
    nk = k // tk
    assert m % tm == 0 and n % tn == 0 and k % tk == 0
    a_spec = pl.BlockSpec((tk, tm), lambda i, j, kk: (kk, i)) if ta else pl.BlockSpec((tm, tk), lambda i, j, kk: (i, kk))
    b_spec = pl.BlockSpec((tn, tk), lambda i, j, kk: (j, kk)) if tb else pl.BlockSpec((tk, tn), lambda i, j, kk: (kk, j))
    if a_halves:
        per = (k // 2) // tk
        a_spec = pl.BlockSpec((None, tm, tk), lambda i, j, kk: (kk // per, i, kk % per))
    if b_halves:
        per_n = (n // 2) // tn
        b_spec = pl.BlockSpec((None, tk, tn), lambda i, j, kk: (j // per_n, kk, j % per_n))
    o_spec = pl.BlockSpec((tm, tn), lambda i, j, kk: (i, j))
    ca, cb = (0 if ta else 1), (1 if tb else 0)

    n_in = 2 + (res is not None) + (norm_g is not None) + (after is not None)
    n_out = 1 + (norm_g is not None)

    def body(*refs):
        a_ref, b_ref = refs[:2]
        res_ref = refs[2] if res is not None else None
        g_ref = refs[2 + (res is not None)] if norm_g is not None else None
        o_ref = refs[n_in]
        scratch = refs[n_in + n_out:]

        def finish(acc):
            val = acc if alpha == 1.0 else alpha * acc
            if res_ref is not None:
                val = res_ref[...].astype(F32) + val
            o_ref[...] = val.astype(o_ref.dtype)
            if g_ref is not None:
                refs[n_in + 1][...] = (_xhat(val)[0] * g_ref[...]).astype(BF16)

        part = _dot(a_ref[...].astype(BF16), b_ref[...].astype(BF16), ca, cb)
        if nk == 1:
            finish(part)
        else:
            acc_ref = scratch[0]
            kk = pl.program_id(2)

            @pl.when(kk == 0)
            def _():
                acc_ref[...] = part

            @pl.when(kk > 0)
            def _():
                acc_ref[...] += part

            @pl.when(kk == nk - 1)
            def _():
                finish(acc_ref[...])

    ins = [a, b] + ([] if res is None else [res]) + ([] if norm_g is None else [norm_g]) + ([] if after is None else [after])
    in_specs = [a_spec, b_spec] + ([] if res is None else [o_spec])
    in_specs += ([] if norm_g is None else [pl.BlockSpec((1, tn), lambda i, j, kk: (0, j))]) + ([] if after is None else [ANY])
    outs = _pcall(
        body, name=name, grid=(m // tm, n // tn, nk), in_specs=in_specs, out_specs=[o_spec] * n_out,
        out_shape=[jax.ShapeDtypeStruct((m, n), out_dtype)] + [jax.ShapeDtypeStruct((m, n), BF16)] * (n_out - 1),
        scratch_shapes=[pltpu.VMEM((tm, tn), F32)] if nk > 1 else [],
        compiler_params=_params("parallel", "parallel", "arbitrary"),
    )(*ins)
    return outs[0] if norm_g is None else outs


def _rowcall(fn, rows, consts, outs, accs=(), *, tm, name, after=None):
    s = rows[0][0].shape[0]
    assert s % tm == 0
    n_read, n_out = len(rows) + len(consts), len(outs)
    n_in = n_read + (after is not None)

    def body(*refs):
        vals = fn(*[r[...] for r in refs[:n_read]])
        vals = vals if isinstance(vals, (tuple, list)) else (vals,)
        for o_ref, v in zip(refs[n_in:n_in + n_out], vals[:n_out]):
            o_ref[...] = v.astype(o_ref.dtype)
        if accs:
            first = pl.program_id(0) == 0
            for a_ref, v in zip(refs[n_in + n_out:], vals[n_out:]):
                tot = jnp.sum(v.astype(F32), axis=0, keepdims=True)

                @pl.when(first)
                def _(a_ref=a_ref, tot=tot):
                    a_ref[...] = tot

                @pl.when(jnp.logical_not(first))
                def _(a_ref=a_ref, tot=tot):
                    a_ref[...] += tot

    in_specs = [pl.BlockSpec((tm, w), lambda i, cb=cb: (i, cb)) for (_, cb, w) in rows]
    in_specs += [pl.BlockSpec(c.shape, lambda i: (0, 0)) for c in consts]
    in_specs += [] if after is None else [ANY]
    out_specs = [pl.BlockSpec((tm, w), lambda i: (i, 0)) for (w, _) in outs]
    out_specs += [pl.BlockSpec((1, w), lambda i: (0, 0)) for w in accs]
    out_shape = [jax.ShapeDtypeStruct((s, w), dt) for (w, dt) in outs]
    out_shape += [jax.ShapeDtypeStruct((1, w), F32) for w in accs]
    return _pcall(
        body, name=name, grid=(s // tm,), in_specs=in_specs, out_specs=out_specs, out_shape=out_shape,
        compiler_params=_params("arbitrary" if accs else "parallel"),
    )(*[r[0] for r in rows], *consts, *([] if after is None else [after]))


def _whole(a):
    return (a, 0, a.shape[1])


def _xhat(x):
    x = x.astype(F32)
    r = lax.rsqrt(jnp.mean(x * x, axis=-1, keepdims=True) + RMS_EPS)
    return x * r, r


def _rms_bwd(dy, x, g):
    xh, r = _xhat(x)
    dxh = dy.astype(F32) * g
    dx = r * (dxh - xh * jnp.mean(dxh * xh, axis=-1, keepdims=True))
    return dx, dy.astype(F32) * xh


def _sigmoid(x):
    return 1.0 / (1.0 + jnp.exp(-x))


def _rms_fwd(x, g, name, tm, after=None):
    d = x.shape[1]
    return _rowcall(lambda xb, gb: _xhat(xb)[0] * gb, [_whole(x)], [g], [(d, BF16)], tm=tm, name=name, after=after)[0]


def _silu_parts(gate):
    sg = _sigmoid(gate)
    return sg, gate * sg


def _ffn_up(n, w_gu, name):
    s, d = n.shape
    f = w_gu.shape[1] // 2
    tn = _pick(f, (1408, 1024, 512, 256, 128))
    tm = _pick(s, (1024, 512, 256, 128))
    nb = f // tn

    def body(n_ref, wg_ref, wu_ref, gu_ref, act_ref):
        nv = n_ref[...]
        gate = _dot(nv, wg_ref[...], 1, 0)
        up = _dot(nv, wu_ref[...], 1, 0)
        gu_ref[0] = gate.astype(gu_ref.dtype)
        gu_ref[1] = up.astype(gu_ref.dtype)
        act_ref[...] = (_silu_parts(gate)[1] * up).astype(act_ref.dtype)

    return _pcall(
        body, name=name, grid=(s // tm, nb),
        in_specs=[pl.BlockSpec((tm, d), lambda i, j: (i, 0)), pl.BlockSpec((d, tn), lambda i, j: (0, j)),
                  pl.BlockSpec((d, tn), lambda i, j: (0, nb + j))],
        out_specs=[pl.BlockSpec((2, tm, tn), lambda i, j: (0, i, j)), pl.BlockSpec((tm, tn), lambda i, j: (i, j))],
        out_shape=[jax.ShapeDtypeStruct((2, s, f), BF16), jax.ShapeDtypeStruct((s, f), BF16)],
        compiler_params=_params("parallel", "parallel"),
    )(n, w_gu, w_gu)


def _ffn_dgu(dhb, w_down, gu, name, after=None):
    s, d = dhb.shape
    f = w_down.shape[0]
    tn = _pick(f, (1408, 1024, 512, 256, 128))
    tm = _pick(s, (1024, 512, 256, 128))

    def body(dh_ref, w_ref, gu_ref, *rest):
        o_ref = rest[-1]
        dact = _dot(dh_ref[...], w_ref[...], 1, 1)
        gate, up = gu_ref[0].astype(F32), gu_ref[1].astype(F32)
        sg, silu = _silu_parts(gate)
        o_ref[0] = (dact * up * (sg + silu * (1.0 - sg))).astype(o_ref.dtype)
        o_ref[1] = (dact * silu).astype(o_ref.dtype)

    blk = pl.BlockSpec((2, tm, tn), lambda i, j: (0, i, j))
    return _pcall(
        body, name=name, grid=(s // tm, f // tn),
        in_specs=[pl.BlockSpec((tm, d), lambda i, j: (i, 0)), pl.BlockSpec((tn, d), lambda i, j: (j, 0)), blk]
        + ([] if after is None else [ANY]),
        out_specs=blk, out_shape=jax.ShapeDtypeStruct((2, s, f), BF16), compiler_params=_params("parallel", "parallel"),
    )(dhb, w_down, gu, *([] if after is None else [after]))


def _dgrad_norm(dy, wmat, dh, x, g, name, *, dy_halves=False, copy_scale=None, after=None):
    s, d = dh.shape
    k = wmat.shape[1]
    tk = k if k <= MM_WHOLE_K else _pick(k, (MM_WHOLE_K, 2048, 1024, 512, 256, 128))
    if dy_halves and (k // 2) % tk:
        tk = _pick(k // 2, (1408, 1024, 512, 256, 128))
    tm = _pick(s, (512, 256, 128))
    nk, per = k // tk, (k // 2) // tk if dy_halves else 0
    n_in = 5 + (after is not None)
    n_out = 2 + (copy_scale is not None)

    def body(*refs):
        dy_ref, w_ref, dh_ref, x_ref, g_ref = refs[:5]
        outs, scratch = refs[n_in:n_in + n_out], refs[n_in + n_out:]
        i, kk = pl.program_id(0), pl.program_id(1)
        part = _dot(dy_ref[...], w_ref[...], 1, 1)

        def finish(dn):
            dx, dg = _rms_bwd(dn, x_ref[...], g_ref[...])
            tot = dh_ref[...] + dx
            outs[0][...] = tot
            if copy_scale is not None:
                outs[1][...] = (copy_scale * tot).astype(outs[1].dtype)
            dg = jnp.sum(dg, axis=0, keepdims=True)

            @pl.when(i == 0)
            def _():
                outs[-1][...] = dg

            @pl.when(i > 0)
            def _():
                outs[-1][...] += dg

        if nk == 1:
            finish(part)
        else:
            acc_ref = scratch[0]

            @pl.when(kk == 0)
            def _():
                acc_ref[...] = part

            @pl.when(kk > 0)
            def _():
                acc_ref[...] += part

            @pl.when(kk == nk - 1)
            def _():
                finish(acc_ref[...])

    row = pl.BlockSpec((tm, d), lambda i, kk: (i, 0))
    dy_spec = pl.BlockSpec((None, tm, tk), lambda i, kk: (kk // per, i, kk % per)) if dy_halves else pl.BlockSpec((tm, tk), lambda i, kk: (i, kk))
    in_specs = [dy_spec, pl.BlockSpec((d, tk), lambda i, kk: (0, kk)), row, row, pl.BlockSpec((1, d), lambda i, kk: (0, 0))]
    out_specs = [row] * (n_out - 1) + [pl.BlockSpec((1, d), lambda i, kk: (0, 0))]
    out_shape = [jax.ShapeDtypeStruct((s, d), F32)] + ([] if copy_scale is None else [jax.ShapeDtypeStruct((s, d), BF16)])
    return _pcall(
        body, name=name, grid=(s // tm, nk), in_specs=in_specs + ([] if after is None else [ANY]), out_specs=out_specs,
        out_shape=out_shape + [jax.ShapeDtypeStruct((1, d), F32)], scratch_shapes=[pltpu.VMEM((tm, d), F32)] if nk > 1 else [],
        compiler_params=_params("arbitrary", "arbitrary"),
    )(dy, wmat, dh, x, g, *([] if after is None else [after]))


def _shift_down(p, k):
    if k == 0:
        return p
    rows = lax.broadcasted_iota(jnp.int32, p.shape, 0)
    return jnp.where(rows >= k, pltpu.roll(p, k, 0), 0.0)


def _shift_up(p, k):
    if k == 0:
        return p
    s = p.shape[0]
    rows = lax.broadcasted_iota(jnp.int32, p.shape, 0)
    return jnp.where(rows < s - k, pltpu.roll(p, s - k, 0), 0.0)


def _conv_fwd(proj, conv_w, d, tc, name):
    s = proj.shape[0]
    nb = d // tc

    def body(cb_ref, cc_ref, cx_ref, w_ref, y_ref):
        p = cc_ref[...].astype(F32) * cx_ref[...].astype(F32)
        w = w_ref[...]
        acc = p * w[CONV_K - 1:CONV_K, :]
        for k in range(1, CONV_K):
            acc = acc + _shift_down(p, k) * w[CONV_K - 1 - k:CONV_K - k, :]
        y_ref[...] = (cb_ref[...].astype(F32) * acc).astype(y_ref.dtype)

    col = lambda off: pl.BlockSpec((s, tc), lambda j: (0, off * nb + j))
    return _pcall(
        body, name=name, grid=(nb,), in_specs=[col(0), col(1), col(2), pl.BlockSpec((CONV_K, tc), lambda j: (0, j))],
        out_specs=pl.BlockSpec((s, tc), lambda j: (0, j)), out_shape=jax.ShapeDtypeStruct((s, d), BF16),
        compiler_params=_params("parallel"),
    )(proj, proj, proj, conv_w)


def _conv_bwd(dy, proj, conv_w, d, tc, name):
    s = proj.shape[0]
    nb = d // tc

    def body(dy_ref, cb_ref, cc_ref, cx_ref, w_ref, dcb_ref, dcc_ref, dcx_ref, dw_ref):
        cc, cx = cc_ref[...].astype(F32), cx_ref[...].astype(F32)
        p = cc * cx
        w = w_ref[...]
        dyv = dy_ref[...].astype(F32)
        shifted = [_shift_down(p, CONV_K - 1 - k) for k in range(CONV_K)]
        conv = shifted[0] * w[0:1, :]
        for k in range(1, CONV_K):
            conv = conv + shifted[k] * w[k:k + 1, :]
        dcb_ref[...] = (dyv * conv).astype(dcb_ref.dtype)
        ds = dyv * cb_ref[...].astype(F32)
        dp = ds * w[CONV_K - 1:CONV_K, :]
        for k in range(1, CONV_K):
            dp = dp + _shift_up(ds, k) * w[CONV_K - 1 - k:CONV_K - k, :]
        dcc_ref[...] = (dp * cx).astype(dcc_ref.dtype)
        dcx_ref[...] = (dp * cc).astype(dcx_ref.dtype)
        for k in range(CONV_K):
            dw_ref[k:k + 1, :] = jnp.sum(ds * shifted[k], axis=0, keepdims=True)

    col = lambda off: pl.BlockSpec((s, tc), lambda j: (0, off * nb + j))
    blk = pl.BlockSpec((s, tc), lambda j: (0, j))
    wblk = pl.BlockSpec((CONV_K, tc), lambda j: (0, j))
    act = jax.ShapeDtypeStruct((s, d), BF16)
    return _pcall(
        body, name=name, grid=(nb,), in_specs=[blk, col(0), col(1), col(2), wblk],
        out_specs=[blk, blk, blk, wblk], out_shape=[act, act, act, jax.ShapeDtypeStruct((CONV_K, d), F32)],
        compiler_params=_params("parallel"),
    )(dy, proj, proj, proj, conv_w)


def _sb_tile(q, kj, scale, carry, tri, mask):
    z = _dot(q, kj, 1, 1) * scale
    lsz = jnp.minimum(z, 0.0) - jnp.log(1.0 + jnp.exp(-jnp.abs(z)))
    l1m = lsz - z
    if mask is not None:
        l1m = jnp.where(mask, l1m, 0.0)
    l1b = l1m.astype(BF16)
    a = jnp.exp(lsz + (carry + _dot(l1b, tri, 1, 0)))
    if mask is not None:
        a = jnp.where(mask, a, 0.0)
    return lsz, l1b, a.astype(BF16)


def _add_rows(x, upd, r0):
    return x + upd if r0 == 0 else jnp.concatenate([x[:r0], x[r0:] + upd], axis=0)


def _sb_masks(tq, tk):
    row = lax.broadcasted_iota(jnp.int32, (tq, tk), 0)
    col = lax.broadcasted_iota(jnp.int32, (tq, tk), 1)
    masks = [col + dj * tk < row for dj in range(tq // tk)]
    r2 = lax.broadcasted_iota(jnp.int32, (tk, tk), 0)
    c2 = lax.broadcasted_iota(jnp.int32, (tk, tk), 1)
    return masks, (r2 > c2).astype(BF16), (r2 < c2).astype(BF16)


def _sb_fwd(proj, heads, col0, tq, tk, name):
    s = proj.shape[0]
    dh = SB_HEAD_DIM
    nq, nd, nkt = s // tq, tq // tk, s // tk
    scale = dh ** -0.5

    def body(q_ref, k_ref, v_ref, o_ref, a_ref, b_ref):
        i = pl.program_id(1)
        q = q_ref[...]
        masks, tri_right, _ = _sb_masks(tq, tk)

        def tile(j, carry, acc, mask, r0=0):
            start = pl.multiple_of(j * tk, tk)
            kj = k_ref[pl.ds(start, tk), :]
            vj = v_ref[pl.ds(start, tk), :]
            lsz, l1b, ab = _sb_tile(q[r0:], kj, scale, carry[r0:], tri_right, None if mask is None else mask[r0:])
            a_ref[j, r0:, :] = ab
            b_ref[j, r0:, :] = jnp.exp(lsz).astype(b_ref.dtype)
            if r0:
                a_ref[j, :r0, :] = jnp.zeros((r0, tk), a_ref.dtype)
                b_ref[j, :r0, :] = jnp.zeros((r0, tk), b_ref.dtype)
            return (_add_rows(carry, jnp.sum(l1b.astype(F32), axis=1, keepdims=True), r0),
                    _add_rows(acc, _dot(ab, vj, 1, 0), r0))

        state = (jnp.zeros((tq, 1), F32), jnp.zeros((tq, dh), F32))
        for dj in reversed(range(nd)):
            state = tile(i * nd + dj, *state, masks[dj], dj * tk)
        def left_block(t, st):
            for dj in reversed(range(nd)):
                st = tile((i - 1 - t) * nd + dj, st[0], st[1], None)
            return st

        state = lax.fori_loop(0, i, left_block, state)
        o_ref[...] = state[1]

    qspec = pl.BlockSpec((tq, dh), lambda h, i: (i, col0[0] + h))
    kspec = pl.BlockSpec((s, dh), lambda h, i: (0, col0[1] + h))
    vspec = pl.BlockSpec((s, dh), lambda h, i: (0, col0[2] + h))
    saved = pl.BlockSpec((None, nkt, tq, tk), lambda h, i: (h, 0, i, 0))
    saved_shape = jax.ShapeDtypeStruct((heads, nkt, s, tk), BF16)
    return _pcall(
        body, name=name, grid=(heads, nq), in_specs=[qspec, kspec, vspec],
        out_specs=[pl.BlockSpec((tq, dh), lambda h, i: (i, h)), saved, saved],
        out_shape=[jax.ShapeDtypeStruct((s, heads * dh), F32), saved_shape, saved_shape],
        compiler_params=_params("parallel", "parallel"),
    )(proj, proj, proj)


SB_BWD_HEADS = 2


def _sb_bwd(proj, o, a_all, beta_all, do, heads, col0, tq, tk, name):
    s = proj.shape[0]
    dh = SB_HEAD_DIM
    nq, nd, nkt = s // tq, tq // tk, s // tk
    scale = dh ** -0.5
    hb = SB_BWD_HEADS if heads % SB_BWD_HEADS == 0 and all(c % SB_BWD_HEADS == 0 for c in col0) else 1
    wide = hb * dh

    def body(q_ref, k_ref, v_ref, o_ref, a_ref, b_ref, do_ref, dq_ref, dk_ref, dv_ref, dk_acc, dv_acc):
        i = pl.program_id(1)

        @pl.when(i == 0)
        def _():
            dk_acc[...] = jnp.zeros_like(dk_acc)
            dv_acc[...] = jnp.zeros_like(dv_acc)

        lanes = [slice(hh * dh, (hh + 1) * dh) for hh in range(hb)]
        q = [q_ref[:, ln] for ln in lanes]
        dob = [do_ref[:, ln].astype(BF16) for ln in lanes]
        delta = [jnp.sum(dob[hh].astype(F32) * o_ref[:, lanes[hh]], axis=1, keepdims=True) for hh in range(hb)]
        masks, _, tri_left = _sb_masks(tq, tk)

        def tile(hh, j, carry_g, dq, mask):
            start = pl.multiple_of(j * tk, tk)
            kj = k_ref[pl.ds(start, tk), lanes[hh]]
            vj = v_ref[pl.ds(start, tk), lanes[hh]]
            ab = a_ref[hh, j]
            g = _dot(dob[hh], vj, 1, 1) * ab.astype(F32)
            carry_g = carry_g + jnp.sum(g, axis=1, keepdims=True)
            left = (delta[hh] - carry_g) + _dot(g.astype(BF16), tri_left, 1, 0)
            dz = g - b_ref[hh, j].astype(F32) * (g + left)
            if mask is not None:
                dz = jnp.where(mask, dz, 0.0)
            dzb = dz.astype(BF16)
            dk_acc[pl.ds(start, tk), lanes[hh]] += _dot(dzb, q[hh], 0, 0)
            dv_acc[pl.ds(start, tk), lanes[hh]] += _dot(ab, dob[hh], 0, 0)
            return carry_g, dq + _dot(dzb, kj, 1, 0)

        def block(jb, st, use_masks):
            st = list(st)
            for dj in reversed(range(nd)):
                for hh in range(hb):
                    st[hh] = tile(hh, jb * nd + dj, *st[hh], masks[dj] if use_masks else None)
            return tuple(st)

        state = block(i, tuple((jnp.zeros((tq, 1), F32), jnp.zeros((tq, dh), F32)) for _ in range(hb)), True)
        state = lax.fori_loop(0, i, lambda t, st: block(i - 1 - t, st, False), state)
        for hh in range(hb):
            dq_ref[:, lanes[hh]] = (state[hh][1] * scale).astype(dq_ref.dtype)

        @pl.when(i == nq - 1)
        def _():
            dk_ref[...] = (dk_acc[...] * scale).astype(dk_ref.dtype)
            dv_ref[...] = dv_acc[...].astype(dv_ref.dtype)

    qspec = pl.BlockSpec((tq, wide), lambda h, i: (i, col0[0] // hb + h))
    kspec = pl.BlockSpec((s, wide), lambda h, i: (0, col0[1] // hb + h))
    vspec = pl.BlockSpec((s, wide), lambda h, i: (0, col0[2] // hb + h))
    blk = pl.BlockSpec((tq, wide), lambda h, i: (i, h))
    full = pl.BlockSpec((s, wide), lambda h, i: (0, h))
    saved = pl.BlockSpec((hb, nkt, tq, tk), lambda h, i: (h, 0, i, 0))
    act = jax.ShapeDtypeStruct((s, heads * dh), BF16)
    return _pcall(
        body, name=name, grid=(heads // hb, nq), in_specs=[qspec, kspec, vspec, blk, saved, saved, blk],
        out_specs=[blk, full, full], out_shape=[act, act, act],
        scratch_shapes=[pltpu.VMEM((s, wide), F32), pltpu.VMEM((s, wide), F32)],
        compiler_params=_params("parallel", "arbitrary"),
    )(proj, proj, proj, o, a_all, beta_all, do)


def _xattn_probs(q, k, scale):
    sc = _dot(q, k, 1, 1) * scale
    e = jnp.exp(sc - jnp.max(sc, axis=1, keepdims=True))
    return e / jnp.sum(e, axis=1, keepdims=True)


def _xattn_fwd(qc, kv, tq, name):
    s, d = qc.shape
    m = kv.shape[0]
    dh = d // X_HEADS
    scale = dh ** -0.5

    def body(q_ref, k_ref, v_ref, o_ref):
        p = _xattn_probs(q_ref[...], k_ref[...], scale)
        o_ref[...] = _dot(p.astype(BF16), v_ref[...], 1, 0).astype(o_ref.dtype)

    blk = pl.BlockSpec((tq, dh), lambda h, i: (i, h))
    return _pcall(
        body, name=name, grid=(X_HEADS, s // tq),
        in_specs=[blk, pl.BlockSpec((m, dh), lambda h, i: (0, h)), pl.BlockSpec((m, dh), lambda h, i: (0, X_HEADS + h))],
        out_specs=blk, out_shape=jax.ShapeDtypeStruct((s, d), BF16), compiler_params=_params("parallel", "parallel"),
    )(qc, kv, kv)


def _xattn_bwd(qc, kv, do, tq, name):
    s, d = qc.shape
    m = kv.shape[0]
    dh = d // X_HEADS
    scale = dh ** -0.5
    nq = s // tq

    def body(q_ref, k_ref, v_ref, do_ref, dq_ref, dk_ref, dv_ref, dk_acc, dv_acc):
        i = pl.program_id(1)
        q, k, v = q_ref[...], k_ref[...], v_ref[...]
        dob = do_ref[...].astype(BF16)
        p = _xattn_probs(q, k, scale)
        pb = p.astype(BF16)
        dp = _dot(dob, v, 1, 1)
        ds = pb.astype(F32) * (dp - jnp.sum(dp * pb.astype(F32), axis=1, keepdims=True))
        dsb = (ds * scale).astype(BF16)
        dq_ref[...] = _dot(dsb, k, 1, 0).astype(dq_ref.dtype)
        dk_part = _dot(dsb, q, 0, 0)
        dv_part = _dot(pb, dob, 0, 0)

        @pl.when(i == 0)
        def _():
            dk_acc[...] = dk_part
            dv_acc[...] = dv_part

        @pl.when(i > 0)
        def _():
            dk_acc[...] += dk_part
            dv_acc[...] += dv_part

        @pl.when(i == nq - 1)
        def _():
            dk_ref[...] = dk_acc[...].astype(dk_ref.dtype)
            dv_ref[...] = dv_acc[...].astype(dv_ref.dtype)

    blk = pl.BlockSpec((tq, dh), lambda h, i: (i, h))
    kblk = pl.BlockSpec((m, dh), lambda h, i: (0, h))
    return _pcall(
        body, name=name, grid=(X_HEADS, nq),
        in_specs=[blk, kblk, pl.BlockSpec((m, dh), lambda h, i: (0, X_HEADS + h)), blk],
        out_specs=[blk, kblk, kblk],
        out_shape=[jax.ShapeDtypeStruct((s, d), BF16), jax.ShapeDtypeStruct((m, d), BF16), jax.ShapeDtypeStruct((m, d), BF16)],
        scratch_shapes=[pltpu.VMEM((m, dh), F32), pltpu.VMEM((m, dh), F32)],
        compiler_params=_params("parallel", "arbitrary"),
    )(qc, kv, kv, do)


def _down_loss(act, w_down, h, tgt, g, name):
    s, f = act.shape
    d = w_down.shape[1]
    tm = _pick(s, (512, 256, 128))

    def body(a_ref, w_ref, h_ref, t_ref, g_ref, dh_ref, dhb_ref, dg_ref, loss_ref):
        xh, r = _xhat(h_ref[...] + 0.5 * _dot(a_ref[...], w_ref[...], 1, 0))
        gv = g_ref[...]
        err = xh * gv - t_ref[...]
        dy = err * (1.0 / d)
        dxh = dy * gv
        dx = r * (dxh - xh * jnp.mean(dxh * xh, axis=-1, keepdims=True))
        dh_ref[...] = dx
        dhb_ref[...] = (0.5 * dx).astype(dhb_ref.dtype)
        dg = jnp.sum(dy * xh, axis=0, keepdims=True)
        loss = jnp.broadcast_to(jnp.sum(0.5 * jnp.mean(err * err, axis=-1, keepdims=True), axis=0, keepdims=True), (1, LANES))

        @pl.when(pl.program_id(0) == 0)
        def _():
            dg_ref[...] = dg
            loss_ref[...] = loss

        @pl.when(pl.program_id(0) > 0)
        def _():
            dg_ref[...] += dg
            loss_ref[...] += loss

    row = pl.BlockSpec((tm, d), lambda i: (i, 0))
    once = lambda shape: pl.BlockSpec(shape, lambda i: (0, 0))
    return _pcall(
        body, name=name, grid=(s // tm,),
        in_specs=[pl.BlockSpec((tm, f), lambda i: (i, 0)), once((f, d)), row, row, once((1, d))],
        out_specs=[row, row, once((1, d)), once((1, LANES))],
        out_shape=[jax.ShapeDtypeStruct((s, d), F32), jax.ShapeDtypeStruct((s, d), BF16), jax.ShapeDtypeStruct((1, d), F32),
                   jax.ShapeDtypeStruct((1, LANES), F32)],
        compiler_params=_params("arbitrary"),
    )(act, w_down, h, tgt, g)


def _mix_merge(y_conv, y_sb, w_conv_out, w_attn_out, proj, gate_blocks, b_conv, b_sb, name):
    s, d = y_conv.shape
    tm, tn = _pick(s, (1024, 512, 256, 128)), _pick(d, (512, 256, 128))
    nb = d // tn

    def body(yc_ref, ys_ref, wc_ref, ws_ref, gc_ref, gs_ref, bc_ref, bs_ref, ac_ref, as_ref, m_ref):
        ac = _dot(yc_ref[...].astype(BF16), wc_ref[...], 1, 0)
        asb = _dot(ys_ref[...].astype(BF16), ws_ref[...], 1, 0)
        gc = _sigmoid(gc_ref[...].astype(F32) + bc_ref[...])
        gs = _sigmoid(gs_ref[...].astype(F32) + bs_ref[...])
        ac_ref[...] = ac.astype(ac_ref.dtype)
        as_ref[...] = asb.astype(as_ref.dtype)
        m_ref[...] = (gc * ac + gs * asb).astype(m_ref.dtype)

    rows = pl.BlockSpec((tm, d), lambda i, j: (i, 0))
    wcol = pl.BlockSpec((d, tn), lambda i, j: (0, j))
    bias = pl.BlockSpec((1, tn), lambda i, j: (0, j))
    gate = lambda blk: pl.BlockSpec((tm, tn), lambda i, j: (i, blk * nb + j))
    out = pl.BlockSpec((tm, tn), lambda i, j: (i, j))
    act = jax.ShapeDtypeStruct((s, d), BF16)
    return _pcall(
        body, name=name, grid=(s // tm, nb),
        in_specs=[rows, rows, wcol, wcol, gate(gate_blocks[0]), gate(gate_blocks[1]), bias, bias],
        out_specs=[out, out, out], out_shape=[act, act, act], compiler_params=_params("parallel", "parallel"),
    )(y_conv, y_sb, w_conv_out, w_attn_out, proj, proj, b_conv, b_sb)


def _local_step(x, mem, tgt, w, fetch=None, prefetch=None, emit=None, tick=None, after=None):
    fetch = fetch or (lambda name, after: {})
    prefetch = prefetch or (lambda name, after: None)
    emit = emit or (lambda group, g: None)
    tick = tick or (lambda group, after: None)
    w = dict(w)
    s, d = x.shape
    heads = d // SB_HEAD_DIM
    tm = _pick(s, (1024, 512, 256, 128))
    tq = _pick(s, (1024, 512, 256, 128))
    sb_tq, sb_tk = _pick(s, (512, 256, 128)), _pick(s, (256, 128))
    tc = _pick(d, (256, 128))
    g = {}

    def wt(name, after):
        if name not in w:
            w.update(fetch(name, after))
        return w[name]

    def ffn_fwd(h, n, wgu, wdown, tag, next_g=None):
        gu, act = _ffn_up(n, wt(wgu, n), tag + "_gu")
        prefetch(wdown, gu)
        return gu, act, _mm(act, wt(wdown, act), name=tag + "_down", out_dtype=F32, res=h, alpha=0.5, norm_g=next_g)

    def ffn_bwd(dh, dhb, h, saved, gname, wgu, wdown, tag, copy_scale=None, after=None):
        n, gu, act = saved
        g[wdown] = _mm(act, dhb, ta=True, name=tag + "_dwdown", after=after)
        dgu = _ffn_dgu(dhb, w[wdown], gu, tag + "_dgu", after=emit(tag + "_down", g))
        g[wgu] = _mm(n, dgu, ta=True, b_halves=True, name=tag + "_dwgu", after=tick(tag + "_down", dgu))
        *dh_in, g[gname] = _dgrad_norm(dgu, w[wgu], dh, h, w[gname], tag + "_dn", dy_halves=True, copy_scale=copy_scale,
                                       after=emit(tag, g))
        return dh_in, tick(tag, dh_in[0])

    n1 = _rms_fwd(x, w["g_ffn1"], "ffn1_norm", tm, after=after)
    gu1, act1, (h1, u) = ffn_fwd(x, n1, "w_ffn1_gu", "w_ffn1_down", "ffn1", w["g_mix"])
    prefetch("w_in", h1)
    proj = _mm(u, wt("w_in", u), name="mix_in")
    prefetch("w_conv_out", proj)
    nd = d // SB_HEAD_DIM
    y_conv = _conv_fwd(proj, w["conv_w"], d, tc, "conv_fwd")
    sb_cols = (3 * nd, 4 * nd, 5 * nd)
    y_sb, sb_a, sb_beta = _sb_fwd(proj, heads, sb_cols, _pick(s, (2 * sb_tq, sb_tq)), sb_tk, "sb_fwd")
    prefetch("w_cq", y_sb)
    b_conv, b_sb = w["b_gate"][:, :d], w["b_gate"][:, d:]
    a_conv, a_sb, merged = _mix_merge(y_conv, y_sb, wt("w_conv_out", y_conv), wt("w_attn_out", y_sb), proj, (6, 7), b_conv, b_sb,
                                      "mix_merge")
    prefetch("w_ffn2_gu", merged)
    h2, hn = _mm(merged, wt("w_o", merged), name="mix_out", out_dtype=F32, res=h1, norm_g=w["g_cross"])
    mn = _rms_fwd(mem, w["g_mem"], "mem_norm", _pick(mem.shape[0], (256, 128)))
    qc = _mm(hn, wt("w_cq", hn), name="cross_q")
    kv = _mm(mn, wt("w_ckv", mn), name="cross_kv")
    oc = _xattn_fwd(qc, kv, tq, "xattn_fwd")
    h3, n2 = _mm(oc, wt("w_co", oc), name="cross_out", out_dtype=F32, res=h2, norm_g=w["g_ffn2"])
    gu2, act2 = _ffn_up(n2, wt("w_ffn2_gu", n2), "ffn2_gu")

    dh4, dh4b, g["g_final"], loss_lanes = _down_loss(act2, wt("w_ffn2_down", act2), h3, tgt, w["g_final"], "ffn2_down_loss")

    (dh3, dh3b), tok = ffn_bwd(dh4, dh4b, h3, (n2, gu2, act2), "g_ffn2", "w_ffn2_gu", "w_ffn2_down", "ffn2", copy_scale=1.0)
    g["w_co"] = _mm(oc, dh3b, ta=True, name="cross_dwco", after=tok)
    doc = _mm(dh3b, w["w_co"], tb=True, name="cross_doc")
    dqc, dk, dv = _xattn_bwd(qc, kv, doc, tq, "xattn_bwd")
    dkv = jnp.concatenate([dk, dv], axis=1)
    g["w_cq"] = _mm(hn, dqc, ta=True, name="cross_dwcq")
    g["w_ckv"] = _mm(mn, dkv, ta=True, name="cross_dwckv")
    dmn = _mm(dkv, w["w_ckv"], tb=True, name="cross_dmn", out_dtype=F32)
    g["g_mem"] = _rowcall(lambda dy, xb: dy * _xhat(xb)[0], [_whole(dmn), _whole(mem)], [], [], [d],
                          tm=_pick(mem.shape[0], (256, 128)), name="mem_dnorm")[0]
    dh2, dh2b, g["g_cross"] = _dgrad_norm(dqc, w["w_cq"], dh3, h2, w["g_cross"], "cross_dhn", copy_scale=1.0, after=emit("cross", g))

    g["w_o"] = _mm(merged, dh2b, ta=True, name="mix_dwo", after=tick("cross", dh2))
    dmerged = _mm(dh2b, w["w_o"], tb=True, name="mix_dmerged")

    def merge_bwd(dm, ac, asb, gcp, gsp, bc, bs):
        dm, ac, asb = dm.astype(F32), ac.astype(F32), asb.astype(F32)
        gc = _sigmoid(gcp.astype(F32) + bc)
        gs = _sigmoid(gsp.astype(F32) + bs)
        dgc = dm * ac * gc * (1.0 - gc)
        dgs = dm * asb * gs * (1.0 - gs)
        return dm * gc, dm * gs, dgc, dgs, dgc, dgs

    da_conv, da_sb, dgc, dgs, db_conv, db_sb = _rowcall(
        merge_bwd, [_whole(dmerged), _whole(a_conv), _whole(a_sb), (proj, 6, d), (proj, 7, d)], [b_conv, b_sb],
        [(d, BF16)] * 4, [d, d], tm=tm, name="merge_bwd")
    g["b_gate"] = jnp.concatenate([db_conv, db_sb], axis=1)
    g["w_conv_out"] = _mm(y_conv, da_conv, ta=True, name="conv_dwout")
    g["w_attn_out"] = _mm(y_sb, da_sb, ta=True, name="attn_dwout")
    dy_conv = _mm(da_conv, w["w_conv_out"], tb=True, name="conv_dy")
    dy_sb = _mm(da_sb, w["w_attn_out"], tb=True, name="attn_dy")
    dcb, dcc, dcx, g["conv_w"] = _conv_bwd(dy_conv, proj, w["conv_w"], d, tc, "conv_bwd")
    dq, dk_sb, dv_sb = _sb_bwd(proj, y_sb, sb_a, sb_beta, dy_sb, heads, sb_cols, sb_tq, sb_tk, "sb_bwd")
    dproj = jnp.concatenate([dcb, dcc, dcx, dq, dk_sb, dv_sb, dgc, dgs], axis=1)
    g["w_in"] = _mm(u, dproj, ta=True, name="mix_dwin")
    dh1, dh1b, g["g_mix"] = _dgrad_norm(dproj, w["w_in"], dh2, h1, w["g_mix"], "mix_du", copy_scale=0.5, after=emit("mix", g))
    (dx,), tok = ffn_bwd(dh1, dh1b, x, (n1, gu1, act1), "g_ffn1", "w_ffn1_gu", "w_ffn1_down", "ffn1", after=tick("mix", dh1))
    return loss_lanes, dx, g, tok


MATS = (("w_ffn1_gu", "col"), ("w_ffn1_down", "row"), ("w_in", "col"), ("w_conv_out", "row"), ("w_attn_out", "row"),
        ("w_o", "row"), ("w_cq", "row"), ("w_ckv", "col"), ("w_co", "row"), ("w_ffn2_gu", "col"), ("w_ffn2_down", "row"))
VECS = ("g_ffn1", "g_mix", "g_cross", "g_mem", "g_ffn2", "g_final")
WEIGHTS = ("g_ffn1", "w_ffn1_gu", "w_ffn1_down", "g_mix", "w_in", "b_gate", "conv_w", "w_conv_out", "w_attn_out", "w_o",
           "g_cross", "g_mem", "w_cq", "w_ckv", "w_co", "g_ffn2", "w_ffn2_gu", "w_ffn2_down", "g_final")
CONV_ROWS = 16


def _full_shape(kind, r, c):
    return (r, N_CHIPS * c) if kind == "col" else (N_CHIPS * r, c)


def _piece(ref, kind, r, c, chip, half):
    hr = r // 2
    if kind == "col":
        return ref.at[pl.ds(pl.multiple_of(half * hr, math.gcd(hr, 16)), hr), pl.ds(pl.multiple_of(chip * c, LANES), c)]
    return ref.at[pl.ds(pl.multiple_of(chip * r + half * hr, math.gcd(hr, 16)), hr), :]


def _shard_of(ref, kind, r, c, chip):
    if kind == "col":
        return ref.at[:, pl.ds(pl.multiple_of(chip * c, LANES), c)]
    return ref.at[pl.ds(pl.multiple_of(chip * r, 16), r), :]


def _place():
    x, y, c = lax.axis_index("x"), lax.axis_index("y"), lax.axis_index("c")
    others = [(1 - x, y), (x, 1 - y), (1 - x, 1 - y)]
    return x, y, c, 2 * x + y, others


def _remote(src, dst, send_sem, recv_sem, to):
    return pltpu.make_async_remote_copy(src_ref=src, dst_ref=dst, send_sem=send_sem, recv_sem=recv_sem,
                                        device_id=to, device_id_type=MESH)


HBM = pl.BlockSpec(memory_space=pltpu.HBM)
SEM = pl.BlockSpec(memory_space=pltpu.SEMAPHORE)
EFFECT = pltpu.SideEffectType.DATAFLOW_SIDE_EFFECTING
TOKEN = (8, LANES)


def _split_start(name, plan, n_copies, srcs, lands, after=None):
    ns, nl = len(srcs), len(lands)
    n_in = ns + nl + (after is not None)

    def body(*refs):
        outs = refs[n_in:]
        sends, _ = plan(refs[:ns], refs[ns:ns + nl], outs[0], outs[1])
        for cp in sends:
            cp.start()
        outs[-1][...] = jnp.zeros(TOKEN, F32)

    held = [pltpu.HBM(a.shape, a.dtype) for a in (*srcs, *lands)]
    dma = pltpu.SemaphoreType.DMA((n_copies,))
    ins = [pltpu.with_memory_space_constraint(a, pltpu.HBM) for a in (*srcs, *lands)]
    outs = _pcall(
        body, name=name, in_specs=[HBM] * (ns + nl) + ([] if after is None else [ANY]),
        out_specs=(SEM, SEM, *[HBM] * (ns + nl), pl.BlockSpec(memory_space=pltpu.VMEM)),
        out_shape=(dma, dma, *held, jax.ShapeDtypeStruct(TOKEN, F32)),
        input_output_aliases={i: 2 + i for i in range(ns + nl)},
        compiler_params=pltpu.CompilerParams(has_side_effects=EFFECT),
    )(*ins, *([] if after is None else [after]))
    return outs[0], outs[1], list(outs[2:2 + ns]), list(outs[2 + ns:2 + ns + nl]), outs[-1]


def _split_wait(name, plan, send_sems, recv_sems, srcs, lands, after):
    ns, nl = len(srcs), len(lands)

    def body(*refs):
        sends, recvs = plan(refs[:ns], refs[ns:ns + nl], refs[ns + nl], refs[ns + nl + 1])
        for cp in sends:
            cp.wait_send()
        for cp in recvs:
            cp.wait_recv()

    outs = _pcall(
        body, name=name, in_specs=[HBM] * (ns + nl) + [SEM, SEM, ANY], out_specs=[HBM] * (ns + nl),
        out_shape=[pltpu.HBM(a.shape, a.dtype) for a in (*srcs, *lands)],
        input_output_aliases={i: i for i in range(ns + nl)},
        compiler_params=pltpu.CompilerParams(has_side_effects=EFFECT),
    )(*srcs, *lands, send_sems, recv_sems, after)
    return list(outs[:ns]), list(outs[ns:])


def _gather_plan(dims):
    def plan(shard_refs, full_refs, ss, rs):
        x, y, c, me, others = _place()
        sends, recvs = [], []
        for wi, (kind, r, cw) in enumerate(dims):
            half = shard_refs[wi].at[pl.ds(pl.multiple_of(c * (r // 2), math.gcd(r // 2, 16)), r // 2), :]
            for k, (ox, oy) in enumerate(others):
                sem = 4 * wi + k
                sends.append(_remote(half, _piece(full_refs[wi], kind, r, cw, me, c), ss.at[sem], rs.at[sem], (ox, oy, c)))
                recvs.append(_remote(half, _piece(full_refs[wi], kind, r, cw, 2 * ox + oy, c), ss.at[sem], rs.at[sem], (x, y, c)))
            sem = 4 * wi + 3
            own = _remote(shard_refs[wi], _shard_of(full_refs[wi], kind, r, cw, me), ss.at[sem], rs.at[sem], (x, y, 1 - c))
            sends.append(own)
            recvs.append(own)
        return sends, recvs

    return plan


def _forward_plan(dims):
    def plan(_, full_refs, ss, rs):
        x, y, c, _, others = _place()
        sends, recvs = [], []
        for wi, (kind, r, cw) in enumerate(dims):
            for k, (ox, oy) in enumerate(others):
                sem = 3 * wi + k
                mine = _piece(full_refs[wi], kind, r, cw, 2 * ox + oy, c)
                theirs = _piece(full_refs[wi], kind, r, cw, 2 * ox + oy, 1 - c)
                sends.append(_remote(mine, mine, ss.at[sem], rs.at[sem], (x, y, 1 - c)))
                recvs.append(_remote(theirs, theirs, ss.at[sem], rs.at[sem], (x, y, 1 - c)))
        return sends, recvs

    return plan


def _rs_cores_plan(dims):
    def plan(g_refs, land_refs, ss, rs):
        x, y, c, _, _ = _place()
        sends, recvs = [], []
        for wi, dm in enumerate(dims):
            for chip in range(N_CHIPS):
                sem = N_CHIPS * wi + chip
                sends.append(_remote(_piece(g_refs[wi], *dm, chip, 1 - c), land_refs[wi].at[chip], ss.at[sem], rs.at[sem], (x, y, 1 - c)))
                recvs.append(_remote(_piece(g_refs[wi], *dm, chip, c), land_refs[wi].at[chip], ss.at[sem], rs.at[sem], (x, y, 1 - c)))
        return sends, recvs

    return plan


def _share_plan(nw):
    def plan(_, buf_refs, ss, rs):
        x, y, c, _, _ = _place()
        sends = [_remote(buf_refs[wi].at[c], buf_refs[wi].at[c], ss.at[wi], rs.at[wi], (x, y, 1 - c)) for wi in range(nw)]
        recvs = [_remote(buf_refs[wi].at[1 - c], buf_refs[wi].at[1 - c], ss.at[wi], rs.at[wi], (x, y, 1 - c)) for wi in range(nw)]
        return sends, recvs

    return plan


def _small_plan():
    def plan(_, buf_refs, ss, rs):
        x, y, c = lax.axis_index("x"), lax.axis_index("y"), lax.axis_index("c")
        buf = buf_refs[0]
        sends, recvs = [], []
        for rel in range(1, N_DEV):
            peer = (x ^ (rel >> 2 & 1), y ^ (rel >> 1 & 1), c ^ (rel & 1))
            sends.append(_remote(buf.at[0], buf.at[rel], ss.at[rel - 1], rs.at[rel - 1], peer))
            recvs.append(_remote(buf.at[0], buf.at[rel], ss.at[rel - 1], rs.at[rel - 1], peer))
        return sends, recvs

    return plan


def _sum_small(buf, me, name):
    _, rows, n = buf.shape

    def body(me_ref, b_ref, o_ref):
        tot = b_ref[me_ref[0]]
        for dev in range(1, N_DEV):
            tot = tot + b_ref[dev ^ me_ref[0]]
        o_ref[...] = tot

    return _pcall(
        body, name=name, out_shape=jax.ShapeDtypeStruct((rows, n), F32),
        grid_spec=pltpu.PrefetchScalarGridSpec(
            num_scalar_prefetch=1, grid=(1,), in_specs=[pl.BlockSpec((N_DEV, rows, n), lambda i, m: (0, 0, 0))],
            out_specs=pl.BlockSpec((rows, n), lambda i, m: (0, 0))),
    )(me, buf)


def _rs_chips_plan(nw):
    def plan(p_refs, land_refs, ss, rs):
        x, y, c, me, others = _place()
        sends, recvs = [], []
        for wi in range(nw):
            for k, (ox, oy) in enumerate(others):
                sem = 3 * wi + k
                sends.append(_remote(p_refs[wi].at[2 * ox + oy], land_refs[wi].at[k], ss.at[sem], rs.at[sem], (ox, oy, c)))
                recvs.append(_remote(p_refs[wi].at[me], land_refs[wi].at[k], ss.at[sem], rs.at[sem], (x, y, c)))
        return sends, recvs

    return plan


SUM_BLOCK_BYTES = 4 << 20


def _rows_per_block(n, c, limit_bytes=2 << 20):
    best = None
    for tm in range(16, n + 1, 16):
        if n % tm == 0 and tm * c * 4 <= limit_bytes:
            best = tm
    return best or n


def _sum_cores(grad, got, kind, place, name):
    _, hr, cw = got.shape
    tm = _rows_per_block(hr, cw, SUM_BLOCK_BYTES)
    nb = hr // tm

    def body(place_ref, g_ref, t_ref, o_ref):
        o_ref[...] = (g_ref[...].astype(F32) + t_ref[...].astype(F32)).astype(o_ref.dtype)

    if kind == "col":
        g_spec = pl.BlockSpec((tm, cw), lambda j, i, pr: (pr[0] * nb + i, j))
    else:
        g_spec = pl.BlockSpec((tm, cw), lambda j, i, pr: ((2 * j + pr[0]) * nb + i, 0))
    blk = pl.BlockSpec((None, tm, cw), lambda j, i, pr: (j, i, 0))
    return _pcall(
        body, name=name, out_shape=jax.ShapeDtypeStruct(got.shape, BF16),
        grid_spec=pltpu.PrefetchScalarGridSpec(num_scalar_prefetch=1, grid=(N_CHIPS, nb), in_specs=[g_spec, blk], out_specs=blk),
        compiler_params=_params("parallel", "parallel"),
    )(place, grad, got)


def _sum_chips(parts, got, place, name):
    _, n, cw = got.shape
    tm = _rows_per_block(n, cw, SUM_BLOCK_BYTES)

    def body(place_ref, p_ref, g_ref, o_ref):
        tot = p_ref[...].astype(F32)
        for k in range(3):
            tot = tot + g_ref[k].astype(F32)
        o_ref[...] = tot

    return _pcall(
        body, name=name, out_shape=jax.ShapeDtypeStruct((2, n, cw), F32),
        grid_spec=pltpu.PrefetchScalarGridSpec(
            num_scalar_prefetch=1, grid=(n // tm,),
            in_specs=[pl.BlockSpec((None, tm, cw), lambda i, pr: (pr[1], i, 0)), pl.BlockSpec((3, tm, cw), lambda i, pr: (0, i, 0))],
            out_specs=pl.BlockSpec((None, tm, cw), lambda i, pr: (pr[0], i, 0))),
        compiler_params=_params("parallel"),
    )(place, parts, got)


def _adamw(g, w, m, v, name):
    n, c = g.shape
    c1 = 1.0 - ADAM_B1 ** ADAM_STEP
    c2 = 1.0 - ADAM_B2 ** ADAM_STEP

    def fn(gb, wb, mb, vb):
        m_new = ADAM_B1 * mb + (1.0 - ADAM_B1) * gb
        v_new = ADAM_B2 * vb + (1.0 - ADAM_B2) * (gb * gb)
        delta = -ADAM_LR * ((m_new / c1) / (jnp.sqrt(v_new / c2) + ADAM_EPS) + ADAM_WD * wb)
        return gb, delta, m_new, v_new

    tm = _rows_per_block(n, c) if n % 16 == 0 else n
    return _rowcall(fn, [_whole(g), _whole(w), _whole(m), _whole(v)], [], [(c, F32)] * 4, tm=tm, name=name)


PACK_ROWS = 16


def _pack_rows(parts, width, name, after=None):
    assert sum(p.shape[0] for p in parts) <= PACK_ROWS

    def body(*refs):
        out_ref = refs[-1]
        out_ref[...] = jnp.zeros_like(out_ref)
        at = 0
        for r in refs[:len(parts)]:
            k, n = r.shape
            if n == width:
                out_ref[at:at + k, :] = r[...]
            else:
                out_ref[at:at + k, :] = jnp.broadcast_to(r[:, :1], (k, width))
            at += k

    vm = pl.BlockSpec(memory_space=pltpu.VMEM)
    return _pcall(body, name=name, in_specs=[vm] * len(parts) + ([] if after is None else [ANY]), out_specs=vm,
                  out_shape=jax.ShapeDtypeStruct((PACK_ROWS, width), F32))(*parts, *([] if after is None else [after]))


def _cast_shard(wm, name, after):
    n, c = wm.shape
    return _rowcall(lambda v: v, [_whole(wm)], [], [(c, BF16)], tm=_rows_per_block(n, c), name=name, after=after)[0]


GATHER_GROUPS = (
    ("w_ffn1_gu", "conv_w"), ("w_ffn1_down",), ("w_in",), ("w_conv_out", "w_attn_out", "w_o"), ("w_cq", "w_ckv", "w_co"),
    ("w_ffn2_gu", "w_ffn2_down"),
)
REDUCE_GROUPS = {
    "ffn2": ("w_ffn2_down", "w_ffn2_gu"),
    "cross": ("w_co", "w_cq", "w_ckv"),
    "mix": ("w_o", "w_conv_out", "w_attn_out", "w_in"),
    "ffn1_down": ("w_ffn1_down",),
    "ffn1": ("w_ffn1_gu",),
}
TAIL_STAGES = (("ffn2", "cross"), ("mix",), ("ffn1_down", "ffn1"))
KIND = dict(MATS)


def _step(x, mem, tgt, wts, m_in, v_in):
    d = x.shape[-1]
    cc = wts["conv_w"].shape[1]
    place = jnp.stack([lax.axis_index("c"), 2 * lax.axis_index("x") + lax.axis_index("y")]).astype(jnp.int32)
    dims = {n: (kind, *wts[n].shape) for n, kind in MATS}
    dims["conv_w"] = ("col", CONV_ROWS, cc)

    w = {n: wts[n].reshape(1, -1) for n in VECS + ("b_gate",)}
    flying, token = {}, None
    for names in GATHER_GROUPS:
        gd = [dims[n] for n in names]
        shards = [jnp.pad(wts[n], ((0, CONV_ROWS - CONV_K), (0, 0))) if n == "conv_w" else _cast_shard(wts[n], "cast_" + n, token)
                  for n in names]
        lands = [lax.empty(_full_shape(*dm), sh.dtype) for dm, sh in zip(gd, shards)]
        plan = _gather_plan(gd)
        ss, rs, srcs, lands, token = _split_start("gather_start_" + names[0], plan, 4 * len(names), shards, lands, token)
        flying.update({n: (names, plan, ss, rs, srcs, lands, gd) for n in names})

    passing = {}

    def prefetch(name, after):
        if name not in passing:
            names, plan, ss, rs, srcs, lands, gd = flying[name]
            _, lands = _split_wait("gather_wait_" + names[0], plan, ss, rs, srcs, lands, after)
            plan = _forward_plan(gd)
            ss, rs, _, lands, _ = _split_start("forward_start_" + names[0], plan, 3 * len(names), [], lands)
            passing.update({n: (names, plan, ss, rs, lands) for n in names})

    def fetch(name, after):
        prefetch(name, after)
        names, plan, ss, rs, lands = passing[name]
        _, lands = _split_wait("forward_wait_" + names[0], plan, ss, rs, [], lands, after)
        return {n: (land[:CONV_K] if n == "conv_w" else land) for n, land in zip(names, lands)}

    swapping, sent = {}, {}

    def emit(tag, g):
        if tag not in REDUCE_GROUPS:
            return None
        names = REDUCE_GROUPS[tag]
        gd = [dims[n] for n in names]
        lands = [lax.empty((N_CHIPS, r // 2, cw), BF16) for (_, r, cw) in gd]
        plan = _rs_cores_plan(gd)
        ss, rs, srcs, lands, tok = _split_start("rs_cores_start_" + tag, plan, N_CHIPS * len(names), [g[n] for n in names], lands)
        swapping[tag] = (plan, ss, rs, srcs, lands)
        return tok

    def tick(tag, after):
        if tag not in REDUCE_GROUPS:
            return None
        names = REDUCE_GROUPS[tag]
        plan, ss, rs, srcs, lands = swapping[tag]
        mine, got = _split_wait("rs_cores_wait_" + tag, plan, ss, rs, srcs, lands, after)
        parts = [_sum_cores(gm, t, KIND[n], place, "sum_cores_" + n) for n, gm, t in zip(names, mine, got)]
        lands = [lax.empty((3, *p.shape[1:]), BF16) for p in parts]
        plan = _rs_chips_plan(len(names))
        ss, rs, srcs, lands, tok = _split_start("rs_chips_start_" + tag, plan, 3 * len(names), parts, lands)
        sent[tag] = (plan, ss, rs, srcs, lands)
        return tok

    loss_lanes, dx, g, last = _local_step(x[0], mem[0], tgt[0], w, fetch, prefetch, emit, tick, token)

    rows = [g[n] for n in VECS] + [g["b_gate"][:, :d], g["b_gate"][:, d:], g["conv_w"], loss_lanes]
    packed = _pack_rows(rows, d, "pack_small", after=last)
    small = jnp.concatenate([packed[None], jnp.zeros((N_DEV - 1, *packed.shape), F32)], axis=0)
    small_plan = _small_plan()
    small_ss, small_rs, _, small, after = _split_start("small_start", small_plan, N_DEV - 1, [], [small])

    grads, out = {}, {}

    def update(n):
        shape = wts[n].shape
        as2d = (lambda a: a.reshape(1, -1)) if len(shape) == 1 else (lambda a: a)
        return [r.reshape(shape) for r in _adamw(grads[n], as2d(wts[n]), as2d(m_in[n]), as2d(v_in[n]), "adamw_" + n)]

    def finish(sharing, after):
        tag, names, plan, ss, rs, halves = sharing
        _, both = _split_wait("share_wait_" + tag, plan, ss, rs, [], halves, after)
        for n, b in zip(names, both):
            grads[n] = b.reshape(-1, b.shape[-1])
            out[n] = update(n)
        return out[names[-1]][1]

    sharing = None
    for stage in TAIL_STAGES:
        names, halves = [], []
        for tag in stage:
            plan, ss, rs, srcs, lands = sent[tag]
            parts, landed = _split_wait("rs_chips_wait_" + tag, plan, ss, rs, srcs, lands, after)
            halves += [_sum_chips(p, t, place, "sum_chips_" + n) for n, p, t in zip(REDUCE_GROUPS[tag], parts, landed)]
            names += REDUCE_GROUPS[tag]
        plan = _share_plan(len(names))
        ss, rs, _, halves, after = _split_start("share_start_" + stage[0], plan, len(names), [], halves)
        if sharing is not None:
            after = finish(sharing, after)
        sharing = (stage[0], names, plan, ss, rs, halves)
    after = finish(sharing, after)

    _, small = _split_wait("small_wait", small_plan, small_ss, small_rs, [], small, after)
    me = (4 * lax.axis_index("x") + 2 * lax.axis_index("y") + lax.axis_index("c")).astype(jnp.int32).reshape(1)
    red = _sum_small(small[0], me, "sum_small")
    grads.update({n: red[i:i + 1] for i, n in enumerate(VECS)})
    nv = len(VECS)
    grads["b_gate"] = jnp.concatenate([red[nv:nv + 1], red[nv + 1:nv + 2]], axis=1)
    chip = 2 * lax.axis_index("x") + lax.axis_index("y")
    grads["conv_w"] = lax.dynamic_slice_in_dim(red[nv + 2:nv + 2 + CONV_K], chip * cc, cc, axis=1)
    loss = red[nv + 2 + CONV_K, 0]
    out.update({n: update(n) for n in WEIGHTS if n not in KIND})
    return (loss, dx[None], *[out[n][0] for n in WEIGHTS], *[out[n][1] for n in WEIGHTS],
            *[out[n][2] for n in WEIGHTS], *[out[n][3] for n in WEIGHTS])


def kernel(x, mem, g_ffn1, w_ffn1_gu, w_ffn1_down, g_mix, w_in, b_gate, conv_w, w_conv_out, w_attn_out, w_o, g_cross, g_mem, w_cq, w_ckv, w_co, g_ffn2, w_ffn2_gu, w_ffn2_down, g_final, loss_target, m_g_ffn1, m_w_ffn1_gu, m_w_ffn1_down, m_g_mix, m_w_in, m_b_gate, m_conv_w, m_w_conv_out, m_w_attn_out, m_w_o, m_g_cross, m_g_mem, m_w_cq, m_w_ckv, m_w_co, m_g_ffn2, m_w_ffn2_gu, m_w_ffn2_down, m_g_final, v_g_ffn1, v_w_ffn1_gu, v_w_ffn1_down, v_g_mix, v_w_in, v_b_gate, v_conv_w, v_w_conv_out, v_w_attn_out, v_w_o, v_g_cross, v_g_mem, v_w_cq, v_w_ckv, v_w_co, v_g_ffn2, v_w_ffn2_gu, v_w_ffn2_down, v_g_final):
    given = dict(locals())
    wts = {n: given[n] for n in WEIGHTS}
    m_in = {n: given["m_" + n] for n in WEIGHTS}
    v_in = {n: given["v_" + n] for n in WEIGHTS}
    return _step(x, mem, loss_target, wts, m_in, v_in)
```

```python
import math

import jax
import jax.numpy as jnp
from jax import lax
from jax.experimental import pallas as pl
from jax.experimental.pallas import tpu as pltpu

F32 = jnp.float32
BF16 = jnp.bfloat16
MESH = pl.DeviceIdType.MESH

V7X_VMEM_LIMIT_BYTES = 48 * 1024 * 1024
MM_VMEM_BUDGET_BYTES = 36 * 1024 * 1024
MM_WHOLE_K = 2816
LANES = 128
SB_HEAD_DIM = 128
X_HEADS = 4
CONV_K = 3
RMS_EPS = 1e-6
N_CHIPS = 4
N_DEV = 8
ADAM_LR, ADAM_B1, ADAM_B2, ADAM_EPS, ADAM_WD, ADAM_STEP = 0.001, 0.9, 0.999, 1e-08, 0.01, 10


ANY = pl.BlockSpec(memory_space=pl.ANY)


def _pcall(body, **kw):
    return pl.pallas_call(body, **kw)


def _params(*sem):
    return pltpu.CompilerParams(dimension_semantics=sem, vmem_limit_bytes=V7X_VMEM_LIMIT_BYTES)


def _pick(dim, cands):
    for c in cands:
        if dim % c == 0:
            return c
    return dim


def _dot(a, b, ca, cb):
    return lax.dot_general(a, b, (((ca,), (cb,)), ((), ())), preferred_element_type=F32)


def _mm(a, b, *, name, ta=False, tb=False, out_dtype=BF16, res=None, alpha=1.0, tm=None, tn=None, tk=None, after=None,
        a_halves=False, b_halves=False, norm_g=None):
    assert not (a_halves and ta) and not (b_halves and tb) and not (norm_g is not None and ta)
    if a_halves:
        m, k = a.shape[1], 2 * a.shape[2]
    else:
        m, k = (a.shape[1], a.shape[0]) if ta else a.shape
    if b_halves:
        n = 2 * b.shape[2]
        assert k == b.shape[1]
    else:
        n = b.shape[0] if tb else b.shape[1]
        assert k == (b.shape[1] if tb else b.shape[0]), (a.shape, b.shape, ta, tb)
    if ta:
        tm = tm or _pick(m, (512, 256, 128))
        tn = tn or _pick(n, (1024, 512, 256, 128))
        tk = tk or (k if k <= MM_WHOLE_K else _pick(k, (1024, 512, 256, 128)))
    else:
        tk = tk or (k if k <= MM_WHOLE_K else _pick(k, (MM_WHOLE_K, 2048, 1024, 512, 256, 128)))
        tn = tn or (n if norm_g is not None else _pick(n, (512, 1408, 256, 128) if tk == k else (1024, 512, 256, 128)))
        out_bytes = jnp.dtype(out_dtype).itemsize + (0 if res is None else res.dtype.itemsize) + (0 if norm_g is None else 2)
        per_row = 2 * (tk * a.dtype.itemsize + tn * out_bytes)
        per_row += 4 * tn if tk < k else 0
        rows = (MM_VMEM_BUDGET_BYTES - 2 * tk * tn * b.dtype.itemsize) // per_row
        tm = tm or next((c for c in (2048, 1024, 512, 256, 128) if m % c == 0 and c <= rows), m)
    if a_halves:
        tk = min(tk, k // 2) if (k // 2) % min(tk, k // 2) == 0 else _pick(k // 2, (1408, 1024, 512, 256, 128))
    if b_halves:
        tn = tn if (n // 2) % tn == 0 else _pick(n // 2, (1408, 1024, 512, 256, 128))
    nk = k // tk
    assert m % tm == 0 and n % tn == 0 and k % tk == 0
    a_spec = pl.BlockSpec((tk, tm), lambda i, j, kk: (kk, i)) if ta else pl.BlockSpec((tm, tk), lambda i, j, kk: (i, kk))
    b_spec = pl.BlockSpec((tn, tk), lambda i, j, kk: (j, kk)) if tb else pl.BlockSpec((tk, tn), lambda i, j, kk: (kk, j))
    if a_halves:
        per = (k // 2) // tk
        a_spec = pl.BlockSpec((None, tm, tk), lambda i, j, kk: (kk // per, i, kk % per))
    if b_halves:
        per_n = (n // 2) // tn
        b_spec = pl.BlockSpec((None, tk, tn), lambda i, j, kk: (j // per_n, kk, j % per_n))
    o_spec = pl.BlockSpec((tm, tn), lambda i, j, kk: (i, j))
    ca, cb = (0 if ta else 1), (1 if tb else 0)

    n_in = 2 + (res is not None) + (norm_g is not None) + (after is not None)
    n_out = 1 + (norm_g is not None)

    def body(*refs):
        a_ref, b_ref = refs[:2]
        res_ref = refs[2] if res is not None else None
        g_ref = refs[2 + (res is not None)] if norm_g is not None else None
        o_ref = refs[n_in]
        scratch = refs[n_in + n_out:]

        def finish(acc):
            val = acc if alpha == 1.0 else alpha * acc
            if res_ref is not None:
                val = res_ref[...].astype(F32) + val
            o_ref[...] = val.astype(o_ref.dtype)
            if g_ref is not None:
                refs[n_in + 1][...] = (_xhat(val)[0] * g_ref[...]).astype(BF16)

        part = _dot(a_ref[...].astype(BF16), b_ref[...].astype(BF16), ca, cb)
        if nk == 1:
            finish(part)
        else:
            acc_ref = scratch[0]
            kk = pl.program_id(2)

            @pl.when(kk == 0)
            def _():
                acc_ref[...] = part

            @pl.when(kk > 0)
            def _():
                acc_ref[...] += part

            @pl.when(kk == nk - 1)
            def _():
                finish(acc_ref[...])

    ins = [a, b] + ([] if res is None else [res]) + ([] if norm_g is None else [norm_g]) + ([] if after is None else [after])
    in_specs = [a_spec, b_spec] + ([] if res is None else [o_spec])
    in_specs += ([] if norm_g is None else [pl.BlockSpec((1, tn), lambda i, j, kk: (0, j))]) + ([] if after is None else [ANY])
    outs = _pcall(
        body, name=name, grid=(m // tm, n // tn, nk), in_specs=in_specs, out_specs=[o_spec] * n_out,
        out_shape=[jax.ShapeDtypeStruct((m, n), out_dtype)] + [jax.ShapeDtypeStruct((m, n), BF16)] * (n_out - 1),
        scratch_shapes=[pltpu.VMEM((tm, tn), F32)] if nk > 1 else [],
        compiler_params=_params("parallel", "parallel", "arbitrary"),
    )(*ins)
    return outs[0] if norm_g is None else outs


def _rowcall(fn, rows, consts, outs, accs=(), *, tm, name, after=None):
    s = rows[0][0].shape[0]
    assert s % tm == 0
    n_read, n_out = len(rows) + len(consts), len(outs)
    n_in = n_read + (after is not None)

    def body(*refs):
        vals = fn(*[r[...] for r in refs[:n_read]])
        vals = vals if isinstance(vals, (tuple, list)) else (vals,)
        for o_ref, v in zip(refs[n_in:n_in + n_out], vals[:n_out]):
            o_ref[...] = v.astype(o_ref.dtype)
        if accs:
            first = pl.program_id(0) == 0
            for a_ref, v in zip(refs[n_in + n_out:], vals[n_out:]):
                tot = jnp.sum(v.astype(F32), axis=0, keepdims=True)

                @pl.when(first)
                def _(a_ref=a_ref, tot=tot):
                    a_ref[...] = tot

                @pl.when(jnp.logical_not(first))
                def _(a_ref=a_ref, tot=tot):
                    a_ref[...] += tot

    in_specs = [pl.BlockSpec((tm, w), lambda i, cb=cb: (i, cb)) for (_, cb, w) in rows]
    in_specs += [pl.BlockSpec(c.shape, lambda i: (0, 0)) for c in consts]
    in_specs += [] if after is None else [ANY]
    out_specs = [pl.BlockSpec((tm, w), lambda i: (i, 0)) for (w, _) in outs]
    out_specs += [pl.BlockSpec((1, w), lambda i: (0, 0)) for w in accs]
    out_shape = [jax.ShapeDtypeStruct((s, w), dt) for (w, dt) in outs]
    out_shape += [jax.ShapeDtypeStruct((1, w), F32) for w in accs]
    return _pcall(
        body, name=name, grid=(s // tm,), in_specs=in_specs, out_specs=out_specs, out_shape=out_shape,
        compiler_params=_params("arbitrary" if accs else "parallel"),
    )(*[r[0] for r in rows], *consts, *([] if after is None else [after]))


def _whole(a):
    return (a, 0, a.shape[1])


def _xhat(x):
    x = x.astype(F32)
    r = lax.rsqrt(jnp.mean(x * x, axis=-1, keepdims=True) + RMS_EPS)
    return x * r, r


def _rms_bwd(dy, x, g):
    xh, r = _xhat(x)
    dxh = dy.astype(F32) * g
    dx = r * (dxh - xh * jnp.mean(dxh * xh, axis=-1, keepdims=True))
    return dx, dy.astype(F32) * xh


def _sigmoid(x):
    return 1.0 / (1.0 + jnp.exp(-x))


def _rms_fwd(x, g, name, tm, after=None):
    d = x.shape[1]
    return _rowcall(lambda xb, gb: _xhat(xb)[0] * gb, [_whole(x)], [g], [(d, BF16)], tm=tm, name=name, after=after)[0]


def _silu_parts(gate):
    sg = _sigmoid(gate)
    return sg, gate * sg


def _ffn_up(n, w_gu, name):
    s, d = n.shape
    f = w_gu.shape[1] // 2
    tn = _pick(f, (1408, 1024, 512, 256, 128))
    tm = _pick(s, (1024, 512, 256, 128))
    nb = f // tn

    def body(n_ref, wg_ref, wu_ref, gu_ref, act_ref):
        nv = n_ref[...]
        gate = _dot(nv, wg_ref[...], 1, 0)
        up = _dot(nv, wu_ref[...], 1, 0)
        gu_ref[0] = gate.astype(gu_ref.dtype)
        gu_ref[1] = up.astype(gu_ref.dtype)
        act_ref[...] = (_silu_parts(gate)[1] * up).astype(act_ref.dtype)

    return _pcall(
        body, name=name, grid=(s // tm, nb),
        in_specs=[pl.BlockSpec((tm, d), lambda i, j: (i, 0)), pl.BlockSpec((d, tn), lambda i, j: (0, j)),
                  pl.BlockSpec((d, tn), lambda i, j: (0, nb + j))],
        out_specs=[pl.BlockSpec((2, tm, tn), lambda i, j: (0, i, j)), pl.BlockSpec((tm, tn), lambda i, j: (i, j))],
        out_shape=[jax.ShapeDtypeStruct((2, s, f), BF16), jax.ShapeDtypeStruct((s, f), BF16)],
        compiler_params=_params("parallel", "parallel"),
    )(n, w_gu, w_gu)


def _ffn_dgu(dhb, w_down, gu, name, after=None):
    s, d = dhb.shape
    f = w_down.shape[0]
    tn = _pick(f, (1408, 1024, 512, 256, 128))
    tm = _pick(s, (1024, 512, 256, 128))

    def body(dh_ref, w_ref, gu_ref, *rest):
        o_ref = rest[-1]
        dact = _dot(dh_ref[...], w_ref[...], 1, 1)
        gate, up = gu_ref[0].astype(F32), gu_ref[1].astype(F32)
        sg, silu = _silu_parts(gate)
        o_ref[0] = (dact * up * (sg + silu * (1.0 - sg))).astype(o_ref.dtype)
        o_ref[1] = (dact * silu).astype(o_ref.dtype)

    blk = pl.BlockSpec((2, tm, tn), lambda i, j: (0, i, j))
    return _pcall(
        body, name=name, grid=(s // tm, f // tn),
        in_specs=[pl.BlockSpec((tm, d), lambda i, j: (i, 0)), pl.BlockSpec((tn, d), lambda i, j: (j, 0)), blk]
        + ([] if after is None else [ANY]),
        out_specs=blk, out_shape=jax.ShapeDtypeStruct((2, s, f), BF16), compiler_params=_params("parallel", "parallel"),
    )(dhb, w_down, gu, *([] if after is None else [after]))


def _dgrad_norm(dy, wmat, dh, x, g, name, *, dy_halves=False, copy_scale=None, after=None):
    s, d = dh.shape
    k = wmat.shape[1]
    tk = k if k <= MM_WHOLE_K else _pick(k, (MM_WHOLE_K, 2048, 1024, 512, 256, 128))
    if dy_halves and (k // 2) % tk:
        tk = _pick(k // 2, (1408, 1024, 512, 256, 128))
    tm = _pick(s, (512, 256, 128))
    nk, per = k // tk, (k // 2) // tk if dy_halves else 0
    n_in = 5 + (after is not None)
    n_out = 2 + (copy_scale is not None)

    def body(*refs):
        dy_ref, w_ref, dh_ref, x_ref, g_ref = refs[:5]
        outs, scratch = refs[n_in:n_in + n_out], refs[n_in + n_out:]
        i, kk = pl.program_id(0), pl.program_id(1)
        part = _dot(dy_ref[...], w_ref[...], 1, 1)

        def finish(dn):
            dx, dg = _rms_bwd(dn, x_ref[...], g_ref[...])
            tot = dh_ref[...] + dx
            outs[0][...] = tot
            if copy_scale is not None:
                outs[1][...] = (copy_scale * tot).astype(outs[1].dtype)
            dg = jnp.sum(dg, axis=0, keepdims=True)

            @pl.when(i == 0)
            def _():
                outs[-1][...] = dg

            @pl.when(i > 0)
            def _():
                outs[-1][...] += dg

        if nk == 1:
            finish(part)
        else:
            acc_ref = scratch[0]

            @pl.when(kk == 0)
            def _():
                acc_ref[...] = part

            @pl.when(kk > 0)
            def _():
                acc_ref[...] += part

            @pl.when(kk == nk - 1)
            def _():
                finish(acc_ref[...])

    row = pl.BlockSpec((tm, d), lambda i, kk: (i, 0))
    dy_spec = pl.BlockSpec((None, tm, tk), lambda i, kk: (kk // per, i, kk % per)) if dy_halves else pl.BlockSpec((tm, tk), lambda i, kk: (i, kk))
    in_specs = [dy_spec, pl.BlockSpec((d, tk), lambda i, kk: (0, kk)), row, row, pl.BlockSpec((1, d), lambda i, kk: (0, 0))]
    out_specs = [row] * (n_out - 1) + [pl.BlockSpec((1, d), lambda i, kk: (0, 0))]
    out_shape = [jax.ShapeDtypeStruct((s, d), F32)] + ([] if copy_scale is None else [jax.ShapeDtypeStruct((s, d), BF16)])
    return _pcall(
        body, name=name, grid=(s // tm, nk), in_specs=in_specs + ([] if after is None else [ANY]), out_specs=out_specs,
        out_shape=out_shape + [jax.ShapeDtypeStruct((1, d), F32)], scratch_shapes=[pltpu.VMEM((tm, d), F32)] if nk > 1 else [],
        compiler_params=_params("arbitrary", "arbitrary"),
    )(dy, wmat, dh, x, g, *([] if after is None else [after]))


def _shift_down(p, k):
    if k == 0:
        return p
    rows = lax.broadcasted_iota(jnp.int32, p.shape, 0)
    return jnp.where(rows >= k, pltpu.roll(p, k, 0), 0.0)


def _shift_up(p, k):
    if k == 0:
        return p
    s = p.shape[0]
    rows = lax.broadcasted_iota(jnp.int32, p.shape, 0)
    return jnp.where(rows < s - k, pltpu.roll(p, s - k, 0), 0.0)


def _conv_fwd(proj, conv_w, d, tc, name):
    s = proj.shape[0]
    nb = d // tc

    def body(cb_ref, cc_ref, cx_ref, w_ref, y_ref):
        p = cc_ref[...].astype(F32) * cx_ref[...].astype(F32)
        w = w_ref[...]
        acc = p * w[CONV_K - 1:CONV_K, :]
        for k in range(1, CONV_K):
            acc = acc + _shift_down(p, k) * w[CONV_K - 1 - k:CONV_K - k, :]
        y_ref[...] = (cb_ref[...].astype(F32) * acc).astype(y_ref.dtype)

    col = lambda off: pl.BlockSpec((s, tc), lambda j: (0, off * nb + j))
    return _pcall(
        body, name=name, grid=(nb,), in_specs=[col(0), col(1), col(2), pl.BlockSpec((CONV_K, tc), lambda j: (0, j))],
        out_specs=pl.BlockSpec((s, tc), lambda j: (0, j)), out_shape=jax.ShapeDtypeStruct((s, d), BF16),
        compiler_params=_params("parallel"),
    )(proj, proj, proj, conv_w)


def _conv_bwd(dy, proj, conv_w, d, tc, name):
    s = proj.shape[0]
    nb = d // tc

    def body(dy_ref, cb_ref, cc_ref, cx_ref, w_ref, dcb_ref, dcc_ref, dcx_ref, dw_ref):
        cc, cx = cc_ref[...].astype(F32), cx_ref[...].astype(F32)
        p = cc * cx
        w = w_ref[...]
        dyv = dy_ref[...].astype(F32)
        shifted = [_shift_down(p, CONV_K - 1 - k) for k in range(CONV_K)]
        conv = shifted[0] * w[0:1, :]
        for k in range(1, CONV_K):
            conv = conv + shifted[k] * w[k:k + 1, :]
        dcb_ref[...] = (dyv * conv).astype(dcb_ref.dtype)
        ds = dyv * cb_ref[...].astype(F32)
        dp = ds * w[CONV_K - 1:CONV_K, :]
        for k in range(1, CONV_K):
            dp = dp + _shift_up(ds, k) * w[CONV_K - 1 - k:CONV_K - k, :]
        dcc_ref[...] = (dp * cx).astype(dcc_ref.dtype)
        dcx_ref[...] = (dp * cc).astype(dcx_ref.dtype)
        for k in range(CONV_K):
            dw_ref[k:k + 1, :] = jnp.sum(ds * shifted[k], axis=0, keepdims=True)

    col = lambda off: pl.BlockSpec((s, tc), lambda j: (0, off * nb + j))
    blk = pl.BlockSpec((s, tc), lambda j: (0, j))
    wblk = pl.BlockSpec((CONV_K, tc), lambda j: (0, j))
    act = jax.ShapeDtypeStruct((s, d), BF16)
    return _pcall(
        body, name=name, grid=(nb,), in_specs=[blk, col(0), col(1), col(2), wblk],
        out_specs=[blk, blk, blk, wblk], out_shape=[act, act, act, jax.ShapeDtypeStruct((CONV_K, d), F32)],
        compiler_params=_params("parallel"),
    )(dy, proj, proj, proj, conv_w)


def _sb_tile(q, kj, scale, carry, tri, mask):
    z = _dot(q, kj, 1, 1) * scale
    lsz = jnp.minimum(z, 0.0) - jnp.log(1.0 + jnp.exp(-jnp.abs(z)))
    l1m = lsz - z
    if mask is not None:
        l1m = jnp.where(mask, l1m, 0.0)
    l1b = l1m.astype(BF16)
    a = jnp.exp(lsz + (carry + _dot(l1b, tri, 1, 0)))
    if mask is not None:
        a = jnp.where(mask, a, 0.0)
    return lsz, l1b, a.astype(BF16)


def _add_rows(x, upd, r0):
    return x + upd if r0 == 0 else jnp.concatenate([x[:r0], x[r0:] + upd], axis=0)


def _sb_masks(tq, tk):
    row = lax.broadcasted_iota(jnp.int32, (tq, tk), 0)
    col = lax.broadcasted_iota(jnp.int32, (tq, tk), 1)
    masks = [col + dj * tk < row for dj in range(tq // tk)]
    r2 = lax.broadcasted_iota(jnp.int32, (tk, tk), 0)
    c2 = lax.broadcasted_iota(jnp.int32, (tk, tk), 1)
    return masks, (r2 > c2).astype(BF16), (r2 < c2).astype(BF16)


def _sb_fwd(proj, heads, col0, tq, tk, name):
    s = proj.shape[0]
    dh = SB_HEAD_DIM
    nq, nd, nkt = s // tq, tq // tk, s // tk
    scale = dh ** -0.5

    def body(q_ref, k_ref, v_ref, o_ref, a_ref, b_ref):
        i = pl.program_id(1)
        q = q_ref[...]
        masks, tri_right, _ = _sb_masks(tq, tk)

        def tile(j, carry, acc, mask, r0=0):
            start = pl.multiple_of(j * tk, tk)
            kj = k_ref[pl.ds(start, tk), :]
            vj = v_ref[pl.ds(start, tk), :]
            lsz, l1b, ab = _sb_tile(q[r0:], kj, scale, carry[r0:], tri_right, None if mask is None else mask[r0:])
            a_ref[j, r0:, :] = ab
            b_ref[j, r0:, :] = jnp.exp(lsz).astype(b_ref.dtype)
            if r0:
                a_ref[j, :r0, :] = jnp.zeros((r0, tk), a_ref.dtype)
                b_ref[j, :r0, :] = jnp.zeros((r0, tk), b_ref.dtype)
            return (_add_rows(carry, jnp.sum(l1b.astype(F32), axis=1, keepdims=True), r0),
                    _add_rows(acc, _dot(ab, vj, 1, 0), r0))

        state = (jnp.zeros((tq, 1), F32), jnp.zeros((tq, dh), F32))
        for dj in reversed(range(nd)):
            state = tile(i * nd + dj, *state, masks[dj], dj * tk)
        def left_block(t, st):
            for dj in reversed(range(nd)):
                st = tile((i - 1 - t) * nd + dj, st[0], st[1], None)
            return st

        state = lax.fori_loop(0, i, left_block, state)
        o_ref[...] = state[1]

    qspec = pl.BlockSpec((tq, dh), lambda h, i: (i, col0[0] + h))
    kspec = pl.BlockSpec((s, dh), lambda h, i: (0, col0[1] + h))
    vspec = pl.BlockSpec((s, dh), lambda h, i: (0, col0[2] + h))
    saved = pl.BlockSpec((None, nkt, tq, tk), lambda h, i: (h, 0, i, 0))
    saved_shape = jax.ShapeDtypeStruct((heads, nkt, s, tk), BF16)
    return _pcall(
        body, name=name, grid=(heads, nq), in_specs=[qspec, kspec, vspec],
        out_specs=[pl.BlockSpec((tq, dh), lambda h, i: (i, h)), saved, saved],
        out_shape=[jax.ShapeDtypeStruct((s, heads * dh), F32), saved_shape, saved_shape],
        compiler_params=_params("parallel", "parallel"),
    )(proj, proj, proj)


SB_BWD_HEADS = 2


def _sb_bwd(proj, o, a_all, beta_all, do, heads, col0, tq, tk, name):
    s = proj.shape[0]
    dh = SB_HEAD_DIM
    nq, nd, nkt = s // tq, tq // tk, s // tk
    scale = dh ** -0.5
    hb = SB_BWD_HEADS if heads % SB_BWD_HEADS == 0 and all(c % SB_BWD_HEADS == 0 for c in col0) else 1
    wide = hb * dh

    def body(q_ref, k_ref, v_ref, o_ref, a_ref, b_ref, do_ref, dq_ref, dk_ref, dv_ref, dk_acc, dv_acc):
        i = pl.program_id(1)

        @pl.when(i == 0)
        def _():
            dk_acc[...] = jnp.zeros_like(dk_acc)
            dv_acc[...] = jnp.zeros_like(dv_acc)

        lanes = [slice(hh * dh, (hh + 1) * dh) for hh in range(hb)]
        q = [q_ref[:, ln] for ln in lanes]
        dob = [do_ref[:, ln].astype(BF16) for ln in lanes]
        delta = [jnp.sum(dob[hh].astype(F32) * o_ref[:, lanes[hh]], axis=1, keepdims=True) for hh in range(hb)]
        masks, _, tri_left = _sb_masks(tq, tk)

        def tile(hh, j, carry_g, dq, mask):
            start = pl.multiple_of(j * tk, tk)
            kj = k_ref[pl.ds(start, tk), lanes[hh]]
            vj = v_ref[pl.ds(start, tk), lanes[hh]]
            ab = a_ref[hh, j]
            g = _dot(dob[hh], vj, 1, 1) * ab.astype(F32)
            carry_g = carry_g + jnp.sum(g, axis=1, keepdims=True)
            left = (delta[hh] - carry_g) + _dot(g.astype(BF16), tri_left, 1, 0)
            dz = g - b_ref[hh, j].astype(F32) * (g + left)
            if mask is not None:
                dz = jnp.where(mask, dz, 0.0)
            dzb = dz.astype(BF16)
            dk_acc[pl.ds(start, tk), lanes[hh]] += _dot(dzb, q[hh], 0, 0)
            dv_acc[pl.ds(start, tk), lanes[hh]] += _dot(ab, dob[hh], 0, 0)
            return carry_g, dq + _dot(dzb, kj, 1, 0)

        def block(jb, st, use_masks):
            st = list(st)
            for dj in reversed(range(nd)):
                for hh in range(hb):
                    st[hh] = tile(hh, jb * nd + dj, *st[hh], masks[dj] if use_masks else None)
            return tuple(st)

        state = block(i, tuple((jnp.zeros((tq, 1), F32), jnp.zeros((tq, dh), F32)) for _ in range(hb)), True)
        state = lax.fori_loop(0, i, lambda t, st: block(i - 1 - t, st, False), state)
        for hh in range(hb):
            dq_ref[:, lanes[hh]] = (state[hh][1] * scale).astype(dq_ref.dtype)

        @pl.when(i == nq - 1)
        def _():
            dk_ref[...] = (dk_acc[...] * scale).astype(dk_ref.dtype)
            dv_ref[...] = dv_acc[...].astype(dv_ref.dtype)

    qspec = pl.BlockSpec((tq, wide), lambda h, i: (i, col0[0] // hb + h))
    kspec = pl.BlockSpec((s, wide), lambda h, i: (0, col0[1] // hb + h))
    vspec = pl.BlockSpec((s, wide), lambda h, i: (0, col0[2] // hb + h))
    blk = pl.BlockSpec((tq, wide), lambda h, i: (i, h))
    full = pl.BlockSpec((s, wide), lambda h, i: (0, h))
    saved = pl.BlockSpec((hb, nkt, tq, tk), lambda h, i: (h, 0, i, 0))
    act = jax.ShapeDtypeStruct((s, heads * dh), BF16)
    return _pcall(
        body, name=name, grid=(heads // hb, nq), in_specs=[qspec, kspec, vspec, blk, saved, saved, blk],
        out_specs=[blk, full, full], out_shape=[act, act, act],
        scratch_shapes=[pltpu.VMEM((s, wide), F32), pltpu.VMEM((s, wide), F32)],
        compiler_params=_params("parallel", "arbitrary"),
    )(proj, proj, proj, o, a_all, beta_all, do)


def _xattn_probs(q, k, scale):
    sc = _dot(q, k, 1, 1) * scale
    e = jnp.exp(sc - jnp.max(sc, axis=1, keepdims=True))
    return e / jnp.sum(e, axis=1, keepdims=True)


def _xattn_fwd(qc, kv, tq, name):
    s, d = qc.shape
    m = kv.shape[0]
    dh = d // X_HEADS
    scale = dh ** -0.5

    def body(q_ref, k_ref, v_ref, o_ref):
        p = _xattn_probs(q_ref[...], k_ref[...], scale)
        o_ref[...] = _dot(p.astype(BF16), v_ref[...], 1, 0).astype(o_ref.dtype)

    blk = pl.BlockSpec((tq, dh), lambda h, i: (i, h))
    return _pcall(
        body, name=name, grid=(X_HEADS, s // tq),
        in_specs=[blk, pl.BlockSpec((m, dh), lambda h, i: (0, h)), pl.BlockSpec((m, dh), lambda h, i: (0, X_HEADS + h))],
        out_specs=blk, out_shape=jax.ShapeDtypeStruct((s, d), BF16), compiler_params=_params("parallel", "parallel"),
    )(qc, kv, kv)


def _xattn_bwd(qc, kv, do, tq, name):
    s, d = qc.shape
    m = kv.shape[0]
    dh = d // X_HEADS
    scale = dh ** -0.5
    nq = s // tq

    def body(q_ref, k_ref, v_ref, do_ref, dq_ref, dk_ref, dv_ref, dk_acc, dv_acc):
        i = pl.program_id(1)
        q, k, v = q_ref[...], k_ref[...], v_ref[...]
        dob = do_ref[...].astype(BF16)
        p = _xattn_probs(q, k, scale)
        pb = p.astype(BF16)
        dp = _dot(dob, v, 1, 1)
        ds = pb.astype(F32) * (dp - jnp.sum(dp * pb.astype(F32), axis=1, keepdims=True))
        dsb = (ds * scale).astype(BF16)
        dq_ref[...] = _dot(dsb, k, 1, 0).astype(dq_ref.dtype)
        dk_part = _dot(dsb, q, 0, 0)
        dv_part = _dot(pb, dob, 0, 0)

        @pl.when(i == 0)
        def _():
            dk_acc[...] = dk_part
            dv_acc[...] = dv_part

        @pl.when(i > 0)
        def _():
            dk_acc[...] += dk_part
            dv_acc[...] += dv_part

        @pl.when(i == nq - 1)
        def _():
            dk_ref[...] = dk_acc[...].astype(dk_ref.dtype)
            dv_ref[...] = dv_acc[...].astype(dv_ref.dtype)

    blk = pl.BlockSpec((tq, dh), lambda h, i: (i, h))
    kblk = pl.BlockSpec((m, dh), lambda h, i: (0, h))
    return _pcall(
        body, name=name, grid=(X_HEADS, nq),
        in_specs=[blk, kblk, pl.BlockSpec((m, dh), lambda h, i: (0, X_HEADS + h)), blk],
        out_specs=[blk, kblk, kblk],
        out_shape=[jax.ShapeDtypeStruct((s, d), BF16), jax.ShapeDtypeStruct((m, d), BF16), jax.ShapeDtypeStruct((m, d), BF16)],
        scratch_shapes=[pltpu.VMEM((m, dh), F32), pltpu.VMEM((m, dh), F32)],
        compiler_params=_params("parallel", "arbitrary"),
    )(qc, kv, kv, do)


def _down_loss(act, w_down, h, tgt, g, name):
    s, f = act.shape
    d = w_down.shape[1]
    tm = _pick(s, (512, 256, 128))

    def body(a_ref, w_ref, h_ref, t_ref, g_ref, dh_ref, dhb_ref, dg_ref, loss_ref):
        xh, r = _xhat(h_ref[...] + 0.5 * _dot(a_ref[...], w_ref[...], 1, 0))
        gv = g_ref[...]
        err = xh * gv - t_ref[...]
        dy = err * (1.0 / d)
        dxh = dy * gv
        dx = r * (dxh - xh * jnp.mean(dxh * xh, axis=-1, keepdims=True))
        dh_ref[...] = dx
        dhb_ref[...] = (0.5 * dx).astype(dhb_ref.dtype)
        dg = jnp.sum(dy * xh, axis=0, keepdims=True)
        loss = jnp.broadcast_to(jnp.sum(0.5 * jnp.mean(err * err, axis=-1, keepdims=True), axis=0, keepdims=True), (1, LANES))

        @pl.when(pl.program_id(0) == 0)
        def _():
            dg_ref[...] = dg
            loss_ref[...] = loss

        @pl.when(pl.program_id(0) > 0)
        def _():
            dg_ref[...] += dg
            loss_ref[...] += loss

    row = pl.BlockSpec((tm, d), lambda i: (i, 0))
    once = lambda shape: pl.BlockSpec(shape, lambda i: (0, 0))
    return _pcall(
        body, name=name, grid=(s // tm,),
        in_specs=[pl.BlockSpec((tm, f), lambda i: (i, 0)), once((f, d)), row, row, once((1, d))],
        out_specs=[row, row, once((1, d)), once((1, LANES))],
        out_shape=[jax.ShapeDtypeStruct((s, d), F32), jax.ShapeDtypeStruct((s, d), BF16), jax.ShapeDtypeStruct((1, d), F32),
                   jax.ShapeDtypeStruct((1, LANES), F32)],
        compiler_params=_params("arbitrary"),
    )(act, w_down, h, tgt, g)


def _mix_merge(y_conv, y_sb, w_conv_out, w_attn_out, proj, gate_blocks, b_conv, b_sb, name):
    s, d = y_conv.shape
    tm, tn = _pick(s, (1024, 512, 256, 128)), _pick(d, (512, 256, 128))
    nb = d // tn

    def body(yc_ref, ys_ref, wc_ref, ws_ref, gc_ref, gs_ref, bc_ref, bs_ref, ac_ref, as_ref, m_ref):
        ac = _dot(yc_ref[...].astype(BF16), wc_ref[...], 1, 0)
        asb = _dot(ys_ref[...].astype(BF16), ws_ref[...], 1, 0)
        gc = _sigmoid(gc_ref[...].astype(F32) + bc_ref[...])
        gs = _sigmoid(gs_ref[...].astype(F32) + bs_ref[...])
        ac_ref[...] = ac.astype(ac_ref.dtype)
        as_ref[...] = asb.astype(as_ref.dtype)
        m_ref[...] = (gc * ac + gs * asb).astype(m_ref.dtype)

    rows = pl.BlockSpec((tm, d), lambda i, j: (i, 0))
    wcol = pl.BlockSpec((d, tn), lambda i, j: (0, j))
    bias = pl.BlockSpec((1, tn), lambda i, j: (0, j))
    gate = lambda blk: pl.BlockSpec((tm, tn), lambda i, j: (i, blk * nb + j))
    out = pl.BlockSpec((tm, tn), lambda i, j: (i, j))
    act = jax.ShapeDtypeStruct((s, d), BF16)
    return _pcall(
        body, name=name, grid=(s // tm, nb),
        in_specs=[rows, rows, wcol, wcol, gate(gate_blocks[0]), gate(gate_blocks[1]), bias, bias],
        out_specs=[out, out, out], out_shape=[act, act, act], compiler_params=_params("parallel", "parallel"),
    )(y_conv, y_sb, w_conv_out, w_attn_out, proj, proj, b_conv, b_sb)


def _mix_dmerge(dh, w_o, a_conv, a_sb, proj, gate_blocks, b_conv, b_sb, name):
    s, d = a_conv.shape
    tm, tn = _pick(s, (1024, 512, 256, 128)), _pick(d, (512, 256, 128))
    nb = d // tn

    def body(dh_ref, w_ref, ac_ref, as_ref, gc_ref, gs_ref, bc_ref, bs_ref, dac_ref, das_ref, dgc_ref, dgs_ref, dbc_ref, dbs_ref):
        dm = _dot(dh_ref[...], w_ref[...], 1, 1)
        gc = _sigmoid(gc_ref[...].astype(F32) + bc_ref[...])
        gs = _sigmoid(gs_ref[...].astype(F32) + bs_ref[...])
        dgc = dm * ac_ref[...].astype(F32) * gc * (1.0 - gc)
        dgs = dm * as_ref[...].astype(F32) * gs * (1.0 - gs)
        dac_ref[...] = (dm * gc).astype(dac_ref.dtype)
        das_ref[...] = (dm * gs).astype(das_ref.dtype)
        dgc_ref[...] = dgc.astype(dgc_ref.dtype)
        dgs_ref[...] = dgs.astype(dgs_ref.dtype)
        sums = jnp.sum(dgc, axis=0, keepdims=True), jnp.sum(dgs, axis=0, keepdims=True)

        @pl.when(pl.program_id(1) == 0)
        def _():
            dbc_ref[...], dbs_ref[...] = sums

        @pl.when(pl.program_id(1) > 0)
        def _():
            dbc_ref[...] += sums[0]
            dbs_ref[...] += sums[1]

    tile = pl.BlockSpec((tm, tn), lambda j, i: (i, j))
    bias = pl.BlockSpec((1, tn), lambda j, i: (0, j))
    gate = lambda blk: pl.BlockSpec((tm, tn), lambda j, i: (i, blk * nb + j))
    act = jax.ShapeDtypeStruct((s, d), BF16)
    vec = jax.ShapeDtypeStruct((1, d), F32)
    return _pcall(
        body, name=name, grid=(nb, s // tm),
        in_specs=[pl.BlockSpec((tm, d), lambda j, i: (i, 0)), pl.BlockSpec((tn, d), lambda j, i: (j, 0)), tile, tile,
                  gate(gate_blocks[0]), gate(gate_blocks[1]), bias, bias],
        out_specs=[tile, tile, tile, tile, bias, bias], out_shape=[act, act, act, act, vec, vec],
        compiler_params=_params("parallel", "arbitrary"),
    )(dh, w_o, a_conv, a_sb, proj, proj, b_conv, b_sb)


def _local_step(x, mem, tgt, w, fetch=None, prefetch=None, emit=None, tick=None, after=None):
    fetch = fetch or (lambda name, after: {})
    prefetch = prefetch or (lambda name, after: None)
    emit = emit or (lambda group, g: None)
    tick = tick or (lambda group, after: None)
    w = dict(w)
    s, d = x.shape
    heads = d // SB_HEAD_DIM
    tm = _pick(s, (1024, 512, 256, 128))
    tq = _pick(s, (1024, 512, 256, 128))
    sb_tq, sb_tk = _pick(s, (512, 256, 128)), _pick(s, (256, 128))
    tc = _pick(d, (256, 128))
    g = {}

    def wt(name, after):
        if name not in w:
            w.update(fetch(name, after))
        return w[name]

    def ffn_fwd(h, n, wgu, wdown, tag, next_g=None):
        gu, act = _ffn_up(n, wt(wgu, n), tag + "_gu")
        prefetch(wdown, gu)
        return gu, act, _mm(act, wt(wdown, act), name=tag + "_down", out_dtype=F32, res=h, alpha=0.5, norm_g=next_g)

    def ffn_bwd(dh, dhb, h, saved, gname, wgu, wdown, tag, copy_scale=None, after=None):
        n, gu, act = saved
        g[wdown] = _mm(act, dhb, ta=True, name=tag + "_dwdown", after=after)
        dgu = _ffn_dgu(dhb, w[wdown], gu, tag + "_dgu", after=emit(tag + "_down", g))
        g[wgu] = _mm(n, dgu, ta=True, b_halves=True, name=tag + "_dwgu", after=tick(tag + "_down", dgu))
        *dh_in, g[gname] = _dgrad_norm(dgu, w[wgu], dh, h, w[gname], tag + "_dn", dy_halves=True, copy_scale=copy_scale,
                                       after=emit(tag, g))
        return dh_in, tick(tag, dh_in[0])

    n1 = _rms_fwd(x, w["g_ffn1"], "ffn1_norm", tm, after=after)
    gu1, act1, (h1, u) = ffn_fwd(x, n1, "w_ffn1_gu", "w_ffn1_down", "ffn1", w["g_mix"])
    prefetch("w_in", h1)
    proj = _mm(u, wt("w_in", u), name="mix_in")
    prefetch("w_conv_out", proj)
    nd = d // SB_HEAD_DIM
    y_conv = _conv_fwd(proj, w["conv_w"], d, tc, "conv_fwd")
    sb_cols = (3 * nd, 4 * nd, 5 * nd)
    y_sb, sb_a, sb_beta = _sb_fwd(proj, heads, sb_cols, _pick(s, (2 * sb_tq, sb_tq)), sb_tk, "sb_fwd")
    prefetch("w_cq", y_sb)
    b_conv, b_sb = w["b_gate"][:, :d], w["b_gate"][:, d:]
    a_conv, a_sb, merged = _mix_merge(y_conv, y_sb, wt("w_conv_out", y_conv), wt("w_attn_out", y_sb), proj, (6, 7), b_conv, b_sb,
                                      "mix_merge")
    prefetch("w_ffn2_gu", merged)
    h2, hn = _mm(merged, wt("w_o", merged), name="mix_out", out_dtype=F32, res=h1, norm_g=w["g_cross"])
    mn = _rms_fwd(mem, w["g_mem"], "mem_norm", _pick(mem.shape[0], (256, 128)))
    qc = _mm(hn, wt("w_cq", hn), name="cross_q")
    kv = _mm(mn, wt("w_ckv", mn), name="cross_kv")
    oc = _xattn_fwd(qc, kv, tq, "xattn_fwd")
    h3, n2 = _mm(oc, wt("w_co", oc), name="cross_out", out_dtype=F32, res=h2, norm_g=w["g_ffn2"])
    gu2, act2 = _ffn_up(n2, wt("w_ffn2_gu", n2), "ffn2_gu")

    dh4, dh4b, g["g_final"], loss_lanes = _down_loss(act2, wt("w_ffn2_down", act2), h3, tgt, w["g_final"], "ffn2_down_loss")

    (dh3, dh3b), tok = ffn_bwd(dh4, dh4b, h3, (n2, gu2, act2), "g_ffn2", "w_ffn2_gu", "w_ffn2_down", "ffn2", copy_scale=1.0)
    g["w_co"] = _mm(oc, dh3b, ta=True, name="cross_dwco", after=tok)
    doc = _mm(dh3b, w["w_co"], tb=True, name="cross_doc")
    dqc, dk, dv = _xattn_bwd(qc, kv, doc, tq, "xattn_bwd")
    dkv = jnp.concatenate([dk, dv], axis=1)
    g["w_cq"] = _mm(hn, dqc, ta=True, name="cross_dwcq")
    g["w_ckv"] = _mm(mn, dkv, ta=True, name="cross_dwckv")
    dmn = _mm(dkv, w["w_ckv"], tb=True, name="cross_dmn", out_dtype=F32)
    g["g_mem"] = _rowcall(lambda dy, xb: dy * _xhat(xb)[0], [_whole(dmn), _whole(mem)], [], [], [d],
                          tm=_pick(mem.shape[0], (256, 128)), name="mem_dnorm")[0]
    dh2, dh2b, g["g_cross"] = _dgrad_norm(dqc, w["w_cq"], dh3, h2, w["g_cross"], "cross_dhn", copy_scale=1.0, after=emit("cross", g))

    g["w_o"] = _mm(merged, dh2b, ta=True, name="mix_dwo", after=tick("cross", dh2))
    da_conv, da_sb, dgc, dgs, db_conv, db_sb = _mix_dmerge(dh2b, w["w_o"], a_conv, a_sb, proj, (6, 7), b_conv, b_sb, "mix_dmerge")
    g["b_gate"] = jnp.concatenate([db_conv, db_sb], axis=1)
    g["w_conv_out"] = _mm(y_conv, da_conv, ta=True, name="conv_dwout")
    g["w_attn_out"] = _mm(y_sb, da_sb, ta=True, name="attn_dwout")
    dy_conv = _mm(da_conv, w["w_conv_out"], tb=True, name="conv_dy")
    dy_sb = _mm(da_sb, w["w_attn_out"], tb=True, name="attn_dy")
    dcb, dcc, dcx, g["conv_w"] = _conv_bwd(dy_conv, proj, w["conv_w"], d, tc, "conv_bwd")
    dq, dk_sb, dv_sb = _sb_bwd(proj, y_sb, sb_a, sb_beta, dy_sb, heads, sb_cols, sb_tq, sb_tk, "sb_bwd")
    dproj = jnp.concatenate([dcb, dcc, dcx, dq, dk_sb, dv_sb, dgc, dgs], axis=1)
    g["w_in"] = _mm(u, dproj, ta=True, name="mix_dwin")
    dh1, dh1b, g["g_mix"] = _dgrad_norm(dproj, w["w_in"], dh2, h1, w["g_mix"], "mix_du", copy_scale=0.5, after=emit("mix", g))
    (dx,), tok = ffn_bwd(dh1, dh1b, x, (n1, gu1, act1), "g_ffn1", "w_ffn1_gu", "w_ffn1_down", "ffn1", after=tick("mix", dh1))
    return loss_lanes, dx, g, tok


MATS = (("w_ffn1_gu", "col"), ("w_ffn1_down", "row"), ("w_in", "col"), ("w_conv_out", "row"), ("w_attn_out", "row"),
        ("w_o", "row"), ("w_cq", "row"), ("w_ckv", "col"), ("w_co", "row"), ("w_ffn2_gu", "col"), ("w_ffn2_down", "row"))
VECS = ("g_ffn1", "g_mix", "g_cross", "g_mem", "g_ffn2", "g_final")
WEIGHTS = ("g_ffn1", "w_ffn1_gu", "w_ffn1_down", "g_mix", "w_in", "b_gate", "conv_w", "w_conv_out", "w_attn_out", "w_o",
           "g_cross", "g_mem", "w_cq", "w_ckv", "w_co", "g_ffn2", "w_ffn2_gu", "w_ffn2_down", "g_final")
CONV_ROWS = 16


def _full_shape(kind, r, c):
    return (r, N_CHIPS * c) if kind == "col" else (N_CHIPS * r, c)


def _piece(ref, kind, r, c, chip, half):
    hr = r // 2
    if kind == "col":
        return ref.at[pl.ds(pl.multiple_of(half * hr, math.gcd(hr, 16)), hr), pl.ds(pl.multiple_of(chip * c, LANES), c)]
    return ref.at[pl.ds(pl.multiple_of(chip * r + half * hr, math.gcd(hr, 16)), hr), :]


def _shard_of(ref, kind, r, c, chip):
    if kind == "col":
        return ref.at[:, pl.ds(pl.multiple_of(chip * c, LANES), c)]
    return ref.at[pl.ds(pl.multiple_of(chip * r, 16), r), :]


def _place():
    x, y, c = lax.axis_index("x"), lax.axis_index("y"), lax.axis_index("c")
    others = [(1 - x, y), (x, 1 - y), (1 - x, 1 - y)]
    return x, y, c, 2 * x + y, others


def _remote(src, dst, send_sem, recv_sem, to):
    return pltpu.make_async_remote_copy(src_ref=src, dst_ref=dst, send_sem=send_sem, recv_sem=recv_sem,
                                        device_id=to, device_id_type=MESH)


HBM = pl.BlockSpec(memory_space=pltpu.HBM)
SEM = pl.BlockSpec(memory_space=pltpu.SEMAPHORE)
EFFECT = pltpu.SideEffectType.DATAFLOW_SIDE_EFFECTING
TOKEN = (8, LANES)


def _split_start(name, plan, n_copies, srcs, lands, after=None):
    ns, nl = len(srcs), len(lands)
    n_in = ns + nl + (after is not None)

    def body(*refs):
        outs = refs[n_in:]
        sends, _ = plan(refs[:ns], refs[ns:ns + nl], outs[0], outs[1])
        for cp in sends:
            cp.start()
        outs[-1][...] = jnp.zeros(TOKEN, F32)

    held = [pltpu.HBM(a.shape, a.dtype) for a in (*srcs, *lands)]
    dma = pltpu.SemaphoreType.DMA((n_copies,))
    ins = [pltpu.with_memory_space_constraint(a, pltpu.HBM) for a in (*srcs, *lands)]
    outs = _pcall(
        body, name=name, in_specs=[HBM] * (ns + nl) + ([] if after is None else [ANY]),
        out_specs=(SEM, SEM, *[HBM] * (ns + nl), pl.BlockSpec(memory_space=pltpu.VMEM)),
        out_shape=(dma, dma, *held, jax.ShapeDtypeStruct(TOKEN, F32)),
        input_output_aliases={i: 2 + i for i in range(ns + nl)},
        compiler_params=pltpu.CompilerParams(has_side_effects=EFFECT),
    )(*ins, *([] if after is None else [after]))
    return outs[0], outs[1], list(outs[2:2 + ns]), list(outs[2 + ns:2 + ns + nl]), outs[-1]


def _split_wait(name, plan, send_sems, recv_sems, srcs, lands, after):
    ns, nl = len(srcs), len(lands)

    def body(*refs):
        sends, recvs = plan(refs[:ns], refs[ns:ns + nl], refs[ns + nl], refs[ns + nl + 1])
        for cp in sends:
            cp.wait_send()
        for cp in recvs:
            cp.wait_recv()

    outs = _pcall(
        body, name=name, in_specs=[HBM] * (ns + nl) + [SEM, SEM, ANY], out_specs=[HBM] * (ns + nl),
        out_shape=[pltpu.HBM(a.shape, a.dtype) for a in (*srcs, *lands)],
        input_output_aliases={i: i for i in range(ns + nl)},
        compiler_params=pltpu.CompilerParams(has_side_effects=EFFECT),
    )(*srcs, *lands, send_sems, recv_sems, after)
    return list(outs[:ns]), list(outs[ns:])


def _gather_plan(dims):
    def plan(shard_refs, full_refs, ss, rs):
        x, y, c, me, others = _place()
        sends, recvs = [], []
        for wi, (kind, r, cw) in enumerate(dims):
            half = shard_refs[wi].at[pl.ds(pl.multiple_of(c * (r // 2), math.gcd(r // 2, 16)), r // 2), :]
            for k, (ox, oy) in enumerate(others):
                sem = 4 * wi + k
                sends.append(_remote(half, _piece(full_refs[wi], kind, r, cw, me, c), ss.at[sem], rs.at[sem], (ox, oy, c)))
                recvs.append(_remote(half, _piece(full_refs[wi], kind, r, cw, 2 * ox + oy, c), ss.at[sem], rs.at[sem], (x, y, c)))
            sem = 4 * wi + 3
            own = _remote(shard_refs[wi], _shard_of(full_refs[wi], kind, r, cw, me), ss.at[sem], rs.at[sem], (x, y, 1 - c))
            sends.append(own)
            recvs.append(own)
        return sends, recvs

    return plan


def _forward_plan(dims):
    def plan(_, full_refs, ss, rs):
        x, y, c, _, others = _place()
        sends, recvs = [], []
        for wi, (kind, r, cw) in enumerate(dims):
            for k, (ox, oy) in enumerate(others):
                sem = 3 * wi + k
                mine = _piece(full_refs[wi], kind, r, cw, 2 * ox + oy, c)
                theirs = _piece(full_refs[wi], kind, r, cw, 2 * ox + oy, 1 - c)
                sends.append(_remote(mine, mine, ss.at[sem], rs.at[sem], (x, y, 1 - c)))
                recvs.append(_remote(theirs, theirs, ss.at[sem], rs.at[sem], (x, y, 1 - c)))
        return sends, recvs

    return plan


def _rs_cores_plan(dims):
    def plan(g_refs, land_refs, ss, rs):
        x, y, c, _, _ = _place()
        sends, recvs = [], []
        for wi, dm in enumerate(dims):
            for chip in range(N_CHIPS):
                sem = N_CHIPS * wi + chip
                sends.append(_remote(_piece(g_refs[wi], *dm, chip, 1 - c), land_refs[wi].at[chip], ss.at[sem], rs.at[sem], (x, y, 1 - c)))
                recvs.append(_remote(_piece(g_refs[wi], *dm, chip, c), land_refs[wi].at[chip], ss.at[sem], rs.at[sem], (x, y, 1 - c)))
        return sends, recvs

    return plan


def _share_plan(nw):
    def plan(_, buf_refs, ss, rs):
        x, y, c, _, _ = _place()
        sends = [_remote(buf_refs[wi].at[c], buf_refs[wi].at[c], ss.at[wi], rs.at[wi], (x, y, 1 - c)) for wi in range(nw)]
        recvs = [_remote(buf_refs[wi].at[1 - c], buf_refs[wi].at[1 - c], ss.at[wi], rs.at[wi], (x, y, 1 - c)) for wi in range(nw)]
        return sends, recvs

    return plan


def _small_plan():
    def plan(_, buf_refs, ss, rs):
        x, y, c = lax.axis_index("x"), lax.axis_index("y"), lax.axis_index("c")
        buf = buf_refs[0]
        sends, recvs = [], []
        for rel in range(1, N_DEV):
            peer = (x ^ (rel >> 2 & 1), y ^ (rel >> 1 & 1), c ^ (rel & 1))
            sends.append(_remote(buf.at[0], buf.at[rel], ss.at[rel - 1], rs.at[rel - 1], peer))
            recvs.append(_remote(buf.at[0], buf.at[rel], ss.at[rel - 1], rs.at[rel - 1], peer))
        return sends, recvs

    return plan


def _sum_small(buf, me, name):
    _, rows, n = buf.shape

    def body(me_ref, b_ref, o_ref):
        tot = b_ref[me_ref[0]]
        for dev in range(1, N_DEV):
            tot = tot + b_ref[dev ^ me_ref[0]]
        o_ref[...] = tot

    return _pcall(
        body, name=name, out_shape=jax.ShapeDtypeStruct((rows, n), F32),
        grid_spec=pltpu.PrefetchScalarGridSpec(
            num_scalar_prefetch=1, grid=(1,), in_specs=[pl.BlockSpec((N_DEV, rows, n), lambda i, m: (0, 0, 0))],
            out_specs=pl.BlockSpec((rows, n), lambda i, m: (0, 0))),
    )(me, buf)


def _rs_chips_plan(nw):
    def plan(p_refs, land_refs, ss, rs):
        x, y, c, me, others = _place()
        sends, recvs = [], []
        for wi in range(nw):
            for k, (ox, oy) in enumerate(others):
                sem = 3 * wi + k
                sends.append(_remote(p_refs[wi].at[2 * ox + oy], land_refs[wi].at[k], ss.at[sem], rs.at[sem], (ox, oy, c)))
                recvs.append(_remote(p_refs[wi].at[me], land_refs[wi].at[k], ss.at[sem], rs.at[sem], (x, y, c)))
        return sends, recvs

    return plan


SUM_BLOCK_BYTES = 4 << 20


def _rows_per_block(n, c, limit_bytes=2 << 20):
    best = None
    for tm in range(16, n + 1, 16):
        if n % tm == 0 and tm * c * 4 <= limit_bytes:
            best = tm
    return best or n


def _sum_cores(grad, got, kind, place, name):
    _, hr, cw = got.shape
    tm = _rows_per_block(hr, cw, SUM_BLOCK_BYTES)
    nb = hr // tm

    def body(place_ref, g_ref, t_ref, o_ref):
        o_ref[...] = (g_ref[...].astype(F32) + t_ref[...].astype(F32)).astype(o_ref.dtype)

    if kind == "col":
        g_spec = pl.BlockSpec((tm, cw), lambda j, i, pr: (pr[0] * nb + i, j))
    else:
        g_spec = pl.BlockSpec((tm, cw), lambda j, i, pr: ((2 * j + pr[0]) * nb + i, 0))
    blk = pl.BlockSpec((None, tm, cw), lambda j, i, pr: (j, i, 0))
    return _pcall(
        body, name=name, out_shape=jax.ShapeDtypeStruct(got.shape, BF16),
        grid_spec=pltpu.PrefetchScalarGridSpec(num_scalar_prefetch=1, grid=(N_CHIPS, nb), in_specs=[g_spec, blk], out_specs=blk),
        compiler_params=_params("parallel", "parallel"),
    )(place, grad, got)


def _sum_chips(parts, got, place, name):
    _, n, cw = got.shape
    tm = _rows_per_block(n, cw, SUM_BLOCK_BYTES)

    def body(place_ref, p_ref, g_ref, o_ref):
        tot = p_ref[...].astype(F32)
        for k in range(3):
            tot = tot + g_ref[k].astype(F32)
        o_ref[...] = tot

    return _pcall(
        body, name=name, out_shape=jax.ShapeDtypeStruct((2, n, cw), F32),
        grid_spec=pltpu.PrefetchScalarGridSpec(
            num_scalar_prefetch=1, grid=(n // tm,),
            in_specs=[pl.BlockSpec((None, tm, cw), lambda i, pr: (pr[1], i, 0)), pl.BlockSpec((3, tm, cw), lambda i, pr: (0, i, 0))],
            out_specs=pl.BlockSpec((None, tm, cw), lambda i, pr: (pr[0], i, 0))),
        compiler_params=_params("parallel"),
    )(place, parts, got)


def _adamw(g, w, m, v, name):
    n, c = g.shape
    c1 = 1.0 - ADAM_B1 ** ADAM_STEP
    c2 = 1.0 - ADAM_B2 ** ADAM_STEP

    def fn(gb, wb, mb, vb):
        m_new = ADAM_B1 * mb + (1.0 - ADAM_B1) * gb
        v_new = ADAM_B2 * vb + (1.0 - ADAM_B2) * (gb * gb)
        delta = -ADAM_LR * ((m_new / c1) / (jnp.sqrt(v_new / c2) + ADAM_EPS) + ADAM_WD * wb)
        return gb, delta, m_new, v_new

    tm = _rows_per_block(n, c) if n % 16 == 0 else n
    return _rowcall(fn, [_whole(g), _whole(w), _whole(m), _whole(v)], [], [(c, F32)] * 4, tm=tm, name=name)


PACK_ROWS = 16


def _pack_rows(parts, width, name, after=None):
    assert sum(p.shape[0] for p in parts) <= PACK_ROWS

    def body(*refs):
        out_ref = refs[-1]
        out_ref[...] = jnp.zeros_like(out_ref)
        at = 0
        for r in refs[:len(parts)]:
            k, n = r.shape
            if n == width:
                out_ref[at:at + k, :] = r[...]
            else:
                out_ref[at:at + k, :] = jnp.broadcast_to(r[:, :1], (k, width))
            at += k

    vm = pl.BlockSpec(memory_space=pltpu.VMEM)
    return _pcall(body, name=name, in_specs=[vm] * len(parts) + ([] if after is None else [ANY]), out_specs=vm,
                  out_shape=jax.ShapeDtypeStruct((PACK_ROWS, width), F32))(*parts, *([] if after is None else [after]))


def _cast_shard(wm, name, after):
    n, c = wm.shape
    return _rowcall(lambda v: v, [_whole(wm)], [], [(c, BF16)], tm=_rows_per_block(n, c), name=name, after=after)[0]


GATHER_GROUPS = (
    ("w_ffn1_gu", "conv_w"), ("w_ffn1_down",), ("w_in",), ("w_conv_out", "w_attn_out", "w_o"), ("w_cq", "w_ckv", "w_co"),
    ("w_ffn2_gu", "w_ffn2_down"),
)
REDUCE_GROUPS = {
    "ffn2": ("w_ffn2_down", "w_ffn2_gu"),
    "cross": ("w_co", "w_cq", "w_ckv"),
    "mix": ("w_o", "w_conv_out", "w_attn_out", "w_in"),
    "ffn1_down": ("w_ffn1_down",),
    "ffn1": ("w_ffn1_gu",),
}
TAIL_STAGES = (("ffn2", "cross"), ("mix",), ("ffn1_down", "ffn1"))
KIND = dict(MATS)


def _step(x, mem, tgt, wts, m_in, v_in):
    d = x.shape[-1]
    cc = wts["conv_w"].shape[1]
    place = jnp.stack([lax.axis_index("c"), 2 * lax.axis_index("x") + lax.axis_index("y")]).astype(jnp.int32)
    dims = {n: (kind, *wts[n].shape) for n, kind in MATS}
    dims["conv_w"] = ("col", CONV_ROWS, cc)

    w = {n: wts[n].reshape(1, -1) for n in VECS + ("b_gate",)}
    flying, token = {}, None
    for names in GATHER_GROUPS:
        gd = [dims[n] for n in names]
        shards = [jnp.pad(wts[n], ((0, CONV_ROWS - CONV_K), (0, 0))) if n == "conv_w" else _cast_shard(wts[n], "cast_" + n, token)
                  for n in names]
        lands = [lax.empty(_full_shape(*dm), sh.dtype) for dm, sh in zip(gd, shards)]
        plan = _gather_plan(gd)
        ss, rs, srcs, lands, token = _split_start("gather_start_" + names[0], plan, 4 * len(names), shards, lands, token)
        flying.update({n: (names, plan, ss, rs, srcs, lands, gd) for n in names})

    passing = {}

    def prefetch(name, after):
        if name not in passing:
            names, plan, ss, rs, srcs, lands, gd = flying[name]
            _, lands = _split_wait("gather_wait_" + names[0], plan, ss, rs, srcs, lands, after)
            plan = _forward_plan(gd)
            ss, rs, _, lands, _ = _split_start("forward_start_" + names[0], plan, 3 * len(names), [], lands)
            passing.update({n: (names, plan, ss, rs, lands) for n in names})

    def fetch(name, after):
        prefetch(name, after)
        names, plan, ss, rs, lands = passing[name]
        _, lands = _split_wait("forward_wait_" + names[0], plan, ss, rs, [], lands, after)
        return {n: (land[:CONV_K] if n == "conv_w" else land) for n, land in zip(names, lands)}

    swapping, sent = {}, {}

    def emit(tag, g):
        if tag not in REDUCE_GROUPS:
            return None
        names = REDUCE_GROUPS[tag]
        gd = [dims[n] for n in names]
        lands = [lax.empty((N_CHIPS, r // 2, cw), BF16) for (_, r, cw) in gd]
        plan = _rs_cores_plan(gd)
        ss, rs, srcs, lands, tok = _split_start("rs_cores_start_" + tag, plan, N_CHIPS * len(names), [g[n] for n in names], lands)
        swapping[tag] = (plan, ss, rs, srcs, lands)
        return tok

    def tick(tag, after):
        if tag not in REDUCE_GROUPS:
            return None
        names = REDUCE_GROUPS[tag]
        plan, ss, rs, srcs, lands = swapping[tag]
        mine, got = _split_wait("rs_cores_wait_" + tag, plan, ss, rs, srcs, lands, after)
        parts = [_sum_cores(gm, t, KIND[n], place, "sum_cores_" + n) for n, gm, t in zip(names, mine, got)]
        lands = [lax.empty((3, *p.shape[1:]), BF16) for p in parts]
        plan = _rs_chips_plan(len(names))
        ss, rs, srcs, lands, tok = _split_start("rs_chips_start_" + tag, plan, 3 * len(names), parts, lands)
        sent[tag] = (plan, ss, rs, srcs, lands)
        return tok

    loss_lanes, dx, g, last = _local_step(x[0], mem[0], tgt[0], w, fetch, prefetch, emit, tick, token)

    rows = [g[n] for n in VECS] + [g["b_gate"][:, :d], g["b_gate"][:, d:], g["conv_w"], loss_lanes]
    packed = _pack_rows(rows, d, "pack_small", after=last)
    small = jnp.concatenate([packed[None], jnp.zeros((N_DEV - 1, *packed.shape), F32)], axis=0)
    small_plan = _small_plan()
    small_ss, small_rs, _, small, after = _split_start("small_start", small_plan, N_DEV - 1, [], [small])

    grads, out = {}, {}

    def update(n):
        shape = wts[n].shape
        as2d = (lambda a: a.reshape(1, -1)) if len(shape) == 1 else (lambda a: a)
        return [r.reshape(shape) for r in _adamw(grads[n], as2d(wts[n]), as2d(m_in[n]), as2d(v_in[n]), "adamw_" + n)]

    def finish(sharing, after):
        tag, names, plan, ss, rs, halves = sharing
        _, both = _split_wait("share_wait_" + tag, plan, ss, rs, [], halves, after)
        for n, b in zip(names, both):
            grads[n] = b.reshape(-1, b.shape[-1])
            out[n] = update(n)
        return out[names[-1]][1]

    sharing = None
    for stage in TAIL_STAGES:
        names, halves = [], []
        for tag in stage:
            plan, ss, rs, srcs, lands = sent[tag]
            parts, landed = _split_wait("rs_chips_wait_" + tag, plan, ss, rs, srcs, lands, after)
            halves += [_sum_chips(p, t, place, "sum_chips_" + n) for n, p, t in zip(REDUCE_GROUPS[tag], parts, landed)]
            names += REDUCE_GROUPS[tag]
        plan = _share_plan(len(names))
        ss, rs, _, halves, after = _split_start("share_start_" + stage[0], plan, len(names), [], halves)
        if sharing is not None:
            after = finish(sharing, after)
        sharing = (stage[0], names, plan, ss, rs, halves)
    after = finish(sharing, after)

    _, small = _split_wait("small_wait", small_plan, small_ss, small_rs, [], small, after)
    me = (4 * lax.axis_index("x") + 2 * lax.axis_index("y") + lax.axis_index("c")).astype(jnp.int32).reshape(1)
    red = _sum_small(small[0], me, "sum_small")
    grads.update({n: red[i:i + 1] for i, n in enumerate(VECS)})
    nv = len(VECS)
    grads["b_gate"] = jnp.concatenate([red[nv:nv + 1], red[nv + 1:nv + 2]], axis=1)
    chip = 2 * lax.axis_index("x") + lax.axis_index("y")
    grads["conv_w"] = lax.dynamic_slice_in_dim(red[nv + 2:nv + 2 + CONV_K], chip * cc, cc, axis=1)
    loss = red[nv + 2 + CONV_K, 0]
    out.update({n: update(n) for n in WEIGHTS if n not in KIND})
    return (loss, dx[None], *[out[n][0] for n in WEIGHTS], *[out[n][1] for n in WEIGHTS],
            *[out[n][2] for n in WEIGHTS], *[out[n][3] for n in WEIGHTS])


def kernel(x, mem, g_ffn1, w_ffn1_gu, w_ffn1_down, g_mix, w_in, b_gate, conv_w, w_conv_out, w_attn_out, w_o, g_cross, g_mem, w_cq, w_ckv, w_co, g_ffn2, w_ffn2_gu, w_ffn2_down, g_final, loss_target, m_g_ffn1, m_w_ffn1_gu, m_w_ffn1_down, m_g_mix, m_w_in, m_b_gate, m_conv_w, m_w_conv_out, m_w_attn_out, m_w_o, m_g_cross, m_g_mem, m_w_cq, m_w_ckv, m_w_co, m_g_ffn2, m_w_ffn2_gu, m_w_ffn2_down, m_g_final, v_g_ffn1, v_w_ffn1_gu, v_w_ffn1_down, v_g_mix, v_w_in, v_b_gate, v_conv_w, v_w_conv_out, v_w_attn_out, v_w_o, v_g_cross, v_g_mem, v_w_cq, v_w_ckv, v_w_co, v_g_ffn2, v_w_ffn2_gu, v_w_ffn2_down, v_g_final):
    given = dict(locals())
    wts = {n: given[n] for n in WEIGHTS}
    m_in = {n: given["m_" + n] for n in WEIGHTS}
    v_in = {n: given["v_" + n] for n in WEIGHTS}
    return _step(x, mem, loss_target, wts, m_in, v_in)
```

```python
import math

import jax
import jax.numpy as jnp
from jax import lax
from jax.experimental import pallas as pl
from jax.experimental.pallas import tpu as pltpu

F32 = jnp.float32
BF16 = jnp.bfloat16
MESH = pl.DeviceIdType.MESH

V7X_VMEM_LIMIT_BYTES = 48 * 1024 * 1024
MM_VMEM_BUDGET_BYTES = 36 * 1024 * 1024
MM_WHOLE_K = 2816
LANES = 128
SB_HEAD_DIM = 128
X_HEADS = 4
CONV_K = 3
RMS_EPS = 1e-6
N_CHIPS = 4
N_DEV = 8
ADAM_LR, ADAM_B1, ADAM_B2, ADAM_EPS, ADAM_WD, ADAM_STEP = 0.001, 0.9, 0.999, 1e-08, 0.01, 10


ANY = pl.BlockSpec(memory_space=pl.ANY)


def _pcall(body, **kw):
    return pl.pallas_call(body, **kw)


def _params(*sem):
    return pltpu.CompilerParams(dimension_semantics=sem, vmem_limit_bytes=V7X_VMEM_LIMIT_BYTES)


def _pick(dim, cands):
    for c in cands:
        if dim % c == 0:
            return c
    return dim


def _dot(a, b, ca, cb):
    return lax.dot_general(a, b, (((ca,), (cb,)), ((), ())), preferred_element_type=F32)


def _mm(a, b, *, name, ta=False, tb=False, out_dtype=BF16, res=None, alpha=1.0, tm=None, tn=None, tk=None, after=None,
        a_halves=False, b_halves=False, norm_g=None):
    assert not (a_halves and ta) and not (b_halves and tb) and not (norm_g is not None and ta)
    if a_halves:
        m, k = a.shape[1], 2 * a.shape[2]
    else:
        m, k = (a.shape[1], a.shape[0]) if ta else a.shape
    if b_halves:
        n = 2 * b.shape[2]
        assert k == b.shape[1]
    else:
        n = b.shape[0] if tb else b.shape[1]
        assert k == (b.shape[1] if tb else b.shape[0]), (a.shape, b.shape, ta, tb)
    if ta:
        tm = tm or _pick(m, (512, 256, 128))
        tn = tn or _pick(n, (2048, 1024, 512, 256, 128))
        tk = tk or (k if k <= MM_WHOLE_K else _pick(k, (1024, 512, 256, 128)))
    else:
        tk = tk or (k if k <= MM_WHOLE_K else _pick(k, (MM_WHOLE_K, 2048, 1024, 512, 256, 128)))
        tn = tn or (n if norm_g is not None else _pick(n, (512, 1408, 256, 128) if tk == k else (1024, 512, 256, 128)))
        out_bytes = jnp.dtype(out_dtype).itemsize + (0 if res is None else res.dtype.itemsize) + (0 if norm_g is None else 2)
        per_row = 2 * (tk * a.dtype.itemsize + tn * out_bytes)
        per_row += 4 * tn if tk < k else 0
        rows = (MM_VMEM_BUDGET_BYTES - 2 * tk * tn * b.dtype.itemsize) // per_row
        tm = tm or next((c for c in (2048, 1024, 512, 256, 128) if m % c == 0 and c <= rows), m)
    if a_halves:
        tk = min(tk, k // 2) if (k // 2) % min(tk, k // 2) == 0 else _pick(k // 2, (1408, 1024, 512, 256, 128))
    if b_halves:
        tn = tn if (n // 2) % tn == 0 else _pick(n // 2, (2816, 1408, 1024, 512, 256, 128) if ta else (1408, 1024, 512, 256, 128))
    nk = k // tk
    assert m % tm == 0 and n % tn == 0 and k % tk == 0
    a_spec = pl.BlockSpec((tk, tm), lambda i, j, kk: (kk, i)) if ta else pl.BlockSpec((tm, tk), lambda i, j, kk: (i, kk))
    b_spec = pl.BlockSpec((tn, tk), lambda i, j, kk: (j, kk)) if tb else pl.BlockSpec((tk, tn), lambda i, j, kk: (kk, j))
    if a_halves:
        per = (k // 2) // tk
        a_spec = pl.BlockSpec((None, tm, tk), lambda i, j, kk: (kk // per, i, kk % per))
    if b_halves:
        per_n = (n // 2) // tn
        b_spec = pl.BlockSpec((None, tk, tn), lambda i, j, kk: (j // per_n, kk, j % per_n))
    o_spec = pl.BlockSpec((tm, tn), lambda i, j, kk: (i, j))
    ca, cb = (0 if ta else 1), (1 if tb else 0)

    n_in = 2 + (res is not None) + (norm_g is not None) + (after is not None)
    n_out = 1 + (norm_g is not None)

    def body(*refs):
        a_ref, b_ref = refs[:2]
        res_ref = refs[2] if res is not None else None
        g_ref = refs[2 + (res is not None)] if norm_g is not None else None
        o_ref = refs[n_in]
        scratch = refs[n_in + n_out:]

        def finish(acc):
            val = acc if alpha == 1.0 else alpha * acc
            if res_ref is not None:
                val = res_ref[...].astype(F32) + val
            o_ref[...] = val.astype(o_ref.dtype)
            if g_ref is not None:
                refs[n_in + 1][...] = (_xhat(val)[0] * g_ref[...]).astype(BF16)

        part = _dot(a_ref[...].astype(BF16), b_ref[...].astype(BF16), ca, cb)
        if nk == 1:
            finish(part)
        else:
            acc_ref = scratch[0]
            kk = pl.program_id(2)

            @pl.when(kk == 0)
            def _():
                acc_ref[...] = part

            @pl.when(kk > 0)
            def _():
                acc_ref[...] += part

            @pl.when(kk == nk - 1)
            def _():
                finish(acc_ref[...])

    ins = [a, b] + ([] if res is None else [res]) + ([] if norm_g is None else [norm_g]) + ([] if after is None else [after])
    in_specs = [a_spec, b_spec] + ([] if res is None else [o_spec])
    in_specs += ([] if norm_g is None else [pl.BlockSpec((1, tn), lambda i, j, kk: (0, j))]) + ([] if after is None else [ANY])
    outs = _pcall(
        body, name=name, grid=(m // tm, n // tn, nk), in_specs=in_specs, out_specs=[o_spec] * n_out,
        out_shape=[jax.ShapeDtypeStruct((m, n), out_dtype)] + [jax.ShapeDtypeStruct((m, n), BF16)] * (n_out - 1),
        scratch_shapes=[pltpu.VMEM((tm, tn), F32)] if nk > 1 else [],
        compiler_params=_params("parallel", "parallel", "arbitrary"),
    )(*ins)
    return outs[0] if norm_g is None else outs


def _rowcall(fn, rows, consts, outs, accs=(), *, tm, name, after=None):
    s = rows[0][0].shape[0]
    assert s % tm == 0
    n_read, n_out = len(rows) + len(consts), len(outs)
    n_in = n_read + (after is not None)

    def body(*refs):
        vals = fn(*[r[...] for r in refs[:n_read]])
        vals = vals if isinstance(vals, (tuple, list)) else (vals,)
        for o_ref, v in zip(refs[n_in:n_in + n_out], vals[:n_out]):
            o_ref[...] = v.astype(o_ref.dtype)
        if accs:
            first = pl.program_id(0) == 0
            for a_ref, v in zip(refs[n_in + n_out:], vals[n_out:]):
                tot = jnp.sum(v.astype(F32), axis=0, keepdims=True)

                @pl.when(first)
                def _(a_ref=a_ref, tot=tot):
                    a_ref[...] = tot

                @pl.when(jnp.logical_not(first))
                def _(a_ref=a_ref, tot=tot):
                    a_ref[...] += tot

    in_specs = [pl.BlockSpec((tm, w), lambda i, cb=cb: (i, cb)) for (_, cb, w) in rows]
    in_specs += [pl.BlockSpec(c.shape, lambda i: (0, 0)) for c in consts]
    in_specs += [] if after is None else [ANY]
    out_specs = [pl.BlockSpec((tm, w), lambda i: (i, 0)) for (w, _) in outs]
    out_specs += [pl.BlockSpec((1, w), lambda i: (0, 0)) for w in accs]
    out_shape = [jax.ShapeDtypeStruct((s, w), dt) for (w, dt) in outs]
    out_shape += [jax.ShapeDtypeStruct((1, w), F32) for w in accs]
    return _pcall(
        body, name=name, grid=(s // tm,), in_specs=in_specs, out_specs=out_specs, out_shape=out_shape,
        compiler_params=_params("arbitrary" if accs else "parallel"),
    )(*[r[0] for r in rows], *consts, *([] if after is None else [after]))


def _whole(a):
    return (a, 0, a.shape[1])


def _xhat(x):
    x = x.astype(F32)
    r = lax.rsqrt(jnp.mean(x * x, axis=-1, keepdims=True) + RMS_EPS)
    return x * r, r


def _rms_bwd(dy, x, g):
    xh, r = _xhat(x)
    dxh = dy.astype(F32) * g
    dx = r * (dxh - xh * jnp.mean(dxh * xh, axis=-1, keepdims=True))
    return dx, dy.astype(F32) * xh


def _sigmoid(x):
    return 1.0 / (1.0 + jnp.exp(-x))


def _rms_fwd(x, g, name, tm, after=None):
    d = x.shape[1]
    return _rowcall(lambda xb, gb: _xhat(xb)[0] * gb, [_whole(x)], [g], [(d, BF16)], tm=tm, name=name, after=after)[0]


def _silu_parts(gate):
    sg = _sigmoid(gate)
    return sg, gate * sg


def _ffn_up(n, w_gu, name):
    s, d = n.shape
    f = w_gu.shape[1] // 2
    tn = _pick(f, (1408, 1024, 512, 256, 128))
    tm = _pick(s, (1024, 512, 256, 128))
    nb = f // tn

    def body(n_ref, wg_ref, wu_ref, gu_ref, act_ref):
        nv = n_ref[...]
        gate = _dot(nv, wg_ref[...], 1, 0)
        up = _dot(nv, wu_ref[...], 1, 0)
        gu_ref[0] = gate.astype(gu_ref.dtype)
        gu_ref[1] = up.astype(gu_ref.dtype)
        act_ref[...] = (_silu_parts(gate)[1] * up).astype(act_ref.dtype)

    return _pcall(
        body, name=name, grid=(s // tm, nb),
        in_specs=[pl.BlockSpec((tm, d), lambda i, j: (i, 0)), pl.BlockSpec((d, tn), lambda i, j: (0, j)),
                  pl.BlockSpec((d, tn), lambda i, j: (0, nb + j))],
        out_specs=[pl.BlockSpec((2, tm, tn), lambda i, j: (0, i, j)), pl.BlockSpec((tm, tn), lambda i, j: (i, j))],
        out_shape=[jax.ShapeDtypeStruct((2, s, f), BF16), jax.ShapeDtypeStruct((s, f), BF16)],
        compiler_params=_params("parallel", "parallel"),
    )(n, w_gu, w_gu)


def _ffn_dgu(dhb, w_down, gu, name, after=None):
    s, d = dhb.shape
    f = w_down.shape[0]
    tn = _pick(f, (1408, 1024, 512, 256, 128))
    tm = _pick(s, (1024, 512, 256, 128))

    def body(dh_ref, w_ref, gu_ref, *rest):
        o_ref = rest[-1]
        dact = _dot(dh_ref[...], w_ref[...], 1, 1)
        gate, up = gu_ref[0].astype(F32), gu_ref[1].astype(F32)
        sg, silu = _silu_parts(gate)
        o_ref[0] = (dact * up * (sg + silu * (1.0 - sg))).astype(o_ref.dtype)
        o_ref[1] = (dact * silu).astype(o_ref.dtype)

    blk = pl.BlockSpec((2, tm, tn), lambda i, j: (0, i, j))
    return _pcall(
        body, name=name, grid=(s // tm, f // tn),
        in_specs=[pl.BlockSpec((tm, d), lambda i, j: (i, 0)), pl.BlockSpec((tn, d), lambda i, j: (j, 0)), blk]
        + ([] if after is None else [ANY]),
        out_specs=blk, out_shape=jax.ShapeDtypeStruct((2, s, f), BF16), compiler_params=_params("parallel", "parallel"),
    )(dhb, w_down, gu, *([] if after is None else [after]))


def _dgrad_norm(dy, wmat, dh, x, g, name, *, dy_halves=False, copy_scale=None, after=None):
    s, d = dh.shape
    k = wmat.shape[1]
    tk = k if k <= MM_WHOLE_K else _pick(k, (MM_WHOLE_K, 2048, 1024, 512, 256, 128))
    if dy_halves and (k // 2) % tk:
        tk = _pick(k // 2, (1408, 1024, 512, 256, 128))
    tm = _pick(s, (512, 256, 128))
    nk, per = k // tk, (k // 2) // tk if dy_halves else 0
    n_in = 5 + (after is not None)
    n_out = 2 + (copy_scale is not None)

    def body(*refs):
        dy_ref, w_ref, dh_ref, x_ref, g_ref = refs[:5]
        outs, scratch = refs[n_in:n_in + n_out], refs[n_in + n_out:]
        i, kk = pl.program_id(0), pl.program_id(1)
        part = _dot(dy_ref[...], w_ref[...], 1, 1)

        def finish(dn):
            dx, dg = _rms_bwd(dn, x_ref[...], g_ref[...])
            tot = dh_ref[...] + dx
            outs[0][...] = tot
            if copy_scale is not None:
                outs[1][...] = (copy_scale * tot).astype(outs[1].dtype)
            dg = jnp.sum(dg, axis=0, keepdims=True)

            @pl.when(i == 0)
            def _():
                outs[-1][...] = dg

            @pl.when(i > 0)
            def _():
                outs[-1][...] += dg

        if nk == 1:
            finish(part)
        else:
            acc_ref = scratch[0]

            @pl.when(kk == 0)
            def _():
                acc_ref[...] = part

            @pl.when(kk > 0)
            def _():
                acc_ref[...] += part

            @pl.when(kk == nk - 1)
            def _():
                finish(acc_ref[...])

    row = pl.BlockSpec((tm, d), lambda i, kk: (i, 0))
    dy_spec = pl.BlockSpec((None, tm, tk), lambda i, kk: (kk // per, i, kk % per)) if dy_halves else pl.BlockSpec((tm, tk), lambda i, kk: (i, kk))
    in_specs = [dy_spec, pl.BlockSpec((d, tk), lambda i, kk: (0, kk)), row, row, pl.BlockSpec((1, d), lambda i, kk: (0, 0))]
    out_specs = [row] * (n_out - 1) + [pl.BlockSpec((1, d), lambda i, kk: (0, 0))]
    out_shape = [jax.ShapeDtypeStruct((s, d), F32)] + ([] if copy_scale is None else [jax.ShapeDtypeStruct((s, d), BF16)])
    return _pcall(
        body, name=name, grid=(s // tm, nk), in_specs=in_specs + ([] if after is None else [ANY]), out_specs=out_specs,
        out_shape=out_shape + [jax.ShapeDtypeStruct((1, d), F32)], scratch_shapes=[pltpu.VMEM((tm, d), F32)] if nk > 1 else [],
        compiler_params=_params("arbitrary", "arbitrary"),
    )(dy, wmat, dh, x, g, *([] if after is None else [after]))


def _shift_down(p, k):
    if k == 0:
        return p
    rows = lax.broadcasted_iota(jnp.int32, p.shape, 0)
    return jnp.where(rows >= k, pltpu.roll(p, k, 0), 0.0)


def _shift_up(p, k):
    if k == 0:
        return p
    s = p.shape[0]
    rows = lax.broadcasted_iota(jnp.int32, p.shape, 0)
    return jnp.where(rows < s - k, pltpu.roll(p, s - k, 0), 0.0)


def _conv_fwd(proj, conv_w, d, tc, name):
    s = proj.shape[0]
    nb = d // tc

    def body(cb_ref, cc_ref, cx_ref, w_ref, y_ref):
        p = cc_ref[...].astype(F32) * cx_ref[...].astype(F32)
        w = w_ref[...]
        acc = p * w[CONV_K - 1:CONV_K, :]
        for k in range(1, CONV_K):
            acc = acc + _shift_down(p, k) * w[CONV_K - 1 - k:CONV_K - k, :]
        y_ref[...] = (cb_ref[...].astype(F32) * acc).astype(y_ref.dtype)

    col = lambda off: pl.BlockSpec((s, tc), lambda j: (0, off * nb + j))
    return _pcall(
        body, name=name, grid=(nb,), in_specs=[col(0), col(1), col(2), pl.BlockSpec((CONV_K, tc), lambda j: (0, j))],
        out_specs=pl.BlockSpec((s, tc), lambda j: (0, j)), out_shape=jax.ShapeDtypeStruct((s, d), BF16),
        compiler_params=_params("parallel"),
    )(proj, proj, proj, conv_w)


def _conv_bwd(dy, proj, conv_w, d, tc, name):
    s = proj.shape[0]
    nb = d // tc

    def body(dy_ref, cb_ref, cc_ref, cx_ref, w_ref, dcb_ref, dcc_ref, dcx_ref, dw_ref):
        cc, cx = cc_ref[...].astype(F32), cx_ref[...].astype(F32)
        p = cc * cx
        w = w_ref[...]
        dyv = dy_ref[...].astype(F32)
        shifted = [_shift_down(p, CONV_K - 1 - k) for k in range(CONV_K)]
        conv = shifted[0] * w[0:1, :]
        for k in range(1, CONV_K):
            conv = conv + shifted[k] * w[k:k + 1, :]
        dcb_ref[...] = (dyv * conv).astype(dcb_ref.dtype)
        ds = dyv * cb_ref[...].astype(F32)
        dp = ds * w[CONV_K - 1:CONV_K, :]
        for k in range(1, CONV_K):
            dp = dp + _shift_up(ds, k) * w[CONV_K - 1 - k:CONV_K - k, :]
        dcc_ref[...] = (dp * cx).astype(dcc_ref.dtype)
        dcx_ref[...] = (dp * cc).astype(dcx_ref.dtype)
        for k in range(CONV_K):
            dw_ref[k:k + 1, :] = jnp.sum(ds * shifted[k], axis=0, keepdims=True)

    col = lambda off: pl.BlockSpec((s, tc), lambda j: (0, off * nb + j))
    blk = pl.BlockSpec((s, tc), lambda j: (0, j))
    wblk = pl.BlockSpec((CONV_K, tc), lambda j: (0, j))
    act = jax.ShapeDtypeStruct((s, d), BF16)
    return _pcall(
        body, name=name, grid=(nb,), in_specs=[blk, col(0), col(1), col(2), wblk],
        out_specs=[blk, blk, blk, wblk], out_shape=[act, act, act, jax.ShapeDtypeStruct((CONV_K, d), F32)],
        compiler_params=_params("parallel"),
    )(dy, proj, proj, proj, conv_w)


def _sb_tile(q, kj, scale, carry, tri, mask):
    z = _dot(q, kj, 1, 1) * scale
    lsz = jnp.minimum(z, 0.0) - jnp.log(1.0 + jnp.exp(-jnp.abs(z)))
    l1m = lsz - z
    if mask is not None:
        l1m = jnp.where(mask, l1m, 0.0)
    l1b = l1m.astype(BF16)
    a = jnp.exp(lsz + (carry + _dot(l1b, tri, 1, 0)))
    if mask is not None:
        a = jnp.where(mask, a, 0.0)
    return lsz, l1b, a.astype(BF16)


def _add_rows(x, upd, r0):
    return x + upd if r0 == 0 else jnp.concatenate([x[:r0], x[r0:] + upd], axis=0)


def _sb_masks(tq, tk):
    row = lax.broadcasted_iota(jnp.int32, (tq, tk), 0)
    col = lax.broadcasted_iota(jnp.int32, (tq, tk), 1)
    masks = [col + dj * tk < row for dj in range(tq // tk)]
    r2 = lax.broadcasted_iota(jnp.int32, (tk, tk), 0)
    c2 = lax.broadcasted_iota(jnp.int32, (tk, tk), 1)
    return masks, (r2 > c2).astype(BF16), (r2 < c2).astype(BF16)


def _sb_fwd(proj, heads, col0, tq, tk, name):
    s = proj.shape[0]
    dh = SB_HEAD_DIM
    nq, nd, nkt = s // tq, tq // tk, s // tk
    scale = dh ** -0.5

    def body(q_ref, k_ref, v_ref, o_ref, a_ref, b_ref):
        i = pl.program_id(1)
        q = q_ref[...]
        masks, tri_right, _ = _sb_masks(tq, tk)

        def tile(j, carry, acc, mask, r0=0):
            start = pl.multiple_of(j * tk, tk)
            kj = k_ref[pl.ds(start, tk), :]
            vj = v_ref[pl.ds(start, tk), :]
            lsz, l1b, ab = _sb_tile(q[r0:], kj, scale, carry[r0:], tri_right, None if mask is None else mask[r0:])
            a_ref[j, r0:, :] = ab
            b_ref[j, r0:, :] = jnp.exp(lsz).astype(b_ref.dtype)
            if r0:
                a_ref[j, :r0, :] = jnp.zeros((r0, tk), a_ref.dtype)
                b_ref[j, :r0, :] = jnp.zeros((r0, tk), b_ref.dtype)
            return (_add_rows(carry, jnp.sum(l1b.astype(F32), axis=1, keepdims=True), r0),
                    _add_rows(acc, _dot(ab, vj, 1, 0), r0))

        state = (jnp.zeros((tq, 1), F32), jnp.zeros((tq, dh), F32))
        for dj in reversed(range(nd)):
            state = tile(i * nd + dj, *state, masks[dj], dj * tk)
        def left_block(t, st):
            for dj in reversed(range(nd)):
                st = tile((i - 1 - t) * nd + dj, st[0], st[1], None)
            return st

        state = lax.fori_loop(0, i, left_block, state)
        o_ref[...] = state[1]

    qspec = pl.BlockSpec((tq, dh), lambda h, i: (i, col0[0] + h))
    kspec = pl.BlockSpec((s, dh), lambda h, i: (0, col0[1] + h))
    vspec = pl.BlockSpec((s, dh), lambda h, i: (0, col0[2] + h))
    saved = pl.BlockSpec((None, nkt, tq, tk), lambda h, i: (h, 0, i, 0))
    saved_shape = jax.ShapeDtypeStruct((heads, nkt, s, tk), BF16)
    return _pcall(
        body, name=name, grid=(heads, nq), in_specs=[qspec, kspec, vspec],
        out_specs=[pl.BlockSpec((tq, dh), lambda h, i: (i, h)), saved, saved],
        out_shape=[jax.ShapeDtypeStruct((s, heads * dh), F32), saved_shape, saved_shape],
        compiler_params=_params("parallel", "parallel"),
    )(proj, proj, proj)


SB_BWD_HEADS = 2


def _sb_bwd(proj, o, a_all, beta_all, do, heads, col0, tq, tk, name):
    s = proj.shape[0]
    dh = SB_HEAD_DIM
    nq, nd, nkt = s // tq, tq // tk, s // tk
    scale = dh ** -0.5
    hb = SB_BWD_HEADS if heads % SB_BWD_HEADS == 0 and all(c % SB_BWD_HEADS == 0 for c in col0) else 1
    wide = hb * dh

    def body(q_ref, k_ref, v_ref, o_ref, a_ref, b_ref, do_ref, dq_ref, dk_ref, dv_ref, dk_acc, dv_acc):
        i = pl.program_id(1)

        @pl.when(i == 0)
        def _():
            dk_acc[...] = jnp.zeros_like(dk_acc)
            dv_acc[...] = jnp.zeros_like(dv_acc)

        lanes = [slice(hh * dh, (hh + 1) * dh) for hh in range(hb)]
        q = [q_ref[:, ln] for ln in lanes]
        dob = [do_ref[:, ln].astype(BF16) for ln in lanes]
        delta = [jnp.sum(dob[hh].astype(F32) * o_ref[:, lanes[hh]], axis=1, keepdims=True) for hh in range(hb)]
        masks, _, tri_left = _sb_masks(tq, tk)

        def tile(hh, j, carry_g, dq, mask):
            start = pl.multiple_of(j * tk, tk)
            kj = k_ref[pl.ds(start, tk), lanes[hh]]
            vj = v_ref[pl.ds(start, tk), lanes[hh]]
            ab = a_ref[hh, j]
            g = _dot(dob[hh], vj, 1, 1) * ab.astype(F32)
            carry_g = carry_g + jnp.sum(g, axis=1, keepdims=True)
            left = (delta[hh] - carry_g) + _dot(g.astype(BF16), tri_left, 1, 0)
            dz = g - b_ref[hh, j].astype(F32) * (g + left)
            if mask is not None:
                dz = jnp.where(mask, dz, 0.0)
            dzb = dz.astype(BF16)
            dk_acc[pl.ds(start, tk), lanes[hh]] += _dot(dzb, q[hh], 0, 0)
            dv_acc[pl.ds(start, tk), lanes[hh]] += _dot(ab, dob[hh], 0, 0)
            return carry_g, dq + _dot(dzb, kj, 1, 0)

        def block(jb, st, use_masks):
            st = list(st)
            for dj in reversed(range(nd)):
                for hh in range(hb):
                    st[hh] = tile(hh, jb * nd + dj, *st[hh], masks[dj] if use_masks else None)
            return tuple(st)

        state = block(i, tuple((jnp.zeros((tq, 1), F32), jnp.zeros((tq, dh), F32)) for _ in range(hb)), True)
        state = lax.fori_loop(0, i, lambda t, st: block(i - 1 - t, st, False), state)
        for hh in range(hb):
            dq_ref[:, lanes[hh]] = (state[hh][1] * scale).astype(dq_ref.dtype)

        @pl.when(i == nq - 1)
        def _():
            dk_ref[...] = (dk_acc[...] * scale).astype(dk_ref.dtype)
            dv_ref[...] = dv_acc[...].astype(dv_ref.dtype)

    qspec = pl.BlockSpec((tq, wide), lambda h, i: (i, col0[0] // hb + h))
    kspec = pl.BlockSpec((s, wide), lambda h, i: (0, col0[1] // hb + h))
    vspec = pl.BlockSpec((s, wide), lambda h, i: (0, col0[2] // hb + h))
    blk = pl.BlockSpec((tq, wide), lambda h, i: (i, h))
    full = pl.BlockSpec((s, wide), lambda h, i: (0, h))
    saved = pl.BlockSpec((hb, nkt, tq, tk), lambda h, i: (h, 0, i, 0))
    act = jax.ShapeDtypeStruct((s, heads * dh), BF16)
    return _pcall(
        body, name=name, grid=(heads // hb, nq), in_specs=[qspec, kspec, vspec, blk, saved, saved, blk],
        out_specs=[blk, full, full], out_shape=[act, act, act],
        scratch_shapes=[pltpu.VMEM((s, wide), F32), pltpu.VMEM((s, wide), F32)],
        compiler_params=_params("parallel", "arbitrary"),
    )(proj, proj, proj, o, a_all, beta_all, do)


def _xattn_probs(q, k, scale):
    sc = _dot(q, k, 1, 1) * scale
    e = jnp.exp(sc - jnp.max(sc, axis=1, keepdims=True))
    return e / jnp.sum(e, axis=1, keepdims=True)


def _xattn_fwd(qc, kv, tq, name):
    s, d = qc.shape
    m = kv.shape[0]
    dh = d // X_HEADS
    scale = dh ** -0.5

    def body(q_ref, k_ref, v_ref, o_ref):
        p = _xattn_probs(q_ref[...], k_ref[...], scale)
        o_ref[...] = _dot(p.astype(BF16), v_ref[...], 1, 0).astype(o_ref.dtype)

    blk = pl.BlockSpec((tq, dh), lambda h, i: (i, h))
    return _pcall(
        body, name=name, grid=(X_HEADS, s // tq),
        in_specs=[blk, pl.BlockSpec((m, dh), lambda h, i: (0, h)), pl.BlockSpec((m, dh), lambda h, i: (0, X_HEADS + h))],
        out_specs=blk, out_shape=jax.ShapeDtypeStruct((s, d), BF16), compiler_params=_params("parallel", "parallel"),
    )(qc, kv, kv)


def _xattn_bwd(qc, kv, do, tq, name):
    s, d = qc.shape
    m = kv.shape[0]
    dh = d // X_HEADS
    scale = dh ** -0.5
    nq = s // tq

    def body(q_ref, k_ref, v_ref, do_ref, dq_ref, dk_ref, dv_ref, dk_acc, dv_acc):
        i = pl.program_id(1)
        q, k, v = q_ref[...], k_ref[...], v_ref[...]
        dob = do_ref[...].astype(BF16)
        p = _xattn_probs(q, k, scale)
        pb = p.astype(BF16)
        dp = _dot(dob, v, 1, 1)
        ds = pb.astype(F32) * (dp - jnp.sum(dp * pb.astype(F32), axis=1, keepdims=True))
        dsb = (ds * scale).astype(BF16)
        dq_ref[...] = _dot(dsb, k, 1, 0).astype(dq_ref.dtype)
        dk_part = _dot(dsb, q, 0, 0)
        dv_part = _dot(pb, dob, 0, 0)

        @pl.when(i == 0)
        def _():
            dk_acc[...] = dk_part
            dv_acc[...] = dv_part

        @pl.when(i > 0)
        def _():
            dk_acc[...] += dk_part
            dv_acc[...] += dv_part

        @pl.when(i == nq - 1)
        def _():
            dk_ref[...] = dk_acc[...].astype(dk_ref.dtype)
            dv_ref[...] = dv_acc[...].astype(dv_ref.dtype)

    blk = pl.BlockSpec((tq, dh), lambda h, i: (i, h))
    kblk = pl.BlockSpec((m, dh), lambda h, i: (0, h))
    return _pcall(
        body, name=name, grid=(X_HEADS, nq),
        in_specs=[blk, kblk, pl.BlockSpec((m, dh), lambda h, i: (0, X_HEADS + h)), blk],
        out_specs=[blk, kblk, kblk],
        out_shape=[jax.ShapeDtypeStruct((s, d), BF16), jax.ShapeDtypeStruct((m, d), BF16), jax.ShapeDtypeStruct((m, d), BF16)],
        scratch_shapes=[pltpu.VMEM((m, dh), F32), pltpu.VMEM((m, dh), F32)],
        compiler_params=_params("parallel", "arbitrary"),
    )(qc, kv, kv, do)


def _down_loss(act, w_down, h, tgt, g, name):
    s, f = act.shape
    d = w_down.shape[1]
    tm = _pick(s, (512, 256, 128))

    def body(a_ref, w_ref, h_ref, t_ref, g_ref, dh_ref, dhb_ref, dg_ref, loss_ref):
        xh, r = _xhat(h_ref[...] + 0.5 * _dot(a_ref[...], w_ref[...], 1, 0))
        gv = g_ref[...]
        err = xh * gv - t_ref[...]
        dy = err * (1.0 / d)
        dxh = dy * gv
        dx = r * (dxh - xh * jnp.mean(dxh * xh, axis=-1, keepdims=True))
        dh_ref[...] = dx
        dhb_ref[...] = (0.5 * dx).astype(dhb_ref.dtype)
        dg = jnp.sum(dy * xh, axis=0, keepdims=True)
        loss = jnp.broadcast_to(jnp.sum(0.5 * jnp.mean(err * err, axis=-1, keepdims=True), axis=0, keepdims=True), (1, LANES))

        @pl.when(pl.program_id(0) == 0)
        def _():
            dg_ref[...] = dg
            loss_ref[...] = loss

        @pl.when(pl.program_id(0) > 0)
        def _():
            dg_ref[...] += dg
            loss_ref[...] += loss

    row = pl.BlockSpec((tm, d), lambda i: (i, 0))
    once = lambda shape: pl.BlockSpec(shape, lambda i: (0, 0))
    return _pcall(
        body, name=name, grid=(s // tm,),
        in_specs=[pl.BlockSpec((tm, f), lambda i: (i, 0)), once((f, d)), row, row, once((1, d))],
        out_specs=[row, row, once((1, d)), once((1, LANES))],
        out_shape=[jax.ShapeDtypeStruct((s, d), F32), jax.ShapeDtypeStruct((s, d), BF16), jax.ShapeDtypeStruct((1, d), F32),
                   jax.ShapeDtypeStruct((1, LANES), F32)],
        compiler_params=_params("arbitrary"),
    )(act, w_down, h, tgt, g)


def _mix_merge(y_conv, y_sb, w_conv_out, w_attn_out, proj, gate_blocks, b_conv, b_sb, name):
    s, d = y_conv.shape
    tm, tn = _pick(s, (1024, 512, 256, 128)), _pick(d, (512, 256, 128))
    nb = d // tn

    def body(yc_ref, ys_ref, wc_ref, ws_ref, gc_ref, gs_ref, bc_ref, bs_ref, ac_ref, as_ref, m_ref):
        ac = _dot(yc_ref[...].astype(BF16), wc_ref[...], 1, 0)
        asb = _dot(ys_ref[...].astype(BF16), ws_ref[...], 1, 0)
        gc = _sigmoid(gc_ref[...].astype(F32) + bc_ref[...])
        gs = _sigmoid(gs_ref[...].astype(F32) + bs_ref[...])
        ac_ref[...] = ac.astype(ac_ref.dtype)
        as_ref[...] = asb.astype(as_ref.dtype)
        m_ref[...] = (gc * ac + gs * asb).astype(m_ref.dtype)

    rows = pl.BlockSpec((tm, d), lambda i, j: (i, 0))
    wcol = pl.BlockSpec((d, tn), lambda i, j: (0, j))
    bias = pl.BlockSpec((1, tn), lambda i, j: (0, j))
    gate = lambda blk: pl.BlockSpec((tm, tn), lambda i, j: (i, blk * nb + j))
    out = pl.BlockSpec((tm, tn), lambda i, j: (i, j))
    act = jax.ShapeDtypeStruct((s, d), BF16)
    return _pcall(
        body, name=name, grid=(s // tm, nb),
        in_specs=[rows, rows, wcol, wcol, gate(gate_blocks[0]), gate(gate_blocks[1]), bias, bias],
        out_specs=[out, out, out], out_shape=[act, act, act], compiler_params=_params("parallel", "parallel"),
    )(y_conv, y_sb, w_conv_out, w_attn_out, proj, proj, b_conv, b_sb)


def _mix_dmerge(dh, w_o, a_conv, a_sb, proj, gate_blocks, b_conv, b_sb, name):
    s, d = a_conv.shape
    tm, tn = _pick(s, (1024, 512, 256, 128)), _pick(d, (512, 256, 128))
    nb = d // tn

    def body(dh_ref, w_ref, ac_ref, as_ref, gc_ref, gs_ref, bc_ref, bs_ref, dac_ref, das_ref, dgc_ref, dgs_ref, dbc_ref, dbs_ref):
        dm = _dot(dh_ref[...], w_ref[...], 1, 1)
        gc = _sigmoid(gc_ref[...].astype(F32) + bc_ref[...])
        gs = _sigmoid(gs_ref[...].astype(F32) + bs_ref[...])
        dgc = dm * ac_ref[...].astype(F32) * gc * (1.0 - gc)
        dgs = dm * as_ref[...].astype(F32) * gs * (1.0 - gs)
        dac_ref[...] = (dm * gc).astype(dac_ref.dtype)
        das_ref[...] = (dm * gs).astype(das_ref.dtype)
        dgc_ref[...] = dgc.astype(dgc_ref.dtype)
        dgs_ref[...] = dgs.astype(dgs_ref.dtype)
        sums = jnp.sum(dgc, axis=0, keepdims=True), jnp.sum(dgs, axis=0, keepdims=True)

        @pl.when(pl.program_id(1) == 0)
        def _():
            dbc_ref[...], dbs_ref[...] = sums

        @pl.when(pl.program_id(1) > 0)
        def _():
            dbc_ref[...] += sums[0]
            dbs_ref[...] += sums[1]

    tile = pl.BlockSpec((tm, tn), lambda j, i: (i, j))
    bias = pl.BlockSpec((1, tn), lambda j, i: (0, j))
    gate = lambda blk: pl.BlockSpec((tm, tn), lambda j, i: (i, blk * nb + j))
    act = jax.ShapeDtypeStruct((s, d), BF16)
    vec = jax.ShapeDtypeStruct((1, d), F32)
    return _pcall(
        body, name=name, grid=(nb, s // tm),
        in_specs=[pl.BlockSpec((tm, d), lambda j, i: (i, 0)), pl.BlockSpec((tn, d), lambda j, i: (j, 0)), tile, tile,
                  gate(gate_blocks[0]), gate(gate_blocks[1]), bias, bias],
        out_specs=[tile, tile, tile, tile, bias, bias], out_shape=[act, act, act, act, vec, vec],
        compiler_params=_params("parallel", "arbitrary"),
    )(dh, w_o, a_conv, a_sb, proj, proj, b_conv, b_sb)


def _local_step(x, mem, tgt, w, fetch=None, prefetch=None, emit=None, tick=None, after=None):
    fetch = fetch or (lambda name, after: {})
    prefetch = prefetch or (lambda name, after: None)
    emit = emit or (lambda group, g: None)
    tick = tick or (lambda group, after: None)
    w = dict(w)
    s, d = x.shape
    heads = d // SB_HEAD_DIM
    tm = _pick(s, (1024, 512, 256, 128))
    tq = _pick(s, (1024, 512, 256, 128))
    sb_tq, sb_tk = _pick(s, (512, 256, 128)), _pick(s, (256, 128))
    tc = _pick(d, (256, 128))
    g = {}

    def wt(name, after):
        if name not in w:
            w.update(fetch(name, after))
        return w[name]

    def ffn_fwd(h, n, wgu, wdown, tag, next_g=None):
        gu, act = _ffn_up(n, wt(wgu, n), tag + "_gu")
        prefetch(wdown, gu)
        return gu, act, _mm(act, wt(wdown, act), name=tag + "_down", out_dtype=F32, res=h, alpha=0.5, norm_g=next_g)

    def ffn_bwd(dh, dhb, h, saved, gname, wgu, wdown, tag, copy_scale=None, after=None):
        n, gu, act = saved
        g[wdown] = _mm(act, dhb, ta=True, name=tag + "_dwdown", after=after)
        dgu = _ffn_dgu(dhb, w[wdown], gu, tag + "_dgu", after=emit(tag + "_down", g))
        g[wgu] = _mm(n, dgu, ta=True, b_halves=True, name=tag + "_dwgu", after=tick(tag + "_down", dgu))
        *dh_in, g[gname] = _dgrad_norm(dgu, w[wgu], dh, h, w[gname], tag + "_dn", dy_halves=True, copy_scale=copy_scale,
                                       after=emit(tag, g))
        return dh_in, tick(tag, dh_in[0])

    n1 = _rms_fwd(x, w["g_ffn1"], "ffn1_norm", tm, after=after)
    gu1, act1, (h1, u) = ffn_fwd(x, n1, "w_ffn1_gu", "w_ffn1_down", "ffn1", w["g_mix"])
    prefetch("w_in", h1)
    proj = _mm(u, wt("w_in", u), name="mix_in")
    prefetch("w_conv_out", proj)
    nd = d // SB_HEAD_DIM
    y_conv = _conv_fwd(proj, w["conv_w"], d, tc, "conv_fwd")
    sb_cols = (3 * nd, 4 * nd, 5 * nd)
    y_sb, sb_a, sb_beta = _sb_fwd(proj, heads, sb_cols, _pick(s, (2 * sb_tq, sb_tq)), sb_tk, "sb_fwd")
    prefetch("w_cq", y_sb)
    b_conv, b_sb = w["b_gate"][:, :d], w["b_gate"][:, d:]
    a_conv, a_sb, merged = _mix_merge(y_conv, y_sb, wt("w_conv_out", y_conv), wt("w_attn_out", y_sb), proj, (6, 7), b_conv, b_sb,
                                      "mix_merge")
    prefetch("w_ffn2_gu", merged)
    h2, hn = _mm(merged, wt("w_o", merged), name="mix_out", out_dtype=F32, res=h1, norm_g=w["g_cross"])
    mn = _rms_fwd(mem, w["g_mem"], "mem_norm", _pick(mem.shape[0], (256, 128)))
    qc = _mm(hn, wt("w_cq", hn), name="cross_q")
    kv = _mm(mn, wt("w_ckv", mn), name="cross_kv")
    oc = _xattn_fwd(qc, kv, tq, "xattn_fwd")
    h3, n2 = _mm(oc, wt("w_co", oc), name="cross_out", out_dtype=F32, res=h2, norm_g=w["g_ffn2"])
    gu2, act2 = _ffn_up(n2, wt("w_ffn2_gu", n2), "ffn2_gu")

    dh4, dh4b, g["g_final"], loss_lanes = _down_loss(act2, wt("w_ffn2_down", act2), h3, tgt, w["g_final"], "ffn2_down_loss")

    (dh3, dh3b), tok = ffn_bwd(dh4, dh4b, h3, (n2, gu2, act2), "g_ffn2", "w_ffn2_gu", "w_ffn2_down", "ffn2", copy_scale=1.0)
    g["w_co"] = _mm(oc, dh3b, ta=True, name="cross_dwco", after=tok)
    doc = _mm(dh3b, w["w_co"], tb=True, name="cross_doc")
    dqc, dk, dv = _xattn_bwd(qc, kv, doc, tq, "xattn_bwd")
    dkv = jnp.concatenate([dk, dv], axis=1)
    g["w_cq"] = _mm(hn, dqc, ta=True, name="cross_dwcq")
    g["w_ckv"] = _mm(mn, dkv, ta=True, name="cross_dwckv")
    dmn = _mm(dkv, w["w_ckv"], tb=True, name="cross_dmn", out_dtype=F32)
    g["g_mem"] = _rowcall(lambda dy, xb: dy * _xhat(xb)[0], [_whole(dmn), _whole(mem)], [], [], [d],
                          tm=_pick(mem.shape[0], (256, 128)), name="mem_dnorm")[0]
    dh2, dh2b, g["g_cross"] = _dgrad_norm(dqc, w["w_cq"], dh3, h2, w["g_cross"], "cross_dhn", copy_scale=1.0, after=emit("cross", g))

    g["w_o"] = _mm(merged, dh2b, ta=True, name="mix_dwo", after=tick("cross", dh2))
    da_conv, da_sb, dgc, dgs, db_conv, db_sb = _mix_dmerge(dh2b, w["w_o"], a_conv, a_sb, proj, (6, 7), b_conv, b_sb, "mix_dmerge")
    g["b_gate"] = jnp.concatenate([db_conv, db_sb], axis=1)
    g["w_conv_out"] = _mm(y_conv, da_conv, ta=True, name="conv_dwout")
    g["w_attn_out"] = _mm(y_sb, da_sb, ta=True, name="attn_dwout")
    dy_conv = _mm(da_conv, w["w_conv_out"], tb=True, name="conv_dy")
    dy_sb = _mm(da_sb, w["w_attn_out"], tb=True, name="attn_dy")
    dcb, dcc, dcx, g["conv_w"] = _conv_bwd(dy_conv, proj, w["conv_w"], d, tc, "conv_bwd")
    dq, dk_sb, dv_sb = _sb_bwd(proj, y_sb, sb_a, sb_beta, dy_sb, heads, sb_cols, sb_tq, sb_tk, "sb_bwd")
    dproj = jnp.concatenate([dcb, dcc, dcx, dq, dk_sb, dv_sb, dgc, dgs], axis=1)
    g["w_in"] = _mm(u, dproj, ta=True, name="mix_dwin")
    dh1, dh1b, g["g_mix"] = _dgrad_norm(dproj, w["w_in"], dh2, h1, w["g_mix"], "mix_du", copy_scale=0.5, after=emit("mix", g))
    (dx,), tok = ffn_bwd(dh1, dh1b, x, (n1, gu1, act1), "g_ffn1", "w_ffn1_gu", "w_ffn1_down", "ffn1", after=tick("mix", dh1))
    return loss_lanes, dx, g, tok


MATS = (("w_ffn1_gu", "col"), ("w_ffn1_down", "row"), ("w_in", "col"), ("w_conv_out", "row"), ("w_attn_out", "row"),
        ("w_o", "row"), ("w_cq", "row"), ("w_ckv", "col"), ("w_co", "row"), ("w_ffn2_gu", "col"), ("w_ffn2_down", "row"))
VECS = ("g_ffn1", "g_mix", "g_cross", "g_mem", "g_ffn2", "g_final")
WEIGHTS = ("g_ffn1", "w_ffn1_gu", "w_ffn1_down", "g_mix", "w_in", "b_gate", "conv_w", "w_conv_out", "w_attn_out", "w_o",
           "g_cross", "g_mem", "w_cq", "w_ckv", "w_co", "g_ffn2", "w_ffn2_gu", "w_ffn2_down", "g_final")
CONV_ROWS = 16


def _full_shape(kind, r, c):
    return (r, N_CHIPS * c) if kind == "col" else (N_CHIPS * r, c)


def _piece(ref, kind, r, c, chip, half):
    hr = r // 2
    if kind == "col":
        return ref.at[pl.ds(pl.multiple_of(half * hr, math.gcd(hr, 16)), hr), pl.ds(pl.multiple_of(chip * c, LANES), c)]
    return ref.at[pl.ds(pl.multiple_of(chip * r + half * hr, math.gcd(hr, 16)), hr), :]


def _shard_of(ref, kind, r, c, chip):
    if kind == "col":
        return ref.at[:, pl.ds(pl.multiple_of(chip * c, LANES), c)]
    return ref.at[pl.ds(pl.multiple_of(chip * r, 16), r), :]


def _place():
    x, y, c = lax.axis_index("x"), lax.axis_index("y"), lax.axis_index("c")
    others = [(1 - x, y), (x, 1 - y), (1 - x, 1 - y)]
    return x, y, c, 2 * x + y, others


def _remote(src, dst, send_sem, recv_sem, to):
    return pltpu.make_async_remote_copy(src_ref=src, dst_ref=dst, send_sem=send_sem, recv_sem=recv_sem,
                                        device_id=to, device_id_type=MESH)


HBM = pl.BlockSpec(memory_space=pltpu.HBM)
SEM = pl.BlockSpec(memory_space=pltpu.SEMAPHORE)
EFFECT = pltpu.SideEffectType.DATAFLOW_SIDE_EFFECTING
TOKEN = (8, LANES)


def _split_start(name, plan, n_copies, srcs, lands, after=None):
    ns, nl = len(srcs), len(lands)
    n_in = ns + nl + (after is not None)

    def body(*refs):
        outs = refs[n_in:]
        sends, _ = plan(refs[:ns], refs[ns:ns + nl], outs[0], outs[1])
        for cp in sends:
            cp.start()
        outs[-1][...] = jnp.zeros(TOKEN, F32)

    held = [pltpu.HBM(a.shape, a.dtype) for a in (*srcs, *lands)]
    dma = pltpu.SemaphoreType.DMA((n_copies,))
    ins = [pltpu.with_memory_space_constraint(a, pltpu.HBM) for a in (*srcs, *lands)]
    outs = _pcall(
        body, name=name, in_specs=[HBM] * (ns + nl) + ([] if after is None else [ANY]),
        out_specs=(SEM, SEM, *[HBM] * (ns + nl), pl.BlockSpec(memory_space=pltpu.VMEM)),
        out_shape=(dma, dma, *held, jax.ShapeDtypeStruct(TOKEN, F32)),
        input_output_aliases={i: 2 + i for i in range(ns + nl)},
        compiler_params=pltpu.CompilerParams(has_side_effects=EFFECT),
    )(*ins, *([] if after is None else [after]))
    return outs[0], outs[1], list(outs[2:2 + ns]), list(outs[2 + ns:2 + ns + nl]), outs[-1]


def _split_wait(name, plan, send_sems, recv_sems, srcs, lands, after):
    ns, nl = len(srcs), len(lands)

    def body(*refs):
        sends, recvs = plan(refs[:ns], refs[ns:ns + nl], refs[ns + nl], refs[ns + nl + 1])
        for cp in sends:
            cp.wait_send()
        for cp in recvs:
            cp.wait_recv()

    outs = _pcall(
        body, name=name, in_specs=[HBM] * (ns + nl) + [SEM, SEM, ANY], out_specs=[HBM] * (ns + nl),
        out_shape=[pltpu.HBM(a.shape, a.dtype) for a in (*srcs, *lands)],
        input_output_aliases={i: i for i in range(ns + nl)},
        compiler_params=pltpu.CompilerParams(has_side_effects=EFFECT),
    )(*srcs, *lands, send_sems, recv_sems, after)
    return list(outs[:ns]), list(outs[ns:])


def _gather_plan(dims):
    def plan(shard_refs, full_refs, ss, rs):
        x, y, c, me, others = _place()
        sends, recvs = [], []
        for wi, (kind, r, cw) in enumerate(dims):
            half = shard_refs[wi].at[pl.ds(pl.multiple_of(c * (r // 2), math.gcd(r // 2, 16)), r // 2), :]
            for k, (ox, oy) in enumerate(others):
                sem = 4 * wi + k
                sends.append(_remote(half, _piece(full_refs[wi], kind, r, cw, me, c), ss.at[sem], rs.at[sem], (ox, oy, c)))
                recvs.append(_remote(half, _piece(full_refs[wi], kind, r, cw, 2 * ox + oy, c), ss.at[sem], rs.at[sem], (x, y, c)))
            sem = 4 * wi + 3
            own = _remote(shard_refs[wi], _shard_of(full_refs[wi], kind, r, cw, me), ss.at[sem], rs.at[sem], (x, y, 1 - c))
            sends.append(own)
            recvs.append(own)
        return sends, recvs

    return plan


def _forward_plan(dims):
    def plan(_, full_refs, ss, rs):
        x, y, c, _, others = _place()
        sends, recvs = [], []
        for wi, (kind, r, cw) in enumerate(dims):
            for k, (ox, oy) in enumerate(others):
                sem = 3 * wi + k
                mine = _piece(full_refs[wi], kind, r, cw, 2 * ox + oy, c)
                theirs = _piece(full_refs[wi], kind, r, cw, 2 * ox + oy, 1 - c)
                sends.append(_remote(mine, mine, ss.at[sem], rs.at[sem], (x, y, 1 - c)))
                recvs.append(_remote(theirs, theirs, ss.at[sem], rs.at[sem], (x, y, 1 - c)))
        return sends, recvs

    return plan


def _rs_cores_plan(dims):
    def plan(g_refs, land_refs, ss, rs):
        x, y, c, _, _ = _place()
        sends, recvs = [], []
        for wi, dm in enumerate(dims):
            for chip in range(N_CHIPS):
                sem = N_CHIPS * wi + chip
                sends.append(_remote(_piece(g_refs[wi], *dm, chip, 1 - c), land_refs[wi].at[chip], ss.at[sem], rs.at[sem], (x, y, 1 - c)))
                recvs.append(_remote(_piece(g_refs[wi], *dm, chip, c), land_refs[wi].at[chip], ss.at[sem], rs.at[sem], (x, y, 1 - c)))
        return sends, recvs

    return plan


def _share_plan(nw):
    def plan(_, buf_refs, ss, rs):
        x, y, c, _, _ = _place()
        sends = [_remote(buf_refs[wi].at[c], buf_refs[wi].at[c], ss.at[wi], rs.at[wi], (x, y, 1 - c)) for wi in range(nw)]
        recvs = [_remote(buf_refs[wi].at[1 - c], buf_refs[wi].at[1 - c], ss.at[wi], rs.at[wi], (x, y, 1 - c)) for wi in range(nw)]
        return sends, recvs

    return plan


def _small_plan():
    def plan(_, buf_refs, ss, rs):
        x, y, c = lax.axis_index("x"), lax.axis_index("y"), lax.axis_index("c")
        buf = buf_refs[0]
        sends, recvs = [], []
        for rel in range(1, N_DEV):
            peer = (x ^ (rel >> 2 & 1), y ^ (rel >> 1 & 1), c ^ (rel & 1))
            sends.append(_remote(buf.at[0], buf.at[rel], ss.at[rel - 1], rs.at[rel - 1], peer))
            recvs.append(_remote(buf.at[0], buf.at[rel], ss.at[rel - 1], rs.at[rel - 1], peer))
        return sends, recvs

    return plan


def _sum_small(buf, me, name):
    _, rows, n = buf.shape

    def body(me_ref, b_ref, o_ref):
        tot = b_ref[me_ref[0]]
        for dev in range(1, N_DEV):
            tot = tot + b_ref[dev ^ me_ref[0]]
        o_ref[...] = tot

    return _pcall(
        body, name=name, out_shape=jax.ShapeDtypeStruct((rows, n), F32),
        grid_spec=pltpu.PrefetchScalarGridSpec(
            num_scalar_prefetch=1, grid=(1,), in_specs=[pl.BlockSpec((N_DEV, rows, n), lambda i, m: (0, 0, 0))],
            out_specs=pl.BlockSpec((rows, n), lambda i, m: (0, 0))),
    )(me, buf)


def _rs_chips_plan(nw):
    def plan(p_refs, land_refs, ss, rs):
        x, y, c, me, others = _place()
        sends, recvs = [], []
        for wi in range(nw):
            for k, (ox, oy) in enumerate(others):
                sem = 3 * wi + k
                sends.append(_remote(p_refs[wi].at[2 * ox + oy], land_refs[wi].at[k], ss.at[sem], rs.at[sem], (ox, oy, c)))
                recvs.append(_remote(p_refs[wi].at[me], land_refs[wi].at[k], ss.at[sem], rs.at[sem], (x, y, c)))
        return sends, recvs

    return plan


SUM_BLOCK_BYTES = 4 << 20


def _rows_per_block(n, c, limit_bytes=2 << 20):
    best = None
    for tm in range(16, n + 1, 16):
        if n % tm == 0 and tm * c * 4 <= limit_bytes:
            best = tm
    return best or n


def _sum_cores(grad, got, kind, place, name):
    _, hr, cw = got.shape
    tm = _rows_per_block(hr, cw, SUM_BLOCK_BYTES)
    nb = hr // tm

    def body(place_ref, g_ref, t_ref, o_ref):
        o_ref[...] = (g_ref[...].astype(F32) + t_ref[...].astype(F32)).astype(o_ref.dtype)

    if kind == "col":
        g_spec = pl.BlockSpec((tm, cw), lambda j, i, pr: (pr[0] * nb + i, j))
    else:
        g_spec = pl.BlockSpec((tm, cw), lambda j, i, pr: ((2 * j + pr[0]) * nb + i, 0))
    blk = pl.BlockSpec((None, tm, cw), lambda j, i, pr: (j, i, 0))
    return _pcall(
        body, name=name, out_shape=jax.ShapeDtypeStruct(got.shape, BF16),
        grid_spec=pltpu.PrefetchScalarGridSpec(num_scalar_prefetch=1, grid=(N_CHIPS, nb), in_specs=[g_spec, blk], out_specs=blk),
        compiler_params=_params("parallel", "parallel"),
    )(place, grad, got)


def _sum_chips(parts, got, place, name):
    _, n, cw = got.shape
    tm = _rows_per_block(n, cw, SUM_BLOCK_BYTES)

    def body(place_ref, p_ref, g_ref, o_ref):
        tot = p_ref[...].astype(F32)
        for k in range(3):
            tot = tot + g_ref[k].astype(F32)
        o_ref[...] = tot

    return _pcall(
        body, name=name, out_shape=jax.ShapeDtypeStruct((2, n, cw), F32),
        grid_spec=pltpu.PrefetchScalarGridSpec(
            num_scalar_prefetch=1, grid=(n // tm,),
            in_specs=[pl.BlockSpec((None, tm, cw), lambda i, pr: (pr[1], i, 0)), pl.BlockSpec((3, tm, cw), lambda i, pr: (0, i, 0))],
            out_specs=pl.BlockSpec((None, tm, cw), lambda i, pr: (pr[0], i, 0))),
        compiler_params=_params("parallel"),
    )(place, parts, got)


def _adamw(g, w, m, v, name):
    n, c = g.shape
    c1 = 1.0 - ADAM_B1 ** ADAM_STEP
    c2 = 1.0 - ADAM_B2 ** ADAM_STEP

    def fn(gb, wb, mb, vb):
        m_new = ADAM_B1 * mb + (1.0 - ADAM_B1) * gb
        v_new = ADAM_B2 * vb + (1.0 - ADAM_B2) * (gb * gb)
        delta = -ADAM_LR * ((m_new / c1) / (jnp.sqrt(v_new / c2) + ADAM_EPS) + ADAM_WD * wb)
        return gb, delta, m_new, v_new

    tm = _rows_per_block(n, c) if n % 16 == 0 else n
    return _rowcall(fn, [_whole(g), _whole(w), _whole(m), _whole(v)], [], [(c, F32)] * 4, tm=tm, name=name)


PACK_ROWS = 16


def _pack_rows(parts, width, name, after=None):
    assert sum(p.shape[0] for p in parts) <= PACK_ROWS

    def body(*refs):
        out_ref = refs[-1]
        out_ref[...] = jnp.zeros_like(out_ref)
        at = 0
        for r in refs[:len(parts)]:
            k, n = r.shape
            if n == width:
                out_ref[at:at + k, :] = r[...]
            else:
                out_ref[at:at + k, :] = jnp.broadcast_to(r[:, :1], (k, width))
            at += k

    vm = pl.BlockSpec(memory_space=pltpu.VMEM)
    return _pcall(body, name=name, in_specs=[vm] * len(parts) + ([] if after is None else [ANY]), out_specs=vm,
                  out_shape=jax.ShapeDtypeStruct((PACK_ROWS, width), F32))(*parts, *([] if after is None else [after]))


def _cast_shard(wm, name, after):
    n, c = wm.shape
    return _rowcall(lambda v: v, [_whole(wm)], [], [(c, BF16)], tm=_rows_per_block(n, c), name=name, after=after)[0]


GATHER_GROUPS = (
    ("w_ffn1_gu", "conv_w"), ("w_ffn1_down",), ("w_in",), ("w_conv_out", "w_attn_out", "w_o"), ("w_cq", "w_ckv", "w_co"),
    ("w_ffn2_gu", "w_ffn2_down"),
)
REDUCE_GROUPS = {
    "ffn2": ("w_ffn2_down", "w_ffn2_gu"),
    "cross": ("w_co", "w_cq", "w_ckv"),
    "mix": ("w_o", "w_conv_out", "w_attn_out", "w_in"),
    "ffn1_down": ("w_ffn1_down",),
    "ffn1": ("w_ffn1_gu",),
}
TAIL_STAGES = (("ffn2", "cross"), ("mix",), ("ffn1_down", "ffn1"))
KIND = dict(MATS)


def _step(x, mem, tgt, wts, m_in, v_in):
    d = x.shape[-1]
    cc = wts["conv_w"].shape[1]
    place = jnp.stack([lax.axis_index("c"), 2 * lax.axis_index("x") + lax.axis_index("y")]).astype(jnp.int32)
    dims = {n: (kind, *wts[n].shape) for n, kind in MATS}
    dims["conv_w"] = ("col", CONV_ROWS, cc)

    w = {n: wts[n].reshape(1, -1) for n in VECS + ("b_gate",)}
    flying, token = {}, None
    for names in GATHER_GROUPS:
        gd = [dims[n] for n in names]
        shards = [jnp.pad(wts[n], ((0, CONV_ROWS - CONV_K), (0, 0))) if n == "conv_w" else _cast_shard(wts[n], "cast_" + n, token)
                  for n in names]
        lands = [lax.empty(_full_shape(*dm), sh.dtype) for dm, sh in zip(gd, shards)]
        plan = _gather_plan(gd)
        ss, rs, srcs, lands, token = _split_start("gather_start_" + names[0], plan, 4 * len(names), shards, lands, token)
        flying.update({n: (names, plan, ss, rs, srcs, lands, gd) for n in names})

    passing = {}

    def prefetch(name, after):
        if name not in passing:
            names, plan, ss, rs, srcs, lands, gd = flying[name]
            _, lands = _split_wait("gather_wait_" + names[0], plan, ss, rs, srcs, lands, after)
            plan = _forward_plan(gd)
            ss, rs, _, lands, _ = _split_start("forward_start_" + names[0], plan, 3 * len(names), [], lands)
            passing.update({n: (names, plan, ss, rs, lands) for n in names})

    def fetch(name, after):
        prefetch(name, after)
        names, plan, ss, rs, lands = passing[name]
        _, lands = _split_wait("forward_wait_" + names[0], plan, ss, rs, [], lands, after)
        return {n: (land[:CONV_K] if n == "conv_w" else land) for n, land in zip(names, lands)}

    swapping, sent = {}, {}

    def emit(tag, g):
        if tag not in REDUCE_GROUPS:
            return None
        names = REDUCE_GROUPS[tag]
        gd = [dims[n] for n in names]
        lands = [lax.empty((N_CHIPS, r // 2, cw), BF16) for (_, r, cw) in gd]
        plan = _rs_cores_plan(gd)
        ss, rs, srcs, lands, tok = _split_start("rs_cores_start_" + tag, plan, N_CHIPS * len(names), [g[n] for n in names], lands)
        swapping[tag] = (plan, ss, rs, srcs, lands)
        return tok

    def tick(tag, after):
        if tag not in REDUCE_GROUPS:
            return None
        names = REDUCE_GROUPS[tag]
        plan, ss, rs, srcs, lands = swapping[tag]
        mine, got = _split_wait("rs_cores_wait_" + tag, plan, ss, rs, srcs, lands, after)
        parts = [_sum_cores(gm, t, KIND[n], place, "sum_cores_" + n) for n, gm, t in zip(names, mine, got)]
        lands = [lax.empty((3, *p.shape[1:]), BF16) for p in parts]
        plan = _rs_chips_plan(len(names))
        ss, rs, srcs, lands, tok = _split_start("rs_chips_start_" + tag, plan, 3 * len(names), parts, lands)
        sent[tag] = (plan, ss, rs, srcs, lands)
        return tok

    loss_lanes, dx, g, last = _local_step(x[0], mem[0], tgt[0], w, fetch, prefetch, emit, tick, token)

    rows = [g[n] for n in VECS] + [g["b_gate"][:, :d], g["b_gate"][:, d:], g["conv_w"], loss_lanes]
    packed = _pack_rows(rows, d, "pack_small", after=last)
    small = jnp.concatenate([packed[None], jnp.zeros((N_DEV - 1, *packed.shape), F32)], axis=0)
    small_plan = _small_plan()
    small_ss, small_rs, _, small, after = _split_start("small_start", small_plan, N_DEV - 1, [], [small])

    grads, out = {}, {}

    def update(n):
        shape = wts[n].shape
        as2d = (lambda a: a.reshape(1, -1)) if len(shape) == 1 else (lambda a: a)
        return [r.reshape(shape) for r in _adamw(grads[n], as2d(wts[n]), as2d(m_in[n]), as2d(v_in[n]), "adamw_" + n)]

    def finish(sharing, after):
        tag, names, plan, ss, rs, halves = sharing
        _, both = _split_wait("share_wait_" + tag, plan, ss, rs, [], halves, after)
        for n, b in zip(names, both):
            grads[n] = b.reshape(-1, b.shape[-1])
            out[n] = update(n)
        return out[names[-1]][1]

    sharing = None
    for stage in TAIL_STAGES:
        names, halves = [], []
        for tag in stage:
            plan, ss, rs, srcs, lands = sent[tag]
            parts, landed = _split_wait("rs_chips_wait_" + tag, plan, ss, rs, srcs, lands, after)
            halves += [_sum_chips(p, t, place, "sum_chips_" + n) for n, p, t in zip(REDUCE_GROUPS[tag], parts, landed)]
            names += REDUCE_GROUPS[tag]
        plan = _share_plan(len(names))
        ss, rs, _, halves, after = _split_start("share_start_" + stage[0], plan, len(names), [], halves)
        if sharing is not None:
            after = finish(sharing, after)
        sharing = (stage[0], names, plan, ss, rs, halves)
    after = finish(sharing, after)

    _, small = _split_wait("small_wait", small_plan, small_ss, small_rs, [], small, after)
    me = (4 * lax.axis_index("x") + 2 * lax.axis_index("y") + lax.axis_index("c")).astype(jnp.int32).reshape(1)
    red = _sum_small(small[0], me, "sum_small")
    grads.update({n: red[i:i + 1] for i, n in enumerate(VECS)})
    nv = len(VECS)
    grads["b_gate"] = jnp.concatenate([red[nv:nv + 1], red[nv + 1:nv + 2]], axis=1)
    chip = 2 * lax.axis_index("x") + lax.axis_index("y")
    grads["conv_w"] = lax.dynamic_slice_in_dim(red[nv + 2:nv + 2 + CONV_K], chip * cc, cc, axis=1)
    loss = red[nv + 2 + CONV_K, 0]
    out.update({n: update(n) for n in WEIGHTS if n not in KIND})
    return (loss, dx[None], *[out[n][0] for n in WEIGHTS], *[out[n][1] for n in WEIGHTS],
            *[out[n][2] for n in WEIGHTS], *[out[n][3] for n in WEIGHTS])


def kernel(x, mem, g_ffn1, w_ffn1_gu, w_ffn1_down, g_mix, w_in, b_gate, conv_w, w_conv_out, w_attn_out, w_o, g_cross, g_mem, w_cq, w_ckv, w_co, g_ffn2, w_ffn2_gu, w_ffn2_down, g_final, loss_target, m_g_ffn1, m_w_ffn1_gu, m_w_ffn1_down, m_g_mix, m_w_in, m_b_gate, m_conv_w, m_w_conv_out, m_w_attn_out, m_w_o, m_g_cross, m_g_mem, m_w_cq, m_w_ckv, m_w_co, m_g_ffn2, m_w_ffn2_gu, m_w_ffn2_down, m_g_final, v_g_ffn1, v_w_ffn1_gu, v_w_ffn1_down, v_g_mix, v_w_in, v_b_gate, v_conv_w, v_w_conv_out, v_w_attn_out, v_w_o, v_g_cross, v_g_mem, v_w_cq, v_w_ckv, v_w_co, v_g_ffn2, v_w_ffn2_gu, v_w_ffn2_down, v_g_final):
    given = dict(locals())
    wts = {n: given[n] for n in WEIGHTS}
    m_in = {n: given["m_" + n] for n in WEIGHTS}
    v_in = {n: given["v_" + n] for n in WEIGHTS}
    return _step(x, mem, loss_target, wts, m_in, v_in)
```

```python
import math

import jax
import jax.numpy as jnp
from jax import lax
from jax.experimental import pallas as pl
from jax.experimental.pallas import tpu as pltpu

F32 = jnp.float32
BF16 = jnp.bfloat16
MESH = pl.DeviceIdType.MESH

V7X_VMEM_LIMIT_BYTES = 48 * 1024 * 1024
MM_VMEM_BUDGET_BYTES = 36 * 1024 * 1024
MM_WHOLE_K = 2816
LANES = 128
SB_HEAD_DIM = 128
X_HEADS = 4
CONV_K = 3
RMS_EPS = 1e-6
N_CHIPS = 4
N_DEV = 8
ADAM_LR, ADAM_B1, ADAM_B2, ADAM_EPS, ADAM_WD, ADAM_STEP = 0.001, 0.9, 0.999, 1e-08, 0.01, 10


ANY = pl.BlockSpec(memory_space=pl.ANY)


def _pcall(body, **kw):
    return pl.pallas_call(body, **kw)


def _params(*sem):
    return pltpu.CompilerParams(dimension_semantics=sem, vmem_limit_bytes=V7X_VMEM_LIMIT_BYTES)


def _pick(dim, cands):
    for c in cands:
        if dim % c == 0:
            return c
    return dim


def _dot(a, b, ca, cb):
    return lax.dot_general(a, b, (((ca,), (cb,)), ((), ())), preferred_element_type=F32)


def _mm(a, b, *, name, ta=False, tb=False, out_dtype=BF16, res=None, alpha=1.0, tm=None, tn=None, tk=None, after=None,
        a_halves=False, b_halves=False, norm_g=None):
    assert not (a_halves and ta) and not (b_halves and tb) and not (norm_g is not None and ta)
    if a_halves:
        m, k = a.shape[1], 2 * a.shape[2]
    else:
        m, k = (a.shape[1], a.shape[0]) if ta else a.shape
    if b_halves:
        n = 2 * b.shape[2]
        assert k == b.shape[1]
    else:
        n = b.shape[0] if tb else b.shape[1]
        assert k == (b.shape[1] if tb else b.shape[0]), (a.shape, b.shape, ta, tb)
    if ta:
        tm = tm or _pick(m, (512, 1408, 256, 128))
        tn = tn or _pick(n, (2048, 1024, 512, 256, 128))
        tk = tk or (k if k <= MM_WHOLE_K else _pick(k, (1024, 512, 256, 128)))
    else:
        tk = tk or (k if k <= MM_WHOLE_K else _pick(k, (MM_WHOLE_K, 2048, 1024, 512, 256, 128)))
        tn = tn or (n if norm_g is not None else _pick(n, (512, 1408, 256, 128) if tk == k else (1024, 512, 256, 128)))
        out_bytes = jnp.dtype(out_dtype).itemsize + (0 if res is None else res.dtype.itemsize) + (0 if norm_g is None else 2)
        per_row = 2 * (tk * a.dtype.itemsize + tn * out_bytes)
        per_row += 4 * tn if tk < k else 0
        rows = (MM_VMEM_BUDGET_BYTES - 2 * tk * tn * b.dtype.itemsize) // per_row
        tm = tm or next((c for c in (2048, 1024, 512, 256, 128) if m % c == 0 and c <= rows), m)
    if a_halves:
        tk = min(tk, k // 2) if (k // 2) % min(tk, k // 2) == 0 else _pick(k // 2, (1408, 1024, 512, 256, 128))
    if b_halves:
        tn = tn if (n // 2) % tn == 0 else _pick(n // 2, (2816, 1408, 1024, 512, 256, 128) if ta else (1408, 1024, 512, 256, 128))
    nk = k // tk
    assert m % tm == 0 and n % tn == 0 and k % tk == 0
    a_spec = pl.BlockSpec((tk, tm), lambda i, j, kk: (kk, i)) if ta else pl.BlockSpec((tm, tk), lambda i, j, kk: (i, kk))
    b_spec = pl.BlockSpec((tn, tk), lambda i, j, kk: (j, kk)) if tb else pl.BlockSpec((tk, tn), lambda i, j, kk: (kk, j))
    if a_halves:
        per = (k // 2) // tk
        a_spec = pl.BlockSpec((None, tm, tk), lambda i, j, kk: (kk // per, i, kk % per))
    if b_halves:
        per_n = (n // 2) // tn
        b_spec = pl.BlockSpec((None, tk, tn), lambda i, j, kk: (j // per_n, kk, j % per_n))
    o_spec = pl.BlockSpec((tm, tn), lambda i, j, kk: (i, j))
    ca, cb = (0 if ta else 1), (1 if tb else 0)

    n_in = 2 + (res is not None) + (norm_g is not None) + (after is not None)
    n_out = 1 + (norm_g is not None)

    def body(*refs):
        a_ref, b_ref = refs[:2]
        res_ref = refs[2] if res is not None else None
        g_ref = refs[2 + (res is not None)] if norm_g is not None else None
        o_ref = refs[n_in]
        scratch = refs[n_in + n_out:]

        def finish(acc):
            val = acc if alpha == 1.0 else alpha * acc
            if res_ref is not None:
                val = res_ref[...].astype(F32) + val
            o_ref[...] = val.astype(o_ref.dtype)
            if g_ref is not None:
                refs[n_in + 1][...] = (_xhat(val)[0] * g_ref[...]).astype(BF16)

        part = _dot(a_ref[...].astype(BF16), b_ref[...].astype(BF16), ca, cb)
        if nk == 1:
            finish(part)
        else:
            acc_ref = scratch[0]
            kk = pl.program_id(2)

            @pl.when(kk == 0)
            def _():
                acc_ref[...] = part

            @pl.when(kk > 0)
            def _():
                acc_ref[...] += part

            @pl.when(kk == nk - 1)
            def _():
                finish(acc_ref[...])

    ins = [a, b] + ([] if res is None else [res]) + ([] if norm_g is None else [norm_g]) + ([] if after is None else [after])
    in_specs = [a_spec, b_spec] + ([] if res is None else [o_spec])
    in_specs += ([] if norm_g is None else [pl.BlockSpec((1, tn), lambda i, j, kk: (0, j))]) + ([] if after is None else [ANY])
    outs = _pcall(
        body, name=name, grid=(m // tm, n // tn, nk), in_specs=in_specs, out_specs=[o_spec] * n_out,
        out_shape=[jax.ShapeDtypeStruct((m, n), out_dtype)] + [jax.ShapeDtypeStruct((m, n), BF16)] * (n_out - 1),
        scratch_shapes=[pltpu.VMEM((tm, tn), F32)] if nk > 1 else [],
        compiler_params=_params("parallel", "parallel", "arbitrary"),
    )(*ins)
    return outs[0] if norm_g is None else outs


def _rowcall(fn, rows, consts, outs, accs=(), *, tm, name, after=None):
    s = rows[0][0].shape[0]
    assert s % tm == 0
    n_read, n_out = len(rows) + len(consts), len(outs)
    n_in = n_read + (after is not None)

    def body(*refs):
        vals = fn(*[r[...] for r in refs[:n_read]])
        vals = vals if isinstance(vals, (tuple, list)) else (vals,)
        for o_ref, v in zip(refs[n_in:n_in + n_out], vals[:n_out]):
            o_ref[...] = v.astype(o_ref.dtype)
        if accs:
            first = pl.program_id(0) == 0
            for a_ref, v in zip(refs[n_in + n_out:], vals[n_out:]):
                tot = jnp.sum(v.astype(F32), axis=0, keepdims=True)

                @pl.when(first)
                def _(a_ref=a_ref, tot=tot):
                    a_ref[...] = tot

                @pl.when(jnp.logical_not(first))
                def _(a_ref=a_ref, tot=tot):
                    a_ref[...] += tot

    in_specs = [pl.BlockSpec((tm, w), lambda i, cb=cb: (i, cb)) for (_, cb, w) in rows]
    in_specs += [pl.BlockSpec(c.shape, lambda i: (0, 0)) for c in consts]
    in_specs += [] if after is None else [ANY]
    out_specs = [pl.BlockSpec((tm, w), lambda i: (i, 0)) for (w, _) in outs]
    out_specs += [pl.BlockSpec((1, w), lambda i: (0, 0)) for w in accs]
    out_shape = [jax.ShapeDtypeStruct((s, w), dt) for (w, dt) in outs]
    out_shape += [jax.ShapeDtypeStruct((1, w), F32) for w in accs]
    return _pcall(
        body, name=name, grid=(s // tm,), in_specs=in_specs, out_specs=out_specs, out_shape=out_shape,
        compiler_params=_params("arbitrary" if accs else "parallel"),
    )(*[r[0] for r in rows], *consts, *([] if after is None else [after]))


def _whole(a):
    return (a, 0, a.shape[1])


def _xhat(x):
    x = x.astype(F32)
    r = lax.rsqrt(jnp.mean(x * x, axis=-1, keepdims=True) + RMS_EPS)
    return x * r, r


def _rms_bwd(dy, x, g):
    xh, r = _xhat(x)
    dxh = dy.astype(F32) * g
    dx = r * (dxh - xh * jnp.mean(dxh * xh, axis=-1, keepdims=True))
    return dx, dy.astype(F32) * xh


def _sigmoid(x):
    return 1.0 / (1.0 + jnp.exp(-x))


def _rms_fwd(x, g, name, tm, after=None):
    d = x.shape[1]
    return _rowcall(lambda xb, gb: _xhat(xb)[0] * gb, [_whole(x)], [g], [(d, BF16)], tm=tm, name=name, after=after)[0]


def _silu_parts(gate):
    sg = _sigmoid(gate)
    return sg, gate * sg


def _ffn_up(n, w_gu, name):
    s, d = n.shape
    f = w_gu.shape[1] // 2
    tn = _pick(f, (1408, 1024, 512, 256, 128))
    tm = _pick(s, (1024, 512, 256, 128))
    nb = f // tn

    def body(n_ref, wg_ref, wu_ref, gu_ref, act_ref):
        nv = n_ref[...]
        gate = _dot(nv, wg_ref[...], 1, 0)
        up = _dot(nv, wu_ref[...], 1, 0)
        gu_ref[0] = gate.astype(gu_ref.dtype)
        gu_ref[1] = up.astype(gu_ref.dtype)
        act_ref[...] = (_silu_parts(gate)[1] * up).astype(act_ref.dtype)

    return _pcall(
        body, name=name, grid=(s // tm, nb),
        in_specs=[pl.BlockSpec((tm, d), lambda i, j: (i, 0)), pl.BlockSpec((d, tn), lambda i, j: (0, j)),
                  pl.BlockSpec((d, tn), lambda i, j: (0, nb + j))],
        out_specs=[pl.BlockSpec((2, tm, tn), lambda i, j: (0, i, j)), pl.BlockSpec((tm, tn), lambda i, j: (i, j))],
        out_shape=[jax.ShapeDtypeStruct((2, s, f), BF16), jax.ShapeDtypeStruct((s, f), BF16)],
        compiler_params=_params("parallel", "parallel"),
    )(n, w_gu, w_gu)


def _ffn_dgu(dhb, w_down, gu, name, after=None):
    s, d = dhb.shape
    f = w_down.shape[0]
    tn = _pick(f, (1408, 1024, 512, 256, 128))
    tm = _pick(s, (1024, 512, 256, 128))

    def body(dh_ref, w_ref, gu_ref, *rest):
        o_ref = rest[-1]
        dact = _dot(dh_ref[...], w_ref[...], 1, 1)
        gate, up = gu_ref[0].astype(F32), gu_ref[1].astype(F32)
        sg, silu = _silu_parts(gate)
        o_ref[0] = (dact * up * (sg + silu * (1.0 - sg))).astype(o_ref.dtype)
        o_ref[1] = (dact * silu).astype(o_ref.dtype)

    blk = pl.BlockSpec((2, tm, tn), lambda i, j: (0, i, j))
    return _pcall(
        body, name=name, grid=(s // tm, f // tn),
        in_specs=[pl.BlockSpec((tm, d), lambda i, j: (i, 0)), pl.BlockSpec((tn, d), lambda i, j: (j, 0)), blk]
        + ([] if after is None else [ANY]),
        out_specs=blk, out_shape=jax.ShapeDtypeStruct((2, s, f), BF16), compiler_params=_params("parallel", "parallel"),
    )(dhb, w_down, gu, *([] if after is None else [after]))


def _dgrad_norm(dy, wmat, dh, x, g, name, *, dy_halves=False, copy_scale=None, after=None):
    s, d = dh.shape
    k = wmat.shape[1]
    tk = k if k <= MM_WHOLE_K else _pick(k, (MM_WHOLE_K, 2048, 1024, 512, 256, 128))
    if dy_halves and (k // 2) % tk:
        tk = _pick(k // 2, (1408, 1024, 512, 256, 128))
    tm = _pick(s, (512, 256, 128))
    nk, per = k // tk, (k // 2) // tk if dy_halves else 0
    n_in = 5 + (after is not None)
    n_out = 2 + (copy_scale is not None)

    def body(*refs):
        dy_ref, w_ref, dh_ref, x_ref, g_ref = refs[:5]
        outs, scratch = refs[n_in:n_in + n_out], refs[n_in + n_out:]
        i, kk = pl.program_id(0), pl.program_id(1)
        part = _dot(dy_ref[...], w_ref[...], 1, 1)

        def finish(dn):
            dx, dg = _rms_bwd(dn, x_ref[...], g_ref[...])
            tot = dh_ref[...] + dx
            outs[0][...] = tot
            if copy_scale is not None:
                outs[1][...] = (copy_scale * tot).astype(outs[1].dtype)
            dg = jnp.sum(dg, axis=0, keepdims=True)

            @pl.when(i == 0)
            def _():
                outs[-1][...] = dg

            @pl.when(i > 0)
            def _():
                outs[-1][...] += dg

        if nk == 1:
            finish(part)
        else:
            acc_ref = scratch[0]

            @pl.when(kk == 0)
            def _():
                acc_ref[...] = part

            @pl.when(kk > 0)
            def _():
                acc_ref[...] += part

            @pl.when(kk == nk - 1)
            def _():
                finish(acc_ref[...])

    row = pl.BlockSpec((tm, d), lambda i, kk: (i, 0))
    dy_spec = pl.BlockSpec((None, tm, tk), lambda i, kk: (kk // per, i, kk % per)) if dy_halves else pl.BlockSpec((tm, tk), lambda i, kk: (i, kk))
    in_specs = [dy_spec, pl.BlockSpec((d, tk), lambda i, kk: (0, kk)), row, row, pl.BlockSpec((1, d), lambda i, kk: (0, 0))]
    out_specs = [row] * (n_out - 1) + [pl.BlockSpec((1, d), lambda i, kk: (0, 0))]
    out_shape = [jax.ShapeDtypeStruct((s, d), F32)] + ([] if copy_scale is None else [jax.ShapeDtypeStruct((s, d), BF16)])
    return _pcall(
        body, name=name, grid=(s // tm, nk), in_specs=in_specs + ([] if after is None else [ANY]), out_specs=out_specs,
        out_shape=out_shape + [jax.ShapeDtypeStruct((1, d), F32)], scratch_shapes=[pltpu.VMEM((tm, d), F32)] if nk > 1 else [],
        compiler_params=_params("arbitrary", "arbitrary"),
    )(dy, wmat, dh, x, g, *([] if after is None else [after]))


def _shift_down(p, k):
    if k == 0:
        return p
    rows = lax.broadcasted_iota(jnp.int32, p.shape, 0)
    return jnp.where(rows >= k, pltpu.roll(p, k, 0), 0.0)


def _shift_up(p, k):
    if k == 0:
        return p
    s = p.shape[0]
    rows = lax.broadcasted_iota(jnp.int32, p.shape, 0)
    return jnp.where(rows < s - k, pltpu.roll(p, s - k, 0), 0.0)


def _conv_fwd(proj, conv_w, d, tc, name):
    s = proj.shape[0]
    nb = d // tc

    def body(cb_ref, cc_ref, cx_ref, w_ref, y_ref):
        p = cc_ref[...].astype(F32) * cx_ref[...].astype(F32)
        w = w_ref[...]
        acc = p * w[CONV_K - 1:CONV_K, :]
        for k in range(1, CONV_K):
            acc = acc + _shift_down(p, k) * w[CONV_K - 1 - k:CONV_K - k, :]
        y_ref[...] = (cb_ref[...].astype(F32) * acc).astype(y_ref.dtype)

    col = lambda off: pl.BlockSpec((s, tc), lambda j: (0, off * nb + j))
    return _pcall(
        body, name=name, grid=(nb,), in_specs=[col(0), col(1), col(2), pl.BlockSpec((CONV_K, tc), lambda j: (0, j))],
        out_specs=pl.BlockSpec((s, tc), lambda j: (0, j)), out_shape=jax.ShapeDtypeStruct((s, d), BF16),
        compiler_params=_params("parallel"),
    )(proj, proj, proj, conv_w)


def _conv_bwd(dy, proj, conv_w, d, tc, name):
    s = proj.shape[0]
    nb = d // tc

    def body(dy_ref, cb_ref, cc_ref, cx_ref, w_ref, dcb_ref, dcc_ref, dcx_ref, dw_ref):
        cc, cx = cc_ref[...].astype(F32), cx_ref[...].astype(F32)
        p = cc * cx
        w = w_ref[...]
        dyv = dy_ref[...].astype(F32)
        shifted = [_shift_down(p, CONV_K - 1 - k) for k in range(CONV_K)]
        conv = shifted[0] * w[0:1, :]
        for k in range(1, CONV_K):
            conv = conv + shifted[k] * w[k:k + 1, :]
        dcb_ref[...] = (dyv * conv).astype(dcb_ref.dtype)
        ds = dyv * cb_ref[...].astype(F32)
        dp = ds * w[CONV_K - 1:CONV_K, :]
        for k in range(1, CONV_K):
            dp = dp + _shift_up(ds, k) * w[CONV_K - 1 - k:CONV_K - k, :]
        dcc_ref[...] = (dp * cx).astype(dcc_ref.dtype)
        dcx_ref[...] = (dp * cc).astype(dcx_ref.dtype)
        for k in range(CONV_K):
            dw_ref[k:k + 1, :] = jnp.sum(ds * shifted[k], axis=0, keepdims=True)

    col = lambda off: pl.BlockSpec((s, tc), lambda j: (0, off * nb + j))
    blk = pl.BlockSpec((s, tc), lambda j: (0, j))
    wblk = pl.BlockSpec((CONV_K, tc), lambda j: (0, j))
    act = jax.ShapeDtypeStruct((s, d), BF16)
    return _pcall(
        body, name=name, grid=(nb,), in_specs=[blk, col(0), col(1), col(2), wblk],
        out_specs=[blk, blk, blk, wblk], out_shape=[act, act, act, jax.ShapeDtypeStruct((CONV_K, d), F32)],
        compiler_params=_params("parallel"),
    )(dy, proj, proj, proj, conv_w)


def _sb_tile(q, kj, scale, carry, tri, mask):
    z = _dot(q, kj, 1, 1) * scale
    lsz = jnp.minimum(z, 0.0) - jnp.log(1.0 + jnp.exp(-jnp.abs(z)))
    l1m = lsz - z
    if mask is not None:
        l1m = jnp.where(mask, l1m, 0.0)
    l1b = l1m.astype(BF16)
    a = jnp.exp(lsz + (carry + _dot(l1b, tri, 1, 0)))
    if mask is not None:
        a = jnp.where(mask, a, 0.0)
    return lsz, l1b, a.astype(BF16)


def _add_rows(x, upd, r0):
    return x + upd if r0 == 0 else jnp.concatenate([x[:r0], x[r0:] + upd], axis=0)


def _sb_masks(tq, tk):
    row = lax.broadcasted_iota(jnp.int32, (tq, tk), 0)
    col = lax.broadcasted_iota(jnp.int32, (tq, tk), 1)
    masks = [col + dj * tk < row for dj in range(tq // tk)]
    r2 = lax.broadcasted_iota(jnp.int32, (tk, tk), 0)
    c2 = lax.broadcasted_iota(jnp.int32, (tk, tk), 1)
    return masks, (r2 > c2).astype(BF16), (r2 < c2).astype(BF16)


def _sb_fwd(proj, heads, col0, tq, tk, name):
    s = proj.shape[0]
    dh = SB_HEAD_DIM
    nq, nd, nkt = s // tq, tq // tk, s // tk
    scale = dh ** -0.5

    def body(q_ref, k_ref, v_ref, o_ref, a_ref, b_ref):
        i = pl.program_id(1)
        q = q_ref[...]
        masks, tri_right, _ = _sb_masks(tq, tk)

        def tile(j, carry, acc, mask, r0=0):
            start = pl.multiple_of(j * tk, tk)
            kj = k_ref[pl.ds(start, tk), :]
            vj = v_ref[pl.ds(start, tk), :]
            lsz, l1b, ab = _sb_tile(q[r0:], kj, scale, carry[r0:], tri_right, None if mask is None else mask[r0:])
            a_ref[j, r0:, :] = ab
            b_ref[j, r0:, :] = jnp.exp(lsz).astype(b_ref.dtype)
            if r0:
                a_ref[j, :r0, :] = jnp.zeros((r0, tk), a_ref.dtype)
                b_ref[j, :r0, :] = jnp.zeros((r0, tk), b_ref.dtype)
            return (_add_rows(carry, jnp.sum(l1b.astype(F32), axis=1, keepdims=True), r0),
                    _add_rows(acc, _dot(ab, vj, 1, 0), r0))

        state = (jnp.zeros((tq, 1), F32), jnp.zeros((tq, dh), F32))
        for dj in reversed(range(nd)):
            state = tile(i * nd + dj, *state, masks[dj], dj * tk)
        def left_block(t, st):
            for dj in reversed(range(nd)):
                st = tile((i - 1 - t) * nd + dj, st[0], st[1], None)
            return st

        state = lax.fori_loop(0, i, left_block, state)
        o_ref[...] = state[1]

    qspec = pl.BlockSpec((tq, dh), lambda h, i: (i, col0[0] + h))
    kspec = pl.BlockSpec((s, dh), lambda h, i: (0, col0[1] + h))
    vspec = pl.BlockSpec((s, dh), lambda h, i: (0, col0[2] + h))
    saved = pl.BlockSpec((None, nkt, tq, tk), lambda h, i: (h, 0, i, 0))
    saved_shape = jax.ShapeDtypeStruct((heads, nkt, s, tk), BF16)
    return _pcall(
        body, name=name, grid=(heads, nq), in_specs=[qspec, kspec, vspec],
        out_specs=[pl.BlockSpec((tq, dh), lambda h, i: (i, h)), saved, saved],
        out_shape=[jax.ShapeDtypeStruct((s, heads * dh), F32), saved_shape, saved_shape],
        compiler_params=_params("parallel", "parallel"),
    )(proj, proj, proj)


SB_BWD_HEADS = 2


def _sb_bwd(proj, o, a_all, beta_all, do, heads, col0, tq, tk, name):
    s = proj.shape[0]
    dh = SB_HEAD_DIM
    nq, nd, nkt = s // tq, tq // tk, s // tk
    scale = dh ** -0.5
    hb = SB_BWD_HEADS if heads % SB_BWD_HEADS == 0 and all(c % SB_BWD_HEADS == 0 for c in col0) else 1
    wide = hb * dh

    def body(q_ref, k_ref, v_ref, o_ref, a_ref, b_ref, do_ref, dq_ref, dk_ref, dv_ref, dk_acc, dv_acc):
        i = pl.program_id(1)

        @pl.when(i == 0)
        def _():
            dk_acc[...] = jnp.zeros_like(dk_acc)
            dv_acc[...] = jnp.zeros_like(dv_acc)

        lanes = [slice(hh * dh, (hh + 1) * dh) for hh in range(hb)]
        q = [q_ref[:, ln] for ln in lanes]
        dob = [do_ref[:, ln].astype(BF16) for ln in lanes]
        delta = [jnp.sum(dob[hh].astype(F32) * o_ref[:, lanes[hh]], axis=1, keepdims=True) for hh in range(hb)]
        masks, _, tri_left = _sb_masks(tq, tk)

        def tile(hh, j, carry_g, dq, mask):
            start = pl.multiple_of(j * tk, tk)
            kj = k_ref[pl.ds(start, tk), lanes[hh]]
            vj = v_ref[pl.ds(start, tk), lanes[hh]]
            ab = a_ref[hh, j]
            g = _dot(dob[hh], vj, 1, 1) * ab.astype(F32)
            carry_g = carry_g + jnp.sum(g, axis=1, keepdims=True)
            left = (delta[hh] - carry_g) + _dot(g.astype(BF16), tri_left, 1, 0)
            dz = g - b_ref[hh, j].astype(F32) * (g + left)
            if mask is not None:
                dz = jnp.where(mask, dz, 0.0)
            dzb = dz.astype(BF16)
            dk_acc[pl.ds(start, tk), lanes[hh]] += _dot(dzb, q[hh], 0, 0)
            dv_acc[pl.ds(start, tk), lanes[hh]] += _dot(ab, dob[hh], 0, 0)
            return carry_g, dq + _dot(dzb, kj, 1, 0)

        def block(jb, st, use_masks):
            st = list(st)
            for dj in reversed(range(nd)):
                for hh in range(hb):
                    st[hh] = tile(hh, jb * nd + dj, *st[hh], masks[dj] if use_masks else None)
            return tuple(st)

        state = block(i, tuple((jnp.zeros((tq, 1), F32), jnp.zeros((tq, dh), F32)) for _ in range(hb)), True)
        state = lax.fori_loop(0, i, lambda t, st: block(i - 1 - t, st, False), state)
        for hh in range(hb):
            dq_ref[:, lanes[hh]] = (state[hh][1] * scale).astype(dq_ref.dtype)

        @pl.when(i == nq - 1)
        def _():
            dk_ref[...] = (dk_acc[...] * scale).astype(dk_ref.dtype)
            dv_ref[...] = dv_acc[...].astype(dv_ref.dtype)

    qspec = pl.BlockSpec((tq, wide), lambda h, i: (i, col0[0] // hb + h))
    kspec = pl.BlockSpec((s, wide), lambda h, i: (0, col0[1] // hb + h))
    vspec = pl.BlockSpec((s, wide), lambda h, i: (0, col0[2] // hb + h))
    blk = pl.BlockSpec((tq, wide), lambda h, i: (i, h))
    full = pl.BlockSpec((s, wide), lambda h, i: (0, h))
    saved = pl.BlockSpec((hb, nkt, tq, tk), lambda h, i: (h, 0, i, 0))
    act = jax.ShapeDtypeStruct((s, heads * dh), BF16)
    return _pcall(
        body, name=name, grid=(heads // hb, nq), in_specs=[qspec, kspec, vspec, blk, saved, saved, blk],
        out_specs=[blk, full, full], out_shape=[act, act, act],
        scratch_shapes=[pltpu.VMEM((s, wide), F32), pltpu.VMEM((s, wide), F32)],
        compiler_params=_params("parallel", "arbitrary"),
    )(proj, proj, proj, o, a_all, beta_all, do)


def _xattn_probs(q, k, scale):
    sc = _dot(q, k, 1, 1) * scale
    e = jnp.exp(sc - jnp.max(sc, axis=1, keepdims=True))
    return e / jnp.sum(e, axis=1, keepdims=True)


def _xattn_fwd(qc, kv, tq, name):
    s, d = qc.shape
    m = kv.shape[0]
    dh = d // X_HEADS
    scale = dh ** -0.5

    def body(q_ref, k_ref, v_ref, o_ref):
        p = _xattn_probs(q_ref[...], k_ref[...], scale)
        o_ref[...] = _dot(p.astype(BF16), v_ref[...], 1, 0).astype(o_ref.dtype)

    blk = pl.BlockSpec((tq, dh), lambda h, i: (i, h))
    return _pcall(
        body, name=name, grid=(X_HEADS, s // tq),
        in_specs=[blk, pl.BlockSpec((m, dh), lambda h, i: (0, h)), pl.BlockSpec((m, dh), lambda h, i: (0, X_HEADS + h))],
        out_specs=blk, out_shape=jax.ShapeDtypeStruct((s, d), BF16), compiler_params=_params("parallel", "parallel"),
    )(qc, kv, kv)


def _xattn_bwd(qc, kv, do, tq, name):
    s, d = qc.shape
    m = kv.shape[0]
    dh = d // X_HEADS
    scale = dh ** -0.5
    nq = s // tq

    def body(q_ref, k_ref, v_ref, do_ref, dq_ref, dk_ref, dv_ref, dk_acc, dv_acc):
        i = pl.program_id(1)
        q, k, v = q_ref[...], k_ref[...], v_ref[...]
        dob = do_ref[...].astype(BF16)
        p = _xattn_probs(q, k, scale)
        pb = p.astype(BF16)
        dp = _dot(dob, v, 1, 1)
        ds = pb.astype(F32) * (dp - jnp.sum(dp * pb.astype(F32), axis=1, keepdims=True))
        dsb = (ds * scale).astype(BF16)
        dq_ref[...] = _dot(dsb, k, 1, 0).astype(dq_ref.dtype)
        dk_part = _dot(dsb, q, 0, 0)
        dv_part = _dot(pb, dob, 0, 0)

        @pl.when(i == 0)
        def _():
            dk_acc[...] = dk_part
            dv_acc[...] = dv_part

        @pl.when(i > 0)
        def _():
            dk_acc[...] += dk_part
            dv_acc[...] += dv_part

        @pl.when(i == nq - 1)
        def _():
            dk_ref[...] = dk_acc[...].astype(dk_ref.dtype)
            dv_ref[...] = dv_acc[...].astype(dv_ref.dtype)

    blk = pl.BlockSpec((tq, dh), lambda h, i: (i, h))
    kblk = pl.BlockSpec((m, dh), lambda h, i: (0, h))
    return _pcall(
        body, name=name, grid=(X_HEADS, nq),
        in_specs=[blk, kblk, pl.BlockSpec((m, dh), lambda h, i: (0, X_HEADS + h)), blk],
        out_specs=[blk, kblk, kblk],
        out_shape=[jax.ShapeDtypeStruct((s, d), BF16), jax.ShapeDtypeStruct((m, d), BF16), jax.ShapeDtypeStruct((m, d), BF16)],
        scratch_shapes=[pltpu.VMEM((m, dh), F32), pltpu.VMEM((m, dh), F32)],
        compiler_params=_params("parallel", "arbitrary"),
    )(qc, kv, kv, do)


def _down_loss(act, w_down, h, tgt, g, name):
    s, f = act.shape
    d = w_down.shape[1]
    tm = _pick(s, (512, 256, 128))

    def body(a_ref, w_ref, h_ref, t_ref, g_ref, dh_ref, dhb_ref, dg_ref, loss_ref):
        xh, r = _xhat(h_ref[...] + 0.5 * _dot(a_ref[...], w_ref[...], 1, 0))
        gv = g_ref[...]
        err = xh * gv - t_ref[...]
        dy = err * (1.0 / d)
        dxh = dy * gv
        dx = r * (dxh - xh * jnp.mean(dxh * xh, axis=-1, keepdims=True))
        dh_ref[...] = dx
        dhb_ref[...] = (0.5 * dx).astype(dhb_ref.dtype)
        dg = jnp.sum(dy * xh, axis=0, keepdims=True)
        loss = jnp.broadcast_to(jnp.sum(0.5 * jnp.mean(err * err, axis=-1, keepdims=True), axis=0, keepdims=True), (1, LANES))

        @pl.when(pl.program_id(0) == 0)
        def _():
            dg_ref[...] = dg
            loss_ref[...] = loss

        @pl.when(pl.program_id(0) > 0)
        def _():
            dg_ref[...] += dg
            loss_ref[...] += loss

    row = pl.BlockSpec((tm, d), lambda i: (i, 0))
    once = lambda shape: pl.BlockSpec(shape, lambda i: (0, 0))
    return _pcall(
        body, name=name, grid=(s // tm,),
        in_specs=[pl.BlockSpec((tm, f), lambda i: (i, 0)), once((f, d)), row, row, once((1, d))],
        out_specs=[row, row, once((1, d)), once((1, LANES))],
        out_shape=[jax.ShapeDtypeStruct((s, d), F32), jax.ShapeDtypeStruct((s, d), BF16), jax.ShapeDtypeStruct((1, d), F32),
                   jax.ShapeDtypeStruct((1, LANES), F32)],
        compiler_params=_params("arbitrary"),
    )(act, w_down, h, tgt, g)


def _mix_merge(y_conv, y_sb, w_conv_out, w_attn_out, proj, gate_blocks, b_conv, b_sb, name):
    s, d = y_conv.shape
    tm, tn = _pick(s, (1024, 512, 256, 128)), _pick(d, (512, 256, 128))
    nb = d // tn

    def body(yc_ref, ys_ref, wc_ref, ws_ref, gc_ref, gs_ref, bc_ref, bs_ref, ac_ref, as_ref, m_ref):
        ac = _dot(yc_ref[...].astype(BF16), wc_ref[...], 1, 0)
        asb = _dot(ys_ref[...].astype(BF16), ws_ref[...], 1, 0)
        gc = _sigmoid(gc_ref[...].astype(F32) + bc_ref[...])
        gs = _sigmoid(gs_ref[...].astype(F32) + bs_ref[...])
        ac_ref[...] = ac.astype(ac_ref.dtype)
        as_ref[...] = asb.astype(as_ref.dtype)
        m_ref[...] = (gc * ac + gs * asb).astype(m_ref.dtype)

    rows = pl.BlockSpec((tm, d), lambda i, j: (i, 0))
    wcol = pl.BlockSpec((d, tn), lambda i, j: (0, j))
    bias = pl.BlockSpec((1, tn), lambda i, j: (0, j))
    gate = lambda blk: pl.BlockSpec((tm, tn), lambda i, j: (i, blk * nb + j))
    out = pl.BlockSpec((tm, tn), lambda i, j: (i, j))
    act = jax.ShapeDtypeStruct((s, d), BF16)
    return _pcall(
        body, name=name, grid=(s // tm, nb),
        in_specs=[rows, rows, wcol, wcol, gate(gate_blocks[0]), gate(gate_blocks[1]), bias, bias],
        out_specs=[out, out, out], out_shape=[act, act, act], compiler_params=_params("parallel", "parallel"),
    )(y_conv, y_sb, w_conv_out, w_attn_out, proj, proj, b_conv, b_sb)


def _mix_dmerge(dh, w_o, a_conv, a_sb, proj, gate_blocks, b_conv, b_sb, name):
    s, d = a_conv.shape
    tm, tn = _pick(s, (1024, 512, 256, 128)), _pick(d, (512, 256, 128))
    nb = d // tn

    def body(dh_ref, w_ref, ac_ref, as_ref, gc_ref, gs_ref, bc_ref, bs_ref, dac_ref, das_ref, dgc_ref, dgs_ref, dbc_ref, dbs_ref):
        dm = _dot(dh_ref[...], w_ref[...], 1, 1)
        gc = _sigmoid(gc_ref[...].astype(F32) + bc_ref[...])
        gs = _sigmoid(gs_ref[...].astype(F32) + bs_ref[...])
        dgc = dm * ac_ref[...].astype(F32) * gc * (1.0 - gc)
        dgs = dm * as_ref[...].astype(F32) * gs * (1.0 - gs)
        dac_ref[...] = (dm * gc).astype(dac_ref.dtype)
        das_ref[...] = (dm * gs).astype(das_ref.dtype)
        dgc_ref[...] = dgc.astype(dgc_ref.dtype)
        dgs_ref[...] = dgs.astype(dgs_ref.dtype)
        sums = jnp.sum(dgc, axis=0, keepdims=True), jnp.sum(dgs, axis=0, keepdims=True)

        @pl.when(pl.program_id(1) == 0)
        def _():
            dbc_ref[...], dbs_ref[...] = sums

        @pl.when(pl.program_id(1) > 0)
        def _():
            dbc_ref[...] += sums[0]
            dbs_ref[...] += sums[1]

    tile = pl.BlockSpec((tm, tn), lambda j, i: (i, j))
    bias = pl.BlockSpec((1, tn), lambda j, i: (0, j))
    gate = lambda blk: pl.BlockSpec((tm, tn), lambda j, i: (i, blk * nb + j))
    act = jax.ShapeDtypeStruct((s, d), BF16)
    vec = jax.ShapeDtypeStruct((1, d), F32)
    return _pcall(
        body, name=name, grid=(nb, s // tm),
        in_specs=[pl.BlockSpec((tm, d), lambda j, i: (i, 0)), pl.BlockSpec((tn, d), lambda j, i: (j, 0)), tile, tile,
                  gate(gate_blocks[0]), gate(gate_blocks[1]), bias, bias],
        out_specs=[tile, tile, tile, tile, bias, bias], out_shape=[act, act, act, act, vec, vec],
        compiler_params=_params("parallel", "arbitrary"),
    )(dh, w_o, a_conv, a_sb, proj, proj, b_conv, b_sb)


def _local_step(x, mem, tgt, w, fetch=None, prefetch=None, emit=None, tick=None, after=None):
    fetch = fetch or (lambda name, after: {})
    prefetch = prefetch or (lambda name, after: None)
    emit = emit or (lambda group, g: None)
    tick = tick or (lambda group, after: None)
    w = dict(w)
    s, d = x.shape
    heads = d // SB_HEAD_DIM
    tm = _pick(s, (1024, 512, 256, 128))
    tq = _pick(s, (2048, 1024, 512, 256, 128))
    sb_tq, sb_tk = _pick(s, (512, 256, 128)), _pick(s, (256, 128))
    tc = _pick(d, (256, 128))
    g = {}

    def wt(name, after):
        if name not in w:
            w.update(fetch(name, after))
        return w[name]

    def ffn_fwd(h, n, wgu, wdown, tag, next_g=None):
        gu, act = _ffn_up(n, wt(wgu, n), tag + "_gu")
        prefetch(wdown, gu)
        return gu, act, _mm(act, wt(wdown, act), name=tag + "_down", out_dtype=F32, res=h, alpha=0.5, norm_g=next_g)

    def ffn_bwd(dh, dhb, h, saved, gname, wgu, wdown, tag, copy_scale=None, after=None):
        n, gu, act = saved
        g[wdown] = _mm(act, dhb, ta=True, name=tag + "_dwdown", after=after)
        dgu = _ffn_dgu(dhb, w[wdown], gu, tag + "_dgu", after=emit(tag + "_down", g))
        g[wgu] = _mm(n, dgu, ta=True, b_halves=True, name=tag + "_dwgu", after=tick(tag + "_down", dgu))
        *dh_in, g[gname] = _dgrad_norm(dgu, w[wgu], dh, h, w[gname], tag + "_dn", dy_halves=True, copy_scale=copy_scale,
                                       after=emit(tag, g))
        return dh_in, tick(tag, dh_in[0])

    n1 = _rms_fwd(x, w["g_ffn1"], "ffn1_norm", tm, after=after)
    gu1, act1, (h1, u) = ffn_fwd(x, n1, "w_ffn1_gu", "w_ffn1_down", "ffn1", w["g_mix"])
    prefetch("w_in", h1)
    proj = _mm(u, wt("w_in", u), name="mix_in")
    prefetch("w_conv_out", proj)
    nd = d // SB_HEAD_DIM
    y_conv = _conv_fwd(proj, w["conv_w"], d, tc, "conv_fwd")
    sb_cols = (3 * nd, 4 * nd, 5 * nd)
    y_sb, sb_a, sb_beta = _sb_fwd(proj, heads, sb_cols, _pick(s, (2 * sb_tq, sb_tq)), sb_tk, "sb_fwd")
    prefetch("w_cq", y_sb)
    b_conv, b_sb = w["b_gate"][:, :d], w["b_gate"][:, d:]
    a_conv, a_sb, merged = _mix_merge(y_conv, y_sb, wt("w_conv_out", y_conv), wt("w_attn_out", y_sb), proj, (6, 7), b_conv, b_sb,
                                      "mix_merge")
    prefetch("w_ffn2_gu", merged)
    h2, hn = _mm(merged, wt("w_o", merged), name="mix_out", out_dtype=F32, res=h1, norm_g=w["g_cross"])
    mn = _rms_fwd(mem, w["g_mem"], "mem_norm", _pick(mem.shape[0], (256, 128)))
    qc = _mm(hn, wt("w_cq", hn), name="cross_q")
    kv = _mm(mn, wt("w_ckv", mn), name="cross_kv")
    oc = _xattn_fwd(qc, kv, tq, "xattn_fwd")
    h3, n2 = _mm(oc, wt("w_co", oc), name="cross_out", out_dtype=F32, res=h2, norm_g=w["g_ffn2"])
    gu2, act2 = _ffn_up(n2, wt("w_ffn2_gu", n2), "ffn2_gu")

    dh4, dh4b, g["g_final"], loss_lanes = _down_loss(act2, wt("w_ffn2_down", act2), h3, tgt, w["g_final"], "ffn2_down_loss")

    (dh3, dh3b), tok = ffn_bwd(dh4, dh4b, h3, (n2, gu2, act2), "g_ffn2", "w_ffn2_gu", "w_ffn2_down", "ffn2", copy_scale=1.0)
    g["w_co"] = _mm(oc, dh3b, ta=True, name="cross_dwco", after=tok)
    doc = _mm(dh3b, w["w_co"], tb=True, name="cross_doc")
    dqc, dk, dv = _xattn_bwd(qc, kv, doc, tq, "xattn_bwd")
    dkv = jnp.concatenate([dk, dv], axis=1)
    g["w_cq"] = _mm(hn, dqc, ta=True, name="cross_dwcq")
    g["w_ckv"] = _mm(mn, dkv, ta=True, name="cross_dwckv")
    dmn = _mm(dkv, w["w_ckv"], tb=True, name="cross_dmn", out_dtype=F32)
    g["g_mem"] = _rowcall(lambda dy, xb: dy * _xhat(xb)[0], [_whole(dmn), _whole(mem)], [], [], [d],
                          tm=_pick(mem.shape[0], (256, 128)), name="mem_dnorm")[0]
    dh2, dh2b, g["g_cross"] = _dgrad_norm(dqc, w["w_cq"], dh3, h2, w["g_cross"], "cross_dhn", copy_scale=1.0, after=emit("cross", g))

    g["w_o"] = _mm(merged, dh2b, ta=True, name="mix_dwo", after=tick("cross", dh2))
    da_conv, da_sb, dgc, dgs, db_conv, db_sb = _mix_dmerge(dh2b, w["w_o"], a_conv, a_sb, proj, (6, 7), b_conv, b_sb, "mix_dmerge")
    g["b_gate"] = jnp.concatenate([db_conv, db_sb], axis=1)
    g["w_conv_out"] = _mm(y_conv, da_conv, ta=True, name="conv_dwout")
    g["w_attn_out"] = _mm(y_sb, da_sb, ta=True, name="attn_dwout")
    dy_conv = _mm(da_conv, w["w_conv_out"], tb=True, name="conv_dy")
    dy_sb = _mm(da_sb, w["w_attn_out"], tb=True, name="attn_dy")
    dcb, dcc, dcx, g["conv_w"] = _conv_bwd(dy_conv, proj, w["conv_w"], d, tc, "conv_bwd")
    dq, dk_sb, dv_sb = _sb_bwd(proj, y_sb, sb_a, sb_beta, dy_sb, heads, sb_cols, sb_tq, sb_tk, "sb_bwd")
    dproj = jnp.concatenate([dcb, dcc, dcx, dq, dk_sb, dv_sb, dgc, dgs], axis=1)
    g["w_in"] = _mm(u, dproj, ta=True, name="mix_dwin")
    dh1, dh1b, g["g_mix"] = _dgrad_norm(dproj, w["w_in"], dh2, h1, w["g_mix"], "mix_du", copy_scale=0.5, after=emit("mix", g))
    (dx,), tok = ffn_bwd(dh1, dh1b, x, (n1, gu1, act1), "g_ffn1", "w_ffn1_gu", "w_ffn1_down", "ffn1", after=tick("mix", dh1))
    return loss_lanes, dx, g, tok


MATS = (("w_ffn1_gu", "col"), ("w_ffn1_down", "row"), ("w_in", "col"), ("w_conv_out", "row"), ("w_attn_out", "row"),
        ("w_o", "row"), ("w_cq", "row"), ("w_ckv", "col"), ("w_co", "row"), ("w_ffn2_gu", "col"), ("w_ffn2_down", "row"))
VECS = ("g_ffn1", "g_mix", "g_cross", "g_mem", "g_ffn2", "g_final")
WEIGHTS = ("g_ffn1", "w_ffn1_gu", "w_ffn1_down", "g_mix", "w_in", "b_gate", "conv_w", "w_conv_out", "w_attn_out", "w_o",
           "g_cross", "g_mem", "w_cq", "w_ckv", "w_co", "g_ffn2", "w_ffn2_gu", "w_ffn2_down", "g_final")
CONV_ROWS = 16


def _full_shape(kind, r, c):
    return (r, N_CHIPS * c) if kind == "col" else (N_CHIPS * r, c)


def _piece(ref, kind, r, c, chip, half):
    hr = r // 2
    if kind == "col":
        return ref.at[pl.ds(pl.multiple_of(half * hr, math.gcd(hr, 16)), hr), pl.ds(pl.multiple_of(chip * c, LANES), c)]
    return ref.at[pl.ds(pl.multiple_of(chip * r + half * hr, math.gcd(hr, 16)), hr), :]


def _shard_of(ref, kind, r, c, chip):
    if kind == "col":
        return ref.at[:, pl.ds(pl.multiple_of(chip * c, LANES), c)]
    return ref.at[pl.ds(pl.multiple_of(chip * r, 16), r), :]


def _place():
    x, y, c = lax.axis_index("x"), lax.axis_index("y"), lax.axis_index("c")
    others = [(1 - x, y), (x, 1 - y), (1 - x, 1 - y)]
    return x, y, c, 2 * x + y, others


def _remote(src, dst, send_sem, recv_sem, to):
    return pltpu.make_async_remote_copy(src_ref=src, dst_ref=dst, send_sem=send_sem, recv_sem=recv_sem,
                                        device_id=to, device_id_type=MESH)


HBM = pl.BlockSpec(memory_space=pltpu.HBM)
SEM = pl.BlockSpec(memory_space=pltpu.SEMAPHORE)
EFFECT = pltpu.SideEffectType.DATAFLOW_SIDE_EFFECTING
TOKEN = (8, LANES)


def _split_start(name, plan, n_copies, srcs, lands, after=None):
    ns, nl = len(srcs), len(lands)
    n_in = ns + nl + (after is not None)

    def body(*refs):
        outs = refs[n_in:]
        sends, _ = plan(refs[:ns], refs[ns:ns + nl], outs[0], outs[1])
        for cp in sends:
            cp.start()
        outs[-1][...] = jnp.zeros(TOKEN, F32)

    held = [pltpu.HBM(a.shape, a.dtype) for a in (*srcs, *lands)]
    dma = pltpu.SemaphoreType.DMA((n_copies,))
    ins = [pltpu.with_memory_space_constraint(a, pltpu.HBM) for a in (*srcs, *lands)]
    outs = _pcall(
        body, name=name, in_specs=[HBM] * (ns + nl) + ([] if after is None else [ANY]),
        out_specs=(SEM, SEM, *[HBM] * (ns + nl), pl.BlockSpec(memory_space=pltpu.VMEM)),
        out_shape=(dma, dma, *held, jax.ShapeDtypeStruct(TOKEN, F32)),
        input_output_aliases={i: 2 + i for i in range(ns + nl)},
        compiler_params=pltpu.CompilerParams(has_side_effects=EFFECT),
    )(*ins, *([] if after is None else [after]))
    return outs[0], outs[1], list(outs[2:2 + ns]), list(outs[2 + ns:2 + ns + nl]), outs[-1]


def _split_wait(name, plan, send_sems, recv_sems, srcs, lands, after):
    ns, nl = len(srcs), len(lands)

    def body(*refs):
        sends, recvs = plan(refs[:ns], refs[ns:ns + nl], refs[ns + nl], refs[ns + nl + 1])
        for cp in sends:
            cp.wait_send()
        for cp in recvs:
            cp.wait_recv()

    outs = _pcall(
        body, name=name, in_specs=[HBM] * (ns + nl) + [SEM, SEM, ANY], out_specs=[HBM] * (ns + nl),
        out_shape=[pltpu.HBM(a.shape, a.dtype) for a in (*srcs, *lands)],
        input_output_aliases={i: i for i in range(ns + nl)},
        compiler_params=pltpu.CompilerParams(has_side_effects=EFFECT),
    )(*srcs, *lands, send_sems, recv_sems, after)
    return list(outs[:ns]), list(outs[ns:])


def _gather_plan(dims):
    def plan(shard_refs, full_refs, ss, rs):
        x, y, c, me, others = _place()
        sends, recvs = [], []
        for wi, (kind, r, cw) in enumerate(dims):
            half = shard_refs[wi].at[pl.ds(pl.multiple_of(c * (r // 2), math.gcd(r // 2, 16)), r // 2), :]
            for k, (ox, oy) in enumerate(others):
                sem = 4 * wi + k
                sends.append(_remote(half, _piece(full_refs[wi], kind, r, cw, me, c), ss.at[sem], rs.at[sem], (ox, oy, c)))
                recvs.append(_remote(half, _piece(full_refs[wi], kind, r, cw, 2 * ox + oy, c), ss.at[sem], rs.at[sem], (x, y, c)))
            sem = 4 * wi + 3
            own = _remote(shard_refs[wi], _shard_of(full_refs[wi], kind, r, cw, me), ss.at[sem], rs.at[sem], (x, y, 1 - c))
            sends.append(own)
            recvs.append(own)
        return sends, recvs

    return plan


def _forward_plan(dims):
    def plan(_, full_refs, ss, rs):
        x, y, c, _, others = _place()
        sends, recvs = [], []
        for wi, (kind, r, cw) in enumerate(dims):
            for k, (ox, oy) in enumerate(others):
                sem = 3 * wi + k
                mine = _piece(full_refs[wi], kind, r, cw, 2 * ox + oy, c)
                theirs = _piece(full_refs[wi], kind, r, cw, 2 * ox + oy, 1 - c)
                sends.append(_remote(mine, mine, ss.at[sem], rs.at[sem], (x, y, 1 - c)))
                recvs.append(_remote(theirs, theirs, ss.at[sem], rs.at[sem], (x, y, 1 - c)))
        return sends, recvs

    return plan


def _rs_cores_plan(dims):
    def plan(g_refs, land_refs, ss, rs):
        x, y, c, _, _ = _place()
        sends, recvs = [], []
        for wi, dm in enumerate(dims):
            for chip in range(N_CHIPS):
                sem = N_CHIPS * wi + chip
                sends.append(_remote(_piece(g_refs[wi], *dm, chip, 1 - c), land_refs[wi].at[chip], ss.at[sem], rs.at[sem], (x, y, 1 - c)))
                recvs.append(_remote(_piece(g_refs[wi], *dm, chip, c), land_refs[wi].at[chip], ss.at[sem], rs.at[sem], (x, y, 1 - c)))
        return sends, recvs

    return plan


def _share_plan(nw):
    def plan(_, buf_refs, ss, rs):
        x, y, c, _, _ = _place()
        sends = [_remote(buf_refs[wi].at[c], buf_refs[wi].at[c], ss.at[wi], rs.at[wi], (x, y, 1 - c)) for wi in range(nw)]
        recvs = [_remote(buf_refs[wi].at[1 - c], buf_refs[wi].at[1 - c], ss.at[wi], rs.at[wi], (x, y, 1 - c)) for wi in range(nw)]
        return sends, recvs

    return plan


def _small_plan():
    def plan(_, buf_refs, ss, rs):
        x, y, c = lax.axis_index("x"), lax.axis_index("y"), lax.axis_index("c")
        buf = buf_refs[0]
        sends, recvs = [], []
        for rel in range(1, N_DEV):
            peer = (x ^ (rel >> 2 & 1), y ^ (rel >> 1 & 1), c ^ (rel & 1))
            sends.append(_remote(buf.at[0], buf.at[rel], ss.at[rel - 1], rs.at[rel - 1], peer))
            recvs.append(_remote(buf.at[0], buf.at[rel], ss.at[rel - 1], rs.at[rel - 1], peer))
        return sends, recvs

    return plan


def _sum_small(buf, me, name):
    _, rows, n = buf.shape

    def body(me_ref, b_ref, o_ref):
        tot = b_ref[me_ref[0]]
        for dev in range(1, N_DEV):
            tot = tot + b_ref[dev ^ me_ref[0]]
        o_ref[...] = tot

    return _pcall(
        body, name=name, out_shape=jax.ShapeDtypeStruct((rows, n), F32),
        grid_spec=pltpu.PrefetchScalarGridSpec(
            num_scalar_prefetch=1, grid=(1,), in_specs=[pl.BlockSpec((N_DEV, rows, n), lambda i, m: (0, 0, 0))],
            out_specs=pl.BlockSpec((rows, n), lambda i, m: (0, 0))),
    )(me, buf)


def _rs_chips_plan(nw):
    def plan(p_refs, land_refs, ss, rs):
        x, y, c, me, others = _place()
        sends, recvs = [], []
        for wi in range(nw):
            for k, (ox, oy) in enumerate(others):
                sem = 3 * wi + k
                sends.append(_remote(p_refs[wi].at[2 * ox + oy], land_refs[wi].at[k], ss.at[sem], rs.at[sem], (ox, oy, c)))
                recvs.append(_remote(p_refs[wi].at[me], land_refs[wi].at[k], ss.at[sem], rs.at[sem], (x, y, c)))
        return sends, recvs

    return plan


SUM_BLOCK_BYTES = 4 << 20


def _rows_per_block(n, c, limit_bytes=2 << 20):
    best = None
    for tm in range(16, n + 1, 16):
        if n % tm == 0 and tm * c * 4 <= limit_bytes:
            best = tm
    return best or n


def _sum_cores(grad, got, kind, place, name):
    _, hr, cw = got.shape
    tm = _rows_per_block(hr, cw, SUM_BLOCK_BYTES)
    nb = hr // tm

    def body(place_ref, g_ref, t_ref, o_ref):
        o_ref[...] = (g_ref[...].astype(F32) + t_ref[...].astype(F32)).astype(o_ref.dtype)

    if kind == "col":
        g_spec = pl.BlockSpec((tm, cw), lambda j, i, pr: (pr[0] * nb + i, j))
    else:
        g_spec = pl.BlockSpec((tm, cw), lambda j, i, pr: ((2 * j + pr[0]) * nb + i, 0))
    blk = pl.BlockSpec((None, tm, cw), lambda j, i, pr: (j, i, 0))
    return _pcall(
        body, name=name, out_shape=jax.ShapeDtypeStruct(got.shape, BF16),
        grid_spec=pltpu.PrefetchScalarGridSpec(num_scalar_prefetch=1, grid=(N_CHIPS, nb), in_specs=[g_spec, blk], out_specs=blk),
        compiler_params=_params("parallel", "parallel"),
    )(place, grad, got)


def _sum_chips(parts, got, place, name):
    _, n, cw = got.shape
    tm = _rows_per_block(n, cw, SUM_BLOCK_BYTES)

    def body(place_ref, p_ref, g_ref, o_ref):
        tot = p_ref[...].astype(F32)
        for k in range(3):
            tot = tot + g_ref[k].astype(F32)
        o_ref[...] = tot

    return _pcall(
        body, name=name, out_shape=jax.ShapeDtypeStruct((2, n, cw), F32),
        grid_spec=pltpu.PrefetchScalarGridSpec(
            num_scalar_prefetch=1, grid=(n // tm,),
            in_specs=[pl.BlockSpec((None, tm, cw), lambda i, pr: (pr[1], i, 0)), pl.BlockSpec((3, tm, cw), lambda i, pr: (0, i, 0))],
            out_specs=pl.BlockSpec((None, tm, cw), lambda i, pr: (pr[0], i, 0))),
        compiler_params=_params("parallel"),
    )(place, parts, got)


def _adamw(g, w, m, v, name):
    n, c = g.shape
    c1 = 1.0 - ADAM_B1 ** ADAM_STEP
    c2 = 1.0 - ADAM_B2 ** ADAM_STEP

    def fn(gb, wb, mb, vb):
        m_new = ADAM_B1 * mb + (1.0 - ADAM_B1) * gb
        v_new = ADAM_B2 * vb + (1.0 - ADAM_B2) * (gb * gb)
        delta = -ADAM_LR * ((m_new / c1) / (jnp.sqrt(v_new / c2) + ADAM_EPS) + ADAM_WD * wb)
        return gb, delta, m_new, v_new

    tm = _rows_per_block(n, c) if n % 16 == 0 else n
    return _rowcall(fn, [_whole(g), _whole(w), _whole(m), _whole(v)], [], [(c, F32)] * 4, tm=tm, name=name)


PACK_ROWS = 16


def _pack_rows(parts, width, name, after=None):
    assert sum(p.shape[0] for p in parts) <= PACK_ROWS

    def body(*refs):
        out_ref = refs[-1]
        out_ref[...] = jnp.zeros_like(out_ref)
        at = 0
        for r in refs[:len(parts)]:
            k, n = r.shape
            if n == width:
                out_ref[at:at + k, :] = r[...]
            else:
                out_ref[at:at + k, :] = jnp.broadcast_to(r[:, :1], (k, width))
            at += k

    vm = pl.BlockSpec(memory_space=pltpu.VMEM)
    return _pcall(body, name=name, in_specs=[vm] * len(parts) + ([] if after is None else [ANY]), out_specs=vm,
                  out_shape=jax.ShapeDtypeStruct((PACK_ROWS, width), F32))(*parts, *([] if after is None else [after]))


def _cast_shard(wm, name, after):
    n, c = wm.shape
    return _rowcall(lambda v: v, [_whole(wm)], [], [(c, BF16)], tm=_rows_per_block(n, c), name=name, after=after)[0]


GATHER_GROUPS = (
    ("w_ffn1_gu", "conv_w"), ("w_ffn1_down",), ("w_in",), ("w_conv_out", "w_attn_out", "w_o"), ("w_cq", "w_ckv", "w_co"),
    ("w_ffn2_gu", "w_ffn2_down"),
)
REDUCE_GROUPS = {
    "ffn2": ("w_ffn2_down", "w_ffn2_gu"),
    "cross": ("w_co", "w_cq", "w_ckv"),
    "mix": ("w_o", "w_conv_out", "w_attn_out", "w_in"),
    "ffn1_down": ("w_ffn1_down",),
    "ffn1": ("w_ffn1_gu",),
}
TAIL_STAGES = (("ffn2", "cross"), ("mix",), ("ffn1_down", "ffn1"))
KIND = dict(MATS)


def _step(x, mem, tgt, wts, m_in, v_in):
    d = x.shape[-1]
    cc = wts["conv_w"].shape[1]
    place = jnp.stack([lax.axis_index("c"), 2 * lax.axis_index("x") + lax.axis_index("y")]).astype(jnp.int32)
    dims = {n: (kind, *wts[n].shape) for n, kind in MATS}
    dims["conv_w"] = ("col", CONV_ROWS, cc)

    w = {n: wts[n].reshape(1, -1) for n in VECS + ("b_gate",)}
    flying, token = {}, None
    for names in GATHER_GROUPS:
        gd = [dims[n] for n in names]
        shards = [jnp.pad(wts[n], ((0, CONV_ROWS - CONV_K), (0, 0))) if n == "conv_w" else _cast_shard(wts[n], "cast_" + n, token)
                  for n in names]
        lands = [lax.empty(_full_shape(*dm), sh.dtype) for dm, sh in zip(gd, shards)]
        plan = _gather_plan(gd)
        ss, rs, srcs, lands, token = _split_start("gather_start_" + names[0], plan, 4 * len(names), shards, lands, token)
        flying.update({n: (names, plan, ss, rs, srcs, lands, gd) for n in names})

    passing = {}

    def prefetch(name, after):
        if name not in passing:
            names, plan, ss, rs, srcs, lands, gd = flying[name]
            _, lands = _split_wait("gather_wait_" + names[0], plan, ss, rs, srcs, lands, after)
            plan = _forward_plan(gd)
            ss, rs, _, lands, _ = _split_start("forward_start_" + names[0], plan, 3 * len(names), [], lands)
            passing.update({n: (names, plan, ss, rs, lands) for n in names})

    def fetch(name, after):
        prefetch(name, after)
        names, plan, ss, rs, lands = passing[name]
        _, lands = _split_wait("forward_wait_" + names[0], plan, ss, rs, [], lands, after)
        return {n: (land[:CONV_K] if n == "conv_w" else land) for n, land in zip(names, lands)}

    swapping, sent = {}, {}

    def emit(tag, g):
        if tag not in REDUCE_GROUPS:
            return None
        names = REDUCE_GROUPS[tag]
        gd = [dims[n] for n in names]
        lands = [lax.empty((N_CHIPS, r // 2, cw), BF16) for (_, r, cw) in gd]
        plan = _rs_cores_plan(gd)
        ss, rs, srcs, lands, tok = _split_start("rs_cores_start_" + tag, plan, N_CHIPS * len(names), [g[n] for n in names], lands)
        swapping[tag] = (plan, ss, rs, srcs, lands)
        return tok

    def tick(tag, after):
        if tag not in REDUCE_GROUPS:
            return None
        names = REDUCE_GROUPS[tag]
        plan, ss, rs, srcs, lands = swapping[tag]
        mine, got = _split_wait("rs_cores_wait_" + tag, plan, ss, rs, srcs, lands, after)
        parts = [_sum_cores(gm, t, KIND[n], place, "sum_cores_" + n) for n, gm, t in zip(names, mine, got)]
        lands = [lax.empty((3, *p.shape[1:]), BF16) for p in parts]
        plan = _rs_chips_plan(len(names))
        ss, rs, srcs, lands, tok = _split_start("rs_chips_start_" + tag, plan, 3 * len(names), parts, lands)
        sent[tag] = (plan, ss, rs, srcs, lands)
        return tok

    loss_lanes, dx, g, last = _local_step(x[0], mem[0], tgt[0], w, fetch, prefetch, emit, tick, token)

    rows = [g[n] for n in VECS] + [g["b_gate"][:, :d], g["b_gate"][:, d:], g["conv_w"], loss_lanes]
    packed = _pack_rows(rows, d, "pack_small", after=last)
    small = jnp.concatenate([packed[None], jnp.zeros((N_DEV - 1, *packed.shape), F32)], axis=0)
    small_plan = _small_plan()
    small_ss, small_rs, _, small, after = _split_start("small_start", small_plan, N_DEV - 1, [], [small])

    grads, out = {}, {}

    def update(n):
        shape = wts[n].shape
        as2d = (lambda a: a.reshape(1, -1)) if len(shape) == 1 else (lambda a: a)
        return [r.reshape(shape) for r in _adamw(grads[n], as2d(wts[n]), as2d(m_in[n]), as2d(v_in[n]), "adamw_" + n)]

    def finish(sharing, after):
        tag, names, plan, ss, rs, halves = sharing
        _, both = _split_wait("share_wait_" + tag, plan, ss, rs, [], halves, after)
        for n, b in zip(names, both):
            grads[n] = b.reshape(-1, b.shape[-1])
            out[n] = update(n)
        return out[names[-1]][1]

    sharing = None
    for stage in TAIL_STAGES:
        names, halves = [], []
        for tag in stage:
            plan, ss, rs, srcs, lands = sent[tag]
            parts, landed = _split_wait("rs_chips_wait_" + tag, plan, ss, rs, srcs, lands, after)
            halves += [_sum_chips(p, t, place, "sum_chips_" + n) for n, p, t in zip(REDUCE_GROUPS[tag], parts, landed)]
            names += REDUCE_GROUPS[tag]
        plan = _share_plan(len(names))
        ss, rs, _, halves, after = _split_start("share_start_" + stage[0], plan, len(names), [], halves)
        if sharing is not None:
            after = finish(sharing, after)
        sharing = (stage[0], names, plan, ss, rs, halves)
    after = finish(sharing, after)

    _, small = _split_wait("small_wait", small_plan, small_ss, small_rs, [], small, after)
    me = (4 * lax.axis_index("x") + 2 * lax.axis_index("y") + lax.axis_index("c")).astype(jnp.int32).reshape(1)
    red = _sum_small(small[0], me, "sum_small")
    grads.update({n: red[i:i + 1] for i, n in enumerate(VECS)})
    nv = len(VECS)
    grads["b_gate"] = jnp.concatenate([red[nv:nv + 1], red[nv + 1:nv + 2]], axis=1)
    chip = 2 * lax.axis_index("x") + lax.axis_index("y")
    grads["conv_w"] = lax.dynamic_slice_in_dim(red[nv + 2:nv + 2 + CONV_K], chip * cc, cc, axis=1)
    loss = red[nv + 2 + CONV_K, 0]
    out.update({n: update(n) for n in WEIGHTS if n not in KIND})
    return (loss, dx[None], *[out[n][0] for n in WEIGHTS], *[out[n][1] for n in WEIGHTS],
            *[out[n][2] for n in WEIGHTS], *[out[n][3] for n in WEIGHTS])


def kernel(x, mem, g_ffn1, w_ffn1_gu, w_ffn1_down, g_mix, w_in, b_gate, conv_w, w_conv_out, w_attn_out, w_o, g_cross, g_mem, w_cq, w_ckv, w_co, g_ffn2, w_ffn2_gu, w_ffn2_down, g_final, loss_target, m_g_ffn1, m_w_ffn1_gu, m_w_ffn1_down, m_g_mix, m_w_in, m_b_gate, m_conv_w, m_w_conv_out, m_w_attn_out, m_w_o, m_g_cross, m_g_mem, m_w_cq, m_w_ckv, m_w_co, m_g_ffn2, m_w_ffn2_gu, m_w_ffn2_down, m_g_final, v_g_ffn1, v_w_ffn1_gu, v_w_ffn1_down, v_g_mix, v_w_in, v_b_gate, v_conv_w, v_w_conv_out, v_w_attn_out, v_w_o, v_g_cross, v_g_mem, v_w_cq, v_w_ckv, v_w_co, v_g_ffn2, v_w_ffn2_gu, v_w_ffn2_down, v_g_final):
    given = dict(locals())
    wts = {n: given[n] for n in WEIGHTS}
    m_in = {n: given["m_" + n] for n in WEIGHTS}
    v_in = {n: given["v_" + n] for n in WEIGHTS}
    return _step(x, mem, loss_target, wts, m_in, v_in)
```

```python
import math

import jax
import jax.numpy as jnp
from jax import lax
from jax.experimental import pallas as pl
from jax.experimental.pallas import tpu as pltpu

F32 = jnp.float32
BF16 = jnp.bfloat16
MESH = pl.DeviceIdType.MESH

V7X_VMEM_LIMIT_BYTES = 48 * 1024 * 1024
MM_VMEM_BUDGET_BYTES = 36 * 1024 * 1024
MM_WHOLE_K = 2816
LANES = 128
SB_HEAD_DIM = 128
X_HEADS = 4
CONV_K = 3
RMS_EPS = 1e-6
N_CHIPS = 4
N_DEV = 8
ADAM_LR, ADAM_B1, ADAM_B2, ADAM_EPS, ADAM_WD, ADAM_STEP = 0.001, 0.9, 0.999, 1e-08, 0.01, 10


ANY = pl.BlockSpec(memory_space=pl.ANY)


def _pcall(body, **kw):
    return pl.pallas_call(body, **kw)


def _params(*sem):
    return pltpu.CompilerParams(dimension_semantics=sem, vmem_limit_bytes=V7X_VMEM_LIMIT_BYTES)


def _pick(dim, cands):
    for c in cands:
        if dim % c == 0:
            return c
    return dim


def _dot(a, b, ca, cb):
    return lax.dot_general(a, b, (((ca,), (cb,)), ((), ())), preferred_element_type=F32)


def _mm(a, b, *, name, ta=False, tb=False, out_dtype=BF16, res=None, alpha=1.0, tm=None, tn=None, tk=None, after=None,
        a_halves=False, b_halves=False, norm_g=None):
    assert not (a_halves and ta) and not (b_halves and tb) and not (norm_g is not None and ta)
    if a_halves:
        m, k = a.shape[1], 2 * a.shape[2]
    else:
        m, k = (a.shape[1], a.shape[0]) if ta else a.shape
    if b_halves:
        n = 2 * b.shape[2]
        assert k == b.shape[1]
    else:
        n = b.shape[0] if tb else b.shape[1]
        assert k == (b.shape[1] if tb else b.shape[0]), (a.shape, b.shape, ta, tb)
    if ta:
        tm = tm or _pick(m, (512, 1408, 256, 128))
        tn = tn or _pick(n, (2048, 1024, 512, 256, 128))
        tk = tk or (k if k <= MM_WHOLE_K else _pick(k, (1024, 512, 256, 128)))
    else:
        tk = tk or (k if k <= MM_WHOLE_K else _pick(k, (MM_WHOLE_K, 2048, 1024, 512, 256, 128)))
        tn = tn or (n if norm_g is not None else _pick(n, (512, 1408, 256, 128) if tk == k else (1024, 512, 256, 128)))
        out_bytes = jnp.dtype(out_dtype).itemsize + (0 if res is None else res.dtype.itemsize) + (0 if norm_g is None else 2)
        per_row = 2 * (tk * a.dtype.itemsize + tn * out_bytes)
        per_row += 4 * tn if tk < k else 0
        rows = (MM_VMEM_BUDGET_BYTES - 2 * tk * tn * b.dtype.itemsize) // per_row
        tm = tm or next((c for c in (2048, 1024, 512, 256, 128) if m % c == 0 and c <= rows), m)
    if a_halves:
        tk = min(tk, k // 2) if (k // 2) % min(tk, k // 2) == 0 else _pick(k // 2, (1408, 1024, 512, 256, 128))
    if b_halves:
        tn = tn if (n // 2) % tn == 0 else _pick(n // 2, (2816, 1408, 1024, 512, 256, 128) if ta else (1408, 1024, 512, 256, 128))
    nk = k // tk
    assert m % tm == 0 and n % tn == 0 and k % tk == 0
    a_spec = pl.BlockSpec((tk, tm), lambda i, j, kk: (kk, i)) if ta else pl.BlockSpec((tm, tk), lambda i, j, kk: (i, kk))
    b_spec = pl.BlockSpec((tn, tk), lambda i, j, kk: (j, kk)) if tb else pl.BlockSpec((tk, tn), lambda i, j, kk: (kk, j))
    if a_halves:
        per = (k // 2) // tk
        a_spec = pl.BlockSpec((None, tm, tk), lambda i, j, kk: (kk // per, i, kk % per))
    if b_halves:
        per_n = (n // 2) // tn
        b_spec = pl.BlockSpec((None, tk, tn), lambda i, j, kk: (j // per_n, kk, j % per_n))
    o_spec = pl.BlockSpec((tm, tn), lambda i, j, kk: (i, j))
    ca, cb = (0 if ta else 1), (1 if tb else 0)

    n_in = 2 + (res is not None) + (norm_g is not None) + (after is not None)
    n_out = 1 + (norm_g is not None)

    def body(*refs):
        a_ref, b_ref = refs[:2]
        res_ref = refs[2] if res is not None else None
        g_ref = refs[2 + (res is not None)] if norm_g is not None else None
        o_ref = refs[n_in]
        scratch = refs[n_in + n_out:]

        def finish(acc):
            val = acc if alpha == 1.0 else alpha * acc
            if res_ref is not None:
                val = res_ref[...].astype(F32) + val
            o_ref[...] = val.astype(o_ref.dtype)
            if g_ref is not None:
                refs[n_in + 1][...] = (_xhat(val)[0] * g_ref[...]).astype(BF16)

        part = _dot(a_ref[...].astype(BF16), b_ref[...].astype(BF16), ca, cb)
        if nk == 1:
            finish(part)
        else:
            acc_ref = scratch[0]
            kk = pl.program_id(2)

            @pl.when(kk == 0)
            def _():
                acc_ref[...] = part

            @pl.when(kk > 0)
            def _():
                acc_ref[...] += part

            @pl.when(kk == nk - 1)
            def _():
                finish(acc_ref[...])

    ins = [a, b] + ([] if res is None else [res]) + ([] if norm_g is None else [norm_g]) + ([] if after is None else [after])
    in_specs = [a_spec, b_spec] + ([] if res is None else [o_spec])
    in_specs += ([] if norm_g is None else [pl.BlockSpec((1, tn), lambda i, j, kk: (0, j))]) + ([] if after is None else [ANY])
    outs = _pcall(
        body, name=name, grid=(m // tm, n // tn, nk), in_specs=in_specs, out_specs=[o_spec] * n_out,
        out_shape=[jax.ShapeDtypeStruct((m, n), out_dtype)] + [jax.ShapeDtypeStruct((m, n), BF16)] * (n_out - 1),
        scratch_shapes=[pltpu.VMEM((tm, tn), F32)] if nk > 1 else [],
        compiler_params=_params("parallel", "parallel", "arbitrary"),
    )(*ins)
    return outs[0] if norm_g is None else outs


def _rowcall(fn, rows, consts, outs, accs=(), *, tm, name, after=None):
    s = rows[0][0].shape[0]
    assert s % tm == 0
    n_read, n_out = len(rows) + len(consts), len(outs)
    n_in = n_read + (after is not None)

    def body(*refs):
        vals = fn(*[r[...] for r in refs[:n_read]])
        vals = vals if isinstance(vals, (tuple, list)) else (vals,)
        for o_ref, v in zip(refs[n_in:n_in + n_out], vals[:n_out]):
            o_ref[...] = v.astype(o_ref.dtype)
        if accs:
            first = pl.program_id(0) == 0
            for a_ref, v in zip(refs[n_in + n_out:], vals[n_out:]):
                tot = jnp.sum(v.astype(F32), axis=0, keepdims=True)

                @pl.when(first)
                def _(a_ref=a_ref, tot=tot):
                    a_ref[...] = tot

                @pl.when(jnp.logical_not(first))
                def _(a_ref=a_ref, tot=tot):
                    a_ref[...] += tot

    in_specs = [pl.BlockSpec((tm, w), lambda i, cb=cb: (i, cb)) for (_, cb, w) in rows]
    in_specs += [pl.BlockSpec(c.shape, lambda i: (0, 0)) for c in consts]
    in_specs += [] if after is None else [ANY]
    out_specs = [pl.BlockSpec((tm, w), lambda i: (i, 0)) for (w, _) in outs]
    out_specs += [pl.BlockSpec((1, w), lambda i: (0, 0)) for w in accs]
    out_shape = [jax.ShapeDtypeStruct((s, w), dt) for (w, dt) in outs]
    out_shape += [jax.ShapeDtypeStruct((1, w), F32) for w in accs]
    return _pcall(
        body, name=name, grid=(s // tm,), in_specs=in_specs, out_specs=out_specs, out_shape=out_shape,
        compiler_params=_params("arbitrary" if accs else "parallel"),
    )(*[r[0] for r in rows], *consts, *([] if after is None else [after]))


def _whole(a):
    return (a, 0, a.shape[1])


def _xhat(x):
    x = x.astype(F32)
    r = lax.rsqrt(jnp.mean(x * x, axis=-1, keepdims=True) + RMS_EPS)
    return x * r, r


def _rms_bwd(dy, x, g):
    xh, r = _xhat(x)
    dxh = dy.astype(F32) * g
    dx = r * (dxh - xh * jnp.mean(dxh * xh, axis=-1, keepdims=True))
    return dx, dy.astype(F32) * xh


def _sigmoid(x):
    return 1.0 / (1.0 + jnp.exp(-x))


def _rms_fwd(x, g, name, tm, after=None):
    d = x.shape[1]
    return _rowcall(lambda xb, gb: _xhat(xb)[0] * gb, [_whole(x)], [g], [(d, BF16)], tm=tm, name=name, after=after)[0]


def _silu_parts(gate):
    sg = _sigmoid(gate)
    return sg, gate * sg


def _ffn_up(n, w_gu, name):
    s, d = n.shape
    f = w_gu.shape[1] // 2
    tn = _pick(f, (1408, 1024, 512, 256, 128))
    tm = _pick(s, (1024, 512, 256, 128))
    nb = f // tn

    def body(n_ref, wg_ref, wu_ref, gu_ref, act_ref):
        nv = n_ref[...]
        gate = _dot(nv, wg_ref[...], 1, 0)
        up = _dot(nv, wu_ref[...], 1, 0)
        gu_ref[0] = gate.astype(gu_ref.dtype)
        gu_ref[1] = up.astype(gu_ref.dtype)
        act_ref[...] = (_silu_parts(gate)[1] * up).astype(act_ref.dtype)

    return _pcall(
        body, name=name, grid=(s // tm, nb),
        in_specs=[pl.BlockSpec((tm, d), lambda i, j: (i, 0)), pl.BlockSpec((d, tn), lambda i, j: (0, j)),
                  pl.BlockSpec((d, tn), lambda i, j: (0, nb + j))],
        out_specs=[pl.BlockSpec((2, tm, tn), lambda i, j: (0, i, j)), pl.BlockSpec((tm, tn), lambda i, j: (i, j))],
        out_shape=[jax.ShapeDtypeStruct((2, s, f), BF16), jax.ShapeDtypeStruct((s, f), BF16)],
        compiler_params=_params("parallel", "parallel"),
    )(n, w_gu, w_gu)


def _ffn_dgu(dhb, w_down, gu, name, after=None):
    s, d = dhb.shape
    f = w_down.shape[0]
    tn = _pick(f, (1408, 1024, 512, 256, 128))
    tm = _pick(s, (1024, 512, 256, 128))

    def body(dh_ref, w_ref, gu_ref, *rest):
        o_ref = rest[-1]
        dact = _dot(dh_ref[...], w_ref[...], 1, 1)
        gate, up = gu_ref[0].astype(F32), gu_ref[1].astype(F32)
        sg, silu = _silu_parts(gate)
        o_ref[0] = (dact * up * (sg + silu * (1.0 - sg))).astype(o_ref.dtype)
        o_ref[1] = (dact * silu).astype(o_ref.dtype)

    blk = pl.BlockSpec((2, tm, tn), lambda i, j: (0, i, j))
    return _pcall(
        body, name=name, grid=(s // tm, f // tn),
        in_specs=[pl.BlockSpec((tm, d), lambda i, j: (i, 0)), pl.BlockSpec((tn, d), lambda i, j: (j, 0)), blk]
        + ([] if after is None else [ANY]),
        out_specs=blk, out_shape=jax.ShapeDtypeStruct((2, s, f), BF16), compiler_params=_params("parallel", "parallel"),
    )(dhb, w_down, gu, *([] if after is None else [after]))


def _dgrad_norm(dy, wmat, dh, x, g, name, *, dy_halves=False, copy_scale=None, after=None):
    s, d = dh.shape
    k = wmat.shape[1]
    tk = k if k <= MM_WHOLE_K else _pick(k, (MM_WHOLE_K, 2048, 1024, 512, 256, 128))
    if dy_halves and (k // 2) % tk:
        tk = _pick(k // 2, (1408, 1024, 512, 256, 128))
    tm = _pick(s, (512, 256, 128))
    nk, per = k // tk, (k // 2) // tk if dy_halves else 0
    n_in = 5 + (after is not None)
    n_out = 2 + (copy_scale is not None)

    def body(*refs):
        dy_ref, w_ref, dh_ref, x_ref, g_ref = refs[:5]
        outs, scratch = refs[n_in:n_in + n_out], refs[n_in + n_out:]
        i, kk = pl.program_id(0), pl.program_id(1)
        part = _dot(dy_ref[...], w_ref[...], 1, 1)

        def finish(dn):
            dx, dg = _rms_bwd(dn, x_ref[...], g_ref[...])
            tot = dh_ref[...] + dx
            outs[0][...] = tot
            if copy_scale is not None:
                outs[1][...] = (copy_scale * tot).astype(outs[1].dtype)
            dg = jnp.sum(dg, axis=0, keepdims=True)

            @pl.when(i == 0)
            def _():
                outs[-1][...] = dg

            @pl.when(i > 0)
            def _():
                outs[-1][...] += dg

        if nk == 1:
            finish(part)
        else:
            acc_ref = scratch[0]

            @pl.when(kk == 0)
            def _():
                acc_ref[...] = part

            @pl.when(kk > 0)
            def _():
                acc_ref[...] += part

            @pl.when(kk == nk - 1)
            def _():
                finish(acc_ref[...])

    row = pl.BlockSpec((tm, d), lambda i, kk: (i, 0))
    dy_spec = pl.BlockSpec((None, tm, tk), lambda i, kk: (kk // per, i, kk % per)) if dy_halves else pl.BlockSpec((tm, tk), lambda i, kk: (i, kk))
    in_specs = [dy_spec, pl.BlockSpec((d, tk), lambda i, kk: (0, kk)), row, row, pl.BlockSpec((1, d), lambda i, kk: (0, 0))]
    out_specs = [row] * (n_out - 1) + [pl.BlockSpec((1, d), lambda i, kk: (0, 0))]
    out_shape = [jax.ShapeDtypeStruct((s, d), F32)] + ([] if copy_scale is None else [jax.ShapeDtypeStruct((s, d), BF16)])
    return _pcall(
        body, name=name, grid=(s // tm, nk), in_specs=in_specs + ([] if after is None else [ANY]), out_specs=out_specs,
        out_shape=out_shape + [jax.ShapeDtypeStruct((1, d), F32)], scratch_shapes=[pltpu.VMEM((tm, d), F32)] if nk > 1 else [],
        compiler_params=_params("arbitrary", "arbitrary"),
    )(dy, wmat, dh, x, g, *([] if after is None else [after]))


def _shift_down(p, k):
    if k == 0:
        return p
    rows = lax.broadcasted_iota(jnp.int32, p.shape, 0)
    return jnp.where(rows >= k, pltpu.roll(p, k, 0), 0.0)


def _shift_up(p, k):
    if k == 0:
        return p
    s = p.shape[0]
    rows = lax.broadcasted_iota(jnp.int32, p.shape, 0)
    return jnp.where(rows < s - k, pltpu.roll(p, s - k, 0), 0.0)


def _conv_fwd(proj, conv_w, d, tc, name):
    s = proj.shape[0]
    nb = d // tc

    def body(cb_ref, cc_ref, cx_ref, w_ref, y_ref):
        p = cc_ref[...].astype(F32) * cx_ref[...].astype(F32)
        w = w_ref[...]
        acc = p * w[CONV_K - 1:CONV_K, :]
        for k in range(1, CONV_K):
            acc = acc + _shift_down(p, k) * w[CONV_K - 1 - k:CONV_K - k, :]
        y_ref[...] = (cb_ref[...].astype(F32) * acc).astype(y_ref.dtype)

    col = lambda off: pl.BlockSpec((s, tc), lambda j: (0, off * nb + j))
    return _pcall(
        body, name=name, grid=(nb,), in_specs=[col(0), col(1), col(2), pl.BlockSpec((CONV_K, tc), lambda j: (0, j))],
        out_specs=pl.BlockSpec((s, tc), lambda j: (0, j)), out_shape=jax.ShapeDtypeStruct((s, d), BF16),
        compiler_params=_params("parallel"),
    )(proj, proj, proj, conv_w)


def _conv_bwd(dy, proj, conv_w, d, tc, name):
    s = proj.shape[0]
    nb = d // tc

    def body(dy_ref, cb_ref, cc_ref, cx_ref, w_ref, dcb_ref, dcc_ref, dcx_ref, dw_ref):
        cc, cx = cc_ref[...].astype(F32), cx_ref[...].astype(F32)
        p = cc * cx
        w = w_ref[...]
        dyv = dy_ref[...].astype(F32)
        shifted = [_shift_down(p, CONV_K - 1 - k) for k in range(CONV_K)]
        conv = shifted[0] * w[0:1, :]
        for k in range(1, CONV_K):
            conv = conv + shifted[k] * w[k:k + 1, :]
        dcb_ref[...] = (dyv * conv).astype(dcb_ref.dtype)
        ds = dyv * cb_ref[...].astype(F32)
        dp = ds * w[CONV_K - 1:CONV_K, :]
        for k in range(1, CONV_K):
            dp = dp + _shift_up(ds, k) * w[CONV_K - 1 - k:CONV_K - k, :]
        dcc_ref[...] = (dp * cx).astype(dcc_ref.dtype)
        dcx_ref[...] = (dp * cc).astype(dcx_ref.dtype)
        for k in range(CONV_K):
            dw_ref[k:k + 1, :] = jnp.sum(ds * shifted[k], axis=0, keepdims=True)

    col = lambda off: pl.BlockSpec((s, tc), lambda j: (0, off * nb + j))
    blk = pl.BlockSpec((s, tc), lambda j: (0, j))
    wblk = pl.BlockSpec((CONV_K, tc), lambda j: (0, j))
    act = jax.ShapeDtypeStruct((s, d), BF16)
    return _pcall(
        body, name=name, grid=(nb,), in_specs=[blk, col(0), col(1), col(2), wblk],
        out_specs=[blk, blk, blk, wblk], out_shape=[act, act, act, jax.ShapeDtypeStruct((CONV_K, d), F32)],
        compiler_params=_params("parallel"),
    )(dy, proj, proj, proj, conv_w)


def _sb_tile(q, kj, scale, carry, tri, mask):
    z = _dot(q, kj, 1, 1) * scale
    lsz = jnp.minimum(z, 0.0) - jnp.log(1.0 + jnp.exp(-jnp.abs(z)))
    l1m = lsz - z
    if mask is not None:
        l1m = jnp.where(mask, l1m, 0.0)
    l1b = l1m.astype(BF16)
    a = jnp.exp(lsz + (carry + _dot(l1b, tri, 1, 0)))
    if mask is not None:
        a = jnp.where(mask, a, 0.0)
    return lsz, l1b, a.astype(BF16)


def _add_rows(x, upd, r0):
    return x + upd if r0 == 0 else jnp.concatenate([x[:r0], x[r0:] + upd], axis=0)


def _sb_masks(tq, tk):
    row = lax.broadcasted_iota(jnp.int32, (tq, tk), 0)
    col = lax.broadcasted_iota(jnp.int32, (tq, tk), 1)
    masks = [col + dj * tk < row for dj in range(tq // tk)]
    r2 = lax.broadcasted_iota(jnp.int32, (tk, tk), 0)
    c2 = lax.broadcasted_iota(jnp.int32, (tk, tk), 1)
    return masks, (r2 > c2).astype(BF16), (r2 < c2).astype(BF16)


def _sb_fwd(proj, heads, col0, tq, tk, name):
    s = proj.shape[0]
    dh = SB_HEAD_DIM
    nq, nd, nkt = s // tq, tq // tk, s // tk
    scale = dh ** -0.5

    def body(q_ref, k_ref, v_ref, o_ref, a_ref, b_ref):
        i = pl.program_id(1)
        q = q_ref[...]
        masks, tri_right, _ = _sb_masks(tq, tk)

        def tile(j, carry, acc, mask, r0=0):
            start = pl.multiple_of(j * tk, tk)
            kj = k_ref[pl.ds(start, tk), :]
            vj = v_ref[pl.ds(start, tk), :]
            lsz, l1b, ab = _sb_tile(q[r0:], kj, scale, carry[r0:], tri_right, None if mask is None else mask[r0:])
            a_ref[j, r0:, :] = ab
            b_ref[j, r0:, :] = jnp.exp(lsz).astype(b_ref.dtype)
            if r0:
                a_ref[j, :r0, :] = jnp.zeros((r0, tk), a_ref.dtype)
                b_ref[j, :r0, :] = jnp.zeros((r0, tk), b_ref.dtype)
            return (_add_rows(carry, jnp.sum(l1b.astype(F32), axis=1, keepdims=True), r0),
                    _add_rows(acc, _dot(ab, vj, 1, 0), r0))

        state = (jnp.zeros((tq, 1), F32), jnp.zeros((tq, dh), F32))
        for dj in reversed(range(nd)):
            state = tile(i * nd + dj, *state, masks[dj], dj * tk)
        def left_block(t, st):
            for dj in reversed(range(nd)):
                st = tile((i - 1 - t) * nd + dj, st[0], st[1], None)
            return st

        state = lax.fori_loop(0, i, left_block, state)
        o_ref[...] = state[1]

    qspec = pl.BlockSpec((tq, dh), lambda h, i: (i, col0[0] + h))
    kspec = pl.BlockSpec((s, dh), lambda h, i: (0, col0[1] + h))
    vspec = pl.BlockSpec((s, dh), lambda h, i: (0, col0[2] + h))
    saved = pl.BlockSpec((None, nkt, tq, tk), lambda h, i: (h, 0, i, 0))
    saved_shape = jax.ShapeDtypeStruct((heads, nkt, s, tk), BF16)
    return _pcall(
        body, name=name, grid=(heads, nq), in_specs=[qspec, kspec, vspec],
        out_specs=[pl.BlockSpec((tq, dh), lambda h, i: (i, h)), saved, saved],
        out_shape=[jax.ShapeDtypeStruct((s, heads * dh), F32), saved_shape, saved_shape],
        compiler_params=_params("parallel", "parallel"),
    )(proj, proj, proj)


SB_BWD_HEADS = 2


def _sb_bwd(proj, o, a_all, beta_all, do, heads, col0, tq, tk, name):
    s = proj.shape[0]
    dh = SB_HEAD_DIM
    nq, nd, nkt = s // tq, tq // tk, s // tk
    scale = dh ** -0.5
    hb = SB_BWD_HEADS if heads % SB_BWD_HEADS == 0 and all(c % SB_BWD_HEADS == 0 for c in col0) else 1
    wide = hb * dh

    def body(q_ref, k_ref, v_ref, o_ref, a_ref, b_ref, do_ref, dq_ref, dk_ref, dv_ref, dk_acc, dv_acc):
        i = pl.program_id(1)

        @pl.when(i == 0)
        def _():
            dk_acc[...] = jnp.zeros_like(dk_acc)
            dv_acc[...] = jnp.zeros_like(dv_acc)

        lanes = [slice(hh * dh, (hh + 1) * dh) for hh in range(hb)]
        q = [q_ref[:, ln] for ln in lanes]
        dob = [do_ref[:, ln].astype(BF16) for ln in lanes]
        delta = [jnp.sum(dob[hh].astype(F32) * o_ref[:, lanes[hh]], axis=1, keepdims=True) for hh in range(hb)]
        masks, _, tri_left = _sb_masks(tq, tk)

        def tile(hh, j, carry_g, dq, mask):
            start = pl.multiple_of(j * tk, tk)
            kj = k_ref[pl.ds(start, tk), lanes[hh]]
            vj = v_ref[pl.ds(start, tk), lanes[hh]]
            ab = a_ref[hh, j]
            g = _dot(dob[hh], vj, 1, 1) * ab.astype(F32)
            carry_g = carry_g + jnp.sum(g, axis=1, keepdims=True)
            left = (delta[hh] - carry_g) + _dot(g.astype(BF16), tri_left, 1, 0)
            dz = g - b_ref[hh, j].astype(F32) * (g + left)
            if mask is not None:
                dz = jnp.where(mask, dz, 0.0)
            dzb = dz.astype(BF16)
            dk_acc[pl.ds(start, tk), lanes[hh]] += _dot(dzb, q[hh], 0, 0)
            dv_acc[pl.ds(start, tk), lanes[hh]] += _dot(ab, dob[hh], 0, 0)
            return carry_g, dq + _dot(dzb, kj, 1, 0)

        def block(jb, st, use_masks):
            st = list(st)
            for dj in reversed(range(nd)):
                for hh in range(hb):
                    st[hh] = tile(hh, jb * nd + dj, *st[hh], masks[dj] if use_masks else None)
            return tuple(st)

        state = block(i, tuple((jnp.zeros((tq, 1), F32), jnp.zeros((tq, dh), F32)) for _ in range(hb)), True)
        state = lax.fori_loop(0, i, lambda t, st: block(i - 1 - t, st, False), state)
        for hh in range(hb):
            dq_ref[:, lanes[hh]] = (state[hh][1] * scale).astype(dq_ref.dtype)

        @pl.when(i == nq - 1)
        def _():
            dk_ref[...] = (dk_acc[...] * scale).astype(dk_ref.dtype)
            dv_ref[...] = dv_acc[...].astype(dv_ref.dtype)

    qspec = pl.BlockSpec((tq, wide), lambda h, i: (i, col0[0] // hb + h))
    kspec = pl.BlockSpec((s, wide), lambda h, i: (0, col0[1] // hb + h))
    vspec = pl.BlockSpec((s, wide), lambda h, i: (0, col0[2] // hb + h))
    blk = pl.BlockSpec((tq, wide), lambda h, i: (i, h))
    full = pl.BlockSpec((s, wide), lambda h, i: (0, h))
    saved = pl.BlockSpec((hb, nkt, tq, tk), lambda h, i: (h, 0, i, 0))
    act = jax.ShapeDtypeStruct((s, heads * dh), BF16)
    return _pcall(
        body, name=name, grid=(heads // hb, nq), in_specs=[qspec, kspec, vspec, blk, saved, saved, blk],
        out_specs=[blk, full, full], out_shape=[act, act, act],
        scratch_shapes=[pltpu.VMEM((s, wide), F32), pltpu.VMEM((s, wide), F32)],
        compiler_params=_params("parallel", "arbitrary"),
    )(proj, proj, proj, o, a_all, beta_all, do)


def _xattn_probs(q, k, scale):
    sc = _dot(q, k, 1, 1) * scale
    e = jnp.exp(sc - jnp.max(sc, axis=1, keepdims=True))
    return e / jnp.sum(e, axis=1, keepdims=True)


def _xattn_fwd(qc, kv, tq, name):
    s, d = qc.shape
    m = kv.shape[0]
    dh = d // X_HEADS
    scale = dh ** -0.5

    def body(q_ref, k_ref, v_ref, o_ref):
        p = _xattn_probs(q_ref[...], k_ref[...], scale)
        o_ref[...] = _dot(p.astype(BF16), v_ref[...], 1, 0).astype(o_ref.dtype)

    blk = pl.BlockSpec((tq, dh), lambda h, i: (i, h))
    return _pcall(
        body, name=name, grid=(X_HEADS, s // tq),
        in_specs=[blk, pl.BlockSpec((m, dh), lambda h, i: (0, h)), pl.BlockSpec((m, dh), lambda h, i: (0, X_HEADS + h))],
        out_specs=blk, out_shape=jax.ShapeDtypeStruct((s, d), BF16), compiler_params=_params("parallel", "parallel"),
    )(qc, kv, kv)


def _xattn_bwd(qc, kv, do, tq, name):
    s, d = qc.shape
    m = kv.shape[0]
    dh = d // X_HEADS
    scale = dh ** -0.5
    nq = s // tq

    def body(q_ref, k_ref, v_ref, do_ref, dq_ref, dk_ref, dv_ref, dk_acc, dv_acc):
        i = pl.program_id(1)
        q, k, v = q_ref[...], k_ref[...], v_ref[...]
        dob = do_ref[...].astype(BF16)
        p = _xattn_probs(q, k, scale)
        pb = p.astype(BF16)
        dp = _dot(dob, v, 1, 1)
        ds = pb.astype(F32) * (dp - jnp.sum(dp * pb.astype(F32), axis=1, keepdims=True))
        dsb = (ds * scale).astype(BF16)
        dq_ref[...] = _dot(dsb, k, 1, 0).astype(dq_ref.dtype)
        dk_part = _dot(dsb, q, 0, 0)
        dv_part = _dot(pb, dob, 0, 0)

        @pl.when(i == 0)
        def _():
            dk_acc[...] = dk_part
            dv_acc[...] = dv_part

        @pl.when(i > 0)
        def _():
            dk_acc[...] += dk_part
            dv_acc[...] += dv_part

        @pl.when(i == nq - 1)
        def _():
            dk_ref[...] = dk_acc[...].astype(dk_ref.dtype)
            dv_ref[...] = dv_acc[...].astype(dv_ref.dtype)

    blk = pl.BlockSpec((tq, dh), lambda h, i: (i, h))
    kblk = pl.BlockSpec((m, dh), lambda h, i: (0, h))
    return _pcall(
        body, name=name, grid=(X_HEADS, nq),
        in_specs=[blk, kblk, pl.BlockSpec((m, dh), lambda h, i: (0, X_HEADS + h)), blk],
        out_specs=[blk, kblk, kblk],
        out_shape=[jax.ShapeDtypeStruct((s, d), BF16), jax.ShapeDtypeStruct((m, d), BF16), jax.ShapeDtypeStruct((m, d), BF16)],
        scratch_shapes=[pltpu.VMEM((m, dh), F32), pltpu.VMEM((m, dh), F32)],
        compiler_params=_params("parallel", "arbitrary"),
    )(qc, kv, kv, do)


def _down_loss(act, w_down, h, tgt, g, name):
    s, f = act.shape
    d = w_down.shape[1]
    tm = _pick(s, (512, 256, 128))

    def body(a_ref, w_ref, h_ref, t_ref, g_ref, dh_ref, dhb_ref, dg_ref, loss_ref):
        xh, r = _xhat(h_ref[...] + 0.5 * _dot(a_ref[...], w_ref[...], 1, 0))
        gv = g_ref[...]
        err = xh * gv - t_ref[...]
        dy = err * (1.0 / d)
        dxh = dy * gv
        dx = r * (dxh - xh * jnp.mean(dxh * xh, axis=-1, keepdims=True))
        dh_ref[...] = dx
        dhb_ref[...] = (0.5 * dx).astype(dhb_ref.dtype)
        dg = jnp.sum(dy * xh, axis=0, keepdims=True)
        loss = jnp.broadcast_to(jnp.sum(0.5 * jnp.mean(err * err, axis=-1, keepdims=True), axis=0, keepdims=True), (1, LANES))

        @pl.when(pl.program_id(0) == 0)
        def _():
            dg_ref[...] = dg
            loss_ref[...] = loss

        @pl.when(pl.program_id(0) > 0)
        def _():
            dg_ref[...] += dg
            loss_ref[...] += loss

    row = pl.BlockSpec((tm, d), lambda i: (i, 0))
    once = lambda shape: pl.BlockSpec(shape, lambda i: (0, 0))
    return _pcall(
        body, name=name, grid=(s // tm,),
        in_specs=[pl.BlockSpec((tm, f), lambda i: (i, 0)), once((f, d)), row, row, once((1, d))],
        out_specs=[row, row, once((1, d)), once((1, LANES))],
        out_shape=[jax.ShapeDtypeStruct((s, d), F32), jax.ShapeDtypeStruct((s, d), BF16), jax.ShapeDtypeStruct((1, d), F32),
                   jax.ShapeDtypeStruct((1, LANES), F32)],
        compiler_params=_params("arbitrary"),
    )(act, w_down, h, tgt, g)


def _mix_merge(y_conv, y_sb, w_conv_out, w_attn_out, proj, gate_blocks, b_conv, b_sb, name):
    s, d = y_conv.shape
    tm, tn = _pick(s, (1024, 512, 256, 128)), _pick(d, (512, 256, 128))
    nb = d // tn

    def body(yc_ref, ys_ref, wc_ref, ws_ref, gc_ref, gs_ref, bc_ref, bs_ref, ac_ref, as_ref, m_ref):
        ac = _dot(yc_ref[...].astype(BF16), wc_ref[...], 1, 0)
        asb = _dot(ys_ref[...].astype(BF16), ws_ref[...], 1, 0)
        gc = _sigmoid(gc_ref[...].astype(F32) + bc_ref[...])
        gs = _sigmoid(gs_ref[...].astype(F32) + bs_ref[...])
        ac_ref[...] = ac.astype(ac_ref.dtype)
        as_ref[...] = asb.astype(as_ref.dtype)
        m_ref[...] = (gc * ac + gs * asb).astype(m_ref.dtype)

    rows = pl.BlockSpec((tm, d), lambda i, j: (i, 0))
    wcol = pl.BlockSpec((d, tn), lambda i, j: (0, j))
    bias = pl.BlockSpec((1, tn), lambda i, j: (0, j))
    gate = lambda blk: pl.BlockSpec((tm, tn), lambda i, j: (i, blk * nb + j))
    out = pl.BlockSpec((tm, tn), lambda i, j: (i, j))
    act = jax.ShapeDtypeStruct((s, d), BF16)
    return _pcall(
        body, name=name, grid=(s // tm, nb),
        in_specs=[rows, rows, wcol, wcol, gate(gate_blocks[0]), gate(gate_blocks[1]), bias, bias],
        out_specs=[out, out, out], out_shape=[act, act, act], compiler_params=_params("parallel", "parallel"),
    )(y_conv, y_sb, w_conv_out, w_attn_out, proj, proj, b_conv, b_sb)


def _mix_dmerge(dh, w_o, a_conv, a_sb, proj, gate_blocks, b_conv, b_sb, name):
    s, d = a_conv.shape
    tm, tn = _pick(s, (1024, 512, 256, 128)), _pick(d, (512, 256, 128))
    nb = d // tn

    def body(dh_ref, w_ref, ac_ref, as_ref, gc_ref, gs_ref, bc_ref, bs_ref, dac_ref, das_ref, dgc_ref, dgs_ref, dbc_ref, dbs_ref):
        dm = _dot(dh_ref[...], w_ref[...], 1, 1)
        gc = _sigmoid(gc_ref[...].astype(F32) + bc_ref[...])
        gs = _sigmoid(gs_ref[...].astype(F32) + bs_ref[...])
        dgc = dm * ac_ref[...].astype(F32) * gc * (1.0 - gc)
        dgs = dm * as_ref[...].astype(F32) * gs * (1.0 - gs)
        dac_ref[...] = (dm * gc).astype(dac_ref.dtype)
        das_ref[...] = (dm * gs).astype(das_ref.dtype)
        dgc_ref[...] = dgc.astype(dgc_ref.dtype)
        dgs_ref[...] = dgs.astype(dgs_ref.dtype)
        sums = jnp.sum(dgc, axis=0, keepdims=True), jnp.sum(dgs, axis=0, keepdims=True)

        @pl.when(pl.program_id(1) == 0)
        def _():
            dbc_ref[...], dbs_ref[...] = sums

        @pl.when(pl.program_id(1) > 0)
        def _():
            dbc_ref[...] += sums[0]
            dbs_ref[...] += sums[1]

    tile = pl.BlockSpec((tm, tn), lambda j, i: (i, j))
    bias = pl.BlockSpec((1, tn), lambda j, i: (0, j))
    gate = lambda blk: pl.BlockSpec((tm, tn), lambda j, i: (i, blk * nb + j))
    act = jax.ShapeDtypeStruct((s, d), BF16)
    vec = jax.ShapeDtypeStruct((1, d), F32)
    return _pcall(
        body, name=name, grid=(nb, s // tm),
        in_specs=[pl.BlockSpec((tm, d), lambda j, i: (i, 0)), pl.BlockSpec((tn, d), lambda j, i: (j, 0)), tile, tile,
                  gate(gate_blocks[0]), gate(gate_blocks[1]), bias, bias],
        out_specs=[tile, tile, tile, tile, bias, bias], out_shape=[act, act, act, act, vec, vec],
        compiler_params=_params("parallel", "arbitrary"),
    )(dh, w_o, a_conv, a_sb, proj, proj, b_conv, b_sb)


def _local_step(x, mem, tgt, w, fetch=None, prefetch=None, emit=None, tick=None, after=None):
    fetch = fetch or (lambda name, after: {})
    prefetch = prefetch or (lambda name, after: None)
    emit = emit or (lambda group, g: None)
    tick = tick or (lambda group, after: None)
    w = dict(w)
    s, d = x.shape
    heads = d // SB_HEAD_DIM
    tm = _pick(s, (1024, 512, 256, 128))
    tq = _pick(s, (2048, 1024, 512, 256, 128))
    sb_tq, sb_tk = _pick(s, (512, 256, 128)), _pick(s, (256, 128))
    tc = _pick(d, (256, 128))
    g = {}

    def wt(name, after):
        if name not in w:
            w.update(fetch(name, after))
        return w[name]

    def ffn_fwd(h, n, wgu, wdown, tag, next_g=None):
        gu, act = _ffn_up(n, wt(wgu, n), tag + "_gu")
        prefetch(wdown, gu)
        return gu, act, _mm(act, wt(wdown, act), name=tag + "_down", out_dtype=F32, res=h, alpha=0.5, norm_g=next_g)

    def ffn_bwd(dh, dhb, h, saved, gname, wgu, wdown, tag, copy_scale=None, after=None):
        n, gu, act = saved
        g[wdown] = _mm(act, dhb, ta=True, name=tag + "_dwdown", after=after)
        dgu = _ffn_dgu(dhb, w[wdown], gu, tag + "_dgu", after=emit(tag + "_down", g))
        g[wgu] = _mm(n, dgu, ta=True, b_halves=True, name=tag + "_dwgu", after=tick(tag + "_down", dgu))
        *dh_in, g[gname] = _dgrad_norm(dgu, w[wgu], dh, h, w[gname], tag + "_dn", dy_halves=True, copy_scale=copy_scale,
                                       after=emit(tag, g))
        return dh_in, tick(tag, dh_in[0])

    n1 = _rms_fwd(x, w["g_ffn1"], "ffn1_norm", tm, after=after)
    gu1, act1, (h1, u) = ffn_fwd(x, n1, "w_ffn1_gu", "w_ffn1_down", "ffn1", w["g_mix"])
    prefetch("w_in", h1)
    proj = _mm(u, wt("w_in", u), name="mix_in")
    prefetch("w_conv_out", proj)
    nd = d // SB_HEAD_DIM
    y_conv = _conv_fwd(proj, w["conv_w"], d, tc, "conv_fwd")
    sb_cols = (3 * nd, 4 * nd, 5 * nd)
    y_sb, sb_a, sb_beta = _sb_fwd(proj, heads, sb_cols, _pick(s, (2 * sb_tq, sb_tq)), sb_tk, "sb_fwd")
    prefetch("w_cq", y_sb)
    b_conv, b_sb = w["b_gate"][:, :d], w["b_gate"][:, d:]
    a_conv, a_sb, merged = _mix_merge(y_conv, y_sb, wt("w_conv_out", y_conv), wt("w_attn_out", y_sb), proj, (6, 7), b_conv, b_sb,
                                      "mix_merge")
    prefetch("w_ffn2_gu", merged)
    h2, hn = _mm(merged, wt("w_o", merged), name="mix_out", out_dtype=F32, res=h1, norm_g=w["g_cross"])
    mn = _rms_fwd(mem, w["g_mem"], "mem_norm", _pick(mem.shape[0], (256, 128)))
    qc = _mm(hn, wt("w_cq", hn), name="cross_q")
    kv = _mm(mn, wt("w_ckv", mn), name="cross_kv")
    oc = _xattn_fwd(qc, kv, tq, "xattn_fwd")
    h3, n2 = _mm(oc, wt("w_co", oc), name="cross_out", out_dtype=F32, res=h2, norm_g=w["g_ffn2"])
    gu2, act2 = _ffn_up(n2, wt("w_ffn2_gu", n2), "ffn2_gu")

    dh4, dh4b, g["g_final"], loss_lanes = _down_loss(act2, wt("w_ffn2_down", act2), h3, tgt, w["g_final"], "ffn2_down_loss")

    (dh3, dh3b), tok = ffn_bwd(dh4, dh4b, h3, (n2, gu2, act2), "g_ffn2", "w_ffn2_gu", "w_ffn2_down", "ffn2", copy_scale=1.0)
    g["w_co"] = _mm(oc, dh3b, ta=True, name="cross_dwco", after=tok)
    doc = _mm(dh3b, w["w_co"], tb=True, name="cross_doc")
    dqc, dk, dv = _xattn_bwd(qc, kv, doc, tq, "xattn_bwd")
    dkv = jnp.concatenate([dk, dv], axis=1)
    g["w_cq"] = _mm(hn, dqc, ta=True, name="cross_dwcq")
    g["w_ckv"] = _mm(mn, dkv, ta=True, name="cross_dwckv")
    dmn = _mm(dkv, w["w_ckv"], tb=True, name="cross_dmn", out_dtype=F32)
    g["g_mem"] = _rowcall(lambda dy, xb: dy * _xhat(xb)[0], [_whole(dmn), _whole(mem)], [], [], [d],
                          tm=_pick(mem.shape[0], (256, 128)), name="mem_dnorm")[0]
    dh2, dh2b, g["g_cross"] = _dgrad_norm(dqc, w["w_cq"], dh3, h2, w["g_cross"], "cross_dhn", copy_scale=1.0, after=emit("cross", g))

    g["w_o"] = _mm(merged, dh2b, ta=True, name="mix_dwo", after=tick("cross", dh2))
    da_conv, da_sb, dgc, dgs, db_conv, db_sb = _mix_dmerge(dh2b, w["w_o"], a_conv, a_sb, proj, (6, 7), b_conv, b_sb, "mix_dmerge")
    g["b_gate"] = jnp.concatenate([db_conv, db_sb], axis=1)
    g["w_conv_out"] = _mm(y_conv, da_conv, ta=True, name="conv_dwout")
    g["w_attn_out"] = _mm(y_sb, da_sb, ta=True, name="attn_dwout")
    dy_conv = _mm(da_conv, w["w_conv_out"], tb=True, name="conv_dy")
    dy_sb = _mm(da_sb, w["w_attn_out"], tb=True, name="attn_dy")
    dcb, dcc, dcx, g["conv_w"] = _conv_bwd(dy_conv, proj, w["conv_w"], d, tc, "conv_bwd")
    dq, dk_sb, dv_sb = _sb_bwd(proj, y_sb, sb_a, sb_beta, dy_sb, heads, sb_cols, sb_tq, sb_tk, "sb_bwd")
    dproj = jnp.concatenate([dcb, dcc, dcx, dq, dk_sb, dv_sb, dgc, dgs], axis=1)
    g["w_in"] = _mm(u, dproj, ta=True, name="mix_dwin")
    dh1, dh1b, g["g_mix"] = _dgrad_norm(dproj, w["w_in"], dh2, h1, w["g_mix"], "mix_du", copy_scale=0.5, after=emit("mix", g))
    (dx,), tok = ffn_bwd(dh1, dh1b, x, (n1, gu1, act1), "g_ffn1", "w_ffn1_gu", "w_ffn1_down", "ffn1", after=tick("mix", dh1))
    return loss_lanes, dx, g, tok


MATS = (("w_ffn1_gu", "col"), ("w_ffn1_down", "row"), ("w_in", "col"), ("w_conv_out", "row"), ("w_attn_out", "row"),
        ("w_o", "row"), ("w_cq", "row"), ("w_ckv", "col"), ("w_co", "row"), ("w_ffn2_gu", "col"), ("w_ffn2_down", "row"))
VECS = ("g_ffn1", "g_mix", "g_cross", "g_mem", "g_ffn2", "g_final")
WEIGHTS = ("g_ffn1", "w_ffn1_gu", "w_ffn1_down", "g_mix", "w_in", "b_gate", "conv_w", "w_conv_out", "w_attn_out", "w_o",
           "g_cross", "g_mem", "w_cq", "w_ckv", "w_co", "g_ffn2", "w_ffn2_gu", "w_ffn2_down", "g_final")
CONV_ROWS = 16


def _full_shape(kind, r, c):
    return (r, N_CHIPS * c) if kind == "col" else (N_CHIPS * r, c)


def _piece(ref, kind, r, c, chip, half):
    hr = r // 2
    if kind == "col":
        return ref.at[pl.ds(pl.multiple_of(half * hr, math.gcd(hr, 16)), hr), pl.ds(pl.multiple_of(chip * c, LANES), c)]
    return ref.at[pl.ds(pl.multiple_of(chip * r + half * hr, math.gcd(hr, 16)), hr), :]


def _shard_of(ref, kind, r, c, chip):
    if kind == "col":
        return ref.at[:, pl.ds(pl.multiple_of(chip * c, LANES), c)]
    return ref.at[pl.ds(pl.multiple_of(chip * r, 16), r), :]


def _place():
    x, y, c = lax.axis_index("x"), lax.axis_index("y"), lax.axis_index("c")
    others = [(1 - x, y), (x, 1 - y), (1 - x, 1 - y)]
    return x, y, c, 2 * x + y, others


def _remote(src, dst, send_sem, recv_sem, to):
    return pltpu.make_async_remote_copy(src_ref=src, dst_ref=dst, send_sem=send_sem, recv_sem=recv_sem,
                                        device_id=to, device_id_type=MESH)


HBM = pl.BlockSpec(memory_space=pltpu.HBM)
SEM = pl.BlockSpec(memory_space=pltpu.SEMAPHORE)
EFFECT = pltpu.SideEffectType.DATAFLOW_SIDE_EFFECTING
TOKEN = (8, LANES)


def _split_start(name, plan, n_copies, srcs, lands, after=None):
    ns, nl = len(srcs), len(lands)
    n_in = ns + nl + (after is not None)

    def body(*refs):
        outs = refs[n_in:]
        sends, _ = plan(refs[:ns], refs[ns:ns + nl], outs[0], outs[1])
        for cp in sends:
            cp.start()
        outs[-1][...] = jnp.zeros(TOKEN, F32)

    held = [pltpu.HBM(a.shape, a.dtype) for a in (*srcs, *lands)]
    dma = pltpu.SemaphoreType.DMA((n_copies,))
    ins = [pltpu.with_memory_space_constraint(a, pltpu.HBM) for a in (*srcs, *lands)]
    outs = _pcall(
        body, name=name, in_specs=[HBM] * (ns + nl) + ([] if after is None else [ANY]),
        out_specs=(SEM, SEM, *[HBM] * (ns + nl), pl.BlockSpec(memory_space=pltpu.VMEM)),
        out_shape=(dma, dma, *held, jax.ShapeDtypeStruct(TOKEN, F32)),
        input_output_aliases={i: 2 + i for i in range(ns + nl)},
        compiler_params=pltpu.CompilerParams(has_side_effects=EFFECT),
    )(*ins, *([] if after is None else [after]))
    return outs[0], outs[1], list(outs[2:2 + ns]), list(outs[2 + ns:2 + ns + nl]), outs[-1]


def _split_wait(name, plan, send_sems, recv_sems, srcs, lands, after):
    ns, nl = len(srcs), len(lands)

    def body(*refs):
        sends, recvs = plan(refs[:ns], refs[ns:ns + nl], refs[ns + nl], refs[ns + nl + 1])
        for cp in sends:
            cp.wait_send()
        for cp in recvs:
            cp.wait_recv()

    outs = _pcall(
        body, name=name, in_specs=[HBM] * (ns + nl) + [SEM, SEM, ANY], out_specs=[HBM] * (ns + nl),
        out_shape=[pltpu.HBM(a.shape, a.dtype) for a in (*srcs, *lands)],
        input_output_aliases={i: i for i in range(ns + nl)},
        compiler_params=pltpu.CompilerParams(has_side_effects=EFFECT),
    )(*srcs, *lands, send_sems, recv_sems, after)
    return list(outs[:ns]), list(outs[ns:])


def _gather_plan(dims):
    def plan(shard_refs, full_refs, ss, rs):
        x, y, c, me, others = _place()
        sends, recvs = [], []
        for wi, (kind, r, cw) in enumerate(dims):
            half = shard_refs[wi].at[pl.ds(pl.multiple_of(c * (r // 2), math.gcd(r // 2, 16)), r // 2), :]
            for k, (ox, oy) in enumerate(others):
                sem = 4 * wi + k
                sends.append(_remote(half, _piece(full_refs[wi], kind, r, cw, me, c), ss.at[sem], rs.at[sem], (ox, oy, c)))
                recvs.append(_remote(half, _piece(full_refs[wi], kind, r, cw, 2 * ox + oy, c), ss.at[sem], rs.at[sem], (x, y, c)))
            sem = 4 * wi + 3
            own = _remote(shard_refs[wi], _shard_of(full_refs[wi], kind, r, cw, me), ss.at[sem], rs.at[sem], (x, y, 1 - c))
            sends.append(own)
            recvs.append(own)
        return sends, recvs

    return plan


def _forward_plan(dims):
    def plan(_, full_refs, ss, rs):
        x, y, c, _, others = _place()
        sends, recvs = [], []
        for wi, (kind, r, cw) in enumerate(dims):
            for k, (ox, oy) in enumerate(others):
                sem = 3 * wi + k
                mine = _piece(full_refs[wi], kind, r, cw, 2 * ox + oy, c)
                theirs = _piece(full_refs[wi], kind, r, cw, 2 * ox + oy, 1 - c)
                sends.append(_remote(mine, mine, ss.at[sem], rs.at[sem], (x, y, 1 - c)))
                recvs.append(_remote(theirs, theirs, ss.at[sem], rs.at[sem], (x, y, 1 - c)))
        return sends, recvs

    return plan


def _rs_cores_plan(dims):
    def plan(g_refs, land_refs, ss, rs):
        x, y, c, _, _ = _place()
        sends, recvs = [], []
        for wi, dm in enumerate(dims):
            for chip in range(N_CHIPS):
                sem = N_CHIPS * wi + chip
                sends.append(_remote(_piece(g_refs[wi], *dm, chip, 1 - c), land_refs[wi].at[chip], ss.at[sem], rs.at[sem], (x, y, 1 - c)))
                recvs.append(_remote(_piece(g_refs[wi], *dm, chip, c), land_refs[wi].at[chip], ss.at[sem], rs.at[sem], (x, y, 1 - c)))
        return sends, recvs

    return plan


def _share_plan(nw):
    def plan(_, buf_refs, ss, rs):
        x, y, c, _, _ = _place()
        sends = [_remote(buf_refs[wi].at[c], buf_refs[wi].at[c], ss.at[wi], rs.at[wi], (x, y, 1 - c)) for wi in range(nw)]
        recvs = [_remote(buf_refs[wi].at[1 - c], buf_refs[wi].at[1 - c], ss.at[wi], rs.at[wi], (x, y, 1 - c)) for wi in range(nw)]
        return sends, recvs

    return plan


def _small_plan():
    def plan(_, buf_refs, ss, rs):
        x, y, c = lax.axis_index("x"), lax.axis_index("y"), lax.axis_index("c")
        buf = buf_refs[0]
        sends, recvs = [], []
        for rel in range(1, N_DEV):
            peer = (x ^ (rel >> 2 & 1), y ^ (rel >> 1 & 1), c ^ (rel & 1))
            sends.append(_remote(buf.at[0], buf.at[rel], ss.at[rel - 1], rs.at[rel - 1], peer))
            recvs.append(_remote(buf.at[0], buf.at[rel], ss.at[rel - 1], rs.at[rel - 1], peer))
        return sends, recvs

    return plan


def _sum_small(buf, me, name):
    _, rows, n = buf.shape

    def body(me_ref, b_ref, o_ref):
        tot = b_ref[me_ref[0]]
        for dev in range(1, N_DEV):
            tot = tot + b_ref[dev ^ me_ref[0]]
        o_ref[...] = tot

    return _pcall(
        body, name=name, out_shape=jax.ShapeDtypeStruct((rows, n), F32),
        grid_spec=pltpu.PrefetchScalarGridSpec(
            num_scalar_prefetch=1, grid=(1,), in_specs=[pl.BlockSpec((N_DEV, rows, n), lambda i, m: (0, 0, 0))],
            out_specs=pl.BlockSpec((rows, n), lambda i, m: (0, 0))),
    )(me, buf)


def _rs_chips_plan(nw):
    def plan(p_refs, land_refs, ss, rs):
        x, y, c, me, others = _place()
        sends, recvs = [], []
        for wi in range(nw):
            for k, (ox, oy) in enumerate(others):
                sem = 3 * wi + k
                sends.append(_remote(p_refs[wi].at[2 * ox + oy], land_refs[wi].at[k], ss.at[sem], rs.at[sem], (ox, oy, c)))
                recvs.append(_remote(p_refs[wi].at[me], land_refs[wi].at[k], ss.at[sem], rs.at[sem], (x, y, c)))
        return sends, recvs

    return plan


SUM_BLOCK_BYTES = 4 << 20


def _rows_per_block(n, c, limit_bytes=2 << 20):
    best = None
    for tm in range(16, n + 1, 16):
        if n % tm == 0 and tm * c * 4 <= limit_bytes:
            best = tm
    return best or n


def _sum_cores(grad, got, kind, place, name):
    _, hr, cw = got.shape
    tm = _rows_per_block(hr, cw, SUM_BLOCK_BYTES)
    nb = hr // tm

    def body(place_ref, g_ref, t_ref, o_ref):
        o_ref[...] = (g_ref[...].astype(F32) + t_ref[...].astype(F32)).astype(o_ref.dtype)

    if kind == "col":
        g_spec = pl.BlockSpec((tm, cw), lambda j, i, pr: (pr[0] * nb + i, j))
    else:
        g_spec = pl.BlockSpec((tm, cw), lambda j, i, pr: ((2 * j + pr[0]) * nb + i, 0))
    blk = pl.BlockSpec((None, tm, cw), lambda j, i, pr: (j, i, 0))
    return _pcall(
        body, name=name, out_shape=jax.ShapeDtypeStruct(got.shape, BF16),
        grid_spec=pltpu.PrefetchScalarGridSpec(num_scalar_prefetch=1, grid=(N_CHIPS, nb), in_specs=[g_spec, blk], out_specs=blk),
        compiler_params=_params("parallel", "parallel"),
    )(place, grad, got)


def _sum_chips(parts, got, place, name):
    _, n, cw = got.shape
    tm = _rows_per_block(n, cw, SUM_BLOCK_BYTES)

    def body(place_ref, p_ref, g_ref, o_ref):
        tot = p_ref[...].astype(F32)
        for k in range(3):
            tot = tot + g_ref[k].astype(F32)
        o_ref[...] = tot

    return _pcall(
        body, name=name, out_shape=jax.ShapeDtypeStruct((2, n, cw), F32),
        grid_spec=pltpu.PrefetchScalarGridSpec(
            num_scalar_prefetch=1, grid=(n // tm,),
            in_specs=[pl.BlockSpec((None, tm, cw), lambda i, pr: (pr[1], i, 0)), pl.BlockSpec((3, tm, cw), lambda i, pr: (0, i, 0))],
            out_specs=pl.BlockSpec((None, tm, cw), lambda i, pr: (pr[0], i, 0))),
        compiler_params=_params("parallel"),
    )(place, parts, got)


def _adamw(g, w, m, v, name, *, copy_g):
    n, c = g.shape
    c1 = 1.0 - ADAM_B1 ** ADAM_STEP
    c2 = 1.0 - ADAM_B2 ** ADAM_STEP

    def fn(gb, wb, mb, vb):
        m_new = ADAM_B1 * mb + (1.0 - ADAM_B1) * gb
        v_new = ADAM_B2 * vb + (1.0 - ADAM_B2) * (gb * gb)
        delta = -ADAM_LR * ((m_new / c1) / (jnp.sqrt(v_new / c2) + ADAM_EPS) + ADAM_WD * wb)
        return (gb, delta, m_new, v_new) if copy_g else (delta, m_new, v_new)

    tm = _rows_per_block(n, c) if n % 16 == 0 else n
    res = _rowcall(fn, [_whole(g), _whole(w), _whole(m), _whole(v)], [], [(c, F32)] * (3 + copy_g), tm=tm, name=name)
    return res if copy_g else [g, *res]


PACK_ROWS = 16


def _pack_rows(parts, width, name, after=None):
    assert sum(p.shape[0] for p in parts) <= PACK_ROWS

    def body(*refs):
        out_ref = refs[-1]
        out_ref[...] = jnp.zeros_like(out_ref)
        at = 0
        for r in refs[:len(parts)]:
            k, n = r.shape
            if n == width:
                out_ref[at:at + k, :] = r[...]
            else:
                out_ref[at:at + k, :] = jnp.broadcast_to(r[:, :1], (k, width))
            at += k

    vm = pl.BlockSpec(memory_space=pltpu.VMEM)
    return _pcall(body, name=name, in_specs=[vm] * len(parts) + ([] if after is None else [ANY]), out_specs=vm,
                  out_shape=jax.ShapeDtypeStruct((PACK_ROWS, width), F32))(*parts, *([] if after is None else [after]))


def _cast_shard(wm, name, after):
    n, c = wm.shape
    return _rowcall(lambda v: v, [_whole(wm)], [], [(c, BF16)], tm=_rows_per_block(n, c), name=name, after=after)[0]


GATHER_GROUPS = (
    ("w_ffn1_gu", "conv_w"), ("w_ffn1_down",), ("w_in",), ("w_conv_out", "w_attn_out", "w_o"), ("w_cq", "w_ckv", "w_co"),
    ("w_ffn2_gu", "w_ffn2_down"),
)
REDUCE_GROUPS = {
    "ffn2": ("w_ffn2_down", "w_ffn2_gu"),
    "cross": ("w_co", "w_cq", "w_ckv"),
    "mix": ("w_o", "w_conv_out", "w_attn_out", "w_in"),
    "ffn1_down": ("w_ffn1_down",),
    "ffn1": ("w_ffn1_gu",),
}
TAIL_STAGES = (("ffn2", "cross"), ("mix",), ("ffn1_down", "ffn1"))
KIND = dict(MATS)


def _step(x, mem, tgt, wts, m_in, v_in):
    d = x.shape[-1]
    cc = wts["conv_w"].shape[1]
    place = jnp.stack([lax.axis_index("c"), 2 * lax.axis_index("x") + lax.axis_index("y")]).astype(jnp.int32)
    dims = {n: (kind, *wts[n].shape) for n, kind in MATS}
    dims["conv_w"] = ("col", CONV_ROWS, cc)

    w = {n: wts[n].reshape(1, -1) for n in VECS + ("b_gate",)}
    flying, token = {}, None
    for names in GATHER_GROUPS:
        gd = [dims[n] for n in names]
        shards = [jnp.pad(wts[n], ((0, CONV_ROWS - CONV_K), (0, 0))) if n == "conv_w" else _cast_shard(wts[n], "cast_" + n, token)
                  for n in names]
        lands = [lax.empty(_full_shape(*dm), sh.dtype) for dm, sh in zip(gd, shards)]
        plan = _gather_plan(gd)
        ss, rs, srcs, lands, token = _split_start("gather_start_" + names[0], plan, 4 * len(names), shards, lands, token)
        flying.update({n: (names, plan, ss, rs, srcs, lands, gd) for n in names})

    passing = {}

    def prefetch(name, after):
        if name not in passing:
            names, plan, ss, rs, srcs, lands, gd = flying[name]
            _, lands = _split_wait("gather_wait_" + names[0], plan, ss, rs, srcs, lands, after)
            plan = _forward_plan(gd)
            ss, rs, _, lands, _ = _split_start("forward_start_" + names[0], plan, 3 * len(names), [], lands)
            passing.update({n: (names, plan, ss, rs, lands) for n in names})

    def fetch(name, after):
        prefetch(name, after)
        names, plan, ss, rs, lands = passing[name]
        _, lands = _split_wait("forward_wait_" + names[0], plan, ss, rs, [], lands, after)
        return {n: (land[:CONV_K] if n == "conv_w" else land) for n, land in zip(names, lands)}

    swapping, sent = {}, {}

    def emit(tag, g):
        if tag not in REDUCE_GROUPS:
            return None
        names = REDUCE_GROUPS[tag]
        gd = [dims[n] for n in names]
        lands = [lax.empty((N_CHIPS, r // 2, cw), BF16) for (_, r, cw) in gd]
        plan = _rs_cores_plan(gd)
        ss, rs, srcs, lands, tok = _split_start("rs_cores_start_" + tag, plan, N_CHIPS * len(names), [g[n] for n in names], lands)
        swapping[tag] = (plan, ss, rs, srcs, lands)
        return tok

    def tick(tag, after):
        if tag not in REDUCE_GROUPS:
            return None
        names = REDUCE_GROUPS[tag]
        plan, ss, rs, srcs, lands = swapping[tag]
        mine, got = _split_wait("rs_cores_wait_" + tag, plan, ss, rs, srcs, lands, after)
        parts = [_sum_cores(gm, t, KIND[n], place, "sum_cores_" + n) for n, gm, t in zip(names, mine, got)]
        lands = [lax.empty((3, *p.shape[1:]), BF16) for p in parts]
        plan = _rs_chips_plan(len(names))
        ss, rs, srcs, lands, tok = _split_start("rs_chips_start_" + tag, plan, 3 * len(names), parts, lands)
        sent[tag] = (plan, ss, rs, srcs, lands)
        return tok

    loss_lanes, dx, g, last = _local_step(x[0], mem[0], tgt[0], w, fetch, prefetch, emit, tick, token)

    rows = [g[n] for n in VECS] + [g["b_gate"][:, :d], g["b_gate"][:, d:], g["conv_w"], loss_lanes]
    packed = _pack_rows(rows, d, "pack_small", after=last)
    small = jnp.concatenate([packed[None], jnp.zeros((N_DEV - 1, *packed.shape), F32)], axis=0)
    small_plan = _small_plan()
    small_ss, small_rs, _, small, after = _split_start("small_start", small_plan, N_DEV - 1, [], [small])

    grads, out = {}, {}

    def update(n):
        shape = wts[n].shape
        as2d = (lambda a: a.reshape(1, -1)) if len(shape) == 1 else (lambda a: a)
        res = _adamw(grads[n], as2d(wts[n]), as2d(m_in[n]), as2d(v_in[n]), "adamw_" + n, copy_g=n not in KIND)
        return [r.reshape(shape) for r in res]

    def finish(sharing, after):
        tag, names, plan, ss, rs, halves = sharing
        _, both = _split_wait("share_wait_" + tag, plan, ss, rs, [], halves, after)
        for n, b in zip(names, both):
            grads[n] = b.reshape(-1, b.shape[-1])
            out[n] = update(n)
        return out[names[-1]][1]

    sharing = None
    for stage in TAIL_STAGES:
        names, halves = [], []
        for tag in stage:
            plan, ss, rs, srcs, lands = sent[tag]
            parts, landed = _split_wait("rs_chips_wait_" + tag, plan, ss, rs, srcs, lands, after)
            halves += [_sum_chips(p, t, place, "sum_chips_" + n) for n, p, t in zip(REDUCE_GROUPS[tag], parts, landed)]
            names += REDUCE_GROUPS[tag]
        plan = _share_plan(len(names))
        ss, rs, _, halves, after = _split_start("share_start_" + stage[0], plan, len(names), [], halves)
        if sharing is not None:
            after = finish(sharing, after)
        sharing = (stage[0], names, plan, ss, rs, halves)
    after = finish(sharing, after)

    _, small = _split_wait("small_wait", small_plan, small_ss, small_rs, [], small, after)
    me = (4 * lax.axis_index("x") + 2 * lax.axis_index("y") + lax.axis_index("c")).astype(jnp.int32).reshape(1)
    red = _sum_small(small[0], me, "sum_small")
    grads.update({n: red[i:i + 1] for i, n in enumerate(VECS)})
    nv = len(VECS)
    grads["b_gate"] = jnp.concatenate([red[nv:nv + 1], red[nv + 1:nv + 2]], axis=1)
    chip = 2 * lax.axis_index("x") + lax.axis_index("y")
    grads["conv_w"] = lax.dynamic_slice_in_dim(red[nv + 2:nv + 2 + CONV_K], chip * cc, cc, axis=1)
    loss = red[nv + 2 + CONV_K, 0]
    out.update({n: update(n) for n in WEIGHTS if n not in KIND})
    return (loss, dx[None], *[out[n][0] for n in WEIGHTS], *[out[n][1] for n in WEIGHTS],
            *[out[n][2] for n in WEIGHTS], *[out[n][3] for n in WEIGHTS])


def kernel(x, mem, g_ffn1, w_ffn1_gu, w_ffn1_down, g_mix, w_in, b_gate, conv_w, w_conv_out, w_attn_out, w_o, g_cross, g_mem, w_cq, w_ckv, w_co, g_ffn2, w_ffn2_gu, w_ffn2_down, g_final, loss_target, m_g_ffn1, m_w_ffn1_gu, m_w_ffn1_down, m_g_mix, m_w_in, m_b_gate, m_conv_w, m_w_conv_out, m_w_attn_out, m_w_o, m_g_cross, m_g_mem, m_w_cq, m_w_ckv, m_w_co, m_g_ffn2, m_w_ffn2_gu, m_w_ffn2_down, m_g_final, v_g_ffn1, v_w_ffn1_gu, v_w_ffn1_down, v_g_mix, v_w_in, v_b_gate, v_conv_w, v_w_conv_out, v_w_attn_out, v_w_o, v_g_cross, v_g_mem, v_w_cq, v_w_ckv, v_w_co, v_g_ffn2, v_w_ffn2_gu, v_w_ffn2_down, v_g_final):
    given = dict(locals())
    wts = {n: given[n] for n in WEIGHTS}
    m_in = {n: given["m_" + n] for n in WEIGHTS}
    v_in = {n: given["v_" + n] for n in WEIGHTS}
    return _step(x, mem, loss_target, wts, m_in, v_in)
```

```python
import math

import jax
import jax.numpy as jnp
from jax import lax
from jax.experimental import pallas as pl
from jax.experimental.pallas import tpu as pltpu

F32 = jnp.float32
BF16 = jnp.bfloat16
MESH = pl.DeviceIdType.MESH

V7X_VMEM_LIMIT_BYTES = 48 * 1024 * 1024
MM_VMEM_BUDGET_BYTES = 36 * 1024 * 1024
MM_WHOLE_K = 2816
LANES = 128
SB_HEAD_DIM = 128
X_HEADS = 4
CONV_K = 3
RMS_EPS = 1e-6
N_CHIPS = 4
N_DEV = 8
ADAM_LR, ADAM_B1, ADAM_B2, ADAM_EPS, ADAM_WD, ADAM_STEP = 0.001, 0.9, 0.999, 1e-08, 0.01, 10


ANY = pl.BlockSpec(memory_space=pl.ANY)


def _pcall(body, **kw):
    return pl.pallas_call(body, **kw)


def _params(*sem):
    return pltpu.CompilerParams(dimension_semantics=sem, vmem_limit_bytes=V7X_VMEM_LIMIT_BYTES)


def _pick(dim, cands):
    for c in cands:
        if dim % c == 0:
            return c
    return dim


def _dot(a, b, ca, cb):
    return lax.dot_general(a, b, (((ca,), (cb,)), ((), ())), preferred_element_type=F32)


def _mm(a, b, *, name, ta=False, tb=False, out_dtype=BF16, res=None, alpha=1.0, tm=None, tn=None, tk=None, after=None,
        a_halves=False, b_halves=False, norm_g=None):
    assert not (a_halves and ta) and not (b_halves and tb) and not (norm_g is not None and ta)
    if a_halves:
        m, k = a.shape[1], 2 * a.shape[2]
    else:
        m, k = (a.shape[1], a.shape[0]) if ta else a.shape
    if b_halves:
        n = 2 * b.shape[2]
        assert k == b.shape[1]
    else:
        n = b.shape[0] if tb else b.shape[1]
        assert k == (b.shape[1] if tb else b.shape[0]), (a.shape, b.shape, ta, tb)
    if ta:
        tm = tm or _pick(m, (512, 1408, 256, 128))
        tn = tn or _pick(n, (2048, 1024, 512, 256, 128))
        tk = tk or (k if k <= MM_WHOLE_K else _pick(k, (1024, 512, 256, 128)))
    else:
        tk = tk or (k if k <= MM_WHOLE_K else _pick(k, (MM_WHOLE_K, 2048, 1024, 512, 256, 128)))
        tn = tn or (n if norm_g is not None else _pick(n, (512, 1408, 256, 128) if tk == k else (1024, 512, 256, 128)))
        out_bytes = jnp.dtype(out_dtype).itemsize + (0 if res is None else res.dtype.itemsize) + (0 if norm_g is None else 2)
        per_row = 2 * (tk * a.dtype.itemsize + tn * out_bytes)
        per_row += 4 * tn if tk < k else 0
        rows = (MM_VMEM_BUDGET_BYTES - 2 * tk * tn * b.dtype.itemsize) // per_row
        tm = tm or next((c for c in (2048, 1024, 512, 256, 128) if m % c == 0 and c <= rows), m)
    if a_halves:
        tk = min(tk, k // 2) if (k // 2) % min(tk, k // 2) == 0 else _pick(k // 2, (1408, 1024, 512, 256, 128))
    if b_halves:
        tn = tn if (n // 2) % tn == 0 else _pick(n // 2, (2816, 1408, 1024, 512, 256, 128) if ta else (1408, 1024, 512, 256, 128))
    nk = k // tk
    assert m % tm == 0 and n % tn == 0 and k % tk == 0
    a_spec = pl.BlockSpec((tk, tm), lambda i, j, kk: (kk, i)) if ta else pl.BlockSpec((tm, tk), lambda i, j, kk: (i, kk))
    b_spec = pl.BlockSpec((tn, tk), lambda i, j, kk: (j, kk)) if tb else pl.BlockSpec((tk, tn), lambda i, j, kk: (kk, j))
    if a_halves:
        per = (k // 2) // tk
        a_spec = pl.BlockSpec((None, tm, tk), lambda i, j, kk: (kk // per, i, kk % per))
    if b_halves:
        per_n = (n // 2) // tn
        b_spec = pl.BlockSpec((None, tk, tn), lambda i, j, kk: (j // per_n, kk, j % per_n))
    o_spec = pl.BlockSpec((tm, tn), lambda i, j, kk: (i, j))
    ca, cb = (0 if ta else 1), (1 if tb else 0)

    n_in = 2 + (res is not None) + (norm_g is not None) + (after is not None)
    n_out = 1 + (norm_g is not None)

    def body(*refs):
        a_ref, b_ref = refs[:2]
        res_ref = refs[2] if res is not None else None
        g_ref = refs[2 + (res is not None)] if norm_g is not None else None
        o_ref = refs[n_in]
        scratch = refs[n_in + n_out:]

        def finish(acc):
            val = acc if alpha == 1.0 else alpha * acc
            if res_ref is not None:
                val = res_ref[...].astype(F32) + val
            o_ref[...] = val.astype(o_ref.dtype)
            if g_ref is not None:
                refs[n_in + 1][...] = (_xhat(val)[0] * g_ref[...]).astype(BF16)

        part = _dot(a_ref[...].astype(BF16), b_ref[...].astype(BF16), ca, cb)
        if nk == 1:
            finish(part)
        else:
            acc_ref = scratch[0]
            kk = pl.program_id(2)

            @pl.when(kk == 0)
            def _():
                acc_ref[...] = part

            @pl.when(kk > 0)
            def _():
                acc_ref[...] += part

            @pl.when(kk == nk - 1)
            def _():
                finish(acc_ref[...])

    ins = [a, b] + ([] if res is None else [res]) + ([] if norm_g is None else [norm_g]) + ([] if after is None else [after])
    in_specs = [a_spec, b_spec] + ([] if res is None else [o_spec])
    in_specs += ([] if norm_g is None else [pl.BlockSpec((1, tn), lambda i, j, kk: (0, j))]) + ([] if after is None else [ANY])
    outs = _pcall(
        body, name=name, grid=(m // tm, n // tn, nk), in_specs=in_specs, out_specs=[o_spec] * n_out,
        out_shape=[jax.ShapeDtypeStruct((m, n), out_dtype)] + [jax.ShapeDtypeStruct((m, n), BF16)] * (n_out - 1),
        scratch_shapes=[pltpu.VMEM((tm, tn), F32)] if nk > 1 else [],
        compiler_params=_params("parallel", "parallel", "arbitrary"),
    )(*ins)
    return outs[0] if norm_g is None else outs


def _rowcall(fn, rows, consts, outs, accs=(), *, tm, name, after=None):
    s = rows[0][0].shape[0]
    assert s % tm == 0
    n_read, n_out = len(rows) + len(consts), len(outs)
    n_in = n_read + (after is not None)

    def body(*refs):
        vals = fn(*[r[...] for r in refs[:n_read]])
        vals = vals if isinstance(vals, (tuple, list)) else (vals,)
        for o_ref, v in zip(refs[n_in:n_in + n_out], vals[:n_out]):
            o_ref[...] = v.astype(o_ref.dtype)
        if accs:
            first = pl.program_id(0) == 0
            for a_ref, v in zip(refs[n_in + n_out:], vals[n_out:]):
                tot = jnp.sum(v.astype(F32), axis=0, keepdims=True)

                @pl.when(first)
                def _(a_ref=a_ref, tot=tot):
                    a_ref[...] = tot

                @pl.when(jnp.logical_not(first))
                def _(a_ref=a_ref, tot=tot):
                    a_ref[...] += tot

    in_specs = [pl.BlockSpec((tm, w), lambda i, cb=cb: (i, cb)) for (_, cb, w) in rows]
    in_specs += [pl.BlockSpec(c.shape, lambda i: (0, 0)) for c in consts]
    in_specs += [] if after is None else [ANY]
    out_specs = [pl.BlockSpec((tm, w), lambda i: (i, 0)) for (w, _) in outs]
    out_specs += [pl.BlockSpec((1, w), lambda i: (0, 0)) for w in accs]
    out_shape = [jax.ShapeDtypeStruct((s, w), dt) for (w, dt) in outs]
    out_shape += [jax.ShapeDtypeStruct((1, w), F32) for w in accs]
    return _pcall(
        body, name=name, grid=(s // tm,), in_specs=in_specs, out_specs=out_specs, out_shape=out_shape,
        compiler_params=_params("arbitrary" if accs else "parallel"),
    )(*[r[0] for r in rows], *consts, *([] if after is None else [after]))


def _whole(a):
    return (a, 0, a.shape[1])


def _xhat(x):
    x = x.astype(F32)
    r = lax.rsqrt(jnp.mean(x * x, axis=-1, keepdims=True) + RMS_EPS)
    return x * r, r


def _rms_bwd(dy, x, g):
    xh, r = _xhat(x)
    dxh = dy.astype(F32) * g
    dx = r * (dxh - xh * jnp.mean(dxh * xh, axis=-1, keepdims=True))
    return dx, dy.astype(F32) * xh


def _sigmoid(x):
    return 1.0 / (1.0 + jnp.exp(-x))


def _rms_fwd(x, g, name, tm, after=None):
    d = x.shape[1]
    return _rowcall(lambda xb, gb: _xhat(xb)[0] * gb, [_whole(x)], [g], [(d, BF16)], tm=tm, name=name, after=after)[0]


def _silu_parts(gate):
    sg = _sigmoid(gate)
    return sg, gate * sg


def _ffn_up(n, w_gu, name):
    s, d = n.shape
    f = w_gu.shape[1] // 2
    tn = _pick(f, (1408, 1024, 512, 256, 128))
    tm = _pick(s, (1024, 512, 256, 128))
    nb = f // tn

    def body(n_ref, wg_ref, wu_ref, gu_ref, act_ref):
        nv = n_ref[...]
        gate = _dot(nv, wg_ref[...], 1, 0)
        up = _dot(nv, wu_ref[...], 1, 0)
        gu_ref[0] = gate.astype(gu_ref.dtype)
        gu_ref[1] = up.astype(gu_ref.dtype)
        act_ref[...] = (_silu_parts(gate)[1] * up).astype(act_ref.dtype)

    return _pcall(
        body, name=name, grid=(s // tm, nb),
        in_specs=[pl.BlockSpec((tm, d), lambda i, j: (i, 0)), pl.BlockSpec((d, tn), lambda i, j: (0, j)),
                  pl.BlockSpec((d, tn), lambda i, j: (0, nb + j))],
        out_specs=[pl.BlockSpec((2, tm, tn), lambda i, j: (0, i, j)), pl.BlockSpec((tm, tn), lambda i, j: (i, j))],
        out_shape=[jax.ShapeDtypeStruct((2, s, f), BF16), jax.ShapeDtypeStruct((s, f), BF16)],
        compiler_params=_params("parallel", "parallel"),
    )(n, w_gu, w_gu)


def _ffn_dgu(dhb, w_down, gu, name, after=None):
    s, d = dhb.shape
    f = w_down.shape[0]
    tn = _pick(f, (1408, 1024, 512, 256, 128))
    tm = _pick(s, (1024, 512, 256, 128))

    def body(dh_ref, w_ref, gu_ref, *rest):
        o_ref = rest[-1]
        dact = _dot(dh_ref[...], w_ref[...], 1, 1)
        gate, up = gu_ref[0].astype(F32), gu_ref[1].astype(F32)
        sg, silu = _silu_parts(gate)
        o_ref[0] = (dact * up * (sg + silu * (1.0 - sg))).astype(o_ref.dtype)
        o_ref[1] = (dact * silu).astype(o_ref.dtype)

    blk = pl.BlockSpec((2, tm, tn), lambda i, j: (0, i, j))
    return _pcall(
        body, name=name, grid=(s // tm, f // tn),
        in_specs=[pl.BlockSpec((tm, d), lambda i, j: (i, 0)), pl.BlockSpec((tn, d), lambda i, j: (j, 0)), blk]
        + ([] if after is None else [ANY]),
        out_specs=blk, out_shape=jax.ShapeDtypeStruct((2, s, f), BF16), compiler_params=_params("parallel", "parallel"),
    )(dhb, w_down, gu, *([] if after is None else [after]))


def _dgrad_norm(dy, wmat, dh, x, g, name, *, dy_halves=False, copy_scale=None, after=None):
    s, d = dh.shape
    k = wmat.shape[1]
    tk = k if k <= MM_WHOLE_K else _pick(k, (MM_WHOLE_K, 2048, 1024, 512, 256, 128))
    if dy_halves and (k // 2) % tk:
        tk = _pick(k // 2, (1408, 1024, 512, 256, 128))
    tm = _pick(s, (512, 256, 128))
    nk, per = k // tk, (k // 2) // tk if dy_halves else 0
    n_in = 5 + (after is not None)
    n_out = 2 + (copy_scale is not None)

    def body(*refs):
        dy_ref, w_ref, dh_ref, x_ref, g_ref = refs[:5]
        outs, scratch = refs[n_in:n_in + n_out], refs[n_in + n_out:]
        i, kk = pl.program_id(0), pl.program_id(1)
        part = _dot(dy_ref[...], w_ref[...], 1, 1)

        def finish(dn):
            dx, dg = _rms_bwd(dn, x_ref[...], g_ref[...])
            tot = dh_ref[...] + dx
            outs[0][...] = tot
            if copy_scale is not None:
                outs[1][...] = (copy_scale * tot).astype(outs[1].dtype)
            dg = jnp.sum(dg, axis=0, keepdims=True)

            @pl.when(i == 0)
            def _():
                outs[-1][...] = dg

            @pl.when(i > 0)
            def _():
                outs[-1][...] += dg

        if nk == 1:
            finish(part)
        else:
            acc_ref = scratch[0]

            @pl.when(kk == 0)
            def _():
                acc_ref[...] = part

            @pl.when(kk > 0)
            def _():
                acc_ref[...] += part

            @pl.when(kk == nk - 1)
            def _():
                finish(acc_ref[...])

    row = pl.BlockSpec((tm, d), lambda i, kk: (i, 0))
    dy_spec = pl.BlockSpec((None, tm, tk), lambda i, kk: (kk // per, i, kk % per)) if dy_halves else pl.BlockSpec((tm, tk), lambda i, kk: (i, kk))
    in_specs = [dy_spec, pl.BlockSpec((d, tk), lambda i, kk: (0, kk)), row, row, pl.BlockSpec((1, d), lambda i, kk: (0, 0))]
    out_specs = [row] * (n_out - 1) + [pl.BlockSpec((1, d), lambda i, kk: (0, 0))]
    out_shape = [jax.ShapeDtypeStruct((s, d), F32)] + ([] if copy_scale is None else [jax.ShapeDtypeStruct((s, d), BF16)])
    return _pcall(
        body, name=name, grid=(s // tm, nk), in_specs=in_specs + ([] if after is None else [ANY]), out_specs=out_specs,
        out_shape=out_shape + [jax.ShapeDtypeStruct((1, d), F32)], scratch_shapes=[pltpu.VMEM((tm, d), F32)] if nk > 1 else [],
        compiler_params=_params("arbitrary", "arbitrary"),
    )(dy, wmat, dh, x, g, *([] if after is None else [after]))


def _shift_down(p, k):
    if k == 0:
        return p
    rows = lax.broadcasted_iota(jnp.int32, p.shape, 0)
    return jnp.where(rows >= k, pltpu.roll(p, k, 0), 0.0)


def _shift_up(p, k):
    if k == 0:
        return p
    s = p.shape[0]
    rows = lax.broadcasted_iota(jnp.int32, p.shape, 0)
    return jnp.where(rows < s - k, pltpu.roll(p, s - k, 0), 0.0)


def _conv_fwd(proj, conv_w, d, tc, name):
    s = proj.shape[0]
    nb = d // tc

    def body(cb_ref, cc_ref, cx_ref, w_ref, y_ref):
        p = cc_ref[...].astype(F32) * cx_ref[...].astype(F32)
        w = w_ref[...]
        acc = p * w[CONV_K - 1:CONV_K, :]
        for k in range(1, CONV_K):
            acc = acc + _shift_down(p, k) * w[CONV_K - 1 - k:CONV_K - k, :]
        y_ref[...] = (cb_ref[...].astype(F32) * acc).astype(y_ref.dtype)

    col = lambda off: pl.BlockSpec((s, tc), lambda j: (0, off * nb + j))
    return _pcall(
        body, name=name, grid=(nb,), in_specs=[col(0), col(1), col(2), pl.BlockSpec((CONV_K, tc), lambda j: (0, j))],
        out_specs=pl.BlockSpec((s, tc), lambda j: (0, j)), out_shape=jax.ShapeDtypeStruct((s, d), BF16),
        compiler_params=_params("parallel"),
    )(proj, proj, proj, conv_w)


def _conv_bwd(dy, proj, conv_w, d, tc, name):
    s = proj.shape[0]
    nb = d // tc

    def body(dy_ref, cb_ref, cc_ref, cx_ref, w_ref, dcb_ref, dcc_ref, dcx_ref, dw_ref):
        cc, cx = cc_ref[...].astype(F32), cx_ref[...].astype(F32)
        p = cc * cx
        w = w_ref[...]
        dyv = dy_ref[...].astype(F32)
        shifted = [_shift_down(p, CONV_K - 1 - k) for k in range(CONV_K)]
        conv = shifted[0] * w[0:1, :]
        for k in range(1, CONV_K):
            conv = conv + shifted[k] * w[k:k + 1, :]
        dcb_ref[...] = (dyv * conv).astype(dcb_ref.dtype)
        ds = dyv * cb_ref[...].astype(F32)
        dp = ds * w[CONV_K - 1:CONV_K, :]
        for k in range(1, CONV_K):
            dp = dp + _shift_up(ds, k) * w[CONV_K - 1 - k:CONV_K - k, :]
        dcc_ref[...] = (dp * cx).astype(dcc_ref.dtype)
        dcx_ref[...] = (dp * cc).astype(dcx_ref.dtype)
        for k in range(CONV_K):
            dw_ref[k:k + 1, :] = jnp.sum(ds * shifted[k], axis=0, keepdims=True)

    col = lambda off: pl.BlockSpec((s, tc), lambda j: (0, off * nb + j))
    blk = pl.BlockSpec((s, tc), lambda j: (0, j))
    wblk = pl.BlockSpec((CONV_K, tc), lambda j: (0, j))
    act = jax.ShapeDtypeStruct((s, d), BF16)
    return _pcall(
        body, name=name, grid=(nb,), in_specs=[blk, col(0), col(1), col(2), wblk],
        out_specs=[blk, blk, blk, wblk], out_shape=[act, act, act, jax.ShapeDtypeStruct((CONV_K, d), F32)],
        compiler_params=_params("parallel"),
    )(dy, proj, proj, proj, conv_w)


def _sb_tile(q, kj, scale, carry, tri, mask):
    z = _dot(q, kj, 1, 1) * scale
    lsz = jnp.minimum(z, 0.0) - jnp.log(1.0 + jnp.exp(-jnp.abs(z)))
    l1m = lsz - z
    if mask is not None:
        l1m = jnp.where(mask, l1m, 0.0)
    l1b = l1m.astype(BF16)
    a = jnp.exp(lsz + (carry + _dot(l1b, tri, 1, 0)))
    if mask is not None:
        a = jnp.where(mask, a, 0.0)
    return lsz, l1b, a.astype(BF16)


def _add_rows(x, upd, r0):
    return x + upd if r0 == 0 else jnp.concatenate([x[:r0], x[r0:] + upd], axis=0)


def _sb_masks(tq, tk):
    row = lax.broadcasted_iota(jnp.int32, (tq, tk), 0)
    col = lax.broadcasted_iota(jnp.int32, (tq, tk), 1)
    masks = [col + dj * tk < row for dj in range(tq // tk)]
    r2 = lax.broadcasted_iota(jnp.int32, (tk, tk), 0)
    c2 = lax.broadcasted_iota(jnp.int32, (tk, tk), 1)
    return masks, (r2 > c2).astype(BF16), (r2 < c2).astype(BF16)


def _sb_fwd(proj, heads, col0, tq, tk, name):
    s = proj.shape[0]
    dh = SB_HEAD_DIM
    nq, nd, nkt = s // tq, tq // tk, s // tk
    scale = dh ** -0.5

    def body(q_ref, k_ref, v_ref, o_ref, a_ref, b_ref):
        i = pl.program_id(1)
        q = q_ref[...]
        masks, tri_right, _ = _sb_masks(tq, tk)

        def tile(j, carry, acc, mask, r0=0):
            start = pl.multiple_of(j * tk, tk)
            kj = k_ref[pl.ds(start, tk), :]
            vj = v_ref[pl.ds(start, tk), :]
            lsz, l1b, ab = _sb_tile(q[r0:], kj, scale, carry[r0:], tri_right, None if mask is None else mask[r0:])
            a_ref[j, r0:, :] = ab
            b_ref[j, r0:, :] = jnp.exp(lsz).astype(b_ref.dtype)
            if r0:
                a_ref[j, :r0, :] = jnp.zeros((r0, tk), a_ref.dtype)
                b_ref[j, :r0, :] = jnp.zeros((r0, tk), b_ref.dtype)
            return (_add_rows(carry, jnp.sum(l1b.astype(F32), axis=1, keepdims=True), r0),
                    _add_rows(acc, _dot(ab, vj, 1, 0), r0))

        state = (jnp.zeros((tq, 1), F32), jnp.zeros((tq, dh), F32))
        for dj in reversed(range(nd)):
            state = tile(i * nd + dj, *state, masks[dj], dj * tk)
        def left_block(t, st):
            for dj in reversed(range(nd)):
                st = tile((i - 1 - t) * nd + dj, st[0], st[1], None)
            return st

        state = lax.fori_loop(0, i, left_block, state)
        o_ref[...] = state[1]

    qspec = pl.BlockSpec((tq, dh), lambda h, i: (i, col0[0] + h))
    kspec = pl.BlockSpec((s, dh), lambda h, i: (0, col0[1] + h))
    vspec = pl.BlockSpec((s, dh), lambda h, i: (0, col0[2] + h))
    saved = pl.BlockSpec((None, nkt, tq, tk), lambda h, i: (h, 0, i, 0))
    saved_shape = jax.ShapeDtypeStruct((heads, nkt, s, tk), BF16)
    return _pcall(
        body, name=name, grid=(heads, nq), in_specs=[qspec, kspec, vspec],
        out_specs=[pl.BlockSpec((tq, dh), lambda h, i: (i, h)), saved, saved],
        out_shape=[jax.ShapeDtypeStruct((s, heads * dh), F32), saved_shape, saved_shape],
        compiler_params=_params("parallel", "parallel"),
    )(proj, proj, proj)


SB_BWD_HEADS = 2


def _sb_bwd(proj, o, a_all, beta_all, do, heads, col0, tq, tk, name):
    s = proj.shape[0]
    dh = SB_HEAD_DIM
    nq, nd, nkt = s // tq, tq // tk, s // tk
    scale = dh ** -0.5
    hb = SB_BWD_HEADS if heads % SB_BWD_HEADS == 0 and all(c % SB_BWD_HEADS == 0 for c in col0) else 1
    wide = hb * dh

    def body(q_ref, k_ref, v_ref, o_ref, a_ref, b_ref, do_ref, dq_ref, dk_ref, dv_ref, dk_acc, dv_acc):
        i = pl.program_id(1)

        @pl.when(i == 0)
        def _():
            dk_acc[...] = jnp.zeros_like(dk_acc)
            dv_acc[...] = jnp.zeros_like(dv_acc)

        lanes = [slice(hh * dh, (hh + 1) * dh) for hh in range(hb)]
        q = [q_ref[:, ln] for ln in lanes]
        dob = [do_ref[:, ln].astype(BF16) for ln in lanes]
        delta = [jnp.sum(dob[hh].astype(F32) * o_ref[:, lanes[hh]], axis=1, keepdims=True) for hh in range(hb)]
        masks, _, tri_left = _sb_masks(tq, tk)

        def tile(hh, j, carry_g, dq, mask):
            start = pl.multiple_of(j * tk, tk)
            kj = k_ref[pl.ds(start, tk), lanes[hh]]
            vj = v_ref[pl.ds(start, tk), lanes[hh]]
            ab = a_ref[hh, j]
            g = _dot(dob[hh], vj, 1, 1) * ab.astype(F32)
            carry_g = carry_g + jnp.sum(g, axis=1, keepdims=True)
            left = (delta[hh] - carry_g) + _dot(g.astype(BF16), tri_left, 1, 0)
            dz = g - b_ref[hh, j].astype(F32) * (g + left)
            if mask is not None:
                dz = jnp.where(mask, dz, 0.0)
            dzb = dz.astype(BF16)
            dk_acc[pl.ds(start, tk), lanes[hh]] += _dot(dzb, q[hh], 0, 0)
            dv_acc[pl.ds(start, tk), lanes[hh]] += _dot(ab, dob[hh], 0, 0)
            return carry_g, dq + _dot(dzb, kj, 1, 0)

        def block(jb, st, use_masks):
            st = list(st)
            for dj in reversed(range(nd)):
                for hh in range(hb):
                    st[hh] = tile(hh, jb * nd + dj, *st[hh], masks[dj] if use_masks else None)
            return tuple(st)

        state = block(i, tuple((jnp.zeros((tq, 1), F32), jnp.zeros((tq, dh), F32)) for _ in range(hb)), True)
        state = lax.fori_loop(0, i, lambda t, st: block(i - 1 - t, st, False), state)
        for hh in range(hb):
            dq_ref[:, lanes[hh]] = (state[hh][1] * scale).astype(dq_ref.dtype)

        @pl.when(i == nq - 1)
        def _():
            dk_ref[...] = (dk_acc[...] * scale).astype(dk_ref.dtype)
            dv_ref[...] = dv_acc[...].astype(dv_ref.dtype)

    qspec = pl.BlockSpec((tq, wide), lambda h, i: (i, col0[0] // hb + h))
    kspec = pl.BlockSpec((s, wide), lambda h, i: (0, col0[1] // hb + h))
    vspec = pl.BlockSpec((s, wide), lambda h, i: (0, col0[2] // hb + h))
    blk = pl.BlockSpec((tq, wide), lambda h, i: (i, h))
    full = pl.BlockSpec((s, wide), lambda h, i: (0, h))
    saved = pl.BlockSpec((hb, nkt, tq, tk), lambda h, i: (h, 0, i, 0))
    act = jax.ShapeDtypeStruct((s, heads * dh), BF16)
    return _pcall(
        body, name=name, grid=(heads // hb, nq), in_specs=[qspec, kspec, vspec, blk, saved, saved, blk],
        out_specs=[blk, full, full], out_shape=[act, act, act],
        scratch_shapes=[pltpu.VMEM((s, wide), F32), pltpu.VMEM((s, wide), F32)],
        compiler_params=_params("parallel", "arbitrary"),
    )(proj, proj, proj, o, a_all, beta_all, do)


def _xattn_probs(q, k, scale):
    sc = _dot(q, k, 1, 1) * scale
    e = jnp.exp(sc - jnp.max(sc, axis=1, keepdims=True))
    return e / jnp.sum(e, axis=1, keepdims=True)


def _xattn_fwd(qc, kv, tq, name):
    s, d = qc.shape
    m = kv.shape[0]
    dh = d // X_HEADS
    scale = dh ** -0.5

    def body(q_ref, k_ref, v_ref, o_ref):
        p = _xattn_probs(q_ref[...], k_ref[...], scale)
        o_ref[...] = _dot(p.astype(BF16), v_ref[...], 1, 0).astype(o_ref.dtype)

    blk = pl.BlockSpec((tq, dh), lambda h, i: (i, h))
    return _pcall(
        body, name=name, grid=(X_HEADS, s // tq),
        in_specs=[blk, pl.BlockSpec((m, dh), lambda h, i: (0, h)), pl.BlockSpec((m, dh), lambda h, i: (0, X_HEADS + h))],
        out_specs=blk, out_shape=jax.ShapeDtypeStruct((s, d), BF16), compiler_params=_params("parallel", "parallel"),
    )(qc, kv, kv)


def _xattn_bwd(qc, kv, do, tq, name):
    s, d = qc.shape
    m = kv.shape[0]
    dh = d // X_HEADS
    scale = dh ** -0.5
    nq = s // tq

    def body(q_ref, k_ref, v_ref, do_ref, dq_ref, dk_ref, dv_ref, dk_acc, dv_acc):
        i = pl.program_id(1)
        q, k, v = q_ref[...], k_ref[...], v_ref[...]
        dob = do_ref[...].astype(BF16)
        p = _xattn_probs(q, k, scale)
        pb = p.astype(BF16)
        dp = _dot(dob, v, 1, 1)
        ds = pb.astype(F32) * (dp - jnp.sum(dp * pb.astype(F32), axis=1, keepdims=True))
        dsb = (ds * scale).astype(BF16)
        dq_ref[...] = _dot(dsb, k, 1, 0).astype(dq_ref.dtype)
        dk_part = _dot(dsb, q, 0, 0)
        dv_part = _dot(pb, dob, 0, 0)

        @pl.when(i == 0)
        def _():
            dk_acc[...] = dk_part
            dv_acc[...] = dv_part

        @pl.when(i > 0)
        def _():
            dk_acc[...] += dk_part
            dv_acc[...] += dv_part

        @pl.when(i == nq - 1)
        def _():
            dk_ref[...] = dk_acc[...].astype(dk_ref.dtype)
            dv_ref[...] = dv_acc[...].astype(dv_ref.dtype)

    blk = pl.BlockSpec((tq, dh), lambda h, i: (i, h))
    kblk = pl.BlockSpec((m, dh), lambda h, i: (0, h))
    return _pcall(
        body, name=name, grid=(X_HEADS, nq),
        in_specs=[blk, kblk, pl.BlockSpec((m, dh), lambda h, i: (0, X_HEADS + h)), blk],
        out_specs=[blk, kblk, kblk],
        out_shape=[jax.ShapeDtypeStruct((s, d), BF16), jax.ShapeDtypeStruct((m, d), BF16), jax.ShapeDtypeStruct((m, d), BF16)],
        scratch_shapes=[pltpu.VMEM((m, dh), F32), pltpu.VMEM((m, dh), F32)],
        compiler_params=_params("parallel", "arbitrary"),
    )(qc, kv, kv, do)


def _down_loss(act, w_down, h, tgt, g, name):
    s, f = act.shape
    d = w_down.shape[1]
    tm = _pick(s, (512, 256, 128))

    def body(a_ref, w_ref, h_ref, t_ref, g_ref, dh_ref, dhb_ref, dg_ref, loss_ref):
        xh, r = _xhat(h_ref[...] + 0.5 * _dot(a_ref[...], w_ref[...], 1, 0))
        gv = g_ref[...]
        err = xh * gv - t_ref[...]
        dy = err * (1.0 / d)
        dxh = dy * gv
        dx = r * (dxh - xh * jnp.mean(dxh * xh, axis=-1, keepdims=True))
        dh_ref[...] = dx
        dhb_ref[...] = (0.5 * dx).astype(dhb_ref.dtype)
        dg = jnp.sum(dy * xh, axis=0, keepdims=True)
        loss = jnp.broadcast_to(jnp.sum(0.5 * jnp.mean(err * err, axis=-1, keepdims=True), axis=0, keepdims=True), (1, LANES))

        @pl.when(pl.program_id(0) == 0)
        def _():
            dg_ref[...] = dg
            loss_ref[...] = loss

        @pl.when(pl.program_id(0) > 0)
        def _():
            dg_ref[...] += dg
            loss_ref[...] += loss

    row = pl.BlockSpec((tm, d), lambda i: (i, 0))
    once = lambda shape: pl.BlockSpec(shape, lambda i: (0, 0))
    return _pcall(
        body, name=name, grid=(s // tm,),
        in_specs=[pl.BlockSpec((tm, f), lambda i: (i, 0)), once((f, d)), row, row, once((1, d))],
        out_specs=[row, row, once((1, d)), once((1, LANES))],
        out_shape=[jax.ShapeDtypeStruct((s, d), F32), jax.ShapeDtypeStruct((s, d), BF16), jax.ShapeDtypeStruct((1, d), F32),
                   jax.ShapeDtypeStruct((1, LANES), F32)],
        compiler_params=_params("arbitrary"),
    )(act, w_down, h, tgt, g)


def _mix_merge(y_conv, y_sb, w_conv_out, w_attn_out, proj, gate_blocks, b_conv, b_sb, name):
    s, d = y_conv.shape
    tm, tn = _pick(s, (1024, 512, 256, 128)), _pick(d, (512, 256, 128))
    nb = d // tn

    def body(yc_ref, ys_ref, wc_ref, ws_ref, gc_ref, gs_ref, bc_ref, bs_ref, ac_ref, as_ref, m_ref):
        ac = _dot(yc_ref[...].astype(BF16), wc_ref[...], 1, 0)
        asb = _dot(ys_ref[...].astype(BF16), ws_ref[...], 1, 0)
        gc = _sigmoid(gc_ref[...].astype(F32) + bc_ref[...])
        gs = _sigmoid(gs_ref[...].astype(F32) + bs_ref[...])
        ac_ref[...] = ac.astype(ac_ref.dtype)
        as_ref[...] = asb.astype(as_ref.dtype)
        m_ref[...] = (gc * ac + gs * asb).astype(m_ref.dtype)

    rows = pl.BlockSpec((tm, d), lambda i, j: (i, 0))
    wcol = pl.BlockSpec((d, tn), lambda i, j: (0, j))
    bias = pl.BlockSpec((1, tn), lambda i, j: (0, j))
    gate = lambda blk: pl.BlockSpec((tm, tn), lambda i, j: (i, blk * nb + j))
    out = pl.BlockSpec((tm, tn), lambda i, j: (i, j))
    act = jax.ShapeDtypeStruct((s, d), BF16)
    return _pcall(
        body, name=name, grid=(s // tm, nb),
        in_specs=[rows, rows, wcol, wcol, gate(gate_blocks[0]), gate(gate_blocks[1]), bias, bias],
        out_specs=[out, out, out], out_shape=[act, act, act], compiler_params=_params("parallel", "parallel"),
    )(y_conv, y_sb, w_conv_out, w_attn_out, proj, proj, b_conv, b_sb)


def _mix_dmerge(dh, w_o, a_conv, a_sb, proj, gate_blocks, b_conv, b_sb, name):
    s, d = a_conv.shape
    tm, tn = _pick(s, (1024, 512, 256, 128)), _pick(d, (512, 256, 128))
    nb = d // tn

    def body(dh_ref, w_ref, ac_ref, as_ref, gc_ref, gs_ref, bc_ref, bs_ref, dac_ref, das_ref, dgc_ref, dgs_ref, dbc_ref, dbs_ref):
        dm = _dot(dh_ref[...], w_ref[...], 1, 1)
        gc = _sigmoid(gc_ref[...].astype(F32) + bc_ref[...])
        gs = _sigmoid(gs_ref[...].astype(F32) + bs_ref[...])
        dgc = dm * ac_ref[...].astype(F32) * gc * (1.0 - gc)
        dgs = dm * as_ref[...].astype(F32) * gs * (1.0 - gs)
        dac_ref[...] = (dm * gc).astype(dac_ref.dtype)
        das_ref[...] = (dm * gs).astype(das_ref.dtype)
        dgc_ref[...] = dgc.astype(dgc_ref.dtype)
        dgs_ref[...] = dgs.astype(dgs_ref.dtype)
        sums = jnp.sum(dgc, axis=0, keepdims=True), jnp.sum(dgs, axis=0, keepdims=True)

        @pl.when(pl.program_id(1) == 0)
        def _():
            dbc_ref[...], dbs_ref[...] = sums

        @pl.when(pl.program_id(1) > 0)
        def _():
            dbc_ref[...] += sums[0]
            dbs_ref[...] += sums[1]

    tile = pl.BlockSpec((tm, tn), lambda j, i: (i, j))
    bias = pl.BlockSpec((1, tn), lambda j, i: (0, j))
    gate = lambda blk: pl.BlockSpec((tm, tn), lambda j, i: (i, blk * nb + j))
    act = jax.ShapeDtypeStruct((s, d), BF16)
    vec = jax.ShapeDtypeStruct((1, d), F32)
    return _pcall(
        body, name=name, grid=(nb, s // tm),
        in_specs=[pl.BlockSpec((tm, d), lambda j, i: (i, 0)), pl.BlockSpec((tn, d), lambda j, i: (j, 0)), tile, tile,
                  gate(gate_blocks[0]), gate(gate_blocks[1]), bias, bias],
        out_specs=[tile, tile, tile, tile, bias, bias], out_shape=[act, act, act, act, vec, vec],
        compiler_params=_params("parallel", "arbitrary"),
    )(dh, w_o, a_conv, a_sb, proj, proj, b_conv, b_sb)


def _local_step(x, mem, tgt, w, fetch=None, prefetch=None, emit=None, tick=None, after=None):
    fetch = fetch or (lambda name, after: {})
    prefetch = prefetch or (lambda name, after: None)
    emit = emit or (lambda group, g: None)
    tick = tick or (lambda group, after: None)
    w = dict(w)
    s, d = x.shape
    heads = d // SB_HEAD_DIM
    tm = _pick(s, (1024, 512, 256, 128))
    tq = _pick(s, (2048, 1024, 512, 256, 128))
    sb_tq, sb_tk = _pick(s, (512, 256, 128)), _pick(s, (256, 128))
    tc = _pick(d, (256, 128))
    g = {}

    def wt(name, after):
        if name not in w:
            w.update(fetch(name, after))
        return w[name]

    def ffn_fwd(h, n, wgu, wdown, tag, next_g=None):
        gu, act = _ffn_up(n, wt(wgu, n), tag + "_gu")
        prefetch(wdown, gu)
        return gu, act, _mm(act, wt(wdown, act), name=tag + "_down", out_dtype=F32, res=h, alpha=0.5, norm_g=next_g)

    def ffn_bwd(dh, dhb, h, saved, gname, wgu, wdown, tag, copy_scale=None, after=None):
        n, gu, act = saved
        g[wdown] = _mm(act, dhb, ta=True, name=tag + "_dwdown", after=after)
        dgu = _ffn_dgu(dhb, w[wdown], gu, tag + "_dgu", after=emit(tag + "_down", g))
        g[wgu] = _mm(n, dgu, ta=True, b_halves=True, name=tag + "_dwgu", after=tick(tag + "_down", dgu))
        *dh_in, g[gname] = _dgrad_norm(dgu, w[wgu], dh, h, w[gname], tag + "_dn", dy_halves=True, copy_scale=copy_scale,
                                       after=emit(tag, g))
        return dh_in, tick(tag, dh_in[0])

    n1 = _rms_fwd(x, w["g_ffn1"], "ffn1_norm", tm, after=after)
    gu1, act1, (h1, u) = ffn_fwd(x, n1, "w_ffn1_gu", "w_ffn1_down", "ffn1", w["g_mix"])
    prefetch("w_in", h1)
    proj = _mm(u, wt("w_in", u), name="mix_in")
    prefetch("w_conv_out", proj)
    nd = d // SB_HEAD_DIM
    y_conv = _conv_fwd(proj, w["conv_w"], d, tc, "conv_fwd")
    sb_cols = (3 * nd, 4 * nd, 5 * nd)
    y_sb, sb_a, sb_beta = _sb_fwd(proj, heads, sb_cols, _pick(s, (2 * sb_tq, sb_tq)), sb_tk, "sb_fwd")
    prefetch("w_cq", y_sb)
    b_conv, b_sb = w["b_gate"][:, :d], w["b_gate"][:, d:]
    a_conv, a_sb, merged = _mix_merge(y_conv, y_sb, wt("w_conv_out", y_conv), wt("w_attn_out", y_sb), proj, (6, 7), b_conv, b_sb,
                                      "mix_merge")
    prefetch("w_ffn2_gu", merged)
    h2, hn = _mm(merged, wt("w_o", merged), name="mix_out", out_dtype=F32, res=h1, norm_g=w["g_cross"])
    mn = _rms_fwd(mem, w["g_mem"], "mem_norm", _pick(mem.shape[0], (256, 128)))
    qc = _mm(hn, wt("w_cq", hn), name="cross_q")
    kv = _mm(mn, wt("w_ckv", mn), name="cross_kv")
    oc = _xattn_fwd(qc, kv, tq, "xattn_fwd")
    h3, n2 = _mm(oc, wt("w_co", oc), name="cross_out", out_dtype=F32, res=h2, norm_g=w["g_ffn2"])
    gu2, act2 = _ffn_up(n2, wt("w_ffn2_gu", n2), "ffn2_gu")

    dh4, dh4b, g["g_final"], loss_lanes = _down_loss(act2, wt("w_ffn2_down", act2), h3, tgt, w["g_final"], "ffn2_down_loss")

    (dh3, dh3b), tok = ffn_bwd(dh4, dh4b, h3, (n2, gu2, act2), "g_ffn2", "w_ffn2_gu", "w_ffn2_down", "ffn2", copy_scale=1.0)
    g["w_co"] = _mm(oc, dh3b, ta=True, name="cross_dwco", after=tok)
    doc = _mm(dh3b, w["w_co"], tb=True, name="cross_doc")
    dqc, dk, dv = _xattn_bwd(qc, kv, doc, tq, "xattn_bwd")
    dkv = jnp.concatenate([dk, dv], axis=1)
    g["w_cq"] = _mm(hn, dqc, ta=True, name="cross_dwcq")
    g["w_ckv"] = _mm(mn, dkv, ta=True, name="cross_dwckv")
    dmn = _mm(dkv, w["w_ckv"], tb=True, name="cross_dmn", out_dtype=F32)
    g["g_mem"] = _rowcall(lambda dy, xb: dy * _xhat(xb)[0], [_whole(dmn), _whole(mem)], [], [], [d],
                          tm=_pick(mem.shape[0], (256, 128)), name="mem_dnorm")[0]
    dh2, dh2b, g["g_cross"] = _dgrad_norm(dqc, w["w_cq"], dh3, h2, w["g_cross"], "cross_dhn", copy_scale=1.0, after=emit("cross", g))

    g["w_o"] = _mm(merged, dh2b, ta=True, name="mix_dwo", after=tick("cross", dh2))
    da_conv, da_sb, dgc, dgs, db_conv, db_sb = _mix_dmerge(dh2b, w["w_o"], a_conv, a_sb, proj, (6, 7), b_conv, b_sb, "mix_dmerge")
    g["b_gate"] = jnp.concatenate([db_conv, db_sb], axis=1)
    g["w_conv_out"] = _mm(y_conv, da_conv, ta=True, name="conv_dwout")
    g["w_attn_out"] = _mm(y_sb, da_sb, ta=True, name="attn_dwout")
    dy_conv = _mm(da_conv, w["w_conv_out"], tb=True, name="conv_dy")
    dy_sb = _mm(da_sb, w["w_attn_out"], tb=True, name="attn_dy")
    dcb, dcc, dcx, g["conv_w"] = _conv_bwd(dy_conv, proj, w["conv_w"], d, tc, "conv_bwd")
    dq, dk_sb, dv_sb = _sb_bwd(proj, y_sb, sb_a, sb_beta, dy_sb, heads, sb_cols, sb_tq, sb_tk, "sb_bwd")
    dproj = jnp.concatenate([dcb, dcc, dcx, dq, dk_sb, dv_sb, dgc, dgs], axis=1)
    g["w_in"] = _mm(u, dproj, ta=True, name="mix_dwin")
    dh1, dh1b, g["g_mix"] = _dgrad_norm(dproj, w["w_in"], dh2, h1, w["g_mix"], "mix_du", copy_scale=0.5, after=emit("mix", g))
    (dx,), tok = ffn_bwd(dh1, dh1b, x, (n1, gu1, act1), "g_ffn1", "w_ffn1_gu", "w_ffn1_down", "ffn1", after=tick("mix", dh1))
    return loss_lanes, dx, g, tok


MATS = (("w_ffn1_gu", "col"), ("w_ffn1_down", "row"), ("w_in", "col"), ("w_conv_out", "row"), ("w_attn_out", "row"),
        ("w_o", "row"), ("w_cq", "row"), ("w_ckv", "col"), ("w_co", "row"), ("w_ffn2_gu", "col"), ("w_ffn2_down", "row"))
VECS = ("g_ffn1", "g_mix", "g_cross", "g_mem", "g_ffn2", "g_final")
WEIGHTS = ("g_ffn1", "w_ffn1_gu", "w_ffn1_down", "g_mix", "w_in", "b_gate", "conv_w", "w_conv_out", "w_attn_out", "w_o",
           "g_cross", "g_mem", "w_cq", "w_ckv", "w_co", "g_ffn2", "w_ffn2_gu", "w_ffn2_down", "g_final")
CONV_ROWS = 16


def _full_shape(kind, r, c):
    return (r, N_CHIPS * c) if kind == "col" else (N_CHIPS * r, c)


def _piece(ref, kind, r, c, chip, half):
    hr = r // 2
    if kind == "col":
        return ref.at[pl.ds(pl.multiple_of(half * hr, math.gcd(hr, 16)), hr), pl.ds(pl.multiple_of(chip * c, LANES), c)]
    return ref.at[pl.ds(pl.multiple_of(chip * r + half * hr, math.gcd(hr, 16)), hr), :]


def _shard_of(ref, kind, r, c, chip):
    if kind == "col":
        return ref.at[:, pl.ds(pl.multiple_of(chip * c, LANES), c)]
    return ref.at[pl.ds(pl.multiple_of(chip * r, 16), r), :]


def _place():
    x, y, c = lax.axis_index("x"), lax.axis_index("y"), lax.axis_index("c")
    others = [(1 - x, y), (x, 1 - y), (1 - x, 1 - y)]
    return x, y, c, 2 * x + y, others


def _remote(src, dst, send_sem, recv_sem, to):
    return pltpu.make_async_remote_copy(src_ref=src, dst_ref=dst, send_sem=send_sem, recv_sem=recv_sem,
                                        device_id=to, device_id_type=MESH)


HBM = pl.BlockSpec(memory_space=pltpu.HBM)
SEM = pl.BlockSpec(memory_space=pltpu.SEMAPHORE)
EFFECT = pltpu.SideEffectType.DATAFLOW_SIDE_EFFECTING
TOKEN = (8, LANES)


def _split_start(name, plan, n_copies, srcs, lands, after=None):
    ns, nl = len(srcs), len(lands)
    n_in = ns + nl + (after is not None)

    def body(*refs):
        outs = refs[n_in:]
        sends, _ = plan(refs[:ns], refs[ns:ns + nl], outs[0], outs[1])
        for cp in sends:
            cp.start()
        outs[-1][...] = jnp.zeros(TOKEN, F32)

    held = [pltpu.HBM(a.shape, a.dtype) for a in (*srcs, *lands)]
    dma = pltpu.SemaphoreType.DMA((n_copies,))
    ins = [pltpu.with_memory_space_constraint(a, pltpu.HBM) for a in (*srcs, *lands)]
    outs = _pcall(
        body, name=name, in_specs=[HBM] * (ns + nl) + ([] if after is None else [ANY]),
        out_specs=(SEM, SEM, *[HBM] * (ns + nl), pl.BlockSpec(memory_space=pltpu.VMEM)),
        out_shape=(dma, dma, *held, jax.ShapeDtypeStruct(TOKEN, F32)),
        input_output_aliases={i: 2 + i for i in range(ns + nl)},
        compiler_params=pltpu.CompilerParams(has_side_effects=EFFECT),
    )(*ins, *([] if after is None else [after]))
    return outs[0], outs[1], list(outs[2:2 + ns]), list(outs[2 + ns:2 + ns + nl]), outs[-1]


def _split_wait(name, plan, send_sems, recv_sems, srcs, lands, after):
    ns, nl = len(srcs), len(lands)

    def body(*refs):
        sends, recvs = plan(refs[:ns], refs[ns:ns + nl], refs[ns + nl], refs[ns + nl + 1])
        for cp in sends:
            cp.wait_send()
        for cp in recvs:
            cp.wait_recv()

    outs = _pcall(
        body, name=name, in_specs=[HBM] * (ns + nl) + [SEM, SEM, ANY], out_specs=[HBM] * (ns + nl),
        out_shape=[pltpu.HBM(a.shape, a.dtype) for a in (*srcs, *lands)],
        input_output_aliases={i: i for i in range(ns + nl)},
        compiler_params=pltpu.CompilerParams(has_side_effects=EFFECT),
    )(*srcs, *lands, send_sems, recv_sems, after)
    return list(outs[:ns]), list(outs[ns:])


def _gather_plan(dims):
    def plan(shard_refs, full_refs, ss, rs):
        x, y, c, me, others = _place()
        sends, recvs = [], []
        for wi, (kind, r, cw) in enumerate(dims):
            half = shard_refs[wi].at[pl.ds(pl.multiple_of(c * (r // 2), math.gcd(r // 2, 16)), r // 2), :]
            for k, (ox, oy) in enumerate(others):
                sem = 4 * wi + k
                sends.append(_remote(half, _piece(full_refs[wi], kind, r, cw, me, c), ss.at[sem], rs.at[sem], (ox, oy, c)))
                recvs.append(_remote(half, _piece(full_refs[wi], kind, r, cw, 2 * ox + oy, c), ss.at[sem], rs.at[sem], (x, y, c)))
            sem = 4 * wi + 3
            own = _remote(shard_refs[wi], _shard_of(full_refs[wi], kind, r, cw, me), ss.at[sem], rs.at[sem], (x, y, 1 - c))
            sends.append(own)
            recvs.append(own)
        return sends, recvs

    return plan


def _forward_plan(dims):
    def plan(_, full_refs, ss, rs):
        x, y, c, _, others = _place()
        sends, recvs = [], []
        for wi, (kind, r, cw) in enumerate(dims):
            for k, (ox, oy) in enumerate(others):
                sem = 3 * wi + k
                mine = _piece(full_refs[wi], kind, r, cw, 2 * ox + oy, c)
                theirs = _piece(full_refs[wi], kind, r, cw, 2 * ox + oy, 1 - c)
                sends.append(_remote(mine, mine, ss.at[sem], rs.at[sem], (x, y, 1 - c)))
                recvs.append(_remote(theirs, theirs, ss.at[sem], rs.at[sem], (x, y, 1 - c)))
        return sends, recvs

    return plan


def _rs_cores_plan(dims):
    def plan(g_refs, land_refs, ss, rs):
        x, y, c, _, _ = _place()
        sends, recvs = [], []
        for wi, dm in enumerate(dims):
            for chip in range(N_CHIPS):
                sem = N_CHIPS * wi + chip
                sends.append(_remote(_piece(g_refs[wi], *dm, chip, 1 - c), land_refs[wi].at[chip], ss.at[sem], rs.at[sem], (x, y, 1 - c)))
                recvs.append(_remote(_piece(g_refs[wi], *dm, chip, c), land_refs[wi].at[chip], ss.at[sem], rs.at[sem], (x, y, 1 - c)))
        return sends, recvs

    return plan


def _share_plan(nw):
    def plan(_, buf_refs, ss, rs):
        x, y, c, _, _ = _place()
        sends = [_remote(buf_refs[wi].at[c], buf_refs[wi].at[c], ss.at[wi], rs.at[wi], (x, y, 1 - c)) for wi in range(nw)]
        recvs = [_remote(buf_refs[wi].at[1 - c], buf_refs[wi].at[1 - c], ss.at[wi], rs.at[wi], (x, y, 1 - c)) for wi in range(nw)]
        return sends, recvs

    return plan


def _small_plan():
    def plan(_, buf_refs, ss, rs):
        x, y, c = lax.axis_index("x"), lax.axis_index("y"), lax.axis_index("c")
        buf = buf_refs[0]
        sends, recvs = [], []
        for rel in range(1, N_DEV):
            peer = (x ^ (rel >> 2 & 1), y ^ (rel >> 1 & 1), c ^ (rel & 1))
            sends.append(_remote(buf.at[0], buf.at[rel], ss.at[rel - 1], rs.at[rel - 1], peer))
            recvs.append(_remote(buf.at[0], buf.at[rel], ss.at[rel - 1], rs.at[rel - 1], peer))
        return sends, recvs

    return plan


def _sum_small(buf, me, name):
    _, rows, n = buf.shape

    def body(me_ref, b_ref, o_ref):
        tot = b_ref[me_ref[0]]
        for dev in range(1, N_DEV):
            tot = tot + b_ref[dev ^ me_ref[0]]
        o_ref[...] = tot

    return _pcall(
        body, name=name, out_shape=jax.ShapeDtypeStruct((rows, n), F32),
        grid_spec=pltpu.PrefetchScalarGridSpec(
            num_scalar_prefetch=1, grid=(1,), in_specs=[pl.BlockSpec((N_DEV, rows, n), lambda i, m: (0, 0, 0))],
            out_specs=pl.BlockSpec((rows, n), lambda i, m: (0, 0))),
    )(me, buf)


def _rs_chips_plan(nw):
    def plan(p_refs, land_refs, ss, rs):
        x, y, c, me, others = _place()
        sends, recvs = [], []
        for wi in range(nw):
            for k, (ox, oy) in enumerate(others):
                sem = 3 * wi + k
                sends.append(_remote(p_refs[wi].at[2 * ox + oy], land_refs[wi].at[k], ss.at[sem], rs.at[sem], (ox, oy, c)))
                recvs.append(_remote(p_refs[wi].at[me], land_refs[wi].at[k], ss.at[sem], rs.at[sem], (x, y, c)))
        return sends, recvs

    return plan


SUM_BLOCK_BYTES = 4 << 20


def _rows_per_block(n, c, limit_bytes=2 << 20):
    best = None
    for tm in range(16, n + 1, 16):
        if n % tm == 0 and tm * c * 4 <= limit_bytes:
            best = tm
    return best or n


def _sum_cores(grad, got, kind, place, name):
    _, hr, cw = got.shape
    tm = _rows_per_block(hr, cw, SUM_BLOCK_BYTES)
    nb = hr // tm

    def body(place_ref, g_ref, t_ref, o_ref):
        o_ref[...] = (g_ref[...].astype(F32) + t_ref[...].astype(F32)).astype(o_ref.dtype)

    if kind == "col":
        g_spec = pl.BlockSpec((tm, cw), lambda j, i, pr: (pr[0] * nb + i, j))
    else:
        g_spec = pl.BlockSpec((tm, cw), lambda j, i, pr: ((2 * j + pr[0]) * nb + i, 0))
    blk = pl.BlockSpec((None, tm, cw), lambda j, i, pr: (j, i, 0))
    return _pcall(
        body, name=name, out_shape=jax.ShapeDtypeStruct(got.shape, BF16),
        grid_spec=pltpu.PrefetchScalarGridSpec(num_scalar_prefetch=1, grid=(N_CHIPS, nb), in_specs=[g_spec, blk], out_specs=blk),
        compiler_params=_params("parallel", "parallel"),
    )(place, grad, got)


def _sum_chips(parts, got, place, name):
    _, n, cw = got.shape
    tm = _rows_per_block(n, cw, SUM_BLOCK_BYTES)

    def body(place_ref, p_ref, g_ref, o_ref):
        tot = p_ref[...].astype(F32)
        for k in range(3):
            tot = tot + g_ref[k].astype(F32)
        o_ref[...] = tot

    return _pcall(
        body, name=name, out_shape=jax.ShapeDtypeStruct((2, n, cw), F32),
        grid_spec=pltpu.PrefetchScalarGridSpec(
            num_scalar_prefetch=1, grid=(n // tm,),
            in_specs=[pl.BlockSpec((None, tm, cw), lambda i, pr: (pr[1], i, 0)), pl.BlockSpec((3, tm, cw), lambda i, pr: (0, i, 0))],
            out_specs=pl.BlockSpec((None, tm, cw), lambda i, pr: (pr[0], i, 0))),
        compiler_params=_params("parallel"),
    )(place, parts, got)


ADAM_BLOCK_BYTES = 1 << 20
ADAM_IN_BUFFERS = 3


def _adamw(g, w, m, v, name):
    n, c = g.shape
    c1 = 1.0 - ADAM_B1 ** ADAM_STEP
    c2 = 1.0 - ADAM_B2 ** ADAM_STEP

    def fn(gb, wb, mb, vb):
        m_new = ADAM_B1 * mb + (1.0 - ADAM_B1) * gb
        v_new = ADAM_B2 * vb + (1.0 - ADAM_B2) * (gb * gb)
        delta = -ADAM_LR * ((m_new / c1) / (jnp.sqrt(v_new / c2) + ADAM_EPS) + ADAM_WD * wb)
        return gb, delta, m_new, v_new

    tm = _rows_per_block(n, c, ADAM_BLOCK_BYTES) if n % 16 == 0 else n
    if n // tm <= ADAM_IN_BUFFERS:
        return _rowcall(fn, [_whole(g), _whole(w), _whole(m), _whole(v)], [], [(c, F32)] * 4, tm=tm, name=name)

    steps, nb = n // tm, ADAM_IN_BUFFERS

    def body(*refs):
        ins, outs, (ibuf, obuf, isem, osem) = refs[:4], refs[4:8], refs[8:]

        def read(s, k):
            return pltpu.make_async_copy(ins[k].at[pl.ds(s * tm, tm)], ibuf.at[k, s % nb], isem.at[k, s % nb])

        def write(s, k):
            return pltpu.make_async_copy(obuf.at[k, s % 2], outs[k].at[pl.ds(s * tm, tm)], osem.at[k, s % 2])

        for s in range(nb):
            for k in range(4):
                read(s, k).start()
        for s in range(steps):
            for k in range(4):
                read(s, k).wait()
            vals = fn(*[ibuf[k, s % nb] for k in range(4)])
            for k in range(4):
                if s >= 2:
                    write(s - 2, k).wait()
                obuf[k, s % 2] = vals[k]
                write(s, k).start()
                if s + nb < steps:
                    read(s + nb, k).start()
        for s in range(steps - 2, steps):
            for k in range(4):
                write(s, k).wait()

    return _pcall(body, name=name, in_specs=[ANY] * 4, out_specs=[ANY] * 4, out_shape=[jax.ShapeDtypeStruct((n, c), F32)] * 4,
                  scratch_shapes=[pltpu.VMEM((4, nb, tm, c), F32), pltpu.VMEM((4, 2, tm, c), F32),
                                  pltpu.SemaphoreType.DMA((4, nb)), pltpu.SemaphoreType.DMA((4, 2))],
                  compiler_params=_params())(g, w, m, v)


PACK_ROWS = 16


def _pack_rows(parts, width, name, after=None):
    assert sum(p.shape[0] for p in parts) <= PACK_ROWS

    def body(*refs):
        out_ref = refs[-1]
        out_ref[...] = jnp.zeros_like(out_ref)
        at = 0
        for r in refs[:len(parts)]:
            k, n = r.shape
            if n == width:
                out_ref[at:at + k, :] = r[...]
            else:
                out_ref[at:at + k, :] = jnp.broadcast_to(r[:, :1], (k, width))
            at += k

    vm = pl.BlockSpec(memory_space=pltpu.VMEM)
    return _pcall(body, name=name, in_specs=[vm] * len(parts) + ([] if after is None else [ANY]), out_specs=vm,
                  out_shape=jax.ShapeDtypeStruct((PACK_ROWS, width), F32))(*parts, *([] if after is None else [after]))


def _cast_shard(wm, name, after):
    n, c = wm.shape
    return _rowcall(lambda v: v, [_whole(wm)], [], [(c, BF16)], tm=_rows_per_block(n, c), name=name, after=after)[0]


GATHER_GROUPS = (
    ("w_ffn1_gu", "conv_w"), ("w_ffn1_down",), ("w_in",), ("w_conv_out", "w_attn_out", "w_o"), ("w_cq", "w_ckv", "w_co"),
    ("w_ffn2_gu", "w_ffn2_down"),
)
REDUCE_GROUPS = {
    "ffn2": ("w_ffn2_down", "w_ffn2_gu"),
    "cross": ("w_co", "w_cq", "w_ckv"),
    "mix": ("w_o", "w_conv_out", "w_attn_out", "w_in"),
    "ffn1_down": ("w_ffn1_down",),
    "ffn1": ("w_ffn1_gu",),
}
TAIL_STAGES = (("ffn2", "cross"), ("mix",), ("ffn1_down", "ffn1"))
KIND = dict(MATS)


def _step(x, mem, tgt, wts, m_in, v_in):
    d = x.shape[-1]
    cc = wts["conv_w"].shape[1]
    place = jnp.stack([lax.axis_index("c"), 2 * lax.axis_index("x") + lax.axis_index("y")]).astype(jnp.int32)
    dims = {n: (kind, *wts[n].shape) for n, kind in MATS}
    dims["conv_w"] = ("col", CONV_ROWS, cc)

    w = {n: wts[n].reshape(1, -1) for n in VECS + ("b_gate",)}
    flying, token = {}, None
    for names in GATHER_GROUPS:
        gd = [dims[n] for n in names]
        shards = [jnp.pad(wts[n], ((0, CONV_ROWS - CONV_K), (0, 0))) if n == "conv_w" else _cast_shard(wts[n], "cast_" + n, token)
                  for n in names]
        lands = [lax.empty(_full_shape(*dm), sh.dtype) for dm, sh in zip(gd, shards)]
        plan = _gather_plan(gd)
        ss, rs, srcs, lands, token = _split_start("gather_start_" + names[0], plan, 4 * len(names), shards, lands, token)
        flying.update({n: (names, plan, ss, rs, srcs, lands, gd) for n in names})

    passing = {}

    def prefetch(name, after):
        if name not in passing:
            names, plan, ss, rs, srcs, lands, gd = flying[name]
            _, lands = _split_wait("gather_wait_" + names[0], plan, ss, rs, srcs, lands, after)
            plan = _forward_plan(gd)
            ss, rs, _, lands, _ = _split_start("forward_start_" + names[0], plan, 3 * len(names), [], lands)
            passing.update({n: (names, plan, ss, rs, lands) for n in names})

    def fetch(name, after):
        prefetch(name, after)
        names, plan, ss, rs, lands = passing[name]
        _, lands = _split_wait("forward_wait_" + names[0], plan, ss, rs, [], lands, after)
        return {n: (land[:CONV_K] if n == "conv_w" else land) for n, land in zip(names, lands)}

    swapping, sent = {}, {}

    def emit(tag, g):
        if tag not in REDUCE_GROUPS:
            return None
        names = REDUCE_GROUPS[tag]
        gd = [dims[n] for n in names]
        lands = [lax.empty((N_CHIPS, r // 2, cw), BF16) for (_, r, cw) in gd]
        plan = _rs_cores_plan(gd)
        ss, rs, srcs, lands, tok = _split_start("rs_cores_start_" + tag, plan, N_CHIPS * len(names), [g[n] for n in names], lands)
        swapping[tag] = (plan, ss, rs, srcs, lands)
        return tok

    def tick(tag, after):
        if tag not in REDUCE_GROUPS:
            return None
        names = REDUCE_GROUPS[tag]
        plan, ss, rs, srcs, lands = swapping[tag]
        mine, got = _split_wait("rs_cores_wait_" + tag, plan, ss, rs, srcs, lands, after)
        parts = [_sum_cores(gm, t, KIND[n], place, "sum_cores_" + n) for n, gm, t in zip(names, mine, got)]
        lands = [lax.empty((3, *p.shape[1:]), BF16) for p in parts]
        plan = _rs_chips_plan(len(names))
        ss, rs, srcs, lands, tok = _split_start("rs_chips_start_" + tag, plan, 3 * len(names), parts, lands)
        sent[tag] = (plan, ss, rs, srcs, lands)
        return tok

    loss_lanes, dx, g, last = _local_step(x[0], mem[0], tgt[0], w, fetch, prefetch, emit, tick, token)

    rows = [g[n] for n in VECS] + [g["b_gate"][:, :d], g["b_gate"][:, d:], g["conv_w"], loss_lanes]
    packed = _pack_rows(rows, d, "pack_small", after=last)
    small = jnp.concatenate([packed[None], jnp.zeros((N_DEV - 1, *packed.shape), F32)], axis=0)
    small_plan = _small_plan()
    small_ss, small_rs, _, small, after = _split_start("small_start", small_plan, N_DEV - 1, [], [small])

    grads, out = {}, {}

    def update(n):
        shape = wts[n].shape
        as2d = (lambda a: a.reshape(1, -1)) if len(shape) == 1 else (lambda a: a)
        return [r.reshape(shape) for r in _adamw(grads[n], as2d(wts[n]), as2d(m_in[n]), as2d(v_in[n]), "adamw_" + n)]

    def finish(sharing, after):
        tag, names, plan, ss, rs, halves = sharing
        _, both = _split_wait("share_wait_" + tag, plan, ss, rs, [], halves, after)
        for n, b in zip(names, both):
            grads[n] = b.reshape(-1, b.shape[-1])
            out[n] = update(n)
        return out[names[-1]][1]

    sharing = None
    for stage in TAIL_STAGES:
        names, halves = [], []
        for tag in stage:
            plan, ss, rs, srcs, lands = sent[tag]
            parts, landed = _split_wait("rs_chips_wait_" + tag, plan, ss, rs, srcs, lands, after)
            halves += [_sum_chips(p, t, place, "sum_chips_" + n) for n, p, t in zip(REDUCE_GROUPS[tag], parts, landed)]
            names += REDUCE_GROUPS[tag]
        plan = _share_plan(len(names))
        ss, rs, _, halves, after = _split_start("share_start_" + stage[0], plan, len(names), [], halves)
        if sharing is not None:
            after = finish(sharing, after)
        sharing = (stage[0], names, plan, ss, rs, halves)
    after = finish(sharing, after)

    _, small = _split_wait("small_wait", small_plan, small_ss, small_rs, [], small, after)
    me = (4 * lax.axis_index("x") + 2 * lax.axis_index("y") + lax.axis_index("c")).astype(jnp.int32).reshape(1)
    red = _sum_small(small[0], me, "sum_small")
    grads.update({n: red[i:i + 1] for i, n in enumerate(VECS)})
    nv = len(VECS)
    grads["b_gate"] = jnp.concatenate([red[nv:nv + 1], red[nv + 1:nv + 2]], axis=1)
    chip = 2 * lax.axis_index("x") + lax.axis_index("y")
    grads["conv_w"] = lax.dynamic_slice_in_dim(red[nv + 2:nv + 2 + CONV_K], chip * cc, cc, axis=1)
    loss = red[nv + 2 + CONV_K, 0]
    out.update({n: update(n) for n in WEIGHTS if n not in KIND})
    return (loss, dx[None], *[out[n][0] for n in WEIGHTS], *[out[n][1] for n in WEIGHTS],
            *[out[n][2] for n in WEIGHTS], *[out[n][3] for n in WEIGHTS])


def kernel(x, mem, g_ffn1, w_ffn1_gu, w_ffn1_down, g_mix, w_in, b_gate, conv_w, w_conv_out, w_attn_out, w_o, g_cross, g_mem, w_cq, w_ckv, w_co, g_ffn2, w_ffn2_gu, w_ffn2_down, g_final, loss_target, m_g_ffn1, m_w_ffn1_gu, m_w_ffn1_down, m_g_mix, m_w_in, m_b_gate, m_conv_w, m_w_conv_out, m_w_attn_out, m_w_o, m_g_cross, m_g_mem, m_w_cq, m_w_ckv, m_w_co, m_g_ffn2, m_w_ffn2_gu, m_w_ffn2_down, m_g_final, v_g_ffn1, v_w_ffn1_gu, v_w_ffn1_down, v_g_mix, v_w_in, v_b_gate, v_conv_w, v_w_conv_out, v_w_attn_out, v_w_o, v_g_cross, v_g_mem, v_w_cq, v_w_ckv, v_w_co, v_g_ffn2, v_w_ffn2_gu, v_w_ffn2_down, v_g_final):
    given = dict(locals())
    wts = {n: given[n] for n in WEIGHTS}
    m_in = {n: given["m_" + n] for n in WEIGHTS}
    v_in = {n: given["v_" + n] for n in WEIGHTS}
    return _step(x, mem, loss_target, wts, m_in, v_in)
```
